```python
import math
import jax, jax.numpy as jnp
from jax import lax
import numpy as np

D_MODEL = 1024
BATCH = 8
SEQ = 2048
DEPTH = 1

RET_HEADS = 4
RET_DK = 256
RET_DV = 512
RET_CHUNK = 128
RET_QK_W = RET_HEADS * RET_DK
RET_V_W = RET_HEADS * RET_DV
ATT_GROUPS = ((128, 1), (512, 4), (2048, 16))
N_ATT_GROUPS = len(ATT_GROUPS)
ATT_HEADS_PER_GROUP = 4
ATT_HEAD_DIM = 128
ATT_GROUP_W = ATT_HEADS_PER_GROUP * ATT_HEAD_DIM
N_ATT_HEADS = N_ATT_GROUPS * ATT_HEADS_PER_GROUP
REL_BUCKETS = 32
REL_MAX_DIST = 2048
D_FF = 4 * D_MODEL
N_BRANCHES = 2
RMS_EPS = 1e-6
GN_EPS = 1e-5
ROPE_BASE = 10000.0

IN_SIZES = ([RET_QK_W, RET_QK_W, RET_V_W, RET_V_W]
            + [ATT_GROUP_W] * (3 * N_ATT_GROUPS)
            + [D_MODEL] * N_BRANCHES)
IN_COLS = sum(IN_SIZES)
IN_OFFSETS = [sum(IN_SIZES[:i + 1]) for i in range(len(IN_SIZES) - 1)]

kernel_name = "hybrid_retention_dilated_attn_block"


def rms_norm(x, g):
    xf = x.astype(jnp.float32)
    y = xf * lax.rsqrt(jnp.mean(xf * xf, axis=-1, keepdims=True) + RMS_EPS)
    return (y * g.astype(jnp.float32)).astype(x.dtype)


def modulate(h, shift, scale):
    return h * (1 + scale[:, None, :]) + shift[:, None, :]


def t5_bucket(dist):
    max_exact = REL_BUCKETS // 2
    d_f = jnp.maximum(dist, 1).astype(jnp.float32)
    large = max_exact + (jnp.log(d_f / max_exact) / math.log(REL_MAX_DIST / max_exact)
                         * (REL_BUCKETS - max_exact)).astype(jnp.int32)
    large = jnp.minimum(large, REL_BUCKETS - 1)
    return jnp.where(dist < max_exact, dist, large)


def rotary(x, pos):
    half = x.shape[-1] // 2
    inv = ROPE_BASE ** (-jnp.arange(half, dtype=jnp.float32) / half)
    ang = pos.astype(jnp.float32)[:, None] * inv[None, :]
    cos, sin = jnp.cos(ang).astype(x.dtype), jnp.sin(ang).astype(x.dtype)
    x1, x2 = x[..., :half], x[..., half:]
    return jnp.concatenate([x1 * cos - x2 * sin, x1 * sin + x2 * cos], axis=-1)


def retention(q, k, v):
    B, H, S, dk = q.shape
    dv = v.shape[-1]
    C = RET_CHUNK
    nc = S // C
    log_g = jnp.log1p(-(2.0 ** (-5.0 - jnp.arange(H, dtype=jnp.float32))))
    idx = jnp.arange(C, dtype=jnp.float32)
    rel = idx[:, None] - idx[None, :]
    inner_decay = jnp.where(rel >= 0, jnp.exp(log_g[:, None, None] * jnp.maximum(rel, 0.0)), 0.0)
    q_decay = jnp.exp(log_g[:, None] * (idx + 1.0))
    k_decay = jnp.exp(log_g[:, None] * (C - 1.0 - idx))
    chunk_decay = jnp.exp(log_g * C)

    def to_chunks(t):
        return jnp.moveaxis(t.astype(jnp.float32).reshape(B, H, nc, C, t.shape[-1]), 2, 0)

    qc, kc, vc = to_chunks(q), to_chunks(k), to_chunks(v)

    def step(state, inp):
        qi, ki, vi = inp
        s = jnp.einsum('bhid,bhjd->bhij', qi, ki) * inner_decay[None]
        o = (jnp.einsum('bhij,bhje->bhie', s, vi)
             + jnp.einsum('bhid,bhde->bhie', qi, state) * q_decay[None, :, :, None])
        state = (state * chunk_decay[None, :, None, None]
                 + jnp.einsum('bhjd,bhje->bhde', ki * k_decay[None, :, :, None], vi))
        return state, o

    state0 = jnp.zeros((B, H, dk, dv), jnp.float32)
    _, o = lax.scan(step, state0, (qc, kc, vc))
    return jnp.moveaxis(o, 0, 2).reshape(B, H, S, dv)


def dilated_group(q, k, v, bias_tab, window, dilation):
    B, H, S, dh = q.shape
    w = window // dilation
    blk = w
    span = dilation * blk
    Sp = -(-S // span) * span
    L = Sp // dilation
    nb = L // blk

    def split(t):
        t = jnp.pad(t, ((0, 0), (0, 0), (0, Sp - S), (0, 0)))
        t = t.reshape(B, H, L, dilation, dh).transpose(0, 1, 3, 2, 4)
        return t.reshape(B, H, dilation, nb, blk, dh)

    qs, ks, vs = split(q), split(k), split(v)

    def with_prev(t):
        prev = jnp.pad(t[:, :, :, :-1], ((0, 0), (0, 0), (0, 0), (1, 0), (0, 0), (0, 0)))
        return jnp.concatenate([prev, t], axis=4)

    kb, vb = with_prev(ks), with_prev(vs)
    qi = jnp.arange(blk)[:, None]
    kj = jnp.arange(2 * blk)[None, :]
    m = blk + qi - kj
    band = (m >= 0) & (m <= w)
    first_ok = kj >= blk
    valid = band[None] & ((jnp.arange(nb)[:, None, None] > 0) | first_ok[None])
    bias = bias_tab[t5_bucket(jnp.clip(m, 0, w) * dilation)]
    bias = jnp.moveaxis(bias, -1, 0).astype(jnp.float32)

    s = (jnp.einsum('bhrnid,bhrnjd->bhrnij', qs, kb).astype(jnp.float32) * (dh ** -0.5)
         + bias[None, :, None, None])
    s = jnp.where(valid[None, None, None], s, -1e30)
    mx = jnp.max(s, axis=-1, keepdims=True)
    e = jnp.exp(s - mx)
    den = jnp.sum(e, axis=-1, keepdims=True)
    p = (e / den).astype(v.dtype)
    lse = (mx + jnp.log(den))[..., 0]
    o = jnp.einsum('bhrnij,bhrnjd->bhrnid', p, vb)

    def merge(t):
        t = t.reshape(B, H, dilation, L, *t.shape[5:])
        t = jnp.swapaxes(t, 2, 3)
        t = t.reshape(B, H, Sp, *t.shape[4:])
        return t[:, :, :S]

    return merge(o), merge(lse)


def token_mixer(h, w_in, rel_bias, gn_g, gn_b, w_ret_out, w_att_out, w_o):
    B, S, _ = h.shape
    proj = h @ w_in
    parts = jnp.split(proj, IN_OFFSETS, axis=-1)

    def heads(t, n):
        return t.reshape(B, S, n, -1).transpose(0, 2, 1, 3)

    pos = jnp.arange(S)
    rq, rk, rv, rg = parts[0], parts[1], parts[2], parts[3]
    rq = rotary(heads(rq, RET_HEADS), pos)
    rk = rotary(heads(rk, RET_HEADS), pos) * (RET_DK ** -0.5)
    ro = retention(rq, rk, heads(rv, RET_HEADS))
    mu = jnp.mean(ro, axis=-1, keepdims=True)
    var = jnp.mean(jnp.square(ro - mu), axis=-1, keepdims=True)
    ro = ((ro - mu) * lax.rsqrt(var + GN_EPS)).transpose(0, 2, 1, 3).reshape(B, S, RET_V_W)
    ro = (ro * gn_g.astype(jnp.float32) + gn_b.astype(jnp.float32)).astype(h.dtype)
    ret_out = (jax.nn.silu(rg) * ro) @ w_ret_out

    outs, lses = [], []
    for gi, (win, dil) in enumerate(ATT_GROUPS):
        aq, ak, av = parts[4 + 3 * gi], parts[5 + 3 * gi], parts[6 + 3 * gi]
        tab = rel_bias[:, gi * ATT_HEADS_PER_GROUP:(gi + 1) * ATT_HEADS_PER_GROUP]
        o, lse = dilated_group(heads(aq, ATT_HEADS_PER_GROUP), heads(ak, ATT_HEADS_PER_GROUP),
                               heads(av, ATT_HEADS_PER_GROUP), tab, win, dil)
        outs.append(o)
        lses.append(lse)
    o_all = jnp.stack(outs, axis=0)
    wts = jax.nn.softmax(jnp.stack(lses, axis=0), axis=0)
    att = jnp.einsum('gbhs,gbhsd->bshd', wts.astype(o_all.dtype), o_all).reshape(B, S, ATT_GROUP_W)
    att_out = att @ w_att_out

    gate_a, gate_b = parts[-2], parts[-1]
    merged = jax.nn.sigmoid(gate_a) * ret_out + jax.nn.sigmoid(gate_b) * att_out
    return merged @ w_o


def squared_relu_mlp(h, w1, w2):
    return jnp.square(jax.nn.relu(h @ w1)) @ w2


def _fwd_setup_inputs(seed: int = 0) -> dict:
    key = jax.random.key(seed)
    ks = jax.random.split(key, 18)
    nrm = jax.random.normal
    f32 = jnp.float32
    return {
        "x": nrm(ks[0], (BATCH, SEQ, D_MODEL), f32),
        "c": nrm(ks[1], (BATCH, D_MODEL), f32),
        "w_ada": nrm(ks[2], (DEPTH, D_MODEL, 6 * D_MODEL), f32) * D_MODEL ** -0.5,
        "b_ada": nrm(ks[3], (DEPTH, 6 * D_MODEL), f32) * 0.02,
        "norm1_g": 1.0 + 0.02 * nrm(ks[4], (DEPTH, D_MODEL), f32),
        "w_in": nrm(ks[5], (DEPTH, D_MODEL, IN_COLS), f32) * D_MODEL ** -0.5,
        "rel_bias": nrm(ks[6], (REL_BUCKETS, N_ATT_HEADS), f32) * 0.5,
        "ret_gn_g": 1.0 + 0.02 * nrm(ks[7], (DEPTH, RET_V_W), f32),
        "ret_gn_b": 0.02 * nrm(ks[8], (DEPTH, RET_V_W), f32),
        "w_ret_out": nrm(ks[9], (DEPTH, RET_V_W, D_MODEL), f32) * RET_V_W ** -0.5,
        "w_att_out": nrm(ks[10], (DEPTH, ATT_GROUP_W, D_MODEL), f32) * ATT_GROUP_W ** -0.5,
        "w_o": nrm(ks[11], (DEPTH, D_MODEL, D_MODEL), f32) * D_MODEL ** -0.5,
        "norm2_g": 1.0 + 0.02 * nrm(ks[12], (DEPTH, D_MODEL), f32),
        "w_ff1": nrm(ks[13], (DEPTH, D_MODEL, D_FF), f32) * D_MODEL ** -0.5,
        "w_ff2": nrm(ks[14], (DEPTH, D_FF, D_MODEL), f32) * D_FF ** -0.5,
        "norm_f_g": 1.0 + 0.02 * nrm(ks[15], (D_MODEL,), f32),
    }


def _fwd_reference(x, c, w_ada, b_ada, norm1_g, w_in, rel_bias, ret_gn_g, ret_gn_b,
              w_ret_out, w_att_out, w_o, norm2_g, w_ff1, w_ff2, norm_f_g):
    for l in range(DEPTH):
        mod = jax.nn.silu(c) @ w_ada[l] + b_ada[l]
        sh1, sc1, g1, sh2, sc2, g2 = jnp.split(mod, 6, axis=-1)
        h = modulate(rms_norm(x, norm1_g[l]), sh1, sc1)
        x = x + g1[:, None, :] * token_mixer(h, w_in[l], rel_bias, ret_gn_g[l], ret_gn_b[l],
                                             w_ret_out[l], w_att_out[l], w_o[l])
        h = modulate(rms_norm(x, norm2_g[l]), sh2, sc2)
        x = x + g2[:, None, :] * squared_relu_mlp(h, w_ff1[l], w_ff2[l])
    return rms_norm(x, norm_f_g)


import jax as _jax
import jax.numpy as _jnp

TWIN_FORMAT = 'train_step'
FWD_PARAMS = ['x', 'c', 'w_ada', 'b_ada', 'norm1_g', 'w_in', 'rel_bias', 'ret_gn_g', 'ret_gn_b', 'w_ret_out', 'w_att_out', 'w_o', 'norm2_g', 'w_ff1', 'w_ff2', 'norm_f_g']
TWIN_WEIGHTS = ['w_ada', 'b_ada', 'norm1_g', 'w_in', 'rel_bias', 'ret_gn_g', 'ret_gn_b', 'w_ret_out', 'w_att_out', 'w_o', 'norm2_g', 'w_ff1', 'w_ff2', 'norm_f_g']
TWIN_DIFF_INPUT = 'x'
TWIN_INPUTS = ['x', 'c', 'w_ada', 'b_ada', 'norm1_g', 'w_in', 'rel_bias', 'ret_gn_g', 'ret_gn_b', 'w_ret_out', 'w_att_out', 'w_o', 'norm2_g', 'w_ff1', 'w_ff2', 'norm_f_g', 'loss_target', 'm_w_ada', 'm_b_ada', 'm_norm1_g', 'm_w_in', 'm_rel_bias', 'm_ret_gn_g', 'm_ret_gn_b', 'm_w_ret_out', 'm_w_att_out', 'm_w_o', 'm_norm2_g', 'm_w_ff1', 'm_w_ff2', 'm_norm_f_g', 'v_w_ada', 'v_b_ada', 'v_norm1_g', 'v_w_in', 'v_rel_bias', 'v_ret_gn_g', 'v_ret_gn_b', 'v_w_ret_out', 'v_w_att_out', 'v_w_o', 'v_norm2_g', 'v_w_ff1', 'v_w_ff2', 'v_norm_f_g']
TWIN_OUTPUTS = ['loss', 'grad_x', 'grad_w_ada', 'grad_b_ada', 'grad_norm1_g', 'grad_w_in', 'grad_rel_bias', 'grad_ret_gn_g', 'grad_ret_gn_b', 'grad_w_ret_out', 'grad_w_att_out', 'grad_w_o', 'grad_norm2_g', 'grad_w_ff1', 'grad_w_ff2', 'grad_norm_f_g', 'delta_w_ada', 'delta_b_ada', 'delta_norm1_g', 'delta_w_in', 'delta_rel_bias', 'delta_ret_gn_g', 'delta_ret_gn_b', 'delta_w_ret_out', 'delta_w_att_out', 'delta_w_o', 'delta_norm2_g', 'delta_w_ff1', 'delta_w_ff2', 'delta_norm_f_g', 'new_m_w_ada', 'new_m_b_ada', 'new_m_norm1_g', 'new_m_w_in', 'new_m_rel_bias', 'new_m_ret_gn_g', 'new_m_ret_gn_b', 'new_m_w_ret_out', 'new_m_w_att_out', 'new_m_w_o', 'new_m_norm2_g', 'new_m_w_ff1', 'new_m_w_ff2', 'new_m_norm_f_g', 'new_v_w_ada', 'new_v_b_ada', 'new_v_norm1_g', 'new_v_w_in', 'new_v_rel_bias', 'new_v_ret_gn_g', 'new_v_ret_gn_b', 'new_v_w_ret_out', 'new_v_w_att_out', 'new_v_w_o', 'new_v_norm2_g', 'new_v_w_ff1', 'new_v_w_ff2', 'new_v_norm_f_g']
TWIN_LEAF_KINDS = {'loss': 'loss', 'grad_x': 'grad_x', 'grad_w_ada': 'grad_w', 'grad_b_ada': 'grad_w', 'grad_norm1_g': 'grad_w', 'grad_w_in': 'grad_w', 'grad_rel_bias': 'grad_w', 'grad_ret_gn_g': 'grad_w', 'grad_ret_gn_b': 'grad_w', 'grad_w_ret_out': 'grad_w', 'grad_w_att_out': 'grad_w', 'grad_w_o': 'grad_w', 'grad_norm2_g': 'grad_w', 'grad_w_ff1': 'grad_w', 'grad_w_ff2': 'grad_w', 'grad_norm_f_g': 'grad_w', 'delta_w_ada': 'delta_w', 'delta_b_ada': 'delta_w', 'delta_norm1_g': 'delta_w', 'delta_w_in': 'delta_w', 'delta_rel_bias': 'delta_w', 'delta_ret_gn_g': 'delta_w', 'delta_ret_gn_b': 'delta_w', 'delta_w_ret_out': 'delta_w', 'delta_w_att_out': 'delta_w', 'delta_w_o': 'delta_w', 'delta_norm2_g': 'delta_w', 'delta_w_ff1': 'delta_w', 'delta_w_ff2': 'delta_w', 'delta_norm_f_g': 'delta_w', 'new_m_w_ada': 'new_m', 'new_m_b_ada': 'new_m', 'new_m_norm1_g': 'new_m', 'new_m_w_in': 'new_m', 'new_m_rel_bias': 'new_m', 'new_m_ret_gn_g': 'new_m', 'new_m_ret_gn_b': 'new_m', 'new_m_w_ret_out': 'new_m', 'new_m_w_att_out': 'new_m', 'new_m_w_o': 'new_m', 'new_m_norm2_g': 'new_m', 'new_m_w_ff1': 'new_m', 'new_m_w_ff2': 'new_m', 'new_m_norm_f_g': 'new_m', 'new_v_w_ada': 'new_v', 'new_v_b_ada': 'new_v', 'new_v_norm1_g': 'new_v', 'new_v_w_in': 'new_v', 'new_v_rel_bias': 'new_v', 'new_v_ret_gn_g': 'new_v', 'new_v_ret_gn_b': 'new_v', 'new_v_w_ret_out': 'new_v', 'new_v_w_att_out': 'new_v', 'new_v_w_o': 'new_v', 'new_v_norm2_g': 'new_v', 'new_v_w_ff1': 'new_v', 'new_v_w_ff2': 'new_v', 'new_v_norm_f_g': 'new_v'}


def _forward(args):
    return _fwd_reference(*[args[k] for k in FWD_PARAMS])


def _output_shape():
    out = _jax.eval_shape(lambda: _forward(_fwd_setup_inputs(0)))
    return out.shape, out.dtype

N_MICROBATCH = 1
ADAM_LR = 0.001
ADAM_B1 = 0.9
ADAM_B2 = 0.999
ADAM_EPS = 1e-08
ADAM_WD = 0.01
ADAM_STEP = 10
PER_EXAMPLE_BATCH_AXIS = {'x': 0, 'c': 0, 'loss_target': 0}
SHARED_INPUTS = []
_WEIGHT_DTYPES = {'w_ada': _jnp.float32, 'b_ada': _jnp.float32, 'norm1_g': _jnp.float32, 'w_in': _jnp.float32, 'rel_bias': _jnp.float32, 'ret_gn_g': _jnp.float32, 'ret_gn_b': _jnp.float32, 'w_ret_out': _jnp.float32, 'w_att_out': _jnp.float32, 'w_o': _jnp.float32, 'norm2_g': _jnp.float32, 'w_ff1': _jnp.float32, 'w_ff2': _jnp.float32, 'norm_f_g': _jnp.float32}
MOMENT_SCALE = {'w_ada': 9.700326e-02, 'b_ada': 1.776570e-01, 'norm1_g': 7.433816e-02, 'w_in': 2.761292e-02, 'rel_bias': 2.126658e-02, 'ret_gn_g': 2.579356e-02, 'ret_gn_b': 2.705460e-02, 'w_ret_out': 3.523191e-02, 'w_att_out': 2.565821e-02, 'w_o': 4.327829e-02, 'norm2_g': 1.025007e-01, 'w_ff1': 6.234813e-02, 'w_ff2': 1.348164e-01, 'norm_f_g': 1.784190e+01}


def _to_microbatches(a, axis):
    t = _jnp.moveaxis(a, axis, 0)
    t = t.reshape((N_MICROBATCH, t.shape[0] // N_MICROBATCH) + t.shape[1:])
    return _jnp.moveaxis(t, 1, axis + 1)


def setup_inputs(seed: int = 0) -> dict:
    inp = _fwd_setup_inputs(seed)
    key = _jax.random.fold_in(_jax.random.key(seed), 7919)
    shape, _ = _output_shape()
    out = dict(inp)
    out["loss_target"] = _jax.random.normal(_jax.random.fold_in(key, 0), shape, _jnp.float32)
    for i, name in enumerate(TWIN_WEIGHTS):
        w = inp[name].astype(_jnp.float32)
        if MOMENT_SCALE is None:
            s = _jnp.sqrt(_jnp.mean(_jnp.square(w)) + 1e-30)
        else:
            s = MOMENT_SCALE[name]
        km, kv = _jax.random.split(_jax.random.fold_in(key, i + 1))
        out[name] = w
        out["m_" + name] = s * _jax.random.normal(km, w.shape, _jnp.float32)
        out["v_" + name] = (s * s) * _jax.random.uniform(kv, w.shape, _jnp.float32, 0.5, 1.5)
    if N_MICROBATCH > 1:
        for name, axis in PER_EXAMPLE_BATCH_AXIS.items():
            out[name] = _to_microbatches(out[name], axis)
    return {'x': out['x'], 'c': out['c'], 'w_ada': out['w_ada'], 'b_ada': out['b_ada'], 'norm1_g': out['norm1_g'], 'w_in': out['w_in'], 'rel_bias': out['rel_bias'], 'ret_gn_g': out['ret_gn_g'], 'ret_gn_b': out['ret_gn_b'], 'w_ret_out': out['w_ret_out'], 'w_att_out': out['w_att_out'], 'w_o': out['w_o'], 'norm2_g': out['norm2_g'], 'w_ff1': out['w_ff1'], 'w_ff2': out['w_ff2'], 'norm_f_g': out['norm_f_g'], 'loss_target': out['loss_target'], 'm_w_ada': out['m_w_ada'], 'm_b_ada': out['m_b_ada'], 'm_norm1_g': out['m_norm1_g'], 'm_w_in': out['m_w_in'], 'm_rel_bias': out['m_rel_bias'], 'm_ret_gn_g': out['m_ret_gn_g'], 'm_ret_gn_b': out['m_ret_gn_b'], 'm_w_ret_out': out['m_w_ret_out'], 'm_w_att_out': out['m_w_att_out'], 'm_w_o': out['m_w_o'], 'm_norm2_g': out['m_norm2_g'], 'm_w_ff1': out['m_w_ff1'], 'm_w_ff2': out['m_w_ff2'], 'm_norm_f_g': out['m_norm_f_g'], 'v_w_ada': out['v_w_ada'], 'v_b_ada': out['v_b_ada'], 'v_norm1_g': out['v_norm1_g'], 'v_w_in': out['v_w_in'], 'v_rel_bias': out['v_rel_bias'], 'v_ret_gn_g': out['v_ret_gn_g'], 'v_ret_gn_b': out['v_ret_gn_b'], 'v_w_ret_out': out['v_w_ret_out'], 'v_w_att_out': out['v_w_att_out'], 'v_w_o': out['v_w_o'], 'v_norm2_g': out['v_norm2_g'], 'v_w_ff1': out['v_w_ff1'], 'v_w_ff2': out['v_w_ff2'], 'v_norm_f_g': out['v_norm_f_g']}


def _loss(weights, diff, rest, loss_target):
    with _jax.named_scope("forward"):
        args = {**rest, TWIN_DIFF_INPUT: diff, **{k: w.astype(_WEIGHT_DTYPES[k]) for k, w in weights.items()}}
        y = _forward(args)
    with _jax.named_scope("loss_head"):
        err = _jnp.square(y.astype(_jnp.float32) - loss_target)
        return 0.5 * _jnp.sum(_jnp.mean(err, axis=-1)) if err.ndim else 0.5 * err


def _adamw(w, g, m, v):
    m = ADAM_B1 * m + (1.0 - ADAM_B1) * g
    v = ADAM_B2 * v + (1.0 - ADAM_B2) * _jnp.square(g)
    m_hat = m / (1.0 - ADAM_B1 ** ADAM_STEP)
    v_hat = v / (1.0 - ADAM_B2 ** ADAM_STEP)
    delta = -ADAM_LR * (m_hat / (_jnp.sqrt(v_hat) + ADAM_EPS) + ADAM_WD * w)
    return delta, m, v


def reference(x, c, w_ada, b_ada, norm1_g, w_in, rel_bias, ret_gn_g, ret_gn_b, w_ret_out, w_att_out, w_o, norm2_g, w_ff1, w_ff2, norm_f_g, loss_target, m_w_ada, m_b_ada, m_norm1_g, m_w_in, m_rel_bias, m_ret_gn_g, m_ret_gn_b, m_w_ret_out, m_w_att_out, m_w_o, m_norm2_g, m_w_ff1, m_w_ff2, m_norm_f_g, v_w_ada, v_b_ada, v_norm1_g, v_w_in, v_rel_bias, v_ret_gn_g, v_ret_gn_b, v_w_ret_out, v_w_att_out, v_w_o, v_norm2_g, v_w_ff1, v_w_ff2, v_norm_f_g):
    given = dict(x=x, c=c, w_ada=w_ada, b_ada=b_ada, norm1_g=norm1_g, w_in=w_in, rel_bias=rel_bias, ret_gn_g=ret_gn_g, ret_gn_b=ret_gn_b, w_ret_out=w_ret_out, w_att_out=w_att_out, w_o=w_o, norm2_g=norm2_g, w_ff1=w_ff1, w_ff2=w_ff2, norm_f_g=norm_f_g, loss_target=loss_target, m_w_ada=m_w_ada, m_b_ada=m_b_ada, m_norm1_g=m_norm1_g, m_w_in=m_w_in, m_rel_bias=m_rel_bias, m_ret_gn_g=m_ret_gn_g, m_ret_gn_b=m_ret_gn_b, m_w_ret_out=m_w_ret_out, m_w_att_out=m_w_att_out, m_w_o=m_w_o, m_norm2_g=m_norm2_g, m_w_ff1=m_w_ff1, m_w_ff2=m_w_ff2, m_norm_f_g=m_norm_f_g, v_w_ada=v_w_ada, v_b_ada=v_b_ada, v_norm1_g=v_norm1_g, v_w_in=v_w_in, v_rel_bias=v_rel_bias, v_ret_gn_g=v_ret_gn_g, v_ret_gn_b=v_ret_gn_b, v_w_ret_out=v_w_ret_out, v_w_att_out=v_w_att_out, v_w_o=v_w_o, v_norm2_g=v_norm2_g, v_w_ff1=v_w_ff1, v_w_ff2=v_w_ff2, v_norm_f_g=v_norm_f_g)
    weights = {n: given[n] for n in TWIN_WEIGHTS}
    shared = {n: given[n] for n in SHARED_INPUTS}
    per_example = {n: given[n] for n in ['x', 'c']}
    grad_fn = _jax.value_and_grad(_loss, argnums=(0, 1))

    def one_microbatch(ex, loss_target):
        ex = dict(ex)
        diff = ex.pop(TWIN_DIFF_INPUT)
        return grad_fn(weights, diff, {**shared, **ex}, loss_target)

    if N_MICROBATCH == 1:
        loss, (grad_w, grad_x) = one_microbatch(per_example, given["loss_target"])
    else:
        def body(carry, xs):
            loss_sum, grad_sum = carry
            l_k, (gw_k, gx_k) = one_microbatch(xs[0], xs[1])
            with _jax.named_scope("update"):
                return (loss_sum + l_k, _jax.tree.map(_jnp.add, grad_sum, gw_k)), gx_k

        init = (_jnp.zeros((), _jnp.float32), _jax.tree.map(_jnp.zeros_like, weights))
        (loss, grad_w), grad_x = _jax.lax.scan(body, init, (per_example, given["loss_target"]))
    with _jax.named_scope("update"):
        delta_w, new_m, new_v = {}, {}, {}
        for n in TWIN_WEIGHTS:
            delta_w[n], new_m[n], new_v[n] = _adamw(weights[n], grad_w[n], given["m_" + n], given["v_" + n])
    return (loss, grad_x, *[grad_w[n] for n in TWIN_WEIGHTS], *[delta_w[n] for n in TWIN_WEIGHTS],
            *[new_m[n] for n in TWIN_WEIGHTS], *[new_v[n] for n in TWIN_WEIGHTS])
```

```python
import functools
import math

import jax
import jax.numpy as jnp
from jax import lax
from jax.experimental import pallas as pl
from jax.experimental.pallas import tpu as pltpu

F32 = jnp.float32
BF16 = jnp.bfloat16
I32 = jnp.int32

SEQ = 2048
D_MODEL = 1024
RET_HEADS = 4
RET_DK = 256
RET_DV = 512
RET_CHUNK = 128
RET_QK_W = RET_HEADS * RET_DK
RET_V_W = RET_HEADS * RET_DV
ATT_GROUPS = ((128, 1), (512, 4), (2048, 16))
ATT_HPG = 4
ATT_DH = 128
ATT_W = ATT_HPG * ATT_DH
ATT_BLK = 128
N_BLK = SEQ // ATT_BLK
REL_BUCKETS = 32
REL_MAX_DIST = 2048
N_ATT_HEADS = 12
D_FF = 4 * D_MODEL
RMS_EPS = 1e-6
GN_EPS = 1e-5
ROPE_BASE = 10000.0
IN_COLS = 2 * RET_QK_W + 2 * RET_V_W + 9 * ATT_W + 2 * D_MODEL
OFF_Q, OFF_K, OFF_V, OFF_G = 0, RET_QK_W, 2 * RET_QK_W, 2 * RET_QK_W + RET_V_W
OFF_ATT = 2 * RET_QK_W + 2 * RET_V_W
OFF_GATE = OFF_ATT + 9 * ATT_W
N_CHIPS = 4
N_DEV = 8
ADA_COLS = 6 * D_MODEL

ADAM_LR = 0.001
ADAM_B1 = 0.9
ADAM_B2 = 0.999
ADAM_EPS = 1e-08
ADAM_WD = 0.01
ADAM_STEP = 10

VMEM_LIMIT_V7X = 56 * 1024 * 1024
MESH = pl.DeviceIdType.MESH


def _cparams(sem):
    return pltpu.CompilerParams(dimension_semantics=sem, vmem_limit_bytes=VMEM_LIMIT_V7X)


def _sigmoid(v):
    return 1.0 / (1.0 + jnp.exp(-v))


def _rowmap(name, body, row_ins, bcast_ins, row_outs, sum_outs=(), tm=256):
    m = row_ins[0].shape[0]
    n_in = len(row_ins) + len(bcast_ins)
    n_ro = len(row_outs)

    def kern(*refs):
        vals = [r[...] for r in refs[:n_in]]
        res = body(*vals)
        if not isinstance(res, (tuple, list)):
            res = (res,)
        outs = refs[n_in:]
        for r, v in zip(outs[:n_ro], res[:n_ro]):
            r[...] = v.astype(r.dtype)
        if sum_outs:
            @pl.when(pl.program_id(0) == 0)
            def _():
                for r in outs[n_ro:]:
                    r[...] = jnp.zeros_like(r)
            for r, v in zip(outs[n_ro:], res[n_ro:]):
                r[...] += v

    in_specs = [pl.BlockSpec((tm, a.shape[1]), lambda i: (i, 0)) for a in row_ins]
    in_specs += [pl.BlockSpec(a.shape, lambda i: (0, 0)) for a in bcast_ins]
    out_specs = [pl.BlockSpec((tm, n), lambda i: (i, 0)) for n, _ in row_outs]
    out_specs += [pl.BlockSpec((1, n), lambda i: (0, 0)) for n in sum_outs]
    out_shape = [jax.ShapeDtypeStruct((m, n), dt) for n, dt in row_outs]
    out_shape += [jax.ShapeDtypeStruct((1, n), F32) for n in sum_outs]
    return pl.pallas_call(
        kern, name=name, grid=(m // tm,), in_specs=in_specs, out_specs=out_specs,
        out_shape=out_shape, compiler_params=_cparams(("arbitrary",)),
    )(*row_ins, *bcast_ins)


def _matmul(name, a, b, kind, m, n, k, outs, *, b_off=0, tm=512, tn=512, tk=512,
            epilogue=None, extras=()):
    tm, tn, tk = min(tm, m), min(tn, n), min(tk, k)
    nk = k // tk
    if kind == "nn":
        a_spec = pl.BlockSpec((tm, tk), lambda i, j, kk: (i, kk))
        b_spec = pl.BlockSpec((tk, tn), lambda i, j, kk: (kk, b_off // tn + j))
        dn = (((1,), (0,)), ((), ()))
    elif kind == "nt":
        a_spec = pl.BlockSpec((tm, tk), lambda i, j, kk: (i, kk))
        b_spec = pl.BlockSpec((tn, tk), lambda i, j, kk: (j, b_off // tk + kk))
        dn = (((1,), (1,)), ((), ()))
    else:
        a_spec = pl.BlockSpec((tk, tm), lambda i, j, kk: (kk, i))
        b_spec = pl.BlockSpec((tk, tn), lambda i, j, kk: (kk, j))
        dn = (((0,), (0,)), ((), ()))
    n_ex, n_out = len(extras), len(outs)
    if epilogue is None:
        epilogue = lambda acc: (acc,)

    def kern(a_ref, b_ref, *rest):
        ex_refs, out_refs, acc_ref = rest[:n_ex], rest[n_ex:n_ex + n_out], rest[-1]
        kk = pl.program_id(2)
        part = lax.dot_general(a_ref[...], b_ref[...], dn, preferred_element_type=F32)

        @pl.when(kk == 0)
        def _():
            acc_ref[...] = part

        @pl.when(kk > 0)
        def _():
            acc_ref[...] += part

        @pl.when(kk == nk - 1)
        def _():
            res = epilogue(acc_ref[...], *[r[...] for r in ex_refs])
            for r, v in zip(out_refs, res):
                r[...] = v.astype(r.dtype)

    in_specs = [a_spec, b_spec] + [pl.BlockSpec(bs, im) for _, bs, im in extras]
    out_specs = [pl.BlockSpec((tm, tn), lambda i, j, kk: (i, j)) for _ in outs]
    out_shape = [jax.ShapeDtypeStruct((m, n), dt) for dt in outs]
    res = pl.pallas_call(
        kern, name=name, grid=(m // tm, n // tn, nk), in_specs=in_specs, out_specs=out_specs,
        out_shape=out_shape, scratch_shapes=[pltpu.VMEM((tm, tn), F32)],
        compiler_params=_cparams(("parallel", "parallel", "arbitrary")),
    )(a, b, *[e[0] for e in extras])
    return res


def _rope_tables():
    half = RET_DK // 2
    inv = ROPE_BASE ** (-jnp.arange(half, dtype=F32) / half)
    ang = jnp.arange(SEQ).astype(F32)[:, None] * inv[None, :]
    return jnp.cos(ang), jnp.sin(ang)


def _decay_tables():
    c = RET_CHUNK
    log_g = jnp.log1p(-(2.0 ** (-5.0 - jnp.arange(RET_HEADS, dtype=F32))))
    idx = jnp.arange(c, dtype=F32)
    rel = idx[:, None] - idx[None, :]
    din = jnp.where(rel >= 0, jnp.exp(log_g[:, None, None] * jnp.maximum(rel, 0.0)), 0.0)
    qd = jnp.exp(log_g[:, None] * (idx + 1.0))[:, :, None]
    kd = jnp.exp(log_g[:, None] * (c - 1.0 - idx))[:, :, None]
    cd = jnp.exp(log_g * c)
    return din, qd, kd, cd


def _t5_bucket(dist):
    max_exact = REL_BUCKETS // 2
    d_f = jnp.maximum(dist, 1).astype(F32)
    large = max_exact + (jnp.log(d_f / max_exact) / math.log(REL_MAX_DIST / max_exact)
                         * (REL_BUCKETS - max_exact)).astype(I32)
    large = jnp.minimum(large, REL_BUCKETS - 1)
    return jnp.where(dist < max_exact, dist, large)


def _bucket_tables():
    qi = jnp.arange(ATT_BLK)[:, None]
    kj = jnp.arange(2 * ATT_BLK)[None, :]
    dist = jnp.clip(ATT_BLK + qi - kj, 0, ATT_BLK)
    return jnp.stack([_t5_bucket(dist * dil) for _, dil in ATT_GROUPS]).astype(I32)


def _permute_rows(t, dil):
    if dil == 1:
        return t
    s, w = t.shape
    return t.reshape(s // dil, dil, w).transpose(1, 0, 2).reshape(s, w)


def _unpermute_rows(t, dil):
    if dil == 1:
        return t
    s, w = t.shape
    return t.reshape(dil, s // dil, w).transpose(1, 0, 2).reshape(s, w)


def _retention_fwd(rqk, rv, din, qd, kd, cd):
    nc = SEQ // RET_CHUNK
    c, dk, dv = RET_CHUNK, RET_DK, RET_DV

    def kern(q_ref, k_ref, v_ref, din_ref, qd_ref, kd_ref, cd_ref, o_ref, st_ref, state):
        h, n = pl.program_id(0), pl.program_id(1)

        @pl.when(n == 0)
        def _():
            state[...] = jnp.zeros_like(state)

        q, k, v = q_ref[...], k_ref[...], v_ref[...]
        s_b = state[...].astype(BF16)
        st_ref[...] = s_b
        a = lax.dot_general(q, k, (((1,), (1,)), ((), ())), preferred_element_type=F32) * din_ref[...]
        o = jnp.dot(a.astype(BF16), v, preferred_element_type=F32)
        o += jnp.dot(q, s_b, preferred_element_type=F32) * qd_ref[...]
        o_ref[...] = o
        kk = (k.astype(F32) * kd_ref[...]).astype(BF16)
        upd = lax.dot_general(kk, v, (((0,), (0,)), ((), ())), preferred_element_type=F32)
        state[...] = state[...] * cd_ref[h] + upd

    return pl.pallas_call(
        kern, name="retention_fwd", grid=(RET_HEADS, nc),
        in_specs=[
            pl.BlockSpec((c, dk), lambda h, n: (n, h)),
            pl.BlockSpec((c, dk), lambda h, n: (n, RET_HEADS + h)),
            pl.BlockSpec((c, dv), lambda h, n: (n, h)),
            pl.BlockSpec((None, c, c), lambda h, n: (h, 0, 0)),
            pl.BlockSpec((None, c, 1), lambda h, n: (h, 0, 0)),
            pl.BlockSpec((None, c, 1), lambda h, n: (h, 0, 0)),
            pl.BlockSpec(memory_space=pltpu.SMEM),
        ],
        out_specs=[
            pl.BlockSpec((c, dv), lambda h, n: (n, h)),
            pl.BlockSpec((None, None, dk, dv), lambda h, n: (h, n, 0, 0)),
        ],
        out_shape=[
            jax.ShapeDtypeStruct((SEQ, RET_V_W), F32),
            jax.ShapeDtypeStruct((RET_HEADS, nc, dk, dv), BF16),
        ],
        scratch_shapes=[pltpu.VMEM((dk, dv), F32)],
        compiler_params=_cparams(("arbitrary", "arbitrary")),
    )(rqk, rqk, rv, din, qd, kd, cd)


def _retention_bwd(rqk, rv, states, d_ro, din, qd, kd, cd, cos, sin):
    nc = SEQ // RET_CHUNK
    c, dk, dv = RET_CHUNK, RET_DK, RET_DV
    half = dk // 2
    last = nc - 1

    def unrot(g, cs, sn):
        g1, g2 = g[:, :half], g[:, half:]
        return jnp.concatenate([g1 * cs + g2 * sn, g2 * cs - g1 * sn], axis=-1)

    def kern(q_ref, k_ref, v_ref, st_ref, do_ref, din_ref, qd_ref, kd_ref, cd_ref, cos_ref, sin_ref,
             dq_ref, dk_ref, dv_ref, dstate):
        h, step = pl.program_id(0), pl.program_id(1)

        @pl.when(step == 0)
        def _():
            dstate[...] = jnp.zeros_like(dstate)

        q, k, v, s_b = q_ref[...], k_ref[...], v_ref[...], st_ref[...]
        d_o = do_ref[...]
        d_ob = d_o.astype(BF16)
        d_oq = (d_o * qd_ref[...]).astype(BF16)
        ds_b = dstate[...].astype(BF16)
        din_m = din_ref[...]
        nt = (((1,), (1,)), ((), ()))
        tn = (((0,), (0,)), ((), ()))
        a_b = (lax.dot_general(q, k, nt, preferred_element_type=F32) * din_m).astype(BF16)
        kk = (k.astype(F32) * kd_ref[...]).astype(BF16)
        d_v = lax.dot_general(a_b, d_ob, tn, preferred_element_type=F32)
        d_v += jnp.dot(kk, ds_b, preferred_element_type=F32)
        d_a = (lax.dot_general(d_ob, v, nt, preferred_element_type=F32) * din_m).astype(BF16)
        d_q = jnp.dot(d_a, k, preferred_element_type=F32)
        d_q += lax.dot_general(d_oq, s_b, nt, preferred_element_type=F32)
        d_k = lax.dot_general(d_a, q, tn, preferred_element_type=F32)
        d_k += lax.dot_general(v, ds_b, nt, preferred_element_type=F32) * kd_ref[...]
        dstate[...] = dstate[...] * cd_ref[h] + lax.dot_general(q, d_oq, tn, preferred_element_type=F32)
        cs, sn = cos_ref[...], sin_ref[...]
        dq_ref[...] = unrot(d_q, cs, sn).astype(BF16)
        dk_ref[...] = (unrot(d_k, cs, sn) * (RET_DK ** -0.5)).astype(BF16)
        dv_ref[...] = d_v.astype(BF16)

    return pl.pallas_call(
        kern, name="retention_bwd", grid=(RET_HEADS, nc),
        in_specs=[
            pl.BlockSpec((c, dk), lambda h, n: (last - n, h)),
            pl.BlockSpec((c, dk), lambda h, n: (last - n, RET_HEADS + h)),
            pl.BlockSpec((c, dv), lambda h, n: (last - n, h)),
            pl.BlockSpec((None, None, dk, dv), lambda h, n: (h, last - n, 0, 0)),
            pl.BlockSpec((c, dv), lambda h, n: (last - n, h)),
            pl.BlockSpec((None, c, c), lambda h, n: (h, 0, 0)),
            pl.BlockSpec((None, c, 1), lambda h, n: (h, 0, 0)),
            pl.BlockSpec((None, c, 1), lambda h, n: (h, 0, 0)),
            pl.BlockSpec(memory_space=pltpu.SMEM),
            pl.BlockSpec((c, half), lambda h, n: (last - n, 0)),
            pl.BlockSpec((c, half), lambda h, n: (last - n, 0)),
        ],
        out_specs=[
            pl.BlockSpec((c, dk), lambda h, n: (last - n, h)),
            pl.BlockSpec((c, dk), lambda h, n: (last - n, h)),
            pl.BlockSpec((c, dv), lambda h, n: (last - n, h)),
        ],
        out_shape=[
            jax.ShapeDtypeStruct((SEQ, RET_QK_W), BF16),
            jax.ShapeDtypeStruct((SEQ, RET_QK_W), BF16),
            jax.ShapeDtypeStruct((SEQ, RET_V_W), BF16),
        ],
        scratch_shapes=[pltpu.VMEM((dk, dv), F32)],
        compiler_params=_cparams(("arbitrary", "arbitrary")),
    )(rqk, rqk, rv, states, d_ro, din, qd, kd, cd, cos, sin)


def _bias_build(rel_bias, buckets):
    ng = len(ATT_GROUPS)

    def kern(tab_ref, bkt_ref, o_ref):
        g, h = pl.program_id(0), pl.program_id(1)
        bkt = bkt_ref[...]
        acc = jnp.zeros(bkt.shape, F32)
        for b in range(REL_BUCKETS):
            acc = jnp.where(bkt == b, tab_ref[b, g * ATT_HPG + h], acc)
        o_ref[...] = acc

    return pl.pallas_call(
        kern, name="bias_build", grid=(ng, ATT_HPG),
        in_specs=[pl.BlockSpec(memory_space=pltpu.SMEM),
                  pl.BlockSpec((None, ATT_BLK, 2 * ATT_BLK), lambda g, h: (g, 0, 0))],
        out_specs=pl.BlockSpec((None, None, ATT_BLK, 2 * ATT_BLK), lambda g, h: (g, h, 0, 0)),
        out_shape=jax.ShapeDtypeStruct((ng, ATT_HPG, ATT_BLK, 2 * ATT_BLK), F32),
        compiler_params=_cparams(("arbitrary", "arbitrary")),
    )(rel_bias, buckets)


def _bias_grad(dsb, buckets):
    ng = len(ATT_GROUPS)

    def kern(ds_ref, bkt_ref, o_ref):
        g, h = pl.program_id(0), pl.program_id(1)
        bkt, ds = bkt_ref[...], ds_ref[...]
        for b in range(REL_BUCKETS):
            o_ref[b, g * ATT_HPG + h] = jnp.sum(jnp.where(bkt == b, ds, 0.0))

    return pl.pallas_call(
        kern, name="bias_grad", grid=(ng, ATT_HPG),
        in_specs=[pl.BlockSpec((None, None, ATT_BLK, 2 * ATT_BLK), lambda g, h: (g, h, 0, 0)),
                  pl.BlockSpec((None, ATT_BLK, 2 * ATT_BLK), lambda g, h: (g, 0, 0))],
        out_specs=pl.BlockSpec(memory_space=pltpu.SMEM),
        out_shape=jax.ShapeDtypeStruct((REL_BUCKETS, N_ATT_HEADS), F32),
        compiler_params=_cparams(("arbitrary", "arbitrary")),
    )(dsb, buckets)


_NT = (((1,), (1,)), ((), ()))
_TN = (((0,), (0,)), ((), ()))
_ATT_SCALE = ATT_DH ** -0.5


def _band_masks():
    qi = lax.broadcasted_iota(I32, (ATT_BLK, ATT_BLK), 0)
    kj = lax.broadcasted_iota(I32, (ATT_BLK, ATT_BLK), 1)
    return kj >= qi, qi >= kj


def _att_fwd(gi, qkv, bias, nb):
    blk, dh = ATT_BLK, ATT_DH

    def kern(q_ref, kc_ref, kp_ref, vc_ref, vp_ref, b_ref, o_ref, l_ref):
        b = pl.program_id(1)
        q = q_ref[...]
        mp, mc = _band_masks()
        mp = jnp.logical_and(mp, (b % nb) > 0)
        bias_m = b_ref[...]
        s_p = lax.dot_general(q, kp_ref[...], _NT, preferred_element_type=F32) * _ATT_SCALE + bias_m[:, :blk]
        s_c = lax.dot_general(q, kc_ref[...], _NT, preferred_element_type=F32) * _ATT_SCALE + bias_m[:, blk:]
        s_p = jnp.where(mp, s_p, -1e30)
        s_c = jnp.where(mc, s_c, -1e30)
        mx = jnp.maximum(jnp.max(s_p, axis=-1, keepdims=True), jnp.max(s_c, axis=-1, keepdims=True))
        e_p, e_c = jnp.exp(s_p - mx), jnp.exp(s_c - mx)
        den = jnp.sum(e_p, axis=-1, keepdims=True) + jnp.sum(e_c, axis=-1, keepdims=True)
        o = jnp.dot((e_p / den).astype(BF16), vp_ref[...], preferred_element_type=F32)
        o += jnp.dot((e_c / den).astype(BF16), vc_ref[...], preferred_element_type=F32)
        o_ref[...] = o
        l_ref[...] = jnp.broadcast_to(mx + jnp.log(den), (blk, dh))

    prev = lambda b: jnp.maximum(b - 1, 0)
    return pl.pallas_call(
        kern, name=f"att_fwd_g{gi}", grid=(ATT_HPG, N_BLK),
        in_specs=[
            pl.BlockSpec((blk, dh), lambda h, b: (b, h)),
            pl.BlockSpec((blk, dh), lambda h, b: (b, ATT_HPG + h)),
            pl.BlockSpec((blk, dh), lambda h, b: (prev(b), ATT_HPG + h)),
            pl.BlockSpec((blk, dh), lambda h, b: (b, 2 * ATT_HPG + h)),
            pl.BlockSpec((blk, dh), lambda h, b: (prev(b), 2 * ATT_HPG + h)),
            pl.BlockSpec((None, None, blk, 2 * blk), lambda h, b: (gi, h, 0, 0)),
        ],
        out_specs=[pl.BlockSpec((blk, dh), lambda h, b: (b, h)),
                   pl.BlockSpec((blk, dh), lambda h, b: (b, h))],
        out_shape=[jax.ShapeDtypeStruct((SEQ, ATT_W), F32), jax.ShapeDtypeStruct((SEQ, ATT_W), F32)],
        compiler_params=_cparams(("arbitrary", "arbitrary")),
    )(qkv, qkv, qkv, qkv, qkv, bias)


def _att_bwd_dq(gi, qkv, d_att, lse, dd, bias, nb):
    blk, dh = ATT_BLK, ATT_DH

    def kern(q_ref, kc_ref, kp_ref, vc_ref, vp_ref, do_ref, l_ref, d_ref, b_ref, dq_ref, dsb_ref):
        b = pl.program_id(1)

        @pl.when(b == 0)
        def _():
            dsb_ref[...] = jnp.zeros_like(dsb_ref)

        q, d_o = q_ref[...], do_ref[...]
        kc, kp = kc_ref[...], kp_ref[...]
        mp, mc = _band_masks()
        mp = jnp.logical_and(mp, (b % nb) > 0)
        bias_m = b_ref[...]
        lrow, drow = l_ref[...][:, :1], d_ref[...][:, :1]
        s_p = lax.dot_general(q, kp, _NT, preferred_element_type=F32) * _ATT_SCALE + bias_m[:, :blk]
        s_c = lax.dot_general(q, kc, _NT, preferred_element_type=F32) * _ATT_SCALE + bias_m[:, blk:]
        p_p = jnp.where(mp, jnp.exp(jnp.where(mp, s_p, -1e30) - lrow), 0.0)
        p_c = jnp.where(mc, jnp.exp(jnp.where(mc, s_c, -1e30) - lrow), 0.0)
        dp_p = lax.dot_general(d_o, vp_ref[...], _NT, preferred_element_type=F32)
        dp_c = lax.dot_general(d_o, vc_ref[...], _NT, preferred_element_type=F32)
        ds_p = p_p * (dp_p - drow)
        ds_c = p_c * (dp_c - drow)
        dq = jnp.dot(ds_p.astype(BF16), kp, preferred_element_type=F32)
        dq += jnp.dot(ds_c.astype(BF16), kc, preferred_element_type=F32)
        dq_ref[...] = (dq * _ATT_SCALE).astype(BF16)
        dsb_ref[:, :blk] += ds_p
        dsb_ref[:, blk:] += ds_c

    prev = lambda b: jnp.maximum(b - 1, 0)
    return pl.pallas_call(
        kern, name=f"att_bwd_dq_g{gi}", grid=(ATT_HPG, N_BLK),
        in_specs=[
            pl.BlockSpec((blk, dh), lambda h, b: (b, h)),
            pl.BlockSpec((blk, dh), lambda h, b: (b, ATT_HPG + h)),
            pl.BlockSpec((blk, dh), lambda h, b: (prev(b), ATT_HPG + h)),
            pl.BlockSpec((blk, dh), lambda h, b: (b, 2 * ATT_HPG + h)),
            pl.BlockSpec((blk, dh), lambda h, b: (prev(b), 2 * ATT_HPG + h)),
            pl.BlockSpec((blk, dh), lambda h, b: (b, h)),
            pl.BlockSpec((blk, dh), lambda h, b: (b, h)),
            pl.BlockSpec((blk, dh), lambda h, b: (b, h)),
            pl.BlockSpec((None, None, blk, 2 * blk), lambda h, b: (gi, h, 0, 0)),
        ],
        out_specs=[pl.BlockSpec((blk, dh), lambda h, b: (b, h)),
                   pl.BlockSpec((None, blk, 2 * blk), lambda h, b: (h, 0, 0))],
        out_shape=[jax.ShapeDtypeStruct((SEQ, ATT_W), BF16),
                   jax.ShapeDtypeStruct((ATT_HPG, blk, 2 * blk), F32)],
        compiler_params=_cparams(("arbitrary", "arbitrary")),
    )(qkv, qkv, qkv, qkv, qkv, d_att, lse, dd, bias)


def _att_bwd_dkv(gi, qkv, d_att, lse, dd, bias, nb):
    blk, dh = ATT_BLK, ATT_DH

    def kern(k_ref, v_ref, qc_ref, qn_ref, doc_ref, don_ref, lc_ref, ln_ref, dc_ref, dn_ref, b_ref,
             dk_ref, dv_ref):
        b = pl.program_id(1)
        k, v = k_ref[...], v_ref[...]
        mp, mc = _band_masks()
        has_next = jnp.logical_and(b + 1 < N_BLK, ((b + 1) % nb) > 0)
        mp = jnp.logical_and(mp, has_next)
        bias_m = b_ref[...]

        def part(q, d_o, lrow, drow, mask, bias_part):
            s = lax.dot_general(q, k, _NT, preferred_element_type=F32) * _ATT_SCALE + bias_part
            p = jnp.where(mask, jnp.exp(jnp.where(mask, s, -1e30) - lrow), 0.0)
            dp = lax.dot_general(d_o, v, _NT, preferred_element_type=F32)
            ds = p * (dp - drow)
            d_v = lax.dot_general(p.astype(BF16), d_o, _TN, preferred_element_type=F32)
            d_k = lax.dot_general(ds.astype(BF16), q, _TN, preferred_element_type=F32)
            return d_k, d_v

        dk_c, dv_c = part(qc_ref[...], doc_ref[...], lc_ref[...][:, :1], dc_ref[...][:, :1], mc,
                          bias_m[:, blk:])
        dk_n, dv_n = part(qn_ref[...], don_ref[...], ln_ref[...][:, :1], dn_ref[...][:, :1], mp,
                          bias_m[:, :blk])
        dk_ref[...] = ((dk_c + dk_n) * _ATT_SCALE).astype(BF16)
        dv_ref[...] = (dv_c + dv_n).astype(BF16)

    nxt = lambda b: jnp.minimum(b + 1, N_BLK - 1)
    return pl.pallas_call(
        kern, name=f"att_bwd_dkv_g{gi}", grid=(ATT_HPG, N_BLK),
        in_specs=[
            pl.BlockSpec((blk, dh), lambda h, b: (b, ATT_HPG + h)),
            pl.BlockSpec((blk, dh), lambda h, b: (b, 2 * ATT_HPG + h)),
            pl.BlockSpec((blk, dh), lambda h, b: (b, h)),
            pl.BlockSpec((blk, dh), lambda h, b: (nxt(b), h)),
            pl.BlockSpec((blk, dh), lambda h, b: (b, h)),
            pl.BlockSpec((blk, dh), lambda h, b: (nxt(b), h)),
            pl.BlockSpec((blk, dh), lambda h, b: (b, h)),
            pl.BlockSpec((blk, dh), lambda h, b: (nxt(b), h)),
            pl.BlockSpec((blk, dh), lambda h, b: (b, h)),
            pl.BlockSpec((blk, dh), lambda h, b: (nxt(b), h)),
            pl.BlockSpec((None, None, blk, 2 * blk), lambda h, b: (gi, h, 0, 0)),
        ],
        out_specs=[pl.BlockSpec((blk, dh), lambda h, b: (b, h)),
                   pl.BlockSpec((blk, dh), lambda h, b: (b, h))],
        out_shape=[jax.ShapeDtypeStruct((SEQ, ATT_W), BF16), jax.ShapeDtypeStruct((SEQ, ATT_W), BF16)],
        compiler_params=_cparams(("arbitrary", "arbitrary")),
    )(qkv, qkv, qkv, qkv, d_att, d_att, lse, lse, dd, dd, bias)


def _rms_parts(x):
    r = lax.rsqrt(jnp.mean(x * x, axis=-1, keepdims=True) + RMS_EPS)
    return x * r, r


def _rms_bwd(d_xhat, xhat, r):
    return r * (d_xhat - xhat * jnp.mean(d_xhat * xhat, axis=-1, keepdims=True))


def _prenorm_fwd(name, x, gain, shift, scale):
    def body(xt, g, sh, sc):
        xhat, _ = _rms_parts(xt)
        return (xhat * g) * (1.0 + sc) + sh
    return _rowmap(name, body, [x], [gain, shift, scale], [(D_MODEL, BF16)])[0]


def _prenorm_bwd(name, d_hs, x, gain, scale, resid):
    n_dh = len(d_hs)

    def body(*args):
        d_h = args[0]
        for t in args[1:n_dh]:
            d_h = d_h + t
        xt, res, g, sc = args[n_dh:]
        xhat, r = _rms_parts(xt)
        nrm = xhat * g
        d_n = d_h * (1.0 + sc)
        dx = _rms_bwd(d_n * g, xhat, r) + res
        return (dx, jnp.sum(d_h, axis=0, keepdims=True), jnp.sum(d_h * nrm, axis=0, keepdims=True),
                jnp.sum(d_n * xhat, axis=0, keepdims=True))

    return _rowmap(name, body, list(d_hs) + [x, resid], [gain, scale], [(D_MODEL, F32)],
                   [D_MODEL, D_MODEL, D_MODEL])


def _gn_parts(ro):
    mu = jnp.mean(ro, axis=-1, keepdims=True)
    cen = ro - mu
    rstd = lax.rsqrt(jnp.mean(cen * cen, axis=-1, keepdims=True) + GN_EPS)
    return cen * rstd, rstd


def _retpost_fwd(ro, rg, gn_g, gn_b):
    def body(rot, rgt, g, b):
        outs = []
        for h in range(RET_HEADS):
            sl = slice(h * RET_DV, (h + 1) * RET_DV)
            nrm, _ = _gn_parts(rot[:, sl])
            gate = rgt[:, sl]
            outs.append((gate * _sigmoid(gate)) * (nrm * g[:, sl] + b[:, sl]))
        return jnp.concatenate(outs, axis=-1)
    return _rowmap("retpost_fwd", body, [ro, rg], [gn_g, gn_b], [(RET_V_W, BF16)])[0]


def _retpost_bwd(d_gated, ro, rg, gn_g, gn_b):
    def body(dgt, rot, rgt, g, b):
        d_ro, d_rg, d_g, d_b = [], [], [], []
        for h in range(RET_HEADS):
            sl = slice(h * RET_DV, (h + 1) * RET_DV)
            nrm, rstd = _gn_parts(rot[:, sl])
            gate, dg = rgt[:, sl], dgt[:, sl]
            sg = _sigmoid(gate)
            ron = nrm * g[:, sl] + b[:, sl]
            d_rg.append(dg * ron * (sg * (1.0 + gate * (1.0 - sg))))
            d_ron = dg * (gate * sg)
            d_g.append(jnp.sum(d_ron * nrm, axis=0, keepdims=True))
            d_b.append(jnp.sum(d_ron, axis=0, keepdims=True))
            d_n = d_ron * g[:, sl]
            d_ro.append(rstd * (d_n - jnp.mean(d_n, axis=-1, keepdims=True)
                                - nrm * jnp.mean(d_n * nrm, axis=-1, keepdims=True)))
        cat = lambda ts: jnp.concatenate(ts, axis=-1)
        return cat(d_ro), cat(d_rg), cat(d_g), cat(d_b)
    return _rowmap("retpost_bwd", body, [d_gated, ro, rg], [gn_g, gn_b],
                   [(RET_V_W, F32), (RET_V_W, BF16)], [RET_V_W, RET_V_W])


def _combine(os_, ls_):
    def body(o0, o1, o2, l0, l1, l2):
        mx = jnp.maximum(jnp.maximum(l0, l1), l2)
        e0, e1, e2 = jnp.exp(l0 - mx), jnp.exp(l1 - mx), jnp.exp(l2 - mx)
        den = e0 + e1 + e2
        att = (e0 / den) * o0 + (e1 / den) * o1 + (e2 / den) * o2
        return att, att, mx + jnp.log(den)
    return _rowmap("att_combine", body, list(os_) + list(ls_), [],
                   [(ATT_W, F32), (ATT_W, BF16), (ATT_W, F32)])


def _att_bwd_pre(d_att, att):
    def body(dt, at):
        outs = []
        for h in range(ATT_HPG):
            sl = slice(h * ATT_DH, (h + 1) * ATT_DH)
            outs.append(jnp.broadcast_to(jnp.sum(dt[:, sl] * at[:, sl], axis=-1, keepdims=True),
                                         (dt.shape[0], ATT_DH)))
        return dt, jnp.concatenate(outs, axis=-1)
    return _rowmap("att_bwd_pre", body, [d_att, att], [], [(ATT_W, BF16), (ATT_W, F32)])


def _merge_fwd(gates, ret_out, att_out):
    def body(gt, ro, ao):
        return _sigmoid(gt[:, :D_MODEL]) * ro + _sigmoid(gt[:, D_MODEL:]) * ao
    return _rowmap("merge_fwd", body, [gates, ret_out, att_out], [], [(D_MODEL, BF16)])[0]


def _merge_bwd(d_merged, gates, ret_out, att_out):
    def body(dm, gt, ro, ao):
        sa, sb = _sigmoid(gt[:, :D_MODEL]), _sigmoid(gt[:, D_MODEL:])
        d_gates = jnp.concatenate([dm * ro * (sa * (1.0 - sa)), dm * ao * (sb * (1.0 - sb))], axis=-1)
        return dm * sa, dm * sb, d_gates
    return _rowmap("merge_bwd", body, [d_merged, gates, ret_out, att_out], [],
                   [(D_MODEL, BF16), (D_MODEL, BF16), (2 * D_MODEL, BF16)])


def _gate_bwd(name, d_x, branch, gate):
    def body(dx, br, g):
        return dx * g, jnp.sum(dx * br, axis=0, keepdims=True)
    return _rowmap(name, body, [d_x, branch], [gate], [(D_MODEL, BF16)], [D_MODEL])


def _loss_head(x3, target, gain):
    def body(xt, tt, g):
        xhat, r = _rms_parts(xt)
        err = xhat * g - tt
        d_y = err / D_MODEL
        loss = 0.5 * jnp.sum(jnp.mean(err * err, axis=-1, keepdims=True), axis=0, keepdims=True)
        d_x = _rms_bwd(d_y * g, xhat, r)
        return d_x, jnp.broadcast_to(loss, (1, 128)), jnp.sum(d_y * xhat, axis=0, keepdims=True)
    return _rowmap("loss_head", body, [x3, target], [gain], [(D_MODEL, F32)], [128, D_MODEL])


def _local_step(x, target, mod, norm1_g, norm2_g, norm_f_g, rel_bias, gn_g, gn_b,
                w_in, w_ret_out, w_att_out, w_o, w_ff1, w_ff2):
    sh1, sc1, g1, sh2, sc2, g2 = [mod[:, i * D_MODEL:(i + 1) * D_MODEL] for i in range(6)]
    cos, sin = _rope_tables()
    din, qd, kd, cd = _decay_tables()
    buckets = _bucket_tables()
    bias = _bias_build(rel_bias, buckets)
    dils = [d for _, d in ATT_GROUPS]
    nbs = [SEQ // d // ATT_BLK for d in dils]

    h1 = _prenorm_fwd("prenorm1_fwd", x, norm1_g, sh1, sc1)
    h1_p = [_permute_rows(h1, d) for d in dils]

    def rot_epi(acc, cs, sn, scale):
        half = RET_DK // 2
        x1, x2 = acc[:, :half], acc[:, half:]
        return (jnp.concatenate([x1 * cs - x2 * sn, x1 * sn + x2 * cs], axis=-1) * scale,)

    qk_scale = jnp.concatenate([jnp.ones((1, RET_QK_W), F32),
                                jnp.full((1, RET_QK_W), RET_DK ** -0.5, F32)], axis=-1)
    rope_ex = [(cos, (512, RET_DK // 2), lambda i, j, kk: (i, 0)),
               (sin, (512, RET_DK // 2), lambda i, j, kk: (i, 0)),
               (qk_scale, (1, RET_DK), lambda i, j, kk: (0, j))]
    rqk = _matmul("proj_qk", h1, w_in, "nn", SEQ, 2 * RET_QK_W, D_MODEL, [BF16], b_off=OFF_Q,
                  tn=RET_DK, tk=D_MODEL, epilogue=rot_epi, extras=rope_ex)[0]
    rv = _matmul("proj_rv", h1, w_in, "nn", SEQ, RET_V_W, D_MODEL, [BF16], b_off=OFF_V, tk=D_MODEL)[0]
    rg = _matmul("proj_rg", h1, w_in, "nn", SEQ, RET_V_W, D_MODEL, [F32], b_off=OFF_G, tk=D_MODEL)[0]
    gates = _matmul("proj_gates", h1, w_in, "nn", SEQ, 2 * D_MODEL, D_MODEL, [F32], b_off=OFF_GATE,
                    tk=D_MODEL)[0]
    aqkv = [_matmul(f"proj_att_g{gi}", h1_p[gi], w_in, "nn", SEQ, 3 * ATT_W, D_MODEL, [BF16],
                    b_off=OFF_ATT + gi * 3 * ATT_W, tk=D_MODEL)[0] for gi in range(3)]

    ro, states = _retention_fwd(rqk, rv, din, qd, kd, cd)
    gated = _retpost_fwd(ro, rg, gn_g, gn_b)
    ret_out = _matmul("ret_out", gated, w_ret_out, "nn", SEQ, D_MODEL, RET_V_W, [F32])[0]

    os_, ls_ = [], []
    for gi in range(3):
        o_g, l_g = _att_fwd(gi, aqkv[gi], bias, nbs[gi])
        os_.append(_unpermute_rows(o_g, dils[gi]))
        ls_.append(_unpermute_rows(l_g, dils[gi]))
    att, att_b, lse = _combine(os_, ls_)
    att_out = _matmul("att_out", att_b, w_att_out, "nn", SEQ, D_MODEL, ATT_W, [F32])[0]

    merged = _merge_fwd(gates, ret_out, att_out)

    def resid_epi(acc, xt, g):
        return xt + g * acc, acc

    def resid_ex(xin, g):
        return [(xin, (512, 512), lambda i, j, kk: (i, j)), (g, (1, 512), lambda i, j, kk: (0, j))]

    x2, mix = _matmul("mix_out", merged, w_o, "nn", SEQ, D_MODEL, D_MODEL, [F32, F32],
                      epilogue=resid_epi, extras=resid_ex(x, g1))
    h2 = _prenorm_fwd("prenorm2_fwd", x2, norm2_g, sh2, sc2)

    def relu2_epi(acc):
        r = jnp.maximum(acc, 0.0)
        return r * r, acc

    act, u = _matmul("ff1", h2, w_ff1, "nn", SEQ, D_FF, D_MODEL, [BF16, F32], tk=D_MODEL,
                     epilogue=relu2_epi)
    x3, y2 = _matmul("ff2", act, w_ff2, "nn", SEQ, D_MODEL, D_FF, [F32, F32],
                     epilogue=resid_epi, extras=resid_ex(x2, g2))

    d_x3, loss, d_gf = _loss_head(x3, target, norm_f_g)

    d_y2, d_g2 = _gate_bwd("ff_gate_bwd", d_x3, y2, g2)

    def relu2_bwd_epi(acc, ut):
        return (acc * (2.0 * jnp.maximum(ut, 0.0)),)

    d_u = _matmul("ff2_dx", d_y2, w_ff2, "nt", SEQ, D_FF, D_MODEL, [BF16], epilogue=relu2_bwd_epi,
                  extras=[(u, (512, 512), lambda i, j, kk: (i, j))])[0]
    gw_ff2 = _matmul("ff2_dw", act, d_y2, "tn", D_FF, D_MODEL, SEQ, [BF16])[0]
    d_h2 = _matmul("ff1_dx", d_u, w_ff1, "nt", SEQ, D_MODEL, D_FF, [F32])[0]
    gw_ff1 = _matmul("ff1_dw", h2, d_u, "tn", D_MODEL, D_FF, SEQ, [BF16])[0]
    d_x2, d_sh2, d_sc2, d_n2g = _prenorm_bwd("prenorm2_bwd", [d_h2], x2, norm2_g, sc2, d_x3)

    d_mix, d_g1 = _gate_bwd("mix_gate_bwd", d_x2, mix, g1)
    d_merged = _matmul("mix_dx", d_mix, w_o, "nt", SEQ, D_MODEL, D_MODEL, [F32])[0]
    gw_o = _matmul("mix_dw", merged, d_mix, "tn", D_MODEL, D_MODEL, SEQ, [BF16])[0]
    d_ret_out, d_att_out, d_gates = _merge_bwd(d_merged, gates, ret_out, att_out)

    d_gated = _matmul("ret_out_dx", d_ret_out, w_ret_out, "nt", SEQ, RET_V_W, D_MODEL, [F32])[0]
    gw_ret_out = _matmul("ret_out_dw", gated, d_ret_out, "tn", RET_V_W, D_MODEL, SEQ, [BF16])[0]
    d_att = _matmul("att_out_dx", d_att_out, w_att_out, "nt", SEQ, ATT_W, D_MODEL, [F32])[0]
    gw_att_out = _matmul("att_out_dw", att_b, d_att_out, "tn", ATT_W, D_MODEL, SEQ, [BF16])[0]

    d_ro, d_rg, d_gn_g, d_gn_b = _retpost_bwd(d_gated, ro, rg, gn_g, gn_b)
    d_rq, d_rk, d_rv = _retention_bwd(rqk, rv, states, d_ro, din, qd, kd, cd, cos, sin)

    d_att_b, dd = _att_bwd_pre(d_att, att)
    d_aqkv, dsbs = [], []
    for gi in range(3):
        da_p = _permute_rows(d_att_b, dils[gi])
        l_p = _permute_rows(lse, dils[gi])
        dd_p = _permute_rows(dd, dils[gi])
        dq, dsb = _att_bwd_dq(gi, aqkv[gi], da_p, l_p, dd_p, bias, nbs[gi])
        dk, dv = _att_bwd_dkv(gi, aqkv[gi], da_p, l_p, dd_p, bias, nbs[gi])
        d_aqkv.append(jnp.concatenate([dq, dk, dv], axis=-1))
        dsbs.append(dsb)
    d_rel_bias = _bias_grad(jnp.stack(dsbs), buckets)

    pieces = [(d_rq, OFF_Q, h1, 1), (d_rk, OFF_K, h1, 1), (d_rv, OFF_V, h1, 1), (d_rg, OFF_G, h1, 1),
              (d_gates, OFF_GATE, h1, 1)]
    pieces += [(d_aqkv[gi], OFF_ATT + gi * 3 * ATT_W, h1_p[gi], dils[gi]) for gi in range(3)]
    d_h1s, gw_in_parts = [], {}
    for idx, (dp, off, hin, dil) in enumerate(pieces):
        wcols = dp.shape[1]
        d_h = _matmul(f"proj_dx_{idx}", dp, w_in, "nt", SEQ, D_MODEL, wcols, [F32], b_off=off)[0]
        d_h1s.append(_unpermute_rows(d_h, dil))
        gw_in_parts[off] = _matmul(f"proj_dw_{idx}", hin, dp, "tn", D_MODEL, wcols, SEQ, [BF16])[0]
    gw_in = jnp.concatenate([gw_in_parts[o] for o in sorted(gw_in_parts)], axis=-1)

    grad_x, d_sh1, d_sc1, d_n1g = _prenorm_bwd("prenorm1_bwd", d_h1s, x, norm1_g, sc1, d_x2)
    d_mod = jnp.concatenate([d_sh1, d_sc1, d_g1, d_sh2, d_sc2, d_g2], axis=-1)
    small = dict(norm1_g=d_n1g, norm2_g=d_n2g, norm_f_g=d_gf, gn_g=d_gn_g, gn_b=d_gn_b,
                 rel_bias=d_rel_bias)
    big = dict(w_in=gw_in, w_ret_out=gw_ret_out, w_att_out=gw_att_out, w_o=gw_o, w_ff1=gw_ff1,
               w_ff2=gw_ff2)
    return loss, grad_x, d_mod, small, big


def _me():
    return lax.axis_index("x"), lax.axis_index("y"), lax.axis_index("c")


def _peer(x, y, c, mask):
    return (x ^ ((mask >> 2) & 1), y ^ ((mask >> 1) & 1), c ^ (mask & 1))


def _gather8(src_ref, dst_ref, send_sems, recv_sems):
    x, y, c = _me()
    me = 4 * x + 2 * y + c
    copies = []
    for mask in range(1, N_DEV):
        cp = pltpu.make_async_remote_copy(
            src_ref=src_ref, dst_ref=dst_ref.at[me], send_sem=send_sems.at[mask - 1],
            recv_sem=recv_sems.at[mask - 1], device_id=_peer(x, y, c, mask), device_id_type=MESH)
        cp.start()
        copies.append(cp)
    dst_ref[me] = src_ref[...]
    for cp in copies:
        cp.wait_recv()
    for cp in copies:
        cp.wait_send()


def _ada_fwd(c_in, w_ada, b_ada):
    ncol = ADA_COLS // N_CHIPS

    def body(c_ref, w_ref, b_ref, mod_ref, sc_ref, cbuf, cg, mbuf, mg, s1, r1, s2, r2):
        x, y, c = _me()
        me = 4 * x + 2 * y + c
        cv = c_ref[...]
        cbuf[...] = jnp.broadcast_to(cv * _sigmoid(cv), cbuf.shape)
        _gather8(cbuf, cg, s1, r1)
        rows = lax.broadcasted_iota(I32, (N_DEV, D_MODEL), 0)
        sc_all = jnp.zeros((N_DEV, D_MODEL), F32)
        for d in range(N_DEV):
            sc_all = jnp.where(rows == d, cg[d], sc_all)
        sc_ref[...] = sc_all
        mbuf[...] = jnp.dot(sc_all.astype(BF16), w_ref[...].astype(BF16), preferred_element_type=F32)
        _gather8(mbuf, mg, s2, r2)
        rowsel = lax.broadcasted_iota(I32, (N_DEV, ncol), 0) == me
        for k in range(N_CHIPS):
            blk = mg[2 * k]
            row = jnp.sum(jnp.where(rowsel, blk, 0.0), axis=0, keepdims=True)
            mod_ref[:, k * ncol:(k + 1) * ncol] = row + b_ref[:, k * ncol:(k + 1) * ncol]

    vm = pl.BlockSpec(memory_space=pltpu.VMEM)
    return pl.pallas_call(
        body, name="ada_fwd",
        in_specs=[vm, vm, vm], out_specs=[vm, vm],
        out_shape=[jax.ShapeDtypeStruct((1, ADA_COLS), F32), jax.ShapeDtypeStruct((N_DEV, D_MODEL), F32)],
        scratch_shapes=[
            pltpu.VMEM((8, D_MODEL), F32), pltpu.VMEM((N_DEV, 8, D_MODEL), F32),
            pltpu.VMEM((8, ncol), F32), pltpu.VMEM((N_DEV, 8, ncol), F32),
            pltpu.SemaphoreType.DMA((N_DEV - 1,)), pltpu.SemaphoreType.DMA((N_DEV - 1,)),
            pltpu.SemaphoreType.DMA((N_DEV - 1,)), pltpu.SemaphoreType.DMA((N_DEV - 1,)),
        ],
        compiler_params=pltpu.CompilerParams(vmem_limit_bytes=VMEM_LIMIT_V7X),
    )(c_in, w_ada, b_ada)


def _small_reduce(pack, sc_all):
    ncol = ADA_COLS // N_CHIPS

    def body(p_ref, sc_ref, tot_ref, gw_ref, pg, s1, r1):
        x, y, _ = _me()
        chip = 2 * x + y
        _gather8(p_ref, pg, s1, r1)
        tot = pg[0]
        for d in range(1, N_DEV):
            tot = tot + pg[d]
        tot_ref[...] = tot
        rows = lax.broadcasted_iota(I32, (N_DEV, ncol), 0)
        dmod = jnp.zeros((N_DEV, ncol), F32)
        for k in range(N_CHIPS):
            part = jnp.zeros((N_DEV, ncol), F32)
            for d in range(N_DEV):
                part = jnp.where(rows == d, pg[d, :, k * ncol:(k + 1) * ncol][0:1, :], part)
            dmod = jnp.where(chip == k, part, dmod)
        gw_ref[...] = lax.dot_general(sc_ref[...].astype(BF16), dmod.astype(BF16), _TN,
                                      preferred_element_type=F32)

    vm = pl.BlockSpec(memory_space=pltpu.VMEM)
    return pl.pallas_call(
        body, name="small_reduce",
        in_specs=[vm, vm], out_specs=[vm, vm],
        out_shape=[jax.ShapeDtypeStruct((8, ADA_COLS), F32), jax.ShapeDtypeStruct((D_MODEL, ncol), F32)],
        scratch_shapes=[pltpu.VMEM((N_DEV, 8, ADA_COLS), F32),
                        pltpu.SemaphoreType.DMA((N_DEV - 1,)), pltpu.SemaphoreType.DMA((N_DEV - 1,))],
        compiler_params=pltpu.CompilerParams(vmem_limit_bytes=VMEM_LIMIT_V7X),
    )(pack, sc_all)


BIG = (("w_in", 1), ("w_ret_out", 0), ("w_att_out", 1), ("w_o", 0), ("w_ff1", 1), ("w_ff2", 0))
_CHIP_FLIPS = ((1, 0), (0, 1), (1, 1))


def _region(ref, axis, chip, half, shard_shape):
    r, cw = shard_shape
    hr = r // 2
    if axis == 1:
        return ref.at[pl.ds(half * hr, hr), pl.ds(chip * cw, cw)]
    return ref.at[pl.ds(chip * r + half * hr, hr), :]


def _gather_weights(shards):
    nw = len(BIG)
    shapes = [s.shape for s in shards]
    full_shapes = [(r, N_CHIPS * cw) if ax == 1 else (N_CHIPS * r, cw)
                   for (r, cw), (_, ax) in zip(shapes, BIG)]

    def body(*refs):
        ins, outs = refs[:nw], refs[nw:2 * nw]
        loc_sem, s_ici, r_ici, s_d2d, r_d2d = refs[2 * nw:]
        x, y, c = _me()
        chip = 2 * x + y
        local, first, passed = [], [], []
        for i, (_, ax) in enumerate(BIG):
            r, cw = shapes[i]
            own = outs[i].at[:, pl.ds(chip * cw, cw)] if ax == 1 else outs[i].at[pl.ds(chip * r, r), :]
            cp = pltpu.make_async_copy(ins[i], own, loc_sem.at[i])
            cp.start()
            local.append(cp)
        for j, (fx, fy) in enumerate(_CHIP_FLIPS):
            for i, (_, ax) in enumerate(BIG):
                hr = shapes[i][0] // 2
                cp = pltpu.make_async_remote_copy(
                    src_ref=ins[i].at[pl.ds(c * hr, hr), :],
                    dst_ref=_region(outs[i], ax, chip, c, shapes[i]),
                    send_sem=s_ici.at[j * nw + i], recv_sem=r_ici.at[j * nw + i],
                    device_id=(x ^ fx, y ^ fy, c), device_id_type=MESH)
                cp.start()
                first.append(cp)
        for j, (fx, fy) in enumerate(_CHIP_FLIPS):
            src_chip = 2 * (x ^ fx) + (y ^ fy)
            for i, (_, ax) in enumerate(BIG):
                reg = _region(outs[i], ax, src_chip, c, shapes[i])
                first[j * nw + i].wait_recv()
                cp = pltpu.make_async_remote_copy(
                    src_ref=reg, dst_ref=reg, send_sem=s_d2d.at[j * nw + i],
                    recv_sem=r_d2d.at[j * nw + i], device_id=(x, y, 1 - c), device_id_type=MESH)
                cp.start()
                passed.append(cp)
        for cp in passed:
            cp.wait_recv()
        for cp in first + passed:
            cp.wait_send()
        for cp in local:
            cp.wait()

    hbm = pl.BlockSpec(memory_space=pl.ANY)
    return pl.pallas_call(
        body, name="gather_weights",
        in_specs=[hbm] * nw, out_specs=[hbm] * nw,
        out_shape=[jax.ShapeDtypeStruct(fs, BF16) for fs in full_shapes],
        scratch_shapes=[pltpu.SemaphoreType.DMA((nw,)),
                        pltpu.SemaphoreType.DMA((3 * nw,)), pltpu.SemaphoreType.DMA((3 * nw,)),
                        pltpu.SemaphoreType.DMA((3 * nw,)), pltpu.SemaphoreType.DMA((3 * nw,))],
    )(*shards)


def _half_shapes(shard_shapes):
    return [(r // 2, cw) for r, cw in shard_shapes]


def _rs_pair(grads, shard_shapes):
    nw = len(BIG)
    hs = _half_shapes(shard_shapes)

    def body(*refs):
        ins, own, got = refs[:nw], refs[nw:2 * nw], refs[2 * nw:3 * nw]
        loc_sem, s_sem, r_sem = refs[3 * nw:]
        x, y, c = _me()
        local, sent = [], []
        for i, (_, ax) in enumerate(BIG):
            for k in range(N_CHIPS):
                cp = pltpu.make_async_copy(_region(ins[i], ax, k, c, shard_shapes[i]), own[i].at[k],
                                           loc_sem.at[i * N_CHIPS + k])
                cp.start()
                local.append(cp)
                rc = pltpu.make_async_remote_copy(
                    src_ref=_region(ins[i], ax, k, 1 - c, shard_shapes[i]), dst_ref=got[i].at[k],
                    send_sem=s_sem.at[i * N_CHIPS + k], recv_sem=r_sem.at[i * N_CHIPS + k],
                    device_id=(x, y, 1 - c), device_id_type=MESH)
                rc.start()
                sent.append(rc)
        for rc in sent:
            rc.wait_recv()
        for rc in sent:
            rc.wait_send()
        for cp in local:
            cp.wait()

    hbm = pl.BlockSpec(memory_space=pl.ANY)
    shp = [jax.ShapeDtypeStruct((N_CHIPS,) + h, BF16) for h in hs]
    res = pl.pallas_call(
        body, name="rs_pair", in_specs=[hbm] * nw, out_specs=[hbm] * (2 * nw), out_shape=shp + shp,
        scratch_shapes=[pltpu.SemaphoreType.DMA((nw * N_CHIPS,)), pltpu.SemaphoreType.DMA((nw * N_CHIPS,)),
                        pltpu.SemaphoreType.DMA((nw * N_CHIPS,))],
    )(*grads)
    return res[:nw], res[nw:]


def _add_stack(name, parts, out_dtype):
    def one(idx, arrs, reduce_k):
        k, r, cw = arrs[0].shape
        tr = min(r, 128)

        if reduce_k:
            def kern(a_ref, o_ref):
                tot = a_ref[0].astype(F32)
                for kk in range(1, k):
                    tot = tot + a_ref[kk].astype(F32)
                o_ref[...] = tot.astype(o_ref.dtype)
            in_specs = [pl.BlockSpec((k, tr, cw), lambda i: (0, i, 0))]
            out_spec = pl.BlockSpec((tr, cw), lambda i: (i, 0))
            out_shape = jax.ShapeDtypeStruct((r, cw), out_dtype)
            grid = (r // tr,)
        else:
            def kern(a_ref, b_ref, o_ref):
                o_ref[...] = (a_ref[...].astype(F32) + b_ref[...].astype(F32)).astype(o_ref.dtype)
            in_specs = [pl.BlockSpec((None, tr, cw), lambda kk, i: (kk, i, 0))] * 2
            out_spec = pl.BlockSpec((None, tr, cw), lambda kk, i: (kk, i, 0))
            out_shape = jax.ShapeDtypeStruct((k, r, cw), out_dtype)
            grid = (k, r // tr)
        return pl.pallas_call(
            kern, name=f"{name}_{idx}", grid=grid, in_specs=in_specs, out_specs=out_spec,
            out_shape=out_shape, compiler_params=_cparams(("parallel",) * len(grid)),
        )(*arrs)
    return one


def _rs_chips(psums, shard_shapes):
    nw = len(BIG)
    hs = _half_shapes(shard_shapes)

    def body(*refs):
        ins, outs = refs[:nw], refs[nw:2 * nw]
        loc_sem, s_sem, r_sem = refs[2 * nw:]
        x, y, c = _me()
        chip = 2 * x + y
        local, sent = [], []
        for i in range(nw):
            cp = pltpu.make_async_copy(ins[i].at[chip], outs[i].at[3], loc_sem.at[i])
            cp.start()
            local.append(cp)
            for j, (fx, fy) in enumerate(_CHIP_FLIPS):
                dst_chip = 2 * (x ^ fx) + (y ^ fy)
                rc = pltpu.make_async_remote_copy(
                    src_ref=ins[i].at[dst_chip], dst_ref=outs[i].at[j],
                    send_sem=s_sem.at[i * 3 + j], recv_sem=r_sem.at[i * 3 + j],
                    device_id=(x ^ fx, y ^ fy, c), device_id_type=MESH)
                rc.start()
                sent.append(rc)
        for rc in sent:
            rc.wait_recv()
        for rc in sent:
            rc.wait_send()
        for cp in local:
            cp.wait()

    hbm = pl.BlockSpec(memory_space=pl.ANY)
    return pl.pallas_call(
        body, name="rs_chips", in_specs=[hbm] * nw, out_specs=[hbm] * nw,
        out_shape=[jax.ShapeDtypeStruct((N_CHIPS,) + h, BF16) for h in hs],
        scratch_shapes=[pltpu.SemaphoreType.DMA((nw,)), pltpu.SemaphoreType.DMA((nw * 3,)),
                        pltpu.SemaphoreType.DMA((nw * 3,))],
    )(*psums)


def _share_halves(halves, shard_shapes):
    nw = len(BIG)

    def body(*refs):
        ins, outs = refs[:nw], refs[nw:2 * nw]
        loc_sem, s_sem, r_sem = refs[2 * nw:]
        x, y, c = _me()
        local, sent = [], []
        for i in range(nw):
            hr = shard_shapes[i][0] // 2
            mine = outs[i].at[pl.ds(c * hr, hr), :]
            cp = pltpu.make_async_copy(ins[i], mine, loc_sem.at[i])
            cp.start()
            local.append(cp)
            rc = pltpu.make_async_remote_copy(
                src_ref=ins[i], dst_ref=mine, send_sem=s_sem.at[i], recv_sem=r_sem.at[i],
                device_id=(x, y, 1 - c), device_id_type=MESH)
            rc.start()
            sent.append(rc)
        for rc in sent:
            rc.wait_recv()
        for rc in sent:
            rc.wait_send()
        for cp in local:
            cp.wait()

    hbm = pl.BlockSpec(memory_space=pl.ANY)
    return pl.pallas_call(
        body, name="rs_share", in_specs=[hbm] * nw, out_specs=[hbm] * nw,
        out_shape=[jax.ShapeDtypeStruct(s, F32) for s in shard_shapes],
        scratch_shapes=[pltpu.SemaphoreType.DMA((nw,)), pltpu.SemaphoreType.DMA((nw,)),
                        pltpu.SemaphoreType.DMA((nw,))],
    )(*halves)


def _reduce_scatter(grads, shard_shapes):
    own, got = _rs_pair(grads, shard_shapes)
    add2 = _add_stack("rs_add_pair", None, BF16)
    psums = [add2(i, [own[i], got[i]], False) for i in range(len(BIG))]
    recv = _rs_chips(psums, shard_shapes)
    add4 = _add_stack("rs_add_chips", None, F32)
    halves = [add4(i, [recv[i]], True) for i in range(len(BIG))]
    return _share_halves(halves, shard_shapes)


def _adamw(name, w, g, m, v):
    r, cw = w.shape
    tr = min(r, 128)

    def kern(w_ref, g_ref, m_ref, v_ref, d_ref, nm_ref, nv_ref):
        gv = g_ref[...]
        mn = ADAM_B1 * m_ref[...] + (1.0 - ADAM_B1) * gv
        vn = ADAM_B2 * v_ref[...] + (1.0 - ADAM_B2) * (gv * gv)
        m_hat = mn / (1.0 - ADAM_B1 ** ADAM_STEP)
        v_hat = vn / (1.0 - ADAM_B2 ** ADAM_STEP)
        d_ref[...] = -ADAM_LR * (m_hat / (jnp.sqrt(v_hat) + ADAM_EPS) + ADAM_WD * w_ref[...])
        nm_ref[...] = mn
        nv_ref[...] = vn

    spec = pl.BlockSpec((tr, cw), lambda i: (i, 0))
    return pl.pallas_call(
        kern, name=name, grid=(r // tr,), in_specs=[spec] * 4, out_specs=[spec] * 3,
        out_shape=[jax.ShapeDtypeStruct((r, cw), F32)] * 3, compiler_params=_cparams(("parallel",)),
    )(w, g, m, v)


_PACK_W = ADA_COLS
_NB = REL_BUCKETS * N_ATT_HEADS
_SMALL_SLOTS = {
    "b_ada": (0, 0, ADA_COLS),
    "norm1_g": (1, 0, D_MODEL), "norm2_g": (1, D_MODEL, D_MODEL), "norm_f_g": (1, 2 * D_MODEL, D_MODEL),
    "ret_gn_g": (1, 3 * D_MODEL, RET_V_W),
    "ret_gn_b": (2, 0, RET_V_W), "rel_bias": (2, RET_V_W, _NB), "loss": (2, RET_V_W + 512, 128),
}


def _pack_small(vals):
    rows = []
    for r in range(8):
        items = sorted([(off, n) for n, (rr, off, _) in _SMALL_SLOTS.items() if rr == r and n in vals])
        parts, pos = [], 0
        for off, n in items:
            if off > pos:
                parts.append(jnp.zeros((1, off - pos), F32))
            parts.append(vals[n].reshape(1, -1).astype(F32))
            pos = off + _SMALL_SLOTS[n][2]
        if pos < _PACK_W:
            parts.append(jnp.zeros((1, _PACK_W - pos), F32))
        rows.append(jnp.concatenate(parts, axis=-1))
    return jnp.concatenate(rows, axis=0)


def _unpack_small(pack, name):
    r, off, wd = _SMALL_SLOTS[name]
    return pack[r:r + 1, off:off + wd]


def kernel(x, c, w_ada, b_ada, norm1_g, w_in, rel_bias, ret_gn_g, ret_gn_b, w_ret_out, w_att_out, w_o, norm2_g, w_ff1, w_ff2, norm_f_g, loss_target, m_w_ada, m_b_ada, m_norm1_g, m_w_in, m_rel_bias, m_ret_gn_g, m_ret_gn_b, m_w_ret_out, m_w_att_out, m_w_o, m_norm2_g, m_w_ff1, m_w_ff2, m_norm_f_g, v_w_ada, v_b_ada, v_norm1_g, v_w_in, v_rel_bias, v_ret_gn_g, v_ret_gn_b, v_w_ret_out, v_w_att_out, v_w_o, v_norm2_g, v_w_ff1, v_w_ff2, v_norm_f_g):
    given = dict(locals())
    big_names = [n for n, _ in BIG]
    shard_w = {n: given[n][0] for n in big_names}
    shard_shapes = [shard_w[n].shape for n in big_names]

    full = _gather_weights([shard_w[n].astype(BF16) for n in big_names])
    full = dict(zip(big_names, full))
    mod, sc_all = _ada_fwd(c, w_ada[0], b_ada)

    loss, grad_x, d_mod, small, big = _local_step(
        x[0], loss_target[0], mod, norm1_g, norm2_g, norm_f_g.reshape(1, -1), rel_bias, ret_gn_g,
        ret_gn_b, full["w_in"], full["w_ret_out"], full["w_att_out"], full["w_o"], full["w_ff1"],
        full["w_ff2"])

    pack_g = _pack_small(dict(b_ada=d_mod, norm1_g=small["norm1_g"], norm2_g=small["norm2_g"],
                              norm_f_g=small["norm_f_g"], ret_gn_g=small["gn_g"], ret_gn_b=small["gn_b"],
                              rel_bias=small["rel_bias"], loss=loss))
    tot, g_w_ada = _small_reduce(pack_g, sc_all)
    g_big = _reduce_scatter([big[n] for n in big_names], shard_shapes)
    g_big = dict(zip(big_names, g_big))

    small_names = ["b_ada", "norm1_g", "rel_bias", "ret_gn_g", "ret_gn_b", "norm2_g", "norm_f_g"]
    pack_w = _pack_small({n: given[n] for n in small_names})
    pack_m = _pack_small({n: given["m_" + n] for n in small_names})
    pack_v = _pack_small({n: given["v_" + n] for n in small_names})
    sd, sm, sv = _adamw("adamw_small", pack_w, tot, pack_m, pack_v)

    grads, deltas, new_m, new_v = {}, {}, {}, {}
    for n in small_names:
        shp = given[n].shape
        grads[n] = _unpack_small(tot, n).reshape(shp)
        deltas[n] = _unpack_small(sd, n).reshape(shp)
        new_m[n] = _unpack_small(sm, n).reshape(shp)
        new_v[n] = _unpack_small(sv, n).reshape(shp)
    for n in ["w_ada"] + big_names:
        g = g_w_ada if n == "w_ada" else g_big[n]
        d, nm, nv = _adamw("adamw_" + n, given[n][0], g, given["m_" + n][0], given["v_" + n][0])
        grads[n], deltas[n], new_m[n], new_v[n] = g[None], d[None], nm[None], nv[None]

    order = ["w_ada", "b_ada", "norm1_g", "w_in", "rel_bias", "ret_gn_g", "ret_gn_b", "w_ret_out",
             "w_att_out", "w_o", "norm2_g", "w_ff1", "w_ff2", "norm_f_g"]
    loss_out = _unpack_small(tot, "loss")[0, 0]
    return (loss_out, grad_x[None], *[grads[n] for n in order], *[deltas[n] for n in order],
            *[new_m[n] for n in order], *[new_v[n] for n in order])
```

```python
import functools
import math

import jax
import jax.numpy as jnp
from jax import lax
from jax.experimental import pallas as pl
from jax.experimental.pallas import tpu as pltpu

F32 = jnp.float32
BF16 = jnp.bfloat16
I32 = jnp.int32

SEQ = 2048
D_MODEL = 1024
RET_HEADS = 4
RET_DK = 256
RET_DV = 512
RET_CHUNK = 128
RET_QK_W = RET_HEADS * RET_DK
RET_V_W = RET_HEADS * RET_DV
ATT_GROUPS = ((128, 1), (512, 4), (2048, 16))
ATT_HPG = 4
ATT_DH = 128
ATT_W = ATT_HPG * ATT_DH
ATT_BLK = 128
N_BLK = SEQ // ATT_BLK
REL_BUCKETS = 32
REL_MAX_DIST = 2048
N_ATT_HEADS = 12
D_FF = 4 * D_MODEL
RMS_EPS = 1e-6
GN_EPS = 1e-5
ROPE_BASE = 10000.0
IN_COLS = 2 * RET_QK_W + 2 * RET_V_W + 9 * ATT_W + 2 * D_MODEL
OFF_Q, OFF_K, OFF_V, OFF_G = 0, RET_QK_W, 2 * RET_QK_W, 2 * RET_QK_W + RET_V_W
OFF_ATT = 2 * RET_QK_W + 2 * RET_V_W
OFF_GATE = OFF_ATT + 9 * ATT_W
N_CHIPS = 4
N_DEV = 8
ADA_COLS = 6 * D_MODEL

ADAM_LR = 0.001
ADAM_B1 = 0.9
ADAM_B2 = 0.999
ADAM_EPS = 1e-08
ADAM_WD = 0.01
ADAM_STEP = 10

VMEM_LIMIT_V7X = 56 * 1024 * 1024
MESH = pl.DeviceIdType.MESH


def _cparams(sem):
    return pltpu.CompilerParams(dimension_semantics=sem, vmem_limit_bytes=VMEM_LIMIT_V7X)


def _sigmoid(v):
    return 1.0 / (1.0 + jnp.exp(-v))


def _rowmap(name, body, row_ins, bcast_ins, row_outs, sum_outs=(), tm=256):
    m = row_ins[0].shape[0]
    n_in = len(row_ins) + len(bcast_ins)
    n_ro = len(row_outs)

    def kern(*refs):
        vals = [r[...] for r in refs[:n_in]]
        res = body(*vals)
        if not isinstance(res, (tuple, list)):
            res = (res,)
        outs = refs[n_in:]
        for r, v in zip(outs[:n_ro], res[:n_ro]):
            r[...] = v.astype(r.dtype)
        if sum_outs:
            @pl.when(pl.program_id(0) == 0)
            def _():
                for r in outs[n_ro:]:
                    r[...] = jnp.zeros_like(r)
            for r, v in zip(outs[n_ro:], res[n_ro:]):
                r[...] += v

    in_specs = [pl.BlockSpec((tm, a.shape[1]), lambda i: (i, 0)) for a in row_ins]
    in_specs += [pl.BlockSpec(a.shape, lambda i: (0, 0)) for a in bcast_ins]
    out_specs = [pl.BlockSpec((tm, n), lambda i: (i, 0)) for n, _ in row_outs]
    out_specs += [pl.BlockSpec((1, n), lambda i: (0, 0)) for n in sum_outs]
    out_shape = [jax.ShapeDtypeStruct((m, n), dt) for n, dt in row_outs]
    out_shape += [jax.ShapeDtypeStruct((1, n), F32) for n in sum_outs]
    return pl.pallas_call(
        kern, name=name, grid=(m // tm,), in_specs=in_specs, out_specs=out_specs,
        out_shape=out_shape, compiler_params=_cparams(("arbitrary",)),
    )(*row_ins, *bcast_ins)


TM, TN = 1024, 512


def _matmul(name, a, b, kind, m, n, k, outs, *, b_off=0, tm=TM, tn=TN, tk=512,
            epilogue=None, extras=()):
    tm, tn, tk = min(tm, m), min(tn, n), min(tk, k)
    nk = k // tk
    if kind == "nn":
        a_spec = pl.BlockSpec((tm, tk), lambda i, j, kk: (i, kk))
        b_spec = pl.BlockSpec((tk, tn), lambda i, j, kk: (kk, b_off // tn + j))
        dn = (((1,), (0,)), ((), ()))
    elif kind == "nt":
        a_spec = pl.BlockSpec((tm, tk), lambda i, j, kk: (i, kk))
        b_spec = pl.BlockSpec((tn, tk), lambda i, j, kk: (j, b_off // tk + kk))
        dn = (((1,), (1,)), ((), ()))
    else:
        a_spec = pl.BlockSpec((tk, tm), lambda i, j, kk: (kk, i))
        b_spec = pl.BlockSpec((tk, tn), lambda i, j, kk: (kk, j))
        dn = (((0,), (0,)), ((), ()))
    n_ex, n_out = len(extras), len(outs)
    if epilogue is None:
        epilogue = lambda acc: (acc,)

    def finish(acc, ex_refs, out_refs):
        res = epilogue(acc, *[r[...] for r in ex_refs])
        for r, v in zip(out_refs, res):
            r[...] = v.astype(r.dtype)

    def kern_one(a_ref, b_ref, *rest):
        part = lax.dot_general(a_ref[...], b_ref[...], dn, preferred_element_type=F32)
        finish(part, rest[:n_ex], rest[n_ex:n_ex + n_out])

    def kern_acc(a_ref, b_ref, *rest):
        ex_refs, out_refs, acc_ref = rest[:n_ex], rest[n_ex:n_ex + n_out], rest[-1]
        kk = pl.program_id(2)
        part = lax.dot_general(a_ref[...], b_ref[...], dn, preferred_element_type=F32)

        @pl.when(kk == 0)
        def _():
            acc_ref[...] = part

        @pl.when(kk > 0)
        def _():
            acc_ref[...] += part

        @pl.when(kk == nk - 1)
        def _():
            finish(acc_ref[...], ex_refs, out_refs)

    in_specs = [a_spec, b_spec] + [pl.BlockSpec(bs, im) for _, bs, im in extras]
    out_specs = [pl.BlockSpec((tm, tn), lambda i, j, kk: (i, j)) for _ in outs]
    out_shape = [jax.ShapeDtypeStruct((m, n), dt) for dt in outs]
    res = pl.pallas_call(
        kern_one if nk == 1 else kern_acc, name=name, grid=(m // tm, n // tn, nk), in_specs=in_specs,
        out_specs=out_specs, out_shape=out_shape,
        scratch_shapes=[] if nk == 1 else [pltpu.VMEM((tm, tn), F32)],
        compiler_params=_cparams(("parallel", "parallel", "arbitrary")),
    )(a, b, *[e[0] for e in extras])
    return res


def _where_am_i():
    x, y, c = _me()
    return jnp.stack([c, 2 * x + y]).astype(I32)


def _sibling():
    x, y, c = _me()
    return (x, y, 1 - c)


def _matmul_tn_pair(name, pos, a, b, m, n, k, shard_rows, *, tm, tn, tk):
    hr = shard_rows // 2
    tm, tn, tk = min(tm, hr), min(tn, n), min(tk, k)
    tph = hr // tm
    nt, nj, nk = (m // 2) // tm, n // tn, k // tk
    n_tiles = nt * nj

    def row_block(p, t, pos_ref):
        half = jnp.where(p == 0, 1 - pos_ref[0], pos_ref[0])
        return (t // tph) * (2 * tph) + half * tph + t % tph

    def kern(pos_ref, a_ref, b_ref, o_ref, acc_ref, send_buf, land_buf, s_sem, r_sem):
        p, t, j, kk = pl.program_id(0), pl.program_id(1), pl.program_id(2), pl.program_id(3)
        idx = t * nj + j
        sib = _sibling()

        def copy(i):
            return pltpu.make_async_remote_copy(
                src_ref=send_buf.at[i], dst_ref=land_buf.at[i], send_sem=s_sem.at[i],
                recv_sem=r_sem.at[i], device_id=sib, device_id_type=MESH)

        part = lax.dot_general(a_ref[...], b_ref[...], _TN, preferred_element_type=F32)

        @pl.when(kk == 0)
        def _():
            acc_ref[...] = part

        @pl.when(kk > 0)
        def _():
            acc_ref[...] += part

        @pl.when(jnp.logical_and(kk == nk - 1, p == 0))
        def _():
            send_buf[idx] = acc_ref[...].astype(BF16)
            copy(idx).start()

        @pl.when(jnp.logical_and(kk == nk - 1, p == 1))
        def _():
            copy(idx).wait_recv()
            o_ref[...] = (acc_ref[...] + land_buf[idx].astype(F32)).astype(BF16)

        @pl.when(jnp.logical_and(jnp.logical_and(p == 1, idx == n_tiles - 1), kk == nk - 1))
        def _():
            for i in range(n_tiles):
                copy(i).wait_send()

    grid_spec = pltpu.PrefetchScalarGridSpec(
        num_scalar_prefetch=1, grid=(2, nt, nj, nk),
        in_specs=[pl.BlockSpec((tk, tm), lambda p, t, j, kk, pos_ref: (kk, row_block(p, t, pos_ref))),
                  pl.BlockSpec((tk, tn), lambda p, t, j, kk, pos_ref: (kk, j))],
        out_specs=pl.BlockSpec((tm, tn), lambda p, t, j, kk, pos_ref: (p * t, p * j)),
        scratch_shapes=[pltpu.VMEM((tm, tn), F32), pltpu.VMEM((n_tiles, tm, tn), BF16),
                        pltpu.VMEM((n_tiles, tm, tn), BF16),
                        pltpu.SemaphoreType.DMA((n_tiles,)), pltpu.SemaphoreType.DMA((n_tiles,))])
    return pl.pallas_call(
        kern, name=name, grid_spec=grid_spec, out_shape=jax.ShapeDtypeStruct((m // 2, n), BF16),
        compiler_params=_cparams(("arbitrary",) * 4),
    )(pos, a, b)


def _rope_tables():
    half = RET_DK // 2
    inv = ROPE_BASE ** (-jnp.arange(half, dtype=F32) / half)
    ang = jnp.arange(SEQ).astype(F32)[:, None] * inv[None, :]
    return jnp.cos(ang), jnp.sin(ang)


def _decay_tables():
    c = RET_CHUNK
    log_g = jnp.log1p(-(2.0 ** (-5.0 - jnp.arange(RET_HEADS, dtype=F32))))
    idx = jnp.arange(c, dtype=F32)
    rel = idx[:, None] - idx[None, :]
    din = jnp.where(rel >= 0, jnp.exp(log_g[:, None, None] * jnp.maximum(rel, 0.0)), 0.0)
    qd = jnp.exp(log_g[:, None] * (idx + 1.0))[:, :, None]
    kd = jnp.exp(log_g[:, None] * (c - 1.0 - idx))[:, :, None]
    cd = jnp.exp(log_g * c)
    return din, qd, kd, cd


def _t5_bucket(dist):
    max_exact = REL_BUCKETS // 2
    d_f = jnp.maximum(dist, 1).astype(F32)
    large = max_exact + (jnp.log(d_f / max_exact) / math.log(REL_MAX_DIST / max_exact)
                         * (REL_BUCKETS - max_exact)).astype(I32)
    large = jnp.minimum(large, REL_BUCKETS - 1)
    return jnp.where(dist < max_exact, dist, large)


def _bucket_tables():
    qi = jnp.arange(ATT_BLK)[:, None]
    kj = jnp.arange(2 * ATT_BLK)[None, :]
    dist = jnp.clip(ATT_BLK + qi - kj, 0, ATT_BLK)
    return jnp.stack([_t5_bucket(dist * dil) for _, dil in ATT_GROUPS]).astype(I32)


def _permute_rows(t, dil):
    if dil == 1:
        return t
    s, w = t.shape
    return t.reshape(s // dil, dil, w).transpose(1, 0, 2).reshape(s, w)


def _unpermute_rows(t, dil):
    if dil == 1:
        return t
    s, w = t.shape
    return t.reshape(dil, s // dil, w).transpose(1, 0, 2).reshape(s, w)


def _retention_fwd(rqk, rv, din, qd, kd, cd):
    nc = SEQ // RET_CHUNK
    c, dk, dv = RET_CHUNK, RET_DK, RET_DV

    def kern(q_ref, k_ref, v_ref, din_ref, qd_ref, kd_ref, cd_ref, o_ref, st_ref, state):
        h, n = pl.program_id(0), pl.program_id(1)

        @pl.when(n == 0)
        def _():
            state[...] = jnp.zeros_like(state)

        q, k, v = q_ref[...], k_ref[...], v_ref[...]
        s_b = state[...].astype(BF16)
        st_ref[...] = s_b
        a = lax.dot_general(q, k, (((1,), (1,)), ((), ())), preferred_element_type=F32) * din_ref[...]
        o = jnp.dot(a.astype(BF16), v, preferred_element_type=F32)
        o += jnp.dot(q, s_b, preferred_element_type=F32) * qd_ref[...]
        o_ref[...] = o
        kk = (k.astype(F32) * kd_ref[...]).astype(BF16)
        upd = lax.dot_general(kk, v, (((0,), (0,)), ((), ())), preferred_element_type=F32)
        state[...] = state[...] * cd_ref[h] + upd

    return pl.pallas_call(
        kern, name="retention_fwd", grid=(RET_HEADS, nc),
        in_specs=[
            pl.BlockSpec((c, dk), lambda h, n: (n, h)),
            pl.BlockSpec((c, dk), lambda h, n: (n, RET_HEADS + h)),
            pl.BlockSpec((c, dv), lambda h, n: (n, h)),
            pl.BlockSpec((None, c, c), lambda h, n: (h, 0, 0)),
            pl.BlockSpec((None, c, 1), lambda h, n: (h, 0, 0)),
            pl.BlockSpec((None, c, 1), lambda h, n: (h, 0, 0)),
            pl.BlockSpec(memory_space=pltpu.SMEM),
        ],
        out_specs=[
            pl.BlockSpec((c, dv), lambda h, n: (n, h)),
            pl.BlockSpec((None, None, dk, dv), lambda h, n: (h, n, 0, 0)),
        ],
        out_shape=[
            jax.ShapeDtypeStruct((SEQ, RET_V_W), F32),
            jax.ShapeDtypeStruct((RET_HEADS, nc, dk, dv), BF16),
        ],
        scratch_shapes=[pltpu.VMEM((dk, dv), F32)],
        compiler_params=_cparams(("arbitrary", "arbitrary")),
    )(rqk, rqk, rv, din, qd, kd, cd)


def _retention_bwd(rqk, rv, states, d_ro, din, qd, kd, cd, cos, sin):
    nc = SEQ // RET_CHUNK
    c, dk, dv = RET_CHUNK, RET_DK, RET_DV
    half = dk // 2
    last = nc - 1

    def unrot(g, cs, sn):
        g1, g2 = g[:, :half], g[:, half:]
        return jnp.concatenate([g1 * cs + g2 * sn, g2 * cs - g1 * sn], axis=-1)

    def kern(q_ref, k_ref, v_ref, st_ref, do_ref, din_ref, qd_ref, kd_ref, cd_ref, cos_ref, sin_ref,
             dq_ref, dk_ref, dv_ref, dstate):
        h, step = pl.program_id(0), pl.program_id(1)

        @pl.when(step == 0)
        def _():
            dstate[...] = jnp.zeros_like(dstate)

        q, k, v, s_b = q_ref[...], k_ref[...], v_ref[...], st_ref[...]
        d_o = do_ref[...]
        d_ob = d_o.astype(BF16)
        d_oq = (d_o * qd_ref[...]).astype(BF16)
        ds_b = dstate[...].astype(BF16)
        din_m = din_ref[...]
        nt = (((1,), (1,)), ((), ()))
        tn = (((0,), (0,)), ((), ()))
        a_b = (lax.dot_general(q, k, nt, preferred_element_type=F32) * din_m).astype(BF16)
        kk = (k.astype(F32) * kd_ref[...]).astype(BF16)
        d_v = lax.dot_general(a_b, d_ob, tn, preferred_element_type=F32)
        d_v += jnp.dot(kk, ds_b, preferred_element_type=F32)
        d_a = (lax.dot_general(d_ob, v, nt, preferred_element_type=F32) * din_m).astype(BF16)
        d_q = jnp.dot(d_a, k, preferred_element_type=F32)
        d_q += lax.dot_general(d_oq, s_b, nt, preferred_element_type=F32)
        d_k = lax.dot_general(d_a, q, tn, preferred_element_type=F32)
        d_k += lax.dot_general(v, ds_b, nt, preferred_element_type=F32) * kd_ref[...]
        dstate[...] = dstate[...] * cd_ref[h] + lax.dot_general(q, d_oq, tn, preferred_element_type=F32)
        cs, sn = cos_ref[...], sin_ref[...]
        dq_ref[...] = unrot(d_q, cs, sn).astype(BF16)
        dk_ref[...] = (unrot(d_k, cs, sn) * (RET_DK ** -0.5)).astype(BF16)
        dv_ref[...] = d_v.astype(BF16)

    return pl.pallas_call(
        kern, name="retention_bwd", grid=(RET_HEADS, nc),
        in_specs=[
            pl.BlockSpec((c, dk), lambda h, n: (last - n, h)),
            pl.BlockSpec((c, dk), lambda h, n: (last - n, RET_HEADS + h)),
            pl.BlockSpec((c, dv), lambda h, n: (last - n, h)),
            pl.BlockSpec((None, None, dk, dv), lambda h, n: (h, last - n, 0, 0)),
            pl.BlockSpec((c, dv), lambda h, n: (last - n, h)),
            pl.BlockSpec((None, c, c), lambda h, n: (h, 0, 0)),
            pl.BlockSpec((None, c, 1), lambda h, n: (h, 0, 0)),
            pl.BlockSpec((None, c, 1), lambda h, n: (h, 0, 0)),
            pl.BlockSpec(memory_space=pltpu.SMEM),
            pl.BlockSpec((c, half), lambda h, n: (last - n, 0)),
            pl.BlockSpec((c, half), lambda h, n: (last - n, 0)),
        ],
        out_specs=[
            pl.BlockSpec((c, dk), lambda h, n: (last - n, h)),
            pl.BlockSpec((c, dk), lambda h, n: (last - n, h)),
            pl.BlockSpec((c, dv), lambda h, n: (last - n, h)),
        ],
        out_shape=[
            jax.ShapeDtypeStruct((SEQ, RET_QK_W), BF16),
            jax.ShapeDtypeStruct((SEQ, RET_QK_W), BF16),
            jax.ShapeDtypeStruct((SEQ, RET_V_W), BF16),
        ],
        scratch_shapes=[pltpu.VMEM((dk, dv), F32)],
        compiler_params=_cparams(("arbitrary", "arbitrary")),
    )(rqk, rqk, rv, states, d_ro, din, qd, kd, cd, cos, sin)


def _bias_build(rel_bias, buckets):
    ng = len(ATT_GROUPS)

    def kern(tab_ref, bkt_ref, o_ref):
        g, h = pl.program_id(0), pl.program_id(1)
        bkt = bkt_ref[...]
        acc = jnp.zeros(bkt.shape, F32)
        for b in range(REL_BUCKETS):
            acc = jnp.where(bkt == b, tab_ref[b, g * ATT_HPG + h], acc)
        o_ref[...] = acc

    return pl.pallas_call(
        kern, name="bias_build", grid=(ng, ATT_HPG),
        in_specs=[pl.BlockSpec(memory_space=pltpu.SMEM),
                  pl.BlockSpec((None, ATT_BLK, 2 * ATT_BLK), lambda g, h: (g, 0, 0))],
        out_specs=pl.BlockSpec((None, None, ATT_BLK, 2 * ATT_BLK), lambda g, h: (g, h, 0, 0)),
        out_shape=jax.ShapeDtypeStruct((ng, ATT_HPG, ATT_BLK, 2 * ATT_BLK), F32),
        compiler_params=_cparams(("arbitrary", "arbitrary")),
    )(rel_bias, buckets)


def _bias_grad(dsb, buckets):
    ng = len(ATT_GROUPS)

    def kern(ds_ref, bkt_ref, o_ref):
        g, h = pl.program_id(0), pl.program_id(1)
        bkt, ds = bkt_ref[...], ds_ref[...]
        for b in range(REL_BUCKETS):
            o_ref[b, g * ATT_HPG + h] = jnp.sum(jnp.where(bkt == b, ds, 0.0))

    return pl.pallas_call(
        kern, name="bias_grad", grid=(ng, ATT_HPG),
        in_specs=[pl.BlockSpec((None, None, ATT_BLK, 2 * ATT_BLK), lambda g, h: (g, h, 0, 0)),
                  pl.BlockSpec((None, ATT_BLK, 2 * ATT_BLK), lambda g, h: (g, 0, 0))],
        out_specs=pl.BlockSpec(memory_space=pltpu.SMEM),
        out_shape=jax.ShapeDtypeStruct((REL_BUCKETS, N_ATT_HEADS), F32),
        compiler_params=_cparams(("arbitrary", "arbitrary")),
    )(dsb, buckets)


_NT = (((1,), (1,)), ((), ()))
_TN = (((0,), (0,)), ((), ()))
_ATT_SCALE = ATT_DH ** -0.5


def _band_masks():
    qi = lax.broadcasted_iota(I32, (ATT_BLK, ATT_BLK), 0)
    kj = lax.broadcasted_iota(I32, (ATT_BLK, ATT_BLK), 1)
    return kj >= qi, qi >= kj


def _att_fwd(gi, qkv, bias, nb):
    blk, dh = ATT_BLK, ATT_DH

    def kern(q_ref, kc_ref, kp_ref, vc_ref, vp_ref, b_ref, o_ref, l_ref):
        b = pl.program_id(1)
        q = q_ref[...]
        mp, mc = _band_masks()
        mp = jnp.logical_and(mp, (b % nb) > 0)
        bias_m = b_ref[...]
        s_p = lax.dot_general(q, kp_ref[...], _NT, preferred_element_type=F32) * _ATT_SCALE + bias_m[:, :blk]
        s_c = lax.dot_general(q, kc_ref[...], _NT, preferred_element_type=F32) * _ATT_SCALE + bias_m[:, blk:]
        s_p = jnp.where(mp, s_p, -1e30)
        s_c = jnp.where(mc, s_c, -1e30)
        mx = jnp.maximum(jnp.max(s_p, axis=-1, keepdims=True), jnp.max(s_c, axis=-1, keepdims=True))
        e_p, e_c = jnp.exp(s_p - mx), jnp.exp(s_c - mx)
        den = jnp.sum(e_p, axis=-1, keepdims=True) + jnp.sum(e_c, axis=-1, keepdims=True)
        o = jnp.dot((e_p / den).astype(BF16), vp_ref[...], preferred_element_type=F32)
        o += jnp.dot((e_c / den).astype(BF16), vc_ref[...], preferred_element_type=F32)
        o_ref[...] = o
        l_ref[...] = jnp.broadcast_to(mx + jnp.log(den), (blk, dh))

    prev = lambda b: jnp.maximum(b - 1, 0)
    return pl.pallas_call(
        kern, name=f"att_fwd_g{gi}", grid=(ATT_HPG, N_BLK),
        in_specs=[
            pl.BlockSpec((blk, dh), lambda h, b: (b, h)),
            pl.BlockSpec((blk, dh), lambda h, b: (b, ATT_HPG + h)),
            pl.BlockSpec((blk, dh), lambda h, b: (prev(b), ATT_HPG + h)),
            pl.BlockSpec((blk, dh), lambda h, b: (b, 2 * ATT_HPG + h)),
            pl.BlockSpec((blk, dh), lambda h, b: (prev(b), 2 * ATT_HPG + h)),
            pl.BlockSpec((None, None, blk, 2 * blk), lambda h, b: (gi, h, 0, 0)),
        ],
        out_specs=[pl.BlockSpec((blk, dh), lambda h, b: (b, h)),
                   pl.BlockSpec((blk, dh), lambda h, b: (b, h))],
        out_shape=[jax.ShapeDtypeStruct((SEQ, ATT_W), F32), jax.ShapeDtypeStruct((SEQ, ATT_W), F32)],
        compiler_params=_cparams(("arbitrary", "arbitrary")),
    )(qkv, qkv, qkv, qkv, qkv, bias)


def _att_bwd_dq(gi, qkv, d_att, lse, dd, bias, nb):
    blk, dh = ATT_BLK, ATT_DH

    def kern(q_ref, kc_ref, kp_ref, vc_ref, vp_ref, do_ref, l_ref, d_ref, b_ref, dq_ref, dsb_ref):
        b = pl.program_id(1)

        @pl.when(b == 0)
        def _():
            dsb_ref[...] = jnp.zeros_like(dsb_ref)

        q, d_o = q_ref[...], do_ref[...]
        kc, kp = kc_ref[...], kp_ref[...]
        mp, mc = _band_masks()
        mp = jnp.logical_and(mp, (b % nb) > 0)
        bias_m = b_ref[...]
        lrow, drow = l_ref[...][:, :1], d_ref[...][:, :1]
        s_p = lax.dot_general(q, kp, _NT, preferred_element_type=F32) * _ATT_SCALE + bias_m[:, :blk]
        s_c = lax.dot_general(q, kc, _NT, preferred_element_type=F32) * _ATT_SCALE + bias_m[:, blk:]
        p_p = jnp.where(mp, jnp.exp(jnp.where(mp, s_p, -1e30) - lrow), 0.0)
        p_c = jnp.where(mc, jnp.exp(jnp.where(mc, s_c, -1e30) - lrow), 0.0)
        dp_p = lax.dot_general(d_o, vp_ref[...], _NT, preferred_element_type=F32)
        dp_c = lax.dot_general(d_o, vc_ref[...], _NT, preferred_element_type=F32)
        ds_p = p_p * (dp_p - drow)
        ds_c = p_c * (dp_c - drow)
        dq = jnp.dot(ds_p.astype(BF16), kp, preferred_element_type=F32)
        dq += jnp.dot(ds_c.astype(BF16), kc, preferred_element_type=F32)
        dq_ref[...] = (dq * _ATT_SCALE).astype(BF16)
        dsb_ref[:, :blk] += ds_p
        dsb_ref[:, blk:] += ds_c

    prev = lambda b: jnp.maximum(b - 1, 0)
    return pl.pallas_call(
        kern, name=f"att_bwd_dq_g{gi}", grid=(ATT_HPG, N_BLK),
        in_specs=[
            pl.BlockSpec((blk, dh), lambda h, b: (b, h)),
            pl.BlockSpec((blk, dh), lambda h, b: (b, ATT_HPG + h)),
            pl.BlockSpec((blk, dh), lambda h, b: (prev(b), ATT_HPG + h)),
            pl.BlockSpec((blk, dh), lambda h, b: (b, 2 * ATT_HPG + h)),
            pl.BlockSpec((blk, dh), lambda h, b: (prev(b), 2 * ATT_HPG + h)),
            pl.BlockSpec((blk, dh), lambda h, b: (b, h)),
            pl.BlockSpec((blk, dh), lambda h, b: (b, h)),
            pl.BlockSpec((blk, dh), lambda h, b: (b, h)),
            pl.BlockSpec((None, None, blk, 2 * blk), lambda h, b: (gi, h, 0, 0)),
        ],
        out_specs=[pl.BlockSpec((blk, dh), lambda h, b: (b, h)),
                   pl.BlockSpec((None, blk, 2 * blk), lambda h, b: (h, 0, 0))],
        out_shape=[jax.ShapeDtypeStruct((SEQ, ATT_W), BF16),
                   jax.ShapeDtypeStruct((ATT_HPG, blk, 2 * blk), F32)],
        compiler_params=_cparams(("arbitrary", "arbitrary")),
    )(qkv, qkv, qkv, qkv, qkv, d_att, lse, dd, bias)


def _att_bwd_dkv(gi, qkv, d_att, lse, dd, bias, nb):
    blk, dh = ATT_BLK, ATT_DH

    def kern(k_ref, v_ref, qc_ref, qn_ref, doc_ref, don_ref, lc_ref, ln_ref, dc_ref, dn_ref, b_ref,
             dk_ref, dv_ref):
        b = pl.program_id(1)
        k, v = k_ref[...], v_ref[...]
        mp, mc = _band_masks()
        has_next = jnp.logical_and(b + 1 < N_BLK, ((b + 1) % nb) > 0)
        mp = jnp.logical_and(mp, has_next)
        bias_m = b_ref[...]

        def part(q, d_o, lrow, drow, mask, bias_part):
            s = lax.dot_general(q, k, _NT, preferred_element_type=F32) * _ATT_SCALE + bias_part
            p = jnp.where(mask, jnp.exp(jnp.where(mask, s, -1e30) - lrow), 0.0)
            dp = lax.dot_general(d_o, v, _NT, preferred_element_type=F32)
            ds = p * (dp - drow)
            d_v = lax.dot_general(p.astype(BF16), d_o, _TN, preferred_element_type=F32)
            d_k = lax.dot_general(ds.astype(BF16), q, _TN, preferred_element_type=F32)
            return d_k, d_v

        dk_c, dv_c = part(qc_ref[...], doc_ref[...], lc_ref[...][:, :1], dc_ref[...][:, :1], mc,
                          bias_m[:, blk:])
        dk_n, dv_n = part(qn_ref[...], don_ref[...], ln_ref[...][:, :1], dn_ref[...][:, :1], mp,
                          bias_m[:, :blk])
        dk_ref[...] = ((dk_c + dk_n) * _ATT_SCALE).astype(BF16)
        dv_ref[...] = (dv_c + dv_n).astype(BF16)

    nxt = lambda b: jnp.minimum(b + 1, N_BLK - 1)
    return pl.pallas_call(
        kern, name=f"att_bwd_dkv_g{gi}", grid=(ATT_HPG, N_BLK),
        in_specs=[
            pl.BlockSpec((blk, dh), lambda h, b: (b, ATT_HPG + h)),
            pl.BlockSpec((blk, dh), lambda h, b: (b, 2 * ATT_HPG + h)),
            pl.BlockSpec((blk, dh), lambda h, b: (b, h)),
            pl.BlockSpec((blk, dh), lambda h, b: (nxt(b), h)),
            pl.BlockSpec((blk, dh), lambda h, b: (b, h)),
            pl.BlockSpec((blk, dh), lambda h, b: (nxt(b), h)),
            pl.BlockSpec((blk, dh), lambda h, b: (b, h)),
            pl.BlockSpec((blk, dh), lambda h, b: (nxt(b), h)),
            pl.BlockSpec((blk, dh), lambda h, b: (b, h)),
            pl.BlockSpec((blk, dh), lambda h, b: (nxt(b), h)),
            pl.BlockSpec((None, None, blk, 2 * blk), lambda h, b: (gi, h, 0, 0)),
        ],
        out_specs=[pl.BlockSpec((blk, dh), lambda h, b: (b, h)),
                   pl.BlockSpec((blk, dh), lambda h, b: (b, h))],
        out_shape=[jax.ShapeDtypeStruct((SEQ, ATT_W), BF16), jax.ShapeDtypeStruct((SEQ, ATT_W), BF16)],
        compiler_params=_cparams(("arbitrary", "arbitrary")),
    )(qkv, qkv, qkv, qkv, d_att, d_att, lse, lse, dd, dd, bias)


def _rms_parts(x):
    r = lax.rsqrt(jnp.mean(x * x, axis=-1, keepdims=True) + RMS_EPS)
    return x * r, r


def _rms_bwd(d_xhat, xhat, r):
    return r * (d_xhat - xhat * jnp.mean(d_xhat * xhat, axis=-1, keepdims=True))


def _prenorm_fwd(name, x, gain, shift, scale):
    def body(xt, g, sh, sc):
        xhat, _ = _rms_parts(xt)
        return (xhat * g) * (1.0 + sc) + sh
    return _rowmap(name, body, [x], [gain, shift, scale], [(D_MODEL, BF16)])[0]


def _prenorm_bwd(name, d_hs, x, gain, scale, resid):
    n_dh = len(d_hs)

    def body(*args):
        d_h = args[0]
        for t in args[1:n_dh]:
            d_h = d_h + t
        xt, res, g, sc = args[n_dh:]
        xhat, r = _rms_parts(xt)
        nrm = xhat * g
        d_n = d_h * (1.0 + sc)
        dx = _rms_bwd(d_n * g, xhat, r) + res
        return (dx, jnp.sum(d_h, axis=0, keepdims=True), jnp.sum(d_h * nrm, axis=0, keepdims=True),
                jnp.sum(d_n * xhat, axis=0, keepdims=True))

    return _rowmap(name, body, list(d_hs) + [x, resid], [gain, scale], [(D_MODEL, F32)],
                   [D_MODEL, D_MODEL, D_MODEL])


def _gn_parts(ro):
    mu = jnp.mean(ro, axis=-1, keepdims=True)
    cen = ro - mu
    rstd = lax.rsqrt(jnp.mean(cen * cen, axis=-1, keepdims=True) + GN_EPS)
    return cen * rstd, rstd


def _retpost_fwd(ro, rg, gn_g, gn_b):
    def body(rot, rgt, g, b):
        outs = []
        for h in range(RET_HEADS):
            sl = slice(h * RET_DV, (h + 1) * RET_DV)
            nrm, _ = _gn_parts(rot[:, sl])
            gate = rgt[:, sl]
            outs.append((gate * _sigmoid(gate)) * (nrm * g[:, sl] + b[:, sl]))
        return jnp.concatenate(outs, axis=-1)
    return _rowmap("retpost_fwd", body, [ro, rg], [gn_g, gn_b], [(RET_V_W, BF16)])[0]


def _retpost_bwd(d_gated, ro, rg, gn_g, gn_b):
    def body(dgt, rot, rgt, g, b):
        d_ro, d_rg, d_g, d_b = [], [], [], []
        for h in range(RET_HEADS):
            sl = slice(h * RET_DV, (h + 1) * RET_DV)
            nrm, rstd = _gn_parts(rot[:, sl])
            gate, dg = rgt[:, sl], dgt[:, sl]
            sg = _sigmoid(gate)
            ron = nrm * g[:, sl] + b[:, sl]
            d_rg.append(dg * ron * (sg * (1.0 + gate * (1.0 - sg))))
            d_ron = dg * (gate * sg)
            d_g.append(jnp.sum(d_ron * nrm, axis=0, keepdims=True))
            d_b.append(jnp.sum(d_ron, axis=0, keepdims=True))
            d_n = d_ron * g[:, sl]
            d_ro.append(rstd * (d_n - jnp.mean(d_n, axis=-1, keepdims=True)
                                - nrm * jnp.mean(d_n * nrm, axis=-1, keepdims=True)))
        cat = lambda ts: jnp.concatenate(ts, axis=-1)
        return cat(d_ro), cat(d_rg), cat(d_g), cat(d_b)
    return _rowmap("retpost_bwd", body, [d_gated, ro, rg], [gn_g, gn_b],
                   [(RET_V_W, F32), (RET_V_W, BF16)], [RET_V_W, RET_V_W])


def _combine(os_, ls_):
    def body(o0, o1, o2, l0, l1, l2):
        mx = jnp.maximum(jnp.maximum(l0, l1), l2)
        e0, e1, e2 = jnp.exp(l0 - mx), jnp.exp(l1 - mx), jnp.exp(l2 - mx)
        den = e0 + e1 + e2
        att = (e0 / den) * o0 + (e1 / den) * o1 + (e2 / den) * o2
        return att, att, mx + jnp.log(den)
    return _rowmap("att_combine", body, list(os_) + list(ls_), [],
                   [(ATT_W, F32), (ATT_W, BF16), (ATT_W, F32)])


def _att_bwd_pre(d_att, att):
    def body(dt, at):
        outs = []
        for h in range(ATT_HPG):
            sl = slice(h * ATT_DH, (h + 1) * ATT_DH)
            outs.append(jnp.broadcast_to(jnp.sum(dt[:, sl] * at[:, sl], axis=-1, keepdims=True),
                                         (dt.shape[0], ATT_DH)))
        return dt, jnp.concatenate(outs, axis=-1)
    return _rowmap("att_bwd_pre", body, [d_att, att], [], [(ATT_W, BF16), (ATT_W, F32)])


def _merge_fwd(gates, ret_out, att_out):
    def body(gt, ro, ao):
        return _sigmoid(gt[:, :D_MODEL]) * ro + _sigmoid(gt[:, D_MODEL:]) * ao
    return _rowmap("merge_fwd", body, [gates, ret_out, att_out], [], [(D_MODEL, BF16)])[0]


def _merge_bwd(d_merged, gates, ret_out, att_out):
    def body(dm, gt, ro, ao):
        sa, sb = _sigmoid(gt[:, :D_MODEL]), _sigmoid(gt[:, D_MODEL:])
        d_gates = jnp.concatenate([dm * ro * (sa * (1.0 - sa)), dm * ao * (sb * (1.0 - sb))], axis=-1)
        return dm * sa, dm * sb, d_gates
    return _rowmap("merge_bwd", body, [d_merged, gates, ret_out, att_out], [],
                   [(D_MODEL, BF16), (D_MODEL, BF16), (2 * D_MODEL, BF16)])


def _gate_bwd(name, d_x, branch, gate):
    def body(dx, br, g):
        return dx * g, jnp.sum(dx * br, axis=0, keepdims=True)
    return _rowmap(name, body, [d_x, branch], [gate], [(D_MODEL, BF16)], [D_MODEL])


def _loss_head(x3, target, gain):
    def body(xt, tt, g):
        xhat, r = _rms_parts(xt)
        err = xhat * g - tt
        d_y = err / D_MODEL
        loss = 0.5 * jnp.sum(jnp.mean(err * err, axis=-1, keepdims=True), axis=0, keepdims=True)
        d_x = _rms_bwd(d_y * g, xhat, r)
        return d_x, jnp.broadcast_to(loss, (1, 128)), jnp.sum(d_y * xhat, axis=0, keepdims=True)
    return _rowmap("loss_head", body, [x3, target], [gain], [(D_MODEL, F32)], [128, D_MODEL])


def _local_step(pos, x, target, mod, norm1_g, norm2_g, norm_f_g, rel_bias, gn_g, gn_b,
                w_in, w_ret_out, w_att_out, w_o, w_ff1, w_ff2):
    sh1, sc1, g1, sh2, sc2, g2 = [mod[:, i * D_MODEL:(i + 1) * D_MODEL] for i in range(6)]
    cos, sin = _rope_tables()
    din, qd, kd, cd = _decay_tables()
    buckets = _bucket_tables()
    bias = _bias_build(rel_bias, buckets)
    dils = [d for _, d in ATT_GROUPS]
    nbs = [SEQ // d // ATT_BLK for d in dils]

    h1 = _prenorm_fwd("prenorm1_fwd", x, norm1_g, sh1, sc1)
    h1_p = [_permute_rows(h1, d) for d in dils]

    def rot_epi(acc, cs, sn, scale):
        half = RET_DK // 2
        x1, x2 = acc[:, :half], acc[:, half:]
        return (jnp.concatenate([x1 * cs - x2 * sn, x1 * sn + x2 * cs], axis=-1) * scale,)

    qk_scale = jnp.concatenate([jnp.ones((1, RET_QK_W), F32),
                                jnp.full((1, RET_QK_W), RET_DK ** -0.5, F32)], axis=-1)
    rope_ex = [(cos, (TM, RET_DK // 2), lambda i, j, kk: (i, 0)),
               (sin, (TM, RET_DK // 2), lambda i, j, kk: (i, 0)),
               (qk_scale, (1, RET_DK), lambda i, j, kk: (0, j))]
    rqk = _matmul("proj_qk", h1, w_in, "nn", SEQ, 2 * RET_QK_W, D_MODEL, [BF16], b_off=OFF_Q,
                  tn=RET_DK, tk=D_MODEL, epilogue=rot_epi, extras=rope_ex)[0]
    rv = _matmul("proj_rv", h1, w_in, "nn", SEQ, RET_V_W, D_MODEL, [BF16], b_off=OFF_V, tk=D_MODEL)[0]
    rg = _matmul("proj_rg", h1, w_in, "nn", SEQ, RET_V_W, D_MODEL, [F32], b_off=OFF_G, tk=D_MODEL)[0]
    gates = _matmul("proj_gates", h1, w_in, "nn", SEQ, 2 * D_MODEL, D_MODEL, [F32], b_off=OFF_GATE,
                    tk=D_MODEL)[0]
    aqkv = [_matmul(f"proj_att_g{gi}", h1_p[gi], w_in, "nn", SEQ, 3 * ATT_W, D_MODEL, [BF16],
                    b_off=OFF_ATT + gi * 3 * ATT_W, tk=D_MODEL)[0] for gi in range(3)]

    ro, states = _retention_fwd(rqk, rv, din, qd, kd, cd)
    gated = _retpost_fwd(ro, rg, gn_g, gn_b)
    ret_out = _matmul("ret_out", gated, w_ret_out, "nn", SEQ, D_MODEL, RET_V_W, [F32])[0]

    os_, ls_ = [], []
    for gi in range(3):
        o_g, l_g = _att_fwd(gi, aqkv[gi], bias, nbs[gi])
        os_.append(_unpermute_rows(o_g, dils[gi]))
        ls_.append(_unpermute_rows(l_g, dils[gi]))
    att, att_b, lse = _combine(os_, ls_)
    att_out = _matmul("att_out", att_b, w_att_out, "nn", SEQ, D_MODEL, ATT_W, [F32])[0]

    merged = _merge_fwd(gates, ret_out, att_out)

    def resid_epi(acc, xt, g):
        return xt + g * acc, acc

    def resid_ex(xin, g):
        return [(xin, (TM, TN), lambda i, j, kk: (i, j)), (g, (1, TN), lambda i, j, kk: (0, j))]

    x2, mix = _matmul("mix_out", merged, w_o, "nn", SEQ, D_MODEL, D_MODEL, [F32, F32],
                      epilogue=resid_epi, extras=resid_ex(x, g1))
    h2 = _prenorm_fwd("prenorm2_fwd", x2, norm2_g, sh2, sc2)

    def relu2_epi(acc):
        r = jnp.maximum(acc, 0.0)
        return r * r, acc

    act, u = _matmul("ff1", h2, w_ff1, "nn", SEQ, D_FF, D_MODEL, [BF16, F32], tk=D_MODEL,
                     epilogue=relu2_epi)
    x3, y2 = _matmul("ff2", act, w_ff2, "nn", SEQ, D_MODEL, D_FF, [F32, F32],
                     epilogue=resid_epi, extras=resid_ex(x2, g2))

    d_x3, loss, d_gf = _loss_head(x3, target, norm_f_g)

    d_y2, d_g2 = _gate_bwd("ff_gate_bwd", d_x3, y2, g2)

    def relu2_bwd_epi(acc, ut):
        return (acc * (2.0 * jnp.maximum(ut, 0.0)),)

    d_u = _matmul("ff2_dx", d_y2, w_ff2, "nt", SEQ, D_FF, D_MODEL, [BF16], epilogue=relu2_bwd_epi,
                  extras=[(u, (TM, TN), lambda i, j, kk: (i, j))])[0]
    gw_ff2 = _matmul_tn_pair("ff2_dw", pos, act, d_y2, D_FF, D_MODEL, SEQ, D_FF // N_CHIPS,
                             tm=512, tn=1024, tk=1024)
    d_h2 = _matmul("ff1_dx", d_u, w_ff1, "nt", SEQ, D_MODEL, D_FF, [F32])[0]
    gw_ff1 = _matmul_tn_pair("ff1_dw", pos, h2, d_u, D_MODEL, D_FF, SEQ, D_MODEL,
                             tm=512, tn=1024, tk=1024)
    d_x2, d_sh2, d_sc2, d_n2g = _prenorm_bwd("prenorm2_bwd", [d_h2], x2, norm2_g, sc2, d_x3)

    d_mix, d_g1 = _gate_bwd("mix_gate_bwd", d_x2, mix, g1)
    d_merged = _matmul("mix_dx", d_mix, w_o, "nt", SEQ, D_MODEL, D_MODEL, [F32])[0]
    gw_o = _matmul_tn_pair("mix_dw", pos, merged, d_mix, D_MODEL, D_MODEL, SEQ, D_MODEL // N_CHIPS,
                           tm=128, tn=1024, tk=2048)
    d_ret_out, d_att_out, d_gates = _merge_bwd(d_merged, gates, ret_out, att_out)

    d_gated = _matmul("ret_out_dx", d_ret_out, w_ret_out, "nt", SEQ, RET_V_W, D_MODEL, [F32])[0]
    gw_ret_out = _matmul_tn_pair("ret_out_dw", pos, gated, d_ret_out, RET_V_W, D_MODEL, SEQ,
                                 RET_V_W // N_CHIPS, tm=256, tn=1024, tk=1024)
    d_att = _matmul("att_out_dx", d_att_out, w_att_out, "nt", SEQ, ATT_W, D_MODEL, [F32])[0]
    gw_att_out = _matmul_tn_pair("att_out_dw", pos, att_b, d_att_out, ATT_W, D_MODEL, SEQ, ATT_W,
                                 tm=256, tn=1024, tk=2048)

    d_ro, d_rg, d_gn_g, d_gn_b = _retpost_bwd(d_gated, ro, rg, gn_g, gn_b)
    d_rq, d_rk, d_rv = _retention_bwd(rqk, rv, states, d_ro, din, qd, kd, cd, cos, sin)

    d_att_b, dd = _att_bwd_pre(d_att, att)
    d_aqkv, dsbs = [], []
    for gi in range(3):
        da_p = _permute_rows(d_att_b, dils[gi])
        l_p = _permute_rows(lse, dils[gi])
        dd_p = _permute_rows(dd, dils[gi])
        dq, dsb = _att_bwd_dq(gi, aqkv[gi], da_p, l_p, dd_p, bias, nbs[gi])
        dk, dv = _att_bwd_dkv(gi, aqkv[gi], da_p, l_p, dd_p, bias, nbs[gi])
        d_aqkv.append(_unpermute_rows(jnp.concatenate([dq, dk, dv], axis=-1), dils[gi]))
        dsbs.append(dsb)
    d_rel_bias = _bias_grad(jnp.stack(dsbs), buckets)

    d_proj = jnp.concatenate([d_rq, d_rk, d_rv, d_rg] + d_aqkv + [d_gates], axis=-1)
    d_h1 = _matmul("proj_dx", d_proj, w_in, "nt", SEQ, D_MODEL, IN_COLS, [F32], tn=1024, tk=1280)[0]
    gw_in = _matmul_tn_pair("proj_dw", pos, h1, d_proj, D_MODEL, IN_COLS, SEQ, D_MODEL,
                            tm=512, tn=640, tk=2048)

    grad_x, d_sh1, d_sc1, d_n1g = _prenorm_bwd("prenorm1_bwd", [d_h1], x, norm1_g, sc1, d_x2)
    d_mod = jnp.concatenate([d_sh1, d_sc1, d_g1, d_sh2, d_sc2, d_g2], axis=-1)
    small = dict(norm1_g=d_n1g, norm2_g=d_n2g, norm_f_g=d_gf, gn_g=d_gn_g, gn_b=d_gn_b,
                 rel_bias=d_rel_bias)
    big = dict(w_in=gw_in, w_ret_out=gw_ret_out, w_att_out=gw_att_out, w_o=gw_o, w_ff1=gw_ff1,
               w_ff2=gw_ff2)
    return loss, grad_x, d_mod, small, big


def _me():
    return lax.axis_index("x"), lax.axis_index("y"), lax.axis_index("c")


def _peer(x, y, c, mask):
    return (x ^ ((mask >> 2) & 1), y ^ ((mask >> 1) & 1), c ^ (mask & 1))


def _gather8(src_ref, dst_ref, send_sems, recv_sems):
    x, y, c = _me()
    me = 4 * x + 2 * y + c
    copies = []
    for mask in range(1, N_DEV):
        cp = pltpu.make_async_remote_copy(
            src_ref=src_ref, dst_ref=dst_ref.at[me], send_sem=send_sems.at[mask - 1],
            recv_sem=recv_sems.at[mask - 1], device_id=_peer(x, y, c, mask), device_id_type=MESH)
        cp.start()
        copies.append(cp)
    dst_ref[me] = src_ref[...]
    for cp in copies:
        cp.wait_recv()
    for cp in copies:
        cp.wait_send()


def _ada_fwd(c_in, w_ada, b_ada):
    ncol = ADA_COLS // N_CHIPS

    def body(c_ref, w_ref, b_ref, mod_ref, sc_ref, cbuf, cg, mbuf, mg, s1, r1, s2, r2):
        x, y, c = _me()
        me = 4 * x + 2 * y + c
        cv = c_ref[...]
        cbuf[...] = jnp.broadcast_to(cv * _sigmoid(cv), cbuf.shape)
        _gather8(cbuf, cg, s1, r1)
        rows = lax.broadcasted_iota(I32, (N_DEV, D_MODEL), 0)
        sc_all = jnp.zeros((N_DEV, D_MODEL), F32)
        for d in range(N_DEV):
            sc_all = jnp.where(rows == d, cg[d], sc_all)
        sc_ref[...] = sc_all
        mbuf[...] = jnp.dot(sc_all.astype(BF16), w_ref[...].astype(BF16), preferred_element_type=F32)
        _gather8(mbuf, mg, s2, r2)
        rowsel = lax.broadcasted_iota(I32, (N_DEV, ncol), 0) == me
        for k in range(N_CHIPS):
            blk = mg[2 * k]
            row = jnp.sum(jnp.where(rowsel, blk, 0.0), axis=0, keepdims=True)
            mod_ref[:, k * ncol:(k + 1) * ncol] = row + b_ref[:, k * ncol:(k + 1) * ncol]

    vm = pl.BlockSpec(memory_space=pltpu.VMEM)
    return pl.pallas_call(
        body, name="ada_fwd",
        in_specs=[vm, vm, vm], out_specs=[vm, vm],
        out_shape=[jax.ShapeDtypeStruct((1, ADA_COLS), F32), jax.ShapeDtypeStruct((N_DEV, D_MODEL), F32)],
        scratch_shapes=[
            pltpu.VMEM((8, D_MODEL), F32), pltpu.VMEM((N_DEV, 8, D_MODEL), F32),
            pltpu.VMEM((8, ncol), F32), pltpu.VMEM((N_DEV, 8, ncol), F32),
            pltpu.SemaphoreType.DMA((N_DEV - 1,)), pltpu.SemaphoreType.DMA((N_DEV - 1,)),
            pltpu.SemaphoreType.DMA((N_DEV - 1,)), pltpu.SemaphoreType.DMA((N_DEV - 1,)),
        ],
        compiler_params=pltpu.CompilerParams(vmem_limit_bytes=VMEM_LIMIT_V7X),
    )(c_in, w_ada, b_ada)


def _small_reduce(pack, sc_all):
    ncol = ADA_COLS // N_CHIPS

    def body(p_ref, sc_ref, tot_ref, gw_ref, pg, s1, r1):
        x, y, _ = _me()
        chip = 2 * x + y
        _gather8(p_ref, pg, s1, r1)
        tot = pg[0]
        for d in range(1, N_DEV):
            tot = tot + pg[d]
        tot_ref[...] = tot
        rows = lax.broadcasted_iota(I32, (N_DEV, ncol), 0)
        dmod = jnp.zeros((N_DEV, ncol), F32)
        for k in range(N_CHIPS):
            part = jnp.zeros((N_DEV, ncol), F32)
            for d in range(N_DEV):
                part = jnp.where(rows == d, pg[d, :, k * ncol:(k + 1) * ncol][0:1, :], part)
            dmod = jnp.where(chip == k, part, dmod)
        gw_ref[...] = lax.dot_general(sc_ref[...].astype(BF16), dmod.astype(BF16), _TN,
                                      preferred_element_type=F32)

    vm = pl.BlockSpec(memory_space=pltpu.VMEM)
    return pl.pallas_call(
        body, name="small_reduce",
        in_specs=[vm, vm], out_specs=[vm, vm],
        out_shape=[jax.ShapeDtypeStruct((8, ADA_COLS), F32), jax.ShapeDtypeStruct((D_MODEL, ncol), F32)],
        scratch_shapes=[pltpu.VMEM((N_DEV, 8, ADA_COLS), F32),
                        pltpu.SemaphoreType.DMA((N_DEV - 1,)), pltpu.SemaphoreType.DMA((N_DEV - 1,))],
        compiler_params=pltpu.CompilerParams(vmem_limit_bytes=VMEM_LIMIT_V7X),
    )(pack, sc_all)


BIG = (("w_in", 1), ("w_ret_out", 0), ("w_att_out", 1), ("w_o", 0), ("w_ff1", 1), ("w_ff2", 0))
_CHIP_FLIPS = ((1, 0), (0, 1), (1, 1))


def _region(ref, axis, chip, half, shard_shape):
    r, cw = shard_shape
    hr = r // 2
    if axis == 1:
        return ref.at[pl.ds(half * hr, hr), pl.ds(chip * cw, cw)]
    return ref.at[pl.ds(chip * r + half * hr, hr), :]


def _gather_weights(shards):
    nw = len(BIG)
    shapes = [s.shape for s in shards]
    full_shapes = [(r, N_CHIPS * cw) if ax == 1 else (N_CHIPS * r, cw)
                   for (r, cw), (_, ax) in zip(shapes, BIG)]

    def body(*refs):
        ins, outs = refs[:nw], refs[nw:2 * nw]
        loc_sem, s_ici, r_ici, s_d2d, r_d2d = refs[2 * nw:]
        x, y, c = _me()
        chip = 2 * x + y
        local, first, passed = [], [], []
        for i, (_, ax) in enumerate(BIG):
            r, cw = shapes[i]
            own = outs[i].at[:, pl.ds(chip * cw, cw)] if ax == 1 else outs[i].at[pl.ds(chip * r, r), :]
            cp = pltpu.make_async_copy(ins[i], own, loc_sem.at[i])
            cp.start()
            local.append(cp)
        for j, (fx, fy) in enumerate(_CHIP_FLIPS):
            for i, (_, ax) in enumerate(BIG):
                hr = shapes[i][0] // 2
                cp = pltpu.make_async_remote_copy(
                    src_ref=ins[i].at[pl.ds(c * hr, hr), :],
                    dst_ref=_region(outs[i], ax, chip, c, shapes[i]),
                    send_sem=s_ici.at[j * nw + i], recv_sem=r_ici.at[j * nw + i],
                    device_id=(x ^ fx, y ^ fy, c), device_id_type=MESH)
                cp.start()
                first.append(cp)
        for j, (fx, fy) in enumerate(_CHIP_FLIPS):
            src_chip = 2 * (x ^ fx) + (y ^ fy)
            for i, (_, ax) in enumerate(BIG):
                reg = _region(outs[i], ax, src_chip, c, shapes[i])
                first[j * nw + i].wait_recv()
                cp = pltpu.make_async_remote_copy(
                    src_ref=reg, dst_ref=reg, send_sem=s_d2d.at[j * nw + i],
                    recv_sem=r_d2d.at[j * nw + i], device_id=(x, y, 1 - c), device_id_type=MESH)
                cp.start()
                passed.append(cp)
        for cp in passed:
            cp.wait_recv()
        for cp in first + passed:
            cp.wait_send()
        for cp in local:
            cp.wait()

    hbm = pl.BlockSpec(memory_space=pl.ANY)
    return pl.pallas_call(
        body, name="gather_weights",
        in_specs=[hbm] * nw, out_specs=[hbm] * nw,
        out_shape=[jax.ShapeDtypeStruct(fs, BF16) for fs in full_shapes],
        scratch_shapes=[pltpu.SemaphoreType.DMA((nw,)),
                        pltpu.SemaphoreType.DMA((3 * nw,)), pltpu.SemaphoreType.DMA((3 * nw,)),
                        pltpu.SemaphoreType.DMA((3 * nw,)), pltpu.SemaphoreType.DMA((3 * nw,))],
    )(*shards)


def _rs_chips(psums, shard_shapes):
    nw = len(BIG)

    def part(ref, ax, chip, shape):
        hr, cw = shape[0] // 2, shape[1]
        return ref.at[:, pl.ds(chip * cw, cw)] if ax == 1 else ref.at[pl.ds(chip * hr, hr), :]

    def body(*refs):
        ins, outs = refs[:nw], refs[nw:2 * nw]
        s_sem, r_sem = refs[2 * nw:]
        x, y, c = _me()
        sent = []
        for i, (_, ax) in enumerate(BIG):
            for j, (fx, fy) in enumerate(_CHIP_FLIPS):
                dst_chip = 2 * (x ^ fx) + (y ^ fy)
                rc = pltpu.make_async_remote_copy(
                    src_ref=part(ins[i], ax, dst_chip, shard_shapes[i]), dst_ref=outs[i].at[j],
                    send_sem=s_sem.at[i * 3 + j], recv_sem=r_sem.at[i * 3 + j],
                    device_id=(x ^ fx, y ^ fy, c), device_id_type=MESH)
                rc.start()
                sent.append(rc)
        for rc in sent:
            rc.wait_recv()
        for rc in sent:
            rc.wait_send()

    hbm = pl.BlockSpec(memory_space=pl.ANY)
    return pl.pallas_call(
        body, name="rs_chips", in_specs=[hbm] * nw, out_specs=[hbm] * nw,
        out_shape=[jax.ShapeDtypeStruct((3, r // 2, cw), BF16) for r, cw in shard_shapes],
        scratch_shapes=[pltpu.SemaphoreType.DMA((nw * 3,)), pltpu.SemaphoreType.DMA((nw * 3,))],
    )(*psums)


def _adam_update(w, g, m, v):
    mn = ADAM_B1 * m + (1.0 - ADAM_B1) * g
    vn = ADAM_B2 * v + (1.0 - ADAM_B2) * (g * g)
    m_hat = mn / (1.0 - ADAM_B1 ** ADAM_STEP)
    v_hat = vn / (1.0 - ADAM_B2 ** ADAM_STEP)
    return -ADAM_LR * (m_hat / (jnp.sqrt(v_hat) + ADAM_EPS) + ADAM_WD * w), mn, vn


def _final_update(name, pos, axis, psum, recv, w, m, v, tr=64):
    r, cw = w.shape
    hr = r // 2
    tr = min(tr, hr)
    nt = hr // tr

    def kern(pos_ref, p_ref, r_ref, w_ref, m_ref, v_ref, g_ref, d_ref, nm_ref, nv_ref,
             send_buf, land_buf, s_sem, r_sem):
        p, t = pl.program_id(0), pl.program_id(1)
        sib = _sibling()

        def copy(i):
            return pltpu.make_async_remote_copy(
                src_ref=send_buf.at[i], dst_ref=land_buf.at[i], send_sem=s_sem.at[i],
                recv_sem=r_sem.at[i], device_id=sib, device_id_type=MESH)

        def update(tot):
            g_ref[...] = tot
            d_ref[...], nm_ref[...], nv_ref[...] = _adam_update(w_ref[...], tot, m_ref[...], v_ref[...])

        @pl.when(p == 0)
        def _():
            tot = p_ref[...].astype(F32)
            for j in range(3):
                tot = tot + r_ref[j].astype(F32)
            send_buf[t] = tot
            copy(t).start()
            update(tot)

        @pl.when(p == 1)
        def _():
            copy(t).wait_recv()
            update(land_buf[t])

        @pl.when(jnp.logical_and(p == 1, t == nt - 1))
        def _():
            for i in range(nt):
                copy(i).wait_send()

    def shard_rows(p, t, pos_ref):
        return (jnp.where(p == 0, pos_ref[0], 1 - pos_ref[0]) * nt + t, 0)

    def own_part(p, t, pos_ref):
        tt = jnp.where(p == 0, t, nt - 1)
        return (tt, pos_ref[1]) if axis == 1 else (pos_ref[1] * nt + tt, 0)

    shard_spec = pl.BlockSpec((tr, cw), shard_rows)
    grid_spec = pltpu.PrefetchScalarGridSpec(
        num_scalar_prefetch=1, grid=(2, nt),
        in_specs=[pl.BlockSpec((tr, cw), own_part),
                  pl.BlockSpec((3, tr, cw), lambda p, t, pos_ref: (0, jnp.where(p == 0, t, nt - 1), 0)),
                  shard_spec, shard_spec, shard_spec],
        out_specs=[shard_spec] * 4,
        scratch_shapes=[pltpu.VMEM((nt, tr, cw), F32), pltpu.VMEM((nt, tr, cw), F32),
                        pltpu.SemaphoreType.DMA((nt,)), pltpu.SemaphoreType.DMA((nt,))])
    return pl.pallas_call(
        kern, name=name, grid_spec=grid_spec, out_shape=[jax.ShapeDtypeStruct((r, cw), F32)] * 4,
        compiler_params=_cparams(("arbitrary", "arbitrary")),
    )(pos, psum, recv, w, m, v)


def _adamw(name, w, g, m, v):
    r, cw = w.shape
    tr = min(r, 128)

    def kern(w_ref, g_ref, m_ref, v_ref, d_ref, nm_ref, nv_ref):
        d_ref[...], nm_ref[...], nv_ref[...] = _adam_update(w_ref[...], g_ref[...], m_ref[...], v_ref[...])

    spec = pl.BlockSpec((tr, cw), lambda i: (i, 0))
    return pl.pallas_call(
        kern, name=name, grid=(r // tr,), in_specs=[spec] * 4, out_specs=[spec] * 3,
        out_shape=[jax.ShapeDtypeStruct((r, cw), F32)] * 3, compiler_params=_cparams(("parallel",)),
    )(w, g, m, v)


_PACK_W = ADA_COLS
_NB = REL_BUCKETS * N_ATT_HEADS
_SMALL_SLOTS = {
    "b_ada": (0, 0, ADA_COLS),
    "norm1_g": (1, 0, D_MODEL), "norm2_g": (1, D_MODEL, D_MODEL), "norm_f_g": (1, 2 * D_MODEL, D_MODEL),
    "ret_gn_g": (1, 3 * D_MODEL, RET_V_W),
    "ret_gn_b": (2, 0, RET_V_W), "rel_bias": (2, RET_V_W, _NB), "loss": (2, RET_V_W + 512, 128),
}


def _pack_small(vals):
    rows = []
    for r in range(8):
        items = sorted([(off, n) for n, (rr, off, _) in _SMALL_SLOTS.items() if rr == r and n in vals])
        parts, pos = [], 0
        for off, n in items:
            if off > pos:
                parts.append(jnp.zeros((1, off - pos), F32))
            parts.append(vals[n].reshape(1, -1).astype(F32))
            pos = off + _SMALL_SLOTS[n][2]
        if pos < _PACK_W:
            parts.append(jnp.zeros((1, _PACK_W - pos), F32))
        rows.append(jnp.concatenate(parts, axis=-1))
    return jnp.concatenate(rows, axis=0)


def _unpack_small(pack, name):
    r, off, wd = _SMALL_SLOTS[name]
    return pack[r:r + 1, off:off + wd]


def kernel(x, c, w_ada, b_ada, norm1_g, w_in, rel_bias, ret_gn_g, ret_gn_b, w_ret_out, w_att_out, w_o, norm2_g, w_ff1, w_ff2, norm_f_g, loss_target, m_w_ada, m_b_ada, m_norm1_g, m_w_in, m_rel_bias, m_ret_gn_g, m_ret_gn_b, m_w_ret_out, m_w_att_out, m_w_o, m_norm2_g, m_w_ff1, m_w_ff2, m_norm_f_g, v_w_ada, v_b_ada, v_norm1_g, v_w_in, v_rel_bias, v_ret_gn_g, v_ret_gn_b, v_w_ret_out, v_w_att_out, v_w_o, v_norm2_g, v_w_ff1, v_w_ff2, v_norm_f_g):
    given = dict(locals())
    big_names = [n for n, _ in BIG]
    shard_w = {n: given[n][0] for n in big_names}
    shard_shapes = [shard_w[n].shape for n in big_names]

    full = _gather_weights([shard_w[n].astype(BF16) for n in big_names])
    full = dict(zip(big_names, full))
    mod, sc_all = _ada_fwd(c, w_ada[0], b_ada)
    pos = _where_am_i()

    loss, grad_x, d_mod, small, big = _local_step(
        pos, x[0], loss_target[0], mod, norm1_g, norm2_g, norm_f_g.reshape(1, -1), rel_bias, ret_gn_g,
        ret_gn_b, full["w_in"], full["w_ret_out"], full["w_att_out"], full["w_o"], full["w_ff1"],
        full["w_ff2"])

    pack_g = _pack_small(dict(b_ada=d_mod, norm1_g=small["norm1_g"], norm2_g=small["norm2_g"],
                              norm_f_g=small["norm_f_g"], ret_gn_g=small["gn_g"], ret_gn_b=small["gn_b"],
                              rel_bias=small["rel_bias"], loss=loss))
    tot, g_w_ada = _small_reduce(pack_g, sc_all)
    recv = dict(zip(big_names, _rs_chips([big[n] for n in big_names], shard_shapes)))

    small_names = ["b_ada", "norm1_g", "rel_bias", "ret_gn_g", "ret_gn_b", "norm2_g", "norm_f_g"]
    pack_w = _pack_small({n: given[n] for n in small_names})
    pack_m = _pack_small({n: given["m_" + n] for n in small_names})
    pack_v = _pack_small({n: given["v_" + n] for n in small_names})
    sd, sm, sv = _adamw("adamw_small", pack_w, tot, pack_m, pack_v)

    grads, deltas, new_m, new_v = {}, {}, {}, {}
    for n in small_names:
        shp = given[n].shape
        grads[n] = _unpack_small(tot, n).reshape(shp)
        deltas[n] = _unpack_small(sd, n).reshape(shp)
        new_m[n] = _unpack_small(sm, n).reshape(shp)
        new_v[n] = _unpack_small(sv, n).reshape(shp)
    d, nm, nv = _adamw("adamw_w_ada", w_ada[0], g_w_ada, m_w_ada[0], v_w_ada[0])
    grads["w_ada"], deltas["w_ada"], new_m["w_ada"], new_v["w_ada"] = g_w_ada[None], d[None], nm[None], nv[None]
    for n, ax in BIG:
        g, d, nm, nv = _final_update("final_" + n, pos, ax, big[n], recv[n], given[n][0],
                                     given["m_" + n][0], given["v_" + n][0])
        grads[n], deltas[n], new_m[n], new_v[n] = g[None], d[None], nm[None], nv[None]

    order = ["w_ada", "b_ada", "norm1_g", "w_in", "rel_bias", "ret_gn_g", "ret_gn_b", "w_ret_out",
             "w_att_out", "w_o", "norm2_g", "w_ff1", "w_ff2", "norm_f_g"]
    loss_out = _unpack_small(tot, "loss")[0, 0]
    return (loss_out, grad_x[None], *[grads[n] for n in order], *[deltas[n] for n in order],
            *[new_m[n] for n in order], *[new_v[n] for n in order])
```

```python
import functools
import math

import jax
import jax.numpy as jnp
from jax import lax
from jax.experimental import pallas as pl
from jax.experimental.pallas import tpu as pltpu

F32 = jnp.float32
BF16 = jnp.bfloat16
I32 = jnp.int32

SEQ = 2048
D_MODEL = 1024
RET_HEADS = 4
RET_DK = 256
RET_DV = 512
RET_CHUNK = 128
RET_QK_W = RET_HEADS * RET_DK
RET_V_W = RET_HEADS * RET_DV
ATT_GROUPS = ((128, 1), (512, 4), (2048, 16))
ATT_HPG = 4
ATT_DH = 128
ATT_W = ATT_HPG * ATT_DH
ATT_BLK = 128
N_BLK = SEQ // ATT_BLK
REL_BUCKETS = 32
REL_MAX_DIST = 2048
N_ATT_HEADS = 12
D_FF = 4 * D_MODEL
RMS_EPS = 1e-6
GN_EPS = 1e-5
ROPE_BASE = 10000.0
IN_COLS = 2 * RET_QK_W + 2 * RET_V_W + 9 * ATT_W + 2 * D_MODEL
OFF_Q, OFF_K, OFF_V, OFF_G = 0, RET_QK_W, 2 * RET_QK_W, 2 * RET_QK_W + RET_V_W
OFF_ATT = 2 * RET_QK_W + 2 * RET_V_W
OFF_GATE = OFF_ATT + 9 * ATT_W
N_CHIPS = 4
N_DEV = 8
ADA_COLS = 6 * D_MODEL

ADAM_LR = 0.001
ADAM_B1 = 0.9
ADAM_B2 = 0.999
ADAM_EPS = 1e-08
ADAM_WD = 0.01
ADAM_STEP = 10

VMEM_LIMIT_V7X = 56 * 1024 * 1024
MESH = pl.DeviceIdType.MESH


def _cparams(sem):
    return pltpu.CompilerParams(dimension_semantics=sem, vmem_limit_bytes=VMEM_LIMIT_V7X)


def _sigmoid(v):
    return 1.0 / (1.0 + jnp.exp(-v))


def _rowmap(name, body, row_ins, bcast_ins, row_outs, sum_outs=(), tm=256):
    m = row_ins[0].shape[0]
    n_in = len(row_ins) + len(bcast_ins)
    n_ro = len(row_outs)

    def kern(*refs):
        vals = [r[...] for r in refs[:n_in]]
        res = body(*vals)
        if not isinstance(res, (tuple, list)):
            res = (res,)
        outs = refs[n_in:]
        for r, v in zip(outs[:n_ro], res[:n_ro]):
            r[...] = v.astype(r.dtype)
        if sum_outs:
            @pl.when(pl.program_id(0) == 0)
            def _():
                for r in outs[n_ro:]:
                    r[...] = jnp.zeros_like(r)
            for r, v in zip(outs[n_ro:], res[n_ro:]):
                r[...] += v

    in_specs = [pl.BlockSpec((tm, a.shape[1]), lambda i: (i, 0)) for a in row_ins]
    in_specs += [pl.BlockSpec(a.shape, lambda i: (0, 0)) for a in bcast_ins]
    out_specs = [pl.BlockSpec((tm, n), lambda i: (i, 0)) for n, _ in row_outs]
    out_specs += [pl.BlockSpec((1, n), lambda i: (0, 0)) for n in sum_outs]
    out_shape = [jax.ShapeDtypeStruct((m, n), dt) for n, dt in row_outs]
    out_shape += [jax.ShapeDtypeStruct((1, n), F32) for n in sum_outs]
    return pl.pallas_call(
        kern, name=name, grid=(m // tm,), in_specs=in_specs, out_specs=out_specs,
        out_shape=out_shape, compiler_params=_cparams(("arbitrary",)),
    )(*row_ins, *bcast_ins)


TM, TN = 1024, 1024


def _matmul(name, a, b, kind, m, n, k, outs, *, b_off=0, tm=TM, tn=TN, tk=1024,
            epilogue=None, extras=(), carry=None):
    tm, tn, tk = min(tm, m), min(tn, n), min(tk, k)
    nk = k // tk
    if kind == "nn":
        a_spec = pl.BlockSpec((tm, tk), lambda i, j, kk: (i, kk))
        b_spec = pl.BlockSpec((tk, tn), lambda i, j, kk: (kk, b_off // tn + j))
        dn = (((1,), (0,)), ((), ()))
    elif kind == "nt":
        a_spec = pl.BlockSpec((tm, tk), lambda i, j, kk: (i, kk))
        b_spec = pl.BlockSpec((tn, tk), lambda i, j, kk: (j, b_off // tk + kk))
        dn = (((1,), (1,)), ((), ()))
    else:
        a_spec = pl.BlockSpec((tk, tm), lambda i, j, kk: (kk, i))
        b_spec = pl.BlockSpec((tk, tn), lambda i, j, kk: (kk, j))
        dn = (((0,), (0,)), ((), ()))
    n_ex, n_out = len(extras), len(outs)
    if epilogue is None:
        epilogue = lambda acc: (acc,)

    def finish(acc, ex_refs, out_refs):
        res = epilogue(acc, *[r[...] for r in ex_refs])
        for r, v in zip(out_refs, res):
            r[...] = v.astype(r.dtype)

    n_c = 0 if carry is None else 1
    ni, nj = m // tm, n // tn

    def kern(a_ref, b_ref, *rest):
        ex_refs = rest[:n_ex]
        out_refs = rest[n_ex + n_c:n_ex + n_c + n_out]
        scratch = rest[n_ex + 2 * n_c + n_out:]
        i, j, kk = pl.program_id(0), pl.program_id(1), pl.program_id(2)
        if carry is not None:
            copies = lambda: _ici_copies(rest[n_ex], rest[n_ex + n_c + n_out], scratch[-2], scratch[-1],
                                         carry[1], carry[2])

            @pl.when(jnp.logical_and(jnp.logical_and(i == 0, j == 0), kk == 0))
            def _():
                for cp in copies():
                    cp.start()

        part = lax.dot_general(a_ref[...], b_ref[...], dn, preferred_element_type=F32)
        if nk == 1:
            finish(part, ex_refs, out_refs)
        else:
            acc_ref = scratch[0]

            @pl.when(kk == 0)
            def _():
                acc_ref[...] = part

            @pl.when(kk > 0)
            def _():
                acc_ref[...] += part

            @pl.when(kk == nk - 1)
            def _():
                finish(acc_ref[...], ex_refs, out_refs)

        if carry is not None:
            @pl.when(jnp.logical_and(jnp.logical_and(i == ni - 1, j == nj - 1), kk == nk - 1))
            def _():
                for cp in copies():
                    cp.wait_recv()
                for cp in copies():
                    cp.wait_send()

    hbm = pl.BlockSpec(memory_space=pl.ANY)
    in_specs = [a_spec, b_spec] + [pl.BlockSpec(bs, im) for _, bs, im in extras] + [hbm] * n_c
    out_specs = [pl.BlockSpec((tm, tn), lambda i, j, kk: (i, j)) for _ in outs] + [hbm] * n_c
    out_shape = [jax.ShapeDtypeStruct((m, n), dt) for dt in outs]
    scratch_shapes = [] if nk == 1 else [pltpu.VMEM((tm, tn), F32)]
    operands = [a, b] + [e[0] for e in extras]
    if carry is not None:
        r, cw = carry[2]
        out_shape.append(jax.ShapeDtypeStruct((3, r // 2, cw), BF16))
        scratch_shapes += [pltpu.SemaphoreType.DMA((3,)), pltpu.SemaphoreType.DMA((3,))]
        operands.append(carry[0])
    sem = ("arbitrary",) * 3 if carry is not None else ("parallel", "parallel", "arbitrary")
    return pl.pallas_call(
        kern, name=name, grid=(ni, nj, nk), in_specs=in_specs, out_specs=out_specs,
        out_shape=out_shape, scratch_shapes=scratch_shapes, compiler_params=_cparams(sem),
    )(*operands)


def _ici_copies(psum_ref, recv_ref, s_sem, r_sem, axis, shard_shape):
    x, y, c = _me()
    hr, cw = shard_shape[0] // 2, shard_shape[1]
    copies = []
    for j, (fx, fy) in enumerate(_CHIP_FLIPS):
        chip = 2 * (x ^ fx) + (y ^ fy)
        src = psum_ref.at[:, pl.ds(chip * cw, cw)] if axis == 1 else psum_ref.at[pl.ds(chip * hr, hr), :]
        copies.append(pltpu.make_async_remote_copy(
            src_ref=src, dst_ref=recv_ref.at[j], send_sem=s_sem.at[j], recv_sem=r_sem.at[j],
            device_id=(x ^ fx, y ^ fy, c), device_id_type=MESH))
    return copies


def _where_am_i():
    x, y, c = _me()
    return jnp.stack([c, 2 * x + y]).astype(I32)


def _sibling():
    x, y, c = _me()
    return (x, y, 1 - c)


def _matmul_tn_pair(name, pos, a, b, m, n, k, shard_rows, *, tm, tn, tk):
    hr = shard_rows // 2
    tm, tn, tk = min(tm, hr), min(tn, n), min(tk, k)
    tph = hr // tm
    nt, nj, nk = (m // 2) // tm, n // tn, k // tk
    n_tiles = nt * nj

    def row_block(p, t, pos_ref):
        half = jnp.where(p == 0, 1 - pos_ref[0], pos_ref[0])
        return (t // tph) * (2 * tph) + half * tph + t % tph

    def kern(pos_ref, a_ref, b_ref, o_ref, acc_ref, send_buf, land_buf, s_sem, r_sem):
        p, t, j, kk = pl.program_id(0), pl.program_id(1), pl.program_id(2), pl.program_id(3)
        idx = t * nj + j
        sib = _sibling()

        def copy(i):
            return pltpu.make_async_remote_copy(
                src_ref=send_buf.at[i], dst_ref=land_buf.at[i], send_sem=s_sem.at[i],
                recv_sem=r_sem.at[i], device_id=sib, device_id_type=MESH)

        part = lax.dot_general(a_ref[...], b_ref[...], _TN, preferred_element_type=F32)

        @pl.when(kk == 0)
        def _():
            acc_ref[...] = part

        @pl.when(kk > 0)
        def _():
            acc_ref[...] += part

        @pl.when(jnp.logical_and(kk == nk - 1, p == 0))
        def _():
            send_buf[idx] = acc_ref[...].astype(BF16)
            copy(idx).start()

        @pl.when(jnp.logical_and(kk == nk - 1, p == 1))
        def _():
            copy(idx).wait_recv()
            o_ref[...] = (acc_ref[...] + land_buf[idx].astype(F32)).astype(BF16)

        @pl.when(jnp.logical_and(jnp.logical_and(p == 1, idx == n_tiles - 1), kk == nk - 1))
        def _():
            for i in range(n_tiles):
                copy(i).wait_send()

    grid_spec = pltpu.PrefetchScalarGridSpec(
        num_scalar_prefetch=1, grid=(2, nt, nj, nk),
        in_specs=[pl.BlockSpec((tk, tm), lambda p, t, j, kk, pos_ref: (kk, row_block(p, t, pos_ref))),
                  pl.BlockSpec((tk, tn), lambda p, t, j, kk, pos_ref: (kk, j))],
        out_specs=pl.BlockSpec((tm, tn), lambda p, t, j, kk, pos_ref: (p * t, p * j)),
        scratch_shapes=[pltpu.VMEM((tm, tn), F32), pltpu.VMEM((n_tiles, tm, tn), BF16),
                        pltpu.VMEM((n_tiles, tm, tn), BF16),
                        pltpu.SemaphoreType.DMA((n_tiles,)), pltpu.SemaphoreType.DMA((n_tiles,))])
    return pl.pallas_call(
        kern, name=name, grid_spec=grid_spec, out_shape=jax.ShapeDtypeStruct((m // 2, n), BF16),
        compiler_params=_cparams(("arbitrary",) * 4),
    )(pos, a, b)


def _rope_tables():
    half = RET_DK // 2
    inv = ROPE_BASE ** (-jnp.arange(half, dtype=F32) / half)
    ang = jnp.arange(SEQ).astype(F32)[:, None] * inv[None, :]
    return jnp.cos(ang), jnp.sin(ang)


def _decay_tables():
    c = RET_CHUNK
    log_g = jnp.log1p(-(2.0 ** (-5.0 - jnp.arange(RET_HEADS, dtype=F32))))
    idx = jnp.arange(c, dtype=F32)
    rel = idx[:, None] - idx[None, :]
    din = jnp.where(rel >= 0, jnp.exp(log_g[:, None, None] * jnp.maximum(rel, 0.0)), 0.0)
    qd = jnp.exp(log_g[:, None] * (idx + 1.0))[:, :, None]
    kd = jnp.exp(log_g[:, None] * (c - 1.0 - idx))[:, :, None]
    cd = jnp.exp(log_g * c)
    return din, qd, kd, cd


def _t5_bucket(dist):
    max_exact = REL_BUCKETS // 2
    d_f = jnp.maximum(dist, 1).astype(F32)
    large = max_exact + (jnp.log(d_f / max_exact) / math.log(REL_MAX_DIST / max_exact)
                         * (REL_BUCKETS - max_exact)).astype(I32)
    large = jnp.minimum(large, REL_BUCKETS - 1)
    return jnp.where(dist < max_exact, dist, large)


def _bucket_tables():
    qi = jnp.arange(ATT_BLK)[:, None]
    kj = jnp.arange(2 * ATT_BLK)[None, :]
    dist = jnp.clip(ATT_BLK + qi - kj, 0, ATT_BLK)
    return jnp.stack([_t5_bucket(dist * dil) for _, dil in ATT_GROUPS]).astype(I32)


def _permute_rows(t, dil):
    if dil == 1:
        return t
    s, w = t.shape
    return t.reshape(s // dil, dil, w).transpose(1, 0, 2).reshape(s, w)


def _unpermute_rows(t, dil):
    if dil == 1:
        return t
    s, w = t.shape
    return t.reshape(dil, s // dil, w).transpose(1, 0, 2).reshape(s, w)


def _retention_fwd(rqk, rv, din, qd, kd, cd):
    nc = SEQ // RET_CHUNK
    c, dk, dv = RET_CHUNK, RET_DK, RET_DV

    def kern(q_ref, k_ref, v_ref, din_ref, qd_ref, kd_ref, cd_ref, o_ref, st_ref, state):
        h, n = pl.program_id(0), pl.program_id(1)

        @pl.when(n == 0)
        def _():
            state[...] = jnp.zeros_like(state)

        q, k, v = q_ref[...], k_ref[...], v_ref[...]
        s_b = state[...].astype(BF16)
        st_ref[...] = s_b
        a = lax.dot_general(q, k, (((1,), (1,)), ((), ())), preferred_element_type=F32) * din_ref[...]
        o = jnp.dot(a.astype(BF16), v, preferred_element_type=F32)
        o += jnp.dot(q, s_b, preferred_element_type=F32) * qd_ref[...]
        o_ref[...] = o
        kk = (k.astype(F32) * kd_ref[...]).astype(BF16)
        upd = lax.dot_general(kk, v, (((0,), (0,)), ((), ())), preferred_element_type=F32)
        state[...] = state[...] * cd_ref[h] + upd

    return pl.pallas_call(
        kern, name="retention_fwd", grid=(RET_HEADS, nc),
        in_specs=[
            pl.BlockSpec((c, dk), lambda h, n: (n, h)),
            pl.BlockSpec((c, dk), lambda h, n: (n, RET_HEADS + h)),
            pl.BlockSpec((c, dv), lambda h, n: (n, h)),
            pl.BlockSpec((None, c, c), lambda h, n: (h, 0, 0)),
            pl.BlockSpec((None, c, 1), lambda h, n: (h, 0, 0)),
            pl.BlockSpec((None, c, 1), lambda h, n: (h, 0, 0)),
            pl.BlockSpec(memory_space=pltpu.SMEM),
        ],
        out_specs=[
            pl.BlockSpec((c, dv), lambda h, n: (n, h)),
            pl.BlockSpec((None, None, dk, dv), lambda h, n: (h, n, 0, 0)),
        ],
        out_shape=[
            jax.ShapeDtypeStruct((SEQ, RET_V_W), F32),
            jax.ShapeDtypeStruct((RET_HEADS, nc, dk, dv), BF16),
        ],
        scratch_shapes=[pltpu.VMEM((dk, dv), F32)],
        compiler_params=_cparams(("arbitrary", "arbitrary")),
    )(rqk, rqk, rv, din, qd, kd, cd)


def _retention_bwd(rqk, rv, states, d_ro, din, qd, kd, cd, cos, sin):
    nc = SEQ // RET_CHUNK
    c, dk, dv = RET_CHUNK, RET_DK, RET_DV
    half = dk // 2
    last = nc - 1

    def unrot(g, cs, sn):
        g1, g2 = g[:, :half], g[:, half:]
        return jnp.concatenate([g1 * cs + g2 * sn, g2 * cs - g1 * sn], axis=-1)

    def kern(q_ref, k_ref, v_ref, st_ref, do_ref, din_ref, qd_ref, kd_ref, cd_ref, cos_ref, sin_ref,
             dq_ref, dk_ref, dv_ref, dstate):
        h, step = pl.program_id(0), pl.program_id(1)

        @pl.when(step == 0)
        def _():
            dstate[...] = jnp.zeros_like(dstate)

        q, k, v, s_b = q_ref[...], k_ref[...], v_ref[...], st_ref[...]
        d_o = do_ref[...]
        d_ob = d_o.astype(BF16)
        d_oq = (d_o * qd_ref[...]).astype(BF16)
        ds_b = dstate[...].astype(BF16)
        din_m = din_ref[...]
        nt = (((1,), (1,)), ((), ()))
        tn = (((0,), (0,)), ((), ()))
        a_b = (lax.dot_general(q, k, nt, preferred_element_type=F32) * din_m).astype(BF16)
        kk = (k.astype(F32) * kd_ref[...]).astype(BF16)
        d_v = lax.dot_general(a_b, d_ob, tn, preferred_element_type=F32)
        d_v += jnp.dot(kk, ds_b, preferred_element_type=F32)
        d_a = (lax.dot_general(d_ob, v, nt, preferred_element_type=F32) * din_m).astype(BF16)
        d_q = jnp.dot(d_a, k, preferred_element_type=F32)
        d_q += lax.dot_general(d_oq, s_b, nt, preferred_element_type=F32)
        d_k = lax.dot_general(d_a, q, tn, preferred_element_type=F32)
        d_k += lax.dot_general(v, ds_b, nt, preferred_element_type=F32) * kd_ref[...]
        dstate[...] = dstate[...] * cd_ref[h] + lax.dot_general(q, d_oq, tn, preferred_element_type=F32)
        cs, sn = cos_ref[...], sin_ref[...]
        dq_ref[...] = unrot(d_q, cs, sn).astype(BF16)
        dk_ref[...] = (unrot(d_k, cs, sn) * (RET_DK ** -0.5)).astype(BF16)
        dv_ref[...] = d_v.astype(BF16)

    return pl.pallas_call(
        kern, name="retention_bwd", grid=(RET_HEADS, nc),
        in_specs=[
            pl.BlockSpec((c, dk), lambda h, n: (last - n, h)),
            pl.BlockSpec((c, dk), lambda h, n: (last - n, RET_HEADS + h)),
            pl.BlockSpec((c, dv), lambda h, n: (last - n, h)),
            pl.BlockSpec((None, None, dk, dv), lambda h, n: (h, last - n, 0, 0)),
            pl.BlockSpec((c, dv), lambda h, n: (last - n, h)),
            pl.BlockSpec((None, c, c), lambda h, n: (h, 0, 0)),
            pl.BlockSpec((None, c, 1), lambda h, n: (h, 0, 0)),
            pl.BlockSpec((None, c, 1), lambda h, n: (h, 0, 0)),
            pl.BlockSpec(memory_space=pltpu.SMEM),
            pl.BlockSpec((c, half), lambda h, n: (last - n, 0)),
            pl.BlockSpec((c, half), lambda h, n: (last - n, 0)),
        ],
        out_specs=[
            pl.BlockSpec((c, dk), lambda h, n: (last - n, h)),
            pl.BlockSpec((c, dk), lambda h, n: (last - n, h)),
            pl.BlockSpec((c, dv), lambda h, n: (last - n, h)),
        ],
        out_shape=[
            jax.ShapeDtypeStruct((SEQ, RET_QK_W), BF16),
            jax.ShapeDtypeStruct((SEQ, RET_QK_W), BF16),
            jax.ShapeDtypeStruct((SEQ, RET_V_W), BF16),
        ],
        scratch_shapes=[pltpu.VMEM((dk, dv), F32)],
        compiler_params=_cparams(("arbitrary", "arbitrary")),
    )(rqk, rqk, rv, states, d_ro, din, qd, kd, cd, cos, sin)


def _bias_build(rel_bias, buckets):
    ng = len(ATT_GROUPS)

    def kern(tab_ref, bkt_ref, o_ref):
        g, h = pl.program_id(0), pl.program_id(1)
        bkt = bkt_ref[...]
        acc = jnp.zeros(bkt.shape, F32)
        for b in range(REL_BUCKETS):
            acc = jnp.where(bkt == b, tab_ref[b, g * ATT_HPG + h], acc)
        o_ref[...] = acc

    return pl.pallas_call(
        kern, name="bias_build", grid=(ng, ATT_HPG),
        in_specs=[pl.BlockSpec(memory_space=pltpu.SMEM),
                  pl.BlockSpec((None, ATT_BLK, 2 * ATT_BLK), lambda g, h: (g, 0, 0))],
        out_specs=pl.BlockSpec((None, None, ATT_BLK, 2 * ATT_BLK), lambda g, h: (g, h, 0, 0)),
        out_shape=jax.ShapeDtypeStruct((ng, ATT_HPG, ATT_BLK, 2 * ATT_BLK), F32),
        compiler_params=_cparams(("arbitrary", "arbitrary")),
    )(rel_bias, buckets)


def _bias_grad(dsb, buckets):
    ng = len(ATT_GROUPS)

    def kern(ds_ref, bkt_ref, o_ref):
        g, h = pl.program_id(0), pl.program_id(1)
        bkt, ds = bkt_ref[...], ds_ref[...]
        for b in range(REL_BUCKETS):
            o_ref[b, g * ATT_HPG + h] = jnp.sum(jnp.where(bkt == b, ds, 0.0))

    return pl.pallas_call(
        kern, name="bias_grad", grid=(ng, ATT_HPG),
        in_specs=[pl.BlockSpec((None, None, ATT_BLK, 2 * ATT_BLK), lambda g, h: (g, h, 0, 0)),
                  pl.BlockSpec((None, ATT_BLK, 2 * ATT_BLK), lambda g, h: (g, 0, 0))],
        out_specs=pl.BlockSpec(memory_space=pltpu.SMEM),
        out_shape=jax.ShapeDtypeStruct((REL_BUCKETS, N_ATT_HEADS), F32),
        compiler_params=_cparams(("arbitrary", "arbitrary")),
    )(dsb, buckets)


_NT = (((1,), (1,)), ((), ()))
_TN = (((0,), (0,)), ((), ()))
_ATT_SCALE = ATT_DH ** -0.5


_PAD_ROWS = SEQ + ATT_BLK


def _window_mask(has_prev):
    qi = lax.broadcasted_iota(I32, (ATT_BLK, 2 * ATT_BLK), 0)
    kj = lax.broadcasted_iota(I32, (ATT_BLK, 2 * ATT_BLK), 1)
    prev_ok = jnp.logical_and(jnp.logical_and(kj < ATT_BLK, kj >= qi), has_prev)
    return jnp.logical_or(prev_ok, jnp.logical_and(kj >= ATT_BLK, qi >= kj - ATT_BLK))


def _head_specs(col0):
    return pl.BlockSpec((SEQ, ATT_DH), lambda h: (0, col0 + h))


def _att_fwd(gi, qkv, bias, nb):
    blk, dh = ATT_BLK, ATT_DH

    def kern(q_ref, k_ref, v_ref, b_ref, o_ref, l_ref, kpad, vpad):
        zero = jnp.zeros((blk, dh), BF16)
        kpad[0:blk, :] = zero
        vpad[0:blk, :] = zero
        kpad[blk:, :] = k_ref[...]
        vpad[blk:, :] = v_ref[...]
        bias_m = b_ref[...]

        def body(b, carry):
            r0 = pl.multiple_of(b * blk, blk)
            q = q_ref[pl.ds(r0, blk), :]
            kw = kpad[pl.ds(r0, 2 * blk), :]
            vw = vpad[pl.ds(r0, 2 * blk), :]
            valid = _window_mask((b % nb) > 0)
            s = lax.dot_general(q, kw, _NT, preferred_element_type=F32) * _ATT_SCALE + bias_m
            s = jnp.where(valid, s, -1e30)
            mx = jnp.max(s, axis=-1, keepdims=True)
            e = jnp.exp(s - mx)
            den = jnp.sum(e, axis=-1, keepdims=True)
            o_ref[pl.ds(r0, blk), :] = jnp.dot((e / den).astype(BF16), vw, preferred_element_type=F32)
            l_ref[pl.ds(r0, blk), :] = jnp.broadcast_to(mx + jnp.log(den), (blk, dh))
            return carry

        lax.fori_loop(0, N_BLK, body, 0)

    return pl.pallas_call(
        kern, name=f"att_fwd_g{gi}", grid=(ATT_HPG,),
        in_specs=[_head_specs(0), _head_specs(ATT_HPG), _head_specs(2 * ATT_HPG),
                  pl.BlockSpec((None, None, blk, 2 * blk), lambda h: (gi, h, 0, 0))],
        out_specs=[_head_specs(0), _head_specs(0)],
        out_shape=[jax.ShapeDtypeStruct((SEQ, ATT_W), F32), jax.ShapeDtypeStruct((SEQ, ATT_W), F32)],
        scratch_shapes=[pltpu.VMEM((_PAD_ROWS, dh), BF16), pltpu.VMEM((_PAD_ROWS, dh), BF16)],
        compiler_params=_cparams(("arbitrary",)),
    )(qkv, qkv, qkv, bias)


def _att_bwd(gi, qkv, d_att, lse, dd, bias, nb):
    blk, dh = ATT_BLK, ATT_DH

    def kern(q_ref, k_ref, v_ref, do_ref, l_ref, d_ref, b_ref, dq_ref, dk_ref, dv_ref, dsb_ref,
             kpad, vpad, qpad, dopad, lpad, dpad):
        zero = jnp.zeros((blk, dh), BF16)
        zero_f = jnp.zeros((blk, dh), F32)
        kpad[0:blk, :] = zero
        vpad[0:blk, :] = zero
        kpad[blk:, :] = k_ref[...]
        vpad[blk:, :] = v_ref[...]
        qpad[SEQ:, :] = zero
        dopad[SEQ:, :] = zero
        lpad[SEQ:, :] = zero_f
        dpad[SEQ:, :] = zero_f
        qpad[0:SEQ, :] = q_ref[...]
        dopad[0:SEQ, :] = do_ref[...]
        lpad[0:SEQ, :] = l_ref[...]
        dpad[0:SEQ, :] = d_ref[...]
        bias_m = b_ref[...]
        bias_t = jnp.concatenate([bias_m[:, blk:], bias_m[:, :blk]], axis=0)
        dsb_ref[...] = jnp.zeros_like(dsb_ref)

        def dq_body(b, carry):
            r0 = pl.multiple_of(b * blk, blk)
            q, d_o = q_ref[pl.ds(r0, blk), :], do_ref[pl.ds(r0, blk), :]
            kw, vw = kpad[pl.ds(r0, 2 * blk), :], vpad[pl.ds(r0, 2 * blk), :]
            lrow, drow = l_ref[pl.ds(r0, blk), :][:, :1], d_ref[pl.ds(r0, blk), :][:, :1]
            valid = _window_mask((b % nb) > 0)
            s = lax.dot_general(q, kw, _NT, preferred_element_type=F32) * _ATT_SCALE + bias_m
            p = jnp.where(valid, jnp.exp(jnp.where(valid, s, -1e30) - lrow), 0.0)
            dp = lax.dot_general(d_o, vw, _NT, preferred_element_type=F32)
            ds = p * (dp - drow)
            dq = jnp.dot(ds.astype(BF16), kw, preferred_element_type=F32)
            dq_ref[pl.ds(r0, blk), :] = (dq * _ATT_SCALE).astype(BF16)
            dsb_ref[...] += ds
            return carry

        lax.fori_loop(0, N_BLK, dq_body, 0)

        qi = lax.broadcasted_iota(I32, (2 * blk, blk), 0)
        kj = lax.broadcasted_iota(I32, (2 * blk, blk), 1)

        def dkv_body(b, carry):
            r0 = pl.multiple_of(b * blk, blk)
            k, v = k_ref[pl.ds(r0, blk), :], v_ref[pl.ds(r0, blk), :]
            qw, dow = qpad[pl.ds(r0, 2 * blk), :], dopad[pl.ds(r0, 2 * blk), :]
            lrow, drow = lpad[pl.ds(r0, 2 * blk), :][:, :1], dpad[pl.ds(r0, 2 * blk), :][:, :1]
            has_next = jnp.logical_and(b + 1 < N_BLK, ((b + 1) % nb) > 0)
            next_ok = jnp.logical_and(jnp.logical_and(qi >= blk, kj >= qi - blk), has_next)
            valid = jnp.logical_or(jnp.logical_and(qi < blk, qi >= kj), next_ok)
            s = lax.dot_general(qw, k, _NT, preferred_element_type=F32) * _ATT_SCALE + bias_t
            p = jnp.where(valid, jnp.exp(jnp.where(valid, s, -1e30) - lrow), 0.0)
            dp = lax.dot_general(dow, v, _NT, preferred_element_type=F32)
            ds = p * (dp - drow)
            d_v = lax.dot_general(p.astype(BF16), dow, _TN, preferred_element_type=F32)
            d_k = lax.dot_general(ds.astype(BF16), qw, _TN, preferred_element_type=F32)
            dk_ref[pl.ds(r0, blk), :] = (d_k * _ATT_SCALE).astype(BF16)
            dv_ref[pl.ds(r0, blk), :] = d_v.astype(BF16)
            return carry

        lax.fori_loop(0, N_BLK, dkv_body, 0)

    return pl.pallas_call(
        kern, name=f"att_bwd_g{gi}", grid=(ATT_HPG,),
        in_specs=[_head_specs(0), _head_specs(ATT_HPG), _head_specs(2 * ATT_HPG),
                  _head_specs(0), _head_specs(0), _head_specs(0),
                  pl.BlockSpec((None, None, blk, 2 * blk), lambda h: (gi, h, 0, 0))],
        out_specs=[_head_specs(0), _head_specs(0), _head_specs(0),
                   pl.BlockSpec((None, blk, 2 * blk), lambda h: (h, 0, 0))],
        out_shape=[jax.ShapeDtypeStruct((SEQ, ATT_W), BF16)] * 3
        + [jax.ShapeDtypeStruct((ATT_HPG, blk, 2 * blk), F32)],
        scratch_shapes=[pltpu.VMEM((_PAD_ROWS, dh), BF16)] * 4 + [pltpu.VMEM((_PAD_ROWS, dh), F32)] * 2,
        compiler_params=_cparams(("arbitrary",)),
    )(qkv, qkv, qkv, d_att, lse, dd, bias)


def _rms_parts(x):
    r = lax.rsqrt(jnp.mean(x * x, axis=-1, keepdims=True) + RMS_EPS)
    return x * r, r


def _rms_bwd(d_xhat, xhat, r):
    return r * (d_xhat - xhat * jnp.mean(d_xhat * xhat, axis=-1, keepdims=True))


def _prenorm_fwd(name, x, gain, shift, scale):
    def body(xt, g, sh, sc):
        xhat, _ = _rms_parts(xt)
        return (xhat * g) * (1.0 + sc) + sh
    return _rowmap(name, body, [x], [gain, shift, scale], [(D_MODEL, BF16)])[0]


def _prenorm_bwd(name, d_hs, x, gain, scale, resid):
    n_dh = len(d_hs)

    def body(*args):
        d_h = args[0]
        for t in args[1:n_dh]:
            d_h = d_h + t
        xt, res, g, sc = args[n_dh:]
        xhat, r = _rms_parts(xt)
        nrm = xhat * g
        d_n = d_h * (1.0 + sc)
        dx = _rms_bwd(d_n * g, xhat, r) + res
        return (dx, jnp.sum(d_h, axis=0, keepdims=True), jnp.sum(d_h * nrm, axis=0, keepdims=True),
                jnp.sum(d_n * xhat, axis=0, keepdims=True))

    return _rowmap(name, body, list(d_hs) + [x, resid], [gain, scale], [(D_MODEL, F32)],
                   [D_MODEL, D_MODEL, D_MODEL])


def _gn_parts(ro):
    mu = jnp.mean(ro, axis=-1, keepdims=True)
    cen = ro - mu
    rstd = lax.rsqrt(jnp.mean(cen * cen, axis=-1, keepdims=True) + GN_EPS)
    return cen * rstd, rstd


def _retpost_fwd(ro, rg, gn_g, gn_b):
    def body(rot, rgt, g, b):
        outs = []
        for h in range(RET_HEADS):
            sl = slice(h * RET_DV, (h + 1) * RET_DV)
            nrm, _ = _gn_parts(rot[:, sl])
            gate = rgt[:, sl]
            outs.append((gate * _sigmoid(gate)) * (nrm * g[:, sl] + b[:, sl]))
        return jnp.concatenate(outs, axis=-1)
    return _rowmap("retpost_fwd", body, [ro, rg], [gn_g, gn_b], [(RET_V_W, BF16)])[0]


def _retpost_bwd(d_gated, ro, rg, gn_g, gn_b):
    def body(dgt, rot, rgt, g, b):
        d_ro, d_rg, d_g, d_b = [], [], [], []
        for h in range(RET_HEADS):
            sl = slice(h * RET_DV, (h + 1) * RET_DV)
            nrm, rstd = _gn_parts(rot[:, sl])
            gate, dg = rgt[:, sl], dgt[:, sl]
            sg = _sigmoid(gate)
            ron = nrm * g[:, sl] + b[:, sl]
            d_rg.append(dg * ron * (sg * (1.0 + gate * (1.0 - sg))))
            d_ron = dg * (gate * sg)
            d_g.append(jnp.sum(d_ron * nrm, axis=0, keepdims=True))
            d_b.append(jnp.sum(d_ron, axis=0, keepdims=True))
            d_n = d_ron * g[:, sl]
            d_ro.append(rstd * (d_n - jnp.mean(d_n, axis=-1, keepdims=True)
                                - nrm * jnp.mean(d_n * nrm, axis=-1, keepdims=True)))
        cat = lambda ts: jnp.concatenate(ts, axis=-1)
        return cat(d_ro), cat(d_rg), cat(d_g), cat(d_b)
    return _rowmap("retpost_bwd", body, [d_gated, ro, rg], [gn_g, gn_b],
                   [(RET_V_W, F32), (RET_V_W, BF16)], [RET_V_W, RET_V_W])


def _combine(os_, ls_):
    def body(o0, o1, o2, l0, l1, l2):
        mx = jnp.maximum(jnp.maximum(l0, l1), l2)
        e0, e1, e2 = jnp.exp(l0 - mx), jnp.exp(l1 - mx), jnp.exp(l2 - mx)
        den = e0 + e1 + e2
        att = (e0 / den) * o0 + (e1 / den) * o1 + (e2 / den) * o2
        return att, att, mx + jnp.log(den)
    return _rowmap("att_combine", body, list(os_) + list(ls_), [],
                   [(ATT_W, F32), (ATT_W, BF16), (ATT_W, F32)])


def _att_bwd_pre(d_att, att):
    def body(dt, at):
        outs = []
        for h in range(ATT_HPG):
            sl = slice(h * ATT_DH, (h + 1) * ATT_DH)
            outs.append(jnp.broadcast_to(jnp.sum(dt[:, sl] * at[:, sl], axis=-1, keepdims=True),
                                         (dt.shape[0], ATT_DH)))
        return dt, jnp.concatenate(outs, axis=-1)
    return _rowmap("att_bwd_pre", body, [d_att, att], [], [(ATT_W, BF16), (ATT_W, F32)])


def _merge_fwd(gates, ret_out, att_out):
    def body(gt, ro, ao):
        return _sigmoid(gt[:, :D_MODEL]) * ro + _sigmoid(gt[:, D_MODEL:]) * ao
    return _rowmap("merge_fwd", body, [gates, ret_out, att_out], [], [(D_MODEL, BF16)])[0]


def _merge_bwd(d_merged, gates, ret_out, att_out):
    def body(dm, gt, ro, ao):
        sa, sb = _sigmoid(gt[:, :D_MODEL]), _sigmoid(gt[:, D_MODEL:])
        d_gates = jnp.concatenate([dm * ro * (sa * (1.0 - sa)), dm * ao * (sb * (1.0 - sb))], axis=-1)
        return dm * sa, dm * sb, d_gates
    return _rowmap("merge_bwd", body, [d_merged, gates, ret_out, att_out], [],
                   [(D_MODEL, BF16), (D_MODEL, BF16), (2 * D_MODEL, BF16)])


def _gate_bwd(name, d_x, branch, gate):
    def body(dx, br, g):
        return dx * g, jnp.sum(dx * br, axis=0, keepdims=True)
    return _rowmap(name, body, [d_x, branch], [gate], [(D_MODEL, BF16)], [D_MODEL])


def _loss_head(x3, target, gain):
    def body(xt, tt, g):
        xhat, r = _rms_parts(xt)
        err = xhat * g - tt
        d_y = err / D_MODEL
        loss = 0.5 * jnp.sum(jnp.mean(err * err, axis=-1, keepdims=True), axis=0, keepdims=True)
        d_x = _rms_bwd(d_y * g, xhat, r)
        return d_x, jnp.broadcast_to(loss, (1, 128)), jnp.sum(d_y * xhat, axis=0, keepdims=True)
    return _rowmap("loss_head", body, [x3, target], [gain], [(D_MODEL, F32)], [128, D_MODEL])


def _local_step(pos, x, target, mod, norm1_g, norm2_g, norm_f_g, rel_bias, gn_g, gn_b,
                w_in, w_ret_out, w_att_out, w_o, w_ff1, w_ff2):
    sh1, sc1, g1, sh2, sc2, g2 = [mod[:, i * D_MODEL:(i + 1) * D_MODEL] for i in range(6)]
    cos, sin = _rope_tables()
    din, qd, kd, cd = _decay_tables()
    buckets = _bucket_tables()
    bias = _bias_build(rel_bias, buckets)
    dils = [d for _, d in ATT_GROUPS]
    nbs = [SEQ // d // ATT_BLK for d in dils]

    h1 = _prenorm_fwd("prenorm1_fwd", x, norm1_g, sh1, sc1)
    h1_p = [_permute_rows(h1, d) for d in dils]

    def rot_epi(acc, cs, sn, scale):
        half = RET_DK // 2
        x1, x2 = acc[:, :half], acc[:, half:]
        return (jnp.concatenate([x1 * cs - x2 * sn, x1 * sn + x2 * cs], axis=-1) * scale,)

    qk_scale = jnp.concatenate([jnp.ones((1, RET_QK_W), F32),
                                jnp.full((1, RET_QK_W), RET_DK ** -0.5, F32)], axis=-1)
    rope_ex = [(cos, (TM, RET_DK // 2), lambda i, j, kk: (i, 0)),
               (sin, (TM, RET_DK // 2), lambda i, j, kk: (i, 0)),
               (qk_scale, (1, RET_DK), lambda i, j, kk: (0, j))]
    rqk = _matmul("proj_qk", h1, w_in, "nn", SEQ, 2 * RET_QK_W, D_MODEL, [BF16], b_off=OFF_Q,
                  tn=RET_DK, tk=D_MODEL, epilogue=rot_epi, extras=rope_ex)[0]
    rv = _matmul("proj_rv", h1, w_in, "nn", SEQ, RET_V_W, D_MODEL, [BF16], b_off=OFF_V, tk=D_MODEL)[0]
    rg = _matmul("proj_rg", h1, w_in, "nn", SEQ, RET_V_W, D_MODEL, [F32], b_off=OFF_G, tk=D_MODEL)[0]
    gates = _matmul("proj_gates", h1, w_in, "nn", SEQ, 2 * D_MODEL, D_MODEL, [F32], b_off=OFF_GATE,
                    tn=512, tk=D_MODEL)[0]
    aqkv = [_matmul(f"proj_att_g{gi}", h1_p[gi], w_in, "nn", SEQ, 3 * ATT_W, D_MODEL, [BF16],
                    b_off=OFF_ATT + gi * 3 * ATT_W, tn=512, tk=D_MODEL)[0] for gi in range(3)]

    ro, states = _retention_fwd(rqk, rv, din, qd, kd, cd)
    gated = _retpost_fwd(ro, rg, gn_g, gn_b)
    ret_out = _matmul("ret_out", gated, w_ret_out, "nn", SEQ, D_MODEL, RET_V_W, [F32])[0]

    os_, ls_ = [], []
    for gi in range(3):
        o_g, l_g = _att_fwd(gi, aqkv[gi], bias, nbs[gi])
        os_.append(_unpermute_rows(o_g, dils[gi]))
        ls_.append(_unpermute_rows(l_g, dils[gi]))
    att, att_b, lse = _combine(os_, ls_)
    att_out = _matmul("att_out", att_b, w_att_out, "nn", SEQ, D_MODEL, ATT_W, [F32])[0]

    merged = _merge_fwd(gates, ret_out, att_out)

    def resid_epi(acc, xt, g):
        return xt + g * acc, acc

    def resid_ex(xin, g):
        return [(xin, (TM, TN), lambda i, j, kk: (i, j)), (g, (1, TN), lambda i, j, kk: (0, j))]

    x2, mix = _matmul("mix_out", merged, w_o, "nn", SEQ, D_MODEL, D_MODEL, [F32, F32],
                      epilogue=resid_epi, extras=resid_ex(x, g1))
    h2 = _prenorm_fwd("prenorm2_fwd", x2, norm2_g, sh2, sc2)

    def relu2_epi(acc):
        r = jnp.maximum(acc, 0.0)
        return r * r, acc

    act, u = _matmul("ff1", h2, w_ff1, "nn", SEQ, D_FF, D_MODEL, [BF16, F32], tk=D_MODEL,
                     epilogue=relu2_epi)
    x3, y2 = _matmul("ff2", act, w_ff2, "nn", SEQ, D_MODEL, D_FF, [F32, F32],
                     epilogue=resid_epi, extras=resid_ex(x2, g2))

    d_x3, loss, d_gf = _loss_head(x3, target, norm_f_g)

    d_y2, d_g2 = _gate_bwd("ff_gate_bwd", d_x3, y2, g2)

    def relu2_bwd_epi(acc, ut):
        return (acc * (2.0 * jnp.maximum(ut, 0.0)),)

    recv = {}
    gw_ff2 = _matmul_tn_pair("ff2_dw", pos, act, d_y2, D_FF, D_MODEL, SEQ, D_FF // N_CHIPS,
                             tm=512, tn=1024, tk=1024)
    d_u, recv["w_ff2"] = _matmul(
        "ff2_dx", d_y2, w_ff2, "nt", SEQ, D_FF, D_MODEL, [BF16], epilogue=relu2_bwd_epi,
        extras=[(u, (TM, TN), lambda i, j, kk: (i, j))], carry=(gw_ff2, 0, SHARD["w_ff2"]))
    gw_ff1 = _matmul_tn_pair("ff1_dw", pos, h2, d_u, D_MODEL, D_FF, SEQ, D_MODEL,
                             tm=512, tn=1024, tk=1024)
    d_h2, recv["w_ff1"] = _matmul("ff1_dx", d_u, w_ff1, "nt", SEQ, D_MODEL, D_FF, [F32],
                                  carry=(gw_ff1, 1, SHARD["w_ff1"]))
    d_x2, d_sh2, d_sc2, d_n2g = _prenorm_bwd("prenorm2_bwd", [d_h2], x2, norm2_g, sc2, d_x3)

    d_mix, d_g1 = _gate_bwd("mix_gate_bwd", d_x2, mix, g1)
    gw_o = _matmul_tn_pair("mix_dw", pos, merged, d_mix, D_MODEL, D_MODEL, SEQ, D_MODEL // N_CHIPS,
                           tm=128, tn=1024, tk=2048)
    d_merged, recv["w_o"] = _matmul("mix_dx", d_mix, w_o, "nt", SEQ, D_MODEL, D_MODEL, [F32],
                                    carry=(gw_o, 0, SHARD["w_o"]))
    d_ret_out, d_att_out, d_gates = _merge_bwd(d_merged, gates, ret_out, att_out)

    gw_ret_out = _matmul_tn_pair("ret_out_dw", pos, gated, d_ret_out, RET_V_W, D_MODEL, SEQ,
                                 RET_V_W // N_CHIPS, tm=256, tn=1024, tk=1024)
    d_gated, recv["w_ret_out"] = _matmul("ret_out_dx", d_ret_out, w_ret_out, "nt", SEQ, RET_V_W, D_MODEL,
                                         [F32], carry=(gw_ret_out, 0, SHARD["w_ret_out"]))
    gw_att_out = _matmul_tn_pair("att_out_dw", pos, att_b, d_att_out, ATT_W, D_MODEL, SEQ, ATT_W,
                                 tm=256, tn=1024, tk=2048)
    d_att, recv["w_att_out"] = _matmul("att_out_dx", d_att_out, w_att_out, "nt", SEQ, ATT_W, D_MODEL,
                                       [F32], carry=(gw_att_out, 1, SHARD["w_att_out"]))

    d_ro, d_rg, d_gn_g, d_gn_b = _retpost_bwd(d_gated, ro, rg, gn_g, gn_b)
    d_rq, d_rk, d_rv = _retention_bwd(rqk, rv, states, d_ro, din, qd, kd, cd, cos, sin)

    d_att_b, dd = _att_bwd_pre(d_att, att)
    d_aqkv, dsbs = [], []
    for gi in range(3):
        da_p = _permute_rows(d_att_b, dils[gi])
        l_p = _permute_rows(lse, dils[gi])
        dd_p = _permute_rows(dd, dils[gi])
        dq, dk, dv, dsb = _att_bwd(gi, aqkv[gi], da_p, l_p, dd_p, bias, nbs[gi])
        d_aqkv.append(_unpermute_rows(jnp.concatenate([dq, dk, dv], axis=-1), dils[gi]))
        dsbs.append(dsb)
    d_rel_bias = _bias_grad(jnp.stack(dsbs), buckets)

    d_proj = jnp.concatenate([d_rq, d_rk, d_rv, d_rg] + d_aqkv + [d_gates], axis=-1)
    gw_in = _matmul_tn_pair("proj_dw", pos, h1, d_proj, D_MODEL, IN_COLS, SEQ, D_MODEL,
                            tm=512, tn=640, tk=2048)
    d_h1, recv["w_in"] = _matmul("proj_dx", d_proj, w_in, "nt", SEQ, D_MODEL, IN_COLS, [F32],
                                 tn=1024, tk=1280, carry=(gw_in, 1, SHARD["w_in"]))

    grad_x, d_sh1, d_sc1, d_n1g = _prenorm_bwd("prenorm1_bwd", [d_h1], x, norm1_g, sc1, d_x2)
    d_mod = jnp.concatenate([d_sh1, d_sc1, d_g1, d_sh2, d_sc2, d_g2], axis=-1)
    small = dict(norm1_g=d_n1g, norm2_g=d_n2g, norm_f_g=d_gf, gn_g=d_gn_g, gn_b=d_gn_b,
                 rel_bias=d_rel_bias)
    big = dict(w_in=gw_in, w_ret_out=gw_ret_out, w_att_out=gw_att_out, w_o=gw_o, w_ff1=gw_ff1,
               w_ff2=gw_ff2)
    return loss, grad_x, d_mod, small, big, recv


def _me():
    return lax.axis_index("x"), lax.axis_index("y"), lax.axis_index("c")


def _peer(x, y, c, mask):
    return (x ^ ((mask >> 2) & 1), y ^ ((mask >> 1) & 1), c ^ (mask & 1))


def _gather8(src_ref, dst_ref, send_sems, recv_sems):
    x, y, c = _me()
    me = 4 * x + 2 * y + c
    copies = []
    for mask in range(1, N_DEV):
        cp = pltpu.make_async_remote_copy(
            src_ref=src_ref, dst_ref=dst_ref.at[me], send_sem=send_sems.at[mask - 1],
            recv_sem=recv_sems.at[mask - 1], device_id=_peer(x, y, c, mask), device_id_type=MESH)
        cp.start()
        copies.append(cp)
    dst_ref[me] = src_ref[...]
    for cp in copies:
        cp.wait_recv()
    for cp in copies:
        cp.wait_send()


def _ada_fwd(c_in, w_ada, b_ada):
    ncol = ADA_COLS // N_CHIPS

    def body(c_ref, w_ref, b_ref, mod_ref, sc_ref, cbuf, cg, mbuf, mg, s1, r1, s2, r2):
        x, y, c = _me()
        me = 4 * x + 2 * y + c
        cv = c_ref[...]
        cbuf[...] = jnp.broadcast_to(cv * _sigmoid(cv), cbuf.shape)
        _gather8(cbuf, cg, s1, r1)
        rows = lax.broadcasted_iota(I32, (N_DEV, D_MODEL), 0)
        sc_all = jnp.zeros((N_DEV, D_MODEL), F32)
        for d in range(N_DEV):
            sc_all = jnp.where(rows == d, cg[d], sc_all)
        sc_ref[...] = sc_all
        mbuf[...] = jnp.dot(sc_all.astype(BF16), w_ref[...].astype(BF16), preferred_element_type=F32)
        _gather8(mbuf, mg, s2, r2)
        rowsel = lax.broadcasted_iota(I32, (N_DEV, ncol), 0) == me
        for k in range(N_CHIPS):
            blk = mg[2 * k]
            row = jnp.sum(jnp.where(rowsel, blk, 0.0), axis=0, keepdims=True)
            mod_ref[:, k * ncol:(k + 1) * ncol] = row + b_ref[:, k * ncol:(k + 1) * ncol]

    vm = pl.BlockSpec(memory_space=pltpu.VMEM)
    return pl.pallas_call(
        body, name="ada_fwd",
        in_specs=[vm, vm, vm], out_specs=[vm, vm],
        out_shape=[jax.ShapeDtypeStruct((1, ADA_COLS), F32), jax.ShapeDtypeStruct((N_DEV, D_MODEL), F32)],
        scratch_shapes=[
            pltpu.VMEM((8, D_MODEL), F32), pltpu.VMEM((N_DEV, 8, D_MODEL), F32),
            pltpu.VMEM((8, ncol), F32), pltpu.VMEM((N_DEV, 8, ncol), F32),
            pltpu.SemaphoreType.DMA((N_DEV - 1,)), pltpu.SemaphoreType.DMA((N_DEV - 1,)),
            pltpu.SemaphoreType.DMA((N_DEV - 1,)), pltpu.SemaphoreType.DMA((N_DEV - 1,)),
        ],
        compiler_params=pltpu.CompilerParams(vmem_limit_bytes=VMEM_LIMIT_V7X),
    )(c_in, w_ada, b_ada)


def _small_reduce(pack, sc_all):
    ncol = ADA_COLS // N_CHIPS

    def body(p_ref, sc_ref, tot_ref, gw_ref, pg, s1, r1):
        x, y, _ = _me()
        chip = 2 * x + y
        _gather8(p_ref, pg, s1, r1)
        tot = pg[0]
        for d in range(1, N_DEV):
            tot = tot + pg[d]
        tot_ref[...] = tot
        rows = lax.broadcasted_iota(I32, (N_DEV, ncol), 0)
        dmod = jnp.zeros((N_DEV, ncol), F32)
        for k in range(N_CHIPS):
            part = jnp.zeros((N_DEV, ncol), F32)
            for d in range(N_DEV):
                part = jnp.where(rows == d, pg[d, :, k * ncol:(k + 1) * ncol][0:1, :], part)
            dmod = jnp.where(chip == k, part, dmod)
        gw_ref[...] = lax.dot_general(sc_ref[...].astype(BF16), dmod.astype(BF16), _TN,
                                      preferred_element_type=F32)

    vm = pl.BlockSpec(memory_space=pltpu.VMEM)
    return pl.pallas_call(
        body, name="small_reduce",
        in_specs=[vm, vm], out_specs=[vm, vm],
        out_shape=[jax.ShapeDtypeStruct((8, ADA_COLS), F32), jax.ShapeDtypeStruct((D_MODEL, ncol), F32)],
        scratch_shapes=[pltpu.VMEM((N_DEV, 8, ADA_COLS), F32),
                        pltpu.SemaphoreType.DMA((N_DEV - 1,)), pltpu.SemaphoreType.DMA((N_DEV - 1,))],
        compiler_params=pltpu.CompilerParams(vmem_limit_bytes=VMEM_LIMIT_V7X),
    )(pack, sc_all)


BIG = (("w_in", 1), ("w_ret_out", 0), ("w_att_out", 1), ("w_o", 0), ("w_ff1", 1), ("w_ff2", 0))
SHARD = {"w_in": (D_MODEL, IN_COLS // N_CHIPS), "w_ret_out": (RET_V_W // N_CHIPS, D_MODEL),
         "w_att_out": (ATT_W, D_MODEL // N_CHIPS), "w_o": (D_MODEL // N_CHIPS, D_MODEL),
         "w_ff1": (D_MODEL, D_FF // N_CHIPS), "w_ff2": (D_FF // N_CHIPS, D_MODEL)}
_CHIP_FLIPS = ((1, 0), (0, 1), (1, 1))


def _region(ref, axis, chip, half, shard_shape):
    r, cw = shard_shape
    hr = r // 2
    if axis == 1:
        return ref.at[pl.ds(half * hr, hr), pl.ds(chip * cw, cw)]
    return ref.at[pl.ds(chip * r + half * hr, hr), :]


def _gather_weights(shards):
    nw = len(BIG)
    shapes = [s.shape for s in shards]
    full_shapes = [(r, N_CHIPS * cw) if ax == 1 else (N_CHIPS * r, cw)
                   for (r, cw), (_, ax) in zip(shapes, BIG)]

    def body(*refs):
        ins, outs = refs[:nw], refs[nw:2 * nw]
        loc_sem, s_ici, r_ici, s_d2d, r_d2d = refs[2 * nw:]
        x, y, c = _me()
        chip = 2 * x + y
        local, first, passed = [], [], []
        for i, (_, ax) in enumerate(BIG):
            r, cw = shapes[i]
            own = outs[i].at[:, pl.ds(chip * cw, cw)] if ax == 1 else outs[i].at[pl.ds(chip * r, r), :]
            cp = pltpu.make_async_copy(ins[i], own, loc_sem.at[i])
            cp.start()
            local.append(cp)
        for j, (fx, fy) in enumerate(_CHIP_FLIPS):
            for i, (_, ax) in enumerate(BIG):
                hr = shapes[i][0] // 2
                cp = pltpu.make_async_remote_copy(
                    src_ref=ins[i].at[pl.ds(c * hr, hr), :],
                    dst_ref=_region(outs[i], ax, chip, c, shapes[i]),
                    send_sem=s_ici.at[j * nw + i], recv_sem=r_ici.at[j * nw + i],
                    device_id=(x ^ fx, y ^ fy, c), device_id_type=MESH)
                cp.start()
                first.append(cp)
        for j, (fx, fy) in enumerate(_CHIP_FLIPS):
            src_chip = 2 * (x ^ fx) + (y ^ fy)
            for i, (_, ax) in enumerate(BIG):
                reg = _region(outs[i], ax, src_chip, c, shapes[i])
                first[j * nw + i].wait_recv()
                cp = pltpu.make_async_remote_copy(
                    src_ref=reg, dst_ref=reg, send_sem=s_d2d.at[j * nw + i],
                    recv_sem=r_d2d.at[j * nw + i], device_id=(x, y, 1 - c), device_id_type=MESH)
                cp.start()
                passed.append(cp)
        for cp in passed:
            cp.wait_recv()
        for cp in first + passed:
            cp.wait_send()
        for cp in local:
            cp.wait()

    hbm = pl.BlockSpec(memory_space=pl.ANY)
    return pl.pallas_call(
        body, name="gather_weights",
        in_specs=[hbm] * nw, out_specs=[hbm] * nw,
        out_shape=[jax.ShapeDtypeStruct(fs, BF16) for fs in full_shapes],
        scratch_shapes=[pltpu.SemaphoreType.DMA((nw,)),
                        pltpu.SemaphoreType.DMA((3 * nw,)), pltpu.SemaphoreType.DMA((3 * nw,)),
                        pltpu.SemaphoreType.DMA((3 * nw,)), pltpu.SemaphoreType.DMA((3 * nw,))],
    )(*shards)


def _adam_update(w, g, m, v):
    mn = ADAM_B1 * m + (1.0 - ADAM_B1) * g
    vn = ADAM_B2 * v + (1.0 - ADAM_B2) * (g * g)
    m_hat = mn / (1.0 - ADAM_B1 ** ADAM_STEP)
    v_hat = vn / (1.0 - ADAM_B2 ** ADAM_STEP)
    return -ADAM_LR * (m_hat / (jnp.sqrt(v_hat) + ADAM_EPS) + ADAM_WD * w), mn, vn


def _final_update(name, pos, axis, psum, recv, w, m, v, tr=64):
    r, cw = w.shape
    hr = r // 2
    tr = min(tr, hr)
    nt = hr // tr

    def kern(pos_ref, p_ref, r_ref, w_ref, m_ref, v_ref, g_ref, d_ref, nm_ref, nv_ref,
             send_buf, land_buf, s_sem, r_sem):
        p, t = pl.program_id(0), pl.program_id(1)
        sib = _sibling()

        def copy(i):
            return pltpu.make_async_remote_copy(
                src_ref=send_buf.at[i], dst_ref=land_buf.at[i], send_sem=s_sem.at[i],
                recv_sem=r_sem.at[i], device_id=sib, device_id_type=MESH)

        def update(tot):
            g_ref[...] = tot
            d_ref[...], nm_ref[...], nv_ref[...] = _adam_update(w_ref[...], tot, m_ref[...], v_ref[...])

        @pl.when(p == 0)
        def _():
            tot = p_ref[...].astype(F32)
            for j in range(3):
                tot = tot + r_ref[j].astype(F32)
            send_buf[t] = tot
            copy(t).start()
            update(tot)

        @pl.when(p == 1)
        def _():
            copy(t).wait_recv()
            update(land_buf[t])

        @pl.when(jnp.logical_and(p == 1, t == nt - 1))
        def _():
            for i in range(nt):
                copy(i).wait_send()

    def shard_rows(p, t, pos_ref):
        return (jnp.where(p == 0, pos_ref[0], 1 - pos_ref[0]) * nt + t, 0)

    def own_part(p, t, pos_ref):
        tt = jnp.where(p == 0, t, nt - 1)
        return (tt, pos_ref[1]) if axis == 1 else (pos_ref[1] * nt + tt, 0)

    shard_spec = pl.BlockSpec((tr, cw), shard_rows)
    grid_spec = pltpu.PrefetchScalarGridSpec(
        num_scalar_prefetch=1, grid=(2, nt),
        in_specs=[pl.BlockSpec((tr, cw), own_part),
                  pl.BlockSpec((3, tr, cw), lambda p, t, pos_ref: (0, jnp.where(p == 0, t, nt - 1), 0)),
                  shard_spec, shard_spec, shard_spec],
        out_specs=[shard_spec] * 4,
        scratch_shapes=[pltpu.VMEM((nt, tr, cw), F32), pltpu.VMEM((nt, tr, cw), F32),
                        pltpu.SemaphoreType.DMA((nt,)), pltpu.SemaphoreType.DMA((nt,))])
    return pl.pallas_call(
        kern, name=name, grid_spec=grid_spec, out_shape=[jax.ShapeDtypeStruct((r, cw), F32)] * 4,
        compiler_params=_cparams(("arbitrary", "arbitrary")),
    )(pos, psum, recv, w, m, v)


def _adamw(name, w, g, m, v):
    r, cw = w.shape
    tr = min(r, 128)

    def kern(w_ref, g_ref, m_ref, v_ref, d_ref, nm_ref, nv_ref):
        d_ref[...], nm_ref[...], nv_ref[...] = _adam_update(w_ref[...], g_ref[...], m_ref[...], v_ref[...])

    spec = pl.BlockSpec((tr, cw), lambda i: (i, 0))
    return pl.pallas_call(
        kern, name=name, grid=(r // tr,), in_specs=[spec] * 4, out_specs=[spec] * 3,
        out_shape=[jax.ShapeDtypeStruct((r, cw), F32)] * 3, compiler_params=_cparams(("parallel",)),
    )(w, g, m, v)


_PACK_W = ADA_COLS
_NB = REL_BUCKETS * N_ATT_HEADS
_SMALL_SLOTS = {
    "b_ada": (0, 0, ADA_COLS),
    "norm1_g": (1, 0, D_MODEL), "norm2_g": (1, D_MODEL, D_MODEL), "norm_f_g": (1, 2 * D_MODEL, D_MODEL),
    "ret_gn_g": (1, 3 * D_MODEL, RET_V_W),
    "ret_gn_b": (2, 0, RET_V_W), "rel_bias": (2, RET_V_W, _NB), "loss": (2, RET_V_W + 512, 128),
}


def _pack_small(vals):
    rows = []
    for r in range(8):
        items = sorted([(off, n) for n, (rr, off, _) in _SMALL_SLOTS.items() if rr == r and n in vals])
        parts, pos = [], 0
        for off, n in items:
            if off > pos:
                parts.append(jnp.zeros((1, off - pos), F32))
            parts.append(vals[n].reshape(1, -1).astype(F32))
            pos = off + _SMALL_SLOTS[n][2]
        if pos < _PACK_W:
            parts.append(jnp.zeros((1, _PACK_W - pos), F32))
        rows.append(jnp.concatenate(parts, axis=-1))
    return jnp.concatenate(rows, axis=0)


def _unpack_small(pack, name):
    r, off, wd = _SMALL_SLOTS[name]
    return pack[r:r + 1, off:off + wd]


def kernel(x, c, w_ada, b_ada, norm1_g, w_in, rel_bias, ret_gn_g, ret_gn_b, w_ret_out, w_att_out, w_o, norm2_g, w_ff1, w_ff2, norm_f_g, loss_target, m_w_ada, m_b_ada, m_norm1_g, m_w_in, m_rel_bias, m_ret_gn_g, m_ret_gn_b, m_w_ret_out, m_w_att_out, m_w_o, m_norm2_g, m_w_ff1, m_w_ff2, m_norm_f_g, v_w_ada, v_b_ada, v_norm1_g, v_w_in, v_rel_bias, v_ret_gn_g, v_ret_gn_b, v_w_ret_out, v_w_att_out, v_w_o, v_norm2_g, v_w_ff1, v_w_ff2, v_norm_f_g):
    given = dict(locals())
    big_names = [n for n, _ in BIG]
    shard_w = {n: given[n][0] for n in big_names}
    assert all(shard_w[n].shape == SHARD[n] for n in big_names)

    full = _gather_weights([shard_w[n].astype(BF16) for n in big_names])
    full = dict(zip(big_names, full))
    mod, sc_all = _ada_fwd(c, w_ada[0], b_ada)
    pos = _where_am_i()

    loss, grad_x, d_mod, small, big, recv = _local_step(
        pos, x[0], loss_target[0], mod, norm1_g, norm2_g, norm_f_g.reshape(1, -1), rel_bias, ret_gn_g,
        ret_gn_b, full["w_in"], full["w_ret_out"], full["w_att_out"], full["w_o"], full["w_ff1"],
        full["w_ff2"])

    pack_g = _pack_small(dict(b_ada=d_mod, norm1_g=small["norm1_g"], norm2_g=small["norm2_g"],
                              norm_f_g=small["norm_f_g"], ret_gn_g=small["gn_g"], ret_gn_b=small["gn_b"],
                              rel_bias=small["rel_bias"], loss=loss))
    tot, g_w_ada = _small_reduce(pack_g, sc_all)

    small_names = ["b_ada", "norm1_g", "rel_bias", "ret_gn_g", "ret_gn_b", "norm2_g", "norm_f_g"]
    pack_w = _pack_small({n: given[n] for n in small_names})
    pack_m = _pack_small({n: given["m_" + n] for n in small_names})
    pack_v = _pack_small({n: given["v_" + n] for n in small_names})
    sd, sm, sv = _adamw("adamw_small", pack_w, tot, pack_m, pack_v)

    grads, deltas, new_m, new_v = {}, {}, {}, {}
    for n in small_names:
        shp = given[n].shape
        grads[n] = _unpack_small(tot, n).reshape(shp)
        deltas[n] = _unpack_small(sd, n).reshape(shp)
        new_m[n] = _unpack_small(sm, n).reshape(shp)
        new_v[n] = _unpack_small(sv, n).reshape(shp)
    d, nm, nv = _adamw("adamw_w_ada", w_ada[0], g_w_ada, m_w_ada[0], v_w_ada[0])
    grads["w_ada"], deltas["w_ada"], new_m["w_ada"], new_v["w_ada"] = g_w_ada[None], d[None], nm[None], nv[None]
    for n, ax in BIG:
        g, d, nm, nv = _final_update("final_" + n, pos, ax, big[n], recv[n], given[n][0],
                                     given["m_" + n][0], given["v_" + n][0])
        grads[n], deltas[n], new_m[n], new_v[n] = g[None], d[None], nm[None], nv[None]

    order = ["w_ada", "b_ada", "norm1_g", "w_in", "rel_bias", "ret_gn_g", "ret_gn_b", "w_ret_out",
             "w_att_out", "w_o", "norm2_g", "w_ff1", "w_ff2", "norm_f_g"]
    loss_out = _unpack_small(tot, "loss")[0, 0]
    return (loss_out, grad_x[None], *[grads[n] for n in order], *[deltas[n] for n in order],
            *[new_m[n] for n in order], *[new_v[n] for n in order])
```

```python
import functools
import math

import jax
import jax.numpy as jnp
from jax import lax
from jax.experimental import pallas as pl
from jax.experimental.pallas import tpu as pltpu

F32 = jnp.float32
BF16 = jnp.bfloat16
I32 = jnp.int32

SEQ = 2048
D_MODEL = 1024
RET_HEADS = 4
RET_DK = 256
RET_DV = 512
RET_CHUNK = 128
RET_QK_W = RET_HEADS * RET_DK
RET_V_W = RET_HEADS * RET_DV
ATT_GROUPS = ((128, 1), (512, 4), (2048, 16))
ATT_HPG = 4
ATT_DH = 128
ATT_W = ATT_HPG * ATT_DH
ATT_BLK = 128
N_BLK = SEQ // ATT_BLK
REL_BUCKETS = 32
REL_MAX_DIST = 2048
N_ATT_HEADS = 12
D_FF = 4 * D_MODEL
RMS_EPS = 1e-6
GN_EPS = 1e-5
ROPE_BASE = 10000.0
IN_COLS = 2 * RET_QK_W + 2 * RET_V_W + 9 * ATT_W + 2 * D_MODEL
OFF_Q, OFF_K, OFF_V, OFF_G = 0, RET_QK_W, 2 * RET_QK_W, 2 * RET_QK_W + RET_V_W
OFF_ATT = 2 * RET_QK_W + 2 * RET_V_W
OFF_GATE = OFF_ATT + 9 * ATT_W
N_CHIPS = 4
N_DEV = 8
ADA_COLS = 6 * D_MODEL

ADAM_LR = 0.001
ADAM_B1 = 0.9
ADAM_B2 = 0.999
ADAM_EPS = 1e-08
ADAM_WD = 0.01
ADAM_STEP = 10

VMEM_LIMIT_V7X = 56 * 1024 * 1024
MESH = pl.DeviceIdType.MESH


def _cparams(sem):
    return pltpu.CompilerParams(dimension_semantics=sem, vmem_limit_bytes=VMEM_LIMIT_V7X)


def _sigmoid(v):
    return 1.0 / (1.0 + jnp.exp(-v))


def _rowmap(name, body, row_ins, bcast_ins, row_outs, sum_outs=(), tm=256):
    m = row_ins[0].shape[0]
    n_in = len(row_ins) + len(bcast_ins)
    n_ro = len(row_outs)

    def kern(*refs):
        vals = [r[...] for r in refs[:n_in]]
        res = body(*vals)
        if not isinstance(res, (tuple, list)):
            res = (res,)
        outs = refs[n_in:]
        for r, v in zip(outs[:n_ro], res[:n_ro]):
            r[...] = v.astype(r.dtype)
        if sum_outs:
            @pl.when(pl.program_id(0) == 0)
            def _():
                for r in outs[n_ro:]:
                    r[...] = jnp.zeros_like(r)
            for r, v in zip(outs[n_ro:], res[n_ro:]):
                r[...] += v

    in_specs = [pl.BlockSpec((tm, a.shape[1]), lambda i: (i, 0)) for a in row_ins]
    in_specs += [pl.BlockSpec(a.shape, lambda i: (0, 0)) for a in bcast_ins]
    out_specs = [pl.BlockSpec((tm, n), lambda i: (i, 0)) for n, _ in row_outs]
    out_specs += [pl.BlockSpec((1, n), lambda i: (0, 0)) for n in sum_outs]
    out_shape = [jax.ShapeDtypeStruct((m, n), dt) for n, dt in row_outs]
    out_shape += [jax.ShapeDtypeStruct((1, n), F32) for n in sum_outs]
    return pl.pallas_call(
        kern, name=name, grid=(m // tm,), in_specs=in_specs, out_specs=out_specs,
        out_shape=out_shape, compiler_params=_cparams(("arbitrary",)),
    )(*row_ins, *bcast_ins)


TM, TN = 1024, 1024


def _matmul(name, a, b, kind, m, n, k, outs, *, b_off=0, tm=TM, tn=TN, tk=1024,
            epilogue=None, extras=(), carry=None):
    tm, tn, tk = min(tm, m), min(tn, n), min(tk, k)
    nk = k // tk
    if kind == "nn":
        a_spec = pl.BlockSpec((tm, tk), lambda i, j, kk: (i, kk))
        b_spec = pl.BlockSpec((tk, tn), lambda i, j, kk: (kk, b_off // tn + j))
        dn = (((1,), (0,)), ((), ()))
    elif kind == "nt":
        a_spec = pl.BlockSpec((tm, tk), lambda i, j, kk: (i, kk))
        b_spec = pl.BlockSpec((tn, tk), lambda i, j, kk: (j, b_off // tk + kk))
        dn = (((1,), (1,)), ((), ()))
    else:
        a_spec = pl.BlockSpec((tk, tm), lambda i, j, kk: (kk, i))
        b_spec = pl.BlockSpec((tk, tn), lambda i, j, kk: (kk, j))
        dn = (((0,), (0,)), ((), ()))
    n_ex, n_out = len(extras), len(outs)
    if epilogue is None:
        epilogue = lambda acc: (acc,)

    def finish(acc, ex_refs, out_refs):
        res = epilogue(acc, *[r[...] for r in ex_refs])
        for r, v in zip(out_refs, res):
            r[...] = v.astype(r.dtype)

    n_c = 0 if carry is None else 1
    ni, nj = m // tm, n // tn

    def kern(a_ref, b_ref, *rest):
        ex_refs = rest[:n_ex]
        out_refs = rest[n_ex + n_c:n_ex + n_c + n_out]
        scratch = rest[n_ex + 2 * n_c + n_out:]
        i, j, kk = pl.program_id(0), pl.program_id(1), pl.program_id(2)
        if carry is not None:
            copies = lambda: _ici_copies(rest[n_ex], rest[n_ex + n_c + n_out], scratch[-2], scratch[-1],
                                         carry[1], carry[2])

            @pl.when(jnp.logical_and(jnp.logical_and(i == 0, j == 0), kk == 0))
            def _():
                for cp in copies():
                    cp.start()

        part = lax.dot_general(a_ref[...], b_ref[...], dn, preferred_element_type=F32)
        if nk == 1:
            finish(part, ex_refs, out_refs)
        else:
            acc_ref = scratch[0]

            @pl.when(kk == 0)
            def _():
                acc_ref[...] = part

            @pl.when(kk > 0)
            def _():
                acc_ref[...] += part

            @pl.when(kk == nk - 1)
            def _():
                finish(acc_ref[...], ex_refs, out_refs)

        if carry is not None:
            @pl.when(jnp.logical_and(jnp.logical_and(i == ni - 1, j == nj - 1), kk == nk - 1))
            def _():
                for cp in copies():
                    cp.wait_recv()
                for cp in copies():
                    cp.wait_send()

    hbm = pl.BlockSpec(memory_space=pl.ANY)
    in_specs = [a_spec, b_spec] + [pl.BlockSpec(bs, im) for _, bs, im in extras] + [hbm] * n_c
    out_specs = [pl.BlockSpec((tm, tn), lambda i, j, kk: (i, j)) for _ in outs] + [hbm] * n_c
    out_shape = [jax.ShapeDtypeStruct((m, n), dt) for dt in outs]
    scratch_shapes = [] if nk == 1 else [pltpu.VMEM((tm, tn), F32)]
    operands = [a, b] + [e[0] for e in extras]
    if carry is not None:
        r, cw = carry[2]
        out_shape.append(jax.ShapeDtypeStruct((3, r // 2, cw), BF16))
        scratch_shapes += [pltpu.SemaphoreType.DMA((3,)), pltpu.SemaphoreType.DMA((3,))]
        operands.append(carry[0])
    sem = ("arbitrary",) * 3 if carry is not None else ("parallel", "parallel", "arbitrary")
    return pl.pallas_call(
        kern, name=name, grid=(ni, nj, nk), in_specs=in_specs, out_specs=out_specs,
        out_shape=out_shape, scratch_shapes=scratch_shapes, compiler_params=_cparams(sem),
    )(*operands)


def _ici_copies(psum_ref, recv_ref, s_sem, r_sem, axis, shard_shape):
    x, y, c = _me()
    hr, cw = shard_shape[0] // 2, shard_shape[1]
    copies = []
    for j, (fx, fy) in enumerate(_CHIP_FLIPS):
        chip = 2 * (x ^ fx) + (y ^ fy)
        src = psum_ref.at[:, pl.ds(chip * cw, cw)] if axis == 1 else psum_ref.at[pl.ds(chip * hr, hr), :]
        copies.append(pltpu.make_async_remote_copy(
            src_ref=src, dst_ref=recv_ref.at[j], send_sem=s_sem.at[j], recv_sem=r_sem.at[j],
            device_id=(x ^ fx, y ^ fy, c), device_id_type=MESH))
    return copies


def _where_am_i():
    x, y, c = _me()
    return jnp.stack([c, 2 * x + y]).astype(I32)


def _sibling():
    x, y, c = _me()
    return (x, y, 1 - c)


def _matmul_tn_pair(name, pos, a, b, m, n, k, shard_rows, *, tm, tn, tk):
    hr = shard_rows // 2
    tm, tn, tk = min(tm, hr), min(tn, n), min(tk, k)
    tph = hr // tm
    nt, nj, nk = (m // 2) // tm, n // tn, k // tk
    n_tiles = nt * nj

    def row_block(p, t, pos_ref):
        half = jnp.where(p == 0, 1 - pos_ref[0], pos_ref[0])
        return (t // tph) * (2 * tph) + half * tph + t % tph

    def kern(pos_ref, a_ref, b_ref, o_ref, acc_ref, send_buf, land_buf, s_sem, r_sem):
        p, t, j, kk = pl.program_id(0), pl.program_id(1), pl.program_id(2), pl.program_id(3)
        idx = t * nj + j
        sib = _sibling()

        def copy(i):
            return pltpu.make_async_remote_copy(
                src_ref=send_buf.at[i], dst_ref=land_buf.at[i], send_sem=s_sem.at[i],
                recv_sem=r_sem.at[i], device_id=sib, device_id_type=MESH)

        part = lax.dot_general(a_ref[...], b_ref[...], _TN, preferred_element_type=F32)

        @pl.when(kk == 0)
        def _():
            acc_ref[...] = part

        @pl.when(kk > 0)
        def _():
            acc_ref[...] += part

        @pl.when(jnp.logical_and(kk == nk - 1, p == 0))
        def _():
            send_buf[idx] = acc_ref[...].astype(BF16)
            copy(idx).start()

        @pl.when(jnp.logical_and(kk == nk - 1, p == 1))
        def _():
            copy(idx).wait_recv()
            o_ref[...] = (acc_ref[...] + land_buf[idx].astype(F32)).astype(BF16)

        @pl.when(jnp.logical_and(jnp.logical_and(p == 1, idx == n_tiles - 1), kk == nk - 1))
        def _():
            for i in range(n_tiles):
                copy(i).wait_send()

    grid_spec = pltpu.PrefetchScalarGridSpec(
        num_scalar_prefetch=1, grid=(2, nt, nj, nk),
        in_specs=[pl.BlockSpec((tk, tm), lambda p, t, j, kk, pos_ref: (kk, row_block(p, t, pos_ref))),
                  pl.BlockSpec((tk, tn), lambda p, t, j, kk, pos_ref: (kk, j))],
        out_specs=pl.BlockSpec((tm, tn), lambda p, t, j, kk, pos_ref: (p * t, p * j)),
        scratch_shapes=[pltpu.VMEM((tm, tn), F32), pltpu.VMEM((n_tiles, tm, tn), BF16),
                        pltpu.VMEM((n_tiles, tm, tn), BF16),
                        pltpu.SemaphoreType.DMA((n_tiles,)), pltpu.SemaphoreType.DMA((n_tiles,))])
    return pl.pallas_call(
        kern, name=name, grid_spec=grid_spec, out_shape=jax.ShapeDtypeStruct((m // 2, n), BF16),
        compiler_params=_cparams(("arbitrary",) * 4),
    )(pos, a, b)


def _rope_tables():
    half = RET_DK // 2
    inv = ROPE_BASE ** (-jnp.arange(half, dtype=F32) / half)
    ang = jnp.arange(SEQ).astype(F32)[:, None] * inv[None, :]
    return jnp.cos(ang), jnp.sin(ang)


def _decay_tables():
    c = RET_CHUNK
    log_g = jnp.log1p(-(2.0 ** (-5.0 - jnp.arange(RET_HEADS, dtype=F32))))
    idx = jnp.arange(c, dtype=F32)
    rel = idx[:, None] - idx[None, :]
    din = jnp.where(rel >= 0, jnp.exp(log_g[:, None, None] * jnp.maximum(rel, 0.0)), 0.0)
    qd = jnp.exp(log_g[:, None] * (idx + 1.0))[:, :, None]
    kd = jnp.exp(log_g[:, None] * (c - 1.0 - idx))[:, :, None]
    cd = jnp.exp(log_g * c)
    return din, qd, kd, cd


def _t5_bucket(dist):
    max_exact = REL_BUCKETS // 2
    d_f = jnp.maximum(dist, 1).astype(F32)
    large = max_exact + (jnp.log(d_f / max_exact) / math.log(REL_MAX_DIST / max_exact)
                         * (REL_BUCKETS - max_exact)).astype(I32)
    large = jnp.minimum(large, REL_BUCKETS - 1)
    return jnp.where(dist < max_exact, dist, large)


def _bucket_tables():
    qi = jnp.arange(ATT_BLK)[:, None]
    kj = jnp.arange(2 * ATT_BLK)[None, :]
    dist = jnp.clip(ATT_BLK + qi - kj, 0, ATT_BLK)
    return jnp.stack([_t5_bucket(dist * dil) for _, dil in ATT_GROUPS]).astype(I32)


def _permute_rows(t, dil):
    if dil == 1:
        return t
    s, w = t.shape
    return t.reshape(s // dil, dil, w).transpose(1, 0, 2).reshape(s, w)


def _unpermute_rows(t, dil):
    if dil == 1:
        return t
    s, w = t.shape
    return t.reshape(dil, s // dil, w).transpose(1, 0, 2).reshape(s, w)


def _retention_fwd(rqk, rv, din, qd, kd, cd):
    nc = SEQ // RET_CHUNK
    c, dk, dv = RET_CHUNK, RET_DK, RET_DV

    def kern(q_ref, k_ref, v_ref, din_ref, qd_ref, kd_ref, cd_ref, o_ref, st_ref, state):
        h, n = pl.program_id(0), pl.program_id(1)

        @pl.when(n == 0)
        def _():
            state[...] = jnp.zeros_like(state)

        q, k, v = q_ref[...], k_ref[...], v_ref[...]
        s_b = state[...].astype(BF16)
        st_ref[...] = s_b
        a = lax.dot_general(q, k, (((1,), (1,)), ((), ())), preferred_element_type=F32) * din_ref[...]
        o = jnp.dot(a.astype(BF16), v, preferred_element_type=F32)
        o += jnp.dot(q, s_b, preferred_element_type=F32) * qd_ref[...]
        o_ref[...] = o
        kk = (k.astype(F32) * kd_ref[...]).astype(BF16)
        upd = lax.dot_general(kk, v, (((0,), (0,)), ((), ())), preferred_element_type=F32)
        state[...] = state[...] * cd_ref[h] + upd

    return pl.pallas_call(
        kern, name="retention_fwd", grid=(RET_HEADS, nc),
        in_specs=[
            pl.BlockSpec((c, dk), lambda h, n: (n, h)),
            pl.BlockSpec((c, dk), lambda h, n: (n, RET_HEADS + h)),
            pl.BlockSpec((c, dv), lambda h, n: (n, h)),
            pl.BlockSpec((None, c, c), lambda h, n: (h, 0, 0)),
            pl.BlockSpec((None, c, 1), lambda h, n: (h, 0, 0)),
            pl.BlockSpec((None, c, 1), lambda h, n: (h, 0, 0)),
            pl.BlockSpec(memory_space=pltpu.SMEM),
        ],
        out_specs=[
            pl.BlockSpec((c, dv), lambda h, n: (n, h)),
            pl.BlockSpec((None, None, dk, dv), lambda h, n: (h, n, 0, 0)),
        ],
        out_shape=[
            jax.ShapeDtypeStruct((SEQ, RET_V_W), F32),
            jax.ShapeDtypeStruct((RET_HEADS, nc, dk, dv), BF16),
        ],
        scratch_shapes=[pltpu.VMEM((dk, dv), F32)],
        compiler_params=_cparams(("arbitrary", "arbitrary")),
    )(rqk, rqk, rv, din, qd, kd, cd)


def _retention_bwd(rqk, rv, states, d_ro, din, qd, kd, cd, cos, sin):
    nc = SEQ // RET_CHUNK
    c, dk, dv = RET_CHUNK, RET_DK, RET_DV
    half = dk // 2
    last = nc - 1

    def unrot(g, cs, sn):
        g1, g2 = g[:, :half], g[:, half:]
        return jnp.concatenate([g1 * cs + g2 * sn, g2 * cs - g1 * sn], axis=-1)

    def kern(q_ref, k_ref, v_ref, st_ref, do_ref, din_ref, qd_ref, kd_ref, cd_ref, cos_ref, sin_ref,
             dq_ref, dk_ref, dv_ref, dstate):
        h, step = pl.program_id(0), pl.program_id(1)

        @pl.when(step == 0)
        def _():
            dstate[...] = jnp.zeros_like(dstate)

        q, k, v, s_b = q_ref[...], k_ref[...], v_ref[...], st_ref[...]
        d_o = do_ref[...]
        d_ob = d_o.astype(BF16)
        d_oq = (d_o * qd_ref[...]).astype(BF16)
        ds_b = dstate[...].astype(BF16)
        din_m = din_ref[...]
        nt = (((1,), (1,)), ((), ()))
        tn = (((0,), (0,)), ((), ()))
        a_b = (lax.dot_general(q, k, nt, preferred_element_type=F32) * din_m).astype(BF16)
        kk = (k.astype(F32) * kd_ref[...]).astype(BF16)
        d_v = lax.dot_general(a_b, d_ob, tn, preferred_element_type=F32)
        d_v += jnp.dot(kk, ds_b, preferred_element_type=F32)
        d_a = (lax.dot_general(d_ob, v, nt, preferred_element_type=F32) * din_m).astype(BF16)
        d_q = jnp.dot(d_a, k, preferred_element_type=F32)
        d_q += lax.dot_general(d_oq, s_b, nt, preferred_element_type=F32)
        d_k = lax.dot_general(d_a, q, tn, preferred_element_type=F32)
        d_k += lax.dot_general(v, ds_b, nt, preferred_element_type=F32) * kd_ref[...]
        dstate[...] = dstate[...] * cd_ref[h] + lax.dot_general(q, d_oq, tn, preferred_element_type=F32)
        cs, sn = cos_ref[...], sin_ref[...]
        dq_ref[...] = unrot(d_q, cs, sn).astype(BF16)
        dk_ref[...] = (unrot(d_k, cs, sn) * (RET_DK ** -0.5)).astype(BF16)
        dv_ref[...] = d_v.astype(BF16)

    return pl.pallas_call(
        kern, name="retention_bwd", grid=(RET_HEADS, nc),
        in_specs=[
            pl.BlockSpec((c, dk), lambda h, n: (last - n, h)),
            pl.BlockSpec((c, dk), lambda h, n: (last - n, RET_HEADS + h)),
            pl.BlockSpec((c, dv), lambda h, n: (last - n, h)),
            pl.BlockSpec((None, None, dk, dv), lambda h, n: (h, last - n, 0, 0)),
            pl.BlockSpec((c, dv), lambda h, n: (last - n, h)),
            pl.BlockSpec((None, c, c), lambda h, n: (h, 0, 0)),
            pl.BlockSpec((None, c, 1), lambda h, n: (h, 0, 0)),
            pl.BlockSpec((None, c, 1), lambda h, n: (h, 0, 0)),
            pl.BlockSpec(memory_space=pltpu.SMEM),
            pl.BlockSpec((c, half), lambda h, n: (last - n, 0)),
            pl.BlockSpec((c, half), lambda h, n: (last - n, 0)),
        ],
        out_specs=[
            pl.BlockSpec((c, dk), lambda h, n: (last - n, h)),
            pl.BlockSpec((c, dk), lambda h, n: (last - n, h)),
            pl.BlockSpec((c, dv), lambda h, n: (last - n, h)),
        ],
        out_shape=[
            jax.ShapeDtypeStruct((SEQ, RET_QK_W), BF16),
            jax.ShapeDtypeStruct((SEQ, RET_QK_W), BF16),
            jax.ShapeDtypeStruct((SEQ, RET_V_W), BF16),
        ],
        scratch_shapes=[pltpu.VMEM((dk, dv), F32)],
        compiler_params=_cparams(("arbitrary", "arbitrary")),
    )(rqk, rqk, rv, states, d_ro, din, qd, kd, cd, cos, sin)


def _bias_build(rel_bias, buckets):
    ng = len(ATT_GROUPS)

    def kern(tab_ref, bkt_ref, o_ref):
        g, h = pl.program_id(0), pl.program_id(1)
        bkt = bkt_ref[...]
        acc = jnp.zeros(bkt.shape, F32)
        for b in range(REL_BUCKETS):
            acc = jnp.where(bkt == b, tab_ref[b, g * ATT_HPG + h], acc)
        o_ref[...] = acc

    return pl.pallas_call(
        kern, name="bias_build", grid=(ng, ATT_HPG),
        in_specs=[pl.BlockSpec(memory_space=pltpu.SMEM),
                  pl.BlockSpec((None, ATT_BLK, 2 * ATT_BLK), lambda g, h: (g, 0, 0))],
        out_specs=pl.BlockSpec((None, None, ATT_BLK, 2 * ATT_BLK), lambda g, h: (g, h, 0, 0)),
        out_shape=jax.ShapeDtypeStruct((ng, ATT_HPG, ATT_BLK, 2 * ATT_BLK), F32),
        compiler_params=_cparams(("arbitrary", "arbitrary")),
    )(rel_bias, buckets)


def _bias_grad(dsb, buckets):
    ng = len(ATT_GROUPS)

    def kern(ds_ref, bkt_ref, o_ref):
        g, h = pl.program_id(0), pl.program_id(1)
        bkt, ds = bkt_ref[...], ds_ref[...]
        for b in range(REL_BUCKETS):
            o_ref[b, g * ATT_HPG + h] = jnp.sum(jnp.where(bkt == b, ds, 0.0))

    return pl.pallas_call(
        kern, name="bias_grad", grid=(ng, ATT_HPG),
        in_specs=[pl.BlockSpec((None, None, ATT_BLK, 2 * ATT_BLK), lambda g, h: (g, h, 0, 0)),
                  pl.BlockSpec((None, ATT_BLK, 2 * ATT_BLK), lambda g, h: (g, 0, 0))],
        out_specs=pl.BlockSpec(memory_space=pltpu.SMEM),
        out_shape=jax.ShapeDtypeStruct((REL_BUCKETS, N_ATT_HEADS), F32),
        compiler_params=_cparams(("arbitrary", "arbitrary")),
    )(dsb, buckets)


_NT = (((1,), (1,)), ((), ()))
_TN = (((0,), (0,)), ((), ()))
_ATT_SCALE = ATT_DH ** -0.5


_PAD_ROWS = SEQ + ATT_BLK


def _window_mask(has_prev):
    qi = lax.broadcasted_iota(I32, (ATT_BLK, 2 * ATT_BLK), 0)
    kj = lax.broadcasted_iota(I32, (ATT_BLK, 2 * ATT_BLK), 1)
    prev_ok = jnp.logical_and(jnp.logical_and(kj < ATT_BLK, kj >= qi), has_prev)
    return jnp.logical_or(prev_ok, jnp.logical_and(kj >= ATT_BLK, qi >= kj - ATT_BLK))


def _head_specs(col0):
    return pl.BlockSpec((SEQ, ATT_DH), lambda h: (0, col0 + h))


def _att_fwd(gi, qkv, bias, nb):
    blk, dh = ATT_BLK, ATT_DH

    def kern(q_ref, k_ref, v_ref, b_ref, o_ref, l_ref, kpad, vpad):
        zero = jnp.zeros((blk, dh), BF16)
        kpad[0:blk, :] = zero
        vpad[0:blk, :] = zero
        kpad[blk:, :] = k_ref[...]
        vpad[blk:, :] = v_ref[...]
        bias_m = b_ref[...]

        def body(b, carry):
            r0 = pl.multiple_of(b * blk, blk)
            q = q_ref[pl.ds(r0, blk), :]
            kw = kpad[pl.ds(r0, 2 * blk), :]
            vw = vpad[pl.ds(r0, 2 * blk), :]
            valid = _window_mask((b % nb) > 0)
            s = lax.dot_general(q, kw, _NT, preferred_element_type=F32) * _ATT_SCALE + bias_m
            s = jnp.where(valid, s, -1e30)
            mx = jnp.max(s, axis=-1, keepdims=True)
            e = jnp.exp(s - mx)
            den = jnp.sum(e, axis=-1, keepdims=True)
            o_ref[pl.ds(r0, blk), :] = jnp.dot((e / den).astype(BF16), vw, preferred_element_type=F32)
            l_ref[pl.ds(r0, blk), :] = jnp.broadcast_to(mx + jnp.log(den), (blk, dh))
            return carry

        lax.fori_loop(0, N_BLK, body, 0)

    return pl.pallas_call(
        kern, name=f"att_fwd_g{gi}", grid=(ATT_HPG,),
        in_specs=[_head_specs(0), _head_specs(ATT_HPG), _head_specs(2 * ATT_HPG),
                  pl.BlockSpec((None, None, blk, 2 * blk), lambda h: (gi, h, 0, 0))],
        out_specs=[_head_specs(0), _head_specs(0)],
        out_shape=[jax.ShapeDtypeStruct((SEQ, ATT_W), F32), jax.ShapeDtypeStruct((SEQ, ATT_W), F32)],
        scratch_shapes=[pltpu.VMEM((_PAD_ROWS, dh), BF16), pltpu.VMEM((_PAD_ROWS, dh), BF16)],
        compiler_params=_cparams(("arbitrary",)),
    )(qkv, qkv, qkv, bias)


def _att_bwd(gi, qkv, d_att, lse, dd, bias, nb):
    blk, dh = ATT_BLK, ATT_DH

    def kern(q_ref, k_ref, v_ref, do_ref, l_ref, d_ref, b_ref, dq_ref, dk_ref, dv_ref, dsb_ref,
             kpad, vpad, qpad, dopad, lpad, dpad):
        zero = jnp.zeros((blk, dh), BF16)
        zero_f = jnp.zeros((blk, dh), F32)
        kpad[0:blk, :] = zero
        vpad[0:blk, :] = zero
        kpad[blk:, :] = k_ref[...]
        vpad[blk:, :] = v_ref[...]
        qpad[SEQ:, :] = zero
        dopad[SEQ:, :] = zero
        lpad[SEQ:, :] = zero_f
        dpad[SEQ:, :] = zero_f
        qpad[0:SEQ, :] = q_ref[...]
        dopad[0:SEQ, :] = do_ref[...]
        lpad[0:SEQ, :] = l_ref[...]
        dpad[0:SEQ, :] = d_ref[...]
        bias_m = b_ref[...]
        bias_t = jnp.concatenate([bias_m[:, blk:], bias_m[:, :blk]], axis=0)
        dsb_ref[...] = jnp.zeros_like(dsb_ref)

        def dq_body(b, carry):
            r0 = pl.multiple_of(b * blk, blk)
            q, d_o = q_ref[pl.ds(r0, blk), :], do_ref[pl.ds(r0, blk), :]
            kw, vw = kpad[pl.ds(r0, 2 * blk), :], vpad[pl.ds(r0, 2 * blk), :]
            lrow, drow = l_ref[pl.ds(r0, blk), :][:, :1], d_ref[pl.ds(r0, blk), :][:, :1]
            valid = _window_mask((b % nb) > 0)
            s = lax.dot_general(q, kw, _NT, preferred_element_type=F32) * _ATT_SCALE + bias_m
            p = jnp.where(valid, jnp.exp(jnp.where(valid, s, -1e30) - lrow), 0.0)
            dp = lax.dot_general(d_o, vw, _NT, preferred_element_type=F32)
            ds = p * (dp - drow)
            dq = jnp.dot(ds.astype(BF16), kw, preferred_element_type=F32)
            dq_ref[pl.ds(r0, blk), :] = (dq * _ATT_SCALE).astype(BF16)
            dsb_ref[...] += ds
            return carry

        lax.fori_loop(0, N_BLK, dq_body, 0)

        qi = lax.broadcasted_iota(I32, (2 * blk, blk), 0)
        kj = lax.broadcasted_iota(I32, (2 * blk, blk), 1)

        def dkv_body(b, carry):
            r0 = pl.multiple_of(b * blk, blk)
            k, v = k_ref[pl.ds(r0, blk), :], v_ref[pl.ds(r0, blk), :]
            qw, dow = qpad[pl.ds(r0, 2 * blk), :], dopad[pl.ds(r0, 2 * blk), :]
            lrow, drow = lpad[pl.ds(r0, 2 * blk), :][:, :1], dpad[pl.ds(r0, 2 * blk), :][:, :1]
            has_next = jnp.logical_and(b + 1 < N_BLK, ((b + 1) % nb) > 0)
            next_ok = jnp.logical_and(jnp.logical_and(qi >= blk, kj >= qi - blk), has_next)
            valid = jnp.logical_or(jnp.logical_and(qi < blk, qi >= kj), next_ok)
            s = lax.dot_general(qw, k, _NT, preferred_element_type=F32) * _ATT_SCALE + bias_t
            p = jnp.where(valid, jnp.exp(jnp.where(valid, s, -1e30) - lrow), 0.0)
            dp = lax.dot_general(dow, v, _NT, preferred_element_type=F32)
            ds = p * (dp - drow)
            d_v = lax.dot_general(p.astype(BF16), dow, _TN, preferred_element_type=F32)
            d_k = lax.dot_general(ds.astype(BF16), qw, _TN, preferred_element_type=F32)
            dk_ref[pl.ds(r0, blk), :] = (d_k * _ATT_SCALE).astype(BF16)
            dv_ref[pl.ds(r0, blk), :] = d_v.astype(BF16)
            return carry

        lax.fori_loop(0, N_BLK, dkv_body, 0)

    return pl.pallas_call(
        kern, name=f"att_bwd_g{gi}", grid=(ATT_HPG,),
        in_specs=[_head_specs(0), _head_specs(ATT_HPG), _head_specs(2 * ATT_HPG),
                  _head_specs(0), _head_specs(0), _head_specs(0),
                  pl.BlockSpec((None, None, blk, 2 * blk), lambda h: (gi, h, 0, 0))],
        out_specs=[_head_specs(0), _head_specs(0), _head_specs(0),
                   pl.BlockSpec((None, blk, 2 * blk), lambda h: (h, 0, 0))],
        out_shape=[jax.ShapeDtypeStruct((SEQ, ATT_W), BF16)] * 3
        + [jax.ShapeDtypeStruct((ATT_HPG, blk, 2 * blk), F32)],
        scratch_shapes=[pltpu.VMEM((_PAD_ROWS, dh), BF16)] * 4 + [pltpu.VMEM((_PAD_ROWS, dh), F32)] * 2,
        compiler_params=_cparams(("arbitrary",)),
    )(qkv, qkv, qkv, d_att, lse, dd, bias)


def _rms_parts(x):
    r = lax.rsqrt(jnp.mean(x * x, axis=-1, keepdims=True) + RMS_EPS)
    return x * r, r


def _rms_bwd(d_xhat, xhat, r):
    return r * (d_xhat - xhat * jnp.mean(d_xhat * xhat, axis=-1, keepdims=True))


def _prenorm_fwd(name, x, gain, shift, scale):
    def body(xt, g, sh, sc):
        xhat, _ = _rms_parts(xt)
        return (xhat * g) * (1.0 + sc) + sh
    return _rowmap(name, body, [x], [gain, shift, scale], [(D_MODEL, BF16)])[0]


def _prenorm_bwd(name, d_hs, x, gain, scale, resid):
    n_dh = len(d_hs)

    def body(*args):
        d_h = args[0]
        for t in args[1:n_dh]:
            d_h = d_h + t
        xt, res, g, sc = args[n_dh:]
        xhat, r = _rms_parts(xt)
        nrm = xhat * g
        d_n = d_h * (1.0 + sc)
        dx = _rms_bwd(d_n * g, xhat, r) + res
        return (dx, jnp.sum(d_h, axis=0, keepdims=True), jnp.sum(d_h * nrm, axis=0, keepdims=True),
                jnp.sum(d_n * xhat, axis=0, keepdims=True))

    return _rowmap(name, body, list(d_hs) + [x, resid], [gain, scale], [(D_MODEL, F32)],
                   [D_MODEL, D_MODEL, D_MODEL])


def _gn_parts(ro):
    mu = jnp.mean(ro, axis=-1, keepdims=True)
    cen = ro - mu
    rstd = lax.rsqrt(jnp.mean(cen * cen, axis=-1, keepdims=True) + GN_EPS)
    return cen * rstd, rstd


def _retpost_fwd(ro, rg, gn_g, gn_b):
    def body(rot, rgt, g, b):
        outs = []
        for h in range(RET_HEADS):
            sl = slice(h * RET_DV, (h + 1) * RET_DV)
            nrm, _ = _gn_parts(rot[:, sl])
            gate = rgt[:, sl]
            outs.append((gate * _sigmoid(gate)) * (nrm * g[:, sl] + b[:, sl]))
        return jnp.concatenate(outs, axis=-1)
    return _rowmap("retpost_fwd", body, [ro, rg], [gn_g, gn_b], [(RET_V_W, BF16)])[0]


def _retpost_bwd(d_gated, ro, rg, gn_g, gn_b):
    def body(dgt, rot, rgt, g, b):
        d_ro, d_rg, d_g, d_b = [], [], [], []
        for h in range(RET_HEADS):
            sl = slice(h * RET_DV, (h + 1) * RET_DV)
            nrm, rstd = _gn_parts(rot[:, sl])
            gate, dg = rgt[:, sl], dgt[:, sl]
            sg = _sigmoid(gate)
            ron = nrm * g[:, sl] + b[:, sl]
            d_rg.append(dg * ron * (sg * (1.0 + gate * (1.0 - sg))))
            d_ron = dg * (gate * sg)
            d_g.append(jnp.sum(d_ron * nrm, axis=0, keepdims=True))
            d_b.append(jnp.sum(d_ron, axis=0, keepdims=True))
            d_n = d_ron * g[:, sl]
            d_ro.append(rstd * (d_n - jnp.mean(d_n, axis=-1, keepdims=True)
                                - nrm * jnp.mean(d_n * nrm, axis=-1, keepdims=True)))
        cat = lambda ts: jnp.concatenate(ts, axis=-1)
        return cat(d_ro), cat(d_rg), cat(d_g), cat(d_b)
    return _rowmap("retpost_bwd", body, [d_gated, ro, rg], [gn_g, gn_b],
                   [(RET_V_W, F32), (RET_V_W, BF16)], [RET_V_W, RET_V_W])


def _combine(os_, ls_):
    def body(o0, o1, o2, l0, l1, l2):
        mx = jnp.maximum(jnp.maximum(l0, l1), l2)
        e0, e1, e2 = jnp.exp(l0 - mx), jnp.exp(l1 - mx), jnp.exp(l2 - mx)
        den = e0 + e1 + e2
        att = (e0 / den) * o0 + (e1 / den) * o1 + (e2 / den) * o2
        return att, att, mx + jnp.log(den)
    return _rowmap("att_combine", body, list(os_) + list(ls_), [],
                   [(ATT_W, F32), (ATT_W, BF16), (ATT_W, F32)])


def _att_bwd_pre(d_att, att):
    def body(dt, at):
        outs = []
        for h in range(ATT_HPG):
            sl = slice(h * ATT_DH, (h + 1) * ATT_DH)
            outs.append(jnp.broadcast_to(jnp.sum(dt[:, sl] * at[:, sl], axis=-1, keepdims=True),
                                         (dt.shape[0], ATT_DH)))
        return dt, jnp.concatenate(outs, axis=-1)
    return _rowmap("att_bwd_pre", body, [d_att, att], [], [(ATT_W, BF16), (ATT_W, F32)])


def _merge_fwd(gates, ret_out, att_out):
    def body(gt, ro, ao):
        return _sigmoid(gt[:, :D_MODEL]) * ro + _sigmoid(gt[:, D_MODEL:]) * ao
    return _rowmap("merge_fwd", body, [gates, ret_out, att_out], [], [(D_MODEL, BF16)])[0]


def _merge_bwd(d_merged, gates, ret_out, att_out):
    def body(dm, gt, ro, ao):
        sa, sb = _sigmoid(gt[:, :D_MODEL]), _sigmoid(gt[:, D_MODEL:])
        d_gates = jnp.concatenate([dm * ro * (sa * (1.0 - sa)), dm * ao * (sb * (1.0 - sb))], axis=-1)
        return dm * sa, dm * sb, d_gates
    return _rowmap("merge_bwd", body, [d_merged, gates, ret_out, att_out], [],
                   [(D_MODEL, BF16), (D_MODEL, BF16), (2 * D_MODEL, BF16)])


def _gate_bwd(name, d_x, branch, gate):
    def body(dx, br, g):
        return dx * g, jnp.sum(dx * br, axis=0, keepdims=True)
    return _rowmap(name, body, [d_x, branch], [gate], [(D_MODEL, BF16)], [D_MODEL])


def _loss_head(x3, target, gain):
    def body(xt, tt, g):
        xhat, r = _rms_parts(xt)
        err = xhat * g - tt
        d_y = err / D_MODEL
        loss = 0.5 * jnp.sum(jnp.mean(err * err, axis=-1, keepdims=True), axis=0, keepdims=True)
        d_x = _rms_bwd(d_y * g, xhat, r)
        return d_x, jnp.broadcast_to(loss, (1, 128)), jnp.sum(d_y * xhat, axis=0, keepdims=True)
    return _rowmap("loss_head", body, [x3, target], [gain], [(D_MODEL, F32)], [128, D_MODEL])


def _local_step(pos, x, target, mod, norm1_g, norm2_g, norm_f_g, rel_bias, gn_g, gn_b,
                w_in, w_ret_out, w_att_out, w_o, w_ff1, w_ff2):
    sh1, sc1, g1, sh2, sc2, g2 = [mod[:, i * D_MODEL:(i + 1) * D_MODEL] for i in range(6)]
    cos, sin = _rope_tables()
    din, qd, kd, cd = _decay_tables()
    buckets = _bucket_tables()
    bias = _bias_build(rel_bias, buckets)
    dils = [d for _, d in ATT_GROUPS]
    nbs = [SEQ // d // ATT_BLK for d in dils]

    h1 = _prenorm_fwd("prenorm1_fwd", x, norm1_g, sh1, sc1)
    h1_p = [_permute_rows(h1, d) for d in dils]

    def rot_epi(acc, cs, sn, scale):
        half = RET_DK // 2
        x1, x2 = acc[:, :half], acc[:, half:]
        return (jnp.concatenate([x1 * cs - x2 * sn, x1 * sn + x2 * cs], axis=-1) * scale,)

    qk_scale = jnp.concatenate([jnp.ones((1, RET_QK_W), F32),
                                jnp.full((1, RET_QK_W), RET_DK ** -0.5, F32)], axis=-1)
    rope_ex = [(cos, (TM, RET_DK // 2), lambda i, j, kk: (i, 0)),
               (sin, (TM, RET_DK // 2), lambda i, j, kk: (i, 0)),
               (qk_scale, (1, RET_DK), lambda i, j, kk: (0, j))]
    rqk = _matmul("proj_qk", h1, w_in, "nn", SEQ, 2 * RET_QK_W, D_MODEL, [BF16], b_off=OFF_Q,
                  tn=RET_DK, tk=D_MODEL, epilogue=rot_epi, extras=rope_ex)[0]
    rv = _matmul("proj_rv", h1, w_in, "nn", SEQ, RET_V_W, D_MODEL, [BF16], b_off=OFF_V, tk=D_MODEL)[0]
    rg = _matmul("proj_rg", h1, w_in, "nn", SEQ, RET_V_W, D_MODEL, [F32], b_off=OFF_G, tk=D_MODEL)[0]
    gates = _matmul("proj_gates", h1, w_in, "nn", SEQ, 2 * D_MODEL, D_MODEL, [F32], b_off=OFF_GATE,
                    tn=512, tk=D_MODEL)[0]
    aqkv = [_matmul(f"proj_att_g{gi}", h1_p[gi], w_in, "nn", SEQ, 3 * ATT_W, D_MODEL, [BF16],
                    b_off=OFF_ATT + gi * 3 * ATT_W, tn=512, tk=D_MODEL)[0] for gi in range(3)]

    ro, states = _retention_fwd(rqk, rv, din, qd, kd, cd)
    gated = _retpost_fwd(ro, rg, gn_g, gn_b)
    ret_out = _matmul("ret_out", gated, w_ret_out, "nn", SEQ, D_MODEL, RET_V_W, [F32])[0]

    os_, ls_ = [], []
    for gi in range(3):
        o_g, l_g = _att_fwd(gi, aqkv[gi], bias, nbs[gi])
        os_.append(_unpermute_rows(o_g, dils[gi]))
        ls_.append(_unpermute_rows(l_g, dils[gi]))
    att, att_b, lse = _combine(os_, ls_)
    att_out = _matmul("att_out", att_b, w_att_out, "nn", SEQ, D_MODEL, ATT_W, [F32])[0]

    merged = _merge_fwd(gates, ret_out, att_out)

    def resid_epi(acc, xt, g):
        return xt + g * acc, acc

    def resid_ex(xin, g):
        return [(xin, (TM, TN), lambda i, j, kk: (i, j)), (g, (1, TN), lambda i, j, kk: (0, j))]

    x2, mix = _matmul("mix_out", merged, w_o, "nn", SEQ, D_MODEL, D_MODEL, [F32, F32],
                      epilogue=resid_epi, extras=resid_ex(x, g1))
    h2 = _prenorm_fwd("prenorm2_fwd", x2, norm2_g, sh2, sc2)

    def relu2_epi(acc):
        r = jnp.maximum(acc, 0.0)
        return r * r, acc

    act, u = _matmul("ff1", h2, w_ff1, "nn", SEQ, D_FF, D_MODEL, [BF16, F32], tk=D_MODEL,
                     epilogue=relu2_epi)
    x3, y2 = _matmul("ff2", act, w_ff2, "nn", SEQ, D_MODEL, D_FF, [F32, F32],
                     epilogue=resid_epi, extras=resid_ex(x2, g2))

    d_x3, loss, d_gf = _loss_head(x3, target, norm_f_g)

    d_y2, d_g2 = _gate_bwd("ff_gate_bwd", d_x3, y2, g2)

    def relu2_bwd_epi(acc, ut):
        return (acc * (2.0 * jnp.maximum(ut, 0.0)),)

    recv = {}
    gw_ff2 = _matmul_tn_pair("ff2_dw", pos, act, d_y2, D_FF, D_MODEL, SEQ, D_FF // N_CHIPS,
                             tm=512, tn=1024, tk=1024)
    d_u, recv["w_ff2"] = _matmul(
        "ff2_dx", d_y2, w_ff2, "nt", SEQ, D_FF, D_MODEL, [BF16], epilogue=relu2_bwd_epi,
        extras=[(u, (TM, TN), lambda i, j, kk: (i, j))], carry=(gw_ff2, 0, SHARD["w_ff2"]))
    gw_ff1 = _matmul_tn_pair("ff1_dw", pos, h2, d_u, D_MODEL, D_FF, SEQ, D_MODEL,
                             tm=512, tn=1024, tk=1024)
    d_h2, recv["w_ff1"] = _matmul("ff1_dx", d_u, w_ff1, "nt", SEQ, D_MODEL, D_FF, [F32],
                                  carry=(gw_ff1, 1, SHARD["w_ff1"]))
    d_x2, d_sh2, d_sc2, d_n2g = _prenorm_bwd("prenorm2_bwd", [d_h2], x2, norm2_g, sc2, d_x3)

    d_mix, d_g1 = _gate_bwd("mix_gate_bwd", d_x2, mix, g1)
    gw_o = _matmul_tn_pair("mix_dw", pos, merged, d_mix, D_MODEL, D_MODEL, SEQ, D_MODEL // N_CHIPS,
                           tm=128, tn=1024, tk=2048)
    d_merged, recv["w_o"] = _matmul("mix_dx", d_mix, w_o, "nt", SEQ, D_MODEL, D_MODEL, [F32],
                                    carry=(gw_o, 0, SHARD["w_o"]))
    d_ret_out, d_att_out, d_gates = _merge_bwd(d_merged, gates, ret_out, att_out)

    gw_ret_out = _matmul_tn_pair("ret_out_dw", pos, gated, d_ret_out, RET_V_W, D_MODEL, SEQ,
                                 RET_V_W // N_CHIPS, tm=256, tn=1024, tk=1024)
    d_gated, recv["w_ret_out"] = _matmul("ret_out_dx", d_ret_out, w_ret_out, "nt", SEQ, RET_V_W, D_MODEL,
                                         [F32], carry=(gw_ret_out, 0, SHARD["w_ret_out"]))
    gw_att_out = _matmul_tn_pair("att_out_dw", pos, att_b, d_att_out, ATT_W, D_MODEL, SEQ, ATT_W,
                                 tm=256, tn=1024, tk=2048)
    d_att, recv["w_att_out"] = _matmul("att_out_dx", d_att_out, w_att_out, "nt", SEQ, ATT_W, D_MODEL,
                                       [F32], carry=(gw_att_out, 1, SHARD["w_att_out"]))

    d_ro, d_rg, d_gn_g, d_gn_b = _retpost_bwd(d_gated, ro, rg, gn_g, gn_b)
    d_rq, d_rk, d_rv = _retention_bwd(rqk, rv, states, d_ro, din, qd, kd, cd, cos, sin)

    d_att_b, dd = _att_bwd_pre(d_att, att)
    d_aqkv, dsbs = [], []
    for gi in range(3):
        da_p = _permute_rows(d_att_b, dils[gi])
        l_p = _permute_rows(lse, dils[gi])
        dd_p = _permute_rows(dd, dils[gi])
        dq, dk, dv, dsb = _att_bwd(gi, aqkv[gi], da_p, l_p, dd_p, bias, nbs[gi])
        d_aqkv.append(_unpermute_rows(jnp.concatenate([dq, dk, dv], axis=-1), dils[gi]))
        dsbs.append(dsb)
    d_rel_bias = _bias_grad(jnp.stack(dsbs), buckets)

    d_proj = jnp.concatenate([d_rq, d_rk, d_rv, d_rg] + d_aqkv + [d_gates], axis=-1)
    gw_in = _matmul_tn_pair("proj_dw", pos, h1, d_proj, D_MODEL, IN_COLS, SEQ, D_MODEL,
                            tm=512, tn=640, tk=2048)
    d_h1, recv["w_in"] = _matmul("proj_dx", d_proj, w_in, "nt", SEQ, D_MODEL, IN_COLS, [F32],
                                 tn=1024, tk=1280, carry=(gw_in, 1, SHARD["w_in"]))

    grad_x, d_sh1, d_sc1, d_n1g = _prenorm_bwd("prenorm1_bwd", [d_h1], x, norm1_g, sc1, d_x2)
    d_mod = jnp.concatenate([d_sh1, d_sc1, d_g1, d_sh2, d_sc2, d_g2], axis=-1)
    small = dict(norm1_g=d_n1g, norm2_g=d_n2g, norm_f_g=d_gf, gn_g=d_gn_g, gn_b=d_gn_b,
                 rel_bias=d_rel_bias)
    big = dict(w_in=gw_in, w_ret_out=gw_ret_out, w_att_out=gw_att_out, w_o=gw_o, w_ff1=gw_ff1,
               w_ff2=gw_ff2)
    return loss, grad_x, d_mod, small, big, recv


def _me():
    return lax.axis_index("x"), lax.axis_index("y"), lax.axis_index("c")


def _peer(x, y, c, mask):
    return (x ^ ((mask >> 2) & 1), y ^ ((mask >> 1) & 1), c ^ (mask & 1))


def _gather8(src_ref, dst_ref, send_sems, recv_sems):
    x, y, c = _me()
    me = 4 * x + 2 * y + c
    copies = []
    for mask in range(1, N_DEV):
        cp = pltpu.make_async_remote_copy(
            src_ref=src_ref, dst_ref=dst_ref.at[me], send_sem=send_sems.at[mask - 1],
            recv_sem=recv_sems.at[mask - 1], device_id=_peer(x, y, c, mask), device_id_type=MESH)
        cp.start()
        copies.append(cp)
    dst_ref[me] = src_ref[...]
    for cp in copies:
        cp.wait_recv()
    for cp in copies:
        cp.wait_send()


def _ada_fwd(c_in, w_ada, b_ada):
    ncol = ADA_COLS // N_CHIPS

    def body(c_ref, w_ref, b_ref, mod_ref, sc_ref, cbuf, cg, mbuf, mg, s1, r1, s2, r2):
        x, y, c = _me()
        me = 4 * x + 2 * y + c
        cv = c_ref[...]
        cbuf[...] = jnp.broadcast_to(cv * _sigmoid(cv), cbuf.shape)
        _gather8(cbuf, cg, s1, r1)
        rows = lax.broadcasted_iota(I32, (N_DEV, D_MODEL), 0)
        sc_all = jnp.zeros((N_DEV, D_MODEL), F32)
        for d in range(N_DEV):
            sc_all = jnp.where(rows == d, cg[d], sc_all)
        sc_ref[...] = sc_all
        mbuf[...] = jnp.dot(sc_all.astype(BF16), w_ref[...].astype(BF16), preferred_element_type=F32)
        _gather8(mbuf, mg, s2, r2)
        rowsel = lax.broadcasted_iota(I32, (N_DEV, ncol), 0) == me
        for k in range(N_CHIPS):
            blk = mg[2 * k]
            row = jnp.sum(jnp.where(rowsel, blk, 0.0), axis=0, keepdims=True)
            mod_ref[:, k * ncol:(k + 1) * ncol] = row + b_ref[:, k * ncol:(k + 1) * ncol]

    vm = pl.BlockSpec(memory_space=pltpu.VMEM)
    return pl.pallas_call(
        body, name="ada_fwd",
        in_specs=[vm, vm, vm], out_specs=[vm, vm],
        out_shape=[jax.ShapeDtypeStruct((1, ADA_COLS), F32), jax.ShapeDtypeStruct((N_DEV, D_MODEL), F32)],
        scratch_shapes=[
            pltpu.VMEM((8, D_MODEL), F32), pltpu.VMEM((N_DEV, 8, D_MODEL), F32),
            pltpu.VMEM((8, ncol), F32), pltpu.VMEM((N_DEV, 8, ncol), F32),
            pltpu.SemaphoreType.DMA((N_DEV - 1,)), pltpu.SemaphoreType.DMA((N_DEV - 1,)),
            pltpu.SemaphoreType.DMA((N_DEV - 1,)), pltpu.SemaphoreType.DMA((N_DEV - 1,)),
        ],
        compiler_params=pltpu.CompilerParams(vmem_limit_bytes=VMEM_LIMIT_V7X),
    )(c_in, w_ada, b_ada)


def _small_reduce(pack, sc_all):
    ncol = ADA_COLS // N_CHIPS

    def body(p_ref, sc_ref, tot_ref, gw_ref, pg, s1, r1):
        x, y, _ = _me()
        chip = 2 * x + y
        _gather8(p_ref, pg, s1, r1)
        tot = pg[0]
        for d in range(1, N_DEV):
            tot = tot + pg[d]
        tot_ref[...] = tot
        rows = lax.broadcasted_iota(I32, (N_DEV, ncol), 0)
        dmod = jnp.zeros((N_DEV, ncol), F32)
        for k in range(N_CHIPS):
            part = jnp.zeros((N_DEV, ncol), F32)
            for d in range(N_DEV):
                part = jnp.where(rows == d, pg[d, :, k * ncol:(k + 1) * ncol][0:1, :], part)
            dmod = jnp.where(chip == k, part, dmod)
        gw_ref[...] = lax.dot_general(sc_ref[...].astype(BF16), dmod.astype(BF16), _TN,
                                      preferred_element_type=F32)

    vm = pl.BlockSpec(memory_space=pltpu.VMEM)
    return pl.pallas_call(
        body, name="small_reduce",
        in_specs=[vm, vm], out_specs=[vm, vm],
        out_shape=[jax.ShapeDtypeStruct((8, ADA_COLS), F32), jax.ShapeDtypeStruct((D_MODEL, ncol), F32)],
        scratch_shapes=[pltpu.VMEM((N_DEV, 8, ADA_COLS), F32),
                        pltpu.SemaphoreType.DMA((N_DEV - 1,)), pltpu.SemaphoreType.DMA((N_DEV - 1,))],
        compiler_params=pltpu.CompilerParams(vmem_limit_bytes=VMEM_LIMIT_V7X),
    )(pack, sc_all)


BIG = (("w_in", 1), ("w_ret_out", 0), ("w_att_out", 1), ("w_o", 0), ("w_ff1", 1), ("w_ff2", 0))
SHARD = {"w_in": (D_MODEL, IN_COLS // N_CHIPS), "w_ret_out": (RET_V_W // N_CHIPS, D_MODEL),
         "w_att_out": (ATT_W, D_MODEL // N_CHIPS), "w_o": (D_MODEL // N_CHIPS, D_MODEL),
         "w_ff1": (D_MODEL, D_FF // N_CHIPS), "w_ff2": (D_FF // N_CHIPS, D_MODEL)}
_CHIP_FLIPS = ((1, 0), (0, 1), (1, 1))


def _region(ref, axis, chip, half, shard_shape):
    r, cw = shard_shape
    hr = r // 2
    if axis == 1:
        return ref.at[pl.ds(half * hr, hr), pl.ds(chip * cw, cw)]
    return ref.at[pl.ds(chip * r + half * hr, hr), :]


def _gather_weights(shards):
    nw = len(BIG)
    shapes = [s.shape for s in shards]
    full_shapes = [(r, N_CHIPS * cw) if ax == 1 else (N_CHIPS * r, cw)
                   for (r, cw), (_, ax) in zip(shapes, BIG)]

    def body(*refs):
        ins, outs = refs[:nw], refs[nw:2 * nw]
        own, from_ici, from_sib = refs[2 * nw:3 * nw], refs[3 * nw:4 * nw], refs[4 * nw:5 * nw]
        ld_sem, st_sem, s_ici, r_ici, s_d2d, r_d2d, st_a, st_b = refs[5 * nw:]
        x, y, c = _me()
        chip = 2 * x + y
        sib = (x, y, 1 - c)
        loads = [pltpu.make_async_copy(ins[i], own[i], ld_sem.at[i]) for i in range(nw)]
        for cp in loads:
            cp.start()
        pending, first = [], []
        for i, (_, ax) in enumerate(BIG):
            r, cw = shapes[i]
            hr = r // 2
            loads[i].wait()
            dst = outs[i].at[:, pl.ds(chip * cw, cw)] if ax == 1 else outs[i].at[pl.ds(chip * r, r), :]
            cp = pltpu.make_async_copy(own[i], dst, st_sem.at[i])
            cp.start()
            pending.append(cp)
            for j, (fx, fy) in enumerate(_CHIP_FLIPS):
                rc = pltpu.make_async_remote_copy(
                    src_ref=own[i].at[pl.ds(c * hr, hr), :], dst_ref=from_ici[i].at[j],
                    send_sem=s_ici.at[j * nw + i], recv_sem=r_ici.at[j * nw + i],
                    device_id=(x ^ fx, y ^ fy, c), device_id_type=MESH)
                rc.start()
                first.append((j, i, rc))
        passed = []
        for j, i, rc in first:
            fx, fy = _CHIP_FLIPS[j]
            src_chip = 2 * (x ^ fx) + (y ^ fy)
            ax = BIG[i][1]
            rc.wait_recv()
            fw = pltpu.make_async_remote_copy(
                src_ref=from_ici[i].at[j], dst_ref=from_sib[i].at[j], send_sem=s_d2d.at[j * nw + i],
                recv_sem=r_d2d.at[j * nw + i], device_id=sib, device_id_type=MESH)
            fw.start()
            passed.append((j, i, src_chip, fw))
            st = pltpu.make_async_copy(from_ici[i].at[j], _region(outs[i], ax, src_chip, c, shapes[i]),
                                       st_a.at[j * nw + i])
            st.start()
            pending.append(st)
        for j, i, src_chip, fw in passed:
            fw.wait_recv()
            st = pltpu.make_async_copy(from_sib[i].at[j],
                                       _region(outs[i], BIG[i][1], src_chip, 1 - c, shapes[i]),
                                       st_b.at[j * nw + i])
            st.start()
            pending.append(st)
        for _, _, rc in first:
            rc.wait_send()
        for _, _, _, fw in passed:
            fw.wait_send()
        for cp in pending:
            cp.wait()

    hbm = pl.BlockSpec(memory_space=pl.ANY)
    halves = [pltpu.VMEM((3, r // 2, cw), BF16) for r, cw in shapes]
    return pl.pallas_call(
        body, name="gather_weights",
        in_specs=[hbm] * nw, out_specs=[hbm] * nw,
        out_shape=[jax.ShapeDtypeStruct(fs, BF16) for fs in full_shapes],
        scratch_shapes=[pltpu.VMEM(sh, BF16) for sh in shapes] + halves + halves
        + [pltpu.SemaphoreType.DMA((nw,)), pltpu.SemaphoreType.DMA((nw,))]
        + [pltpu.SemaphoreType.DMA((3 * nw,))] * 6,
        compiler_params=pltpu.CompilerParams(vmem_limit_bytes=VMEM_LIMIT_V7X),
    )(*shards)


def _adam_update(w, g, m, v):
    mn = ADAM_B1 * m + (1.0 - ADAM_B1) * g
    vn = ADAM_B2 * v + (1.0 - ADAM_B2) * (g * g)
    m_hat = mn / (1.0 - ADAM_B1 ** ADAM_STEP)
    v_hat = vn / (1.0 - ADAM_B2 ** ADAM_STEP)
    return -ADAM_LR * (m_hat / (jnp.sqrt(v_hat) + ADAM_EPS) + ADAM_WD * w), mn, vn


def _final_update(name, pos, axis, psum, recv, w, m, v, tr=64):
    r, cw = w.shape
    hr = r // 2
    tr = min(tr, hr)
    nt = hr // tr

    def kern(pos_ref, p_ref, r_ref, w_ref, m_ref, v_ref, g_ref, d_ref, nm_ref, nv_ref,
             send_buf, land_buf, s_sem, r_sem):
        p, t = pl.program_id(0), pl.program_id(1)
        sib = _sibling()

        def copy(i):
            return pltpu.make_async_remote_copy(
                src_ref=send_buf.at[i], dst_ref=land_buf.at[i], send_sem=s_sem.at[i],
                recv_sem=r_sem.at[i], device_id=sib, device_id_type=MESH)

        def update(tot):
            g_ref[...] = tot
            d_ref[...], nm_ref[...], nv_ref[...] = _adam_update(w_ref[...], tot, m_ref[...], v_ref[...])

        @pl.when(p == 0)
        def _():
            tot = p_ref[...].astype(F32)
            for j in range(3):
                tot = tot + r_ref[j].astype(F32)
            send_buf[t] = tot
            copy(t).start()
            update(tot)

        @pl.when(p == 1)
        def _():
            copy(t).wait_recv()
            update(land_buf[t])

        @pl.when(jnp.logical_and(p == 1, t == nt - 1))
        def _():
            for i in range(nt):
                copy(i).wait_send()

    def shard_rows(p, t, pos_ref):
        return (jnp.where(p == 0, pos_ref[0], 1 - pos_ref[0]) * nt + t, 0)

    def own_part(p, t, pos_ref):
        tt = jnp.where(p == 0, t, nt - 1)
        return (tt, pos_ref[1]) if axis == 1 else (pos_ref[1] * nt + tt, 0)

    shard_spec = pl.BlockSpec((tr, cw), shard_rows)
    grid_spec = pltpu.PrefetchScalarGridSpec(
        num_scalar_prefetch=1, grid=(2, nt),
        in_specs=[pl.BlockSpec((tr, cw), own_part),
                  pl.BlockSpec((3, tr, cw), lambda p, t, pos_ref: (0, jnp.where(p == 0, t, nt - 1), 0)),
                  shard_spec, shard_spec, shard_spec],
        out_specs=[shard_spec] * 4,
        scratch_shapes=[pltpu.VMEM((nt, tr, cw), F32), pltpu.VMEM((nt, tr, cw), F32),
                        pltpu.SemaphoreType.DMA((nt,)), pltpu.SemaphoreType.DMA((nt,))])
    return pl.pallas_call(
        kern, name=name, grid_spec=grid_spec, out_shape=[jax.ShapeDtypeStruct((r, cw), F32)] * 4,
        compiler_params=_cparams(("arbitrary", "arbitrary")),
    )(pos, psum, recv, w, m, v)


def _adamw(name, w, g, m, v):
    r, cw = w.shape
    tr = min(r, 128)

    def kern(w_ref, g_ref, m_ref, v_ref, d_ref, nm_ref, nv_ref):
        d_ref[...], nm_ref[...], nv_ref[...] = _adam_update(w_ref[...], g_ref[...], m_ref[...], v_ref[...])

    spec = pl.BlockSpec((tr, cw), lambda i: (i, 0))
    return pl.pallas_call(
        kern, name=name, grid=(r // tr,), in_specs=[spec] * 4, out_specs=[spec] * 3,
        out_shape=[jax.ShapeDtypeStruct((r, cw), F32)] * 3, compiler_params=_cparams(("parallel",)),
    )(w, g, m, v)


_PACK_W = ADA_COLS
_NB = REL_BUCKETS * N_ATT_HEADS
_SMALL_SLOTS = {
    "b_ada": (0, 0, ADA_COLS),
    "norm1_g": (1, 0, D_MODEL), "norm2_g": (1, D_MODEL, D_MODEL), "norm_f_g": (1, 2 * D_MODEL, D_MODEL),
    "ret_gn_g": (1, 3 * D_MODEL, RET_V_W),
    "ret_gn_b": (2, 0, RET_V_W), "rel_bias": (2, RET_V_W, _NB), "loss": (2, RET_V_W + 512, 128),
}


def _pack_small(vals):
    rows = []
    for r in range(8):
        items = sorted([(off, n) for n, (rr, off, _) in _SMALL_SLOTS.items() if rr == r and n in vals])
        parts, pos = [], 0
        for off, n in items:
            if off > pos:
                parts.append(jnp.zeros((1, off - pos), F32))
            parts.append(vals[n].reshape(1, -1).astype(F32))
            pos = off + _SMALL_SLOTS[n][2]
        if pos < _PACK_W:
            parts.append(jnp.zeros((1, _PACK_W - pos), F32))
        rows.append(jnp.concatenate(parts, axis=-1))
    return jnp.concatenate(rows, axis=0)


def _unpack_small(pack, name):
    r, off, wd = _SMALL_SLOTS[name]
    return pack[r:r + 1, off:off + wd]


def kernel(x, c, w_ada, b_ada, norm1_g, w_in, rel_bias, ret_gn_g, ret_gn_b, w_ret_out, w_att_out, w_o, norm2_g, w_ff1, w_ff2, norm_f_g, loss_target, m_w_ada, m_b_ada, m_norm1_g, m_w_in, m_rel_bias, m_ret_gn_g, m_ret_gn_b, m_w_ret_out, m_w_att_out, m_w_o, m_norm2_g, m_w_ff1, m_w_ff2, m_norm_f_g, v_w_ada, v_b_ada, v_norm1_g, v_w_in, v_rel_bias, v_ret_gn_g, v_ret_gn_b, v_w_ret_out, v_w_att_out, v_w_o, v_norm2_g, v_w_ff1, v_w_ff2, v_norm_f_g):
    given = dict(locals())
    big_names = [n for n, _ in BIG]
    shard_w = {n: given[n][0] for n in big_names}
    assert all(shard_w[n].shape == SHARD[n] for n in big_names)

    full = _gather_weights([shard_w[n].astype(BF16) for n in big_names])
    full = dict(zip(big_names, full))
    mod, sc_all = _ada_fwd(c, w_ada[0], b_ada)
    pos = _where_am_i()

    loss, grad_x, d_mod, small, big, recv = _local_step(
        pos, x[0], loss_target[0], mod, norm1_g, norm2_g, norm_f_g.reshape(1, -1), rel_bias, ret_gn_g,
        ret_gn_b, full["w_in"], full["w_ret_out"], full["w_att_out"], full["w_o"], full["w_ff1"],
        full["w_ff2"])

    pack_g = _pack_small(dict(b_ada=d_mod, norm1_g=small["norm1_g"], norm2_g=small["norm2_g"],
                              norm_f_g=small["norm_f_g"], ret_gn_g=small["gn_g"], ret_gn_b=small["gn_b"],
                              rel_bias=small["rel_bias"], loss=loss))
    tot, g_w_ada = _small_reduce(pack_g, sc_all)

    small_names = ["b_ada", "norm1_g", "rel_bias", "ret_gn_g", "ret_gn_b", "norm2_g", "norm_f_g"]
    pack_w = _pack_small({n: given[n] for n in small_names})
    pack_m = _pack_small({n: given["m_" + n] for n in small_names})
    pack_v = _pack_small({n: given["v_" + n] for n in small_names})
    sd, sm, sv = _adamw("adamw_small", pack_w, tot, pack_m, pack_v)

    grads, deltas, new_m, new_v = {}, {}, {}, {}
    for n in small_names:
        shp = given[n].shape
        grads[n] = _unpack_small(tot, n).reshape(shp)
        deltas[n] = _unpack_small(sd, n).reshape(shp)
        new_m[n] = _unpack_small(sm, n).reshape(shp)
        new_v[n] = _unpack_small(sv, n).reshape(shp)
    d, nm, nv = _adamw("adamw_w_ada", w_ada[0], g_w_ada, m_w_ada[0], v_w_ada[0])
    grads["w_ada"], deltas["w_ada"], new_m["w_ada"], new_v["w_ada"] = g_w_ada[None], d[None], nm[None], nv[None]
    for n, ax in BIG:
        g, d, nm, nv = _final_update("final_" + n, pos, ax, big[n], recv[n], given[n][0],
                                     given["m_" + n][0], given["v_" + n][0])
        grads[n], deltas[n], new_m[n], new_v[n] = g[None], d[None], nm[None], nv[None]

    order = ["w_ada", "b_ada", "norm1_g", "w_in", "rel_bias", "ret_gn_g", "ret_gn_b", "w_ret_out",
             "w_att_out", "w_o", "norm2_g", "w_ff1", "w_ff2", "norm_f_g"]
    loss_out = _unpack_small(tot, "loss")[0, 0]
    return (loss_out, grad_x[None], *[grads[n] for n in order], *[deltas[n] for n in order],
            *[new_m[n] for n in order], *[new_v[n] for n in order])
```

```python
import functools
import math

import jax
import jax.numpy as jnp
from jax import lax
from jax.experimental import pallas as pl
from jax.experimental.pallas import tpu as pltpu

F32 = jnp.float32
BF16 = jnp.bfloat16
I32 = jnp.int32

SEQ = 2048
D_MODEL = 1024
RET_HEADS = 4
RET_DK = 256
RET_DV = 512
RET_CHUNK = 128
RET_QK_W = RET_HEADS * RET_DK
RET_V_W = RET_HEADS * RET_DV
ATT_GROUPS = ((128, 1), (512, 4), (2048, 16))
ATT_HPG = 4
ATT_DH = 128
ATT_W = ATT_HPG * ATT_DH
ATT_BLK = 128
N_BLK = SEQ // ATT_BLK
REL_BUCKETS = 32
REL_MAX_DIST = 2048
N_ATT_HEADS = 12
D_FF = 4 * D_MODEL
RMS_EPS = 1e-6
GN_EPS = 1e-5
ROPE_BASE = 10000.0
IN_COLS = 2 * RET_QK_W + 2 * RET_V_W + 9 * ATT_W + 2 * D_MODEL
OFF_Q, OFF_K, OFF_V, OFF_G = 0, RET_QK_W, 2 * RET_QK_W, 2 * RET_QK_W + RET_V_W
OFF_ATT = 2 * RET_QK_W + 2 * RET_V_W
OFF_GATE = OFF_ATT + 9 * ATT_W
N_CHIPS = 4
N_DEV = 8
ADA_COLS = 6 * D_MODEL

ADAM_LR = 0.001
ADAM_B1 = 0.9
ADAM_B2 = 0.999
ADAM_EPS = 1e-08
ADAM_WD = 0.01
ADAM_STEP = 10

VMEM_LIMIT_V7X = 56 * 1024 * 1024
MESH = pl.DeviceIdType.MESH


def _cparams(sem):
    return pltpu.CompilerParams(dimension_semantics=sem, vmem_limit_bytes=VMEM_LIMIT_V7X)


def _sigmoid(v):
    return 1.0 / (1.0 + jnp.exp(-v))


def _rowmap(name, body, row_ins, bcast_ins, row_outs, sum_outs=(), tm=256):
    m = row_ins[0].shape[0]
    n_in = len(row_ins) + len(bcast_ins)
    n_ro = len(row_outs)

    def kern(*refs):
        vals = [r[...] for r in refs[:n_in]]
        res = body(*vals)
        if not isinstance(res, (tuple, list)):
            res = (res,)
        outs = refs[n_in:]
        for r, v in zip(outs[:n_ro], res[:n_ro]):
            r[...] = v.astype(r.dtype)
        if sum_outs:
            @pl.when(pl.program_id(0) == 0)
            def _():
                for r in outs[n_ro:]:
                    r[...] = jnp.zeros_like(r)
            for r, v in zip(outs[n_ro:], res[n_ro:]):
                r[...] += v

    in_specs = [pl.BlockSpec((tm, a.shape[1]), lambda i: (i, 0)) for a in row_ins]
    in_specs += [pl.BlockSpec(a.shape, lambda i: (0, 0)) for a in bcast_ins]
    out_specs = [pl.BlockSpec((tm, n), lambda i: (i, 0)) for n, _ in row_outs]
    out_specs += [pl.BlockSpec((1, n), lambda i: (0, 0)) for n in sum_outs]
    out_shape = [jax.ShapeDtypeStruct((m, n), dt) for n, dt in row_outs]
    out_shape += [jax.ShapeDtypeStruct((1, n), F32) for n in sum_outs]
    return pl.pallas_call(
        kern, name=name, grid=(m // tm,), in_specs=in_specs, out_specs=out_specs,
        out_shape=out_shape, compiler_params=_cparams(("arbitrary",)),
    )(*row_ins, *bcast_ins)


TM, TN = 1024, 1024


def _matmul(name, a, b, kind, m, n, k, outs, *, b_off=0, tm=TM, tn=TN, tk=1024,
            epilogue=None, extras=(), carry=None):
    tm, tn, tk = min(tm, m), min(tn, n), min(tk, k)
    nk = k // tk
    if kind == "nn":
        a_spec = pl.BlockSpec((tm, tk), lambda i, j, kk: (i, kk))
        b_spec = pl.BlockSpec((tk, tn), lambda i, j, kk: (kk, b_off // tn + j))
        dn = (((1,), (0,)), ((), ()))
    elif kind == "nt":
        a_spec = pl.BlockSpec((tm, tk), lambda i, j, kk: (i, kk))
        b_spec = pl.BlockSpec((tn, tk), lambda i, j, kk: (j, b_off // tk + kk))
        dn = (((1,), (1,)), ((), ()))
    else:
        a_spec = pl.BlockSpec((tk, tm), lambda i, j, kk: (kk, i))
        b_spec = pl.BlockSpec((tk, tn), lambda i, j, kk: (kk, j))
        dn = (((0,), (0,)), ((), ()))
    n_ex, n_out = len(extras), len(outs)
    if epilogue is None:
        epilogue = lambda acc: (acc,)

    def finish(acc, ex_refs, out_refs):
        res = epilogue(acc, *[r[...] for r in ex_refs])
        for r, v in zip(out_refs, res):
            r[...] = v.astype(r.dtype)

    n_c = 0 if carry is None else 1
    ni, nj = m // tm, n // tn

    def kern(a_ref, b_ref, *rest):
        ex_refs = rest[:n_ex]
        out_refs = rest[n_ex + n_c:n_ex + n_c + n_out]
        scratch = rest[n_ex + 2 * n_c + n_out:]
        i, j, kk = pl.program_id(0), pl.program_id(1), pl.program_id(2)
        if carry is not None:
            copies = lambda: _ici_copies(rest[n_ex], rest[n_ex + n_c + n_out], scratch[-2], scratch[-1],
                                         carry[1], carry[2])

            @pl.when(jnp.logical_and(jnp.logical_and(i == 0, j == 0), kk == 0))
            def _():
                for cp in copies():
                    cp.start()

        part = lax.dot_general(a_ref[...], b_ref[...], dn, preferred_element_type=F32)
        if nk == 1:
            finish(part, ex_refs, out_refs)
        else:
            acc_ref = scratch[0]

            @pl.when(kk == 0)
            def _():
                acc_ref[...] = part

            @pl.when(kk > 0)
            def _():
                acc_ref[...] += part

            @pl.when(kk == nk - 1)
            def _():
                finish(acc_ref[...], ex_refs, out_refs)

        if carry is not None:
            @pl.when(jnp.logical_and(jnp.logical_and(i == ni - 1, j == nj - 1), kk == nk - 1))
            def _():
                for cp in copies():
                    cp.wait_recv()
                for cp in copies():
                    cp.wait_send()

    hbm = pl.BlockSpec(memory_space=pl.ANY)
    in_specs = [a_spec, b_spec] + [pl.BlockSpec(bs, im) for _, bs, im in extras] + [hbm] * n_c
    out_specs = [pl.BlockSpec((tm, tn), lambda i, j, kk: (i, j)) for _ in outs] + [hbm] * n_c
    out_shape = [jax.ShapeDtypeStruct((m, n), dt) for dt in outs]
    scratch_shapes = [] if nk == 1 else [pltpu.VMEM((tm, tn), F32)]
    operands = [a, b] + [e[0] for e in extras]
    if carry is not None:
        r, cw = carry[2]
        out_shape.append(jax.ShapeDtypeStruct((3, r // 2, cw), BF16))
        scratch_shapes += [pltpu.SemaphoreType.DMA((3,)), pltpu.SemaphoreType.DMA((3,))]
        operands.append(carry[0])
    sem = ("arbitrary",) * 3 if carry is not None else ("parallel", "parallel", "arbitrary")
    return pl.pallas_call(
        kern, name=name, grid=(ni, nj, nk), in_specs=in_specs, out_specs=out_specs,
        out_shape=out_shape, scratch_shapes=scratch_shapes, compiler_params=_cparams(sem),
    )(*operands)


def _ici_copies(psum_ref, recv_ref, s_sem, r_sem, axis, shard_shape):
    x, y, c = _me()
    hr, cw = shard_shape[0] // 2, shard_shape[1]
    pick = lambda sems, j: sems[j] if isinstance(sems, (list, tuple)) else sems.at[j]
    copies = []
    for j, (fx, fy) in enumerate(_CHIP_FLIPS):
        chip = 2 * (x ^ fx) + (y ^ fy)
        src = psum_ref.at[:, pl.ds(chip * cw, cw)] if axis == 1 else psum_ref.at[pl.ds(chip * hr, hr), :]
        copies.append(pltpu.make_async_remote_copy(
            src_ref=src, dst_ref=recv_ref.at[j], send_sem=pick(s_sem, j), recv_sem=pick(r_sem, j),
            device_id=(x ^ fx, y ^ fy, c), device_id_type=MESH))
    return copies


_HBM_SPEC = pl.BlockSpec(memory_space=pltpu.HBM)
_SEM_SPEC = pl.BlockSpec(memory_space=pltpu.SEMAPHORE)


def _ici_start(name, psum, axis, shard_shape):
    r, cw = shard_shape
    land = lax.empty((3, r // 2, cw), BF16)

    def body(p_ref, land_ref, s0, s1, s2, r0, r1, r2, p_thru, land_thru, token):
        for cp in _ici_copies(p_ref, land_ref, [s0, s1, s2], [r0, r1, r2], axis, shard_shape):
            cp.start()
        token[...] = jnp.zeros_like(token)

    sem = pltpu.SemaphoreType.DMA(())
    res = pl.pallas_call(
        body, name=name,
        out_shape=(sem,) * 6 + (pltpu.HBM(psum.shape, BF16), pltpu.HBM(land.shape, BF16),
                                jax.ShapeDtypeStruct((8, 128), F32)),
        in_specs=(_HBM_SPEC, _HBM_SPEC),
        out_specs=(_SEM_SPEC,) * 6 + (_HBM_SPEC, _HBM_SPEC, pl.BlockSpec(memory_space=pltpu.VMEM)),
        input_output_aliases={0: 6, 1: 7},
        compiler_params=pltpu.CompilerParams(has_side_effects=pltpu.SideEffectType.DATAFLOW_SIDE_EFFECTING),
    )(pltpu.with_memory_space_constraint(psum, pltpu.HBM),
      pltpu.with_memory_space_constraint(land, pltpu.HBM))
    return res[:6], res[6], res[7], res[8]


def _ici_wait(name, sems, p_thru, land_thru, axis, shard_shape, after):
    n_after = len(after)

    def body(p_ref, land_ref, s0, s1, s2, r0, r1, r2, *rest):
        for cp in _ici_copies(p_ref, land_ref, [s0, s1, s2], [r0, r1, r2], axis, shard_shape):
            cp.wait_send()
            cp.wait_recv()

    return pl.pallas_call(
        body, name=name,
        out_shape=(pltpu.HBM(p_thru.shape, BF16), pltpu.HBM(land_thru.shape, BF16)),
        in_specs=(_HBM_SPEC, _HBM_SPEC) + (_SEM_SPEC,) * 6 + (pl.BlockSpec(memory_space=pl.ANY),) * n_after,
        out_specs=(_HBM_SPEC, _HBM_SPEC), input_output_aliases={0: 0, 1: 1},
        compiler_params=pltpu.CompilerParams(has_side_effects=pltpu.SideEffectType.DATAFLOW_SIDE_EFFECTING),
    )(p_thru, land_thru, *sems, *after)


def _where_am_i():
    x, y, c = _me()
    return jnp.stack([c, 2 * x + y]).astype(I32)


def _sibling():
    x, y, c = _me()
    return (x, y, 1 - c)


def _matmul_tn_pair(name, pos, a, b, m, n, k, shard_rows, *, tm, tn, tk):
    hr = shard_rows // 2
    tm, tn, tk = min(tm, hr), min(tn, n), min(tk, k)
    tph = hr // tm
    nt, nj, nk = (m // 2) // tm, n // tn, k // tk
    n_tiles = nt * nj

    def row_block(p, t, pos_ref):
        half = jnp.where(p == 0, 1 - pos_ref[0], pos_ref[0])
        return (t // tph) * (2 * tph) + half * tph + t % tph

    def kern(pos_ref, a_ref, b_ref, o_ref, acc_ref, send_buf, land_buf, s_sem, r_sem):
        p, t, j, kk = pl.program_id(0), pl.program_id(1), pl.program_id(2), pl.program_id(3)
        idx = t * nj + j
        sib = _sibling()

        def copy(i):
            return pltpu.make_async_remote_copy(
                src_ref=send_buf.at[i], dst_ref=land_buf.at[i], send_sem=s_sem.at[i],
                recv_sem=r_sem.at[i], device_id=sib, device_id_type=MESH)

        part = lax.dot_general(a_ref[...], b_ref[...], _TN, preferred_element_type=F32)

        @pl.when(kk == 0)
        def _():
            acc_ref[...] = part

        @pl.when(kk > 0)
        def _():
            acc_ref[...] += part

        @pl.when(jnp.logical_and(kk == nk - 1, p == 0))
        def _():
            send_buf[idx] = acc_ref[...].astype(BF16)
            copy(idx).start()

        @pl.when(jnp.logical_and(kk == nk - 1, p == 1))
        def _():
            copy(idx).wait_recv()
            o_ref[...] = (acc_ref[...] + land_buf[idx].astype(F32)).astype(BF16)

        @pl.when(jnp.logical_and(jnp.logical_and(p == 1, idx == n_tiles - 1), kk == nk - 1))
        def _():
            for i in range(n_tiles):
                copy(i).wait_send()

    grid_spec = pltpu.PrefetchScalarGridSpec(
        num_scalar_prefetch=1, grid=(2, nt, nj, nk),
        in_specs=[pl.BlockSpec((tk, tm), lambda p, t, j, kk, pos_ref: (kk, row_block(p, t, pos_ref))),
                  pl.BlockSpec((tk, tn), lambda p, t, j, kk, pos_ref: (kk, j))],
        out_specs=pl.BlockSpec((tm, tn), lambda p, t, j, kk, pos_ref: (p * t, p * j)),
        scratch_shapes=[pltpu.VMEM((tm, tn), F32), pltpu.VMEM((n_tiles, tm, tn), BF16),
                        pltpu.VMEM((n_tiles, tm, tn), BF16),
                        pltpu.SemaphoreType.DMA((n_tiles,)), pltpu.SemaphoreType.DMA((n_tiles,))])
    return pl.pallas_call(
        kern, name=name, grid_spec=grid_spec, out_shape=jax.ShapeDtypeStruct((m // 2, n), BF16),
        compiler_params=_cparams(("arbitrary",) * 4),
    )(pos, a, b)


def _rope_tables():
    half = RET_DK // 2
    inv = ROPE_BASE ** (-jnp.arange(half, dtype=F32) / half)
    ang = jnp.arange(SEQ).astype(F32)[:, None] * inv[None, :]
    return jnp.cos(ang), jnp.sin(ang)


def _decay_tables():
    c = RET_CHUNK
    log_g = jnp.log1p(-(2.0 ** (-5.0 - jnp.arange(RET_HEADS, dtype=F32))))
    idx = jnp.arange(c, dtype=F32)
    rel = idx[:, None] - idx[None, :]
    din = jnp.where(rel >= 0, jnp.exp(log_g[:, None, None] * jnp.maximum(rel, 0.0)), 0.0)
    qd = jnp.exp(log_g[:, None] * (idx + 1.0))[:, :, None]
    kd = jnp.exp(log_g[:, None] * (c - 1.0 - idx))[:, :, None]
    cd = jnp.exp(log_g * c)
    return din, qd, kd, cd


def _t5_bucket(dist):
    max_exact = REL_BUCKETS // 2
    d_f = jnp.maximum(dist, 1).astype(F32)
    large = max_exact + (jnp.log(d_f / max_exact) / math.log(REL_MAX_DIST / max_exact)
                         * (REL_BUCKETS - max_exact)).astype(I32)
    large = jnp.minimum(large, REL_BUCKETS - 1)
    return jnp.where(dist < max_exact, dist, large)


def _bucket_tables():
    qi = jnp.arange(ATT_BLK)[:, None]
    kj = jnp.arange(2 * ATT_BLK)[None, :]
    dist = jnp.clip(ATT_BLK + qi - kj, 0, ATT_BLK)
    return jnp.stack([_t5_bucket(dist * dil) for _, dil in ATT_GROUPS]).astype(I32)


def _permute_rows(t, dil):
    if dil == 1:
        return t
    s, w = t.shape
    return t.reshape(s // dil, dil, w).transpose(1, 0, 2).reshape(s, w)


def _unpermute_rows(t, dil):
    if dil == 1:
        return t
    s, w = t.shape
    return t.reshape(dil, s // dil, w).transpose(1, 0, 2).reshape(s, w)


def _retention_fwd(rqk, rv, din, qd, kd, cd):
    nc = SEQ // RET_CHUNK
    c, dk, dv = RET_CHUNK, RET_DK, RET_DV

    def kern(q_ref, k_ref, v_ref, din_ref, qd_ref, kd_ref, cd_ref, o_ref, st_ref, state):
        h, n = pl.program_id(0), pl.program_id(1)

        @pl.when(n == 0)
        def _():
            state[...] = jnp.zeros_like(state)

        q, k, v = q_ref[...], k_ref[...], v_ref[...]
        s_b = state[...].astype(BF16)
        st_ref[...] = s_b
        a = lax.dot_general(q, k, (((1,), (1,)), ((), ())), preferred_element_type=F32) * din_ref[...]
        o = jnp.dot(a.astype(BF16), v, preferred_element_type=F32)
        o += jnp.dot(q, s_b, preferred_element_type=F32) * qd_ref[...]
        o_ref[...] = o
        kk = (k.astype(F32) * kd_ref[...]).astype(BF16)
        upd = lax.dot_general(kk, v, (((0,), (0,)), ((), ())), preferred_element_type=F32)
        state[...] = state[...] * cd_ref[h] + upd

    return pl.pallas_call(
        kern, name="retention_fwd", grid=(RET_HEADS, nc),
        in_specs=[
            pl.BlockSpec((c, dk), lambda h, n: (n, h)),
            pl.BlockSpec((c, dk), lambda h, n: (n, RET_HEADS + h)),
            pl.BlockSpec((c, dv), lambda h, n: (n, h)),
            pl.BlockSpec((None, c, c), lambda h, n: (h, 0, 0)),
            pl.BlockSpec((None, c, 1), lambda h, n: (h, 0, 0)),
            pl.BlockSpec((None, c, 1), lambda h, n: (h, 0, 0)),
            pl.BlockSpec(memory_space=pltpu.SMEM),
        ],
        out_specs=[
            pl.BlockSpec((c, dv), lambda h, n: (n, h)),
            pl.BlockSpec((None, None, dk, dv), lambda h, n: (h, n, 0, 0)),
        ],
        out_shape=[
            jax.ShapeDtypeStruct((SEQ, RET_V_W), F32),
            jax.ShapeDtypeStruct((RET_HEADS, nc, dk, dv), BF16),
        ],
        scratch_shapes=[pltpu.VMEM((dk, dv), F32)],
        compiler_params=_cparams(("arbitrary", "arbitrary")),
    )(rqk, rqk, rv, din, qd, kd, cd)


def _retention_bwd(rqk, rv, states, d_ro, din, qd, kd, cd, cos, sin):
    nc = SEQ // RET_CHUNK
    c, dk, dv = RET_CHUNK, RET_DK, RET_DV
    half = dk // 2
    last = nc - 1

    def unrot(g, cs, sn):
        g1, g2 = g[:, :half], g[:, half:]
        return jnp.concatenate([g1 * cs + g2 * sn, g2 * cs - g1 * sn], axis=-1)

    def kern(q_ref, k_ref, v_ref, st_ref, do_ref, din_ref, qd_ref, kd_ref, cd_ref, cos_ref, sin_ref,
             dq_ref, dk_ref, dv_ref, dstate):
        h, step = pl.program_id(0), pl.program_id(1)

        @pl.when(step == 0)
        def _():
            dstate[...] = jnp.zeros_like(dstate)

        q, k, v, s_b = q_ref[...], k_ref[...], v_ref[...], st_ref[...]
        d_o = do_ref[...]
        d_ob = d_o.astype(BF16)
        d_oq = (d_o * qd_ref[...]).astype(BF16)
        ds_b = dstate[...].astype(BF16)
        din_m = din_ref[...]
        nt = (((1,), (1,)), ((), ()))
        tn = (((0,), (0,)), ((), ()))
        a_b = (lax.dot_general(q, k, nt, preferred_element_type=F32) * din_m).astype(BF16)
        kk = (k.astype(F32) * kd_ref[...]).astype(BF16)
        d_v = lax.dot_general(a_b, d_ob, tn, preferred_element_type=F32)
        d_v += jnp.dot(kk, ds_b, preferred_element_type=F32)
        d_a = (lax.dot_general(d_ob, v, nt, preferred_element_type=F32) * din_m).astype(BF16)
        d_q = jnp.dot(d_a, k, preferred_element_type=F32)
        d_q += lax.dot_general(d_oq, s_b, nt, preferred_element_type=F32)
        d_k = lax.dot_general(d_a, q, tn, preferred_element_type=F32)
        d_k += lax.dot_general(v, ds_b, nt, preferred_element_type=F32) * kd_ref[...]
        dstate[...] = dstate[...] * cd_ref[h] + lax.dot_general(q, d_oq, tn, preferred_element_type=F32)
        cs, sn = cos_ref[...], sin_ref[...]
        dq_ref[...] = unrot(d_q, cs, sn).astype(BF16)
        dk_ref[...] = (unrot(d_k, cs, sn) * (RET_DK ** -0.5)).astype(BF16)
        dv_ref[...] = d_v.astype(BF16)

    return pl.pallas_call(
        kern, name="retention_bwd", grid=(RET_HEADS, nc),
        in_specs=[
            pl.BlockSpec((c, dk), lambda h, n: (last - n, h)),
            pl.BlockSpec((c, dk), lambda h, n: (last - n, RET_HEADS + h)),
            pl.BlockSpec((c, dv), lambda h, n: (last - n, h)),
            pl.BlockSpec((None, None, dk, dv), lambda h, n: (h, last - n, 0, 0)),
            pl.BlockSpec((c, dv), lambda h, n: (last - n, h)),
            pl.BlockSpec((None, c, c), lambda h, n: (h, 0, 0)),
            pl.BlockSpec((None, c, 1), lambda h, n: (h, 0, 0)),
            pl.BlockSpec((None, c, 1), lambda h, n: (h, 0, 0)),
            pl.BlockSpec(memory_space=pltpu.SMEM),
            pl.BlockSpec((c, half), lambda h, n: (last - n, 0)),
            pl.BlockSpec((c, half), lambda h, n: (last - n, 0)),
        ],
        out_specs=[
            pl.BlockSpec((c, dk), lambda h, n: (last - n, h)),
            pl.BlockSpec((c, dk), lambda h, n: (last - n, h)),
            pl.BlockSpec((c, dv), lambda h, n: (last - n, h)),
        ],
        out_shape=[
            jax.ShapeDtypeStruct((SEQ, RET_QK_W), BF16),
            jax.ShapeDtypeStruct((SEQ, RET_QK_W), BF16),
            jax.ShapeDtypeStruct((SEQ, RET_V_W), BF16),
        ],
        scratch_shapes=[pltpu.VMEM((dk, dv), F32)],
        compiler_params=_cparams(("arbitrary", "arbitrary")),
    )(rqk, rqk, rv, states, d_ro, din, qd, kd, cd, cos, sin)


def _bias_build(rel_bias, buckets):
    ng = len(ATT_GROUPS)

    def kern(tab_ref, bkt_ref, o_ref):
        g, h = pl.program_id(0), pl.program_id(1)
        bkt = bkt_ref[...]
        acc = jnp.zeros(bkt.shape, F32)
        for b in range(REL_BUCKETS):
            acc = jnp.where(bkt == b, tab_ref[b, g * ATT_HPG + h], acc)
        o_ref[...] = acc

    return pl.pallas_call(
        kern, name="bias_build", grid=(ng, ATT_HPG),
        in_specs=[pl.BlockSpec(memory_space=pltpu.SMEM),
                  pl.BlockSpec((None, ATT_BLK, 2 * ATT_BLK), lambda g, h: (g, 0, 0))],
        out_specs=pl.BlockSpec((None, None, ATT_BLK, 2 * ATT_BLK), lambda g, h: (g, h, 0, 0)),
        out_shape=jax.ShapeDtypeStruct((ng, ATT_HPG, ATT_BLK, 2 * ATT_BLK), F32),
        compiler_params=_cparams(("arbitrary", "arbitrary")),
    )(rel_bias, buckets)


def _bias_grad(dsb, buckets):
    ng = len(ATT_GROUPS)

    def kern(ds_ref, bkt_ref, o_ref):
        g, h = pl.program_id(0), pl.program_id(1)
        bkt, ds = bkt_ref[...], ds_ref[...]
        for b in range(REL_BUCKETS):
            o_ref[b, g * ATT_HPG + h] = jnp.sum(jnp.where(bkt == b, ds, 0.0))

    return pl.pallas_call(
        kern, name="bias_grad", grid=(ng, ATT_HPG),
        in_specs=[pl.BlockSpec((None, None, ATT_BLK, 2 * ATT_BLK), lambda g, h: (g, h, 0, 0)),
                  pl.BlockSpec((None, ATT_BLK, 2 * ATT_BLK), lambda g, h: (g, 0, 0))],
        out_specs=pl.BlockSpec(memory_space=pltpu.SMEM),
        out_shape=jax.ShapeDtypeStruct((REL_BUCKETS, N_ATT_HEADS), F32),
        compiler_params=_cparams(("arbitrary", "arbitrary")),
    )(dsb, buckets)


_NT = (((1,), (1,)), ((), ()))
_TN = (((0,), (0,)), ((), ()))
_ATT_SCALE = ATT_DH ** -0.5


_PAD_ROWS = SEQ + ATT_BLK


def _window_mask(has_prev):
    qi = lax.broadcasted_iota(I32, (ATT_BLK, 2 * ATT_BLK), 0)
    kj = lax.broadcasted_iota(I32, (ATT_BLK, 2 * ATT_BLK), 1)
    prev_ok = jnp.logical_and(jnp.logical_and(kj < ATT_BLK, kj >= qi), has_prev)
    return jnp.logical_or(prev_ok, jnp.logical_and(kj >= ATT_BLK, qi >= kj - ATT_BLK))


def _head_specs(col0):
    return pl.BlockSpec((SEQ, ATT_DH), lambda h: (0, col0 + h))


def _att_fwd(gi, qkv, bias, nb):
    blk, dh = ATT_BLK, ATT_DH

    def kern(q_ref, k_ref, v_ref, b_ref, o_ref, l_ref, kpad, vpad):
        zero = jnp.zeros((blk, dh), BF16)
        kpad[0:blk, :] = zero
        vpad[0:blk, :] = zero
        kpad[blk:, :] = k_ref[...]
        vpad[blk:, :] = v_ref[...]
        bias_m = b_ref[...]

        def body(b, carry):
            r0 = pl.multiple_of(b * blk, blk)
            q = q_ref[pl.ds(r0, blk), :]
            kw = kpad[pl.ds(r0, 2 * blk), :]
            vw = vpad[pl.ds(r0, 2 * blk), :]
            valid = _window_mask((b % nb) > 0)
            s = lax.dot_general(q, kw, _NT, preferred_element_type=F32) * _ATT_SCALE + bias_m
            s = jnp.where(valid, s, -1e30)
            mx = jnp.max(s, axis=-1, keepdims=True)
            e = jnp.exp(s - mx)
            den = jnp.sum(e, axis=-1, keepdims=True)
            o_ref[pl.ds(r0, blk), :] = jnp.dot((e / den).astype(BF16), vw, preferred_element_type=F32)
            l_ref[pl.ds(r0, blk), :] = jnp.broadcast_to(mx + jnp.log(den), (blk, dh))
            return carry

        lax.fori_loop(0, N_BLK, body, 0)

    return pl.pallas_call(
        kern, name=f"att_fwd_g{gi}", grid=(ATT_HPG,),
        in_specs=[_head_specs(0), _head_specs(ATT_HPG), _head_specs(2 * ATT_HPG),
                  pl.BlockSpec((None, None, blk, 2 * blk), lambda h: (gi, h, 0, 0))],
        out_specs=[_head_specs(0), _head_specs(0)],
        out_shape=[jax.ShapeDtypeStruct((SEQ, ATT_W), F32), jax.ShapeDtypeStruct((SEQ, ATT_W), F32)],
        scratch_shapes=[pltpu.VMEM((_PAD_ROWS, dh), BF16), pltpu.VMEM((_PAD_ROWS, dh), BF16)],
        compiler_params=_cparams(("arbitrary",)),
    )(qkv, qkv, qkv, bias)


def _att_bwd(gi, qkv, d_att, lse, dd, bias, nb):
    blk, dh = ATT_BLK, ATT_DH

    def kern(q_ref, k_ref, v_ref, do_ref, l_ref, d_ref, b_ref, dq_ref, dk_ref, dv_ref, dsb_ref,
             kpad, vpad, qpad, dopad, lpad, dpad):
        zero = jnp.zeros((blk, dh), BF16)
        zero_f = jnp.zeros((blk, dh), F32)
        kpad[0:blk, :] = zero
        vpad[0:blk, :] = zero
        kpad[blk:, :] = k_ref[...]
        vpad[blk:, :] = v_ref[...]
        qpad[SEQ:, :] = zero
        dopad[SEQ:, :] = zero
        lpad[SEQ:, :] = zero_f
        dpad[SEQ:, :] = zero_f
        qpad[0:SEQ, :] = q_ref[...]
        dopad[0:SEQ, :] = do_ref[...]
        lpad[0:SEQ, :] = l_ref[...]
        dpad[0:SEQ, :] = d_ref[...]
        bias_m = b_ref[...]
        bias_t = jnp.concatenate([bias_m[:, blk:], bias_m[:, :blk]], axis=0)
        dsb_ref[...] = jnp.zeros_like(dsb_ref)

        def dq_body(b, carry):
            r0 = pl.multiple_of(b * blk, blk)
            q, d_o = q_ref[pl.ds(r0, blk), :], do_ref[pl.ds(r0, blk), :]
            kw, vw = kpad[pl.ds(r0, 2 * blk), :], vpad[pl.ds(r0, 2 * blk), :]
            lrow, drow = l_ref[pl.ds(r0, blk), :][:, :1], d_ref[pl.ds(r0, blk), :][:, :1]
            valid = _window_mask((b % nb) > 0)
            s = lax.dot_general(q, kw, _NT, preferred_element_type=F32) * _ATT_SCALE + bias_m
            p = jnp.where(valid, jnp.exp(jnp.where(valid, s, -1e30) - lrow), 0.0)
            dp = lax.dot_general(d_o, vw, _NT, preferred_element_type=F32)
            ds = p * (dp - drow)
            dq = jnp.dot(ds.astype(BF16), kw, preferred_element_type=F32)
            dq_ref[pl.ds(r0, blk), :] = (dq * _ATT_SCALE).astype(BF16)
            dsb_ref[...] += ds
            return carry

        lax.fori_loop(0, N_BLK, dq_body, 0)

        qi = lax.broadcasted_iota(I32, (2 * blk, blk), 0)
        kj = lax.broadcasted_iota(I32, (2 * blk, blk), 1)

        def dkv_body(b, carry):
            r0 = pl.multiple_of(b * blk, blk)
            k, v = k_ref[pl.ds(r0, blk), :], v_ref[pl.ds(r0, blk), :]
            qw, dow = qpad[pl.ds(r0, 2 * blk), :], dopad[pl.ds(r0, 2 * blk), :]
            lrow, drow = lpad[pl.ds(r0, 2 * blk), :][:, :1], dpad[pl.ds(r0, 2 * blk), :][:, :1]
            has_next = jnp.logical_and(b + 1 < N_BLK, ((b + 1) % nb) > 0)
            next_ok = jnp.logical_and(jnp.logical_and(qi >= blk, kj >= qi - blk), has_next)
            valid = jnp.logical_or(jnp.logical_and(qi < blk, qi >= kj), next_ok)
            s = lax.dot_general(qw, k, _NT, preferred_element_type=F32) * _ATT_SCALE + bias_t
            p = jnp.where(valid, jnp.exp(jnp.where(valid, s, -1e30) - lrow), 0.0)
            dp = lax.dot_general(dow, v, _NT, preferred_element_type=F32)
            ds = p * (dp - drow)
            d_v = lax.dot_general(p.astype(BF16), dow, _TN, preferred_element_type=F32)
            d_k = lax.dot_general(ds.astype(BF16), qw, _TN, preferred_element_type=F32)
            dk_ref[pl.ds(r0, blk), :] = (d_k * _ATT_SCALE).astype(BF16)
            dv_ref[pl.ds(r0, blk), :] = d_v.astype(BF16)
            return carry

        lax.fori_loop(0, N_BLK, dkv_body, 0)

    return pl.pallas_call(
        kern, name=f"att_bwd_g{gi}", grid=(ATT_HPG,),
        in_specs=[_head_specs(0), _head_specs(ATT_HPG), _head_specs(2 * ATT_HPG),
                  _head_specs(0), _head_specs(0), _head_specs(0),
                  pl.BlockSpec((None, None, blk, 2 * blk), lambda h: (gi, h, 0, 0))],
        out_specs=[_head_specs(0), _head_specs(0), _head_specs(0),
                   pl.BlockSpec((None, blk, 2 * blk), lambda h: (h, 0, 0))],
        out_shape=[jax.ShapeDtypeStruct((SEQ, ATT_W), BF16)] * 3
        + [jax.ShapeDtypeStruct((ATT_HPG, blk, 2 * blk), F32)],
        scratch_shapes=[pltpu.VMEM((_PAD_ROWS, dh), BF16)] * 4 + [pltpu.VMEM((_PAD_ROWS, dh), F32)] * 2,
        compiler_params=_cparams(("arbitrary",)),
    )(qkv, qkv, qkv, d_att, lse, dd, bias)


def _rms_parts(x):
    r = lax.rsqrt(jnp.mean(x * x, axis=-1, keepdims=True) + RMS_EPS)
    return x * r, r


def _rms_bwd(d_xhat, xhat, r):
    return r * (d_xhat - xhat * jnp.mean(d_xhat * xhat, axis=-1, keepdims=True))


def _prenorm_fwd(name, x, gain, shift, scale):
    def body(xt, g, sh, sc):
        xhat, _ = _rms_parts(xt)
        return (xhat * g) * (1.0 + sc) + sh
    return _rowmap(name, body, [x], [gain, shift, scale], [(D_MODEL, BF16)])[0]


def _prenorm_bwd(name, d_hs, x, gain, scale, resid):
    n_dh = len(d_hs)

    def body(*args):
        d_h = args[0]
        for t in args[1:n_dh]:
            d_h = d_h + t
        xt, res, g, sc = args[n_dh:]
        xhat, r = _rms_parts(xt)
        nrm = xhat * g
        d_n = d_h * (1.0 + sc)
        dx = _rms_bwd(d_n * g, xhat, r) + res
        return (dx, jnp.sum(d_h, axis=0, keepdims=True), jnp.sum(d_h * nrm, axis=0, keepdims=True),
                jnp.sum(d_n * xhat, axis=0, keepdims=True))

    return _rowmap(name, body, list(d_hs) + [x, resid], [gain, scale], [(D_MODEL, F32)],
                   [D_MODEL, D_MODEL, D_MODEL])


def _gn_parts(ro):
    mu = jnp.mean(ro, axis=-1, keepdims=True)
    cen = ro - mu
    rstd = lax.rsqrt(jnp.mean(cen * cen, axis=-1, keepdims=True) + GN_EPS)
    return cen * rstd, rstd


def _retpost_fwd(ro, rg, gn_g, gn_b):
    def body(rot, rgt, g, b):
        outs = []
        for h in range(RET_HEADS):
            sl = slice(h * RET_DV, (h + 1) * RET_DV)
            nrm, _ = _gn_parts(rot[:, sl])
            gate = rgt[:, sl]
            outs.append((gate * _sigmoid(gate)) * (nrm * g[:, sl] + b[:, sl]))
        return jnp.concatenate(outs, axis=-1)
    return _rowmap("retpost_fwd", body, [ro, rg], [gn_g, gn_b], [(RET_V_W, BF16)])[0]


def _retpost_bwd(d_gated, ro, rg, gn_g, gn_b):
    def body(dgt, rot, rgt, g, b):
        d_ro, d_rg, d_g, d_b = [], [], [], []
        for h in range(RET_HEADS):
            sl = slice(h * RET_DV, (h + 1) * RET_DV)
            nrm, rstd = _gn_parts(rot[:, sl])
            gate, dg = rgt[:, sl], dgt[:, sl]
            sg = _sigmoid(gate)
            ron = nrm * g[:, sl] + b[:, sl]
            d_rg.append(dg * ron * (sg * (1.0 + gate * (1.0 - sg))))
            d_ron = dg * (gate * sg)
            d_g.append(jnp.sum(d_ron * nrm, axis=0, keepdims=True))
            d_b.append(jnp.sum(d_ron, axis=0, keepdims=True))
            d_n = d_ron * g[:, sl]
            d_ro.append(rstd * (d_n - jnp.mean(d_n, axis=-1, keepdims=True)
                                - nrm * jnp.mean(d_n * nrm, axis=-1, keepdims=True)))
        cat = lambda ts: jnp.concatenate(ts, axis=-1)
        return cat(d_ro), cat(d_rg), cat(d_g), cat(d_b)
    return _rowmap("retpost_bwd", body, [d_gated, ro, rg], [gn_g, gn_b],
                   [(RET_V_W, F32), (RET_V_W, BF16)], [RET_V_W, RET_V_W])


def _combine(os_, ls_):
    def body(o0, o1, o2, l0, l1, l2):
        mx = jnp.maximum(jnp.maximum(l0, l1), l2)
        e0, e1, e2 = jnp.exp(l0 - mx), jnp.exp(l1 - mx), jnp.exp(l2 - mx)
        den = e0 + e1 + e2
        att = (e0 / den) * o0 + (e1 / den) * o1 + (e2 / den) * o2
        return att, att, mx + jnp.log(den)
    return _rowmap("att_combine", body, list(os_) + list(ls_), [],
                   [(ATT_W, F32), (ATT_W, BF16), (ATT_W, F32)])


def _att_bwd_pre(d_att, att):
    def body(dt, at):
        outs = []
        for h in range(ATT_HPG):
            sl = slice(h * ATT_DH, (h + 1) * ATT_DH)
            outs.append(jnp.broadcast_to(jnp.sum(dt[:, sl] * at[:, sl], axis=-1, keepdims=True),
                                         (dt.shape[0], ATT_DH)))
        return dt, jnp.concatenate(outs, axis=-1)
    return _rowmap("att_bwd_pre", body, [d_att, att], [], [(ATT_W, BF16), (ATT_W, F32)])


def _merge_fwd(gates, ret_out, att_out):
    def body(gt, ro, ao):
        return _sigmoid(gt[:, :D_MODEL]) * ro + _sigmoid(gt[:, D_MODEL:]) * ao
    return _rowmap("merge_fwd", body, [gates, ret_out, att_out], [], [(D_MODEL, BF16)])[0]


def _merge_bwd(d_merged, gates, ret_out, att_out):
    def body(dm, gt, ro, ao):
        sa, sb = _sigmoid(gt[:, :D_MODEL]), _sigmoid(gt[:, D_MODEL:])
        d_gates = jnp.concatenate([dm * ro * (sa * (1.0 - sa)), dm * ao * (sb * (1.0 - sb))], axis=-1)
        return dm * sa, dm * sb, d_gates
    return _rowmap("merge_bwd", body, [d_merged, gates, ret_out, att_out], [],
                   [(D_MODEL, BF16), (D_MODEL, BF16), (2 * D_MODEL, BF16)])


def _gate_bwd(name, d_x, branch, gate):
    def body(dx, br, g):
        return dx * g, jnp.sum(dx * br, axis=0, keepdims=True)
    return _rowmap(name, body, [d_x, branch], [gate], [(D_MODEL, BF16)], [D_MODEL])


def _loss_head(x3, target, gain):
    def body(xt, tt, g):
        xhat, r = _rms_parts(xt)
        err = xhat * g - tt
        d_y = err / D_MODEL
        loss = 0.5 * jnp.sum(jnp.mean(err * err, axis=-1, keepdims=True), axis=0, keepdims=True)
        d_x = _rms_bwd(d_y * g, xhat, r)
        return d_x, jnp.broadcast_to(loss, (1, 128)), jnp.sum(d_y * xhat, axis=0, keepdims=True)
    return _rowmap("loss_head", body, [x3, target], [gain], [(D_MODEL, F32)], [128, D_MODEL])


def _local_step(pos, x, target, mod, norm1_g, norm2_g, norm_f_g, rel_bias, gn_g, gn_b,
                w_in, w_ret_out, w_att_out, w_o, w_ff1, w_ff2):
    sh1, sc1, g1, sh2, sc2, g2 = [mod[:, i * D_MODEL:(i + 1) * D_MODEL] for i in range(6)]
    cos, sin = _rope_tables()
    din, qd, kd, cd = _decay_tables()
    buckets = _bucket_tables()
    bias = _bias_build(rel_bias, buckets)
    dils = [d for _, d in ATT_GROUPS]
    nbs = [SEQ // d // ATT_BLK for d in dils]

    h1 = _prenorm_fwd("prenorm1_fwd", x, norm1_g, sh1, sc1)
    h1_p = [_permute_rows(h1, d) for d in dils]

    def rot_epi(acc, cs, sn, scale):
        half = RET_DK // 2
        x1, x2 = acc[:, :half], acc[:, half:]
        return (jnp.concatenate([x1 * cs - x2 * sn, x1 * sn + x2 * cs], axis=-1) * scale,)

    qk_scale = jnp.concatenate([jnp.ones((1, RET_QK_W), F32),
                                jnp.full((1, RET_QK_W), RET_DK ** -0.5, F32)], axis=-1)
    rope_ex = [(cos, (TM, RET_DK // 2), lambda i, j, kk: (i, 0)),
               (sin, (TM, RET_DK // 2), lambda i, j, kk: (i, 0)),
               (qk_scale, (1, RET_DK), lambda i, j, kk: (0, j))]
    rqk = _matmul("proj_qk", h1, w_in, "nn", SEQ, 2 * RET_QK_W, D_MODEL, [BF16], b_off=OFF_Q,
                  tn=RET_DK, tk=D_MODEL, epilogue=rot_epi, extras=rope_ex)[0]
    rv = _matmul("proj_rv", h1, w_in, "nn", SEQ, RET_V_W, D_MODEL, [BF16], b_off=OFF_V, tk=D_MODEL)[0]
    rg = _matmul("proj_rg", h1, w_in, "nn", SEQ, RET_V_W, D_MODEL, [F32], b_off=OFF_G, tk=D_MODEL)[0]
    gates = _matmul("proj_gates", h1, w_in, "nn", SEQ, 2 * D_MODEL, D_MODEL, [F32], b_off=OFF_GATE,
                    tn=512, tk=D_MODEL)[0]
    aqkv = [_matmul(f"proj_att_g{gi}", h1_p[gi], w_in, "nn", SEQ, 3 * ATT_W, D_MODEL, [BF16],
                    b_off=OFF_ATT + gi * 3 * ATT_W, tn=512, tk=D_MODEL)[0] for gi in range(3)]

    ro, states = _retention_fwd(rqk, rv, din, qd, kd, cd)
    gated = _retpost_fwd(ro, rg, gn_g, gn_b)
    ret_out = _matmul("ret_out", gated, w_ret_out, "nn", SEQ, D_MODEL, RET_V_W, [F32])[0]

    os_, ls_ = [], []
    for gi in range(3):
        o_g, l_g = _att_fwd(gi, aqkv[gi], bias, nbs[gi])
        os_.append(_unpermute_rows(o_g, dils[gi]))
        ls_.append(_unpermute_rows(l_g, dils[gi]))
    att, att_b, lse = _combine(os_, ls_)
    att_out = _matmul("att_out", att_b, w_att_out, "nn", SEQ, D_MODEL, ATT_W, [F32])[0]

    merged = _merge_fwd(gates, ret_out, att_out)

    def resid_epi(acc, xt, g):
        return xt + g * acc, acc

    def resid_ex(xin, g):
        return [(xin, (TM, TN), lambda i, j, kk: (i, j)), (g, (1, TN), lambda i, j, kk: (0, j))]

    x2, mix = _matmul("mix_out", merged, w_o, "nn", SEQ, D_MODEL, D_MODEL, [F32, F32],
                      epilogue=resid_epi, extras=resid_ex(x, g1))
    h2 = _prenorm_fwd("prenorm2_fwd", x2, norm2_g, sh2, sc2)

    def relu2_epi(acc):
        r = jnp.maximum(acc, 0.0)
        return r * r, acc

    act, u = _matmul("ff1", h2, w_ff1, "nn", SEQ, D_FF, D_MODEL, [BF16, F32], tk=D_MODEL,
                     epilogue=relu2_epi)
    x3, y2 = _matmul("ff2", act, w_ff2, "nn", SEQ, D_MODEL, D_FF, [F32, F32],
                     epilogue=resid_epi, extras=resid_ex(x2, g2))

    d_x3, loss, d_gf = _loss_head(x3, target, norm_f_g)

    d_y2, d_g2 = _gate_bwd("ff_gate_bwd", d_x3, y2, g2)

    def relu2_bwd_epi(acc, ut):
        return (acc * (2.0 * jnp.maximum(ut, 0.0)),)

    recv = {}
    gw_ff2 = _matmul_tn_pair("ff2_dw", pos, act, d_y2, D_FF, D_MODEL, SEQ, D_FF // N_CHIPS,
                             tm=512, tn=1024, tk=1024)
    d_u, recv["w_ff2"] = _matmul(
        "ff2_dx", d_y2, w_ff2, "nt", SEQ, D_FF, D_MODEL, [BF16], epilogue=relu2_bwd_epi,
        extras=[(u, (TM, TN), lambda i, j, kk: (i, j))], carry=(gw_ff2, 0, SHARD["w_ff2"]))
    gw_ff1 = _matmul_tn_pair("ff1_dw", pos, h2, d_u, D_MODEL, D_FF, SEQ, D_MODEL,
                             tm=512, tn=1024, tk=1024)
    d_h2, recv["w_ff1"] = _matmul("ff1_dx", d_u, w_ff1, "nt", SEQ, D_MODEL, D_FF, [F32],
                                  carry=(gw_ff1, 1, SHARD["w_ff1"]))
    d_x2, d_sh2, d_sc2, d_n2g = _prenorm_bwd("prenorm2_bwd", [d_h2], x2, norm2_g, sc2, d_x3)

    d_mix, d_g1 = _gate_bwd("mix_gate_bwd", d_x2, mix, g1)
    gw_o = _matmul_tn_pair("mix_dw", pos, merged, d_mix, D_MODEL, D_MODEL, SEQ, D_MODEL // N_CHIPS,
                           tm=128, tn=1024, tk=2048)
    d_merged, recv["w_o"] = _matmul("mix_dx", d_mix, w_o, "nt", SEQ, D_MODEL, D_MODEL, [F32],
                                    carry=(gw_o, 0, SHARD["w_o"]))
    d_ret_out, d_att_out, d_gates = _merge_bwd(d_merged, gates, ret_out, att_out)

    gw_ret_out = _matmul_tn_pair("ret_out_dw", pos, gated, d_ret_out, RET_V_W, D_MODEL, SEQ,
                                 RET_V_W // N_CHIPS, tm=256, tn=1024, tk=1024)
    d_gated, recv["w_ret_out"] = _matmul("ret_out_dx", d_ret_out, w_ret_out, "nt", SEQ, RET_V_W, D_MODEL,
                                         [F32], carry=(gw_ret_out, 0, SHARD["w_ret_out"]))
    gw_att_out = _matmul_tn_pair("att_out_dw", pos, att_b, d_att_out, ATT_W, D_MODEL, SEQ, ATT_W,
                                 tm=256, tn=1024, tk=2048)
    d_att, recv["w_att_out"] = _matmul("att_out_dx", d_att_out, w_att_out, "nt", SEQ, ATT_W, D_MODEL,
                                       [F32], carry=(gw_att_out, 1, SHARD["w_att_out"]))

    d_ro, d_rg, d_gn_g, d_gn_b = _retpost_bwd(d_gated, ro, rg, gn_g, gn_b)
    d_rq, d_rk, d_rv = _retention_bwd(rqk, rv, states, d_ro, din, qd, kd, cd, cos, sin)

    d_att_b, dd = _att_bwd_pre(d_att, att)
    d_aqkv, dsbs = [], []
    for gi in range(3):
        da_p = _permute_rows(d_att_b, dils[gi])
        l_p = _permute_rows(lse, dils[gi])
        dd_p = _permute_rows(dd, dils[gi])
        dq, dk, dv, dsb = _att_bwd(gi, aqkv[gi], da_p, l_p, dd_p, bias, nbs[gi])
        d_aqkv.append(_unpermute_rows(jnp.concatenate([dq, dk, dv], axis=-1), dils[gi]))
        dsbs.append(dsb)
    d_rel_bias = _bias_grad(jnp.stack(dsbs), buckets)

    d_proj = jnp.concatenate([d_rq, d_rk, d_rv, d_rg] + d_aqkv + [d_gates], axis=-1)
    gw_in = _matmul_tn_pair("proj_dw", pos, h1, d_proj, D_MODEL, IN_COLS, SEQ, D_MODEL,
                            tm=512, tn=640, tk=2048)
    sems, gw_in, land, token = _ici_start("ici_start_w_in", gw_in, 1, SHARD["w_in"])
    d_h1 = _matmul("proj_dx", d_proj, w_in, "nt", SEQ, D_MODEL, IN_COLS, [F32], tn=1024, tk=1280,
                   epilogue=lambda acc, tok: (acc,),
                   extras=[(token, (8, 128), lambda i, j, kk: (0, 0))])[0]
    recv["w_in"] = (sems, land)

    grad_x, d_sh1, d_sc1, d_n1g = _prenorm_bwd("prenorm1_bwd", [d_h1], x, norm1_g, sc1, d_x2)
    d_mod = jnp.concatenate([d_sh1, d_sc1, d_g1, d_sh2, d_sc2, d_g2], axis=-1)
    small = dict(norm1_g=d_n1g, norm2_g=d_n2g, norm_f_g=d_gf, gn_g=d_gn_g, gn_b=d_gn_b,
                 rel_bias=d_rel_bias)
    big = dict(w_in=gw_in, w_ret_out=gw_ret_out, w_att_out=gw_att_out, w_o=gw_o, w_ff1=gw_ff1,
               w_ff2=gw_ff2)
    return loss, grad_x, d_mod, small, big, recv


def _me():
    return lax.axis_index("x"), lax.axis_index("y"), lax.axis_index("c")


def _peer(x, y, c, mask):
    return (x ^ ((mask >> 2) & 1), y ^ ((mask >> 1) & 1), c ^ (mask & 1))


def _gather8(src_ref, dst_ref, send_sems, recv_sems):
    x, y, c = _me()
    me = 4 * x + 2 * y + c
    copies = []
    for mask in range(1, N_DEV):
        cp = pltpu.make_async_remote_copy(
            src_ref=src_ref, dst_ref=dst_ref.at[me], send_sem=send_sems.at[mask - 1],
            recv_sem=recv_sems.at[mask - 1], device_id=_peer(x, y, c, mask), device_id_type=MESH)
        cp.start()
        copies.append(cp)
    dst_ref[me] = src_ref[...]
    for cp in copies:
        cp.wait_recv()
    for cp in copies:
        cp.wait_send()


def _ada_fwd(c_in, w_ada, b_ada):
    ncol = ADA_COLS // N_CHIPS

    def body(c_ref, w_ref, b_ref, mod_ref, sc_ref, cbuf, cg, mbuf, mg, s1, r1, s2, r2):
        x, y, c = _me()
        me = 4 * x + 2 * y + c
        cv = c_ref[...]
        cbuf[...] = jnp.broadcast_to(cv * _sigmoid(cv), cbuf.shape)
        _gather8(cbuf, cg, s1, r1)
        rows = lax.broadcasted_iota(I32, (N_DEV, D_MODEL), 0)
        sc_all = jnp.zeros((N_DEV, D_MODEL), F32)
        for d in range(N_DEV):
            sc_all = jnp.where(rows == d, cg[d], sc_all)
        sc_ref[...] = sc_all
        mbuf[...] = jnp.dot(sc_all.astype(BF16), w_ref[...].astype(BF16), preferred_element_type=F32)
        _gather8(mbuf, mg, s2, r2)
        rowsel = lax.broadcasted_iota(I32, (N_DEV, ncol), 0) == me
        for k in range(N_CHIPS):
            blk = mg[2 * k]
            row = jnp.sum(jnp.where(rowsel, blk, 0.0), axis=0, keepdims=True)
            mod_ref[:, k * ncol:(k + 1) * ncol] = row + b_ref[:, k * ncol:(k + 1) * ncol]

    vm = pl.BlockSpec(memory_space=pltpu.VMEM)
    return pl.pallas_call(
        body, name="ada_fwd",
        in_specs=[vm, vm, vm], out_specs=[vm, vm],
        out_shape=[jax.ShapeDtypeStruct((1, ADA_COLS), F32), jax.ShapeDtypeStruct((N_DEV, D_MODEL), F32)],
        scratch_shapes=[
            pltpu.VMEM((8, D_MODEL), F32), pltpu.VMEM((N_DEV, 8, D_MODEL), F32),
            pltpu.VMEM((8, ncol), F32), pltpu.VMEM((N_DEV, 8, ncol), F32),
            pltpu.SemaphoreType.DMA((N_DEV - 1,)), pltpu.SemaphoreType.DMA((N_DEV - 1,)),
            pltpu.SemaphoreType.DMA((N_DEV - 1,)), pltpu.SemaphoreType.DMA((N_DEV - 1,)),
        ],
        compiler_params=pltpu.CompilerParams(vmem_limit_bytes=VMEM_LIMIT_V7X),
    )(c_in, w_ada, b_ada)


def _small_reduce(pack, sc_all):
    ncol = ADA_COLS // N_CHIPS

    def body(p_ref, sc_ref, tot_ref, gw_ref, pg, s1, r1):
        x, y, _ = _me()
        chip = 2 * x + y
        _gather8(p_ref, pg, s1, r1)
        tot = pg[0]
        for d in range(1, N_DEV):
            tot = tot + pg[d]
        tot_ref[...] = tot
        rows = lax.broadcasted_iota(I32, (N_DEV, ncol), 0)
        dmod = jnp.zeros((N_DEV, ncol), F32)
        for k in range(N_CHIPS):
            part = jnp.zeros((N_DEV, ncol), F32)
            for d in range(N_DEV):
                part = jnp.where(rows == d, pg[d, :, k * ncol:(k + 1) * ncol][0:1, :], part)
            dmod = jnp.where(chip == k, part, dmod)
        gw_ref[...] = lax.dot_general(sc_ref[...].astype(BF16), dmod.astype(BF16), _TN,
                                      preferred_element_type=F32)

    vm = pl.BlockSpec(memory_space=pltpu.VMEM)
    return pl.pallas_call(
        body, name="small_reduce",
        in_specs=[vm, vm], out_specs=[vm, vm],
        out_shape=[jax.ShapeDtypeStruct((8, ADA_COLS), F32), jax.ShapeDtypeStruct((D_MODEL, ncol), F32)],
        scratch_shapes=[pltpu.VMEM((N_DEV, 8, ADA_COLS), F32),
                        pltpu.SemaphoreType.DMA((N_DEV - 1,)), pltpu.SemaphoreType.DMA((N_DEV - 1,))],
        compiler_params=pltpu.CompilerParams(vmem_limit_bytes=VMEM_LIMIT_V7X),
    )(pack, sc_all)


BIG = (("w_in", 1), ("w_ret_out", 0), ("w_att_out", 1), ("w_o", 0), ("w_ff1", 1), ("w_ff2", 0))
SHARD = {"w_in": (D_MODEL, IN_COLS // N_CHIPS), "w_ret_out": (RET_V_W // N_CHIPS, D_MODEL),
         "w_att_out": (ATT_W, D_MODEL // N_CHIPS), "w_o": (D_MODEL // N_CHIPS, D_MODEL),
         "w_ff1": (D_MODEL, D_FF // N_CHIPS), "w_ff2": (D_FF // N_CHIPS, D_MODEL)}
_CHIP_FLIPS = ((1, 0), (0, 1), (1, 1))


def _region(ref, axis, chip, half, shard_shape):
    r, cw = shard_shape
    hr = r // 2
    if axis == 1:
        return ref.at[pl.ds(half * hr, hr), pl.ds(chip * cw, cw)]
    return ref.at[pl.ds(chip * r + half * hr, hr), :]


def _gather_weights(shards):
    nw = len(BIG)
    shapes = [s.shape for s in shards]
    full_shapes = [(r, N_CHIPS * cw) if ax == 1 else (N_CHIPS * r, cw)
                   for (r, cw), (_, ax) in zip(shapes, BIG)]

    def body(*refs):
        ins, outs = refs[:nw], refs[nw:2 * nw]
        own, from_ici, from_sib = refs[2 * nw:3 * nw], refs[3 * nw:4 * nw], refs[4 * nw:5 * nw]
        ld_sem, st_sem, s_ici, r_ici, s_d2d, r_d2d, st_a, st_b = refs[5 * nw:]
        x, y, c = _me()
        chip = 2 * x + y
        sib = (x, y, 1 - c)
        loads = [pltpu.make_async_copy(ins[i], own[i], ld_sem.at[i]) for i in range(nw)]
        for cp in loads:
            cp.start()
        pending, first = [], []
        for i, (_, ax) in enumerate(BIG):
            r, cw = shapes[i]
            hr = r // 2
            loads[i].wait()
            dst = outs[i].at[:, pl.ds(chip * cw, cw)] if ax == 1 else outs[i].at[pl.ds(chip * r, r), :]
            cp = pltpu.make_async_copy(own[i], dst, st_sem.at[i])
            cp.start()
            pending.append(cp)
            for j, (fx, fy) in enumerate(_CHIP_FLIPS):
                rc = pltpu.make_async_remote_copy(
                    src_ref=own[i].at[pl.ds(c * hr, hr), :], dst_ref=from_ici[i].at[j],
                    send_sem=s_ici.at[j * nw + i], recv_sem=r_ici.at[j * nw + i],
                    device_id=(x ^ fx, y ^ fy, c), device_id_type=MESH)
                rc.start()
                first.append((j, i, rc))
        passed = []
        for j, i, rc in first:
            fx, fy = _CHIP_FLIPS[j]
            src_chip = 2 * (x ^ fx) + (y ^ fy)
            ax = BIG[i][1]
            rc.wait_recv()
            fw = pltpu.make_async_remote_copy(
                src_ref=from_ici[i].at[j], dst_ref=from_sib[i].at[j], send_sem=s_d2d.at[j * nw + i],
                recv_sem=r_d2d.at[j * nw + i], device_id=sib, device_id_type=MESH)
            fw.start()
            passed.append((j, i, src_chip, fw))
            st = pltpu.make_async_copy(from_ici[i].at[j], _region(outs[i], ax, src_chip, c, shapes[i]),
                                       st_a.at[j * nw + i])
            st.start()
            pending.append(st)
        for j, i, src_chip, fw in passed:
            fw.wait_recv()
            st = pltpu.make_async_copy(from_sib[i].at[j],
                                       _region(outs[i], BIG[i][1], src_chip, 1 - c, shapes[i]),
                                       st_b.at[j * nw + i])
            st.start()
            pending.append(st)
        for _, _, rc in first:
            rc.wait_send()
        for _, _, _, fw in passed:
            fw.wait_send()
        for cp in pending:
            cp.wait()

    hbm = pl.BlockSpec(memory_space=pl.ANY)
    halves = [pltpu.VMEM((3, r // 2, cw), BF16) for r, cw in shapes]
    return pl.pallas_call(
        body, name="gather_weights",
        in_specs=[hbm] * nw, out_specs=[hbm] * nw,
        out_shape=[jax.ShapeDtypeStruct(fs, BF16) for fs in full_shapes],
        scratch_shapes=[pltpu.VMEM(sh, BF16) for sh in shapes] + halves + halves
        + [pltpu.SemaphoreType.DMA((nw,)), pltpu.SemaphoreType.DMA((nw,))]
        + [pltpu.SemaphoreType.DMA((3 * nw,))] * 6,
        compiler_params=pltpu.CompilerParams(vmem_limit_bytes=VMEM_LIMIT_V7X),
    )(*shards)


def _adam_update(w, g, m, v):
    mn = ADAM_B1 * m + (1.0 - ADAM_B1) * g
    vn = ADAM_B2 * v + (1.0 - ADAM_B2) * (g * g)
    m_hat = mn / (1.0 - ADAM_B1 ** ADAM_STEP)
    v_hat = vn / (1.0 - ADAM_B2 ** ADAM_STEP)
    return -ADAM_LR * (m_hat / (jnp.sqrt(v_hat) + ADAM_EPS) + ADAM_WD * w), mn, vn


def _final_update(name, pos, axis, psum, recv, w, m, v, tr=64):
    r, cw = w.shape
    hr = r // 2
    tr = min(tr, hr)
    nt = hr // tr

    def kern(pos_ref, p_ref, r_ref, w_ref, m_ref, v_ref, g_ref, d_ref, nm_ref, nv_ref,
             send_buf, land_buf, s_sem, r_sem):
        p, t = pl.program_id(0), pl.program_id(1)
        sib = _sibling()

        def copy(i):
            return pltpu.make_async_remote_copy(
                src_ref=send_buf.at[i], dst_ref=land_buf.at[i], send_sem=s_sem.at[i],
                recv_sem=r_sem.at[i], device_id=sib, device_id_type=MESH)

        def update(tot):
            g_ref[...] = tot
            d_ref[...], nm_ref[...], nv_ref[...] = _adam_update(w_ref[...], tot, m_ref[...], v_ref[...])

        @pl.when(p == 0)
        def _():
            tot = p_ref[...].astype(F32)
            for j in range(3):
                tot = tot + r_ref[j].astype(F32)
            send_buf[t] = tot
            copy(t).start()
            update(tot)

        @pl.when(p == 1)
        def _():
            copy(t).wait_recv()
            update(land_buf[t])

        @pl.when(jnp.logical_and(p == 1, t == nt - 1))
        def _():
            for i in range(nt):
                copy(i).wait_send()

    def shard_rows(p, t, pos_ref):
        return (jnp.where(p == 0, pos_ref[0], 1 - pos_ref[0]) * nt + t, 0)

    def own_part(p, t, pos_ref):
        tt = jnp.where(p == 0, t, nt - 1)
        return (tt, pos_ref[1]) if axis == 1 else (pos_ref[1] * nt + tt, 0)

    shard_spec = pl.BlockSpec((tr, cw), shard_rows)
    grid_spec = pltpu.PrefetchScalarGridSpec(
        num_scalar_prefetch=1, grid=(2, nt),
        in_specs=[pl.BlockSpec((tr, cw), own_part),
                  pl.BlockSpec((3, tr, cw), lambda p, t, pos_ref: (0, jnp.where(p == 0, t, nt - 1), 0)),
                  shard_spec, shard_spec, shard_spec],
        out_specs=[shard_spec] * 4,
        scratch_shapes=[pltpu.VMEM((nt, tr, cw), F32), pltpu.VMEM((nt, tr, cw), F32),
                        pltpu.SemaphoreType.DMA((nt,)), pltpu.SemaphoreType.DMA((nt,))])
    return pl.pallas_call(
        kern, name=name, grid_spec=grid_spec, out_shape=[jax.ShapeDtypeStruct((r, cw), F32)] * 4,
        compiler_params=_cparams(("arbitrary", "arbitrary")),
    )(pos, psum, recv, w, m, v)


def _adamw(name, w, g, m, v):
    r, cw = w.shape
    tr = min(r, 128)

    def kern(w_ref, g_ref, m_ref, v_ref, d_ref, nm_ref, nv_ref):
        d_ref[...], nm_ref[...], nv_ref[...] = _adam_update(w_ref[...], g_ref[...], m_ref[...], v_ref[...])

    spec = pl.BlockSpec((tr, cw), lambda i: (i, 0))
    return pl.pallas_call(
        kern, name=name, grid=(r // tr,), in_specs=[spec] * 4, out_specs=[spec] * 3,
        out_shape=[jax.ShapeDtypeStruct((r, cw), F32)] * 3, compiler_params=_cparams(("parallel",)),
    )(w, g, m, v)


_PACK_W = ADA_COLS
_NB = REL_BUCKETS * N_ATT_HEADS
_SMALL_SLOTS = {
    "b_ada": (0, 0, ADA_COLS),
    "norm1_g": (1, 0, D_MODEL), "norm2_g": (1, D_MODEL, D_MODEL), "norm_f_g": (1, 2 * D_MODEL, D_MODEL),
    "ret_gn_g": (1, 3 * D_MODEL, RET_V_W),
    "ret_gn_b": (2, 0, RET_V_W), "rel_bias": (2, RET_V_W, _NB), "loss": (2, RET_V_W + 512, 128),
}


def _pack_small(vals):
    rows = []
    for r in range(8):
        items = sorted([(off, n) for n, (rr, off, _) in _SMALL_SLOTS.items() if rr == r and n in vals])
        parts, pos = [], 0
        for off, n in items:
            if off > pos:
                parts.append(jnp.zeros((1, off - pos), F32))
            parts.append(vals[n].reshape(1, -1).astype(F32))
            pos = off + _SMALL_SLOTS[n][2]
        if pos < _PACK_W:
            parts.append(jnp.zeros((1, _PACK_W - pos), F32))
        rows.append(jnp.concatenate(parts, axis=-1))
    return jnp.concatenate(rows, axis=0)


def _unpack_small(pack, name):
    r, off, wd = _SMALL_SLOTS[name]
    return pack[r:r + 1, off:off + wd]


def kernel(x, c, w_ada, b_ada, norm1_g, w_in, rel_bias, ret_gn_g, ret_gn_b, w_ret_out, w_att_out, w_o, norm2_g, w_ff1, w_ff2, norm_f_g, loss_target, m_w_ada, m_b_ada, m_norm1_g, m_w_in, m_rel_bias, m_ret_gn_g, m_ret_gn_b, m_w_ret_out, m_w_att_out, m_w_o, m_norm2_g, m_w_ff1, m_w_ff2, m_norm_f_g, v_w_ada, v_b_ada, v_norm1_g, v_w_in, v_rel_bias, v_ret_gn_g, v_ret_gn_b, v_w_ret_out, v_w_att_out, v_w_o, v_norm2_g, v_w_ff1, v_w_ff2, v_norm_f_g):
    given = dict(locals())
    big_names = [n for n, _ in BIG]
    shard_w = {n: given[n][0] for n in big_names}
    assert all(shard_w[n].shape == SHARD[n] for n in big_names)

    full = _gather_weights([shard_w[n].astype(BF16) for n in big_names])
    full = dict(zip(big_names, full))
    mod, sc_all = _ada_fwd(c, w_ada[0], b_ada)
    pos = _where_am_i()

    loss, grad_x, d_mod, small, big, recv = _local_step(
        pos, x[0], loss_target[0], mod, norm1_g, norm2_g, norm_f_g.reshape(1, -1), rel_bias, ret_gn_g,
        ret_gn_b, full["w_in"], full["w_ret_out"], full["w_att_out"], full["w_o"], full["w_ff1"],
        full["w_ff2"])

    pack_g = _pack_small(dict(b_ada=d_mod, norm1_g=small["norm1_g"], norm2_g=small["norm2_g"],
                              norm_f_g=small["norm_f_g"], ret_gn_g=small["gn_g"], ret_gn_b=small["gn_b"],
                              rel_bias=small["rel_bias"], loss=loss))
    tot, g_w_ada = _small_reduce(pack_g, sc_all)

    small_names = ["b_ada", "norm1_g", "rel_bias", "ret_gn_g", "ret_gn_b", "norm2_g", "norm_f_g"]
    pack_w = _pack_small({n: given[n] for n in small_names})
    pack_m = _pack_small({n: given["m_" + n] for n in small_names})
    pack_v = _pack_small({n: given["v_" + n] for n in small_names})
    sd, sm, sv = _adamw("adamw_small", pack_w, tot, pack_m, pack_v)

    grads, deltas, new_m, new_v = {}, {}, {}, {}
    for n in small_names:
        shp = given[n].shape
        grads[n] = _unpack_small(tot, n).reshape(shp)
        deltas[n] = _unpack_small(sd, n).reshape(shp)
        new_m[n] = _unpack_small(sm, n).reshape(shp)
        new_v[n] = _unpack_small(sv, n).reshape(shp)
    d, nm, nv = _adamw("adamw_w_ada", w_ada[0], g_w_ada, m_w_ada[0], v_w_ada[0])
    grads["w_ada"], deltas["w_ada"], new_m["w_ada"], new_v["w_ada"] = g_w_ada[None], d[None], nm[None], nv[None]
    for n, ax in BIG[1:] + BIG[:1]:
        if n == "w_in":
            sems, land = recv[n]
            done = [tot, sd, deltas["w_ada"]] + [deltas[k] for k, _ in BIG[1:]]
            big[n], recv[n] = _ici_wait("ici_wait_w_in", sems, big[n], land, ax, SHARD[n], done)
        g, d, nm, nv = _final_update("final_" + n, pos, ax, big[n], recv[n], given[n][0],
                                     given["m_" + n][0], given["v_" + n][0])
        grads[n], deltas[n], new_m[n], new_v[n] = g[None], d[None], nm[None], nv[None]

    order = ["w_ada", "b_ada", "norm1_g", "w_in", "rel_bias", "ret_gn_g", "ret_gn_b", "w_ret_out",
             "w_att_out", "w_o", "norm2_g", "w_ff1", "w_ff2", "norm_f_g"]
    loss_out = _unpack_small(tot, "loss")[0, 0]
    return (loss_out, grad_x[None], *[grads[n] for n in order], *[deltas[n] for n in order],
            *[new_m[n] for n in order], *[new_v[n] for n in order])
```

```python
import functools
import math

import jax
import jax.numpy as jnp
from jax import lax
from jax.experimental import pallas as pl
from jax.experimental.pallas import tpu as pltpu

F32 = jnp.float32
BF16 = jnp.bfloat16
I32 = jnp.int32

SEQ = 2048
D_MODEL = 1024
RET_HEADS = 4
RET_DK = 256
RET_DV = 512
RET_CHUNK = 128
RET_QK_W = RET_HEADS * RET_DK
RET_V_W = RET_HEADS * RET_DV
ATT_GROUPS = ((128, 1), (512, 4), (2048, 16))
ATT_HPG = 4
ATT_DH = 128
ATT_W = ATT_HPG * ATT_DH
ATT_BLK = 128
N_BLK = SEQ // ATT_BLK
REL_BUCKETS = 32
REL_MAX_DIST = 2048
N_ATT_HEADS = 12
D_FF = 4 * D_MODEL
RMS_EPS = 1e-6
GN_EPS = 1e-5
ROPE_BASE = 10000.0
IN_COLS = 2 * RET_QK_W + 2 * RET_V_W + 9 * ATT_W + 2 * D_MODEL
OFF_Q, OFF_K, OFF_V, OFF_G = 0, RET_QK_W, 2 * RET_QK_W, 2 * RET_QK_W + RET_V_W
OFF_ATT = 2 * RET_QK_W + 2 * RET_V_W
OFF_GATE = OFF_ATT + 9 * ATT_W
N_CHIPS = 4
N_DEV = 8
ADA_COLS = 6 * D_MODEL

ADAM_LR = 0.001
ADAM_B1 = 0.9
ADAM_B2 = 0.999
ADAM_EPS = 1e-08
ADAM_WD = 0.01
ADAM_STEP = 10

VMEM_LIMIT_V7X = 56 * 1024 * 1024
MESH = pl.DeviceIdType.MESH


def _cparams(sem):
    return pltpu.CompilerParams(dimension_semantics=sem, vmem_limit_bytes=VMEM_LIMIT_V7X)


def _sigmoid(v):
    return 1.0 / (1.0 + jnp.exp(-v))


def _rowmap(name, body, row_ins, bcast_ins, row_outs, sum_outs=(), tm=256, after=()):
    m = row_ins[0].shape[0]
    n_in = len(row_ins) + len(bcast_ins)
    n_ro = len(row_outs)

    def kern(*refs):
        vals = [r[...] for r in refs[:n_in]]
        res = body(*vals)
        if not isinstance(res, (tuple, list)):
            res = (res,)
        outs = refs[n_in + len(after):]
        for r, v in zip(outs[:n_ro], res[:n_ro]):
            r[...] = v.astype(r.dtype)
        if sum_outs:
            @pl.when(pl.program_id(0) == 0)
            def _():
                for r in outs[n_ro:]:
                    r[...] = jnp.zeros_like(r)
            for r, v in zip(outs[n_ro:], res[n_ro:]):
                r[...] += v

    in_specs = [pl.BlockSpec((tm, a.shape[1]), lambda i: (i, 0)) for a in row_ins]
    in_specs += [pl.BlockSpec(a.shape, lambda i: (0, 0)) for a in bcast_ins]
    in_specs += [pl.BlockSpec(memory_space=pl.ANY)] * len(after)
    out_specs = [pl.BlockSpec((tm, n), lambda i: (i, 0)) for n, _ in row_outs]
    out_specs += [pl.BlockSpec((1, n), lambda i: (0, 0)) for n in sum_outs]
    out_shape = [jax.ShapeDtypeStruct((m, n), dt) for n, dt in row_outs]
    out_shape += [jax.ShapeDtypeStruct((1, n), F32) for n in sum_outs]
    return pl.pallas_call(
        kern, name=name, grid=(m // tm,), in_specs=in_specs, out_specs=out_specs,
        out_shape=out_shape, compiler_params=_cparams(("arbitrary",)),
    )(*row_ins, *bcast_ins, *after)


TM, TN = 1024, 1024


def _matmul(name, a, b, kind, m, n, k, outs, *, b_off=0, tm=TM, tn=TN, tk=1024,
            epilogue=None, extras=(), carry=None):
    tm, tn, tk = min(tm, m), min(tn, n), min(tk, k)
    nk = k // tk
    if kind == "nn":
        a_spec = pl.BlockSpec((tm, tk), lambda i, j, kk: (i, kk))
        b_spec = pl.BlockSpec((tk, tn), lambda i, j, kk: (kk, b_off // tn + j))
        dn = (((1,), (0,)), ((), ()))
    elif kind == "nt":
        a_spec = pl.BlockSpec((tm, tk), lambda i, j, kk: (i, kk))
        b_spec = pl.BlockSpec((tn, tk), lambda i, j, kk: (j, b_off // tk + kk))
        dn = (((1,), (1,)), ((), ()))
    else:
        a_spec = pl.BlockSpec((tk, tm), lambda i, j, kk: (kk, i))
        b_spec = pl.BlockSpec((tk, tn), lambda i, j, kk: (kk, j))
        dn = (((0,), (0,)), ((), ()))
    n_ex, n_out = len(extras), len(outs)
    if epilogue is None:
        epilogue = lambda acc: (acc,)

    def finish(acc, ex_refs, out_refs):
        res = epilogue(acc, *[r[...] for r in ex_refs])
        for r, v in zip(out_refs, res):
            r[...] = v.astype(r.dtype)

    n_c = 0 if carry is None else 1
    ni, nj = m // tm, n // tn

    def kern(a_ref, b_ref, *rest):
        ex_refs = rest[:n_ex]
        out_refs = rest[n_ex + n_c:n_ex + n_c + n_out]
        scratch = rest[n_ex + 2 * n_c + n_out:]
        i, j, kk = pl.program_id(0), pl.program_id(1), pl.program_id(2)
        if carry is not None:
            copies = lambda: _ici_copies(rest[n_ex], rest[n_ex + n_c + n_out], scratch[-2], scratch[-1],
                                         carry[1], carry[2])

            @pl.when(jnp.logical_and(jnp.logical_and(i == 0, j == 0), kk == 0))
            def _():
                for cp in copies():
                    cp.start()

        part = lax.dot_general(a_ref[...], b_ref[...], dn, preferred_element_type=F32)
        if nk == 1:
            finish(part, ex_refs, out_refs)
        else:
            acc_ref = scratch[0]

            @pl.when(kk == 0)
            def _():
                acc_ref[...] = part

            @pl.when(kk > 0)
            def _():
                acc_ref[...] += part

            @pl.when(kk == nk - 1)
            def _():
                finish(acc_ref[...], ex_refs, out_refs)

        if carry is not None:
            @pl.when(jnp.logical_and(jnp.logical_and(i == ni - 1, j == nj - 1), kk == nk - 1))
            def _():
                for cp in copies():
                    cp.wait_recv()
                for cp in copies():
                    cp.wait_send()

    hbm = pl.BlockSpec(memory_space=pl.ANY)
    in_specs = [a_spec, b_spec] + [pl.BlockSpec(bs, im) for _, bs, im in extras] + [hbm] * n_c
    out_specs = [pl.BlockSpec((tm, tn), lambda i, j, kk: (i, j)) for _ in outs] + [hbm] * n_c
    out_shape = [jax.ShapeDtypeStruct((m, n), dt) for dt in outs]
    scratch_shapes = [] if nk == 1 else [pltpu.VMEM((tm, tn), F32)]
    operands = [a, b] + [e[0] for e in extras]
    if carry is not None:
        r, cw = carry[2]
        out_shape.append(jax.ShapeDtypeStruct((3, r // 2, cw), BF16))
        scratch_shapes += [pltpu.SemaphoreType.DMA((3,)), pltpu.SemaphoreType.DMA((3,))]
        operands.append(carry[0])
    sem = ("arbitrary",) * 3 if carry is not None else ("parallel", "parallel", "arbitrary")
    return pl.pallas_call(
        kern, name=name, grid=(ni, nj, nk), in_specs=in_specs, out_specs=out_specs,
        out_shape=out_shape, scratch_shapes=scratch_shapes, compiler_params=_cparams(sem),
    )(*operands)


def _ici_copies(psum_ref, recv_ref, s_sem, r_sem, axis, shard_shape):
    x, y, c = _me()
    hr, cw = shard_shape[0] // 2, shard_shape[1]
    pick = lambda sems, j: sems[j] if isinstance(sems, (list, tuple)) else sems.at[j]
    copies = []
    for j, (fx, fy) in enumerate(_CHIP_FLIPS):
        chip = 2 * (x ^ fx) + (y ^ fy)
        src = psum_ref.at[:, pl.ds(chip * cw, cw)] if axis == 1 else psum_ref.at[pl.ds(chip * hr, hr), :]
        copies.append(pltpu.make_async_remote_copy(
            src_ref=src, dst_ref=recv_ref.at[j], send_sem=pick(s_sem, j), recv_sem=pick(r_sem, j),
            device_id=(x ^ fx, y ^ fy, c), device_id_type=MESH))
    return copies


_HBM_SPEC = pl.BlockSpec(memory_space=pltpu.HBM)
_SEM_SPEC = pl.BlockSpec(memory_space=pltpu.SEMAPHORE)


def _ici_start(name, psum, axis, shard_shape):
    r, cw = shard_shape
    land = lax.empty((3, r // 2, cw), BF16)

    def body(p_ref, land_ref, s0, s1, s2, r0, r1, r2, p_thru, land_thru, token):
        for cp in _ici_copies(p_ref, land_ref, [s0, s1, s2], [r0, r1, r2], axis, shard_shape):
            cp.start()
        token[...] = jnp.zeros_like(token)

    sem = pltpu.SemaphoreType.DMA(())
    res = pl.pallas_call(
        body, name=name,
        out_shape=(sem,) * 6 + (pltpu.HBM(psum.shape, BF16), pltpu.HBM(land.shape, BF16),
                                jax.ShapeDtypeStruct((8, 128), F32)),
        in_specs=(_HBM_SPEC, _HBM_SPEC),
        out_specs=(_SEM_SPEC,) * 6 + (_HBM_SPEC, _HBM_SPEC, pl.BlockSpec(memory_space=pltpu.VMEM)),
        input_output_aliases={0: 6, 1: 7},
        compiler_params=pltpu.CompilerParams(has_side_effects=pltpu.SideEffectType.DATAFLOW_SIDE_EFFECTING),
    )(pltpu.with_memory_space_constraint(psum, pltpu.HBM),
      pltpu.with_memory_space_constraint(land, pltpu.HBM))
    return res[:6], res[6], res[7], res[8]


def _ici_wait(name, sems, p_thru, land_thru, axis, shard_shape, after):
    n_after = len(after)

    def body(p_ref, land_ref, s0, s1, s2, r0, r1, r2, *rest):
        for cp in _ici_copies(p_ref, land_ref, [s0, s1, s2], [r0, r1, r2], axis, shard_shape):
            cp.wait_send()
            cp.wait_recv()

    return pl.pallas_call(
        body, name=name,
        out_shape=(pltpu.HBM(p_thru.shape, BF16), pltpu.HBM(land_thru.shape, BF16)),
        in_specs=(_HBM_SPEC, _HBM_SPEC) + (_SEM_SPEC,) * 6 + (pl.BlockSpec(memory_space=pl.ANY),) * n_after,
        out_specs=(_HBM_SPEC, _HBM_SPEC), input_output_aliases={0: 0, 1: 1},
        compiler_params=pltpu.CompilerParams(has_side_effects=pltpu.SideEffectType.DATAFLOW_SIDE_EFFECTING),
    )(p_thru, land_thru, *sems, *after)


def _where_am_i():
    x, y, c = _me()
    return jnp.stack([c, 2 * x + y]).astype(I32)


def _sibling():
    x, y, c = _me()
    return (x, y, 1 - c)


def _matmul_tn_pair(name, pos, a, b, m, n, k, shard_rows, *, tm, tn, tk):
    hr = shard_rows // 2
    tm, tn, tk = min(tm, hr), min(tn, n), min(tk, k)
    tph = hr // tm
    nt, nj, nk = (m // 2) // tm, n // tn, k // tk
    n_tiles = nt * nj

    def row_block(p, t, pos_ref):
        half = jnp.where(p == 0, 1 - pos_ref[0], pos_ref[0])
        return (t // tph) * (2 * tph) + half * tph + t % tph

    def kern(pos_ref, a_ref, b_ref, o_ref, acc_ref, send_buf, land_buf, s_sem, r_sem):
        p, t, j, kk = pl.program_id(0), pl.program_id(1), pl.program_id(2), pl.program_id(3)
        idx = t * nj + j
        sib = _sibling()

        def copy(i):
            return pltpu.make_async_remote_copy(
                src_ref=send_buf.at[i], dst_ref=land_buf.at[i], send_sem=s_sem.at[i],
                recv_sem=r_sem.at[i], device_id=sib, device_id_type=MESH)

        part = lax.dot_general(a_ref[...], b_ref[...], _TN, preferred_element_type=F32)

        @pl.when(kk == 0)
        def _():
            acc_ref[...] = part

        @pl.when(kk > 0)
        def _():
            acc_ref[...] += part

        @pl.when(jnp.logical_and(kk == nk - 1, p == 0))
        def _():
            send_buf[idx] = acc_ref[...].astype(BF16)
            copy(idx).start()

        @pl.when(jnp.logical_and(kk == nk - 1, p == 1))
        def _():
            copy(idx).wait_recv()
            o_ref[...] = (acc_ref[...] + land_buf[idx].astype(F32)).astype(BF16)

        @pl.when(jnp.logical_and(jnp.logical_and(p == 1, idx == n_tiles - 1), kk == nk - 1))
        def _():
            for i in range(n_tiles):
                copy(i).wait_send()

    grid_spec = pltpu.PrefetchScalarGridSpec(
        num_scalar_prefetch=1, grid=(2, nt, nj, nk),
        in_specs=[pl.BlockSpec((tk, tm), lambda p, t, j, kk, pos_ref: (kk, row_block(p, t, pos_ref))),
                  pl.BlockSpec((tk, tn), lambda p, t, j, kk, pos_ref: (kk, j))],
        out_specs=pl.BlockSpec((tm, tn), lambda p, t, j, kk, pos_ref: (p * t, p * j)),
        scratch_shapes=[pltpu.VMEM((tm, tn), F32), pltpu.VMEM((n_tiles, tm, tn), BF16),
                        pltpu.VMEM((n_tiles, tm, tn), BF16),
                        pltpu.SemaphoreType.DMA((n_tiles,)), pltpu.SemaphoreType.DMA((n_tiles,))])
    return pl.pallas_call(
        kern, name=name, grid_spec=grid_spec, out_shape=jax.ShapeDtypeStruct((m // 2, n), BF16),
        compiler_params=_cparams(("arbitrary",) * 4),
    )(pos, a, b)


def _rope_tables():
    half = RET_DK // 2
    inv = ROPE_BASE ** (-jnp.arange(half, dtype=F32) / half)
    ang = jnp.arange(SEQ).astype(F32)[:, None] * inv[None, :]
    return jnp.cos(ang), jnp.sin(ang)


def _decay_tables():
    c = RET_CHUNK
    log_g = jnp.log1p(-(2.0 ** (-5.0 - jnp.arange(RET_HEADS, dtype=F32))))
    idx = jnp.arange(c, dtype=F32)
    rel = idx[:, None] - idx[None, :]
    din = jnp.where(rel >= 0, jnp.exp(log_g[:, None, None] * jnp.maximum(rel, 0.0)), 0.0)
    qd = jnp.exp(log_g[:, None] * (idx + 1.0))[:, :, None]
    kd = jnp.exp(log_g[:, None] * (c - 1.0 - idx))[:, :, None]
    cd = jnp.exp(log_g * c)
    return din, qd, kd, cd


def _t5_bucket(dist):
    max_exact = REL_BUCKETS // 2
    d_f = jnp.maximum(dist, 1).astype(F32)
    large = max_exact + (jnp.log(d_f / max_exact) / math.log(REL_MAX_DIST / max_exact)
                         * (REL_BUCKETS - max_exact)).astype(I32)
    large = jnp.minimum(large, REL_BUCKETS - 1)
    return jnp.where(dist < max_exact, dist, large)


def _bucket_tables():
    qi = jnp.arange(ATT_BLK)[:, None]
    kj = jnp.arange(2 * ATT_BLK)[None, :]
    dist = jnp.clip(ATT_BLK + qi - kj, 0, ATT_BLK)
    return jnp.stack([_t5_bucket(dist * dil) for _, dil in ATT_GROUPS]).astype(I32)


def _permute_rows(t, dil):
    if dil == 1:
        return t
    s, w = t.shape
    return t.reshape(s // dil, dil, w).transpose(1, 0, 2).reshape(s, w)


def _unpermute_rows(t, dil):
    if dil == 1:
        return t
    s, w = t.shape
    return t.reshape(dil, s // dil, w).transpose(1, 0, 2).reshape(s, w)


def _retention_fwd(rqk, rv, din, qd, kd, cd):
    nc = SEQ // RET_CHUNK
    c, dk, dv = RET_CHUNK, RET_DK, RET_DV

    def kern(q_ref, k_ref, v_ref, din_ref, qd_ref, kd_ref, cd_ref, o_ref, st_ref, state):
        h, n = pl.program_id(0), pl.program_id(1)

        @pl.when(n == 0)
        def _():
            state[...] = jnp.zeros_like(state)

        q, k, v = q_ref[...], k_ref[...], v_ref[...]
        s_b = state[...].astype(BF16)
        st_ref[...] = s_b
        a = lax.dot_general(q, k, (((1,), (1,)), ((), ())), preferred_element_type=F32) * din_ref[...]
        o = jnp.dot(a.astype(BF16), v, preferred_element_type=F32)
        o += jnp.dot(q, s_b, preferred_element_type=F32) * qd_ref[...]
        o_ref[...] = o
        kk = (k.astype(F32) * kd_ref[...]).astype(BF16)
        upd = lax.dot_general(kk, v, (((0,), (0,)), ((), ())), preferred_element_type=F32)
        state[...] = state[...] * cd_ref[h] + upd

    return pl.pallas_call(
        kern, name="retention_fwd", grid=(RET_HEADS, nc),
        in_specs=[
            pl.BlockSpec((c, dk), lambda h, n: (n, h)),
            pl.BlockSpec((c, dk), lambda h, n: (n, RET_HEADS + h)),
            pl.BlockSpec((c, dv), lambda h, n: (n, h)),
            pl.BlockSpec((None, c, c), lambda h, n: (h, 0, 0)),
            pl.BlockSpec((None, c, 1), lambda h, n: (h, 0, 0)),
            pl.BlockSpec((None, c, 1), lambda h, n: (h, 0, 0)),
            pl.BlockSpec(memory_space=pltpu.SMEM),
        ],
        out_specs=[
            pl.BlockSpec((c, dv), lambda h, n: (n, h)),
            pl.BlockSpec((None, None, dk, dv), lambda h, n: (h, n, 0, 0)),
        ],
        out_shape=[
            jax.ShapeDtypeStruct((SEQ, RET_V_W), F32),
            jax.ShapeDtypeStruct((RET_HEADS, nc, dk, dv), BF16),
        ],
        scratch_shapes=[pltpu.VMEM((dk, dv), F32)],
        compiler_params=_cparams(("arbitrary", "arbitrary")),
    )(rqk, rqk, rv, din, qd, kd, cd)


def _retention_bwd(rqk, rv, states, d_ro, din, qd, kd, cd, cos, sin):
    nc = SEQ // RET_CHUNK
    c, dk, dv = RET_CHUNK, RET_DK, RET_DV
    half = dk // 2
    last = nc - 1

    def unrot(g, cs, sn):
        g1, g2 = g[:, :half], g[:, half:]
        return jnp.concatenate([g1 * cs + g2 * sn, g2 * cs - g1 * sn], axis=-1)

    def kern(q_ref, k_ref, v_ref, st_ref, do_ref, din_ref, qd_ref, kd_ref, cd_ref, cos_ref, sin_ref,
             dq_ref, dk_ref, dv_ref, dstate):
        h, step = pl.program_id(0), pl.program_id(1)

        @pl.when(step == 0)
        def _():
            dstate[...] = jnp.zeros_like(dstate)

        q, k, v, s_b = q_ref[...], k_ref[...], v_ref[...], st_ref[...]
        d_o = do_ref[...]
        d_ob = d_o.astype(BF16)
        d_oq = (d_o * qd_ref[...]).astype(BF16)
        ds_b = dstate[...].astype(BF16)
        din_m = din_ref[...]
        nt = (((1,), (1,)), ((), ()))
        tn = (((0,), (0,)), ((), ()))
        a_b = (lax.dot_general(q, k, nt, preferred_element_type=F32) * din_m).astype(BF16)
        kk = (k.astype(F32) * kd_ref[...]).astype(BF16)
        d_v = lax.dot_general(a_b, d_ob, tn, preferred_element_type=F32)
        d_v += jnp.dot(kk, ds_b, preferred_element_type=F32)
        d_a = (lax.dot_general(d_ob, v, nt, preferred_element_type=F32) * din_m).astype(BF16)
        d_q = jnp.dot(d_a, k, preferred_element_type=F32)
        d_q += lax.dot_general(d_oq, s_b, nt, preferred_element_type=F32)
        d_k = lax.dot_general(d_a, q, tn, preferred_element_type=F32)
        d_k += lax.dot_general(v, ds_b, nt, preferred_element_type=F32) * kd_ref[...]
        dstate[...] = dstate[...] * cd_ref[h] + lax.dot_general(q, d_oq, tn, preferred_element_type=F32)
        cs, sn = cos_ref[...], sin_ref[...]
        dq_ref[...] = unrot(d_q, cs, sn).astype(BF16)
        dk_ref[...] = (unrot(d_k, cs, sn) * (RET_DK ** -0.5)).astype(BF16)
        dv_ref[...] = d_v.astype(BF16)

    return pl.pallas_call(
        kern, name="retention_bwd", grid=(RET_HEADS, nc),
        in_specs=[
            pl.BlockSpec((c, dk), lambda h, n: (last - n, h)),
            pl.BlockSpec((c, dk), lambda h, n: (last - n, RET_HEADS + h)),
            pl.BlockSpec((c, dv), lambda h, n: (last - n, h)),
            pl.BlockSpec((None, None, dk, dv), lambda h, n: (h, last - n, 0, 0)),
            pl.BlockSpec((c, dv), lambda h, n: (last - n, h)),
            pl.BlockSpec((None, c, c), lambda h, n: (h, 0, 0)),
            pl.BlockSpec((None, c, 1), lambda h, n: (h, 0, 0)),
            pl.BlockSpec((None, c, 1), lambda h, n: (h, 0, 0)),
            pl.BlockSpec(memory_space=pltpu.SMEM),
            pl.BlockSpec((c, half), lambda h, n: (last - n, 0)),
            pl.BlockSpec((c, half), lambda h, n: (last - n, 0)),
        ],
        out_specs=[
            pl.BlockSpec((c, dk), lambda h, n: (last - n, h)),
            pl.BlockSpec((c, dk), lambda h, n: (last - n, h)),
            pl.BlockSpec((c, dv), lambda h, n: (last - n, h)),
        ],
        out_shape=[
            jax.ShapeDtypeStruct((SEQ, RET_QK_W), BF16),
            jax.ShapeDtypeStruct((SEQ, RET_QK_W), BF16),
            jax.ShapeDtypeStruct((SEQ, RET_V_W), BF16),
        ],
        scratch_shapes=[pltpu.VMEM((dk, dv), F32)],
        compiler_params=_cparams(("arbitrary", "arbitrary")),
    )(rqk, rqk, rv, states, d_ro, din, qd, kd, cd, cos, sin)


def _bias_build(rel_bias, buckets):
    ng = len(ATT_GROUPS)

    def kern(tab_ref, bkt_ref, o_ref):
        g, h = pl.program_id(0), pl.program_id(1)
        bkt = bkt_ref[...]
        acc = jnp.zeros(bkt.shape, F32)
        for b in range(REL_BUCKETS):
            acc = jnp.where(bkt == b, tab_ref[b, g * ATT_HPG + h], acc)
        o_ref[...] = acc

    return pl.pallas_call(
        kern, name="bias_build", grid=(ng, ATT_HPG),
        in_specs=[pl.BlockSpec(memory_space=pltpu.SMEM),
                  pl.BlockSpec((None, ATT_BLK, 2 * ATT_BLK), lambda g, h: (g, 0, 0))],
        out_specs=pl.BlockSpec((None, None, ATT_BLK, 2 * ATT_BLK), lambda g, h: (g, h, 0, 0)),
        out_shape=jax.ShapeDtypeStruct((ng, ATT_HPG, ATT_BLK, 2 * ATT_BLK), F32),
        compiler_params=_cparams(("arbitrary", "arbitrary")),
    )(rel_bias, buckets)


def _bias_grad(dsb, buckets):
    ng = len(ATT_GROUPS)

    def kern(ds_ref, bkt_ref, o_ref):
        g, h = pl.program_id(0), pl.program_id(1)
        bkt, ds = bkt_ref[...], ds_ref[...]
        for b in range(REL_BUCKETS):
            o_ref[b, g * ATT_HPG + h] = jnp.sum(jnp.where(bkt == b, ds, 0.0))

    return pl.pallas_call(
        kern, name="bias_grad", grid=(ng, ATT_HPG),
        in_specs=[pl.BlockSpec((None, None, ATT_BLK, 2 * ATT_BLK), lambda g, h: (g, h, 0, 0)),
                  pl.BlockSpec((None, ATT_BLK, 2 * ATT_BLK), lambda g, h: (g, 0, 0))],
        out_specs=pl.BlockSpec(memory_space=pltpu.SMEM),
        out_shape=jax.ShapeDtypeStruct((REL_BUCKETS, N_ATT_HEADS), F32),
        compiler_params=_cparams(("arbitrary", "arbitrary")),
    )(dsb, buckets)


_NT = (((1,), (1,)), ((), ()))
_TN = (((0,), (0,)), ((), ()))
_ATT_SCALE = ATT_DH ** -0.5


_PAD_ROWS = SEQ + ATT_BLK


def _window_mask(has_prev):
    qi = lax.broadcasted_iota(I32, (ATT_BLK, 2 * ATT_BLK), 0)
    kj = lax.broadcasted_iota(I32, (ATT_BLK, 2 * ATT_BLK), 1)
    prev_ok = jnp.logical_and(jnp.logical_and(kj < ATT_BLK, kj >= qi), has_prev)
    return jnp.logical_or(prev_ok, jnp.logical_and(kj >= ATT_BLK, qi >= kj - ATT_BLK))


def _head_specs(col0):
    return pl.BlockSpec((SEQ, ATT_DH), lambda h: (0, col0 + h))


def _att_fwd(gi, qkv, bias, nb):
    blk, dh = ATT_BLK, ATT_DH

    def kern(q_ref, k_ref, v_ref, b_ref, o_ref, l_ref, kpad, vpad):
        zero = jnp.zeros((blk, dh), BF16)
        kpad[0:blk, :] = zero
        vpad[0:blk, :] = zero
        kpad[blk:, :] = k_ref[...]
        vpad[blk:, :] = v_ref[...]
        bias_m = b_ref[...]

        def body(b, carry):
            r0 = pl.multiple_of(b * blk, blk)
            q = q_ref[pl.ds(r0, blk), :]
            kw = kpad[pl.ds(r0, 2 * blk), :]
            vw = vpad[pl.ds(r0, 2 * blk), :]
            valid = _window_mask((b % nb) > 0)
            s = lax.dot_general(q, kw, _NT, preferred_element_type=F32) * _ATT_SCALE + bias_m
            s = jnp.where(valid, s, -1e30)
            mx = jnp.max(s, axis=-1, keepdims=True)
            e = jnp.exp(s - mx)
            den = jnp.sum(e, axis=-1, keepdims=True)
            o_ref[pl.ds(r0, blk), :] = jnp.dot((e / den).astype(BF16), vw, preferred_element_type=F32)
            l_ref[pl.ds(r0, blk), :] = jnp.broadcast_to(mx + jnp.log(den), (blk, dh))
            return carry

        lax.fori_loop(0, N_BLK, body, 0)

    return pl.pallas_call(
        kern, name=f"att_fwd_g{gi}", grid=(ATT_HPG,),
        in_specs=[_head_specs(0), _head_specs(ATT_HPG), _head_specs(2 * ATT_HPG),
                  pl.BlockSpec((None, None, blk, 2 * blk), lambda h: (gi, h, 0, 0))],
        out_specs=[_head_specs(0), _head_specs(0)],
        out_shape=[jax.ShapeDtypeStruct((SEQ, ATT_W), F32), jax.ShapeDtypeStruct((SEQ, ATT_W), F32)],
        scratch_shapes=[pltpu.VMEM((_PAD_ROWS, dh), BF16), pltpu.VMEM((_PAD_ROWS, dh), BF16)],
        compiler_params=_cparams(("arbitrary",)),
    )(qkv, qkv, qkv, bias)


def _att_bwd(gi, qkv, d_att, lse, dd, bias, nb):
    blk, dh = ATT_BLK, ATT_DH

    def kern(q_ref, k_ref, v_ref, do_ref, l_ref, d_ref, b_ref, dq_ref, dk_ref, dv_ref, dsb_ref,
             kpad, vpad, qpad, dopad, lpad, dpad):
        zero = jnp.zeros((blk, dh), BF16)
        zero_f = jnp.zeros((blk, dh), F32)
        kpad[0:blk, :] = zero
        vpad[0:blk, :] = zero
        kpad[blk:, :] = k_ref[...]
        vpad[blk:, :] = v_ref[...]
        qpad[SEQ:, :] = zero
        dopad[SEQ:, :] = zero
        lpad[SEQ:, :] = zero_f
        dpad[SEQ:, :] = zero_f
        qpad[0:SEQ, :] = q_ref[...]
        dopad[0:SEQ, :] = do_ref[...]
        lpad[0:SEQ, :] = l_ref[...]
        dpad[0:SEQ, :] = d_ref[...]
        bias_m = b_ref[...]
        bias_t = jnp.concatenate([bias_m[:, blk:], bias_m[:, :blk]], axis=0)
        dsb_ref[...] = jnp.zeros_like(dsb_ref)

        def dq_body(b, carry):
            r0 = pl.multiple_of(b * blk, blk)
            q, d_o = q_ref[pl.ds(r0, blk), :], do_ref[pl.ds(r0, blk), :]
            kw, vw = kpad[pl.ds(r0, 2 * blk), :], vpad[pl.ds(r0, 2 * blk), :]
            lrow, drow = l_ref[pl.ds(r0, blk), :][:, :1], d_ref[pl.ds(r0, blk), :][:, :1]
            valid = _window_mask((b % nb) > 0)
            s = lax.dot_general(q, kw, _NT, preferred_element_type=F32) * _ATT_SCALE + bias_m
            p = jnp.where(valid, jnp.exp(jnp.where(valid, s, -1e30) - lrow), 0.0)
            dp = lax.dot_general(d_o, vw, _NT, preferred_element_type=F32)
            ds = p * (dp - drow)
            dq = jnp.dot(ds.astype(BF16), kw, preferred_element_type=F32)
            dq_ref[pl.ds(r0, blk), :] = (dq * _ATT_SCALE).astype(BF16)
            dsb_ref[...] += ds
            return carry

        lax.fori_loop(0, N_BLK, dq_body, 0)

        qi = lax.broadcasted_iota(I32, (2 * blk, blk), 0)
        kj = lax.broadcasted_iota(I32, (2 * blk, blk), 1)

        def dkv_body(b, carry):
            r0 = pl.multiple_of(b * blk, blk)
            k, v = k_ref[pl.ds(r0, blk), :], v_ref[pl.ds(r0, blk), :]
            qw, dow = qpad[pl.ds(r0, 2 * blk), :], dopad[pl.ds(r0, 2 * blk), :]
            lrow, drow = lpad[pl.ds(r0, 2 * blk), :][:, :1], dpad[pl.ds(r0, 2 * blk), :][:, :1]
            has_next = jnp.logical_and(b + 1 < N_BLK, ((b + 1) % nb) > 0)
            next_ok = jnp.logical_and(jnp.logical_and(qi >= blk, kj >= qi - blk), has_next)
            valid = jnp.logical_or(jnp.logical_and(qi < blk, qi >= kj), next_ok)
            s = lax.dot_general(qw, k, _NT, preferred_element_type=F32) * _ATT_SCALE + bias_t
            p = jnp.where(valid, jnp.exp(jnp.where(valid, s, -1e30) - lrow), 0.0)
            dp = lax.dot_general(dow, v, _NT, preferred_element_type=F32)
            ds = p * (dp - drow)
            d_v = lax.dot_general(p.astype(BF16), dow, _TN, preferred_element_type=F32)
            d_k = lax.dot_general(ds.astype(BF16), qw, _TN, preferred_element_type=F32)
            dk_ref[pl.ds(r0, blk), :] = (d_k * _ATT_SCALE).astype(BF16)
            dv_ref[pl.ds(r0, blk), :] = d_v.astype(BF16)
            return carry

        lax.fori_loop(0, N_BLK, dkv_body, 0)

    return pl.pallas_call(
        kern, name=f"att_bwd_g{gi}", grid=(ATT_HPG,),
        in_specs=[_head_specs(0), _head_specs(ATT_HPG), _head_specs(2 * ATT_HPG),
                  _head_specs(0), _head_specs(0), _head_specs(0),
                  pl.BlockSpec((None, None, blk, 2 * blk), lambda h: (gi, h, 0, 0))],
        out_specs=[_head_specs(0), _head_specs(0), _head_specs(0),
                   pl.BlockSpec((None, blk, 2 * blk), lambda h: (h, 0, 0))],
        out_shape=[jax.ShapeDtypeStruct((SEQ, ATT_W), BF16)] * 3
        + [jax.ShapeDtypeStruct((ATT_HPG, blk, 2 * blk), F32)],
        scratch_shapes=[pltpu.VMEM((_PAD_ROWS, dh), BF16)] * 4 + [pltpu.VMEM((_PAD_ROWS, dh), F32)] * 2,
        compiler_params=_cparams(("arbitrary",)),
    )(qkv, qkv, qkv, d_att, lse, dd, bias)


def _rms_parts(x):
    r = lax.rsqrt(jnp.mean(x * x, axis=-1, keepdims=True) + RMS_EPS)
    return x * r, r


def _rms_bwd(d_xhat, xhat, r):
    return r * (d_xhat - xhat * jnp.mean(d_xhat * xhat, axis=-1, keepdims=True))


def _prenorm_fwd(name, x, gain, shift, scale):
    def body(xt, g, sh, sc):
        xhat, _ = _rms_parts(xt)
        return (xhat * g) * (1.0 + sc) + sh
    return _rowmap(name, body, [x], [gain, shift, scale], [(D_MODEL, BF16)])[0]


def _prenorm_bwd(name, d_hs, x, gain, scale, resid, after=()):
    n_dh = len(d_hs)

    def body(*args):
        d_h = args[0]
        for t in args[1:n_dh]:
            d_h = d_h + t
        xt, res, g, sc = args[n_dh:]
        xhat, r = _rms_parts(xt)
        nrm = xhat * g
        d_n = d_h * (1.0 + sc)
        dx = _rms_bwd(d_n * g, xhat, r) + res
        return (dx, jnp.sum(d_h, axis=0, keepdims=True), jnp.sum(d_h * nrm, axis=0, keepdims=True),
                jnp.sum(d_n * xhat, axis=0, keepdims=True))

    return _rowmap(name, body, list(d_hs) + [x, resid], [gain, scale], [(D_MODEL, F32)],
                   [D_MODEL, D_MODEL, D_MODEL], after=after)


def _gn_parts(ro):
    mu = jnp.mean(ro, axis=-1, keepdims=True)
    cen = ro - mu
    rstd = lax.rsqrt(jnp.mean(cen * cen, axis=-1, keepdims=True) + GN_EPS)
    return cen * rstd, rstd


def _retpost_fwd(ro, rg, gn_g, gn_b):
    def body(rot, rgt, g, b):
        outs = []
        for h in range(RET_HEADS):
            sl = slice(h * RET_DV, (h + 1) * RET_DV)
            nrm, _ = _gn_parts(rot[:, sl])
            gate = rgt[:, sl]
            outs.append((gate * _sigmoid(gate)) * (nrm * g[:, sl] + b[:, sl]))
        return jnp.concatenate(outs, axis=-1)
    return _rowmap("retpost_fwd", body, [ro, rg], [gn_g, gn_b], [(RET_V_W, BF16)])[0]


def _retpost_bwd(d_gated, ro, rg, gn_g, gn_b):
    def body(dgt, rot, rgt, g, b):
        d_ro, d_rg, d_g, d_b = [], [], [], []
        for h in range(RET_HEADS):
            sl = slice(h * RET_DV, (h + 1) * RET_DV)
            nrm, rstd = _gn_parts(rot[:, sl])
            gate, dg = rgt[:, sl], dgt[:, sl]
            sg = _sigmoid(gate)
            ron = nrm * g[:, sl] + b[:, sl]
            d_rg.append(dg * ron * (sg * (1.0 + gate * (1.0 - sg))))
            d_ron = dg * (gate * sg)
            d_g.append(jnp.sum(d_ron * nrm, axis=0, keepdims=True))
            d_b.append(jnp.sum(d_ron, axis=0, keepdims=True))
            d_n = d_ron * g[:, sl]
            d_ro.append(rstd * (d_n - jnp.mean(d_n, axis=-1, keepdims=True)
                                - nrm * jnp.mean(d_n * nrm, axis=-1, keepdims=True)))
        cat = lambda ts: jnp.concatenate(ts, axis=-1)
        return cat(d_ro), cat(d_rg), cat(d_g), cat(d_b)
    return _rowmap("retpost_bwd", body, [d_gated, ro, rg], [gn_g, gn_b],
                   [(RET_V_W, F32), (RET_V_W, BF16)], [RET_V_W, RET_V_W])


def _combine(os_, ls_):
    def body(o0, o1, o2, l0, l1, l2):
        mx = jnp.maximum(jnp.maximum(l0, l1), l2)
        e0, e1, e2 = jnp.exp(l0 - mx), jnp.exp(l1 - mx), jnp.exp(l2 - mx)
        den = e0 + e1 + e2
        att = (e0 / den) * o0 + (e1 / den) * o1 + (e2 / den) * o2
        return att, att, mx + jnp.log(den)
    return _rowmap("att_combine", body, list(os_) + list(ls_), [],
                   [(ATT_W, F32), (ATT_W, BF16), (ATT_W, F32)])


def _att_bwd_pre(d_att, att):
    def body(dt, at):
        outs = []
        for h in range(ATT_HPG):
            sl = slice(h * ATT_DH, (h + 1) * ATT_DH)
            outs.append(jnp.broadcast_to(jnp.sum(dt[:, sl] * at[:, sl], axis=-1, keepdims=True),
                                         (dt.shape[0], ATT_DH)))
        return dt, jnp.concatenate(outs, axis=-1)
    return _rowmap("att_bwd_pre", body, [d_att, att], [], [(ATT_W, BF16), (ATT_W, F32)])


def _merge_fwd(gates, ret_out, att_out):
    def body(gt, ro, ao):
        return _sigmoid(gt[:, :D_MODEL]) * ro + _sigmoid(gt[:, D_MODEL:]) * ao
    return _rowmap("merge_fwd", body, [gates, ret_out, att_out], [], [(D_MODEL, BF16)])[0]


def _merge_bwd(d_merged, gates, ret_out, att_out):
    def body(dm, gt, ro, ao):
        sa, sb = _sigmoid(gt[:, :D_MODEL]), _sigmoid(gt[:, D_MODEL:])
        d_gates = jnp.concatenate([dm * ro * (sa * (1.0 - sa)), dm * ao * (sb * (1.0 - sb))], axis=-1)
        return dm * sa, dm * sb, d_gates
    return _rowmap("merge_bwd", body, [d_merged, gates, ret_out, att_out], [],
                   [(D_MODEL, BF16), (D_MODEL, BF16), (2 * D_MODEL, BF16)])


def _gate_bwd(name, d_x, branch, gate):
    def body(dx, br, g):
        return dx * g, jnp.sum(dx * br, axis=0, keepdims=True)
    return _rowmap(name, body, [d_x, branch], [gate], [(D_MODEL, BF16)], [D_MODEL])


def _loss_head(x3, target, gain):
    def body(xt, tt, g):
        xhat, r = _rms_parts(xt)
        err = xhat * g - tt
        d_y = err / D_MODEL
        loss = 0.5 * jnp.sum(jnp.mean(err * err, axis=-1, keepdims=True), axis=0, keepdims=True)
        d_x = _rms_bwd(d_y * g, xhat, r)
        return d_x, jnp.broadcast_to(loss, (1, 128)), jnp.sum(d_y * xhat, axis=0, keepdims=True)
    return _rowmap("loss_head", body, [x3, target], [gain], [(D_MODEL, F32)], [128, D_MODEL])


def _local_step(pos, x, target, mod, norm1_g, norm2_g, norm_f_g, rel_bias, gn_g, gn_b,
                w_in, w_ret_out, w_att_out, w_o, w_ff1, w_ff2):
    sh1, sc1, g1, sh2, sc2, g2 = [mod[:, i * D_MODEL:(i + 1) * D_MODEL] for i in range(6)]
    cos, sin = _rope_tables()
    din, qd, kd, cd = _decay_tables()
    buckets = _bucket_tables()
    bias = _bias_build(rel_bias, buckets)
    dils = [d for _, d in ATT_GROUPS]
    nbs = [SEQ // d // ATT_BLK for d in dils]

    h1 = _prenorm_fwd("prenorm1_fwd", x, norm1_g, sh1, sc1)
    h1_p = [_permute_rows(h1, d) for d in dils]

    def rot_epi(acc, cs, sn, scale):
        half = RET_DK // 2
        x1, x2 = acc[:, :half], acc[:, half:]
        return (jnp.concatenate([x1 * cs - x2 * sn, x1 * sn + x2 * cs], axis=-1) * scale,)

    qk_scale = jnp.concatenate([jnp.ones((1, RET_QK_W), F32),
                                jnp.full((1, RET_QK_W), RET_DK ** -0.5, F32)], axis=-1)
    rope_ex = [(cos, (TM, RET_DK // 2), lambda i, j, kk: (i, 0)),
               (sin, (TM, RET_DK // 2), lambda i, j, kk: (i, 0)),
               (qk_scale, (1, RET_DK), lambda i, j, kk: (0, j))]
    rqk = _matmul("proj_qk", h1, w_in, "nn", SEQ, 2 * RET_QK_W, D_MODEL, [BF16], b_off=OFF_Q,
                  tn=RET_DK, tk=D_MODEL, epilogue=rot_epi, extras=rope_ex)[0]
    rv = _matmul("proj_rv", h1, w_in, "nn", SEQ, RET_V_W, D_MODEL, [BF16], b_off=OFF_V, tk=D_MODEL)[0]
    rg = _matmul("proj_rg", h1, w_in, "nn", SEQ, RET_V_W, D_MODEL, [F32], b_off=OFF_G, tk=D_MODEL)[0]
    gates = _matmul("proj_gates", h1, w_in, "nn", SEQ, 2 * D_MODEL, D_MODEL, [F32], b_off=OFF_GATE,
                    tn=512, tk=D_MODEL)[0]
    aqkv = [_matmul(f"proj_att_g{gi}", h1_p[gi], w_in, "nn", SEQ, 3 * ATT_W, D_MODEL, [BF16],
                    b_off=OFF_ATT + gi * 3 * ATT_W, tn=512, tk=D_MODEL)[0] for gi in range(3)]

    ro, states = _retention_fwd(rqk, rv, din, qd, kd, cd)
    gated = _retpost_fwd(ro, rg, gn_g, gn_b)
    ret_out = _matmul("ret_out", gated, w_ret_out, "nn", SEQ, D_MODEL, RET_V_W, [F32])[0]

    os_, ls_ = [], []
    for gi in range(3):
        o_g, l_g = _att_fwd(gi, aqkv[gi], bias, nbs[gi])
        os_.append(_unpermute_rows(o_g, dils[gi]))
        ls_.append(_unpermute_rows(l_g, dils[gi]))
    att, att_b, lse = _combine(os_, ls_)
    att_out = _matmul("att_out", att_b, w_att_out, "nn", SEQ, D_MODEL, ATT_W, [F32])[0]

    merged = _merge_fwd(gates, ret_out, att_out)

    def resid_epi(acc, xt, g):
        return xt + g * acc, acc

    def resid_ex(xin, g):
        return [(xin, (TM, TN), lambda i, j, kk: (i, j)), (g, (1, TN), lambda i, j, kk: (0, j))]

    x2, mix = _matmul("mix_out", merged, w_o, "nn", SEQ, D_MODEL, D_MODEL, [F32, F32],
                      epilogue=resid_epi, extras=resid_ex(x, g1))
    h2 = _prenorm_fwd("prenorm2_fwd", x2, norm2_g, sh2, sc2)

    def relu2_epi(acc):
        r = jnp.maximum(acc, 0.0)
        return r * r, acc

    act, u = _matmul("ff1", h2, w_ff1, "nn", SEQ, D_FF, D_MODEL, [BF16, F32], tk=D_MODEL,
                     epilogue=relu2_epi)
    x3, y2 = _matmul("ff2", act, w_ff2, "nn", SEQ, D_MODEL, D_FF, [F32, F32],
                     epilogue=resid_epi, extras=resid_ex(x2, g2))

    d_x3, loss, d_gf = _loss_head(x3, target, norm_f_g)

    d_y2, d_g2 = _gate_bwd("ff_gate_bwd", d_x3, y2, g2)

    def relu2_bwd_epi(acc, ut):
        return (acc * (2.0 * jnp.maximum(ut, 0.0)),)

    recv = {}
    gw_ff2 = _matmul_tn_pair("ff2_dw", pos, act, d_y2, D_FF, D_MODEL, SEQ, D_FF // N_CHIPS,
                             tm=512, tn=1024, tk=1024)
    d_u, recv["w_ff2"] = _matmul(
        "ff2_dx", d_y2, w_ff2, "nt", SEQ, D_FF, D_MODEL, [BF16], epilogue=relu2_bwd_epi,
        extras=[(u, (TM, TN), lambda i, j, kk: (i, j))], carry=(gw_ff2, 0, SHARD["w_ff2"]))
    gw_ff1 = _matmul_tn_pair("ff1_dw", pos, h2, d_u, D_MODEL, D_FF, SEQ, D_MODEL,
                             tm=512, tn=1024, tk=1024)
    d_h2, recv["w_ff1"] = _matmul("ff1_dx", d_u, w_ff1, "nt", SEQ, D_MODEL, D_FF, [F32],
                                  carry=(gw_ff1, 1, SHARD["w_ff1"]))
    d_x2, d_sh2, d_sc2, d_n2g = _prenorm_bwd("prenorm2_bwd", [d_h2], x2, norm2_g, sc2, d_x3)

    d_mix, d_g1 = _gate_bwd("mix_gate_bwd", d_x2, mix, g1)
    gw_o = _matmul_tn_pair("mix_dw", pos, merged, d_mix, D_MODEL, D_MODEL, SEQ, D_MODEL // N_CHIPS,
                           tm=128, tn=1024, tk=2048)
    d_merged, recv["w_o"] = _matmul("mix_dx", d_mix, w_o, "nt", SEQ, D_MODEL, D_MODEL, [F32],
                                    carry=(gw_o, 0, SHARD["w_o"]))
    d_ret_out, d_att_out, d_gates = _merge_bwd(d_merged, gates, ret_out, att_out)

    gw_ret_out = _matmul_tn_pair("ret_out_dw", pos, gated, d_ret_out, RET_V_W, D_MODEL, SEQ,
                                 RET_V_W // N_CHIPS, tm=256, tn=1024, tk=1024)
    d_gated, recv["w_ret_out"] = _matmul("ret_out_dx", d_ret_out, w_ret_out, "nt", SEQ, RET_V_W, D_MODEL,
                                         [F32], carry=(gw_ret_out, 0, SHARD["w_ret_out"]))
    gw_att_out = _matmul_tn_pair("att_out_dw", pos, att_b, d_att_out, ATT_W, D_MODEL, SEQ, ATT_W,
                                 tm=256, tn=1024, tk=2048)
    d_att, recv["w_att_out"] = _matmul("att_out_dx", d_att_out, w_att_out, "nt", SEQ, ATT_W, D_MODEL,
                                       [F32], carry=(gw_att_out, 1, SHARD["w_att_out"]))

    d_ro, d_rg, d_gn_g, d_gn_b = _retpost_bwd(d_gated, ro, rg, gn_g, gn_b)
    d_rq, d_rk, d_rv = _retention_bwd(rqk, rv, states, d_ro, din, qd, kd, cd, cos, sin)

    d_att_b, dd = _att_bwd_pre(d_att, att)
    d_aqkv, dsbs = [], []
    for gi in range(3):
        da_p = _permute_rows(d_att_b, dils[gi])
        l_p = _permute_rows(lse, dils[gi])
        dd_p = _permute_rows(dd, dils[gi])
        dq, dk, dv, dsb = _att_bwd(gi, aqkv[gi], da_p, l_p, dd_p, bias, nbs[gi])
        d_aqkv.append(_unpermute_rows(jnp.concatenate([dq, dk, dv], axis=-1), dils[gi]))
        dsbs.append(dsb)
    d_rel_bias = _bias_grad(jnp.stack(dsbs), buckets)

    d_proj = jnp.concatenate([d_rq, d_rk, d_rv, d_rg] + d_aqkv + [d_gates], axis=-1)
    gw_in = _matmul_tn_pair("proj_dw", pos, h1, d_proj, D_MODEL, IN_COLS, SEQ, D_MODEL,
                            tm=512, tn=640, tk=2048)
    sems, gw_in, land, token = _ici_start("ici_start_w_in", gw_in, 1, SHARD["w_in"])
    d_h1 = _matmul("proj_dx", d_proj, w_in, "nt", SEQ, D_MODEL, IN_COLS, [F32], tn=1024, tk=1280,
                   epilogue=lambda acc, tok: (acc,),
                   extras=[(token, (8, 128), lambda i, j, kk: (0, 0))])[0]
    pending = (sems, land)

    big = dict(w_in=gw_in, w_ret_out=gw_ret_out, w_att_out=gw_att_out, w_o=gw_o, w_ff1=gw_ff1,
               w_ff2=gw_ff2)
    g_big = {n: _final_sum("final_" + n, pos, ax, big[n], recv[n], SHARD[n], after=[token])
             for n, ax in BIG[1:]}
    grad_x, d_sh1, d_sc1, d_n1g = _prenorm_bwd("prenorm1_bwd", [d_h1], x, norm1_g, sc1, d_x2,
                                               after=list(g_big.values()))
    d_mod = jnp.concatenate([d_sh1, d_sc1, d_g1, d_sh2, d_sc2, d_g2], axis=-1)
    small = dict(norm1_g=d_n1g, norm2_g=d_n2g, norm_f_g=d_gf, gn_g=d_gn_g, gn_b=d_gn_b,
                 rel_bias=d_rel_bias)
    return loss, grad_x, d_mod, small, g_big, (gw_in,) + pending


def _me():
    return lax.axis_index("x"), lax.axis_index("y"), lax.axis_index("c")


def _peer(x, y, c, mask):
    return (x ^ ((mask >> 2) & 1), y ^ ((mask >> 1) & 1), c ^ (mask & 1))


def _gather8(src_ref, dst_ref, send_sems, recv_sems):
    x, y, c = _me()
    me = 4 * x + 2 * y + c
    copies = []
    for mask in range(1, N_DEV):
        cp = pltpu.make_async_remote_copy(
            src_ref=src_ref, dst_ref=dst_ref.at[me], send_sem=send_sems.at[mask - 1],
            recv_sem=recv_sems.at[mask - 1], device_id=_peer(x, y, c, mask), device_id_type=MESH)
        cp.start()
        copies.append(cp)
    dst_ref[me] = src_ref[...]
    for cp in copies:
        cp.wait_recv()
    for cp in copies:
        cp.wait_send()


def _ada_fwd(c_in, w_ada, b_ada):
    ncol = ADA_COLS // N_CHIPS

    def body(c_ref, w_ref, b_ref, mod_ref, sc_ref, cbuf, cg, mbuf, mg, s1, r1, s2, r2):
        x, y, c = _me()
        me = 4 * x + 2 * y + c
        cv = c_ref[...]
        cbuf[...] = jnp.broadcast_to(cv * _sigmoid(cv), cbuf.shape)
        _gather8(cbuf, cg, s1, r1)
        rows = lax.broadcasted_iota(I32, (N_DEV, D_MODEL), 0)
        sc_all = jnp.zeros((N_DEV, D_MODEL), F32)
        for d in range(N_DEV):
            sc_all = jnp.where(rows == d, cg[d], sc_all)
        sc_ref[...] = sc_all
        mbuf[...] = jnp.dot(sc_all.astype(BF16), w_ref[...].astype(BF16), preferred_element_type=F32)
        _gather8(mbuf, mg, s2, r2)
        rowsel = lax.broadcasted_iota(I32, (N_DEV, ncol), 0) == me
        for k in range(N_CHIPS):
            blk = mg[2 * k]
            row = jnp.sum(jnp.where(rowsel, blk, 0.0), axis=0, keepdims=True)
            mod_ref[:, k * ncol:(k + 1) * ncol] = row + b_ref[:, k * ncol:(k + 1) * ncol]

    vm = pl.BlockSpec(memory_space=pltpu.VMEM)
    return pl.pallas_call(
        body, name="ada_fwd",
        in_specs=[vm, vm, vm], out_specs=[vm, vm],
        out_shape=[jax.ShapeDtypeStruct((1, ADA_COLS), F32), jax.ShapeDtypeStruct((N_DEV, D_MODEL), F32)],
        scratch_shapes=[
            pltpu.VMEM((8, D_MODEL), F32), pltpu.VMEM((N_DEV, 8, D_MODEL), F32),
            pltpu.VMEM((8, ncol), F32), pltpu.VMEM((N_DEV, 8, ncol), F32),
            pltpu.SemaphoreType.DMA((N_DEV - 1,)), pltpu.SemaphoreType.DMA((N_DEV - 1,)),
            pltpu.SemaphoreType.DMA((N_DEV - 1,)), pltpu.SemaphoreType.DMA((N_DEV - 1,)),
        ],
        compiler_params=pltpu.CompilerParams(vmem_limit_bytes=VMEM_LIMIT_V7X),
    )(c_in, w_ada, b_ada)


def _small_reduce(pack, sc_all):
    ncol = ADA_COLS // N_CHIPS

    def body(p_ref, sc_ref, tot_ref, gw_ref, pg, s1, r1):
        x, y, _ = _me()
        chip = 2 * x + y
        _gather8(p_ref, pg, s1, r1)
        tot = pg[0]
        for d in range(1, N_DEV):
            tot = tot + pg[d]
        tot_ref[...] = tot
        rows = lax.broadcasted_iota(I32, (N_DEV, ncol), 0)
        dmod = jnp.zeros((N_DEV, ncol), F32)
        for k in range(N_CHIPS):
            part = jnp.zeros((N_DEV, ncol), F32)
            for d in range(N_DEV):
                part = jnp.where(rows == d, pg[d, :, k * ncol:(k + 1) * ncol][0:1, :], part)
            dmod = jnp.where(chip == k, part, dmod)
        gw_ref[...] = lax.dot_general(sc_ref[...].astype(BF16), dmod.astype(BF16), _TN,
                                      preferred_element_type=F32)

    vm = pl.BlockSpec(memory_space=pltpu.VMEM)
    return pl.pallas_call(
        body, name="small_reduce",
        in_specs=[vm, vm], out_specs=[vm, vm],
        out_shape=[jax.ShapeDtypeStruct((8, ADA_COLS), F32), jax.ShapeDtypeStruct((D_MODEL, ncol), F32)],
        scratch_shapes=[pltpu.VMEM((N_DEV, 8, ADA_COLS), F32),
                        pltpu.SemaphoreType.DMA((N_DEV - 1,)), pltpu.SemaphoreType.DMA((N_DEV - 1,))],
        compiler_params=pltpu.CompilerParams(vmem_limit_bytes=VMEM_LIMIT_V7X),
    )(pack, sc_all)


BIG = (("w_in", 1), ("w_ret_out", 0), ("w_att_out", 1), ("w_o", 0), ("w_ff1", 1), ("w_ff2", 0))
SHARD = {"w_in": (D_MODEL, IN_COLS // N_CHIPS), "w_ret_out": (RET_V_W // N_CHIPS, D_MODEL),
         "w_att_out": (ATT_W, D_MODEL // N_CHIPS), "w_o": (D_MODEL // N_CHIPS, D_MODEL),
         "w_ff1": (D_MODEL, D_FF // N_CHIPS), "w_ff2": (D_FF // N_CHIPS, D_MODEL)}
_CHIP_FLIPS = ((1, 0), (0, 1), (1, 1))


def _region(ref, axis, chip, half, shard_shape):
    r, cw = shard_shape
    hr = r // 2
    if axis == 1:
        return ref.at[pl.ds(half * hr, hr), pl.ds(chip * cw, cw)]
    return ref.at[pl.ds(chip * r + half * hr, hr), :]


def _gather_weights(shards):
    nw = len(BIG)
    shapes = [s.shape for s in shards]
    full_shapes = [(r, N_CHIPS * cw) if ax == 1 else (N_CHIPS * r, cw)
                   for (r, cw), (_, ax) in zip(shapes, BIG)]

    def body(*refs):
        ins, outs = refs[:nw], refs[nw:2 * nw]
        own, from_ici, from_sib = refs[2 * nw:3 * nw], refs[3 * nw:4 * nw], refs[4 * nw:5 * nw]
        ld_sem, st_sem, s_ici, r_ici, s_d2d, r_d2d, st_a, st_b = refs[5 * nw:]
        x, y, c = _me()
        chip = 2 * x + y
        sib = (x, y, 1 - c)
        loads = [pltpu.make_async_copy(ins[i], own[i], ld_sem.at[i]) for i in range(nw)]
        for cp in loads:
            cp.start()
        pending, first = [], []
        for i, (_, ax) in enumerate(BIG):
            r, cw = shapes[i]
            hr = r // 2
            loads[i].wait()
            dst = outs[i].at[:, pl.ds(chip * cw, cw)] if ax == 1 else outs[i].at[pl.ds(chip * r, r), :]
            cp = pltpu.make_async_copy(own[i], dst, st_sem.at[i])
            cp.start()
            pending.append(cp)
            for j, (fx, fy) in enumerate(_CHIP_FLIPS):
                rc = pltpu.make_async_remote_copy(
                    src_ref=own[i].at[pl.ds(c * hr, hr), :], dst_ref=from_ici[i].at[j],
                    send_sem=s_ici.at[j * nw + i], recv_sem=r_ici.at[j * nw + i],
                    device_id=(x ^ fx, y ^ fy, c), device_id_type=MESH)
                rc.start()
                first.append((j, i, rc))
        passed = []
        for j, i, rc in first:
            fx, fy = _CHIP_FLIPS[j]
            src_chip = 2 * (x ^ fx) + (y ^ fy)
            ax = BIG[i][1]
            rc.wait_recv()
            fw = pltpu.make_async_remote_copy(
                src_ref=from_ici[i].at[j], dst_ref=from_sib[i].at[j], send_sem=s_d2d.at[j * nw + i],
                recv_sem=r_d2d.at[j * nw + i], device_id=sib, device_id_type=MESH)
            fw.start()
            passed.append((j, i, src_chip, fw))
            st = pltpu.make_async_copy(from_ici[i].at[j], _region(outs[i], ax, src_chip, c, shapes[i]),
                                       st_a.at[j * nw + i])
            st.start()
            pending.append(st)
        for j, i, src_chip, fw in passed:
            fw.wait_recv()
            st = pltpu.make_async_copy(from_sib[i].at[j],
                                       _region(outs[i], BIG[i][1], src_chip, 1 - c, shapes[i]),
                                       st_b.at[j * nw + i])
            st.start()
            pending.append(st)
        for _, _, rc in first:
            rc.wait_send()
        for _, _, _, fw in passed:
            fw.wait_send()
        for cp in pending:
            cp.wait()

    hbm = pl.BlockSpec(memory_space=pl.ANY)
    halves = [pltpu.VMEM((3, r // 2, cw), BF16) for r, cw in shapes]
    return pl.pallas_call(
        body, name="gather_weights",
        in_specs=[hbm] * nw, out_specs=[hbm] * nw,
        out_shape=[jax.ShapeDtypeStruct(fs, BF16) for fs in full_shapes],
        scratch_shapes=[pltpu.VMEM(sh, BF16) for sh in shapes] + halves + halves
        + [pltpu.SemaphoreType.DMA((nw,)), pltpu.SemaphoreType.DMA((nw,))]
        + [pltpu.SemaphoreType.DMA((3 * nw,))] * 6,
        compiler_params=pltpu.CompilerParams(vmem_limit_bytes=VMEM_LIMIT_V7X),
    )(*shards)


def _adam_update(w, g, m, v):
    mn = ADAM_B1 * m + (1.0 - ADAM_B1) * g
    vn = ADAM_B2 * v + (1.0 - ADAM_B2) * (g * g)
    m_hat = mn / (1.0 - ADAM_B1 ** ADAM_STEP)
    v_hat = vn / (1.0 - ADAM_B2 ** ADAM_STEP)
    return -ADAM_LR * (m_hat / (jnp.sqrt(v_hat) + ADAM_EPS) + ADAM_WD * w), mn, vn


def _final_sum(name, pos, axis, psum, recv, shard_shape, after=(), tr=128):
    r, cw = shard_shape
    hr = r // 2
    tr = min(tr, hr)
    nt = hr // tr
    n_after = len(after)

    def kern(pos_ref, p_ref, r_ref, *rest):
        g_ref, send_buf, land_buf, s_sem, r_sem = rest[n_after:]
        p, t = pl.program_id(0), pl.program_id(1)
        sib = _sibling()

        def copy(i):
            return pltpu.make_async_remote_copy(
                src_ref=send_buf.at[i], dst_ref=land_buf.at[i], send_sem=s_sem.at[i],
                recv_sem=r_sem.at[i], device_id=sib, device_id_type=MESH)

        @pl.when(p == 0)
        def _():
            tot = p_ref[...].astype(F32)
            for j in range(3):
                tot = tot + r_ref[j].astype(F32)
            send_buf[t] = tot
            copy(t).start()
            g_ref[...] = tot

        @pl.when(p == 1)
        def _():
            copy(t).wait_recv()
            g_ref[...] = land_buf[t]

        @pl.when(jnp.logical_and(p == 1, t == nt - 1))
        def _():
            for i in range(nt):
                copy(i).wait_send()

    def shard_rows(p, t, pos_ref):
        return (jnp.where(p == 0, pos_ref[0], 1 - pos_ref[0]) * nt + t, 0)

    def own_part(p, t, pos_ref):
        tt = jnp.where(p == 0, t, nt - 1)
        return (tt, pos_ref[1]) if axis == 1 else (pos_ref[1] * nt + tt, 0)

    grid_spec = pltpu.PrefetchScalarGridSpec(
        num_scalar_prefetch=1, grid=(2, nt),
        in_specs=[pl.BlockSpec((tr, cw), own_part),
                  pl.BlockSpec((3, tr, cw), lambda p, t, pos_ref: (0, jnp.where(p == 0, t, nt - 1), 0))]
        + [pl.BlockSpec(memory_space=pl.ANY)] * n_after,
        out_specs=pl.BlockSpec((tr, cw), shard_rows),
        scratch_shapes=[pltpu.VMEM((nt, tr, cw), F32), pltpu.VMEM((nt, tr, cw), F32),
                        pltpu.SemaphoreType.DMA((nt,)), pltpu.SemaphoreType.DMA((nt,))])
    return pl.pallas_call(
        kern, name=name, grid_spec=grid_spec, out_shape=jax.ShapeDtypeStruct((r, cw), F32),
        compiler_params=_cparams(("arbitrary", "arbitrary")),
    )(pos, psum, recv, *after)


def _adamw(name, w, g, m, v):
    r, cw = w.shape
    tr = min(r, 128)

    def kern(w_ref, g_ref, m_ref, v_ref, go_ref, d_ref, nm_ref, nv_ref):
        gv = g_ref[...]
        go_ref[...] = gv
        d_ref[...], nm_ref[...], nv_ref[...] = _adam_update(w_ref[...], gv, m_ref[...], v_ref[...])

    spec = pl.BlockSpec((tr, cw), lambda i: (i, 0))
    return pl.pallas_call(
        kern, name=name, grid=(r // tr,), in_specs=[spec] * 4, out_specs=[spec] * 4,
        out_shape=[jax.ShapeDtypeStruct((r, cw), F32)] * 4, compiler_params=_cparams(("parallel",)),
    )(w, g, m, v)


_PACK_W = ADA_COLS
_NB = REL_BUCKETS * N_ATT_HEADS
_SMALL_SLOTS = {
    "b_ada": (0, 0, ADA_COLS),
    "norm1_g": (1, 0, D_MODEL), "norm2_g": (1, D_MODEL, D_MODEL), "norm_f_g": (1, 2 * D_MODEL, D_MODEL),
    "ret_gn_g": (1, 3 * D_MODEL, RET_V_W),
    "ret_gn_b": (2, 0, RET_V_W), "rel_bias": (2, RET_V_W, _NB), "loss": (2, RET_V_W + 512, 128),
}


def _pack_small(vals):
    rows = []
    for r in range(8):
        items = sorted([(off, n) for n, (rr, off, _) in _SMALL_SLOTS.items() if rr == r and n in vals])
        parts, pos = [], 0
        for off, n in items:
            if off > pos:
                parts.append(jnp.zeros((1, off - pos), F32))
            parts.append(vals[n].reshape(1, -1).astype(F32))
            pos = off + _SMALL_SLOTS[n][2]
        if pos < _PACK_W:
            parts.append(jnp.zeros((1, _PACK_W - pos), F32))
        rows.append(jnp.concatenate(parts, axis=-1))
    return jnp.concatenate(rows, axis=0)


def _unpack_small(pack, name):
    r, off, wd = _SMALL_SLOTS[name]
    return pack[r:r + 1, off:off + wd]


def kernel(x, c, w_ada, b_ada, norm1_g, w_in, rel_bias, ret_gn_g, ret_gn_b, w_ret_out, w_att_out, w_o, norm2_g, w_ff1, w_ff2, norm_f_g, loss_target, m_w_ada, m_b_ada, m_norm1_g, m_w_in, m_rel_bias, m_ret_gn_g, m_ret_gn_b, m_w_ret_out, m_w_att_out, m_w_o, m_norm2_g, m_w_ff1, m_w_ff2, m_norm_f_g, v_w_ada, v_b_ada, v_norm1_g, v_w_in, v_rel_bias, v_ret_gn_g, v_ret_gn_b, v_w_ret_out, v_w_att_out, v_w_o, v_norm2_g, v_w_ff1, v_w_ff2, v_norm_f_g):
    given = dict(locals())
    big_names = [n for n, _ in BIG]
    shard_w = {n: given[n][0] for n in big_names}
    assert all(shard_w[n].shape == SHARD[n] for n in big_names)

    full = _gather_weights([shard_w[n].astype(BF16) for n in big_names])
    full = dict(zip(big_names, full))
    mod, sc_all = _ada_fwd(c, w_ada[0], b_ada)
    pos = _where_am_i()

    loss, grad_x, d_mod, small, g_big, pending = _local_step(
        pos, x[0], loss_target[0], mod, norm1_g, norm2_g, norm_f_g.reshape(1, -1), rel_bias, ret_gn_g,
        ret_gn_b, full["w_in"], full["w_ret_out"], full["w_att_out"], full["w_o"], full["w_ff1"],
        full["w_ff2"])

    pack_g = _pack_small(dict(b_ada=d_mod, norm1_g=small["norm1_g"], norm2_g=small["norm2_g"],
                              norm_f_g=small["norm_f_g"], ret_gn_g=small["gn_g"], ret_gn_b=small["gn_b"],
                              rel_bias=small["rel_bias"], loss=loss))
    tot, g_w_ada = _small_reduce(pack_g, sc_all)

    small_names = ["b_ada", "norm1_g", "rel_bias", "ret_gn_g", "ret_gn_b", "norm2_g", "norm_f_g"]
    pack_w = _pack_small({n: given[n] for n in small_names})
    pack_m = _pack_small({n: given["m_" + n] for n in small_names})
    pack_v = _pack_small({n: given["v_" + n] for n in small_names})
    _, sd, sm, sv = _adamw("adamw_small", pack_w, tot, pack_m, pack_v)

    grads, deltas, new_m, new_v = {}, {}, {}, {}
    for n in small_names:
        shp = given[n].shape
        grads[n] = _unpack_small(tot, n).reshape(shp)
        deltas[n] = _unpack_small(sd, n).reshape(shp)
        new_m[n] = _unpack_small(sm, n).reshape(shp)
        new_v[n] = _unpack_small(sv, n).reshape(shp)
    g_big["w_ada"] = g_w_ada
    for n in ["w_ada"] + big_names[1:] + big_names[:1]:
        if n == "w_in":
            gw_in, sems, land = pending
            done = [tot, sd] + [deltas[k] for k in ["w_ada"] + big_names[1:]]
            gw_in, got = _ici_wait("ici_wait_w_in", sems, gw_in, land, 1, SHARD[n], done)
            g_big[n] = _final_sum("final_w_in", pos, 1, gw_in, got, SHARD[n])
        g, d, nm, nv = _adamw("adamw_" + n, given[n][0], g_big[n], given["m_" + n][0], given["v_" + n][0])
        grads[n], deltas[n], new_m[n], new_v[n] = g[None], d[None], nm[None], nv[None]

    order = ["w_ada", "b_ada", "norm1_g", "w_in", "rel_bias", "ret_gn_g", "ret_gn_b", "w_ret_out",
             "w_att_out", "w_o", "norm2_g", "w_ff1", "w_ff2", "norm_f_g"]
    loss_out = _unpack_small(tot, "loss")[0, 0]
    return (loss_out, grad_x[None], *[grads[n] for n in order], *[deltas[n] for n in order],
            *[new_m[n] for n in order], *[new_v[n] for n in order])
```

```python
import functools
import math

import jax
import jax.numpy as jnp
from jax import lax
from jax.experimental import pallas as pl
from jax.experimental.pallas import tpu as pltpu

F32 = jnp.float32
BF16 = jnp.bfloat16
I32 = jnp.int32

SEQ = 2048
D_MODEL = 1024
RET_HEADS = 4
RET_DK = 256
RET_DV = 512
RET_CHUNK = 128
RET_QK_W = RET_HEADS * RET_DK
RET_V_W = RET_HEADS * RET_DV
ATT_GROUPS = ((128, 1), (512, 4), (2048, 16))
ATT_HPG = 4
ATT_DH = 128
ATT_W = ATT_HPG * ATT_DH
ATT_BLK = 128
N_BLK = SEQ // ATT_BLK
REL_BUCKETS = 32
REL_MAX_DIST = 2048
N_ATT_HEADS = 12
D_FF = 4 * D_MODEL
RMS_EPS = 1e-6
GN_EPS = 1e-5
ROPE_BASE = 10000.0
IN_COLS = 2 * RET_QK_W + 2 * RET_V_W + 9 * ATT_W + 2 * D_MODEL
OFF_Q, OFF_K, OFF_V, OFF_G = 0, RET_QK_W, 2 * RET_QK_W, 2 * RET_QK_W + RET_V_W
OFF_ATT = 2 * RET_QK_W + 2 * RET_V_W
OFF_GATE = OFF_ATT + 9 * ATT_W
N_CHIPS = 4
N_DEV = 8
ADA_COLS = 6 * D_MODEL

ADAM_LR = 0.001
ADAM_B1 = 0.9
ADAM_B2 = 0.999
ADAM_EPS = 1e-08
ADAM_WD = 0.01
ADAM_STEP = 10

VMEM_LIMIT_V7X = 56 * 1024 * 1024
MESH = pl.DeviceIdType.MESH


def _cparams(sem):
    return pltpu.CompilerParams(dimension_semantics=sem, vmem_limit_bytes=VMEM_LIMIT_V7X)


def _sigmoid(v):
    return 1.0 / (1.0 + jnp.exp(-v))


def _rowmap(name, body, row_ins, bcast_ins, row_outs, sum_outs=(), tm=256, after=()):
    m = row_ins[0].shape[0]
    n_in = len(row_ins) + len(bcast_ins)
    n_ro = len(row_outs)

    def kern(*refs):
        vals = [r[...] for r in refs[:n_in]]
        res = body(*vals)
        if not isinstance(res, (tuple, list)):
            res = (res,)
        outs = refs[n_in + len(after):]
        for r, v in zip(outs[:n_ro], res[:n_ro]):
            r[...] = v.astype(r.dtype)
        if sum_outs:
            @pl.when(pl.program_id(0) == 0)
            def _():
                for r in outs[n_ro:]:
                    r[...] = jnp.zeros_like(r)
            for r, v in zip(outs[n_ro:], res[n_ro:]):
                r[...] += v

    in_specs = [pl.BlockSpec((tm, a.shape[1]), lambda i: (i, 0)) for a in row_ins]
    in_specs += [pl.BlockSpec(a.shape, lambda i: (0, 0)) for a in bcast_ins]
    in_specs += [pl.BlockSpec(memory_space=pl.ANY)] * len(after)
    out_specs = [pl.BlockSpec((tm, n), lambda i: (i, 0)) for n, _ in row_outs]
    out_specs += [pl.BlockSpec((1, n), lambda i: (0, 0)) for n in sum_outs]
    out_shape = [jax.ShapeDtypeStruct((m, n), dt) for n, dt in row_outs]
    out_shape += [jax.ShapeDtypeStruct((1, n), F32) for n in sum_outs]
    return pl.pallas_call(
        kern, name=name, grid=(m // tm,), in_specs=in_specs, out_specs=out_specs,
        out_shape=out_shape, compiler_params=_cparams(("arbitrary",)),
    )(*row_ins, *bcast_ins, *after)


TM, TN = 1024, 1024


def _matmul(name, a, b, kind, m, n, k, outs, *, b_off=0, tm=TM, tn=TN, tk=1024,
            epilogue=None, extras=(), carry=None, after=()):
    tm, tn, tk = min(tm, m), min(tn, n), min(tk, k)
    nk = k // tk
    if kind == "nn":
        a_spec = pl.BlockSpec((tm, tk), lambda i, j, kk: (i, kk))
        b_spec = pl.BlockSpec((tk, tn), lambda i, j, kk: (kk, b_off // tn + j))
        dn = (((1,), (0,)), ((), ()))
    elif kind == "nt":
        a_spec = pl.BlockSpec((tm, tk), lambda i, j, kk: (i, kk))
        b_spec = pl.BlockSpec((tn, tk), lambda i, j, kk: (j, b_off // tk + kk))
        dn = (((1,), (1,)), ((), ()))
    else:
        a_spec = pl.BlockSpec((tk, tm), lambda i, j, kk: (kk, i))
        b_spec = pl.BlockSpec((tk, tn), lambda i, j, kk: (kk, j))
        dn = (((0,), (0,)), ((), ()))
    n_ex, n_out = len(extras), len(outs)
    if epilogue is None:
        epilogue = lambda acc: (acc,)

    def finish(acc, ex_refs, out_refs):
        res = epilogue(acc, *[r[...] for r in ex_refs])
        for r, v in zip(out_refs, res):
            r[...] = v.astype(r.dtype)

    n_c = 0 if carry is None else 1
    n_in = n_ex + n_c + len(after)
    ni, nj = m // tm, n // tn

    def kern(a_ref, b_ref, *rest):
        ex_refs = rest[:n_ex]
        out_refs = rest[n_in:n_in + n_out]
        scratch = rest[n_in + n_out + n_c:]
        i, j, kk = pl.program_id(0), pl.program_id(1), pl.program_id(2)
        if carry is not None:
            copies = lambda: _ici_copies(rest[n_ex], rest[n_in + n_out], scratch[-2], scratch[-1],
                                         carry[1], carry[2])

            @pl.when(jnp.logical_and(jnp.logical_and(i == 0, j == 0), kk == 0))
            def _():
                for cp in copies():
                    cp.start()

        part = lax.dot_general(a_ref[...], b_ref[...], dn, preferred_element_type=F32)
        if nk == 1:
            finish(part, ex_refs, out_refs)
        else:
            acc_ref = scratch[0]

            @pl.when(kk == 0)
            def _():
                acc_ref[...] = part

            @pl.when(kk > 0)
            def _():
                acc_ref[...] += part

            @pl.when(kk == nk - 1)
            def _():
                finish(acc_ref[...], ex_refs, out_refs)

        if carry is not None:
            @pl.when(jnp.logical_and(jnp.logical_and(i == ni - 1, j == nj - 1), kk == nk - 1))
            def _():
                for cp in copies():
                    cp.wait_recv()
                for cp in copies():
                    cp.wait_send()

    hbm = pl.BlockSpec(memory_space=pl.ANY)
    in_specs = [a_spec, b_spec] + [pl.BlockSpec(bs, im) for _, bs, im in extras]
    in_specs += [hbm] * (n_c + len(after))
    out_specs = [pl.BlockSpec((tm, tn), lambda i, j, kk: (i, j)) for _ in outs] + [hbm] * n_c
    out_shape = [jax.ShapeDtypeStruct((m, n), dt) for dt in outs]
    scratch_shapes = [] if nk == 1 else [pltpu.VMEM((tm, tn), F32)]
    operands = [a, b] + [e[0] for e in extras]
    if carry is not None:
        r, cw = carry[2]
        out_shape.append(jax.ShapeDtypeStruct((3, r // 2, cw), BF16))
        scratch_shapes += [pltpu.SemaphoreType.DMA((3,)), pltpu.SemaphoreType.DMA((3,))]
        operands.append(carry[0])
    operands += list(after)
    sem = ("arbitrary",) * 3 if carry is not None else ("parallel", "parallel", "arbitrary")
    return pl.pallas_call(
        kern, name=name, grid=(ni, nj, nk), in_specs=in_specs, out_specs=out_specs,
        out_shape=out_shape, scratch_shapes=scratch_shapes, compiler_params=_cparams(sem),
    )(*operands)


def _ici_copies(psum_ref, recv_ref, s_sem, r_sem, axis, shard_shape):
    x, y, c = _me()
    hr, cw = shard_shape[0] // 2, shard_shape[1]
    pick = lambda sems, j: sems[j] if isinstance(sems, (list, tuple)) else sems.at[j]
    copies = []
    for j, (fx, fy) in enumerate(_CHIP_FLIPS):
        chip = 2 * (x ^ fx) + (y ^ fy)
        src = psum_ref.at[:, pl.ds(chip * cw, cw)] if axis == 1 else psum_ref.at[pl.ds(chip * hr, hr), :]
        copies.append(pltpu.make_async_remote_copy(
            src_ref=src, dst_ref=recv_ref.at[j], send_sem=pick(s_sem, j), recv_sem=pick(r_sem, j),
            device_id=(x ^ fx, y ^ fy, c), device_id_type=MESH))
    return copies


_HBM_SPEC = pl.BlockSpec(memory_space=pltpu.HBM)
_SEM_SPEC = pl.BlockSpec(memory_space=pltpu.SEMAPHORE)


def _ici_start(name, psum, axis, shard_shape):
    r, cw = shard_shape
    land = lax.empty((3, r // 2, cw), BF16)

    def body(p_ref, land_ref, s0, s1, s2, r0, r1, r2, p_thru, land_thru, token):
        for cp in _ici_copies(p_ref, land_ref, [s0, s1, s2], [r0, r1, r2], axis, shard_shape):
            cp.start()
        token[...] = jnp.zeros_like(token)

    sem = pltpu.SemaphoreType.DMA(())
    res = pl.pallas_call(
        body, name=name,
        out_shape=(sem,) * 6 + (pltpu.HBM(psum.shape, BF16), pltpu.HBM(land.shape, BF16),
                                jax.ShapeDtypeStruct((8, 128), F32)),
        in_specs=(_HBM_SPEC, _HBM_SPEC),
        out_specs=(_SEM_SPEC,) * 6 + (_HBM_SPEC, _HBM_SPEC, pl.BlockSpec(memory_space=pltpu.VMEM)),
        input_output_aliases={0: 6, 1: 7},
        compiler_params=pltpu.CompilerParams(has_side_effects=pltpu.SideEffectType.DATAFLOW_SIDE_EFFECTING),
    )(pltpu.with_memory_space_constraint(psum, pltpu.HBM),
      pltpu.with_memory_space_constraint(land, pltpu.HBM))
    return res[:6], res[6], res[7], res[8]


def _ici_wait(name, sems, p_thru, land_thru, axis, shard_shape, after):
    n_after = len(after)

    def body(p_ref, land_ref, s0, s1, s2, r0, r1, r2, *rest):
        for cp in _ici_copies(p_ref, land_ref, [s0, s1, s2], [r0, r1, r2], axis, shard_shape):
            cp.wait_send()
            cp.wait_recv()

    return pl.pallas_call(
        body, name=name,
        out_shape=(pltpu.HBM(p_thru.shape, BF16), pltpu.HBM(land_thru.shape, BF16)),
        in_specs=(_HBM_SPEC, _HBM_SPEC) + (_SEM_SPEC,) * 6 + (pl.BlockSpec(memory_space=pl.ANY),) * n_after,
        out_specs=(_HBM_SPEC, _HBM_SPEC), input_output_aliases={0: 0, 1: 1},
        compiler_params=pltpu.CompilerParams(has_side_effects=pltpu.SideEffectType.DATAFLOW_SIDE_EFFECTING),
    )(p_thru, land_thru, *sems, *after)


def _where_am_i():
    x, y, c = _me()
    return jnp.stack([c, 2 * x + y]).astype(I32)


def _sibling():
    x, y, c = _me()
    return (x, y, 1 - c)


def _matmul_tn_pair(name, pos, a, b, m, n, k, shard_rows, *, tm, tn, tk):
    hr = shard_rows // 2
    tm, tn, tk = min(tm, hr), min(tn, n), min(tk, k)
    tph = hr // tm
    nt, nj, nk = (m // 2) // tm, n // tn, k // tk
    n_tiles = nt * nj

    def row_block(p, t, pos_ref):
        half = jnp.where(p == 0, 1 - pos_ref[0], pos_ref[0])
        return (t // tph) * (2 * tph) + half * tph + t % tph

    def kern(pos_ref, a_ref, b_ref, o_ref, acc_ref, send_buf, land_buf, s_sem, r_sem):
        p, t, j, kk = pl.program_id(0), pl.program_id(1), pl.program_id(2), pl.program_id(3)
        idx = t * nj + j
        sib = _sibling()

        def copy(i):
            return pltpu.make_async_remote_copy(
                src_ref=send_buf.at[i], dst_ref=land_buf.at[i], send_sem=s_sem.at[i],
                recv_sem=r_sem.at[i], device_id=sib, device_id_type=MESH)

        part = lax.dot_general(a_ref[...], b_ref[...], _TN, preferred_element_type=F32)

        @pl.when(kk == 0)
        def _():
            acc_ref[...] = part

        @pl.when(kk > 0)
        def _():
            acc_ref[...] += part

        @pl.when(jnp.logical_and(kk == nk - 1, p == 0))
        def _():
            send_buf[idx] = acc_ref[...].astype(BF16)
            copy(idx).start()

        @pl.when(jnp.logical_and(kk == nk - 1, p == 1))
        def _():
            copy(idx).wait_recv()
            o_ref[...] = (acc_ref[...] + land_buf[idx].astype(F32)).astype(BF16)

        @pl.when(jnp.logical_and(jnp.logical_and(p == 1, idx == n_tiles - 1), kk == nk - 1))
        def _():
            for i in range(n_tiles):
                copy(i).wait_send()

    grid_spec = pltpu.PrefetchScalarGridSpec(
        num_scalar_prefetch=1, grid=(2, nt, nj, nk),
        in_specs=[pl.BlockSpec((tk, tm), lambda p, t, j, kk, pos_ref: (kk, row_block(p, t, pos_ref))),
                  pl.BlockSpec((tk, tn), lambda p, t, j, kk, pos_ref: (kk, j))],
        out_specs=pl.BlockSpec((tm, tn), lambda p, t, j, kk, pos_ref: (p * t, p * j)),
        scratch_shapes=[pltpu.VMEM((tm, tn), F32), pltpu.VMEM((n_tiles, tm, tn), BF16),
                        pltpu.VMEM((n_tiles, tm, tn), BF16),
                        pltpu.SemaphoreType.DMA((n_tiles,)), pltpu.SemaphoreType.DMA((n_tiles,))])
    return pl.pallas_call(
        kern, name=name, grid_spec=grid_spec, out_shape=jax.ShapeDtypeStruct((m // 2, n), BF16),
        compiler_params=_cparams(("arbitrary",) * 4),
    )(pos, a, b)


def _rope_tables():
    half = RET_DK // 2
    inv = ROPE_BASE ** (-jnp.arange(half, dtype=F32) / half)
    ang = jnp.arange(SEQ).astype(F32)[:, None] * inv[None, :]
    return jnp.cos(ang), jnp.sin(ang)


def _decay_tables():
    c = RET_CHUNK
    log_g = jnp.log1p(-(2.0 ** (-5.0 - jnp.arange(RET_HEADS, dtype=F32))))
    idx = jnp.arange(c, dtype=F32)
    rel = idx[:, None] - idx[None, :]
    din = jnp.where(rel >= 0, jnp.exp(log_g[:, None, None] * jnp.maximum(rel, 0.0)), 0.0)
    qd = jnp.exp(log_g[:, None] * (idx + 1.0))[:, :, None]
    kd = jnp.exp(log_g[:, None] * (c - 1.0 - idx))[:, :, None]
    cd = jnp.exp(log_g * c)
    return din, qd, kd, cd


def _t5_bucket(dist):
    max_exact = REL_BUCKETS // 2
    d_f = jnp.maximum(dist, 1).astype(F32)
    large = max_exact + (jnp.log(d_f / max_exact) / math.log(REL_MAX_DIST / max_exact)
                         * (REL_BUCKETS - max_exact)).astype(I32)
    large = jnp.minimum(large, REL_BUCKETS - 1)
    return jnp.where(dist < max_exact, dist, large)


def _bucket_tables():
    qi = jnp.arange(ATT_BLK)[:, None]
    kj = jnp.arange(2 * ATT_BLK)[None, :]
    dist = jnp.clip(ATT_BLK + qi - kj, 0, ATT_BLK)
    return jnp.stack([_t5_bucket(dist * dil) for _, dil in ATT_GROUPS]).astype(I32)


def _permute_rows(t, dil):
    if dil == 1:
        return t
    s, w = t.shape
    return t.reshape(s // dil, dil, w).transpose(1, 0, 2).reshape(s, w)


def _unpermute_rows(t, dil):
    if dil == 1:
        return t
    s, w = t.shape
    return t.reshape(dil, s // dil, w).transpose(1, 0, 2).reshape(s, w)


def _retention_fwd(rqk, rv, din, qd, kd, cd):
    nc = SEQ // RET_CHUNK
    c, dk, dv = RET_CHUNK, RET_DK, RET_DV

    def kern(q_ref, k_ref, v_ref, din_ref, qd_ref, kd_ref, cd_ref, o_ref, st_ref, state):
        h, n = pl.program_id(0), pl.program_id(1)

        @pl.when(n == 0)
        def _():
            state[...] = jnp.zeros_like(state)

        q, k, v = q_ref[...], k_ref[...], v_ref[...]
        s_b = state[...].astype(BF16)
        st_ref[...] = s_b
        a = lax.dot_general(q, k, (((1,), (1,)), ((), ())), preferred_element_type=F32) * din_ref[...]
        o = jnp.dot(a.astype(BF16), v, preferred_element_type=F32)
        o += jnp.dot(q, s_b, preferred_element_type=F32) * qd_ref[...]
        o_ref[...] = o
        kk = (k.astype(F32) * kd_ref[...]).astype(BF16)
        upd = lax.dot_general(kk, v, (((0,), (0,)), ((), ())), preferred_element_type=F32)
        state[...] = state[...] * cd_ref[h] + upd

    return pl.pallas_call(
        kern, name="retention_fwd", grid=(RET_HEADS, nc),
        in_specs=[
            pl.BlockSpec((c, dk), lambda h, n: (n, h)),
            pl.BlockSpec((c, dk), lambda h, n: (n, RET_HEADS + h)),
            pl.BlockSpec((c, dv), lambda h, n: (n, h)),
            pl.BlockSpec((None, c, c), lambda h, n: (h, 0, 0)),
            pl.BlockSpec((None, c, 1), lambda h, n: (h, 0, 0)),
            pl.BlockSpec((None, c, 1), lambda h, n: (h, 0, 0)),
            pl.BlockSpec(memory_space=pltpu.SMEM),
        ],
        out_specs=[
            pl.BlockSpec((c, dv), lambda h, n: (n, h)),
            pl.BlockSpec((None, None, dk, dv), lambda h, n: (h, n, 0, 0)),
        ],
        out_shape=[
            jax.ShapeDtypeStruct((SEQ, RET_V_W), F32),
            jax.ShapeDtypeStruct((RET_HEADS, nc, dk, dv), BF16),
        ],
        scratch_shapes=[pltpu.VMEM((dk, dv), F32)],
        compiler_params=_cparams(("arbitrary", "arbitrary")),
    )(rqk, rqk, rv, din, qd, kd, cd)


def _retention_bwd(rqk, rv, states, d_ro, din, qd, kd, cd, cos, sin):
    nc = SEQ // RET_CHUNK
    c, dk, dv = RET_CHUNK, RET_DK, RET_DV
    half = dk // 2
    last = nc - 1

    def unrot(g, cs, sn):
        g1, g2 = g[:, :half], g[:, half:]
        return jnp.concatenate([g1 * cs + g2 * sn, g2 * cs - g1 * sn], axis=-1)

    def kern(q_ref, k_ref, v_ref, st_ref, do_ref, din_ref, qd_ref, kd_ref, cd_ref, cos_ref, sin_ref,
             dq_ref, dk_ref, dv_ref, dstate):
        h, step = pl.program_id(0), pl.program_id(1)

        @pl.when(step == 0)
        def _():
            dstate[...] = jnp.zeros_like(dstate)

        q, k, v, s_b = q_ref[...], k_ref[...], v_ref[...], st_ref[...]
        d_o = do_ref[...]
        d_ob = d_o.astype(BF16)
        d_oq = (d_o * qd_ref[...]).astype(BF16)
        ds_b = dstate[...].astype(BF16)
        din_m = din_ref[...]
        nt = (((1,), (1,)), ((), ()))
        tn = (((0,), (0,)), ((), ()))
        a_b = (lax.dot_general(q, k, nt, preferred_element_type=F32) * din_m).astype(BF16)
        kk = (k.astype(F32) * kd_ref[...]).astype(BF16)
        d_v = lax.dot_general(a_b, d_ob, tn, preferred_element_type=F32)
        d_v += jnp.dot(kk, ds_b, preferred_element_type=F32)
        d_a = (lax.dot_general(d_ob, v, nt, preferred_element_type=F32) * din_m).astype(BF16)
        d_q = jnp.dot(d_a, k, preferred_element_type=F32)
        d_q += lax.dot_general(d_oq, s_b, nt, preferred_element_type=F32)
        d_k = lax.dot_general(d_a, q, tn, preferred_element_type=F32)
        d_k += lax.dot_general(v, ds_b, nt, preferred_element_type=F32) * kd_ref[...]
        dstate[...] = dstate[...] * cd_ref[h] + lax.dot_general(q, d_oq, tn, preferred_element_type=F32)
        cs, sn = cos_ref[...], sin_ref[...]
        dq_ref[...] = unrot(d_q, cs, sn).astype(BF16)
        dk_ref[...] = (unrot(d_k, cs, sn) * (RET_DK ** -0.5)).astype(BF16)
        dv_ref[...] = d_v.astype(BF16)

    return pl.pallas_call(
        kern, name="retention_bwd", grid=(RET_HEADS, nc),
        in_specs=[
            pl.BlockSpec((c, dk), lambda h, n: (last - n, h)),
            pl.BlockSpec((c, dk), lambda h, n: (last - n, RET_HEADS + h)),
            pl.BlockSpec((c, dv), lambda h, n: (last - n, h)),
            pl.BlockSpec((None, None, dk, dv), lambda h, n: (h, last - n, 0, 0)),
            pl.BlockSpec((c, dv), lambda h, n: (last - n, h)),
            pl.BlockSpec((None, c, c), lambda h, n: (h, 0, 0)),
            pl.BlockSpec((None, c, 1), lambda h, n: (h, 0, 0)),
            pl.BlockSpec((None, c, 1), lambda h, n: (h, 0, 0)),
            pl.BlockSpec(memory_space=pltpu.SMEM),
            pl.BlockSpec((c, half), lambda h, n: (last - n, 0)),
            pl.BlockSpec((c, half), lambda h, n: (last - n, 0)),
        ],
        out_specs=[
            pl.BlockSpec((c, dk), lambda h, n: (last - n, h)),
            pl.BlockSpec((c, dk), lambda h, n: (last - n, h)),
            pl.BlockSpec((c, dv), lambda h, n: (last - n, h)),
        ],
        out_shape=[
            jax.ShapeDtypeStruct((SEQ, RET_QK_W), BF16),
            jax.ShapeDtypeStruct((SEQ, RET_QK_W), BF16),
            jax.ShapeDtypeStruct((SEQ, RET_V_W), BF16),
        ],
        scratch_shapes=[pltpu.VMEM((dk, dv), F32)],
        compiler_params=_cparams(("arbitrary", "arbitrary")),
    )(rqk, rqk, rv, states, d_ro, din, qd, kd, cd, cos, sin)


def _bias_build(rel_bias, buckets):
    ng = len(ATT_GROUPS)

    def kern(tab_ref, bkt_ref, o_ref):
        g, h = pl.program_id(0), pl.program_id(1)
        bkt = bkt_ref[...]
        acc = jnp.zeros(bkt.shape, F32)
        for b in range(REL_BUCKETS):
            acc = jnp.where(bkt == b, tab_ref[b, g * ATT_HPG + h], acc)
        o_ref[...] = acc

    return pl.pallas_call(
        kern, name="bias_build", grid=(ng, ATT_HPG),
        in_specs=[pl.BlockSpec(memory_space=pltpu.SMEM),
                  pl.BlockSpec((None, ATT_BLK, 2 * ATT_BLK), lambda g, h: (g, 0, 0))],
        out_specs=pl.BlockSpec((None, None, ATT_BLK, 2 * ATT_BLK), lambda g, h: (g, h, 0, 0)),
        out_shape=jax.ShapeDtypeStruct((ng, ATT_HPG, ATT_BLK, 2 * ATT_BLK), F32),
        compiler_params=_cparams(("arbitrary", "arbitrary")),
    )(rel_bias, buckets)


def _bias_grad(dsb, buckets):
    ng = len(ATT_GROUPS)

    def kern(ds_ref, bkt_ref, o_ref):
        g, h = pl.program_id(0), pl.program_id(1)
        bkt, ds = bkt_ref[...], ds_ref[...]
        for b in range(REL_BUCKETS):
            o_ref[b, g * ATT_HPG + h] = jnp.sum(jnp.where(bkt == b, ds, 0.0))

    return pl.pallas_call(
        kern, name="bias_grad", grid=(ng, ATT_HPG),
        in_specs=[pl.BlockSpec((None, None, ATT_BLK, 2 * ATT_BLK), lambda g, h: (g, h, 0, 0)),
                  pl.BlockSpec((None, ATT_BLK, 2 * ATT_BLK), lambda g, h: (g, 0, 0))],
        out_specs=pl.BlockSpec(memory_space=pltpu.SMEM),
        out_shape=jax.ShapeDtypeStruct((REL_BUCKETS, N_ATT_HEADS), F32),
        compiler_params=_cparams(("arbitrary", "arbitrary")),
    )(dsb, buckets)


_NT = (((1,), (1,)), ((), ()))
_TN = (((0,), (0,)), ((), ()))
_ATT_SCALE = ATT_DH ** -0.5


_PAD_ROWS = SEQ + ATT_BLK


def _window_mask(has_prev):
    qi = lax.broadcasted_iota(I32, (ATT_BLK, 2 * ATT_BLK), 0)
    kj = lax.broadcasted_iota(I32, (ATT_BLK, 2 * ATT_BLK), 1)
    prev_ok = jnp.logical_and(jnp.logical_and(kj < ATT_BLK, kj >= qi), has_prev)
    return jnp.logical_or(prev_ok, jnp.logical_and(kj >= ATT_BLK, qi >= kj - ATT_BLK))


def _head_specs(col0):
    return pl.BlockSpec((SEQ, ATT_DH), lambda h: (0, col0 + h))


def _att_fwd(gi, qkv, bias, nb):
    blk, dh = ATT_BLK, ATT_DH

    def kern(q_ref, k_ref, v_ref, b_ref, o_ref, l_ref, kpad, vpad):
        zero = jnp.zeros((blk, dh), BF16)
        kpad[0:blk, :] = zero
        vpad[0:blk, :] = zero
        kpad[blk:, :] = k_ref[...]
        vpad[blk:, :] = v_ref[...]
        bias_m = b_ref[...]

        def body(b, carry):
            r0 = pl.multiple_of(b * blk, blk)
            q = q_ref[pl.ds(r0, blk), :]
            kw = kpad[pl.ds(r0, 2 * blk), :]
            vw = vpad[pl.ds(r0, 2 * blk), :]
            valid = _window_mask((b % nb) > 0)
            s = lax.dot_general(q, kw, _NT, preferred_element_type=F32) * _ATT_SCALE + bias_m
            s = jnp.where(valid, s, -1e30)
            mx = jnp.max(s, axis=-1, keepdims=True)
            e = jnp.exp(s - mx)
            den = jnp.sum(e, axis=-1, keepdims=True)
            o_ref[pl.ds(r0, blk), :] = jnp.dot((e / den).astype(BF16), vw, preferred_element_type=F32)
            l_ref[pl.ds(r0, blk), :] = jnp.broadcast_to(mx + jnp.log(den), (blk, dh))
            return carry

        lax.fori_loop(0, N_BLK, body, 0)

    return pl.pallas_call(
        kern, name=f"att_fwd_g{gi}", grid=(ATT_HPG,),
        in_specs=[_head_specs(0), _head_specs(ATT_HPG), _head_specs(2 * ATT_HPG),
                  pl.BlockSpec((None, None, blk, 2 * blk), lambda h: (gi, h, 0, 0))],
        out_specs=[_head_specs(0), _head_specs(0)],
        out_shape=[jax.ShapeDtypeStruct((SEQ, ATT_W), F32), jax.ShapeDtypeStruct((SEQ, ATT_W), F32)],
        scratch_shapes=[pltpu.VMEM((_PAD_ROWS, dh), BF16), pltpu.VMEM((_PAD_ROWS, dh), BF16)],
        compiler_params=_cparams(("arbitrary",)),
    )(qkv, qkv, qkv, bias)


def _att_bwd(gi, qkv, d_att, lse, dd, bias, nb):
    blk, dh = ATT_BLK, ATT_DH

    def kern(q_ref, k_ref, v_ref, do_ref, l_ref, d_ref, b_ref, dq_ref, dk_ref, dv_ref, dsb_ref,
             kpad, vpad, qpad, dopad, lpad, dpad):
        zero = jnp.zeros((blk, dh), BF16)
        zero_f = jnp.zeros((blk, dh), F32)
        kpad[0:blk, :] = zero
        vpad[0:blk, :] = zero
        kpad[blk:, :] = k_ref[...]
        vpad[blk:, :] = v_ref[...]
        qpad[SEQ:, :] = zero
        dopad[SEQ:, :] = zero
        lpad[SEQ:, :] = zero_f
        dpad[SEQ:, :] = zero_f
        qpad[0:SEQ, :] = q_ref[...]
        dopad[0:SEQ, :] = do_ref[...]
        lpad[0:SEQ, :] = l_ref[...]
        dpad[0:SEQ, :] = d_ref[...]
        bias_m = b_ref[...]
        bias_t = jnp.concatenate([bias_m[:, blk:], bias_m[:, :blk]], axis=0)
        dsb_ref[...] = jnp.zeros_like(dsb_ref)

        def dq_body(b, carry):
            r0 = pl.multiple_of(b * blk, blk)
            q, d_o = q_ref[pl.ds(r0, blk), :], do_ref[pl.ds(r0, blk), :]
            kw, vw = kpad[pl.ds(r0, 2 * blk), :], vpad[pl.ds(r0, 2 * blk), :]
            lrow, drow = l_ref[pl.ds(r0, blk), :][:, :1], d_ref[pl.ds(r0, blk), :][:, :1]
            valid = _window_mask((b % nb) > 0)
            s = lax.dot_general(q, kw, _NT, preferred_element_type=F32) * _ATT_SCALE + bias_m
            p = jnp.where(valid, jnp.exp(jnp.where(valid, s, -1e30) - lrow), 0.0)
            dp = lax.dot_general(d_o, vw, _NT, preferred_element_type=F32)
            ds = p * (dp - drow)
            dq = jnp.dot(ds.astype(BF16), kw, preferred_element_type=F32)
            dq_ref[pl.ds(r0, blk), :] = (dq * _ATT_SCALE).astype(BF16)
            dsb_ref[...] += ds
            return carry

        lax.fori_loop(0, N_BLK, dq_body, 0)

        qi = lax.broadcasted_iota(I32, (2 * blk, blk), 0)
        kj = lax.broadcasted_iota(I32, (2 * blk, blk), 1)

        def dkv_body(b, carry):
            r0 = pl.multiple_of(b * blk, blk)
            k, v = k_ref[pl.ds(r0, blk), :], v_ref[pl.ds(r0, blk), :]
            qw, dow = qpad[pl.ds(r0, 2 * blk), :], dopad[pl.ds(r0, 2 * blk), :]
            lrow, drow = lpad[pl.ds(r0, 2 * blk), :][:, :1], dpad[pl.ds(r0, 2 * blk), :][:, :1]
            has_next = jnp.logical_and(b + 1 < N_BLK, ((b + 1) % nb) > 0)
            next_ok = jnp.logical_and(jnp.logical_and(qi >= blk, kj >= qi - blk), has_next)
            valid = jnp.logical_or(jnp.logical_and(qi < blk, qi >= kj), next_ok)
            s = lax.dot_general(qw, k, _NT, preferred_element_type=F32) * _ATT_SCALE + bias_t
            p = jnp.where(valid, jnp.exp(jnp.where(valid, s, -1e30) - lrow), 0.0)
            dp = lax.dot_general(dow, v, _NT, preferred_element_type=F32)
            ds = p * (dp - drow)
            d_v = lax.dot_general(p.astype(BF16), dow, _TN, preferred_element_type=F32)
            d_k = lax.dot_general(ds.astype(BF16), qw, _TN, preferred_element_type=F32)
            dk_ref[pl.ds(r0, blk), :] = (d_k * _ATT_SCALE).astype(BF16)
            dv_ref[pl.ds(r0, blk), :] = d_v.astype(BF16)
            return carry

        lax.fori_loop(0, N_BLK, dkv_body, 0)

    return pl.pallas_call(
        kern, name=f"att_bwd_g{gi}", grid=(ATT_HPG,),
        in_specs=[_head_specs(0), _head_specs(ATT_HPG), _head_specs(2 * ATT_HPG),
                  _head_specs(0), _head_specs(0), _head_specs(0),
                  pl.BlockSpec((None, None, blk, 2 * blk), lambda h: (gi, h, 0, 0))],
        out_specs=[_head_specs(0), _head_specs(0), _head_specs(0),
                   pl.BlockSpec((None, blk, 2 * blk), lambda h: (h, 0, 0))],
        out_shape=[jax.ShapeDtypeStruct((SEQ, ATT_W), BF16)] * 3
        + [jax.ShapeDtypeStruct((ATT_HPG, blk, 2 * blk), F32)],
        scratch_shapes=[pltpu.VMEM((_PAD_ROWS, dh), BF16)] * 4 + [pltpu.VMEM((_PAD_ROWS, dh), F32)] * 2,
        compiler_params=_cparams(("arbitrary",)),
    )(qkv, qkv, qkv, d_att, lse, dd, bias)


def _rms_parts(x):
    r = lax.rsqrt(jnp.mean(x * x, axis=-1, keepdims=True) + RMS_EPS)
    return x * r, r


def _rms_bwd(d_xhat, xhat, r):
    return r * (d_xhat - xhat * jnp.mean(d_xhat * xhat, axis=-1, keepdims=True))


def _prenorm_fwd(name, x, gain, shift, scale):
    def body(xt, g, sh, sc):
        xhat, _ = _rms_parts(xt)
        return (xhat * g) * (1.0 + sc) + sh
    return _rowmap(name, body, [x], [gain, shift, scale], [(D_MODEL, BF16)])[0]


def _prenorm_bwd(name, d_hs, x, gain, scale, resid, after=()):
    n_dh = len(d_hs)

    def body(*args):
        d_h = args[0]
        for t in args[1:n_dh]:
            d_h = d_h + t
        xt, res, g, sc = args[n_dh:]
        xhat, r = _rms_parts(xt)
        nrm = xhat * g
        d_n = d_h * (1.0 + sc)
        dx = _rms_bwd(d_n * g, xhat, r) + res
        return (dx, jnp.sum(d_h, axis=0, keepdims=True), jnp.sum(d_h * nrm, axis=0, keepdims=True),
                jnp.sum(d_n * xhat, axis=0, keepdims=True))

    return _rowmap(name, body, list(d_hs) + [x, resid], [gain, scale], [(D_MODEL, F32)],
                   [D_MODEL, D_MODEL, D_MODEL], after=after)


def _gn_parts(ro):
    mu = jnp.mean(ro, axis=-1, keepdims=True)
    cen = ro - mu
    rstd = lax.rsqrt(jnp.mean(cen * cen, axis=-1, keepdims=True) + GN_EPS)
    return cen * rstd, rstd


def _retpost_fwd(ro, rg, gn_g, gn_b):
    def body(rot, rgt, g, b):
        outs = []
        for h in range(RET_HEADS):
            sl = slice(h * RET_DV, (h + 1) * RET_DV)
            nrm, _ = _gn_parts(rot[:, sl])
            gate = rgt[:, sl]
            outs.append((gate * _sigmoid(gate)) * (nrm * g[:, sl] + b[:, sl]))
        return jnp.concatenate(outs, axis=-1)
    return _rowmap("retpost_fwd", body, [ro, rg], [gn_g, gn_b], [(RET_V_W, BF16)])[0]


def _retpost_bwd(d_gated, ro, rg, gn_g, gn_b):
    def body(dgt, rot, rgt, g, b):
        d_ro, d_rg, d_g, d_b = [], [], [], []
        for h in range(RET_HEADS):
            sl = slice(h * RET_DV, (h + 1) * RET_DV)
            nrm, rstd = _gn_parts(rot[:, sl])
            gate, dg = rgt[:, sl], dgt[:, sl]
            sg = _sigmoid(gate)
            ron = nrm * g[:, sl] + b[:, sl]
            d_rg.append(dg * ron * (sg * (1.0 + gate * (1.0 - sg))))
            d_ron = dg * (gate * sg)
            d_g.append(jnp.sum(d_ron * nrm, axis=0, keepdims=True))
            d_b.append(jnp.sum(d_ron, axis=0, keepdims=True))
            d_n = d_ron * g[:, sl]
            d_ro.append(rstd * (d_n - jnp.mean(d_n, axis=-1, keepdims=True)
                                - nrm * jnp.mean(d_n * nrm, axis=-1, keepdims=True)))
        cat = lambda ts: jnp.concatenate(ts, axis=-1)
        return cat(d_ro), cat(d_rg), cat(d_g), cat(d_b)
    return _rowmap("retpost_bwd", body, [d_gated, ro, rg], [gn_g, gn_b],
                   [(RET_V_W, F32), (RET_V_W, BF16)], [RET_V_W, RET_V_W])


def _combine(os_, ls_):
    def body(o0, o1, o2, l0, l1, l2):
        mx = jnp.maximum(jnp.maximum(l0, l1), l2)
        e0, e1, e2 = jnp.exp(l0 - mx), jnp.exp(l1 - mx), jnp.exp(l2 - mx)
        den = e0 + e1 + e2
        att = (e0 / den) * o0 + (e1 / den) * o1 + (e2 / den) * o2
        return att, att, mx + jnp.log(den)
    return _rowmap("att_combine", body, list(os_) + list(ls_), [],
                   [(ATT_W, F32), (ATT_W, BF16), (ATT_W, F32)])


def _att_bwd_pre(d_att, att):
    def body(dt, at):
        outs = []
        for h in range(ATT_HPG):
            sl = slice(h * ATT_DH, (h + 1) * ATT_DH)
            outs.append(jnp.broadcast_to(jnp.sum(dt[:, sl] * at[:, sl], axis=-1, keepdims=True),
                                         (dt.shape[0], ATT_DH)))
        return dt, jnp.concatenate(outs, axis=-1)
    return _rowmap("att_bwd_pre", body, [d_att, att], [], [(ATT_W, BF16), (ATT_W, F32)])


def _merge_fwd(gates, ret_out, att_out):
    def body(gt, ro, ao):
        return _sigmoid(gt[:, :D_MODEL]) * ro + _sigmoid(gt[:, D_MODEL:]) * ao
    return _rowmap("merge_fwd", body, [gates, ret_out, att_out], [], [(D_MODEL, BF16)])[0]


def _merge_bwd(d_merged, gates, ret_out, att_out):
    def body(dm, gt, ro, ao):
        sa, sb = _sigmoid(gt[:, :D_MODEL]), _sigmoid(gt[:, D_MODEL:])
        d_gates = jnp.concatenate([dm * ro * (sa * (1.0 - sa)), dm * ao * (sb * (1.0 - sb))], axis=-1)
        return dm * sa, dm * sb, d_gates
    return _rowmap("merge_bwd", body, [d_merged, gates, ret_out, att_out], [],
                   [(D_MODEL, BF16), (D_MODEL, BF16), (2 * D_MODEL, BF16)])


def _gate_bwd(name, d_x, branch, gate):
    def body(dx, br, g):
        return dx * g, jnp.sum(dx * br, axis=0, keepdims=True)
    return _rowmap(name, body, [d_x, branch], [gate], [(D_MODEL, BF16)], [D_MODEL])


def _loss_head(x3, target, gain):
    def body(xt, tt, g):
        xhat, r = _rms_parts(xt)
        err = xhat * g - tt
        d_y = err / D_MODEL
        loss = 0.5 * jnp.sum(jnp.mean(err * err, axis=-1, keepdims=True), axis=0, keepdims=True)
        d_x = _rms_bwd(d_y * g, xhat, r)
        return d_x, jnp.broadcast_to(loss, (1, 128)), jnp.sum(d_y * xhat, axis=0, keepdims=True)
    return _rowmap("loss_head", body, [x3, target], [gain], [(D_MODEL, F32)], [128, D_MODEL])


def _local_step(pos, x, target, mod, norm1_g, norm2_g, norm_f_g, rel_bias, gn_g, gn_b, w_in, rest_gather):
    sh1, sc1, g1, sh2, sc2, g2 = [mod[:, i * D_MODEL:(i + 1) * D_MODEL] for i in range(6)]
    cos, sin = _rope_tables()
    din, qd, kd, cd = _decay_tables()
    buckets = _bucket_tables()
    bias = _bias_build(rel_bias, buckets)
    dils = [d for _, d in ATT_GROUPS]
    nbs = [SEQ // d // ATT_BLK for d in dils]

    h1 = _prenorm_fwd("prenorm1_fwd", x, norm1_g, sh1, sc1)
    h1_p = [_permute_rows(h1, d) for d in dils]

    def rot_epi(acc, cs, sn, scale):
        half = RET_DK // 2
        x1, x2 = acc[:, :half], acc[:, half:]
        return (jnp.concatenate([x1 * cs - x2 * sn, x1 * sn + x2 * cs], axis=-1) * scale,)

    qk_scale = jnp.concatenate([jnp.ones((1, RET_QK_W), F32),
                                jnp.full((1, RET_QK_W), RET_DK ** -0.5, F32)], axis=-1)
    rope_ex = [(cos, (TM, RET_DK // 2), lambda i, j, kk: (i, 0)),
               (sin, (TM, RET_DK // 2), lambda i, j, kk: (i, 0)),
               (qk_scale, (1, RET_DK), lambda i, j, kk: (0, j))]
    rest_sems, rest_shards, rest_fulls, rest_token = rest_gather
    behind = [rest_token]
    rqk = _matmul("proj_qk", h1, w_in, "nn", SEQ, 2 * RET_QK_W, D_MODEL, [BF16], b_off=OFF_Q,
                  tn=RET_DK, tk=D_MODEL, epilogue=rot_epi, extras=rope_ex, after=behind)[0]
    rv = _matmul("proj_rv", h1, w_in, "nn", SEQ, RET_V_W, D_MODEL, [BF16], b_off=OFF_V, tk=D_MODEL,
                 after=behind)[0]
    rg = _matmul("proj_rg", h1, w_in, "nn", SEQ, RET_V_W, D_MODEL, [F32], b_off=OFF_G, tk=D_MODEL,
                 after=behind)[0]
    gates = _matmul("proj_gates", h1, w_in, "nn", SEQ, 2 * D_MODEL, D_MODEL, [F32], b_off=OFF_GATE,
                    tn=512, tk=D_MODEL, after=behind)[0]
    aqkv = [_matmul(f"proj_att_g{gi}", h1_p[gi], w_in, "nn", SEQ, 3 * ATT_W, D_MODEL, [BF16],
                    b_off=OFF_ATT + gi * 3 * ATT_W, tn=512, tk=D_MODEL, after=behind)[0]
            for gi in range(3)]

    os_, ls_ = [], []
    for gi in range(3):
        o_g, l_g = _att_fwd(gi, aqkv[gi], bias, nbs[gi])
        os_.append(_unpermute_rows(o_g, dils[gi]))
        ls_.append(_unpermute_rows(l_g, dils[gi]))
        if gi == 1:
            rest_sems, rest_fulls = _gather_rest_forward(rest_sems, rest_shards, rest_fulls,
                                                         [o_g, rqk, rv, rg, gates])

    ro, states = _retention_fwd(rqk, rv, din, qd, kd, cd)
    gated = _retpost_fwd(ro, rg, gn_g, gn_b)
    w_ret_out, w_att_out, w_o, w_ff1, w_ff2 = _gather_rest_end(rest_sems, rest_fulls, [gated, os_[2]])
    ret_out = _matmul("ret_out", gated, w_ret_out, "nn", SEQ, D_MODEL, RET_V_W, [F32])[0]
    att, att_b, lse = _combine(os_, ls_)
    att_out = _matmul("att_out", att_b, w_att_out, "nn", SEQ, D_MODEL, ATT_W, [F32])[0]

    merged = _merge_fwd(gates, ret_out, att_out)

    def resid_epi(acc, xt, g):
        return xt + g * acc, acc

    def resid_ex(xin, g):
        return [(xin, (TM, TN), lambda i, j, kk: (i, j)), (g, (1, TN), lambda i, j, kk: (0, j))]

    x2, mix = _matmul("mix_out", merged, w_o, "nn", SEQ, D_MODEL, D_MODEL, [F32, F32],
                      epilogue=resid_epi, extras=resid_ex(x, g1))
    h2 = _prenorm_fwd("prenorm2_fwd", x2, norm2_g, sh2, sc2)

    def relu2_epi(acc):
        r = jnp.maximum(acc, 0.0)
        return r * r, acc

    act, u = _matmul("ff1", h2, w_ff1, "nn", SEQ, D_FF, D_MODEL, [BF16, F32], tk=D_MODEL,
                     epilogue=relu2_epi)
    x3, y2 = _matmul("ff2", act, w_ff2, "nn", SEQ, D_MODEL, D_FF, [F32, F32],
                     epilogue=resid_epi, extras=resid_ex(x2, g2))

    d_x3, loss, d_gf = _loss_head(x3, target, norm_f_g)

    d_y2, d_g2 = _gate_bwd("ff_gate_bwd", d_x3, y2, g2)

    def relu2_bwd_epi(acc, ut):
        return (acc * (2.0 * jnp.maximum(ut, 0.0)),)

    recv = {}
    gw_ff2 = _matmul_tn_pair("ff2_dw", pos, act, d_y2, D_FF, D_MODEL, SEQ, D_FF // N_CHIPS,
                             tm=512, tn=1024, tk=1024)
    d_u, recv["w_ff2"] = _matmul(
        "ff2_dx", d_y2, w_ff2, "nt", SEQ, D_FF, D_MODEL, [BF16], epilogue=relu2_bwd_epi,
        extras=[(u, (TM, TN), lambda i, j, kk: (i, j))], carry=(gw_ff2, 0, SHARD["w_ff2"]))
    gw_ff1 = _matmul_tn_pair("ff1_dw", pos, h2, d_u, D_MODEL, D_FF, SEQ, D_MODEL,
                             tm=512, tn=1024, tk=1024)
    d_h2, recv["w_ff1"] = _matmul("ff1_dx", d_u, w_ff1, "nt", SEQ, D_MODEL, D_FF, [F32],
                                  carry=(gw_ff1, 1, SHARD["w_ff1"]))
    d_x2, d_sh2, d_sc2, d_n2g = _prenorm_bwd("prenorm2_bwd", [d_h2], x2, norm2_g, sc2, d_x3)

    d_mix, d_g1 = _gate_bwd("mix_gate_bwd", d_x2, mix, g1)
    gw_o = _matmul_tn_pair("mix_dw", pos, merged, d_mix, D_MODEL, D_MODEL, SEQ, D_MODEL // N_CHIPS,
                           tm=128, tn=1024, tk=2048)
    d_merged, recv["w_o"] = _matmul("mix_dx", d_mix, w_o, "nt", SEQ, D_MODEL, D_MODEL, [F32],
                                    carry=(gw_o, 0, SHARD["w_o"]))
    d_ret_out, d_att_out, d_gates = _merge_bwd(d_merged, gates, ret_out, att_out)

    gw_ret_out = _matmul_tn_pair("ret_out_dw", pos, gated, d_ret_out, RET_V_W, D_MODEL, SEQ,
                                 RET_V_W // N_CHIPS, tm=256, tn=1024, tk=1024)
    d_gated, recv["w_ret_out"] = _matmul("ret_out_dx", d_ret_out, w_ret_out, "nt", SEQ, RET_V_W, D_MODEL,
                                         [F32], carry=(gw_ret_out, 0, SHARD["w_ret_out"]))
    gw_att_out = _matmul_tn_pair("att_out_dw", pos, att_b, d_att_out, ATT_W, D_MODEL, SEQ, ATT_W,
                                 tm=256, tn=1024, tk=2048)
    d_att, recv["w_att_out"] = _matmul("att_out_dx", d_att_out, w_att_out, "nt", SEQ, ATT_W, D_MODEL,
                                       [F32], carry=(gw_att_out, 1, SHARD["w_att_out"]))

    d_ro, d_rg, d_gn_g, d_gn_b = _retpost_bwd(d_gated, ro, rg, gn_g, gn_b)
    d_rq, d_rk, d_rv = _retention_bwd(rqk, rv, states, d_ro, din, qd, kd, cd, cos, sin)

    d_att_b, dd = _att_bwd_pre(d_att, att)
    d_aqkv, dsbs = [], []
    for gi in range(3):
        da_p = _permute_rows(d_att_b, dils[gi])
        l_p = _permute_rows(lse, dils[gi])
        dd_p = _permute_rows(dd, dils[gi])
        dq, dk, dv, dsb = _att_bwd(gi, aqkv[gi], da_p, l_p, dd_p, bias, nbs[gi])
        d_aqkv.append(_unpermute_rows(jnp.concatenate([dq, dk, dv], axis=-1), dils[gi]))
        dsbs.append(dsb)
    d_rel_bias = _bias_grad(jnp.stack(dsbs), buckets)

    d_proj = jnp.concatenate([d_rq, d_rk, d_rv, d_rg] + d_aqkv + [d_gates], axis=-1)
    gw_in = _matmul_tn_pair("proj_dw", pos, h1, d_proj, D_MODEL, IN_COLS, SEQ, D_MODEL,
                            tm=512, tn=640, tk=2048)
    sems, gw_in, land, token = _ici_start("ici_start_w_in", gw_in, 1, SHARD["w_in"])
    d_h1 = _matmul("proj_dx", d_proj, w_in, "nt", SEQ, D_MODEL, IN_COLS, [F32], tn=1024, tk=1280,
                   after=[token])[0]
    pending = (sems, land)

    big = dict(w_in=gw_in, w_ret_out=gw_ret_out, w_att_out=gw_att_out, w_o=gw_o, w_ff1=gw_ff1,
               w_ff2=gw_ff2)
    g_big = {n: _final_sum("final_" + n, pos, ax, big[n], recv[n], SHARD[n], after=[token])
             for n, ax in BIG[1:]}
    grad_x, d_sh1, d_sc1, d_n1g = _prenorm_bwd("prenorm1_bwd", [d_h1], x, norm1_g, sc1, d_x2,
                                               after=list(g_big.values()))
    d_mod = jnp.concatenate([d_sh1, d_sc1, d_g1, d_sh2, d_sc2, d_g2], axis=-1)
    small = dict(norm1_g=d_n1g, norm2_g=d_n2g, norm_f_g=d_gf, gn_g=d_gn_g, gn_b=d_gn_b,
                 rel_bias=d_rel_bias)
    return loss, grad_x, d_mod, small, g_big, (gw_in,) + pending


def _me():
    return lax.axis_index("x"), lax.axis_index("y"), lax.axis_index("c")


def _peer(x, y, c, mask):
    return (x ^ ((mask >> 2) & 1), y ^ ((mask >> 1) & 1), c ^ (mask & 1))


def _gather8(src_ref, dst_ref, send_sems, recv_sems):
    x, y, c = _me()
    me = 4 * x + 2 * y + c
    copies = []
    for mask in range(1, N_DEV):
        cp = pltpu.make_async_remote_copy(
            src_ref=src_ref, dst_ref=dst_ref.at[me], send_sem=send_sems.at[mask - 1],
            recv_sem=recv_sems.at[mask - 1], device_id=_peer(x, y, c, mask), device_id_type=MESH)
        cp.start()
        copies.append(cp)
    dst_ref[me] = src_ref[...]
    for cp in copies:
        cp.wait_recv()
    for cp in copies:
        cp.wait_send()


def _ada_fwd(c_in, w_ada, b_ada):
    ncol = ADA_COLS // N_CHIPS

    def body(c_ref, w_ref, b_ref, mod_ref, sc_ref, cbuf, cg, mbuf, mg, s1, r1, s2, r2):
        x, y, c = _me()
        me = 4 * x + 2 * y + c
        cv = c_ref[...]
        cbuf[...] = jnp.broadcast_to(cv * _sigmoid(cv), cbuf.shape)
        _gather8(cbuf, cg, s1, r1)
        rows = lax.broadcasted_iota(I32, (N_DEV, D_MODEL), 0)
        sc_all = jnp.zeros((N_DEV, D_MODEL), F32)
        for d in range(N_DEV):
            sc_all = jnp.where(rows == d, cg[d], sc_all)
        sc_ref[...] = sc_all
        mbuf[...] = jnp.dot(sc_all.astype(BF16), w_ref[...].astype(BF16), preferred_element_type=F32)
        _gather8(mbuf, mg, s2, r2)
        rowsel = lax.broadcasted_iota(I32, (N_DEV, ncol), 0) == me
        for k in range(N_CHIPS):
            blk = mg[2 * k]
            row = jnp.sum(jnp.where(rowsel, blk, 0.0), axis=0, keepdims=True)
            mod_ref[:, k * ncol:(k + 1) * ncol] = row + b_ref[:, k * ncol:(k + 1) * ncol]

    vm = pl.BlockSpec(memory_space=pltpu.VMEM)
    return pl.pallas_call(
        body, name="ada_fwd",
        in_specs=[vm, vm, vm], out_specs=[vm, vm],
        out_shape=[jax.ShapeDtypeStruct((1, ADA_COLS), F32), jax.ShapeDtypeStruct((N_DEV, D_MODEL), F32)],
        scratch_shapes=[
            pltpu.VMEM((8, D_MODEL), F32), pltpu.VMEM((N_DEV, 8, D_MODEL), F32),
            pltpu.VMEM((8, ncol), F32), pltpu.VMEM((N_DEV, 8, ncol), F32),
            pltpu.SemaphoreType.DMA((N_DEV - 1,)), pltpu.SemaphoreType.DMA((N_DEV - 1,)),
            pltpu.SemaphoreType.DMA((N_DEV - 1,)), pltpu.SemaphoreType.DMA((N_DEV - 1,)),
        ],
        compiler_params=pltpu.CompilerParams(vmem_limit_bytes=VMEM_LIMIT_V7X),
    )(c_in, w_ada, b_ada)


def _small_reduce(pack, sc_all):
    ncol = ADA_COLS // N_CHIPS

    def body(p_ref, sc_ref, tot_ref, gw_ref, pg, s1, r1):
        x, y, _ = _me()
        chip = 2 * x + y
        _gather8(p_ref, pg, s1, r1)
        tot = pg[0]
        for d in range(1, N_DEV):
            tot = tot + pg[d]
        tot_ref[...] = tot
        rows = lax.broadcasted_iota(I32, (N_DEV, ncol), 0)
        dmod = jnp.zeros((N_DEV, ncol), F32)
        for k in range(N_CHIPS):
            part = jnp.zeros((N_DEV, ncol), F32)
            for d in range(N_DEV):
                part = jnp.where(rows == d, pg[d, :, k * ncol:(k + 1) * ncol][0:1, :], part)
            dmod = jnp.where(chip == k, part, dmod)
        gw_ref[...] = lax.dot_general(sc_ref[...].astype(BF16), dmod.astype(BF16), _TN,
                                      preferred_element_type=F32)

    vm = pl.BlockSpec(memory_space=pltpu.VMEM)
    return pl.pallas_call(
        body, name="small_reduce",
        in_specs=[vm, vm], out_specs=[vm, vm],
        out_shape=[jax.ShapeDtypeStruct((8, ADA_COLS), F32), jax.ShapeDtypeStruct((D_MODEL, ncol), F32)],
        scratch_shapes=[pltpu.VMEM((N_DEV, 8, ADA_COLS), F32),
                        pltpu.SemaphoreType.DMA((N_DEV - 1,)), pltpu.SemaphoreType.DMA((N_DEV - 1,))],
        compiler_params=pltpu.CompilerParams(vmem_limit_bytes=VMEM_LIMIT_V7X),
    )(pack, sc_all)


BIG = (("w_in", 1), ("w_ret_out", 0), ("w_att_out", 1), ("w_o", 0), ("w_ff1", 1), ("w_ff2", 0))
SHARD = {"w_in": (D_MODEL, IN_COLS // N_CHIPS), "w_ret_out": (RET_V_W // N_CHIPS, D_MODEL),
         "w_att_out": (ATT_W, D_MODEL // N_CHIPS), "w_o": (D_MODEL // N_CHIPS, D_MODEL),
         "w_ff1": (D_MODEL, D_FF // N_CHIPS), "w_ff2": (D_FF // N_CHIPS, D_MODEL)}
_CHIP_FLIPS = ((1, 0), (0, 1), (1, 1))


def _region(ref, axis, chip, half, shard_shape):
    r, cw = shard_shape
    hr = r // 2
    if axis == 1:
        return ref.at[pl.ds(half * hr, hr), pl.ds(chip * cw, cw)]
    return ref.at[pl.ds(chip * r + half * hr, hr), :]


def _gather_weights(shards, n_remote):
    nw = len(BIG)
    shapes = [s.shape for s in shards]
    full_shapes = [(r, N_CHIPS * cw) if ax == 1 else (N_CHIPS * r, cw)
                   for (r, cw), (_, ax) in zip(shapes, BIG)]

    def body(*refs):
        ins, outs = refs[:nw], refs[nw:2 * nw]
        own = refs[2 * nw:3 * nw]
        from_ici, from_sib = refs[3 * nw:3 * nw + n_remote], refs[3 * nw + n_remote:3 * nw + 2 * n_remote]
        ld_sem, st_sem, s_ici, r_ici, s_d2d, r_d2d, st_a, st_b = refs[3 * nw + 2 * n_remote:]
        x, y, c = _me()
        chip = 2 * x + y
        sib = (x, y, 1 - c)
        loads = [pltpu.make_async_copy(ins[i], own[i], ld_sem.at[i]) for i in range(nw)]
        for cp in loads:
            cp.start()
        pending, first = [], []
        for i, (_, ax) in enumerate(BIG):
            r, cw = shapes[i]
            hr = r // 2
            loads[i].wait()
            dst = outs[i].at[:, pl.ds(chip * cw, cw)] if ax == 1 else outs[i].at[pl.ds(chip * r, r), :]
            cp = pltpu.make_async_copy(own[i], dst, st_sem.at[i])
            cp.start()
            pending.append(cp)
            for j, (fx, fy) in enumerate(_CHIP_FLIPS if i < n_remote else ()):
                rc = pltpu.make_async_remote_copy(
                    src_ref=own[i].at[pl.ds(c * hr, hr), :], dst_ref=from_ici[i].at[j],
                    send_sem=s_ici.at[j * nw + i], recv_sem=r_ici.at[j * nw + i],
                    device_id=(x ^ fx, y ^ fy, c), device_id_type=MESH)
                rc.start()
                first.append((j, i, rc))
        passed = []
        for j, i, rc in first:
            fx, fy = _CHIP_FLIPS[j]
            src_chip = 2 * (x ^ fx) + (y ^ fy)
            ax = BIG[i][1]
            rc.wait_recv()
            fw = pltpu.make_async_remote_copy(
                src_ref=from_ici[i].at[j], dst_ref=from_sib[i].at[j], send_sem=s_d2d.at[j * nw + i],
                recv_sem=r_d2d.at[j * nw + i], device_id=sib, device_id_type=MESH)
            fw.start()
            passed.append((j, i, src_chip, fw))
            st = pltpu.make_async_copy(from_ici[i].at[j], _region(outs[i], ax, src_chip, c, shapes[i]),
                                       st_a.at[j * nw + i])
            st.start()
            pending.append(st)
        for j, i, src_chip, fw in passed:
            fw.wait_recv()
            st = pltpu.make_async_copy(from_sib[i].at[j],
                                       _region(outs[i], BIG[i][1], src_chip, 1 - c, shapes[i]),
                                       st_b.at[j * nw + i])
            st.start()
            pending.append(st)
        for _, _, rc in first:
            rc.wait_send()
        for _, _, _, fw in passed:
            fw.wait_send()
        for cp in pending:
            cp.wait()

    hbm = pl.BlockSpec(memory_space=pl.ANY)
    halves = [pltpu.VMEM((3, r // 2, cw), BF16) for r, cw in shapes[:n_remote]]
    return pl.pallas_call(
        body, name="gather_weights",
        in_specs=[hbm] * nw, out_specs=[hbm] * nw,
        out_shape=[jax.ShapeDtypeStruct(fs, BF16) for fs in full_shapes],
        scratch_shapes=[pltpu.VMEM(sh, BF16) for sh in shapes] + halves + halves
        + [pltpu.SemaphoreType.DMA((nw,)), pltpu.SemaphoreType.DMA((nw,))]
        + [pltpu.SemaphoreType.DMA((3 * nw,))] * 6,
        compiler_params=pltpu.CompilerParams(vmem_limit_bytes=VMEM_LIMIT_V7X),
    )(*shards)


REST = BIG[1:]
_SIDE_EFFECTS = pltpu.CompilerParams(has_side_effects=pltpu.SideEffectType.DATAFLOW_SIDE_EFFECTING)
_ANY_SPEC = pl.BlockSpec(memory_space=pl.ANY)


def _rest_ici_copies(shard_refs, full_refs, sems):
    x, y, c = _me()
    chip = 2 * x + y
    n = 3 * len(REST)
    copies = []
    for i, (name, ax) in enumerate(REST):
        hr = SHARD[name][0] // 2
        for j, (fx, fy) in enumerate(_CHIP_FLIPS):
            copies.append(pltpu.make_async_remote_copy(
                src_ref=shard_refs[i].at[pl.ds(c * hr, hr), :],
                dst_ref=_region(full_refs[i], ax, chip, c, SHARD[name]),
                send_sem=sems[3 * i + j], recv_sem=sems[n + 3 * i + j],
                device_id=(x ^ fx, y ^ fy, c), device_id_type=MESH))
    return copies


def _rest_d2d_copies(full_refs, sems):
    x, y, c = _me()
    n = 3 * len(REST)
    copies = []
    for i, (name, ax) in enumerate(REST):
        for j, (fx, fy) in enumerate(_CHIP_FLIPS):
            reg = _region(full_refs[i], ax, 2 * (x ^ fx) + (y ^ fy), c, SHARD[name])
            copies.append(pltpu.make_async_remote_copy(
                src_ref=reg, dst_ref=reg, send_sem=sems[3 * i + j], recv_sem=sems[n + 3 * i + j],
                device_id=(x, y, 1 - c), device_id_type=MESH))
    return copies


def _gather_rest_start(shards, fulls, after):
    nr, ns, na = len(REST), 6 * len(REST), len(after)

    def body(*refs):
        for cp in _rest_ici_copies(refs[:nr], refs[nr:2 * nr], refs[2 * nr + na:2 * nr + na + ns]):
            cp.start()
        token = refs[-1]
        token[...] = jnp.zeros_like(token)

    hbm = lambda a: pltpu.HBM(a.shape, a.dtype)
    res = pl.pallas_call(
        body, name="gather_rest_start",
        out_shape=(pltpu.SemaphoreType.DMA(()),) * ns + tuple(hbm(a) for a in shards + fulls)
        + (jax.ShapeDtypeStruct((8, 128), F32),),
        in_specs=(_HBM_SPEC,) * (2 * nr) + (_ANY_SPEC,) * na,
        out_specs=(_SEM_SPEC,) * ns + (_HBM_SPEC,) * (2 * nr) + (pl.BlockSpec(memory_space=pltpu.VMEM),),
        input_output_aliases={k: ns + k for k in range(2 * nr)}, compiler_params=_SIDE_EFFECTS,
    )(*[pltpu.with_memory_space_constraint(a, pltpu.HBM) for a in shards + fulls], *after)
    return res[:ns], res[ns:ns + nr], res[ns + nr:ns + 2 * nr], res[-1]


def _gather_rest_forward(sems, shards, fulls, after):
    nr, ns = len(REST), 6 * len(REST)

    def body(*refs):
        shard_refs, full_refs, old = refs[:nr], refs[nr:2 * nr], refs[2 * nr:2 * nr + ns]
        new = refs[2 * nr + ns + len(after):2 * nr + 2 * ns + len(after)]
        for cp in _rest_ici_copies(shard_refs, full_refs, old):
            cp.wait_send()
            cp.wait_recv()
        for cp in _rest_d2d_copies(full_refs, new):
            cp.start()

    res = pl.pallas_call(
        body, name="gather_rest_forward",
        out_shape=(pltpu.SemaphoreType.DMA(()),) * ns + tuple(pltpu.HBM(a.shape, a.dtype) for a in fulls),
        in_specs=(_HBM_SPEC,) * (2 * nr) + (_SEM_SPEC,) * ns + (_ANY_SPEC,) * len(after),
        out_specs=(_SEM_SPEC,) * ns + (_HBM_SPEC,) * nr,
        input_output_aliases={nr + k: ns + k for k in range(nr)}, compiler_params=_SIDE_EFFECTS,
    )(*shards, *fulls, *sems, *after)
    return res[:ns], res[ns:]


def _gather_rest_end(sems, fulls, after):
    nr, ns = len(REST), 6 * len(REST)

    def body(*refs):
        for cp in _rest_d2d_copies(refs[:nr], refs[nr:nr + ns]):
            cp.wait_send()
            cp.wait_recv()

    return pl.pallas_call(
        body, name="gather_rest_end",
        out_shape=tuple(pltpu.HBM(a.shape, a.dtype) for a in fulls),
        in_specs=(_HBM_SPEC,) * nr + (_SEM_SPEC,) * ns + (_ANY_SPEC,) * len(after),
        out_specs=(_HBM_SPEC,) * nr,
        input_output_aliases={k: k for k in range(nr)}, compiler_params=_SIDE_EFFECTS,
    )(*fulls, *sems, *after)


def _adam_update(w, g, m, v):
    mn = ADAM_B1 * m + (1.0 - ADAM_B1) * g
    vn = ADAM_B2 * v + (1.0 - ADAM_B2) * (g * g)
    m_hat = mn / (1.0 - ADAM_B1 ** ADAM_STEP)
    v_hat = vn / (1.0 - ADAM_B2 ** ADAM_STEP)
    return -ADAM_LR * (m_hat / (jnp.sqrt(v_hat) + ADAM_EPS) + ADAM_WD * w), mn, vn


def _final_sum(name, pos, axis, psum, recv, shard_shape, after=(), tr=128):
    r, cw = shard_shape
    hr = r // 2
    tr = min(tr, hr)
    nt = hr // tr
    n_after = len(after)

    def kern(pos_ref, p_ref, r_ref, *rest):
        g_ref, send_buf, land_buf, s_sem, r_sem = rest[n_after:]
        p, t = pl.program_id(0), pl.program_id(1)
        sib = _sibling()

        def copy(i):
            return pltpu.make_async_remote_copy(
                src_ref=send_buf.at[i], dst_ref=land_buf.at[i], send_sem=s_sem.at[i],
                recv_sem=r_sem.at[i], device_id=sib, device_id_type=MESH)

        @pl.when(p == 0)
        def _():
            tot = p_ref[...].astype(F32)
            for j in range(3):
                tot = tot + r_ref[j].astype(F32)
            send_buf[t] = tot
            copy(t).start()
            g_ref[...] = tot

        @pl.when(p == 1)
        def _():
            copy(t).wait_recv()
            g_ref[...] = land_buf[t]

        @pl.when(jnp.logical_and(p == 1, t == nt - 1))
        def _():
            for i in range(nt):
                copy(i).wait_send()

    def shard_rows(p, t, pos_ref):
        return (jnp.where(p == 0, pos_ref[0], 1 - pos_ref[0]) * nt + t, 0)

    def own_part(p, t, pos_ref):
        tt = jnp.where(p == 0, t, nt - 1)
        return (tt, pos_ref[1]) if axis == 1 else (pos_ref[1] * nt + tt, 0)

    grid_spec = pltpu.PrefetchScalarGridSpec(
        num_scalar_prefetch=1, grid=(2, nt),
        in_specs=[pl.BlockSpec((tr, cw), own_part),
                  pl.BlockSpec((3, tr, cw), lambda p, t, pos_ref: (0, jnp.where(p == 0, t, nt - 1), 0))]
        + [pl.BlockSpec(memory_space=pl.ANY)] * n_after,
        out_specs=pl.BlockSpec((tr, cw), shard_rows),
        scratch_shapes=[pltpu.VMEM((nt, tr, cw), F32), pltpu.VMEM((nt, tr, cw), F32),
                        pltpu.SemaphoreType.DMA((nt,)), pltpu.SemaphoreType.DMA((nt,))])
    return pl.pallas_call(
        kern, name=name, grid_spec=grid_spec, out_shape=jax.ShapeDtypeStruct((r, cw), F32),
        compiler_params=_cparams(("arbitrary", "arbitrary")),
    )(pos, psum, recv, *after)


def _adamw(name, w, g, m, v):
    r, cw = w.shape
    tr = min(r, 128)

    def kern(w_ref, g_ref, m_ref, v_ref, go_ref, d_ref, nm_ref, nv_ref):
        gv = g_ref[...]
        go_ref[...] = gv
        d_ref[...], nm_ref[...], nv_ref[...] = _adam_update(w_ref[...], gv, m_ref[...], v_ref[...])

    spec = pl.BlockSpec((tr, cw), lambda i: (i, 0))
    return pl.pallas_call(
        kern, name=name, grid=(r // tr,), in_specs=[spec] * 4, out_specs=[spec] * 4,
        out_shape=[jax.ShapeDtypeStruct((r, cw), F32)] * 4, compiler_params=_cparams(("parallel",)),
    )(w, g, m, v)


_PACK_W = ADA_COLS
_NB = REL_BUCKETS * N_ATT_HEADS
_SMALL_SLOTS = {
    "b_ada": (0, 0, ADA_COLS),
    "norm1_g": (1, 0, D_MODEL), "norm2_g": (1, D_MODEL, D_MODEL), "norm_f_g": (1, 2 * D_MODEL, D_MODEL),
    "ret_gn_g": (1, 3 * D_MODEL, RET_V_W),
    "ret_gn_b": (2, 0, RET_V_W), "rel_bias": (2, RET_V_W, _NB), "loss": (2, RET_V_W + 512, 128),
}


def _pack_small(vals):
    rows = []
    for r in range(8):
        items = sorted([(off, n) for n, (rr, off, _) in _SMALL_SLOTS.items() if rr == r and n in vals])
        parts, pos = [], 0
        for off, n in items:
            if off > pos:
                parts.append(jnp.zeros((1, off - pos), F32))
            parts.append(vals[n].reshape(1, -1).astype(F32))
            pos = off + _SMALL_SLOTS[n][2]
        if pos < _PACK_W:
            parts.append(jnp.zeros((1, _PACK_W - pos), F32))
        rows.append(jnp.concatenate(parts, axis=-1))
    return jnp.concatenate(rows, axis=0)


def _unpack_small(pack, name):
    r, off, wd = _SMALL_SLOTS[name]
    return pack[r:r + 1, off:off + wd]


def kernel(x, c, w_ada, b_ada, norm1_g, w_in, rel_bias, ret_gn_g, ret_gn_b, w_ret_out, w_att_out, w_o, norm2_g, w_ff1, w_ff2, norm_f_g, loss_target, m_w_ada, m_b_ada, m_norm1_g, m_w_in, m_rel_bias, m_ret_gn_g, m_ret_gn_b, m_w_ret_out, m_w_att_out, m_w_o, m_norm2_g, m_w_ff1, m_w_ff2, m_norm_f_g, v_w_ada, v_b_ada, v_norm1_g, v_w_in, v_rel_bias, v_ret_gn_g, v_ret_gn_b, v_w_ret_out, v_w_att_out, v_w_o, v_norm2_g, v_w_ff1, v_w_ff2, v_norm_f_g):
    given = dict(locals())
    big_names = [n for n, _ in BIG]
    shard_w = {n: given[n][0] for n in big_names}
    assert all(shard_w[n].shape == SHARD[n] for n in big_names)

    shards_bf = [shard_w[n].astype(BF16) for n in big_names]
    full = _gather_weights(shards_bf, 1)
    mod, sc_all = _ada_fwd(c, w_ada[0], b_ada)
    rest_gather = _gather_rest_start(shards_bf[1:], list(full[1:]), [mod])
    pos = _where_am_i()

    loss, grad_x, d_mod, small, g_big, pending = _local_step(
        pos, x[0], loss_target[0], mod, norm1_g, norm2_g, norm_f_g.reshape(1, -1), rel_bias, ret_gn_g,
        ret_gn_b, full[0], rest_gather)

    pack_g = _pack_small(dict(b_ada=d_mod, norm1_g=small["norm1_g"], norm2_g=small["norm2_g"],
                              norm_f_g=small["norm_f_g"], ret_gn_g=small["gn_g"], ret_gn_b=small["gn_b"],
                              rel_bias=small["rel_bias"], loss=loss))
    tot, g_w_ada = _small_reduce(pack_g, sc_all)

    small_names = ["b_ada", "norm1_g", "rel_bias", "ret_gn_g", "ret_gn_b", "norm2_g", "norm_f_g"]
    pack_w = _pack_small({n: given[n] for n in small_names})
    pack_m = _pack_small({n: given["m_" + n] for n in small_names})
    pack_v = _pack_small({n: given["v_" + n] for n in small_names})
    _, sd, sm, sv = _adamw("adamw_small", pack_w, tot, pack_m, pack_v)

    grads, deltas, new_m, new_v = {}, {}, {}, {}
    for n in small_names:
        shp = given[n].shape
        grads[n] = _unpack_small(tot, n).reshape(shp)
        deltas[n] = _unpack_small(sd, n).reshape(shp)
        new_m[n] = _unpack_small(sm, n).reshape(shp)
        new_v[n] = _unpack_small(sv, n).reshape(shp)
    g_big["w_ada"] = g_w_ada
    for n in ["w_ada"] + big_names[1:] + big_names[:1]:
        if n == "w_in":
            gw_in, sems, land = pending
            done = [tot, sd] + [deltas[k] for k in ["w_ada"] + big_names[1:]]
            gw_in, got = _ici_wait("ici_wait_w_in", sems, gw_in, land, 1, SHARD[n], done)
            g_big[n] = _final_sum("final_w_in", pos, 1, gw_in, got, SHARD[n])
        g, d, nm, nv = _adamw("adamw_" + n, given[n][0], g_big[n], given["m_" + n][0], given["v_" + n][0])
        grads[n], deltas[n], new_m[n], new_v[n] = g[None], d[None], nm[None], nv[None]

    order = ["w_ada", "b_ada", "norm1_g", "w_in", "rel_bias", "ret_gn_g", "ret_gn_b", "w_ret_out",
             "w_att_out", "w_o", "norm2_g", "w_ff1", "w_ff2", "norm_f_g"]
    loss_out = _unpack_small(tot, "loss")[0, 0]
    return (loss_out, grad_x[None], *[grads[n] for n in order], *[deltas[n] for n in order],
            *[new_m[n] for n in order], *[new_v[n] for n in order])
```

```python
import functools
import math

import jax
import jax.numpy as jnp
import numpy as np
from jax import lax
from jax.experimental import pallas as pl
from jax.experimental.pallas import tpu as pltpu

F32 = jnp.float32
BF16 = jnp.bfloat16
I32 = jnp.int32

SEQ = 2048
D_MODEL = 1024
RET_HEADS = 4
RET_DK = 256
RET_DV = 512
RET_CHUNK = 128
RET_QK_W = RET_HEADS * RET_DK
RET_V_W = RET_HEADS * RET_DV
ATT_GROUPS = ((128, 1), (512, 4), (2048, 16))
ATT_HPG = 4
ATT_DH = 128
ATT_W = ATT_HPG * ATT_DH
ATT_BLK = 128
N_BLK = SEQ // ATT_BLK
REL_BUCKETS = 32
REL_MAX_DIST = 2048
N_ATT_HEADS = 12
D_FF = 4 * D_MODEL
RMS_EPS = 1e-6
GN_EPS = 1e-5
ROPE_BASE = 10000.0
IN_COLS = 2 * RET_QK_W + 2 * RET_V_W + 9 * ATT_W + 2 * D_MODEL
OFF_Q, OFF_K, OFF_V, OFF_G = 0, RET_QK_W, 2 * RET_QK_W, 2 * RET_QK_W + RET_V_W
OFF_ATT = 2 * RET_QK_W + 2 * RET_V_W
OFF_GATE = OFF_ATT + 9 * ATT_W
N_CHIPS = 4
N_DEV = 8
ADA_COLS = 6 * D_MODEL

ADAM_LR = 0.001
ADAM_B1 = 0.9
ADAM_B2 = 0.999
ADAM_EPS = 1e-08
ADAM_WD = 0.01
ADAM_STEP = 10

VMEM_LIMIT_V7X = 56 * 1024 * 1024
MESH = pl.DeviceIdType.MESH


def _cparams(sem):
    return pltpu.CompilerParams(dimension_semantics=sem, vmem_limit_bytes=VMEM_LIMIT_V7X)


def _sigmoid(v):
    return 1.0 / (1.0 + jnp.exp(-v))


def _rowmap(name, body, row_ins, bcast_ins, row_outs, sum_outs=(), tm=256, after=()):
    m = row_ins[0].shape[0]
    n_in = len(row_ins) + len(bcast_ins)
    n_ro = len(row_outs)

    def kern(*refs):
        vals = [r[...] for r in refs[:n_in]]
        res = body(*vals)
        if not isinstance(res, (tuple, list)):
            res = (res,)
        outs = refs[n_in + len(after):]
        for r, v in zip(outs[:n_ro], res[:n_ro]):
            r[...] = v.astype(r.dtype)
        if sum_outs:
            @pl.when(pl.program_id(0) == 0)
            def _():
                for r in outs[n_ro:]:
                    r[...] = jnp.zeros_like(r)
            for r, v in zip(outs[n_ro:], res[n_ro:]):
                r[...] += v

    in_specs = [pl.BlockSpec((tm, a.shape[1]), lambda i: (i, 0)) for a in row_ins]
    in_specs += [pl.BlockSpec(a.shape, lambda i: (0, 0)) for a in bcast_ins]
    in_specs += [pl.BlockSpec(memory_space=pl.ANY)] * len(after)
    out_specs = [pl.BlockSpec((tm, n), lambda i: (i, 0)) for n, _ in row_outs]
    out_specs += [pl.BlockSpec((1, n), lambda i: (0, 0)) for n in sum_outs]
    out_shape = [jax.ShapeDtypeStruct((m, n), dt) for n, dt in row_outs]
    out_shape += [jax.ShapeDtypeStruct((1, n), F32) for n in sum_outs]
    return pl.pallas_call(
        kern, name=name, grid=(m // tm,), in_specs=in_specs, out_specs=out_specs,
        out_shape=out_shape, compiler_params=_cparams(("arbitrary",)),
    )(*row_ins, *bcast_ins, *after)


TM, TN = 1024, 1024


def _matmul(name, a, b, kind, m, n, k, outs, *, b_off=0, tm=TM, tn=TN, tk=1024,
            epilogue=None, extras=(), carry=None, after=()):
    tm, tn, tk = min(tm, m), min(tn, n), min(tk, k)
    nk = k // tk
    if kind == "nn":
        a_spec = pl.BlockSpec((tm, tk), lambda i, j, kk: (i, kk))
        b_spec = pl.BlockSpec((tk, tn), lambda i, j, kk: (kk, b_off // tn + j))
        dn = (((1,), (0,)), ((), ()))
    elif kind == "nt":
        a_spec = pl.BlockSpec((tm, tk), lambda i, j, kk: (i, kk))
        b_spec = pl.BlockSpec((tn, tk), lambda i, j, kk: (j, b_off // tk + kk))
        dn = (((1,), (1,)), ((), ()))
    else:
        a_spec = pl.BlockSpec((tk, tm), lambda i, j, kk: (kk, i))
        b_spec = pl.BlockSpec((tk, tn), lambda i, j, kk: (kk, j))
        dn = (((0,), (0,)), ((), ()))
    n_ex, n_out = len(extras), len(outs)
    if epilogue is None:
        epilogue = lambda acc: (acc,)

    def finish(acc, ex_refs, out_refs):
        res = epilogue(acc, *[r[...] for r in ex_refs])
        for r, v in zip(out_refs, res):
            r[...] = v.astype(r.dtype)

    n_c = 0 if carry is None else 1
    n_in = n_ex + n_c + len(after)
    ni, nj = m // tm, n // tn

    def kern(a_ref, b_ref, *rest):
        ex_refs = rest[:n_ex]
        out_refs = rest[n_in:n_in + n_out]
        scratch = rest[n_in + n_out + n_c:]
        i, j, kk = pl.program_id(0), pl.program_id(1), pl.program_id(2)
        if carry is not None:
            copies = lambda: _ici_copies(rest[n_ex], rest[n_in + n_out], scratch[-2], scratch[-1],
                                         carry[1], carry[2])

            @pl.when(jnp.logical_and(jnp.logical_and(i == 0, j == 0), kk == 0))
            def _():
                for cp in copies():
                    cp.start()

        part = lax.dot_general(a_ref[...], b_ref[...], dn, preferred_element_type=F32)
        if nk == 1:
            finish(part, ex_refs, out_refs)
        else:
            acc_ref = scratch[0]

            @pl.when(kk == 0)
            def _():
                acc_ref[...] = part

            @pl.when(kk > 0)
            def _():
                acc_ref[...] += part

            @pl.when(kk == nk - 1)
            def _():
                finish(acc_ref[...], ex_refs, out_refs)

        if carry is not None:
            @pl.when(jnp.logical_and(jnp.logical_and(i == ni - 1, j == nj - 1), kk == nk - 1))
            def _():
                for cp in copies():
                    cp.wait_recv()
                for cp in copies():
                    cp.wait_send()

    hbm = pl.BlockSpec(memory_space=pl.ANY)
    in_specs = [a_spec, b_spec] + [pl.BlockSpec(bs, im) for _, bs, im in extras]
    in_specs += [hbm] * (n_c + len(after))
    out_specs = [pl.BlockSpec((tm, tn), lambda i, j, kk: (i, j)) for _ in outs] + [hbm] * n_c
    out_shape = [jax.ShapeDtypeStruct((m, n), dt) for dt in outs]
    scratch_shapes = [] if nk == 1 else [pltpu.VMEM((tm, tn), F32)]
    operands = [a, b] + [e[0] for e in extras]
    if carry is not None:
        r, cw = carry[2]
        out_shape.append(jax.ShapeDtypeStruct((3, r // 2, cw), BF16))
        scratch_shapes += [pltpu.SemaphoreType.DMA((3,)), pltpu.SemaphoreType.DMA((3,))]
        operands.append(carry[0])
    operands += list(after)
    sem = ("arbitrary",) * 3 if carry is not None else ("parallel", "parallel", "arbitrary")
    return pl.pallas_call(
        kern, name=name, grid=(ni, nj, nk), in_specs=in_specs, out_specs=out_specs,
        out_shape=out_shape, scratch_shapes=scratch_shapes, compiler_params=_cparams(sem),
    )(*operands)


def _ici_copies(psum_ref, recv_ref, s_sem, r_sem, axis, shard_shape):
    x, y, c = _me()
    hr, cw = shard_shape[0] // 2, shard_shape[1]
    pick = lambda sems, j: sems[j] if isinstance(sems, (list, tuple)) else sems.at[j]
    copies = []
    for j, (fx, fy) in enumerate(_CHIP_FLIPS):
        chip = 2 * (x ^ fx) + (y ^ fy)
        src = psum_ref.at[:, pl.ds(chip * cw, cw)] if axis == 1 else psum_ref.at[pl.ds(chip * hr, hr), :]
        copies.append(pltpu.make_async_remote_copy(
            src_ref=src, dst_ref=recv_ref.at[j], send_sem=pick(s_sem, j), recv_sem=pick(r_sem, j),
            device_id=(x ^ fx, y ^ fy, c), device_id_type=MESH))
    return copies


_HBM_SPEC = pl.BlockSpec(memory_space=pltpu.HBM)
_SEM_SPEC = pl.BlockSpec(memory_space=pltpu.SEMAPHORE)


def _ici_start(name, psum, axis, shard_shape):
    r, cw = shard_shape
    land = lax.empty((3, r // 2, cw), BF16)

    def body(p_ref, land_ref, s0, s1, s2, r0, r1, r2, p_thru, land_thru, token):
        for cp in _ici_copies(p_ref, land_ref, [s0, s1, s2], [r0, r1, r2], axis, shard_shape):
            cp.start()
        token[...] = jnp.zeros_like(token)

    sem = pltpu.SemaphoreType.DMA(())
    res = pl.pallas_call(
        body, name=name,
        out_shape=(sem,) * 6 + (pltpu.HBM(psum.shape, BF16), pltpu.HBM(land.shape, BF16),
                                jax.ShapeDtypeStruct((8, 128), F32)),
        in_specs=(_HBM_SPEC, _HBM_SPEC),
        out_specs=(_SEM_SPEC,) * 6 + (_HBM_SPEC, _HBM_SPEC, pl.BlockSpec(memory_space=pltpu.VMEM)),
        input_output_aliases={0: 6, 1: 7},
        compiler_params=pltpu.CompilerParams(has_side_effects=pltpu.SideEffectType.DATAFLOW_SIDE_EFFECTING),
    )(pltpu.with_memory_space_constraint(psum, pltpu.HBM),
      pltpu.with_memory_space_constraint(land, pltpu.HBM))
    return res[:6], res[6], res[7], res[8]


def _ici_wait(name, sems, p_thru, land_thru, axis, shard_shape, after):
    n_after = len(after)

    def body(p_ref, land_ref, s0, s1, s2, r0, r1, r2, *rest):
        for cp in _ici_copies(p_ref, land_ref, [s0, s1, s2], [r0, r1, r2], axis, shard_shape):
            cp.wait_send()
            cp.wait_recv()

    return pl.pallas_call(
        body, name=name,
        out_shape=(pltpu.HBM(p_thru.shape, BF16), pltpu.HBM(land_thru.shape, BF16)),
        in_specs=(_HBM_SPEC, _HBM_SPEC) + (_SEM_SPEC,) * 6 + (pl.BlockSpec(memory_space=pl.ANY),) * n_after,
        out_specs=(_HBM_SPEC, _HBM_SPEC), input_output_aliases={0: 0, 1: 1},
        compiler_params=pltpu.CompilerParams(has_side_effects=pltpu.SideEffectType.DATAFLOW_SIDE_EFFECTING),
    )(p_thru, land_thru, *sems, *after)


def _where_am_i():
    x, y, c = _me()
    return jnp.stack([c, 2 * x + y]).astype(I32)


def _sibling():
    x, y, c = _me()
    return (x, y, 1 - c)


def _matmul_tn_pair(name, pos, a, b, m, n, k, shard_rows, *, tm, tn, tk):
    hr = shard_rows // 2
    tm, tn, tk = min(tm, hr), min(tn, n), min(tk, k)
    tph = hr // tm
    nt, nj, nk = (m // 2) // tm, n // tn, k // tk
    n_tiles = nt * nj

    def row_block(p, t, pos_ref):
        half = jnp.where(p == 0, 1 - pos_ref[0], pos_ref[0])
        return (t // tph) * (2 * tph) + half * tph + t % tph

    def kern(pos_ref, a_ref, b_ref, o_ref, acc_ref, send_buf, land_buf, s_sem, r_sem):
        p, t, j, kk = pl.program_id(0), pl.program_id(1), pl.program_id(2), pl.program_id(3)
        idx = t * nj + j
        sib = _sibling()

        def copy(i):
            return pltpu.make_async_remote_copy(
                src_ref=send_buf.at[i], dst_ref=land_buf.at[i], send_sem=s_sem.at[i],
                recv_sem=r_sem.at[i], device_id=sib, device_id_type=MESH)

        part = lax.dot_general(a_ref[...], b_ref[...], _TN, preferred_element_type=F32)

        @pl.when(kk == 0)
        def _():
            acc_ref[...] = part

        @pl.when(kk > 0)
        def _():
            acc_ref[...] += part

        @pl.when(jnp.logical_and(kk == nk - 1, p == 0))
        def _():
            send_buf[idx] = acc_ref[...].astype(BF16)
            copy(idx).start()

        @pl.when(jnp.logical_and(kk == nk - 1, p == 1))
        def _():
            copy(idx).wait_recv()
            o_ref[...] = (acc_ref[...] + land_buf[idx].astype(F32)).astype(BF16)

        @pl.when(jnp.logical_and(jnp.logical_and(p == 1, idx == n_tiles - 1), kk == nk - 1))
        def _():
            for i in range(n_tiles):
                copy(i).wait_send()

    grid_spec = pltpu.PrefetchScalarGridSpec(
        num_scalar_prefetch=1, grid=(2, nt, nj, nk),
        in_specs=[pl.BlockSpec((tk, tm), lambda p, t, j, kk, pos_ref: (kk, row_block(p, t, pos_ref))),
                  pl.BlockSpec((tk, tn), lambda p, t, j, kk, pos_ref: (kk, j))],
        out_specs=pl.BlockSpec((tm, tn), lambda p, t, j, kk, pos_ref: (p * t, p * j)),
        scratch_shapes=[pltpu.VMEM((tm, tn), F32), pltpu.VMEM((n_tiles, tm, tn), BF16),
                        pltpu.VMEM((n_tiles, tm, tn), BF16),
                        pltpu.SemaphoreType.DMA((n_tiles,)), pltpu.SemaphoreType.DMA((n_tiles,))])
    return pl.pallas_call(
        kern, name=name, grid_spec=grid_spec, out_shape=jax.ShapeDtypeStruct((m // 2, n), BF16),
        compiler_params=_cparams(("arbitrary",) * 4),
    )(pos, a, b)


def _rope_tables():
    half = RET_DK // 2
    f32 = np.float32
    inv = np.power(f32(ROPE_BASE), -np.arange(half, dtype=f32) / f32(half)).astype(f32)
    ang = (np.arange(SEQ, dtype=f32)[:, None] * inv[None, :]).astype(f32)
    return jnp.asarray(np.cos(ang).astype(f32)), jnp.asarray(np.sin(ang).astype(f32))


def _decay_tables():
    c = RET_CHUNK
    f32 = np.float32
    log_g = np.log1p(-np.power(f32(2.0), f32(-5.0) - np.arange(RET_HEADS, dtype=f32))).astype(f32)
    idx = np.arange(c, dtype=f32)
    rel = idx[:, None] - idx[None, :]
    din = np.where(rel >= 0, np.exp(log_g[:, None, None] * np.maximum(rel, f32(0.0))), f32(0.0)).astype(f32)
    qd = np.exp(log_g[:, None] * (idx + f32(1.0))).astype(f32)[:, :, None]
    kd = np.exp(log_g[:, None] * (f32(c) - f32(1.0) - idx)).astype(f32)[:, :, None]
    cd = np.exp(log_g * f32(c)).astype(f32)
    return jnp.asarray(din), jnp.asarray(qd), jnp.asarray(kd), jnp.asarray(cd)


def _t5_bucket(dist):
    max_exact = REL_BUCKETS // 2
    d_f = jnp.maximum(dist, 1).astype(F32)
    large = max_exact + (jnp.log(d_f / max_exact) / math.log(REL_MAX_DIST / max_exact)
                         * (REL_BUCKETS - max_exact)).astype(I32)
    large = jnp.minimum(large, REL_BUCKETS - 1)
    return jnp.where(dist < max_exact, dist, large)


def _bucket_tables():
    qi = jnp.arange(ATT_BLK)[:, None]
    kj = jnp.arange(2 * ATT_BLK)[None, :]
    dist = jnp.clip(ATT_BLK + qi - kj, 0, ATT_BLK)
    return jnp.stack([_t5_bucket(dist * dil) for _, dil in ATT_GROUPS]).astype(I32)


def _permute_rows(t, dil):
    if dil == 1:
        return t
    s, w = t.shape
    return t.reshape(s // dil, dil, w).transpose(1, 0, 2).reshape(s, w)


def _unpermute_rows(t, dil):
    if dil == 1:
        return t
    s, w = t.shape
    return t.reshape(dil, s // dil, w).transpose(1, 0, 2).reshape(s, w)


def _retention_fwd(rqk, rv, din, qd, kd, cd):
    nc = SEQ // RET_CHUNK
    c, dk, dv = RET_CHUNK, RET_DK, RET_DV

    def kern(q_ref, k_ref, v_ref, din_ref, qd_ref, kd_ref, cd_ref, o_ref, st_ref, state):
        n = pl.program_id(0)

        @pl.when(n == 0)
        def _():
            state[...] = jnp.zeros_like(state)

        for h in range(RET_HEADS):
            q, k = q_ref[:, h * dk:(h + 1) * dk], k_ref[:, h * dk:(h + 1) * dk]
            v = v_ref[:, h * dv:(h + 1) * dv]
            s_b = state[h].astype(BF16)
            st_ref[h] = s_b
            a = lax.dot_general(q, k, _NT, preferred_element_type=F32) * din_ref[h]
            o = jnp.dot(a.astype(BF16), v, preferred_element_type=F32)
            o += jnp.dot(q, s_b, preferred_element_type=F32) * qd_ref[h]
            o_ref[:, h * dv:(h + 1) * dv] = o
            kk = (k.astype(F32) * kd_ref[h]).astype(BF16)
            state[h] = state[h] * cd_ref[h] + lax.dot_general(kk, v, _TN, preferred_element_type=F32)

    whole = lambda a: pl.BlockSpec(a.shape, lambda n: (0,) * a.ndim)
    return pl.pallas_call(
        kern, name="retention_fwd", grid=(nc,),
        in_specs=[
            pl.BlockSpec((c, RET_QK_W), lambda n: (n, 0)),
            pl.BlockSpec((c, RET_QK_W), lambda n: (n, 1)),
            pl.BlockSpec((c, RET_V_W), lambda n: (n, 0)),
            whole(din), whole(qd), whole(kd),
            pl.BlockSpec(memory_space=pltpu.SMEM),
        ],
        out_specs=[
            pl.BlockSpec((c, RET_V_W), lambda n: (n, 0)),
            pl.BlockSpec((RET_HEADS, None, dk, dv), lambda n: (0, n, 0, 0)),
        ],
        out_shape=[
            jax.ShapeDtypeStruct((SEQ, RET_V_W), F32),
            jax.ShapeDtypeStruct((RET_HEADS, nc, dk, dv), BF16),
        ],
        scratch_shapes=[pltpu.VMEM((RET_HEADS, dk, dv), F32)],
        compiler_params=_cparams(("arbitrary",)),
    )(rqk, rqk, rv, din, qd, kd, cd)


def _retention_bwd(rqk, rv, states, d_ro, din, qd, kd, cd, cos, sin):
    nc = SEQ // RET_CHUNK
    c, dk, dv = RET_CHUNK, RET_DK, RET_DV
    half = dk // 2
    last = nc - 1

    def unrot(g, cs, sn):
        g1, g2 = g[:, :half], g[:, half:]
        return jnp.concatenate([g1 * cs + g2 * sn, g2 * cs - g1 * sn], axis=-1)

    def kern(q_ref, k_ref, v_ref, st_ref, do_ref, din_ref, qd_ref, kd_ref, cd_ref, cos_ref, sin_ref,
             dq_ref, dk_ref, dv_ref, dstate):
        step = pl.program_id(0)

        @pl.when(step == 0)
        def _():
            dstate[...] = jnp.zeros_like(dstate)

        cs, sn = cos_ref[...], sin_ref[...]
        for h in range(RET_HEADS):
            qk_cols, v_cols = slice(h * dk, (h + 1) * dk), slice(h * dv, (h + 1) * dv)
            q, k, v, s_b = q_ref[:, qk_cols], k_ref[:, qk_cols], v_ref[:, v_cols], st_ref[h]
            d_o = do_ref[:, v_cols]
            d_ob = d_o.astype(BF16)
            d_oq = (d_o * qd_ref[h]).astype(BF16)
            ds_b = dstate[h].astype(BF16)
            din_m = din_ref[h]
            a_b = (lax.dot_general(q, k, _NT, preferred_element_type=F32) * din_m).astype(BF16)
            kk = (k.astype(F32) * kd_ref[h]).astype(BF16)
            d_v = lax.dot_general(a_b, d_ob, _TN, preferred_element_type=F32)
            d_v += jnp.dot(kk, ds_b, preferred_element_type=F32)
            d_a = (lax.dot_general(d_ob, v, _NT, preferred_element_type=F32) * din_m).astype(BF16)
            d_q = jnp.dot(d_a, k, preferred_element_type=F32)
            d_q += lax.dot_general(d_oq, s_b, _NT, preferred_element_type=F32)
            d_k = lax.dot_general(d_a, q, _TN, preferred_element_type=F32)
            d_k += lax.dot_general(v, ds_b, _NT, preferred_element_type=F32) * kd_ref[h]
            dstate[h] = dstate[h] * cd_ref[h] + lax.dot_general(q, d_oq, _TN, preferred_element_type=F32)
            dq_ref[:, qk_cols] = unrot(d_q, cs, sn).astype(BF16)
            dk_ref[:, qk_cols] = (unrot(d_k, cs, sn) * (RET_DK ** -0.5)).astype(BF16)
            dv_ref[:, v_cols] = d_v.astype(BF16)

    whole = lambda a: pl.BlockSpec(a.shape, lambda n: (0,) * a.ndim)
    return pl.pallas_call(
        kern, name="retention_bwd", grid=(nc,),
        in_specs=[
            pl.BlockSpec((c, RET_QK_W), lambda n: (last - n, 0)),
            pl.BlockSpec((c, RET_QK_W), lambda n: (last - n, 1)),
            pl.BlockSpec((c, RET_V_W), lambda n: (last - n, 0)),
            pl.BlockSpec((RET_HEADS, None, dk, dv), lambda n: (0, last - n, 0, 0)),
            pl.BlockSpec((c, RET_V_W), lambda n: (last - n, 0)),
            whole(din), whole(qd), whole(kd),
            pl.BlockSpec(memory_space=pltpu.SMEM),
            pl.BlockSpec((c, half), lambda n: (last - n, 0)),
            pl.BlockSpec((c, half), lambda n: (last - n, 0)),
        ],
        out_specs=[
            pl.BlockSpec((c, RET_QK_W), lambda n: (last - n, 0)),
            pl.BlockSpec((c, RET_QK_W), lambda n: (last - n, 0)),
            pl.BlockSpec((c, RET_V_W), lambda n: (last - n, 0)),
        ],
        out_shape=[
            jax.ShapeDtypeStruct((SEQ, RET_QK_W), BF16),
            jax.ShapeDtypeStruct((SEQ, RET_QK_W), BF16),
            jax.ShapeDtypeStruct((SEQ, RET_V_W), BF16),
        ],
        scratch_shapes=[pltpu.VMEM((RET_HEADS, dk, dv), F32)],
        compiler_params=_cparams(("arbitrary",)),
    )(rqk, rqk, rv, states, d_ro, din, qd, kd, cd, cos, sin)


def _bias_build(rel_bias, buckets):
    ng = len(ATT_GROUPS)

    def kern(tab_ref, bkt_ref, o_ref):
        g, h = pl.program_id(0), pl.program_id(1)
        bkt = bkt_ref[...]
        acc = jnp.zeros(bkt.shape, F32)
        for b in range(REL_BUCKETS):
            acc = jnp.where(bkt == b, tab_ref[b, g * ATT_HPG + h], acc)
        o_ref[...] = acc

    return pl.pallas_call(
        kern, name="bias_build", grid=(ng, ATT_HPG),
        in_specs=[pl.BlockSpec(memory_space=pltpu.SMEM),
                  pl.BlockSpec((None, ATT_BLK, 2 * ATT_BLK), lambda g, h: (g, 0, 0))],
        out_specs=pl.BlockSpec((None, None, ATT_BLK, 2 * ATT_BLK), lambda g, h: (g, h, 0, 0)),
        out_shape=jax.ShapeDtypeStruct((ng, ATT_HPG, ATT_BLK, 2 * ATT_BLK), F32),
        compiler_params=_cparams(("arbitrary", "arbitrary")),
    )(rel_bias, buckets)


def _bias_grad(dsb, buckets):
    ng = len(ATT_GROUPS)

    def kern(ds_ref, bkt_ref, o_ref):
        g, h = pl.program_id(0), pl.program_id(1)
        bkt, ds = bkt_ref[...], ds_ref[...]
        for b in range(REL_BUCKETS):
            o_ref[b, g * ATT_HPG + h] = jnp.sum(jnp.where(bkt == b, ds, 0.0))

    return pl.pallas_call(
        kern, name="bias_grad", grid=(ng, ATT_HPG),
        in_specs=[pl.BlockSpec((None, None, ATT_BLK, 2 * ATT_BLK), lambda g, h: (g, h, 0, 0)),
                  pl.BlockSpec((None, ATT_BLK, 2 * ATT_BLK), lambda g, h: (g, 0, 0))],
        out_specs=pl.BlockSpec(memory_space=pltpu.SMEM),
        out_shape=jax.ShapeDtypeStruct((REL_BUCKETS, N_ATT_HEADS), F32),
        compiler_params=_cparams(("arbitrary", "arbitrary")),
    )(dsb, buckets)


_NT = (((1,), (1,)), ((), ()))
_TN = (((0,), (0,)), ((), ()))
_ATT_SCALE = ATT_DH ** -0.5


_PAD_ROWS = SEQ + ATT_BLK


def _window_mask(has_prev):
    qi = lax.broadcasted_iota(I32, (ATT_BLK, 2 * ATT_BLK), 0)
    kj = lax.broadcasted_iota(I32, (ATT_BLK, 2 * ATT_BLK), 1)
    prev_ok = jnp.logical_and(jnp.logical_and(kj < ATT_BLK, kj >= qi), has_prev)
    return jnp.logical_or(prev_ok, jnp.logical_and(kj >= ATT_BLK, qi >= kj - ATT_BLK))


def _head_specs(col0):
    return pl.BlockSpec((SEQ, ATT_DH), lambda h: (0, col0 + h))


def _att_fwd(gi, qkv, bias, nb):
    blk, dh = ATT_BLK, ATT_DH

    def kern(q_ref, k_ref, v_ref, b_ref, o_ref, l_ref, kpad, vpad):
        zero = jnp.zeros((blk, dh), BF16)
        kpad[0:blk, :] = zero
        vpad[0:blk, :] = zero
        kpad[blk:, :] = k_ref[...]
        vpad[blk:, :] = v_ref[...]
        bias_m = b_ref[...]

        def body(b, carry):
            r0 = pl.multiple_of(b * blk, blk)
            q = q_ref[pl.ds(r0, blk), :]
            kw = kpad[pl.ds(r0, 2 * blk), :]
            vw = vpad[pl.ds(r0, 2 * blk), :]
            valid = _window_mask((b % nb) > 0)
            s = lax.dot_general(q, kw, _NT, preferred_element_type=F32) * _ATT_SCALE + bias_m
            s = jnp.where(valid, s, -1e30)
            mx = jnp.max(s, axis=-1, keepdims=True)
            e = jnp.exp(s - mx)
            den = jnp.sum(e, axis=-1, keepdims=True)
            o_ref[pl.ds(r0, blk), :] = jnp.dot((e / den).astype(BF16), vw, preferred_element_type=F32)
            l_ref[pl.ds(r0, blk), :] = jnp.broadcast_to(mx + jnp.log(den), (blk, dh))
            return carry

        lax.fori_loop(0, N_BLK, body, 0, unroll=2)

    return pl.pallas_call(
        kern, name=f"att_fwd_g{gi}", grid=(ATT_HPG,),
        in_specs=[_head_specs(0), _head_specs(ATT_HPG), _head_specs(2 * ATT_HPG),
                  pl.BlockSpec((None, None, blk, 2 * blk), lambda h: (gi, h, 0, 0))],
        out_specs=[_head_specs(0), _head_specs(0)],
        out_shape=[jax.ShapeDtypeStruct((SEQ, ATT_W), F32), jax.ShapeDtypeStruct((SEQ, ATT_W), F32)],
        scratch_shapes=[pltpu.VMEM((_PAD_ROWS, dh), BF16), pltpu.VMEM((_PAD_ROWS, dh), BF16)],
        compiler_params=_cparams(("arbitrary",)),
    )(qkv, qkv, qkv, bias)


def _att_bwd(gi, qkv, d_att, lse, dd, bias, nb):
    blk, dh = ATT_BLK, ATT_DH

    def kern(q_ref, k_ref, v_ref, do_ref, l_ref, d_ref, b_ref, dq_ref, dk_ref, dv_ref, dsb_ref,
             kpad, vpad, qpad, dopad, lpad, dpad):
        zero = jnp.zeros((blk, dh), BF16)
        zero_f = jnp.zeros((blk, dh), F32)
        kpad[0:blk, :] = zero
        vpad[0:blk, :] = zero
        kpad[blk:, :] = k_ref[...]
        vpad[blk:, :] = v_ref[...]
        qpad[SEQ:, :] = zero
        dopad[SEQ:, :] = zero
        lpad[SEQ:, :] = zero_f
        dpad[SEQ:, :] = zero_f
        qpad[0:SEQ, :] = q_ref[...]
        dopad[0:SEQ, :] = do_ref[...]
        lpad[0:SEQ, :] = l_ref[...]
        dpad[0:SEQ, :] = d_ref[...]
        bias_m = b_ref[...]
        bias_t = jnp.concatenate([bias_m[:, blk:], bias_m[:, :blk]], axis=0)
        dsb_ref[...] = jnp.zeros_like(dsb_ref)

        def dq_body(b, carry):
            r0 = pl.multiple_of(b * blk, blk)
            q, d_o = q_ref[pl.ds(r0, blk), :], do_ref[pl.ds(r0, blk), :]
            kw, vw = kpad[pl.ds(r0, 2 * blk), :], vpad[pl.ds(r0, 2 * blk), :]
            lrow, drow = l_ref[pl.ds(r0, blk), :][:, :1], d_ref[pl.ds(r0, blk), :][:, :1]
            valid = _window_mask((b % nb) > 0)
            s = lax.dot_general(q, kw, _NT, preferred_element_type=F32) * _ATT_SCALE + bias_m
            p = jnp.where(valid, jnp.exp(jnp.where(valid, s, -1e30) - lrow), 0.0)
            dp = lax.dot_general(d_o, vw, _NT, preferred_element_type=F32)
            ds = p * (dp - drow)
            dq = jnp.dot(ds.astype(BF16), kw, preferred_element_type=F32)
            dq_ref[pl.ds(r0, blk), :] = (dq * _ATT_SCALE).astype(BF16)
            dsb_ref[...] += ds
            return carry

        lax.fori_loop(0, N_BLK, dq_body, 0, unroll=2)

        qi = lax.broadcasted_iota(I32, (2 * blk, blk), 0)
        kj = lax.broadcasted_iota(I32, (2 * blk, blk), 1)

        def dkv_body(b, carry):
            r0 = pl.multiple_of(b * blk, blk)
            k, v = k_ref[pl.ds(r0, blk), :], v_ref[pl.ds(r0, blk), :]
            qw, dow = qpad[pl.ds(r0, 2 * blk), :], dopad[pl.ds(r0, 2 * blk), :]
            lrow, drow = lpad[pl.ds(r0, 2 * blk), :][:, :1], dpad[pl.ds(r0, 2 * blk), :][:, :1]
            has_next = jnp.logical_and(b + 1 < N_BLK, ((b + 1) % nb) > 0)
            next_ok = jnp.logical_and(jnp.logical_and(qi >= blk, kj >= qi - blk), has_next)
            valid = jnp.logical_or(jnp.logical_and(qi < blk, qi >= kj), next_ok)
            s = lax.dot_general(qw, k, _NT, preferred_element_type=F32) * _ATT_SCALE + bias_t
            p = jnp.where(valid, jnp.exp(jnp.where(valid, s, -1e30) - lrow), 0.0)
            dp = lax.dot_general(dow, v, _NT, preferred_element_type=F32)
            ds = p * (dp - drow)
            d_v = lax.dot_general(p.astype(BF16), dow, _TN, preferred_element_type=F32)
            d_k = lax.dot_general(ds.astype(BF16), qw, _TN, preferred_element_type=F32)
            dk_ref[pl.ds(r0, blk), :] = (d_k * _ATT_SCALE).astype(BF16)
            dv_ref[pl.ds(r0, blk), :] = d_v.astype(BF16)
            return carry

        lax.fori_loop(0, N_BLK, dkv_body, 0, unroll=2)

    return pl.pallas_call(
        kern, name=f"att_bwd_g{gi}", grid=(ATT_HPG,),
        in_specs=[_head_specs(0), _head_specs(ATT_HPG), _head_specs(2 * ATT_HPG),
                  _head_specs(0), _head_specs(0), _head_specs(0),
                  pl.BlockSpec((None, None, blk, 2 * blk), lambda h: (gi, h, 0, 0))],
        out_specs=[_head_specs(0), _head_specs(0), _head_specs(0),
                   pl.BlockSpec((None, blk, 2 * blk), lambda h: (h, 0, 0))],
        out_shape=[jax.ShapeDtypeStruct((SEQ, ATT_W), BF16)] * 3
        + [jax.ShapeDtypeStruct((ATT_HPG, blk, 2 * blk), F32)],
        scratch_shapes=[pltpu.VMEM((_PAD_ROWS, dh), BF16)] * 4 + [pltpu.VMEM((_PAD_ROWS, dh), F32)] * 2,
        compiler_params=_cparams(("arbitrary",)),
    )(qkv, qkv, qkv, d_att, lse, dd, bias)


def _rms_parts(x):
    r = lax.rsqrt(jnp.mean(x * x, axis=-1, keepdims=True) + RMS_EPS)
    return x * r, r


def _rms_bwd(d_xhat, xhat, r):
    return r * (d_xhat - xhat * jnp.mean(d_xhat * xhat, axis=-1, keepdims=True))


def _prenorm_fwd(name, x, gain, shift, scale):
    def body(xt, g, sh, sc):
        xhat, _ = _rms_parts(xt)
        return (xhat * g) * (1.0 + sc) + sh
    return _rowmap(name, body, [x], [gain, shift, scale], [(D_MODEL, BF16)])[0]


def _prenorm_bwd(name, d_hs, x, gain, scale, resid, after=()):
    n_dh = len(d_hs)

    def body(*args):
        d_h = args[0]
        for t in args[1:n_dh]:
            d_h = d_h + t
        xt, res, g, sc = args[n_dh:]
        xhat, r = _rms_parts(xt)
        nrm = xhat * g
        d_n = d_h * (1.0 + sc)
        dx = _rms_bwd(d_n * g, xhat, r) + res
        return (dx, jnp.sum(d_h, axis=0, keepdims=True), jnp.sum(d_h * nrm, axis=0, keepdims=True),
                jnp.sum(d_n * xhat, axis=0, keepdims=True))

    return _rowmap(name, body, list(d_hs) + [x, resid], [gain, scale], [(D_MODEL, F32)],
                   [D_MODEL, D_MODEL, D_MODEL], after=after)


def _gn_parts(ro):
    mu = jnp.mean(ro, axis=-1, keepdims=True)
    cen = ro - mu
    rstd = lax.rsqrt(jnp.mean(cen * cen, axis=-1, keepdims=True) + GN_EPS)
    return cen * rstd, rstd


def _retpost_fwd(ro, rg, gn_g, gn_b):
    def body(rot, rgt, g, b):
        outs = []
        for h in range(RET_HEADS):
            sl = slice(h * RET_DV, (h + 1) * RET_DV)
            nrm, _ = _gn_parts(rot[:, sl])
            gate = rgt[:, sl]
            outs.append((gate * _sigmoid(gate)) * (nrm * g[:, sl] + b[:, sl]))
        return jnp.concatenate(outs, axis=-1)
    return _rowmap("retpost_fwd", body, [ro, rg], [gn_g, gn_b], [(RET_V_W, BF16)])[0]


def _retpost_bwd(d_gated, ro, rg, gn_g, gn_b):
    def body(dgt, rot, rgt, g, b):
        d_ro, d_rg, d_g, d_b = [], [], [], []
        for h in range(RET_HEADS):
            sl = slice(h * RET_DV, (h + 1) * RET_DV)
            nrm, rstd = _gn_parts(rot[:, sl])
            gate, dg = rgt[:, sl], dgt[:, sl]
            sg = _sigmoid(gate)
            ron = nrm * g[:, sl] + b[:, sl]
            d_rg.append(dg * ron * (sg * (1.0 + gate * (1.0 - sg))))
            d_ron = dg * (gate * sg)
            d_g.append(jnp.sum(d_ron * nrm, axis=0, keepdims=True))
            d_b.append(jnp.sum(d_ron, axis=0, keepdims=True))
            d_n = d_ron * g[:, sl]
            d_ro.append(rstd * (d_n - jnp.mean(d_n, axis=-1, keepdims=True)
                                - nrm * jnp.mean(d_n * nrm, axis=-1, keepdims=True)))
        cat = lambda ts: jnp.concatenate(ts, axis=-1)
        return cat(d_ro), cat(d_rg), cat(d_g), cat(d_b)
    return _rowmap("retpost_bwd", body, [d_gated, ro, rg], [gn_g, gn_b],
                   [(RET_V_W, F32), (RET_V_W, BF16)], [RET_V_W, RET_V_W])


def _combine(os_, ls_):
    def body(o0, o1, o2, l0, l1, l2):
        mx = jnp.maximum(jnp.maximum(l0, l1), l2)
        e0, e1, e2 = jnp.exp(l0 - mx), jnp.exp(l1 - mx), jnp.exp(l2 - mx)
        den = e0 + e1 + e2
        att = (e0 / den) * o0 + (e1 / den) * o1 + (e2 / den) * o2
        return att, att, mx + jnp.log(den)
    return _rowmap("att_combine", body, list(os_) + list(ls_), [],
                   [(ATT_W, F32), (ATT_W, BF16), (ATT_W, F32)])


def _att_bwd_pre(d_att, att):
    def body(dt, at):
        outs = []
        for h in range(ATT_HPG):
            sl = slice(h * ATT_DH, (h + 1) * ATT_DH)
            outs.append(jnp.broadcast_to(jnp.sum(dt[:, sl] * at[:, sl], axis=-1, keepdims=True),
                                         (dt.shape[0], ATT_DH)))
        return dt, jnp.concatenate(outs, axis=-1)
    return _rowmap("att_bwd_pre", body, [d_att, att], [], [(ATT_W, BF16), (ATT_W, F32)])


def _merge_fwd(gates, ret_out, att_out):
    def body(gt, ro, ao):
        return _sigmoid(gt[:, :D_MODEL]) * ro + _sigmoid(gt[:, D_MODEL:]) * ao
    return _rowmap("merge_fwd", body, [gates, ret_out, att_out], [], [(D_MODEL, BF16)])[0]


def _merge_bwd(d_merged, gates, ret_out, att_out):
    def body(dm, gt, ro, ao):
        sa, sb = _sigmoid(gt[:, :D_MODEL]), _sigmoid(gt[:, D_MODEL:])
        d_gates = jnp.concatenate([dm * ro * (sa * (1.0 - sa)), dm * ao * (sb * (1.0 - sb))], axis=-1)
        return dm * sa, dm * sb, d_gates
    return _rowmap("merge_bwd", body, [d_merged, gates, ret_out, att_out], [],
                   [(D_MODEL, BF16), (D_MODEL, BF16), (2 * D_MODEL, BF16)])


def _gate_bwd(name, d_x, branch, gate):
    def body(dx, br, g):
        return dx * g, jnp.sum(dx * br, axis=0, keepdims=True)
    return _rowmap(name, body, [d_x, branch], [gate], [(D_MODEL, BF16)], [D_MODEL])


def _loss_head(x3, target, gain):
    def body(xt, tt, g):
        xhat, r = _rms_parts(xt)
        err = xhat * g - tt
        d_y = err / D_MODEL
        loss = 0.5 * jnp.sum(jnp.mean(err * err, axis=-1, keepdims=True), axis=0, keepdims=True)
        d_x = _rms_bwd(d_y * g, xhat, r)
        return d_x, jnp.broadcast_to(loss, (1, 128)), jnp.sum(d_y * xhat, axis=0, keepdims=True)
    return _rowmap("loss_head", body, [x3, target], [gain], [(D_MODEL, F32)], [128, D_MODEL])


def _local_step(pos, x, target, mod, norm1_g, norm2_g, norm_f_g, rel_bias, gn_g, gn_b, w_in, rest_gather):
    sh1, sc1, g1, sh2, sc2, g2 = [mod[:, i * D_MODEL:(i + 1) * D_MODEL] for i in range(6)]
    cos, sin = _rope_tables()
    din, qd, kd, cd = _decay_tables()
    buckets = _bucket_tables()
    bias = _bias_build(rel_bias, buckets)
    dils = [d for _, d in ATT_GROUPS]
    nbs = [SEQ // d // ATT_BLK for d in dils]

    h1 = _prenorm_fwd("prenorm1_fwd", x, norm1_g, sh1, sc1)
    h1_p = [_permute_rows(h1, d) for d in dils]

    def rot_epi(acc, cs, sn, scale):
        half = RET_DK // 2
        x1, x2 = acc[:, :half], acc[:, half:]
        return (jnp.concatenate([x1 * cs - x2 * sn, x1 * sn + x2 * cs], axis=-1) * scale,)

    qk_scale = jnp.concatenate([jnp.ones((1, RET_QK_W), F32),
                                jnp.full((1, RET_QK_W), RET_DK ** -0.5, F32)], axis=-1)
    rope_ex = [(cos, (TM, RET_DK // 2), lambda i, j, kk: (i, 0)),
               (sin, (TM, RET_DK // 2), lambda i, j, kk: (i, 0)),
               (qk_scale, (1, RET_DK), lambda i, j, kk: (0, j))]
    rest_sems, rest_shards, rest_fulls, rest_token = rest_gather
    behind = [rest_token]
    rv = _matmul("proj_rv", h1, w_in, "nn", SEQ, RET_V_W, D_MODEL, [BF16], b_off=OFF_V, tk=D_MODEL,
                 after=behind)[0]
    rg = _matmul("proj_rg", h1, w_in, "nn", SEQ, RET_V_W, D_MODEL, [F32], b_off=OFF_G, tk=D_MODEL,
                 after=behind)[0]
    gates = _matmul("proj_gates", h1, w_in, "nn", SEQ, 2 * D_MODEL, D_MODEL, [F32], b_off=OFF_GATE,
                    tn=512, tk=D_MODEL, after=behind)[0]
    aqkv = [_matmul(f"proj_att_g{gi}", h1_p[gi], w_in, "nn", SEQ, 3 * ATT_W, D_MODEL, [BF16],
                    b_off=OFF_ATT + gi * 3 * ATT_W, tn=512, tk=D_MODEL, after=behind)[0]
            for gi in range(3)]

    os_, ls_ = [], []
    for gi in range(3):
        o_g, l_g = _att_fwd(gi, aqkv[gi], bias, nbs[gi])
        os_.append(_unpermute_rows(o_g, dils[gi]))
        ls_.append(_unpermute_rows(l_g, dils[gi]))
        if gi == 1:
            rest_sems, rest_fulls, fwd_token = _gather_rest_forward(rest_sems, rest_shards, rest_fulls,
                                                                    [o_g, rv, rg, gates])

    rqk = _matmul("proj_qk", h1, w_in, "nn", SEQ, 2 * RET_QK_W, D_MODEL, [BF16], b_off=OFF_Q,
                  tn=RET_DK, tk=D_MODEL, epilogue=rot_epi, extras=rope_ex, after=[fwd_token])[0]
    ro, states = _retention_fwd(rqk, rv, din, qd, kd, cd)
    gated = _retpost_fwd(ro, rg, gn_g, gn_b)
    w_ret_out, w_att_out, w_o, w_ff1, w_ff2 = _gather_rest_end(rest_sems, rest_fulls, [gated, os_[2]])
    ret_out = _matmul("ret_out", gated, w_ret_out, "nn", SEQ, D_MODEL, RET_V_W, [F32])[0]
    att, att_b, lse = _combine(os_, ls_)
    att_out = _matmul("att_out", att_b, w_att_out, "nn", SEQ, D_MODEL, ATT_W, [F32])[0]

    merged = _merge_fwd(gates, ret_out, att_out)

    def resid_epi(acc, xt, g):
        return xt + g * acc, acc

    def resid_ex(xin, g):
        return [(xin, (TM, TN), lambda i, j, kk: (i, j)), (g, (1, TN), lambda i, j, kk: (0, j))]

    x2, mix = _matmul("mix_out", merged, w_o, "nn", SEQ, D_MODEL, D_MODEL, [F32, F32],
                      epilogue=resid_epi, extras=resid_ex(x, g1))
    h2 = _prenorm_fwd("prenorm2_fwd", x2, norm2_g, sh2, sc2)

    def relu2_epi(acc):
        r = jnp.maximum(acc, 0.0)
        return r * r, acc

    act, u = _matmul("ff1", h2, w_ff1, "nn", SEQ, D_FF, D_MODEL, [BF16, F32], tk=D_MODEL,
                     epilogue=relu2_epi)
    x3, y2 = _matmul("ff2", act, w_ff2, "nn", SEQ, D_MODEL, D_FF, [F32, F32],
                     epilogue=resid_epi, extras=resid_ex(x2, g2))

    d_x3, loss, d_gf = _loss_head(x3, target, norm_f_g)

    d_y2, d_g2 = _gate_bwd("ff_gate_bwd", d_x3, y2, g2)

    def relu2_bwd_epi(acc, ut):
        return (acc * (2.0 * jnp.maximum(ut, 0.0)),)

    recv = {}
    gw_ff2 = _matmul_tn_pair("ff2_dw", pos, act, d_y2, D_FF, D_MODEL, SEQ, D_FF // N_CHIPS,
                             tm=512, tn=1024, tk=1024)
    d_u, recv["w_ff2"] = _matmul(
        "ff2_dx", d_y2, w_ff2, "nt", SEQ, D_FF, D_MODEL, [BF16], epilogue=relu2_bwd_epi,
        extras=[(u, (TM, TN), lambda i, j, kk: (i, j))], carry=(gw_ff2, 0, SHARD["w_ff2"]))
    gw_ff1 = _matmul_tn_pair("ff1_dw", pos, h2, d_u, D_MODEL, D_FF, SEQ, D_MODEL,
                             tm=512, tn=1024, tk=1024)
    d_h2, recv["w_ff1"] = _matmul("ff1_dx", d_u, w_ff1, "nt", SEQ, D_MODEL, D_FF, [F32],
                                  carry=(gw_ff1, 1, SHARD["w_ff1"]))
    d_x2, d_sh2, d_sc2, d_n2g = _prenorm_bwd("prenorm2_bwd", [d_h2], x2, norm2_g, sc2, d_x3)

    d_mix, d_g1 = _gate_bwd("mix_gate_bwd", d_x2, mix, g1)
    gw_o = _matmul_tn_pair("mix_dw", pos, merged, d_mix, D_MODEL, D_MODEL, SEQ, D_MODEL // N_CHIPS,
                           tm=128, tn=1024, tk=2048)
    d_merged, recv["w_o"] = _matmul("mix_dx", d_mix, w_o, "nt", SEQ, D_MODEL, D_MODEL, [F32],
                                    carry=(gw_o, 0, SHARD["w_o"]))
    d_ret_out, d_att_out, d_gates = _merge_bwd(d_merged, gates, ret_out, att_out)

    gw_ret_out = _matmul_tn_pair("ret_out_dw", pos, gated, d_ret_out, RET_V_W, D_MODEL, SEQ,
                                 RET_V_W // N_CHIPS, tm=256, tn=1024, tk=1024)
    d_gated, recv["w_ret_out"] = _matmul("ret_out_dx", d_ret_out, w_ret_out, "nt", SEQ, RET_V_W, D_MODEL,
                                         [F32], carry=(gw_ret_out, 0, SHARD["w_ret_out"]))
    gw_att_out = _matmul_tn_pair("att_out_dw", pos, att_b, d_att_out, ATT_W, D_MODEL, SEQ, ATT_W,
                                 tm=256, tn=1024, tk=2048)
    d_att, recv["w_att_out"] = _matmul("att_out_dx", d_att_out, w_att_out, "nt", SEQ, ATT_W, D_MODEL,
                                       [F32], carry=(gw_att_out, 1, SHARD["w_att_out"]))

    d_ro, d_rg, d_gn_g, d_gn_b = _retpost_bwd(d_gated, ro, rg, gn_g, gn_b)
    d_rq, d_rk, d_rv = _retention_bwd(rqk, rv, states, d_ro, din, qd, kd, cd, cos, sin)

    d_att_b, dd = _att_bwd_pre(d_att, att)
    d_aqkv, dsbs = [], []
    for gi in range(3):
        da_p = _permute_rows(d_att_b, dils[gi])
        l_p = _permute_rows(lse, dils[gi])
        dd_p = _permute_rows(dd, dils[gi])
        dq, dk, dv, dsb = _att_bwd(gi, aqkv[gi], da_p, l_p, dd_p, bias, nbs[gi])
        d_aqkv.append(_unpermute_rows(jnp.concatenate([dq, dk, dv], axis=-1), dils[gi]))
        dsbs.append(dsb)
    d_rel_bias = _bias_grad(jnp.stack(dsbs), buckets)

    d_proj = jnp.concatenate([d_rq, d_rk, d_rv, d_rg] + d_aqkv + [d_gates], axis=-1)
    gw_in = _matmul_tn_pair("proj_dw", pos, h1, d_proj, D_MODEL, IN_COLS, SEQ, D_MODEL,
                            tm=512, tn=640, tk=2048)
    sems, gw_in, land, token = _ici_start("ici_start_w_in", gw_in, 1, SHARD["w_in"])
    d_h1 = _matmul("proj_dx", d_proj, w_in, "nt", SEQ, D_MODEL, IN_COLS, [F32], tn=1024, tk=1280,
                   after=[token])[0]
    pending = (sems, land)

    big = dict(w_in=gw_in, w_ret_out=gw_ret_out, w_att_out=gw_att_out, w_o=gw_o, w_ff1=gw_ff1,
               w_ff2=gw_ff2)
    g_big = {n: _final_sum("final_" + n, pos, ax, big[n], recv[n], SHARD[n], after=[token])
             for n, ax in BIG[1:]}
    grad_x, d_sh1, d_sc1, d_n1g = _prenorm_bwd("prenorm1_bwd", [d_h1], x, norm1_g, sc1, d_x2,
                                               after=list(g_big.values()))
    d_mod = jnp.concatenate([d_sh1, d_sc1, d_g1, d_sh2, d_sc2, d_g2], axis=-1)
    small = dict(norm1_g=d_n1g, norm2_g=d_n2g, norm_f_g=d_gf, gn_g=d_gn_g, gn_b=d_gn_b,
                 rel_bias=d_rel_bias)
    return loss, grad_x, d_mod, small, g_big, (gw_in,) + pending


def _me():
    return lax.axis_index("x"), lax.axis_index("y"), lax.axis_index("c")


def _peer(x, y, c, mask):
    return (x ^ ((mask >> 2) & 1), y ^ ((mask >> 1) & 1), c ^ (mask & 1))


def _gather8(src_ref, dst_ref, send_sems, recv_sems):
    x, y, c = _me()
    me = 4 * x + 2 * y + c
    copies = []
    for mask in range(1, N_DEV):
        cp = pltpu.make_async_remote_copy(
            src_ref=src_ref, dst_ref=dst_ref.at[me], send_sem=send_sems.at[mask - 1],
            recv_sem=recv_sems.at[mask - 1], device_id=_peer(x, y, c, mask), device_id_type=MESH)
        cp.start()
        copies.append(cp)
    dst_ref[me] = src_ref[...]
    for cp in copies:
        cp.wait_recv()
    for cp in copies:
        cp.wait_send()


def _ada_fwd(c_in, w_ada, b_ada):
    ncol = ADA_COLS // N_CHIPS

    def body(c_ref, w_ref, b_ref, mod_ref, sc_ref, cbuf, cg, mbuf, mg, s1, r1, s2, r2):
        x, y, c = _me()
        me = 4 * x + 2 * y + c
        cv = c_ref[...]
        cbuf[...] = jnp.broadcast_to(cv * _sigmoid(cv), cbuf.shape)
        _gather8(cbuf, cg, s1, r1)
        rows = lax.broadcasted_iota(I32, (N_DEV, D_MODEL), 0)
        sc_all = jnp.zeros((N_DEV, D_MODEL), F32)
        for d in range(N_DEV):
            sc_all = jnp.where(rows == d, cg[d], sc_all)
        sc_ref[...] = sc_all
        mbuf[...] = jnp.dot(sc_all.astype(BF16), w_ref[...].astype(BF16), preferred_element_type=F32)
        _gather8(mbuf, mg, s2, r2)
        rowsel = lax.broadcasted_iota(I32, (N_DEV, ncol), 0) == me
        for k in range(N_CHIPS):
            blk = mg[2 * k]
            row = jnp.sum(jnp.where(rowsel, blk, 0.0), axis=0, keepdims=True)
            mod_ref[:, k * ncol:(k + 1) * ncol] = row + b_ref[:, k * ncol:(k + 1) * ncol]

    vm = pl.BlockSpec(memory_space=pltpu.VMEM)
    return pl.pallas_call(
        body, name="ada_fwd",
        in_specs=[vm, vm, vm], out_specs=[vm, vm],
        out_shape=[jax.ShapeDtypeStruct((1, ADA_COLS), F32), jax.ShapeDtypeStruct((N_DEV, D_MODEL), F32)],
        scratch_shapes=[
            pltpu.VMEM((8, D_MODEL), F32), pltpu.VMEM((N_DEV, 8, D_MODEL), F32),
            pltpu.VMEM((8, ncol), F32), pltpu.VMEM((N_DEV, 8, ncol), F32),
            pltpu.SemaphoreType.DMA((N_DEV - 1,)), pltpu.SemaphoreType.DMA((N_DEV - 1,)),
            pltpu.SemaphoreType.DMA((N_DEV - 1,)), pltpu.SemaphoreType.DMA((N_DEV - 1,)),
        ],
        compiler_params=pltpu.CompilerParams(vmem_limit_bytes=VMEM_LIMIT_V7X),
    )(c_in, w_ada, b_ada)


def _small_reduce(pack, sc_all):
    ncol = ADA_COLS // N_CHIPS

    def body(p_ref, sc_ref, tot_ref, gw_ref, pg, s1, r1):
        x, y, _ = _me()
        chip = 2 * x + y
        _gather8(p_ref, pg, s1, r1)
        tot = pg[0]
        for d in range(1, N_DEV):
            tot = tot + pg[d]
        tot_ref[...] = tot
        rows = lax.broadcasted_iota(I32, (N_DEV, ncol), 0)
        dmod = jnp.zeros((N_DEV, ncol), F32)
        for k in range(N_CHIPS):
            part = jnp.zeros((N_DEV, ncol), F32)
            for d in range(N_DEV):
                part = jnp.where(rows == d, pg[d, :, k * ncol:(k + 1) * ncol][0:1, :], part)
            dmod = jnp.where(chip == k, part, dmod)
        gw_ref[...] = lax.dot_general(sc_ref[...].astype(BF16), dmod.astype(BF16), _TN,
                                      preferred_element_type=F32)

    vm = pl.BlockSpec(memory_space=pltpu.VMEM)
    return pl.pallas_call(
        body, name="small_reduce",
        in_specs=[vm, vm], out_specs=[vm, vm],
        out_shape=[jax.ShapeDtypeStruct((8, ADA_COLS), F32), jax.ShapeDtypeStruct((D_MODEL, ncol), F32)],
        scratch_shapes=[pltpu.VMEM((N_DEV, 8, ADA_COLS), F32),
                        pltpu.SemaphoreType.DMA((N_DEV - 1,)), pltpu.SemaphoreType.DMA((N_DEV - 1,))],
        compiler_params=pltpu.CompilerParams(vmem_limit_bytes=VMEM_LIMIT_V7X),
    )(pack, sc_all)


BIG = (("w_in", 1), ("w_ret_out", 0), ("w_att_out", 1), ("w_o", 0), ("w_ff1", 1), ("w_ff2", 0))
SHARD = {"w_in": (D_MODEL, IN_COLS // N_CHIPS), "w_ret_out": (RET_V_W // N_CHIPS, D_MODEL),
         "w_att_out": (ATT_W, D_MODEL // N_CHIPS), "w_o": (D_MODEL // N_CHIPS, D_MODEL),
         "w_ff1": (D_MODEL, D_FF // N_CHIPS), "w_ff2": (D_FF // N_CHIPS, D_MODEL)}
_CHIP_FLIPS = ((1, 0), (0, 1), (1, 1))


def _region(ref, axis, chip, half, shard_shape):
    r, cw = shard_shape
    hr = r // 2
    if axis == 1:
        return ref.at[pl.ds(half * hr, hr), pl.ds(chip * cw, cw)]
    return ref.at[pl.ds(chip * r + half * hr, hr), :]


def _gather_weights(shards, n_remote):
    nw = len(BIG)
    shapes = [s.shape for s in shards]
    full_shapes = [(r, N_CHIPS * cw) if ax == 1 else (N_CHIPS * r, cw)
                   for (r, cw), (_, ax) in zip(shapes, BIG)]

    def body(*refs):
        ins, outs = refs[:nw], refs[nw:2 * nw]
        own = refs[2 * nw:3 * nw]
        from_ici, from_sib = refs[3 * nw:3 * nw + n_remote], refs[3 * nw + n_remote:3 * nw + 2 * n_remote]
        ld_sem, st_sem, s_ici, r_ici, s_d2d, r_d2d, st_a, st_b = refs[3 * nw + 2 * n_remote:]
        x, y, c = _me()
        chip = 2 * x + y
        sib = (x, y, 1 - c)
        loads = [pltpu.make_async_copy(ins[i], own[i], ld_sem.at[i]) for i in range(nw)]
        for cp in loads:
            cp.start()
        pending, first = [], []
        for i, (_, ax) in enumerate(BIG):
            r, cw = shapes[i]
            hr = r // 2
            loads[i].wait()
            dst = outs[i].at[:, pl.ds(chip * cw, cw)] if ax == 1 else outs[i].at[pl.ds(chip * r, r), :]
            cp = pltpu.make_async_copy(own[i], dst, st_sem.at[i])
            cp.start()
            pending.append(cp)
            for j, (fx, fy) in enumerate(_CHIP_FLIPS if i < n_remote else ()):
                rc = pltpu.make_async_remote_copy(
                    src_ref=own[i].at[pl.ds(c * hr, hr), :], dst_ref=from_ici[i].at[j],
                    send_sem=s_ici.at[j * nw + i], recv_sem=r_ici.at[j * nw + i],
                    device_id=(x ^ fx, y ^ fy, c), device_id_type=MESH)
                rc.start()
                first.append((j, i, rc))
        passed = []
        for j, i, rc in first:
            fx, fy = _CHIP_FLIPS[j]
            src_chip = 2 * (x ^ fx) + (y ^ fy)
            ax = BIG[i][1]
            rc.wait_recv()
            fw = pltpu.make_async_remote_copy(
                src_ref=from_ici[i].at[j], dst_ref=from_sib[i].at[j], send_sem=s_d2d.at[j * nw + i],
                recv_sem=r_d2d.at[j * nw + i], device_id=sib, device_id_type=MESH)
            fw.start()
            passed.append((j, i, src_chip, fw))
            st = pltpu.make_async_copy(from_ici[i].at[j], _region(outs[i], ax, src_chip, c, shapes[i]),
                                       st_a.at[j * nw + i])
            st.start()
            pending.append(st)
        for j, i, src_chip, fw in passed:
            fw.wait_recv()
            st = pltpu.make_async_copy(from_sib[i].at[j],
                                       _region(outs[i], BIG[i][1], src_chip, 1 - c, shapes[i]),
                                       st_b.at[j * nw + i])
            st.start()
            pending.append(st)
        for _, _, rc in first:
            rc.wait_send()
        for _, _, _, fw in passed:
            fw.wait_send()
        for cp in pending:
            cp.wait()

    hbm = pl.BlockSpec(memory_space=pl.ANY)
    halves = [pltpu.VMEM((3, r // 2, cw), BF16) for r, cw in shapes[:n_remote]]
    return pl.pallas_call(
        body, name="gather_weights",
        in_specs=[hbm] * nw, out_specs=[hbm] * nw,
        out_shape=[jax.ShapeDtypeStruct(fs, BF16) for fs in full_shapes],
        scratch_shapes=[pltpu.VMEM(sh, BF16) for sh in shapes] + halves + halves
        + [pltpu.SemaphoreType.DMA((nw,)), pltpu.SemaphoreType.DMA((nw,))]
        + [pltpu.SemaphoreType.DMA((3 * nw,))] * 6,
        compiler_params=pltpu.CompilerParams(vmem_limit_bytes=VMEM_LIMIT_V7X),
    )(*shards)


REST = BIG[1:]
_SIDE_EFFECTS = pltpu.CompilerParams(has_side_effects=pltpu.SideEffectType.DATAFLOW_SIDE_EFFECTING)
_ANY_SPEC = pl.BlockSpec(memory_space=pl.ANY)


def _rest_ici_copies(shard_refs, full_refs, sems):
    x, y, c = _me()
    chip = 2 * x + y
    n = 3 * len(REST)
    copies = []
    for i, (name, ax) in enumerate(REST):
        hr = SHARD[name][0] // 2
        for j, (fx, fy) in enumerate(_CHIP_FLIPS):
            copies.append(pltpu.make_async_remote_copy(
                src_ref=shard_refs[i].at[pl.ds(c * hr, hr), :],
                dst_ref=_region(full_refs[i], ax, chip, c, SHARD[name]),
                send_sem=sems[3 * i + j], recv_sem=sems[n + 3 * i + j],
                device_id=(x ^ fx, y ^ fy, c), device_id_type=MESH))
    return copies


def _rest_d2d_copies(full_refs, sems):
    x, y, c = _me()
    n = 3 * len(REST)
    copies = []
    for i, (name, ax) in enumerate(REST):
        for j, (fx, fy) in enumerate(_CHIP_FLIPS):
            reg = _region(full_refs[i], ax, 2 * (x ^ fx) + (y ^ fy), c, SHARD[name])
            copies.append(pltpu.make_async_remote_copy(
                src_ref=reg, dst_ref=reg, send_sem=sems[3 * i + j], recv_sem=sems[n + 3 * i + j],
                device_id=(x, y, 1 - c), device_id_type=MESH))
    return copies


def _gather_rest_start(shards, fulls, after):
    nr, ns, na = len(REST), 6 * len(REST), len(after)

    def body(*refs):
        for cp in _rest_ici_copies(refs[:nr], refs[nr:2 * nr], refs[2 * nr + na:2 * nr + na + ns]):
            cp.start()
        token = refs[-1]
        token[...] = jnp.zeros_like(token)

    hbm = lambda a: pltpu.HBM(a.shape, a.dtype)
    res = pl.pallas_call(
        body, name="gather_rest_start",
        out_shape=(pltpu.SemaphoreType.DMA(()),) * ns + tuple(hbm(a) for a in shards + fulls)
        + (jax.ShapeDtypeStruct((8, 128), F32),),
        in_specs=(_HBM_SPEC,) * (2 * nr) + (_ANY_SPEC,) * na,
        out_specs=(_SEM_SPEC,) * ns + (_HBM_SPEC,) * (2 * nr) + (pl.BlockSpec(memory_space=pltpu.VMEM),),
        input_output_aliases={k: ns + k for k in range(2 * nr)}, compiler_params=_SIDE_EFFECTS,
    )(*[pltpu.with_memory_space_constraint(a, pltpu.HBM) for a in shards + fulls], *after)
    return res[:ns], res[ns:ns + nr], res[ns + nr:ns + 2 * nr], res[-1]


def _gather_rest_forward(sems, shards, fulls, after):
    nr, ns = len(REST), 6 * len(REST)

    def body(*refs):
        shard_refs, full_refs, old = refs[:nr], refs[nr:2 * nr], refs[2 * nr:2 * nr + ns]
        new = refs[2 * nr + ns + len(after):2 * nr + 2 * ns + len(after)]
        for cp in _rest_ici_copies(shard_refs, full_refs, old):
            cp.wait_send()
            cp.wait_recv()
        for cp in _rest_d2d_copies(full_refs, new):
            cp.start()
        token = refs[-1]
        token[...] = jnp.zeros_like(token)

    res = pl.pallas_call(
        body, name="gather_rest_forward",
        out_shape=(pltpu.SemaphoreType.DMA(()),) * ns + tuple(pltpu.HBM(a.shape, a.dtype) for a in fulls)
        + (jax.ShapeDtypeStruct((8, 128), F32),),
        in_specs=(_HBM_SPEC,) * (2 * nr) + (_SEM_SPEC,) * ns + (_ANY_SPEC,) * len(after),
        out_specs=(_SEM_SPEC,) * ns + (_HBM_SPEC,) * nr + (pl.BlockSpec(memory_space=pltpu.VMEM),),
        input_output_aliases={nr + k: ns + k for k in range(nr)}, compiler_params=_SIDE_EFFECTS,
    )(*shards, *fulls, *sems, *after)
    return res[:ns], res[ns:ns + nr], res[-1]


def _gather_rest_end(sems, fulls, after):
    nr, ns = len(REST), 6 * len(REST)

    def body(*refs):
        for cp in _rest_d2d_copies(refs[:nr], refs[nr:nr + ns]):
            cp.wait_send()
            cp.wait_recv()

    return pl.pallas_call(
        body, name="gather_rest_end",
        out_shape=tuple(pltpu.HBM(a.shape, a.dtype) for a in fulls),
        in_specs=(_HBM_SPEC,) * nr + (_SEM_SPEC,) * ns + (_ANY_SPEC,) * len(after),
        out_specs=(_HBM_SPEC,) * nr,
        input_output_aliases={k: k for k in range(nr)}, compiler_params=_SIDE_EFFECTS,
    )(*fulls, *sems, *after)


def _adam_update(w, g, m, v):
    mn = ADAM_B1 * m + (1.0 - ADAM_B1) * g
    vn = ADAM_B2 * v + (1.0 - ADAM_B2) * (g * g)
    m_hat = mn / (1.0 - ADAM_B1 ** ADAM_STEP)
    v_hat = vn / (1.0 - ADAM_B2 ** ADAM_STEP)
    return -ADAM_LR * (m_hat / (jnp.sqrt(v_hat) + ADAM_EPS) + ADAM_WD * w), mn, vn


def _final_sum(name, pos, axis, psum, recv, shard_shape, after=(), tr=128):
    r, cw = shard_shape
    hr = r // 2
    tr = min(tr, hr)
    nt = hr // tr
    n_after = len(after)

    def kern(pos_ref, p_ref, r_ref, *rest):
        g_ref, send_buf, land_buf, s_sem, r_sem = rest[n_after:]
        p, t = pl.program_id(0), pl.program_id(1)
        sib = _sibling()

        def copy(i):
            return pltpu.make_async_remote_copy(
                src_ref=send_buf.at[i], dst_ref=land_buf.at[i], send_sem=s_sem.at[i],
                recv_sem=r_sem.at[i], device_id=sib, device_id_type=MESH)

        @pl.when(p == 0)
        def _():
            tot = p_ref[...].astype(F32)
            for j in range(3):
                tot = tot + r_ref[j].astype(F32)
            send_buf[t] = tot
            copy(t).start()
            g_ref[...] = tot

        @pl.when(p == 1)
        def _():
            copy(t).wait_recv()
            g_ref[...] = land_buf[t]

        @pl.when(jnp.logical_and(p == 1, t == nt - 1))
        def _():
            for i in range(nt):
                copy(i).wait_send()

    def shard_rows(p, t, pos_ref):
        return (jnp.where(p == 0, pos_ref[0], 1 - pos_ref[0]) * nt + t, 0)

    def own_part(p, t, pos_ref):
        tt = jnp.where(p == 0, t, nt - 1)
        return (tt, pos_ref[1]) if axis == 1 else (pos_ref[1] * nt + tt, 0)

    grid_spec = pltpu.PrefetchScalarGridSpec(
        num_scalar_prefetch=1, grid=(2, nt),
        in_specs=[pl.BlockSpec((tr, cw), own_part),
                  pl.BlockSpec((3, tr, cw), lambda p, t, pos_ref: (0, jnp.where(p == 0, t, nt - 1), 0))]
        + [pl.BlockSpec(memory_space=pl.ANY)] * n_after,
        out_specs=pl.BlockSpec((tr, cw), shard_rows),
        scratch_shapes=[pltpu.VMEM((nt, tr, cw), F32), pltpu.VMEM((nt, tr, cw), F32),
                        pltpu.SemaphoreType.DMA((nt,)), pltpu.SemaphoreType.DMA((nt,))])
    return pl.pallas_call(
        kern, name=name, grid_spec=grid_spec, out_shape=jax.ShapeDtypeStruct((r, cw), F32),
        compiler_params=_cparams(("arbitrary", "arbitrary")),
    )(pos, psum, recv, *after)


def _adamw(name, w, g, m, v):
    r, cw = w.shape
    tr = min(r, 128)

    def kern(w_ref, g_ref, m_ref, v_ref, go_ref, d_ref, nm_ref, nv_ref):
        gv = g_ref[...]
        go_ref[...] = gv
        d_ref[...], nm_ref[...], nv_ref[...] = _adam_update(w_ref[...], gv, m_ref[...], v_ref[...])

    spec = pl.BlockSpec((tr, cw), lambda i: (i, 0))
    return pl.pallas_call(
        kern, name=name, grid=(r // tr,), in_specs=[spec] * 4, out_specs=[spec] * 4,
        out_shape=[jax.ShapeDtypeStruct((r, cw), F32)] * 4, compiler_params=_cparams(("parallel",)),
    )(w, g, m, v)


_PACK_W = ADA_COLS
_NB = REL_BUCKETS * N_ATT_HEADS
_SMALL_SLOTS = {
    "b_ada": (0, 0, ADA_COLS),
    "norm1_g": (1, 0, D_MODEL), "norm2_g": (1, D_MODEL, D_MODEL), "norm_f_g": (1, 2 * D_MODEL, D_MODEL),
    "ret_gn_g": (1, 3 * D_MODEL, RET_V_W),
    "ret_gn_b": (2, 0, RET_V_W), "rel_bias": (2, RET_V_W, _NB), "loss": (2, RET_V_W + 512, 128),
}


def _pack_small(vals):
    rows = []
    for r in range(8):
        items = sorted([(off, n) for n, (rr, off, _) in _SMALL_SLOTS.items() if rr == r and n in vals])
        parts, pos = [], 0
        for off, n in items:
            if off > pos:
                parts.append(jnp.zeros((1, off - pos), F32))
            parts.append(vals[n].reshape(1, -1).astype(F32))
            pos = off + _SMALL_SLOTS[n][2]
        if pos < _PACK_W:
            parts.append(jnp.zeros((1, _PACK_W - pos), F32))
        rows.append(jnp.concatenate(parts, axis=-1))
    return jnp.concatenate(rows, axis=0)


def _unpack_small(pack, name):
    r, off, wd = _SMALL_SLOTS[name]
    return pack[r:r + 1, off:off + wd]


def kernel(x, c, w_ada, b_ada, norm1_g, w_in, rel_bias, ret_gn_g, ret_gn_b, w_ret_out, w_att_out, w_o, norm2_g, w_ff1, w_ff2, norm_f_g, loss_target, m_w_ada, m_b_ada, m_norm1_g, m_w_in, m_rel_bias, m_ret_gn_g, m_ret_gn_b, m_w_ret_out, m_w_att_out, m_w_o, m_norm2_g, m_w_ff1, m_w_ff2, m_norm_f_g, v_w_ada, v_b_ada, v_norm1_g, v_w_in, v_rel_bias, v_ret_gn_g, v_ret_gn_b, v_w_ret_out, v_w_att_out, v_w_o, v_norm2_g, v_w_ff1, v_w_ff2, v_norm_f_g):
    given = dict(locals())
    big_names = [n for n, _ in BIG]
    shard_w = {n: given[n][0] for n in big_names}
    assert all(shard_w[n].shape == SHARD[n] for n in big_names)

    shards_bf = [shard_w[n].astype(BF16) for n in big_names]
    full = _gather_weights(shards_bf, 1)
    mod, sc_all = _ada_fwd(c, w_ada[0], b_ada)
    rest_gather = _gather_rest_start(shards_bf[1:], list(full[1:]), [mod])
    pos = _where_am_i()

    loss, grad_x, d_mod, small, g_big, pending = _local_step(
        pos, x[0], loss_target[0], mod, norm1_g, norm2_g, norm_f_g.reshape(1, -1), rel_bias, ret_gn_g,
        ret_gn_b, full[0], rest_gather)

    pack_g = _pack_small(dict(b_ada=d_mod, norm1_g=small["norm1_g"], norm2_g=small["norm2_g"],
                              norm_f_g=small["norm_f_g"], ret_gn_g=small["gn_g"], ret_gn_b=small["gn_b"],
                              rel_bias=small["rel_bias"], loss=loss))
    tot, g_w_ada = _small_reduce(pack_g, sc_all)

    small_names = ["b_ada", "norm1_g", "rel_bias", "ret_gn_g", "ret_gn_b", "norm2_g", "norm_f_g"]
    pack_w = _pack_small({n: given[n] for n in small_names})
    pack_m = _pack_small({n: given["m_" + n] for n in small_names})
    pack_v = _pack_small({n: given["v_" + n] for n in small_names})
    _, sd, sm, sv = _adamw("adamw_small", pack_w, tot, pack_m, pack_v)

    grads, deltas, new_m, new_v = {}, {}, {}, {}
    for n in small_names:
        shp = given[n].shape
        grads[n] = _unpack_small(tot, n).reshape(shp)
        deltas[n] = _unpack_small(sd, n).reshape(shp)
        new_m[n] = _unpack_small(sm, n).reshape(shp)
        new_v[n] = _unpack_small(sv, n).reshape(shp)
    g_big["w_ada"] = g_w_ada
    for n in ["w_ada"] + big_names[1:] + big_names[:1]:
        if n == "w_in":
            gw_in, sems, land = pending
            done = [tot, sd] + [deltas[k] for k in ["w_ada"] + big_names[1:]]
            gw_in, got = _ici_wait("ici_wait_w_in", sems, gw_in, land, 1, SHARD[n], done)
            g_big[n] = _final_sum("final_w_in", pos, 1, gw_in, got, SHARD[n])
        g, d, nm, nv = _adamw("adamw_" + n, given[n][0], g_big[n], given["m_" + n][0], given["v_" + n][0])
        grads[n], deltas[n], new_m[n], new_v[n] = g[None], d[None], nm[None], nv[None]

    order = ["w_ada", "b_ada", "norm1_g", "w_in", "rel_bias", "ret_gn_g", "ret_gn_b", "w_ret_out",
             "w_att_out", "w_o", "norm2_g", "w_ff1", "w_ff2", "norm_f_g"]
    loss_out = _unpack_small(tot, "loss")[0, 0]
    return (loss_out, grad_x[None], *[grads[n] for n in order], *[deltas[n] for n in order],
            *[new_m[n] for n in order], *[new_v[n] for n in order])
```

```python
import functools
import math

import jax
import jax.numpy as jnp
import numpy as np
from jax import lax
from jax.experimental import pallas as pl
from jax.experimental.pallas import tpu as pltpu

F32 = jnp.float32
BF16 = jnp.bfloat16
I32 = jnp.int32

SEQ = 2048
D_MODEL = 1024
RET_HEADS = 4
RET_DK = 256
RET_DV = 512
RET_CHUNK = 128
RET_QK_W = RET_HEADS * RET_DK
RET_V_W = RET_HEADS * RET_DV
ATT_GROUPS = ((128, 1), (512, 4), (2048, 16))
ATT_HPG = 4
ATT_DH = 128
ATT_W = ATT_HPG * ATT_DH
ATT_BLK = 128
N_BLK = SEQ // ATT_BLK
REL_BUCKETS = 32
REL_MAX_DIST = 2048
N_ATT_HEADS = 12
D_FF = 4 * D_MODEL
RMS_EPS = 1e-6
GN_EPS = 1e-5
ROPE_BASE = 10000.0
IN_COLS = 2 * RET_QK_W + 2 * RET_V_W + 9 * ATT_W + 2 * D_MODEL
OFF_Q, OFF_K, OFF_V, OFF_G = 0, RET_QK_W, 2 * RET_QK_W, 2 * RET_QK_W + RET_V_W
OFF_ATT = 2 * RET_QK_W + 2 * RET_V_W
OFF_GATE = OFF_ATT + 9 * ATT_W
N_CHIPS = 4
N_DEV = 8
ADA_COLS = 6 * D_MODEL

ADAM_LR = 0.001
ADAM_B1 = 0.9
ADAM_B2 = 0.999
ADAM_EPS = 1e-08
ADAM_WD = 0.01
ADAM_STEP = 10

VMEM_LIMIT_V7X = 56 * 1024 * 1024
MESH = pl.DeviceIdType.MESH


def _cparams(sem):
    return pltpu.CompilerParams(dimension_semantics=sem, vmem_limit_bytes=VMEM_LIMIT_V7X)


def _sigmoid(v):
    return 1.0 / (1.0 + jnp.exp(-v))


def _rowmap(name, body, row_ins, bcast_ins, row_outs, sum_outs=(), tm=256, after=()):
    m = row_ins[0].shape[0]
    n_in = len(row_ins) + len(bcast_ins)
    n_ro = len(row_outs)

    def kern(*refs):
        vals = [r[...] for r in refs[:n_in]]
        res = body(*vals)
        if not isinstance(res, (tuple, list)):
            res = (res,)
        outs = refs[n_in + len(after):]
        for r, v in zip(outs[:n_ro], res[:n_ro]):
            r[...] = v.astype(r.dtype)
        if sum_outs:
            @pl.when(pl.program_id(0) == 0)
            def _():
                for r in outs[n_ro:]:
                    r[...] = jnp.zeros_like(r)
            for r, v in zip(outs[n_ro:], res[n_ro:]):
                r[...] += v

    in_specs = [pl.BlockSpec((tm, a.shape[1]), lambda i: (i, 0)) for a in row_ins]
    in_specs += [pl.BlockSpec(a.shape, lambda i: (0, 0)) for a in bcast_ins]
    in_specs += [pl.BlockSpec(memory_space=pl.ANY)] * len(after)
    out_specs = [pl.BlockSpec((tm, n), lambda i: (i, 0)) for n, _ in row_outs]
    out_specs += [pl.BlockSpec((1, n), lambda i: (0, 0)) for n in sum_outs]
    out_shape = [jax.ShapeDtypeStruct((m, n), dt) for n, dt in row_outs]
    out_shape += [jax.ShapeDtypeStruct((1, n), F32) for n in sum_outs]
    return pl.pallas_call(
        kern, name=name, grid=(m // tm,), in_specs=in_specs, out_specs=out_specs,
        out_shape=out_shape, compiler_params=_cparams(("arbitrary",)),
    )(*row_ins, *bcast_ins, *after)


TM, TN = 1024, 1024


def _matmul(name, a, b, kind, m, n, k, outs, *, b_off=0, tm=TM, tn=TN, tk=1024,
            epilogue=None, extras=(), carry=None, after=()):
    tm, tn, tk = min(tm, m), min(tn, n), min(tk, k)
    nk = k // tk
    if kind == "nn":
        a_spec = pl.BlockSpec((tm, tk), lambda i, j, kk: (i, kk))
        b_spec = pl.BlockSpec((tk, tn), lambda i, j, kk: (kk, b_off // tn + j))
        dn = (((1,), (0,)), ((), ()))
    elif kind == "nt":
        a_spec = pl.BlockSpec((tm, tk), lambda i, j, kk: (i, kk))
        b_spec = pl.BlockSpec((tn, tk), lambda i, j, kk: (j, b_off // tk + kk))
        dn = (((1,), (1,)), ((), ()))
    else:
        a_spec = pl.BlockSpec((tk, tm), lambda i, j, kk: (kk, i))
        b_spec = pl.BlockSpec((tk, tn), lambda i, j, kk: (kk, j))
        dn = (((0,), (0,)), ((), ()))
    n_ex, n_out = len(extras), len(outs)
    if epilogue is None:
        epilogue = lambda acc: (acc,)

    def finish(acc, ex_refs, out_refs):
        res = epilogue(acc, *[r[...] for r in ex_refs])
        for r, v in zip(out_refs, res):
            r[...] = v.astype(r.dtype)

    n_c = 0 if carry is None else 1
    n_in = n_ex + n_c + len(after)
    ni, nj = m // tm, n // tn

    def kern(a_ref, b_ref, *rest):
        ex_refs = rest[:n_ex]
        out_refs = rest[n_in:n_in + n_out]
        scratch = rest[n_in + n_out + n_c:]
        i, j, kk = pl.program_id(0), pl.program_id(1), pl.program_id(2)
        if carry is not None:
            copies = lambda: _ici_copies(rest[n_ex], rest[n_in + n_out], scratch[-2], scratch[-1],
                                         carry[1], carry[2])

            @pl.when(jnp.logical_and(jnp.logical_and(i == 0, j == 0), kk == 0))
            def _():
                for cp in copies():
                    cp.start()

        part = lax.dot_general(a_ref[...], b_ref[...], dn, preferred_element_type=F32)
        if nk == 1:
            finish(part, ex_refs, out_refs)
        else:
            acc_ref = scratch[0]

            @pl.when(kk == 0)
            def _():
                acc_ref[...] = part

            @pl.when(kk > 0)
            def _():
                acc_ref[...] += part

            @pl.when(kk == nk - 1)
            def _():
                finish(acc_ref[...], ex_refs, out_refs)

        if carry is not None:
            @pl.when(jnp.logical_and(jnp.logical_and(i == ni - 1, j == nj - 1), kk == nk - 1))
            def _():
                for cp in copies():
                    cp.wait_recv()
                for cp in copies():
                    cp.wait_send()

    hbm = pl.BlockSpec(memory_space=pl.ANY)
    in_specs = [a_spec, b_spec] + [pl.BlockSpec(bs, im) for _, bs, im in extras]
    in_specs += [hbm] * (n_c + len(after))
    out_specs = [pl.BlockSpec((tm, tn), lambda i, j, kk: (i, j)) for _ in outs] + [hbm] * n_c
    out_shape = [jax.ShapeDtypeStruct((m, n), dt) for dt in outs]
    scratch_shapes = [] if nk == 1 else [pltpu.VMEM((tm, tn), F32)]
    operands = [a, b] + [e[0] for e in extras]
    if carry is not None:
        r, cw = carry[2]
        out_shape.append(jax.ShapeDtypeStruct((3, r // 2, cw), BF16))
        scratch_shapes += [pltpu.SemaphoreType.DMA((3,)), pltpu.SemaphoreType.DMA((3,))]
        operands.append(carry[0])
    operands += list(after)
    sem = ("arbitrary",) * 3 if carry is not None else ("parallel", "parallel", "arbitrary")
    return pl.pallas_call(
        kern, name=name, grid=(ni, nj, nk), in_specs=in_specs, out_specs=out_specs,
        out_shape=out_shape, scratch_shapes=scratch_shapes, compiler_params=_cparams(sem),
    )(*operands)


def _ici_copies(psum_ref, recv_ref, s_sem, r_sem, axis, shard_shape):
    x, y, c = _me()
    hr, cw = shard_shape[0] // 2, shard_shape[1]
    pick = lambda sems, j: sems[j] if isinstance(sems, (list, tuple)) else sems.at[j]
    copies = []
    for j, (fx, fy) in enumerate(_CHIP_FLIPS):
        chip = 2 * (x ^ fx) + (y ^ fy)
        src = psum_ref.at[:, pl.ds(chip * cw, cw)] if axis == 1 else psum_ref.at[pl.ds(chip * hr, hr), :]
        copies.append(pltpu.make_async_remote_copy(
            src_ref=src, dst_ref=recv_ref.at[j], send_sem=pick(s_sem, j), recv_sem=pick(r_sem, j),
            device_id=(x ^ fx, y ^ fy, c), device_id_type=MESH))
    return copies


_HBM_SPEC = pl.BlockSpec(memory_space=pltpu.HBM)
_SEM_SPEC = pl.BlockSpec(memory_space=pltpu.SEMAPHORE)


def _split_ici_copies(names, p_refs, land_refs, sems):
    copies = []
    for i, n in enumerate(names):
        copies += _ici_copies(p_refs[i], land_refs[i], list(sems[6 * i:6 * i + 3]),
                              list(sems[6 * i + 3:6 * i + 6]), dict(BIG)[n], SHARD[n])
    return copies


def _ici_start(name, names, psums):
    nw, ns = len(names), 6 * len(names)
    lands = [lax.empty((3, SHARD[n][0] // 2, SHARD[n][1]), BF16) for n in names]

    def body(*refs):
        for cp in _split_ici_copies(names, refs[:nw], refs[nw:2 * nw], refs[2 * nw:2 * nw + ns]):
            cp.start()
        token = refs[-1]
        token[...] = jnp.zeros_like(token)

    res = pl.pallas_call(
        body, name=name,
        out_shape=(pltpu.SemaphoreType.DMA(()),) * ns
        + tuple(pltpu.HBM(a.shape, BF16) for a in list(psums) + lands)
        + (jax.ShapeDtypeStruct((8, 128), F32),),
        in_specs=(_HBM_SPEC,) * (2 * nw),
        out_specs=(_SEM_SPEC,) * ns + (_HBM_SPEC,) * (2 * nw) + (pl.BlockSpec(memory_space=pltpu.VMEM),),
        input_output_aliases={k: ns + k for k in range(2 * nw)},
        compiler_params=pltpu.CompilerParams(has_side_effects=pltpu.SideEffectType.DATAFLOW_SIDE_EFFECTING),
    )(*[pltpu.with_memory_space_constraint(a, pltpu.HBM) for a in list(psums) + lands])
    return res[:ns], res[ns:ns + nw], res[ns + nw:ns + 2 * nw], res[-1]


def _ici_wait(name, names, sems, p_thru, land_thru, after):
    nw, ns = len(names), 6 * len(names)

    def body(*refs):
        for cp in _split_ici_copies(names, refs[:nw], refs[nw:2 * nw], refs[2 * nw:2 * nw + ns]):
            cp.wait_send()
            cp.wait_recv()

    res = pl.pallas_call(
        body, name=name,
        out_shape=tuple(pltpu.HBM(a.shape, BF16) for a in list(p_thru) + list(land_thru)),
        in_specs=(_HBM_SPEC,) * (2 * nw) + (_SEM_SPEC,) * ns + (pl.BlockSpec(memory_space=pl.ANY),) * len(after),
        out_specs=(_HBM_SPEC,) * (2 * nw), input_output_aliases={k: k for k in range(2 * nw)},
        compiler_params=pltpu.CompilerParams(has_side_effects=pltpu.SideEffectType.DATAFLOW_SIDE_EFFECTING),
    )(*p_thru, *land_thru, *sems, *after)
    return res[:nw], res[nw:]


def _where_am_i():
    x, y, c = _me()
    return jnp.stack([c, 2 * x + y]).astype(I32)


def _sibling():
    x, y, c = _me()
    return (x, y, 1 - c)


def _matmul_tn_pair(name, pos, a, b, m, n, k, shard_rows, *, tm, tn, tk):
    hr = shard_rows // 2
    tm, tn, tk = min(tm, hr), min(tn, n), min(tk, k)
    tph = hr // tm
    nt, nj, nk = (m // 2) // tm, n // tn, k // tk
    n_tiles = nt * nj

    def row_block(p, t, pos_ref):
        half = jnp.where(p == 0, 1 - pos_ref[0], pos_ref[0])
        return (t // tph) * (2 * tph) + half * tph + t % tph

    def kern(pos_ref, a_ref, b_ref, o_ref, acc_ref, send_buf, land_buf, s_sem, r_sem):
        p, t, j, kk = pl.program_id(0), pl.program_id(1), pl.program_id(2), pl.program_id(3)
        idx = t * nj + j
        sib = _sibling()

        def copy(i):
            return pltpu.make_async_remote_copy(
                src_ref=send_buf.at[i], dst_ref=land_buf.at[i], send_sem=s_sem.at[i],
                recv_sem=r_sem.at[i], device_id=sib, device_id_type=MESH)

        part = lax.dot_general(a_ref[...], b_ref[...], _TN, preferred_element_type=F32)

        @pl.when(kk == 0)
        def _():
            acc_ref[...] = part

        @pl.when(kk > 0)
        def _():
            acc_ref[...] += part

        @pl.when(jnp.logical_and(kk == nk - 1, p == 0))
        def _():
            send_buf[idx] = acc_ref[...].astype(BF16)
            copy(idx).start()

        @pl.when(jnp.logical_and(kk == nk - 1, p == 1))
        def _():
            copy(idx).wait_recv()
            o_ref[...] = (acc_ref[...] + land_buf[idx].astype(F32)).astype(BF16)

        @pl.when(jnp.logical_and(jnp.logical_and(p == 1, idx == n_tiles - 1), kk == nk - 1))
        def _():
            for i in range(n_tiles):
                copy(i).wait_send()

    grid_spec = pltpu.PrefetchScalarGridSpec(
        num_scalar_prefetch=1, grid=(2, nt, nj, nk),
        in_specs=[pl.BlockSpec((tk, tm), lambda p, t, j, kk, pos_ref: (kk, row_block(p, t, pos_ref))),
                  pl.BlockSpec((tk, tn), lambda p, t, j, kk, pos_ref: (kk, j))],
        out_specs=pl.BlockSpec((tm, tn), lambda p, t, j, kk, pos_ref: (p * t, p * j)),
        scratch_shapes=[pltpu.VMEM((tm, tn), F32), pltpu.VMEM((n_tiles, tm, tn), BF16),
                        pltpu.VMEM((n_tiles, tm, tn), BF16),
                        pltpu.SemaphoreType.DMA((n_tiles,)), pltpu.SemaphoreType.DMA((n_tiles,))])
    return pl.pallas_call(
        kern, name=name, grid_spec=grid_spec, out_shape=jax.ShapeDtypeStruct((m // 2, n), BF16),
        compiler_params=_cparams(("arbitrary",) * 4),
    )(pos, a, b)


def _rope_tables():
    half = RET_DK // 2
    f32 = np.float32
    inv = np.power(f32(ROPE_BASE), -np.arange(half, dtype=f32) / f32(half)).astype(f32)
    ang = (np.arange(SEQ, dtype=f32)[:, None] * inv[None, :]).astype(f32)
    return jnp.asarray(np.cos(ang).astype(f32)), jnp.asarray(np.sin(ang).astype(f32))


def _decay_tables():
    c = RET_CHUNK
    f32 = np.float32
    log_g = np.log1p(-np.power(f32(2.0), f32(-5.0) - np.arange(RET_HEADS, dtype=f32))).astype(f32)
    idx = np.arange(c, dtype=f32)
    rel = idx[:, None] - idx[None, :]
    din = np.where(rel >= 0, np.exp(log_g[:, None, None] * np.maximum(rel, f32(0.0))), f32(0.0)).astype(f32)
    qd = np.exp(log_g[:, None] * (idx + f32(1.0))).astype(f32)[:, :, None]
    kd = np.exp(log_g[:, None] * (f32(c) - f32(1.0) - idx)).astype(f32)[:, :, None]
    cd = np.exp(log_g * f32(c)).astype(f32)
    return jnp.asarray(din), jnp.asarray(qd), jnp.asarray(kd), jnp.asarray(cd)


def _t5_bucket(dist):
    max_exact = REL_BUCKETS // 2
    d_f = jnp.maximum(dist, 1).astype(F32)
    large = max_exact + (jnp.log(d_f / max_exact) / math.log(REL_MAX_DIST / max_exact)
                         * (REL_BUCKETS - max_exact)).astype(I32)
    large = jnp.minimum(large, REL_BUCKETS - 1)
    return jnp.where(dist < max_exact, dist, large)


def _bucket_tables():
    qi = jnp.arange(ATT_BLK)[:, None]
    kj = jnp.arange(2 * ATT_BLK)[None, :]
    dist = jnp.clip(ATT_BLK + qi - kj, 0, ATT_BLK)
    return jnp.stack([_t5_bucket(dist * dil) for _, dil in ATT_GROUPS]).astype(I32)


def _permute_rows(t, dil):
    if dil == 1:
        return t
    s, w = t.shape
    return t.reshape(s // dil, dil, w).transpose(1, 0, 2).reshape(s, w)


def _unpermute_rows(t, dil):
    if dil == 1:
        return t
    s, w = t.shape
    return t.reshape(dil, s // dil, w).transpose(1, 0, 2).reshape(s, w)


def _retention_fwd(rqk, rv, din, qd, kd, cd):
    nc = SEQ // RET_CHUNK
    c, dk, dv = RET_CHUNK, RET_DK, RET_DV

    def kern(q_ref, k_ref, v_ref, din_ref, qd_ref, kd_ref, cd_ref, o_ref, st_ref, state):
        n = pl.program_id(0)

        @pl.when(n == 0)
        def _():
            state[...] = jnp.zeros_like(state)

        for h in range(RET_HEADS):
            q, k = q_ref[:, h * dk:(h + 1) * dk], k_ref[:, h * dk:(h + 1) * dk]
            v = v_ref[:, h * dv:(h + 1) * dv]
            s_b = state[h].astype(BF16)
            st_ref[h] = s_b
            a = lax.dot_general(q, k, _NT, preferred_element_type=F32) * din_ref[h]
            o = jnp.dot(a.astype(BF16), v, preferred_element_type=F32)
            o += jnp.dot(q, s_b, preferred_element_type=F32) * qd_ref[h]
            o_ref[:, h * dv:(h + 1) * dv] = o
            kk = (k.astype(F32) * kd_ref[h]).astype(BF16)
            state[h] = state[h] * cd_ref[h] + lax.dot_general(kk, v, _TN, preferred_element_type=F32)

    whole = lambda a: pl.BlockSpec(a.shape, lambda n: (0,) * a.ndim)
    return pl.pallas_call(
        kern, name="retention_fwd", grid=(nc,),
        in_specs=[
            pl.BlockSpec((c, RET_QK_W), lambda n: (n, 0)),
            pl.BlockSpec((c, RET_QK_W), lambda n: (n, 1)),
            pl.BlockSpec((c, RET_V_W), lambda n: (n, 0)),
            whole(din), whole(qd), whole(kd),
            pl.BlockSpec(memory_space=pltpu.SMEM),
        ],
        out_specs=[
            pl.BlockSpec((c, RET_V_W), lambda n: (n, 0)),
            pl.BlockSpec((RET_HEADS, None, dk, dv), lambda n: (0, n, 0, 0)),
        ],
        out_shape=[
            jax.ShapeDtypeStruct((SEQ, RET_V_W), F32),
            jax.ShapeDtypeStruct((RET_HEADS, nc, dk, dv), BF16),
        ],
        scratch_shapes=[pltpu.VMEM((RET_HEADS, dk, dv), F32)],
        compiler_params=_cparams(("arbitrary",)),
    )(rqk, rqk, rv, din, qd, kd, cd)


def _retention_bwd(rqk, rv, states, d_ro, din, qd, kd, cd, cos, sin):
    nc = SEQ // RET_CHUNK
    c, dk, dv = RET_CHUNK, RET_DK, RET_DV
    half = dk // 2
    last = nc - 1

    def unrot(g, cs, sn):
        g1, g2 = g[:, :half], g[:, half:]
        return jnp.concatenate([g1 * cs + g2 * sn, g2 * cs - g1 * sn], axis=-1)

    def kern(q_ref, k_ref, v_ref, st_ref, do_ref, din_ref, qd_ref, kd_ref, cd_ref, cos_ref, sin_ref,
             dq_ref, dk_ref, dv_ref, dstate):
        step = pl.program_id(0)

        @pl.when(step == 0)
        def _():
            dstate[...] = jnp.zeros_like(dstate)

        cs, sn = cos_ref[...], sin_ref[...]
        for h in range(RET_HEADS):
            qk_cols, v_cols = slice(h * dk, (h + 1) * dk), slice(h * dv, (h + 1) * dv)
            q, k, v, s_b = q_ref[:, qk_cols], k_ref[:, qk_cols], v_ref[:, v_cols], st_ref[h]
            d_o = do_ref[:, v_cols]
            d_ob = d_o.astype(BF16)
            d_oq = (d_o * qd_ref[h]).astype(BF16)
            ds_b = dstate[h].astype(BF16)
            din_m = din_ref[h]
            a_b = (lax.dot_general(q, k, _NT, preferred_element_type=F32) * din_m).astype(BF16)
            kk = (k.astype(F32) * kd_ref[h]).astype(BF16)
            d_v = lax.dot_general(a_b, d_ob, _TN, preferred_element_type=F32)
            d_v += jnp.dot(kk, ds_b, preferred_element_type=F32)
            d_a = (lax.dot_general(d_ob, v, _NT, preferred_element_type=F32) * din_m).astype(BF16)
            d_q = jnp.dot(d_a, k, preferred_element_type=F32)
            d_q += lax.dot_general(d_oq, s_b, _NT, preferred_element_type=F32)
            d_k = lax.dot_general(d_a, q, _TN, preferred_element_type=F32)
            d_k += lax.dot_general(v, ds_b, _NT, preferred_element_type=F32) * kd_ref[h]
            dstate[h] = dstate[h] * cd_ref[h] + lax.dot_general(q, d_oq, _TN, preferred_element_type=F32)
            dq_ref[:, qk_cols] = unrot(d_q, cs, sn).astype(BF16)
            dk_ref[:, qk_cols] = (unrot(d_k, cs, sn) * (RET_DK ** -0.5)).astype(BF16)
            dv_ref[:, v_cols] = d_v.astype(BF16)

    whole = lambda a: pl.BlockSpec(a.shape, lambda n: (0,) * a.ndim)
    return pl.pallas_call(
        kern, name="retention_bwd", grid=(nc,),
        in_specs=[
            pl.BlockSpec((c, RET_QK_W), lambda n: (last - n, 0)),
            pl.BlockSpec((c, RET_QK_W), lambda n: (last - n, 1)),
            pl.BlockSpec((c, RET_V_W), lambda n: (last - n, 0)),
            pl.BlockSpec((RET_HEADS, None, dk, dv), lambda n: (0, last - n, 0, 0)),
            pl.BlockSpec((c, RET_V_W), lambda n: (last - n, 0)),
            whole(din), whole(qd), whole(kd),
            pl.BlockSpec(memory_space=pltpu.SMEM),
            pl.BlockSpec((c, half), lambda n: (last - n, 0)),
            pl.BlockSpec((c, half), lambda n: (last - n, 0)),
        ],
        out_specs=[
            pl.BlockSpec((c, RET_QK_W), lambda n: (last - n, 0)),
            pl.BlockSpec((c, RET_QK_W), lambda n: (last - n, 0)),
            pl.BlockSpec((c, RET_V_W), lambda n: (last - n, 0)),
        ],
        out_shape=[
            jax.ShapeDtypeStruct((SEQ, RET_QK_W), BF16),
            jax.ShapeDtypeStruct((SEQ, RET_QK_W), BF16),
            jax.ShapeDtypeStruct((SEQ, RET_V_W), BF16),
        ],
        scratch_shapes=[pltpu.VMEM((RET_HEADS, dk, dv), F32)],
        compiler_params=_cparams(("arbitrary",)),
    )(rqk, rqk, rv, states, d_ro, din, qd, kd, cd, cos, sin)


def _bias_build(rel_bias, buckets):
    ng = len(ATT_GROUPS)

    def kern(tab_ref, bkt_ref, o_ref):
        g, h = pl.program_id(0), pl.program_id(1)
        bkt = bkt_ref[...]
        acc = jnp.zeros(bkt.shape, F32)
        for b in range(REL_BUCKETS):
            acc = jnp.where(bkt == b, tab_ref[b, g * ATT_HPG + h], acc)
        o_ref[...] = acc

    return pl.pallas_call(
        kern, name="bias_build", grid=(ng, ATT_HPG),
        in_specs=[pl.BlockSpec(memory_space=pltpu.SMEM),
                  pl.BlockSpec((None, ATT_BLK, 2 * ATT_BLK), lambda g, h: (g, 0, 0))],
        out_specs=pl.BlockSpec((None, None, ATT_BLK, 2 * ATT_BLK), lambda g, h: (g, h, 0, 0)),
        out_shape=jax.ShapeDtypeStruct((ng, ATT_HPG, ATT_BLK, 2 * ATT_BLK), F32),
        compiler_params=_cparams(("arbitrary", "arbitrary")),
    )(rel_bias, buckets)


def _bias_grad(dsb, buckets):
    ng = len(ATT_GROUPS)

    def kern(ds_ref, bkt_ref, o_ref):
        g, h = pl.program_id(0), pl.program_id(1)
        bkt, ds = bkt_ref[...], ds_ref[...]
        for b in range(REL_BUCKETS):
            o_ref[b, g * ATT_HPG + h] = jnp.sum(jnp.where(bkt == b, ds, 0.0))

    return pl.pallas_call(
        kern, name="bias_grad", grid=(ng, ATT_HPG),
        in_specs=[pl.BlockSpec((None, None, ATT_BLK, 2 * ATT_BLK), lambda g, h: (g, h, 0, 0)),
                  pl.BlockSpec((None, ATT_BLK, 2 * ATT_BLK), lambda g, h: (g, 0, 0))],
        out_specs=pl.BlockSpec(memory_space=pltpu.SMEM),
        out_shape=jax.ShapeDtypeStruct((REL_BUCKETS, N_ATT_HEADS), F32),
        compiler_params=_cparams(("arbitrary", "arbitrary")),
    )(dsb, buckets)


_NT = (((1,), (1,)), ((), ()))
_TN = (((0,), (0,)), ((), ()))
_ATT_SCALE = ATT_DH ** -0.5


_PAD_ROWS = SEQ + ATT_BLK


def _window_mask(has_prev):
    qi = lax.broadcasted_iota(I32, (ATT_BLK, 2 * ATT_BLK), 0)
    kj = lax.broadcasted_iota(I32, (ATT_BLK, 2 * ATT_BLK), 1)
    prev_ok = jnp.logical_and(jnp.logical_and(kj < ATT_BLK, kj >= qi), has_prev)
    return jnp.logical_or(prev_ok, jnp.logical_and(kj >= ATT_BLK, qi >= kj - ATT_BLK))


def _head_specs(col0):
    return pl.BlockSpec((SEQ, ATT_DH), lambda h: (0, col0 + h))


def _att_fwd(gi, qkv, bias, nb):
    blk, dh = ATT_BLK, ATT_DH

    def kern(q_ref, k_ref, v_ref, b_ref, o_ref, l_ref, kpad, vpad):
        zero = jnp.zeros((blk, dh), BF16)
        kpad[0:blk, :] = zero
        vpad[0:blk, :] = zero
        kpad[blk:, :] = k_ref[...]
        vpad[blk:, :] = v_ref[...]
        bias_m = b_ref[...]

        def body(b, carry):
            r0 = pl.multiple_of(b * blk, blk)
            q = q_ref[pl.ds(r0, blk), :]
            kw = kpad[pl.ds(r0, 2 * blk), :]
            vw = vpad[pl.ds(r0, 2 * blk), :]
            valid = _window_mask((b % nb) > 0)
            s = lax.dot_general(q, kw, _NT, preferred_element_type=F32) * _ATT_SCALE + bias_m
            s = jnp.where(valid, s, -1e30)
            mx = jnp.max(s, axis=-1, keepdims=True)
            e = jnp.exp(s - mx)
            den = jnp.sum(e, axis=-1, keepdims=True)
            o_ref[pl.ds(r0, blk), :] = jnp.dot((e / den).astype(BF16), vw, preferred_element_type=F32)
            l_ref[pl.ds(r0, blk), :] = jnp.broadcast_to(mx + jnp.log(den), (blk, dh))
            return carry

        lax.fori_loop(0, N_BLK, body, 0, unroll=2)

    return pl.pallas_call(
        kern, name=f"att_fwd_g{gi}", grid=(ATT_HPG,),
        in_specs=[_head_specs(0), _head_specs(ATT_HPG), _head_specs(2 * ATT_HPG),
                  pl.BlockSpec((None, None, blk, 2 * blk), lambda h: (gi, h, 0, 0))],
        out_specs=[_head_specs(0), _head_specs(0)],
        out_shape=[jax.ShapeDtypeStruct((SEQ, ATT_W), F32), jax.ShapeDtypeStruct((SEQ, ATT_W), F32)],
        scratch_shapes=[pltpu.VMEM((_PAD_ROWS, dh), BF16), pltpu.VMEM((_PAD_ROWS, dh), BF16)],
        compiler_params=_cparams(("arbitrary",)),
    )(qkv, qkv, qkv, bias)


def _att_bwd(gi, qkv, d_att, lse, dd, bias, nb):
    blk, dh = ATT_BLK, ATT_DH

    def kern(q_ref, k_ref, v_ref, do_ref, l_ref, d_ref, b_ref, dq_ref, dk_ref, dv_ref, dsb_ref,
             kpad, vpad, qpad, dopad, lpad, dpad):
        zero = jnp.zeros((blk, dh), BF16)
        zero_f = jnp.zeros((blk, dh), F32)
        kpad[0:blk, :] = zero
        vpad[0:blk, :] = zero
        kpad[blk:, :] = k_ref[...]
        vpad[blk:, :] = v_ref[...]
        qpad[SEQ:, :] = zero
        dopad[SEQ:, :] = zero
        lpad[SEQ:, :] = zero_f
        dpad[SEQ:, :] = zero_f
        qpad[0:SEQ, :] = q_ref[...]
        dopad[0:SEQ, :] = do_ref[...]
        lpad[0:SEQ, :] = l_ref[...]
        dpad[0:SEQ, :] = d_ref[...]
        bias_m = b_ref[...]
        bias_t = jnp.concatenate([bias_m[:, blk:], bias_m[:, :blk]], axis=0)
        dsb_ref[...] = jnp.zeros_like(dsb_ref)

        def dq_body(b, carry):
            r0 = pl.multiple_of(b * blk, blk)
            q, d_o = q_ref[pl.ds(r0, blk), :], do_ref[pl.ds(r0, blk), :]
            kw, vw = kpad[pl.ds(r0, 2 * blk), :], vpad[pl.ds(r0, 2 * blk), :]
            lrow, drow = l_ref[pl.ds(r0, blk), :][:, :1], d_ref[pl.ds(r0, blk), :][:, :1]
            valid = _window_mask((b % nb) > 0)
            s = lax.dot_general(q, kw, _NT, preferred_element_type=F32) * _ATT_SCALE + bias_m
            p = jnp.where(valid, jnp.exp(jnp.where(valid, s, -1e30) - lrow), 0.0)
            dp = lax.dot_general(d_o, vw, _NT, preferred_element_type=F32)
            ds = p * (dp - drow)
            dq = jnp.dot(ds.astype(BF16), kw, preferred_element_type=F32)
            dq_ref[pl.ds(r0, blk), :] = (dq * _ATT_SCALE).astype(BF16)
            dsb_ref[...] += ds
            return carry

        lax.fori_loop(0, N_BLK, dq_body, 0, unroll=2)

        qi = lax.broadcasted_iota(I32, (2 * blk, blk), 0)
        kj = lax.broadcasted_iota(I32, (2 * blk, blk), 1)

        def dkv_body(b, carry):
            r0 = pl.multiple_of(b * blk, blk)
            k, v = k_ref[pl.ds(r0, blk), :], v_ref[pl.ds(r0, blk), :]
            qw, dow = qpad[pl.ds(r0, 2 * blk), :], dopad[pl.ds(r0, 2 * blk), :]
            lrow, drow = lpad[pl.ds(r0, 2 * blk), :][:, :1], dpad[pl.ds(r0, 2 * blk), :][:, :1]
            has_next = jnp.logical_and(b + 1 < N_BLK, ((b + 1) % nb) > 0)
            next_ok = jnp.logical_and(jnp.logical_and(qi >= blk, kj >= qi - blk), has_next)
            valid = jnp.logical_or(jnp.logical_and(qi < blk, qi >= kj), next_ok)
            s = lax.dot_general(qw, k, _NT, preferred_element_type=F32) * _ATT_SCALE + bias_t
            p = jnp.where(valid, jnp.exp(jnp.where(valid, s, -1e30) - lrow), 0.0)
            dp = lax.dot_general(dow, v, _NT, preferred_element_type=F32)
            ds = p * (dp - drow)
            d_v = lax.dot_general(p.astype(BF16), dow, _TN, preferred_element_type=F32)
            d_k = lax.dot_general(ds.astype(BF16), qw, _TN, preferred_element_type=F32)
            dk_ref[pl.ds(r0, blk), :] = (d_k * _ATT_SCALE).astype(BF16)
            dv_ref[pl.ds(r0, blk), :] = d_v.astype(BF16)
            return carry

        lax.fori_loop(0, N_BLK, dkv_body, 0, unroll=2)

    return pl.pallas_call(
        kern, name=f"att_bwd_g{gi}", grid=(ATT_HPG,),
        in_specs=[_head_specs(0), _head_specs(ATT_HPG), _head_specs(2 * ATT_HPG),
                  _head_specs(0), _head_specs(0), _head_specs(0),
                  pl.BlockSpec((None, None, blk, 2 * blk), lambda h: (gi, h, 0, 0))],
        out_specs=[_head_specs(0), _head_specs(0), _head_specs(0),
                   pl.BlockSpec((None, blk, 2 * blk), lambda h: (h, 0, 0))],
        out_shape=[jax.ShapeDtypeStruct((SEQ, ATT_W), BF16)] * 3
        + [jax.ShapeDtypeStruct((ATT_HPG, blk, 2 * blk), F32)],
        scratch_shapes=[pltpu.VMEM((_PAD_ROWS, dh), BF16)] * 4 + [pltpu.VMEM((_PAD_ROWS, dh), F32)] * 2,
        compiler_params=_cparams(("arbitrary",)),
    )(qkv, qkv, qkv, d_att, lse, dd, bias)


def _rms_parts(x):
    r = lax.rsqrt(jnp.mean(x * x, axis=-1, keepdims=True) + RMS_EPS)
    return x * r, r


def _rms_bwd(d_xhat, xhat, r):
    return r * (d_xhat - xhat * jnp.mean(d_xhat * xhat, axis=-1, keepdims=True))


def _prenorm_fwd(name, x, gain, shift, scale):
    def body(xt, g, sh, sc):
        xhat, _ = _rms_parts(xt)
        return (xhat * g) * (1.0 + sc) + sh
    return _rowmap(name, body, [x], [gain, shift, scale], [(D_MODEL, BF16)])[0]


def _prenorm_bwd(name, d_hs, x, gain, scale, resid, after=()):
    n_dh = len(d_hs)

    def body(*args):
        d_h = args[0]
        for t in args[1:n_dh]:
            d_h = d_h + t
        xt, res, g, sc = args[n_dh:]
        xhat, r = _rms_parts(xt)
        nrm = xhat * g
        d_n = d_h * (1.0 + sc)
        dx = _rms_bwd(d_n * g, xhat, r) + res
        return (dx, jnp.sum(d_h, axis=0, keepdims=True), jnp.sum(d_h * nrm, axis=0, keepdims=True),
                jnp.sum(d_n * xhat, axis=0, keepdims=True))

    return _rowmap(name, body, list(d_hs) + [x, resid], [gain, scale], [(D_MODEL, F32)],
                   [D_MODEL, D_MODEL, D_MODEL], after=after)


def _gn_parts(ro):
    mu = jnp.mean(ro, axis=-1, keepdims=True)
    cen = ro - mu
    rstd = lax.rsqrt(jnp.mean(cen * cen, axis=-1, keepdims=True) + GN_EPS)
    return cen * rstd, rstd


def _retpost_fwd(ro, rg, gn_g, gn_b):
    def body(rot, rgt, g, b):
        outs = []
        for h in range(RET_HEADS):
            sl = slice(h * RET_DV, (h + 1) * RET_DV)
            nrm, _ = _gn_parts(rot[:, sl])
            gate = rgt[:, sl]
            outs.append((gate * _sigmoid(gate)) * (nrm * g[:, sl] + b[:, sl]))
        return jnp.concatenate(outs, axis=-1)
    return _rowmap("retpost_fwd", body, [ro, rg], [gn_g, gn_b], [(RET_V_W, BF16)])[0]


def _retpost_bwd(d_gated, ro, rg, gn_g, gn_b):
    def body(dgt, rot, rgt, g, b):
        d_ro, d_rg, d_g, d_b = [], [], [], []
        for h in range(RET_HEADS):
            sl = slice(h * RET_DV, (h + 1) * RET_DV)
            nrm, rstd = _gn_parts(rot[:, sl])
            gate, dg = rgt[:, sl], dgt[:, sl]
            sg = _sigmoid(gate)
            ron = nrm * g[:, sl] + b[:, sl]
            d_rg.append(dg * ron * (sg * (1.0 + gate * (1.0 - sg))))
            d_ron = dg * (gate * sg)
            d_g.append(jnp.sum(d_ron * nrm, axis=0, keepdims=True))
            d_b.append(jnp.sum(d_ron, axis=0, keepdims=True))
            d_n = d_ron * g[:, sl]
            d_ro.append(rstd * (d_n - jnp.mean(d_n, axis=-1, keepdims=True)
                                - nrm * jnp.mean(d_n * nrm, axis=-1, keepdims=True)))
        cat = lambda ts: jnp.concatenate(ts, axis=-1)
        return cat(d_ro), cat(d_rg), cat(d_g), cat(d_b)
    return _rowmap("retpost_bwd", body, [d_gated, ro, rg], [gn_g, gn_b],
                   [(RET_V_W, F32), (RET_V_W, BF16)], [RET_V_W, RET_V_W])


def _combine(os_, ls_):
    def body(o0, o1, o2, l0, l1, l2):
        mx = jnp.maximum(jnp.maximum(l0, l1), l2)
        e0, e1, e2 = jnp.exp(l0 - mx), jnp.exp(l1 - mx), jnp.exp(l2 - mx)
        den = e0 + e1 + e2
        att = (e0 / den) * o0 + (e1 / den) * o1 + (e2 / den) * o2
        return att, att, mx + jnp.log(den)
    return _rowmap("att_combine", body, list(os_) + list(ls_), [],
                   [(ATT_W, F32), (ATT_W, BF16), (ATT_W, F32)])


def _att_bwd_pre(d_att, att):
    def body(dt, at):
        outs = []
        for h in range(ATT_HPG):
            sl = slice(h * ATT_DH, (h + 1) * ATT_DH)
            outs.append(jnp.broadcast_to(jnp.sum(dt[:, sl] * at[:, sl], axis=-1, keepdims=True),
                                         (dt.shape[0], ATT_DH)))
        return dt, jnp.concatenate(outs, axis=-1)
    return _rowmap("att_bwd_pre", body, [d_att, att], [], [(ATT_W, BF16), (ATT_W, F32)])


def _merge_fwd(gates, ret_out, att_out):
    def body(gt, ro, ao):
        return _sigmoid(gt[:, :D_MODEL]) * ro + _sigmoid(gt[:, D_MODEL:]) * ao
    return _rowmap("merge_fwd", body, [gates, ret_out, att_out], [], [(D_MODEL, BF16)])[0]


def _merge_bwd(d_merged, gates, ret_out, att_out):
    def body(dm, gt, ro, ao):
        sa, sb = _sigmoid(gt[:, :D_MODEL]), _sigmoid(gt[:, D_MODEL:])
        d_gates = jnp.concatenate([dm * ro * (sa * (1.0 - sa)), dm * ao * (sb * (1.0 - sb))], axis=-1)
        return dm * sa, dm * sb, d_gates
    return _rowmap("merge_bwd", body, [d_merged, gates, ret_out, att_out], [],
                   [(D_MODEL, BF16), (D_MODEL, BF16), (2 * D_MODEL, BF16)])


def _gate_bwd(name, d_x, branch, gate):
    def body(dx, br, g):
        return dx * g, jnp.sum(dx * br, axis=0, keepdims=True)
    return _rowmap(name, body, [d_x, branch], [gate], [(D_MODEL, BF16)], [D_MODEL])


def _loss_head(x3, target, gain):
    def body(xt, tt, g):
        xhat, r = _rms_parts(xt)
        err = xhat * g - tt
        d_y = err / D_MODEL
        loss = 0.5 * jnp.sum(jnp.mean(err * err, axis=-1, keepdims=True), axis=0, keepdims=True)
        d_x = _rms_bwd(d_y * g, xhat, r)
        return d_x, jnp.broadcast_to(loss, (1, 128)), jnp.sum(d_y * xhat, axis=0, keepdims=True)
    return _rowmap("loss_head", body, [x3, target], [gain], [(D_MODEL, F32)], [128, D_MODEL])


def _local_step(pos, x, target, mod, norm1_g, norm2_g, norm_f_g, rel_bias, gn_g, gn_b, w_in, rest_gather):
    sh1, sc1, g1, sh2, sc2, g2 = [mod[:, i * D_MODEL:(i + 1) * D_MODEL] for i in range(6)]
    cos, sin = _rope_tables()
    din, qd, kd, cd = _decay_tables()
    buckets = _bucket_tables()
    bias = _bias_build(rel_bias, buckets)
    dils = [d for _, d in ATT_GROUPS]
    nbs = [SEQ // d // ATT_BLK for d in dils]

    h1 = _prenorm_fwd("prenorm1_fwd", x, norm1_g, sh1, sc1)
    h1_p = [_permute_rows(h1, d) for d in dils]

    def rot_epi(acc, cs, sn, scale):
        half = RET_DK // 2
        x1, x2 = acc[:, :half], acc[:, half:]
        return (jnp.concatenate([x1 * cs - x2 * sn, x1 * sn + x2 * cs], axis=-1) * scale,)

    qk_scale = jnp.concatenate([jnp.ones((1, RET_QK_W), F32),
                                jnp.full((1, RET_QK_W), RET_DK ** -0.5, F32)], axis=-1)
    rope_ex = [(cos, (TM, RET_DK // 2), lambda i, j, kk: (i, 0)),
               (sin, (TM, RET_DK // 2), lambda i, j, kk: (i, 0)),
               (qk_scale, (1, RET_DK), lambda i, j, kk: (0, j))]
    rest_sems, rest_shards, rest_fulls, rest_token = rest_gather
    behind = [rest_token]
    rv = _matmul("proj_rv", h1, w_in, "nn", SEQ, RET_V_W, D_MODEL, [BF16], b_off=OFF_V, tk=D_MODEL,
                 after=behind)[0]
    rg = _matmul("proj_rg", h1, w_in, "nn", SEQ, RET_V_W, D_MODEL, [F32], b_off=OFF_G, tk=D_MODEL,
                 after=behind)[0]
    gates = _matmul("proj_gates", h1, w_in, "nn", SEQ, 2 * D_MODEL, D_MODEL, [F32], b_off=OFF_GATE,
                    tn=512, tk=D_MODEL, after=behind)[0]
    aqkv = [_matmul(f"proj_att_g{gi}", h1_p[gi], w_in, "nn", SEQ, 3 * ATT_W, D_MODEL, [BF16],
                    b_off=OFF_ATT + gi * 3 * ATT_W, tn=512, tk=D_MODEL, after=behind)[0]
            for gi in range(3)]

    os_, ls_ = [], []
    for gi in range(3):
        o_g, l_g = _att_fwd(gi, aqkv[gi], bias, nbs[gi])
        os_.append(_unpermute_rows(o_g, dils[gi]))
        ls_.append(_unpermute_rows(l_g, dils[gi]))
        if gi == 1:
            rest_sems, rest_fulls, fwd_token = _gather_rest_forward(rest_sems, rest_shards, rest_fulls,
                                                                    [o_g, rv, rg, gates])

    rqk = _matmul("proj_qk", h1, w_in, "nn", SEQ, 2 * RET_QK_W, D_MODEL, [BF16], b_off=OFF_Q,
                  tn=RET_DK, tk=D_MODEL, epilogue=rot_epi, extras=rope_ex, after=[fwd_token])[0]
    ro, states = _retention_fwd(rqk, rv, din, qd, kd, cd)
    gated = _retpost_fwd(ro, rg, gn_g, gn_b)
    w_ret_out, w_att_out, w_o, w_ff1, w_ff2 = _gather_rest_end(rest_sems, rest_fulls, [gated, os_[2]])
    ret_out = _matmul("ret_out", gated, w_ret_out, "nn", SEQ, D_MODEL, RET_V_W, [F32])[0]
    att, att_b, lse = _combine(os_, ls_)
    att_out = _matmul("att_out", att_b, w_att_out, "nn", SEQ, D_MODEL, ATT_W, [F32])[0]

    merged = _merge_fwd(gates, ret_out, att_out)

    def resid_epi(acc, xt, g):
        return xt + g * acc, acc

    def resid_ex(xin, g):
        return [(xin, (TM, TN), lambda i, j, kk: (i, j)), (g, (1, TN), lambda i, j, kk: (0, j))]

    x2, mix = _matmul("mix_out", merged, w_o, "nn", SEQ, D_MODEL, D_MODEL, [F32, F32],
                      epilogue=resid_epi, extras=resid_ex(x, g1))
    h2 = _prenorm_fwd("prenorm2_fwd", x2, norm2_g, sh2, sc2)

    def relu2_epi(acc):
        r = jnp.maximum(acc, 0.0)
        return r * r, acc

    act, u = _matmul("ff1", h2, w_ff1, "nn", SEQ, D_FF, D_MODEL, [BF16, F32], tk=D_MODEL,
                     epilogue=relu2_epi)
    x3, y2 = _matmul("ff2", act, w_ff2, "nn", SEQ, D_MODEL, D_FF, [F32, F32],
                     epilogue=resid_epi, extras=resid_ex(x2, g2))

    d_x3, loss, d_gf = _loss_head(x3, target, norm_f_g)

    d_y2, d_g2 = _gate_bwd("ff_gate_bwd", d_x3, y2, g2)

    def relu2_bwd_epi(acc, ut):
        return (acc * (2.0 * jnp.maximum(ut, 0.0)),)

    gw_ff2 = _matmul_tn_pair("ff2_dw", pos, act, d_y2, D_FF, D_MODEL, SEQ, D_FF // N_CHIPS,
                             tm=512, tn=1024, tk=1024)
    d_u = _matmul("ff2_dx", d_y2, w_ff2, "nt", SEQ, D_FF, D_MODEL, [BF16], epilogue=relu2_bwd_epi,
                  extras=[(u, (TM, TN), lambda i, j, kk: (i, j))])[0]
    gw_ff1 = _matmul_tn_pair("ff1_dw", pos, h2, d_u, D_MODEL, D_FF, SEQ, D_MODEL,
                             tm=512, tn=1024, tk=1024)
    ffn = ["w_ff2", "w_ff1"]
    ffn_started = _ici_start("ici_start_ffn", ffn, [gw_ff2, gw_ff1])
    d_h2 = _matmul("ff1_dx", d_u, w_ff1, "nt", SEQ, D_MODEL, D_FF, [F32], after=[ffn_started[3]])[0]
    d_x2, d_sh2, d_sc2, d_n2g = _prenorm_bwd("prenorm2_bwd", [d_h2], x2, norm2_g, sc2, d_x3)

    d_mix, d_g1 = _gate_bwd("mix_gate_bwd", d_x2, mix, g1)
    gw_o = _matmul_tn_pair("mix_dw", pos, merged, d_mix, D_MODEL, D_MODEL, SEQ, D_MODEL // N_CHIPS,
                           tm=128, tn=1024, tk=2048)
    d_merged = _matmul("mix_dx", d_mix, w_o, "nt", SEQ, D_MODEL, D_MODEL, [F32])[0]
    d_ret_out, d_att_out, d_gates = _merge_bwd(d_merged, gates, ret_out, att_out)

    gw_ret_out = _matmul_tn_pair("ret_out_dw", pos, gated, d_ret_out, RET_V_W, D_MODEL, SEQ,
                                 RET_V_W // N_CHIPS, tm=256, tn=1024, tk=1024)
    gw_att_out = _matmul_tn_pair("att_out_dw", pos, att_b, d_att_out, ATT_W, D_MODEL, SEQ, ATT_W,
                                 tm=256, tn=1024, tk=2048)
    mixer = ["w_o", "w_ret_out", "w_att_out"]
    mixer_started = _ici_start("ici_start_mixer", mixer, [gw_o, gw_ret_out, gw_att_out])
    d_gated = _matmul("ret_out_dx", d_ret_out, w_ret_out, "nt", SEQ, RET_V_W, D_MODEL, [F32],
                      after=[mixer_started[3]])[0]
    d_att = _matmul("att_out_dx", d_att_out, w_att_out, "nt", SEQ, ATT_W, D_MODEL, [F32],
                    after=[mixer_started[3]])[0]

    d_ro, d_rg, d_gn_g, d_gn_b = _retpost_bwd(d_gated, ro, rg, gn_g, gn_b)
    d_rq, d_rk, d_rv = _retention_bwd(rqk, rv, states, d_ro, din, qd, kd, cd, cos, sin)

    d_att_b, dd = _att_bwd_pre(d_att, att)
    d_aqkv, dsbs = [], []
    for gi in range(3):
        da_p = _permute_rows(d_att_b, dils[gi])
        l_p = _permute_rows(lse, dils[gi])
        dd_p = _permute_rows(dd, dils[gi])
        dq, dk, dv, dsb = _att_bwd(gi, aqkv[gi], da_p, l_p, dd_p, bias, nbs[gi])
        d_aqkv.append(_unpermute_rows(jnp.concatenate([dq, dk, dv], axis=-1), dils[gi]))
        dsbs.append(dsb)
    d_rel_bias = _bias_grad(jnp.stack(dsbs), buckets)

    d_proj = jnp.concatenate([d_rq, d_rk, d_rv, d_rg] + d_aqkv + [d_gates], axis=-1)
    gw_in = _matmul_tn_pair("proj_dw", pos, h1, d_proj, D_MODEL, IN_COLS, SEQ, D_MODEL,
                            tm=512, tn=640, tk=2048)
    sems, (gw_in,), (land,), token = _ici_start("ici_start_w_in", ["w_in"], [gw_in])
    d_h1 = _matmul("proj_dx", d_proj, w_in, "nt", SEQ, D_MODEL, IN_COLS, [F32], tn=1024, tk=1280,
                   after=[token])[0]
    pending = (sems, land)

    names = ffn + mixer
    psums, got = _ici_wait("ici_wait_rest", names, list(ffn_started[0]) + list(mixer_started[0]),
                           list(ffn_started[1]) + list(mixer_started[1]),
                           list(ffn_started[2]) + list(mixer_started[2]), [d_h1])
    g_big = {n: _final_sum("final_" + n, pos, dict(BIG)[n], psums[i], got[i], SHARD[n])
             for i, n in enumerate(names)}
    grad_x, d_sh1, d_sc1, d_n1g = _prenorm_bwd("prenorm1_bwd", [d_h1], x, norm1_g, sc1, d_x2,
                                               after=list(g_big.values()))
    d_mod = jnp.concatenate([d_sh1, d_sc1, d_g1, d_sh2, d_sc2, d_g2], axis=-1)
    small = dict(norm1_g=d_n1g, norm2_g=d_n2g, norm_f_g=d_gf, gn_g=d_gn_g, gn_b=d_gn_b,
                 rel_bias=d_rel_bias)
    return loss, grad_x, d_mod, small, g_big, (gw_in,) + pending


def _me():
    return lax.axis_index("x"), lax.axis_index("y"), lax.axis_index("c")


def _peer(x, y, c, mask):
    return (x ^ ((mask >> 2) & 1), y ^ ((mask >> 1) & 1), c ^ (mask & 1))


def _gather8(src_ref, dst_ref, send_sems, recv_sems):
    x, y, c = _me()
    me = 4 * x + 2 * y + c
    copies = []
    for mask in range(1, N_DEV):
        cp = pltpu.make_async_remote_copy(
            src_ref=src_ref, dst_ref=dst_ref.at[me], send_sem=send_sems.at[mask - 1],
            recv_sem=recv_sems.at[mask - 1], device_id=_peer(x, y, c, mask), device_id_type=MESH)
        cp.start()
        copies.append(cp)
    dst_ref[me] = src_ref[...]
    for cp in copies:
        cp.wait_recv()
    for cp in copies:
        cp.wait_send()


def _ada_fwd(c_in, w_ada, b_ada):
    ncol = ADA_COLS // N_CHIPS

    def body(c_ref, w_ref, b_ref, mod_ref, sc_ref, cbuf, cg, mbuf, mg, s1, r1, s2, r2):
        x, y, c = _me()
        me = 4 * x + 2 * y + c
        cv = c_ref[...]
        cbuf[...] = jnp.broadcast_to(cv * _sigmoid(cv), cbuf.shape)
        _gather8(cbuf, cg, s1, r1)
        rows = lax.broadcasted_iota(I32, (N_DEV, D_MODEL), 0)
        sc_all = jnp.zeros((N_DEV, D_MODEL), F32)
        for d in range(N_DEV):
            sc_all = jnp.where(rows == d, cg[d], sc_all)
        sc_ref[...] = sc_all
        mbuf[...] = jnp.dot(sc_all.astype(BF16), w_ref[...].astype(BF16), preferred_element_type=F32)
        _gather8(mbuf, mg, s2, r2)
        rowsel = lax.broadcasted_iota(I32, (N_DEV, ncol), 0) == me
        for k in range(N_CHIPS):
            blk = mg[2 * k]
            row = jnp.sum(jnp.where(rowsel, blk, 0.0), axis=0, keepdims=True)
            mod_ref[:, k * ncol:(k + 1) * ncol] = row + b_ref[:, k * ncol:(k + 1) * ncol]

    vm = pl.BlockSpec(memory_space=pltpu.VMEM)
    return pl.pallas_call(
        body, name="ada_fwd",
        in_specs=[vm, vm, vm], out_specs=[vm, vm],
        out_shape=[jax.ShapeDtypeStruct((1, ADA_COLS), F32), jax.ShapeDtypeStruct((N_DEV, D_MODEL), F32)],
        scratch_shapes=[
            pltpu.VMEM((8, D_MODEL), F32), pltpu.VMEM((N_DEV, 8, D_MODEL), F32),
            pltpu.VMEM((8, ncol), F32), pltpu.VMEM((N_DEV, 8, ncol), F32),
            pltpu.SemaphoreType.DMA((N_DEV - 1,)), pltpu.SemaphoreType.DMA((N_DEV - 1,)),
            pltpu.SemaphoreType.DMA((N_DEV - 1,)), pltpu.SemaphoreType.DMA((N_DEV - 1,)),
        ],
        compiler_params=pltpu.CompilerParams(vmem_limit_bytes=VMEM_LIMIT_V7X),
    )(c_in, w_ada, b_ada)


def _small_reduce(pack, sc_all):
    ncol = ADA_COLS // N_CHIPS

    def body(p_ref, sc_ref, tot_ref, gw_ref, pg, s1, r1):
        x, y, _ = _me()
        chip = 2 * x + y
        _gather8(p_ref, pg, s1, r1)
        tot = pg[0]
        for d in range(1, N_DEV):
            tot = tot + pg[d]
        tot_ref[...] = tot
        rows = lax.broadcasted_iota(I32, (N_DEV, ncol), 0)
        dmod = jnp.zeros((N_DEV, ncol), F32)
        for k in range(N_CHIPS):
            part = jnp.zeros((N_DEV, ncol), F32)
            for d in range(N_DEV):
                part = jnp.where(rows == d, pg[d, :, k * ncol:(k + 1) * ncol][0:1, :], part)
            dmod = jnp.where(chip == k, part, dmod)
        gw_ref[...] = lax.dot_general(sc_ref[...].astype(BF16), dmod.astype(BF16), _TN,
                                      preferred_element_type=F32)

    vm = pl.BlockSpec(memory_space=pltpu.VMEM)
    return pl.pallas_call(
        body, name="small_reduce",
        in_specs=[vm, vm], out_specs=[vm, vm],
        out_shape=[jax.ShapeDtypeStruct((8, ADA_COLS), F32), jax.ShapeDtypeStruct((D_MODEL, ncol), F32)],
        scratch_shapes=[pltpu.VMEM((N_DEV, 8, ADA_COLS), F32),
                        pltpu.SemaphoreType.DMA((N_DEV - 1,)), pltpu.SemaphoreType.DMA((N_DEV - 1,))],
        compiler_params=pltpu.CompilerParams(vmem_limit_bytes=VMEM_LIMIT_V7X),
    )(pack, sc_all)


BIG = (("w_in", 1), ("w_ret_out", 0), ("w_att_out", 1), ("w_o", 0), ("w_ff1", 1), ("w_ff2", 0))
SHARD = {"w_in": (D_MODEL, IN_COLS // N_CHIPS), "w_ret_out": (RET_V_W // N_CHIPS, D_MODEL),
         "w_att_out": (ATT_W, D_MODEL // N_CHIPS), "w_o": (D_MODEL // N_CHIPS, D_MODEL),
         "w_ff1": (D_MODEL, D_FF // N_CHIPS), "w_ff2": (D_FF // N_CHIPS, D_MODEL)}
_CHIP_FLIPS = ((1, 0), (0, 1), (1, 1))


def _region(ref, axis, chip, half, shard_shape):
    r, cw = shard_shape
    hr = r // 2
    if axis == 1:
        return ref.at[pl.ds(half * hr, hr), pl.ds(chip * cw, cw)]
    return ref.at[pl.ds(chip * r + half * hr, hr), :]


def _gather_weights(shards, n_remote):
    nw = len(BIG)
    shapes = [s.shape for s in shards]
    full_shapes = [(r, N_CHIPS * cw) if ax == 1 else (N_CHIPS * r, cw)
                   for (r, cw), (_, ax) in zip(shapes, BIG)]

    def body(*refs):
        ins, outs = refs[:nw], refs[nw:2 * nw]
        own = refs[2 * nw:3 * nw]
        from_ici, from_sib = refs[3 * nw:3 * nw + n_remote], refs[3 * nw + n_remote:3 * nw + 2 * n_remote]
        ld_sem, st_sem, s_ici, r_ici, s_d2d, r_d2d, st_a, st_b = refs[3 * nw + 2 * n_remote:]
        x, y, c = _me()
        chip = 2 * x + y
        sib = (x, y, 1 - c)
        loads = [pltpu.make_async_copy(ins[i], own[i], ld_sem.at[i]) for i in range(nw)]
        for cp in loads:
            cp.start()
        pending, first = [], []
        for i, (_, ax) in enumerate(BIG):
            r, cw = shapes[i]
            hr = r // 2
            loads[i].wait()
            dst = outs[i].at[:, pl.ds(chip * cw, cw)] if ax == 1 else outs[i].at[pl.ds(chip * r, r), :]
            cp = pltpu.make_async_copy(own[i], dst, st_sem.at[i])
            cp.start()
            pending.append(cp)
            for j, (fx, fy) in enumerate(_CHIP_FLIPS if i < n_remote else ()):
                rc = pltpu.make_async_remote_copy(
                    src_ref=own[i].at[pl.ds(c * hr, hr), :], dst_ref=from_ici[i].at[j],
                    send_sem=s_ici.at[j * nw + i], recv_sem=r_ici.at[j * nw + i],
                    device_id=(x ^ fx, y ^ fy, c), device_id_type=MESH)
                rc.start()
                first.append((j, i, rc))
        passed = []
        for j, i, rc in first:
            fx, fy = _CHIP_FLIPS[j]
            src_chip = 2 * (x ^ fx) + (y ^ fy)
            ax = BIG[i][1]
            rc.wait_recv()
            fw = pltpu.make_async_remote_copy(
                src_ref=from_ici[i].at[j], dst_ref=from_sib[i].at[j], send_sem=s_d2d.at[j * nw + i],
                recv_sem=r_d2d.at[j * nw + i], device_id=sib, device_id_type=MESH)
            fw.start()
            passed.append((j, i, src_chip, fw))
            st = pltpu.make_async_copy(from_ici[i].at[j], _region(outs[i], ax, src_chip, c, shapes[i]),
                                       st_a.at[j * nw + i])
            st.start()
            pending.append(st)
        for j, i, src_chip, fw in passed:
            fw.wait_recv()
            st = pltpu.make_async_copy(from_sib[i].at[j],
                                       _region(outs[i], BIG[i][1], src_chip, 1 - c, shapes[i]),
                                       st_b.at[j * nw + i])
            st.start()
            pending.append(st)
        for _, _, rc in first:
            rc.wait_send()
        for _, _, _, fw in passed:
            fw.wait_send()
        for cp in pending:
            cp.wait()

    hbm = pl.BlockSpec(memory_space=pl.ANY)
    halves = [pltpu.VMEM((3, r // 2, cw), BF16) for r, cw in shapes[:n_remote]]
    return pl.pallas_call(
        body, name="gather_weights",
        in_specs=[hbm] * nw, out_specs=[hbm] * nw,
        out_shape=[jax.ShapeDtypeStruct(fs, BF16) for fs in full_shapes],
        scratch_shapes=[pltpu.VMEM(sh, BF16) for sh in shapes] + halves + halves
        + [pltpu.SemaphoreType.DMA((nw,)), pltpu.SemaphoreType.DMA((nw,))]
        + [pltpu.SemaphoreType.DMA((3 * nw,))] * 6,
        compiler_params=pltpu.CompilerParams(vmem_limit_bytes=VMEM_LIMIT_V7X),
    )(*shards)


REST = BIG[1:]
_SIDE_EFFECTS = pltpu.CompilerParams(has_side_effects=pltpu.SideEffectType.DATAFLOW_SIDE_EFFECTING)
_ANY_SPEC = pl.BlockSpec(memory_space=pl.ANY)


def _rest_ici_copies(shard_refs, full_refs, sems):
    x, y, c = _me()
    chip = 2 * x + y
    n = 3 * len(REST)
    copies = []
    for i, (name, ax) in enumerate(REST):
        hr = SHARD[name][0] // 2
        for j, (fx, fy) in enumerate(_CHIP_FLIPS):
            copies.append(pltpu.make_async_remote_copy(
                src_ref=shard_refs[i].at[pl.ds(c * hr, hr), :],
                dst_ref=_region(full_refs[i], ax, chip, c, SHARD[name]),
                send_sem=sems[3 * i + j], recv_sem=sems[n + 3 * i + j],
                device_id=(x ^ fx, y ^ fy, c), device_id_type=MESH))
    return copies


def _rest_d2d_copies(full_refs, sems):
    x, y, c = _me()
    n = 3 * len(REST)
    copies = []
    for i, (name, ax) in enumerate(REST):
        for j, (fx, fy) in enumerate(_CHIP_FLIPS):
            reg = _region(full_refs[i], ax, 2 * (x ^ fx) + (y ^ fy), c, SHARD[name])
            copies.append(pltpu.make_async_remote_copy(
                src_ref=reg, dst_ref=reg, send_sem=sems[3 * i + j], recv_sem=sems[n + 3 * i + j],
                device_id=(x, y, 1 - c), device_id_type=MESH))
    return copies


def _gather_rest_start(shards, fulls, after):
    nr, ns, na = len(REST), 6 * len(REST), len(after)

    def body(*refs):
        for cp in _rest_ici_copies(refs[:nr], refs[nr:2 * nr], refs[2 * nr + na:2 * nr + na + ns]):
            cp.start()
        token = refs[-1]
        token[...] = jnp.zeros_like(token)

    hbm = lambda a: pltpu.HBM(a.shape, a.dtype)
    res = pl.pallas_call(
        body, name="gather_rest_start",
        out_shape=(pltpu.SemaphoreType.DMA(()),) * ns + tuple(hbm(a) for a in shards + fulls)
        + (jax.ShapeDtypeStruct((8, 128), F32),),
        in_specs=(_HBM_SPEC,) * (2 * nr) + (_ANY_SPEC,) * na,
        out_specs=(_SEM_SPEC,) * ns + (_HBM_SPEC,) * (2 * nr) + (pl.BlockSpec(memory_space=pltpu.VMEM),),
        input_output_aliases={k: ns + k for k in range(2 * nr)}, compiler_params=_SIDE_EFFECTS,
    )(*[pltpu.with_memory_space_constraint(a, pltpu.HBM) for a in shards + fulls], *after)
    return res[:ns], res[ns:ns + nr], res[ns + nr:ns + 2 * nr], res[-1]


def _gather_rest_forward(sems, shards, fulls, after):
    nr, ns = len(REST), 6 * len(REST)

    def body(*refs):
        shard_refs, full_refs, old = refs[:nr], refs[nr:2 * nr], refs[2 * nr:2 * nr + ns]
        new = refs[2 * nr + ns + len(after):2 * nr + 2 * ns + len(after)]
        for cp in _rest_ici_copies(shard_refs, full_refs, old):
            cp.wait_send()
            cp.wait_recv()
        for cp in _rest_d2d_copies(full_refs, new):
            cp.start()
        token = refs[-1]
        token[...] = jnp.zeros_like(token)

    res = pl.pallas_call(
        body, name="gather_rest_forward",
        out_shape=(pltpu.SemaphoreType.DMA(()),) * ns + tuple(pltpu.HBM(a.shape, a.dtype) for a in fulls)
        + (jax.ShapeDtypeStruct((8, 128), F32),),
        in_specs=(_HBM_SPEC,) * (2 * nr) + (_SEM_SPEC,) * ns + (_ANY_SPEC,) * len(after),
        out_specs=(_SEM_SPEC,) * ns + (_HBM_SPEC,) * nr + (pl.BlockSpec(memory_space=pltpu.VMEM),),
        input_output_aliases={nr + k: ns + k for k in range(nr)}, compiler_params=_SIDE_EFFECTS,
    )(*shards, *fulls, *sems, *after)
    return res[:ns], res[ns:ns + nr], res[-1]


def _gather_rest_end(sems, fulls, after):
    nr, ns = len(REST), 6 * len(REST)

    def body(*refs):
        for cp in _rest_d2d_copies(refs[:nr], refs[nr:nr + ns]):
            cp.wait_send()
            cp.wait_recv()

    return pl.pallas_call(
        body, name="gather_rest_end",
        out_shape=tuple(pltpu.HBM(a.shape, a.dtype) for a in fulls),
        in_specs=(_HBM_SPEC,) * nr + (_SEM_SPEC,) * ns + (_ANY_SPEC,) * len(after),
        out_specs=(_HBM_SPEC,) * nr,
        input_output_aliases={k: k for k in range(nr)}, compiler_params=_SIDE_EFFECTS,
    )(*fulls, *sems, *after)


def _adam_update(w, g, m, v):
    mn = ADAM_B1 * m + (1.0 - ADAM_B1) * g
    vn = ADAM_B2 * v + (1.0 - ADAM_B2) * (g * g)
    m_hat = mn / (1.0 - ADAM_B1 ** ADAM_STEP)
    v_hat = vn / (1.0 - ADAM_B2 ** ADAM_STEP)
    return -ADAM_LR * (m_hat / (jnp.sqrt(v_hat) + ADAM_EPS) + ADAM_WD * w), mn, vn


def _final_sum(name, pos, axis, psum, recv, shard_shape, after=(), tr=128):
    r, cw = shard_shape
    hr = r // 2
    tr = min(tr, hr)
    nt = hr // tr
    n_after = len(after)

    def kern(pos_ref, p_ref, r_ref, *rest):
        g_ref, send_buf, land_buf, s_sem, r_sem = rest[n_after:]
        p, t = pl.program_id(0), pl.program_id(1)
        sib = _sibling()

        def copy(i):
            return pltpu.make_async_remote_copy(
                src_ref=send_buf.at[i], dst_ref=land_buf.at[i], send_sem=s_sem.at[i],
                recv_sem=r_sem.at[i], device_id=sib, device_id_type=MESH)

        @pl.when(p == 0)
        def _():
            tot = p_ref[...].astype(F32)
            for j in range(3):
                tot = tot + r_ref[j].astype(F32)
            send_buf[t] = tot
            copy(t).start()
            g_ref[...] = tot

        @pl.when(p == 1)
        def _():
            copy(t).wait_recv()
            g_ref[...] = land_buf[t]

        @pl.when(jnp.logical_and(p == 1, t == nt - 1))
        def _():
            for i in range(nt):
                copy(i).wait_send()

    def shard_rows(p, t, pos_ref):
        return (jnp.where(p == 0, pos_ref[0], 1 - pos_ref[0]) * nt + t, 0)

    def own_part(p, t, pos_ref):
        tt = jnp.where(p == 0, t, nt - 1)
        return (tt, pos_ref[1]) if axis == 1 else (pos_ref[1] * nt + tt, 0)

    grid_spec = pltpu.PrefetchScalarGridSpec(
        num_scalar_prefetch=1, grid=(2, nt),
        in_specs=[pl.BlockSpec((tr, cw), own_part),
                  pl.BlockSpec((3, tr, cw), lambda p, t, pos_ref: (0, jnp.where(p == 0, t, nt - 1), 0))]
        + [pl.BlockSpec(memory_space=pl.ANY)] * n_after,
        out_specs=pl.BlockSpec((tr, cw), shard_rows),
        scratch_shapes=[pltpu.VMEM((nt, tr, cw), F32), pltpu.VMEM((nt, tr, cw), F32),
                        pltpu.SemaphoreType.DMA((nt,)), pltpu.SemaphoreType.DMA((nt,))])
    return pl.pallas_call(
        kern, name=name, grid_spec=grid_spec, out_shape=jax.ShapeDtypeStruct((r, cw), F32),
        compiler_params=_cparams(("arbitrary", "arbitrary")),
    )(pos, psum, recv, *after)


def _adamw(name, w, g, m, v):
    r, cw = w.shape
    tr = min(r, 128)

    def kern(w_ref, g_ref, m_ref, v_ref, go_ref, d_ref, nm_ref, nv_ref):
        gv = g_ref[...]
        go_ref[...] = gv
        d_ref[...], nm_ref[...], nv_ref[...] = _adam_update(w_ref[...], gv, m_ref[...], v_ref[...])

    spec = pl.BlockSpec((tr, cw), lambda i: (i, 0))
    return pl.pallas_call(
        kern, name=name, grid=(r // tr,), in_specs=[spec] * 4, out_specs=[spec] * 4,
        out_shape=[jax.ShapeDtypeStruct((r, cw), F32)] * 4, compiler_params=_cparams(("parallel",)),
    )(w, g, m, v)


_PACK_W = ADA_COLS
_NB = REL_BUCKETS * N_ATT_HEADS
_SMALL_SLOTS = {
    "b_ada": (0, 0, ADA_COLS),
    "norm1_g": (1, 0, D_MODEL), "norm2_g": (1, D_MODEL, D_MODEL), "norm_f_g": (1, 2 * D_MODEL, D_MODEL),
    "ret_gn_g": (1, 3 * D_MODEL, RET_V_W),
    "ret_gn_b": (2, 0, RET_V_W), "rel_bias": (2, RET_V_W, _NB), "loss": (2, RET_V_W + 512, 128),
}


def _pack_small(vals):
    rows = []
    for r in range(8):
        items = sorted([(off, n) for n, (rr, off, _) in _SMALL_SLOTS.items() if rr == r and n in vals])
        parts, pos = [], 0
        for off, n in items:
            if off > pos:
                parts.append(jnp.zeros((1, off - pos), F32))
            parts.append(vals[n].reshape(1, -1).astype(F32))
            pos = off + _SMALL_SLOTS[n][2]
        if pos < _PACK_W:
            parts.append(jnp.zeros((1, _PACK_W - pos), F32))
        rows.append(jnp.concatenate(parts, axis=-1))
    return jnp.concatenate(rows, axis=0)


def _unpack_small(pack, name):
    r, off, wd = _SMALL_SLOTS[name]
    return pack[r:r + 1, off:off + wd]


def kernel(x, c, w_ada, b_ada, norm1_g, w_in, rel_bias, ret_gn_g, ret_gn_b, w_ret_out, w_att_out, w_o, norm2_g, w_ff1, w_ff2, norm_f_g, loss_target, m_w_ada, m_b_ada, m_norm1_g, m_w_in, m_rel_bias, m_ret_gn_g, m_ret_gn_b, m_w_ret_out, m_w_att_out, m_w_o, m_norm2_g, m_w_ff1, m_w_ff2, m_norm_f_g, v_w_ada, v_b_ada, v_norm1_g, v_w_in, v_rel_bias, v_ret_gn_g, v_ret_gn_b, v_w_ret_out, v_w_att_out, v_w_o, v_norm2_g, v_w_ff1, v_w_ff2, v_norm_f_g):
    given = dict(locals())
    big_names = [n for n, _ in BIG]
    shard_w = {n: given[n][0] for n in big_names}
    assert all(shard_w[n].shape == SHARD[n] for n in big_names)

    shards_bf = [shard_w[n].astype(BF16) for n in big_names]
    full = _gather_weights(shards_bf, 1)
    mod, sc_all = _ada_fwd(c, w_ada[0], b_ada)
    rest_gather = _gather_rest_start(shards_bf[1:], list(full[1:]), [mod])
    pos = _where_am_i()

    loss, grad_x, d_mod, small, g_big, pending = _local_step(
        pos, x[0], loss_target[0], mod, norm1_g, norm2_g, norm_f_g.reshape(1, -1), rel_bias, ret_gn_g,
        ret_gn_b, full[0], rest_gather)

    pack_g = _pack_small(dict(b_ada=d_mod, norm1_g=small["norm1_g"], norm2_g=small["norm2_g"],
                              norm_f_g=small["norm_f_g"], ret_gn_g=small["gn_g"], ret_gn_b=small["gn_b"],
                              rel_bias=small["rel_bias"], loss=loss))
    tot, g_w_ada = _small_reduce(pack_g, sc_all)

    small_names = ["b_ada", "norm1_g", "rel_bias", "ret_gn_g", "ret_gn_b", "norm2_g", "norm_f_g"]
    pack_w = _pack_small({n: given[n] for n in small_names})
    pack_m = _pack_small({n: given["m_" + n] for n in small_names})
    pack_v = _pack_small({n: given["v_" + n] for n in small_names})
    _, sd, sm, sv = _adamw("adamw_small", pack_w, tot, pack_m, pack_v)

    grads, deltas, new_m, new_v = {}, {}, {}, {}
    for n in small_names:
        shp = given[n].shape
        grads[n] = _unpack_small(tot, n).reshape(shp)
        deltas[n] = _unpack_small(sd, n).reshape(shp)
        new_m[n] = _unpack_small(sm, n).reshape(shp)
        new_v[n] = _unpack_small(sv, n).reshape(shp)
    g_big["w_ada"] = g_w_ada
    for n in ["w_ada"] + big_names[1:] + big_names[:1]:
        if n == "w_in":
            gw_in, sems, land = pending
            done = [tot, sd] + [deltas[k] for k in ["w_ada"] + big_names[1:]]
            (gw_in,), (got,) = _ici_wait("ici_wait_w_in", [n], sems, [gw_in], [land], done)
            g_big[n] = _final_sum("final_w_in", pos, 1, gw_in, got, SHARD[n])
        g, d, nm, nv = _adamw("adamw_" + n, given[n][0], g_big[n], given["m_" + n][0], given["v_" + n][0])
        grads[n], deltas[n], new_m[n], new_v[n] = g[None], d[None], nm[None], nv[None]

    order = ["w_ada", "b_ada", "norm1_g", "w_in", "rel_bias", "ret_gn_g", "ret_gn_b", "w_ret_out",
             "w_att_out", "w_o", "norm2_g", "w_ff1", "w_ff2", "norm_f_g"]
    loss_out = _unpack_small(tot, "loss")[0, 0]
    return (loss_out, grad_x[None], *[grads[n] for n in order], *[deltas[n] for n in order],
            *[new_m[n] for n in order], *[new_v[n] for n in order])
```

```python
import functools
import math

import jax
import jax.numpy as jnp
import numpy as np
from jax import lax
from jax.experimental import pallas as pl
from jax.experimental.pallas import tpu as pltpu

F32 = jnp.float32
BF16 = jnp.bfloat16
I32 = jnp.int32

SEQ = 2048
D_MODEL = 1024
RET_HEADS = 4
RET_DK = 256
RET_DV = 512
RET_CHUNK = 128
RET_QK_W = RET_HEADS * RET_DK
RET_V_W = RET_HEADS * RET_DV
ATT_GROUPS = ((128, 1), (512, 4), (2048, 16))
ATT_HPG = 4
ATT_DH = 128
ATT_W = ATT_HPG * ATT_DH
ATT_BLK = 128
N_BLK = SEQ // ATT_BLK
REL_BUCKETS = 32
REL_MAX_DIST = 2048
N_ATT_HEADS = 12
D_FF = 4 * D_MODEL
RMS_EPS = 1e-6
GN_EPS = 1e-5
ROPE_BASE = 10000.0
IN_COLS = 2 * RET_QK_W + 2 * RET_V_W + 9 * ATT_W + 2 * D_MODEL
OFF_Q, OFF_K, OFF_V, OFF_G = 0, RET_QK_W, 2 * RET_QK_W, 2 * RET_QK_W + RET_V_W
OFF_ATT = 2 * RET_QK_W + 2 * RET_V_W
OFF_GATE = OFF_ATT + 9 * ATT_W
N_CHIPS = 4
N_DEV = 8
ADA_COLS = 6 * D_MODEL

ADAM_LR = 0.001
ADAM_B1 = 0.9
ADAM_B2 = 0.999
ADAM_EPS = 1e-08
ADAM_WD = 0.01
ADAM_STEP = 10

VMEM_LIMIT_V7X = 56 * 1024 * 1024
MESH = pl.DeviceIdType.MESH


def _cparams(sem):
    return pltpu.CompilerParams(dimension_semantics=sem, vmem_limit_bytes=VMEM_LIMIT_V7X)


def _sigmoid(v):
    return 1.0 / (1.0 + jnp.exp(-v))


def _rowmap(name, body, row_ins, bcast_ins, row_outs, sum_outs=(), tm=256, after=()):
    m = row_ins[0].shape[0]
    n_in = len(row_ins) + len(bcast_ins)
    n_ro = len(row_outs)

    def kern(*refs):
        vals = [r[...] for r in refs[:n_in]]
        res = body(*vals)
        if not isinstance(res, (tuple, list)):
            res = (res,)
        outs = refs[n_in + len(after):]
        for r, v in zip(outs[:n_ro], res[:n_ro]):
            r[...] = v.astype(r.dtype)
        if sum_outs:
            @pl.when(pl.program_id(0) == 0)
            def _():
                for r in outs[n_ro:]:
                    r[...] = jnp.zeros_like(r)
            for r, v in zip(outs[n_ro:], res[n_ro:]):
                r[...] += v

    in_specs = [pl.BlockSpec((tm, a.shape[1]), lambda i: (i, 0)) for a in row_ins]
    in_specs += [pl.BlockSpec(a.shape, lambda i: (0, 0)) for a in bcast_ins]
    in_specs += [pl.BlockSpec(memory_space=pl.ANY)] * len(after)
    out_specs = [pl.BlockSpec((tm, n), lambda i: (i, 0)) for n, _ in row_outs]
    out_specs += [pl.BlockSpec((1, n), lambda i: (0, 0)) for n in sum_outs]
    out_shape = [jax.ShapeDtypeStruct((m, n), dt) for n, dt in row_outs]
    out_shape += [jax.ShapeDtypeStruct((1, n), F32) for n in sum_outs]
    return pl.pallas_call(
        kern, name=name, grid=(m // tm,), in_specs=in_specs, out_specs=out_specs,
        out_shape=out_shape, compiler_params=_cparams(("arbitrary",)),
    )(*row_ins, *bcast_ins, *after)


TM, TN = 1024, 1024


def _piece_chunks(piece, width):
    arr, stacked = piece
    return arr.shape[0] if stacked else arr.shape[1] // width


def _piece_spec(piece, rows, width, start, row_of, chunk_of):
    arr, stacked = piece
    last = _piece_chunks(piece, width) - 1

    def local(*ids):
        return jnp.clip(chunk_of(*ids) - start, 0, last)

    if stacked:
        return pl.BlockSpec((None, rows, width), lambda *ids: (local(*ids), row_of(*ids), 0))
    return pl.BlockSpec((rows, width), lambda *ids: (row_of(*ids), local(*ids)))


def _piece_starts(pieces, width):
    return [sum(_piece_chunks(p, width) for p in pieces[:q]) for q in range(len(pieces))]


def _matmul(name, a, b, kind, m, n, k, outs, *, b_off=0, tm=TM, tn=TN, tk=1024,
            epilogue=None, extras=(), after=()):
    tm, tn, tk = min(tm, m), min(tn, n), min(tk, k)
    nk = k // tk
    pieces = a if isinstance(a, list) else [(a, False)]
    starts = _piece_starts(pieces, tk)
    if kind == "nn":
        a_specs = [pl.BlockSpec((tm, tk), lambda i, j, kk: (i, kk))]
        b_spec = pl.BlockSpec((tk, tn), lambda i, j, kk: (kk, b_off // tn + j))
        dn = (((1,), (0,)), ((), ()))
    elif kind == "nt":
        a_specs = [_piece_spec(p, tm, tk, st, lambda i, j, kk: i, lambda i, j, kk: kk)
                   for p, st in zip(pieces, starts)]
        b_spec = pl.BlockSpec((tn, tk), lambda i, j, kk: (j, b_off // tk + kk))
        dn = (((1,), (1,)), ((), ()))
    else:
        a_specs = [pl.BlockSpec((tk, tm), lambda i, j, kk: (kk, i))]
        b_spec = pl.BlockSpec((tk, tn), lambda i, j, kk: (kk, j))
        dn = (((0,), (0,)), ((), ()))
    n_a, n_ex, n_out = len(pieces), len(extras), len(outs)
    if epilogue is None:
        epilogue = lambda acc: (acc,)

    def finish(acc, ex_refs, out_refs):
        res = epilogue(acc, *[r[...] for r in ex_refs])
        for r, v in zip(out_refs, res):
            r[...] = v.astype(r.dtype)

    n_in = n_a + 1 + n_ex + len(after)

    def kern(*refs):
        a_refs, b_ref = refs[:n_a], refs[n_a]
        ex_refs = refs[n_a + 1:n_a + 1 + n_ex]
        out_refs = refs[n_in:n_in + n_out]
        kk = pl.program_id(2)
        dot = lambda a_ref: lax.dot_general(a_ref[...], b_ref[...], dn, preferred_element_type=F32)
        if nk == 1:
            finish(dot(a_refs[0]), ex_refs, out_refs)
            return
        acc_ref = refs[n_in + n_out]
        if n_a == 1:
            part = dot(a_refs[0])

            @pl.when(kk == 0)
            def _():
                acc_ref[...] = part

            @pl.when(kk > 0)
            def _():
                acc_ref[...] += part
        else:
            @pl.when(kk == 0)
            def _():
                acc_ref[...] = jnp.zeros_like(acc_ref)

            for q in range(n_a):
                @pl.when(jnp.logical_and(kk >= starts[q], kk < starts[q] + _piece_chunks(pieces[q], tk)))
                def _(q=q):
                    acc_ref[...] += dot(a_refs[q])

        @pl.when(kk == nk - 1)
        def _():
            finish(acc_ref[...], ex_refs, out_refs)

    in_specs = a_specs + [b_spec] + [pl.BlockSpec(bs, im) for _, bs, im in extras]
    in_specs += [pl.BlockSpec(memory_space=pl.ANY)] * len(after)
    return pl.pallas_call(
        kern, name=name, grid=(m // tm, n // tn, nk), in_specs=in_specs,
        out_specs=[pl.BlockSpec((tm, tn), lambda i, j, kk: (i, j)) for _ in outs],
        out_shape=[jax.ShapeDtypeStruct((m, n), dt) for dt in outs],
        scratch_shapes=[] if nk == 1 else [pltpu.VMEM((tm, tn), F32)],
        compiler_params=_cparams(("parallel", "parallel", "arbitrary")),
    )(*[p[0] for p in pieces], b, *[e[0] for e in extras], *after)


def _ici_copies(psum_ref, recv_ref, s_sem, r_sem, axis, shard_shape):
    x, y, c = _me()
    hr, cw = shard_shape[0] // 2, shard_shape[1]
    pick = lambda sems, j: sems[j] if isinstance(sems, (list, tuple)) else sems.at[j]
    copies = []
    for j, (fx, fy) in enumerate(_CHIP_FLIPS):
        chip = 2 * (x ^ fx) + (y ^ fy)
        src = psum_ref.at[:, pl.ds(chip * cw, cw)] if axis == 1 else psum_ref.at[pl.ds(chip * hr, hr), :]
        copies.append(pltpu.make_async_remote_copy(
            src_ref=src, dst_ref=recv_ref.at[j], send_sem=pick(s_sem, j), recv_sem=pick(r_sem, j),
            device_id=(x ^ fx, y ^ fy, c), device_id_type=MESH))
    return copies


_HBM_SPEC = pl.BlockSpec(memory_space=pltpu.HBM)
_SEM_SPEC = pl.BlockSpec(memory_space=pltpu.SEMAPHORE)


def _split_ici_copies(names, p_refs, land_refs, sems):
    copies = []
    for i, n in enumerate(names):
        copies += _ici_copies(p_refs[i], land_refs[i], list(sems[6 * i:6 * i + 3]),
                              list(sems[6 * i + 3:6 * i + 6]), dict(BIG)[n], SHARD[n])
    return copies


def _ici_start(name, names, psums):
    nw, ns = len(names), 6 * len(names)
    lands = [lax.empty((3, SHARD[n][0] // 2, SHARD[n][1]), BF16) for n in names]

    def body(*refs):
        for cp in _split_ici_copies(names, refs[:nw], refs[nw:2 * nw], refs[2 * nw:2 * nw + ns]):
            cp.start()
        token = refs[-1]
        token[...] = jnp.zeros_like(token)

    res = pl.pallas_call(
        body, name=name,
        out_shape=(pltpu.SemaphoreType.DMA(()),) * ns
        + tuple(pltpu.HBM(a.shape, BF16) for a in list(psums) + lands)
        + (jax.ShapeDtypeStruct((8, 128), F32),),
        in_specs=(_HBM_SPEC,) * (2 * nw),
        out_specs=(_SEM_SPEC,) * ns + (_HBM_SPEC,) * (2 * nw) + (pl.BlockSpec(memory_space=pltpu.VMEM),),
        input_output_aliases={k: ns + k for k in range(2 * nw)},
        compiler_params=pltpu.CompilerParams(has_side_effects=pltpu.SideEffectType.DATAFLOW_SIDE_EFFECTING),
    )(*[pltpu.with_memory_space_constraint(a, pltpu.HBM) for a in list(psums) + lands])
    return res[:ns], res[ns:ns + nw], res[ns + nw:ns + 2 * nw], res[-1]


def _ici_wait(name, names, sems, p_thru, land_thru, after):
    nw, ns = len(names), 6 * len(names)

    def body(*refs):
        for cp in _split_ici_copies(names, refs[:nw], refs[nw:2 * nw], refs[2 * nw:2 * nw + ns]):
            cp.wait_send()
            cp.wait_recv()

    res = pl.pallas_call(
        body, name=name,
        out_shape=tuple(pltpu.HBM(a.shape, BF16) for a in list(p_thru) + list(land_thru)),
        in_specs=(_HBM_SPEC,) * (2 * nw) + (_SEM_SPEC,) * ns + (pl.BlockSpec(memory_space=pl.ANY),) * len(after),
        out_specs=(_HBM_SPEC,) * (2 * nw), input_output_aliases={k: k for k in range(2 * nw)},
        compiler_params=pltpu.CompilerParams(has_side_effects=pltpu.SideEffectType.DATAFLOW_SIDE_EFFECTING),
    )(*p_thru, *land_thru, *sems, *after)
    return res[:nw], res[nw:]


def _where_am_i():
    x, y, c = _me()
    return jnp.stack([c, 2 * x + y]).astype(I32)


def _sibling():
    x, y, c = _me()
    return (x, y, 1 - c)


def _matmul_tn_pair(name, pos, a, b, m, n, k, shard_rows, *, tm, tn, tk):
    hr = shard_rows // 2
    tm, tn, tk = min(tm, hr), min(tn, n), min(tk, k)
    tph = hr // tm
    nt, nj, nk = (m // 2) // tm, n // tn, k // tk
    n_tiles = nt * nj

    def row_block(p, t, pos_ref):
        half = jnp.where(p == 0, 1 - pos_ref[0], pos_ref[0])
        return (t // tph) * (2 * tph) + half * tph + t % tph

    pieces = b if isinstance(b, list) else [(b, False)]
    starts = _piece_starts(pieces, tn)
    n_b = len(pieces)

    def kern(pos_ref, a_ref, *rest):
        b_refs = rest[:n_b]
        o_ref, acc_ref, send_buf, land_buf, s_sem, r_sem = rest[n_b:]
        p, t, j, kk = pl.program_id(0), pl.program_id(1), pl.program_id(2), pl.program_id(3)
        idx = t * nj + j
        sib = _sibling()

        def copy(i):
            return pltpu.make_async_remote_copy(
                src_ref=send_buf.at[i], dst_ref=land_buf.at[i], send_sem=s_sem.at[i],
                recv_sem=r_sem.at[i], device_id=sib, device_id_type=MESH)

        @pl.when(kk == 0)
        def _():
            acc_ref[...] = jnp.zeros_like(acc_ref)

        for q in range(n_b):
            @pl.when(jnp.logical_and(j >= starts[q], j < starts[q] + _piece_chunks(pieces[q], tn)))
            def _(q=q):
                acc_ref[...] += lax.dot_general(a_ref[...], b_refs[q][...], _TN, preferred_element_type=F32)

        @pl.when(jnp.logical_and(kk == nk - 1, p == 0))
        def _():
            send_buf[idx] = acc_ref[...].astype(BF16)
            copy(idx).start()

        @pl.when(jnp.logical_and(kk == nk - 1, p == 1))
        def _():
            copy(idx).wait_recv()
            o_ref[...] = (acc_ref[...] + land_buf[idx].astype(F32)).astype(BF16)

        @pl.when(jnp.logical_and(jnp.logical_and(p == 1, idx == n_tiles - 1), kk == nk - 1))
        def _():
            for i in range(n_tiles):
                copy(i).wait_send()

    grid_spec = pltpu.PrefetchScalarGridSpec(
        num_scalar_prefetch=1, grid=(2, nt, nj, nk),
        in_specs=[pl.BlockSpec((tk, tm), lambda p, t, j, kk, pos_ref: (kk, row_block(p, t, pos_ref)))]
        + [_piece_spec(pc, tk, tn, st, lambda p, t, j, kk, pos_ref: kk, lambda p, t, j, kk, pos_ref: j)
           for pc, st in zip(pieces, starts)],
        out_specs=pl.BlockSpec((tm, tn), lambda p, t, j, kk, pos_ref: (p * t, p * j)),
        scratch_shapes=[pltpu.VMEM((tm, tn), F32), pltpu.VMEM((n_tiles, tm, tn), BF16),
                        pltpu.VMEM((n_tiles, tm, tn), BF16),
                        pltpu.SemaphoreType.DMA((n_tiles,)), pltpu.SemaphoreType.DMA((n_tiles,))])
    return pl.pallas_call(
        kern, name=name, grid_spec=grid_spec, out_shape=jax.ShapeDtypeStruct((m // 2, n), BF16),
        compiler_params=_cparams(("arbitrary",) * 4),
    )(pos, a, *[pc[0] for pc in pieces])


def _rope_tables():
    half = RET_DK // 2
    f32 = np.float32
    inv = np.power(f32(ROPE_BASE), -np.arange(half, dtype=f32) / f32(half)).astype(f32)
    ang = (np.arange(SEQ, dtype=f32)[:, None] * inv[None, :]).astype(f32)
    return jnp.asarray(np.cos(ang).astype(f32)), jnp.asarray(np.sin(ang).astype(f32))


def _decay_tables():
    c = RET_CHUNK
    f32 = np.float32
    log_g = np.log1p(-np.power(f32(2.0), f32(-5.0) - np.arange(RET_HEADS, dtype=f32))).astype(f32)
    idx = np.arange(c, dtype=f32)
    rel = idx[:, None] - idx[None, :]
    din = np.where(rel >= 0, np.exp(log_g[:, None, None] * np.maximum(rel, f32(0.0))), f32(0.0)).astype(f32)
    qd = np.exp(log_g[:, None] * (idx + f32(1.0))).astype(f32)[:, :, None]
    kd = np.exp(log_g[:, None] * (f32(c) - f32(1.0) - idx)).astype(f32)[:, :, None]
    cd = np.exp(log_g * f32(c)).astype(f32)
    return jnp.asarray(din), jnp.asarray(qd), jnp.asarray(kd), jnp.asarray(cd)


def _t5_bucket(dist):
    max_exact = REL_BUCKETS // 2
    d_f = jnp.maximum(dist, 1).astype(F32)
    large = max_exact + (jnp.log(d_f / max_exact) / math.log(REL_MAX_DIST / max_exact)
                         * (REL_BUCKETS - max_exact)).astype(I32)
    large = jnp.minimum(large, REL_BUCKETS - 1)
    return jnp.where(dist < max_exact, dist, large)


def _bucket_tables():
    qi = jnp.arange(ATT_BLK)[:, None]
    kj = jnp.arange(2 * ATT_BLK)[None, :]
    dist = jnp.clip(ATT_BLK + qi - kj, 0, ATT_BLK)
    return jnp.stack([_t5_bucket(dist * dil) for _, dil in ATT_GROUPS]).astype(I32)


def _permute_rows(t, dil):
    if dil == 1:
        return t
    s, w = t.shape
    return t.reshape(s // dil, dil, w).transpose(1, 0, 2).reshape(s, w)


def _unpermute_rows(t, dil):
    if dil == 1:
        return t
    s, w = t.shape
    return t.reshape(dil, s // dil, w).transpose(1, 0, 2).reshape(s, w)


def _retention_fwd(rqk, rv, din, qd, kd, cd):
    nc = SEQ // RET_CHUNK
    c, dk, dv = RET_CHUNK, RET_DK, RET_DV

    def kern(q_ref, k_ref, v_ref, din_ref, qd_ref, kd_ref, cd_ref, o_ref, st_ref, state):
        n = pl.program_id(0)

        @pl.when(n == 0)
        def _():
            state[...] = jnp.zeros_like(state)

        for h in range(RET_HEADS):
            q, k = q_ref[:, h * dk:(h + 1) * dk], k_ref[:, h * dk:(h + 1) * dk]
            v = v_ref[:, h * dv:(h + 1) * dv]
            s_b = state[h].astype(BF16)
            st_ref[h] = s_b
            a = lax.dot_general(q, k, _NT, preferred_element_type=F32) * din_ref[h]
            o = jnp.dot(a.astype(BF16), v, preferred_element_type=F32)
            o += jnp.dot(q, s_b, preferred_element_type=F32) * qd_ref[h]
            o_ref[:, h * dv:(h + 1) * dv] = o
            kk = (k.astype(F32) * kd_ref[h]).astype(BF16)
            state[h] = state[h] * cd_ref[h] + lax.dot_general(kk, v, _TN, preferred_element_type=F32)

    whole = lambda a: pl.BlockSpec(a.shape, lambda n: (0,) * a.ndim)
    return pl.pallas_call(
        kern, name="retention_fwd", grid=(nc,),
        in_specs=[
            pl.BlockSpec((c, RET_QK_W), lambda n: (n, 0)),
            pl.BlockSpec((c, RET_QK_W), lambda n: (n, 1)),
            pl.BlockSpec((c, RET_V_W), lambda n: (n, 0)),
            whole(din), whole(qd), whole(kd),
            pl.BlockSpec(memory_space=pltpu.SMEM),
        ],
        out_specs=[
            pl.BlockSpec((c, RET_V_W), lambda n: (n, 0)),
            pl.BlockSpec((RET_HEADS, None, dk, dv), lambda n: (0, n, 0, 0)),
        ],
        out_shape=[
            jax.ShapeDtypeStruct((SEQ, RET_V_W), F32),
            jax.ShapeDtypeStruct((RET_HEADS, nc, dk, dv), BF16),
        ],
        scratch_shapes=[pltpu.VMEM((RET_HEADS, dk, dv), F32)],
        compiler_params=_cparams(("arbitrary",)),
    )(rqk, rqk, rv, din, qd, kd, cd)


def _retention_bwd(rqk, rv, states, d_ro, din, qd, kd, cd, cos, sin):
    nc = SEQ // RET_CHUNK
    c, dk, dv = RET_CHUNK, RET_DK, RET_DV
    half = dk // 2
    last = nc - 1

    def unrot(g, cs, sn):
        g1, g2 = g[:, :half], g[:, half:]
        return jnp.concatenate([g1 * cs + g2 * sn, g2 * cs - g1 * sn], axis=-1)

    def kern(q_ref, k_ref, v_ref, st_ref, do_ref, din_ref, qd_ref, kd_ref, cd_ref, cos_ref, sin_ref,
             out_ref, dstate):
        step = pl.program_id(0)

        @pl.when(step == 0)
        def _():
            dstate[...] = jnp.zeros_like(dstate)

        cs, sn = cos_ref[...], sin_ref[...]
        for h in range(RET_HEADS):
            qk_cols, v_cols = slice(h * dk, (h + 1) * dk), slice(h * dv, (h + 1) * dv)
            q, k, v, s_b = q_ref[:, qk_cols], k_ref[:, qk_cols], v_ref[:, v_cols], st_ref[h]
            d_o = do_ref[:, v_cols]
            d_ob = d_o.astype(BF16)
            d_oq = (d_o * qd_ref[h]).astype(BF16)
            ds_b = dstate[h].astype(BF16)
            din_m = din_ref[h]
            a_b = (lax.dot_general(q, k, _NT, preferred_element_type=F32) * din_m).astype(BF16)
            kk = (k.astype(F32) * kd_ref[h]).astype(BF16)
            d_v = lax.dot_general(a_b, d_ob, _TN, preferred_element_type=F32)
            d_v += jnp.dot(kk, ds_b, preferred_element_type=F32)
            d_a = (lax.dot_general(d_ob, v, _NT, preferred_element_type=F32) * din_m).astype(BF16)
            d_q = jnp.dot(d_a, k, preferred_element_type=F32)
            d_q += lax.dot_general(d_oq, s_b, _NT, preferred_element_type=F32)
            d_k = lax.dot_general(d_a, q, _TN, preferred_element_type=F32)
            d_k += lax.dot_general(v, ds_b, _NT, preferred_element_type=F32) * kd_ref[h]
            dstate[h] = dstate[h] * cd_ref[h] + lax.dot_general(q, d_oq, _TN, preferred_element_type=F32)
            out_ref[:, h * dk:(h + 1) * dk] = unrot(d_q, cs, sn).astype(BF16)
            out_ref[:, RET_QK_W + h * dk:RET_QK_W + (h + 1) * dk] = (
                unrot(d_k, cs, sn) * (RET_DK ** -0.5)).astype(BF16)
            out_ref[:, 2 * RET_QK_W + h * dv:2 * RET_QK_W + (h + 1) * dv] = d_v.astype(BF16)

    whole = lambda a: pl.BlockSpec(a.shape, lambda n: (0,) * a.ndim)
    return pl.pallas_call(
        kern, name="retention_bwd", grid=(nc,),
        in_specs=[
            pl.BlockSpec((c, RET_QK_W), lambda n: (last - n, 0)),
            pl.BlockSpec((c, RET_QK_W), lambda n: (last - n, 1)),
            pl.BlockSpec((c, RET_V_W), lambda n: (last - n, 0)),
            pl.BlockSpec((RET_HEADS, None, dk, dv), lambda n: (0, last - n, 0, 0)),
            pl.BlockSpec((c, RET_V_W), lambda n: (last - n, 0)),
            whole(din), whole(qd), whole(kd),
            pl.BlockSpec(memory_space=pltpu.SMEM),
            pl.BlockSpec((c, half), lambda n: (last - n, 0)),
            pl.BlockSpec((c, half), lambda n: (last - n, 0)),
        ],
        out_specs=pl.BlockSpec((c, 2 * RET_QK_W + RET_V_W), lambda n: (last - n, 0)),
        out_shape=jax.ShapeDtypeStruct((SEQ, 2 * RET_QK_W + RET_V_W), BF16),
        scratch_shapes=[pltpu.VMEM((RET_HEADS, dk, dv), F32)],
        compiler_params=_cparams(("arbitrary",)),
    )(rqk, rqk, rv, states, d_ro, din, qd, kd, cd, cos, sin)


def _bias_build(rel_bias, buckets):
    ng = len(ATT_GROUPS)

    def kern(tab_ref, bkt_ref, o_ref):
        g, h = pl.program_id(0), pl.program_id(1)
        bkt = bkt_ref[...]
        acc = jnp.zeros(bkt.shape, F32)
        for b in range(REL_BUCKETS):
            acc = jnp.where(bkt == b, tab_ref[b, g * ATT_HPG + h], acc)
        o_ref[...] = acc

    return pl.pallas_call(
        kern, name="bias_build", grid=(ng, ATT_HPG),
        in_specs=[pl.BlockSpec(memory_space=pltpu.SMEM),
                  pl.BlockSpec((None, ATT_BLK, 2 * ATT_BLK), lambda g, h: (g, 0, 0))],
        out_specs=pl.BlockSpec((None, None, ATT_BLK, 2 * ATT_BLK), lambda g, h: (g, h, 0, 0)),
        out_shape=jax.ShapeDtypeStruct((ng, ATT_HPG, ATT_BLK, 2 * ATT_BLK), F32),
        compiler_params=_cparams(("arbitrary", "arbitrary")),
    )(rel_bias, buckets)


def _bias_grad(dsb, buckets):
    ng = len(ATT_GROUPS)

    def kern(ds_ref, bkt_ref, o_ref):
        g, h = pl.program_id(0), pl.program_id(1)
        bkt, ds = bkt_ref[...], ds_ref[...]
        for b in range(REL_BUCKETS):
            o_ref[b, g * ATT_HPG + h] = jnp.sum(jnp.where(bkt == b, ds, 0.0))

    return pl.pallas_call(
        kern, name="bias_grad", grid=(ng, ATT_HPG),
        in_specs=[pl.BlockSpec((None, None, ATT_BLK, 2 * ATT_BLK), lambda g, h: (g, h, 0, 0)),
                  pl.BlockSpec((None, ATT_BLK, 2 * ATT_BLK), lambda g, h: (g, 0, 0))],
        out_specs=pl.BlockSpec(memory_space=pltpu.SMEM),
        out_shape=jax.ShapeDtypeStruct((REL_BUCKETS, N_ATT_HEADS), F32),
        compiler_params=_cparams(("arbitrary", "arbitrary")),
    )(dsb, buckets)


_NT = (((1,), (1,)), ((), ()))
_TN = (((0,), (0,)), ((), ()))
_ATT_SCALE = ATT_DH ** -0.5


_PAD_ROWS = SEQ + ATT_BLK


def _window_mask(has_prev):
    qi = lax.broadcasted_iota(I32, (ATT_BLK, 2 * ATT_BLK), 0)
    kj = lax.broadcasted_iota(I32, (ATT_BLK, 2 * ATT_BLK), 1)
    prev_ok = jnp.logical_and(jnp.logical_and(kj < ATT_BLK, kj >= qi), has_prev)
    return jnp.logical_or(prev_ok, jnp.logical_and(kj >= ATT_BLK, qi >= kj - ATT_BLK))


def _head_specs(col0):
    return pl.BlockSpec((SEQ, ATT_DH), lambda h: (0, col0 + h))


def _att_fwd(gi, qkv, bias, nb):
    blk, dh = ATT_BLK, ATT_DH

    def kern(q_ref, k_ref, v_ref, b_ref, o_ref, l_ref, kpad, vpad):
        zero = jnp.zeros((blk, dh), BF16)
        kpad[0:blk, :] = zero
        vpad[0:blk, :] = zero
        kpad[blk:, :] = k_ref[...]
        vpad[blk:, :] = v_ref[...]
        bias_m = b_ref[...]

        def body(b, carry):
            r0 = pl.multiple_of(b * blk, blk)
            q = q_ref[pl.ds(r0, blk), :]
            kw = kpad[pl.ds(r0, 2 * blk), :]
            vw = vpad[pl.ds(r0, 2 * blk), :]
            valid = _window_mask((b % nb) > 0)
            s = lax.dot_general(q, kw, _NT, preferred_element_type=F32) * _ATT_SCALE + bias_m
            s = jnp.where(valid, s, -1e30)
            mx = jnp.max(s, axis=-1, keepdims=True)
            e = jnp.exp(s - mx)
            den = jnp.sum(e, axis=-1, keepdims=True)
            o_ref[pl.ds(r0, blk), :] = jnp.dot((e / den).astype(BF16), vw, preferred_element_type=F32)
            l_ref[pl.ds(r0, blk), :] = jnp.broadcast_to(mx + jnp.log(den), (blk, dh))
            return carry

        lax.fori_loop(0, N_BLK, body, 0, unroll=2)

    return pl.pallas_call(
        kern, name=f"att_fwd_g{gi}", grid=(ATT_HPG,),
        in_specs=[_head_specs(0), _head_specs(ATT_HPG), _head_specs(2 * ATT_HPG),
                  pl.BlockSpec((None, None, blk, 2 * blk), lambda h: (gi, h, 0, 0))],
        out_specs=[_head_specs(0), _head_specs(0)],
        out_shape=[jax.ShapeDtypeStruct((SEQ, ATT_W), F32), jax.ShapeDtypeStruct((SEQ, ATT_W), F32)],
        scratch_shapes=[pltpu.VMEM((_PAD_ROWS, dh), BF16), pltpu.VMEM((_PAD_ROWS, dh), BF16)],
        compiler_params=_cparams(("arbitrary",)),
    )(qkv, qkv, qkv, bias)


def _att_bwd(gi, qkv, d_att, lse, dd, bias, nb):
    blk, dh = ATT_BLK, ATT_DH

    def kern(q_ref, k_ref, v_ref, do_ref, l_ref, d_ref, b_ref, dqkv_ref, dsb_ref,
             kpad, vpad, qpad, dopad, lpad, dpad):
        zero = jnp.zeros((blk, dh), BF16)
        zero_f = jnp.zeros((blk, dh), F32)
        kpad[0:blk, :] = zero
        vpad[0:blk, :] = zero
        kpad[blk:, :] = k_ref[...]
        vpad[blk:, :] = v_ref[...]
        qpad[SEQ:, :] = zero
        dopad[SEQ:, :] = zero
        lpad[SEQ:, :] = zero_f
        dpad[SEQ:, :] = zero_f
        qpad[0:SEQ, :] = q_ref[...]
        dopad[0:SEQ, :] = do_ref[...]
        lpad[0:SEQ, :] = l_ref[...]
        dpad[0:SEQ, :] = d_ref[...]
        bias_m = b_ref[...]
        bias_t = jnp.concatenate([bias_m[:, blk:], bias_m[:, :blk]], axis=0)
        dsb_ref[...] = jnp.zeros_like(dsb_ref)

        def dq_body(b, carry):
            r0 = pl.multiple_of(b * blk, blk)
            q, d_o = q_ref[pl.ds(r0, blk), :], do_ref[pl.ds(r0, blk), :]
            kw, vw = kpad[pl.ds(r0, 2 * blk), :], vpad[pl.ds(r0, 2 * blk), :]
            lrow, drow = l_ref[pl.ds(r0, blk), :][:, :1], d_ref[pl.ds(r0, blk), :][:, :1]
            valid = _window_mask((b % nb) > 0)
            s = lax.dot_general(q, kw, _NT, preferred_element_type=F32) * _ATT_SCALE + bias_m
            p = jnp.where(valid, jnp.exp(jnp.where(valid, s, -1e30) - lrow), 0.0)
            dp = lax.dot_general(d_o, vw, _NT, preferred_element_type=F32)
            ds = p * (dp - drow)
            dq = jnp.dot(ds.astype(BF16), kw, preferred_element_type=F32)
            dqkv_ref[0, pl.ds(r0, blk), :] = (dq * _ATT_SCALE).astype(BF16)
            dsb_ref[...] += ds
            return carry

        lax.fori_loop(0, N_BLK, dq_body, 0, unroll=2)

        qi = lax.broadcasted_iota(I32, (2 * blk, blk), 0)
        kj = lax.broadcasted_iota(I32, (2 * blk, blk), 1)

        def dkv_body(b, carry):
            r0 = pl.multiple_of(b * blk, blk)
            k, v = k_ref[pl.ds(r0, blk), :], v_ref[pl.ds(r0, blk), :]
            qw, dow = qpad[pl.ds(r0, 2 * blk), :], dopad[pl.ds(r0, 2 * blk), :]
            lrow, drow = lpad[pl.ds(r0, 2 * blk), :][:, :1], dpad[pl.ds(r0, 2 * blk), :][:, :1]
            has_next = jnp.logical_and(b + 1 < N_BLK, ((b + 1) % nb) > 0)
            next_ok = jnp.logical_and(jnp.logical_and(qi >= blk, kj >= qi - blk), has_next)
            valid = jnp.logical_or(jnp.logical_and(qi < blk, qi >= kj), next_ok)
            s = lax.dot_general(qw, k, _NT, preferred_element_type=F32) * _ATT_SCALE + bias_t
            p = jnp.where(valid, jnp.exp(jnp.where(valid, s, -1e30) - lrow), 0.0)
            dp = lax.dot_general(dow, v, _NT, preferred_element_type=F32)
            ds = p * (dp - drow)
            d_v = lax.dot_general(p.astype(BF16), dow, _TN, preferred_element_type=F32)
            d_k = lax.dot_general(ds.astype(BF16), qw, _TN, preferred_element_type=F32)
            dqkv_ref[1, pl.ds(r0, blk), :] = (d_k * _ATT_SCALE).astype(BF16)
            dqkv_ref[2, pl.ds(r0, blk), :] = d_v.astype(BF16)
            return carry

        lax.fori_loop(0, N_BLK, dkv_body, 0, unroll=2)

    return pl.pallas_call(
        kern, name=f"att_bwd_g{gi}", grid=(ATT_HPG,),
        in_specs=[_head_specs(0), _head_specs(ATT_HPG), _head_specs(2 * ATT_HPG),
                  _head_specs(0), _head_specs(0), _head_specs(0),
                  pl.BlockSpec((None, None, blk, 2 * blk), lambda h: (gi, h, 0, 0))],
        out_specs=[pl.BlockSpec((3, SEQ, dh), lambda h: (0, 0, h)),
                   pl.BlockSpec((None, blk, 2 * blk), lambda h: (h, 0, 0))],
        out_shape=[jax.ShapeDtypeStruct((3, SEQ, ATT_W), BF16),
                   jax.ShapeDtypeStruct((ATT_HPG, blk, 2 * blk), F32)],
        scratch_shapes=[pltpu.VMEM((_PAD_ROWS, dh), BF16)] * 4 + [pltpu.VMEM((_PAD_ROWS, dh), F32)] * 2,
        compiler_params=_cparams(("arbitrary",)),
    )(qkv, qkv, qkv, d_att, lse, dd, bias)


def _rms_parts(x):
    r = lax.rsqrt(jnp.mean(x * x, axis=-1, keepdims=True) + RMS_EPS)
    return x * r, r


def _rms_bwd(d_xhat, xhat, r):
    return r * (d_xhat - xhat * jnp.mean(d_xhat * xhat, axis=-1, keepdims=True))


def _prenorm_fwd(name, x, gain, shift, scale):
    def body(xt, g, sh, sc):
        xhat, _ = _rms_parts(xt)
        return (xhat * g) * (1.0 + sc) + sh
    return _rowmap(name, body, [x], [gain, shift, scale], [(D_MODEL, BF16)])[0]


def _prenorm_bwd(name, d_hs, x, gain, scale, resid, after=()):
    n_dh = len(d_hs)

    def body(*args):
        d_h = args[0]
        for t in args[1:n_dh]:
            d_h = d_h + t
        xt, res, g, sc = args[n_dh:]
        xhat, r = _rms_parts(xt)
        nrm = xhat * g
        d_n = d_h * (1.0 + sc)
        dx = _rms_bwd(d_n * g, xhat, r) + res
        return (dx, jnp.sum(d_h, axis=0, keepdims=True), jnp.sum(d_h * nrm, axis=0, keepdims=True),
                jnp.sum(d_n * xhat, axis=0, keepdims=True))

    return _rowmap(name, body, list(d_hs) + [x, resid], [gain, scale], [(D_MODEL, F32)],
                   [D_MODEL, D_MODEL, D_MODEL], after=after)


def _gn_parts(ro):
    mu = jnp.mean(ro, axis=-1, keepdims=True)
    cen = ro - mu
    rstd = lax.rsqrt(jnp.mean(cen * cen, axis=-1, keepdims=True) + GN_EPS)
    return cen * rstd, rstd


def _retpost_fwd(ro, rg, gn_g, gn_b):
    def body(rot, rgt, g, b):
        outs = []
        for h in range(RET_HEADS):
            sl = slice(h * RET_DV, (h + 1) * RET_DV)
            nrm, _ = _gn_parts(rot[:, sl])
            gate = rgt[:, sl]
            outs.append((gate * _sigmoid(gate)) * (nrm * g[:, sl] + b[:, sl]))
        return jnp.concatenate(outs, axis=-1)
    return _rowmap("retpost_fwd", body, [ro, rg], [gn_g, gn_b], [(RET_V_W, BF16)])[0]


def _retpost_bwd(d_gated, ro, rg, gn_g, gn_b):
    def body(dgt, rot, rgt, g, b):
        d_ro, d_rg, d_g, d_b = [], [], [], []
        for h in range(RET_HEADS):
            sl = slice(h * RET_DV, (h + 1) * RET_DV)
            nrm, rstd = _gn_parts(rot[:, sl])
            gate, dg = rgt[:, sl], dgt[:, sl]
            sg = _sigmoid(gate)
            ron = nrm * g[:, sl] + b[:, sl]
            d_rg.append(dg * ron * (sg * (1.0 + gate * (1.0 - sg))))
            d_ron = dg * (gate * sg)
            d_g.append(jnp.sum(d_ron * nrm, axis=0, keepdims=True))
            d_b.append(jnp.sum(d_ron, axis=0, keepdims=True))
            d_n = d_ron * g[:, sl]
            d_ro.append(rstd * (d_n - jnp.mean(d_n, axis=-1, keepdims=True)
                                - nrm * jnp.mean(d_n * nrm, axis=-1, keepdims=True)))
        cat = lambda ts: jnp.concatenate(ts, axis=-1)
        return cat(d_ro), cat(d_rg), cat(d_g), cat(d_b)
    return _rowmap("retpost_bwd", body, [d_gated, ro, rg], [gn_g, gn_b],
                   [(RET_V_W, F32), (RET_V_W, BF16)], [RET_V_W, RET_V_W])


def _combine(os_, ls_):
    def body(o0, o1, o2, l0, l1, l2):
        mx = jnp.maximum(jnp.maximum(l0, l1), l2)
        e0, e1, e2 = jnp.exp(l0 - mx), jnp.exp(l1 - mx), jnp.exp(l2 - mx)
        den = e0 + e1 + e2
        att = (e0 / den) * o0 + (e1 / den) * o1 + (e2 / den) * o2
        return att, att, mx + jnp.log(den)
    return _rowmap("att_combine", body, list(os_) + list(ls_), [],
                   [(ATT_W, F32), (ATT_W, BF16), (ATT_W, F32)])


def _att_bwd_pre(d_att, att):
    def body(dt, at):
        outs = []
        for h in range(ATT_HPG):
            sl = slice(h * ATT_DH, (h + 1) * ATT_DH)
            outs.append(jnp.broadcast_to(jnp.sum(dt[:, sl] * at[:, sl], axis=-1, keepdims=True),
                                         (dt.shape[0], ATT_DH)))
        return dt, jnp.concatenate(outs, axis=-1)
    return _rowmap("att_bwd_pre", body, [d_att, att], [], [(ATT_W, BF16), (ATT_W, F32)])


def _merge_fwd(gates, ret_out, att_out):
    def body(gt, ro, ao):
        return _sigmoid(gt[:, :D_MODEL]) * ro + _sigmoid(gt[:, D_MODEL:]) * ao
    return _rowmap("merge_fwd", body, [gates, ret_out, att_out], [], [(D_MODEL, BF16)])[0]


def _merge_bwd(d_merged, gates, ret_out, att_out):
    def body(dm, gt, ro, ao):
        sa, sb = _sigmoid(gt[:, :D_MODEL]), _sigmoid(gt[:, D_MODEL:])
        d_gates = jnp.concatenate([dm * ro * (sa * (1.0 - sa)), dm * ao * (sb * (1.0 - sb))], axis=-1)
        return dm * sa, dm * sb, d_gates
    return _rowmap("merge_bwd", body, [d_merged, gates, ret_out, att_out], [],
                   [(D_MODEL, BF16), (D_MODEL, BF16), (2 * D_MODEL, BF16)])


def _gate_bwd(name, d_x, branch, gate):
    def body(dx, br, g):
        return dx * g, jnp.sum(dx * br, axis=0, keepdims=True)
    return _rowmap(name, body, [d_x, branch], [gate], [(D_MODEL, BF16)], [D_MODEL])


def _loss_head(x3, target, gain):
    def body(xt, tt, g):
        xhat, r = _rms_parts(xt)
        err = xhat * g - tt
        d_y = err / D_MODEL
        loss = 0.5 * jnp.sum(jnp.mean(err * err, axis=-1, keepdims=True), axis=0, keepdims=True)
        d_x = _rms_bwd(d_y * g, xhat, r)
        return d_x, jnp.broadcast_to(loss, (1, 128)), jnp.sum(d_y * xhat, axis=0, keepdims=True)
    return _rowmap("loss_head", body, [x3, target], [gain], [(D_MODEL, F32)], [128, D_MODEL])


def _local_step(pos, x, target, mod, norm1_g, norm2_g, norm_f_g, rel_bias, gn_g, gn_b, w_in, rest_gather):
    sh1, sc1, g1, sh2, sc2, g2 = [mod[:, i * D_MODEL:(i + 1) * D_MODEL] for i in range(6)]
    cos, sin = _rope_tables()
    din, qd, kd, cd = _decay_tables()
    buckets = _bucket_tables()
    bias = _bias_build(rel_bias, buckets)
    dils = [d for _, d in ATT_GROUPS]
    nbs = [SEQ // d // ATT_BLK for d in dils]

    h1 = _prenorm_fwd("prenorm1_fwd", x, norm1_g, sh1, sc1)
    h1_p = [_permute_rows(h1, d) for d in dils]

    def rot_epi(acc, cs, sn, scale):
        half = RET_DK // 2
        x1, x2 = acc[:, :half], acc[:, half:]
        return (jnp.concatenate([x1 * cs - x2 * sn, x1 * sn + x2 * cs], axis=-1) * scale,)

    qk_scale = jnp.concatenate([jnp.ones((1, RET_QK_W), F32),
                                jnp.full((1, RET_QK_W), RET_DK ** -0.5, F32)], axis=-1)
    rope_ex = [(cos, (TM, RET_DK // 2), lambda i, j, kk: (i, 0)),
               (sin, (TM, RET_DK // 2), lambda i, j, kk: (i, 0)),
               (qk_scale, (1, RET_DK), lambda i, j, kk: (0, j))]
    rest_sems, rest_shards, rest_fulls, rest_token = rest_gather
    behind = [rest_token]
    rv = _matmul("proj_rv", h1, w_in, "nn", SEQ, RET_V_W, D_MODEL, [BF16], b_off=OFF_V, tk=D_MODEL,
                 after=behind)[0]
    rg = _matmul("proj_rg", h1, w_in, "nn", SEQ, RET_V_W, D_MODEL, [F32], b_off=OFF_G, tk=D_MODEL,
                 after=behind)[0]
    gates = _matmul("proj_gates", h1, w_in, "nn", SEQ, 2 * D_MODEL, D_MODEL, [F32], b_off=OFF_GATE,
                    tn=512, tk=D_MODEL, after=behind)[0]
    aqkv = [_matmul(f"proj_att_g{gi}", h1_p[gi], w_in, "nn", SEQ, 3 * ATT_W, D_MODEL, [BF16],
                    b_off=OFF_ATT + gi * 3 * ATT_W, tn=512, tk=D_MODEL, after=behind)[0]
            for gi in range(3)]

    os_, ls_ = [], []
    for gi in range(3):
        o_g, l_g = _att_fwd(gi, aqkv[gi], bias, nbs[gi])
        os_.append(_unpermute_rows(o_g, dils[gi]))
        ls_.append(_unpermute_rows(l_g, dils[gi]))
        if gi == 1:
            rest_sems, rest_fulls, fwd_token = _gather_rest_forward(rest_sems, rest_shards, rest_fulls,
                                                                    [o_g, rv, rg, gates])

    rqk = _matmul("proj_qk", h1, w_in, "nn", SEQ, 2 * RET_QK_W, D_MODEL, [BF16], b_off=OFF_Q,
                  tn=RET_DK, tk=D_MODEL, epilogue=rot_epi, extras=rope_ex, after=[fwd_token])[0]
    ro, states = _retention_fwd(rqk, rv, din, qd, kd, cd)
    gated = _retpost_fwd(ro, rg, gn_g, gn_b)
    w_ret_out, w_att_out, w_o, w_ff1, w_ff2 = _gather_rest_end(rest_sems, rest_fulls, [gated, os_[2]])
    ret_out = _matmul("ret_out", gated, w_ret_out, "nn", SEQ, D_MODEL, RET_V_W, [F32])[0]
    att, att_b, lse = _combine(os_, ls_)
    att_out = _matmul("att_out", att_b, w_att_out, "nn", SEQ, D_MODEL, ATT_W, [F32])[0]

    merged = _merge_fwd(gates, ret_out, att_out)

    def resid_epi(acc, xt, g):
        return xt + g * acc, acc

    def resid_ex(xin, g):
        return [(xin, (TM, TN), lambda i, j, kk: (i, j)), (g, (1, TN), lambda i, j, kk: (0, j))]

    x2, mix = _matmul("mix_out", merged, w_o, "nn", SEQ, D_MODEL, D_MODEL, [F32, F32],
                      epilogue=resid_epi, extras=resid_ex(x, g1))
    h2 = _prenorm_fwd("prenorm2_fwd", x2, norm2_g, sh2, sc2)

    def relu2_epi(acc):
        r = jnp.maximum(acc, 0.0)
        return r * r, acc

    act, u = _matmul("ff1", h2, w_ff1, "nn", SEQ, D_FF, D_MODEL, [BF16, F32], tk=D_MODEL,
                     epilogue=relu2_epi)
    x3, y2 = _matmul("ff2", act, w_ff2, "nn", SEQ, D_MODEL, D_FF, [F32, F32],
                     epilogue=resid_epi, extras=resid_ex(x2, g2))

    d_x3, loss, d_gf = _loss_head(x3, target, norm_f_g)

    d_y2, d_g2 = _gate_bwd("ff_gate_bwd", d_x3, y2, g2)

    def relu2_bwd_epi(acc, ut):
        return (acc * (2.0 * jnp.maximum(ut, 0.0)),)

    gw_ff2 = _matmul_tn_pair("ff2_dw", pos, act, d_y2, D_FF, D_MODEL, SEQ, D_FF // N_CHIPS,
                             tm=512, tn=1024, tk=1024)
    d_u = _matmul("ff2_dx", d_y2, w_ff2, "nt", SEQ, D_FF, D_MODEL, [BF16], epilogue=relu2_bwd_epi,
                  extras=[(u, (TM, TN), lambda i, j, kk: (i, j))])[0]
    gw_ff1 = _matmul_tn_pair("ff1_dw", pos, h2, d_u, D_MODEL, D_FF, SEQ, D_MODEL,
                             tm=512, tn=1024, tk=1024)
    ffn = ["w_ff2", "w_ff1"]
    ffn_started = _ici_start("ici_start_ffn", ffn, [gw_ff2, gw_ff1])
    d_h2 = _matmul("ff1_dx", d_u, w_ff1, "nt", SEQ, D_MODEL, D_FF, [F32], after=[ffn_started[3]])[0]
    d_x2, d_sh2, d_sc2, d_n2g = _prenorm_bwd("prenorm2_bwd", [d_h2], x2, norm2_g, sc2, d_x3)

    d_mix, d_g1 = _gate_bwd("mix_gate_bwd", d_x2, mix, g1)
    gw_o = _matmul_tn_pair("mix_dw", pos, merged, d_mix, D_MODEL, D_MODEL, SEQ, D_MODEL // N_CHIPS,
                           tm=128, tn=1024, tk=2048)
    d_merged = _matmul("mix_dx", d_mix, w_o, "nt", SEQ, D_MODEL, D_MODEL, [F32])[0]
    d_ret_out, d_att_out, d_gates = _merge_bwd(d_merged, gates, ret_out, att_out)

    gw_ret_out = _matmul_tn_pair("ret_out_dw", pos, gated, d_ret_out, RET_V_W, D_MODEL, SEQ,
                                 RET_V_W // N_CHIPS, tm=256, tn=1024, tk=1024)
    gw_att_out = _matmul_tn_pair("att_out_dw", pos, att_b, d_att_out, ATT_W, D_MODEL, SEQ, ATT_W,
                                 tm=256, tn=1024, tk=2048)
    mixer = ["w_o", "w_ret_out", "w_att_out"]
    mixer_started = _ici_start("ici_start_mixer", mixer, [gw_o, gw_ret_out, gw_att_out])
    d_gated = _matmul("ret_out_dx", d_ret_out, w_ret_out, "nt", SEQ, RET_V_W, D_MODEL, [F32],
                      after=[mixer_started[3]])[0]
    d_att = _matmul("att_out_dx", d_att_out, w_att_out, "nt", SEQ, ATT_W, D_MODEL, [F32],
                    after=[mixer_started[3]])[0]

    d_ro, d_rg, d_gn_g, d_gn_b = _retpost_bwd(d_gated, ro, rg, gn_g, gn_b)
    d_rqkv = _retention_bwd(rqk, rv, states, d_ro, din, qd, kd, cd, cos, sin)

    d_att_b, dd = _att_bwd_pre(d_att, att)
    d_aqkv, dsbs = [], []
    for gi in range(3):
        da_p = _permute_rows(d_att_b, dils[gi])
        l_p = _permute_rows(lse, dils[gi])
        dd_p = _permute_rows(dd, dils[gi])
        dqkv, dsb = _att_bwd(gi, aqkv[gi], da_p, l_p, dd_p, bias, nbs[gi])
        if dils[gi] > 1:
            dqkv = dqkv.reshape(3, dils[gi], SEQ // dils[gi], ATT_W).transpose(0, 2, 1, 3).reshape(
                3, SEQ, ATT_W)
        d_aqkv.append(dqkv)
        dsbs.append(dsb)
    d_rel_bias = _bias_grad(jnp.stack(dsbs), buckets)

    d_proj = [(d_rqkv, False), (d_rg, False)] + [(t, True) for t in d_aqkv] + [(d_gates, False)]
    gw_in = _matmul_tn_pair("proj_dw", pos, h1, d_proj, D_MODEL, IN_COLS, SEQ, D_MODEL,
                            tm=512, tn=ATT_W, tk=1024)
    sems, (gw_in,), (land,), token = _ici_start("ici_start_w_in", ["w_in"], [gw_in])
    d_h1 = _matmul("proj_dx", d_proj, w_in, "nt", SEQ, D_MODEL, IN_COLS, [F32], tn=1024, tk=ATT_W,
                   after=[token])[0]
    pending = (sems, land)

    names = ffn + mixer
    psums, got = _ici_wait("ici_wait_rest", names, list(ffn_started[0]) + list(mixer_started[0]),
                           list(ffn_started[1]) + list(mixer_started[1]),
                           list(ffn_started[2]) + list(mixer_started[2]), [d_h1])
    g_big = {n: _final_sum("final_" + n, pos, dict(BIG)[n], psums[i], got[i], SHARD[n])
             for i, n in enumerate(names)}
    grad_x, d_sh1, d_sc1, d_n1g = _prenorm_bwd("prenorm1_bwd", [d_h1], x, norm1_g, sc1, d_x2,
                                               after=list(g_big.values()))
    d_mod = jnp.concatenate([d_sh1, d_sc1, d_g1, d_sh2, d_sc2, d_g2], axis=-1)
    small = dict(norm1_g=d_n1g, norm2_g=d_n2g, norm_f_g=d_gf, gn_g=d_gn_g, gn_b=d_gn_b,
                 rel_bias=d_rel_bias)
    return loss, grad_x, d_mod, small, g_big, (gw_in,) + pending


def _me():
    return lax.axis_index("x"), lax.axis_index("y"), lax.axis_index("c")


def _peer(x, y, c, mask):
    return (x ^ ((mask >> 2) & 1), y ^ ((mask >> 1) & 1), c ^ (mask & 1))


def _gather8(src_ref, dst_ref, send_sems, recv_sems):
    x, y, c = _me()
    me = 4 * x + 2 * y + c
    copies = []
    for mask in range(1, N_DEV):
        cp = pltpu.make_async_remote_copy(
            src_ref=src_ref, dst_ref=dst_ref.at[me], send_sem=send_sems.at[mask - 1],
            recv_sem=recv_sems.at[mask - 1], device_id=_peer(x, y, c, mask), device_id_type=MESH)
        cp.start()
        copies.append(cp)
    dst_ref[me] = src_ref[...]
    for cp in copies:
        cp.wait_recv()
    for cp in copies:
        cp.wait_send()


def _ada_fwd(c_in, w_ada, b_ada):
    ncol = ADA_COLS // N_CHIPS

    def body(c_ref, w_ref, b_ref, mod_ref, sc_ref, cbuf, cg, mbuf, mg, s1, r1, s2, r2):
        x, y, c = _me()
        me = 4 * x + 2 * y + c
        cv = c_ref[...]
        cbuf[...] = jnp.broadcast_to(cv * _sigmoid(cv), cbuf.shape)
        _gather8(cbuf, cg, s1, r1)
        rows = lax.broadcasted_iota(I32, (N_DEV, D_MODEL), 0)
        sc_all = jnp.zeros((N_DEV, D_MODEL), F32)
        for d in range(N_DEV):
            sc_all = jnp.where(rows == d, cg[d], sc_all)
        sc_ref[...] = sc_all
        mbuf[...] = jnp.dot(sc_all.astype(BF16), w_ref[...].astype(BF16), preferred_element_type=F32)
        _gather8(mbuf, mg, s2, r2)
        rowsel = lax.broadcasted_iota(I32, (N_DEV, ncol), 0) == me
        for k in range(N_CHIPS):
            blk = mg[2 * k]
            row = jnp.sum(jnp.where(rowsel, blk, 0.0), axis=0, keepdims=True)
            mod_ref[:, k * ncol:(k + 1) * ncol] = row + b_ref[:, k * ncol:(k + 1) * ncol]

    vm = pl.BlockSpec(memory_space=pltpu.VMEM)
    return pl.pallas_call(
        body, name="ada_fwd",
        in_specs=[vm, vm, vm], out_specs=[vm, vm],
        out_shape=[jax.ShapeDtypeStruct((1, ADA_COLS), F32), jax.ShapeDtypeStruct((N_DEV, D_MODEL), F32)],
        scratch_shapes=[
            pltpu.VMEM((8, D_MODEL), F32), pltpu.VMEM((N_DEV, 8, D_MODEL), F32),
            pltpu.VMEM((8, ncol), F32), pltpu.VMEM((N_DEV, 8, ncol), F32),
            pltpu.SemaphoreType.DMA((N_DEV - 1,)), pltpu.SemaphoreType.DMA((N_DEV - 1,)),
            pltpu.SemaphoreType.DMA((N_DEV - 1,)), pltpu.SemaphoreType.DMA((N_DEV - 1,)),
        ],
        compiler_params=pltpu.CompilerParams(vmem_limit_bytes=VMEM_LIMIT_V7X),
    )(c_in, w_ada, b_ada)


def _small_reduce(pack, sc_all):
    ncol = ADA_COLS // N_CHIPS

    def body(p_ref, sc_ref, tot_ref, gw_ref, pg, s1, r1):
        x, y, _ = _me()
        chip = 2 * x + y
        _gather8(p_ref, pg, s1, r1)
        tot = pg[0]
        for d in range(1, N_DEV):
            tot = tot + pg[d]
        tot_ref[...] = tot
        rows = lax.broadcasted_iota(I32, (N_DEV, ncol), 0)
        dmod = jnp.zeros((N_DEV, ncol), F32)
        for k in range(N_CHIPS):
            part = jnp.zeros((N_DEV, ncol), F32)
            for d in range(N_DEV):
                part = jnp.where(rows == d, pg[d, :, k * ncol:(k + 1) * ncol][0:1, :], part)
            dmod = jnp.where(chip == k, part, dmod)
        gw_ref[...] = lax.dot_general(sc_ref[...].astype(BF16), dmod.astype(BF16), _TN,
                                      preferred_element_type=F32)

    vm = pl.BlockSpec(memory_space=pltpu.VMEM)
    return pl.pallas_call(
        body, name="small_reduce",
        in_specs=[vm, vm], out_specs=[vm, vm],
        out_shape=[jax.ShapeDtypeStruct((8, ADA_COLS), F32), jax.ShapeDtypeStruct((D_MODEL, ncol), F32)],
        scratch_shapes=[pltpu.VMEM((N_DEV, 8, ADA_COLS), F32),
                        pltpu.SemaphoreType.DMA((N_DEV - 1,)), pltpu.SemaphoreType.DMA((N_DEV - 1,))],
        compiler_params=pltpu.CompilerParams(vmem_limit_bytes=VMEM_LIMIT_V7X),
    )(pack, sc_all)


BIG = (("w_in", 1), ("w_ret_out", 0), ("w_att_out", 1), ("w_o", 0), ("w_ff1", 1), ("w_ff2", 0))
SHARD = {"w_in": (D_MODEL, IN_COLS // N_CHIPS), "w_ret_out": (RET_V_W // N_CHIPS, D_MODEL),
         "w_att_out": (ATT_W, D_MODEL // N_CHIPS), "w_o": (D_MODEL // N_CHIPS, D_MODEL),
         "w_ff1": (D_MODEL, D_FF // N_CHIPS), "w_ff2": (D_FF // N_CHIPS, D_MODEL)}
_CHIP_FLIPS = ((1, 0), (0, 1), (1, 1))


def _region(ref, axis, chip, half, shard_shape):
    r, cw = shard_shape
    hr = r // 2
    if axis == 1:
        return ref.at[pl.ds(half * hr, hr), pl.ds(chip * cw, cw)]
    return ref.at[pl.ds(chip * r + half * hr, hr), :]


def _gather_weights(shards, n_remote):
    nw = len(BIG)
    shapes = [s.shape for s in shards]
    full_shapes = [(r, N_CHIPS * cw) if ax == 1 else (N_CHIPS * r, cw)
                   for (r, cw), (_, ax) in zip(shapes, BIG)]

    def body(*refs):
        ins, outs = refs[:nw], refs[nw:2 * nw]
        own = refs[2 * nw:3 * nw]
        from_ici, from_sib = refs[3 * nw:3 * nw + n_remote], refs[3 * nw + n_remote:3 * nw + 2 * n_remote]
        ld_sem, st_sem, s_ici, r_ici, s_d2d, r_d2d, st_a, st_b = refs[3 * nw + 2 * n_remote:]
        x, y, c = _me()
        chip = 2 * x + y
        sib = (x, y, 1 - c)
        loads = [pltpu.make_async_copy(ins[i], own[i], ld_sem.at[i]) for i in range(nw)]
        for cp in loads:
            cp.start()
        pending, first = [], []
        for i, (_, ax) in enumerate(BIG):
            r, cw = shapes[i]
            hr = r // 2
            loads[i].wait()
            dst = outs[i].at[:, pl.ds(chip * cw, cw)] if ax == 1 else outs[i].at[pl.ds(chip * r, r), :]
            cp = pltpu.make_async_copy(own[i], dst, st_sem.at[i])
            cp.start()
            pending.append(cp)
            for j, (fx, fy) in enumerate(_CHIP_FLIPS if i < n_remote else ()):
                rc = pltpu.make_async_remote_copy(
                    src_ref=own[i].at[pl.ds(c * hr, hr), :], dst_ref=from_ici[i].at[j],
                    send_sem=s_ici.at[j * nw + i], recv_sem=r_ici.at[j * nw + i],
                    device_id=(x ^ fx, y ^ fy, c), device_id_type=MESH)
                rc.start()
                first.append((j, i, rc))
        passed = []
        for j, i, rc in first:
            fx, fy = _CHIP_FLIPS[j]
            src_chip = 2 * (x ^ fx) + (y ^ fy)
            ax = BIG[i][1]
            rc.wait_recv()
            fw = pltpu.make_async_remote_copy(
                src_ref=from_ici[i].at[j], dst_ref=from_sib[i].at[j], send_sem=s_d2d.at[j * nw + i],
                recv_sem=r_d2d.at[j * nw + i], device_id=sib, device_id_type=MESH)
            fw.start()
            passed.append((j, i, src_chip, fw))
            st = pltpu.make_async_copy(from_ici[i].at[j], _region(outs[i], ax, src_chip, c, shapes[i]),
                                       st_a.at[j * nw + i])
            st.start()
            pending.append(st)
        for j, i, src_chip, fw in passed:
            fw.wait_recv()
            st = pltpu.make_async_copy(from_sib[i].at[j],
                                       _region(outs[i], BIG[i][1], src_chip, 1 - c, shapes[i]),
                                       st_b.at[j * nw + i])
            st.start()
            pending.append(st)
        for _, _, rc in first:
            rc.wait_send()
        for _, _, _, fw in passed:
            fw.wait_send()
        for cp in pending:
            cp.wait()

    hbm = pl.BlockSpec(memory_space=pl.ANY)
    halves = [pltpu.VMEM((3, r // 2, cw), BF16) for r, cw in shapes[:n_remote]]
    return pl.pallas_call(
        body, name="gather_weights",
        in_specs=[hbm] * nw, out_specs=[hbm] * nw,
        out_shape=[jax.ShapeDtypeStruct(fs, BF16) for fs in full_shapes],
        scratch_shapes=[pltpu.VMEM(sh, BF16) for sh in shapes] + halves + halves
        + [pltpu.SemaphoreType.DMA((nw,)), pltpu.SemaphoreType.DMA((nw,))]
        + [pltpu.SemaphoreType.DMA((3 * nw,))] * 6,
        compiler_params=pltpu.CompilerParams(vmem_limit_bytes=VMEM_LIMIT_V7X),
    )(*shards)


REST = BIG[1:]
_SIDE_EFFECTS = pltpu.CompilerParams(has_side_effects=pltpu.SideEffectType.DATAFLOW_SIDE_EFFECTING)
_ANY_SPEC = pl.BlockSpec(memory_space=pl.ANY)


def _rest_ici_copies(shard_refs, full_refs, sems):
    x, y, c = _me()
    chip = 2 * x + y
    n = 3 * len(REST)
    copies = []
    for i, (name, ax) in enumerate(REST):
        hr = SHARD[name][0] // 2
        for j, (fx, fy) in enumerate(_CHIP_FLIPS):
            copies.append(pltpu.make_async_remote_copy(
                src_ref=shard_refs[i].at[pl.ds(c * hr, hr), :],
                dst_ref=_region(full_refs[i], ax, chip, c, SHARD[name]),
                send_sem=sems[3 * i + j], recv_sem=sems[n + 3 * i + j],
                device_id=(x ^ fx, y ^ fy, c), device_id_type=MESH))
    return copies


def _rest_d2d_copies(full_refs, sems):
    x, y, c = _me()
    n = 3 * len(REST)
    copies = []
    for i, (name, ax) in enumerate(REST):
        for j, (fx, fy) in enumerate(_CHIP_FLIPS):
            reg = _region(full_refs[i], ax, 2 * (x ^ fx) + (y ^ fy), c, SHARD[name])
            copies.append(pltpu.make_async_remote_copy(
                src_ref=reg, dst_ref=reg, send_sem=sems[3 * i + j], recv_sem=sems[n + 3 * i + j],
                device_id=(x, y, 1 - c), device_id_type=MESH))
    return copies


def _gather_rest_start(shards, fulls, after):
    nr, ns, na = len(REST), 6 * len(REST), len(after)

    def body(*refs):
        for cp in _rest_ici_copies(refs[:nr], refs[nr:2 * nr], refs[2 * nr + na:2 * nr + na + ns]):
            cp.start()
        token = refs[-1]
        token[...] = jnp.zeros_like(token)

    hbm = lambda a: pltpu.HBM(a.shape, a.dtype)
    res = pl.pallas_call(
        body, name="gather_rest_start",
        out_shape=(pltpu.SemaphoreType.DMA(()),) * ns + tuple(hbm(a) for a in shards + fulls)
        + (jax.ShapeDtypeStruct((8, 128), F32),),
        in_specs=(_HBM_SPEC,) * (2 * nr) + (_ANY_SPEC,) * na,
        out_specs=(_SEM_SPEC,) * ns + (_HBM_SPEC,) * (2 * nr) + (pl.BlockSpec(memory_space=pltpu.VMEM),),
        input_output_aliases={k: ns + k for k in range(2 * nr)}, compiler_params=_SIDE_EFFECTS,
    )(*[pltpu.with_memory_space_constraint(a, pltpu.HBM) for a in shards + fulls], *after)
    return res[:ns], res[ns:ns + nr], res[ns + nr:ns + 2 * nr], res[-1]


def _gather_rest_forward(sems, shards, fulls, after):
    nr, ns = len(REST), 6 * len(REST)

    def body(*refs):
        shard_refs, full_refs, old = refs[:nr], refs[nr:2 * nr], refs[2 * nr:2 * nr + ns]
        new = refs[2 * nr + ns + len(after):2 * nr + 2 * ns + len(after)]
        for cp in _rest_ici_copies(shard_refs, full_refs, old):
            cp.wait_send()
            cp.wait_recv()
        for cp in _rest_d2d_copies(full_refs, new):
            cp.start()
        token = refs[-1]
        token[...] = jnp.zeros_like(token)

    res = pl.pallas_call(
        body, name="gather_rest_forward",
        out_shape=(pltpu.SemaphoreType.DMA(()),) * ns + tuple(pltpu.HBM(a.shape, a.dtype) for a in fulls)
        + (jax.ShapeDtypeStruct((8, 128), F32),),
        in_specs=(_HBM_SPEC,) * (2 * nr) + (_SEM_SPEC,) * ns + (_ANY_SPEC,) * len(after),
        out_specs=(_SEM_SPEC,) * ns + (_HBM_SPEC,) * nr + (pl.BlockSpec(memory_space=pltpu.VMEM),),
        input_output_aliases={nr + k: ns + k for k in range(nr)}, compiler_params=_SIDE_EFFECTS,
    )(*shards, *fulls, *sems, *after)
    return res[:ns], res[ns:ns + nr], res[-1]


def _gather_rest_end(sems, fulls, after):
    nr, ns = len(REST), 6 * len(REST)

    def body(*refs):
        for cp in _rest_d2d_copies(refs[:nr], refs[nr:nr + ns]):
            cp.wait_send()
            cp.wait_recv()

    return pl.pallas_call(
        body, name="gather_rest_end",
        out_shape=tuple(pltpu.HBM(a.shape, a.dtype) for a in fulls),
        in_specs=(_HBM_SPEC,) * nr + (_SEM_SPEC,) * ns + (_ANY_SPEC,) * len(after),
        out_specs=(_HBM_SPEC,) * nr,
        input_output_aliases={k: k for k in range(nr)}, compiler_params=_SIDE_EFFECTS,
    )(*fulls, *sems, *after)


def _adam_update(w, g, m, v):
    mn = ADAM_B1 * m + (1.0 - ADAM_B1) * g
    vn = ADAM_B2 * v + (1.0 - ADAM_B2) * (g * g)
    m_hat = mn / (1.0 - ADAM_B1 ** ADAM_STEP)
    v_hat = vn / (1.0 - ADAM_B2 ** ADAM_STEP)
    return -ADAM_LR * (m_hat / (jnp.sqrt(v_hat) + ADAM_EPS) + ADAM_WD * w), mn, vn


def _final_sum(name, pos, axis, psum, recv, shard_shape, after=(), tr=128):
    r, cw = shard_shape
    hr = r // 2
    tr = min(tr, hr)
    nt = hr // tr
    n_after = len(after)

    def kern(pos_ref, p_ref, r_ref, *rest):
        g_ref, send_buf, land_buf, s_sem, r_sem = rest[n_after:]
        p, t = pl.program_id(0), pl.program_id(1)
        sib = _sibling()

        def copy(i):
            return pltpu.make_async_remote_copy(
                src_ref=send_buf.at[i], dst_ref=land_buf.at[i], send_sem=s_sem.at[i],
                recv_sem=r_sem.at[i], device_id=sib, device_id_type=MESH)

        @pl.when(p == 0)
        def _():
            tot = p_ref[...].astype(F32)
            for j in range(3):
                tot = tot + r_ref[j].astype(F32)
            send_buf[t] = tot
            copy(t).start()
            g_ref[...] = tot

        @pl.when(p == 1)
        def _():
            copy(t).wait_recv()
            g_ref[...] = land_buf[t]

        @pl.when(jnp.logical_and(p == 1, t == nt - 1))
        def _():
            for i in range(nt):
                copy(i).wait_send()

    def shard_rows(p, t, pos_ref):
        return (jnp.where(p == 0, pos_ref[0], 1 - pos_ref[0]) * nt + t, 0)

    def own_part(p, t, pos_ref):
        tt = jnp.where(p == 0, t, nt - 1)
        return (tt, pos_ref[1]) if axis == 1 else (pos_ref[1] * nt + tt, 0)

    grid_spec = pltpu.PrefetchScalarGridSpec(
        num_scalar_prefetch=1, grid=(2, nt),
        in_specs=[pl.BlockSpec((tr, cw), own_part),
                  pl.BlockSpec((3, tr, cw), lambda p, t, pos_ref: (0, jnp.where(p == 0, t, nt - 1), 0))]
        + [pl.BlockSpec(memory_space=pl.ANY)] * n_after,
        out_specs=pl.BlockSpec((tr, cw), shard_rows),
        scratch_shapes=[pltpu.VMEM((nt, tr, cw), F32), pltpu.VMEM((nt, tr, cw), F32),
                        pltpu.SemaphoreType.DMA((nt,)), pltpu.SemaphoreType.DMA((nt,))])
    return pl.pallas_call(
        kern, name=name, grid_spec=grid_spec, out_shape=jax.ShapeDtypeStruct((r, cw), F32),
        compiler_params=_cparams(("arbitrary", "arbitrary")),
    )(pos, psum, recv, *after)


def _adamw(name, w, g, m, v):
    r, cw = w.shape
    tr = min(r, 128)

    def kern(w_ref, g_ref, m_ref, v_ref, go_ref, d_ref, nm_ref, nv_ref):
        gv = g_ref[...]
        go_ref[...] = gv
        d_ref[...], nm_ref[...], nv_ref[...] = _adam_update(w_ref[...], gv, m_ref[...], v_ref[...])

    spec = pl.BlockSpec((tr, cw), lambda i: (i, 0))
    return pl.pallas_call(
        kern, name=name, grid=(r // tr,), in_specs=[spec] * 4, out_specs=[spec] * 4,
        out_shape=[jax.ShapeDtypeStruct((r, cw), F32)] * 4, compiler_params=_cparams(("parallel",)),
    )(w, g, m, v)


_PACK_W = ADA_COLS
_NB = REL_BUCKETS * N_ATT_HEADS
_SMALL_SLOTS = {
    "b_ada": (0, 0, ADA_COLS),
    "norm1_g": (1, 0, D_MODEL), "norm2_g": (1, D_MODEL, D_MODEL), "norm_f_g": (1, 2 * D_MODEL, D_MODEL),
    "ret_gn_g": (1, 3 * D_MODEL, RET_V_W),
    "ret_gn_b": (2, 0, RET_V_W), "rel_bias": (2, RET_V_W, _NB), "loss": (2, RET_V_W + 512, 128),
}


def _pack_small(vals):
    rows = []
    for r in range(8):
        items = sorted([(off, n) for n, (rr, off, _) in _SMALL_SLOTS.items() if rr == r and n in vals])
        parts, pos = [], 0
        for off, n in items:
            if off > pos:
                parts.append(jnp.zeros((1, off - pos), F32))
            parts.append(vals[n].reshape(1, -1).astype(F32))
            pos = off + _SMALL_SLOTS[n][2]
        if pos < _PACK_W:
            parts.append(jnp.zeros((1, _PACK_W - pos), F32))
        rows.append(jnp.concatenate(parts, axis=-1))
    return jnp.concatenate(rows, axis=0)


def _unpack_small(pack, name):
    r, off, wd = _SMALL_SLOTS[name]
    return pack[r:r + 1, off:off + wd]


def kernel(x, c, w_ada, b_ada, norm1_g, w_in, rel_bias, ret_gn_g, ret_gn_b, w_ret_out, w_att_out, w_o, norm2_g, w_ff1, w_ff2, norm_f_g, loss_target, m_w_ada, m_b_ada, m_norm1_g, m_w_in, m_rel_bias, m_ret_gn_g, m_ret_gn_b, m_w_ret_out, m_w_att_out, m_w_o, m_norm2_g, m_w_ff1, m_w_ff2, m_norm_f_g, v_w_ada, v_b_ada, v_norm1_g, v_w_in, v_rel_bias, v_ret_gn_g, v_ret_gn_b, v_w_ret_out, v_w_att_out, v_w_o, v_norm2_g, v_w_ff1, v_w_ff2, v_norm_f_g):
    given = dict(locals())
    big_names = [n for n, _ in BIG]
    shard_w = {n: given[n][0] for n in big_names}
    assert all(shard_w[n].shape == SHARD[n] for n in big_names)

    shards_bf = [shard_w[n].astype(BF16) for n in big_names]
    full = _gather_weights(shards_bf, 1)
    mod, sc_all = _ada_fwd(c, w_ada[0], b_ada)
    rest_gather = _gather_rest_start(shards_bf[1:], list(full[1:]), [mod])
    pos = _where_am_i()

    loss, grad_x, d_mod, small, g_big, pending = _local_step(
        pos, x[0], loss_target[0], mod, norm1_g, norm2_g, norm_f_g.reshape(1, -1), rel_bias, ret_gn_g,
        ret_gn_b, full[0], rest_gather)

    pack_g = _pack_small(dict(b_ada=d_mod, norm1_g=small["norm1_g"], norm2_g=small["norm2_g"],
                              norm_f_g=small["norm_f_g"], ret_gn_g=small["gn_g"], ret_gn_b=small["gn_b"],
                              rel_bias=small["rel_bias"], loss=loss))
    tot, g_w_ada = _small_reduce(pack_g, sc_all)

    small_names = ["b_ada", "norm1_g", "rel_bias", "ret_gn_g", "ret_gn_b", "norm2_g", "norm_f_g"]
    pack_w = _pack_small({n: given[n] for n in small_names})
    pack_m = _pack_small({n: given["m_" + n] for n in small_names})
    pack_v = _pack_small({n: given["v_" + n] for n in small_names})
    _, sd, sm, sv = _adamw("adamw_small", pack_w, tot, pack_m, pack_v)

    grads, deltas, new_m, new_v = {}, {}, {}, {}
    for n in small_names:
        shp = given[n].shape
        grads[n] = _unpack_small(tot, n).reshape(shp)
        deltas[n] = _unpack_small(sd, n).reshape(shp)
        new_m[n] = _unpack_small(sm, n).reshape(shp)
        new_v[n] = _unpack_small(sv, n).reshape(shp)
    g_big["w_ada"] = g_w_ada
    for n in ["w_ada"] + big_names[1:] + big_names[:1]:
        if n == "w_in":
            gw_in, sems, land = pending
            done = [tot, sd] + [deltas[k] for k in ["w_ada"] + big_names[1:]]
            (gw_in,), (got,) = _ici_wait("ici_wait_w_in", [n], sems, [gw_in], [land], done)
            g_big[n] = _final_sum("final_w_in", pos, 1, gw_in, got, SHARD[n])
        g, d, nm, nv = _adamw("adamw_" + n, given[n][0], g_big[n], given["m_" + n][0], given["v_" + n][0])
        grads[n], deltas[n], new_m[n], new_v[n] = g[None], d[None], nm[None], nv[None]

    order = ["w_ada", "b_ada", "norm1_g", "w_in", "rel_bias", "ret_gn_g", "ret_gn_b", "w_ret_out",
             "w_att_out", "w_o", "norm2_g", "w_ff1", "w_ff2", "norm_f_g"]
    loss_out = _unpack_small(tot, "loss")[0, 0]
    return (loss_out, grad_x[None], *[grads[n] for n in order], *[deltas[n] for n in order],
            *[new_m[n] for n in order], *[new_v[n] for n in order])
```

```python
import functools
import math

import jax
import jax.numpy as jnp
import numpy as np
from jax import lax
from jax.experimental import pallas as pl
from jax.experimental.pallas import tpu as pltpu

F32 = jnp.float32
BF16 = jnp.bfloat16
I32 = jnp.int32

SEQ = 2048
D_MODEL = 1024
RET_HEADS = 4
RET_DK = 256
RET_DV = 512
RET_CHUNK = 128
RET_QK_W = RET_HEADS * RET_DK
RET_V_W = RET_HEADS * RET_DV
ATT_GROUPS = ((128, 1), (512, 4), (2048, 16))
ATT_HPG = 4
ATT_DH = 128
ATT_W = ATT_HPG * ATT_DH
ATT_BLK = 128
N_BLK = SEQ // ATT_BLK
REL_BUCKETS = 32
REL_MAX_DIST = 2048
N_ATT_HEADS = 12
D_FF = 4 * D_MODEL
RMS_EPS = 1e-6
GN_EPS = 1e-5
ROPE_BASE = 10000.0
IN_COLS = 2 * RET_QK_W + 2 * RET_V_W + 9 * ATT_W + 2 * D_MODEL
OFF_Q, OFF_K, OFF_V, OFF_G = 0, RET_QK_W, 2 * RET_QK_W, 2 * RET_QK_W + RET_V_W
OFF_ATT = 2 * RET_QK_W + 2 * RET_V_W
OFF_GATE = OFF_ATT + 9 * ATT_W
N_CHIPS = 4
N_DEV = 8
ADA_COLS = 6 * D_MODEL

ADAM_LR = 0.001
ADAM_B1 = 0.9
ADAM_B2 = 0.999
ADAM_EPS = 1e-08
ADAM_WD = 0.01
ADAM_STEP = 10

VMEM_LIMIT_V7X = 56 * 1024 * 1024
MESH = pl.DeviceIdType.MESH


def _cparams(sem):
    return pltpu.CompilerParams(dimension_semantics=sem, vmem_limit_bytes=VMEM_LIMIT_V7X)


def _sigmoid(v):
    return 1.0 / (1.0 + jnp.exp(-v))


def _rowmap(name, body, row_ins, bcast_ins, row_outs, sum_outs=(), tm=256, after=()):
    m = row_ins[0].shape[0]
    n_in = len(row_ins) + len(bcast_ins)
    n_ro = len(row_outs)

    def kern(*refs):
        vals = [r[...] for r in refs[:n_in]]
        res = body(*vals)
        if not isinstance(res, (tuple, list)):
            res = (res,)
        outs = refs[n_in + len(after):]
        for r, v in zip(outs[:n_ro], res[:n_ro]):
            r[...] = v.astype(r.dtype)
        if sum_outs:
            @pl.when(pl.program_id(0) == 0)
            def _():
                for r in outs[n_ro:]:
                    r[...] = jnp.zeros_like(r)
            for r, v in zip(outs[n_ro:], res[n_ro:]):
                r[...] += v

    in_specs = [pl.BlockSpec((tm, a.shape[1]), lambda i: (i, 0)) for a in row_ins]
    in_specs += [pl.BlockSpec(a.shape, lambda i: (0, 0)) for a in bcast_ins]
    in_specs += [pl.BlockSpec(memory_space=pl.ANY)] * len(after)
    out_specs = [pl.BlockSpec((tm, n), lambda i: (i, 0)) for n, _ in row_outs]
    out_specs += [pl.BlockSpec((1, n), lambda i: (0, 0)) for n in sum_outs]
    out_shape = [jax.ShapeDtypeStruct((m, n), dt) for n, dt in row_outs]
    out_shape += [jax.ShapeDtypeStruct((1, n), F32) for n in sum_outs]
    return pl.pallas_call(
        kern, name=name, grid=(m // tm,), in_specs=in_specs, out_specs=out_specs,
        out_shape=out_shape, compiler_params=_cparams(("arbitrary",)),
    )(*row_ins, *bcast_ins, *after)


TM, TN = 1024, 1024


def _piece_chunks(piece, width):
    arr, stacked = piece
    return arr.shape[0] if stacked else arr.shape[1] // width


def _piece_spec(piece, rows, width, start, row_of, chunk_of):
    arr, stacked = piece
    last = _piece_chunks(piece, width) - 1

    def local(*ids):
        return jnp.clip(chunk_of(*ids) - start, 0, last)

    def row(*ids):
        rel = chunk_of(*ids) - start
        return jnp.where(jnp.logical_and(rel >= 0, rel <= last), row_of(*ids), 0)

    if stacked:
        return pl.BlockSpec((None, rows, width), lambda *ids: (local(*ids), row(*ids), 0))
    return pl.BlockSpec((rows, width), lambda *ids: (row(*ids), local(*ids)))


def _piece_starts(pieces, width):
    return [sum(_piece_chunks(p, width) for p in pieces[:q]) for q in range(len(pieces))]


def _matmul(name, a, b, kind, m, n, k, outs, *, b_off=0, tm=TM, tn=TN, tk=1024,
            epilogue=None, extras=(), after=()):
    tm, tn, tk = min(tm, m), min(tn, n), min(tk, k)
    nk = k // tk
    pieces = a if isinstance(a, list) else [(a, False)]
    starts = _piece_starts(pieces, tk)
    if kind == "nn":
        a_specs = [pl.BlockSpec((tm, tk), lambda i, j, kk: (i, kk))]
        b_spec = pl.BlockSpec((tk, tn), lambda i, j, kk: (kk, b_off // tn + j))
        dn = (((1,), (0,)), ((), ()))
    elif kind == "nt":
        a_specs = [_piece_spec(p, tm, tk, st, lambda i, j, kk: i, lambda i, j, kk: kk)
                   for p, st in zip(pieces, starts)]
        b_spec = pl.BlockSpec((tn, tk), lambda i, j, kk: (j, b_off // tk + kk))
        dn = (((1,), (1,)), ((), ()))
    else:
        a_specs = [pl.BlockSpec((tk, tm), lambda i, j, kk: (kk, i))]
        b_spec = pl.BlockSpec((tk, tn), lambda i, j, kk: (kk, j))
        dn = (((0,), (0,)), ((), ()))
    n_a, n_ex, n_out = len(pieces), len(extras), len(outs)
    if epilogue is None:
        epilogue = lambda acc: (acc,)

    def finish(acc, ex_refs, out_refs):
        res = epilogue(acc, *[r[...] for r in ex_refs])
        for r, v in zip(out_refs, res):
            r[...] = v.astype(r.dtype)

    n_in = n_a + 1 + n_ex + len(after)

    def kern(*refs):
        a_refs, b_ref = refs[:n_a], refs[n_a]
        ex_refs = refs[n_a + 1:n_a + 1 + n_ex]
        out_refs = refs[n_in:n_in + n_out]
        kk = pl.program_id(2)
        dot = lambda a_ref: lax.dot_general(a_ref[...], b_ref[...], dn, preferred_element_type=F32)
        if nk == 1:
            finish(dot(a_refs[0]), ex_refs, out_refs)
            return
        acc_ref = refs[n_in + n_out]
        if n_a == 1:
            part = dot(a_refs[0])

            @pl.when(kk == 0)
            def _():
                acc_ref[...] = part

            @pl.when(kk > 0)
            def _():
                acc_ref[...] += part
        else:
            @pl.when(kk == 0)
            def _():
                acc_ref[...] = jnp.zeros_like(acc_ref)

            for q in range(n_a):
                @pl.when(jnp.logical_and(kk >= starts[q], kk < starts[q] + _piece_chunks(pieces[q], tk)))
                def _(q=q):
                    acc_ref[...] += dot(a_refs[q])

        @pl.when(kk == nk - 1)
        def _():
            finish(acc_ref[...], ex_refs, out_refs)

    in_specs = a_specs + [b_spec] + [pl.BlockSpec(bs, im) for _, bs, im in extras]
    in_specs += [pl.BlockSpec(memory_space=pl.ANY)] * len(after)
    return pl.pallas_call(
        kern, name=name, grid=(m // tm, n // tn, nk), in_specs=in_specs,
        out_specs=[pl.BlockSpec((tm, tn), lambda i, j, kk: (i, j)) for _ in outs],
        out_shape=[jax.ShapeDtypeStruct((m, n), dt) for dt in outs],
        scratch_shapes=[] if nk == 1 else [pltpu.VMEM((tm, tn), F32)],
        compiler_params=_cparams(("parallel", "parallel", "arbitrary")),
    )(*[p[0] for p in pieces], b, *[e[0] for e in extras], *after)


def _ici_copies(psum_ref, recv_ref, s_sem, r_sem, axis, shard_shape):
    x, y, c = _me()
    hr, cw = shard_shape[0] // 2, shard_shape[1]
    pick = lambda sems, j: sems[j] if isinstance(sems, (list, tuple)) else sems.at[j]
    copies = []
    for j, (fx, fy) in enumerate(_CHIP_FLIPS):
        chip = 2 * (x ^ fx) + (y ^ fy)
        src = psum_ref.at[:, pl.ds(chip * cw, cw)] if axis == 1 else psum_ref.at[pl.ds(chip * hr, hr), :]
        copies.append(pltpu.make_async_remote_copy(
            src_ref=src, dst_ref=recv_ref.at[j], send_sem=pick(s_sem, j), recv_sem=pick(r_sem, j),
            device_id=(x ^ fx, y ^ fy, c), device_id_type=MESH))
    return copies


_HBM_SPEC = pl.BlockSpec(memory_space=pltpu.HBM)
_SEM_SPEC = pl.BlockSpec(memory_space=pltpu.SEMAPHORE)


def _split_ici_copies(names, p_refs, land_refs, sems):
    copies = []
    for i, n in enumerate(names):
        copies += _ici_copies(p_refs[i], land_refs[i], list(sems[6 * i:6 * i + 3]),
                              list(sems[6 * i + 3:6 * i + 6]), dict(BIG)[n], SHARD[n])
    return copies


def _ici_start(name, names, psums):
    nw, ns = len(names), 6 * len(names)
    lands = [lax.empty((3, SHARD[n][0] // 2, SHARD[n][1]), BF16) for n in names]

    def body(*refs):
        for cp in _split_ici_copies(names, refs[:nw], refs[nw:2 * nw], refs[2 * nw:2 * nw + ns]):
            cp.start()
        token = refs[-1]
        token[...] = jnp.zeros_like(token)

    res = pl.pallas_call(
        body, name=name,
        out_shape=(pltpu.SemaphoreType.DMA(()),) * ns
        + tuple(pltpu.HBM(a.shape, BF16) for a in list(psums) + lands)
        + (jax.ShapeDtypeStruct((8, 128), F32),),
        in_specs=(_HBM_SPEC,) * (2 * nw),
        out_specs=(_SEM_SPEC,) * ns + (_HBM_SPEC,) * (2 * nw) + (pl.BlockSpec(memory_space=pltpu.VMEM),),
        input_output_aliases={k: ns + k for k in range(2 * nw)},
        compiler_params=pltpu.CompilerParams(has_side_effects=pltpu.SideEffectType.DATAFLOW_SIDE_EFFECTING),
    )(*[pltpu.with_memory_space_constraint(a, pltpu.HBM) for a in list(psums) + lands])
    return res[:ns], res[ns:ns + nw], res[ns + nw:ns + 2 * nw], res[-1]


def _ici_wait(name, names, sems, p_thru, land_thru, after):
    nw, ns = len(names), 6 * len(names)

    def body(*refs):
        for cp in _split_ici_copies(names, refs[:nw], refs[nw:2 * nw], refs[2 * nw:2 * nw + ns]):
            cp.wait_send()
            cp.wait_recv()

    res = pl.pallas_call(
        body, name=name,
        out_shape=tuple(pltpu.HBM(a.shape, BF16) for a in list(p_thru) + list(land_thru)),
        in_specs=(_HBM_SPEC,) * (2 * nw) + (_SEM_SPEC,) * ns + (pl.BlockSpec(memory_space=pl.ANY),) * len(after),
        out_specs=(_HBM_SPEC,) * (2 * nw), input_output_aliases={k: k for k in range(2 * nw)},
        compiler_params=pltpu.CompilerParams(has_side_effects=pltpu.SideEffectType.DATAFLOW_SIDE_EFFECTING),
    )(*p_thru, *land_thru, *sems, *after)
    return res[:nw], res[nw:]


def _where_am_i():
    x, y, c = _me()
    return jnp.stack([c, 2 * x + y]).astype(I32)


def _sibling():
    x, y, c = _me()
    return (x, y, 1 - c)


def _matmul_tn_pair(name, pos, a, b, m, n, k, shard_rows, *, tm, tn, tk):
    hr = shard_rows // 2
    tm, tn, tk = min(tm, hr), min(tn, n), min(tk, k)
    tph = hr // tm
    nt, nj, nk = (m // 2) // tm, n // tn, k // tk
    n_tiles = nt * nj

    def row_block(p, t, pos_ref):
        half = jnp.where(p == 0, 1 - pos_ref[0], pos_ref[0])
        return (t // tph) * (2 * tph) + half * tph + t % tph

    pieces = b if isinstance(b, list) else [(b, False)]
    starts = _piece_starts(pieces, tn)
    n_b = len(pieces)

    def kern(pos_ref, a_ref, *rest):
        b_refs = rest[:n_b]
        o_ref, acc_ref, send_buf, land_buf, s_sem, r_sem = rest[n_b:]
        p, t, j, kk = pl.program_id(0), pl.program_id(1), pl.program_id(2), pl.program_id(3)
        idx = t * nj + j
        sib = _sibling()

        def copy(i):
            return pltpu.make_async_remote_copy(
                src_ref=send_buf.at[i], dst_ref=land_buf.at[i], send_sem=s_sem.at[i],
                recv_sem=r_sem.at[i], device_id=sib, device_id_type=MESH)

        @pl.when(kk == 0)
        def _():
            acc_ref[...] = jnp.zeros_like(acc_ref)

        for q in range(n_b):
            @pl.when(jnp.logical_and(j >= starts[q], j < starts[q] + _piece_chunks(pieces[q], tn)))
            def _(q=q):
                acc_ref[...] += lax.dot_general(a_ref[...], b_refs[q][...], _TN, preferred_element_type=F32)

        @pl.when(jnp.logical_and(kk == nk - 1, p == 0))
        def _():
            send_buf[idx] = acc_ref[...].astype(BF16)
            copy(idx).start()

        @pl.when(jnp.logical_and(kk == nk - 1, p == 1))
        def _():
            copy(idx).wait_recv()
            o_ref[...] = (acc_ref[...] + land_buf[idx].astype(F32)).astype(BF16)

        @pl.when(jnp.logical_and(jnp.logical_and(p == 1, idx == n_tiles - 1), kk == nk - 1))
        def _():
            for i in range(n_tiles):
                copy(i).wait_send()

    grid_spec = pltpu.PrefetchScalarGridSpec(
        num_scalar_prefetch=1, grid=(2, nt, nj, nk),
        in_specs=[pl.BlockSpec((tk, tm), lambda p, t, j, kk, pos_ref: (kk, row_block(p, t, pos_ref)))]
        + [_piece_spec(pc, tk, tn, st, lambda p, t, j, kk, pos_ref: kk, lambda p, t, j, kk, pos_ref: j)
           for pc, st in zip(pieces, starts)],
        out_specs=pl.BlockSpec((tm, tn), lambda p, t, j, kk, pos_ref: (p * t, p * j)),
        scratch_shapes=[pltpu.VMEM((tm, tn), F32), pltpu.VMEM((n_tiles, tm, tn), BF16),
                        pltpu.VMEM((n_tiles, tm, tn), BF16),
                        pltpu.SemaphoreType.DMA((n_tiles,)), pltpu.SemaphoreType.DMA((n_tiles,))])
    return pl.pallas_call(
        kern, name=name, grid_spec=grid_spec, out_shape=jax.ShapeDtypeStruct((m // 2, n), BF16),
        compiler_params=_cparams(("arbitrary",) * 4),
    )(pos, a, *[pc[0] for pc in pieces])


def _rope_tables():
    half = RET_DK // 2
    f32 = np.float32
    inv = np.power(f32(ROPE_BASE), -np.arange(half, dtype=f32) / f32(half)).astype(f32)
    ang = (np.arange(SEQ, dtype=f32)[:, None] * inv[None, :]).astype(f32)
    return jnp.asarray(np.cos(ang).astype(f32)), jnp.asarray(np.sin(ang).astype(f32))


def _decay_tables():
    c = RET_CHUNK
    f32 = np.float32
    log_g = np.log1p(-np.power(f32(2.0), f32(-5.0) - np.arange(RET_HEADS, dtype=f32))).astype(f32)
    idx = np.arange(c, dtype=f32)
    rel = idx[:, None] - idx[None, :]
    din = np.where(rel >= 0, np.exp(log_g[:, None, None] * np.maximum(rel, f32(0.0))), f32(0.0)).astype(f32)
    qd = np.exp(log_g[:, None] * (idx + f32(1.0))).astype(f32)[:, :, None]
    kd = np.exp(log_g[:, None] * (f32(c) - f32(1.0) - idx)).astype(f32)[:, :, None]
    cd = np.exp(log_g * f32(c)).astype(f32)
    return jnp.asarray(din), jnp.asarray(qd), jnp.asarray(kd), jnp.asarray(cd)


def _t5_bucket(dist):
    max_exact = REL_BUCKETS // 2
    d_f = jnp.maximum(dist, 1).astype(F32)
    large = max_exact + (jnp.log(d_f / max_exact) / math.log(REL_MAX_DIST / max_exact)
                         * (REL_BUCKETS - max_exact)).astype(I32)
    large = jnp.minimum(large, REL_BUCKETS - 1)
    return jnp.where(dist < max_exact, dist, large)


def _bucket_tables():
    qi = jnp.arange(ATT_BLK)[:, None]
    kj = jnp.arange(2 * ATT_BLK)[None, :]
    dist = jnp.clip(ATT_BLK + qi - kj, 0, ATT_BLK)
    return jnp.stack([_t5_bucket(dist * dil) for _, dil in ATT_GROUPS]).astype(I32)


def _permute_rows(t, dil):
    if dil == 1:
        return t
    s, w = t.shape
    return t.reshape(s // dil, dil, w).transpose(1, 0, 2).reshape(s, w)


def _unpermute_rows(t, dil):
    if dil == 1:
        return t
    s, w = t.shape
    return t.reshape(dil, s // dil, w).transpose(1, 0, 2).reshape(s, w)


def _retention_fwd(rqk, rv, din, qd, kd, cd):
    nc = SEQ // RET_CHUNK
    c, dk, dv = RET_CHUNK, RET_DK, RET_DV

    def kern(q_ref, k_ref, v_ref, din_ref, qd_ref, kd_ref, cd_ref, o_ref, st_ref, state):
        n = pl.program_id(0)

        @pl.when(n == 0)
        def _():
            state[...] = jnp.zeros_like(state)

        for h in range(RET_HEADS):
            q, k = q_ref[:, h * dk:(h + 1) * dk], k_ref[:, h * dk:(h + 1) * dk]
            v = v_ref[:, h * dv:(h + 1) * dv]
            s_b = state[h].astype(BF16)
            st_ref[h] = s_b
            a = lax.dot_general(q, k, _NT, preferred_element_type=F32) * din_ref[h]
            o = jnp.dot(a.astype(BF16), v, preferred_element_type=F32)
            o += jnp.dot(q, s_b, preferred_element_type=F32) * qd_ref[h]
            o_ref[:, h * dv:(h + 1) * dv] = o
            kk = (k.astype(F32) * kd_ref[h]).astype(BF16)
            state[h] = state[h] * cd_ref[h] + lax.dot_general(kk, v, _TN, preferred_element_type=F32)

    whole = lambda a: pl.BlockSpec(a.shape, lambda n: (0,) * a.ndim)
    return pl.pallas_call(
        kern, name="retention_fwd", grid=(nc,),
        in_specs=[
            pl.BlockSpec((c, RET_QK_W), lambda n: (n, 0)),
            pl.BlockSpec((c, RET_QK_W), lambda n: (n, 1)),
            pl.BlockSpec((c, RET_V_W), lambda n: (n, 0)),
            whole(din), whole(qd), whole(kd),
            pl.BlockSpec(memory_space=pltpu.SMEM),
        ],
        out_specs=[
            pl.BlockSpec((c, RET_V_W), lambda n: (n, 0)),
            pl.BlockSpec((RET_HEADS, None, dk, dv), lambda n: (0, n, 0, 0)),
        ],
        out_shape=[
            jax.ShapeDtypeStruct((SEQ, RET_V_W), F32),
            jax.ShapeDtypeStruct((RET_HEADS, nc, dk, dv), BF16),
        ],
        scratch_shapes=[pltpu.VMEM((RET_HEADS, dk, dv), F32)],
        compiler_params=_cparams(("arbitrary",)),
    )(rqk, rqk, rv, din, qd, kd, cd)


def _retention_bwd(rqk, rv, states, d_ro, din, qd, kd, cd, cos, sin):
    nc = SEQ // RET_CHUNK
    c, dk, dv = RET_CHUNK, RET_DK, RET_DV
    half = dk // 2
    last = nc - 1

    def unrot(g, cs, sn):
        g1, g2 = g[:, :half], g[:, half:]
        return jnp.concatenate([g1 * cs + g2 * sn, g2 * cs - g1 * sn], axis=-1)

    def kern(q_ref, k_ref, v_ref, st_ref, do_ref, din_ref, qd_ref, kd_ref, cd_ref, cos_ref, sin_ref,
             out_ref, dstate):
        step = pl.program_id(0)

        @pl.when(step == 0)
        def _():
            dstate[...] = jnp.zeros_like(dstate)

        cs, sn = cos_ref[...], sin_ref[...]
        for h in range(RET_HEADS):
            qk_cols, v_cols = slice(h * dk, (h + 1) * dk), slice(h * dv, (h + 1) * dv)
            q, k, v, s_b = q_ref[:, qk_cols], k_ref[:, qk_cols], v_ref[:, v_cols], st_ref[h]
            d_o = do_ref[:, v_cols]
            d_ob = d_o.astype(BF16)
            d_oq = (d_o * qd_ref[h]).astype(BF16)
            ds_b = dstate[h].astype(BF16)
            din_m = din_ref[h]
            a_b = (lax.dot_general(q, k, _NT, preferred_element_type=F32) * din_m).astype(BF16)
            kk = (k.astype(F32) * kd_ref[h]).astype(BF16)
            d_v = lax.dot_general(a_b, d_ob, _TN, preferred_element_type=F32)
            d_v += jnp.dot(kk, ds_b, preferred_element_type=F32)
            d_a = (lax.dot_general(d_ob, v, _NT, preferred_element_type=F32) * din_m).astype(BF16)
            d_q = jnp.dot(d_a, k, preferred_element_type=F32)
            d_q += lax.dot_general(d_oq, s_b, _NT, preferred_element_type=F32)
            d_k = lax.dot_general(d_a, q, _TN, preferred_element_type=F32)
            d_k += lax.dot_general(v, ds_b, _NT, preferred_element_type=F32) * kd_ref[h]
            dstate[h] = dstate[h] * cd_ref[h] + lax.dot_general(q, d_oq, _TN, preferred_element_type=F32)
            out_ref[:, h * dk:(h + 1) * dk] = unrot(d_q, cs, sn).astype(BF16)
            out_ref[:, RET_QK_W + h * dk:RET_QK_W + (h + 1) * dk] = (
                unrot(d_k, cs, sn) * (RET_DK ** -0.5)).astype(BF16)
            out_ref[:, 2 * RET_QK_W + h * dv:2 * RET_QK_W + (h + 1) * dv] = d_v.astype(BF16)

    whole = lambda a: pl.BlockSpec(a.shape, lambda n: (0,) * a.ndim)
    return pl.pallas_call(
        kern, name="retention_bwd", grid=(nc,),
        in_specs=[
            pl.BlockSpec((c, RET_QK_W), lambda n: (last - n, 0)),
            pl.BlockSpec((c, RET_QK_W), lambda n: (last - n, 1)),
            pl.BlockSpec((c, RET_V_W), lambda n: (last - n, 0)),
            pl.BlockSpec((RET_HEADS, None, dk, dv), lambda n: (0, last - n, 0, 0)),
            pl.BlockSpec((c, RET_V_W), lambda n: (last - n, 0)),
            whole(din), whole(qd), whole(kd),
            pl.BlockSpec(memory_space=pltpu.SMEM),
            pl.BlockSpec((c, half), lambda n: (last - n, 0)),
            pl.BlockSpec((c, half), lambda n: (last - n, 0)),
        ],
        out_specs=pl.BlockSpec((c, 2 * RET_QK_W + RET_V_W), lambda n: (last - n, 0)),
        out_shape=jax.ShapeDtypeStruct((SEQ, 2 * RET_QK_W + RET_V_W), BF16),
        scratch_shapes=[pltpu.VMEM((RET_HEADS, dk, dv), F32)],
        compiler_params=_cparams(("arbitrary",)),
    )(rqk, rqk, rv, states, d_ro, din, qd, kd, cd, cos, sin)


def _bias_build(rel_bias, buckets):
    ng = len(ATT_GROUPS)

    def kern(tab_ref, bkt_ref, o_ref):
        g, h = pl.program_id(0), pl.program_id(1)
        bkt = bkt_ref[...]
        acc = jnp.zeros(bkt.shape, F32)
        for b in range(REL_BUCKETS):
            acc = jnp.where(bkt == b, tab_ref[b, g * ATT_HPG + h], acc)
        o_ref[...] = acc

    return pl.pallas_call(
        kern, name="bias_build", grid=(ng, ATT_HPG),
        in_specs=[pl.BlockSpec(memory_space=pltpu.SMEM),
                  pl.BlockSpec((None, ATT_BLK, 2 * ATT_BLK), lambda g, h: (g, 0, 0))],
        out_specs=pl.BlockSpec((None, None, ATT_BLK, 2 * ATT_BLK), lambda g, h: (g, h, 0, 0)),
        out_shape=jax.ShapeDtypeStruct((ng, ATT_HPG, ATT_BLK, 2 * ATT_BLK), F32),
        compiler_params=_cparams(("arbitrary", "arbitrary")),
    )(rel_bias, buckets)


def _bias_grad(dsb, buckets):
    ng = len(ATT_GROUPS)

    def kern(ds_ref, bkt_ref, o_ref):
        g, h = pl.program_id(0), pl.program_id(1)
        bkt, ds = bkt_ref[...], ds_ref[...]
        for b in range(REL_BUCKETS):
            o_ref[b, g * ATT_HPG + h] = jnp.sum(jnp.where(bkt == b, ds, 0.0))

    return pl.pallas_call(
        kern, name="bias_grad", grid=(ng, ATT_HPG),
        in_specs=[pl.BlockSpec((None, None, ATT_BLK, 2 * ATT_BLK), lambda g, h: (g, h, 0, 0)),
                  pl.BlockSpec((None, ATT_BLK, 2 * ATT_BLK), lambda g, h: (g, 0, 0))],
        out_specs=pl.BlockSpec(memory_space=pltpu.SMEM),
        out_shape=jax.ShapeDtypeStruct((REL_BUCKETS, N_ATT_HEADS), F32),
        compiler_params=_cparams(("arbitrary", "arbitrary")),
    )(dsb, buckets)


_NT = (((1,), (1,)), ((), ()))
_TN = (((0,), (0,)), ((), ()))
_ATT_SCALE = ATT_DH ** -0.5


_PAD_ROWS = SEQ + ATT_BLK


def _window_mask(has_prev):
    qi = lax.broadcasted_iota(I32, (ATT_BLK, 2 * ATT_BLK), 0)
    kj = lax.broadcasted_iota(I32, (ATT_BLK, 2 * ATT_BLK), 1)
    prev_ok = jnp.logical_and(jnp.logical_and(kj < ATT_BLK, kj >= qi), has_prev)
    return jnp.logical_or(prev_ok, jnp.logical_and(kj >= ATT_BLK, qi >= kj - ATT_BLK))


def _head_specs(col0):
    return pl.BlockSpec((SEQ, ATT_DH), lambda h: (0, col0 + h))


def _att_fwd(gi, qkv, bias, nb):
    blk, dh = ATT_BLK, ATT_DH

    def kern(q_ref, k_ref, v_ref, b_ref, o_ref, l_ref, kpad, vpad):
        zero = jnp.zeros((blk, dh), BF16)
        kpad[0:blk, :] = zero
        vpad[0:blk, :] = zero
        kpad[blk:, :] = k_ref[...]
        vpad[blk:, :] = v_ref[...]
        bias_m = b_ref[...]

        def body(b, carry):
            r0 = pl.multiple_of(b * blk, blk)
            q = q_ref[pl.ds(r0, blk), :]
            kw = kpad[pl.ds(r0, 2 * blk), :]
            vw = vpad[pl.ds(r0, 2 * blk), :]
            valid = _window_mask((b % nb) > 0)
            s = lax.dot_general(q, kw, _NT, preferred_element_type=F32) * _ATT_SCALE + bias_m
            s = jnp.where(valid, s, -1e30)
            mx = jnp.max(s, axis=-1, keepdims=True)
            e = jnp.exp(s - mx)
            den = jnp.sum(e, axis=-1, keepdims=True)
            o_ref[pl.ds(r0, blk), :] = jnp.dot((e / den).astype(BF16), vw, preferred_element_type=F32)
            l_ref[pl.ds(r0, blk), :] = jnp.broadcast_to(mx + jnp.log(den), (blk, dh))
            return carry

        lax.fori_loop(0, N_BLK, body, 0, unroll=2)

    return pl.pallas_call(
        kern, name=f"att_fwd_g{gi}", grid=(ATT_HPG,),
        in_specs=[_head_specs(0), _head_specs(ATT_HPG), _head_specs(2 * ATT_HPG),
                  pl.BlockSpec((None, None, blk, 2 * blk), lambda h: (gi, h, 0, 0))],
        out_specs=[_head_specs(0), _head_specs(0)],
        out_shape=[jax.ShapeDtypeStruct((SEQ, ATT_W), F32), jax.ShapeDtypeStruct((SEQ, ATT_W), F32)],
        scratch_shapes=[pltpu.VMEM((_PAD_ROWS, dh), BF16), pltpu.VMEM((_PAD_ROWS, dh), BF16)],
        compiler_params=_cparams(("arbitrary",)),
    )(qkv, qkv, qkv, bias)


def _att_bwd(gi, qkv, d_att, lse, dd, bias, nb):
    blk, dh = ATT_BLK, ATT_DH

    def kern(q_ref, k_ref, v_ref, do_ref, l_ref, d_ref, b_ref, dqkv_ref, dsb_ref,
             kpad, vpad, qpad, dopad, lpad, dpad):
        zero = jnp.zeros((blk, dh), BF16)
        zero_f = jnp.zeros((blk, dh), F32)
        kpad[0:blk, :] = zero
        vpad[0:blk, :] = zero
        kpad[blk:, :] = k_ref[...]
        vpad[blk:, :] = v_ref[...]
        qpad[SEQ:, :] = zero
        dopad[SEQ:, :] = zero
        lpad[SEQ:, :] = zero_f
        dpad[SEQ:, :] = zero_f
        qpad[0:SEQ, :] = q_ref[...]
        dopad[0:SEQ, :] = do_ref[...]
        lpad[0:SEQ, :] = l_ref[...]
        dpad[0:SEQ, :] = d_ref[...]
        bias_m = b_ref[...]
        bias_t = jnp.concatenate([bias_m[:, blk:], bias_m[:, :blk]], axis=0)
        dsb_ref[...] = jnp.zeros_like(dsb_ref)

        def dq_body(b, carry):
            r0 = pl.multiple_of(b * blk, blk)
            q, d_o = q_ref[pl.ds(r0, blk), :], do_ref[pl.ds(r0, blk), :]
            kw, vw = kpad[pl.ds(r0, 2 * blk), :], vpad[pl.ds(r0, 2 * blk), :]
            lrow, drow = l_ref[pl.ds(r0, blk), :][:, :1], d_ref[pl.ds(r0, blk), :][:, :1]
            valid = _window_mask((b % nb) > 0)
            s = lax.dot_general(q, kw, _NT, preferred_element_type=F32) * _ATT_SCALE + bias_m
            p = jnp.where(valid, jnp.exp(jnp.where(valid, s, -1e30) - lrow), 0.0)
            dp = lax.dot_general(d_o, vw, _NT, preferred_element_type=F32)
            ds = p * (dp - drow)
            dq = jnp.dot(ds.astype(BF16), kw, preferred_element_type=F32)
            dqkv_ref[0, pl.ds(r0, blk), :] = (dq * _ATT_SCALE).astype(BF16)
            dsb_ref[...] += ds
            return carry

        lax.fori_loop(0, N_BLK, dq_body, 0, unroll=2)

        qi = lax.broadcasted_iota(I32, (2 * blk, blk), 0)
        kj = lax.broadcasted_iota(I32, (2 * blk, blk), 1)

        def dkv_body(b, carry):
            r0 = pl.multiple_of(b * blk, blk)
            k, v = k_ref[pl.ds(r0, blk), :], v_ref[pl.ds(r0, blk), :]
            qw, dow = qpad[pl.ds(r0, 2 * blk), :], dopad[pl.ds(r0, 2 * blk), :]
            lrow, drow = lpad[pl.ds(r0, 2 * blk), :][:, :1], dpad[pl.ds(r0, 2 * blk), :][:, :1]
            has_next = jnp.logical_and(b + 1 < N_BLK, ((b + 1) % nb) > 0)
            next_ok = jnp.logical_and(jnp.logical_and(qi >= blk, kj >= qi - blk), has_next)
            valid = jnp.logical_or(jnp.logical_and(qi < blk, qi >= kj), next_ok)
            s = lax.dot_general(qw, k, _NT, preferred_element_type=F32) * _ATT_SCALE + bias_t
            p = jnp.where(valid, jnp.exp(jnp.where(valid, s, -1e30) - lrow), 0.0)
            dp = lax.dot_general(dow, v, _NT, preferred_element_type=F32)
            ds = p * (dp - drow)
            d_v = lax.dot_general(p.astype(BF16), dow, _TN, preferred_element_type=F32)
            d_k = lax.dot_general(ds.astype(BF16), qw, _TN, preferred_element_type=F32)
            dqkv_ref[1, pl.ds(r0, blk), :] = (d_k * _ATT_SCALE).astype(BF16)
            dqkv_ref[2, pl.ds(r0, blk), :] = d_v.astype(BF16)
            return carry

        lax.fori_loop(0, N_BLK, dkv_body, 0, unroll=2)

    return pl.pallas_call(
        kern, name=f"att_bwd_g{gi}", grid=(ATT_HPG,),
        in_specs=[_head_specs(0), _head_specs(ATT_HPG), _head_specs(2 * ATT_HPG),
                  _head_specs(0), _head_specs(0), _head_specs(0),
                  pl.BlockSpec((None, None, blk, 2 * blk), lambda h: (gi, h, 0, 0))],
        out_specs=[pl.BlockSpec((3, SEQ, dh), lambda h: (0, 0, h)),
                   pl.BlockSpec((None, blk, 2 * blk), lambda h: (h, 0, 0))],
        out_shape=[jax.ShapeDtypeStruct((3, SEQ, ATT_W), BF16),
                   jax.ShapeDtypeStruct((ATT_HPG, blk, 2 * blk), F32)],
        scratch_shapes=[pltpu.VMEM((_PAD_ROWS, dh), BF16)] * 4 + [pltpu.VMEM((_PAD_ROWS, dh), F32)] * 2,
        compiler_params=_cparams(("arbitrary",)),
    )(qkv, qkv, qkv, d_att, lse, dd, bias)


def _rms_parts(x):
    r = lax.rsqrt(jnp.mean(x * x, axis=-1, keepdims=True) + RMS_EPS)
    return x * r, r


def _rms_bwd(d_xhat, xhat, r):
    return r * (d_xhat - xhat * jnp.mean(d_xhat * xhat, axis=-1, keepdims=True))


def _prenorm_fwd(name, x, gain, shift, scale):
    def body(xt, g, sh, sc):
        xhat, _ = _rms_parts(xt)
        return (xhat * g) * (1.0 + sc) + sh
    return _rowmap(name, body, [x], [gain, shift, scale], [(D_MODEL, BF16)])[0]


def _prenorm_bwd(name, d_hs, x, gain, scale, resid, after=()):
    n_dh = len(d_hs)

    def body(*args):
        d_h = args[0]
        for t in args[1:n_dh]:
            d_h = d_h + t
        xt, res, g, sc = args[n_dh:]
        xhat, r = _rms_parts(xt)
        nrm = xhat * g
        d_n = d_h * (1.0 + sc)
        dx = _rms_bwd(d_n * g, xhat, r) + res
        return (dx, jnp.sum(d_h, axis=0, keepdims=True), jnp.sum(d_h * nrm, axis=0, keepdims=True),
                jnp.sum(d_n * xhat, axis=0, keepdims=True))

    return _rowmap(name, body, list(d_hs) + [x, resid], [gain, scale], [(D_MODEL, F32)],
                   [D_MODEL, D_MODEL, D_MODEL], after=after)


def _gn_parts(ro):
    mu = jnp.mean(ro, axis=-1, keepdims=True)
    cen = ro - mu
    rstd = lax.rsqrt(jnp.mean(cen * cen, axis=-1, keepdims=True) + GN_EPS)
    return cen * rstd, rstd


def _retpost_fwd(ro, rg, gn_g, gn_b):
    def body(rot, rgt, g, b):
        outs = []
        for h in range(RET_HEADS):
            sl = slice(h * RET_DV, (h + 1) * RET_DV)
            nrm, _ = _gn_parts(rot[:, sl])
            gate = rgt[:, sl]
            outs.append((gate * _sigmoid(gate)) * (nrm * g[:, sl] + b[:, sl]))
        return jnp.concatenate(outs, axis=-1)
    return _rowmap("retpost_fwd", body, [ro, rg], [gn_g, gn_b], [(RET_V_W, BF16)])[0]


def _retpost_bwd(d_gated, ro, rg, gn_g, gn_b):
    def body(dgt, rot, rgt, g, b):
        d_ro, d_rg, d_g, d_b = [], [], [], []
        for h in range(RET_HEADS):
            sl = slice(h * RET_DV, (h + 1) * RET_DV)
            nrm, rstd = _gn_parts(rot[:, sl])
            gate, dg = rgt[:, sl], dgt[:, sl]
            sg = _sigmoid(gate)
            ron = nrm * g[:, sl] + b[:, sl]
            d_rg.append(dg * ron * (sg * (1.0 + gate * (1.0 - sg))))
            d_ron = dg * (gate * sg)
            d_g.append(jnp.sum(d_ron * nrm, axis=0, keepdims=True))
            d_b.append(jnp.sum(d_ron, axis=0, keepdims=True))
            d_n = d_ron * g[:, sl]
            d_ro.append(rstd * (d_n - jnp.mean(d_n, axis=-1, keepdims=True)
                                - nrm * jnp.mean(d_n * nrm, axis=-1, keepdims=True)))
        cat = lambda ts: jnp.concatenate(ts, axis=-1)
        return cat(d_ro), cat(d_rg), cat(d_g), cat(d_b)
    return _rowmap("retpost_bwd", body, [d_gated, ro, rg], [gn_g, gn_b],
                   [(RET_V_W, F32), (RET_V_W, BF16)], [RET_V_W, RET_V_W])


def _combine(os_, ls_):
    def body(o0, o1, o2, l0, l1, l2):
        mx = jnp.maximum(jnp.maximum(l0, l1), l2)
        e0, e1, e2 = jnp.exp(l0 - mx), jnp.exp(l1 - mx), jnp.exp(l2 - mx)
        den = e0 + e1 + e2
        att = (e0 / den) * o0 + (e1 / den) * o1 + (e2 / den) * o2
        return att, att, mx + jnp.log(den)
    return _rowmap("att_combine", body, list(os_) + list(ls_), [],
                   [(ATT_W, F32), (ATT_W, BF16), (ATT_W, F32)])


def _att_bwd_pre(d_att, att):
    def body(dt, at):
        outs = []
        for h in range(ATT_HPG):
            sl = slice(h * ATT_DH, (h + 1) * ATT_DH)
            outs.append(jnp.broadcast_to(jnp.sum(dt[:, sl] * at[:, sl], axis=-1, keepdims=True),
                                         (dt.shape[0], ATT_DH)))
        return dt, jnp.concatenate(outs, axis=-1)
    return _rowmap("att_bwd_pre", body, [d_att, att], [], [(ATT_W, BF16), (ATT_W, F32)])


def _merge_fwd(gates, ret_out, att_out):
    def body(gt, ro, ao):
        return _sigmoid(gt[:, :D_MODEL]) * ro + _sigmoid(gt[:, D_MODEL:]) * ao
    return _rowmap("merge_fwd", body, [gates, ret_out, att_out], [], [(D_MODEL, BF16)])[0]


def _merge_bwd(d_merged, gates, ret_out, att_out):
    def body(dm, gt, ro, ao):
        sa, sb = _sigmoid(gt[:, :D_MODEL]), _sigmoid(gt[:, D_MODEL:])
        d_gates = jnp.concatenate([dm * ro * (sa * (1.0 - sa)), dm * ao * (sb * (1.0 - sb))], axis=-1)
        return dm * sa, dm * sb, d_gates
    return _rowmap("merge_bwd", body, [d_merged, gates, ret_out, att_out], [],
                   [(D_MODEL, BF16), (D_MODEL, BF16), (2 * D_MODEL, BF16)])


def _gate_bwd(name, d_x, branch, gate):
    def body(dx, br, g):
        return dx * g, jnp.sum(dx * br, axis=0, keepdims=True)
    return _rowmap(name, body, [d_x, branch], [gate], [(D_MODEL, BF16)], [D_MODEL])


def _loss_head(x3, target, gain):
    def body(xt, tt, g):
        xhat, r = _rms_parts(xt)
        err = xhat * g - tt
        d_y = err / D_MODEL
        loss = 0.5 * jnp.sum(jnp.mean(err * err, axis=-1, keepdims=True), axis=0, keepdims=True)
        d_x = _rms_bwd(d_y * g, xhat, r)
        return d_x, jnp.broadcast_to(loss, (1, 128)), jnp.sum(d_y * xhat, axis=0, keepdims=True)
    return _rowmap("loss_head", body, [x3, target], [gain], [(D_MODEL, F32)], [128, D_MODEL])


def _local_step(pos, x, target, mod, norm1_g, norm2_g, norm_f_g, rel_bias, gn_g, gn_b, w_in, rest_gather):
    sh1, sc1, g1, sh2, sc2, g2 = [mod[:, i * D_MODEL:(i + 1) * D_MODEL] for i in range(6)]
    cos, sin = _rope_tables()
    din, qd, kd, cd = _decay_tables()
    buckets = _bucket_tables()
    bias = _bias_build(rel_bias, buckets)
    dils = [d for _, d in ATT_GROUPS]
    nbs = [SEQ // d // ATT_BLK for d in dils]

    h1 = _prenorm_fwd("prenorm1_fwd", x, norm1_g, sh1, sc1)
    h1_p = [_permute_rows(h1, d) for d in dils]

    def rot_epi(acc, cs, sn, scale):
        half = RET_DK // 2
        x1, x2 = acc[:, :half], acc[:, half:]
        return (jnp.concatenate([x1 * cs - x2 * sn, x1 * sn + x2 * cs], axis=-1) * scale,)

    qk_scale = jnp.concatenate([jnp.ones((1, RET_QK_W), F32),
                                jnp.full((1, RET_QK_W), RET_DK ** -0.5, F32)], axis=-1)
    rope_ex = [(cos, (TM, RET_DK // 2), lambda i, j, kk: (i, 0)),
               (sin, (TM, RET_DK // 2), lambda i, j, kk: (i, 0)),
               (qk_scale, (1, RET_DK), lambda i, j, kk: (0, j))]
    rest_sems, rest_shards, rest_fulls, rest_token = rest_gather
    behind = [rest_token]
    rv = _matmul("proj_rv", h1, w_in, "nn", SEQ, RET_V_W, D_MODEL, [BF16], b_off=OFF_V, tk=D_MODEL,
                 after=behind)[0]
    rg = _matmul("proj_rg", h1, w_in, "nn", SEQ, RET_V_W, D_MODEL, [F32], b_off=OFF_G, tk=D_MODEL,
                 after=behind)[0]
    gates = _matmul("proj_gates", h1, w_in, "nn", SEQ, 2 * D_MODEL, D_MODEL, [F32], b_off=OFF_GATE,
                    tn=512, tk=D_MODEL, after=behind)[0]
    aqkv = [_matmul(f"proj_att_g{gi}", h1_p[gi], w_in, "nn", SEQ, 3 * ATT_W, D_MODEL, [BF16],
                    b_off=OFF_ATT + gi * 3 * ATT_W, tn=512, tk=D_MODEL, after=behind)[0]
            for gi in range(3)]

    os_, ls_ = [], []
    for gi in range(3):
        o_g, l_g = _att_fwd(gi, aqkv[gi], bias, nbs[gi])
        os_.append(_unpermute_rows(o_g, dils[gi]))
        ls_.append(_unpermute_rows(l_g, dils[gi]))
        if gi == 1:
            rest_sems, rest_fulls, fwd_token = _gather_rest_forward(rest_sems, rest_shards, rest_fulls,
                                                                    [o_g, rv, rg, gates])

    rqk = _matmul("proj_qk", h1, w_in, "nn", SEQ, 2 * RET_QK_W, D_MODEL, [BF16], b_off=OFF_Q,
                  tn=RET_DK, tk=D_MODEL, epilogue=rot_epi, extras=rope_ex, after=[fwd_token])[0]
    ro, states = _retention_fwd(rqk, rv, din, qd, kd, cd)
    gated = _retpost_fwd(ro, rg, gn_g, gn_b)
    w_ret_out, w_att_out, w_o, w_ff1, w_ff2 = _gather_rest_end(rest_sems, rest_fulls, [gated, os_[2]])
    ret_out = _matmul("ret_out", gated, w_ret_out, "nn", SEQ, D_MODEL, RET_V_W, [F32])[0]
    att, att_b, lse = _combine(os_, ls_)
    att_out = _matmul("att_out", att_b, w_att_out, "nn", SEQ, D_MODEL, ATT_W, [F32])[0]

    merged = _merge_fwd(gates, ret_out, att_out)

    def resid_epi(acc, xt, g):
        return xt + g * acc, acc

    def resid_ex(xin, g):
        return [(xin, (TM, TN), lambda i, j, kk: (i, j)), (g, (1, TN), lambda i, j, kk: (0, j))]

    x2, mix = _matmul("mix_out", merged, w_o, "nn", SEQ, D_MODEL, D_MODEL, [F32, F32],
                      epilogue=resid_epi, extras=resid_ex(x, g1))
    h2 = _prenorm_fwd("prenorm2_fwd", x2, norm2_g, sh2, sc2)

    def relu2_epi(acc):
        r = jnp.maximum(acc, 0.0)
        return r * r, acc

    act, u = _matmul("ff1", h2, w_ff1, "nn", SEQ, D_FF, D_MODEL, [BF16, F32], tk=D_MODEL,
                     epilogue=relu2_epi)
    x3, y2 = _matmul("ff2", act, w_ff2, "nn", SEQ, D_MODEL, D_FF, [F32, F32],
                     epilogue=resid_epi, extras=resid_ex(x2, g2))

    d_x3, loss, d_gf = _loss_head(x3, target, norm_f_g)

    d_y2, d_g2 = _gate_bwd("ff_gate_bwd", d_x3, y2, g2)

    def relu2_bwd_epi(acc, ut):
        return (acc * (2.0 * jnp.maximum(ut, 0.0)),)

    gw_ff2 = _matmul_tn_pair("ff2_dw", pos, act, d_y2, D_FF, D_MODEL, SEQ, D_FF // N_CHIPS,
                             tm=512, tn=1024, tk=1024)
    d_u = _matmul("ff2_dx", d_y2, w_ff2, "nt", SEQ, D_FF, D_MODEL, [BF16], epilogue=relu2_bwd_epi,
                  extras=[(u, (TM, TN), lambda i, j, kk: (i, j))])[0]
    gw_ff1 = _matmul_tn_pair("ff1_dw", pos, h2, d_u, D_MODEL, D_FF, SEQ, D_MODEL,
                             tm=512, tn=1024, tk=1024)
    ffn = ["w_ff2", "w_ff1"]
    ffn_started = _ici_start("ici_start_ffn", ffn, [gw_ff2, gw_ff1])
    d_h2 = _matmul("ff1_dx", d_u, w_ff1, "nt", SEQ, D_MODEL, D_FF, [F32], after=[ffn_started[3]])[0]
    d_x2, d_sh2, d_sc2, d_n2g = _prenorm_bwd("prenorm2_bwd", [d_h2], x2, norm2_g, sc2, d_x3)

    d_mix, d_g1 = _gate_bwd("mix_gate_bwd", d_x2, mix, g1)
    gw_o = _matmul_tn_pair("mix_dw", pos, merged, d_mix, D_MODEL, D_MODEL, SEQ, D_MODEL // N_CHIPS,
                           tm=128, tn=1024, tk=2048)
    d_merged = _matmul("mix_dx", d_mix, w_o, "nt", SEQ, D_MODEL, D_MODEL, [F32])[0]
    d_ret_out, d_att_out, d_gates = _merge_bwd(d_merged, gates, ret_out, att_out)

    gw_ret_out = _matmul_tn_pair("ret_out_dw", pos, gated, d_ret_out, RET_V_W, D_MODEL, SEQ,
                                 RET_V_W // N_CHIPS, tm=256, tn=1024, tk=1024)
    gw_att_out = _matmul_tn_pair("att_out_dw", pos, att_b, d_att_out, ATT_W, D_MODEL, SEQ, ATT_W,
                                 tm=256, tn=1024, tk=2048)
    mixer = ["w_o", "w_ret_out", "w_att_out"]
    mixer_started = _ici_start("ici_start_mixer", mixer, [gw_o, gw_ret_out, gw_att_out])
    d_gated = _matmul("ret_out_dx", d_ret_out, w_ret_out, "nt", SEQ, RET_V_W, D_MODEL, [F32],
                      after=[mixer_started[3]])[0]
    d_att = _matmul("att_out_dx", d_att_out, w_att_out, "nt", SEQ, ATT_W, D_MODEL, [F32],
                    after=[mixer_started[3]])[0]

    d_ro, d_rg, d_gn_g, d_gn_b = _retpost_bwd(d_gated, ro, rg, gn_g, gn_b)
    d_rqkv = _retention_bwd(rqk, rv, states, d_ro, din, qd, kd, cd, cos, sin)

    d_att_b, dd = _att_bwd_pre(d_att, att)
    d_aqkv, dsbs = [], []
    for gi in range(3):
        da_p = _permute_rows(d_att_b, dils[gi])
        l_p = _permute_rows(lse, dils[gi])
        dd_p = _permute_rows(dd, dils[gi])
        dqkv, dsb = _att_bwd(gi, aqkv[gi], da_p, l_p, dd_p, bias, nbs[gi])
        if dils[gi] > 1:
            dqkv = dqkv.reshape(3, dils[gi], SEQ // dils[gi], ATT_W).transpose(0, 2, 1, 3).reshape(
                3, SEQ, ATT_W)
        d_aqkv.append(dqkv)
        dsbs.append(dsb)
    d_rel_bias = _bias_grad(jnp.stack(dsbs), buckets)

    d_proj = [(d_rqkv, False), (d_rg, False)] + [(t, True) for t in d_aqkv] + [(d_gates, False)]
    gw_in = _matmul_tn_pair("proj_dw", pos, h1, d_proj, D_MODEL, IN_COLS, SEQ, D_MODEL,
                            tm=512, tn=ATT_W, tk=1024)
    sems, (gw_in,), (land,), token = _ici_start("ici_start_w_in", ["w_in"], [gw_in])
    d_h1 = _matmul("proj_dx", d_proj, w_in, "nt", SEQ, D_MODEL, IN_COLS, [F32], tn=1024, tk=ATT_W,
                   after=[token])[0]
    pending = (sems, land)

    names = ffn + mixer
    psums, got = _ici_wait("ici_wait_rest", names, list(ffn_started[0]) + list(mixer_started[0]),
                           list(ffn_started[1]) + list(mixer_started[1]),
                           list(ffn_started[2]) + list(mixer_started[2]), [d_h1])
    g_big = {n: _final_sum("final_" + n, pos, dict(BIG)[n], psums[i], got[i], SHARD[n])
             for i, n in enumerate(names)}
    grad_x, d_sh1, d_sc1, d_n1g = _prenorm_bwd("prenorm1_bwd", [d_h1], x, norm1_g, sc1, d_x2,
                                               after=list(g_big.values()))
    d_mod = jnp.concatenate([d_sh1, d_sc1, d_g1, d_sh2, d_sc2, d_g2], axis=-1)
    small = dict(norm1_g=d_n1g, norm2_g=d_n2g, norm_f_g=d_gf, gn_g=d_gn_g, gn_b=d_gn_b,
                 rel_bias=d_rel_bias)
    return loss, grad_x, d_mod, small, g_big, (gw_in,) + pending


def _me():
    return lax.axis_index("x"), lax.axis_index("y"), lax.axis_index("c")


def _peer(x, y, c, mask):
    return (x ^ ((mask >> 2) & 1), y ^ ((mask >> 1) & 1), c ^ (mask & 1))


def _gather8(src_ref, dst_ref, send_sems, recv_sems):
    x, y, c = _me()
    me = 4 * x + 2 * y + c
    copies = []
    for mask in range(1, N_DEV):
        cp = pltpu.make_async_remote_copy(
            src_ref=src_ref, dst_ref=dst_ref.at[me], send_sem=send_sems.at[mask - 1],
            recv_sem=recv_sems.at[mask - 1], device_id=_peer(x, y, c, mask), device_id_type=MESH)
        cp.start()
        copies.append(cp)
    dst_ref[me] = src_ref[...]
    for cp in copies:
        cp.wait_recv()
    for cp in copies:
        cp.wait_send()


def _ada_fwd(c_in, w_ada, b_ada):
    ncol = ADA_COLS // N_CHIPS

    def body(c_ref, w_ref, b_ref, mod_ref, sc_ref, cbuf, cg, mbuf, mg, s1, r1, s2, r2):
        x, y, c = _me()
        me = 4 * x + 2 * y + c
        cv = c_ref[...]
        cbuf[...] = jnp.broadcast_to(cv * _sigmoid(cv), cbuf.shape)
        _gather8(cbuf, cg, s1, r1)
        rows = lax.broadcasted_iota(I32, (N_DEV, D_MODEL), 0)
        sc_all = jnp.zeros((N_DEV, D_MODEL), F32)
        for d in range(N_DEV):
            sc_all = jnp.where(rows == d, cg[d], sc_all)
        sc_ref[...] = sc_all
        mbuf[...] = jnp.dot(sc_all.astype(BF16), w_ref[...].astype(BF16), preferred_element_type=F32)
        _gather8(mbuf, mg, s2, r2)
        rowsel = lax.broadcasted_iota(I32, (N_DEV, ncol), 0) == me
        for k in range(N_CHIPS):
            blk = mg[2 * k]
            row = jnp.sum(jnp.where(rowsel, blk, 0.0), axis=0, keepdims=True)
            mod_ref[:, k * ncol:(k + 1) * ncol] = row + b_ref[:, k * ncol:(k + 1) * ncol]

    vm = pl.BlockSpec(memory_space=pltpu.VMEM)
    return pl.pallas_call(
        body, name="ada_fwd",
        in_specs=[vm, vm, vm], out_specs=[vm, vm],
        out_shape=[jax.ShapeDtypeStruct((1, ADA_COLS), F32), jax.ShapeDtypeStruct((N_DEV, D_MODEL), F32)],
        scratch_shapes=[
            pltpu.VMEM((8, D_MODEL), F32), pltpu.VMEM((N_DEV, 8, D_MODEL), F32),
            pltpu.VMEM((8, ncol), F32), pltpu.VMEM((N_DEV, 8, ncol), F32),
            pltpu.SemaphoreType.DMA((N_DEV - 1,)), pltpu.SemaphoreType.DMA((N_DEV - 1,)),
            pltpu.SemaphoreType.DMA((N_DEV - 1,)), pltpu.SemaphoreType.DMA((N_DEV - 1,)),
        ],
        compiler_params=pltpu.CompilerParams(vmem_limit_bytes=VMEM_LIMIT_V7X),
    )(c_in, w_ada, b_ada)


def _small_reduce(pack, sc_all):
    ncol = ADA_COLS // N_CHIPS

    def body(p_ref, sc_ref, tot_ref, gw_ref, pg, s1, r1):
        x, y, _ = _me()
        chip = 2 * x + y
        _gather8(p_ref, pg, s1, r1)
        tot = pg[0]
        for d in range(1, N_DEV):
            tot = tot + pg[d]
        tot_ref[...] = tot
        rows = lax.broadcasted_iota(I32, (N_DEV, ncol), 0)
        dmod = jnp.zeros((N_DEV, ncol), F32)
        for k in range(N_CHIPS):
            part = jnp.zeros((N_DEV, ncol), F32)
            for d in range(N_DEV):
                part = jnp.where(rows == d, pg[d, :, k * ncol:(k + 1) * ncol][0:1, :], part)
            dmod = jnp.where(chip == k, part, dmod)
        gw_ref[...] = lax.dot_general(sc_ref[...].astype(BF16), dmod.astype(BF16), _TN,
                                      preferred_element_type=F32)

    vm = pl.BlockSpec(memory_space=pltpu.VMEM)
    return pl.pallas_call(
        body, name="small_reduce",
        in_specs=[vm, vm], out_specs=[vm, vm],
        out_shape=[jax.ShapeDtypeStruct((8, ADA_COLS), F32), jax.ShapeDtypeStruct((D_MODEL, ncol), F32)],
        scratch_shapes=[pltpu.VMEM((N_DEV, 8, ADA_COLS), F32),
                        pltpu.SemaphoreType.DMA((N_DEV - 1,)), pltpu.SemaphoreType.DMA((N_DEV - 1,))],
        compiler_params=pltpu.CompilerParams(vmem_limit_bytes=VMEM_LIMIT_V7X),
    )(pack, sc_all)


BIG = (("w_in", 1), ("w_ret_out", 0), ("w_att_out", 1), ("w_o", 0), ("w_ff1", 1), ("w_ff2", 0))
SHARD = {"w_in": (D_MODEL, IN_COLS // N_CHIPS), "w_ret_out": (RET_V_W // N_CHIPS, D_MODEL),
         "w_att_out": (ATT_W, D_MODEL // N_CHIPS), "w_o": (D_MODEL // N_CHIPS, D_MODEL),
         "w_ff1": (D_MODEL, D_FF // N_CHIPS), "w_ff2": (D_FF // N_CHIPS, D_MODEL)}
_CHIP_FLIPS = ((1, 0), (0, 1), (1, 1))


def _region(ref, axis, chip, half, shard_shape):
    r, cw = shard_shape
    hr = r // 2
    if axis == 1:
        return ref.at[pl.ds(half * hr, hr), pl.ds(chip * cw, cw)]
    return ref.at[pl.ds(chip * r + half * hr, hr), :]


def _gather_weights(shards, n_remote):
    nw = len(BIG)
    shapes = [s.shape for s in shards]
    full_shapes = [(r, N_CHIPS * cw) if ax == 1 else (N_CHIPS * r, cw)
                   for (r, cw), (_, ax) in zip(shapes, BIG)]

    def body(*refs):
        ins, outs = refs[:nw], refs[nw:2 * nw]
        own = refs[2 * nw:3 * nw]
        from_ici, from_sib = refs[3 * nw:3 * nw + n_remote], refs[3 * nw + n_remote:3 * nw + 2 * n_remote]
        ld_sem, st_sem, s_ici, r_ici, s_d2d, r_d2d, st_a, st_b = refs[3 * nw + 2 * n_remote:]
        x, y, c = _me()
        chip = 2 * x + y
        sib = (x, y, 1 - c)
        loads = [pltpu.make_async_copy(ins[i], own[i], ld_sem.at[i]) for i in range(nw)]
        for cp in loads:
            cp.start()
        pending, first = [], []
        for i, (_, ax) in enumerate(BIG):
            r, cw = shapes[i]
            hr = r // 2
            loads[i].wait()
            dst = outs[i].at[:, pl.ds(chip * cw, cw)] if ax == 1 else outs[i].at[pl.ds(chip * r, r), :]
            cp = pltpu.make_async_copy(own[i], dst, st_sem.at[i])
            cp.start()
            pending.append(cp)
            for j, (fx, fy) in enumerate(_CHIP_FLIPS if i < n_remote else ()):
                rc = pltpu.make_async_remote_copy(
                    src_ref=own[i].at[pl.ds(c * hr, hr), :], dst_ref=from_ici[i].at[j],
                    send_sem=s_ici.at[j * nw + i], recv_sem=r_ici.at[j * nw + i],
                    device_id=(x ^ fx, y ^ fy, c), device_id_type=MESH)
                rc.start()
                first.append((j, i, rc))
        passed = []
        for j, i, rc in first:
            fx, fy = _CHIP_FLIPS[j]
            src_chip = 2 * (x ^ fx) + (y ^ fy)
            ax = BIG[i][1]
            rc.wait_recv()
            fw = pltpu.make_async_remote_copy(
                src_ref=from_ici[i].at[j], dst_ref=from_sib[i].at[j], send_sem=s_d2d.at[j * nw + i],
                recv_sem=r_d2d.at[j * nw + i], device_id=sib, device_id_type=MESH)
            fw.start()
            passed.append((j, i, src_chip, fw))
            st = pltpu.make_async_copy(from_ici[i].at[j], _region(outs[i], ax, src_chip, c, shapes[i]),
                                       st_a.at[j * nw + i])
            st.start()
            pending.append(st)
        for j, i, src_chip, fw in passed:
            fw.wait_recv()
            st = pltpu.make_async_copy(from_sib[i].at[j],
                                       _region(outs[i], BIG[i][1], src_chip, 1 - c, shapes[i]),
                                       st_b.at[j * nw + i])
            st.start()
            pending.append(st)
        for _, _, rc in first:
            rc.wait_send()
        for _, _, _, fw in passed:
            fw.wait_send()
        for cp in pending:
            cp.wait()

    hbm = pl.BlockSpec(memory_space=pl.ANY)
    halves = [pltpu.VMEM((3, r // 2, cw), BF16) for r, cw in shapes[:n_remote]]
    return pl.pallas_call(
        body, name="gather_weights",
        in_specs=[hbm] * nw, out_specs=[hbm] * nw,
        out_shape=[jax.ShapeDtypeStruct(fs, BF16) for fs in full_shapes],
        scratch_shapes=[pltpu.VMEM(sh, BF16) for sh in shapes] + halves + halves
        + [pltpu.SemaphoreType.DMA((nw,)), pltpu.SemaphoreType.DMA((nw,))]
        + [pltpu.SemaphoreType.DMA((3 * nw,))] * 6,
        compiler_params=pltpu.CompilerParams(vmem_limit_bytes=VMEM_LIMIT_V7X),
    )(*shards)


REST = BIG[1:]
_SIDE_EFFECTS = pltpu.CompilerParams(has_side_effects=pltpu.SideEffectType.DATAFLOW_SIDE_EFFECTING)
_ANY_SPEC = pl.BlockSpec(memory_space=pl.ANY)


def _rest_ici_copies(shard_refs, full_refs, sems):
    x, y, c = _me()
    chip = 2 * x + y
    n = 3 * len(REST)
    copies = []
    for i, (name, ax) in enumerate(REST):
        hr = SHARD[name][0] // 2
        for j, (fx, fy) in enumerate(_CHIP_FLIPS):
            copies.append(pltpu.make_async_remote_copy(
                src_ref=shard_refs[i].at[pl.ds(c * hr, hr), :],
                dst_ref=_region(full_refs[i], ax, chip, c, SHARD[name]),
                send_sem=sems[3 * i + j], recv_sem=sems[n + 3 * i + j],
                device_id=(x ^ fx, y ^ fy, c), device_id_type=MESH))
    return copies


def _rest_d2d_copies(full_refs, sems):
    x, y, c = _me()
    n = 3 * len(REST)
    copies = []
    for i, (name, ax) in enumerate(REST):
        for j, (fx, fy) in enumerate(_CHIP_FLIPS):
            reg = _region(full_refs[i], ax, 2 * (x ^ fx) + (y ^ fy), c, SHARD[name])
            copies.append(pltpu.make_async_remote_copy(
                src_ref=reg, dst_ref=reg, send_sem=sems[3 * i + j], recv_sem=sems[n + 3 * i + j],
                device_id=(x, y, 1 - c), device_id_type=MESH))
    return copies


def _gather_rest_start(shards, fulls, after):
    nr, ns, na = len(REST), 6 * len(REST), len(after)

    def body(*refs):
        for cp in _rest_ici_copies(refs[:nr], refs[nr:2 * nr], refs[2 * nr + na:2 * nr + na + ns]):
            cp.start()
        token = refs[-1]
        token[...] = jnp.zeros_like(token)

    hbm = lambda a: pltpu.HBM(a.shape, a.dtype)
    res = pl.pallas_call(
        body, name="gather_rest_start",
        out_shape=(pltpu.SemaphoreType.DMA(()),) * ns + tuple(hbm(a) for a in shards + fulls)
        + (jax.ShapeDtypeStruct((8, 128), F32),),
        in_specs=(_HBM_SPEC,) * (2 * nr) + (_ANY_SPEC,) * na,
        out_specs=(_SEM_SPEC,) * ns + (_HBM_SPEC,) * (2 * nr) + (pl.BlockSpec(memory_space=pltpu.VMEM),),
        input_output_aliases={k: ns + k for k in range(2 * nr)}, compiler_params=_SIDE_EFFECTS,
    )(*[pltpu.with_memory_space_constraint(a, pltpu.HBM) for a in shards + fulls], *after)
    return res[:ns], res[ns:ns + nr], res[ns + nr:ns + 2 * nr], res[-1]


def _gather_rest_forward(sems, shards, fulls, after):
    nr, ns = len(REST), 6 * len(REST)

    def body(*refs):
        shard_refs, full_refs, old = refs[:nr], refs[nr:2 * nr], refs[2 * nr:2 * nr + ns]
        new = refs[2 * nr + ns + len(after):2 * nr + 2 * ns + len(after)]
        for cp in _rest_ici_copies(shard_refs, full_refs, old):
            cp.wait_send()
            cp.wait_recv()
        for cp in _rest_d2d_copies(full_refs, new):
            cp.start()
        token = refs[-1]
        token[...] = jnp.zeros_like(token)

    res = pl.pallas_call(
        body, name="gather_rest_forward",
        out_shape=(pltpu.SemaphoreType.DMA(()),) * ns + tuple(pltpu.HBM(a.shape, a.dtype) for a in fulls)
        + (jax.ShapeDtypeStruct((8, 128), F32),),
        in_specs=(_HBM_SPEC,) * (2 * nr) + (_SEM_SPEC,) * ns + (_ANY_SPEC,) * len(after),
        out_specs=(_SEM_SPEC,) * ns + (_HBM_SPEC,) * nr + (pl.BlockSpec(memory_space=pltpu.VMEM),),
        input_output_aliases={nr + k: ns + k for k in range(nr)}, compiler_params=_SIDE_EFFECTS,
    )(*shards, *fulls, *sems, *after)
    return res[:ns], res[ns:ns + nr], res[-1]


def _gather_rest_end(sems, fulls, after):
    nr, ns = len(REST), 6 * len(REST)

    def body(*refs):
        for cp in _rest_d2d_copies(refs[:nr], refs[nr:nr + ns]):
            cp.wait_send()
            cp.wait_recv()

    return pl.pallas_call(
        body, name="gather_rest_end",
        out_shape=tuple(pltpu.HBM(a.shape, a.dtype) for a in fulls),
        in_specs=(_HBM_SPEC,) * nr + (_SEM_SPEC,) * ns + (_ANY_SPEC,) * len(after),
        out_specs=(_HBM_SPEC,) * nr,
        input_output_aliases={k: k for k in range(nr)}, compiler_params=_SIDE_EFFECTS,
    )(*fulls, *sems, *after)


def _adam_update(w, g, m, v):
    mn = ADAM_B1 * m + (1.0 - ADAM_B1) * g
    vn = ADAM_B2 * v + (1.0 - ADAM_B2) * (g * g)
    m_hat = mn / (1.0 - ADAM_B1 ** ADAM_STEP)
    v_hat = vn / (1.0 - ADAM_B2 ** ADAM_STEP)
    return -ADAM_LR * (m_hat / (jnp.sqrt(v_hat) + ADAM_EPS) + ADAM_WD * w), mn, vn


def _final_sum(name, pos, axis, psum, recv, shard_shape, after=(), tr=128):
    r, cw = shard_shape
    hr = r // 2
    tr = min(tr, hr)
    nt = hr // tr
    n_after = len(after)

    def kern(pos_ref, p_ref, r_ref, *rest):
        g_ref, send_buf, land_buf, s_sem, r_sem = rest[n_after:]
        p, t = pl.program_id(0), pl.program_id(1)
        sib = _sibling()

        def copy(i):
            return pltpu.make_async_remote_copy(
                src_ref=send_buf.at[i], dst_ref=land_buf.at[i], send_sem=s_sem.at[i],
                recv_sem=r_sem.at[i], device_id=sib, device_id_type=MESH)

        @pl.when(p == 0)
        def _():
            tot = p_ref[...].astype(F32)
            for j in range(3):
                tot = tot + r_ref[j].astype(F32)
            send_buf[t] = tot
            copy(t).start()
            g_ref[...] = tot

        @pl.when(p == 1)
        def _():
            copy(t).wait_recv()
            g_ref[...] = land_buf[t]

        @pl.when(jnp.logical_and(p == 1, t == nt - 1))
        def _():
            for i in range(nt):
                copy(i).wait_send()

    def shard_rows(p, t, pos_ref):
        return (jnp.where(p == 0, pos_ref[0], 1 - pos_ref[0]) * nt + t, 0)

    def own_part(p, t, pos_ref):
        tt = jnp.where(p == 0, t, nt - 1)
        return (tt, pos_ref[1]) if axis == 1 else (pos_ref[1] * nt + tt, 0)

    grid_spec = pltpu.PrefetchScalarGridSpec(
        num_scalar_prefetch=1, grid=(2, nt),
        in_specs=[pl.BlockSpec((tr, cw), own_part),
                  pl.BlockSpec((3, tr, cw), lambda p, t, pos_ref: (0, jnp.where(p == 0, t, nt - 1), 0))]
        + [pl.BlockSpec(memory_space=pl.ANY)] * n_after,
        out_specs=pl.BlockSpec((tr, cw), shard_rows),
        scratch_shapes=[pltpu.VMEM((nt, tr, cw), F32), pltpu.VMEM((nt, tr, cw), F32),
                        pltpu.SemaphoreType.DMA((nt,)), pltpu.SemaphoreType.DMA((nt,))])
    return pl.pallas_call(
        kern, name=name, grid_spec=grid_spec, out_shape=jax.ShapeDtypeStruct((r, cw), F32),
        compiler_params=_cparams(("arbitrary", "arbitrary")),
    )(pos, psum, recv, *after)


def _adamw(name, w, g, m, v):
    r, cw = w.shape
    tr = min(r, 128)

    def kern(w_ref, g_ref, m_ref, v_ref, go_ref, d_ref, nm_ref, nv_ref):
        gv = g_ref[...]
        go_ref[...] = gv
        d_ref[...], nm_ref[...], nv_ref[...] = _adam_update(w_ref[...], gv, m_ref[...], v_ref[...])

    spec = pl.BlockSpec((tr, cw), lambda i: (i, 0))
    return pl.pallas_call(
        kern, name=name, grid=(r // tr,), in_specs=[spec] * 4, out_specs=[spec] * 4,
        out_shape=[jax.ShapeDtypeStruct((r, cw), F32)] * 4, compiler_params=_cparams(("parallel",)),
    )(w, g, m, v)


_PACK_W = ADA_COLS
_NB = REL_BUCKETS * N_ATT_HEADS
_SMALL_SLOTS = {
    "b_ada": (0, 0, ADA_COLS),
    "norm1_g": (1, 0, D_MODEL), "norm2_g": (1, D_MODEL, D_MODEL), "norm_f_g": (1, 2 * D_MODEL, D_MODEL),
    "ret_gn_g": (1, 3 * D_MODEL, RET_V_W),
    "ret_gn_b": (2, 0, RET_V_W), "rel_bias": (2, RET_V_W, _NB), "loss": (2, RET_V_W + 512, 128),
}


def _pack_small(vals):
    rows = []
    for r in range(8):
        items = sorted([(off, n) for n, (rr, off, _) in _SMALL_SLOTS.items() if rr == r and n in vals])
        parts, pos = [], 0
        for off, n in items:
            if off > pos:
                parts.append(jnp.zeros((1, off - pos), F32))
            parts.append(vals[n].reshape(1, -1).astype(F32))
            pos = off + _SMALL_SLOTS[n][2]
        if pos < _PACK_W:
            parts.append(jnp.zeros((1, _PACK_W - pos), F32))
        rows.append(jnp.concatenate(parts, axis=-1))
    return jnp.concatenate(rows, axis=0)


def _unpack_small(pack, name):
    r, off, wd = _SMALL_SLOTS[name]
    return pack[r:r + 1, off:off + wd]


def kernel(x, c, w_ada, b_ada, norm1_g, w_in, rel_bias, ret_gn_g, ret_gn_b, w_ret_out, w_att_out, w_o, norm2_g, w_ff1, w_ff2, norm_f_g, loss_target, m_w_ada, m_b_ada, m_norm1_g, m_w_in, m_rel_bias, m_ret_gn_g, m_ret_gn_b, m_w_ret_out, m_w_att_out, m_w_o, m_norm2_g, m_w_ff1, m_w_ff2, m_norm_f_g, v_w_ada, v_b_ada, v_norm1_g, v_w_in, v_rel_bias, v_ret_gn_g, v_ret_gn_b, v_w_ret_out, v_w_att_out, v_w_o, v_norm2_g, v_w_ff1, v_w_ff2, v_norm_f_g):
    given = dict(locals())
    big_names = [n for n, _ in BIG]
    shard_w = {n: given[n][0] for n in big_names}
    assert all(shard_w[n].shape == SHARD[n] for n in big_names)

    shards_bf = [shard_w[n].astype(BF16) for n in big_names]
    full = _gather_weights(shards_bf, 1)
    mod, sc_all = _ada_fwd(c, w_ada[0], b_ada)
    rest_gather = _gather_rest_start(shards_bf[1:], list(full[1:]), [mod])
    pos = _where_am_i()

    loss, grad_x, d_mod, small, g_big, pending = _local_step(
        pos, x[0], loss_target[0], mod, norm1_g, norm2_g, norm_f_g.reshape(1, -1), rel_bias, ret_gn_g,
        ret_gn_b, full[0], rest_gather)

    pack_g = _pack_small(dict(b_ada=d_mod, norm1_g=small["norm1_g"], norm2_g=small["norm2_g"],
                              norm_f_g=small["norm_f_g"], ret_gn_g=small["gn_g"], ret_gn_b=small["gn_b"],
                              rel_bias=small["rel_bias"], loss=loss))
    tot, g_w_ada = _small_reduce(pack_g, sc_all)

    small_names = ["b_ada", "norm1_g", "rel_bias", "ret_gn_g", "ret_gn_b", "norm2_g", "norm_f_g"]
    pack_w = _pack_small({n: given[n] for n in small_names})
    pack_m = _pack_small({n: given["m_" + n] for n in small_names})
    pack_v = _pack_small({n: given["v_" + n] for n in small_names})
    _, sd, sm, sv = _adamw("adamw_small", pack_w, tot, pack_m, pack_v)

    grads, deltas, new_m, new_v = {}, {}, {}, {}
    for n in small_names:
        shp = given[n].shape
        grads[n] = _unpack_small(tot, n).reshape(shp)
        deltas[n] = _unpack_small(sd, n).reshape(shp)
        new_m[n] = _unpack_small(sm, n).reshape(shp)
        new_v[n] = _unpack_small(sv, n).reshape(shp)
    g_big["w_ada"] = g_w_ada
    for n in ["w_ada"] + big_names[1:] + big_names[:1]:
        if n == "w_in":
            gw_in, sems, land = pending
            done = [tot, sd] + [deltas[k] for k in ["w_ada"] + big_names[1:]]
            (gw_in,), (got,) = _ici_wait("ici_wait_w_in", [n], sems, [gw_in], [land], done)
            g_big[n] = _final_sum("final_w_in", pos, 1, gw_in, got, SHARD[n])
        g, d, nm, nv = _adamw("adamw_" + n, given[n][0], g_big[n], given["m_" + n][0], given["v_" + n][0])
        grads[n], deltas[n], new_m[n], new_v[n] = g[None], d[None], nm[None], nv[None]

    order = ["w_ada", "b_ada", "norm1_g", "w_in", "rel_bias", "ret_gn_g", "ret_gn_b", "w_ret_out",
             "w_att_out", "w_o", "norm2_g", "w_ff1", "w_ff2", "norm_f_g"]
    loss_out = _unpack_small(tot, "loss")[0, 0]
    return (loss_out, grad_x[None], *[grads[n] for n in order], *[deltas[n] for n in order],
            *[new_m[n] for n in order], *[new_v[n] for n in order])
```

```python
import functools
import math

import jax
import jax.numpy as jnp
import numpy as np
from jax import lax
from jax.experimental import pallas as pl
from jax.experimental.pallas import tpu as pltpu

F32 = jnp.float32
BF16 = jnp.bfloat16
I32 = jnp.int32

SEQ = 2048
D_MODEL = 1024
RET_HEADS = 4
RET_DK = 256
RET_DV = 512
RET_CHUNK = 128
RET_QK_W = RET_HEADS * RET_DK
RET_V_W = RET_HEADS * RET_DV
ATT_GROUPS = ((128, 1), (512, 4), (2048, 16))
ATT_HPG = 4
ATT_DH = 128
ATT_W = ATT_HPG * ATT_DH
ATT_BLK = 128
N_BLK = SEQ // ATT_BLK
REL_BUCKETS = 32
REL_MAX_DIST = 2048
N_ATT_HEADS = 12
D_FF = 4 * D_MODEL
RMS_EPS = 1e-6
GN_EPS = 1e-5
ROPE_BASE = 10000.0
IN_COLS = 2 * RET_QK_W + 2 * RET_V_W + 9 * ATT_W + 2 * D_MODEL
OFF_Q, OFF_K, OFF_V, OFF_G = 0, RET_QK_W, 2 * RET_QK_W, 2 * RET_QK_W + RET_V_W
OFF_ATT = 2 * RET_QK_W + 2 * RET_V_W
OFF_GATE = OFF_ATT + 9 * ATT_W
N_CHIPS = 4
N_DEV = 8
ADA_COLS = 6 * D_MODEL

ADAM_LR = 0.001
ADAM_B1 = 0.9
ADAM_B2 = 0.999
ADAM_EPS = 1e-08
ADAM_WD = 0.01
ADAM_STEP = 10

VMEM_LIMIT_V7X = 56 * 1024 * 1024
MESH = pl.DeviceIdType.MESH


def _cparams(sem):
    return pltpu.CompilerParams(dimension_semantics=sem, vmem_limit_bytes=VMEM_LIMIT_V7X)


def _sigmoid(v):
    return 1.0 / (1.0 + jnp.exp(-v))


def _rowmap(name, body, row_ins, bcast_ins, row_outs, sum_outs=(), tm=256, after=()):
    m = row_ins[0].shape[0]
    n_in = len(row_ins) + len(bcast_ins)
    n_ro = len(row_outs)

    def kern(*refs):
        vals = [r[...] for r in refs[:n_in]]
        res = body(*vals)
        if not isinstance(res, (tuple, list)):
            res = (res,)
        outs = refs[n_in + len(after):]
        for r, v in zip(outs[:n_ro], res[:n_ro]):
            r[...] = v.astype(r.dtype)
        if sum_outs:
            @pl.when(pl.program_id(0) == 0)
            def _():
                for r in outs[n_ro:]:
                    r[...] = jnp.zeros_like(r)
            for r, v in zip(outs[n_ro:], res[n_ro:]):
                r[...] += v

    in_specs = [pl.BlockSpec((tm, a.shape[1]), lambda i: (i, 0)) for a in row_ins]
    in_specs += [pl.BlockSpec(a.shape, lambda i: (0, 0)) for a in bcast_ins]
    in_specs += [pl.BlockSpec(memory_space=pl.ANY)] * len(after)
    out_specs = [pl.BlockSpec((tm, n), lambda i: (i, 0)) for n, _ in row_outs]
    out_specs += [pl.BlockSpec((1, n), lambda i: (0, 0)) for n in sum_outs]
    out_shape = [jax.ShapeDtypeStruct((m, n), dt) for n, dt in row_outs]
    out_shape += [jax.ShapeDtypeStruct((1, n), F32) for n in sum_outs]
    return pl.pallas_call(
        kern, name=name, grid=(m // tm,), in_specs=in_specs, out_specs=out_specs,
        out_shape=out_shape, compiler_params=_cparams(("arbitrary",)),
    )(*row_ins, *bcast_ins, *after)


TM, TN = 1024, 1024


def _piece_chunks(piece, width):
    arr, stacked = piece
    return arr.shape[0] if stacked else arr.shape[1] // width


def _piece_spec(piece, rows, width, start, row_of, chunk_of):
    arr, stacked = piece
    last = _piece_chunks(piece, width) - 1

    def local(*ids):
        return jnp.clip(chunk_of(*ids) - start, 0, last)

    def row(*ids):
        rel = chunk_of(*ids) - start
        return jnp.where(jnp.logical_and(rel >= 0, rel <= last), row_of(*ids), 0)

    if stacked:
        return pl.BlockSpec((None, rows, width), lambda *ids: (local(*ids), row(*ids), 0))
    return pl.BlockSpec((rows, width), lambda *ids: (row(*ids), local(*ids)))


def _piece_starts(pieces, width):
    return [sum(_piece_chunks(p, width) for p in pieces[:q]) for q in range(len(pieces))]


def _matmul(name, a, b, kind, m, n, k, outs, *, b_off=0, tm=TM, tn=TN, tk=1024,
            epilogue=None, extras=(), after=()):
    tm, tn, tk = min(tm, m), min(tn, n), min(tk, k)
    nk = k // tk
    pieces = a if isinstance(a, list) else [(a, False)]
    starts = _piece_starts(pieces, tk)
    if kind == "nn":
        a_specs = [pl.BlockSpec((tm, tk), lambda i, j, kk: (i, kk))]
        b_spec = pl.BlockSpec((tk, tn), lambda i, j, kk: (kk, b_off // tn + j))
        dn = (((1,), (0,)), ((), ()))
    elif kind == "nt":
        a_specs = [_piece_spec(p, tm, tk, st, lambda i, j, kk: i, lambda i, j, kk: kk)
                   for p, st in zip(pieces, starts)]
        b_spec = pl.BlockSpec((tn, tk), lambda i, j, kk: (j, b_off // tk + kk))
        dn = (((1,), (1,)), ((), ()))
    else:
        a_specs = [pl.BlockSpec((tk, tm), lambda i, j, kk: (kk, i))]
        b_spec = pl.BlockSpec((tk, tn), lambda i, j, kk: (kk, j))
        dn = (((0,), (0,)), ((), ()))
    n_a, n_ex, n_out = len(pieces), len(extras), len(outs)
    if epilogue is None:
        epilogue = lambda acc: (acc,)

    def finish(acc, ex_refs, out_refs):
        res = epilogue(acc, *[r[...] for r in ex_refs])
        for r, v in zip(out_refs, res):
            r[...] = v.astype(r.dtype)

    n_in = n_a + 1 + n_ex + len(after)

    def kern(*refs):
        a_refs, b_ref = refs[:n_a], refs[n_a]
        ex_refs = refs[n_a + 1:n_a + 1 + n_ex]
        out_refs = refs[n_in:n_in + n_out]
        kk = pl.program_id(2)
        dot = lambda a_ref: lax.dot_general(a_ref[...], b_ref[...], dn, preferred_element_type=F32)
        if nk == 1:
            finish(dot(a_refs[0]), ex_refs, out_refs)
            return
        acc_ref = refs[n_in + n_out]
        if n_a == 1:
            part = dot(a_refs[0])

            @pl.when(kk == 0)
            def _():
                acc_ref[...] = part

            @pl.when(kk > 0)
            def _():
                acc_ref[...] += part
        else:
            @pl.when(kk == 0)
            def _():
                acc_ref[...] = jnp.zeros_like(acc_ref)

            for q in range(n_a):
                @pl.when(jnp.logical_and(kk >= starts[q], kk < starts[q] + _piece_chunks(pieces[q], tk)))
                def _(q=q):
                    acc_ref[...] += dot(a_refs[q])

        @pl.when(kk == nk - 1)
        def _():
            finish(acc_ref[...], ex_refs, out_refs)

    in_specs = a_specs + [b_spec] + [pl.BlockSpec(bs, im) for _, bs, im in extras]
    in_specs += [pl.BlockSpec(memory_space=pl.ANY)] * len(after)
    return pl.pallas_call(
        kern, name=name, grid=(m // tm, n // tn, nk), in_specs=in_specs,
        out_specs=[pl.BlockSpec((tm, tn), lambda i, j, kk: (i, j)) for _ in outs],
        out_shape=[jax.ShapeDtypeStruct((m, n), dt) for dt in outs],
        scratch_shapes=[] if nk == 1 else [pltpu.VMEM((tm, tn), F32)],
        compiler_params=_cparams(("parallel", "parallel", "arbitrary")),
    )(*[p[0] for p in pieces], b, *[e[0] for e in extras], *after)


def _ici_copies(psum_ref, recv_ref, s_sem, r_sem, axis, shard_shape):
    x, y, c = _me()
    hr, cw = shard_shape[0] // 2, shard_shape[1]
    pick = lambda sems, j: sems[j] if isinstance(sems, (list, tuple)) else sems.at[j]
    copies = []
    for j, (fx, fy) in enumerate(_CHIP_FLIPS):
        chip = 2 * (x ^ fx) + (y ^ fy)
        src = psum_ref.at[:, pl.ds(chip * cw, cw)] if axis == 1 else psum_ref.at[pl.ds(chip * hr, hr), :]
        copies.append(pltpu.make_async_remote_copy(
            src_ref=src, dst_ref=recv_ref.at[j], send_sem=pick(s_sem, j), recv_sem=pick(r_sem, j),
            device_id=(x ^ fx, y ^ fy, c), device_id_type=MESH))
    return copies


_HBM_SPEC = pl.BlockSpec(memory_space=pltpu.HBM)
_SEM_SPEC = pl.BlockSpec(memory_space=pltpu.SEMAPHORE)


def _split_ici_copies(names, p_refs, land_refs, sems):
    copies = []
    for i, n in enumerate(names):
        copies += _ici_copies(p_refs[i], land_refs[i], list(sems[6 * i:6 * i + 3]),
                              list(sems[6 * i + 3:6 * i + 6]), dict(BIG)[n], SHARD[n])
    return copies


def _ici_start(name, names, psums):
    nw, ns = len(names), 6 * len(names)
    lands = [lax.empty((3, SHARD[n][0] // 2, SHARD[n][1]), BF16) for n in names]

    def body(*refs):
        for cp in _split_ici_copies(names, refs[:nw], refs[nw:2 * nw], refs[2 * nw:2 * nw + ns]):
            cp.start()
        token = refs[-1]
        token[...] = jnp.zeros_like(token)

    res = pl.pallas_call(
        body, name=name,
        out_shape=(pltpu.SemaphoreType.DMA(()),) * ns
        + tuple(pltpu.HBM(a.shape, BF16) for a in list(psums) + lands)
        + (jax.ShapeDtypeStruct((8, 128), F32),),
        in_specs=(_HBM_SPEC,) * (2 * nw),
        out_specs=(_SEM_SPEC,) * ns + (_HBM_SPEC,) * (2 * nw) + (pl.BlockSpec(memory_space=pltpu.VMEM),),
        input_output_aliases={k: ns + k for k in range(2 * nw)},
        compiler_params=pltpu.CompilerParams(has_side_effects=pltpu.SideEffectType.DATAFLOW_SIDE_EFFECTING),
    )(*[pltpu.with_memory_space_constraint(a, pltpu.HBM) for a in list(psums) + lands])
    return res[:ns], res[ns:ns + nw], res[ns + nw:ns + 2 * nw], res[-1]


def _ici_wait(name, names, sems, p_thru, land_thru, after):
    nw, ns = len(names), 6 * len(names)

    def body(*refs):
        for cp in _split_ici_copies(names, refs[:nw], refs[nw:2 * nw], refs[2 * nw:2 * nw + ns]):
            cp.wait_send()
            cp.wait_recv()

    res = pl.pallas_call(
        body, name=name,
        out_shape=tuple(pltpu.HBM(a.shape, BF16) for a in list(p_thru) + list(land_thru)),
        in_specs=(_HBM_SPEC,) * (2 * nw) + (_SEM_SPEC,) * ns + (pl.BlockSpec(memory_space=pl.ANY),) * len(after),
        out_specs=(_HBM_SPEC,) * (2 * nw), input_output_aliases={k: k for k in range(2 * nw)},
        compiler_params=pltpu.CompilerParams(has_side_effects=pltpu.SideEffectType.DATAFLOW_SIDE_EFFECTING),
    )(*p_thru, *land_thru, *sems, *after)
    return res[:nw], res[nw:]


def _where_am_i():
    x, y, c = _me()
    return jnp.stack([c, 2 * x + y]).astype(I32)


def _sibling():
    x, y, c = _me()
    return (x, y, 1 - c)


N_SEND_SLOTS = 2


def _matmul_tn_pair(name, pos, a, b, m, n, k, shard_rows, *, tm, tn, tk):
    hr = shard_rows // 2
    tm, tn, tk = min(tm, hr), min(tn, n), min(tk, k)
    tph = hr // tm
    nt, nj, nk = (m // 2) // tm, n // tn, k // tk
    n_tiles = nt * nj

    def row_block(p, t, pos_ref):
        half = jnp.where(p == 0, 1 - pos_ref[0], pos_ref[0])
        return (t // tph) * (2 * tph) + half * tph + t % tph

    pieces = b if isinstance(b, list) else [(b, False)]
    starts = _piece_starts(pieces, tn)
    n_b = len(pieces)

    def kern(pos_ref, a_ref, *rest):
        b_refs = rest[:n_b]
        o_ref, acc_ref, send_buf, land_buf, s_sem, r_sem = rest[n_b:]
        p, t, j, kk = pl.program_id(0), pl.program_id(1), pl.program_id(2), pl.program_id(3)
        idx = t * nj + j
        sib = _sibling()

        def copy(i):
            return pltpu.make_async_remote_copy(
                src_ref=send_buf.at[i % N_SEND_SLOTS], dst_ref=land_buf.at[i], send_sem=s_sem.at[i],
                recv_sem=r_sem.at[i], device_id=sib, device_id_type=MESH)

        @pl.when(kk == 0)
        def _():
            acc_ref[...] = jnp.zeros_like(acc_ref)

        for q in range(n_b):
            @pl.when(jnp.logical_and(j >= starts[q], j < starts[q] + _piece_chunks(pieces[q], tn)))
            def _(q=q):
                acc_ref[...] += lax.dot_general(a_ref[...], b_refs[q][...], _TN, preferred_element_type=F32)

        @pl.when(jnp.logical_and(kk == nk - 1, p == 0))
        def _():
            @pl.when(idx >= N_SEND_SLOTS)
            def _():
                copy(idx - N_SEND_SLOTS).wait_send()

            send_buf[idx % N_SEND_SLOTS] = acc_ref[...].astype(BF16)
            copy(idx).start()

        @pl.when(jnp.logical_and(kk == nk - 1, p == 1))
        def _():
            copy(idx).wait_recv()
            o_ref[...] = (acc_ref[...] + land_buf[idx].astype(F32)).astype(BF16)

        @pl.when(jnp.logical_and(jnp.logical_and(p == 1, idx == n_tiles - 1), kk == nk - 1))
        def _():
            for i in range(max(n_tiles - N_SEND_SLOTS, 0), n_tiles):
                copy(i).wait_send()

    grid_spec = pltpu.PrefetchScalarGridSpec(
        num_scalar_prefetch=1, grid=(2, nt, nj, nk),
        in_specs=[pl.BlockSpec((tk, tm), lambda p, t, j, kk, pos_ref: (kk, row_block(p, t, pos_ref)))]
        + [_piece_spec(pc, tk, tn, st, lambda p, t, j, kk, pos_ref: kk, lambda p, t, j, kk, pos_ref: j)
           for pc, st in zip(pieces, starts)],
        out_specs=pl.BlockSpec((tm, tn), lambda p, t, j, kk, pos_ref: (p * t, p * j)),
        scratch_shapes=[pltpu.VMEM((tm, tn), F32), pltpu.VMEM((N_SEND_SLOTS, tm, tn), BF16),
                        pltpu.VMEM((n_tiles, tm, tn), BF16),
                        pltpu.SemaphoreType.DMA((n_tiles,)), pltpu.SemaphoreType.DMA((n_tiles,))])
    return pl.pallas_call(
        kern, name=name, grid_spec=grid_spec, out_shape=jax.ShapeDtypeStruct((m // 2, n), BF16),
        compiler_params=_cparams(("arbitrary",) * 4),
    )(pos, a, *[pc[0] for pc in pieces])


def _rope_tables():
    half = RET_DK // 2
    f32 = np.float32
    inv = np.power(f32(ROPE_BASE), -np.arange(half, dtype=f32) / f32(half)).astype(f32)
    ang = (np.arange(SEQ, dtype=f32)[:, None] * inv[None, :]).astype(f32)
    return jnp.asarray(np.cos(ang).astype(f32)), jnp.asarray(np.sin(ang).astype(f32))


def _decay_tables():
    c = RET_CHUNK
    f32 = np.float32
    log_g = np.log1p(-np.power(f32(2.0), f32(-5.0) - np.arange(RET_HEADS, dtype=f32))).astype(f32)
    idx = np.arange(c, dtype=f32)
    rel = idx[:, None] - idx[None, :]
    din = np.where(rel >= 0, np.exp(log_g[:, None, None] * np.maximum(rel, f32(0.0))), f32(0.0)).astype(f32)
    qd = np.exp(log_g[:, None] * (idx + f32(1.0))).astype(f32)[:, :, None]
    kd = np.exp(log_g[:, None] * (f32(c) - f32(1.0) - idx)).astype(f32)[:, :, None]
    cd = np.exp(log_g * f32(c)).astype(f32)
    return jnp.asarray(din), jnp.asarray(qd), jnp.asarray(kd), jnp.asarray(cd)


def _t5_bucket(dist):
    max_exact = REL_BUCKETS // 2
    d_f = jnp.maximum(dist, 1).astype(F32)
    large = max_exact + (jnp.log(d_f / max_exact) / math.log(REL_MAX_DIST / max_exact)
                         * (REL_BUCKETS - max_exact)).astype(I32)
    large = jnp.minimum(large, REL_BUCKETS - 1)
    return jnp.where(dist < max_exact, dist, large)


def _bucket_tables():
    qi = jnp.arange(ATT_BLK)[:, None]
    kj = jnp.arange(2 * ATT_BLK)[None, :]
    dist = jnp.clip(ATT_BLK + qi - kj, 0, ATT_BLK)
    return jnp.stack([_t5_bucket(dist * dil) for _, dil in ATT_GROUPS]).astype(I32)


def _permute_rows(t, dil):
    if dil == 1:
        return t
    s, w = t.shape
    return t.reshape(s // dil, dil, w).transpose(1, 0, 2).reshape(s, w)


def _unpermute_rows(t, dil):
    if dil == 1:
        return t
    s, w = t.shape
    return t.reshape(dil, s // dil, w).transpose(1, 0, 2).reshape(s, w)


def _retention_fwd(rqk, rv, din, qd, kd, cd):
    nc = SEQ // RET_CHUNK
    c, dk, dv = RET_CHUNK, RET_DK, RET_DV

    def kern(q_ref, k_ref, v_ref, din_ref, qd_ref, kd_ref, cd_ref, o_ref, st_ref, state):
        n = pl.program_id(0)

        @pl.when(n == 0)
        def _():
            state[...] = jnp.zeros_like(state)

        for h in range(RET_HEADS):
            q, k = q_ref[:, h * dk:(h + 1) * dk], k_ref[:, h * dk:(h + 1) * dk]
            v = v_ref[:, h * dv:(h + 1) * dv]
            s_b = state[h].astype(BF16)
            st_ref[h] = s_b
            a = lax.dot_general(q, k, _NT, preferred_element_type=F32) * din_ref[h]
            o = jnp.dot(a.astype(BF16), v, preferred_element_type=F32)
            o += jnp.dot(q, s_b, preferred_element_type=F32) * qd_ref[h]
            o_ref[:, h * dv:(h + 1) * dv] = o
            kk = (k.astype(F32) * kd_ref[h]).astype(BF16)
            state[h] = state[h] * cd_ref[h] + lax.dot_general(kk, v, _TN, preferred_element_type=F32)

    whole = lambda a: pl.BlockSpec(a.shape, lambda n: (0,) * a.ndim)
    return pl.pallas_call(
        kern, name="retention_fwd", grid=(nc,),
        in_specs=[
            pl.BlockSpec((c, RET_QK_W), lambda n: (n, 0)),
            pl.BlockSpec((c, RET_QK_W), lambda n: (n, 1)),
            pl.BlockSpec((c, RET_V_W), lambda n: (n, 0)),
            whole(din), whole(qd), whole(kd),
            pl.BlockSpec(memory_space=pltpu.SMEM),
        ],
        out_specs=[
            pl.BlockSpec((c, RET_V_W), lambda n: (n, 0)),
            pl.BlockSpec((RET_HEADS, None, dk, dv), lambda n: (0, n, 0, 0)),
        ],
        out_shape=[
            jax.ShapeDtypeStruct((SEQ, RET_V_W), F32),
            jax.ShapeDtypeStruct((RET_HEADS, nc, dk, dv), BF16),
        ],
        scratch_shapes=[pltpu.VMEM((RET_HEADS, dk, dv), F32)],
        compiler_params=_cparams(("arbitrary",)),
    )(rqk, rqk, rv, din, qd, kd, cd)


def _retention_bwd(rqk, rv, states, d_ro, din, qd, kd, cd, cos, sin):
    nc = SEQ // RET_CHUNK
    c, dk, dv = RET_CHUNK, RET_DK, RET_DV
    half = dk // 2
    last = nc - 1

    def unrot(g, cs, sn):
        g1, g2 = g[:, :half], g[:, half:]
        return jnp.concatenate([g1 * cs + g2 * sn, g2 * cs - g1 * sn], axis=-1)

    def kern(q_ref, k_ref, v_ref, st_ref, do_ref, din_ref, qd_ref, kd_ref, cd_ref, cos_ref, sin_ref,
             out_ref, dstate):
        step = pl.program_id(0)

        @pl.when(step == 0)
        def _():
            dstate[...] = jnp.zeros_like(dstate)

        cs, sn = cos_ref[...], sin_ref[...]
        for h in range(RET_HEADS):
            qk_cols, v_cols = slice(h * dk, (h + 1) * dk), slice(h * dv, (h + 1) * dv)
            q, k, v, s_b = q_ref[:, qk_cols], k_ref[:, qk_cols], v_ref[:, v_cols], st_ref[h]
            d_o = do_ref[:, v_cols]
            d_ob = d_o.astype(BF16)
            d_oq = (d_o * qd_ref[h]).astype(BF16)
            ds_b = dstate[h].astype(BF16)
            din_m = din_ref[h]
            a_b = (lax.dot_general(q, k, _NT, preferred_element_type=F32) * din_m).astype(BF16)
            kk = (k.astype(F32) * kd_ref[h]).astype(BF16)
            d_v = lax.dot_general(a_b, d_ob, _TN, preferred_element_type=F32)
            d_v += jnp.dot(kk, ds_b, preferred_element_type=F32)
            d_a = (lax.dot_general(d_ob, v, _NT, preferred_element_type=F32) * din_m).astype(BF16)
            d_q = jnp.dot(d_a, k, preferred_element_type=F32)
            d_q += lax.dot_general(d_oq, s_b, _NT, preferred_element_type=F32)
            d_k = lax.dot_general(d_a, q, _TN, preferred_element_type=F32)
            d_k += lax.dot_general(v, ds_b, _NT, preferred_element_type=F32) * kd_ref[h]
            dstate[h] = dstate[h] * cd_ref[h] + lax.dot_general(q, d_oq, _TN, preferred_element_type=F32)
            out_ref[:, h * dk:(h + 1) * dk] = unrot(d_q, cs, sn).astype(BF16)
            out_ref[:, RET_QK_W + h * dk:RET_QK_W + (h + 1) * dk] = (
                unrot(d_k, cs, sn) * (RET_DK ** -0.5)).astype(BF16)
            out_ref[:, 2 * RET_QK_W + h * dv:2 * RET_QK_W + (h + 1) * dv] = d_v.astype(BF16)

    whole = lambda a: pl.BlockSpec(a.shape, lambda n: (0,) * a.ndim)
    return pl.pallas_call(
        kern, name="retention_bwd", grid=(nc,),
        in_specs=[
            pl.BlockSpec((c, RET_QK_W), lambda n: (last - n, 0)),
            pl.BlockSpec((c, RET_QK_W), lambda n: (last - n, 1)),
            pl.BlockSpec((c, RET_V_W), lambda n: (last - n, 0)),
            pl.BlockSpec((RET_HEADS, None, dk, dv), lambda n: (0, last - n, 0, 0)),
            pl.BlockSpec((c, RET_V_W), lambda n: (last - n, 0)),
            whole(din), whole(qd), whole(kd),
            pl.BlockSpec(memory_space=pltpu.SMEM),
            pl.BlockSpec((c, half), lambda n: (last - n, 0)),
            pl.BlockSpec((c, half), lambda n: (last - n, 0)),
        ],
        out_specs=pl.BlockSpec((c, 2 * RET_QK_W + RET_V_W), lambda n: (last - n, 0)),
        out_shape=jax.ShapeDtypeStruct((SEQ, 2 * RET_QK_W + RET_V_W), BF16),
        scratch_shapes=[pltpu.VMEM((RET_HEADS, dk, dv), F32)],
        compiler_params=_cparams(("arbitrary",)),
    )(rqk, rqk, rv, states, d_ro, din, qd, kd, cd, cos, sin)


def _bias_build(rel_bias, buckets):
    ng = len(ATT_GROUPS)

    def kern(tab_ref, bkt_ref, o_ref):
        g, h = pl.program_id(0), pl.program_id(1)
        bkt = bkt_ref[...]
        acc = jnp.zeros(bkt.shape, F32)
        for b in range(REL_BUCKETS):
            acc = jnp.where(bkt == b, tab_ref[b, g * ATT_HPG + h], acc)
        o_ref[...] = acc

    return pl.pallas_call(
        kern, name="bias_build", grid=(ng, ATT_HPG),
        in_specs=[pl.BlockSpec(memory_space=pltpu.SMEM),
                  pl.BlockSpec((None, ATT_BLK, 2 * ATT_BLK), lambda g, h: (g, 0, 0))],
        out_specs=pl.BlockSpec((None, None, ATT_BLK, 2 * ATT_BLK), lambda g, h: (g, h, 0, 0)),
        out_shape=jax.ShapeDtypeStruct((ng, ATT_HPG, ATT_BLK, 2 * ATT_BLK), F32),
        compiler_params=_cparams(("arbitrary", "arbitrary")),
    )(rel_bias, buckets)


def _bias_grad(dsb, buckets):
    ng = len(ATT_GROUPS)

    def kern(ds_ref, bkt_ref, o_ref):
        g, h = pl.program_id(0), pl.program_id(1)
        bkt, ds = bkt_ref[...], ds_ref[...]
        for b in range(REL_BUCKETS):
            o_ref[b, g * ATT_HPG + h] = jnp.sum(jnp.where(bkt == b, ds, 0.0))

    return pl.pallas_call(
        kern, name="bias_grad", grid=(ng, ATT_HPG),
        in_specs=[pl.BlockSpec((None, None, ATT_BLK, 2 * ATT_BLK), lambda g, h: (g, h, 0, 0)),
                  pl.BlockSpec((None, ATT_BLK, 2 * ATT_BLK), lambda g, h: (g, 0, 0))],
        out_specs=pl.BlockSpec(memory_space=pltpu.SMEM),
        out_shape=jax.ShapeDtypeStruct((REL_BUCKETS, N_ATT_HEADS), F32),
        compiler_params=_cparams(("arbitrary", "arbitrary")),
    )(dsb, buckets)


_NT = (((1,), (1,)), ((), ()))
_TN = (((0,), (0,)), ((), ()))
_ATT_SCALE = ATT_DH ** -0.5


_PAD_ROWS = SEQ + ATT_BLK


def _window_mask(has_prev):
    qi = lax.broadcasted_iota(I32, (ATT_BLK, 2 * ATT_BLK), 0)
    kj = lax.broadcasted_iota(I32, (ATT_BLK, 2 * ATT_BLK), 1)
    prev_ok = jnp.logical_and(jnp.logical_and(kj < ATT_BLK, kj >= qi), has_prev)
    return jnp.logical_or(prev_ok, jnp.logical_and(kj >= ATT_BLK, qi >= kj - ATT_BLK))


def _head_specs(col0):
    return pl.BlockSpec((SEQ, ATT_DH), lambda h: (0, col0 + h))


def _att_fwd(gi, qkv, bias, nb):
    blk, dh = ATT_BLK, ATT_DH

    def kern(q_ref, k_ref, v_ref, b_ref, o_ref, l_ref, kpad, vpad):
        zero = jnp.zeros((blk, dh), BF16)
        kpad[0:blk, :] = zero
        vpad[0:blk, :] = zero
        kpad[blk:, :] = k_ref[...]
        vpad[blk:, :] = v_ref[...]
        bias_m = b_ref[...]

        def body(b, carry):
            r0 = pl.multiple_of(b * blk, blk)
            q = q_ref[pl.ds(r0, blk), :]
            kw = kpad[pl.ds(r0, 2 * blk), :]
            vw = vpad[pl.ds(r0, 2 * blk), :]
            valid = _window_mask((b % nb) > 0)
            s = lax.dot_general(q, kw, _NT, preferred_element_type=F32) * _ATT_SCALE + bias_m
            s = jnp.where(valid, s, -1e30)
            mx = jnp.max(s, axis=-1, keepdims=True)
            e = jnp.exp(s - mx)
            den = jnp.sum(e, axis=-1, keepdims=True)
            o_ref[pl.ds(r0, blk), :] = jnp.dot((e / den).astype(BF16), vw, preferred_element_type=F32)
            l_ref[pl.ds(r0, blk), :] = jnp.broadcast_to(mx + jnp.log(den), (blk, dh))
            return carry

        lax.fori_loop(0, N_BLK, body, 0, unroll=2)

    return pl.pallas_call(
        kern, name=f"att_fwd_g{gi}", grid=(ATT_HPG,),
        in_specs=[_head_specs(0), _head_specs(ATT_HPG), _head_specs(2 * ATT_HPG),
                  pl.BlockSpec((None, None, blk, 2 * blk), lambda h: (gi, h, 0, 0))],
        out_specs=[_head_specs(0), _head_specs(0)],
        out_shape=[jax.ShapeDtypeStruct((SEQ, ATT_W), F32), jax.ShapeDtypeStruct((SEQ, ATT_W), F32)],
        scratch_shapes=[pltpu.VMEM((_PAD_ROWS, dh), BF16), pltpu.VMEM((_PAD_ROWS, dh), BF16)],
        compiler_params=_cparams(("arbitrary",)),
    )(qkv, qkv, qkv, bias)


def _att_bwd(gi, qkv, d_att, lse, dd, bias, nb):
    blk, dh = ATT_BLK, ATT_DH

    def kern(q_ref, k_ref, v_ref, do_ref, l_ref, d_ref, b_ref, dqkv_ref, dsb_ref,
             kpad, vpad, qpad, dopad, lpad, dpad):
        zero = jnp.zeros((blk, dh), BF16)
        zero_f = jnp.zeros((blk, dh), F32)
        kpad[0:blk, :] = zero
        vpad[0:blk, :] = zero
        kpad[blk:, :] = k_ref[...]
        vpad[blk:, :] = v_ref[...]
        qpad[SEQ:, :] = zero
        dopad[SEQ:, :] = zero
        lpad[SEQ:, :] = zero_f
        dpad[SEQ:, :] = zero_f
        qpad[0:SEQ, :] = q_ref[...]
        dopad[0:SEQ, :] = do_ref[...]
        lpad[0:SEQ, :] = l_ref[...]
        dpad[0:SEQ, :] = d_ref[...]
        bias_m = b_ref[...]
        bias_t = jnp.concatenate([bias_m[:, blk:], bias_m[:, :blk]], axis=0)
        dsb_ref[...] = jnp.zeros_like(dsb_ref)

        def dq_body(b, carry):
            r0 = pl.multiple_of(b * blk, blk)
            q, d_o = q_ref[pl.ds(r0, blk), :], do_ref[pl.ds(r0, blk), :]
            kw, vw = kpad[pl.ds(r0, 2 * blk), :], vpad[pl.ds(r0, 2 * blk), :]
            lrow, drow = l_ref[pl.ds(r0, blk), :][:, :1], d_ref[pl.ds(r0, blk), :][:, :1]
            valid = _window_mask((b % nb) > 0)
            s = lax.dot_general(q, kw, _NT, preferred_element_type=F32) * _ATT_SCALE + bias_m
            p = jnp.where(valid, jnp.exp(jnp.where(valid, s, -1e30) - lrow), 0.0)
            dp = lax.dot_general(d_o, vw, _NT, preferred_element_type=F32)
            ds = p * (dp - drow)
            dq = jnp.dot(ds.astype(BF16), kw, preferred_element_type=F32)
            dqkv_ref[0, pl.ds(r0, blk), :] = (dq * _ATT_SCALE).astype(BF16)
            dsb_ref[...] += ds
            return carry

        lax.fori_loop(0, N_BLK, dq_body, 0, unroll=2)

        qi = lax.broadcasted_iota(I32, (2 * blk, blk), 0)
        kj = lax.broadcasted_iota(I32, (2 * blk, blk), 1)

        def dkv_body(b, carry):
            r0 = pl.multiple_of(b * blk, blk)
            k, v = k_ref[pl.ds(r0, blk), :], v_ref[pl.ds(r0, blk), :]
            qw, dow = qpad[pl.ds(r0, 2 * blk), :], dopad[pl.ds(r0, 2 * blk), :]
            lrow, drow = lpad[pl.ds(r0, 2 * blk), :][:, :1], dpad[pl.ds(r0, 2 * blk), :][:, :1]
            has_next = jnp.logical_and(b + 1 < N_BLK, ((b + 1) % nb) > 0)
            next_ok = jnp.logical_and(jnp.logical_and(qi >= blk, kj >= qi - blk), has_next)
            valid = jnp.logical_or(jnp.logical_and(qi < blk, qi >= kj), next_ok)
            s = lax.dot_general(qw, k, _NT, preferred_element_type=F32) * _ATT_SCALE + bias_t
            p = jnp.where(valid, jnp.exp(jnp.where(valid, s, -1e30) - lrow), 0.0)
            dp = lax.dot_general(dow, v, _NT, preferred_element_type=F32)
            ds = p * (dp - drow)
            d_v = lax.dot_general(p.astype(BF16), dow, _TN, preferred_element_type=F32)
            d_k = lax.dot_general(ds.astype(BF16), qw, _TN, preferred_element_type=F32)
            dqkv_ref[1, pl.ds(r0, blk), :] = (d_k * _ATT_SCALE).astype(BF16)
            dqkv_ref[2, pl.ds(r0, blk), :] = d_v.astype(BF16)
            return carry

        lax.fori_loop(0, N_BLK, dkv_body, 0, unroll=2)

    return pl.pallas_call(
        kern, name=f"att_bwd_g{gi}", grid=(ATT_HPG,),
        in_specs=[_head_specs(0), _head_specs(ATT_HPG), _head_specs(2 * ATT_HPG),
                  _head_specs(0), _head_specs(0), _head_specs(0),
                  pl.BlockSpec((None, None, blk, 2 * blk), lambda h: (gi, h, 0, 0))],
        out_specs=[pl.BlockSpec((3, SEQ, dh), lambda h: (0, 0, h)),
                   pl.BlockSpec((None, blk, 2 * blk), lambda h: (h, 0, 0))],
        out_shape=[jax.ShapeDtypeStruct((3, SEQ, ATT_W), BF16),
                   jax.ShapeDtypeStruct((ATT_HPG, blk, 2 * blk), F32)],
        scratch_shapes=[pltpu.VMEM((_PAD_ROWS, dh), BF16)] * 4 + [pltpu.VMEM((_PAD_ROWS, dh), F32)] * 2,
        compiler_params=_cparams(("arbitrary",)),
    )(qkv, qkv, qkv, d_att, lse, dd, bias)


def _rms_parts(x):
    r = lax.rsqrt(jnp.mean(x * x, axis=-1, keepdims=True) + RMS_EPS)
    return x * r, r


def _rms_bwd(d_xhat, xhat, r):
    return r * (d_xhat - xhat * jnp.mean(d_xhat * xhat, axis=-1, keepdims=True))


def _prenorm_fwd(name, x, gain, shift, scale):
    def body(xt, g, sh, sc):
        xhat, _ = _rms_parts(xt)
        return (xhat * g) * (1.0 + sc) + sh
    return _rowmap(name, body, [x], [gain, shift, scale], [(D_MODEL, BF16)])[0]


def _prenorm_bwd(name, d_hs, x, gain, scale, resid, after=()):
    n_dh = len(d_hs)

    def body(*args):
        d_h = args[0]
        for t in args[1:n_dh]:
            d_h = d_h + t
        xt, res, g, sc = args[n_dh:]
        xhat, r = _rms_parts(xt)
        nrm = xhat * g
        d_n = d_h * (1.0 + sc)
        dx = _rms_bwd(d_n * g, xhat, r) + res
        return (dx, jnp.sum(d_h, axis=0, keepdims=True), jnp.sum(d_h * nrm, axis=0, keepdims=True),
                jnp.sum(d_n * xhat, axis=0, keepdims=True))

    return _rowmap(name, body, list(d_hs) + [x, resid], [gain, scale], [(D_MODEL, F32)],
                   [D_MODEL, D_MODEL, D_MODEL], after=after)


def _gn_parts(ro):
    mu = jnp.mean(ro, axis=-1, keepdims=True)
    cen = ro - mu
    rstd = lax.rsqrt(jnp.mean(cen * cen, axis=-1, keepdims=True) + GN_EPS)
    return cen * rstd, rstd


def _retpost_fwd(ro, rg, gn_g, gn_b):
    def body(rot, rgt, g, b):
        outs = []
        for h in range(RET_HEADS):
            sl = slice(h * RET_DV, (h + 1) * RET_DV)
            nrm, _ = _gn_parts(rot[:, sl])
            gate = rgt[:, sl]
            outs.append((gate * _sigmoid(gate)) * (nrm * g[:, sl] + b[:, sl]))
        return jnp.concatenate(outs, axis=-1)
    return _rowmap("retpost_fwd", body, [ro, rg], [gn_g, gn_b], [(RET_V_W, BF16)])[0]


def _retpost_bwd(d_gated, ro, rg, gn_g, gn_b):
    def body(dgt, rot, rgt, g, b):
        d_ro, d_rg, d_g, d_b = [], [], [], []
        for h in range(RET_HEADS):
            sl = slice(h * RET_DV, (h + 1) * RET_DV)
            nrm, rstd = _gn_parts(rot[:, sl])
            gate, dg = rgt[:, sl], dgt[:, sl]
            sg = _sigmoid(gate)
            ron = nrm * g[:, sl] + b[:, sl]
            d_rg.append(dg * ron * (sg * (1.0 + gate * (1.0 - sg))))
            d_ron = dg * (gate * sg)
            d_g.append(jnp.sum(d_ron * nrm, axis=0, keepdims=True))
            d_b.append(jnp.sum(d_ron, axis=0, keepdims=True))
            d_n = d_ron * g[:, sl]
            d_ro.append(rstd * (d_n - jnp.mean(d_n, axis=-1, keepdims=True)
                                - nrm * jnp.mean(d_n * nrm, axis=-1, keepdims=True)))
        cat = lambda ts: jnp.concatenate(ts, axis=-1)
        return cat(d_ro), cat(d_rg), cat(d_g), cat(d_b)
    return _rowmap("retpost_bwd", body, [d_gated, ro, rg], [gn_g, gn_b],
                   [(RET_V_W, F32), (RET_V_W, BF16)], [RET_V_W, RET_V_W])


def _combine(os_, ls_):
    def body(o0, o1, o2, l0, l1, l2):
        mx = jnp.maximum(jnp.maximum(l0, l1), l2)
        e0, e1, e2 = jnp.exp(l0 - mx), jnp.exp(l1 - mx), jnp.exp(l2 - mx)
        den = e0 + e1 + e2
        att = (e0 / den) * o0 + (e1 / den) * o1 + (e2 / den) * o2
        return att, att, mx + jnp.log(den)
    return _rowmap("att_combine", body, list(os_) + list(ls_), [],
                   [(ATT_W, F32), (ATT_W, BF16), (ATT_W, F32)])


def _att_bwd_pre(d_att, att):
    def body(dt, at):
        outs = []
        for h in range(ATT_HPG):
            sl = slice(h * ATT_DH, (h + 1) * ATT_DH)
            outs.append(jnp.broadcast_to(jnp.sum(dt[:, sl] * at[:, sl], axis=-1, keepdims=True),
                                         (dt.shape[0], ATT_DH)))
        return dt, jnp.concatenate(outs, axis=-1)
    return _rowmap("att_bwd_pre", body, [d_att, att], [], [(ATT_W, BF16), (ATT_W, F32)])


def _merge_fwd(gates, ret_out, att_out):
    def body(gt, ro, ao):
        return _sigmoid(gt[:, :D_MODEL]) * ro + _sigmoid(gt[:, D_MODEL:]) * ao
    return _rowmap("merge_fwd", body, [gates, ret_out, att_out], [], [(D_MODEL, BF16)])[0]


def _merge_bwd(d_merged, gates, ret_out, att_out):
    def body(dm, gt, ro, ao):
        sa, sb = _sigmoid(gt[:, :D_MODEL]), _sigmoid(gt[:, D_MODEL:])
        d_gates = jnp.concatenate([dm * ro * (sa * (1.0 - sa)), dm * ao * (sb * (1.0 - sb))], axis=-1)
        return dm * sa, dm * sb, d_gates
    return _rowmap("merge_bwd", body, [d_merged, gates, ret_out, att_out], [],
                   [(D_MODEL, BF16), (D_MODEL, BF16), (2 * D_MODEL, BF16)])


def _gate_bwd(name, d_x, branch, gate):
    def body(dx, br, g):
        return dx * g, jnp.sum(dx * br, axis=0, keepdims=True)
    return _rowmap(name, body, [d_x, branch], [gate], [(D_MODEL, BF16)], [D_MODEL])


def _loss_head(x3, target, gain):
    def body(xt, tt, g):
        xhat, r = _rms_parts(xt)
        err = xhat * g - tt
        d_y = err / D_MODEL
        loss = 0.5 * jnp.sum(jnp.mean(err * err, axis=-1, keepdims=True), axis=0, keepdims=True)
        d_x = _rms_bwd(d_y * g, xhat, r)
        return d_x, jnp.broadcast_to(loss, (1, 128)), jnp.sum(d_y * xhat, axis=0, keepdims=True)
    return _rowmap("loss_head", body, [x3, target], [gain], [(D_MODEL, F32)], [128, D_MODEL])


def _local_step(pos, x, target, mod, norm1_g, norm2_g, norm_f_g, rel_bias, gn_g, gn_b, w_in, rest_gather):
    sh1, sc1, g1, sh2, sc2, g2 = [mod[:, i * D_MODEL:(i + 1) * D_MODEL] for i in range(6)]
    cos, sin = _rope_tables()
    din, qd, kd, cd = _decay_tables()
    buckets = _bucket_tables()
    bias = _bias_build(rel_bias, buckets)
    dils = [d for _, d in ATT_GROUPS]
    nbs = [SEQ // d // ATT_BLK for d in dils]

    h1 = _prenorm_fwd("prenorm1_fwd", x, norm1_g, sh1, sc1)
    h1_p = [_permute_rows(h1, d) for d in dils]

    def rot_epi(acc, cs, sn, scale):
        half = RET_DK // 2
        x1, x2 = acc[:, :half], acc[:, half:]
        return (jnp.concatenate([x1 * cs - x2 * sn, x1 * sn + x2 * cs], axis=-1) * scale,)

    qk_scale = jnp.concatenate([jnp.ones((1, RET_QK_W), F32),
                                jnp.full((1, RET_QK_W), RET_DK ** -0.5, F32)], axis=-1)
    rope_ex = [(cos, (TM, RET_DK // 2), lambda i, j, kk: (i, 0)),
               (sin, (TM, RET_DK // 2), lambda i, j, kk: (i, 0)),
               (qk_scale, (1, RET_DK), lambda i, j, kk: (0, j))]
    rest_sems, rest_shards, rest_fulls, rest_token = rest_gather
    behind = [rest_token]
    rv = _matmul("proj_rv", h1, w_in, "nn", SEQ, RET_V_W, D_MODEL, [BF16], b_off=OFF_V, tk=D_MODEL,
                 after=behind)[0]
    rg = _matmul("proj_rg", h1, w_in, "nn", SEQ, RET_V_W, D_MODEL, [F32], b_off=OFF_G, tk=D_MODEL,
                 after=behind)[0]
    gates = _matmul("proj_gates", h1, w_in, "nn", SEQ, 2 * D_MODEL, D_MODEL, [F32], b_off=OFF_GATE,
                    tn=512, tk=D_MODEL, after=behind)[0]
    aqkv = [_matmul(f"proj_att_g{gi}", h1_p[gi], w_in, "nn", SEQ, 3 * ATT_W, D_MODEL, [BF16],
                    b_off=OFF_ATT + gi * 3 * ATT_W, tn=512, tk=D_MODEL, after=behind)[0]
            for gi in range(3)]

    os_, ls_ = [], []
    for gi in range(3):
        o_g, l_g = _att_fwd(gi, aqkv[gi], bias, nbs[gi])
        os_.append(_unpermute_rows(o_g, dils[gi]))
        ls_.append(_unpermute_rows(l_g, dils[gi]))
        if gi == 1:
            rest_sems, rest_fulls, fwd_token = _gather_rest_forward(rest_sems, rest_shards, rest_fulls,
                                                                    [o_g, rv, rg, gates])

    rqk = _matmul("proj_qk", h1, w_in, "nn", SEQ, 2 * RET_QK_W, D_MODEL, [BF16], b_off=OFF_Q,
                  tn=RET_DK, tk=D_MODEL, epilogue=rot_epi, extras=rope_ex, after=[fwd_token])[0]
    ro, states = _retention_fwd(rqk, rv, din, qd, kd, cd)
    gated = _retpost_fwd(ro, rg, gn_g, gn_b)
    w_ret_out, w_att_out, w_o, w_ff1, w_ff2 = _gather_rest_end(rest_sems, rest_fulls, [gated, os_[2]])
    ret_out = _matmul("ret_out", gated, w_ret_out, "nn", SEQ, D_MODEL, RET_V_W, [F32])[0]
    att, att_b, lse = _combine(os_, ls_)
    att_out = _matmul("att_out", att_b, w_att_out, "nn", SEQ, D_MODEL, ATT_W, [F32])[0]

    merged = _merge_fwd(gates, ret_out, att_out)

    def resid_epi(acc, xt, g):
        return xt + g * acc, acc

    def resid_ex(xin, g):
        return [(xin, (TM, TN), lambda i, j, kk: (i, j)), (g, (1, TN), lambda i, j, kk: (0, j))]

    x2, mix = _matmul("mix_out", merged, w_o, "nn", SEQ, D_MODEL, D_MODEL, [F32, F32],
                      epilogue=resid_epi, extras=resid_ex(x, g1))
    h2 = _prenorm_fwd("prenorm2_fwd", x2, norm2_g, sh2, sc2)

    def relu2_epi(acc):
        r = jnp.maximum(acc, 0.0)
        return r * r, acc

    act, u = _matmul("ff1", h2, w_ff1, "nn", SEQ, D_FF, D_MODEL, [BF16, F32], tk=D_MODEL,
                     epilogue=relu2_epi)
    x3, y2 = _matmul("ff2", act, w_ff2, "nn", SEQ, D_MODEL, D_FF, [F32, F32],
                     epilogue=resid_epi, extras=resid_ex(x2, g2))

    d_x3, loss, d_gf = _loss_head(x3, target, norm_f_g)

    d_y2, d_g2 = _gate_bwd("ff_gate_bwd", d_x3, y2, g2)

    def relu2_bwd_epi(acc, ut):
        return (acc * (2.0 * jnp.maximum(ut, 0.0)),)

    gw_ff2 = _matmul_tn_pair("ff2_dw", pos, act, d_y2, D_FF, D_MODEL, SEQ, D_FF // N_CHIPS,
                             tm=512, tn=1024, tk=1024)
    d_u = _matmul("ff2_dx", d_y2, w_ff2, "nt", SEQ, D_FF, D_MODEL, [BF16], epilogue=relu2_bwd_epi,
                  extras=[(u, (TM, TN), lambda i, j, kk: (i, j))])[0]
    gw_ff1 = _matmul_tn_pair("ff1_dw", pos, h2, d_u, D_MODEL, D_FF, SEQ, D_MODEL,
                             tm=512, tn=1024, tk=1024)
    ffn = ["w_ff2", "w_ff1"]
    ffn_started = _ici_start("ici_start_ffn", ffn, [gw_ff2, gw_ff1])
    d_h2 = _matmul("ff1_dx", d_u, w_ff1, "nt", SEQ, D_MODEL, D_FF, [F32], after=[ffn_started[3]])[0]
    d_x2, d_sh2, d_sc2, d_n2g = _prenorm_bwd("prenorm2_bwd", [d_h2], x2, norm2_g, sc2, d_x3)

    d_mix, d_g1 = _gate_bwd("mix_gate_bwd", d_x2, mix, g1)
    gw_o = _matmul_tn_pair("mix_dw", pos, merged, d_mix, D_MODEL, D_MODEL, SEQ, D_MODEL // N_CHIPS,
                           tm=128, tn=1024, tk=2048)
    d_merged = _matmul("mix_dx", d_mix, w_o, "nt", SEQ, D_MODEL, D_MODEL, [F32])[0]
    d_ret_out, d_att_out, d_gates = _merge_bwd(d_merged, gates, ret_out, att_out)

    gw_ret_out = _matmul_tn_pair("ret_out_dw", pos, gated, d_ret_out, RET_V_W, D_MODEL, SEQ,
                                 RET_V_W // N_CHIPS, tm=256, tn=1024, tk=1024)
    gw_att_out = _matmul_tn_pair("att_out_dw", pos, att_b, d_att_out, ATT_W, D_MODEL, SEQ, ATT_W,
                                 tm=256, tn=1024, tk=2048)
    mixer = ["w_o", "w_ret_out", "w_att_out"]
    mixer_started = _ici_start("ici_start_mixer", mixer, [gw_o, gw_ret_out, gw_att_out])
    d_gated = _matmul("ret_out_dx", d_ret_out, w_ret_out, "nt", SEQ, RET_V_W, D_MODEL, [F32],
                      after=[mixer_started[3]])[0]
    d_att = _matmul("att_out_dx", d_att_out, w_att_out, "nt", SEQ, ATT_W, D_MODEL, [F32],
                    after=[mixer_started[3]])[0]

    d_ro, d_rg, d_gn_g, d_gn_b = _retpost_bwd(d_gated, ro, rg, gn_g, gn_b)
    d_rqkv = _retention_bwd(rqk, rv, states, d_ro, din, qd, kd, cd, cos, sin)

    d_att_b, dd = _att_bwd_pre(d_att, att)
    d_aqkv, dsbs = [], []
    for gi in range(3):
        da_p = _permute_rows(d_att_b, dils[gi])
        l_p = _permute_rows(lse, dils[gi])
        dd_p = _permute_rows(dd, dils[gi])
        dqkv, dsb = _att_bwd(gi, aqkv[gi], da_p, l_p, dd_p, bias, nbs[gi])
        if dils[gi] > 1:
            dqkv = dqkv.reshape(3, dils[gi], SEQ // dils[gi], ATT_W).transpose(0, 2, 1, 3).reshape(
                3, SEQ, ATT_W)
        d_aqkv.append(dqkv)
        dsbs.append(dsb)
    d_rel_bias = _bias_grad(jnp.stack(dsbs), buckets)

    d_proj = [(d_rqkv, False), (d_rg, False)] + [(t, True) for t in d_aqkv] + [(d_gates, False)]
    gw_in = _matmul_tn_pair("proj_dw", pos, h1, d_proj, D_MODEL, IN_COLS, SEQ, D_MODEL,
                            tm=512, tn=ATT_W, tk=SEQ)
    sems, (gw_in,), (land,), token = _ici_start("ici_start_w_in", ["w_in"], [gw_in])
    d_h1 = _matmul("proj_dx", d_proj, w_in, "nt", SEQ, D_MODEL, IN_COLS, [F32], tn=1024, tk=ATT_W,
                   after=[token])[0]
    pending = (sems, land)

    names = ffn + mixer
    psums, got = _ici_wait("ici_wait_rest", names, list(ffn_started[0]) + list(mixer_started[0]),
                           list(ffn_started[1]) + list(mixer_started[1]),
                           list(ffn_started[2]) + list(mixer_started[2]), [d_h1])
    g_big = {n: _final_sum("final_" + n, pos, dict(BIG)[n], psums[i], got[i], SHARD[n])
             for i, n in enumerate(names)}
    grad_x, d_sh1, d_sc1, d_n1g = _prenorm_bwd("prenorm1_bwd", [d_h1], x, norm1_g, sc1, d_x2,
                                               after=list(g_big.values()))
    d_mod = jnp.concatenate([d_sh1, d_sc1, d_g1, d_sh2, d_sc2, d_g2], axis=-1)
    small = dict(norm1_g=d_n1g, norm2_g=d_n2g, norm_f_g=d_gf, gn_g=d_gn_g, gn_b=d_gn_b,
                 rel_bias=d_rel_bias)
    return loss, grad_x, d_mod, small, g_big, (gw_in,) + pending


def _me():
    return lax.axis_index("x"), lax.axis_index("y"), lax.axis_index("c")


def _peer(x, y, c, mask):
    return (x ^ ((mask >> 2) & 1), y ^ ((mask >> 1) & 1), c ^ (mask & 1))


def _gather8(src_ref, dst_ref, send_sems, recv_sems):
    x, y, c = _me()
    me = 4 * x + 2 * y + c
    copies = []
    for mask in range(1, N_DEV):
        cp = pltpu.make_async_remote_copy(
            src_ref=src_ref, dst_ref=dst_ref.at[me], send_sem=send_sems.at[mask - 1],
            recv_sem=recv_sems.at[mask - 1], device_id=_peer(x, y, c, mask), device_id_type=MESH)
        cp.start()
        copies.append(cp)
    dst_ref[me] = src_ref[...]
    for cp in copies:
        cp.wait_recv()
    for cp in copies:
        cp.wait_send()


def _ada_fwd(c_in, w_ada, b_ada):
    ncol = ADA_COLS // N_CHIPS

    def body(c_ref, w_ref, b_ref, mod_ref, sc_ref, cbuf, cg, mbuf, mg, s1, r1, s2, r2):
        x, y, c = _me()
        me = 4 * x + 2 * y + c
        cv = c_ref[...]
        cbuf[...] = jnp.broadcast_to(cv * _sigmoid(cv), cbuf.shape)
        _gather8(cbuf, cg, s1, r1)
        rows = lax.broadcasted_iota(I32, (N_DEV, D_MODEL), 0)
        sc_all = jnp.zeros((N_DEV, D_MODEL), F32)
        for d in range(N_DEV):
            sc_all = jnp.where(rows == d, cg[d], sc_all)
        sc_ref[...] = sc_all
        mbuf[...] = jnp.dot(sc_all.astype(BF16), w_ref[...].astype(BF16), preferred_element_type=F32)
        _gather8(mbuf, mg, s2, r2)
        rowsel = lax.broadcasted_iota(I32, (N_DEV, ncol), 0) == me
        for k in range(N_CHIPS):
            blk = mg[2 * k]
            row = jnp.sum(jnp.where(rowsel, blk, 0.0), axis=0, keepdims=True)
            mod_ref[:, k * ncol:(k + 1) * ncol] = row + b_ref[:, k * ncol:(k + 1) * ncol]

    vm = pl.BlockSpec(memory_space=pltpu.VMEM)
    return pl.pallas_call(
        body, name="ada_fwd",
        in_specs=[vm, vm, vm], out_specs=[vm, vm],
        out_shape=[jax.ShapeDtypeStruct((1, ADA_COLS), F32), jax.ShapeDtypeStruct((N_DEV, D_MODEL), F32)],
        scratch_shapes=[
            pltpu.VMEM((8, D_MODEL), F32), pltpu.VMEM((N_DEV, 8, D_MODEL), F32),
            pltpu.VMEM((8, ncol), F32), pltpu.VMEM((N_DEV, 8, ncol), F32),
            pltpu.SemaphoreType.DMA((N_DEV - 1,)), pltpu.SemaphoreType.DMA((N_DEV - 1,)),
            pltpu.SemaphoreType.DMA((N_DEV - 1,)), pltpu.SemaphoreType.DMA((N_DEV - 1,)),
        ],
        compiler_params=pltpu.CompilerParams(vmem_limit_bytes=VMEM_LIMIT_V7X),
    )(c_in, w_ada, b_ada)


def _small_reduce(pack, sc_all):
    ncol = ADA_COLS // N_CHIPS

    def body(p_ref, sc_ref, tot_ref, gw_ref, pg, s1, r1):
        x, y, _ = _me()
        chip = 2 * x + y
        _gather8(p_ref, pg, s1, r1)
        tot = pg[0]
        for d in range(1, N_DEV):
            tot = tot + pg[d]
        tot_ref[...] = tot
        rows = lax.broadcasted_iota(I32, (N_DEV, ncol), 0)
        dmod = jnp.zeros((N_DEV, ncol), F32)
        for k in range(N_CHIPS):
            part = jnp.zeros((N_DEV, ncol), F32)
            for d in range(N_DEV):
                part = jnp.where(rows == d, pg[d, :, k * ncol:(k + 1) * ncol][0:1, :], part)
            dmod = jnp.where(chip == k, part, dmod)
        gw_ref[...] = lax.dot_general(sc_ref[...].astype(BF16), dmod.astype(BF16), _TN,
                                      preferred_element_type=F32)

    vm = pl.BlockSpec(memory_space=pltpu.VMEM)
    return pl.pallas_call(
        body, name="small_reduce",
        in_specs=[vm, vm], out_specs=[vm, vm],
        out_shape=[jax.ShapeDtypeStruct((8, ADA_COLS), F32), jax.ShapeDtypeStruct((D_MODEL, ncol), F32)],
        scratch_shapes=[pltpu.VMEM((N_DEV, 8, ADA_COLS), F32),
                        pltpu.SemaphoreType.DMA((N_DEV - 1,)), pltpu.SemaphoreType.DMA((N_DEV - 1,))],
        compiler_params=pltpu.CompilerParams(vmem_limit_bytes=VMEM_LIMIT_V7X),
    )(pack, sc_all)


BIG = (("w_in", 1), ("w_ret_out", 0), ("w_att_out", 1), ("w_o", 0), ("w_ff1", 1), ("w_ff2", 0))
SHARD = {"w_in": (D_MODEL, IN_COLS // N_CHIPS), "w_ret_out": (RET_V_W // N_CHIPS, D_MODEL),
         "w_att_out": (ATT_W, D_MODEL // N_CHIPS), "w_o": (D_MODEL // N_CHIPS, D_MODEL),
         "w_ff1": (D_MODEL, D_FF // N_CHIPS), "w_ff2": (D_FF // N_CHIPS, D_MODEL)}
_CHIP_FLIPS = ((1, 0), (0, 1), (1, 1))


def _region(ref, axis, chip, half, shard_shape):
    r, cw = shard_shape
    hr = r // 2
    if axis == 1:
        return ref.at[pl.ds(half * hr, hr), pl.ds(chip * cw, cw)]
    return ref.at[pl.ds(chip * r + half * hr, hr), :]


def _gather_weights(shards, n_remote):
    nw = len(BIG)
    shapes = [s.shape for s in shards]
    full_shapes = [(r, N_CHIPS * cw) if ax == 1 else (N_CHIPS * r, cw)
                   for (r, cw), (_, ax) in zip(shapes, BIG)]

    def body(*refs):
        ins, outs = refs[:nw], refs[nw:2 * nw]
        own = refs[2 * nw:3 * nw]
        from_ici, from_sib = refs[3 * nw:3 * nw + n_remote], refs[3 * nw + n_remote:3 * nw + 2 * n_remote]
        ld_sem, st_sem, s_ici, r_ici, s_d2d, r_d2d, st_a, st_b = refs[3 * nw + 2 * n_remote:]
        x, y, c = _me()
        chip = 2 * x + y
        sib = (x, y, 1 - c)
        loads = [pltpu.make_async_copy(ins[i], own[i], ld_sem.at[i]) for i in range(nw)]
        for cp in loads:
            cp.start()
        pending, first = [], []
        for i, (_, ax) in enumerate(BIG):
            r, cw = shapes[i]
            hr = r // 2
            loads[i].wait()
            dst = outs[i].at[:, pl.ds(chip * cw, cw)] if ax == 1 else outs[i].at[pl.ds(chip * r, r), :]
            cp = pltpu.make_async_copy(own[i], dst, st_sem.at[i])
            cp.start()
            pending.append(cp)
            for j, (fx, fy) in enumerate(_CHIP_FLIPS if i < n_remote else ()):
                rc = pltpu.make_async_remote_copy(
                    src_ref=own[i].at[pl.ds(c * hr, hr), :], dst_ref=from_ici[i].at[j],
                    send_sem=s_ici.at[j * nw + i], recv_sem=r_ici.at[j * nw + i],
                    device_id=(x ^ fx, y ^ fy, c), device_id_type=MESH)
                rc.start()
                first.append((j, i, rc))
        passed = []
        for j, i, rc in first:
            fx, fy = _CHIP_FLIPS[j]
            src_chip = 2 * (x ^ fx) + (y ^ fy)
            ax = BIG[i][1]
            rc.wait_recv()
            fw = pltpu.make_async_remote_copy(
                src_ref=from_ici[i].at[j], dst_ref=from_sib[i].at[j], send_sem=s_d2d.at[j * nw + i],
                recv_sem=r_d2d.at[j * nw + i], device_id=sib, device_id_type=MESH)
            fw.start()
            passed.append((j, i, src_chip, fw))
            st = pltpu.make_async_copy(from_ici[i].at[j], _region(outs[i], ax, src_chip, c, shapes[i]),
                                       st_a.at[j * nw + i])
            st.start()
            pending.append(st)
        for j, i, src_chip, fw in passed:
            fw.wait_recv()
            st = pltpu.make_async_copy(from_sib[i].at[j],
                                       _region(outs[i], BIG[i][1], src_chip, 1 - c, shapes[i]),
                                       st_b.at[j * nw + i])
            st.start()
            pending.append(st)
        for _, _, rc in first:
            rc.wait_send()
        for _, _, _, fw in passed:
            fw.wait_send()
        for cp in pending:
            cp.wait()

    hbm = pl.BlockSpec(memory_space=pl.ANY)
    halves = [pltpu.VMEM((3, r // 2, cw), BF16) for r, cw in shapes[:n_remote]]
    return pl.pallas_call(
        body, name="gather_weights",
        in_specs=[hbm] * nw, out_specs=[hbm] * nw,
        out_shape=[jax.ShapeDtypeStruct(fs, BF16) for fs in full_shapes],
        scratch_shapes=[pltpu.VMEM(sh, BF16) for sh in shapes] + halves + halves
        + [pltpu.SemaphoreType.DMA((nw,)), pltpu.SemaphoreType.DMA((nw,))]
        + [pltpu.SemaphoreType.DMA((3 * nw,))] * 6,
        compiler_params=pltpu.CompilerParams(vmem_limit_bytes=VMEM_LIMIT_V7X),
    )(*shards)


REST = BIG[1:]
_SIDE_EFFECTS = pltpu.CompilerParams(has_side_effects=pltpu.SideEffectType.DATAFLOW_SIDE_EFFECTING)
_ANY_SPEC = pl.BlockSpec(memory_space=pl.ANY)


def _rest_ici_copies(shard_refs, full_refs, sems):
    x, y, c = _me()
    chip = 2 * x + y
    n = 3 * len(REST)
    copies = []
    for i, (name, ax) in enumerate(REST):
        hr = SHARD[name][0] // 2
        for j, (fx, fy) in enumerate(_CHIP_FLIPS):
            copies.append(pltpu.make_async_remote_copy(
                src_ref=shard_refs[i].at[pl.ds(c * hr, hr), :],
                dst_ref=_region(full_refs[i], ax, chip, c, SHARD[name]),
                send_sem=sems[3 * i + j], recv_sem=sems[n + 3 * i + j],
                device_id=(x ^ fx, y ^ fy, c), device_id_type=MESH))
    return copies


def _rest_d2d_copies(full_refs, sems):
    x, y, c = _me()
    n = 3 * len(REST)
    copies = []
    for i, (name, ax) in enumerate(REST):
        for j, (fx, fy) in enumerate(_CHIP_FLIPS):
            reg = _region(full_refs[i], ax, 2 * (x ^ fx) + (y ^ fy), c, SHARD[name])
            copies.append(pltpu.make_async_remote_copy(
                src_ref=reg, dst_ref=reg, send_sem=sems[3 * i + j], recv_sem=sems[n + 3 * i + j],
                device_id=(x, y, 1 - c), device_id_type=MESH))
    return copies


def _gather_rest_start(shards, fulls, after):
    nr, ns, na = len(REST), 6 * len(REST), len(after)

    def body(*refs):
        for cp in _rest_ici_copies(refs[:nr], refs[nr:2 * nr], refs[2 * nr + na:2 * nr + na + ns]):
            cp.start()
        token = refs[-1]
        token[...] = jnp.zeros_like(token)

    hbm = lambda a: pltpu.HBM(a.shape, a.dtype)
    res = pl.pallas_call(
        body, name="gather_rest_start",
        out_shape=(pltpu.SemaphoreType.DMA(()),) * ns + tuple(hbm(a) for a in shards + fulls)
        + (jax.ShapeDtypeStruct((8, 128), F32),),
        in_specs=(_HBM_SPEC,) * (2 * nr) + (_ANY_SPEC,) * na,
        out_specs=(_SEM_SPEC,) * ns + (_HBM_SPEC,) * (2 * nr) + (pl.BlockSpec(memory_space=pltpu.VMEM),),
        input_output_aliases={k: ns + k for k in range(2 * nr)}, compiler_params=_SIDE_EFFECTS,
    )(*[pltpu.with_memory_space_constraint(a, pltpu.HBM) for a in shards + fulls], *after)
    return res[:ns], res[ns:ns + nr], res[ns + nr:ns + 2 * nr], res[-1]


def _gather_rest_forward(sems, shards, fulls, after):
    nr, ns = len(REST), 6 * len(REST)

    def body(*refs):
        shard_refs, full_refs, old = refs[:nr], refs[nr:2 * nr], refs[2 * nr:2 * nr + ns]
        new = refs[2 * nr + ns + len(after):2 * nr + 2 * ns + len(after)]
        for cp in _rest_ici_copies(shard_refs, full_refs, old):
            cp.wait_send()
            cp.wait_recv()
        for cp in _rest_d2d_copies(full_refs, new):
            cp.start()
        token = refs[-1]
        token[...] = jnp.zeros_like(token)

    res = pl.pallas_call(
        body, name="gather_rest_forward",
        out_shape=(pltpu.SemaphoreType.DMA(()),) * ns + tuple(pltpu.HBM(a.shape, a.dtype) for a in fulls)
        + (jax.ShapeDtypeStruct((8, 128), F32),),
        in_specs=(_HBM_SPEC,) * (2 * nr) + (_SEM_SPEC,) * ns + (_ANY_SPEC,) * len(after),
        out_specs=(_SEM_SPEC,) * ns + (_HBM_SPEC,) * nr + (pl.BlockSpec(memory_space=pltpu.VMEM),),
        input_output_aliases={nr + k: ns + k for k in range(nr)}, compiler_params=_SIDE_EFFECTS,
    )(*shards, *fulls, *sems, *after)
    return res[:ns], res[ns:ns + nr], res[-1]


def _gather_rest_end(sems, fulls, after):
    nr, ns = len(REST), 6 * len(REST)

    def body(*refs):
        for cp in _rest_d2d_copies(refs[:nr], refs[nr:nr + ns]):
            cp.wait_send()
            cp.wait_recv()

    return pl.pallas_call(
        body, name="gather_rest_end",
        out_shape=tuple(pltpu.HBM(a.shape, a.dtype) for a in fulls),
        in_specs=(_HBM_SPEC,) * nr + (_SEM_SPEC,) * ns + (_ANY_SPEC,) * len(after),
        out_specs=(_HBM_SPEC,) * nr,
        input_output_aliases={k: k for k in range(nr)}, compiler_params=_SIDE_EFFECTS,
    )(*fulls, *sems, *after)


def _adam_update(w, g, m, v):
    mn = ADAM_B1 * m + (1.0 - ADAM_B1) * g
    vn = ADAM_B2 * v + (1.0 - ADAM_B2) * (g * g)
    m_hat = mn / (1.0 - ADAM_B1 ** ADAM_STEP)
    v_hat = vn / (1.0 - ADAM_B2 ** ADAM_STEP)
    return -ADAM_LR * (m_hat / (jnp.sqrt(v_hat) + ADAM_EPS) + ADAM_WD * w), mn, vn


def _final_sum(name, pos, axis, psum, recv, shard_shape, after=(), tr=128):
    r, cw = shard_shape
    hr = r // 2
    tr = min(tr, hr)
    nt = hr // tr
    n_after = len(after)

    def kern(pos_ref, p_ref, r_ref, *rest):
        g_ref, send_buf, land_buf, s_sem, r_sem = rest[n_after:]
        p, t = pl.program_id(0), pl.program_id(1)
        sib = _sibling()

        def copy(i):
            return pltpu.make_async_remote_copy(
                src_ref=send_buf.at[i], dst_ref=land_buf.at[i], send_sem=s_sem.at[i],
                recv_sem=r_sem.at[i], device_id=sib, device_id_type=MESH)

        @pl.when(p == 0)
        def _():
            tot = p_ref[...].astype(F32)
            for j in range(3):
                tot = tot + r_ref[j].astype(F32)
            send_buf[t] = tot
            copy(t).start()
            g_ref[...] = tot

        @pl.when(p == 1)
        def _():
            copy(t).wait_recv()
            g_ref[...] = land_buf[t]

        @pl.when(jnp.logical_and(p == 1, t == nt - 1))
        def _():
            for i in range(nt):
                copy(i).wait_send()

    def shard_rows(p, t, pos_ref):
        return (jnp.where(p == 0, pos_ref[0], 1 - pos_ref[0]) * nt + t, 0)

    def own_part(p, t, pos_ref):
        tt = jnp.where(p == 0, t, nt - 1)
        return (tt, pos_ref[1]) if axis == 1 else (pos_ref[1] * nt + tt, 0)

    grid_spec = pltpu.PrefetchScalarGridSpec(
        num_scalar_prefetch=1, grid=(2, nt),
        in_specs=[pl.BlockSpec((tr, cw), own_part),
                  pl.BlockSpec((3, tr, cw), lambda p, t, pos_ref: (0, jnp.where(p == 0, t, nt - 1), 0))]
        + [pl.BlockSpec(memory_space=pl.ANY)] * n_after,
        out_specs=pl.BlockSpec((tr, cw), shard_rows),
        scratch_shapes=[pltpu.VMEM((nt, tr, cw), F32), pltpu.VMEM((nt, tr, cw), F32),
                        pltpu.SemaphoreType.DMA((nt,)), pltpu.SemaphoreType.DMA((nt,))])
    return pl.pallas_call(
        kern, name=name, grid_spec=grid_spec, out_shape=jax.ShapeDtypeStruct((r, cw), F32),
        compiler_params=_cparams(("arbitrary", "arbitrary")),
    )(pos, psum, recv, *after)


def _adamw(name, w, g, m, v):
    r, cw = w.shape
    tr = min(r, 128)

    def kern(w_ref, g_ref, m_ref, v_ref, go_ref, d_ref, nm_ref, nv_ref):
        gv = g_ref[...]
        go_ref[...] = gv
        d_ref[...], nm_ref[...], nv_ref[...] = _adam_update(w_ref[...], gv, m_ref[...], v_ref[...])

    spec = pl.BlockSpec((tr, cw), lambda i: (i, 0))
    return pl.pallas_call(
        kern, name=name, grid=(r // tr,), in_specs=[spec] * 4, out_specs=[spec] * 4,
        out_shape=[jax.ShapeDtypeStruct((r, cw), F32)] * 4, compiler_params=_cparams(("parallel",)),
    )(w, g, m, v)


_PACK_W = ADA_COLS
_NB = REL_BUCKETS * N_ATT_HEADS
_SMALL_SLOTS = {
    "b_ada": (0, 0, ADA_COLS),
    "norm1_g": (1, 0, D_MODEL), "norm2_g": (1, D_MODEL, D_MODEL), "norm_f_g": (1, 2 * D_MODEL, D_MODEL),
    "ret_gn_g": (1, 3 * D_MODEL, RET_V_W),
    "ret_gn_b": (2, 0, RET_V_W), "rel_bias": (2, RET_V_W, _NB), "loss": (2, RET_V_W + 512, 128),
}


def _pack_small(vals):
    rows = []
    for r in range(8):
        items = sorted([(off, n) for n, (rr, off, _) in _SMALL_SLOTS.items() if rr == r and n in vals])
        parts, pos = [], 0
        for off, n in items:
            if off > pos:
                parts.append(jnp.zeros((1, off - pos), F32))
            parts.append(vals[n].reshape(1, -1).astype(F32))
            pos = off + _SMALL_SLOTS[n][2]
        if pos < _PACK_W:
            parts.append(jnp.zeros((1, _PACK_W - pos), F32))
        rows.append(jnp.concatenate(parts, axis=-1))
    return jnp.concatenate(rows, axis=0)


def _unpack_small(pack, name):
    r, off, wd = _SMALL_SLOTS[name]
    return pack[r:r + 1, off:off + wd]


def kernel(x, c, w_ada, b_ada, norm1_g, w_in, rel_bias, ret_gn_g, ret_gn_b, w_ret_out, w_att_out, w_o, norm2_g, w_ff1, w_ff2, norm_f_g, loss_target, m_w_ada, m_b_ada, m_norm1_g, m_w_in, m_rel_bias, m_ret_gn_g, m_ret_gn_b, m_w_ret_out, m_w_att_out, m_w_o, m_norm2_g, m_w_ff1, m_w_ff2, m_norm_f_g, v_w_ada, v_b_ada, v_norm1_g, v_w_in, v_rel_bias, v_ret_gn_g, v_ret_gn_b, v_w_ret_out, v_w_att_out, v_w_o, v_norm2_g, v_w_ff1, v_w_ff2, v_norm_f_g):
    given = dict(locals())
    big_names = [n for n, _ in BIG]
    shard_w = {n: given[n][0] for n in big_names}
    assert all(shard_w[n].shape == SHARD[n] for n in big_names)

    shards_bf = [shard_w[n].astype(BF16) for n in big_names]
    full = _gather_weights(shards_bf, 1)
    mod, sc_all = _ada_fwd(c, w_ada[0], b_ada)
    rest_gather = _gather_rest_start(shards_bf[1:], list(full[1:]), [mod])
    pos = _where_am_i()

    loss, grad_x, d_mod, small, g_big, pending = _local_step(
        pos, x[0], loss_target[0], mod, norm1_g, norm2_g, norm_f_g.reshape(1, -1), rel_bias, ret_gn_g,
        ret_gn_b, full[0], rest_gather)

    pack_g = _pack_small(dict(b_ada=d_mod, norm1_g=small["norm1_g"], norm2_g=small["norm2_g"],
                              norm_f_g=small["norm_f_g"], ret_gn_g=small["gn_g"], ret_gn_b=small["gn_b"],
                              rel_bias=small["rel_bias"], loss=loss))
    tot, g_w_ada = _small_reduce(pack_g, sc_all)

    small_names = ["b_ada", "norm1_g", "rel_bias", "ret_gn_g", "ret_gn_b", "norm2_g", "norm_f_g"]
    pack_w = _pack_small({n: given[n] for n in small_names})
    pack_m = _pack_small({n: given["m_" + n] for n in small_names})
    pack_v = _pack_small({n: given["v_" + n] for n in small_names})
    _, sd, sm, sv = _adamw("adamw_small", pack_w, tot, pack_m, pack_v)

    grads, deltas, new_m, new_v = {}, {}, {}, {}
    for n in small_names:
        shp = given[n].shape
        grads[n] = _unpack_small(tot, n).reshape(shp)
        deltas[n] = _unpack_small(sd, n).reshape(shp)
        new_m[n] = _unpack_small(sm, n).reshape(shp)
        new_v[n] = _unpack_small(sv, n).reshape(shp)
    g_big["w_ada"] = g_w_ada
    for n in ["w_ada"] + big_names[1:] + big_names[:1]:
        if n == "w_in":
            gw_in, sems, land = pending
            done = [tot, sd] + [deltas[k] for k in ["w_ada"] + big_names[1:]]
            (gw_in,), (got,) = _ici_wait("ici_wait_w_in", [n], sems, [gw_in], [land], done)
            g_big[n] = _final_sum("final_w_in", pos, 1, gw_in, got, SHARD[n])
        g, d, nm, nv = _adamw("adamw_" + n, given[n][0], g_big[n], given["m_" + n][0], given["v_" + n][0])
        grads[n], deltas[n], new_m[n], new_v[n] = g[None], d[None], nm[None], nv[None]

    order = ["w_ada", "b_ada", "norm1_g", "w_in", "rel_bias", "ret_gn_g", "ret_gn_b", "w_ret_out",
             "w_att_out", "w_o", "norm2_g", "w_ff1", "w_ff2", "norm_f_g"]
    loss_out = _unpack_small(tot, "loss")[0, 0]
    return (loss_out, grad_x[None], *[grads[n] for n in order], *[deltas[n] for n in order],
            *[new_m[n] for n in order], *[new_v[n] for n in order])
```

```python
import functools
import math

import jax
import jax.numpy as jnp
import numpy as np
from jax import lax
from jax.experimental import pallas as pl
from jax.experimental.pallas import tpu as pltpu

F32 = jnp.float32
BF16 = jnp.bfloat16
I32 = jnp.int32

SEQ = 2048
D_MODEL = 1024
RET_HEADS = 4
RET_DK = 256
RET_DV = 512
RET_CHUNK = 128
RET_QK_W = RET_HEADS * RET_DK
RET_V_W = RET_HEADS * RET_DV
ATT_GROUPS = ((128, 1), (512, 4), (2048, 16))
ATT_HPG = 4
ATT_DH = 128
ATT_W = ATT_HPG * ATT_DH
ATT_BLK = 128
N_BLK = SEQ // ATT_BLK
REL_BUCKETS = 32
REL_MAX_DIST = 2048
N_ATT_HEADS = 12
D_FF = 4 * D_MODEL
RMS_EPS = 1e-6
GN_EPS = 1e-5
ROPE_BASE = 10000.0
IN_COLS = 2 * RET_QK_W + 2 * RET_V_W + 9 * ATT_W + 2 * D_MODEL
OFF_Q, OFF_K, OFF_V, OFF_G = 0, RET_QK_W, 2 * RET_QK_W, 2 * RET_QK_W + RET_V_W
OFF_ATT = 2 * RET_QK_W + 2 * RET_V_W
OFF_GATE = OFF_ATT + 9 * ATT_W
N_CHIPS = 4
N_DEV = 8
ADA_COLS = 6 * D_MODEL

ADAM_LR = 0.001
ADAM_B1 = 0.9
ADAM_B2 = 0.999
ADAM_EPS = 1e-08
ADAM_WD = 0.01
ADAM_STEP = 10

VMEM_LIMIT_V7X = 56 * 1024 * 1024
MESH = pl.DeviceIdType.MESH


def _cparams(sem):
    return pltpu.CompilerParams(dimension_semantics=sem, vmem_limit_bytes=VMEM_LIMIT_V7X)


def _sigmoid(v):
    return 1.0 / (1.0 + jnp.exp(-v))


def _rowmap(name, body, row_ins, bcast_ins, row_outs, sum_outs=(), tm=256, after=()):
    m = row_ins[0].shape[0]
    n_in = len(row_ins) + len(bcast_ins)
    n_ro = len(row_outs)

    def kern(*refs):
        vals = [r[...] for r in refs[:n_in]]
        res = body(*vals)
        if not isinstance(res, (tuple, list)):
            res = (res,)
        outs = refs[n_in + len(after):]
        for r, v in zip(outs[:n_ro], res[:n_ro]):
            r[...] = v.astype(r.dtype)
        if sum_outs:
            @pl.when(pl.program_id(0) == 0)
            def _():
                for r in outs[n_ro:]:
                    r[...] = jnp.zeros_like(r)
            for r, v in zip(outs[n_ro:], res[n_ro:]):
                r[...] += v

    in_specs = [pl.BlockSpec((tm, a.shape[1]), lambda i: (i, 0)) for a in row_ins]
    in_specs += [pl.BlockSpec(a.shape, lambda i: (0, 0)) for a in bcast_ins]
    in_specs += [pl.BlockSpec(memory_space=pl.ANY)] * len(after)
    out_specs = [pl.BlockSpec((tm, n), lambda i: (i, 0)) for n, _ in row_outs]
    out_specs += [pl.BlockSpec((1, n), lambda i: (0, 0)) for n in sum_outs]
    out_shape = [jax.ShapeDtypeStruct((m, n), dt) for n, dt in row_outs]
    out_shape += [jax.ShapeDtypeStruct((1, n), F32) for n in sum_outs]
    return pl.pallas_call(
        kern, name=name, grid=(m // tm,), in_specs=in_specs, out_specs=out_specs,
        out_shape=out_shape, compiler_params=_cparams(("arbitrary",)),
    )(*row_ins, *bcast_ins, *after)


TM, TN = 1024, 1024


def _piece_chunks(piece, width):
    arr, stacked = piece
    return arr.shape[0] if stacked else arr.shape[1] // width


def _piece_spec(piece, rows, width, start, row_of, chunk_of):
    arr, stacked = piece
    last = _piece_chunks(piece, width) - 1

    def local(*ids):
        return jnp.clip(chunk_of(*ids) - start, 0, last)

    def row(*ids):
        rel = chunk_of(*ids) - start
        return jnp.where(jnp.logical_and(rel >= 0, rel <= last), row_of(*ids), 0)

    if stacked:
        return pl.BlockSpec((None, rows, width), lambda *ids: (local(*ids), row(*ids), 0))
    return pl.BlockSpec((rows, width), lambda *ids: (row(*ids), local(*ids)))


def _piece_starts(pieces, width):
    return [sum(_piece_chunks(p, width) for p in pieces[:q]) for q in range(len(pieces))]


def _matmul(name, a, b, kind, m, n, k, outs, *, b_off=0, tm=TM, tn=TN, tk=1024,
            epilogue=None, extras=(), after=()):
    tm, tn, tk = min(tm, m), min(tn, n), min(tk, k)
    nk = k // tk
    pieces = a if isinstance(a, list) else [(a, False)]
    starts = _piece_starts(pieces, tk)
    if kind == "nn":
        a_specs = [pl.BlockSpec((tm, tk), lambda i, j, kk: (i, kk))]
        b_spec = pl.BlockSpec((tk, tn), lambda i, j, kk: (kk, b_off // tn + j))
        dn = (((1,), (0,)), ((), ()))
    elif kind == "nt":
        a_specs = [_piece_spec(p, tm, tk, st, lambda i, j, kk: i, lambda i, j, kk: kk)
                   for p, st in zip(pieces, starts)]
        b_spec = pl.BlockSpec((tn, tk), lambda i, j, kk: (j, b_off // tk + kk))
        dn = (((1,), (1,)), ((), ()))
    else:
        a_specs = [pl.BlockSpec((tk, tm), lambda i, j, kk: (kk, i))]
        b_spec = pl.BlockSpec((tk, tn), lambda i, j, kk: (kk, j))
        dn = (((0,), (0,)), ((), ()))
    n_a, n_ex, n_out = len(pieces), len(extras), len(outs)
    if epilogue is None:
        epilogue = lambda acc: (acc,)

    def finish(acc, ex_refs, out_refs):
        res = epilogue(acc, *[r[...] for r in ex_refs])
        for r, v in zip(out_refs, res):
            r[...] = v.astype(r.dtype)

    n_in = n_a + 1 + n_ex + len(after)

    def kern(*refs):
        a_refs, b_ref = refs[:n_a], refs[n_a]
        ex_refs = refs[n_a + 1:n_a + 1 + n_ex]
        out_refs = refs[n_in:n_in + n_out]
        kk = pl.program_id(2)
        dot = lambda a_ref: lax.dot_general(a_ref[...], b_ref[...], dn, preferred_element_type=F32)
        if nk == 1:
            finish(dot(a_refs[0]), ex_refs, out_refs)
            return
        acc_ref = refs[n_in + n_out]
        if n_a == 1:
            part = dot(a_refs[0])

            @pl.when(kk == 0)
            def _():
                acc_ref[...] = part

            @pl.when(kk > 0)
            def _():
                acc_ref[...] += part
        else:
            @pl.when(kk == 0)
            def _():
                acc_ref[...] = jnp.zeros_like(acc_ref)

            for q in range(n_a):
                @pl.when(jnp.logical_and(kk >= starts[q], kk < starts[q] + _piece_chunks(pieces[q], tk)))
                def _(q=q):
                    acc_ref[...] += dot(a_refs[q])

        @pl.when(kk == nk - 1)
        def _():
            finish(acc_ref[...], ex_refs, out_refs)

    in_specs = a_specs + [b_spec] + [pl.BlockSpec(bs, im) for _, bs, im in extras]
    in_specs += [pl.BlockSpec(memory_space=pl.ANY)] * len(after)
    return pl.pallas_call(
        kern, name=name, grid=(m // tm, n // tn, nk), in_specs=in_specs,
        out_specs=[pl.BlockSpec((tm, tn), lambda i, j, kk: (i, j)) for _ in outs],
        out_shape=[jax.ShapeDtypeStruct((m, n), dt) for dt in outs],
        scratch_shapes=[] if nk == 1 else [pltpu.VMEM((tm, tn), F32)],
        compiler_params=_cparams(("parallel", "parallel", "arbitrary")),
    )(*[p[0] for p in pieces], b, *[e[0] for e in extras], *after)


def _ici_copies(psum_ref, recv_ref, s_sem, r_sem, axis, shard_shape):
    x, y, c = _me()
    hr, cw = shard_shape[0] // 2, shard_shape[1]
    pick = lambda sems, j: sems[j] if isinstance(sems, (list, tuple)) else sems.at[j]
    copies = []
    for j, (fx, fy) in enumerate(_CHIP_FLIPS):
        chip = 2 * (x ^ fx) + (y ^ fy)
        src = psum_ref.at[:, pl.ds(chip * cw, cw)] if axis == 1 else psum_ref.at[pl.ds(chip * hr, hr), :]
        copies.append(pltpu.make_async_remote_copy(
            src_ref=src, dst_ref=recv_ref.at[j], send_sem=pick(s_sem, j), recv_sem=pick(r_sem, j),
            device_id=(x ^ fx, y ^ fy, c), device_id_type=MESH))
    return copies


_HBM_SPEC = pl.BlockSpec(memory_space=pltpu.HBM)
_SEM_SPEC = pl.BlockSpec(memory_space=pltpu.SEMAPHORE)


def _split_ici_copies(names, p_refs, land_refs, sems):
    copies = []
    for i, n in enumerate(names):
        copies += _ici_copies(p_refs[i], land_refs[i], list(sems[6 * i:6 * i + 3]),
                              list(sems[6 * i + 3:6 * i + 6]), dict(BIG)[n], SHARD[n])
    return copies


def _ici_start(name, names, psums):
    nw, ns = len(names), 6 * len(names)
    lands = [lax.empty((3, SHARD[n][0] // 2, SHARD[n][1]), BF16) for n in names]

    def body(*refs):
        for cp in _split_ici_copies(names, refs[:nw], refs[nw:2 * nw], refs[2 * nw:2 * nw + ns]):
            cp.start()
        token = refs[-1]
        token[...] = jnp.zeros_like(token)

    res = pl.pallas_call(
        body, name=name,
        out_shape=(pltpu.SemaphoreType.DMA(()),) * ns
        + tuple(pltpu.HBM(a.shape, BF16) for a in list(psums) + lands)
        + (jax.ShapeDtypeStruct((8, 128), F32),),
        in_specs=(_HBM_SPEC,) * (2 * nw),
        out_specs=(_SEM_SPEC,) * ns + (_HBM_SPEC,) * (2 * nw) + (pl.BlockSpec(memory_space=pltpu.VMEM),),
        input_output_aliases={k: ns + k for k in range(2 * nw)},
        compiler_params=pltpu.CompilerParams(has_side_effects=pltpu.SideEffectType.DATAFLOW_SIDE_EFFECTING),
    )(*[pltpu.with_memory_space_constraint(a, pltpu.HBM) for a in list(psums) + lands])
    return res[:ns], res[ns:ns + nw], res[ns + nw:ns + 2 * nw], res[-1]


def _ici_wait(name, names, sems, p_thru, land_thru, after):
    nw, ns = len(names), 6 * len(names)

    def body(*refs):
        for cp in _split_ici_copies(names, refs[:nw], refs[nw:2 * nw], refs[2 * nw:2 * nw + ns]):
            cp.wait_send()
            cp.wait_recv()

    res = pl.pallas_call(
        body, name=name,
        out_shape=tuple(pltpu.HBM(a.shape, BF16) for a in list(p_thru) + list(land_thru)),
        in_specs=(_HBM_SPEC,) * (2 * nw) + (_SEM_SPEC,) * ns + (pl.BlockSpec(memory_space=pl.ANY),) * len(after),
        out_specs=(_HBM_SPEC,) * (2 * nw), input_output_aliases={k: k for k in range(2 * nw)},
        compiler_params=pltpu.CompilerParams(has_side_effects=pltpu.SideEffectType.DATAFLOW_SIDE_EFFECTING),
    )(*p_thru, *land_thru, *sems, *after)
    return res[:nw], res[nw:]


def _where_am_i():
    x, y, c = _me()
    return jnp.stack([c, 2 * x + y]).astype(I32)


def _sibling():
    x, y, c = _me()
    return (x, y, 1 - c)


N_SEND_SLOTS = 2


def _matmul_tn_pair(name, pos, a, b, m, n, k, shard_rows, *, tm, tn, tk):
    hr = shard_rows // 2
    tm, tn, tk = min(tm, hr), min(tn, n), min(tk, k)
    tph = hr // tm
    nt, nj, nk = (m // 2) // tm, n // tn, k // tk
    n_tiles = nt * nj

    def row_block(p, t, pos_ref):
        half = jnp.where(p == 0, 1 - pos_ref[0], pos_ref[0])
        return (t // tph) * (2 * tph) + half * tph + t % tph

    pieces = b if isinstance(b, list) else [(b, False)]
    starts = _piece_starts(pieces, tn)
    n_b = len(pieces)

    def kern(pos_ref, a_ref, *rest):
        b_refs = rest[:n_b]
        o_ref, acc_ref, send_buf, land_buf, s_sem, r_sem = rest[n_b:]
        p, t, j, kk = pl.program_id(0), pl.program_id(1), pl.program_id(2), pl.program_id(3)
        idx = t * nj + j
        sib = _sibling()

        def copy(i):
            return pltpu.make_async_remote_copy(
                src_ref=send_buf.at[i % N_SEND_SLOTS], dst_ref=land_buf.at[i], send_sem=s_sem.at[i],
                recv_sem=r_sem.at[i], device_id=sib, device_id_type=MESH)

        @pl.when(kk == 0)
        def _():
            acc_ref[...] = jnp.zeros_like(acc_ref)

        for q in range(n_b):
            @pl.when(jnp.logical_and(j >= starts[q], j < starts[q] + _piece_chunks(pieces[q], tn)))
            def _(q=q):
                acc_ref[...] += lax.dot_general(a_ref[...], b_refs[q][...], _TN, preferred_element_type=F32)

        @pl.when(jnp.logical_and(kk == nk - 1, p == 0))
        def _():
            @pl.when(idx >= N_SEND_SLOTS)
            def _():
                copy(idx - N_SEND_SLOTS).wait_send()

            send_buf[idx % N_SEND_SLOTS] = acc_ref[...].astype(BF16)
            copy(idx).start()

        @pl.when(jnp.logical_and(kk == nk - 1, p == 1))
        def _():
            copy(idx).wait_recv()
            o_ref[...] = (acc_ref[...] + land_buf[idx].astype(F32)).astype(BF16)

        @pl.when(jnp.logical_and(jnp.logical_and(p == 1, idx == n_tiles - 1), kk == nk - 1))
        def _():
            for i in range(max(n_tiles - N_SEND_SLOTS, 0), n_tiles):
                copy(i).wait_send()

    grid_spec = pltpu.PrefetchScalarGridSpec(
        num_scalar_prefetch=1, grid=(2, nt, nj, nk),
        in_specs=[pl.BlockSpec((tk, tm), lambda p, t, j, kk, pos_ref: (kk, row_block(p, t, pos_ref)))]
        + [_piece_spec(pc, tk, tn, st, lambda p, t, j, kk, pos_ref: kk, lambda p, t, j, kk, pos_ref: j)
           for pc, st in zip(pieces, starts)],
        out_specs=pl.BlockSpec((tm, tn), lambda p, t, j, kk, pos_ref: (p * t, p * j)),
        scratch_shapes=[pltpu.VMEM((tm, tn), F32), pltpu.VMEM((N_SEND_SLOTS, tm, tn), BF16),
                        pltpu.VMEM((n_tiles, tm, tn), BF16),
                        pltpu.SemaphoreType.DMA((n_tiles,)), pltpu.SemaphoreType.DMA((n_tiles,))])
    return pl.pallas_call(
        kern, name=name, grid_spec=grid_spec, out_shape=jax.ShapeDtypeStruct((m // 2, n), BF16),
        compiler_params=_cparams(("arbitrary",) * 4),
    )(pos, a, *[pc[0] for pc in pieces])


def _rope_tables():
    half = RET_DK // 2
    f32 = np.float32
    inv = np.power(f32(ROPE_BASE), -np.arange(half, dtype=f32) / f32(half)).astype(f32)
    ang = (np.arange(SEQ, dtype=f32)[:, None] * inv[None, :]).astype(f32)
    return jnp.asarray(np.cos(ang).astype(f32)), jnp.asarray(np.sin(ang).astype(f32))


def _decay_tables():
    c = RET_CHUNK
    f32 = np.float32
    log_g = np.log1p(-np.power(f32(2.0), f32(-5.0) - np.arange(RET_HEADS, dtype=f32))).astype(f32)
    idx = np.arange(c, dtype=f32)
    rel = idx[:, None] - idx[None, :]
    din = np.where(rel >= 0, np.exp(log_g[:, None, None] * np.maximum(rel, f32(0.0))), f32(0.0)).astype(f32)
    qd = np.exp(log_g[:, None] * (idx + f32(1.0))).astype(f32)[:, :, None]
    kd = np.exp(log_g[:, None] * (f32(c) - f32(1.0) - idx)).astype(f32)[:, :, None]
    cd = np.exp(log_g * f32(c)).astype(f32)
    return jnp.asarray(din), jnp.asarray(qd), jnp.asarray(kd), jnp.asarray(cd)


def _t5_bucket(dist):
    max_exact = REL_BUCKETS // 2
    d_f = jnp.maximum(dist, 1).astype(F32)
    large = max_exact + (jnp.log(d_f / max_exact) / math.log(REL_MAX_DIST / max_exact)
                         * (REL_BUCKETS - max_exact)).astype(I32)
    large = jnp.minimum(large, REL_BUCKETS - 1)
    return jnp.where(dist < max_exact, dist, large)


def _bucket_tables():
    qi = jnp.arange(ATT_BLK)[:, None]
    kj = jnp.arange(2 * ATT_BLK)[None, :]
    dist = jnp.clip(ATT_BLK + qi - kj, 0, ATT_BLK)
    return jnp.stack([_t5_bucket(dist * dil) for _, dil in ATT_GROUPS]).astype(I32)


def _permute_rows(t, dil):
    if dil == 1:
        return t
    s, w = t.shape
    return t.reshape(s // dil, dil, w).transpose(1, 0, 2).reshape(s, w)


def _unpermute_rows(t, dil):
    if dil == 1:
        return t
    s, w = t.shape
    return t.reshape(dil, s // dil, w).transpose(1, 0, 2).reshape(s, w)


def _retention_fwd(rqk, rv, din, qd, kd, cd):
    nc = SEQ // RET_CHUNK
    c, dk, dv = RET_CHUNK, RET_DK, RET_DV

    def kern(q_ref, k_ref, v_ref, din_ref, qd_ref, kd_ref, cd_ref, o_ref, st_ref, state):
        n = pl.program_id(0)

        @pl.when(n == 0)
        def _():
            state[...] = jnp.zeros_like(state)

        for h in range(RET_HEADS):
            q, k = q_ref[:, h * dk:(h + 1) * dk], k_ref[:, h * dk:(h + 1) * dk]
            v = v_ref[:, h * dv:(h + 1) * dv]
            s_b = state[h].astype(BF16)
            st_ref[h] = s_b
            a = lax.dot_general(q, k, _NT, preferred_element_type=F32) * din_ref[h]
            o = jnp.dot(a.astype(BF16), v, preferred_element_type=F32)
            o += jnp.dot(q, s_b, preferred_element_type=F32) * qd_ref[h]
            o_ref[:, h * dv:(h + 1) * dv] = o
            kk = (k.astype(F32) * kd_ref[h]).astype(BF16)
            state[h] = state[h] * cd_ref[h] + lax.dot_general(kk, v, _TN, preferred_element_type=F32)

    whole = lambda a: pl.BlockSpec(a.shape, lambda n: (0,) * a.ndim)
    return pl.pallas_call(
        kern, name="retention_fwd", grid=(nc,),
        in_specs=[
            pl.BlockSpec((c, RET_QK_W), lambda n: (n, 0)),
            pl.BlockSpec((c, RET_QK_W), lambda n: (n, 1)),
            pl.BlockSpec((c, RET_V_W), lambda n: (n, 0)),
            whole(din), whole(qd), whole(kd),
            pl.BlockSpec(memory_space=pltpu.SMEM),
        ],
        out_specs=[
            pl.BlockSpec((c, RET_V_W), lambda n: (n, 0)),
            pl.BlockSpec((RET_HEADS, None, dk, dv), lambda n: (0, n, 0, 0)),
        ],
        out_shape=[
            jax.ShapeDtypeStruct((SEQ, RET_V_W), F32),
            jax.ShapeDtypeStruct((RET_HEADS, nc, dk, dv), BF16),
        ],
        scratch_shapes=[pltpu.VMEM((RET_HEADS, dk, dv), F32)],
        compiler_params=_cparams(("arbitrary",)),
    )(rqk, rqk, rv, din, qd, kd, cd)


def _retention_bwd(rqk, rv, states, d_ro, din, qd, kd, cd, cos, sin):
    nc = SEQ // RET_CHUNK
    c, dk, dv = RET_CHUNK, RET_DK, RET_DV
    half = dk // 2
    last = nc - 1

    def unrot(g, cs, sn):
        g1, g2 = g[:, :half], g[:, half:]
        return jnp.concatenate([g1 * cs + g2 * sn, g2 * cs - g1 * sn], axis=-1)

    def kern(q_ref, k_ref, v_ref, st_ref, do_ref, din_ref, qd_ref, kd_ref, cd_ref, cos_ref, sin_ref,
             out_ref, dstate):
        step = pl.program_id(0)

        @pl.when(step == 0)
        def _():
            dstate[...] = jnp.zeros_like(dstate)

        cs, sn = cos_ref[...], sin_ref[...]
        for h in range(RET_HEADS):
            qk_cols, v_cols = slice(h * dk, (h + 1) * dk), slice(h * dv, (h + 1) * dv)
            q, k, v, s_b = q_ref[:, qk_cols], k_ref[:, qk_cols], v_ref[:, v_cols], st_ref[h]
            d_o = do_ref[:, v_cols]
            d_ob = d_o.astype(BF16)
            d_oq = (d_o * qd_ref[h]).astype(BF16)
            ds_b = dstate[h].astype(BF16)
            din_m = din_ref[h]
            a_b = (lax.dot_general(q, k, _NT, preferred_element_type=F32) * din_m).astype(BF16)
            kk = (k.astype(F32) * kd_ref[h]).astype(BF16)
            d_v = lax.dot_general(a_b, d_ob, _TN, preferred_element_type=F32)
            d_v += jnp.dot(kk, ds_b, preferred_element_type=F32)
            d_a = (lax.dot_general(d_ob, v, _NT, preferred_element_type=F32) * din_m).astype(BF16)
            d_q = jnp.dot(d_a, k, preferred_element_type=F32)
            d_q += lax.dot_general(d_oq, s_b, _NT, preferred_element_type=F32)
            d_k = lax.dot_general(d_a, q, _TN, preferred_element_type=F32)
            d_k += lax.dot_general(v, ds_b, _NT, preferred_element_type=F32) * kd_ref[h]
            dstate[h] = dstate[h] * cd_ref[h] + lax.dot_general(q, d_oq, _TN, preferred_element_type=F32)
            out_ref[:, h * dk:(h + 1) * dk] = unrot(d_q, cs, sn).astype(BF16)
            out_ref[:, RET_QK_W + h * dk:RET_QK_W + (h + 1) * dk] = (
                unrot(d_k, cs, sn) * (RET_DK ** -0.5)).astype(BF16)
            out_ref[:, 2 * RET_QK_W + h * dv:2 * RET_QK_W + (h + 1) * dv] = d_v.astype(BF16)

    whole = lambda a: pl.BlockSpec(a.shape, lambda n: (0,) * a.ndim)
    return pl.pallas_call(
        kern, name="retention_bwd", grid=(nc,),
        in_specs=[
            pl.BlockSpec((c, RET_QK_W), lambda n: (last - n, 0)),
            pl.BlockSpec((c, RET_QK_W), lambda n: (last - n, 1)),
            pl.BlockSpec((c, RET_V_W), lambda n: (last - n, 0)),
            pl.BlockSpec((RET_HEADS, None, dk, dv), lambda n: (0, last - n, 0, 0)),
            pl.BlockSpec((c, RET_V_W), lambda n: (last - n, 0)),
            whole(din), whole(qd), whole(kd),
            pl.BlockSpec(memory_space=pltpu.SMEM),
            pl.BlockSpec((c, half), lambda n: (last - n, 0)),
            pl.BlockSpec((c, half), lambda n: (last - n, 0)),
        ],
        out_specs=pl.BlockSpec((c, 2 * RET_QK_W + RET_V_W), lambda n: (last - n, 0)),
        out_shape=jax.ShapeDtypeStruct((SEQ, 2 * RET_QK_W + RET_V_W), BF16),
        scratch_shapes=[pltpu.VMEM((RET_HEADS, dk, dv), F32)],
        compiler_params=_cparams(("arbitrary",)),
    )(rqk, rqk, rv, states, d_ro, din, qd, kd, cd, cos, sin)


def _bias_build(rel_bias, buckets):
    ng = len(ATT_GROUPS)

    def kern(tab_ref, bkt_ref, o_ref):
        g, h = pl.program_id(0), pl.program_id(1)
        bkt = bkt_ref[...]
        acc = jnp.zeros(bkt.shape, F32)
        for b in range(REL_BUCKETS):
            acc = jnp.where(bkt == b, tab_ref[b, g * ATT_HPG + h], acc)
        o_ref[...] = acc

    return pl.pallas_call(
        kern, name="bias_build", grid=(ng, ATT_HPG),
        in_specs=[pl.BlockSpec(memory_space=pltpu.SMEM),
                  pl.BlockSpec((None, ATT_BLK, 2 * ATT_BLK), lambda g, h: (g, 0, 0))],
        out_specs=pl.BlockSpec((None, None, ATT_BLK, 2 * ATT_BLK), lambda g, h: (g, h, 0, 0)),
        out_shape=jax.ShapeDtypeStruct((ng, ATT_HPG, ATT_BLK, 2 * ATT_BLK), F32),
        compiler_params=_cparams(("arbitrary", "arbitrary")),
    )(rel_bias, buckets)


def _bias_grad(dsb, buckets):
    ng = len(ATT_GROUPS)

    def kern(ds_ref, bkt_ref, o_ref):
        g, h = pl.program_id(0), pl.program_id(1)
        bkt, ds = bkt_ref[...], ds_ref[...]
        for b in range(REL_BUCKETS):
            o_ref[b, g * ATT_HPG + h] = jnp.sum(jnp.where(bkt == b, ds, 0.0))

    return pl.pallas_call(
        kern, name="bias_grad", grid=(ng, ATT_HPG),
        in_specs=[pl.BlockSpec((None, None, ATT_BLK, 2 * ATT_BLK), lambda g, h: (g, h, 0, 0)),
                  pl.BlockSpec((None, ATT_BLK, 2 * ATT_BLK), lambda g, h: (g, 0, 0))],
        out_specs=pl.BlockSpec(memory_space=pltpu.SMEM),
        out_shape=jax.ShapeDtypeStruct((REL_BUCKETS, N_ATT_HEADS), F32),
        compiler_params=_cparams(("arbitrary", "arbitrary")),
    )(dsb, buckets)


_NT = (((1,), (1,)), ((), ()))
_TN = (((0,), (0,)), ((), ()))
_ATT_SCALE = ATT_DH ** -0.5


_PAD_ROWS = SEQ + ATT_BLK
ATT_UNROLL = 16


def _window_mask(has_prev):
    qi = lax.broadcasted_iota(I32, (ATT_BLK, 2 * ATT_BLK), 0)
    kj = lax.broadcasted_iota(I32, (ATT_BLK, 2 * ATT_BLK), 1)
    prev_ok = jnp.logical_and(jnp.logical_and(kj < ATT_BLK, kj >= qi), has_prev)
    return jnp.logical_or(prev_ok, jnp.logical_and(kj >= ATT_BLK, qi >= kj - ATT_BLK))


def _head_specs(col0):
    return pl.BlockSpec((SEQ, ATT_DH), lambda h: (0, col0 + h))


def _att_fwd(gi, qkv, bias, nb):
    blk, dh = ATT_BLK, ATT_DH

    def kern(q_ref, k_ref, v_ref, b_ref, o_ref, l_ref, kpad, vpad):
        zero = jnp.zeros((blk, dh), BF16)
        kpad[0:blk, :] = zero
        vpad[0:blk, :] = zero
        kpad[blk:, :] = k_ref[...]
        vpad[blk:, :] = v_ref[...]
        bias_m = b_ref[...]

        def body(b, carry):
            r0 = pl.multiple_of(b * blk, blk)
            q = q_ref[pl.ds(r0, blk), :]
            kw = kpad[pl.ds(r0, 2 * blk), :]
            vw = vpad[pl.ds(r0, 2 * blk), :]
            valid = _window_mask((b % nb) > 0)
            s = lax.dot_general(q, kw, _NT, preferred_element_type=F32) * _ATT_SCALE + bias_m
            s = jnp.where(valid, s, -1e30)
            mx = jnp.max(s, axis=-1, keepdims=True)
            e = jnp.exp(s - mx)
            den = jnp.sum(e, axis=-1, keepdims=True)
            o_ref[pl.ds(r0, blk), :] = jnp.dot((e / den).astype(BF16), vw, preferred_element_type=F32)
            l_ref[pl.ds(r0, blk), :] = jnp.broadcast_to(mx + jnp.log(den), (blk, dh))
            return carry

        lax.fori_loop(0, N_BLK, body, 0, unroll=ATT_UNROLL)

    return pl.pallas_call(
        kern, name=f"att_fwd_g{gi}", grid=(ATT_HPG,),
        in_specs=[_head_specs(0), _head_specs(ATT_HPG), _head_specs(2 * ATT_HPG),
                  pl.BlockSpec((None, None, blk, 2 * blk), lambda h: (gi, h, 0, 0))],
        out_specs=[_head_specs(0), _head_specs(0)],
        out_shape=[jax.ShapeDtypeStruct((SEQ, ATT_W), F32), jax.ShapeDtypeStruct((SEQ, ATT_W), F32)],
        scratch_shapes=[pltpu.VMEM((_PAD_ROWS, dh), BF16), pltpu.VMEM((_PAD_ROWS, dh), BF16)],
        compiler_params=_cparams(("arbitrary",)),
    )(qkv, qkv, qkv, bias)


def _att_bwd(gi, qkv, d_att, lse, dd, bias, nb):
    blk, dh = ATT_BLK, ATT_DH

    def kern(q_ref, k_ref, v_ref, do_ref, l_ref, d_ref, b_ref, dqkv_ref, dsb_ref,
             kpad, vpad, qpad, dopad, lpad, dpad):
        zero = jnp.zeros((blk, dh), BF16)
        zero_f = jnp.zeros((blk, dh), F32)
        kpad[0:blk, :] = zero
        vpad[0:blk, :] = zero
        kpad[blk:, :] = k_ref[...]
        vpad[blk:, :] = v_ref[...]
        qpad[SEQ:, :] = zero
        dopad[SEQ:, :] = zero
        lpad[SEQ:, :] = zero_f
        dpad[SEQ:, :] = zero_f
        qpad[0:SEQ, :] = q_ref[...]
        dopad[0:SEQ, :] = do_ref[...]
        lpad[0:SEQ, :] = l_ref[...]
        dpad[0:SEQ, :] = d_ref[...]
        bias_m = b_ref[...]
        bias_t = jnp.concatenate([bias_m[:, blk:], bias_m[:, :blk]], axis=0)
        dsb_ref[...] = jnp.zeros_like(dsb_ref)

        def dq_body(b, carry):
            r0 = pl.multiple_of(b * blk, blk)
            q, d_o = q_ref[pl.ds(r0, blk), :], do_ref[pl.ds(r0, blk), :]
            kw, vw = kpad[pl.ds(r0, 2 * blk), :], vpad[pl.ds(r0, 2 * blk), :]
            lrow, drow = l_ref[pl.ds(r0, blk), :][:, :1], d_ref[pl.ds(r0, blk), :][:, :1]
            valid = _window_mask((b % nb) > 0)
            s = lax.dot_general(q, kw, _NT, preferred_element_type=F32) * _ATT_SCALE + bias_m
            p = jnp.where(valid, jnp.exp(jnp.where(valid, s, -1e30) - lrow), 0.0)
            dp = lax.dot_general(d_o, vw, _NT, preferred_element_type=F32)
            ds = p * (dp - drow)
            dq = jnp.dot(ds.astype(BF16), kw, preferred_element_type=F32)
            dqkv_ref[0, pl.ds(r0, blk), :] = (dq * _ATT_SCALE).astype(BF16)
            dsb_ref[...] += ds
            return carry

        lax.fori_loop(0, N_BLK, dq_body, 0, unroll=ATT_UNROLL)

        qi = lax.broadcasted_iota(I32, (2 * blk, blk), 0)
        kj = lax.broadcasted_iota(I32, (2 * blk, blk), 1)

        def dkv_body(b, carry):
            r0 = pl.multiple_of(b * blk, blk)
            k, v = k_ref[pl.ds(r0, blk), :], v_ref[pl.ds(r0, blk), :]
            qw, dow = qpad[pl.ds(r0, 2 * blk), :], dopad[pl.ds(r0, 2 * blk), :]
            lrow, drow = lpad[pl.ds(r0, 2 * blk), :][:, :1], dpad[pl.ds(r0, 2 * blk), :][:, :1]
            has_next = jnp.logical_and(b + 1 < N_BLK, ((b + 1) % nb) > 0)
            next_ok = jnp.logical_and(jnp.logical_and(qi >= blk, kj >= qi - blk), has_next)
            valid = jnp.logical_or(jnp.logical_and(qi < blk, qi >= kj), next_ok)
            s = lax.dot_general(qw, k, _NT, preferred_element_type=F32) * _ATT_SCALE + bias_t
            p = jnp.where(valid, jnp.exp(jnp.where(valid, s, -1e30) - lrow), 0.0)
            dp = lax.dot_general(dow, v, _NT, preferred_element_type=F32)
            ds = p * (dp - drow)
            d_v = lax.dot_general(p.astype(BF16), dow, _TN, preferred_element_type=F32)
            d_k = lax.dot_general(ds.astype(BF16), qw, _TN, preferred_element_type=F32)
            dqkv_ref[1, pl.ds(r0, blk), :] = (d_k * _ATT_SCALE).astype(BF16)
            dqkv_ref[2, pl.ds(r0, blk), :] = d_v.astype(BF16)
            return carry

        lax.fori_loop(0, N_BLK, dkv_body, 0, unroll=ATT_UNROLL)

    return pl.pallas_call(
        kern, name=f"att_bwd_g{gi}", grid=(ATT_HPG,),
        in_specs=[_head_specs(0), _head_specs(ATT_HPG), _head_specs(2 * ATT_HPG),
                  _head_specs(0), _head_specs(0), _head_specs(0),
                  pl.BlockSpec((None, None, blk, 2 * blk), lambda h: (gi, h, 0, 0))],
        out_specs=[pl.BlockSpec((3, SEQ, dh), lambda h: (0, 0, h)),
                   pl.BlockSpec((None, blk, 2 * blk), lambda h: (h, 0, 0))],
        out_shape=[jax.ShapeDtypeStruct((3, SEQ, ATT_W), BF16),
                   jax.ShapeDtypeStruct((ATT_HPG, blk, 2 * blk), F32)],
        scratch_shapes=[pltpu.VMEM((_PAD_ROWS, dh), BF16)] * 4 + [pltpu.VMEM((_PAD_ROWS, dh), F32)] * 2,
        compiler_params=_cparams(("arbitrary",)),
    )(qkv, qkv, qkv, d_att, lse, dd, bias)


def _rms_parts(x):
    r = lax.rsqrt(jnp.mean(x * x, axis=-1, keepdims=True) + RMS_EPS)
    return x * r, r


def _rms_bwd(d_xhat, xhat, r):
    return r * (d_xhat - xhat * jnp.mean(d_xhat * xhat, axis=-1, keepdims=True))


def _prenorm_fwd(name, x, gain, shift, scale):
    def body(xt, g, sh, sc):
        xhat, _ = _rms_parts(xt)
        return (xhat * g) * (1.0 + sc) + sh
    return _rowmap(name, body, [x], [gain, shift, scale], [(D_MODEL, BF16)])[0]


def _prenorm_bwd(name, d_hs, x, gain, scale, resid, after=()):
    n_dh = len(d_hs)

    def body(*args):
        d_h = args[0]
        for t in args[1:n_dh]:
            d_h = d_h + t
        xt, res, g, sc = args[n_dh:]
        xhat, r = _rms_parts(xt)
        nrm = xhat * g
        d_n = d_h * (1.0 + sc)
        dx = _rms_bwd(d_n * g, xhat, r) + res
        return (dx, jnp.sum(d_h, axis=0, keepdims=True), jnp.sum(d_h * nrm, axis=0, keepdims=True),
                jnp.sum(d_n * xhat, axis=0, keepdims=True))

    return _rowmap(name, body, list(d_hs) + [x, resid], [gain, scale], [(D_MODEL, F32)],
                   [D_MODEL, D_MODEL, D_MODEL], after=after)


def _gn_parts(ro):
    mu = jnp.mean(ro, axis=-1, keepdims=True)
    cen = ro - mu
    rstd = lax.rsqrt(jnp.mean(cen * cen, axis=-1, keepdims=True) + GN_EPS)
    return cen * rstd, rstd


def _retpost_fwd(ro, rg, gn_g, gn_b):
    def body(rot, rgt, g, b):
        outs = []
        for h in range(RET_HEADS):
            sl = slice(h * RET_DV, (h + 1) * RET_DV)
            nrm, _ = _gn_parts(rot[:, sl])
            gate = rgt[:, sl]
            outs.append((gate * _sigmoid(gate)) * (nrm * g[:, sl] + b[:, sl]))
        return jnp.concatenate(outs, axis=-1)
    return _rowmap("retpost_fwd", body, [ro, rg], [gn_g, gn_b], [(RET_V_W, BF16)])[0]


def _retpost_bwd(d_gated, ro, rg, gn_g, gn_b):
    def body(dgt, rot, rgt, g, b):
        d_ro, d_rg, d_g, d_b = [], [], [], []
        for h in range(RET_HEADS):
            sl = slice(h * RET_DV, (h + 1) * RET_DV)
            nrm, rstd = _gn_parts(rot[:, sl])
            gate, dg = rgt[:, sl], dgt[:, sl]
            sg = _sigmoid(gate)
            ron = nrm * g[:, sl] + b[:, sl]
            d_rg.append(dg * ron * (sg * (1.0 + gate * (1.0 - sg))))
            d_ron = dg * (gate * sg)
            d_g.append(jnp.sum(d_ron * nrm, axis=0, keepdims=True))
            d_b.append(jnp.sum(d_ron, axis=0, keepdims=True))
            d_n = d_ron * g[:, sl]
            d_ro.append(rstd * (d_n - jnp.mean(d_n, axis=-1, keepdims=True)
                                - nrm * jnp.mean(d_n * nrm, axis=-1, keepdims=True)))
        cat = lambda ts: jnp.concatenate(ts, axis=-1)
        return cat(d_ro), cat(d_rg), cat(d_g), cat(d_b)
    return _rowmap("retpost_bwd", body, [d_gated, ro, rg], [gn_g, gn_b],
                   [(RET_V_W, F32), (RET_V_W, BF16)], [RET_V_W, RET_V_W])


def _combine(os_, ls_):
    def body(o0, o1, o2, l0, l1, l2):
        mx = jnp.maximum(jnp.maximum(l0, l1), l2)
        e0, e1, e2 = jnp.exp(l0 - mx), jnp.exp(l1 - mx), jnp.exp(l2 - mx)
        den = e0 + e1 + e2
        att = (e0 / den) * o0 + (e1 / den) * o1 + (e2 / den) * o2
        return att, att, mx + jnp.log(den)
    return _rowmap("att_combine", body, list(os_) + list(ls_), [],
                   [(ATT_W, F32), (ATT_W, BF16), (ATT_W, F32)])


def _att_bwd_pre(d_att, att):
    def body(dt, at):
        outs = []
        for h in range(ATT_HPG):
            sl = slice(h * ATT_DH, (h + 1) * ATT_DH)
            outs.append(jnp.broadcast_to(jnp.sum(dt[:, sl] * at[:, sl], axis=-1, keepdims=True),
                                         (dt.shape[0], ATT_DH)))
        return dt, jnp.concatenate(outs, axis=-1)
    return _rowmap("att_bwd_pre", body, [d_att, att], [], [(ATT_W, BF16), (ATT_W, F32)])


def _merge_fwd(gates, ret_out, att_out):
    def body(gt, ro, ao):
        return _sigmoid(gt[:, :D_MODEL]) * ro + _sigmoid(gt[:, D_MODEL:]) * ao
    return _rowmap("merge_fwd", body, [gates, ret_out, att_out], [], [(D_MODEL, BF16)])[0]


def _merge_bwd(d_merged, gates, ret_out, att_out):
    def body(dm, gt, ro, ao):
        sa, sb = _sigmoid(gt[:, :D_MODEL]), _sigmoid(gt[:, D_MODEL:])
        d_gates = jnp.concatenate([dm * ro * (sa * (1.0 - sa)), dm * ao * (sb * (1.0 - sb))], axis=-1)
        return dm * sa, dm * sb, d_gates
    return _rowmap("merge_bwd", body, [d_merged, gates, ret_out, att_out], [],
                   [(D_MODEL, BF16), (D_MODEL, BF16), (2 * D_MODEL, BF16)])


def _gate_bwd(name, d_x, branch, gate):
    def body(dx, br, g):
        return dx * g, jnp.sum(dx * br, axis=0, keepdims=True)
    return _rowmap(name, body, [d_x, branch], [gate], [(D_MODEL, BF16)], [D_MODEL])


def _loss_head(x3, target, gain):
    def body(xt, tt, g):
        xhat, r = _rms_parts(xt)
        err = xhat * g - tt
        d_y = err / D_MODEL
        loss = 0.5 * jnp.sum(jnp.mean(err * err, axis=-1, keepdims=True), axis=0, keepdims=True)
        d_x = _rms_bwd(d_y * g, xhat, r)
        return d_x, jnp.broadcast_to(loss, (1, 128)), jnp.sum(d_y * xhat, axis=0, keepdims=True)
    return _rowmap("loss_head", body, [x3, target], [gain], [(D_MODEL, F32)], [128, D_MODEL])


def _local_step(pos, x, target, mod, norm1_g, norm2_g, norm_f_g, rel_bias, gn_g, gn_b, w_in, rest_gather):
    sh1, sc1, g1, sh2, sc2, g2 = [mod[:, i * D_MODEL:(i + 1) * D_MODEL] for i in range(6)]
    cos, sin = _rope_tables()
    din, qd, kd, cd = _decay_tables()
    buckets = _bucket_tables()
    bias = _bias_build(rel_bias, buckets)
    dils = [d for _, d in ATT_GROUPS]
    nbs = [SEQ // d // ATT_BLK for d in dils]

    h1 = _prenorm_fwd("prenorm1_fwd", x, norm1_g, sh1, sc1)
    h1_p = [_permute_rows(h1, d) for d in dils]

    def rot_epi(acc, cs, sn, scale):
        half = RET_DK // 2
        x1, x2 = acc[:, :half], acc[:, half:]
        return (jnp.concatenate([x1 * cs - x2 * sn, x1 * sn + x2 * cs], axis=-1) * scale,)

    qk_scale = jnp.concatenate([jnp.ones((1, RET_QK_W), F32),
                                jnp.full((1, RET_QK_W), RET_DK ** -0.5, F32)], axis=-1)
    rope_ex = [(cos, (TM, RET_DK // 2), lambda i, j, kk: (i, 0)),
               (sin, (TM, RET_DK // 2), lambda i, j, kk: (i, 0)),
               (qk_scale, (1, RET_DK), lambda i, j, kk: (0, j))]
    rest_sems, rest_shards, rest_fulls, rest_token = rest_gather
    behind = [rest_token]
    rv = _matmul("proj_rv", h1, w_in, "nn", SEQ, RET_V_W, D_MODEL, [BF16], b_off=OFF_V, tk=D_MODEL,
                 after=behind)[0]
    rg = _matmul("proj_rg", h1, w_in, "nn", SEQ, RET_V_W, D_MODEL, [F32], b_off=OFF_G, tk=D_MODEL,
                 after=behind)[0]
    gates = _matmul("proj_gates", h1, w_in, "nn", SEQ, 2 * D_MODEL, D_MODEL, [F32], b_off=OFF_GATE,
                    tn=512, tk=D_MODEL, after=behind)[0]
    aqkv = [_matmul(f"proj_att_g{gi}", h1_p[gi], w_in, "nn", SEQ, 3 * ATT_W, D_MODEL, [BF16],
                    b_off=OFF_ATT + gi * 3 * ATT_W, tn=512, tk=D_MODEL, after=behind)[0]
            for gi in range(3)]

    os_, ls_ = [], []
    for gi in range(3):
        o_g, l_g = _att_fwd(gi, aqkv[gi], bias, nbs[gi])
        os_.append(_unpermute_rows(o_g, dils[gi]))
        ls_.append(_unpermute_rows(l_g, dils[gi]))
        if gi == 1:
            rest_sems, rest_fulls, fwd_token = _gather_rest_forward(rest_sems, rest_shards, rest_fulls,
                                                                    [o_g, rv, rg, gates])

    rqk = _matmul("proj_qk", h1, w_in, "nn", SEQ, 2 * RET_QK_W, D_MODEL, [BF16], b_off=OFF_Q,
                  tn=RET_DK, tk=D_MODEL, epilogue=rot_epi, extras=rope_ex, after=[fwd_token])[0]
    ro, states = _retention_fwd(rqk, rv, din, qd, kd, cd)
    gated = _retpost_fwd(ro, rg, gn_g, gn_b)
    w_ret_out, w_att_out, w_o, w_ff1, w_ff2 = _gather_rest_end(rest_sems, rest_fulls, [gated, os_[2]])
    ret_out = _matmul("ret_out", gated, w_ret_out, "nn", SEQ, D_MODEL, RET_V_W, [F32])[0]
    att, att_b, lse = _combine(os_, ls_)
    att_out = _matmul("att_out", att_b, w_att_out, "nn", SEQ, D_MODEL, ATT_W, [F32])[0]

    merged = _merge_fwd(gates, ret_out, att_out)

    def resid_epi(acc, xt, g):
        return xt + g * acc, acc

    def resid_ex(xin, g):
        return [(xin, (TM, TN), lambda i, j, kk: (i, j)), (g, (1, TN), lambda i, j, kk: (0, j))]

    x2, mix = _matmul("mix_out", merged, w_o, "nn", SEQ, D_MODEL, D_MODEL, [F32, F32],
                      epilogue=resid_epi, extras=resid_ex(x, g1))
    h2 = _prenorm_fwd("prenorm2_fwd", x2, norm2_g, sh2, sc2)

    def relu2_epi(acc):
        r = jnp.maximum(acc, 0.0)
        return r * r, r

    act, relu_u = _matmul("ff1", h2, w_ff1, "nn", SEQ, D_FF, D_MODEL, [BF16, BF16], tk=D_MODEL,
                          epilogue=relu2_epi)
    x3, y2 = _matmul("ff2", act, w_ff2, "nn", SEQ, D_MODEL, D_FF, [F32, F32],
                     epilogue=resid_epi, extras=resid_ex(x2, g2))

    d_x3, loss, d_gf = _loss_head(x3, target, norm_f_g)

    d_y2, d_g2 = _gate_bwd("ff_gate_bwd", d_x3, y2, g2)

    def relu2_bwd_epi(acc, rt):
        return (acc * (2.0 * rt.astype(F32)),)

    gw_ff2 = _matmul_tn_pair("ff2_dw", pos, act, d_y2, D_FF, D_MODEL, SEQ, D_FF // N_CHIPS,
                             tm=512, tn=1024, tk=1024)
    d_u = _matmul("ff2_dx", d_y2, w_ff2, "nt", SEQ, D_FF, D_MODEL, [BF16], epilogue=relu2_bwd_epi,
                  extras=[(relu_u, (TM, TN), lambda i, j, kk: (i, j))])[0]
    gw_ff1 = _matmul_tn_pair("ff1_dw", pos, h2, d_u, D_MODEL, D_FF, SEQ, D_MODEL,
                             tm=512, tn=1024, tk=1024)
    ffn = ["w_ff2", "w_ff1"]
    ffn_started = _ici_start("ici_start_ffn", ffn, [gw_ff2, gw_ff1])
    d_h2 = _matmul("ff1_dx", d_u, w_ff1, "nt", SEQ, D_MODEL, D_FF, [F32], after=[ffn_started[3]])[0]
    d_x2, d_sh2, d_sc2, d_n2g = _prenorm_bwd("prenorm2_bwd", [d_h2], x2, norm2_g, sc2, d_x3)

    d_mix, d_g1 = _gate_bwd("mix_gate_bwd", d_x2, mix, g1)
    gw_o = _matmul_tn_pair("mix_dw", pos, merged, d_mix, D_MODEL, D_MODEL, SEQ, D_MODEL // N_CHIPS,
                           tm=128, tn=1024, tk=2048)
    d_merged = _matmul("mix_dx", d_mix, w_o, "nt", SEQ, D_MODEL, D_MODEL, [F32])[0]
    d_ret_out, d_att_out, d_gates = _merge_bwd(d_merged, gates, ret_out, att_out)

    gw_ret_out = _matmul_tn_pair("ret_out_dw", pos, gated, d_ret_out, RET_V_W, D_MODEL, SEQ,
                                 RET_V_W // N_CHIPS, tm=256, tn=1024, tk=1024)
    gw_att_out = _matmul_tn_pair("att_out_dw", pos, att_b, d_att_out, ATT_W, D_MODEL, SEQ, ATT_W,
                                 tm=256, tn=1024, tk=2048)
    mixer = ["w_o", "w_ret_out", "w_att_out"]
    mixer_started = _ici_start("ici_start_mixer", mixer, [gw_o, gw_ret_out, gw_att_out])
    d_gated = _matmul("ret_out_dx", d_ret_out, w_ret_out, "nt", SEQ, RET_V_W, D_MODEL, [F32],
                      after=[mixer_started[3]])[0]
    d_att = _matmul("att_out_dx", d_att_out, w_att_out, "nt", SEQ, ATT_W, D_MODEL, [F32],
                    after=[mixer_started[3]])[0]

    d_ro, d_rg, d_gn_g, d_gn_b = _retpost_bwd(d_gated, ro, rg, gn_g, gn_b)
    d_rqkv = _retention_bwd(rqk, rv, states, d_ro, din, qd, kd, cd, cos, sin)

    d_att_b, dd = _att_bwd_pre(d_att, att)
    d_aqkv, dsbs = [], []
    for gi in range(3):
        da_p = _permute_rows(d_att_b, dils[gi])
        l_p = _permute_rows(lse, dils[gi])
        dd_p = _permute_rows(dd, dils[gi])
        dqkv, dsb = _att_bwd(gi, aqkv[gi], da_p, l_p, dd_p, bias, nbs[gi])
        if dils[gi] > 1:
            dqkv = dqkv.reshape(3, dils[gi], SEQ // dils[gi], ATT_W).transpose(0, 2, 1, 3).reshape(
                3, SEQ, ATT_W)
        d_aqkv.append(dqkv)
        dsbs.append(dsb)
    d_rel_bias = _bias_grad(jnp.stack(dsbs), buckets)

    d_proj = [(d_rqkv, False), (d_rg, False)] + [(t, True) for t in d_aqkv] + [(d_gates, False)]
    gw_in = _matmul_tn_pair("proj_dw", pos, h1, d_proj, D_MODEL, IN_COLS, SEQ, D_MODEL,
                            tm=512, tn=ATT_W, tk=SEQ)
    sems, (gw_in,), (land,), token = _ici_start("ici_start_w_in", ["w_in"], [gw_in])
    d_h1 = _matmul("proj_dx", d_proj, w_in, "nt", SEQ, D_MODEL, IN_COLS, [F32], tn=1024, tk=ATT_W,
                   after=[token])[0]
    pending = (sems, land)

    names = ffn + mixer
    psums, got = _ici_wait("ici_wait_rest", names, list(ffn_started[0]) + list(mixer_started[0]),
                           list(ffn_started[1]) + list(mixer_started[1]),
                           list(ffn_started[2]) + list(mixer_started[2]), [d_h1])
    g_big = {n: _final_sum("final_" + n, pos, dict(BIG)[n], psums[i], got[i], SHARD[n])
             for i, n in enumerate(names)}
    grad_x, d_sh1, d_sc1, d_n1g = _prenorm_bwd("prenorm1_bwd", [d_h1], x, norm1_g, sc1, d_x2,
                                               after=list(g_big.values()))
    d_mod = jnp.concatenate([d_sh1, d_sc1, d_g1, d_sh2, d_sc2, d_g2], axis=-1)
    small = dict(norm1_g=d_n1g, norm2_g=d_n2g, norm_f_g=d_gf, gn_g=d_gn_g, gn_b=d_gn_b,
                 rel_bias=d_rel_bias)
    return loss, grad_x, d_mod, small, g_big, (gw_in,) + pending


def _me():
    return lax.axis_index("x"), lax.axis_index("y"), lax.axis_index("c")


def _peer(x, y, c, mask):
    return (x ^ ((mask >> 2) & 1), y ^ ((mask >> 1) & 1), c ^ (mask & 1))


def _gather8(src_ref, dst_ref, send_sems, recv_sems):
    x, y, c = _me()
    me = 4 * x + 2 * y + c
    copies = []
    for mask in range(1, N_DEV):
        cp = pltpu.make_async_remote_copy(
            src_ref=src_ref, dst_ref=dst_ref.at[me], send_sem=send_sems.at[mask - 1],
            recv_sem=recv_sems.at[mask - 1], device_id=_peer(x, y, c, mask), device_id_type=MESH)
        cp.start()
        copies.append(cp)
    dst_ref[me] = src_ref[...]
    for cp in copies:
        cp.wait_recv()
    for cp in copies:
        cp.wait_send()


def _ada_fwd(c_in, w_ada, b_ada):
    ncol = ADA_COLS // N_CHIPS

    def body(c_ref, w_ref, b_ref, mod_ref, sc_ref, cbuf, cg, mbuf, mg, s1, r1, s2, r2):
        x, y, c = _me()
        me = 4 * x + 2 * y + c
        cv = c_ref[...]
        cbuf[...] = jnp.broadcast_to(cv * _sigmoid(cv), cbuf.shape)
        _gather8(cbuf, cg, s1, r1)
        rows = lax.broadcasted_iota(I32, (N_DEV, D_MODEL), 0)
        sc_all = jnp.zeros((N_DEV, D_MODEL), F32)
        for d in range(N_DEV):
            sc_all = jnp.where(rows == d, cg[d], sc_all)
        sc_ref[...] = sc_all
        mbuf[...] = jnp.dot(sc_all.astype(BF16), w_ref[...].astype(BF16), preferred_element_type=F32)
        _gather8(mbuf, mg, s2, r2)
        rowsel = lax.broadcasted_iota(I32, (N_DEV, ncol), 0) == me
        for k in range(N_CHIPS):
            blk = mg[2 * k]
            row = jnp.sum(jnp.where(rowsel, blk, 0.0), axis=0, keepdims=True)
            mod_ref[:, k * ncol:(k + 1) * ncol] = row + b_ref[:, k * ncol:(k + 1) * ncol]

    vm = pl.BlockSpec(memory_space=pltpu.VMEM)
    return pl.pallas_call(
        body, name="ada_fwd",
        in_specs=[vm, vm, vm], out_specs=[vm, vm],
        out_shape=[jax.ShapeDtypeStruct((1, ADA_COLS), F32), jax.ShapeDtypeStruct((N_DEV, D_MODEL), F32)],
        scratch_shapes=[
            pltpu.VMEM((8, D_MODEL), F32), pltpu.VMEM((N_DEV, 8, D_MODEL), F32),
            pltpu.VMEM((8, ncol), F32), pltpu.VMEM((N_DEV, 8, ncol), F32),
            pltpu.SemaphoreType.DMA((N_DEV - 1,)), pltpu.SemaphoreType.DMA((N_DEV - 1,)),
            pltpu.SemaphoreType.DMA((N_DEV - 1,)), pltpu.SemaphoreType.DMA((N_DEV - 1,)),
        ],
        compiler_params=pltpu.CompilerParams(vmem_limit_bytes=VMEM_LIMIT_V7X),
    )(c_in, w_ada, b_ada)


def _small_reduce(pack, sc_all):
    ncol = ADA_COLS // N_CHIPS

    def body(p_ref, sc_ref, tot_ref, gw_ref, pg, s1, r1):
        x, y, _ = _me()
        chip = 2 * x + y
        _gather8(p_ref, pg, s1, r1)
        tot = pg[0]
        for d in range(1, N_DEV):
            tot = tot + pg[d]
        tot_ref[...] = tot
        rows = lax.broadcasted_iota(I32, (N_DEV, ncol), 0)
        dmod = jnp.zeros((N_DEV, ncol), F32)
        for k in range(N_CHIPS):
            part = jnp.zeros((N_DEV, ncol), F32)
            for d in range(N_DEV):
                part = jnp.where(rows == d, pg[d, :, k * ncol:(k + 1) * ncol][0:1, :], part)
            dmod = jnp.where(chip == k, part, dmod)
        gw_ref[...] = lax.dot_general(sc_ref[...].astype(BF16), dmod.astype(BF16), _TN,
                                      preferred_element_type=F32)

    vm = pl.BlockSpec(memory_space=pltpu.VMEM)
    return pl.pallas_call(
        body, name="small_reduce",
        in_specs=[vm, vm], out_specs=[vm, vm],
        out_shape=[jax.ShapeDtypeStruct((8, ADA_COLS), F32), jax.ShapeDtypeStruct((D_MODEL, ncol), F32)],
        scratch_shapes=[pltpu.VMEM((N_DEV, 8, ADA_COLS), F32),
                        pltpu.SemaphoreType.DMA((N_DEV - 1,)), pltpu.SemaphoreType.DMA((N_DEV - 1,))],
        compiler_params=pltpu.CompilerParams(vmem_limit_bytes=VMEM_LIMIT_V7X),
    )(pack, sc_all)


BIG = (("w_in", 1), ("w_ret_out", 0), ("w_att_out", 1), ("w_o", 0), ("w_ff1", 1), ("w_ff2", 0))
SHARD = {"w_in": (D_MODEL, IN_COLS // N_CHIPS), "w_ret_out": (RET_V_W // N_CHIPS, D_MODEL),
         "w_att_out": (ATT_W, D_MODEL // N_CHIPS), "w_o": (D_MODEL // N_CHIPS, D_MODEL),
         "w_ff1": (D_MODEL, D_FF // N_CHIPS), "w_ff2": (D_FF // N_CHIPS, D_MODEL)}
_CHIP_FLIPS = ((1, 0), (0, 1), (1, 1))


def _region(ref, axis, chip, half, shard_shape):
    r, cw = shard_shape
    hr = r // 2
    if axis == 1:
        return ref.at[pl.ds(half * hr, hr), pl.ds(chip * cw, cw)]
    return ref.at[pl.ds(chip * r + half * hr, hr), :]


def _gather_weights(shards, n_remote):
    nw = len(BIG)
    shapes = [s.shape for s in shards]
    full_shapes = [(r, N_CHIPS * cw) if ax == 1 else (N_CHIPS * r, cw)
                   for (r, cw), (_, ax) in zip(shapes, BIG)]

    def body(*refs):
        ins, outs = refs[:nw], refs[nw:2 * nw]
        own = refs[2 * nw:3 * nw]
        from_ici, from_sib = refs[3 * nw:3 * nw + n_remote], refs[3 * nw + n_remote:3 * nw + 2 * n_remote]
        ld_sem, st_sem, s_ici, r_ici, s_d2d, r_d2d, st_a, st_b = refs[3 * nw + 2 * n_remote:]
        x, y, c = _me()
        chip = 2 * x + y
        sib = (x, y, 1 - c)
        loads = [pltpu.make_async_copy(ins[i], own[i], ld_sem.at[i]) for i in range(nw)]
        for cp in loads:
            cp.start()
        pending, first = [], []
        for i, (_, ax) in enumerate(BIG):
            r, cw = shapes[i]
            hr = r // 2
            loads[i].wait()
            dst = outs[i].at[:, pl.ds(chip * cw, cw)] if ax == 1 else outs[i].at[pl.ds(chip * r, r), :]
            cp = pltpu.make_async_copy(own[i], dst, st_sem.at[i])
            cp.start()
            pending.append(cp)
            for j, (fx, fy) in enumerate(_CHIP_FLIPS if i < n_remote else ()):
                rc = pltpu.make_async_remote_copy(
                    src_ref=own[i].at[pl.ds(c * hr, hr), :], dst_ref=from_ici[i].at[j],
                    send_sem=s_ici.at[j * nw + i], recv_sem=r_ici.at[j * nw + i],
                    device_id=(x ^ fx, y ^ fy, c), device_id_type=MESH)
                rc.start()
                first.append((j, i, rc))
        passed = []
        for j, i, rc in first:
            fx, fy = _CHIP_FLIPS[j]
            src_chip = 2 * (x ^ fx) + (y ^ fy)
            ax = BIG[i][1]
            rc.wait_recv()
            fw = pltpu.make_async_remote_copy(
                src_ref=from_ici[i].at[j], dst_ref=from_sib[i].at[j], send_sem=s_d2d.at[j * nw + i],
                recv_sem=r_d2d.at[j * nw + i], device_id=sib, device_id_type=MESH)
            fw.start()
            passed.append((j, i, src_chip, fw))
            st = pltpu.make_async_copy(from_ici[i].at[j], _region(outs[i], ax, src_chip, c, shapes[i]),
                                       st_a.at[j * nw + i])
            st.start()
            pending.append(st)
        for j, i, src_chip, fw in passed:
            fw.wait_recv()
            st = pltpu.make_async_copy(from_sib[i].at[j],
                                       _region(outs[i], BIG[i][1], src_chip, 1 - c, shapes[i]),
                                       st_b.at[j * nw + i])
            st.start()
            pending.append(st)
        for _, _, rc in first:
            rc.wait_send()
        for _, _, _, fw in passed:
            fw.wait_send()
        for cp in pending:
            cp.wait()

    hbm = pl.BlockSpec(memory_space=pl.ANY)
    halves = [pltpu.VMEM((3, r // 2, cw), BF16) for r, cw in shapes[:n_remote]]
    return pl.pallas_call(
        body, name="gather_weights",
        in_specs=[hbm] * nw, out_specs=[hbm] * nw,
        out_shape=[jax.ShapeDtypeStruct(fs, BF16) for fs in full_shapes],
        scratch_shapes=[pltpu.VMEM(sh, BF16) for sh in shapes] + halves + halves
        + [pltpu.SemaphoreType.DMA((nw,)), pltpu.SemaphoreType.DMA((nw,))]
        + [pltpu.SemaphoreType.DMA((3 * nw,))] * 6,
        compiler_params=pltpu.CompilerParams(vmem_limit_bytes=VMEM_LIMIT_V7X),
    )(*shards)


REST = BIG[1:]
_SIDE_EFFECTS = pltpu.CompilerParams(has_side_effects=pltpu.SideEffectType.DATAFLOW_SIDE_EFFECTING)
_ANY_SPEC = pl.BlockSpec(memory_space=pl.ANY)


def _rest_ici_copies(shard_refs, full_refs, sems):
    x, y, c = _me()
    chip = 2 * x + y
    n = 3 * len(REST)
    copies = []
    for i, (name, ax) in enumerate(REST):
        hr = SHARD[name][0] // 2
        for j, (fx, fy) in enumerate(_CHIP_FLIPS):
            copies.append(pltpu.make_async_remote_copy(
                src_ref=shard_refs[i].at[pl.ds(c * hr, hr), :],
                dst_ref=_region(full_refs[i], ax, chip, c, SHARD[name]),
                send_sem=sems[3 * i + j], recv_sem=sems[n + 3 * i + j],
                device_id=(x ^ fx, y ^ fy, c), device_id_type=MESH))
    return copies


def _rest_d2d_copies(full_refs, sems):
    x, y, c = _me()
    n = 3 * len(REST)
    copies = []
    for i, (name, ax) in enumerate(REST):
        for j, (fx, fy) in enumerate(_CHIP_FLIPS):
            reg = _region(full_refs[i], ax, 2 * (x ^ fx) + (y ^ fy), c, SHARD[name])
            copies.append(pltpu.make_async_remote_copy(
                src_ref=reg, dst_ref=reg, send_sem=sems[3 * i + j], recv_sem=sems[n + 3 * i + j],
                device_id=(x, y, 1 - c), device_id_type=MESH))
    return copies


def _gather_rest_start(shards, fulls, after):
    nr, ns, na = len(REST), 6 * len(REST), len(after)

    def body(*refs):
        for cp in _rest_ici_copies(refs[:nr], refs[nr:2 * nr], refs[2 * nr + na:2 * nr + na + ns]):
            cp.start()
        token = refs[-1]
        token[...] = jnp.zeros_like(token)

    hbm = lambda a: pltpu.HBM(a.shape, a.dtype)
    res = pl.pallas_call(
        body, name="gather_rest_start",
        out_shape=(pltpu.SemaphoreType.DMA(()),) * ns + tuple(hbm(a) for a in shards + fulls)
        + (jax.ShapeDtypeStruct((8, 128), F32),),
        in_specs=(_HBM_SPEC,) * (2 * nr) + (_ANY_SPEC,) * na,
        out_specs=(_SEM_SPEC,) * ns + (_HBM_SPEC,) * (2 * nr) + (pl.BlockSpec(memory_space=pltpu.VMEM),),
        input_output_aliases={k: ns + k for k in range(2 * nr)}, compiler_params=_SIDE_EFFECTS,
    )(*[pltpu.with_memory_space_constraint(a, pltpu.HBM) for a in shards + fulls], *after)
    return res[:ns], res[ns:ns + nr], res[ns + nr:ns + 2 * nr], res[-1]


def _gather_rest_forward(sems, shards, fulls, after):
    nr, ns = len(REST), 6 * len(REST)

    def body(*refs):
        shard_refs, full_refs, old = refs[:nr], refs[nr:2 * nr], refs[2 * nr:2 * nr + ns]
        new = refs[2 * nr + ns + len(after):2 * nr + 2 * ns + len(after)]
        for cp in _rest_ici_copies(shard_refs, full_refs, old):
            cp.wait_send()
            cp.wait_recv()
        for cp in _rest_d2d_copies(full_refs, new):
            cp.start()
        token = refs[-1]
        token[...] = jnp.zeros_like(token)

    res = pl.pallas_call(
        body, name="gather_rest_forward",
        out_shape=(pltpu.SemaphoreType.DMA(()),) * ns + tuple(pltpu.HBM(a.shape, a.dtype) for a in fulls)
        + (jax.ShapeDtypeStruct((8, 128), F32),),
        in_specs=(_HBM_SPEC,) * (2 * nr) + (_SEM_SPEC,) * ns + (_ANY_SPEC,) * len(after),
        out_specs=(_SEM_SPEC,) * ns + (_HBM_SPEC,) * nr + (pl.BlockSpec(memory_space=pltpu.VMEM),),
        input_output_aliases={nr + k: ns + k for k in range(nr)}, compiler_params=_SIDE_EFFECTS,
    )(*shards, *fulls, *sems, *after)
    return res[:ns], res[ns:ns + nr], res[-1]


def _gather_rest_end(sems, fulls, after):
    nr, ns = len(REST), 6 * len(REST)

    def body(*refs):
        for cp in _rest_d2d_copies(refs[:nr], refs[nr:nr + ns]):
            cp.wait_send()
            cp.wait_recv()

    return pl.pallas_call(
        body, name="gather_rest_end",
        out_shape=tuple(pltpu.HBM(a.shape, a.dtype) for a in fulls),
        in_specs=(_HBM_SPEC,) * nr + (_SEM_SPEC,) * ns + (_ANY_SPEC,) * len(after),
        out_specs=(_HBM_SPEC,) * nr,
        input_output_aliases={k: k for k in range(nr)}, compiler_params=_SIDE_EFFECTS,
    )(*fulls, *sems, *after)


def _adam_update(w, g, m, v):
    mn = ADAM_B1 * m + (1.0 - ADAM_B1) * g
    vn = ADAM_B2 * v + (1.0 - ADAM_B2) * (g * g)
    m_hat = mn / (1.0 - ADAM_B1 ** ADAM_STEP)
    v_hat = vn / (1.0 - ADAM_B2 ** ADAM_STEP)
    return -ADAM_LR * (m_hat / (jnp.sqrt(v_hat) + ADAM_EPS) + ADAM_WD * w), mn, vn


def _final_sum(name, pos, axis, psum, recv, shard_shape, after=(), tr=128):
    r, cw = shard_shape
    hr = r // 2
    tr = min(tr, hr)
    nt = hr // tr
    n_after = len(after)

    def kern(pos_ref, p_ref, r_ref, *rest):
        g_ref, send_buf, land_buf, s_sem, r_sem = rest[n_after:]
        p, t = pl.program_id(0), pl.program_id(1)
        sib = _sibling()

        def copy(i):
            return pltpu.make_async_remote_copy(
                src_ref=send_buf.at[i], dst_ref=land_buf.at[i], send_sem=s_sem.at[i],
                recv_sem=r_sem.at[i], device_id=sib, device_id_type=MESH)

        @pl.when(p == 0)
        def _():
            tot = p_ref[...].astype(F32)
            for j in range(3):
                tot = tot + r_ref[j].astype(F32)
            send_buf[t] = tot
            copy(t).start()
            g_ref[...] = tot

        @pl.when(p == 1)
        def _():
            copy(t).wait_recv()
            g_ref[...] = land_buf[t]

        @pl.when(jnp.logical_and(p == 1, t == nt - 1))
        def _():
            for i in range(nt):
                copy(i).wait_send()

    def shard_rows(p, t, pos_ref):
        return (jnp.where(p == 0, pos_ref[0], 1 - pos_ref[0]) * nt + t, 0)

    def own_part(p, t, pos_ref):
        tt = jnp.where(p == 0, t, nt - 1)
        return (tt, pos_ref[1]) if axis == 1 else (pos_ref[1] * nt + tt, 0)

    grid_spec = pltpu.PrefetchScalarGridSpec(
        num_scalar_prefetch=1, grid=(2, nt),
        in_specs=[pl.BlockSpec((tr, cw), own_part),
                  pl.BlockSpec((3, tr, cw), lambda p, t, pos_ref: (0, jnp.where(p == 0, t, nt - 1), 0))]
        + [pl.BlockSpec(memory_space=pl.ANY)] * n_after,
        out_specs=pl.BlockSpec((tr, cw), shard_rows),
        scratch_shapes=[pltpu.VMEM((nt, tr, cw), F32), pltpu.VMEM((nt, tr, cw), F32),
                        pltpu.SemaphoreType.DMA((nt,)), pltpu.SemaphoreType.DMA((nt,))])
    return pl.pallas_call(
        kern, name=name, grid_spec=grid_spec, out_shape=jax.ShapeDtypeStruct((r, cw), F32),
        compiler_params=_cparams(("arbitrary", "arbitrary")),
    )(pos, psum, recv, *after)


def _adamw(name, w, g, m, v):
    r, cw = w.shape
    tr = min(r, 128)

    def kern(w_ref, g_ref, m_ref, v_ref, go_ref, d_ref, nm_ref, nv_ref):
        gv = g_ref[...]
        go_ref[...] = gv
        d_ref[...], nm_ref[...], nv_ref[...] = _adam_update(w_ref[...], gv, m_ref[...], v_ref[...])

    spec = pl.BlockSpec((tr, cw), lambda i: (i, 0))
    return pl.pallas_call(
        kern, name=name, grid=(r // tr,), in_specs=[spec] * 4, out_specs=[spec] * 4,
        out_shape=[jax.ShapeDtypeStruct((r, cw), F32)] * 4, compiler_params=_cparams(("parallel",)),
    )(w, g, m, v)


_PACK_W = ADA_COLS
_NB = REL_BUCKETS * N_ATT_HEADS
_SMALL_SLOTS = {
    "b_ada": (0, 0, ADA_COLS),
    "norm1_g": (1, 0, D_MODEL), "norm2_g": (1, D_MODEL, D_MODEL), "norm_f_g": (1, 2 * D_MODEL, D_MODEL),
    "ret_gn_g": (1, 3 * D_MODEL, RET_V_W),
    "ret_gn_b": (2, 0, RET_V_W), "rel_bias": (2, RET_V_W, _NB), "loss": (2, RET_V_W + 512, 128),
}


def _pack_small(vals):
    rows = []
    for r in range(8):
        items = sorted([(off, n) for n, (rr, off, _) in _SMALL_SLOTS.items() if rr == r and n in vals])
        parts, pos = [], 0
        for off, n in items:
            if off > pos:
                parts.append(jnp.zeros((1, off - pos), F32))
            parts.append(vals[n].reshape(1, -1).astype(F32))
            pos = off + _SMALL_SLOTS[n][2]
        if pos < _PACK_W:
            parts.append(jnp.zeros((1, _PACK_W - pos), F32))
        rows.append(jnp.concatenate(parts, axis=-1))
    return jnp.concatenate(rows, axis=0)


def _unpack_small(pack, name):
    r, off, wd = _SMALL_SLOTS[name]
    return pack[r:r + 1, off:off + wd]


def kernel(x, c, w_ada, b_ada, norm1_g, w_in, rel_bias, ret_gn_g, ret_gn_b, w_ret_out, w_att_out, w_o, norm2_g, w_ff1, w_ff2, norm_f_g, loss_target, m_w_ada, m_b_ada, m_norm1_g, m_w_in, m_rel_bias, m_ret_gn_g, m_ret_gn_b, m_w_ret_out, m_w_att_out, m_w_o, m_norm2_g, m_w_ff1, m_w_ff2, m_norm_f_g, v_w_ada, v_b_ada, v_norm1_g, v_w_in, v_rel_bias, v_ret_gn_g, v_ret_gn_b, v_w_ret_out, v_w_att_out, v_w_o, v_norm2_g, v_w_ff1, v_w_ff2, v_norm_f_g):
    given = dict(locals())
    big_names = [n for n, _ in BIG]
    shard_w = {n: given[n][0] for n in big_names}
    assert all(shard_w[n].shape == SHARD[n] for n in big_names)

    shards_bf = [shard_w[n].astype(BF16) for n in big_names]
    full = _gather_weights(shards_bf, 1)
    mod, sc_all = _ada_fwd(c, w_ada[0], b_ada)
    rest_gather = _gather_rest_start(shards_bf[1:], list(full[1:]), [mod])
    pos = _where_am_i()

    loss, grad_x, d_mod, small, g_big, pending = _local_step(
        pos, x[0], loss_target[0], mod, norm1_g, norm2_g, norm_f_g.reshape(1, -1), rel_bias, ret_gn_g,
        ret_gn_b, full[0], rest_gather)

    pack_g = _pack_small(dict(b_ada=d_mod, norm1_g=small["norm1_g"], norm2_g=small["norm2_g"],
                              norm_f_g=small["norm_f_g"], ret_gn_g=small["gn_g"], ret_gn_b=small["gn_b"],
                              rel_bias=small["rel_bias"], loss=loss))
    tot, g_w_ada = _small_reduce(pack_g, sc_all)

    small_names = ["b_ada", "norm1_g", "rel_bias", "ret_gn_g", "ret_gn_b", "norm2_g", "norm_f_g"]
    pack_w = _pack_small({n: given[n] for n in small_names})
    pack_m = _pack_small({n: given["m_" + n] for n in small_names})
    pack_v = _pack_small({n: given["v_" + n] for n in small_names})
    _, sd, sm, sv = _adamw("adamw_small", pack_w, tot, pack_m, pack_v)

    grads, deltas, new_m, new_v = {}, {}, {}, {}
    for n in small_names:
        shp = given[n].shape
        grads[n] = _unpack_small(tot, n).reshape(shp)
        deltas[n] = _unpack_small(sd, n).reshape(shp)
        new_m[n] = _unpack_small(sm, n).reshape(shp)
        new_v[n] = _unpack_small(sv, n).reshape(shp)
    g_big["w_ada"] = g_w_ada
    for n in ["w_ada"] + big_names[1:] + big_names[:1]:
        if n == "w_in":
            gw_in, sems, land = pending
            done = [tot, sd] + [deltas[k] for k in ["w_ada"] + big_names[1:]]
            (gw_in,), (got,) = _ici_wait("ici_wait_w_in", [n], sems, [gw_in], [land], done)
            g_big[n] = _final_sum("final_w_in", pos, 1, gw_in, got, SHARD[n])
        g, d, nm, nv = _adamw("adamw_" + n, given[n][0], g_big[n], given["m_" + n][0], given["v_" + n][0])
        grads[n], deltas[n], new_m[n], new_v[n] = g[None], d[None], nm[None], nv[None]

    order = ["w_ada", "b_ada", "norm1_g", "w_in", "rel_bias", "ret_gn_g", "ret_gn_b", "w_ret_out",
             "w_att_out", "w_o", "norm2_g", "w_ff1", "w_ff2", "norm_f_g"]
    loss_out = _unpack_small(tot, "loss")[0, 0]
    return (loss_out, grad_x[None], *[grads[n] for n in order], *[deltas[n] for n in order],
            *[new_m[n] for n in order], *[new_v[n] for n in order])
```

```python
import functools
import math

import jax
import jax.numpy as jnp
import numpy as np
from jax import lax
from jax.experimental import pallas as pl
from jax.experimental.pallas import tpu as pltpu

F32 = jnp.float32
BF16 = jnp.bfloat16
I32 = jnp.int32

SEQ = 2048
D_MODEL = 1024
RET_HEADS = 4
RET_DK = 256
RET_DV = 512
RET_CHUNK = 128
RET_SUB = 2
RET_QK_W = RET_HEADS * RET_DK
RET_V_W = RET_HEADS * RET_DV
ATT_GROUPS = ((128, 1), (512, 4), (2048, 16))
ATT_HPG = 4
ATT_DH = 128
ATT_W = ATT_HPG * ATT_DH
ATT_BLK = 128
N_BLK = SEQ // ATT_BLK
REL_BUCKETS = 32
REL_MAX_DIST = 2048
N_ATT_HEADS = 12
D_FF = 4 * D_MODEL
RMS_EPS = 1e-6
GN_EPS = 1e-5
ROPE_BASE = 10000.0
IN_COLS = 2 * RET_QK_W + 2 * RET_V_W + 9 * ATT_W + 2 * D_MODEL
OFF_Q, OFF_K, OFF_V, OFF_G = 0, RET_QK_W, 2 * RET_QK_W, 2 * RET_QK_W + RET_V_W
OFF_ATT = 2 * RET_QK_W + 2 * RET_V_W
OFF_GATE = OFF_ATT + 9 * ATT_W
N_CHIPS = 4
N_DEV = 8
ADA_COLS = 6 * D_MODEL

ADAM_LR = 0.001
ADAM_B1 = 0.9
ADAM_B2 = 0.999
ADAM_EPS = 1e-08
ADAM_WD = 0.01
ADAM_STEP = 10

VMEM_LIMIT_V7X = 56 * 1024 * 1024
MESH = pl.DeviceIdType.MESH


def _cparams(sem):
    return pltpu.CompilerParams(dimension_semantics=sem, vmem_limit_bytes=VMEM_LIMIT_V7X)


def _sigmoid(v):
    return 1.0 / (1.0 + jnp.exp(-v))


def _rowmap(name, body, row_ins, bcast_ins, row_outs, sum_outs=(), tm=256, after=()):
    m = row_ins[0].shape[0]
    n_in = len(row_ins) + len(bcast_ins)
    n_ro = len(row_outs)

    def kern(*refs):
        vals = [r[...] for r in refs[:n_in]]
        res = body(*vals)
        if not isinstance(res, (tuple, list)):
            res = (res,)
        outs = refs[n_in + len(after):]
        for r, v in zip(outs[:n_ro], res[:n_ro]):
            r[...] = v.astype(r.dtype)
        if sum_outs:
            @pl.when(pl.program_id(0) == 0)
            def _():
                for r in outs[n_ro:]:
                    r[...] = jnp.zeros_like(r)
            for r, v in zip(outs[n_ro:], res[n_ro:]):
                r[...] += v

    in_specs = [pl.BlockSpec((tm, a.shape[1]), lambda i: (i, 0)) for a in row_ins]
    in_specs += [pl.BlockSpec(a.shape, lambda i: (0, 0)) for a in bcast_ins]
    in_specs += [pl.BlockSpec(memory_space=pl.ANY)] * len(after)
    out_specs = [pl.BlockSpec((tm, n), lambda i: (i, 0)) for n, _ in row_outs]
    out_specs += [pl.BlockSpec((1, n), lambda i: (0, 0)) for n in sum_outs]
    out_shape = [jax.ShapeDtypeStruct((m, n), dt) for n, dt in row_outs]
    out_shape += [jax.ShapeDtypeStruct((1, n), F32) for n in sum_outs]
    return pl.pallas_call(
        kern, name=name, grid=(m // tm,), in_specs=in_specs, out_specs=out_specs,
        out_shape=out_shape, compiler_params=_cparams(("arbitrary",)),
    )(*row_ins, *bcast_ins, *after)


TM, TN = 1024, 1024


def _piece_chunks(piece, width):
    arr, stacked = piece
    return arr.shape[0] if stacked else arr.shape[1] // width


def _piece_spec(piece, rows, width, start, row_of, chunk_of):
    arr, stacked = piece
    last = _piece_chunks(piece, width) - 1

    def local(*ids):
        return jnp.clip(chunk_of(*ids) - start, 0, last)

    def row(*ids):
        rel = chunk_of(*ids) - start
        return jnp.where(jnp.logical_and(rel >= 0, rel <= last), row_of(*ids), 0)

    if stacked:
        return pl.BlockSpec((None, rows, width), lambda *ids: (local(*ids), row(*ids), 0))
    return pl.BlockSpec((rows, width), lambda *ids: (row(*ids), local(*ids)))


def _piece_starts(pieces, width):
    return [sum(_piece_chunks(p, width) for p in pieces[:q]) for q in range(len(pieces))]


def _matmul(name, a, b, kind, m, n, k, outs, *, b_off=0, tm=TM, tn=TN, tk=1024,
            epilogue=None, extras=(), after=()):
    tm, tn, tk = min(tm, m), min(tn, n), min(tk, k)
    nk = k // tk
    pieces = a if isinstance(a, list) else [(a, False)]
    starts = _piece_starts(pieces, tk)
    if kind == "nn":
        a_specs = [pl.BlockSpec((tm, tk), lambda i, j, kk: (i, kk))]
        b_spec = pl.BlockSpec((tk, tn), lambda i, j, kk: (kk, b_off // tn + j))
        dn = (((1,), (0,)), ((), ()))
    elif kind == "nt":
        a_specs = [_piece_spec(p, tm, tk, st, lambda i, j, kk: i, lambda i, j, kk: kk)
                   for p, st in zip(pieces, starts)]
        b_spec = pl.BlockSpec((tn, tk), lambda i, j, kk: (j, b_off // tk + kk))
        dn = (((1,), (1,)), ((), ()))
    else:
        a_specs = [pl.BlockSpec((tk, tm), lambda i, j, kk: (kk, i))]
        b_spec = pl.BlockSpec((tk, tn), lambda i, j, kk: (kk, j))
        dn = (((0,), (0,)), ((), ()))
    n_a, n_ex, n_out = len(pieces), len(extras), len(outs)
    if epilogue is None:
        epilogue = lambda acc: (acc,)

    def finish(acc, ex_refs, out_refs):
        res = epilogue(acc, *[r[...] for r in ex_refs])
        for r, v in zip(out_refs, res):
            r[...] = v.astype(r.dtype)

    n_in = n_a + 1 + n_ex + len(after)

    def kern(*refs):
        a_refs, b_ref = refs[:n_a], refs[n_a]
        ex_refs = refs[n_a + 1:n_a + 1 + n_ex]
        out_refs = refs[n_in:n_in + n_out]
        kk = pl.program_id(2)
        dot = lambda a_ref: lax.dot_general(a_ref[...], b_ref[...], dn, preferred_element_type=F32)
        if nk == 1:
            finish(dot(a_refs[0]), ex_refs, out_refs)
            return
        acc_ref = refs[n_in + n_out]
        if n_a == 1:
            part = dot(a_refs[0])

            @pl.when(kk == 0)
            def _():
                acc_ref[...] = part

            @pl.when(kk > 0)
            def _():
                acc_ref[...] += part
        else:
            @pl.when(kk == 0)
            def _():
                acc_ref[...] = jnp.zeros_like(acc_ref)

            for q in range(n_a):
                @pl.when(jnp.logical_and(kk >= starts[q], kk < starts[q] + _piece_chunks(pieces[q], tk)))
                def _(q=q):
                    acc_ref[...] += dot(a_refs[q])

        @pl.when(kk == nk - 1)
        def _():
            finish(acc_ref[...], ex_refs, out_refs)

    in_specs = a_specs + [b_spec] + [pl.BlockSpec(bs, im) for _, bs, im in extras]
    in_specs += [pl.BlockSpec(memory_space=pl.ANY)] * len(after)
    return pl.pallas_call(
        kern, name=name, grid=(m // tm, n // tn, nk), in_specs=in_specs,
        out_specs=[pl.BlockSpec((tm, tn), lambda i, j, kk: (i, j)) for _ in outs],
        out_shape=[jax.ShapeDtypeStruct((m, n), dt) for dt in outs],
        scratch_shapes=[] if nk == 1 else [pltpu.VMEM((tm, tn), F32)],
        compiler_params=_cparams(("parallel", "parallel", "arbitrary")),
    )(*[p[0] for p in pieces], b, *[e[0] for e in extras], *after)


def _ici_copies(psum_ref, recv_ref, s_sem, r_sem, axis, shard_shape):
    x, y, c = _me()
    hr, cw = shard_shape[0] // 2, shard_shape[1]
    pick = lambda sems, j: sems[j] if isinstance(sems, (list, tuple)) else sems.at[j]
    copies = []
    for j, (fx, fy) in enumerate(_CHIP_FLIPS):
        chip = 2 * (x ^ fx) + (y ^ fy)
        src = psum_ref.at[:, pl.ds(chip * cw, cw)] if axis == 1 else psum_ref.at[pl.ds(chip * hr, hr), :]
        copies.append(pltpu.make_async_remote_copy(
            src_ref=src, dst_ref=recv_ref.at[j], send_sem=pick(s_sem, j), recv_sem=pick(r_sem, j),
            device_id=(x ^ fx, y ^ fy, c), device_id_type=MESH))
    return copies


_HBM_SPEC = pl.BlockSpec(memory_space=pltpu.HBM)
_SEM_SPEC = pl.BlockSpec(memory_space=pltpu.SEMAPHORE)


def _split_ici_copies(names, p_refs, land_refs, sems):
    copies = []
    for i, n in enumerate(names):
        copies += _ici_copies(p_refs[i], land_refs[i], list(sems[6 * i:6 * i + 3]),
                              list(sems[6 * i + 3:6 * i + 6]), dict(BIG)[n], SHARD[n])
    return copies


def _ici_start(name, names, psums):
    nw, ns = len(names), 6 * len(names)
    lands = [lax.empty((3, SHARD[n][0] // 2, SHARD[n][1]), BF16) for n in names]

    def body(*refs):
        for cp in _split_ici_copies(names, refs[:nw], refs[nw:2 * nw], refs[2 * nw:2 * nw + ns]):
            cp.start()
        token = refs[-1]
        token[...] = jnp.zeros_like(token)

    res = pl.pallas_call(
        body, name=name,
        out_shape=(pltpu.SemaphoreType.DMA(()),) * ns
        + tuple(pltpu.HBM(a.shape, BF16) for a in list(psums) + lands)
        + (jax.ShapeDtypeStruct((8, 128), F32),),
        in_specs=(_HBM_SPEC,) * (2 * nw),
        out_specs=(_SEM_SPEC,) * ns + (_HBM_SPEC,) * (2 * nw) + (pl.BlockSpec(memory_space=pltpu.VMEM),),
        input_output_aliases={k: ns + k for k in range(2 * nw)},
        compiler_params=pltpu.CompilerParams(has_side_effects=pltpu.SideEffectType.DATAFLOW_SIDE_EFFECTING),
    )(*[pltpu.with_memory_space_constraint(a, pltpu.HBM) for a in list(psums) + lands])
    return res[:ns], res[ns:ns + nw], res[ns + nw:ns + 2 * nw], res[-1]


def _ici_wait(name, names, sems, p_thru, land_thru, after):
    nw, ns = len(names), 6 * len(names)

    def body(*refs):
        for cp in _split_ici_copies(names, refs[:nw], refs[nw:2 * nw], refs[2 * nw:2 * nw + ns]):
            cp.wait_send()
            cp.wait_recv()

    res = pl.pallas_call(
        body, name=name,
        out_shape=tuple(pltpu.HBM(a.shape, BF16) for a in list(p_thru) + list(land_thru)),
        in_specs=(_HBM_SPEC,) * (2 * nw) + (_SEM_SPEC,) * ns + (pl.BlockSpec(memory_space=pl.ANY),) * len(after),
        out_specs=(_HBM_SPEC,) * (2 * nw), input_output_aliases={k: k for k in range(2 * nw)},
        compiler_params=pltpu.CompilerParams(has_side_effects=pltpu.SideEffectType.DATAFLOW_SIDE_EFFECTING),
    )(*p_thru, *land_thru, *sems, *after)
    return res[:nw], res[nw:]


def _where_am_i():
    x, y, c = _me()
    return jnp.stack([c, 2 * x + y]).astype(I32)


def _sibling():
    x, y, c = _me()
    return (x, y, 1 - c)


N_SEND_SLOTS = 2


def _matmul_tn_pair(name, pos, a, b, m, n, k, shard_rows, *, tm, tn, tk):
    hr = shard_rows // 2
    tm, tn, tk = min(tm, hr), min(tn, n), min(tk, k)
    tph = hr // tm
    nt, nj, nk = (m // 2) // tm, n // tn, k // tk
    n_tiles = nt * nj

    def row_block(p, t, pos_ref):
        half = jnp.where(p == 0, 1 - pos_ref[0], pos_ref[0])
        return (t // tph) * (2 * tph) + half * tph + t % tph

    pieces = b if isinstance(b, list) else [(b, False)]
    starts = _piece_starts(pieces, tn)
    n_b = len(pieces)

    def kern(pos_ref, a_ref, *rest):
        b_refs = rest[:n_b]
        o_ref, acc_ref, send_buf, land_buf, s_sem, r_sem = rest[n_b:]
        p, t, j, kk = pl.program_id(0), pl.program_id(1), pl.program_id(2), pl.program_id(3)
        idx = t * nj + j
        sib = _sibling()

        def copy(i):
            return pltpu.make_async_remote_copy(
                src_ref=send_buf.at[i % N_SEND_SLOTS], dst_ref=land_buf.at[i], send_sem=s_sem.at[i],
                recv_sem=r_sem.at[i], device_id=sib, device_id_type=MESH)

        @pl.when(kk == 0)
        def _():
            acc_ref[...] = jnp.zeros_like(acc_ref)

        for q in range(n_b):
            @pl.when(jnp.logical_and(j >= starts[q], j < starts[q] + _piece_chunks(pieces[q], tn)))
            def _(q=q):
                acc_ref[...] += lax.dot_general(a_ref[...], b_refs[q][...], _TN, preferred_element_type=F32)

        @pl.when(jnp.logical_and(kk == nk - 1, p == 0))
        def _():
            @pl.when(idx >= N_SEND_SLOTS)
            def _():
                copy(idx - N_SEND_SLOTS).wait_send()

            send_buf[idx % N_SEND_SLOTS] = acc_ref[...].astype(BF16)
            copy(idx).start()

        @pl.when(jnp.logical_and(kk == nk - 1, p == 1))
        def _():
            copy(idx).wait_recv()
            o_ref[...] = (acc_ref[...] + land_buf[idx].astype(F32)).astype(BF16)

        @pl.when(jnp.logical_and(jnp.logical_and(p == 1, idx == n_tiles - 1), kk == nk - 1))
        def _():
            for i in range(max(n_tiles - N_SEND_SLOTS, 0), n_tiles):
                copy(i).wait_send()

    grid_spec = pltpu.PrefetchScalarGridSpec(
        num_scalar_prefetch=1, grid=(2, nt, nj, nk),
        in_specs=[pl.BlockSpec((tk, tm), lambda p, t, j, kk, pos_ref: (kk, row_block(p, t, pos_ref)))]
        + [_piece_spec(pc, tk, tn, st, lambda p, t, j, kk, pos_ref: kk, lambda p, t, j, kk, pos_ref: j)
           for pc, st in zip(pieces, starts)],
        out_specs=pl.BlockSpec((tm, tn), lambda p, t, j, kk, pos_ref: (p * t, p * j)),
        scratch_shapes=[pltpu.VMEM((tm, tn), F32), pltpu.VMEM((N_SEND_SLOTS, tm, tn), BF16),
                        pltpu.VMEM((n_tiles, tm, tn), BF16),
                        pltpu.SemaphoreType.DMA((n_tiles,)), pltpu.SemaphoreType.DMA((n_tiles,))])
    return pl.pallas_call(
        kern, name=name, grid_spec=grid_spec, out_shape=jax.ShapeDtypeStruct((m // 2, n), BF16),
        compiler_params=_cparams(("arbitrary",) * 4),
    )(pos, a, *[pc[0] for pc in pieces])


def _rope_tables():
    half = RET_DK // 2
    f32 = np.float32
    inv = np.power(f32(ROPE_BASE), -np.arange(half, dtype=f32) / f32(half)).astype(f32)
    ang = (np.arange(SEQ, dtype=f32)[:, None] * inv[None, :]).astype(f32)
    return jnp.asarray(np.cos(ang).astype(f32)), jnp.asarray(np.sin(ang).astype(f32))


def _decay_tables():
    c = RET_CHUNK
    f32 = np.float32
    log_g = np.log1p(-np.power(f32(2.0), f32(-5.0) - np.arange(RET_HEADS, dtype=f32))).astype(f32)
    idx = np.arange(c, dtype=f32)
    rel = idx[:, None] - idx[None, :]
    din = np.where(rel >= 0, np.exp(log_g[:, None, None] * np.maximum(rel, f32(0.0))), f32(0.0)).astype(f32)
    qd = np.exp(log_g[:, None] * (idx + f32(1.0))).astype(f32)[:, :, None]
    kd = np.exp(log_g[:, None] * (f32(c) - f32(1.0) - idx)).astype(f32)[:, :, None]
    cd = np.exp(log_g * f32(c)).astype(f32)
    return jnp.asarray(din), jnp.asarray(qd), jnp.asarray(kd), jnp.asarray(cd)


def _t5_bucket(dist):
    max_exact = REL_BUCKETS // 2
    d_f = jnp.maximum(dist, 1).astype(F32)
    large = max_exact + (jnp.log(d_f / max_exact) / math.log(REL_MAX_DIST / max_exact)
                         * (REL_BUCKETS - max_exact)).astype(I32)
    large = jnp.minimum(large, REL_BUCKETS - 1)
    return jnp.where(dist < max_exact, dist, large)


def _bucket_tables():
    qi = jnp.arange(ATT_BLK)[:, None]
    kj = jnp.arange(2 * ATT_BLK)[None, :]
    dist = jnp.clip(ATT_BLK + qi - kj, 0, ATT_BLK)
    return jnp.stack([_t5_bucket(dist * dil) for _, dil in ATT_GROUPS]).astype(I32)


def _permute_rows(t, dil):
    if dil == 1:
        return t
    s, w = t.shape
    return t.reshape(s // dil, dil, w).transpose(1, 0, 2).reshape(s, w)


def _unpermute_rows(t, dil):
    if dil == 1:
        return t
    s, w = t.shape
    return t.reshape(dil, s // dil, w).transpose(1, 0, 2).reshape(s, w)


def _retention_fwd(rqk, rv, din, qd, kd, cd):
    nc = SEQ // RET_CHUNK
    c, dk, dv = RET_CHUNK, RET_DK, RET_DV

    def kern(q_ref, k_ref, v_ref, din_ref, qd_ref, kd_ref, cd_ref, o_ref, st_ref, state):
        n = pl.program_id(0)

        @pl.when(n == 0)
        def _():
            state[...] = jnp.zeros_like(state)

        for sub in range(RET_SUB):
            rows = slice(sub * c, (sub + 1) * c)
            for h in range(RET_HEADS):
                q, k = q_ref[rows, h * dk:(h + 1) * dk], k_ref[rows, h * dk:(h + 1) * dk]
                v = v_ref[rows, h * dv:(h + 1) * dv]
                s_b = state[h].astype(BF16)
                st_ref[h, sub] = s_b
                a = lax.dot_general(q, k, _NT, preferred_element_type=F32) * din_ref[h]
                o = jnp.dot(a.astype(BF16), v, preferred_element_type=F32)
                o += jnp.dot(q, s_b, preferred_element_type=F32) * qd_ref[h]
                o_ref[rows, h * dv:(h + 1) * dv] = o
                kk = (k.astype(F32) * kd_ref[h]).astype(BF16)
                state[h] = state[h] * cd_ref[h] + lax.dot_general(kk, v, _TN, preferred_element_type=F32)

    whole = lambda a: pl.BlockSpec(a.shape, lambda n: (0,) * a.ndim)
    cs = RET_SUB * c
    return pl.pallas_call(
        kern, name="retention_fwd", grid=(nc // RET_SUB,),
        in_specs=[
            pl.BlockSpec((cs, RET_QK_W), lambda n: (n, 0)),
            pl.BlockSpec((cs, RET_QK_W), lambda n: (n, 1)),
            pl.BlockSpec((cs, RET_V_W), lambda n: (n, 0)),
            whole(din), whole(qd), whole(kd),
            pl.BlockSpec(memory_space=pltpu.SMEM),
        ],
        out_specs=[
            pl.BlockSpec((cs, RET_V_W), lambda n: (n, 0)),
            pl.BlockSpec((RET_HEADS, RET_SUB, dk, dv), lambda n: (0, n, 0, 0)),
        ],
        out_shape=[
            jax.ShapeDtypeStruct((SEQ, RET_V_W), F32),
            jax.ShapeDtypeStruct((RET_HEADS, nc, dk, dv), BF16),
        ],
        scratch_shapes=[pltpu.VMEM((RET_HEADS, dk, dv), F32)],
        compiler_params=_cparams(("arbitrary",)),
    )(rqk, rqk, rv, din, qd, kd, cd)


def _retention_bwd(rqk, rv, states, d_ro, din, qd, kd, cd, cos, sin):
    nc = SEQ // RET_CHUNK
    c, dk, dv = RET_CHUNK, RET_DK, RET_DV
    half = dk // 2
    last = nc // RET_SUB - 1

    def unrot(g, cs, sn):
        g1, g2 = g[:, :half], g[:, half:]
        return jnp.concatenate([g1 * cs + g2 * sn, g2 * cs - g1 * sn], axis=-1)

    def kern(q_ref, k_ref, v_ref, st_ref, do_ref, din_ref, qd_ref, kd_ref, cd_ref, cos_ref, sin_ref,
             out_ref, dstate):
        step = pl.program_id(0)

        @pl.when(step == 0)
        def _():
            dstate[...] = jnp.zeros_like(dstate)

        for sub in reversed(range(RET_SUB)):
            rows = slice(sub * c, (sub + 1) * c)
            cs, sn = cos_ref[rows, :], sin_ref[rows, :]
            for h in range(RET_HEADS):
                qk_cols, v_cols = slice(h * dk, (h + 1) * dk), slice(h * dv, (h + 1) * dv)
                q, k, v = q_ref[rows, qk_cols], k_ref[rows, qk_cols], v_ref[rows, v_cols]
                s_b = st_ref[h, sub]
                d_o = do_ref[rows, v_cols]
                d_ob = d_o.astype(BF16)
                d_oq = (d_o * qd_ref[h]).astype(BF16)
                ds_b = dstate[h].astype(BF16)
                din_m = din_ref[h]
                a_b = (lax.dot_general(q, k, _NT, preferred_element_type=F32) * din_m).astype(BF16)
                kk = (k.astype(F32) * kd_ref[h]).astype(BF16)
                d_v = lax.dot_general(a_b, d_ob, _TN, preferred_element_type=F32)
                d_v += jnp.dot(kk, ds_b, preferred_element_type=F32)
                d_a = (lax.dot_general(d_ob, v, _NT, preferred_element_type=F32) * din_m).astype(BF16)
                d_q = jnp.dot(d_a, k, preferred_element_type=F32)
                d_q += lax.dot_general(d_oq, s_b, _NT, preferred_element_type=F32)
                d_k = lax.dot_general(d_a, q, _TN, preferred_element_type=F32)
                d_k += lax.dot_general(v, ds_b, _NT, preferred_element_type=F32) * kd_ref[h]
                dstate[h] = dstate[h] * cd_ref[h] + lax.dot_general(q, d_oq, _TN,
                                                                    preferred_element_type=F32)
                out_ref[rows, h * dk:(h + 1) * dk] = unrot(d_q, cs, sn).astype(BF16)
                out_ref[rows, RET_QK_W + h * dk:RET_QK_W + (h + 1) * dk] = (
                    unrot(d_k, cs, sn) * (RET_DK ** -0.5)).astype(BF16)
                out_ref[rows, 2 * RET_QK_W + h * dv:2 * RET_QK_W + (h + 1) * dv] = d_v.astype(BF16)

    whole = lambda a: pl.BlockSpec(a.shape, lambda n: (0,) * a.ndim)
    rs = RET_SUB * c
    return pl.pallas_call(
        kern, name="retention_bwd", grid=(nc // RET_SUB,),
        in_specs=[
            pl.BlockSpec((rs, RET_QK_W), lambda n: (last - n, 0)),
            pl.BlockSpec((rs, RET_QK_W), lambda n: (last - n, 1)),
            pl.BlockSpec((rs, RET_V_W), lambda n: (last - n, 0)),
            pl.BlockSpec((RET_HEADS, RET_SUB, dk, dv), lambda n: (0, last - n, 0, 0)),
            pl.BlockSpec((rs, RET_V_W), lambda n: (last - n, 0)),
            whole(din), whole(qd), whole(kd),
            pl.BlockSpec(memory_space=pltpu.SMEM),
            pl.BlockSpec((rs, half), lambda n: (last - n, 0)),
            pl.BlockSpec((rs, half), lambda n: (last - n, 0)),
        ],
        out_specs=pl.BlockSpec((rs, 2 * RET_QK_W + RET_V_W), lambda n: (last - n, 0)),
        out_shape=jax.ShapeDtypeStruct((SEQ, 2 * RET_QK_W + RET_V_W), BF16),
        scratch_shapes=[pltpu.VMEM((RET_HEADS, dk, dv), F32)],
        compiler_params=_cparams(("arbitrary",)),
    )(rqk, rqk, rv, states, d_ro, din, qd, kd, cd, cos, sin)


def _bias_build(rel_bias, buckets):
    ng = len(ATT_GROUPS)

    def kern(tab_ref, bkt_ref, o_ref):
        g, h = pl.program_id(0), pl.program_id(1)
        bkt = bkt_ref[...]
        acc = jnp.zeros(bkt.shape, F32)
        for b in range(REL_BUCKETS):
            acc = jnp.where(bkt == b, tab_ref[b, g * ATT_HPG + h], acc)
        o_ref[...] = acc

    return pl.pallas_call(
        kern, name="bias_build", grid=(ng, ATT_HPG),
        in_specs=[pl.BlockSpec(memory_space=pltpu.SMEM),
                  pl.BlockSpec((None, ATT_BLK, 2 * ATT_BLK), lambda g, h: (g, 0, 0))],
        out_specs=pl.BlockSpec((None, None, ATT_BLK, 2 * ATT_BLK), lambda g, h: (g, h, 0, 0)),
        out_shape=jax.ShapeDtypeStruct((ng, ATT_HPG, ATT_BLK, 2 * ATT_BLK), F32),
        compiler_params=_cparams(("arbitrary", "arbitrary")),
    )(rel_bias, buckets)


def _bias_grad(dsb, buckets):
    ng = len(ATT_GROUPS)

    def kern(ds_ref, bkt_ref, o_ref):
        g, h = pl.program_id(0), pl.program_id(1)
        bkt, ds = bkt_ref[...], ds_ref[...]
        for b in range(REL_BUCKETS):
            o_ref[b, g * ATT_HPG + h] = jnp.sum(jnp.where(bkt == b, ds, 0.0))

    return pl.pallas_call(
        kern, name="bias_grad", grid=(ng, ATT_HPG),
        in_specs=[pl.BlockSpec((None, None, ATT_BLK, 2 * ATT_BLK), lambda g, h: (g, h, 0, 0)),
                  pl.BlockSpec((None, ATT_BLK, 2 * ATT_BLK), lambda g, h: (g, 0, 0))],
        out_specs=pl.BlockSpec(memory_space=pltpu.SMEM),
        out_shape=jax.ShapeDtypeStruct((REL_BUCKETS, N_ATT_HEADS), F32),
        compiler_params=_cparams(("arbitrary", "arbitrary")),
    )(dsb, buckets)


_NT = (((1,), (1,)), ((), ()))
_TN = (((0,), (0,)), ((), ()))
_ATT_SCALE = ATT_DH ** -0.5


_PAD_ROWS = SEQ + ATT_BLK
ATT_UNROLL = 16


def _window_mask(has_prev):
    qi = lax.broadcasted_iota(I32, (ATT_BLK, 2 * ATT_BLK), 0)
    kj = lax.broadcasted_iota(I32, (ATT_BLK, 2 * ATT_BLK), 1)
    prev_ok = jnp.logical_and(jnp.logical_and(kj < ATT_BLK, kj >= qi), has_prev)
    return jnp.logical_or(prev_ok, jnp.logical_and(kj >= ATT_BLK, qi >= kj - ATT_BLK))


def _head_specs(col0):
    return pl.BlockSpec((SEQ, ATT_DH), lambda h: (0, col0 + h))


def _att_fwd(gi, qkv, bias, nb):
    blk, dh = ATT_BLK, ATT_DH

    def kern(q_ref, k_ref, v_ref, b_ref, o_ref, l_ref, kpad, vpad):
        zero = jnp.zeros((blk, dh), BF16)
        kpad[0:blk, :] = zero
        vpad[0:blk, :] = zero
        kpad[blk:, :] = k_ref[...]
        vpad[blk:, :] = v_ref[...]
        bias_m = b_ref[...]

        def body(b, carry):
            r0 = pl.multiple_of(b * blk, blk)
            q = q_ref[pl.ds(r0, blk), :]
            kw = kpad[pl.ds(r0, 2 * blk), :]
            vw = vpad[pl.ds(r0, 2 * blk), :]
            valid = _window_mask((b % nb) > 0)
            s = lax.dot_general(q, kw, _NT, preferred_element_type=F32) * _ATT_SCALE + bias_m
            s = jnp.where(valid, s, -1e30)
            mx = jnp.max(s, axis=-1, keepdims=True)
            e = jnp.exp(s - mx)
            den = jnp.sum(e, axis=-1, keepdims=True)
            o_ref[pl.ds(r0, blk), :] = jnp.dot((e / den).astype(BF16), vw, preferred_element_type=F32)
            l_ref[pl.ds(r0, blk), :] = jnp.broadcast_to(mx + jnp.log(den), (blk, dh))
            return carry

        lax.fori_loop(0, N_BLK, body, 0, unroll=ATT_UNROLL)

    return pl.pallas_call(
        kern, name=f"att_fwd_g{gi}", grid=(ATT_HPG,),
        in_specs=[_head_specs(0), _head_specs(ATT_HPG), _head_specs(2 * ATT_HPG),
                  pl.BlockSpec((None, None, blk, 2 * blk), lambda h: (gi, h, 0, 0))],
        out_specs=[_head_specs(0), _head_specs(0)],
        out_shape=[jax.ShapeDtypeStruct((SEQ, ATT_W), F32), jax.ShapeDtypeStruct((SEQ, ATT_W), F32)],
        scratch_shapes=[pltpu.VMEM((_PAD_ROWS, dh), BF16), pltpu.VMEM((_PAD_ROWS, dh), BF16)],
        compiler_params=_cparams(("arbitrary",)),
    )(qkv, qkv, qkv, bias)


def _att_bwd(gi, qkv, d_att, lse, dd, bias, nb):
    blk, dh = ATT_BLK, ATT_DH

    def kern(q_ref, k_ref, v_ref, do_ref, l_ref, d_ref, b_ref, dqkv_ref, dsb_ref,
             kpad, vpad, qpad, dopad, lpad, dpad):
        zero = jnp.zeros((blk, dh), BF16)
        zero_f = jnp.zeros((blk, dh), F32)
        kpad[0:blk, :] = zero
        vpad[0:blk, :] = zero
        kpad[blk:, :] = k_ref[...]
        vpad[blk:, :] = v_ref[...]
        qpad[SEQ:, :] = zero
        dopad[SEQ:, :] = zero
        lpad[SEQ:, :] = zero_f
        dpad[SEQ:, :] = zero_f
        qpad[0:SEQ, :] = q_ref[...]
        dopad[0:SEQ, :] = do_ref[...]
        lpad[0:SEQ, :] = l_ref[...]
        dpad[0:SEQ, :] = d_ref[...]
        bias_m = b_ref[...]
        bias_t = jnp.concatenate([bias_m[:, blk:], bias_m[:, :blk]], axis=0)
        dsb_ref[...] = jnp.zeros_like(dsb_ref)

        def dq_body(b, carry):
            r0 = pl.multiple_of(b * blk, blk)
            q, d_o = q_ref[pl.ds(r0, blk), :], do_ref[pl.ds(r0, blk), :]
            kw, vw = kpad[pl.ds(r0, 2 * blk), :], vpad[pl.ds(r0, 2 * blk), :]
            lrow, drow = l_ref[pl.ds(r0, blk), :][:, :1], d_ref[pl.ds(r0, blk), :][:, :1]
            valid = _window_mask((b % nb) > 0)
            s = lax.dot_general(q, kw, _NT, preferred_element_type=F32) * _ATT_SCALE + bias_m
            p = jnp.where(valid, jnp.exp(jnp.where(valid, s, -1e30) - lrow), 0.0)
            dp = lax.dot_general(d_o, vw, _NT, preferred_element_type=F32)
            ds = p * (dp - drow)
            dq = jnp.dot(ds.astype(BF16), kw, preferred_element_type=F32)
            dqkv_ref[0, pl.ds(r0, blk), :] = (dq * _ATT_SCALE).astype(BF16)
            dsb_ref[...] += ds
            return carry

        lax.fori_loop(0, N_BLK, dq_body, 0, unroll=ATT_UNROLL)

        qi = lax.broadcasted_iota(I32, (2 * blk, blk), 0)
        kj = lax.broadcasted_iota(I32, (2 * blk, blk), 1)

        def dkv_body(b, carry):
            r0 = pl.multiple_of(b * blk, blk)
            k, v = k_ref[pl.ds(r0, blk), :], v_ref[pl.ds(r0, blk), :]
            qw, dow = qpad[pl.ds(r0, 2 * blk), :], dopad[pl.ds(r0, 2 * blk), :]
            lrow, drow = lpad[pl.ds(r0, 2 * blk), :][:, :1], dpad[pl.ds(r0, 2 * blk), :][:, :1]
            has_next = jnp.logical_and(b + 1 < N_BLK, ((b + 1) % nb) > 0)
            next_ok = jnp.logical_and(jnp.logical_and(qi >= blk, kj >= qi - blk), has_next)
            valid = jnp.logical_or(jnp.logical_and(qi < blk, qi >= kj), next_ok)
            s = lax.dot_general(qw, k, _NT, preferred_element_type=F32) * _ATT_SCALE + bias_t
            p = jnp.where(valid, jnp.exp(jnp.where(valid, s, -1e30) - lrow), 0.0)
            dp = lax.dot_general(dow, v, _NT, preferred_element_type=F32)
            ds = p * (dp - drow)
            d_v = lax.dot_general(p.astype(BF16), dow, _TN, preferred_element_type=F32)
            d_k = lax.dot_general(ds.astype(BF16), qw, _TN, preferred_element_type=F32)
            dqkv_ref[1, pl.ds(r0, blk), :] = (d_k * _ATT_SCALE).astype(BF16)
            dqkv_ref[2, pl.ds(r0, blk), :] = d_v.astype(BF16)
            return carry

        lax.fori_loop(0, N_BLK, dkv_body, 0, unroll=ATT_UNROLL)

    return pl.pallas_call(
        kern, name=f"att_bwd_g{gi}", grid=(ATT_HPG,),
        in_specs=[_head_specs(0), _head_specs(ATT_HPG), _head_specs(2 * ATT_HPG),
                  _head_specs(0), _head_specs(0), _head_specs(0),
                  pl.BlockSpec((None, None, blk, 2 * blk), lambda h: (gi, h, 0, 0))],
        out_specs=[pl.BlockSpec((3, SEQ, dh), lambda h: (0, 0, h)),
                   pl.BlockSpec((None, blk, 2 * blk), lambda h: (h, 0, 0))],
        out_shape=[jax.ShapeDtypeStruct((3, SEQ, ATT_W), BF16),
                   jax.ShapeDtypeStruct((ATT_HPG, blk, 2 * blk), F32)],
        scratch_shapes=[pltpu.VMEM((_PAD_ROWS, dh), BF16)] * 4 + [pltpu.VMEM((_PAD_ROWS, dh), F32)] * 2,
        compiler_params=_cparams(("arbitrary",)),
    )(qkv, qkv, qkv, d_att, lse, dd, bias)


def _rms_parts(x):
    r = lax.rsqrt(jnp.mean(x * x, axis=-1, keepdims=True) + RMS_EPS)
    return x * r, r


def _rms_bwd(d_xhat, xhat, r):
    return r * (d_xhat - xhat * jnp.mean(d_xhat * xhat, axis=-1, keepdims=True))


def _prenorm_fwd(name, x, gain, shift, scale):
    def body(xt, g, sh, sc):
        xhat, _ = _rms_parts(xt)
        return (xhat * g) * (1.0 + sc) + sh
    return _rowmap(name, body, [x], [gain, shift, scale], [(D_MODEL, BF16)])[0]


def _prenorm_bwd(name, d_h, x, gain, scale, resid, branch=None, gate=None, after=()):
    gated = branch is not None

    def body(d_ht, xt, res, *rest):
        g, sc = rest[-2 - gated], rest[-1 - gated]
        xhat, r = _rms_parts(xt)
        nrm = xhat * g
        d_n = d_ht * (1.0 + sc)
        dx = _rms_bwd(d_n * g, xhat, r) + res
        sums = (jnp.sum(d_ht, axis=0, keepdims=True), jnp.sum(d_ht * nrm, axis=0, keepdims=True),
                jnp.sum(d_n * xhat, axis=0, keepdims=True))
        if not gated:
            return (dx,) + sums
        return (dx, dx * rest[-1]) + sums + (jnp.sum(dx * rest[0], axis=0, keepdims=True),)

    return _rowmap(name, body, [d_h, x, resid] + ([branch] if gated else []),
                   [gain, scale] + ([gate] if gated else []),
                   [(D_MODEL, F32)] + ([(D_MODEL, BF16)] if gated else []),
                   [D_MODEL] * (3 + gated), after=after)


def _gn_parts(ro):
    mu = jnp.mean(ro, axis=-1, keepdims=True)
    cen = ro - mu
    rstd = lax.rsqrt(jnp.mean(cen * cen, axis=-1, keepdims=True) + GN_EPS)
    return cen * rstd, rstd


def _retpost_fwd(ro, rg, gn_g, gn_b):
    def body(rot, rgt, g, b):
        outs = []
        for h in range(RET_HEADS):
            sl = slice(h * RET_DV, (h + 1) * RET_DV)
            nrm, _ = _gn_parts(rot[:, sl])
            gate = rgt[:, sl]
            outs.append((gate * _sigmoid(gate)) * (nrm * g[:, sl] + b[:, sl]))
        return jnp.concatenate(outs, axis=-1)
    return _rowmap("retpost_fwd", body, [ro, rg], [gn_g, gn_b], [(RET_V_W, BF16)])[0]


def _retpost_bwd(d_gated, ro, rg, gn_g, gn_b):
    def body(dgt, rot, rgt, g, b):
        d_ro, d_rg, d_g, d_b = [], [], [], []
        for h in range(RET_HEADS):
            sl = slice(h * RET_DV, (h + 1) * RET_DV)
            nrm, rstd = _gn_parts(rot[:, sl])
            gate, dg = rgt[:, sl], dgt[:, sl]
            sg = _sigmoid(gate)
            ron = nrm * g[:, sl] + b[:, sl]
            d_rg.append(dg * ron * (sg * (1.0 + gate * (1.0 - sg))))
            d_ron = dg * (gate * sg)
            d_g.append(jnp.sum(d_ron * nrm, axis=0, keepdims=True))
            d_b.append(jnp.sum(d_ron, axis=0, keepdims=True))
            d_n = d_ron * g[:, sl]
            d_ro.append(rstd * (d_n - jnp.mean(d_n, axis=-1, keepdims=True)
                                - nrm * jnp.mean(d_n * nrm, axis=-1, keepdims=True)))
        cat = lambda ts: jnp.concatenate(ts, axis=-1)
        return cat(d_ro), cat(d_rg), cat(d_g), cat(d_b)
    return _rowmap("retpost_bwd", body, [d_gated, ro, rg], [gn_g, gn_b],
                   [(RET_V_W, F32), (RET_V_W, BF16)], [RET_V_W, RET_V_W])


def _combine(os_, ls_):
    def body(o0, o1, o2, l0, l1, l2):
        mx = jnp.maximum(jnp.maximum(l0, l1), l2)
        e0, e1, e2 = jnp.exp(l0 - mx), jnp.exp(l1 - mx), jnp.exp(l2 - mx)
        den = e0 + e1 + e2
        att = (e0 / den) * o0 + (e1 / den) * o1 + (e2 / den) * o2
        return att, att, mx + jnp.log(den)
    return _rowmap("att_combine", body, list(os_) + list(ls_), [],
                   [(ATT_W, F32), (ATT_W, BF16), (ATT_W, F32)])


def _att_bwd_pre(d_att, att):
    def body(dt, at):
        outs = []
        for h in range(ATT_HPG):
            sl = slice(h * ATT_DH, (h + 1) * ATT_DH)
            outs.append(jnp.broadcast_to(jnp.sum(dt[:, sl] * at[:, sl], axis=-1, keepdims=True),
                                         (dt.shape[0], ATT_DH)))
        return dt, jnp.concatenate(outs, axis=-1)
    return _rowmap("att_bwd_pre", body, [d_att, att], [], [(ATT_W, BF16), (ATT_W, F32)])


def _merge_fwd(gates, ret_out, att_out):
    def body(gt, ro, ao):
        return _sigmoid(gt[:, :D_MODEL]) * ro + _sigmoid(gt[:, D_MODEL:]) * ao
    return _rowmap("merge_fwd", body, [gates, ret_out, att_out], [], [(D_MODEL, BF16)])[0]


def _merge_bwd(d_merged, gates, ret_out, att_out):
    def body(dm, gt, ro, ao):
        sa, sb = _sigmoid(gt[:, :D_MODEL]), _sigmoid(gt[:, D_MODEL:])
        d_gates = jnp.concatenate([dm * ro * (sa * (1.0 - sa)), dm * ao * (sb * (1.0 - sb))], axis=-1)
        return dm * sa, dm * sb, d_gates
    return _rowmap("merge_bwd", body, [d_merged, gates, ret_out, att_out], [],
                   [(D_MODEL, BF16), (D_MODEL, BF16), (2 * D_MODEL, BF16)])


def _loss_head(x3, target, gain, branch, gate):
    def body(xt, tt, br, g, gt):
        xhat, r = _rms_parts(xt)
        err = xhat * g - tt
        d_y = err / D_MODEL
        loss = 0.5 * jnp.sum(jnp.mean(err * err, axis=-1, keepdims=True), axis=0, keepdims=True)
        d_x = _rms_bwd(d_y * g, xhat, r)
        return (d_x, d_x * gt, jnp.broadcast_to(loss, (1, 128)), jnp.sum(d_y * xhat, axis=0, keepdims=True),
                jnp.sum(d_x * br, axis=0, keepdims=True))
    return _rowmap("loss_head", body, [x3, target, branch], [gain, gate],
                   [(D_MODEL, F32), (D_MODEL, BF16)], [128, D_MODEL, D_MODEL])


def _local_step(pos, x, target, mod, norm1_g, norm2_g, norm_f_g, rel_bias, gn_g, gn_b, w_in, rest_gather):
    sh1, sc1, g1, sh2, sc2, g2 = [mod[:, i * D_MODEL:(i + 1) * D_MODEL] for i in range(6)]
    cos, sin = _rope_tables()
    din, qd, kd, cd = _decay_tables()
    buckets = _bucket_tables()
    bias = _bias_build(rel_bias, buckets)
    dils = [d for _, d in ATT_GROUPS]
    nbs = [SEQ // d // ATT_BLK for d in dils]

    h1 = _prenorm_fwd("prenorm1_fwd", x, norm1_g, sh1, sc1)
    h1_p = [_permute_rows(h1, d) for d in dils]

    def rot_epi(acc, cs, sn, scale):
        half = RET_DK // 2
        x1, x2 = acc[:, :half], acc[:, half:]
        return (jnp.concatenate([x1 * cs - x2 * sn, x1 * sn + x2 * cs], axis=-1) * scale,)

    qk_scale = jnp.concatenate([jnp.ones((1, RET_QK_W), F32),
                                jnp.full((1, RET_QK_W), RET_DK ** -0.5, F32)], axis=-1)
    rope_ex = [(cos, (TM, RET_DK // 2), lambda i, j, kk: (i, 0)),
               (sin, (TM, RET_DK // 2), lambda i, j, kk: (i, 0)),
               (qk_scale, (1, RET_DK), lambda i, j, kk: (0, j))]
    rest_sems, rest_shards, rest_fulls, rest_token = rest_gather
    behind = [rest_token]
    rv = _matmul("proj_rv", h1, w_in, "nn", SEQ, RET_V_W, D_MODEL, [BF16], b_off=OFF_V, tk=D_MODEL,
                 after=behind)[0]
    rg = _matmul("proj_rg", h1, w_in, "nn", SEQ, RET_V_W, D_MODEL, [F32], b_off=OFF_G, tk=D_MODEL,
                 after=behind)[0]
    gates = _matmul("proj_gates", h1, w_in, "nn", SEQ, 2 * D_MODEL, D_MODEL, [F32], b_off=OFF_GATE,
                    tn=512, tk=D_MODEL, after=behind)[0]
    aqkv = [_matmul(f"proj_att_g{gi}", h1_p[gi], w_in, "nn", SEQ, 3 * ATT_W, D_MODEL, [BF16],
                    b_off=OFF_ATT + gi * 3 * ATT_W, tn=512, tk=D_MODEL, after=behind)[0]
            for gi in range(3)]

    os_, ls_ = [], []
    for gi in range(3):
        o_g, l_g = _att_fwd(gi, aqkv[gi], bias, nbs[gi])
        os_.append(_unpermute_rows(o_g, dils[gi]))
        ls_.append(_unpermute_rows(l_g, dils[gi]))
        if gi == 1:
            rest_sems, rest_fulls, fwd_token = _gather_rest_forward(rest_sems, rest_shards, rest_fulls,
                                                                    [o_g, rv, rg, gates])

    rqk = _matmul("proj_qk", h1, w_in, "nn", SEQ, 2 * RET_QK_W, D_MODEL, [BF16], b_off=OFF_Q,
                  tn=RET_DK, tk=D_MODEL, epilogue=rot_epi, extras=rope_ex, after=[fwd_token])[0]
    ro, states = _retention_fwd(rqk, rv, din, qd, kd, cd)
    gated = _retpost_fwd(ro, rg, gn_g, gn_b)
    w_ret_out, w_att_out, w_o, w_ff1, w_ff2 = _gather_rest_end(rest_sems, rest_fulls, [gated, os_[2]])
    ret_out = _matmul("ret_out", gated, w_ret_out, "nn", SEQ, D_MODEL, RET_V_W, [F32], tk=RET_V_W)[0]
    att, att_b, lse = _combine(os_, ls_)
    att_out = _matmul("att_out", att_b, w_att_out, "nn", SEQ, D_MODEL, ATT_W, [F32])[0]

    merged = _merge_fwd(gates, ret_out, att_out)

    def resid_epi(acc, xt, g):
        return xt + g * acc, acc

    def resid_ex(xin, g):
        return [(xin, (TM, TN), lambda i, j, kk: (i, j)), (g, (1, TN), lambda i, j, kk: (0, j))]

    x2, mix = _matmul("mix_out", merged, w_o, "nn", SEQ, D_MODEL, D_MODEL, [F32, F32],
                      epilogue=resid_epi, extras=resid_ex(x, g1))
    h2 = _prenorm_fwd("prenorm2_fwd", x2, norm2_g, sh2, sc2)

    def relu2_epi(acc):
        r = jnp.maximum(acc, 0.0)
        return r * r, r

    act, relu_u = _matmul("ff1", h2, w_ff1, "nn", SEQ, D_FF, D_MODEL, [BF16, BF16], tk=D_MODEL,
                          epilogue=relu2_epi)
    x3, y2 = _matmul("ff2", act, w_ff2, "nn", SEQ, D_MODEL, D_FF, [F32, F32], tk=2048,
                     epilogue=resid_epi, extras=resid_ex(x2, g2))

    d_x3, d_y2, loss, d_gf, d_g2 = _loss_head(x3, target, norm_f_g, y2, g2)

    def relu2_bwd_epi(acc, rt):
        return (acc * (2.0 * rt.astype(F32)),)

    gw_ff2 = _matmul_tn_pair("ff2_dw", pos, act, d_y2, D_FF, D_MODEL, SEQ, D_FF // N_CHIPS,
                             tm=512, tn=1024, tk=1024)
    d_u = _matmul("ff2_dx", d_y2, w_ff2, "nt", SEQ, D_FF, D_MODEL, [BF16], epilogue=relu2_bwd_epi,
                  extras=[(relu_u, (TM, TN), lambda i, j, kk: (i, j))])[0]
    gw_ff1 = _matmul_tn_pair("ff1_dw", pos, h2, d_u, D_MODEL, D_FF, SEQ, D_MODEL,
                             tm=512, tn=1024, tk=1024)
    ffn = ["w_ff2", "w_ff1"]
    ffn_started = _ici_start("ici_start_ffn", ffn, [gw_ff2, gw_ff1])
    d_h2 = _matmul("ff1_dx", d_u, w_ff1, "nt", SEQ, D_MODEL, D_FF, [F32], tk=2048,
                   after=[ffn_started[3]])[0]
    d_x2, d_mix, d_sh2, d_sc2, d_n2g, d_g1 = _prenorm_bwd("prenorm2_bwd", d_h2, x2, norm2_g, sc2, d_x3,
                                                          branch=mix, gate=g1)
    gw_o = _matmul_tn_pair("mix_dw", pos, merged, d_mix, D_MODEL, D_MODEL, SEQ, D_MODEL // N_CHIPS,
                           tm=128, tn=1024, tk=2048)
    d_merged = _matmul("mix_dx", d_mix, w_o, "nt", SEQ, D_MODEL, D_MODEL, [F32])[0]
    d_ret_out, d_att_out, d_gates = _merge_bwd(d_merged, gates, ret_out, att_out)

    gw_ret_out = _matmul_tn_pair("ret_out_dw", pos, gated, d_ret_out, RET_V_W, D_MODEL, SEQ,
                                 RET_V_W // N_CHIPS, tm=256, tn=1024, tk=1024)
    gw_att_out = _matmul_tn_pair("att_out_dw", pos, att_b, d_att_out, ATT_W, D_MODEL, SEQ, ATT_W,
                                 tm=256, tn=1024, tk=2048)
    mixer = ["w_o", "w_ret_out", "w_att_out"]
    mixer_started = _ici_start("ici_start_mixer", mixer, [gw_o, gw_ret_out, gw_att_out])
    d_gated = _matmul("ret_out_dx", d_ret_out, w_ret_out, "nt", SEQ, RET_V_W, D_MODEL, [F32],
                      after=[mixer_started[3]])[0]
    d_att = _matmul("att_out_dx", d_att_out, w_att_out, "nt", SEQ, ATT_W, D_MODEL, [F32],
                    after=[mixer_started[3]])[0]

    d_ro, d_rg, d_gn_g, d_gn_b = _retpost_bwd(d_gated, ro, rg, gn_g, gn_b)
    d_rqkv = _retention_bwd(rqk, rv, states, d_ro, din, qd, kd, cd, cos, sin)

    d_att_b, dd = _att_bwd_pre(d_att, att)
    d_aqkv, dsbs = [], []
    for gi in range(3):
        da_p = _permute_rows(d_att_b, dils[gi])
        l_p = _permute_rows(lse, dils[gi])
        dd_p = _permute_rows(dd, dils[gi])
        dqkv, dsb = _att_bwd(gi, aqkv[gi], da_p, l_p, dd_p, bias, nbs[gi])
        if dils[gi] > 1:
            dqkv = dqkv.reshape(3, dils[gi], SEQ // dils[gi], ATT_W).transpose(0, 2, 1, 3).reshape(
                3, SEQ, ATT_W)
        d_aqkv.append(dqkv)
        dsbs.append(dsb)
    d_rel_bias = _bias_grad(jnp.stack(dsbs), buckets)

    d_proj = [(d_rqkv, False), (d_rg, False)] + [(t, True) for t in d_aqkv] + [(d_gates, False)]
    gw_in = _matmul_tn_pair("proj_dw", pos, h1, d_proj, D_MODEL, IN_COLS, SEQ, D_MODEL,
                            tm=512, tn=ATT_W, tk=SEQ)
    sems, (gw_in,), (land,), token = _ici_start("ici_start_w_in", ["w_in"], [gw_in])
    d_h1 = _matmul("proj_dx", d_proj, w_in, "nt", SEQ, D_MODEL, IN_COLS, [F32], tn=1024, tk=ATT_W,
                   after=[token])[0]
    pending = (sems, land)

    names = ffn + mixer
    psums, got = _ici_wait("ici_wait_rest", names, list(ffn_started[0]) + list(mixer_started[0]),
                           list(ffn_started[1]) + list(mixer_started[1]),
                           list(ffn_started[2]) + list(mixer_started[2]), [d_h1])
    g_big = {n: _final_sum("final_" + n, pos, dict(BIG)[n], psums[i], got[i], SHARD[n])
             for i, n in enumerate(names)}
    grad_x, d_sh1, d_sc1, d_n1g = _prenorm_bwd("prenorm1_bwd", d_h1, x, norm1_g, sc1, d_x2,
                                               after=list(g_big.values()))
    d_mod = jnp.concatenate([d_sh1, d_sc1, d_g1, d_sh2, d_sc2, d_g2], axis=-1)
    small = dict(norm1_g=d_n1g, norm2_g=d_n2g, norm_f_g=d_gf, gn_g=d_gn_g, gn_b=d_gn_b,
                 rel_bias=d_rel_bias)
    return loss, grad_x, d_mod, small, g_big, (gw_in,) + pending


def _me():
    return lax.axis_index("x"), lax.axis_index("y"), lax.axis_index("c")


def _peer(x, y, c, mask):
    return (x ^ ((mask >> 2) & 1), y ^ ((mask >> 1) & 1), c ^ (mask & 1))


def _gather8(src_ref, dst_ref, send_sems, recv_sems):
    x, y, c = _me()
    me = 4 * x + 2 * y + c
    copies = []
    for mask in range(1, N_DEV):
        cp = pltpu.make_async_remote_copy(
            src_ref=src_ref, dst_ref=dst_ref.at[me], send_sem=send_sems.at[mask - 1],
            recv_sem=recv_sems.at[mask - 1], device_id=_peer(x, y, c, mask), device_id_type=MESH)
        cp.start()
        copies.append(cp)
    dst_ref[me] = src_ref[...]
    for cp in copies:
        cp.wait_recv()
    for cp in copies:
        cp.wait_send()


def _ada_fwd(c_in, w_ada, b_ada):
    ncol = ADA_COLS // N_CHIPS

    def body(c_ref, w_ref, b_ref, mod_ref, sc_ref, cbuf, cg, mbuf, mg, s1, r1, s2, r2):
        x, y, c = _me()
        me = 4 * x + 2 * y + c
        cv = c_ref[...]
        cbuf[...] = jnp.broadcast_to(cv * _sigmoid(cv), cbuf.shape)
        _gather8(cbuf, cg, s1, r1)
        rows = lax.broadcasted_iota(I32, (N_DEV, D_MODEL), 0)
        sc_all = jnp.zeros((N_DEV, D_MODEL), F32)
        for d in range(N_DEV):
            sc_all = jnp.where(rows == d, cg[d], sc_all)
        sc_ref[...] = sc_all
        mbuf[...] = jnp.dot(sc_all.astype(BF16), w_ref[...].astype(BF16), preferred_element_type=F32)
        _gather8(mbuf, mg, s2, r2)
        rowsel = lax.broadcasted_iota(I32, (N_DEV, ncol), 0) == me
        for k in range(N_CHIPS):
            blk = mg[2 * k]
            row = jnp.sum(jnp.where(rowsel, blk, 0.0), axis=0, keepdims=True)
            mod_ref[:, k * ncol:(k + 1) * ncol] = row + b_ref[:, k * ncol:(k + 1) * ncol]

    vm = pl.BlockSpec(memory_space=pltpu.VMEM)
    return pl.pallas_call(
        body, name="ada_fwd",
        in_specs=[vm, vm, vm], out_specs=[vm, vm],
        out_shape=[jax.ShapeDtypeStruct((1, ADA_COLS), F32), jax.ShapeDtypeStruct((N_DEV, D_MODEL), F32)],
        scratch_shapes=[
            pltpu.VMEM((8, D_MODEL), F32), pltpu.VMEM((N_DEV, 8, D_MODEL), F32),
            pltpu.VMEM((8, ncol), F32), pltpu.VMEM((N_DEV, 8, ncol), F32),
            pltpu.SemaphoreType.DMA((N_DEV - 1,)), pltpu.SemaphoreType.DMA((N_DEV - 1,)),
            pltpu.SemaphoreType.DMA((N_DEV - 1,)), pltpu.SemaphoreType.DMA((N_DEV - 1,)),
        ],
        compiler_params=pltpu.CompilerParams(vmem_limit_bytes=VMEM_LIMIT_V7X),
    )(c_in, w_ada, b_ada)


def _small_reduce(pack, sc_all):
    ncol = ADA_COLS // N_CHIPS

    def body(p_ref, sc_ref, tot_ref, gw_ref, pg, s1, r1):
        x, y, _ = _me()
        chip = 2 * x + y
        _gather8(p_ref, pg, s1, r1)
        tot = pg[0]
        for d in range(1, N_DEV):
            tot = tot + pg[d]
        tot_ref[...] = tot
        rows = lax.broadcasted_iota(I32, (N_DEV, ncol), 0)
        dmod = jnp.zeros((N_DEV, ncol), F32)
        for k in range(N_CHIPS):
            part = jnp.zeros((N_DEV, ncol), F32)
            for d in range(N_DEV):
                part = jnp.where(rows == d, pg[d, :, k * ncol:(k + 1) * ncol][0:1, :], part)
            dmod = jnp.where(chip == k, part, dmod)
        gw_ref[...] = lax.dot_general(sc_ref[...].astype(BF16), dmod.astype(BF16), _TN,
                                      preferred_element_type=F32)

    vm = pl.BlockSpec(memory_space=pltpu.VMEM)
    return pl.pallas_call(
        body, name="small_reduce",
        in_specs=[vm, vm], out_specs=[vm, vm],
        out_shape=[jax.ShapeDtypeStruct((8, ADA_COLS), F32), jax.ShapeDtypeStruct((D_MODEL, ncol), F32)],
        scratch_shapes=[pltpu.VMEM((N_DEV, 8, ADA_COLS), F32),
                        pltpu.SemaphoreType.DMA((N_DEV - 1,)), pltpu.SemaphoreType.DMA((N_DEV - 1,))],
        compiler_params=pltpu.CompilerParams(vmem_limit_bytes=VMEM_LIMIT_V7X),
    )(pack, sc_all)


BIG = (("w_in", 1), ("w_ret_out", 0), ("w_att_out", 1), ("w_o", 0), ("w_ff1", 1), ("w_ff2", 0))
SHARD = {"w_in": (D_MODEL, IN_COLS // N_CHIPS), "w_ret_out": (RET_V_W // N_CHIPS, D_MODEL),
         "w_att_out": (ATT_W, D_MODEL // N_CHIPS), "w_o": (D_MODEL // N_CHIPS, D_MODEL),
         "w_ff1": (D_MODEL, D_FF // N_CHIPS), "w_ff2": (D_FF // N_CHIPS, D_MODEL)}
_CHIP_FLIPS = ((1, 0), (0, 1), (1, 1))


def _region(ref, axis, chip, half, shard_shape):
    r, cw = shard_shape
    hr = r // 2
    if axis == 1:
        return ref.at[pl.ds(half * hr, hr), pl.ds(chip * cw, cw)]
    return ref.at[pl.ds(chip * r + half * hr, hr), :]


def _gather_weights(shards, n_remote):
    nw = len(BIG)
    shapes = [s.shape for s in shards]
    full_shapes = [(r, N_CHIPS * cw) if ax == 1 else (N_CHIPS * r, cw)
                   for (r, cw), (_, ax) in zip(shapes, BIG)]

    def body(*refs):
        ins, outs = refs[:nw], refs[nw:2 * nw]
        own = refs[2 * nw:3 * nw]
        from_ici, from_sib = refs[3 * nw:3 * nw + n_remote], refs[3 * nw + n_remote:3 * nw + 2 * n_remote]
        ld_sem, st_sem, s_ici, r_ici, s_d2d, r_d2d, st_a, st_b = refs[3 * nw + 2 * n_remote:]
        x, y, c = _me()
        chip = 2 * x + y
        sib = (x, y, 1 - c)
        loads = [pltpu.make_async_copy(ins[i], own[i], ld_sem.at[i]) for i in range(nw)]
        for cp in loads:
            cp.start()
        pending, first = [], []
        for i, (_, ax) in enumerate(BIG):
            r, cw = shapes[i]
            hr = r // 2
            loads[i].wait()
            dst = outs[i].at[:, pl.ds(chip * cw, cw)] if ax == 1 else outs[i].at[pl.ds(chip * r, r), :]
            cp = pltpu.make_async_copy(own[i], dst, st_sem.at[i])
            cp.start()
            pending.append(cp)
            for j, (fx, fy) in enumerate(_CHIP_FLIPS if i < n_remote else ()):
                rc = pltpu.make_async_remote_copy(
                    src_ref=own[i].at[pl.ds(c * hr, hr), :], dst_ref=from_ici[i].at[j],
                    send_sem=s_ici.at[j * nw + i], recv_sem=r_ici.at[j * nw + i],
                    device_id=(x ^ fx, y ^ fy, c), device_id_type=MESH)
                rc.start()
                first.append((j, i, rc))
        passed = []
        for j, i, rc in first:
            fx, fy = _CHIP_FLIPS[j]
            src_chip = 2 * (x ^ fx) + (y ^ fy)
            ax = BIG[i][1]
            rc.wait_recv()
            fw = pltpu.make_async_remote_copy(
                src_ref=from_ici[i].at[j], dst_ref=from_sib[i].at[j], send_sem=s_d2d.at[j * nw + i],
                recv_sem=r_d2d.at[j * nw + i], device_id=sib, device_id_type=MESH)
            fw.start()
            passed.append((j, i, src_chip, fw))
            st = pltpu.make_async_copy(from_ici[i].at[j], _region(outs[i], ax, src_chip, c, shapes[i]),
                                       st_a.at[j * nw + i])
            st.start()
            pending.append(st)
        for j, i, src_chip, fw in passed:
            fw.wait_recv()
            st = pltpu.make_async_copy(from_sib[i].at[j],
                                       _region(outs[i], BIG[i][1], src_chip, 1 - c, shapes[i]),
                                       st_b.at[j * nw + i])
            st.start()
            pending.append(st)
        for _, _, rc in first:
            rc.wait_send()
        for _, _, _, fw in passed:
            fw.wait_send()
        for cp in pending:
            cp.wait()

    hbm = pl.BlockSpec(memory_space=pl.ANY)
    halves = [pltpu.VMEM((3, r // 2, cw), BF16) for r, cw in shapes[:n_remote]]
    return pl.pallas_call(
        body, name="gather_weights",
        in_specs=[hbm] * nw, out_specs=[hbm] * nw,
        out_shape=[jax.ShapeDtypeStruct(fs, BF16) for fs in full_shapes],
        scratch_shapes=[pltpu.VMEM(sh, BF16) for sh in shapes] + halves + halves
        + [pltpu.SemaphoreType.DMA((nw,)), pltpu.SemaphoreType.DMA((nw,))]
        + [pltpu.SemaphoreType.DMA((3 * nw,))] * 6,
        compiler_params=pltpu.CompilerParams(vmem_limit_bytes=VMEM_LIMIT_V7X),
    )(*shards)


REST = BIG[1:]
_SIDE_EFFECTS = pltpu.CompilerParams(has_side_effects=pltpu.SideEffectType.DATAFLOW_SIDE_EFFECTING)
_ANY_SPEC = pl.BlockSpec(memory_space=pl.ANY)


def _rest_ici_copies(shard_refs, full_refs, sems):
    x, y, c = _me()
    chip = 2 * x + y
    n = 3 * len(REST)
    copies = []
    for i, (name, ax) in enumerate(REST):
        hr = SHARD[name][0] // 2
        for j, (fx, fy) in enumerate(_CHIP_FLIPS):
            copies.append(pltpu.make_async_remote_copy(
                src_ref=shard_refs[i].at[pl.ds(c * hr, hr), :],
                dst_ref=_region(full_refs[i], ax, chip, c, SHARD[name]),
                send_sem=sems[3 * i + j], recv_sem=sems[n + 3 * i + j],
                device_id=(x ^ fx, y ^ fy, c), device_id_type=MESH))
    return copies


def _rest_d2d_copies(full_refs, sems):
    x, y, c = _me()
    n = 3 * len(REST)
    copies = []
    for i, (name, ax) in enumerate(REST):
        for j, (fx, fy) in enumerate(_CHIP_FLIPS):
            reg = _region(full_refs[i], ax, 2 * (x ^ fx) + (y ^ fy), c, SHARD[name])
            copies.append(pltpu.make_async_remote_copy(
                src_ref=reg, dst_ref=reg, send_sem=sems[3 * i + j], recv_sem=sems[n + 3 * i + j],
                device_id=(x, y, 1 - c), device_id_type=MESH))
    return copies


def _gather_rest_start(shards, fulls, after):
    nr, ns, na = len(REST), 6 * len(REST), len(after)

    def body(*refs):
        for cp in _rest_ici_copies(refs[:nr], refs[nr:2 * nr], refs[2 * nr + na:2 * nr + na + ns]):
            cp.start()
        token = refs[-1]
        token[...] = jnp.zeros_like(token)

    hbm = lambda a: pltpu.HBM(a.shape, a.dtype)
    res = pl.pallas_call(
        body, name="gather_rest_start",
        out_shape=(pltpu.SemaphoreType.DMA(()),) * ns + tuple(hbm(a) for a in shards + fulls)
        + (jax.ShapeDtypeStruct((8, 128), F32),),
        in_specs=(_HBM_SPEC,) * (2 * nr) + (_ANY_SPEC,) * na,
        out_specs=(_SEM_SPEC,) * ns + (_HBM_SPEC,) * (2 * nr) + (pl.BlockSpec(memory_space=pltpu.VMEM),),
        input_output_aliases={k: ns + k for k in range(2 * nr)}, compiler_params=_SIDE_EFFECTS,
    )(*[pltpu.with_memory_space_constraint(a, pltpu.HBM) for a in shards + fulls], *after)
    return res[:ns], res[ns:ns + nr], res[ns + nr:ns + 2 * nr], res[-1]


def _gather_rest_forward(sems, shards, fulls, after):
    nr, ns = len(REST), 6 * len(REST)

    def body(*refs):
        shard_refs, full_refs, old = refs[:nr], refs[nr:2 * nr], refs[2 * nr:2 * nr + ns]
        new = refs[2 * nr + ns + len(after):2 * nr + 2 * ns + len(after)]
        for cp in _rest_ici_copies(shard_refs, full_refs, old):
            cp.wait_send()
            cp.wait_recv()
        for cp in _rest_d2d_copies(full_refs, new):
            cp.start()
        token = refs[-1]
        token[...] = jnp.zeros_like(token)

    res = pl.pallas_call(
        body, name="gather_rest_forward",
        out_shape=(pltpu.SemaphoreType.DMA(()),) * ns + tuple(pltpu.HBM(a.shape, a.dtype) for a in fulls)
        + (jax.ShapeDtypeStruct((8, 128), F32),),
        in_specs=(_HBM_SPEC,) * (2 * nr) + (_SEM_SPEC,) * ns + (_ANY_SPEC,) * len(after),
        out_specs=(_SEM_SPEC,) * ns + (_HBM_SPEC,) * nr + (pl.BlockSpec(memory_space=pltpu.VMEM),),
        input_output_aliases={nr + k: ns + k for k in range(nr)}, compiler_params=_SIDE_EFFECTS,
    )(*shards, *fulls, *sems, *after)
    return res[:ns], res[ns:ns + nr], res[-1]


def _gather_rest_end(sems, fulls, after):
    nr, ns = len(REST), 6 * len(REST)

    def body(*refs):
        for cp in _rest_d2d_copies(refs[:nr], refs[nr:nr + ns]):
            cp.wait_send()
            cp.wait_recv()

    return pl.pallas_call(
        body, name="gather_rest_end",
        out_shape=tuple(pltpu.HBM(a.shape, a.dtype) for a in fulls),
        in_specs=(_HBM_SPEC,) * nr + (_SEM_SPEC,) * ns + (_ANY_SPEC,) * len(after),
        out_specs=(_HBM_SPEC,) * nr,
        input_output_aliases={k: k for k in range(nr)}, compiler_params=_SIDE_EFFECTS,
    )(*fulls, *sems, *after)


def _adam_update(w, g, m, v):
    mn = ADAM_B1 * m + (1.0 - ADAM_B1) * g
    vn = ADAM_B2 * v + (1.0 - ADAM_B2) * (g * g)
    m_hat = mn / (1.0 - ADAM_B1 ** ADAM_STEP)
    v_hat = vn / (1.0 - ADAM_B2 ** ADAM_STEP)
    return -ADAM_LR * (m_hat / (jnp.sqrt(v_hat) + ADAM_EPS) + ADAM_WD * w), mn, vn


def _final_sum(name, pos, axis, psum, recv, shard_shape, after=(), tr=128):
    r, cw = shard_shape
    hr = r // 2
    tr = min(tr, hr)
    nt = hr // tr
    n_after = len(after)

    def kern(pos_ref, p_ref, r_ref, *rest):
        g_ref, send_buf, land_buf, s_sem, r_sem = rest[n_after:]
        p, t = pl.program_id(0), pl.program_id(1)
        sib = _sibling()

        def copy(i):
            return pltpu.make_async_remote_copy(
                src_ref=send_buf.at[i], dst_ref=land_buf.at[i], send_sem=s_sem.at[i],
                recv_sem=r_sem.at[i], device_id=sib, device_id_type=MESH)

        @pl.when(p == 0)
        def _():
            tot = p_ref[...].astype(F32)
            for j in range(3):
                tot = tot + r_ref[j].astype(F32)
            send_buf[t] = tot
            copy(t).start()
            g_ref[...] = tot

        @pl.when(p == 1)
        def _():
            copy(t).wait_recv()
            g_ref[...] = land_buf[t]

        @pl.when(jnp.logical_and(p == 1, t == nt - 1))
        def _():
            for i in range(nt):
                copy(i).wait_send()

    def shard_rows(p, t, pos_ref):
        return (jnp.where(p == 0, pos_ref[0], 1 - pos_ref[0]) * nt + t, 0)

    def own_part(p, t, pos_ref):
        tt = jnp.where(p == 0, t, nt - 1)
        return (tt, pos_ref[1]) if axis == 1 else (pos_ref[1] * nt + tt, 0)

    grid_spec = pltpu.PrefetchScalarGridSpec(
        num_scalar_prefetch=1, grid=(2, nt),
        in_specs=[pl.BlockSpec((tr, cw), own_part),
                  pl.BlockSpec((3, tr, cw), lambda p, t, pos_ref: (0, jnp.where(p == 0, t, nt - 1), 0))]
        + [pl.BlockSpec(memory_space=pl.ANY)] * n_after,
        out_specs=pl.BlockSpec((tr, cw), shard_rows),
        scratch_shapes=[pltpu.VMEM((nt, tr, cw), F32), pltpu.VMEM((nt, tr, cw), F32),
                        pltpu.SemaphoreType.DMA((nt,)), pltpu.SemaphoreType.DMA((nt,))])
    return pl.pallas_call(
        kern, name=name, grid_spec=grid_spec, out_shape=jax.ShapeDtypeStruct((r, cw), F32),
        compiler_params=_cparams(("arbitrary", "arbitrary")),
    )(pos, psum, recv, *after)


def _adamw(name, w, g, m, v):
    r, cw = w.shape
    tr = min(r, 128)

    def kern(w_ref, g_ref, m_ref, v_ref, go_ref, d_ref, nm_ref, nv_ref):
        gv = g_ref[...]
        go_ref[...] = gv
        d_ref[...], nm_ref[...], nv_ref[...] = _adam_update(w_ref[...], gv, m_ref[...], v_ref[...])

    spec = pl.BlockSpec((tr, cw), lambda i: (i, 0))
    return pl.pallas_call(
        kern, name=name, grid=(r // tr,), in_specs=[spec] * 4, out_specs=[spec] * 4,
        out_shape=[jax.ShapeDtypeStruct((r, cw), F32)] * 4, compiler_params=_cparams(("parallel",)),
    )(w, g, m, v)


_PACK_W = ADA_COLS
_NB = REL_BUCKETS * N_ATT_HEADS
_SMALL_SLOTS = {
    "b_ada": (0, 0, ADA_COLS),
    "norm1_g": (1, 0, D_MODEL), "norm2_g": (1, D_MODEL, D_MODEL), "norm_f_g": (1, 2 * D_MODEL, D_MODEL),
    "ret_gn_g": (1, 3 * D_MODEL, RET_V_W),
    "ret_gn_b": (2, 0, RET_V_W), "rel_bias": (2, RET_V_W, _NB), "loss": (2, RET_V_W + 512, 128),
}


def _pack_small(vals):
    rows = []
    for r in range(8):
        items = sorted([(off, n) for n, (rr, off, _) in _SMALL_SLOTS.items() if rr == r and n in vals])
        parts, pos = [], 0
        for off, n in items:
            if off > pos:
                parts.append(jnp.zeros((1, off - pos), F32))
            parts.append(vals[n].reshape(1, -1).astype(F32))
            pos = off + _SMALL_SLOTS[n][2]
        if pos < _PACK_W:
            parts.append(jnp.zeros((1, _PACK_W - pos), F32))
        rows.append(jnp.concatenate(parts, axis=-1))
    return jnp.concatenate(rows, axis=0)


def _unpack_small(pack, name):
    r, off, wd = _SMALL_SLOTS[name]
    return pack[r:r + 1, off:off + wd]


def kernel(x, c, w_ada, b_ada, norm1_g, w_in, rel_bias, ret_gn_g, ret_gn_b, w_ret_out, w_att_out, w_o, norm2_g, w_ff1, w_ff2, norm_f_g, loss_target, m_w_ada, m_b_ada, m_norm1_g, m_w_in, m_rel_bias, m_ret_gn_g, m_ret_gn_b, m_w_ret_out, m_w_att_out, m_w_o, m_norm2_g, m_w_ff1, m_w_ff2, m_norm_f_g, v_w_ada, v_b_ada, v_norm1_g, v_w_in, v_rel_bias, v_ret_gn_g, v_ret_gn_b, v_w_ret_out, v_w_att_out, v_w_o, v_norm2_g, v_w_ff1, v_w_ff2, v_norm_f_g):
    given = dict(locals())
    big_names = [n for n, _ in BIG]
    shard_w = {n: given[n][0] for n in big_names}
    assert all(shard_w[n].shape == SHARD[n] for n in big_names)

    shards_bf = [shard_w[n].astype(BF16) for n in big_names]
    full = _gather_weights(shards_bf, 1)
    mod, sc_all = _ada_fwd(c, w_ada[0], b_ada)
    rest_gather = _gather_rest_start(shards_bf[1:], list(full[1:]), [mod])
    pos = _where_am_i()

    loss, grad_x, d_mod, small, g_big, pending = _local_step(
        pos, x[0], loss_target[0], mod, norm1_g, norm2_g, norm_f_g.reshape(1, -1), rel_bias, ret_gn_g,
        ret_gn_b, full[0], rest_gather)

    pack_g = _pack_small(dict(b_ada=d_mod, norm1_g=small["norm1_g"], norm2_g=small["norm2_g"],
                              norm_f_g=small["norm_f_g"], ret_gn_g=small["gn_g"], ret_gn_b=small["gn_b"],
                              rel_bias=small["rel_bias"], loss=loss))
    tot, g_w_ada = _small_reduce(pack_g, sc_all)

    small_names = ["b_ada", "norm1_g", "rel_bias", "ret_gn_g", "ret_gn_b", "norm2_g", "norm_f_g"]
    pack_w = _pack_small({n: given[n] for n in small_names})
    pack_m = _pack_small({n: given["m_" + n] for n in small_names})
    pack_v = _pack_small({n: given["v_" + n] for n in small_names})
    _, sd, sm, sv = _adamw("adamw_small", pack_w, tot, pack_m, pack_v)

    grads, deltas, new_m, new_v = {}, {}, {}, {}
    for n in small_names:
        shp = given[n].shape
        grads[n] = _unpack_small(tot, n).reshape(shp)
        deltas[n] = _unpack_small(sd, n).reshape(shp)
        new_m[n] = _unpack_small(sm, n).reshape(shp)
        new_v[n] = _unpack_small(sv, n).reshape(shp)
    g_big["w_ada"] = g_w_ada
    for n in ["w_ada"] + big_names[1:] + big_names[:1]:
        if n == "w_in":
            gw_in, sems, land = pending
            done = [tot, sd] + [deltas[k] for k in ["w_ada"] + big_names[1:]]
            (gw_in,), (got,) = _ici_wait("ici_wait_w_in", [n], sems, [gw_in], [land], done)
            g_big[n] = _final_sum("final_w_in", pos, 1, gw_in, got, SHARD[n])
        g, d, nm, nv = _adamw("adamw_" + n, given[n][0], g_big[n], given["m_" + n][0], given["v_" + n][0])
        grads[n], deltas[n], new_m[n], new_v[n] = g[None], d[None], nm[None], nv[None]

    order = ["w_ada", "b_ada", "norm1_g", "w_in", "rel_bias", "ret_gn_g", "ret_gn_b", "w_ret_out",
             "w_att_out", "w_o", "norm2_g", "w_ff1", "w_ff2", "norm_f_g"]
    loss_out = _unpack_small(tot, "loss")[0, 0]
    return (loss_out, grad_x[None], *[grads[n] for n in order], *[deltas[n] for n in order],
            *[new_m[n] for n in order], *[new_v[n] for n in order])
```

```python
import functools
import math

import jax
import jax.numpy as jnp
import numpy as np
from jax import lax
from jax.experimental import pallas as pl
from jax.experimental.pallas import tpu as pltpu

F32 = jnp.float32
BF16 = jnp.bfloat16
I32 = jnp.int32

SEQ = 2048
D_MODEL = 1024
RET_HEADS = 4
RET_DK = 256
RET_DV = 512
RET_CHUNK = 128
RET_SUB = 2
RET_QK_W = RET_HEADS * RET_DK
RET_V_W = RET_HEADS * RET_DV
ATT_GROUPS = ((128, 1), (512, 4), (2048, 16))
ATT_HPG = 4
ATT_DH = 128
ATT_W = ATT_HPG * ATT_DH
ATT_BLK = 128
N_BLK = SEQ // ATT_BLK
REL_BUCKETS = 32
REL_MAX_DIST = 2048
N_ATT_HEADS = 12
D_FF = 4 * D_MODEL
RMS_EPS = 1e-6
GN_EPS = 1e-5
ROPE_BASE = 10000.0
IN_COLS = 2 * RET_QK_W + 2 * RET_V_W + 9 * ATT_W + 2 * D_MODEL
OFF_Q, OFF_K, OFF_V, OFF_G = 0, RET_QK_W, 2 * RET_QK_W, 2 * RET_QK_W + RET_V_W
OFF_ATT = 2 * RET_QK_W + 2 * RET_V_W
OFF_GATE = OFF_ATT + 9 * ATT_W
N_CHIPS = 4
N_DEV = 8
ADA_COLS = 6 * D_MODEL

ADAM_LR = 0.001
ADAM_B1 = 0.9
ADAM_B2 = 0.999
ADAM_EPS = 1e-08
ADAM_WD = 0.01
ADAM_STEP = 10

VMEM_LIMIT_V7X = 56 * 1024 * 1024
MESH = pl.DeviceIdType.MESH


def _cparams(sem):
    return pltpu.CompilerParams(dimension_semantics=sem, vmem_limit_bytes=VMEM_LIMIT_V7X)


def _sigmoid(v):
    return 1.0 / (1.0 + jnp.exp(-v))


def _rowmap(name, body, row_ins, bcast_ins, row_outs, sum_outs=(), tm=256, after=()):
    m = row_ins[0].shape[0]
    n_in = len(row_ins) + len(bcast_ins)
    n_ro = len(row_outs)

    def kern(*refs):
        vals = [r[...] for r in refs[:n_in]]
        res = body(*vals)
        if not isinstance(res, (tuple, list)):
            res = (res,)
        outs = refs[n_in + len(after):]
        for r, v in zip(outs[:n_ro], res[:n_ro]):
            r[...] = v.astype(r.dtype)
        if sum_outs:
            @pl.when(pl.program_id(0) == 0)
            def _():
                for r in outs[n_ro:]:
                    r[...] = jnp.zeros_like(r)
            for r, v in zip(outs[n_ro:], res[n_ro:]):
                r[...] += v

    in_specs = [pl.BlockSpec((tm, a.shape[1]), lambda i: (i, 0)) for a in row_ins]
    in_specs += [pl.BlockSpec(a.shape, lambda i: (0, 0)) for a in bcast_ins]
    in_specs += [pl.BlockSpec(memory_space=pl.ANY)] * len(after)
    out_specs = [pl.BlockSpec((tm, n), lambda i: (i, 0)) for n, _ in row_outs]
    out_specs += [pl.BlockSpec((1, n), lambda i: (0, 0)) for n in sum_outs]
    out_shape = [jax.ShapeDtypeStruct((m, n), dt) for n, dt in row_outs]
    out_shape += [jax.ShapeDtypeStruct((1, n), F32) for n in sum_outs]
    return pl.pallas_call(
        kern, name=name, grid=(m // tm,), in_specs=in_specs, out_specs=out_specs,
        out_shape=out_shape, compiler_params=_cparams(("arbitrary",)),
    )(*row_ins, *bcast_ins, *after)


TM, TN = 1024, 1024


def _piece_chunks(piece, width):
    arr, stacked = piece
    return arr.shape[0] if stacked else arr.shape[1] // width


def _piece_spec(piece, rows, width, start, row_of, chunk_of):
    arr, stacked = piece
    last = _piece_chunks(piece, width) - 1

    def local(*ids):
        return jnp.clip(chunk_of(*ids) - start, 0, last)

    def row(*ids):
        rel = chunk_of(*ids) - start
        return jnp.where(jnp.logical_and(rel >= 0, rel <= last), row_of(*ids), 0)

    if stacked:
        return pl.BlockSpec((None, rows, width), lambda *ids: (local(*ids), row(*ids), 0))
    return pl.BlockSpec((rows, width), lambda *ids: (row(*ids), local(*ids)))


def _piece_starts(pieces, width):
    return [sum(_piece_chunks(p, width) for p in pieces[:q]) for q in range(len(pieces))]


def _matmul(name, a, b, kind, m, n, k, outs, *, b_off=0, tm=TM, tn=TN, tk=1024,
            epilogue=None, extras=(), after=()):
    tm, tn, tk = min(tm, m), min(tn, n), min(tk, k)
    nk = k // tk
    pieces = a if isinstance(a, list) else [(a, False)]
    starts = _piece_starts(pieces, tk)
    if kind == "nn":
        a_specs = [pl.BlockSpec((tm, tk), lambda i, j, kk: (i, kk))]
        b_spec = pl.BlockSpec((tk, tn), lambda i, j, kk: (kk, b_off // tn + j))
        dn = (((1,), (0,)), ((), ()))
    elif kind == "nt":
        a_specs = [_piece_spec(p, tm, tk, st, lambda i, j, kk: i, lambda i, j, kk: kk)
                   for p, st in zip(pieces, starts)]
        b_spec = pl.BlockSpec((tn, tk), lambda i, j, kk: (j, b_off // tk + kk))
        dn = (((1,), (1,)), ((), ()))
    else:
        a_specs = [pl.BlockSpec((tk, tm), lambda i, j, kk: (kk, i))]
        b_spec = pl.BlockSpec((tk, tn), lambda i, j, kk: (kk, j))
        dn = (((0,), (0,)), ((), ()))
    n_a, n_ex, n_out = len(pieces), len(extras), len(outs)
    if epilogue is None:
        epilogue = lambda acc: (acc,)

    def finish(acc, ex_refs, out_refs):
        res = epilogue(acc, *[r[...] for r in ex_refs])
        for r, v in zip(out_refs, res):
            r[...] = v.astype(r.dtype)

    n_in = n_a + 1 + n_ex + len(after)

    def kern(*refs):
        a_refs, b_ref = refs[:n_a], refs[n_a]
        ex_refs = refs[n_a + 1:n_a + 1 + n_ex]
        out_refs = refs[n_in:n_in + n_out]
        kk = pl.program_id(2)
        dot = lambda a_ref: lax.dot_general(a_ref[...], b_ref[...], dn, preferred_element_type=F32)
        if nk == 1:
            finish(dot(a_refs[0]), ex_refs, out_refs)
            return
        acc_ref = refs[n_in + n_out]
        if n_a == 1:
            part = dot(a_refs[0])

            @pl.when(kk == 0)
            def _():
                acc_ref[...] = part

            @pl.when(kk > 0)
            def _():
                acc_ref[...] += part
        else:
            @pl.when(kk == 0)
            def _():
                acc_ref[...] = jnp.zeros_like(acc_ref)

            for q in range(n_a):
                @pl.when(jnp.logical_and(kk >= starts[q], kk < starts[q] + _piece_chunks(pieces[q], tk)))
                def _(q=q):
                    acc_ref[...] += dot(a_refs[q])

        @pl.when(kk == nk - 1)
        def _():
            finish(acc_ref[...], ex_refs, out_refs)

    in_specs = a_specs + [b_spec] + [pl.BlockSpec(bs, im) for _, bs, im in extras]
    in_specs += [pl.BlockSpec(memory_space=pl.ANY)] * len(after)
    return pl.pallas_call(
        kern, name=name, grid=(m // tm, n // tn, nk), in_specs=in_specs,
        out_specs=[pl.BlockSpec((tm, tn), lambda i, j, kk: (i, j)) for _ in outs],
        out_shape=[jax.ShapeDtypeStruct((m, n), dt) for dt in outs],
        scratch_shapes=[] if nk == 1 else [pltpu.VMEM((tm, tn), F32)],
        compiler_params=_cparams(("parallel", "parallel", "arbitrary")),
    )(*[p[0] for p in pieces], b, *[e[0] for e in extras], *after)


def _ici_copies(psum_ref, recv_ref, s_sem, r_sem, axis, shard_shape):
    x, y, c = _me()
    hr, cw = shard_shape[0] // 2, shard_shape[1]
    pick = lambda sems, j: sems[j] if isinstance(sems, (list, tuple)) else sems.at[j]
    copies = []
    for j, (fx, fy) in enumerate(_CHIP_FLIPS):
        chip = 2 * (x ^ fx) + (y ^ fy)
        src = psum_ref.at[:, pl.ds(chip * cw, cw)] if axis == 1 else psum_ref.at[pl.ds(chip * hr, hr), :]
        copies.append(pltpu.make_async_remote_copy(
            src_ref=src, dst_ref=recv_ref.at[j], send_sem=pick(s_sem, j), recv_sem=pick(r_sem, j),
            device_id=(x ^ fx, y ^ fy, c), device_id_type=MESH))
    return copies


_HBM_SPEC = pl.BlockSpec(memory_space=pltpu.HBM)
_SEM_SPEC = pl.BlockSpec(memory_space=pltpu.SEMAPHORE)


def _split_ici_copies(names, p_refs, land_refs, sems):
    copies = []
    for i, n in enumerate(names):
        copies += _ici_copies(p_refs[i], land_refs[i], list(sems[6 * i:6 * i + 3]),
                              list(sems[6 * i + 3:6 * i + 6]), dict(BIG)[n], SHARD[n])
    return copies


def _ici_start(name, names, psums):
    nw, ns = len(names), 6 * len(names)
    lands = [lax.empty((3, SHARD[n][0] // 2, SHARD[n][1]), BF16) for n in names]

    def body(*refs):
        for cp in _split_ici_copies(names, refs[:nw], refs[nw:2 * nw], refs[2 * nw:2 * nw + ns]):
            cp.start()
        token = refs[-1]
        token[...] = jnp.zeros_like(token)

    res = pl.pallas_call(
        body, name=name,
        out_shape=(pltpu.SemaphoreType.DMA(()),) * ns
        + tuple(pltpu.HBM(a.shape, BF16) for a in list(psums) + lands)
        + (jax.ShapeDtypeStruct((8, 128), F32),),
        in_specs=(_HBM_SPEC,) * (2 * nw),
        out_specs=(_SEM_SPEC,) * ns + (_HBM_SPEC,) * (2 * nw) + (pl.BlockSpec(memory_space=pltpu.VMEM),),
        input_output_aliases={k: ns + k for k in range(2 * nw)},
        compiler_params=pltpu.CompilerParams(has_side_effects=pltpu.SideEffectType.DATAFLOW_SIDE_EFFECTING),
    )(*[pltpu.with_memory_space_constraint(a, pltpu.HBM) for a in list(psums) + lands])
    return res[:ns], res[ns:ns + nw], res[ns + nw:ns + 2 * nw], res[-1]


def _ici_wait(name, names, sems, p_thru, land_thru, after):
    nw, ns = len(names), 6 * len(names)

    def body(*refs):
        for cp in _split_ici_copies(names, refs[:nw], refs[nw:2 * nw], refs[2 * nw:2 * nw + ns]):
            cp.wait_send()
            cp.wait_recv()

    res = pl.pallas_call(
        body, name=name,
        out_shape=tuple(pltpu.HBM(a.shape, BF16) for a in list(p_thru) + list(land_thru)),
        in_specs=(_HBM_SPEC,) * (2 * nw) + (_SEM_SPEC,) * ns + (pl.BlockSpec(memory_space=pl.ANY),) * len(after),
        out_specs=(_HBM_SPEC,) * (2 * nw), input_output_aliases={k: k for k in range(2 * nw)},
        compiler_params=pltpu.CompilerParams(has_side_effects=pltpu.SideEffectType.DATAFLOW_SIDE_EFFECTING),
    )(*p_thru, *land_thru, *sems, *after)
    return res[:nw], res[nw:]


def _where_am_i():
    x, y, c = _me()
    return jnp.stack([c, 2 * x + y]).astype(I32)


def _sibling():
    x, y, c = _me()
    return (x, y, 1 - c)


N_SEND_SLOTS = 2


def _matmul_tn_pair(name, pos, a, b, m, n, k, shard_rows, *, tm, tn, tk):
    hr = shard_rows // 2
    tm, tn, tk = min(tm, hr), min(tn, n), min(tk, k)
    tph = hr // tm
    nt, nj, nk = (m // 2) // tm, n // tn, k // tk
    n_tiles = nt * nj

    def row_block(p, t, pos_ref):
        half = jnp.where(p == 0, 1 - pos_ref[0], pos_ref[0])
        return (t // tph) * (2 * tph) + half * tph + t % tph

    pieces = b if isinstance(b, list) else [(b, False)]
    starts = _piece_starts(pieces, tn)
    n_b = len(pieces)

    def kern(pos_ref, a_ref, *rest):
        b_refs = rest[:n_b]
        o_ref, acc_ref, send_buf, land_buf, s_sem, r_sem = rest[n_b:]
        p, t, j, kk = pl.program_id(0), pl.program_id(1), pl.program_id(2), pl.program_id(3)
        idx = t * nj + j
        sib = _sibling()

        def copy(i):
            return pltpu.make_async_remote_copy(
                src_ref=send_buf.at[i % N_SEND_SLOTS], dst_ref=land_buf.at[i], send_sem=s_sem.at[i],
                recv_sem=r_sem.at[i], device_id=sib, device_id_type=MESH)

        @pl.when(kk == 0)
        def _():
            acc_ref[...] = jnp.zeros_like(acc_ref)

        for q in range(n_b):
            @pl.when(jnp.logical_and(j >= starts[q], j < starts[q] + _piece_chunks(pieces[q], tn)))
            def _(q=q):
                acc_ref[...] += lax.dot_general(a_ref[...], b_refs[q][...], _TN, preferred_element_type=F32)

        @pl.when(jnp.logical_and(kk == nk - 1, p == 0))
        def _():
            @pl.when(idx >= N_SEND_SLOTS)
            def _():
                copy(idx - N_SEND_SLOTS).wait_send()

            send_buf[idx % N_SEND_SLOTS] = acc_ref[...].astype(BF16)
            copy(idx).start()

        @pl.when(jnp.logical_and(kk == nk - 1, p == 1))
        def _():
            copy(idx).wait_recv()
            o_ref[...] = (acc_ref[...] + land_buf[idx].astype(F32)).astype(BF16)

        @pl.when(jnp.logical_and(jnp.logical_and(p == 1, idx == n_tiles - 1), kk == nk - 1))
        def _():
            for i in range(max(n_tiles - N_SEND_SLOTS, 0), n_tiles):
                copy(i).wait_send()

    grid_spec = pltpu.PrefetchScalarGridSpec(
        num_scalar_prefetch=1, grid=(2, nt, nj, nk),
        in_specs=[pl.BlockSpec((tk, tm), lambda p, t, j, kk, pos_ref: (kk, row_block(p, t, pos_ref)))]
        + [_piece_spec(pc, tk, tn, st, lambda p, t, j, kk, pos_ref: kk, lambda p, t, j, kk, pos_ref: j)
           for pc, st in zip(pieces, starts)],
        out_specs=pl.BlockSpec((tm, tn), lambda p, t, j, kk, pos_ref: (p * t, p * j)),
        scratch_shapes=[pltpu.VMEM((tm, tn), F32), pltpu.VMEM((N_SEND_SLOTS, tm, tn), BF16),
                        pltpu.VMEM((n_tiles, tm, tn), BF16),
                        pltpu.SemaphoreType.DMA((n_tiles,)), pltpu.SemaphoreType.DMA((n_tiles,))])
    return pl.pallas_call(
        kern, name=name, grid_spec=grid_spec, out_shape=jax.ShapeDtypeStruct((m // 2, n), BF16),
        compiler_params=_cparams(("arbitrary",) * 4),
    )(pos, a, *[pc[0] for pc in pieces])


def _rope_tables():
    half = RET_DK // 2
    f32 = np.float32
    inv = np.power(f32(ROPE_BASE), -np.arange(half, dtype=f32) / f32(half)).astype(f32)
    ang = (np.arange(SEQ, dtype=f32)[:, None] * inv[None, :]).astype(f32)
    return jnp.asarray(np.cos(ang).astype(f32)), jnp.asarray(np.sin(ang).astype(f32))


def _decay_tables():
    c = RET_CHUNK
    f32 = np.float32
    log_g = np.log1p(-np.power(f32(2.0), f32(-5.0) - np.arange(RET_HEADS, dtype=f32))).astype(f32)
    idx = np.arange(c, dtype=f32)
    rel = idx[:, None] - idx[None, :]
    din = np.where(rel >= 0, np.exp(log_g[:, None, None] * np.maximum(rel, f32(0.0))), f32(0.0)).astype(f32)
    qd = np.exp(log_g[:, None] * (idx + f32(1.0))).astype(f32)[:, :, None]
    kd = np.exp(log_g[:, None] * (f32(c) - f32(1.0) - idx)).astype(f32)[:, :, None]
    cd = np.exp(log_g * f32(c)).astype(f32)
    return jnp.asarray(din), jnp.asarray(qd), jnp.asarray(kd), jnp.asarray(cd)


def _t5_bucket(dist):
    max_exact = REL_BUCKETS // 2
    d_f = jnp.maximum(dist, 1).astype(F32)
    large = max_exact + (jnp.log(d_f / max_exact) / math.log(REL_MAX_DIST / max_exact)
                         * (REL_BUCKETS - max_exact)).astype(I32)
    large = jnp.minimum(large, REL_BUCKETS - 1)
    return jnp.where(dist < max_exact, dist, large)


def _bucket_tables():
    qi = jnp.arange(ATT_BLK)[:, None]
    kj = jnp.arange(2 * ATT_BLK)[None, :]
    dist = jnp.clip(ATT_BLK + qi - kj, 0, ATT_BLK)
    return jnp.stack([_t5_bucket(dist * dil) for _, dil in ATT_GROUPS]).astype(I32)


def _permute_rows(t, dil):
    if dil == 1:
        return t
    s, w = t.shape
    return t.reshape(s // dil, dil, w).transpose(1, 0, 2).reshape(s, w)


def _unpermute_rows(t, dil):
    if dil == 1:
        return t
    s, w = t.shape
    return t.reshape(dil, s // dil, w).transpose(1, 0, 2).reshape(s, w)


def _retention_fwd(rqk, rv, din, qd, kd, cd):
    nc = SEQ // RET_CHUNK
    c, dk, dv = RET_CHUNK, RET_DK, RET_DV

    def kern(q_ref, k_ref, v_ref, din_ref, qd_ref, kd_ref, cd_ref, o_ref, st_ref, state):
        n = pl.program_id(0)

        @pl.when(n == 0)
        def _():
            state[...] = jnp.zeros_like(state)

        for sub in range(RET_SUB):
            rows = slice(sub * c, (sub + 1) * c)
            for h in range(RET_HEADS):
                q, k = q_ref[rows, h * dk:(h + 1) * dk], k_ref[rows, h * dk:(h + 1) * dk]
                v = v_ref[rows, h * dv:(h + 1) * dv]
                s_b = state[h].astype(BF16)
                st_ref[h, sub] = s_b
                a = lax.dot_general(q, k, _NT, preferred_element_type=F32) * din_ref[h]
                o = jnp.dot(a.astype(BF16), v, preferred_element_type=F32)
                o += jnp.dot(q, s_b, preferred_element_type=F32) * qd_ref[h]
                o_ref[rows, h * dv:(h + 1) * dv] = o
                kk = (k.astype(F32) * kd_ref[h]).astype(BF16)
                state[h] = state[h] * cd_ref[h] + lax.dot_general(kk, v, _TN, preferred_element_type=F32)

    whole = lambda a: pl.BlockSpec(a.shape, lambda n: (0,) * a.ndim)
    cs = RET_SUB * c
    return pl.pallas_call(
        kern, name="retention_fwd", grid=(nc // RET_SUB,),
        in_specs=[
            pl.BlockSpec((cs, RET_QK_W), lambda n: (n, 0)),
            pl.BlockSpec((cs, RET_QK_W), lambda n: (n, 1)),
            pl.BlockSpec((cs, RET_V_W), lambda n: (n, 0)),
            whole(din), whole(qd), whole(kd),
            pl.BlockSpec(memory_space=pltpu.SMEM),
        ],
        out_specs=[
            pl.BlockSpec((cs, RET_V_W), lambda n: (n, 0)),
            pl.BlockSpec((RET_HEADS, RET_SUB, dk, dv), lambda n: (0, n, 0, 0)),
        ],
        out_shape=[
            jax.ShapeDtypeStruct((SEQ, RET_V_W), F32),
            jax.ShapeDtypeStruct((RET_HEADS, nc, dk, dv), BF16),
        ],
        scratch_shapes=[pltpu.VMEM((RET_HEADS, dk, dv), F32)],
        compiler_params=_cparams(("arbitrary",)),
    )(rqk, rqk, rv, din, qd, kd, cd)


def _retention_bwd(rqk, rv, states, d_ro, din, qd, kd, cd, cos, sin):
    nc = SEQ // RET_CHUNK
    c, dk, dv = RET_CHUNK, RET_DK, RET_DV
    half = dk // 2
    last = nc // RET_SUB - 1

    def unrot(g, cs, sn):
        g1, g2 = g[:, :half], g[:, half:]
        return jnp.concatenate([g1 * cs + g2 * sn, g2 * cs - g1 * sn], axis=-1)

    def kern(q_ref, k_ref, v_ref, st_ref, do_ref, din_ref, qd_ref, kd_ref, cd_ref, cos_ref, sin_ref,
             out_ref, dstate):
        step = pl.program_id(0)

        @pl.when(step == 0)
        def _():
            dstate[...] = jnp.zeros_like(dstate)

        for sub in reversed(range(RET_SUB)):
            rows = slice(sub * c, (sub + 1) * c)
            cs, sn = cos_ref[rows, :], sin_ref[rows, :]
            for h in range(RET_HEADS):
                qk_cols, v_cols = slice(h * dk, (h + 1) * dk), slice(h * dv, (h + 1) * dv)
                q, k, v = q_ref[rows, qk_cols], k_ref[rows, qk_cols], v_ref[rows, v_cols]
                s_b = st_ref[h, sub]
                d_ob = do_ref[rows, v_cols]
                d_oq = (d_ob.astype(F32) * qd_ref[h]).astype(BF16)
                ds_b = dstate[h].astype(BF16)
                din_m = din_ref[h]
                a_b = (lax.dot_general(q, k, _NT, preferred_element_type=F32) * din_m).astype(BF16)
                kk = (k.astype(F32) * kd_ref[h]).astype(BF16)
                d_v = lax.dot_general(a_b, d_ob, _TN, preferred_element_type=F32)
                d_v += jnp.dot(kk, ds_b, preferred_element_type=F32)
                d_a = (lax.dot_general(d_ob, v, _NT, preferred_element_type=F32) * din_m).astype(BF16)
                d_q = jnp.dot(d_a, k, preferred_element_type=F32)
                d_q += lax.dot_general(d_oq, s_b, _NT, preferred_element_type=F32)
                d_k = lax.dot_general(d_a, q, _TN, preferred_element_type=F32)
                d_k += lax.dot_general(v, ds_b, _NT, preferred_element_type=F32) * kd_ref[h]
                dstate[h] = dstate[h] * cd_ref[h] + lax.dot_general(q, d_oq, _TN,
                                                                    preferred_element_type=F32)
                out_ref[rows, h * dk:(h + 1) * dk] = unrot(d_q, cs, sn).astype(BF16)
                out_ref[rows, RET_QK_W + h * dk:RET_QK_W + (h + 1) * dk] = (
                    unrot(d_k, cs, sn) * (RET_DK ** -0.5)).astype(BF16)
                out_ref[rows, 2 * RET_QK_W + h * dv:2 * RET_QK_W + (h + 1) * dv] = d_v.astype(BF16)

    whole = lambda a: pl.BlockSpec(a.shape, lambda n: (0,) * a.ndim)
    rs = RET_SUB * c
    return pl.pallas_call(
        kern, name="retention_bwd", grid=(nc // RET_SUB,),
        in_specs=[
            pl.BlockSpec((rs, RET_QK_W), lambda n: (last - n, 0)),
            pl.BlockSpec((rs, RET_QK_W), lambda n: (last - n, 1)),
            pl.BlockSpec((rs, RET_V_W), lambda n: (last - n, 0)),
            pl.BlockSpec((RET_HEADS, RET_SUB, dk, dv), lambda n: (0, last - n, 0, 0)),
            pl.BlockSpec((rs, RET_V_W), lambda n: (last - n, 0)),
            whole(din), whole(qd), whole(kd),
            pl.BlockSpec(memory_space=pltpu.SMEM),
            pl.BlockSpec((rs, half), lambda n: (last - n, 0)),
            pl.BlockSpec((rs, half), lambda n: (last - n, 0)),
        ],
        out_specs=pl.BlockSpec((rs, 2 * RET_QK_W + RET_V_W), lambda n: (last - n, 0)),
        out_shape=jax.ShapeDtypeStruct((SEQ, 2 * RET_QK_W + RET_V_W), BF16),
        scratch_shapes=[pltpu.VMEM((RET_HEADS, dk, dv), F32)],
        compiler_params=_cparams(("arbitrary",)),
    )(rqk, rqk, rv, states, d_ro, din, qd, kd, cd, cos, sin)


def _bias_build(rel_bias, buckets):
    ng = len(ATT_GROUPS)

    def kern(tab_ref, bkt_ref, o_ref):
        g, h = pl.program_id(0), pl.program_id(1)
        bkt = bkt_ref[...]
        acc = jnp.zeros(bkt.shape, F32)
        for b in range(REL_BUCKETS):
            acc = jnp.where(bkt == b, tab_ref[b, g * ATT_HPG + h], acc)
        o_ref[...] = acc

    return pl.pallas_call(
        kern, name="bias_build", grid=(ng, ATT_HPG),
        in_specs=[pl.BlockSpec(memory_space=pltpu.SMEM),
                  pl.BlockSpec((None, ATT_BLK, 2 * ATT_BLK), lambda g, h: (g, 0, 0))],
        out_specs=pl.BlockSpec((None, None, ATT_BLK, 2 * ATT_BLK), lambda g, h: (g, h, 0, 0)),
        out_shape=jax.ShapeDtypeStruct((ng, ATT_HPG, ATT_BLK, 2 * ATT_BLK), F32),
        compiler_params=_cparams(("arbitrary", "arbitrary")),
    )(rel_bias, buckets)


def _bias_grad(dsb, buckets):
    ng = len(ATT_GROUPS)

    def kern(ds_ref, bkt_ref, o_ref):
        g, h = pl.program_id(0), pl.program_id(1)
        bkt, ds = bkt_ref[...], ds_ref[...]
        for b in range(REL_BUCKETS):
            o_ref[b, g * ATT_HPG + h] = jnp.sum(jnp.where(bkt == b, ds, 0.0))

    return pl.pallas_call(
        kern, name="bias_grad", grid=(ng, ATT_HPG),
        in_specs=[pl.BlockSpec((None, None, ATT_BLK, 2 * ATT_BLK), lambda g, h: (g, h, 0, 0)),
                  pl.BlockSpec((None, ATT_BLK, 2 * ATT_BLK), lambda g, h: (g, 0, 0))],
        out_specs=pl.BlockSpec(memory_space=pltpu.SMEM),
        out_shape=jax.ShapeDtypeStruct((REL_BUCKETS, N_ATT_HEADS), F32),
        compiler_params=_cparams(("arbitrary", "arbitrary")),
    )(dsb, buckets)


_NT = (((1,), (1,)), ((), ()))
_TN = (((0,), (0,)), ((), ()))
_ATT_SCALE = ATT_DH ** -0.5


_PAD_ROWS = SEQ + ATT_BLK
ATT_UNROLL = 16


def _window_mask(has_prev):
    qi = lax.broadcasted_iota(I32, (ATT_BLK, 2 * ATT_BLK), 0)
    kj = lax.broadcasted_iota(I32, (ATT_BLK, 2 * ATT_BLK), 1)
    prev_ok = jnp.logical_and(jnp.logical_and(kj < ATT_BLK, kj >= qi), has_prev)
    return jnp.logical_or(prev_ok, jnp.logical_and(kj >= ATT_BLK, qi >= kj - ATT_BLK))


def _head_specs(col0):
    return pl.BlockSpec((SEQ, ATT_DH), lambda h: (0, col0 + h))


def _att_fwd(gi, qkv, bias, nb):
    blk, dh = ATT_BLK, ATT_DH

    def kern(q_ref, k_ref, v_ref, b_ref, o_ref, l_ref, kpad, vpad):
        zero = jnp.zeros((blk, dh), BF16)
        kpad[0:blk, :] = zero
        vpad[0:blk, :] = zero
        kpad[blk:, :] = k_ref[...]
        vpad[blk:, :] = v_ref[...]
        bias_m = b_ref[...]

        def body(b, carry):
            r0 = pl.multiple_of(b * blk, blk)
            q = q_ref[pl.ds(r0, blk), :]
            kw = kpad[pl.ds(r0, 2 * blk), :]
            vw = vpad[pl.ds(r0, 2 * blk), :]
            valid = _window_mask((b % nb) > 0)
            s = lax.dot_general(q, kw, _NT, preferred_element_type=F32) * _ATT_SCALE + bias_m
            s = jnp.where(valid, s, -1e30)
            mx = jnp.max(s, axis=-1, keepdims=True)
            e = jnp.exp(s - mx)
            den = jnp.sum(e, axis=-1, keepdims=True)
            o_ref[pl.ds(r0, blk), :] = jnp.dot((e / den).astype(BF16), vw, preferred_element_type=F32)
            l_ref[pl.ds(r0, blk), :] = jnp.broadcast_to(mx + jnp.log(den), (blk, dh))
            return carry

        lax.fori_loop(0, N_BLK, body, 0, unroll=ATT_UNROLL)

    return pl.pallas_call(
        kern, name=f"att_fwd_g{gi}", grid=(ATT_HPG,),
        in_specs=[_head_specs(0), _head_specs(ATT_HPG), _head_specs(2 * ATT_HPG),
                  pl.BlockSpec((None, None, blk, 2 * blk), lambda h: (gi, h, 0, 0))],
        out_specs=[_head_specs(0), _head_specs(0)],
        out_shape=[jax.ShapeDtypeStruct((SEQ, ATT_W), F32), jax.ShapeDtypeStruct((SEQ, ATT_W), F32)],
        scratch_shapes=[pltpu.VMEM((_PAD_ROWS, dh), BF16), pltpu.VMEM((_PAD_ROWS, dh), BF16)],
        compiler_params=_cparams(("arbitrary",)),
    )(qkv, qkv, qkv, bias)


def _att_bwd(gi, qkv, d_att, lse, dd, bias, nb):
    blk, dh = ATT_BLK, ATT_DH

    def kern(q_ref, k_ref, v_ref, do_ref, l_ref, d_ref, b_ref, dqkv_ref, dsb_ref,
             kpad, vpad, qpad, dopad, lpad, dpad):
        zero = jnp.zeros((blk, dh), BF16)
        zero_f = jnp.zeros((blk, dh), F32)
        kpad[0:blk, :] = zero
        vpad[0:blk, :] = zero
        kpad[blk:, :] = k_ref[...]
        vpad[blk:, :] = v_ref[...]
        qpad[SEQ:, :] = zero
        dopad[SEQ:, :] = zero
        lpad[SEQ:, :] = zero_f
        dpad[SEQ:, :] = zero_f
        qpad[0:SEQ, :] = q_ref[...]
        dopad[0:SEQ, :] = do_ref[...]
        lpad[0:SEQ, :] = l_ref[...]
        dpad[0:SEQ, :] = d_ref[...]
        bias_m = b_ref[...]
        bias_t = jnp.concatenate([bias_m[:, blk:], bias_m[:, :blk]], axis=0)
        dsb_ref[...] = jnp.zeros_like(dsb_ref)

        def dq_body(b, carry):
            r0 = pl.multiple_of(b * blk, blk)
            q, d_o = q_ref[pl.ds(r0, blk), :], do_ref[pl.ds(r0, blk), :]
            kw, vw = kpad[pl.ds(r0, 2 * blk), :], vpad[pl.ds(r0, 2 * blk), :]
            lrow, drow = l_ref[pl.ds(r0, blk), :][:, :1], d_ref[pl.ds(r0, blk), :][:, :1]
            valid = _window_mask((b % nb) > 0)
            s = lax.dot_general(q, kw, _NT, preferred_element_type=F32) * _ATT_SCALE + bias_m
            p = jnp.where(valid, jnp.exp(jnp.where(valid, s, -1e30) - lrow), 0.0)
            dp = lax.dot_general(d_o, vw, _NT, preferred_element_type=F32)
            ds = p * (dp - drow)
            dq = jnp.dot(ds.astype(BF16), kw, preferred_element_type=F32)
            dqkv_ref[0, pl.ds(r0, blk), :] = (dq * _ATT_SCALE).astype(BF16)
            dsb_ref[...] += ds
            return carry

        lax.fori_loop(0, N_BLK, dq_body, 0, unroll=ATT_UNROLL)

        qi = lax.broadcasted_iota(I32, (2 * blk, blk), 0)
        kj = lax.broadcasted_iota(I32, (2 * blk, blk), 1)

        def dkv_body(b, carry):
            r0 = pl.multiple_of(b * blk, blk)
            k, v = k_ref[pl.ds(r0, blk), :], v_ref[pl.ds(r0, blk), :]
            qw, dow = qpad[pl.ds(r0, 2 * blk), :], dopad[pl.ds(r0, 2 * blk), :]
            lrow, drow = lpad[pl.ds(r0, 2 * blk), :][:, :1], dpad[pl.ds(r0, 2 * blk), :][:, :1]
            has_next = jnp.logical_and(b + 1 < N_BLK, ((b + 1) % nb) > 0)
            next_ok = jnp.logical_and(jnp.logical_and(qi >= blk, kj >= qi - blk), has_next)
            valid = jnp.logical_or(jnp.logical_and(qi < blk, qi >= kj), next_ok)
            s = lax.dot_general(qw, k, _NT, preferred_element_type=F32) * _ATT_SCALE + bias_t
            p = jnp.where(valid, jnp.exp(jnp.where(valid, s, -1e30) - lrow), 0.0)
            dp = lax.dot_general(dow, v, _NT, preferred_element_type=F32)
            ds = p * (dp - drow)
            d_v = lax.dot_general(p.astype(BF16), dow, _TN, preferred_element_type=F32)
            d_k = lax.dot_general(ds.astype(BF16), qw, _TN, preferred_element_type=F32)
            dqkv_ref[1, pl.ds(r0, blk), :] = (d_k * _ATT_SCALE).astype(BF16)
            dqkv_ref[2, pl.ds(r0, blk), :] = d_v.astype(BF16)
            return carry

        lax.fori_loop(0, N_BLK, dkv_body, 0, unroll=ATT_UNROLL)

    return pl.pallas_call(
        kern, name=f"att_bwd_g{gi}", grid=(ATT_HPG,),
        in_specs=[_head_specs(0), _head_specs(ATT_HPG), _head_specs(2 * ATT_HPG),
                  _head_specs(0), _head_specs(0), _head_specs(0),
                  pl.BlockSpec((None, None, blk, 2 * blk), lambda h: (gi, h, 0, 0))],
        out_specs=[pl.BlockSpec((3, SEQ, dh), lambda h: (0, 0, h)),
                   pl.BlockSpec((None, blk, 2 * blk), lambda h: (h, 0, 0))],
        out_shape=[jax.ShapeDtypeStruct((3, SEQ, ATT_W), BF16),
                   jax.ShapeDtypeStruct((ATT_HPG, blk, 2 * blk), F32)],
        scratch_shapes=[pltpu.VMEM((_PAD_ROWS, dh), BF16)] * 4 + [pltpu.VMEM((_PAD_ROWS, dh), F32)] * 2,
        compiler_params=_cparams(("arbitrary",)),
    )(qkv, qkv, qkv, d_att, lse, dd, bias)


def _rms_parts(x):
    r = lax.rsqrt(jnp.mean(x * x, axis=-1, keepdims=True) + RMS_EPS)
    return x * r, r


def _rms_bwd(d_xhat, xhat, r):
    return r * (d_xhat - xhat * jnp.mean(d_xhat * xhat, axis=-1, keepdims=True))


def _prenorm_fwd(name, x, gain, shift, scale):
    def body(xt, g, sh, sc):
        xhat, _ = _rms_parts(xt)
        return (xhat * g) * (1.0 + sc) + sh
    return _rowmap(name, body, [x], [gain, shift, scale], [(D_MODEL, BF16)])[0]


def _prenorm_bwd(name, d_h, x, gain, scale, resid, branch=None, gate=None, after=()):
    gated = branch is not None

    def body(d_ht, xt, res, *rest):
        g, sc = rest[-2 - gated], rest[-1 - gated]
        xhat, r = _rms_parts(xt)
        nrm = xhat * g
        d_n = d_ht * (1.0 + sc)
        dx = _rms_bwd(d_n * g, xhat, r) + res
        sums = (jnp.sum(d_ht, axis=0, keepdims=True), jnp.sum(d_ht * nrm, axis=0, keepdims=True),
                jnp.sum(d_n * xhat, axis=0, keepdims=True))
        if not gated:
            return (dx,) + sums
        return (dx, dx * rest[-1]) + sums + (jnp.sum(dx * rest[0], axis=0, keepdims=True),)

    return _rowmap(name, body, [d_h, x, resid] + ([branch] if gated else []),
                   [gain, scale] + ([gate] if gated else []),
                   [(D_MODEL, F32)] + ([(D_MODEL, BF16)] if gated else []),
                   [D_MODEL] * (3 + gated), after=after)


def _gn_parts(ro):
    mu = jnp.mean(ro, axis=-1, keepdims=True)
    cen = ro - mu
    rstd = lax.rsqrt(jnp.mean(cen * cen, axis=-1, keepdims=True) + GN_EPS)
    return cen * rstd, rstd


def _retpost_fwd(ro, rg, gn_g, gn_b):
    def body(rot, rgt, g, b):
        outs = []
        for h in range(RET_HEADS):
            sl = slice(h * RET_DV, (h + 1) * RET_DV)
            nrm, _ = _gn_parts(rot[:, sl])
            gate = rgt[:, sl].astype(F32)
            outs.append((gate * _sigmoid(gate)) * (nrm * g[:, sl] + b[:, sl]))
        return jnp.concatenate(outs, axis=-1)
    return _rowmap("retpost_fwd", body, [ro, rg], [gn_g, gn_b], [(RET_V_W, BF16)])[0]


def _retpost_bwd(d_gated, ro, rg, gn_g, gn_b):
    def body(dgt, rot, rgt, g, b):
        d_ro, d_rg, d_g, d_b = [], [], [], []
        for h in range(RET_HEADS):
            sl = slice(h * RET_DV, (h + 1) * RET_DV)
            nrm, rstd = _gn_parts(rot[:, sl])
            gate, dg = rgt[:, sl].astype(F32), dgt[:, sl].astype(F32)
            sg = _sigmoid(gate)
            ron = nrm * g[:, sl] + b[:, sl]
            d_rg.append(dg * ron * (sg * (1.0 + gate * (1.0 - sg))))
            d_ron = dg * (gate * sg)
            d_g.append(jnp.sum(d_ron * nrm, axis=0, keepdims=True))
            d_b.append(jnp.sum(d_ron, axis=0, keepdims=True))
            d_n = d_ron * g[:, sl]
            d_ro.append(rstd * (d_n - jnp.mean(d_n, axis=-1, keepdims=True)
                                - nrm * jnp.mean(d_n * nrm, axis=-1, keepdims=True)))
        cat = lambda ts: jnp.concatenate(ts, axis=-1)
        return cat(d_ro), cat(d_rg), cat(d_g), cat(d_b)
    return _rowmap("retpost_bwd", body, [d_gated, ro, rg], [gn_g, gn_b],
                   [(RET_V_W, BF16), (RET_V_W, BF16)], [RET_V_W, RET_V_W])


def _combine(os_, ls_):
    def body(o0, o1, o2, l0, l1, l2):
        mx = jnp.maximum(jnp.maximum(l0, l1), l2)
        e0, e1, e2 = jnp.exp(l0 - mx), jnp.exp(l1 - mx), jnp.exp(l2 - mx)
        den = e0 + e1 + e2
        att = (e0 / den) * o0 + (e1 / den) * o1 + (e2 / den) * o2
        return att, att, mx + jnp.log(den)
    return _rowmap("att_combine", body, list(os_) + list(ls_), [],
                   [(ATT_W, F32), (ATT_W, BF16), (ATT_W, F32)])


def _att_bwd_pre(d_att, att):
    def body(dt, at):
        outs = []
        for h in range(ATT_HPG):
            sl = slice(h * ATT_DH, (h + 1) * ATT_DH)
            outs.append(jnp.broadcast_to(jnp.sum(dt[:, sl] * at[:, sl], axis=-1, keepdims=True),
                                         (dt.shape[0], ATT_DH)))
        return dt, jnp.concatenate(outs, axis=-1)
    return _rowmap("att_bwd_pre", body, [d_att, att], [], [(ATT_W, BF16), (ATT_W, F32)])


def _merge_fwd(gates, ret_out, att_out):
    def body(gt, ro, ao):
        gt = gt.astype(F32)
        return _sigmoid(gt[:, :D_MODEL]) * ro + _sigmoid(gt[:, D_MODEL:]) * ao
    return _rowmap("merge_fwd", body, [gates, ret_out, att_out], [], [(D_MODEL, BF16)])[0]


def _merge_bwd(d_merged, gates, ret_out, att_out):
    def body(dm, gt, ro, ao):
        dm, gt = dm.astype(F32), gt.astype(F32)
        sa, sb = _sigmoid(gt[:, :D_MODEL]), _sigmoid(gt[:, D_MODEL:])
        d_gates = jnp.concatenate([dm * ro * (sa * (1.0 - sa)), dm * ao * (sb * (1.0 - sb))], axis=-1)
        return dm * sa, dm * sb, d_gates
    return _rowmap("merge_bwd", body, [d_merged, gates, ret_out, att_out], [],
                   [(D_MODEL, BF16), (D_MODEL, BF16), (2 * D_MODEL, BF16)])


def _loss_head(x3, target, gain, branch, gate):
    def body(xt, tt, br, g, gt):
        xhat, r = _rms_parts(xt)
        err = xhat * g - tt
        d_y = err / D_MODEL
        loss = 0.5 * jnp.sum(jnp.mean(err * err, axis=-1, keepdims=True), axis=0, keepdims=True)
        d_x = _rms_bwd(d_y * g, xhat, r)
        return (d_x, d_x * gt, jnp.broadcast_to(loss, (1, 128)), jnp.sum(d_y * xhat, axis=0, keepdims=True),
                jnp.sum(d_x * br, axis=0, keepdims=True))
    return _rowmap("loss_head", body, [x3, target, branch], [gain, gate],
                   [(D_MODEL, F32), (D_MODEL, BF16)], [128, D_MODEL, D_MODEL])


def _local_step(pos, x, target, mod, norm1_g, norm2_g, norm_f_g, rel_bias, gn_g, gn_b, w_in, rest_gather):
    sh1, sc1, g1, sh2, sc2, g2 = [mod[:, i * D_MODEL:(i + 1) * D_MODEL] for i in range(6)]
    cos, sin = _rope_tables()
    din, qd, kd, cd = _decay_tables()
    buckets = _bucket_tables()
    bias = _bias_build(rel_bias, buckets)
    dils = [d for _, d in ATT_GROUPS]
    nbs = [SEQ // d // ATT_BLK for d in dils]

    h1 = _prenorm_fwd("prenorm1_fwd", x, norm1_g, sh1, sc1)
    h1_p = [_permute_rows(h1, d) for d in dils]

    qk_tn = 2 * RET_DK

    def rot_epi(acc, cs, sn, scale):
        half = RET_DK // 2
        outs = []
        for h0 in range(0, qk_tn, RET_DK):
            x1, x2 = acc[:, h0:h0 + half], acc[:, h0 + half:h0 + RET_DK]
            outs += [x1 * cs - x2 * sn, x1 * sn + x2 * cs]
        return (jnp.concatenate(outs, axis=-1) * scale,)

    qk_scale = jnp.concatenate([jnp.ones((1, RET_QK_W), F32),
                                jnp.full((1, RET_QK_W), RET_DK ** -0.5, F32)], axis=-1)
    rope_ex = [(cos, (TM, RET_DK // 2), lambda i, j, kk: (i, 0)),
               (sin, (TM, RET_DK // 2), lambda i, j, kk: (i, 0)),
               (qk_scale, (1, qk_tn), lambda i, j, kk: (0, j))]
    rest_sems, rest_shards, rest_fulls, rest_token = rest_gather
    behind = [rest_token]
    rv = _matmul("proj_rv", h1, w_in, "nn", SEQ, RET_V_W, D_MODEL, [BF16], b_off=OFF_V, tk=D_MODEL,
                 after=behind)[0]
    rg = _matmul("proj_rg", h1, w_in, "nn", SEQ, RET_V_W, D_MODEL, [BF16], b_off=OFF_G, tk=D_MODEL,
                 after=behind)[0]
    gates = _matmul("proj_gates", h1, w_in, "nn", SEQ, 2 * D_MODEL, D_MODEL, [BF16], b_off=OFF_GATE,
                    tn=512, tk=D_MODEL, after=behind)[0]
    aqkv = [_matmul(f"proj_att_g{gi}", h1_p[gi], w_in, "nn", SEQ, 3 * ATT_W, D_MODEL, [BF16],
                    b_off=OFF_ATT + gi * 3 * ATT_W, tn=512, tk=D_MODEL, after=behind)[0]
            for gi in range(3)]

    os_, ls_ = [], []
    for gi in range(3):
        o_g, l_g = _att_fwd(gi, aqkv[gi], bias, nbs[gi])
        os_.append(_unpermute_rows(o_g, dils[gi]))
        ls_.append(_unpermute_rows(l_g, dils[gi]))
        if gi == 1:
            rest_sems, rest_fulls, fwd_token = _gather_rest_forward(rest_sems, rest_shards, rest_fulls,
                                                                    [o_g, rv, rg, gates])

    rqk = _matmul("proj_qk", h1, w_in, "nn", SEQ, 2 * RET_QK_W, D_MODEL, [BF16], b_off=OFF_Q,
                  tn=qk_tn, tk=D_MODEL, epilogue=rot_epi, extras=rope_ex, after=[fwd_token])[0]
    ro, states = _retention_fwd(rqk, rv, din, qd, kd, cd)
    gated = _retpost_fwd(ro, rg, gn_g, gn_b)
    w_ret_out, w_att_out, w_o, w_ff1, w_ff2 = _gather_rest_end(rest_sems, rest_fulls, [gated, os_[2]])
    ret_out = _matmul("ret_out", gated, w_ret_out, "nn", SEQ, D_MODEL, RET_V_W, [F32], tk=RET_V_W)[0]
    att, att_b, lse = _combine(os_, ls_)
    att_out = _matmul("att_out", att_b, w_att_out, "nn", SEQ, D_MODEL, ATT_W, [F32])[0]

    merged = _merge_fwd(gates, ret_out, att_out)

    def resid_epi(acc, xt, g):
        return xt + g * acc, acc

    def resid_ex(xin, g):
        return [(xin, (TM, TN), lambda i, j, kk: (i, j)), (g, (1, TN), lambda i, j, kk: (0, j))]

    x2, mix = _matmul("mix_out", merged, w_o, "nn", SEQ, D_MODEL, D_MODEL, [F32, F32],
                      epilogue=resid_epi, extras=resid_ex(x, g1))
    h2 = _prenorm_fwd("prenorm2_fwd", x2, norm2_g, sh2, sc2)

    def relu2_epi(acc):
        r = jnp.maximum(acc, 0.0)
        return r * r, r

    act, relu_u = _matmul("ff1", h2, w_ff1, "nn", SEQ, D_FF, D_MODEL, [BF16, BF16], tk=D_MODEL,
                          epilogue=relu2_epi)
    x3, y2 = _matmul("ff2", act, w_ff2, "nn", SEQ, D_MODEL, D_FF, [F32, F32], tk=2048,
                     epilogue=resid_epi, extras=resid_ex(x2, g2))

    d_x3, d_y2, loss, d_gf, d_g2 = _loss_head(x3, target, norm_f_g, y2, g2)

    def relu2_bwd_epi(acc, rt):
        return (acc * (2.0 * rt.astype(F32)),)

    gw_ff2 = _matmul_tn_pair("ff2_dw", pos, act, d_y2, D_FF, D_MODEL, SEQ, D_FF // N_CHIPS,
                             tm=512, tn=1024, tk=1024)
    d_u = _matmul("ff2_dx", d_y2, w_ff2, "nt", SEQ, D_FF, D_MODEL, [BF16], epilogue=relu2_bwd_epi,
                  extras=[(relu_u, (TM, TN), lambda i, j, kk: (i, j))])[0]
    gw_ff1 = _matmul_tn_pair("ff1_dw", pos, h2, d_u, D_MODEL, D_FF, SEQ, D_MODEL,
                             tm=512, tn=1024, tk=1024)
    ffn = ["w_ff2", "w_ff1"]
    ffn_started = _ici_start("ici_start_ffn", ffn, [gw_ff2, gw_ff1])
    d_h2 = _matmul("ff1_dx", d_u, w_ff1, "nt", SEQ, D_MODEL, D_FF, [F32], tk=2048,
                   after=[ffn_started[3]])[0]
    d_x2, d_mix, d_sh2, d_sc2, d_n2g, d_g1 = _prenorm_bwd("prenorm2_bwd", d_h2, x2, norm2_g, sc2, d_x3,
                                                          branch=mix, gate=g1)
    gw_o = _matmul_tn_pair("mix_dw", pos, merged, d_mix, D_MODEL, D_MODEL, SEQ, D_MODEL // N_CHIPS,
                           tm=128, tn=1024, tk=2048)
    d_merged = _matmul("mix_dx", d_mix, w_o, "nt", SEQ, D_MODEL, D_MODEL, [BF16])[0]
    d_ret_out, d_att_out, d_gates = _merge_bwd(d_merged, gates, ret_out, att_out)

    gw_ret_out = _matmul_tn_pair("ret_out_dw", pos, gated, d_ret_out, RET_V_W, D_MODEL, SEQ,
                                 RET_V_W // N_CHIPS, tm=256, tn=1024, tk=1024)
    gw_att_out = _matmul_tn_pair("att_out_dw", pos, att_b, d_att_out, ATT_W, D_MODEL, SEQ, ATT_W,
                                 tm=256, tn=1024, tk=2048)
    mixer = ["w_o", "w_ret_out", "w_att_out"]
    mixer_started = _ici_start("ici_start_mixer", mixer, [gw_o, gw_ret_out, gw_att_out])
    d_gated = _matmul("ret_out_dx", d_ret_out, w_ret_out, "nt", SEQ, RET_V_W, D_MODEL, [BF16],
                      after=[mixer_started[3]])[0]
    d_att = _matmul("att_out_dx", d_att_out, w_att_out, "nt", SEQ, ATT_W, D_MODEL, [F32],
                    after=[mixer_started[3]])[0]

    d_ro, d_rg, d_gn_g, d_gn_b = _retpost_bwd(d_gated, ro, rg, gn_g, gn_b)
    d_rqkv = _retention_bwd(rqk, rv, states, d_ro, din, qd, kd, cd, cos, sin)

    d_att_b, dd = _att_bwd_pre(d_att, att)
    d_aqkv, dsbs = [], []
    for gi in range(3):
        da_p = _permute_rows(d_att_b, dils[gi])
        l_p = _permute_rows(lse, dils[gi])
        dd_p = _permute_rows(dd, dils[gi])
        dqkv, dsb = _att_bwd(gi, aqkv[gi], da_p, l_p, dd_p, bias, nbs[gi])
        if dils[gi] > 1:
            dqkv = dqkv.reshape(3, dils[gi], SEQ // dils[gi], ATT_W).transpose(0, 2, 1, 3).reshape(
                3, SEQ, ATT_W)
        d_aqkv.append(dqkv)
        dsbs.append(dsb)
    d_rel_bias = _bias_grad(jnp.stack(dsbs), buckets)

    d_proj = [(d_rqkv, False), (d_rg, False)] + [(t, True) for t in d_aqkv] + [(d_gates, False)]
    gw_in = _matmul_tn_pair("proj_dw", pos, h1, d_proj, D_MODEL, IN_COLS, SEQ, D_MODEL,
                            tm=512, tn=ATT_W, tk=SEQ)
    sems, (gw_in,), (land,), token = _ici_start("ici_start_w_in", ["w_in"], [gw_in])
    d_h1 = _matmul("proj_dx", d_proj, w_in, "nt", SEQ, D_MODEL, IN_COLS, [F32], tn=1024, tk=ATT_W,
                   after=[token])[0]
    pending = (sems, land)

    names = ffn + mixer
    psums, got = _ici_wait("ici_wait_rest", names, list(ffn_started[0]) + list(mixer_started[0]),
                           list(ffn_started[1]) + list(mixer_started[1]),
                           list(ffn_started[2]) + list(mixer_started[2]), [d_h1])
    g_big = {n: _final_sum("final_" + n, pos, dict(BIG)[n], psums[i], got[i], SHARD[n])
             for i, n in enumerate(names)}
    grad_x, d_sh1, d_sc1, d_n1g = _prenorm_bwd("prenorm1_bwd", d_h1, x, norm1_g, sc1, d_x2,
                                               after=list(g_big.values()))
    d_mod = jnp.concatenate([d_sh1, d_sc1, d_g1, d_sh2, d_sc2, d_g2], axis=-1)
    small = dict(norm1_g=d_n1g, norm2_g=d_n2g, norm_f_g=d_gf, gn_g=d_gn_g, gn_b=d_gn_b,
                 rel_bias=d_rel_bias)
    return loss, grad_x, d_mod, small, g_big, (gw_in,) + pending


def _me():
    return lax.axis_index("x"), lax.axis_index("y"), lax.axis_index("c")


def _peer(x, y, c, mask):
    return (x ^ ((mask >> 2) & 1), y ^ ((mask >> 1) & 1), c ^ (mask & 1))


def _gather8(src_ref, dst_ref, send_sems, recv_sems):
    x, y, c = _me()
    me = 4 * x + 2 * y + c
    copies = []
    for mask in range(1, N_DEV):
        cp = pltpu.make_async_remote_copy(
            src_ref=src_ref, dst_ref=dst_ref.at[me], send_sem=send_sems.at[mask - 1],
            recv_sem=recv_sems.at[mask - 1], device_id=_peer(x, y, c, mask), device_id_type=MESH)
        cp.start()
        copies.append(cp)
    dst_ref[me] = src_ref[...]
    for cp in copies:
        cp.wait_recv()
    for cp in copies:
        cp.wait_send()


def _ada_fwd(c_in, w_ada, b_ada):
    ncol = ADA_COLS // N_CHIPS

    def body(c_ref, w_ref, b_ref, mod_ref, sc_ref, cbuf, cg, mbuf, mg, s1, r1, s2, r2):
        x, y, c = _me()
        me = 4 * x + 2 * y + c
        cv = c_ref[...]
        cbuf[...] = jnp.broadcast_to(cv * _sigmoid(cv), cbuf.shape)
        _gather8(cbuf, cg, s1, r1)
        rows = lax.broadcasted_iota(I32, (N_DEV, D_MODEL), 0)
        sc_all = jnp.zeros((N_DEV, D_MODEL), F32)
        for d in range(N_DEV):
            sc_all = jnp.where(rows == d, cg[d], sc_all)
        sc_ref[...] = sc_all
        mbuf[...] = jnp.dot(sc_all.astype(BF16), w_ref[...].astype(BF16), preferred_element_type=F32)
        _gather8(mbuf, mg, s2, r2)
        rowsel = lax.broadcasted_iota(I32, (N_DEV, ncol), 0) == me
        for k in range(N_CHIPS):
            blk = mg[2 * k]
            row = jnp.sum(jnp.where(rowsel, blk, 0.0), axis=0, keepdims=True)
            mod_ref[:, k * ncol:(k + 1) * ncol] = row + b_ref[:, k * ncol:(k + 1) * ncol]

    vm = pl.BlockSpec(memory_space=pltpu.VMEM)
    return pl.pallas_call(
        body, name="ada_fwd",
        in_specs=[vm, vm, vm], out_specs=[vm, vm],
        out_shape=[jax.ShapeDtypeStruct((1, ADA_COLS), F32), jax.ShapeDtypeStruct((N_DEV, D_MODEL), F32)],
        scratch_shapes=[
            pltpu.VMEM((8, D_MODEL), F32), pltpu.VMEM((N_DEV, 8, D_MODEL), F32),
            pltpu.VMEM((8, ncol), F32), pltpu.VMEM((N_DEV, 8, ncol), F32),
            pltpu.SemaphoreType.DMA((N_DEV - 1,)), pltpu.SemaphoreType.DMA((N_DEV - 1,)),
            pltpu.SemaphoreType.DMA((N_DEV - 1,)), pltpu.SemaphoreType.DMA((N_DEV - 1,)),
        ],
        compiler_params=pltpu.CompilerParams(vmem_limit_bytes=VMEM_LIMIT_V7X),
    )(c_in, w_ada, b_ada)


def _small_reduce(pack, sc_all):
    ncol = ADA_COLS // N_CHIPS

    def body(p_ref, sc_ref, tot_ref, gw_ref, pg, s1, r1):
        x, y, _ = _me()
        chip = 2 * x + y
        _gather8(p_ref, pg, s1, r1)
        tot = pg[0]
        for d in range(1, N_DEV):
            tot = tot + pg[d]
        tot_ref[...] = tot
        rows = lax.broadcasted_iota(I32, (N_DEV, ncol), 0)
        dmod = jnp.zeros((N_DEV, ncol), F32)
        for k in range(N_CHIPS):
            part = jnp.zeros((N_DEV, ncol), F32)
            for d in range(N_DEV):
                part = jnp.where(rows == d, pg[d, :, k * ncol:(k + 1) * ncol][0:1, :], part)
            dmod = jnp.where(chip == k, part, dmod)
        gw_ref[...] = lax.dot_general(sc_ref[...].astype(BF16), dmod.astype(BF16), _TN,
                                      preferred_element_type=F32)

    vm = pl.BlockSpec(memory_space=pltpu.VMEM)
    return pl.pallas_call(
        body, name="small_reduce",
        in_specs=[vm, vm], out_specs=[vm, vm],
        out_shape=[jax.ShapeDtypeStruct((8, ADA_COLS), F32), jax.ShapeDtypeStruct((D_MODEL, ncol), F32)],
        scratch_shapes=[pltpu.VMEM((N_DEV, 8, ADA_COLS), F32),
                        pltpu.SemaphoreType.DMA((N_DEV - 1,)), pltpu.SemaphoreType.DMA((N_DEV - 1,))],
        compiler_params=pltpu.CompilerParams(vmem_limit_bytes=VMEM_LIMIT_V7X),
    )(pack, sc_all)


BIG = (("w_in", 1), ("w_ret_out", 0), ("w_att_out", 1), ("w_o", 0), ("w_ff1", 1), ("w_ff2", 0))
SHARD = {"w_in": (D_MODEL, IN_COLS // N_CHIPS), "w_ret_out": (RET_V_W // N_CHIPS, D_MODEL),
         "w_att_out": (ATT_W, D_MODEL // N_CHIPS), "w_o": (D_MODEL // N_CHIPS, D_MODEL),
         "w_ff1": (D_MODEL, D_FF // N_CHIPS), "w_ff2": (D_FF // N_CHIPS, D_MODEL)}
_CHIP_FLIPS = ((1, 0), (0, 1), (1, 1))


def _region(ref, axis, chip, half, shard_shape):
    r, cw = shard_shape
    hr = r // 2
    if axis == 1:
        return ref.at[pl.ds(half * hr, hr), pl.ds(chip * cw, cw)]
    return ref.at[pl.ds(chip * r + half * hr, hr), :]


def _gather_weights(shards, n_remote):
    nw = len(BIG)
    shapes = [s.shape for s in shards]
    full_shapes = [(r, N_CHIPS * cw) if ax == 1 else (N_CHIPS * r, cw)
                   for (r, cw), (_, ax) in zip(shapes, BIG)]

    def body(*refs):
        ins, outs = refs[:nw], refs[nw:2 * nw]
        own = refs[2 * nw:3 * nw]
        from_ici, from_sib = refs[3 * nw:3 * nw + n_remote], refs[3 * nw + n_remote:3 * nw + 2 * n_remote]
        ld_sem, st_sem, s_ici, r_ici, s_d2d, r_d2d, st_a, st_b = refs[3 * nw + 2 * n_remote:]
        x, y, c = _me()
        chip = 2 * x + y
        sib = (x, y, 1 - c)
        loads = [pltpu.make_async_copy(ins[i], own[i], ld_sem.at[i]) for i in range(nw)]
        for cp in loads:
            cp.start()
        pending, first = [], []
        for i, (_, ax) in enumerate(BIG):
            r, cw = shapes[i]
            hr = r // 2
            loads[i].wait()
            dst = outs[i].at[:, pl.ds(chip * cw, cw)] if ax == 1 else outs[i].at[pl.ds(chip * r, r), :]
            cp = pltpu.make_async_copy(own[i], dst, st_sem.at[i])
            cp.start()
            pending.append(cp)
            for j, (fx, fy) in enumerate(_CHIP_FLIPS if i < n_remote else ()):
                rc = pltpu.make_async_remote_copy(
                    src_ref=own[i].at[pl.ds(c * hr, hr), :], dst_ref=from_ici[i].at[j],
                    send_sem=s_ici.at[j * nw + i], recv_sem=r_ici.at[j * nw + i],
                    device_id=(x ^ fx, y ^ fy, c), device_id_type=MESH)
                rc.start()
                first.append((j, i, rc))
        passed = []
        for j, i, rc in first:
            fx, fy = _CHIP_FLIPS[j]
            src_chip = 2 * (x ^ fx) + (y ^ fy)
            ax = BIG[i][1]
            rc.wait_recv()
            fw = pltpu.make_async_remote_copy(
                src_ref=from_ici[i].at[j], dst_ref=from_sib[i].at[j], send_sem=s_d2d.at[j * nw + i],
                recv_sem=r_d2d.at[j * nw + i], device_id=sib, device_id_type=MESH)
            fw.start()
            passed.append((j, i, src_chip, fw))
            st = pltpu.make_async_copy(from_ici[i].at[j], _region(outs[i], ax, src_chip, c, shapes[i]),
                                       st_a.at[j * nw + i])
            st.start()
            pending.append(st)
        for j, i, src_chip, fw in passed:
            fw.wait_recv()
            st = pltpu.make_async_copy(from_sib[i].at[j],
                                       _region(outs[i], BIG[i][1], src_chip, 1 - c, shapes[i]),
                                       st_b.at[j * nw + i])
            st.start()
            pending.append(st)
        for _, _, rc in first:
            rc.wait_send()
        for _, _, _, fw in passed:
            fw.wait_send()
        for cp in pending:
            cp.wait()

    hbm = pl.BlockSpec(memory_space=pl.ANY)
    halves = [pltpu.VMEM((3, r // 2, cw), BF16) for r, cw in shapes[:n_remote]]
    return pl.pallas_call(
        body, name="gather_weights",
        in_specs=[hbm] * nw, out_specs=[hbm] * nw,
        out_shape=[jax.ShapeDtypeStruct(fs, BF16) for fs in full_shapes],
        scratch_shapes=[pltpu.VMEM(sh, BF16) for sh in shapes] + halves + halves
        + [pltpu.SemaphoreType.DMA((nw,)), pltpu.SemaphoreType.DMA((nw,))]
        + [pltpu.SemaphoreType.DMA((3 * nw,))] * 6,
        compiler_params=pltpu.CompilerParams(vmem_limit_bytes=VMEM_LIMIT_V7X),
    )(*shards)


REST = BIG[1:]
_SIDE_EFFECTS = pltpu.CompilerParams(has_side_effects=pltpu.SideEffectType.DATAFLOW_SIDE_EFFECTING)
_ANY_SPEC = pl.BlockSpec(memory_space=pl.ANY)


def _rest_ici_copies(shard_refs, full_refs, sems):
    x, y, c = _me()
    chip = 2 * x + y
    n = 3 * len(REST)
    copies = []
    for i, (name, ax) in enumerate(REST):
        hr = SHARD[name][0] // 2
        for j, (fx, fy) in enumerate(_CHIP_FLIPS):
            copies.append(pltpu.make_async_remote_copy(
                src_ref=shard_refs[i].at[pl.ds(c * hr, hr), :],
                dst_ref=_region(full_refs[i], ax, chip, c, SHARD[name]),
                send_sem=sems[3 * i + j], recv_sem=sems[n + 3 * i + j],
                device_id=(x ^ fx, y ^ fy, c), device_id_type=MESH))
    return copies


def _rest_d2d_copies(full_refs, sems):
    x, y, c = _me()
    n = 3 * len(REST)
    copies = []
    for i, (name, ax) in enumerate(REST):
        for j, (fx, fy) in enumerate(_CHIP_FLIPS):
            reg = _region(full_refs[i], ax, 2 * (x ^ fx) + (y ^ fy), c, SHARD[name])
            copies.append(pltpu.make_async_remote_copy(
                src_ref=reg, dst_ref=reg, send_sem=sems[3 * i + j], recv_sem=sems[n + 3 * i + j],
                device_id=(x, y, 1 - c), device_id_type=MESH))
    return copies


def _gather_rest_start(shards, fulls, after):
    nr, ns, na = len(REST), 6 * len(REST), len(after)

    def body(*refs):
        for cp in _rest_ici_copies(refs[:nr], refs[nr:2 * nr], refs[2 * nr + na:2 * nr + na + ns]):
            cp.start()
        token = refs[-1]
        token[...] = jnp.zeros_like(token)

    hbm = lambda a: pltpu.HBM(a.shape, a.dtype)
    res = pl.pallas_call(
        body, name="gather_rest_start",
        out_shape=(pltpu.SemaphoreType.DMA(()),) * ns + tuple(hbm(a) for a in shards + fulls)
        + (jax.ShapeDtypeStruct((8, 128), F32),),
        in_specs=(_HBM_SPEC,) * (2 * nr) + (_ANY_SPEC,) * na,
        out_specs=(_SEM_SPEC,) * ns + (_HBM_SPEC,) * (2 * nr) + (pl.BlockSpec(memory_space=pltpu.VMEM),),
        input_output_aliases={k: ns + k for k in range(2 * nr)}, compiler_params=_SIDE_EFFECTS,
    )(*[pltpu.with_memory_space_constraint(a, pltpu.HBM) for a in shards + fulls], *after)
    return res[:ns], res[ns:ns + nr], res[ns + nr:ns + 2 * nr], res[-1]


def _gather_rest_forward(sems, shards, fulls, after):
    nr, ns = len(REST), 6 * len(REST)

    def body(*refs):
        shard_refs, full_refs, old = refs[:nr], refs[nr:2 * nr], refs[2 * nr:2 * nr + ns]
        new = refs[2 * nr + ns + len(after):2 * nr + 2 * ns + len(after)]
        for cp in _rest_ici_copies(shard_refs, full_refs, old):
            cp.wait_send()
            cp.wait_recv()
        for cp in _rest_d2d_copies(full_refs, new):
            cp.start()
        token = refs[-1]
        token[...] = jnp.zeros_like(token)

    res = pl.pallas_call(
        body, name="gather_rest_forward",
        out_shape=(pltpu.SemaphoreType.DMA(()),) * ns + tuple(pltpu.HBM(a.shape, a.dtype) for a in fulls)
        + (jax.ShapeDtypeStruct((8, 128), F32),),
        in_specs=(_HBM_SPEC,) * (2 * nr) + (_SEM_SPEC,) * ns + (_ANY_SPEC,) * len(after),
        out_specs=(_SEM_SPEC,) * ns + (_HBM_SPEC,) * nr + (pl.BlockSpec(memory_space=pltpu.VMEM),),
        input_output_aliases={nr + k: ns + k for k in range(nr)}, compiler_params=_SIDE_EFFECTS,
    )(*shards, *fulls, *sems, *after)
    return res[:ns], res[ns:ns + nr], res[-1]


def _gather_rest_end(sems, fulls, after):
    nr, ns = len(REST), 6 * len(REST)

    def body(*refs):
        for cp in _rest_d2d_copies(refs[:nr], refs[nr:nr + ns]):
            cp.wait_send()
            cp.wait_recv()

    return pl.pallas_call(
        body, name="gather_rest_end",
        out_shape=tuple(pltpu.HBM(a.shape, a.dtype) for a in fulls),
        in_specs=(_HBM_SPEC,) * nr + (_SEM_SPEC,) * ns + (_ANY_SPEC,) * len(after),
        out_specs=(_HBM_SPEC,) * nr,
        input_output_aliases={k: k for k in range(nr)}, compiler_params=_SIDE_EFFECTS,
    )(*fulls, *sems, *after)


def _adam_update(w, g, m, v):
    mn = ADAM_B1 * m + (1.0 - ADAM_B1) * g
    vn = ADAM_B2 * v + (1.0 - ADAM_B2) * (g * g)
    m_hat = mn / (1.0 - ADAM_B1 ** ADAM_STEP)
    v_hat = vn / (1.0 - ADAM_B2 ** ADAM_STEP)
    return -ADAM_LR * (m_hat / (jnp.sqrt(v_hat) + ADAM_EPS) + ADAM_WD * w), mn, vn


def _final_sum(name, pos, axis, psum, recv, shard_shape, after=(), tr=128):
    r, cw = shard_shape
    hr = r // 2
    tr = min(tr, hr)
    nt = hr // tr
    n_after = len(after)

    def kern(pos_ref, p_ref, r_ref, *rest):
        g_ref, send_buf, land_buf, s_sem, r_sem = rest[n_after:]
        p, t = pl.program_id(0), pl.program_id(1)
        sib = _sibling()

        def copy(i):
            return pltpu.make_async_remote_copy(
                src_ref=send_buf.at[i], dst_ref=land_buf.at[i], send_sem=s_sem.at[i],
                recv_sem=r_sem.at[i], device_id=sib, device_id_type=MESH)

        @pl.when(p == 0)
        def _():
            tot = p_ref[...].astype(F32)
            for j in range(3):
                tot = tot + r_ref[j].astype(F32)
            send_buf[t] = tot
            copy(t).start()
            g_ref[...] = tot

        @pl.when(p == 1)
        def _():
            copy(t).wait_recv()
            g_ref[...] = land_buf[t]

        @pl.when(jnp.logical_and(p == 1, t == nt - 1))
        def _():
            for i in range(nt):
                copy(i).wait_send()

    def shard_rows(p, t, pos_ref):
        return (jnp.where(p == 0, pos_ref[0], 1 - pos_ref[0]) * nt + t, 0)

    def own_part(p, t, pos_ref):
        tt = jnp.where(p == 0, t, nt - 1)
        return (tt, pos_ref[1]) if axis == 1 else (pos_ref[1] * nt + tt, 0)

    grid_spec = pltpu.PrefetchScalarGridSpec(
        num_scalar_prefetch=1, grid=(2, nt),
        in_specs=[pl.BlockSpec((tr, cw), own_part),
                  pl.BlockSpec((3, tr, cw), lambda p, t, pos_ref: (0, jnp.where(p == 0, t, nt - 1), 0))]
        + [pl.BlockSpec(memory_space=pl.ANY)] * n_after,
        out_specs=pl.BlockSpec((tr, cw), shard_rows),
        scratch_shapes=[pltpu.VMEM((nt, tr, cw), F32), pltpu.VMEM((nt, tr, cw), F32),
                        pltpu.SemaphoreType.DMA((nt,)), pltpu.SemaphoreType.DMA((nt,))])
    return pl.pallas_call(
        kern, name=name, grid_spec=grid_spec, out_shape=jax.ShapeDtypeStruct((r, cw), F32),
        compiler_params=_cparams(("arbitrary", "arbitrary")),
    )(pos, psum, recv, *after)


def _adamw(name, w, g, m, v):
    r, cw = w.shape
    tr = min(r, 128)

    def kern(w_ref, g_ref, m_ref, v_ref, go_ref, d_ref, nm_ref, nv_ref):
        gv = g_ref[...]
        go_ref[...] = gv
        d_ref[...], nm_ref[...], nv_ref[...] = _adam_update(w_ref[...], gv, m_ref[...], v_ref[...])

    spec = pl.BlockSpec((tr, cw), lambda i: (i, 0))
    return pl.pallas_call(
        kern, name=name, grid=(r // tr,), in_specs=[spec] * 4, out_specs=[spec] * 4,
        out_shape=[jax.ShapeDtypeStruct((r, cw), F32)] * 4, compiler_params=_cparams(("parallel",)),
    )(w, g, m, v)


_PACK_W = ADA_COLS
_NB = REL_BUCKETS * N_ATT_HEADS
_SMALL_SLOTS = {
    "b_ada": (0, 0, ADA_COLS),
    "norm1_g": (1, 0, D_MODEL), "norm2_g": (1, D_MODEL, D_MODEL), "norm_f_g": (1, 2 * D_MODEL, D_MODEL),
    "ret_gn_g": (1, 3 * D_MODEL, RET_V_W),
    "ret_gn_b": (2, 0, RET_V_W), "rel_bias": (2, RET_V_W, _NB), "loss": (2, RET_V_W + 512, 128),
}


def _pack_small(vals):
    rows = []
    for r in range(8):
        items = sorted([(off, n) for n, (rr, off, _) in _SMALL_SLOTS.items() if rr == r and n in vals])
        parts, pos = [], 0
        for off, n in items:
            if off > pos:
                parts.append(jnp.zeros((1, off - pos), F32))
            parts.append(vals[n].reshape(1, -1).astype(F32))
            pos = off + _SMALL_SLOTS[n][2]
        if pos < _PACK_W:
            parts.append(jnp.zeros((1, _PACK_W - pos), F32))
        rows.append(jnp.concatenate(parts, axis=-1))
    return jnp.concatenate(rows, axis=0)


def _unpack_small(pack, name):
    r, off, wd = _SMALL_SLOTS[name]
    return pack[r:r + 1, off:off + wd]


def kernel(x, c, w_ada, b_ada, norm1_g, w_in, rel_bias, ret_gn_g, ret_gn_b, w_ret_out, w_att_out, w_o, norm2_g, w_ff1, w_ff2, norm_f_g, loss_target, m_w_ada, m_b_ada, m_norm1_g, m_w_in, m_rel_bias, m_ret_gn_g, m_ret_gn_b, m_w_ret_out, m_w_att_out, m_w_o, m_norm2_g, m_w_ff1, m_w_ff2, m_norm_f_g, v_w_ada, v_b_ada, v_norm1_g, v_w_in, v_rel_bias, v_ret_gn_g, v_ret_gn_b, v_w_ret_out, v_w_att_out, v_w_o, v_norm2_g, v_w_ff1, v_w_ff2, v_norm_f_g):
    given = dict(locals())
    big_names = [n for n, _ in BIG]
    shard_w = {n: given[n][0] for n in big_names}
    assert all(shard_w[n].shape == SHARD[n] for n in big_names)

    shards_bf = [shard_w[n].astype(BF16) for n in big_names]
    full = _gather_weights(shards_bf, 1)
    mod, sc_all = _ada_fwd(c, w_ada[0], b_ada)
    rest_gather = _gather_rest_start(shards_bf[1:], list(full[1:]), [mod])
    pos = _where_am_i()

    loss, grad_x, d_mod, small, g_big, pending = _local_step(
        pos, x[0], loss_target[0], mod, norm1_g, norm2_g, norm_f_g.reshape(1, -1), rel_bias, ret_gn_g,
        ret_gn_b, full[0], rest_gather)

    pack_g = _pack_small(dict(b_ada=d_mod, norm1_g=small["norm1_g"], norm2_g=small["norm2_g"],
                              norm_f_g=small["norm_f_g"], ret_gn_g=small["gn_g"], ret_gn_b=small["gn_b"],
                              rel_bias=small["rel_bias"], loss=loss))
    tot, g_w_ada = _small_reduce(pack_g, sc_all)

    small_names = ["b_ada", "norm1_g", "rel_bias", "ret_gn_g", "ret_gn_b", "norm2_g", "norm_f_g"]
    pack_w = _pack_small({n: given[n] for n in small_names})
    pack_m = _pack_small({n: given["m_" + n] for n in small_names})
    pack_v = _pack_small({n: given["v_" + n] for n in small_names})
    _, sd, sm, sv = _adamw("adamw_small", pack_w, tot, pack_m, pack_v)

    grads, deltas, new_m, new_v = {}, {}, {}, {}
    for n in small_names:
        shp = given[n].shape
        grads[n] = _unpack_small(tot, n).reshape(shp)
        deltas[n] = _unpack_small(sd, n).reshape(shp)
        new_m[n] = _unpack_small(sm, n).reshape(shp)
        new_v[n] = _unpack_small(sv, n).reshape(shp)
    g_big["w_ada"] = g_w_ada
    for n in ["w_ada"] + big_names[1:] + big_names[:1]:
        if n == "w_in":
            gw_in, sems, land = pending
            done = [tot, sd] + [deltas[k] for k in ["w_ada"] + big_names[1:]]
            (gw_in,), (got,) = _ici_wait("ici_wait_w_in", [n], sems, [gw_in], [land], done)
            g_big[n] = _final_sum("final_w_in", pos, 1, gw_in, got, SHARD[n])
        g, d, nm, nv = _adamw("adamw_" + n, given[n][0], g_big[n], given["m_" + n][0], given["v_" + n][0])
        grads[n], deltas[n], new_m[n], new_v[n] = g[None], d[None], nm[None], nv[None]

    order = ["w_ada", "b_ada", "norm1_g", "w_in", "rel_bias", "ret_gn_g", "ret_gn_b", "w_ret_out",
             "w_att_out", "w_o", "norm2_g", "w_ff1", "w_ff2", "norm_f_g"]
    loss_out = _unpack_small(tot, "loss")[0, 0]
    return (loss_out, grad_x[None], *[grads[n] for n in order], *[deltas[n] for n in order],
            *[new_m[n] for n in order], *[new_v[n] for n in order])
```

```python
import functools
import math

import jax
import jax.numpy as jnp
import numpy as np
from jax import lax
from jax.experimental import pallas as pl
from jax.experimental.pallas import tpu as pltpu

F32 = jnp.float32
BF16 = jnp.bfloat16
I32 = jnp.int32

SEQ = 2048
D_MODEL = 1024
RET_HEADS = 4
RET_DK = 256
RET_DV = 512
RET_CHUNK = 128
RET_SUB = 2
RET_QK_W = RET_HEADS * RET_DK
RET_V_W = RET_HEADS * RET_DV
ATT_GROUPS = ((128, 1), (512, 4), (2048, 16))
ATT_HPG = 4
ATT_DH = 128
ATT_W = ATT_HPG * ATT_DH
ATT_BLK = 128
N_BLK = SEQ // ATT_BLK
REL_BUCKETS = 32
REL_MAX_DIST = 2048
N_ATT_HEADS = 12
D_FF = 4 * D_MODEL
RMS_EPS = 1e-6
GN_EPS = 1e-5
ROPE_BASE = 10000.0
IN_COLS = 2 * RET_QK_W + 2 * RET_V_W + 9 * ATT_W + 2 * D_MODEL
OFF_Q, OFF_K, OFF_V, OFF_G = 0, RET_QK_W, 2 * RET_QK_W, 2 * RET_QK_W + RET_V_W
OFF_ATT = 2 * RET_QK_W + 2 * RET_V_W
OFF_GATE = OFF_ATT + 9 * ATT_W
N_CHIPS = 4
N_DEV = 8
ADA_COLS = 6 * D_MODEL

ADAM_LR = 0.001
ADAM_B1 = 0.9
ADAM_B2 = 0.999
ADAM_EPS = 1e-08
ADAM_WD = 0.01
ADAM_STEP = 10

VMEM_LIMIT_V7X = 56 * 1024 * 1024
MESH = pl.DeviceIdType.MESH


def _cparams(sem):
    return pltpu.CompilerParams(dimension_semantics=sem, vmem_limit_bytes=VMEM_LIMIT_V7X)


def _sigmoid(v):
    return 1.0 / (1.0 + jnp.exp(-v))


def _rowmap(name, body, row_ins, bcast_ins, row_outs, sum_outs=(), tm=256, after=()):
    m = row_ins[0].shape[0]
    n_in = len(row_ins) + len(bcast_ins)
    n_ro = len(row_outs)

    def kern(*refs):
        vals = [r[...] for r in refs[:n_in]]
        res = body(*vals)
        if not isinstance(res, (tuple, list)):
            res = (res,)
        outs = refs[n_in + len(after):]
        for r, v in zip(outs[:n_ro], res[:n_ro]):
            r[...] = v.astype(r.dtype)
        if sum_outs:
            @pl.when(pl.program_id(0) == 0)
            def _():
                for r in outs[n_ro:]:
                    r[...] = jnp.zeros_like(r)
            for r, v in zip(outs[n_ro:], res[n_ro:]):
                r[...] += v

    in_specs = [pl.BlockSpec((tm, a.shape[1]), lambda i: (i, 0)) for a in row_ins]
    in_specs += [pl.BlockSpec(a.shape, lambda i: (0, 0)) for a in bcast_ins]
    in_specs += [pl.BlockSpec(memory_space=pl.ANY)] * len(after)
    out_specs = [pl.BlockSpec((tm, n), lambda i: (i, 0)) for n, _ in row_outs]
    out_specs += [pl.BlockSpec((1, n), lambda i: (0, 0)) for n in sum_outs]
    out_shape = [jax.ShapeDtypeStruct((m, n), dt) for n, dt in row_outs]
    out_shape += [jax.ShapeDtypeStruct((1, n), F32) for n in sum_outs]
    return pl.pallas_call(
        kern, name=name, grid=(m // tm,), in_specs=in_specs, out_specs=out_specs,
        out_shape=out_shape, compiler_params=_cparams(("arbitrary",)),
    )(*row_ins, *bcast_ins, *after)


TM, TN = 1024, 1024


def _piece_chunks(piece, width):
    arr, stacked = piece
    return arr.shape[0] if stacked else arr.shape[1] // width


def _piece_spec(piece, rows, width, start, row_of, chunk_of):
    arr, stacked = piece
    last = _piece_chunks(piece, width) - 1

    def local(*ids):
        return jnp.clip(chunk_of(*ids) - start, 0, last)

    def row(*ids):
        rel = chunk_of(*ids) - start
        return jnp.where(jnp.logical_and(rel >= 0, rel <= last), row_of(*ids), 0)

    if stacked:
        return pl.BlockSpec((None, rows, width), lambda *ids: (local(*ids), row(*ids), 0))
    return pl.BlockSpec((rows, width), lambda *ids: (row(*ids), local(*ids)))


def _piece_starts(pieces, width):
    return [sum(_piece_chunks(p, width) for p in pieces[:q]) for q in range(len(pieces))]


def _matmul(name, a, b, kind, m, n, k, outs, *, b_off=0, tm=TM, tn=TN, tk=1024,
            epilogue=None, extras=(), after=()):
    tm, tn, tk = min(tm, m), min(tn, n), min(tk, k)
    nk = k // tk
    pieces = a if isinstance(a, list) else [(a, False)]
    starts = _piece_starts(pieces, tk)
    if kind == "nn":
        a_specs = [pl.BlockSpec((tm, tk), lambda i, j, kk: (i, kk))]
        b_spec = pl.BlockSpec((tk, tn), lambda i, j, kk: (kk, b_off // tn + j))
        dn = (((1,), (0,)), ((), ()))
    elif kind == "nt":
        a_specs = [_piece_spec(p, tm, tk, st, lambda i, j, kk: i, lambda i, j, kk: kk)
                   for p, st in zip(pieces, starts)]
        b_spec = pl.BlockSpec((tn, tk), lambda i, j, kk: (j, b_off // tk + kk))
        dn = (((1,), (1,)), ((), ()))
    else:
        a_specs = [pl.BlockSpec((tk, tm), lambda i, j, kk: (kk, i))]
        b_spec = pl.BlockSpec((tk, tn), lambda i, j, kk: (kk, j))
        dn = (((0,), (0,)), ((), ()))
    n_a, n_ex, n_out = len(pieces), len(extras), len(outs)
    if epilogue is None:
        epilogue = lambda acc: (acc,)

    def finish(acc, ex_refs, out_refs):
        res = epilogue(acc, *[r[...] for r in ex_refs])
        for r, v in zip(out_refs, res):
            r[...] = v.astype(r.dtype)

    n_in = n_a + 1 + n_ex + len(after)

    def kern(*refs):
        a_refs, b_ref = refs[:n_a], refs[n_a]
        ex_refs = refs[n_a + 1:n_a + 1 + n_ex]
        out_refs = refs[n_in:n_in + n_out]
        kk = pl.program_id(2)
        dot = lambda a_ref: lax.dot_general(a_ref[...], b_ref[...], dn, preferred_element_type=F32)
        if nk == 1:
            finish(dot(a_refs[0]), ex_refs, out_refs)
            return
        acc_ref = refs[n_in + n_out]
        if n_a == 1:
            part = dot(a_refs[0])

            @pl.when(kk == 0)
            def _():
                acc_ref[...] = part

            @pl.when(kk > 0)
            def _():
                acc_ref[...] += part
        else:
            @pl.when(kk == 0)
            def _():
                acc_ref[...] = jnp.zeros_like(acc_ref)

            for q in range(n_a):
                @pl.when(jnp.logical_and(kk >= starts[q], kk < starts[q] + _piece_chunks(pieces[q], tk)))
                def _(q=q):
                    acc_ref[...] += dot(a_refs[q])

        @pl.when(kk == nk - 1)
        def _():
            finish(acc_ref[...], ex_refs, out_refs)

    in_specs = a_specs + [b_spec] + [pl.BlockSpec(bs, im) for _, bs, im in extras]
    in_specs += [pl.BlockSpec(memory_space=pl.ANY)] * len(after)
    return pl.pallas_call(
        kern, name=name, grid=(m // tm, n // tn, nk), in_specs=in_specs,
        out_specs=[pl.BlockSpec((tm, tn), lambda i, j, kk: (i, j)) for _ in outs],
        out_shape=[jax.ShapeDtypeStruct((m, n), dt) for dt in outs],
        scratch_shapes=[] if nk == 1 else [pltpu.VMEM((tm, tn), F32)],
        compiler_params=_cparams(("parallel", "parallel", "arbitrary")),
    )(*[p[0] for p in pieces], b, *[e[0] for e in extras], *after)


def _ici_copies(psum_ref, recv_ref, s_sem, r_sem, axis, shard_shape):
    x, y, c = _me()
    hr, cw = shard_shape[0] // 2, shard_shape[1]
    pick = lambda sems, j: sems[j] if isinstance(sems, (list, tuple)) else sems.at[j]
    copies = []
    for j, (fx, fy) in enumerate(_CHIP_FLIPS):
        chip = 2 * (x ^ fx) + (y ^ fy)
        src = psum_ref.at[:, pl.ds(chip * cw, cw)] if axis == 1 else psum_ref.at[pl.ds(chip * hr, hr), :]
        copies.append(pltpu.make_async_remote_copy(
            src_ref=src, dst_ref=recv_ref.at[j], send_sem=pick(s_sem, j), recv_sem=pick(r_sem, j),
            device_id=(x ^ fx, y ^ fy, c), device_id_type=MESH))
    return copies


_HBM_SPEC = pl.BlockSpec(memory_space=pltpu.HBM)
_SEM_SPEC = pl.BlockSpec(memory_space=pltpu.SEMAPHORE)


def _split_ici_copies(names, p_refs, land_refs, sems):
    copies = []
    for i, n in enumerate(names):
        copies += _ici_copies(p_refs[i], land_refs[i], list(sems[6 * i:6 * i + 3]),
                              list(sems[6 * i + 3:6 * i + 6]), dict(BIG)[n], SHARD[n])
    return copies


def _ici_start(name, names, psums):
    nw, ns = len(names), 6 * len(names)
    lands = [lax.empty((3, SHARD[n][0] // 2, SHARD[n][1]), BF16) for n in names]

    def body(*refs):
        for cp in _split_ici_copies(names, refs[:nw], refs[nw:2 * nw], refs[2 * nw:2 * nw + ns]):
            cp.start()
        token = refs[-1]
        token[...] = jnp.zeros_like(token)

    res = pl.pallas_call(
        body, name=name,
        out_shape=(pltpu.SemaphoreType.DMA(()),) * ns
        + tuple(pltpu.HBM(a.shape, BF16) for a in list(psums) + lands)
        + (jax.ShapeDtypeStruct((8, 128), F32),),
        in_specs=(_HBM_SPEC,) * (2 * nw),
        out_specs=(_SEM_SPEC,) * ns + (_HBM_SPEC,) * (2 * nw) + (pl.BlockSpec(memory_space=pltpu.VMEM),),
        input_output_aliases={k: ns + k for k in range(2 * nw)},
        compiler_params=pltpu.CompilerParams(has_side_effects=pltpu.SideEffectType.DATAFLOW_SIDE_EFFECTING),
    )(*[pltpu.with_memory_space_constraint(a, pltpu.HBM) for a in list(psums) + lands])
    return res[:ns], res[ns:ns + nw], res[ns + nw:ns + 2 * nw], res[-1]


def _ici_wait(name, names, sems, p_thru, land_thru, after):
    nw, ns = len(names), 6 * len(names)

    def body(*refs):
        for cp in _split_ici_copies(names, refs[:nw], refs[nw:2 * nw], refs[2 * nw:2 * nw + ns]):
            cp.wait_send()
            cp.wait_recv()

    res = pl.pallas_call(
        body, name=name,
        out_shape=tuple(pltpu.HBM(a.shape, BF16) for a in list(p_thru) + list(land_thru)),
        in_specs=(_HBM_SPEC,) * (2 * nw) + (_SEM_SPEC,) * ns + (pl.BlockSpec(memory_space=pl.ANY),) * len(after),
        out_specs=(_HBM_SPEC,) * (2 * nw), input_output_aliases={k: k for k in range(2 * nw)},
        compiler_params=pltpu.CompilerParams(has_side_effects=pltpu.SideEffectType.DATAFLOW_SIDE_EFFECTING),
    )(*p_thru, *land_thru, *sems, *after)
    return res[:nw], res[nw:]


def _where_am_i():
    x, y, c = _me()
    return jnp.stack([c, 2 * x + y]).astype(I32)


def _sibling():
    x, y, c = _me()
    return (x, y, 1 - c)


N_SEND_SLOTS = 2


def _matmul_tn_pair(name, pos, a, b, m, n, k, shard_rows, *, tm, tn, tk):
    hr = shard_rows // 2
    tm, tn, tk = min(tm, hr), min(tn, n), min(tk, k)
    tph = hr // tm
    nt, nj, nk = (m // 2) // tm, n // tn, k // tk
    n_tiles = nt * nj

    def row_block(p, t, pos_ref):
        half = jnp.where(p == 0, 1 - pos_ref[0], pos_ref[0])
        return (t // tph) * (2 * tph) + half * tph + t % tph

    pieces = b if isinstance(b, list) else [(b, False)]
    starts = _piece_starts(pieces, tn)
    n_b = len(pieces)

    def kern(pos_ref, a_ref, *rest):
        b_refs = rest[:n_b]
        o_ref, acc_ref, send_buf, land_buf, s_sem, r_sem = rest[n_b:]
        p, t, j, kk = pl.program_id(0), pl.program_id(1), pl.program_id(2), pl.program_id(3)
        idx = t * nj + j
        sib = _sibling()

        def copy(i):
            return pltpu.make_async_remote_copy(
                src_ref=send_buf.at[i % N_SEND_SLOTS], dst_ref=land_buf.at[i], send_sem=s_sem.at[i],
                recv_sem=r_sem.at[i], device_id=sib, device_id_type=MESH)

        @pl.when(kk == 0)
        def _():
            acc_ref[...] = jnp.zeros_like(acc_ref)

        for q in range(n_b):
            @pl.when(jnp.logical_and(j >= starts[q], j < starts[q] + _piece_chunks(pieces[q], tn)))
            def _(q=q):
                acc_ref[...] += lax.dot_general(a_ref[...], b_refs[q][...], _TN, preferred_element_type=F32)

        @pl.when(jnp.logical_and(kk == nk - 1, p == 0))
        def _():
            @pl.when(idx >= N_SEND_SLOTS)
            def _():
                copy(idx - N_SEND_SLOTS).wait_send()

            send_buf[idx % N_SEND_SLOTS] = acc_ref[...].astype(BF16)
            copy(idx).start()

        @pl.when(jnp.logical_and(kk == nk - 1, p == 1))
        def _():
            copy(idx).wait_recv()
            o_ref[...] = (acc_ref[...] + land_buf[idx].astype(F32)).astype(BF16)

        @pl.when(jnp.logical_and(jnp.logical_and(p == 1, idx == n_tiles - 1), kk == nk - 1))
        def _():
            for i in range(max(n_tiles - N_SEND_SLOTS, 0), n_tiles):
                copy(i).wait_send()

    grid_spec = pltpu.PrefetchScalarGridSpec(
        num_scalar_prefetch=1, grid=(2, nt, nj, nk),
        in_specs=[pl.BlockSpec((tk, tm), lambda p, t, j, kk, pos_ref: (kk, row_block(p, t, pos_ref)))]
        + [_piece_spec(pc, tk, tn, st, lambda p, t, j, kk, pos_ref: kk, lambda p, t, j, kk, pos_ref: j)
           for pc, st in zip(pieces, starts)],
        out_specs=pl.BlockSpec((tm, tn), lambda p, t, j, kk, pos_ref: (p * t, p * j)),
        scratch_shapes=[pltpu.VMEM((tm, tn), F32), pltpu.VMEM((N_SEND_SLOTS, tm, tn), BF16),
                        pltpu.VMEM((n_tiles, tm, tn), BF16),
                        pltpu.SemaphoreType.DMA((n_tiles,)), pltpu.SemaphoreType.DMA((n_tiles,))])
    return pl.pallas_call(
        kern, name=name, grid_spec=grid_spec, out_shape=jax.ShapeDtypeStruct((m // 2, n), BF16),
        compiler_params=_cparams(("arbitrary",) * 4),
    )(pos, a, *[pc[0] for pc in pieces])


def _rope_tables():
    half = RET_DK // 2
    f32 = np.float32
    inv = np.power(f32(ROPE_BASE), -np.arange(half, dtype=f32) / f32(half)).astype(f32)
    ang = (np.arange(SEQ, dtype=f32)[:, None] * inv[None, :]).astype(f32)
    return jnp.asarray(np.cos(ang).astype(f32)), jnp.asarray(np.sin(ang).astype(f32))


def _decay_tables():
    c = RET_CHUNK
    f32 = np.float32
    log_g = np.log1p(-np.power(f32(2.0), f32(-5.0) - np.arange(RET_HEADS, dtype=f32))).astype(f32)
    idx = np.arange(c, dtype=f32)
    rel = idx[:, None] - idx[None, :]
    din = np.where(rel >= 0, np.exp(log_g[:, None, None] * np.maximum(rel, f32(0.0))), f32(0.0)).astype(f32)
    qd = np.exp(log_g[:, None] * (idx + f32(1.0))).astype(f32)[:, :, None]
    kd = np.exp(log_g[:, None] * (f32(c) - f32(1.0) - idx)).astype(f32)[:, :, None]
    cd = np.exp(log_g * f32(c)).astype(f32)
    return jnp.asarray(din), jnp.asarray(qd), jnp.asarray(kd), jnp.asarray(cd)


def _t5_bucket(dist):
    max_exact = REL_BUCKETS // 2
    d_f = jnp.maximum(dist, 1).astype(F32)
    large = max_exact + (jnp.log(d_f / max_exact) / math.log(REL_MAX_DIST / max_exact)
                         * (REL_BUCKETS - max_exact)).astype(I32)
    large = jnp.minimum(large, REL_BUCKETS - 1)
    return jnp.where(dist < max_exact, dist, large)


def _bucket_tables():
    qi = jnp.arange(ATT_BLK)[:, None]
    kj = jnp.arange(2 * ATT_BLK)[None, :]
    dist = jnp.clip(ATT_BLK + qi - kj, 0, ATT_BLK)
    return jnp.stack([_t5_bucket(dist * dil) for _, dil in ATT_GROUPS]).astype(I32)


def _retention_fwd(rqk, rv, din, qd, kd, cd):
    nc = SEQ // RET_CHUNK
    c, dk, dv = RET_CHUNK, RET_DK, RET_DV

    def kern(q_ref, k_ref, v_ref, din_ref, qd_ref, kd_ref, cd_ref, o_ref, st_ref, state):
        n = pl.program_id(0)

        @pl.when(n == 0)
        def _():
            state[...] = jnp.zeros_like(state)

        for sub in range(RET_SUB):
            rows = slice(sub * c, (sub + 1) * c)
            for h in range(RET_HEADS):
                q, k = q_ref[rows, h * dk:(h + 1) * dk], k_ref[rows, h * dk:(h + 1) * dk]
                v = v_ref[rows, h * dv:(h + 1) * dv]
                s_b = state[h].astype(BF16)
                st_ref[h, sub] = s_b
                a = lax.dot_general(q, k, _NT, preferred_element_type=F32) * din_ref[h]
                o = jnp.dot(a.astype(BF16), v, preferred_element_type=F32)
                o += jnp.dot(q, s_b, preferred_element_type=F32) * qd_ref[h]
                o_ref[rows, h * dv:(h + 1) * dv] = o
                kk = (k.astype(F32) * kd_ref[h]).astype(BF16)
                state[h] = state[h] * cd_ref[h] + lax.dot_general(kk, v, _TN, preferred_element_type=F32)

    whole = lambda a: pl.BlockSpec(a.shape, lambda n: (0,) * a.ndim)
    cs = RET_SUB * c
    return pl.pallas_call(
        kern, name="retention_fwd", grid=(nc // RET_SUB,),
        in_specs=[
            pl.BlockSpec((cs, RET_QK_W), lambda n: (n, 0)),
            pl.BlockSpec((cs, RET_QK_W), lambda n: (n, 1)),
            pl.BlockSpec((cs, RET_V_W), lambda n: (n, 0)),
            whole(din), whole(qd), whole(kd),
            pl.BlockSpec(memory_space=pltpu.SMEM),
        ],
        out_specs=[
            pl.BlockSpec((cs, RET_V_W), lambda n: (n, 0)),
            pl.BlockSpec((RET_HEADS, RET_SUB, dk, dv), lambda n: (0, n, 0, 0)),
        ],
        out_shape=[
            jax.ShapeDtypeStruct((SEQ, RET_V_W), F32),
            jax.ShapeDtypeStruct((RET_HEADS, nc, dk, dv), BF16),
        ],
        scratch_shapes=[pltpu.VMEM((RET_HEADS, dk, dv), F32)],
        compiler_params=_cparams(("arbitrary",)),
    )(rqk, rqk, rv, din, qd, kd, cd)


def _retention_bwd(rqk, rv, states, d_ro, din, qd, kd, cd, cos, sin):
    nc = SEQ // RET_CHUNK
    c, dk, dv = RET_CHUNK, RET_DK, RET_DV
    half = dk // 2
    last = nc // RET_SUB - 1

    def unrot(g, cs, sn):
        g1, g2 = g[:, :half], g[:, half:]
        return jnp.concatenate([g1 * cs + g2 * sn, g2 * cs - g1 * sn], axis=-1)

    def kern(q_ref, k_ref, v_ref, st_ref, do_ref, din_ref, qd_ref, kd_ref, cd_ref, cos_ref, sin_ref,
             out_ref, dstate):
        step = pl.program_id(0)

        @pl.when(step == 0)
        def _():
            dstate[...] = jnp.zeros_like(dstate)

        for sub in reversed(range(RET_SUB)):
            rows = slice(sub * c, (sub + 1) * c)
            cs, sn = cos_ref[rows, :], sin_ref[rows, :]
            for h in range(RET_HEADS):
                qk_cols, v_cols = slice(h * dk, (h + 1) * dk), slice(h * dv, (h + 1) * dv)
                q, k, v = q_ref[rows, qk_cols], k_ref[rows, qk_cols], v_ref[rows, v_cols]
                s_b = st_ref[h, sub]
                d_ob = do_ref[rows, v_cols]
                d_oq = (d_ob.astype(F32) * qd_ref[h]).astype(BF16)
                ds_b = dstate[h].astype(BF16)
                din_m = din_ref[h]
                a_b = (lax.dot_general(q, k, _NT, preferred_element_type=F32) * din_m).astype(BF16)
                kk = (k.astype(F32) * kd_ref[h]).astype(BF16)
                d_v = lax.dot_general(a_b, d_ob, _TN, preferred_element_type=F32)
                d_v += jnp.dot(kk, ds_b, preferred_element_type=F32)
                d_a = (lax.dot_general(d_ob, v, _NT, preferred_element_type=F32) * din_m).astype(BF16)
                d_q = jnp.dot(d_a, k, preferred_element_type=F32)
                d_q += lax.dot_general(d_oq, s_b, _NT, preferred_element_type=F32)
                d_k = lax.dot_general(d_a, q, _TN, preferred_element_type=F32)
                d_k += lax.dot_general(v, ds_b, _NT, preferred_element_type=F32) * kd_ref[h]
                dstate[h] = dstate[h] * cd_ref[h] + lax.dot_general(q, d_oq, _TN,
                                                                    preferred_element_type=F32)
                out_ref[rows, h * dk:(h + 1) * dk] = unrot(d_q, cs, sn).astype(BF16)
                out_ref[rows, RET_QK_W + h * dk:RET_QK_W + (h + 1) * dk] = (
                    unrot(d_k, cs, sn) * (RET_DK ** -0.5)).astype(BF16)
                out_ref[rows, 2 * RET_QK_W + h * dv:2 * RET_QK_W + (h + 1) * dv] = d_v.astype(BF16)

    whole = lambda a: pl.BlockSpec(a.shape, lambda n: (0,) * a.ndim)
    rs = RET_SUB * c
    return pl.pallas_call(
        kern, name="retention_bwd", grid=(nc // RET_SUB,),
        in_specs=[
            pl.BlockSpec((rs, RET_QK_W), lambda n: (last - n, 0)),
            pl.BlockSpec((rs, RET_QK_W), lambda n: (last - n, 1)),
            pl.BlockSpec((rs, RET_V_W), lambda n: (last - n, 0)),
            pl.BlockSpec((RET_HEADS, RET_SUB, dk, dv), lambda n: (0, last - n, 0, 0)),
            pl.BlockSpec((rs, RET_V_W), lambda n: (last - n, 0)),
            whole(din), whole(qd), whole(kd),
            pl.BlockSpec(memory_space=pltpu.SMEM),
            pl.BlockSpec((rs, half), lambda n: (last - n, 0)),
            pl.BlockSpec((rs, half), lambda n: (last - n, 0)),
        ],
        out_specs=pl.BlockSpec((rs, 2 * RET_QK_W + RET_V_W), lambda n: (last - n, 0)),
        out_shape=jax.ShapeDtypeStruct((SEQ, 2 * RET_QK_W + RET_V_W), BF16),
        scratch_shapes=[pltpu.VMEM((RET_HEADS, dk, dv), F32)],
        compiler_params=_cparams(("arbitrary",)),
    )(rqk, rqk, rv, states, d_ro, din, qd, kd, cd, cos, sin)


def _bias_build(rel_bias, buckets):
    ng = len(ATT_GROUPS)

    def kern(tab_ref, bkt_ref, o_ref):
        g, h = pl.program_id(0), pl.program_id(1)
        bkt = bkt_ref[...]
        acc = jnp.zeros(bkt.shape, F32)
        for b in range(REL_BUCKETS):
            acc = jnp.where(bkt == b, tab_ref[b, g * ATT_HPG + h], acc)
        o_ref[...] = acc

    return pl.pallas_call(
        kern, name="bias_build", grid=(ng, ATT_HPG),
        in_specs=[pl.BlockSpec(memory_space=pltpu.SMEM),
                  pl.BlockSpec((None, ATT_BLK, 2 * ATT_BLK), lambda g, h: (g, 0, 0))],
        out_specs=pl.BlockSpec((None, None, ATT_BLK, 2 * ATT_BLK), lambda g, h: (g, h, 0, 0)),
        out_shape=jax.ShapeDtypeStruct((ng, ATT_HPG, ATT_BLK, 2 * ATT_BLK), F32),
        compiler_params=_cparams(("arbitrary", "arbitrary")),
    )(rel_bias, buckets)


def _bias_grad(dsb, buckets):
    ng = len(ATT_GROUPS)

    def kern(ds_ref, bkt_ref, o_ref):
        g, h = pl.program_id(0), pl.program_id(1)
        bkt, ds = bkt_ref[...], ds_ref[...]
        for b in range(REL_BUCKETS):
            o_ref[b, g * ATT_HPG + h] = jnp.sum(jnp.where(bkt == b, ds, 0.0))

    return pl.pallas_call(
        kern, name="bias_grad", grid=(ng, ATT_HPG),
        in_specs=[pl.BlockSpec((None, None, ATT_BLK, 2 * ATT_BLK), lambda g, h: (g, h, 0, 0)),
                  pl.BlockSpec((None, ATT_BLK, 2 * ATT_BLK), lambda g, h: (g, 0, 0))],
        out_specs=pl.BlockSpec(memory_space=pltpu.SMEM),
        out_shape=jax.ShapeDtypeStruct((REL_BUCKETS, N_ATT_HEADS), F32),
        compiler_params=_cparams(("arbitrary", "arbitrary")),
    )(dsb, buckets)


_NT = (((1,), (1,)), ((), ()))
_TN = (((0,), (0,)), ((), ()))
_ATT_SCALE = ATT_DH ** -0.5


def _window_mask(has_prev):
    qi = lax.broadcasted_iota(I32, (ATT_BLK, 2 * ATT_BLK), 0)
    kj = lax.broadcasted_iota(I32, (ATT_BLK, 2 * ATT_BLK), 1)
    prev_ok = jnp.logical_and(jnp.logical_and(kj < ATT_BLK, kj >= qi), has_prev)
    return jnp.logical_or(prev_ok, jnp.logical_and(kj >= ATT_BLK, qi >= kj - ATT_BLK))


def _head_specs(col0):
    return pl.BlockSpec((SEQ, ATT_DH), lambda h: (0, col0 + h))


def _sub_rows(start, size, dil):
    return pl.ds(start, size) if dil == 1 else pl.ds(start, size, stride=dil)


def _att_blocks(dil):
    nb = SEQ // dil // ATT_BLK
    return [(r + dil * n * ATT_BLK, n > 0, n + 1 < nb) for r in range(dil) for n in range(nb)]


def _att_fwd(gi, dil, qkv, bias):
    blk, dh = ATT_BLK, ATT_DH
    pad = dil * blk
    col0 = 3 * ATT_HPG * gi

    def kern(q_ref, k_ref, v_ref, b_ref, o_ref, l_ref, qf, kpad, vpad):
        zero = jnp.zeros((pad, dh), F32)
        kpad[0:pad, :] = zero
        vpad[0:pad, :] = zero
        kpad[pad:, :] = k_ref[...].astype(F32)
        vpad[pad:, :] = v_ref[...].astype(F32)
        qf[...] = q_ref[...].astype(F32)
        bias_m = b_ref[...]
        for start, has_prev, _ in _att_blocks(dil):
            rows, window = _sub_rows(start, blk, dil), _sub_rows(start, 2 * blk, dil)
            q = qf[rows, :].astype(BF16)
            kw, vw = kpad[window, :].astype(BF16), vpad[window, :].astype(BF16)
            valid = _window_mask(has_prev)
            s = lax.dot_general(q, kw, _NT, preferred_element_type=F32) * _ATT_SCALE + bias_m
            s = jnp.where(valid, s, -1e30)
            mx = jnp.max(s, axis=-1, keepdims=True)
            e = jnp.exp(s - mx)
            den = jnp.sum(e, axis=-1, keepdims=True)
            o_ref[rows, :] = jnp.dot((e / den).astype(BF16), vw, preferred_element_type=F32)
            l_ref[rows, :] = jnp.broadcast_to(mx + jnp.log(den), (blk, dh))

    return pl.pallas_call(
        kern, name=f"att_fwd_g{gi}", grid=(ATT_HPG,),
        in_specs=[_head_specs(col0), _head_specs(col0 + ATT_HPG), _head_specs(col0 + 2 * ATT_HPG),
                  pl.BlockSpec((None, None, blk, 2 * blk), lambda h: (gi, h, 0, 0))],
        out_specs=[_head_specs(0), _head_specs(0)],
        out_shape=[jax.ShapeDtypeStruct((SEQ, ATT_W), F32), jax.ShapeDtypeStruct((SEQ, ATT_W), F32)],
        scratch_shapes=[pltpu.VMEM((SEQ, dh), F32), pltpu.VMEM((SEQ + pad, dh), F32),
                        pltpu.VMEM((SEQ + pad, dh), F32)],
        compiler_params=_cparams(("arbitrary",)),
    )(qkv, qkv, qkv, bias)


def _att_bwd(gi, dil, qkv, d_att, lse, dd, bias):
    blk, dh = ATT_BLK, ATT_DH
    pad = dil * blk
    col0 = 3 * ATT_HPG * gi

    def kern(q_ref, k_ref, v_ref, do_ref, l_ref, d_ref, b_ref, dqkv_ref, dsb_ref,
             kpad, vpad, qpad, dopad, lpad, dpad, dq_s, dk_s, dv_s):
        zero = jnp.zeros((pad, dh), F32)
        kpad[0:pad, :] = zero
        vpad[0:pad, :] = zero
        kpad[pad:, :] = k_ref[...].astype(F32)
        vpad[pad:, :] = v_ref[...].astype(F32)
        for ref, src in ((qpad, q_ref), (dopad, do_ref), (lpad, l_ref), (dpad, d_ref)):
            ref[SEQ:, :] = zero
            ref[0:SEQ, :] = src[...].astype(F32)
        bias_m = b_ref[...]
        bias_t = jnp.concatenate([bias_m[:, blk:], bias_m[:, :blk]], axis=0)
        ds_sum = jnp.zeros((blk, 2 * blk), F32)

        for start, has_prev, _ in _att_blocks(dil):
            rows, window = _sub_rows(start, blk, dil), _sub_rows(start, 2 * blk, dil)
            q, d_o = qpad[rows, :].astype(BF16), dopad[rows, :].astype(BF16)
            kw, vw = kpad[window, :].astype(BF16), vpad[window, :].astype(BF16)
            lrow, drow = lpad[rows, :][:, :1], dpad[rows, :][:, :1]
            valid = _window_mask(has_prev)
            s = lax.dot_general(q, kw, _NT, preferred_element_type=F32) * _ATT_SCALE + bias_m
            p = jnp.where(valid, jnp.exp(jnp.where(valid, s, -1e30) - lrow), 0.0)
            dp = lax.dot_general(d_o, vw, _NT, preferred_element_type=F32)
            ds = p * (dp - drow)
            dq_s[rows, :] = jnp.dot(ds.astype(BF16), kw, preferred_element_type=F32) * _ATT_SCALE
            ds_sum = ds_sum + ds
        dsb_ref[...] = ds_sum

        qi = lax.broadcasted_iota(I32, (2 * blk, blk), 0)
        kj = lax.broadcasted_iota(I32, (2 * blk, blk), 1)
        for start, _, has_next in _att_blocks(dil):
            rows, window = _sub_rows(start, blk, dil), _sub_rows(start, 2 * blk, dil)
            own = _sub_rows(pad + start, blk, dil)
            k, v = kpad[own, :].astype(BF16), vpad[own, :].astype(BF16)
            qw, dow = qpad[window, :].astype(BF16), dopad[window, :].astype(BF16)
            lrow, drow = lpad[window, :][:, :1], dpad[window, :][:, :1]
            next_ok = jnp.logical_and(jnp.logical_and(qi >= blk, kj >= qi - blk), has_next)
            valid = jnp.logical_or(jnp.logical_and(qi < blk, qi >= kj), next_ok)
            s = lax.dot_general(qw, k, _NT, preferred_element_type=F32) * _ATT_SCALE + bias_t
            p = jnp.where(valid, jnp.exp(jnp.where(valid, s, -1e30) - lrow), 0.0)
            dp = lax.dot_general(dow, v, _NT, preferred_element_type=F32)
            ds = p * (dp - drow)
            dv_s[rows, :] = lax.dot_general(p.astype(BF16), dow, _TN, preferred_element_type=F32)
            dk_s[rows, :] = lax.dot_general(ds.astype(BF16), qw, _TN, preferred_element_type=F32) * _ATT_SCALE

        dqkv_ref[0] = dq_s[...].astype(BF16)
        dqkv_ref[1] = dk_s[...].astype(BF16)
        dqkv_ref[2] = dv_s[...].astype(BF16)

    return pl.pallas_call(
        kern, name=f"att_bwd_g{gi}", grid=(ATT_HPG,),
        in_specs=[_head_specs(col0), _head_specs(col0 + ATT_HPG), _head_specs(col0 + 2 * ATT_HPG),
                  _head_specs(0), _head_specs(0), _head_specs(0),
                  pl.BlockSpec((None, None, blk, 2 * blk), lambda h: (gi, h, 0, 0))],
        out_specs=[pl.BlockSpec((3, SEQ, dh), lambda h: (0, 0, h)),
                   pl.BlockSpec((None, blk, 2 * blk), lambda h: (h, 0, 0))],
        out_shape=[jax.ShapeDtypeStruct((3, SEQ, ATT_W), BF16),
                   jax.ShapeDtypeStruct((ATT_HPG, blk, 2 * blk), F32)],
        scratch_shapes=[pltpu.VMEM((SEQ + pad, dh), F32)] * 6 + [pltpu.VMEM((SEQ, dh), F32)] * 3,
        compiler_params=_cparams(("arbitrary",)),
    )(qkv, qkv, qkv, d_att, lse, dd, bias)


def _rms_parts(x):
    r = lax.rsqrt(jnp.mean(x * x, axis=-1, keepdims=True) + RMS_EPS)
    return x * r, r


def _rms_bwd(d_xhat, xhat, r):
    return r * (d_xhat - xhat * jnp.mean(d_xhat * xhat, axis=-1, keepdims=True))


def _prenorm_fwd(name, x, gain, shift, scale):
    def body(xt, g, sh, sc):
        xhat, _ = _rms_parts(xt)
        return (xhat * g) * (1.0 + sc) + sh
    return _rowmap(name, body, [x], [gain, shift, scale], [(D_MODEL, BF16)])[0]


def _prenorm_bwd(name, d_h, x, gain, scale, resid, branch=None, gate=None, after=()):
    gated = branch is not None

    def body(d_ht, xt, res, *rest):
        g, sc = rest[-2 - gated], rest[-1 - gated]
        xhat, r = _rms_parts(xt)
        nrm = xhat * g
        d_n = d_ht * (1.0 + sc)
        dx = _rms_bwd(d_n * g, xhat, r) + res
        sums = (jnp.sum(d_ht, axis=0, keepdims=True), jnp.sum(d_ht * nrm, axis=0, keepdims=True),
                jnp.sum(d_n * xhat, axis=0, keepdims=True))
        if not gated:
            return (dx,) + sums
        return (dx, dx * rest[-1]) + sums + (jnp.sum(dx * rest[0], axis=0, keepdims=True),)

    return _rowmap(name, body, [d_h, x, resid] + ([branch] if gated else []),
                   [gain, scale] + ([gate] if gated else []),
                   [(D_MODEL, F32)] + ([(D_MODEL, BF16)] if gated else []),
                   [D_MODEL] * (3 + gated), after=after)


def _gn_parts(ro):
    mu = jnp.mean(ro, axis=-1, keepdims=True)
    cen = ro - mu
    rstd = lax.rsqrt(jnp.mean(cen * cen, axis=-1, keepdims=True) + GN_EPS)
    return cen * rstd, rstd


def _retpost_fwd(ro, rg, gn_g, gn_b):
    def body(rot, rgt, g, b):
        outs = []
        for h in range(RET_HEADS):
            sl = slice(h * RET_DV, (h + 1) * RET_DV)
            nrm, _ = _gn_parts(rot[:, sl])
            gate = rgt[:, sl].astype(F32)
            outs.append((gate * _sigmoid(gate)) * (nrm * g[:, sl] + b[:, sl]))
        return jnp.concatenate(outs, axis=-1)
    return _rowmap("retpost_fwd", body, [ro, rg], [gn_g, gn_b], [(RET_V_W, BF16)])[0]


def _retpost_bwd(d_gated, ro, rg, gn_g, gn_b):
    def body(dgt, rot, rgt, g, b):
        d_ro, d_rg, d_g, d_b = [], [], [], []
        for h in range(RET_HEADS):
            sl = slice(h * RET_DV, (h + 1) * RET_DV)
            nrm, rstd = _gn_parts(rot[:, sl])
            gate, dg = rgt[:, sl].astype(F32), dgt[:, sl].astype(F32)
            sg = _sigmoid(gate)
            ron = nrm * g[:, sl] + b[:, sl]
            d_rg.append(dg * ron * (sg * (1.0 + gate * (1.0 - sg))))
            d_ron = dg * (gate * sg)
            d_g.append(jnp.sum(d_ron * nrm, axis=0, keepdims=True))
            d_b.append(jnp.sum(d_ron, axis=0, keepdims=True))
            d_n = d_ron * g[:, sl]
            d_ro.append(rstd * (d_n - jnp.mean(d_n, axis=-1, keepdims=True)
                                - nrm * jnp.mean(d_n * nrm, axis=-1, keepdims=True)))
        cat = lambda ts: jnp.concatenate(ts, axis=-1)
        return cat(d_ro), cat(d_rg), cat(d_g), cat(d_b)
    return _rowmap("retpost_bwd", body, [d_gated, ro, rg], [gn_g, gn_b],
                   [(RET_V_W, BF16), (RET_V_W, BF16)], [RET_V_W, RET_V_W])


def _combine(os_, ls_):
    def body(o0, o1, o2, l0, l1, l2):
        mx = jnp.maximum(jnp.maximum(l0, l1), l2)
        e0, e1, e2 = jnp.exp(l0 - mx), jnp.exp(l1 - mx), jnp.exp(l2 - mx)
        den = e0 + e1 + e2
        att = (e0 / den) * o0 + (e1 / den) * o1 + (e2 / den) * o2
        return att, att, mx + jnp.log(den)
    return _rowmap("att_combine", body, list(os_) + list(ls_), [],
                   [(ATT_W, F32), (ATT_W, BF16), (ATT_W, F32)])


def _att_bwd_pre(d_att, att):
    def body(dt, at):
        outs = []
        for h in range(ATT_HPG):
            sl = slice(h * ATT_DH, (h + 1) * ATT_DH)
            outs.append(jnp.broadcast_to(jnp.sum(dt[:, sl] * at[:, sl], axis=-1, keepdims=True),
                                         (dt.shape[0], ATT_DH)))
        return jnp.concatenate(outs, axis=-1)
    return _rowmap("att_bwd_pre", body, [d_att, att], [], [(ATT_W, F32)])[0]


def _merge_fwd(gates, ret_out, att_out):
    def body(gt, ro, ao):
        gt = gt.astype(F32)
        return _sigmoid(gt[:, :D_MODEL]) * ro + _sigmoid(gt[:, D_MODEL:]) * ao
    return _rowmap("merge_fwd", body, [gates, ret_out, att_out], [], [(D_MODEL, BF16)])[0]


def _merge_bwd(d_merged, gates, ret_out, att_out):
    def body(dm, gt, ro, ao):
        dm, gt = dm.astype(F32), gt.astype(F32)
        sa, sb = _sigmoid(gt[:, :D_MODEL]), _sigmoid(gt[:, D_MODEL:])
        d_gates = jnp.concatenate([dm * ro * (sa * (1.0 - sa)), dm * ao * (sb * (1.0 - sb))], axis=-1)
        return dm * sa, dm * sb, d_gates
    return _rowmap("merge_bwd", body, [d_merged, gates, ret_out, att_out], [],
                   [(D_MODEL, BF16), (D_MODEL, BF16), (2 * D_MODEL, BF16)])


def _loss_head(x3, target, gain, branch, gate):
    def body(xt, tt, br, g, gt):
        xhat, r = _rms_parts(xt)
        err = xhat * g - tt
        d_y = err / D_MODEL
        loss = 0.5 * jnp.sum(jnp.mean(err * err, axis=-1, keepdims=True), axis=0, keepdims=True)
        d_x = _rms_bwd(d_y * g, xhat, r)
        return (d_x, d_x * gt, jnp.broadcast_to(loss, (1, 128)), jnp.sum(d_y * xhat, axis=0, keepdims=True),
                jnp.sum(d_x * br, axis=0, keepdims=True))
    return _rowmap("loss_head", body, [x3, target, branch], [gain, gate],
                   [(D_MODEL, F32), (D_MODEL, BF16)], [128, D_MODEL, D_MODEL])


def _local_step(pos, x, target, mod, norm1_g, norm2_g, norm_f_g, rel_bias, gn_g, gn_b, w_in, rest_gather):
    sh1, sc1, g1, sh2, sc2, g2 = [mod[:, i * D_MODEL:(i + 1) * D_MODEL] for i in range(6)]
    cos, sin = _rope_tables()
    din, qd, kd, cd = _decay_tables()
    buckets = _bucket_tables()
    bias = _bias_build(rel_bias, buckets)
    dils = [d for _, d in ATT_GROUPS]

    h1 = _prenorm_fwd("prenorm1_fwd", x, norm1_g, sh1, sc1)

    qk_tn = 2 * RET_DK

    def rot_epi(acc, cs, sn, scale):
        half = RET_DK // 2
        outs = []
        for h0 in range(0, qk_tn, RET_DK):
            x1, x2 = acc[:, h0:h0 + half], acc[:, h0 + half:h0 + RET_DK]
            outs += [x1 * cs - x2 * sn, x1 * sn + x2 * cs]
        return (jnp.concatenate(outs, axis=-1) * scale,)

    qk_scale = jnp.concatenate([jnp.ones((1, RET_QK_W), F32),
                                jnp.full((1, RET_QK_W), RET_DK ** -0.5, F32)], axis=-1)
    rope_ex = [(cos, (TM, RET_DK // 2), lambda i, j, kk: (i, 0)),
               (sin, (TM, RET_DK // 2), lambda i, j, kk: (i, 0)),
               (qk_scale, (1, qk_tn), lambda i, j, kk: (0, j))]
    rest_sems, rest_shards, rest_fulls, rest_token = rest_gather
    behind = [rest_token]
    rv = _matmul("proj_rv", h1, w_in, "nn", SEQ, RET_V_W, D_MODEL, [BF16], b_off=OFF_V, tk=D_MODEL,
                 after=behind)[0]
    rg = _matmul("proj_rg", h1, w_in, "nn", SEQ, RET_V_W, D_MODEL, [BF16], b_off=OFF_G, tk=D_MODEL,
                 after=behind)[0]
    gates = _matmul("proj_gates", h1, w_in, "nn", SEQ, 2 * D_MODEL, D_MODEL, [BF16], b_off=OFF_GATE,
                    tn=512, tk=D_MODEL, after=behind)[0]
    aqkv = _matmul("proj_att", h1, w_in, "nn", SEQ, 9 * ATT_W, D_MODEL, [BF16], b_off=OFF_ATT,
                   tn=512, tk=D_MODEL, after=behind)[0]

    os_, ls_ = [], []
    for gi in range(3):
        o_g, l_g = _att_fwd(gi, dils[gi], aqkv, bias)
        os_.append(o_g)
        ls_.append(l_g)
        if gi == 1:
            rest_sems, rest_fulls, fwd_token = _gather_rest_forward(rest_sems, rest_shards, rest_fulls,
                                                                    [o_g, rv, rg, gates])

    rqk = _matmul("proj_qk", h1, w_in, "nn", SEQ, 2 * RET_QK_W, D_MODEL, [BF16], b_off=OFF_Q,
                  tn=qk_tn, tk=D_MODEL, epilogue=rot_epi, extras=rope_ex, after=[fwd_token])[0]
    ro, states = _retention_fwd(rqk, rv, din, qd, kd, cd)
    gated = _retpost_fwd(ro, rg, gn_g, gn_b)
    w_ret_out, w_att_out, w_o, w_ff1, w_ff2 = _gather_rest_end(rest_sems, rest_fulls, [gated, os_[2]])
    ret_out = _matmul("ret_out", gated, w_ret_out, "nn", SEQ, D_MODEL, RET_V_W, [F32], tk=RET_V_W)[0]
    att, att_b, lse = _combine(os_, ls_)
    att_out = _matmul("att_out", att_b, w_att_out, "nn", SEQ, D_MODEL, ATT_W, [F32])[0]

    merged = _merge_fwd(gates, ret_out, att_out)

    def resid_epi(acc, xt, g):
        return xt + g * acc, acc

    def resid_ex(xin, g):
        return [(xin, (TM, TN), lambda i, j, kk: (i, j)), (g, (1, TN), lambda i, j, kk: (0, j))]

    x2, mix = _matmul("mix_out", merged, w_o, "nn", SEQ, D_MODEL, D_MODEL, [F32, F32],
                      epilogue=resid_epi, extras=resid_ex(x, g1))
    h2 = _prenorm_fwd("prenorm2_fwd", x2, norm2_g, sh2, sc2)

    def relu2_epi(acc):
        r = jnp.maximum(acc, 0.0)
        return r * r, r

    act, relu_u = _matmul("ff1", h2, w_ff1, "nn", SEQ, D_FF, D_MODEL, [BF16, BF16], tk=D_MODEL,
                          epilogue=relu2_epi)
    x3, y2 = _matmul("ff2", act, w_ff2, "nn", SEQ, D_MODEL, D_FF, [F32, F32], tk=2048,
                     epilogue=resid_epi, extras=resid_ex(x2, g2))

    d_x3, d_y2, loss, d_gf, d_g2 = _loss_head(x3, target, norm_f_g, y2, g2)

    def relu2_bwd_epi(acc, rt):
        return (acc * (2.0 * rt.astype(F32)),)

    gw_ff2 = _matmul_tn_pair("ff2_dw", pos, act, d_y2, D_FF, D_MODEL, SEQ, D_FF // N_CHIPS,
                             tm=512, tn=1024, tk=1024)
    d_u = _matmul("ff2_dx", d_y2, w_ff2, "nt", SEQ, D_FF, D_MODEL, [BF16], epilogue=relu2_bwd_epi,
                  extras=[(relu_u, (TM, TN), lambda i, j, kk: (i, j))])[0]
    gw_ff1 = _matmul_tn_pair("ff1_dw", pos, h2, d_u, D_MODEL, D_FF, SEQ, D_MODEL,
                             tm=512, tn=1024, tk=1024)
    ffn = ["w_ff2", "w_ff1"]
    ffn_started = _ici_start("ici_start_ffn", ffn, [gw_ff2, gw_ff1])
    d_h2 = _matmul("ff1_dx", d_u, w_ff1, "nt", SEQ, D_MODEL, D_FF, [F32], tk=2048,
                   after=[ffn_started[3]])[0]
    d_x2, d_mix, d_sh2, d_sc2, d_n2g, d_g1 = _prenorm_bwd("prenorm2_bwd", d_h2, x2, norm2_g, sc2, d_x3,
                                                          branch=mix, gate=g1)
    gw_o = _matmul_tn_pair("mix_dw", pos, merged, d_mix, D_MODEL, D_MODEL, SEQ, D_MODEL // N_CHIPS,
                           tm=128, tn=1024, tk=2048)
    d_merged = _matmul("mix_dx", d_mix, w_o, "nt", SEQ, D_MODEL, D_MODEL, [BF16])[0]
    d_ret_out, d_att_out, d_gates = _merge_bwd(d_merged, gates, ret_out, att_out)

    gw_ret_out = _matmul_tn_pair("ret_out_dw", pos, gated, d_ret_out, RET_V_W, D_MODEL, SEQ,
                                 RET_V_W // N_CHIPS, tm=256, tn=1024, tk=1024)
    gw_att_out = _matmul_tn_pair("att_out_dw", pos, att_b, d_att_out, ATT_W, D_MODEL, SEQ, ATT_W,
                                 tm=256, tn=1024, tk=2048)
    mixer = ["w_o", "w_ret_out", "w_att_out"]
    mixer_started = _ici_start("ici_start_mixer", mixer, [gw_o, gw_ret_out, gw_att_out])
    d_gated = _matmul("ret_out_dx", d_ret_out, w_ret_out, "nt", SEQ, RET_V_W, D_MODEL, [BF16],
                      after=[mixer_started[3]])[0]
    d_att = _matmul("att_out_dx", d_att_out, w_att_out, "nt", SEQ, ATT_W, D_MODEL, [F32],
                    after=[mixer_started[3]])[0]

    d_ro, d_rg, d_gn_g, d_gn_b = _retpost_bwd(d_gated, ro, rg, gn_g, gn_b)
    d_rqkv = _retention_bwd(rqk, rv, states, d_ro, din, qd, kd, cd, cos, sin)

    dd = _att_bwd_pre(d_att, att)
    d_aqkv, dsbs = [], []
    for gi in range(3):
        dqkv, dsb = _att_bwd(gi, dils[gi], aqkv, d_att, lse, dd, bias)
        d_aqkv.append(dqkv)
        dsbs.append(dsb)
    d_rel_bias = _bias_grad(jnp.stack(dsbs), buckets)

    d_proj = [(d_rqkv, False), (d_rg, False)] + [(t, True) for t in d_aqkv] + [(d_gates, False)]
    gw_in = _matmul_tn_pair("proj_dw", pos, h1, d_proj, D_MODEL, IN_COLS, SEQ, D_MODEL,
                            tm=512, tn=ATT_W, tk=SEQ)
    sems, (gw_in,), (land,), token = _ici_start("ici_start_w_in", ["w_in"], [gw_in])
    d_h1 = _matmul("proj_dx", d_proj, w_in, "nt", SEQ, D_MODEL, IN_COLS, [F32], tn=1024, tk=ATT_W,
                   after=[token])[0]
    pending = (sems, land)

    names = ffn + mixer
    psums, got = _ici_wait("ici_wait_rest", names, list(ffn_started[0]) + list(mixer_started[0]),
                           list(ffn_started[1]) + list(mixer_started[1]),
                           list(ffn_started[2]) + list(mixer_started[2]), [d_h1])
    g_big = {n: _final_sum("final_" + n, pos, dict(BIG)[n], psums[i], got[i], SHARD[n])
             for i, n in enumerate(names)}
    grad_x, d_sh1, d_sc1, d_n1g = _prenorm_bwd("prenorm1_bwd", d_h1, x, norm1_g, sc1, d_x2,
                                               after=list(g_big.values()))
    d_mod = jnp.concatenate([d_sh1, d_sc1, d_g1, d_sh2, d_sc2, d_g2], axis=-1)
    small = dict(norm1_g=d_n1g, norm2_g=d_n2g, norm_f_g=d_gf, gn_g=d_gn_g, gn_b=d_gn_b,
                 rel_bias=d_rel_bias)
    return loss, grad_x, d_mod, small, g_big, (gw_in,) + pending


def _me():
    return lax.axis_index("x"), lax.axis_index("y"), lax.axis_index("c")


def _peer(x, y, c, mask):
    return (x ^ ((mask >> 2) & 1), y ^ ((mask >> 1) & 1), c ^ (mask & 1))


def _gather8(src_ref, dst_ref, send_sems, recv_sems):
    x, y, c = _me()
    me = 4 * x + 2 * y + c
    copies = []
    for mask in range(1, N_DEV):
        cp = pltpu.make_async_remote_copy(
            src_ref=src_ref, dst_ref=dst_ref.at[me], send_sem=send_sems.at[mask - 1],
            recv_sem=recv_sems.at[mask - 1], device_id=_peer(x, y, c, mask), device_id_type=MESH)
        cp.start()
        copies.append(cp)
    dst_ref[me] = src_ref[...]
    for cp in copies:
        cp.wait_recv()
    for cp in copies:
        cp.wait_send()


def _ada_fwd(c_in, w_ada, b_ada):
    ncol = ADA_COLS // N_CHIPS

    def body(c_ref, w_ref, b_ref, mod_ref, sc_ref, cbuf, cg, mbuf, mg, s1, r1, s2, r2):
        x, y, c = _me()
        me = 4 * x + 2 * y + c
        cv = c_ref[...]
        cbuf[...] = jnp.broadcast_to(cv * _sigmoid(cv), cbuf.shape)
        _gather8(cbuf, cg, s1, r1)
        rows = lax.broadcasted_iota(I32, (N_DEV, D_MODEL), 0)
        sc_all = jnp.zeros((N_DEV, D_MODEL), F32)
        for d in range(N_DEV):
            sc_all = jnp.where(rows == d, cg[d], sc_all)
        sc_ref[...] = sc_all
        mbuf[...] = jnp.dot(sc_all.astype(BF16), w_ref[...].astype(BF16), preferred_element_type=F32)
        _gather8(mbuf, mg, s2, r2)
        rowsel = lax.broadcasted_iota(I32, (N_DEV, ncol), 0) == me
        for k in range(N_CHIPS):
            blk = mg[2 * k]
            row = jnp.sum(jnp.where(rowsel, blk, 0.0), axis=0, keepdims=True)
            mod_ref[:, k * ncol:(k + 1) * ncol] = row + b_ref[:, k * ncol:(k + 1) * ncol]

    vm = pl.BlockSpec(memory_space=pltpu.VMEM)
    return pl.pallas_call(
        body, name="ada_fwd",
        in_specs=[vm, vm, vm], out_specs=[vm, vm],
        out_shape=[jax.ShapeDtypeStruct((1, ADA_COLS), F32), jax.ShapeDtypeStruct((N_DEV, D_MODEL), F32)],
        scratch_shapes=[
            pltpu.VMEM((8, D_MODEL), F32), pltpu.VMEM((N_DEV, 8, D_MODEL), F32),
            pltpu.VMEM((8, ncol), F32), pltpu.VMEM((N_DEV, 8, ncol), F32),
            pltpu.SemaphoreType.DMA((N_DEV - 1,)), pltpu.SemaphoreType.DMA((N_DEV - 1,)),
            pltpu.SemaphoreType.DMA((N_DEV - 1,)), pltpu.SemaphoreType.DMA((N_DEV - 1,)),
        ],
        compiler_params=pltpu.CompilerParams(vmem_limit_bytes=VMEM_LIMIT_V7X),
    )(c_in, w_ada, b_ada)


def _small_reduce(pack, sc_all):
    ncol = ADA_COLS // N_CHIPS

    def body(p_ref, sc_ref, tot_ref, gw_ref, pg, s1, r1):
        x, y, _ = _me()
        chip = 2 * x + y
        _gather8(p_ref, pg, s1, r1)
        tot = pg[0]
        for d in range(1, N_DEV):
            tot = tot + pg[d]
        tot_ref[...] = tot
        rows = lax.broadcasted_iota(I32, (N_DEV, ncol), 0)
        dmod = jnp.zeros((N_DEV, ncol), F32)
        for k in range(N_CHIPS):
            part = jnp.zeros((N_DEV, ncol), F32)
            for d in range(N_DEV):
                part = jnp.where(rows == d, pg[d, :, k * ncol:(k + 1) * ncol][0:1, :], part)
            dmod = jnp.where(chip == k, part, dmod)
        gw_ref[...] = lax.dot_general(sc_ref[...].astype(BF16), dmod.astype(BF16), _TN,
                                      preferred_element_type=F32)

    vm = pl.BlockSpec(memory_space=pltpu.VMEM)
    return pl.pallas_call(
        body, name="small_reduce",
        in_specs=[vm, vm], out_specs=[vm, vm],
        out_shape=[jax.ShapeDtypeStruct((8, ADA_COLS), F32), jax.ShapeDtypeStruct((D_MODEL, ncol), F32)],
        scratch_shapes=[pltpu.VMEM((N_DEV, 8, ADA_COLS), F32),
                        pltpu.SemaphoreType.DMA((N_DEV - 1,)), pltpu.SemaphoreType.DMA((N_DEV - 1,))],
        compiler_params=pltpu.CompilerParams(vmem_limit_bytes=VMEM_LIMIT_V7X),
    )(pack, sc_all)


BIG = (("w_in", 1), ("w_ret_out", 0), ("w_att_out", 1), ("w_o", 0), ("w_ff1", 1), ("w_ff2", 0))
SHARD = {"w_in": (D_MODEL, IN_COLS // N_CHIPS), "w_ret_out": (RET_V_W // N_CHIPS, D_MODEL),
         "w_att_out": (ATT_W, D_MODEL // N_CHIPS), "w_o": (D_MODEL // N_CHIPS, D_MODEL),
         "w_ff1": (D_MODEL, D_FF // N_CHIPS), "w_ff2": (D_FF // N_CHIPS, D_MODEL)}
_CHIP_FLIPS = ((1, 0), (0, 1), (1, 1))


def _region(ref, axis, chip, half, shard_shape):
    r, cw = shard_shape
    hr = r // 2
    if axis == 1:
        return ref.at[pl.ds(half * hr, hr), pl.ds(chip * cw, cw)]
    return ref.at[pl.ds(chip * r + half * hr, hr), :]


def _gather_weights(shards, n_remote):
    nw = len(BIG)
    shapes = [s.shape for s in shards]
    full_shapes = [(r, N_CHIPS * cw) if ax == 1 else (N_CHIPS * r, cw)
                   for (r, cw), (_, ax) in zip(shapes, BIG)]

    def body(*refs):
        ins, outs = refs[:nw], refs[nw:2 * nw]
        own = refs[2 * nw:3 * nw]
        from_ici, from_sib = refs[3 * nw:3 * nw + n_remote], refs[3 * nw + n_remote:3 * nw + 2 * n_remote]
        ld_sem, st_sem, s_ici, r_ici, s_d2d, r_d2d, st_a, st_b = refs[3 * nw + 2 * n_remote:]
        x, y, c = _me()
        chip = 2 * x + y
        sib = (x, y, 1 - c)
        loads = [pltpu.make_async_copy(ins[i], own[i], ld_sem.at[i]) for i in range(nw)]
        for cp in loads:
            cp.start()
        pending, first = [], []
        for i, (_, ax) in enumerate(BIG):
            r, cw = shapes[i]
            hr = r // 2
            loads[i].wait()
            dst = outs[i].at[:, pl.ds(chip * cw, cw)] if ax == 1 else outs[i].at[pl.ds(chip * r, r), :]
            cp = pltpu.make_async_copy(own[i], dst, st_sem.at[i])
            cp.start()
            pending.append(cp)
            for j, (fx, fy) in enumerate(_CHIP_FLIPS if i < n_remote else ()):
                rc = pltpu.make_async_remote_copy(
                    src_ref=own[i].at[pl.ds(c * hr, hr), :], dst_ref=from_ici[i].at[j],
                    send_sem=s_ici.at[j * nw + i], recv_sem=r_ici.at[j * nw + i],
                    device_id=(x ^ fx, y ^ fy, c), device_id_type=MESH)
                rc.start()
                first.append((j, i, rc))
        passed = []
        for j, i, rc in first:
            fx, fy = _CHIP_FLIPS[j]
            src_chip = 2 * (x ^ fx) + (y ^ fy)
            ax = BIG[i][1]
            rc.wait_recv()
            fw = pltpu.make_async_remote_copy(
                src_ref=from_ici[i].at[j], dst_ref=from_sib[i].at[j], send_sem=s_d2d.at[j * nw + i],
                recv_sem=r_d2d.at[j * nw + i], device_id=sib, device_id_type=MESH)
            fw.start()
            passed.append((j, i, src_chip, fw))
            st = pltpu.make_async_copy(from_ici[i].at[j], _region(outs[i], ax, src_chip, c, shapes[i]),
                                       st_a.at[j * nw + i])
            st.start()
            pending.append(st)
        for j, i, src_chip, fw in passed:
            fw.wait_recv()
            st = pltpu.make_async_copy(from_sib[i].at[j],
                                       _region(outs[i], BIG[i][1], src_chip, 1 - c, shapes[i]),
                                       st_b.at[j * nw + i])
            st.start()
            pending.append(st)
        for _, _, rc in first:
            rc.wait_send()
        for _, _, _, fw in passed:
            fw.wait_send()
        for cp in pending:
            cp.wait()

    hbm = pl.BlockSpec(memory_space=pl.ANY)
    halves = [pltpu.VMEM((3, r // 2, cw), BF16) for r, cw in shapes[:n_remote]]
    return pl.pallas_call(
        body, name="gather_weights",
        in_specs=[hbm] * nw, out_specs=[hbm] * nw,
        out_shape=[jax.ShapeDtypeStruct(fs, BF16) for fs in full_shapes],
        scratch_shapes=[pltpu.VMEM(sh, BF16) for sh in shapes] + halves + halves
        + [pltpu.SemaphoreType.DMA((nw,)), pltpu.SemaphoreType.DMA((nw,))]
        + [pltpu.SemaphoreType.DMA((3 * nw,))] * 6,
        compiler_params=pltpu.CompilerParams(vmem_limit_bytes=VMEM_LIMIT_V7X),
    )(*shards)


REST = BIG[1:]
_SIDE_EFFECTS = pltpu.CompilerParams(has_side_effects=pltpu.SideEffectType.DATAFLOW_SIDE_EFFECTING)
_ANY_SPEC = pl.BlockSpec(memory_space=pl.ANY)


def _rest_ici_copies(shard_refs, full_refs, sems):
    x, y, c = _me()
    chip = 2 * x + y
    n = 3 * len(REST)
    copies = []
    for i, (name, ax) in enumerate(REST):
        hr = SHARD[name][0] // 2
        for j, (fx, fy) in enumerate(_CHIP_FLIPS):
            copies.append(pltpu.make_async_remote_copy(
                src_ref=shard_refs[i].at[pl.ds(c * hr, hr), :],
                dst_ref=_region(full_refs[i], ax, chip, c, SHARD[name]),
                send_sem=sems[3 * i + j], recv_sem=sems[n + 3 * i + j],
                device_id=(x ^ fx, y ^ fy, c), device_id_type=MESH))
    return copies


def _rest_d2d_copies(full_refs, sems):
    x, y, c = _me()
    n = 3 * len(REST)
    copies = []
    for i, (name, ax) in enumerate(REST):
        for j, (fx, fy) in enumerate(_CHIP_FLIPS):
            reg = _region(full_refs[i], ax, 2 * (x ^ fx) + (y ^ fy), c, SHARD[name])
            copies.append(pltpu.make_async_remote_copy(
                src_ref=reg, dst_ref=reg, send_sem=sems[3 * i + j], recv_sem=sems[n + 3 * i + j],
                device_id=(x, y, 1 - c), device_id_type=MESH))
    return copies


def _gather_rest_start(shards, fulls, after):
    nr, ns, na = len(REST), 6 * len(REST), len(after)

    def body(*refs):
        for cp in _rest_ici_copies(refs[:nr], refs[nr:2 * nr], refs[2 * nr + na:2 * nr + na + ns]):
            cp.start()
        token = refs[-1]
        token[...] = jnp.zeros_like(token)

    hbm = lambda a: pltpu.HBM(a.shape, a.dtype)
    res = pl.pallas_call(
        body, name="gather_rest_start",
        out_shape=(pltpu.SemaphoreType.DMA(()),) * ns + tuple(hbm(a) for a in shards + fulls)
        + (jax.ShapeDtypeStruct((8, 128), F32),),
        in_specs=(_HBM_SPEC,) * (2 * nr) + (_ANY_SPEC,) * na,
        out_specs=(_SEM_SPEC,) * ns + (_HBM_SPEC,) * (2 * nr) + (pl.BlockSpec(memory_space=pltpu.VMEM),),
        input_output_aliases={k: ns + k for k in range(2 * nr)}, compiler_params=_SIDE_EFFECTS,
    )(*[pltpu.with_memory_space_constraint(a, pltpu.HBM) for a in shards + fulls], *after)
    return res[:ns], res[ns:ns + nr], res[ns + nr:ns + 2 * nr], res[-1]


def _gather_rest_forward(sems, shards, fulls, after):
    nr, ns = len(REST), 6 * len(REST)

    def body(*refs):
        shard_refs, full_refs, old = refs[:nr], refs[nr:2 * nr], refs[2 * nr:2 * nr + ns]
        new = refs[2 * nr + ns + len(after):2 * nr + 2 * ns + len(after)]
        for cp in _rest_ici_copies(shard_refs, full_refs, old):
            cp.wait_send()
            cp.wait_recv()
        for cp in _rest_d2d_copies(full_refs, new):
            cp.start()
        token = refs[-1]
        token[...] = jnp.zeros_like(token)

    res = pl.pallas_call(
        body, name="gather_rest_forward",
        out_shape=(pltpu.SemaphoreType.DMA(()),) * ns + tuple(pltpu.HBM(a.shape, a.dtype) for a in fulls)
        + (jax.ShapeDtypeStruct((8, 128), F32),),
        in_specs=(_HBM_SPEC,) * (2 * nr) + (_SEM_SPEC,) * ns + (_ANY_SPEC,) * len(after),
        out_specs=(_SEM_SPEC,) * ns + (_HBM_SPEC,) * nr + (pl.BlockSpec(memory_space=pltpu.VMEM),),
        input_output_aliases={nr + k: ns + k for k in range(nr)}, compiler_params=_SIDE_EFFECTS,
    )(*shards, *fulls, *sems, *after)
    return res[:ns], res[ns:ns + nr], res[-1]


def _gather_rest_end(sems, fulls, after):
    nr, ns = len(REST), 6 * len(REST)

    def body(*refs):
        for cp in _rest_d2d_copies(refs[:nr], refs[nr:nr + ns]):
            cp.wait_send()
            cp.wait_recv()

    return pl.pallas_call(
        body, name="gather_rest_end",
        out_shape=tuple(pltpu.HBM(a.shape, a.dtype) for a in fulls),
        in_specs=(_HBM_SPEC,) * nr + (_SEM_SPEC,) * ns + (_ANY_SPEC,) * len(after),
        out_specs=(_HBM_SPEC,) * nr,
        input_output_aliases={k: k for k in range(nr)}, compiler_params=_SIDE_EFFECTS,
    )(*fulls, *sems, *after)


def _adam_update(w, g, m, v):
    mn = ADAM_B1 * m + (1.0 - ADAM_B1) * g
    vn = ADAM_B2 * v + (1.0 - ADAM_B2) * (g * g)
    m_hat = mn / (1.0 - ADAM_B1 ** ADAM_STEP)
    v_hat = vn / (1.0 - ADAM_B2 ** ADAM_STEP)
    return -ADAM_LR * (m_hat / (jnp.sqrt(v_hat) + ADAM_EPS) + ADAM_WD * w), mn, vn


def _final_sum(name, pos, axis, psum, recv, shard_shape, after=(), tr=128):
    r, cw = shard_shape
    hr = r // 2
    tr = min(tr, hr)
    nt = hr // tr
    n_after = len(after)

    def kern(pos_ref, p_ref, r_ref, *rest):
        g_ref, send_buf, land_buf, s_sem, r_sem = rest[n_after:]
        p, t = pl.program_id(0), pl.program_id(1)
        sib = _sibling()

        def copy(i):
            return pltpu.make_async_remote_copy(
                src_ref=send_buf.at[i], dst_ref=land_buf.at[i], send_sem=s_sem.at[i],
                recv_sem=r_sem.at[i], device_id=sib, device_id_type=MESH)

        @pl.when(p == 0)
        def _():
            tot = p_ref[...].astype(F32)
            for j in range(3):
                tot = tot + r_ref[j].astype(F32)
            send_buf[t] = tot
            copy(t).start()
            g_ref[...] = tot

        @pl.when(p == 1)
        def _():
            copy(t).wait_recv()
            g_ref[...] = land_buf[t]

        @pl.when(jnp.logical_and(p == 1, t == nt - 1))
        def _():
            for i in range(nt):
                copy(i).wait_send()

    def shard_rows(p, t, pos_ref):
        return (jnp.where(p == 0, pos_ref[0], 1 - pos_ref[0]) * nt + t, 0)

    def own_part(p, t, pos_ref):
        tt = jnp.where(p == 0, t, nt - 1)
        return (tt, pos_ref[1]) if axis == 1 else (pos_ref[1] * nt + tt, 0)

    grid_spec = pltpu.PrefetchScalarGridSpec(
        num_scalar_prefetch=1, grid=(2, nt),
        in_specs=[pl.BlockSpec((tr, cw), own_part),
                  pl.BlockSpec((3, tr, cw), lambda p, t, pos_ref: (0, jnp.where(p == 0, t, nt - 1), 0))]
        + [pl.BlockSpec(memory_space=pl.ANY)] * n_after,
        out_specs=pl.BlockSpec((tr, cw), shard_rows),
        scratch_shapes=[pltpu.VMEM((nt, tr, cw), F32), pltpu.VMEM((nt, tr, cw), F32),
                        pltpu.SemaphoreType.DMA((nt,)), pltpu.SemaphoreType.DMA((nt,))])
    return pl.pallas_call(
        kern, name=name, grid_spec=grid_spec, out_shape=jax.ShapeDtypeStruct((r, cw), F32),
        compiler_params=_cparams(("arbitrary", "arbitrary")),
    )(pos, psum, recv, *after)


def _adamw(name, w, g, m, v):
    r, cw = w.shape
    tr = min(r, 128)

    def kern(w_ref, g_ref, m_ref, v_ref, go_ref, d_ref, nm_ref, nv_ref):
        gv = g_ref[...]
        go_ref[...] = gv
        d_ref[...], nm_ref[...], nv_ref[...] = _adam_update(w_ref[...], gv, m_ref[...], v_ref[...])

    spec = pl.BlockSpec((tr, cw), lambda i: (i, 0))
    return pl.pallas_call(
        kern, name=name, grid=(r // tr,), in_specs=[spec] * 4, out_specs=[spec] * 4,
        out_shape=[jax.ShapeDtypeStruct((r, cw), F32)] * 4, compiler_params=_cparams(("parallel",)),
    )(w, g, m, v)


_PACK_W = ADA_COLS
_NB = REL_BUCKETS * N_ATT_HEADS
_SMALL_SLOTS = {
    "b_ada": (0, 0, ADA_COLS),
    "norm1_g": (1, 0, D_MODEL), "norm2_g": (1, D_MODEL, D_MODEL), "norm_f_g": (1, 2 * D_MODEL, D_MODEL),
    "ret_gn_g": (1, 3 * D_MODEL, RET_V_W),
    "ret_gn_b": (2, 0, RET_V_W), "rel_bias": (2, RET_V_W, _NB), "loss": (2, RET_V_W + 512, 128),
}


def _pack_small(vals):
    rows = []
    for r in range(8):
        items = sorted([(off, n) for n, (rr, off, _) in _SMALL_SLOTS.items() if rr == r and n in vals])
        parts, pos = [], 0
        for off, n in items:
            if off > pos:
                parts.append(jnp.zeros((1, off - pos), F32))
            parts.append(vals[n].reshape(1, -1).astype(F32))
            pos = off + _SMALL_SLOTS[n][2]
        if pos < _PACK_W:
            parts.append(jnp.zeros((1, _PACK_W - pos), F32))
        rows.append(jnp.concatenate(parts, axis=-1))
    return jnp.concatenate(rows, axis=0)


def _unpack_small(pack, name):
    r, off, wd = _SMALL_SLOTS[name]
    return pack[r:r + 1, off:off + wd]


def kernel(x, c, w_ada, b_ada, norm1_g, w_in, rel_bias, ret_gn_g, ret_gn_b, w_ret_out, w_att_out, w_o, norm2_g, w_ff1, w_ff2, norm_f_g, loss_target, m_w_ada, m_b_ada, m_norm1_g, m_w_in, m_rel_bias, m_ret_gn_g, m_ret_gn_b, m_w_ret_out, m_w_att_out, m_w_o, m_norm2_g, m_w_ff1, m_w_ff2, m_norm_f_g, v_w_ada, v_b_ada, v_norm1_g, v_w_in, v_rel_bias, v_ret_gn_g, v_ret_gn_b, v_w_ret_out, v_w_att_out, v_w_o, v_norm2_g, v_w_ff1, v_w_ff2, v_norm_f_g):
    given = dict(locals())
    big_names = [n for n, _ in BIG]
    shard_w = {n: given[n][0] for n in big_names}
    assert all(shard_w[n].shape == SHARD[n] for n in big_names)

    shards_bf = [shard_w[n].astype(BF16) for n in big_names]
    full = _gather_weights(shards_bf, 1)
    mod, sc_all = _ada_fwd(c, w_ada[0], b_ada)
    rest_gather = _gather_rest_start(shards_bf[1:], list(full[1:]), [mod])
    pos = _where_am_i()

    loss, grad_x, d_mod, small, g_big, pending = _local_step(
        pos, x[0], loss_target[0], mod, norm1_g, norm2_g, norm_f_g.reshape(1, -1), rel_bias, ret_gn_g,
        ret_gn_b, full[0], rest_gather)

    pack_g = _pack_small(dict(b_ada=d_mod, norm1_g=small["norm1_g"], norm2_g=small["norm2_g"],
                              norm_f_g=small["norm_f_g"], ret_gn_g=small["gn_g"], ret_gn_b=small["gn_b"],
                              rel_bias=small["rel_bias"], loss=loss))
    tot, g_w_ada = _small_reduce(pack_g, sc_all)

    small_names = ["b_ada", "norm1_g", "rel_bias", "ret_gn_g", "ret_gn_b", "norm2_g", "norm_f_g"]
    pack_w = _pack_small({n: given[n] for n in small_names})
    pack_m = _pack_small({n: given["m_" + n] for n in small_names})
    pack_v = _pack_small({n: given["v_" + n] for n in small_names})
    _, sd, sm, sv = _adamw("adamw_small", pack_w, tot, pack_m, pack_v)

    grads, deltas, new_m, new_v = {}, {}, {}, {}
    for n in small_names:
        shp = given[n].shape
        grads[n] = _unpack_small(tot, n).reshape(shp)
        deltas[n] = _unpack_small(sd, n).reshape(shp)
        new_m[n] = _unpack_small(sm, n).reshape(shp)
        new_v[n] = _unpack_small(sv, n).reshape(shp)
    g_big["w_ada"] = g_w_ada
    for n in ["w_ada"] + big_names[1:] + big_names[:1]:
        if n == "w_in":
            gw_in, sems, land = pending
            done = [tot, sd] + [deltas[k] for k in ["w_ada"] + big_names[1:]]
            (gw_in,), (got,) = _ici_wait("ici_wait_w_in", [n], sems, [gw_in], [land], done)
            g_big[n] = _final_sum("final_w_in", pos, 1, gw_in, got, SHARD[n])
        g, d, nm, nv = _adamw("adamw_" + n, given[n][0], g_big[n], given["m_" + n][0], given["v_" + n][0])
        grads[n], deltas[n], new_m[n], new_v[n] = g[None], d[None], nm[None], nv[None]

    order = ["w_ada", "b_ada", "norm1_g", "w_in", "rel_bias", "ret_gn_g", "ret_gn_b", "w_ret_out",
             "w_att_out", "w_o", "norm2_g", "w_ff1", "w_ff2", "norm_f_g"]
    loss_out = _unpack_small(tot, "loss")[0, 0]
    return (loss_out, grad_x[None], *[grads[n] for n in order], *[deltas[n] for n in order],
            *[new_m[n] for n in order], *[new_v[n] for n in order])
```

```python
import functools
import math

import jax
import jax.numpy as jnp
import numpy as np
from jax import lax
from jax.experimental import pallas as pl
from jax.experimental.pallas import tpu as pltpu

F32 = jnp.float32
BF16 = jnp.bfloat16
I32 = jnp.int32

SEQ = 2048
D_MODEL = 1024
RET_HEADS = 4
RET_DK = 256
RET_DV = 512
RET_CHUNK = 128
RET_SUB = 2
RET_QK_W = RET_HEADS * RET_DK
RET_V_W = RET_HEADS * RET_DV
ATT_GROUPS = ((128, 1), (512, 4), (2048, 16))
ATT_HPG = 4
ATT_DH = 128
ATT_W = ATT_HPG * ATT_DH
ATT_BLK = 128
N_BLK = SEQ // ATT_BLK
REL_BUCKETS = 32
REL_MAX_DIST = 2048
N_ATT_HEADS = 12
D_FF = 4 * D_MODEL
RMS_EPS = 1e-6
GN_EPS = 1e-5
ROPE_BASE = 10000.0
IN_COLS = 2 * RET_QK_W + 2 * RET_V_W + 9 * ATT_W + 2 * D_MODEL
OFF_Q, OFF_K, OFF_V, OFF_G = 0, RET_QK_W, 2 * RET_QK_W, 2 * RET_QK_W + RET_V_W
OFF_ATT = 2 * RET_QK_W + 2 * RET_V_W
OFF_GATE = OFF_ATT + 9 * ATT_W
N_CHIPS = 4
N_DEV = 8
ADA_COLS = 6 * D_MODEL

ADAM_LR = 0.001
ADAM_B1 = 0.9
ADAM_B2 = 0.999
ADAM_EPS = 1e-08
ADAM_WD = 0.01
ADAM_STEP = 10

VMEM_LIMIT_V7X = 56 * 1024 * 1024
MESH = pl.DeviceIdType.MESH


def _cparams(sem):
    return pltpu.CompilerParams(dimension_semantics=sem, vmem_limit_bytes=VMEM_LIMIT_V7X)


def _sigmoid(v):
    return 1.0 / (1.0 + jnp.exp(-v))


def _rowmap(name, body, row_ins, bcast_ins, row_outs, sum_outs=(), tm=256, after=()):
    m = row_ins[0].shape[0]
    n_in = len(row_ins) + len(bcast_ins)
    n_ro = len(row_outs)

    def kern(*refs):
        vals = [r[...] for r in refs[:n_in]]
        res = body(*vals)
        if not isinstance(res, (tuple, list)):
            res = (res,)
        outs = refs[n_in + len(after):]
        for r, v in zip(outs[:n_ro], res[:n_ro]):
            r[...] = v.astype(r.dtype)
        if sum_outs:
            @pl.when(pl.program_id(0) == 0)
            def _():
                for r in outs[n_ro:]:
                    r[...] = jnp.zeros_like(r)
            for r, v in zip(outs[n_ro:], res[n_ro:]):
                r[...] += v

    in_specs = [pl.BlockSpec((tm, a.shape[1]), lambda i: (i, 0)) for a in row_ins]
    in_specs += [pl.BlockSpec(a.shape, lambda i: (0, 0)) for a in bcast_ins]
    in_specs += [pl.BlockSpec(memory_space=pl.ANY)] * len(after)
    out_specs = [pl.BlockSpec((tm, n), lambda i: (i, 0)) for n, _ in row_outs]
    out_specs += [pl.BlockSpec((1, n), lambda i: (0, 0)) for n in sum_outs]
    out_shape = [jax.ShapeDtypeStruct((m, n), dt) for n, dt in row_outs]
    out_shape += [jax.ShapeDtypeStruct((1, n), F32) for n in sum_outs]
    return pl.pallas_call(
        kern, name=name, grid=(m // tm,), in_specs=in_specs, out_specs=out_specs,
        out_shape=out_shape, compiler_params=_cparams(("arbitrary",)),
    )(*row_ins, *bcast_ins, *after)


TM, TN = 1024, 1024


def _piece_chunks(piece, width):
    arr, stacked = piece
    return arr.shape[0] if stacked else arr.shape[1] // width


def _piece_spec(piece, rows, width, start, row_of, chunk_of):
    arr, stacked = piece
    last = _piece_chunks(piece, width) - 1

    def local(*ids):
        return jnp.clip(chunk_of(*ids) - start, 0, last)

    def row(*ids):
        rel = chunk_of(*ids) - start
        return jnp.where(jnp.logical_and(rel >= 0, rel <= last), row_of(*ids), 0)

    if stacked:
        return pl.BlockSpec((None, rows, width), lambda *ids: (local(*ids), row(*ids), 0))
    return pl.BlockSpec((rows, width), lambda *ids: (row(*ids), local(*ids)))


def _piece_starts(pieces, width):
    return [sum(_piece_chunks(p, width) for p in pieces[:q]) for q in range(len(pieces))]


def _matmul(name, a, b, kind, m, n, k, outs, *, b_off=0, tm=TM, tn=TN, tk=1024,
            epilogue=None, extras=(), after=()):
    tm, tn, tk = min(tm, m), min(tn, n), min(tk, k)
    nk = k // tk
    pieces = a if isinstance(a, list) else [(a, False)]
    starts = _piece_starts(pieces, tk)
    if kind == "nn":
        a_specs = [pl.BlockSpec((tm, tk), lambda i, j, kk: (i, kk))]
        b_spec = pl.BlockSpec((tk, tn), lambda i, j, kk: (kk, b_off // tn + j))
        dn = (((1,), (0,)), ((), ()))
    elif kind == "nt":
        a_specs = [_piece_spec(p, tm, tk, st, lambda i, j, kk: i, lambda i, j, kk: kk)
                   for p, st in zip(pieces, starts)]
        b_spec = pl.BlockSpec((tn, tk), lambda i, j, kk: (j, b_off // tk + kk))
        dn = (((1,), (1,)), ((), ()))
    else:
        a_specs = [pl.BlockSpec((tk, tm), lambda i, j, kk: (kk, i))]
        b_spec = pl.BlockSpec((tk, tn), lambda i, j, kk: (kk, j))
        dn = (((0,), (0,)), ((), ()))
    n_a, n_ex, n_out = len(pieces), len(extras), len(outs)
    if epilogue is None:
        epilogue = lambda acc: (acc,)

    def finish(acc, ex_refs, out_refs):
        res = epilogue(acc, *[r[...] for r in ex_refs])
        for r, v in zip(out_refs, res):
            r[...] = v.astype(r.dtype)

    n_in = n_a + 1 + n_ex + len(after)

    def kern(*refs):
        a_refs, b_ref = refs[:n_a], refs[n_a]
        ex_refs = refs[n_a + 1:n_a + 1 + n_ex]
        out_refs = refs[n_in:n_in + n_out]
        kk = pl.program_id(2)
        dot = lambda a_ref: lax.dot_general(a_ref[...], b_ref[...], dn, preferred_element_type=F32)
        if nk == 1:
            finish(dot(a_refs[0]), ex_refs, out_refs)
            return
        acc_ref = refs[n_in + n_out]
        if n_a == 1:
            part = dot(a_refs[0])

            @pl.when(kk == 0)
            def _():
                acc_ref[...] = part

            @pl.when(kk > 0)
            def _():
                acc_ref[...] += part
        else:
            @pl.when(kk == 0)
            def _():
                acc_ref[...] = jnp.zeros_like(acc_ref)

            for q in range(n_a):
                @pl.when(jnp.logical_and(kk >= starts[q], kk < starts[q] + _piece_chunks(pieces[q], tk)))
                def _(q=q):
                    acc_ref[...] += dot(a_refs[q])

        @pl.when(kk == nk - 1)
        def _():
            finish(acc_ref[...], ex_refs, out_refs)

    in_specs = a_specs + [b_spec] + [pl.BlockSpec(bs, im) for _, bs, im in extras]
    in_specs += [pl.BlockSpec(memory_space=pl.ANY)] * len(after)
    return pl.pallas_call(
        kern, name=name, grid=(m // tm, n // tn, nk), in_specs=in_specs,
        out_specs=[pl.BlockSpec((tm, tn), lambda i, j, kk: (i, j)) for _ in outs],
        out_shape=[jax.ShapeDtypeStruct((m, n), dt) for dt in outs],
        scratch_shapes=[] if nk == 1 else [pltpu.VMEM((tm, tn), F32)],
        compiler_params=_cparams(("parallel", "parallel", "arbitrary")),
    )(*[p[0] for p in pieces], b, *[e[0] for e in extras], *after)


def _ici_copies(psum_ref, recv_ref, s_sem, r_sem, axis, shard_shape):
    x, y, c = _me()
    hr, cw = shard_shape[0] // 2, shard_shape[1]
    pick = lambda sems, j: sems[j] if isinstance(sems, (list, tuple)) else sems.at[j]
    copies = []
    for j, (fx, fy) in enumerate(_CHIP_FLIPS):
        chip = 2 * (x ^ fx) + (y ^ fy)
        src = psum_ref.at[:, pl.ds(chip * cw, cw)] if axis == 1 else psum_ref.at[pl.ds(chip * hr, hr), :]
        copies.append(pltpu.make_async_remote_copy(
            src_ref=src, dst_ref=recv_ref.at[j], send_sem=pick(s_sem, j), recv_sem=pick(r_sem, j),
            device_id=(x ^ fx, y ^ fy, c), device_id_type=MESH))
    return copies


_HBM_SPEC = pl.BlockSpec(memory_space=pltpu.HBM)
_SEM_SPEC = pl.BlockSpec(memory_space=pltpu.SEMAPHORE)


def _split_ici_copies(names, p_refs, land_refs, sems):
    copies = []
    for i, n in enumerate(names):
        copies += _ici_copies(p_refs[i], land_refs[i], list(sems[6 * i:6 * i + 3]),
                              list(sems[6 * i + 3:6 * i + 6]), dict(BIG)[n], SHARD[n])
    return copies


def _ici_start(name, names, psums):
    nw, ns = len(names), 6 * len(names)
    lands = [lax.empty((3, SHARD[n][0] // 2, SHARD[n][1]), BF16) for n in names]

    def body(*refs):
        for cp in _split_ici_copies(names, refs[:nw], refs[nw:2 * nw], refs[2 * nw:2 * nw + ns]):
            cp.start()
        token = refs[-1]
        token[...] = jnp.zeros_like(token)

    res = pl.pallas_call(
        body, name=name,
        out_shape=(pltpu.SemaphoreType.DMA(()),) * ns
        + tuple(pltpu.HBM(a.shape, BF16) for a in list(psums) + lands)
        + (jax.ShapeDtypeStruct((8, 128), F32),),
        in_specs=(_HBM_SPEC,) * (2 * nw),
        out_specs=(_SEM_SPEC,) * ns + (_HBM_SPEC,) * (2 * nw) + (pl.BlockSpec(memory_space=pltpu.VMEM),),
        input_output_aliases={k: ns + k for k in range(2 * nw)},
        compiler_params=pltpu.CompilerParams(has_side_effects=pltpu.SideEffectType.DATAFLOW_SIDE_EFFECTING),
    )(*[pltpu.with_memory_space_constraint(a, pltpu.HBM) for a in list(psums) + lands])
    return res[:ns], res[ns:ns + nw], res[ns + nw:ns + 2 * nw], res[-1]


def _ici_wait(name, names, sems, p_thru, land_thru, after):
    nw, ns = len(names), 6 * len(names)

    def body(*refs):
        for cp in _split_ici_copies(names, refs[:nw], refs[nw:2 * nw], refs[2 * nw:2 * nw + ns]):
            cp.wait_send()
            cp.wait_recv()

    res = pl.pallas_call(
        body, name=name,
        out_shape=tuple(pltpu.HBM(a.shape, BF16) for a in list(p_thru) + list(land_thru)),
        in_specs=(_HBM_SPEC,) * (2 * nw) + (_SEM_SPEC,) * ns + (pl.BlockSpec(memory_space=pl.ANY),) * len(after),
        out_specs=(_HBM_SPEC,) * (2 * nw), input_output_aliases={k: k for k in range(2 * nw)},
        compiler_params=pltpu.CompilerParams(has_side_effects=pltpu.SideEffectType.DATAFLOW_SIDE_EFFECTING),
    )(*p_thru, *land_thru, *sems, *after)
    return res[:nw], res[nw:]


def _where_am_i():
    x, y, c = _me()
    return jnp.stack([c, 2 * x + y]).astype(I32)


def _sibling():
    x, y, c = _me()
    return (x, y, 1 - c)


N_SEND_SLOTS = 2


def _matmul_tn_pair(name, pos, a, b, m, n, k, shard_rows, *, tm, tn, tk):
    hr = shard_rows // 2
    tm, tn, tk = min(tm, hr), min(tn, n), min(tk, k)
    tph = hr // tm
    nt, nj, nk = (m // 2) // tm, n // tn, k // tk
    n_tiles = nt * nj

    def row_block(p, t, pos_ref):
        half = jnp.where(p == 0, 1 - pos_ref[0], pos_ref[0])
        return (t // tph) * (2 * tph) + half * tph + t % tph

    pieces = b if isinstance(b, list) else [(b, False)]
    starts = _piece_starts(pieces, tn)
    n_b = len(pieces)

    def kern(pos_ref, a_ref, *rest):
        b_refs = rest[:n_b]
        o_ref, acc_ref, send_buf, land_buf, s_sem, r_sem = rest[n_b:]
        p, t, j, kk = pl.program_id(0), pl.program_id(1), pl.program_id(2), pl.program_id(3)
        idx = t * nj + j
        sib = _sibling()

        def copy(i):
            return pltpu.make_async_remote_copy(
                src_ref=send_buf.at[i % N_SEND_SLOTS], dst_ref=land_buf.at[i], send_sem=s_sem.at[i],
                recv_sem=r_sem.at[i], device_id=sib, device_id_type=MESH)

        @pl.when(kk == 0)
        def _():
            acc_ref[...] = jnp.zeros_like(acc_ref)

        for q in range(n_b):
            @pl.when(jnp.logical_and(j >= starts[q], j < starts[q] + _piece_chunks(pieces[q], tn)))
            def _(q=q):
                acc_ref[...] += lax.dot_general(a_ref[...], b_refs[q][...], _TN, preferred_element_type=F32)

        @pl.when(jnp.logical_and(kk == nk - 1, p == 0))
        def _():
            @pl.when(idx >= N_SEND_SLOTS)
            def _():
                copy(idx - N_SEND_SLOTS).wait_send()

            send_buf[idx % N_SEND_SLOTS] = acc_ref[...].astype(BF16)
            copy(idx).start()

        @pl.when(jnp.logical_and(kk == nk - 1, p == 1))
        def _():
            copy(idx).wait_recv()
            o_ref[...] = (acc_ref[...] + land_buf[idx].astype(F32)).astype(BF16)

        @pl.when(jnp.logical_and(jnp.logical_and(p == 1, idx == n_tiles - 1), kk == nk - 1))
        def _():
            for i in range(max(n_tiles - N_SEND_SLOTS, 0), n_tiles):
                copy(i).wait_send()

    grid_spec = pltpu.PrefetchScalarGridSpec(
        num_scalar_prefetch=1, grid=(2, nt, nj, nk),
        in_specs=[pl.BlockSpec((tk, tm), lambda p, t, j, kk, pos_ref: (kk, row_block(p, t, pos_ref)))]
        + [_piece_spec(pc, tk, tn, st, lambda p, t, j, kk, pos_ref: kk, lambda p, t, j, kk, pos_ref: j)
           for pc, st in zip(pieces, starts)],
        out_specs=pl.BlockSpec((tm, tn), lambda p, t, j, kk, pos_ref: (p * t, p * j)),
        scratch_shapes=[pltpu.VMEM((tm, tn), F32), pltpu.VMEM((N_SEND_SLOTS, tm, tn), BF16),
                        pltpu.VMEM((n_tiles, tm, tn), BF16),
                        pltpu.SemaphoreType.DMA((n_tiles,)), pltpu.SemaphoreType.DMA((n_tiles,))])
    return pl.pallas_call(
        kern, name=name, grid_spec=grid_spec, out_shape=jax.ShapeDtypeStruct((m // 2, n), BF16),
        compiler_params=_cparams(("arbitrary",) * 4),
    )(pos, a, *[pc[0] for pc in pieces])


def _rope_tables():
    half = RET_DK // 2
    f32 = np.float32
    inv = np.power(f32(ROPE_BASE), -np.arange(half, dtype=f32) / f32(half)).astype(f32)
    ang = (np.arange(SEQ, dtype=f32)[:, None] * inv[None, :]).astype(f32)
    return jnp.asarray(np.cos(ang).astype(f32)), jnp.asarray(np.sin(ang).astype(f32))


def _decay_tables():
    c = RET_CHUNK
    f32 = np.float32
    log_g = np.log1p(-np.power(f32(2.0), f32(-5.0) - np.arange(RET_HEADS, dtype=f32))).astype(f32)
    idx = np.arange(c, dtype=f32)
    rel = idx[:, None] - idx[None, :]
    din = np.where(rel >= 0, np.exp(log_g[:, None, None] * np.maximum(rel, f32(0.0))), f32(0.0)).astype(f32)
    qd = np.exp(log_g[:, None] * (idx + f32(1.0))).astype(f32)[:, :, None]
    kd = np.exp(log_g[:, None] * (f32(c) - f32(1.0) - idx)).astype(f32)[:, :, None]
    cd = np.exp(log_g * f32(c)).astype(f32)
    return jnp.asarray(din), jnp.asarray(qd), jnp.asarray(kd), jnp.asarray(cd)


def _t5_bucket(dist):
    max_exact = REL_BUCKETS // 2
    d_f = jnp.maximum(dist, 1).astype(F32)
    large = max_exact + (jnp.log(d_f / max_exact) / math.log(REL_MAX_DIST / max_exact)
                         * (REL_BUCKETS - max_exact)).astype(I32)
    large = jnp.minimum(large, REL_BUCKETS - 1)
    return jnp.where(dist < max_exact, dist, large)


def _bucket_tables():
    qi = jnp.arange(ATT_BLK)[:, None]
    kj = jnp.arange(2 * ATT_BLK)[None, :]
    dist = jnp.clip(ATT_BLK + qi - kj, 0, ATT_BLK)
    return jnp.stack([_t5_bucket(dist * dil) for _, dil in ATT_GROUPS]).astype(I32)


def _retention_fwd(rqk, rv, din, qd, kd, cd):
    nc = SEQ // RET_CHUNK
    c, dk, dv = RET_CHUNK, RET_DK, RET_DV

    def kern(q_ref, k_ref, v_ref, din_ref, qd_ref, kd_ref, cd_ref, o_ref, st_ref, state):
        n = pl.program_id(0)

        @pl.when(n == 0)
        def _():
            state[...] = jnp.zeros_like(state)

        for sub in range(RET_SUB):
            rows = slice(sub * c, (sub + 1) * c)
            for h in range(RET_HEADS):
                q, k = q_ref[rows, h * dk:(h + 1) * dk], k_ref[rows, h * dk:(h + 1) * dk]
                v = v_ref[rows, h * dv:(h + 1) * dv]
                s_b = state[h].astype(BF16)
                st_ref[h, sub] = s_b
                a = lax.dot_general(q, k, _NT, preferred_element_type=F32) * din_ref[h]
                o = jnp.dot(a.astype(BF16), v, preferred_element_type=F32)
                o += jnp.dot(q, s_b, preferred_element_type=F32) * qd_ref[h]
                o_ref[rows, h * dv:(h + 1) * dv] = o
                kk = (k.astype(F32) * kd_ref[h]).astype(BF16)
                state[h] = state[h] * cd_ref[h] + lax.dot_general(kk, v, _TN, preferred_element_type=F32)

    whole = lambda a: pl.BlockSpec(a.shape, lambda n: (0,) * a.ndim)
    cs = RET_SUB * c
    return pl.pallas_call(
        kern, name="retention_fwd", grid=(nc // RET_SUB,),
        in_specs=[
            pl.BlockSpec((cs, RET_QK_W), lambda n: (n, 0)),
            pl.BlockSpec((cs, RET_QK_W), lambda n: (n, 1)),
            pl.BlockSpec((cs, RET_V_W), lambda n: (n, 0)),
            whole(din), whole(qd), whole(kd),
            pl.BlockSpec(memory_space=pltpu.SMEM),
        ],
        out_specs=[
            pl.BlockSpec((cs, RET_V_W), lambda n: (n, 0)),
            pl.BlockSpec((RET_HEADS, RET_SUB, dk, dv), lambda n: (0, n, 0, 0)),
        ],
        out_shape=[
            jax.ShapeDtypeStruct((SEQ, RET_V_W), F32),
            jax.ShapeDtypeStruct((RET_HEADS, nc, dk, dv), BF16),
        ],
        scratch_shapes=[pltpu.VMEM((RET_HEADS, dk, dv), F32)],
        compiler_params=_cparams(("arbitrary",)),
    )(rqk, rqk, rv, din, qd, kd, cd)


def _retention_bwd(rqk, rv, states, d_ro, din, qd, kd, cd, cos, sin):
    nc = SEQ // RET_CHUNK
    c, dk, dv = RET_CHUNK, RET_DK, RET_DV
    half = dk // 2
    last = nc // RET_SUB - 1

    def unrot(g, cs, sn):
        g1, g2 = g[:, :half], g[:, half:]
        return jnp.concatenate([g1 * cs + g2 * sn, g2 * cs - g1 * sn], axis=-1)

    def kern(q_ref, k_ref, v_ref, st_ref, do_ref, din_ref, qd_ref, kd_ref, cd_ref, cos_ref, sin_ref,
             out_ref, dstate):
        step = pl.program_id(0)

        @pl.when(step == 0)
        def _():
            dstate[...] = jnp.zeros_like(dstate)

        for sub in reversed(range(RET_SUB)):
            rows = slice(sub * c, (sub + 1) * c)
            cs, sn = cos_ref[rows, :], sin_ref[rows, :]
            for h in range(RET_HEADS):
                qk_cols, v_cols = slice(h * dk, (h + 1) * dk), slice(h * dv, (h + 1) * dv)
                q, k, v = q_ref[rows, qk_cols], k_ref[rows, qk_cols], v_ref[rows, v_cols]
                s_b = st_ref[h, sub]
                d_ob = do_ref[rows, v_cols]
                d_oq = (d_ob.astype(F32) * qd_ref[h]).astype(BF16)
                ds_b = dstate[h].astype(BF16)
                din_m = din_ref[h]
                a_b = (lax.dot_general(q, k, _NT, preferred_element_type=F32) * din_m).astype(BF16)
                kk = (k.astype(F32) * kd_ref[h]).astype(BF16)
                d_v = lax.dot_general(a_b, d_ob, _TN, preferred_element_type=F32)
                d_v += jnp.dot(kk, ds_b, preferred_element_type=F32)
                d_a = (lax.dot_general(d_ob, v, _NT, preferred_element_type=F32) * din_m).astype(BF16)
                d_q = jnp.dot(d_a, k, preferred_element_type=F32)
                d_q += lax.dot_general(d_oq, s_b, _NT, preferred_element_type=F32)
                d_k = lax.dot_general(d_a, q, _TN, preferred_element_type=F32)
                d_k += lax.dot_general(v, ds_b, _NT, preferred_element_type=F32) * kd_ref[h]
                dstate[h] = dstate[h] * cd_ref[h] + lax.dot_general(q, d_oq, _TN,
                                                                    preferred_element_type=F32)
                out_ref[rows, h * dk:(h + 1) * dk] = unrot(d_q, cs, sn).astype(BF16)
                out_ref[rows, RET_QK_W + h * dk:RET_QK_W + (h + 1) * dk] = (
                    unrot(d_k, cs, sn) * (RET_DK ** -0.5)).astype(BF16)
                out_ref[rows, 2 * RET_QK_W + h * dv:2 * RET_QK_W + (h + 1) * dv] = d_v.astype(BF16)

    whole = lambda a: pl.BlockSpec(a.shape, lambda n: (0,) * a.ndim)
    rs = RET_SUB * c
    return pl.pallas_call(
        kern, name="retention_bwd", grid=(nc // RET_SUB,),
        in_specs=[
            pl.BlockSpec((rs, RET_QK_W), lambda n: (last - n, 0)),
            pl.BlockSpec((rs, RET_QK_W), lambda n: (last - n, 1)),
            pl.BlockSpec((rs, RET_V_W), lambda n: (last - n, 0)),
            pl.BlockSpec((RET_HEADS, RET_SUB, dk, dv), lambda n: (0, last - n, 0, 0)),
            pl.BlockSpec((rs, RET_V_W), lambda n: (last - n, 0)),
            whole(din), whole(qd), whole(kd),
            pl.BlockSpec(memory_space=pltpu.SMEM),
            pl.BlockSpec((rs, half), lambda n: (last - n, 0)),
            pl.BlockSpec((rs, half), lambda n: (last - n, 0)),
        ],
        out_specs=pl.BlockSpec((rs, 2 * RET_QK_W + RET_V_W), lambda n: (last - n, 0)),
        out_shape=jax.ShapeDtypeStruct((SEQ, 2 * RET_QK_W + RET_V_W), BF16),
        scratch_shapes=[pltpu.VMEM((RET_HEADS, dk, dv), F32)],
        compiler_params=_cparams(("arbitrary",)),
    )(rqk, rqk, rv, states, d_ro, din, qd, kd, cd, cos, sin)


def _bias_build(rel_bias, buckets):
    ng = len(ATT_GROUPS)

    def kern(tab_ref, bkt_ref, o_ref):
        g, h = pl.program_id(0), pl.program_id(1)
        bkt = bkt_ref[...]
        acc = jnp.zeros(bkt.shape, F32)
        for b in range(REL_BUCKETS):
            acc = jnp.where(bkt == b, tab_ref[b, g * ATT_HPG + h], acc)
        o_ref[...] = acc

    return pl.pallas_call(
        kern, name="bias_build", grid=(ng, ATT_HPG),
        in_specs=[pl.BlockSpec(memory_space=pltpu.SMEM),
                  pl.BlockSpec((None, ATT_BLK, 2 * ATT_BLK), lambda g, h: (g, 0, 0))],
        out_specs=pl.BlockSpec((None, None, ATT_BLK, 2 * ATT_BLK), lambda g, h: (g, h, 0, 0)),
        out_shape=jax.ShapeDtypeStruct((ng, ATT_HPG, ATT_BLK, 2 * ATT_BLK), F32),
        compiler_params=_cparams(("arbitrary", "arbitrary")),
    )(rel_bias, buckets)


def _bias_grad(dsb, buckets):
    ng = len(ATT_GROUPS)

    def kern(ds_ref, bkt_ref, o_ref):
        g, h = pl.program_id(0), pl.program_id(1)
        bkt, ds = bkt_ref[...], ds_ref[...]
        for b in range(REL_BUCKETS):
            o_ref[b, g * ATT_HPG + h] = jnp.sum(jnp.where(bkt == b, ds, 0.0))

    return pl.pallas_call(
        kern, name="bias_grad", grid=(ng, ATT_HPG),
        in_specs=[pl.BlockSpec((None, None, ATT_BLK, 2 * ATT_BLK), lambda g, h: (g, h, 0, 0)),
                  pl.BlockSpec((None, ATT_BLK, 2 * ATT_BLK), lambda g, h: (g, 0, 0))],
        out_specs=pl.BlockSpec(memory_space=pltpu.SMEM),
        out_shape=jax.ShapeDtypeStruct((REL_BUCKETS, N_ATT_HEADS), F32),
        compiler_params=_cparams(("arbitrary", "arbitrary")),
    )(dsb, buckets)


_NT = (((1,), (1,)), ((), ()))
_TN = (((0,), (0,)), ((), ()))
_ATT_SCALE = ATT_DH ** -0.5


def _window_mask(has_prev):
    qi = lax.broadcasted_iota(I32, (ATT_BLK, 2 * ATT_BLK), 0)
    kj = lax.broadcasted_iota(I32, (ATT_BLK, 2 * ATT_BLK), 1)
    prev_ok = jnp.logical_and(jnp.logical_and(kj < ATT_BLK, kj >= qi), has_prev)
    return jnp.logical_or(prev_ok, jnp.logical_and(kj >= ATT_BLK, qi >= kj - ATT_BLK))


def _head_specs(col0):
    return pl.BlockSpec((SEQ, ATT_DH), lambda h: (0, col0 + h))


def _sub_rows(start, size, dil):
    return pl.ds(start, size) if dil == 1 else pl.ds(start, size, stride=dil)


def _att_blocks(dil):
    nb = SEQ // dil // ATT_BLK
    return [(r + dil * n * ATT_BLK, n > 0, n + 1 < nb) for r in range(dil) for n in range(nb)]


def _att_fwd(gi, dil, qkv, bias):
    blk, dh = ATT_BLK, ATT_DH
    pad = dil * blk
    col0 = 3 * ATT_HPG * gi

    def kern(q_ref, k_ref, v_ref, b_ref, o_ref, l_ref, qf, kpad, vpad):
        zero = jnp.zeros((pad, dh), F32)
        kpad[0:pad, :] = zero
        vpad[0:pad, :] = zero
        kpad[pad:, :] = k_ref[...].astype(F32)
        vpad[pad:, :] = v_ref[...].astype(F32)
        qf[...] = q_ref[...].astype(F32)
        bias_m = b_ref[...]
        for start, has_prev, _ in _att_blocks(dil):
            rows, window = _sub_rows(start, blk, dil), _sub_rows(start, 2 * blk, dil)
            q = qf[rows, :].astype(BF16)
            kw, vw = kpad[window, :].astype(BF16), vpad[window, :].astype(BF16)
            valid = _window_mask(has_prev)
            s = lax.dot_general(q, kw, _NT, preferred_element_type=F32) * _ATT_SCALE + bias_m
            s = jnp.where(valid, s, -1e30)
            mx = jnp.max(s, axis=-1, keepdims=True)
            e = jnp.exp(s - mx)
            den = jnp.sum(e, axis=-1, keepdims=True)
            o_ref[rows, :] = jnp.dot((e / den).astype(BF16), vw, preferred_element_type=F32)
            l_ref[rows, :] = jnp.broadcast_to(mx + jnp.log(den), (blk, dh))

    return pl.pallas_call(
        kern, name=f"att_fwd_g{gi}", grid=(ATT_HPG,),
        in_specs=[_head_specs(col0), _head_specs(col0 + ATT_HPG), _head_specs(col0 + 2 * ATT_HPG),
                  pl.BlockSpec((None, None, blk, 2 * blk), lambda h: (gi, h, 0, 0))],
        out_specs=[_head_specs(0), _head_specs(0)],
        out_shape=[jax.ShapeDtypeStruct((SEQ, ATT_W), F32), jax.ShapeDtypeStruct((SEQ, ATT_W), F32)],
        scratch_shapes=[pltpu.VMEM((SEQ, dh), F32), pltpu.VMEM((SEQ + pad, dh), F32),
                        pltpu.VMEM((SEQ + pad, dh), F32)],
        compiler_params=_cparams(("arbitrary",)),
    )(qkv, qkv, qkv, bias)


def _att_bwd(gi, dil, qkv, d_att, lse, dd, bias):
    blk, dh = ATT_BLK, ATT_DH
    pad = dil * blk
    col0 = 3 * ATT_HPG * gi

    def kern(q_ref, k_ref, v_ref, do_ref, l_ref, d_ref, b_ref, dqkv_ref, dsb_ref,
             qf, kpad, vpad, dq_s, dkpad, dvpad):
        zero = jnp.zeros((pad, dh), F32)
        kpad[0:pad, :] = zero
        vpad[0:pad, :] = zero
        kpad[pad:, :] = k_ref[...].astype(F32)
        vpad[pad:, :] = v_ref[...].astype(F32)
        qf[...] = q_ref[...].astype(F32)
        dkpad[...] = jnp.zeros_like(dkpad)
        dvpad[...] = jnp.zeros_like(dvpad)
        bias_m = b_ref[...]
        ds_sum = jnp.zeros((blk, 2 * blk), F32)

        for start, has_prev, _ in _att_blocks(dil):
            rows, window = _sub_rows(start, blk, dil), _sub_rows(start, 2 * blk, dil)
            q, d_o = qf[rows, :].astype(BF16), do_ref[rows, :].astype(BF16)
            kw, vw = kpad[window, :].astype(BF16), vpad[window, :].astype(BF16)
            lrow, drow = l_ref[rows, :][:, :1], d_ref[rows, :][:, :1]
            valid = _window_mask(has_prev)
            s = lax.dot_general(q, kw, _NT, preferred_element_type=F32) * _ATT_SCALE + bias_m
            p = jnp.where(valid, jnp.exp(jnp.where(valid, s, -1e30) - lrow), 0.0)
            dp = lax.dot_general(d_o, vw, _NT, preferred_element_type=F32)
            ds = p * (dp - drow)
            ds_b = ds.astype(BF16)
            dq_s[rows, :] = jnp.dot(ds_b, kw, preferred_element_type=F32) * _ATT_SCALE
            dkpad[window, :] += lax.dot_general(ds_b, q, _TN, preferred_element_type=F32) * _ATT_SCALE
            dvpad[window, :] += lax.dot_general(p.astype(BF16), d_o, _TN, preferred_element_type=F32)
            ds_sum = ds_sum + ds
        dsb_ref[...] = ds_sum

        dqkv_ref[0] = dq_s[...].astype(BF16)
        dqkv_ref[1] = dkpad[pad:, :].astype(BF16)
        dqkv_ref[2] = dvpad[pad:, :].astype(BF16)

    return pl.pallas_call(
        kern, name=f"att_bwd_g{gi}", grid=(ATT_HPG,),
        in_specs=[_head_specs(col0), _head_specs(col0 + ATT_HPG), _head_specs(col0 + 2 * ATT_HPG),
                  _head_specs(0), _head_specs(0), _head_specs(0),
                  pl.BlockSpec((None, None, blk, 2 * blk), lambda h: (gi, h, 0, 0))],
        out_specs=[pl.BlockSpec((3, SEQ, dh), lambda h: (0, 0, h)),
                   pl.BlockSpec((None, blk, 2 * blk), lambda h: (h, 0, 0))],
        out_shape=[jax.ShapeDtypeStruct((3, SEQ, ATT_W), BF16),
                   jax.ShapeDtypeStruct((ATT_HPG, blk, 2 * blk), F32)],
        scratch_shapes=[pltpu.VMEM((SEQ, dh), F32), pltpu.VMEM((SEQ + pad, dh), F32),
                        pltpu.VMEM((SEQ + pad, dh), F32), pltpu.VMEM((SEQ, dh), F32),
                        pltpu.VMEM((SEQ + pad, dh), F32), pltpu.VMEM((SEQ + pad, dh), F32)],
        compiler_params=_cparams(("arbitrary",)),
    )(qkv, qkv, qkv, d_att, lse, dd, bias)


def _rms_parts(x):
    r = lax.rsqrt(jnp.mean(x * x, axis=-1, keepdims=True) + RMS_EPS)
    return x * r, r


def _rms_bwd(d_xhat, xhat, r):
    return r * (d_xhat - xhat * jnp.mean(d_xhat * xhat, axis=-1, keepdims=True))


def _prenorm_fwd(name, x, gain, shift, scale):
    def body(xt, g, sh, sc):
        xhat, _ = _rms_parts(xt)
        return (xhat * g) * (1.0 + sc) + sh
    return _rowmap(name, body, [x], [gain, shift, scale], [(D_MODEL, BF16)])[0]


def _prenorm_bwd(name, d_h, x, gain, scale, resid, branch=None, gate=None, after=()):
    gated = branch is not None

    def body(d_ht, xt, res, *rest):
        g, sc = rest[-2 - gated], rest[-1 - gated]
        xhat, r = _rms_parts(xt)
        nrm = xhat * g
        d_n = d_ht * (1.0 + sc)
        dx = _rms_bwd(d_n * g, xhat, r) + res
        sums = (jnp.sum(d_ht, axis=0, keepdims=True), jnp.sum(d_ht * nrm, axis=0, keepdims=True),
                jnp.sum(d_n * xhat, axis=0, keepdims=True))
        if not gated:
            return (dx,) + sums
        return (dx, dx * rest[-1]) + sums + (jnp.sum(dx * rest[0], axis=0, keepdims=True),)

    return _rowmap(name, body, [d_h, x, resid] + ([branch] if gated else []),
                   [gain, scale] + ([gate] if gated else []),
                   [(D_MODEL, F32)] + ([(D_MODEL, BF16)] if gated else []),
                   [D_MODEL] * (3 + gated), after=after)


def _gn_parts(ro):
    mu = jnp.mean(ro, axis=-1, keepdims=True)
    cen = ro - mu
    rstd = lax.rsqrt(jnp.mean(cen * cen, axis=-1, keepdims=True) + GN_EPS)
    return cen * rstd, rstd


def _retpost_fwd(ro, rg, gn_g, gn_b):
    def body(rot, rgt, g, b):
        outs = []
        for h in range(RET_HEADS):
            sl = slice(h * RET_DV, (h + 1) * RET_DV)
            nrm, _ = _gn_parts(rot[:, sl])
            gate = rgt[:, sl].astype(F32)
            outs.append((gate * _sigmoid(gate)) * (nrm * g[:, sl] + b[:, sl]))
        return jnp.concatenate(outs, axis=-1)
    return _rowmap("retpost_fwd", body, [ro, rg], [gn_g, gn_b], [(RET_V_W, BF16)])[0]


def _retpost_bwd(d_gated, ro, rg, gn_g, gn_b):
    def body(dgt, rot, rgt, g, b):
        d_ro, d_rg, d_g, d_b = [], [], [], []
        for h in range(RET_HEADS):
            sl = slice(h * RET_DV, (h + 1) * RET_DV)
            nrm, rstd = _gn_parts(rot[:, sl])
            gate, dg = rgt[:, sl].astype(F32), dgt[:, sl].astype(F32)
            sg = _sigmoid(gate)
            ron = nrm * g[:, sl] + b[:, sl]
            d_rg.append(dg * ron * (sg * (1.0 + gate * (1.0 - sg))))
            d_ron = dg * (gate * sg)
            d_g.append(jnp.sum(d_ron * nrm, axis=0, keepdims=True))
            d_b.append(jnp.sum(d_ron, axis=0, keepdims=True))
            d_n = d_ron * g[:, sl]
            d_ro.append(rstd * (d_n - jnp.mean(d_n, axis=-1, keepdims=True)
                                - nrm * jnp.mean(d_n * nrm, axis=-1, keepdims=True)))
        cat = lambda ts: jnp.concatenate(ts, axis=-1)
        return cat(d_ro), cat(d_rg), cat(d_g), cat(d_b)
    return _rowmap("retpost_bwd", body, [d_gated, ro, rg], [gn_g, gn_b],
                   [(RET_V_W, BF16), (RET_V_W, BF16)], [RET_V_W, RET_V_W])


def _combine(os_, ls_):
    def body(o0, o1, o2, l0, l1, l2):
        mx = jnp.maximum(jnp.maximum(l0, l1), l2)
        e0, e1, e2 = jnp.exp(l0 - mx), jnp.exp(l1 - mx), jnp.exp(l2 - mx)
        den = e0 + e1 + e2
        att = (e0 / den) * o0 + (e1 / den) * o1 + (e2 / den) * o2
        return att, att, mx + jnp.log(den)
    return _rowmap("att_combine", body, list(os_) + list(ls_), [],
                   [(ATT_W, F32), (ATT_W, BF16), (ATT_W, F32)])


def _att_bwd_pre(d_att, att):
    def body(dt, at):
        outs = []
        for h in range(ATT_HPG):
            sl = slice(h * ATT_DH, (h + 1) * ATT_DH)
            outs.append(jnp.broadcast_to(jnp.sum(dt[:, sl] * at[:, sl], axis=-1, keepdims=True),
                                         (dt.shape[0], ATT_DH)))
        return jnp.concatenate(outs, axis=-1)
    return _rowmap("att_bwd_pre", body, [d_att, att], [], [(ATT_W, F32)])[0]


def _merge_fwd(gates, ret_out, att_out):
    def body(gt, ro, ao):
        gt = gt.astype(F32)
        return _sigmoid(gt[:, :D_MODEL]) * ro + _sigmoid(gt[:, D_MODEL:]) * ao
    return _rowmap("merge_fwd", body, [gates, ret_out, att_out], [], [(D_MODEL, BF16)])[0]


def _merge_bwd(d_merged, gates, ret_out, att_out):
    def body(dm, gt, ro, ao):
        dm, gt = dm.astype(F32), gt.astype(F32)
        sa, sb = _sigmoid(gt[:, :D_MODEL]), _sigmoid(gt[:, D_MODEL:])
        d_gates = jnp.concatenate([dm * ro * (sa * (1.0 - sa)), dm * ao * (sb * (1.0 - sb))], axis=-1)
        return dm * sa, dm * sb, d_gates
    return _rowmap("merge_bwd", body, [d_merged, gates, ret_out, att_out], [],
                   [(D_MODEL, BF16), (D_MODEL, BF16), (2 * D_MODEL, BF16)])


def _loss_head(x3, target, gain, branch, gate):
    def body(xt, tt, br, g, gt):
        xhat, r = _rms_parts(xt)
        err = xhat * g - tt
        d_y = err / D_MODEL
        loss = 0.5 * jnp.sum(jnp.mean(err * err, axis=-1, keepdims=True), axis=0, keepdims=True)
        d_x = _rms_bwd(d_y * g, xhat, r)
        return (d_x, d_x * gt, jnp.broadcast_to(loss, (1, 128)), jnp.sum(d_y * xhat, axis=0, keepdims=True),
                jnp.sum(d_x * br, axis=0, keepdims=True))
    return _rowmap("loss_head", body, [x3, target, branch], [gain, gate],
                   [(D_MODEL, F32), (D_MODEL, BF16)], [128, D_MODEL, D_MODEL])


def _local_step(pos, x, target, mod, norm1_g, norm2_g, norm_f_g, rel_bias, gn_g, gn_b, w_in, rest_gather):
    sh1, sc1, g1, sh2, sc2, g2 = [mod[:, i * D_MODEL:(i + 1) * D_MODEL] for i in range(6)]
    cos, sin = _rope_tables()
    din, qd, kd, cd = _decay_tables()
    buckets = _bucket_tables()
    bias = _bias_build(rel_bias, buckets)
    dils = [d for _, d in ATT_GROUPS]

    h1 = _prenorm_fwd("prenorm1_fwd", x, norm1_g, sh1, sc1)

    qk_tn = 2 * RET_DK

    def rot_epi(acc, cs, sn, scale):
        half = RET_DK // 2
        outs = []
        for h0 in range(0, qk_tn, RET_DK):
            x1, x2 = acc[:, h0:h0 + half], acc[:, h0 + half:h0 + RET_DK]
            outs += [x1 * cs - x2 * sn, x1 * sn + x2 * cs]
        return (jnp.concatenate(outs, axis=-1) * scale,)

    qk_scale = jnp.concatenate([jnp.ones((1, RET_QK_W), F32),
                                jnp.full((1, RET_QK_W), RET_DK ** -0.5, F32)], axis=-1)
    rope_ex = [(cos, (TM, RET_DK // 2), lambda i, j, kk: (i, 0)),
               (sin, (TM, RET_DK // 2), lambda i, j, kk: (i, 0)),
               (qk_scale, (1, qk_tn), lambda i, j, kk: (0, j))]
    rest_sems, rest_shards, rest_fulls, rest_token = rest_gather
    behind = [rest_token]
    rv = _matmul("proj_rv", h1, w_in, "nn", SEQ, RET_V_W, D_MODEL, [BF16], b_off=OFF_V, tk=D_MODEL,
                 after=behind)[0]
    rg = _matmul("proj_rg", h1, w_in, "nn", SEQ, RET_V_W, D_MODEL, [BF16], b_off=OFF_G, tk=D_MODEL,
                 after=behind)[0]
    gates = _matmul("proj_gates", h1, w_in, "nn", SEQ, 2 * D_MODEL, D_MODEL, [BF16], b_off=OFF_GATE,
                    tn=512, tk=D_MODEL, after=behind)[0]
    aqkv = _matmul("proj_att", h1, w_in, "nn", SEQ, 9 * ATT_W, D_MODEL, [BF16], b_off=OFF_ATT,
                   tn=512, tk=D_MODEL, after=behind)[0]

    os_, ls_ = [], []
    for gi in range(3):
        o_g, l_g = _att_fwd(gi, dils[gi], aqkv, bias)
        os_.append(o_g)
        ls_.append(l_g)
        if gi == 1:
            rest_sems, rest_fulls, fwd_token = _gather_rest_forward(rest_sems, rest_shards, rest_fulls,
                                                                    [o_g, rv, rg, gates])

    rqk = _matmul("proj_qk", h1, w_in, "nn", SEQ, 2 * RET_QK_W, D_MODEL, [BF16], b_off=OFF_Q,
                  tn=qk_tn, tk=D_MODEL, epilogue=rot_epi, extras=rope_ex, after=[fwd_token])[0]
    ro, states = _retention_fwd(rqk, rv, din, qd, kd, cd)
    gated = _retpost_fwd(ro, rg, gn_g, gn_b)
    w_ret_out, w_att_out, w_o, w_ff1, w_ff2 = _gather_rest_end(rest_sems, rest_fulls, [gated, os_[2]])
    ret_out = _matmul("ret_out", gated, w_ret_out, "nn", SEQ, D_MODEL, RET_V_W, [F32], tk=RET_V_W)[0]
    att, att_b, lse = _combine(os_, ls_)
    att_out = _matmul("att_out", att_b, w_att_out, "nn", SEQ, D_MODEL, ATT_W, [F32])[0]

    merged = _merge_fwd(gates, ret_out, att_out)

    def resid_epi(acc, xt, g):
        return xt + g * acc, acc

    def resid_ex(xin, g):
        return [(xin, (TM, TN), lambda i, j, kk: (i, j)), (g, (1, TN), lambda i, j, kk: (0, j))]

    x2, mix = _matmul("mix_out", merged, w_o, "nn", SEQ, D_MODEL, D_MODEL, [F32, F32],
                      epilogue=resid_epi, extras=resid_ex(x, g1))
    h2 = _prenorm_fwd("prenorm2_fwd", x2, norm2_g, sh2, sc2)

    def relu2_epi(acc):
        r = jnp.maximum(acc, 0.0)
        return r * r, r

    act, relu_u = _matmul("ff1", h2, w_ff1, "nn", SEQ, D_FF, D_MODEL, [BF16, BF16], tk=D_MODEL,
                          epilogue=relu2_epi)
    x3, y2 = _matmul("ff2", act, w_ff2, "nn", SEQ, D_MODEL, D_FF, [F32, F32], tk=2048,
                     epilogue=resid_epi, extras=resid_ex(x2, g2))

    d_x3, d_y2, loss, d_gf, d_g2 = _loss_head(x3, target, norm_f_g, y2, g2)

    def relu2_bwd_epi(acc, rt):
        return (acc * (2.0 * rt.astype(F32)),)

    gw_ff2 = _matmul_tn_pair("ff2_dw", pos, act, d_y2, D_FF, D_MODEL, SEQ, D_FF // N_CHIPS,
                             tm=512, tn=1024, tk=SEQ)
    d_u = _matmul("ff2_dx", d_y2, w_ff2, "nt", SEQ, D_FF, D_MODEL, [BF16], epilogue=relu2_bwd_epi,
                  extras=[(relu_u, (TM, TN), lambda i, j, kk: (i, j))])[0]
    gw_ff1 = _matmul_tn_pair("ff1_dw", pos, h2, d_u, D_MODEL, D_FF, SEQ, D_MODEL,
                             tm=512, tn=1024, tk=SEQ)
    ffn = ["w_ff2", "w_ff1"]
    ffn_started = _ici_start("ici_start_ffn", ffn, [gw_ff2, gw_ff1])
    d_h2 = _matmul("ff1_dx", d_u, w_ff1, "nt", SEQ, D_MODEL, D_FF, [F32], tk=2048,
                   after=[ffn_started[3]])[0]
    d_x2, d_mix, d_sh2, d_sc2, d_n2g, d_g1 = _prenorm_bwd("prenorm2_bwd", d_h2, x2, norm2_g, sc2, d_x3,
                                                          branch=mix, gate=g1)
    gw_o = _matmul_tn_pair("mix_dw", pos, merged, d_mix, D_MODEL, D_MODEL, SEQ, D_MODEL // N_CHIPS,
                           tm=128, tn=1024, tk=2048)
    d_merged = _matmul("mix_dx", d_mix, w_o, "nt", SEQ, D_MODEL, D_MODEL, [BF16])[0]
    d_ret_out, d_att_out, d_gates = _merge_bwd(d_merged, gates, ret_out, att_out)

    gw_ret_out = _matmul_tn_pair("ret_out_dw", pos, gated, d_ret_out, RET_V_W, D_MODEL, SEQ,
                                 RET_V_W // N_CHIPS, tm=256, tn=1024, tk=SEQ)
    gw_att_out = _matmul_tn_pair("att_out_dw", pos, att_b, d_att_out, ATT_W, D_MODEL, SEQ, ATT_W,
                                 tm=256, tn=1024, tk=2048)
    mixer = ["w_o", "w_ret_out", "w_att_out"]
    mixer_started = _ici_start("ici_start_mixer", mixer, [gw_o, gw_ret_out, gw_att_out])
    d_gated = _matmul("ret_out_dx", d_ret_out, w_ret_out, "nt", SEQ, RET_V_W, D_MODEL, [BF16],
                      after=[mixer_started[3]])[0]
    d_att = _matmul("att_out_dx", d_att_out, w_att_out, "nt", SEQ, ATT_W, D_MODEL, [F32],
                    after=[mixer_started[3]])[0]

    d_ro, d_rg, d_gn_g, d_gn_b = _retpost_bwd(d_gated, ro, rg, gn_g, gn_b)
    d_rqkv = _retention_bwd(rqk, rv, states, d_ro, din, qd, kd, cd, cos, sin)

    dd = _att_bwd_pre(d_att, att)
    d_aqkv, dsbs = [], []
    for gi in range(3):
        dqkv, dsb = _att_bwd(gi, dils[gi], aqkv, d_att, lse, dd, bias)
        d_aqkv.append(dqkv)
        dsbs.append(dsb)
    d_rel_bias = _bias_grad(jnp.stack(dsbs), buckets)

    d_proj = [(d_rqkv, False), (d_rg, False)] + [(t, True) for t in d_aqkv] + [(d_gates, False)]
    gw_in = _matmul_tn_pair("proj_dw", pos, h1, d_proj, D_MODEL, IN_COLS, SEQ, D_MODEL,
                            tm=512, tn=ATT_W, tk=SEQ)
    sems, (gw_in,), (land,), token = _ici_start("ici_start_w_in", ["w_in"], [gw_in])
    d_h1 = _matmul("proj_dx", d_proj, w_in, "nt", SEQ, D_MODEL, IN_COLS, [F32], tn=1024, tk=ATT_W,
                   after=[token])[0]
    pending = (sems, land)

    names = ffn + mixer
    psums, got = _ici_wait("ici_wait_rest", names, list(ffn_started[0]) + list(mixer_started[0]),
                           list(ffn_started[1]) + list(mixer_started[1]),
                           list(ffn_started[2]) + list(mixer_started[2]), [d_h1])
    g_big = {n: _final_sum("final_" + n, pos, dict(BIG)[n], psums[i], got[i], SHARD[n])
             for i, n in enumerate(names)}
    grad_x, d_sh1, d_sc1, d_n1g = _prenorm_bwd("prenorm1_bwd", d_h1, x, norm1_g, sc1, d_x2,
                                               after=list(g_big.values()))
    d_mod = jnp.concatenate([d_sh1, d_sc1, d_g1, d_sh2, d_sc2, d_g2], axis=-1)
    small = dict(norm1_g=d_n1g, norm2_g=d_n2g, norm_f_g=d_gf, gn_g=d_gn_g, gn_b=d_gn_b,
                 rel_bias=d_rel_bias)
    return loss, grad_x, d_mod, small, g_big, (gw_in,) + pending


def _me():
    return lax.axis_index("x"), lax.axis_index("y"), lax.axis_index("c")


def _peer(x, y, c, mask):
    return (x ^ ((mask >> 2) & 1), y ^ ((mask >> 1) & 1), c ^ (mask & 1))


def _gather8(src_ref, dst_ref, send_sems, recv_sems):
    x, y, c = _me()
    me = 4 * x + 2 * y + c
    copies = []
    for mask in range(1, N_DEV):
        cp = pltpu.make_async_remote_copy(
            src_ref=src_ref, dst_ref=dst_ref.at[me], send_sem=send_sems.at[mask - 1],
            recv_sem=recv_sems.at[mask - 1], device_id=_peer(x, y, c, mask), device_id_type=MESH)
        cp.start()
        copies.append(cp)
    dst_ref[me] = src_ref[...]
    for cp in copies:
        cp.wait_recv()
    for cp in copies:
        cp.wait_send()


def _ada_fwd(c_in, w_ada, b_ada):
    ncol = ADA_COLS // N_CHIPS

    def body(c_ref, w_ref, b_ref, mod_ref, sc_ref, cbuf, cg, mbuf, mg, s1, r1, s2, r2):
        x, y, c = _me()
        me = 4 * x + 2 * y + c
        cv = c_ref[...]
        cbuf[...] = jnp.broadcast_to(cv * _sigmoid(cv), cbuf.shape)
        _gather8(cbuf, cg, s1, r1)
        rows = lax.broadcasted_iota(I32, (N_DEV, D_MODEL), 0)
        sc_all = jnp.zeros((N_DEV, D_MODEL), F32)
        for d in range(N_DEV):
            sc_all = jnp.where(rows == d, cg[d], sc_all)
        sc_ref[...] = sc_all
        mbuf[...] = jnp.dot(sc_all.astype(BF16), w_ref[...].astype(BF16), preferred_element_type=F32)
        _gather8(mbuf, mg, s2, r2)
        rowsel = lax.broadcasted_iota(I32, (N_DEV, ncol), 0) == me
        for k in range(N_CHIPS):
            blk = mg[2 * k]
            row = jnp.sum(jnp.where(rowsel, blk, 0.0), axis=0, keepdims=True)
            mod_ref[:, k * ncol:(k + 1) * ncol] = row + b_ref[:, k * ncol:(k + 1) * ncol]

    vm = pl.BlockSpec(memory_space=pltpu.VMEM)
    return pl.pallas_call(
        body, name="ada_fwd",
        in_specs=[vm, vm, vm], out_specs=[vm, vm],
        out_shape=[jax.ShapeDtypeStruct((1, ADA_COLS), F32), jax.ShapeDtypeStruct((N_DEV, D_MODEL), F32)],
        scratch_shapes=[
            pltpu.VMEM((8, D_MODEL), F32), pltpu.VMEM((N_DEV, 8, D_MODEL), F32),
            pltpu.VMEM((8, ncol), F32), pltpu.VMEM((N_DEV, 8, ncol), F32),
            pltpu.SemaphoreType.DMA((N_DEV - 1,)), pltpu.SemaphoreType.DMA((N_DEV - 1,)),
            pltpu.SemaphoreType.DMA((N_DEV - 1,)), pltpu.SemaphoreType.DMA((N_DEV - 1,)),
        ],
        compiler_params=pltpu.CompilerParams(vmem_limit_bytes=VMEM_LIMIT_V7X),
    )(c_in, w_ada, b_ada)


def _small_reduce(pack, sc_all):
    ncol = ADA_COLS // N_CHIPS

    def body(p_ref, sc_ref, tot_ref, gw_ref, pg, s1, r1):
        x, y, _ = _me()
        chip = 2 * x + y
        _gather8(p_ref, pg, s1, r1)
        tot = pg[0]
        for d in range(1, N_DEV):
            tot = tot + pg[d]
        tot_ref[...] = tot
        rows = lax.broadcasted_iota(I32, (N_DEV, ncol), 0)
        dmod = jnp.zeros((N_DEV, ncol), F32)
        for k in range(N_CHIPS):
            part = jnp.zeros((N_DEV, ncol), F32)
            for d in range(N_DEV):
                part = jnp.where(rows == d, pg[d, :, k * ncol:(k + 1) * ncol][0:1, :], part)
            dmod = jnp.where(chip == k, part, dmod)
        gw_ref[...] = lax.dot_general(sc_ref[...].astype(BF16), dmod.astype(BF16), _TN,
                                      preferred_element_type=F32)

    vm = pl.BlockSpec(memory_space=pltpu.VMEM)
    return pl.pallas_call(
        body, name="small_reduce",
        in_specs=[vm, vm], out_specs=[vm, vm],
        out_shape=[jax.ShapeDtypeStruct((8, ADA_COLS), F32), jax.ShapeDtypeStruct((D_MODEL, ncol), F32)],
        scratch_shapes=[pltpu.VMEM((N_DEV, 8, ADA_COLS), F32),
                        pltpu.SemaphoreType.DMA((N_DEV - 1,)), pltpu.SemaphoreType.DMA((N_DEV - 1,))],
        compiler_params=pltpu.CompilerParams(vmem_limit_bytes=VMEM_LIMIT_V7X),
    )(pack, sc_all)


BIG = (("w_in", 1), ("w_ret_out", 0), ("w_att_out", 1), ("w_o", 0), ("w_ff1", 1), ("w_ff2", 0))
SHARD = {"w_in": (D_MODEL, IN_COLS // N_CHIPS), "w_ret_out": (RET_V_W // N_CHIPS, D_MODEL),
         "w_att_out": (ATT_W, D_MODEL // N_CHIPS), "w_o": (D_MODEL // N_CHIPS, D_MODEL),
         "w_ff1": (D_MODEL, D_FF // N_CHIPS), "w_ff2": (D_FF // N_CHIPS, D_MODEL)}
_CHIP_FLIPS = ((1, 0), (0, 1), (1, 1))


def _region(ref, axis, chip, half, shard_shape):
    r, cw = shard_shape
    hr = r // 2
    if axis == 1:
        return ref.at[pl.ds(half * hr, hr), pl.ds(chip * cw, cw)]
    return ref.at[pl.ds(chip * r + half * hr, hr), :]


def _gather_weights(shards, n_remote):
    nw = len(BIG)
    shapes = [s.shape for s in shards]
    full_shapes = [(r, N_CHIPS * cw) if ax == 1 else (N_CHIPS * r, cw)
                   for (r, cw), (_, ax) in zip(shapes, BIG)]

    def body(*refs):
        ins, outs = refs[:nw], refs[nw:2 * nw]
        own = refs[2 * nw:3 * nw]
        from_ici, from_sib = refs[3 * nw:3 * nw + n_remote], refs[3 * nw + n_remote:3 * nw + 2 * n_remote]
        ld_sem, st_sem, s_ici, r_ici, s_d2d, r_d2d, st_a, st_b = refs[3 * nw + 2 * n_remote:]
        x, y, c = _me()
        chip = 2 * x + y
        sib = (x, y, 1 - c)
        loads = [pltpu.make_async_copy(ins[i], own[i], ld_sem.at[i]) for i in range(nw)]
        for cp in loads:
            cp.start()
        pending, first = [], []
        for i, (_, ax) in enumerate(BIG):
            r, cw = shapes[i]
            hr = r // 2
            loads[i].wait()
            dst = outs[i].at[:, pl.ds(chip * cw, cw)] if ax == 1 else outs[i].at[pl.ds(chip * r, r), :]
            cp = pltpu.make_async_copy(own[i], dst, st_sem.at[i])
            cp.start()
            pending.append(cp)
            for j, (fx, fy) in enumerate(_CHIP_FLIPS if i < n_remote else ()):
                rc = pltpu.make_async_remote_copy(
                    src_ref=own[i].at[pl.ds(c * hr, hr), :], dst_ref=from_ici[i].at[j],
                    send_sem=s_ici.at[j * nw + i], recv_sem=r_ici.at[j * nw + i],
                    device_id=(x ^ fx, y ^ fy, c), device_id_type=MESH)
                rc.start()
                first.append((j, i, rc))
        passed = []
        for j, i, rc in first:
            fx, fy = _CHIP_FLIPS[j]
            src_chip = 2 * (x ^ fx) + (y ^ fy)
            ax = BIG[i][1]
            rc.wait_recv()
            fw = pltpu.make_async_remote_copy(
                src_ref=from_ici[i].at[j], dst_ref=from_sib[i].at[j], send_sem=s_d2d.at[j * nw + i],
                recv_sem=r_d2d.at[j * nw + i], device_id=sib, device_id_type=MESH)
            fw.start()
            passed.append((j, i, src_chip, fw))
            st = pltpu.make_async_copy(from_ici[i].at[j], _region(outs[i], ax, src_chip, c, shapes[i]),
                                       st_a.at[j * nw + i])
            st.start()
            pending.append(st)
        for j, i, src_chip, fw in passed:
            fw.wait_recv()
            st = pltpu.make_async_copy(from_sib[i].at[j],
                                       _region(outs[i], BIG[i][1], src_chip, 1 - c, shapes[i]),
                                       st_b.at[j * nw + i])
            st.start()
            pending.append(st)
        for _, _, rc in first:
            rc.wait_send()
        for _, _, _, fw in passed:
            fw.wait_send()
        for cp in pending:
            cp.wait()

    hbm = pl.BlockSpec(memory_space=pl.ANY)
    halves = [pltpu.VMEM((3, r // 2, cw), BF16) for r, cw in shapes[:n_remote]]
    return pl.pallas_call(
        body, name="gather_weights",
        in_specs=[hbm] * nw, out_specs=[hbm] * nw,
        out_shape=[jax.ShapeDtypeStruct(fs, BF16) for fs in full_shapes],
        scratch_shapes=[pltpu.VMEM(sh, BF16) for sh in shapes] + halves + halves
        + [pltpu.SemaphoreType.DMA((nw,)), pltpu.SemaphoreType.DMA((nw,))]
        + [pltpu.SemaphoreType.DMA((3 * nw,))] * 6,
        compiler_params=pltpu.CompilerParams(vmem_limit_bytes=VMEM_LIMIT_V7X),
    )(*shards)


REST = BIG[1:]
_SIDE_EFFECTS = pltpu.CompilerParams(has_side_effects=pltpu.SideEffectType.DATAFLOW_SIDE_EFFECTING)
_ANY_SPEC = pl.BlockSpec(memory_space=pl.ANY)


def _rest_ici_copies(shard_refs, full_refs, sems):
    x, y, c = _me()
    chip = 2 * x + y
    n = 3 * len(REST)
    copies = []
    for i, (name, ax) in enumerate(REST):
        hr = SHARD[name][0] // 2
        for j, (fx, fy) in enumerate(_CHIP_FLIPS):
            copies.append(pltpu.make_async_remote_copy(
                src_ref=shard_refs[i].at[pl.ds(c * hr, hr), :],
                dst_ref=_region(full_refs[i], ax, chip, c, SHARD[name]),
                send_sem=sems[3 * i + j], recv_sem=sems[n + 3 * i + j],
                device_id=(x ^ fx, y ^ fy, c), device_id_type=MESH))
    return copies


def _rest_d2d_copies(full_refs, sems):
    x, y, c = _me()
    n = 3 * len(REST)
    copies = []
    for i, (name, ax) in enumerate(REST):
        for j, (fx, fy) in enumerate(_CHIP_FLIPS):
            reg = _region(full_refs[i], ax, 2 * (x ^ fx) + (y ^ fy), c, SHARD[name])
            copies.append(pltpu.make_async_remote_copy(
                src_ref=reg, dst_ref=reg, send_sem=sems[3 * i + j], recv_sem=sems[n + 3 * i + j],
                device_id=(x, y, 1 - c), device_id_type=MESH))
    return copies


def _gather_rest_start(shards, fulls, after):
    nr, ns, na = len(REST), 6 * len(REST), len(after)

    def body(*refs):
        for cp in _rest_ici_copies(refs[:nr], refs[nr:2 * nr], refs[2 * nr + na:2 * nr + na + ns]):
            cp.start()
        token = refs[-1]
        token[...] = jnp.zeros_like(token)

    hbm = lambda a: pltpu.HBM(a.shape, a.dtype)
    res = pl.pallas_call(
        body, name="gather_rest_start",
        out_shape=(pltpu.SemaphoreType.DMA(()),) * ns + tuple(hbm(a) for a in shards + fulls)
        + (jax.ShapeDtypeStruct((8, 128), F32),),
        in_specs=(_HBM_SPEC,) * (2 * nr) + (_ANY_SPEC,) * na,
        out_specs=(_SEM_SPEC,) * ns + (_HBM_SPEC,) * (2 * nr) + (pl.BlockSpec(memory_space=pltpu.VMEM),),
        input_output_aliases={k: ns + k for k in range(2 * nr)}, compiler_params=_SIDE_EFFECTS,
    )(*[pltpu.with_memory_space_constraint(a, pltpu.HBM) for a in shards + fulls], *after)
    return res[:ns], res[ns:ns + nr], res[ns + nr:ns + 2 * nr], res[-1]


def _gather_rest_forward(sems, shards, fulls, after):
    nr, ns = len(REST), 6 * len(REST)

    def body(*refs):
        shard_refs, full_refs, old = refs[:nr], refs[nr:2 * nr], refs[2 * nr:2 * nr + ns]
        new = refs[2 * nr + ns + len(after):2 * nr + 2 * ns + len(after)]
        for cp in _rest_ici_copies(shard_refs, full_refs, old):
            cp.wait_send()
            cp.wait_recv()
        for cp in _rest_d2d_copies(full_refs, new):
            cp.start()
        token = refs[-1]
        token[...] = jnp.zeros_like(token)

    res = pl.pallas_call(
        body, name="gather_rest_forward",
        out_shape=(pltpu.SemaphoreType.DMA(()),) * ns + tuple(pltpu.HBM(a.shape, a.dtype) for a in fulls)
        + (jax.ShapeDtypeStruct((8, 128), F32),),
        in_specs=(_HBM_SPEC,) * (2 * nr) + (_SEM_SPEC,) * ns + (_ANY_SPEC,) * len(after),
        out_specs=(_SEM_SPEC,) * ns + (_HBM_SPEC,) * nr + (pl.BlockSpec(memory_space=pltpu.VMEM),),
        input_output_aliases={nr + k: ns + k for k in range(nr)}, compiler_params=_SIDE_EFFECTS,
    )(*shards, *fulls, *sems, *after)
    return res[:ns], res[ns:ns + nr], res[-1]


def _gather_rest_end(sems, fulls, after):
    nr, ns = len(REST), 6 * len(REST)

    def body(*refs):
        for cp in _rest_d2d_copies(refs[:nr], refs[nr:nr + ns]):
            cp.wait_send()
            cp.wait_recv()

    return pl.pallas_call(
        body, name="gather_rest_end",
        out_shape=tuple(pltpu.HBM(a.shape, a.dtype) for a in fulls),
        in_specs=(_HBM_SPEC,) * nr + (_SEM_SPEC,) * ns + (_ANY_SPEC,) * len(after),
        out_specs=(_HBM_SPEC,) * nr,
        input_output_aliases={k: k for k in range(nr)}, compiler_params=_SIDE_EFFECTS,
    )(*fulls, *sems, *after)


def _adam_update(w, g, m, v):
    mn = ADAM_B1 * m + (1.0 - ADAM_B1) * g
    vn = ADAM_B2 * v + (1.0 - ADAM_B2) * (g * g)
    m_hat = mn / (1.0 - ADAM_B1 ** ADAM_STEP)
    v_hat = vn / (1.0 - ADAM_B2 ** ADAM_STEP)
    return -ADAM_LR * (m_hat / (jnp.sqrt(v_hat) + ADAM_EPS) + ADAM_WD * w), mn, vn


def _final_sum(name, pos, axis, psum, recv, shard_shape, after=(), tr=128):
    r, cw = shard_shape
    hr = r // 2
    tr = min(tr, hr)
    nt = hr // tr
    n_after = len(after)

    def kern(pos_ref, p_ref, r_ref, *rest):
        g_ref, send_buf, land_buf, s_sem, r_sem = rest[n_after:]
        p, t = pl.program_id(0), pl.program_id(1)
        sib = _sibling()

        def copy(i):
            return pltpu.make_async_remote_copy(
                src_ref=send_buf.at[i], dst_ref=land_buf.at[i], send_sem=s_sem.at[i],
                recv_sem=r_sem.at[i], device_id=sib, device_id_type=MESH)

        @pl.when(p == 0)
        def _():
            tot = p_ref[...].astype(F32)
            for j in range(3):
                tot = tot + r_ref[j].astype(F32)
            send_buf[t] = tot
            copy(t).start()
            g_ref[...] = tot

        @pl.when(p == 1)
        def _():
            copy(t).wait_recv()
            g_ref[...] = land_buf[t]

        @pl.when(jnp.logical_and(p == 1, t == nt - 1))
        def _():
            for i in range(nt):
                copy(i).wait_send()

    def shard_rows(p, t, pos_ref):
        return (jnp.where(p == 0, pos_ref[0], 1 - pos_ref[0]) * nt + t, 0)

    def own_part(p, t, pos_ref):
        tt = jnp.where(p == 0, t, nt - 1)
        return (tt, pos_ref[1]) if axis == 1 else (pos_ref[1] * nt + tt, 0)

    grid_spec = pltpu.PrefetchScalarGridSpec(
        num_scalar_prefetch=1, grid=(2, nt),
        in_specs=[pl.BlockSpec((tr, cw), own_part),
                  pl.BlockSpec((3, tr, cw), lambda p, t, pos_ref: (0, jnp.where(p == 0, t, nt - 1), 0))]
        + [pl.BlockSpec(memory_space=pl.ANY)] * n_after,
        out_specs=pl.BlockSpec((tr, cw), shard_rows),
        scratch_shapes=[pltpu.VMEM((nt, tr, cw), F32), pltpu.VMEM((nt, tr, cw), F32),
                        pltpu.SemaphoreType.DMA((nt,)), pltpu.SemaphoreType.DMA((nt,))])
    return pl.pallas_call(
        kern, name=name, grid_spec=grid_spec, out_shape=jax.ShapeDtypeStruct((r, cw), F32),
        compiler_params=_cparams(("arbitrary", "arbitrary")),
    )(pos, psum, recv, *after)


def _adamw(name, w, g, m, v):
    r, cw = w.shape
    tr = min(r, 128)

    def kern(w_ref, g_ref, m_ref, v_ref, go_ref, d_ref, nm_ref, nv_ref):
        gv = g_ref[...]
        go_ref[...] = gv
        d_ref[...], nm_ref[...], nv_ref[...] = _adam_update(w_ref[...], gv, m_ref[...], v_ref[...])

    spec = pl.BlockSpec((tr, cw), lambda i: (i, 0))
    return pl.pallas_call(
        kern, name=name, grid=(r // tr,), in_specs=[spec] * 4, out_specs=[spec] * 4,
        out_shape=[jax.ShapeDtypeStruct((r, cw), F32)] * 4, compiler_params=_cparams(("parallel",)),
    )(w, g, m, v)


_PACK_W = ADA_COLS
_NB = REL_BUCKETS * N_ATT_HEADS
_SMALL_SLOTS = {
    "b_ada": (0, 0, ADA_COLS),
    "norm1_g": (1, 0, D_MODEL), "norm2_g": (1, D_MODEL, D_MODEL), "norm_f_g": (1, 2 * D_MODEL, D_MODEL),
    "ret_gn_g": (1, 3 * D_MODEL, RET_V_W),
    "ret_gn_b": (2, 0, RET_V_W), "rel_bias": (2, RET_V_W, _NB), "loss": (2, RET_V_W + 512, 128),
}


def _pack_small(vals):
    rows = []
    for r in range(8):
        items = sorted([(off, n) for n, (rr, off, _) in _SMALL_SLOTS.items() if rr == r and n in vals])
        parts, pos = [], 0
        for off, n in items:
            if off > pos:
                parts.append(jnp.zeros((1, off - pos), F32))
            parts.append(vals[n].reshape(1, -1).astype(F32))
            pos = off + _SMALL_SLOTS[n][2]
        if pos < _PACK_W:
            parts.append(jnp.zeros((1, _PACK_W - pos), F32))
        rows.append(jnp.concatenate(parts, axis=-1))
    return jnp.concatenate(rows, axis=0)


def _unpack_small(pack, name):
    r, off, wd = _SMALL_SLOTS[name]
    return pack[r:r + 1, off:off + wd]


def kernel(x, c, w_ada, b_ada, norm1_g, w_in, rel_bias, ret_gn_g, ret_gn_b, w_ret_out, w_att_out, w_o, norm2_g, w_ff1, w_ff2, norm_f_g, loss_target, m_w_ada, m_b_ada, m_norm1_g, m_w_in, m_rel_bias, m_ret_gn_g, m_ret_gn_b, m_w_ret_out, m_w_att_out, m_w_o, m_norm2_g, m_w_ff1, m_w_ff2, m_norm_f_g, v_w_ada, v_b_ada, v_norm1_g, v_w_in, v_rel_bias, v_ret_gn_g, v_ret_gn_b, v_w_ret_out, v_w_att_out, v_w_o, v_norm2_g, v_w_ff1, v_w_ff2, v_norm_f_g):
    given = dict(locals())
    big_names = [n for n, _ in BIG]
    shard_w = {n: given[n][0] for n in big_names}
    assert all(shard_w[n].shape == SHARD[n] for n in big_names)

    shards_bf = [shard_w[n].astype(BF16) for n in big_names]
    full = _gather_weights(shards_bf, 1)
    mod, sc_all = _ada_fwd(c, w_ada[0], b_ada)
    rest_gather = _gather_rest_start(shards_bf[1:], list(full[1:]), [mod])
    pos = _where_am_i()

    loss, grad_x, d_mod, small, g_big, pending = _local_step(
        pos, x[0], loss_target[0], mod, norm1_g, norm2_g, norm_f_g.reshape(1, -1), rel_bias, ret_gn_g,
        ret_gn_b, full[0], rest_gather)

    pack_g = _pack_small(dict(b_ada=d_mod, norm1_g=small["norm1_g"], norm2_g=small["norm2_g"],
                              norm_f_g=small["norm_f_g"], ret_gn_g=small["gn_g"], ret_gn_b=small["gn_b"],
                              rel_bias=small["rel_bias"], loss=loss))
    tot, g_w_ada = _small_reduce(pack_g, sc_all)

    small_names = ["b_ada", "norm1_g", "rel_bias", "ret_gn_g", "ret_gn_b", "norm2_g", "norm_f_g"]
    pack_w = _pack_small({n: given[n] for n in small_names})
    pack_m = _pack_small({n: given["m_" + n] for n in small_names})
    pack_v = _pack_small({n: given["v_" + n] for n in small_names})
    _, sd, sm, sv = _adamw("adamw_small", pack_w, tot, pack_m, pack_v)

    grads, deltas, new_m, new_v = {}, {}, {}, {}
    for n in small_names:
        shp = given[n].shape
        grads[n] = _unpack_small(tot, n).reshape(shp)
        deltas[n] = _unpack_small(sd, n).reshape(shp)
        new_m[n] = _unpack_small(sm, n).reshape(shp)
        new_v[n] = _unpack_small(sv, n).reshape(shp)
    g_big["w_ada"] = g_w_ada
    for n in ["w_ada"] + big_names[1:] + big_names[:1]:
        if n == "w_in":
            gw_in, sems, land = pending
            done = [tot, sd] + [deltas[k] for k in ["w_ada"] + big_names[1:]]
            (gw_in,), (got,) = _ici_wait("ici_wait_w_in", [n], sems, [gw_in], [land], done)
            g_big[n] = _final_sum("final_w_in", pos, 1, gw_in, got, SHARD[n])
        g, d, nm, nv = _adamw("adamw_" + n, given[n][0], g_big[n], given["m_" + n][0], given["v_" + n][0])
        grads[n], deltas[n], new_m[n], new_v[n] = g[None], d[None], nm[None], nv[None]

    order = ["w_ada", "b_ada", "norm1_g", "w_in", "rel_bias", "ret_gn_g", "ret_gn_b", "w_ret_out",
             "w_att_out", "w_o", "norm2_g", "w_ff1", "w_ff2", "norm_f_g"]
    loss_out = _unpack_small(tot, "loss")[0, 0]
    return (loss_out, grad_x[None], *[grads[n] for n in order], *[deltas[n] for n in order],
            *[new_m[n] for n in order], *[new_v[n] for n in order])
```

```python
import functools
import math

import jax
import jax.numpy as jnp
import numpy as np
from jax import lax
from jax.experimental import pallas as pl
from jax.experimental.pallas import tpu as pltpu

F32 = jnp.float32
BF16 = jnp.bfloat16
I32 = jnp.int32

SEQ = 2048
D_MODEL = 1024
RET_HEADS = 4
RET_DK = 256
RET_DV = 512
RET_CHUNK = 128
RET_SUB = 2
RET_QK_W = RET_HEADS * RET_DK
RET_V_W = RET_HEADS * RET_DV
ATT_GROUPS = ((128, 1), (512, 4), (2048, 16))
ATT_HPG = 4
ATT_DH = 128
ATT_W = ATT_HPG * ATT_DH
ATT_BLK = 128
N_BLK = SEQ // ATT_BLK
REL_BUCKETS = 32
REL_MAX_DIST = 2048
N_ATT_HEADS = 12
D_FF = 4 * D_MODEL
RMS_EPS = 1e-6
GN_EPS = 1e-5
ROPE_BASE = 10000.0
IN_COLS = 2 * RET_QK_W + 2 * RET_V_W + 9 * ATT_W + 2 * D_MODEL
OFF_Q, OFF_K, OFF_V, OFF_G = 0, RET_QK_W, 2 * RET_QK_W, 2 * RET_QK_W + RET_V_W
OFF_ATT = 2 * RET_QK_W + 2 * RET_V_W
OFF_GATE = OFF_ATT + 9 * ATT_W
N_CHIPS = 4
N_DEV = 8
ADA_COLS = 6 * D_MODEL

ADAM_LR = 0.001
ADAM_B1 = 0.9
ADAM_B2 = 0.999
ADAM_EPS = 1e-08
ADAM_WD = 0.01
ADAM_STEP = 10

VMEM_LIMIT_V7X = 56 * 1024 * 1024
MESH = pl.DeviceIdType.MESH


def _cparams(sem):
    return pltpu.CompilerParams(dimension_semantics=sem, vmem_limit_bytes=VMEM_LIMIT_V7X)


def _sigmoid(v):
    return 1.0 / (1.0 + jnp.exp(-v))


def _rowmap(name, body, row_ins, bcast_ins, row_outs, sum_outs=(), tm=256, after=()):
    m = row_ins[0].shape[0]
    n_in = len(row_ins) + len(bcast_ins)
    n_ro = len(row_outs)

    def kern(*refs):
        vals = [r[...] for r in refs[:n_in]]
        res = body(*vals)
        if not isinstance(res, (tuple, list)):
            res = (res,)
        outs = refs[n_in + len(after):]
        for r, v in zip(outs[:n_ro], res[:n_ro]):
            r[...] = v.astype(r.dtype)
        if sum_outs:
            @pl.when(pl.program_id(0) == 0)
            def _():
                for r in outs[n_ro:]:
                    r[...] = jnp.zeros_like(r)
            for r, v in zip(outs[n_ro:], res[n_ro:]):
                r[...] += v

    in_specs = [pl.BlockSpec((tm, a.shape[1]), lambda i: (i, 0)) for a in row_ins]
    in_specs += [pl.BlockSpec(a.shape, lambda i: (0, 0)) for a in bcast_ins]
    in_specs += [pl.BlockSpec(memory_space=pl.ANY)] * len(after)
    out_specs = [pl.BlockSpec((tm, n), lambda i: (i, 0)) for n, _ in row_outs]
    out_specs += [pl.BlockSpec((1, n), lambda i: (0, 0)) for n in sum_outs]
    out_shape = [jax.ShapeDtypeStruct((m, n), dt) for n, dt in row_outs]
    out_shape += [jax.ShapeDtypeStruct((1, n), F32) for n in sum_outs]
    return pl.pallas_call(
        kern, name=name, grid=(m // tm,), in_specs=in_specs, out_specs=out_specs,
        out_shape=out_shape, compiler_params=_cparams(("arbitrary",)),
    )(*row_ins, *bcast_ins, *after)


TM, TN = 1024, 1024


def _piece_chunks(piece, width):
    arr, stacked = piece
    return arr.shape[0] if stacked else arr.shape[1] // width


def _piece_spec(piece, rows, width, start, row_of, chunk_of):
    arr, stacked = piece
    last = _piece_chunks(piece, width) - 1

    def local(*ids):
        return jnp.clip(chunk_of(*ids) - start, 0, last)

    def row(*ids):
        rel = chunk_of(*ids) - start
        return jnp.where(jnp.logical_and(rel >= 0, rel <= last), row_of(*ids), 0)

    if stacked:
        return pl.BlockSpec((None, rows, width), lambda *ids: (local(*ids), row(*ids), 0))
    return pl.BlockSpec((rows, width), lambda *ids: (row(*ids), local(*ids)))


def _piece_starts(pieces, width):
    return [sum(_piece_chunks(p, width) for p in pieces[:q]) for q in range(len(pieces))]


def _matmul(name, a, b, kind, m, n, k, outs, *, b_off=0, tm=TM, tn=TN, tk=1024,
            epilogue=None, extras=(), after=()):
    tm, tn, tk = min(tm, m), min(tn, n), min(tk, k)
    nk = k // tk
    pieces = a if isinstance(a, list) else [(a, False)]
    starts = _piece_starts(pieces, tk)
    if kind == "nn":
        a_specs = [pl.BlockSpec((tm, tk), lambda i, j, kk: (i, kk))]
        b_spec = pl.BlockSpec((tk, tn), lambda i, j, kk: (kk, b_off // tn + j))
        dn = (((1,), (0,)), ((), ()))
    elif kind == "nt":
        a_specs = [_piece_spec(p, tm, tk, st, lambda i, j, kk: i, lambda i, j, kk: kk)
                   for p, st in zip(pieces, starts)]
        b_spec = pl.BlockSpec((tn, tk), lambda i, j, kk: (j, b_off // tk + kk))
        dn = (((1,), (1,)), ((), ()))
    else:
        a_specs = [pl.BlockSpec((tk, tm), lambda i, j, kk: (kk, i))]
        b_spec = pl.BlockSpec((tk, tn), lambda i, j, kk: (kk, j))
        dn = (((0,), (0,)), ((), ()))
    n_a, n_ex, n_out = len(pieces), len(extras), len(outs)
    if epilogue is None:
        epilogue = lambda acc: (acc,)

    def finish(acc, ex_refs, out_refs):
        res = epilogue(acc, *[r[...] for r in ex_refs])
        for r, v in zip(out_refs, res):
            r[...] = v.astype(r.dtype)

    n_in = n_a + 1 + n_ex + len(after)

    def kern(*refs):
        a_refs, b_ref = refs[:n_a], refs[n_a]
        ex_refs = refs[n_a + 1:n_a + 1 + n_ex]
        out_refs = refs[n_in:n_in + n_out]
        kk = pl.program_id(2)
        dot = lambda a_ref: lax.dot_general(a_ref[...], b_ref[...], dn, preferred_element_type=F32)
        if nk == 1:
            finish(dot(a_refs[0]), ex_refs, out_refs)
            return
        acc_ref = refs[n_in + n_out]
        if n_a == 1:
            part = dot(a_refs[0])

            @pl.when(kk == 0)
            def _():
                acc_ref[...] = part

            @pl.when(kk > 0)
            def _():
                acc_ref[...] += part
        else:
            @pl.when(kk == 0)
            def _():
                acc_ref[...] = jnp.zeros_like(acc_ref)

            for q in range(n_a):
                @pl.when(jnp.logical_and(kk >= starts[q], kk < starts[q] + _piece_chunks(pieces[q], tk)))
                def _(q=q):
                    acc_ref[...] += dot(a_refs[q])

        @pl.when(kk == nk - 1)
        def _():
            finish(acc_ref[...], ex_refs, out_refs)

    in_specs = a_specs + [b_spec] + [pl.BlockSpec(bs, im) for _, bs, im in extras]
    in_specs += [pl.BlockSpec(memory_space=pl.ANY)] * len(after)
    return pl.pallas_call(
        kern, name=name, grid=(m // tm, n // tn, nk), in_specs=in_specs,
        out_specs=[pl.BlockSpec((tm, tn), lambda i, j, kk: (i, j)) for _ in outs],
        out_shape=[jax.ShapeDtypeStruct((m, n), dt) for dt in outs],
        scratch_shapes=[] if nk == 1 else [pltpu.VMEM((tm, tn), F32)],
        compiler_params=_cparams(("parallel", "parallel", "arbitrary")),
    )(*[p[0] for p in pieces], b, *[e[0] for e in extras], *after)


def _ici_copies(psum_ref, recv_ref, s_sem, r_sem, axis, shard_shape):
    x, y, c = _me()
    hr, cw = shard_shape[0] // 2, shard_shape[1]
    pick = lambda sems, j: sems[j] if isinstance(sems, (list, tuple)) else sems.at[j]
    copies = []
    for j, (fx, fy) in enumerate(_CHIP_FLIPS):
        chip = 2 * (x ^ fx) + (y ^ fy)
        src = psum_ref.at[:, pl.ds(chip * cw, cw)] if axis == 1 else psum_ref.at[pl.ds(chip * hr, hr), :]
        copies.append(pltpu.make_async_remote_copy(
            src_ref=src, dst_ref=recv_ref.at[j], send_sem=pick(s_sem, j), recv_sem=pick(r_sem, j),
            device_id=(x ^ fx, y ^ fy, c), device_id_type=MESH))
    return copies


_HBM_SPEC = pl.BlockSpec(memory_space=pltpu.HBM)
_SEM_SPEC = pl.BlockSpec(memory_space=pltpu.SEMAPHORE)


def _split_ici_copies(names, p_refs, land_refs, sems):
    copies = []
    for i, n in enumerate(names):
        copies += _ici_copies(p_refs[i], land_refs[i], list(sems[6 * i:6 * i + 3]),
                              list(sems[6 * i + 3:6 * i + 6]), dict(BIG)[n], SHARD[n])
    return copies


def _ici_start(name, names, psums):
    nw, ns = len(names), 6 * len(names)
    lands = [lax.empty((3, SHARD[n][0] // 2, SHARD[n][1]), BF16) for n in names]

    def body(*refs):
        for cp in _split_ici_copies(names, refs[:nw], refs[nw:2 * nw], refs[2 * nw:2 * nw + ns]):
            cp.start()
        token = refs[-1]
        token[...] = jnp.zeros_like(token)

    res = pl.pallas_call(
        body, name=name,
        out_shape=(pltpu.SemaphoreType.DMA(()),) * ns
        + tuple(pltpu.HBM(a.shape, BF16) for a in list(psums) + lands)
        + (jax.ShapeDtypeStruct((8, 128), F32),),
        in_specs=(_HBM_SPEC,) * (2 * nw),
        out_specs=(_SEM_SPEC,) * ns + (_HBM_SPEC,) * (2 * nw) + (pl.BlockSpec(memory_space=pltpu.VMEM),),
        input_output_aliases={k: ns + k for k in range(2 * nw)},
        compiler_params=pltpu.CompilerParams(has_side_effects=pltpu.SideEffectType.DATAFLOW_SIDE_EFFECTING),
    )(*[pltpu.with_memory_space_constraint(a, pltpu.HBM) for a in list(psums) + lands])
    return res[:ns], res[ns:ns + nw], res[ns + nw:ns + 2 * nw], res[-1]


def _ici_wait(name, names, sems, p_thru, land_thru, after):
    nw, ns = len(names), 6 * len(names)

    def body(*refs):
        for cp in _split_ici_copies(names, refs[:nw], refs[nw:2 * nw], refs[2 * nw:2 * nw + ns]):
            cp.wait_send()
            cp.wait_recv()

    res = pl.pallas_call(
        body, name=name,
        out_shape=tuple(pltpu.HBM(a.shape, BF16) for a in list(p_thru) + list(land_thru)),
        in_specs=(_HBM_SPEC,) * (2 * nw) + (_SEM_SPEC,) * ns + (pl.BlockSpec(memory_space=pl.ANY),) * len(after),
        out_specs=(_HBM_SPEC,) * (2 * nw), input_output_aliases={k: k for k in range(2 * nw)},
        compiler_params=pltpu.CompilerParams(has_side_effects=pltpu.SideEffectType.DATAFLOW_SIDE_EFFECTING),
    )(*p_thru, *land_thru, *sems, *after)
    return res[:nw], res[nw:]


def _where_am_i():
    x, y, c = _me()
    return jnp.stack([c, 2 * x + y]).astype(I32)


def _sibling():
    x, y, c = _me()
    return (x, y, 1 - c)


N_SEND_SLOTS = 2


def _matmul_tn_pair(name, pos, a, b, m, n, k, shard_rows, *, tm, tn, tk):
    hr = shard_rows // 2
    tm, tn, tk = min(tm, hr), min(tn, n), min(tk, k)
    tph = hr // tm
    nt, nj, nk = (m // 2) // tm, n // tn, k // tk
    n_tiles = nt * nj

    def row_block(p, t, pos_ref):
        half = jnp.where(p == 0, 1 - pos_ref[0], pos_ref[0])
        return (t // tph) * (2 * tph) + half * tph + t % tph

    pieces = b if isinstance(b, list) else [(b, False)]
    starts = _piece_starts(pieces, tn)
    n_b = len(pieces)

    def kern(pos_ref, a_ref, *rest):
        b_refs = rest[:n_b]
        o_ref, acc_ref, send_buf, land_buf, s_sem, r_sem = rest[n_b:]
        p, t, j, kk = pl.program_id(0), pl.program_id(1), pl.program_id(2), pl.program_id(3)
        idx = t * nj + j
        sib = _sibling()

        def copy(i):
            return pltpu.make_async_remote_copy(
                src_ref=send_buf.at[i % N_SEND_SLOTS], dst_ref=land_buf.at[i], send_sem=s_sem.at[i],
                recv_sem=r_sem.at[i], device_id=sib, device_id_type=MESH)

        @pl.when(kk == 0)
        def _():
            acc_ref[...] = jnp.zeros_like(acc_ref)

        for q in range(n_b):
            @pl.when(jnp.logical_and(j >= starts[q], j < starts[q] + _piece_chunks(pieces[q], tn)))
            def _(q=q):
                acc_ref[...] += lax.dot_general(a_ref[...], b_refs[q][...], _TN, preferred_element_type=F32)

        @pl.when(jnp.logical_and(kk == nk - 1, p == 0))
        def _():
            @pl.when(idx >= N_SEND_SLOTS)
            def _():
                copy(idx - N_SEND_SLOTS).wait_send()

            send_buf[idx % N_SEND_SLOTS] = acc_ref[...].astype(BF16)
            copy(idx).start()

        @pl.when(jnp.logical_and(kk == nk - 1, p == 1))
        def _():
            copy(idx).wait_recv()
            o_ref[...] = (acc_ref[...] + land_buf[idx].astype(F32)).astype(BF16)

        @pl.when(jnp.logical_and(jnp.logical_and(p == 1, idx == n_tiles - 1), kk == nk - 1))
        def _():
            for i in range(max(n_tiles - N_SEND_SLOTS, 0), n_tiles):
                copy(i).wait_send()

    grid_spec = pltpu.PrefetchScalarGridSpec(
        num_scalar_prefetch=1, grid=(2, nt, nj, nk),
        in_specs=[pl.BlockSpec((tk, tm), lambda p, t, j, kk, pos_ref: (kk, row_block(p, t, pos_ref)))]
        + [_piece_spec(pc, tk, tn, st, lambda p, t, j, kk, pos_ref: kk, lambda p, t, j, kk, pos_ref: j)
           for pc, st in zip(pieces, starts)],
        out_specs=pl.BlockSpec((tm, tn), lambda p, t, j, kk, pos_ref: (p * t, p * j)),
        scratch_shapes=[pltpu.VMEM((tm, tn), F32), pltpu.VMEM((N_SEND_SLOTS, tm, tn), BF16),
                        pltpu.VMEM((n_tiles, tm, tn), BF16),
                        pltpu.SemaphoreType.DMA((n_tiles,)), pltpu.SemaphoreType.DMA((n_tiles,))])
    return pl.pallas_call(
        kern, name=name, grid_spec=grid_spec, out_shape=jax.ShapeDtypeStruct((m // 2, n), BF16),
        compiler_params=_cparams(("arbitrary",) * 4),
    )(pos, a, *[pc[0] for pc in pieces])


def _rope_tables():
    half = RET_DK // 2
    f32 = np.float32
    inv = np.power(f32(ROPE_BASE), -np.arange(half, dtype=f32) / f32(half)).astype(f32)
    ang = (np.arange(SEQ, dtype=f32)[:, None] * inv[None, :]).astype(f32)
    return jnp.asarray(np.cos(ang).astype(f32)), jnp.asarray(np.sin(ang).astype(f32))


def _decay_tables():
    c = RET_CHUNK
    f32 = np.float32
    log_g = np.log1p(-np.power(f32(2.0), f32(-5.0) - np.arange(RET_HEADS, dtype=f32))).astype(f32)
    idx = np.arange(c, dtype=f32)
    rel = idx[:, None] - idx[None, :]
    din = np.where(rel >= 0, np.exp(log_g[:, None, None] * np.maximum(rel, f32(0.0))), f32(0.0)).astype(f32)
    qd = np.exp(log_g[:, None] * (idx + f32(1.0))).astype(f32)[:, :, None]
    kd = np.exp(log_g[:, None] * (f32(c) - f32(1.0) - idx)).astype(f32)[:, :, None]
    cd = np.exp(log_g * f32(c)).astype(f32)
    return jnp.asarray(din), jnp.asarray(qd), jnp.asarray(kd), jnp.asarray(cd)


def _t5_bucket(dist):
    max_exact = REL_BUCKETS // 2
    d_f = jnp.maximum(dist, 1).astype(F32)
    large = max_exact + (jnp.log(d_f / max_exact) / math.log(REL_MAX_DIST / max_exact)
                         * (REL_BUCKETS - max_exact)).astype(I32)
    large = jnp.minimum(large, REL_BUCKETS - 1)
    return jnp.where(dist < max_exact, dist, large)


def _bucket_tables():
    qi = jnp.arange(ATT_BLK)[:, None]
    kj = jnp.arange(2 * ATT_BLK)[None, :]
    dist = jnp.clip(ATT_BLK + qi - kj, 0, ATT_BLK)
    return jnp.stack([_t5_bucket(dist * dil) for _, dil in ATT_GROUPS]).astype(I32)


def _retention_fwd(rqk, rv, din, qd, kd, cd):
    nc = SEQ // RET_CHUNK
    c, dk, dv = RET_CHUNK, RET_DK, RET_DV

    def kern(q_ref, k_ref, v_ref, din_ref, qd_ref, kd_ref, cd_ref, o_ref, st_ref, state):
        n = pl.program_id(0)

        @pl.when(n == 0)
        def _():
            state[...] = jnp.zeros_like(state)

        for sub in range(RET_SUB):
            rows = slice(sub * c, (sub + 1) * c)
            for h in range(RET_HEADS):
                q, k = q_ref[rows, h * dk:(h + 1) * dk], k_ref[rows, h * dk:(h + 1) * dk]
                v = v_ref[rows, h * dv:(h + 1) * dv]
                s_b = state[h].astype(BF16)
                st_ref[h, sub] = s_b
                a = lax.dot_general(q, k, _NT, preferred_element_type=F32) * din_ref[h]
                o = jnp.dot(a.astype(BF16), v, preferred_element_type=F32)
                o += jnp.dot(q, s_b, preferred_element_type=F32) * qd_ref[h]
                o_ref[rows, h * dv:(h + 1) * dv] = o
                kk = (k.astype(F32) * kd_ref[h]).astype(BF16)
                state[h] = state[h] * cd_ref[h] + lax.dot_general(kk, v, _TN, preferred_element_type=F32)

    whole = lambda a: pl.BlockSpec(a.shape, lambda n: (0,) * a.ndim)
    cs = RET_SUB * c
    return pl.pallas_call(
        kern, name="retention_fwd", grid=(nc // RET_SUB,),
        in_specs=[
            pl.BlockSpec((cs, RET_QK_W), lambda n: (n, 0)),
            pl.BlockSpec((cs, RET_QK_W), lambda n: (n, 1)),
            pl.BlockSpec((cs, RET_V_W), lambda n: (n, 0)),
            whole(din), whole(qd), whole(kd),
            pl.BlockSpec(memory_space=pltpu.SMEM),
        ],
        out_specs=[
            pl.BlockSpec((cs, RET_V_W), lambda n: (n, 0)),
            pl.BlockSpec((RET_HEADS, RET_SUB, dk, dv), lambda n: (0, n, 0, 0)),
        ],
        out_shape=[
            jax.ShapeDtypeStruct((SEQ, RET_V_W), F32),
            jax.ShapeDtypeStruct((RET_HEADS, nc, dk, dv), BF16),
        ],
        scratch_shapes=[pltpu.VMEM((RET_HEADS, dk, dv), F32)],
        compiler_params=_cparams(("arbitrary",)),
    )(rqk, rqk, rv, din, qd, kd, cd)


def _retention_bwd(rqk, rv, states, d_ro, din, qd, kd, cd, cos, sin):
    nc = SEQ // RET_CHUNK
    c, dk, dv = RET_CHUNK, RET_DK, RET_DV
    half = dk // 2
    last = nc // RET_SUB - 1

    def unrot(g, cs, sn):
        g1, g2 = g[:, :half], g[:, half:]
        return jnp.concatenate([g1 * cs + g2 * sn, g2 * cs - g1 * sn], axis=-1)

    def kern(q_ref, k_ref, v_ref, st_ref, do_ref, din_ref, qd_ref, kd_ref, cd_ref, cos_ref, sin_ref,
             out_ref, dstate):
        step = pl.program_id(0)

        @pl.when(step == 0)
        def _():
            dstate[...] = jnp.zeros_like(dstate)

        for sub in reversed(range(RET_SUB)):
            rows = slice(sub * c, (sub + 1) * c)
            cs, sn = cos_ref[rows, :], sin_ref[rows, :]
            for h in range(RET_HEADS):
                qk_cols, v_cols = slice(h * dk, (h + 1) * dk), slice(h * dv, (h + 1) * dv)
                q, k, v = q_ref[rows, qk_cols], k_ref[rows, qk_cols], v_ref[rows, v_cols]
                s_b = st_ref[h, sub]
                d_ob = do_ref[rows, v_cols]
                d_oq = (d_ob.astype(F32) * qd_ref[h]).astype(BF16)
                ds_b = dstate[h].astype(BF16)
                din_m = din_ref[h]
                a_b = (lax.dot_general(q, k, _NT, preferred_element_type=F32) * din_m).astype(BF16)
                kk = (k.astype(F32) * kd_ref[h]).astype(BF16)
                d_v = lax.dot_general(a_b, d_ob, _TN, preferred_element_type=F32)
                d_v += jnp.dot(kk, ds_b, preferred_element_type=F32)
                d_a = (lax.dot_general(d_ob, v, _NT, preferred_element_type=F32) * din_m).astype(BF16)
                d_q = jnp.dot(d_a, k, preferred_element_type=F32)
                d_q += lax.dot_general(d_oq, s_b, _NT, preferred_element_type=F32)
                d_k = lax.dot_general(d_a, q, _TN, preferred_element_type=F32)
                d_k += lax.dot_general(v, ds_b, _NT, preferred_element_type=F32) * kd_ref[h]
                dstate[h] = dstate[h] * cd_ref[h] + lax.dot_general(q, d_oq, _TN,
                                                                    preferred_element_type=F32)
                out_ref[rows, h * dk:(h + 1) * dk] = unrot(d_q, cs, sn).astype(BF16)
                out_ref[rows, RET_QK_W + h * dk:RET_QK_W + (h + 1) * dk] = (
                    unrot(d_k, cs, sn) * (RET_DK ** -0.5)).astype(BF16)
                out_ref[rows, 2 * RET_QK_W + h * dv:2 * RET_QK_W + (h + 1) * dv] = d_v.astype(BF16)

    whole = lambda a: pl.BlockSpec(a.shape, lambda n: (0,) * a.ndim)
    rs = RET_SUB * c
    return pl.pallas_call(
        kern, name="retention_bwd", grid=(nc // RET_SUB,),
        in_specs=[
            pl.BlockSpec((rs, RET_QK_W), lambda n: (last - n, 0)),
            pl.BlockSpec((rs, RET_QK_W), lambda n: (last - n, 1)),
            pl.BlockSpec((rs, RET_V_W), lambda n: (last - n, 0)),
            pl.BlockSpec((RET_HEADS, RET_SUB, dk, dv), lambda n: (0, last - n, 0, 0)),
            pl.BlockSpec((rs, RET_V_W), lambda n: (last - n, 0)),
            whole(din), whole(qd), whole(kd),
            pl.BlockSpec(memory_space=pltpu.SMEM),
            pl.BlockSpec((rs, half), lambda n: (last - n, 0)),
            pl.BlockSpec((rs, half), lambda n: (last - n, 0)),
        ],
        out_specs=pl.BlockSpec((rs, 2 * RET_QK_W + RET_V_W), lambda n: (last - n, 0)),
        out_shape=jax.ShapeDtypeStruct((SEQ, 2 * RET_QK_W + RET_V_W), BF16),
        scratch_shapes=[pltpu.VMEM((RET_HEADS, dk, dv), F32)],
        compiler_params=_cparams(("arbitrary",)),
    )(rqk, rqk, rv, states, d_ro, din, qd, kd, cd, cos, sin)


def _bias_build(rel_bias, buckets):
    ng = len(ATT_GROUPS)

    def kern(tab_ref, bkt_ref, o_ref):
        g, h = pl.program_id(0), pl.program_id(1)
        bkt = bkt_ref[...]
        acc = jnp.zeros(bkt.shape, F32)
        for b in range(REL_BUCKETS):
            acc = jnp.where(bkt == b, tab_ref[b, g * ATT_HPG + h], acc)
        o_ref[...] = acc

    return pl.pallas_call(
        kern, name="bias_build", grid=(ng, ATT_HPG),
        in_specs=[pl.BlockSpec(memory_space=pltpu.SMEM),
                  pl.BlockSpec((None, ATT_BLK, 2 * ATT_BLK), lambda g, h: (g, 0, 0))],
        out_specs=pl.BlockSpec((None, None, ATT_BLK, 2 * ATT_BLK), lambda g, h: (g, h, 0, 0)),
        out_shape=jax.ShapeDtypeStruct((ng, ATT_HPG, ATT_BLK, 2 * ATT_BLK), F32),
        compiler_params=_cparams(("arbitrary", "arbitrary")),
    )(rel_bias, buckets)


def _bias_grad(dsb, buckets):
    ng = len(ATT_GROUPS)

    def kern(ds_ref, bkt_ref, o_ref):
        g, h = pl.program_id(0), pl.program_id(1)
        bkt, ds = bkt_ref[...], ds_ref[...]
        for b in range(REL_BUCKETS):
            o_ref[b, g * ATT_HPG + h] = jnp.sum(jnp.where(bkt == b, ds, 0.0))

    return pl.pallas_call(
        kern, name="bias_grad", grid=(ng, ATT_HPG),
        in_specs=[pl.BlockSpec((None, None, ATT_BLK, 2 * ATT_BLK), lambda g, h: (g, h, 0, 0)),
                  pl.BlockSpec((None, ATT_BLK, 2 * ATT_BLK), lambda g, h: (g, 0, 0))],
        out_specs=pl.BlockSpec(memory_space=pltpu.SMEM),
        out_shape=jax.ShapeDtypeStruct((REL_BUCKETS, N_ATT_HEADS), F32),
        compiler_params=_cparams(("arbitrary", "arbitrary")),
    )(dsb, buckets)


_NT = (((1,), (1,)), ((), ()))
_TN = (((0,), (0,)), ((), ()))
_ATT_SCALE = ATT_DH ** -0.5


def _window_mask(has_prev):
    qi = lax.broadcasted_iota(I32, (ATT_BLK, 2 * ATT_BLK), 0)
    kj = lax.broadcasted_iota(I32, (ATT_BLK, 2 * ATT_BLK), 1)
    prev_ok = jnp.logical_and(jnp.logical_and(kj < ATT_BLK, kj >= qi), has_prev)
    return jnp.logical_or(prev_ok, jnp.logical_and(kj >= ATT_BLK, qi >= kj - ATT_BLK))


def _head_specs(col0):
    return pl.BlockSpec((SEQ, ATT_DH), lambda h: (0, col0 + h))


def _sub_rows(start, size, dil):
    return pl.ds(start, size) if dil == 1 else pl.ds(start, size, stride=dil)


def _att_blocks(dil):
    nb = SEQ // dil // ATT_BLK
    return [(r + dil * n * ATT_BLK, n > 0, n + 1 < nb) for r in range(dil) for n in range(nb)]


def _att_fwd(gi, dil, qkv, bias):
    blk, dh = ATT_BLK, ATT_DH
    pad = dil * blk
    col0 = 3 * ATT_HPG * gi

    def kern(q_ref, k_ref, v_ref, b_ref, o_ref, l_ref, qf, kpad, vpad):
        zero = jnp.zeros((pad, dh), F32)
        kpad[0:pad, :] = zero
        vpad[0:pad, :] = zero
        kpad[pad:, :] = k_ref[...].astype(F32)
        vpad[pad:, :] = v_ref[...].astype(F32)
        qf[...] = q_ref[...].astype(F32)
        bias_m = b_ref[...]
        for start, has_prev, _ in _att_blocks(dil):
            rows, window = _sub_rows(start, blk, dil), _sub_rows(start, 2 * blk, dil)
            q = qf[rows, :].astype(BF16)
            kw, vw = kpad[window, :].astype(BF16), vpad[window, :].astype(BF16)
            valid = _window_mask(has_prev)
            s = lax.dot_general(q, kw, _NT, preferred_element_type=F32) * _ATT_SCALE + bias_m
            s = jnp.where(valid, s, -1e30)
            mx = jnp.max(s, axis=-1, keepdims=True)
            e = jnp.exp(s - mx)
            den = jnp.sum(e, axis=-1, keepdims=True)
            o_ref[rows, :] = jnp.dot((e / den).astype(BF16), vw, preferred_element_type=F32)
            l_ref[rows, :] = jnp.broadcast_to(mx + jnp.log(den), (blk, dh))

    return pl.pallas_call(
        kern, name=f"att_fwd_g{gi}", grid=(ATT_HPG,),
        in_specs=[_head_specs(col0), _head_specs(col0 + ATT_HPG), _head_specs(col0 + 2 * ATT_HPG),
                  pl.BlockSpec((None, None, blk, 2 * blk), lambda h: (gi, h, 0, 0))],
        out_specs=[_head_specs(0), _head_specs(0)],
        out_shape=[jax.ShapeDtypeStruct((SEQ, ATT_W), F32), jax.ShapeDtypeStruct((SEQ, ATT_W), F32)],
        scratch_shapes=[pltpu.VMEM((SEQ, dh), F32), pltpu.VMEM((SEQ + pad, dh), F32),
                        pltpu.VMEM((SEQ + pad, dh), F32)],
        compiler_params=_cparams(("arbitrary",)),
    )(qkv, qkv, qkv, bias)


def _att_bwd(gi, dil, qkv, d_att, lse, dd, bias):
    blk, dh = ATT_BLK, ATT_DH
    pad = dil * blk
    col0 = 3 * ATT_HPG * gi

    def kern(q_ref, k_ref, v_ref, do_ref, l_ref, d_ref, b_ref, dqkv_ref, dsb_ref,
             qf, kpad, vpad, dq_s, dkpad, dvpad):
        zero = jnp.zeros((pad, dh), F32)
        kpad[0:pad, :] = zero
        vpad[0:pad, :] = zero
        kpad[pad:, :] = k_ref[...].astype(F32)
        vpad[pad:, :] = v_ref[...].astype(F32)
        qf[...] = q_ref[...].astype(F32)
        dkpad[...] = jnp.zeros_like(dkpad)
        dvpad[...] = jnp.zeros_like(dvpad)
        bias_m = b_ref[...]
        ds_sum = jnp.zeros((blk, 2 * blk), F32)

        for start, has_prev, _ in _att_blocks(dil):
            rows, window = _sub_rows(start, blk, dil), _sub_rows(start, 2 * blk, dil)
            q, d_o = qf[rows, :].astype(BF16), do_ref[rows, :].astype(BF16)
            kw, vw = kpad[window, :].astype(BF16), vpad[window, :].astype(BF16)
            lrow, drow = l_ref[rows, :][:, :1], d_ref[rows, :][:, :1]
            valid = _window_mask(has_prev)
            s = lax.dot_general(q, kw, _NT, preferred_element_type=F32) * _ATT_SCALE + bias_m
            p = jnp.where(valid, jnp.exp(jnp.where(valid, s, -1e30) - lrow), 0.0)
            dp = lax.dot_general(d_o, vw, _NT, preferred_element_type=F32)
            ds = p * (dp - drow)
            ds_b = ds.astype(BF16)
            dq_s[rows, :] = jnp.dot(ds_b, kw, preferred_element_type=F32) * _ATT_SCALE
            dkpad[window, :] += lax.dot_general(ds_b, q, _TN, preferred_element_type=F32) * _ATT_SCALE
            dvpad[window, :] += lax.dot_general(p.astype(BF16), d_o, _TN, preferred_element_type=F32)
            ds_sum = ds_sum + ds
        dsb_ref[...] = ds_sum

        dqkv_ref[0] = dq_s[...].astype(BF16)
        dqkv_ref[1] = dkpad[pad:, :].astype(BF16)
        dqkv_ref[2] = dvpad[pad:, :].astype(BF16)

    return pl.pallas_call(
        kern, name=f"att_bwd_g{gi}", grid=(ATT_HPG,),
        in_specs=[_head_specs(col0), _head_specs(col0 + ATT_HPG), _head_specs(col0 + 2 * ATT_HPG),
                  _head_specs(0), _head_specs(0), _head_specs(0),
                  pl.BlockSpec((None, None, blk, 2 * blk), lambda h: (gi, h, 0, 0))],
        out_specs=[pl.BlockSpec((3, SEQ, dh), lambda h: (0, 0, h)),
                   pl.BlockSpec((None, blk, 2 * blk), lambda h: (h, 0, 0))],
        out_shape=[jax.ShapeDtypeStruct((3, SEQ, ATT_W), BF16),
                   jax.ShapeDtypeStruct((ATT_HPG, blk, 2 * blk), F32)],
        scratch_shapes=[pltpu.VMEM((SEQ, dh), F32), pltpu.VMEM((SEQ + pad, dh), F32),
                        pltpu.VMEM((SEQ + pad, dh), F32), pltpu.VMEM((SEQ, dh), F32),
                        pltpu.VMEM((SEQ + pad, dh), F32), pltpu.VMEM((SEQ + pad, dh), F32)],
        compiler_params=_cparams(("arbitrary",)),
    )(qkv, qkv, qkv, d_att, lse, dd, bias)


def _rms_parts(x):
    r = lax.rsqrt(jnp.mean(x * x, axis=-1, keepdims=True) + RMS_EPS)
    return x * r, r


def _rms_bwd(d_xhat, xhat, r):
    return r * (d_xhat - xhat * jnp.mean(d_xhat * xhat, axis=-1, keepdims=True))


def _prenorm_fwd(name, x, gain, shift, scale):
    def body(xt, g, sh, sc):
        xhat, _ = _rms_parts(xt)
        return (xhat * g) * (1.0 + sc) + sh
    return _rowmap(name, body, [x], [gain, shift, scale], [(D_MODEL, BF16)])[0]


def _prenorm_bwd(name, d_h, x, gain, scale, resid, branch=None, gate=None, after=()):
    gated = branch is not None

    def body(d_ht, xt, res, *rest):
        g, sc = rest[-2 - gated], rest[-1 - gated]
        xhat, r = _rms_parts(xt)
        nrm = xhat * g
        d_n = d_ht * (1.0 + sc)
        dx = _rms_bwd(d_n * g, xhat, r) + res
        sums = (jnp.sum(d_ht, axis=0, keepdims=True), jnp.sum(d_ht * nrm, axis=0, keepdims=True),
                jnp.sum(d_n * xhat, axis=0, keepdims=True))
        if not gated:
            return (dx,) + sums
        return (dx, dx * rest[-1]) + sums + (jnp.sum(dx * rest[0], axis=0, keepdims=True),)

    return _rowmap(name, body, [d_h, x, resid] + ([branch] if gated else []),
                   [gain, scale] + ([gate] if gated else []),
                   [(D_MODEL, F32)] + ([(D_MODEL, BF16)] if gated else []),
                   [D_MODEL] * (3 + gated), after=after)


def _gn_parts(ro):
    mu = jnp.mean(ro, axis=-1, keepdims=True)
    cen = ro - mu
    rstd = lax.rsqrt(jnp.mean(cen * cen, axis=-1, keepdims=True) + GN_EPS)
    return cen * rstd, rstd


def _retpost_fwd(ro, rg, gn_g, gn_b, after=()):
    def body(rot, rgt, g, b):
        outs = []
        for h in range(RET_HEADS):
            sl = slice(h * RET_DV, (h + 1) * RET_DV)
            nrm, _ = _gn_parts(rot[:, sl])
            gate = rgt[:, sl].astype(F32)
            outs.append((gate * _sigmoid(gate)) * (nrm * g[:, sl] + b[:, sl]))
        return jnp.concatenate(outs, axis=-1)
    return _rowmap("retpost_fwd", body, [ro, rg], [gn_g, gn_b], [(RET_V_W, BF16)], after=after)[0]


def _retpost_bwd(d_gated, ro, rg, gn_g, gn_b):
    def body(dgt, rot, rgt, g, b):
        d_ro, d_rg, d_g, d_b = [], [], [], []
        for h in range(RET_HEADS):
            sl = slice(h * RET_DV, (h + 1) * RET_DV)
            nrm, rstd = _gn_parts(rot[:, sl])
            gate, dg = rgt[:, sl].astype(F32), dgt[:, sl].astype(F32)
            sg = _sigmoid(gate)
            ron = nrm * g[:, sl] + b[:, sl]
            d_rg.append(dg * ron * (sg * (1.0 + gate * (1.0 - sg))))
            d_ron = dg * (gate * sg)
            d_g.append(jnp.sum(d_ron * nrm, axis=0, keepdims=True))
            d_b.append(jnp.sum(d_ron, axis=0, keepdims=True))
            d_n = d_ron * g[:, sl]
            d_ro.append(rstd * (d_n - jnp.mean(d_n, axis=-1, keepdims=True)
                                - nrm * jnp.mean(d_n * nrm, axis=-1, keepdims=True)))
        cat = lambda ts: jnp.concatenate(ts, axis=-1)
        return cat(d_ro), cat(d_rg), cat(d_g), cat(d_b)
    return _rowmap("retpost_bwd", body, [d_gated, ro, rg], [gn_g, gn_b],
                   [(RET_V_W, BF16), (RET_V_W, BF16)], [RET_V_W, RET_V_W])


def _combine(os_, ls_):
    def body(o0, o1, o2, l0, l1, l2):
        mx = jnp.maximum(jnp.maximum(l0, l1), l2)
        e0, e1, e2 = jnp.exp(l0 - mx), jnp.exp(l1 - mx), jnp.exp(l2 - mx)
        den = e0 + e1 + e2
        att = (e0 / den) * o0 + (e1 / den) * o1 + (e2 / den) * o2
        return att, att, mx + jnp.log(den)
    return _rowmap("att_combine", body, list(os_) + list(ls_), [],
                   [(ATT_W, F32), (ATT_W, BF16), (ATT_W, F32)])


def _att_bwd_pre(d_att, att):
    def body(dt, at):
        outs = []
        for h in range(ATT_HPG):
            sl = slice(h * ATT_DH, (h + 1) * ATT_DH)
            outs.append(jnp.broadcast_to(jnp.sum(dt[:, sl] * at[:, sl], axis=-1, keepdims=True),
                                         (dt.shape[0], ATT_DH)))
        return jnp.concatenate(outs, axis=-1)
    return _rowmap("att_bwd_pre", body, [d_att, att], [], [(ATT_W, F32)])[0]


def _merge_fwd(gates, ret_out, att_out):
    def body(gt, ro, ao):
        gt = gt.astype(F32)
        return _sigmoid(gt[:, :D_MODEL]) * ro + _sigmoid(gt[:, D_MODEL:]) * ao
    return _rowmap("merge_fwd", body, [gates, ret_out, att_out], [], [(D_MODEL, BF16)])[0]


def _merge_bwd(d_merged, gates, ret_out, att_out):
    def body(dm, gt, ro, ao):
        dm, gt = dm.astype(F32), gt.astype(F32)
        sa, sb = _sigmoid(gt[:, :D_MODEL]), _sigmoid(gt[:, D_MODEL:])
        d_gates = jnp.concatenate([dm * ro * (sa * (1.0 - sa)), dm * ao * (sb * (1.0 - sb))], axis=-1)
        return dm * sa, dm * sb, d_gates
    return _rowmap("merge_bwd", body, [d_merged, gates, ret_out, att_out], [],
                   [(D_MODEL, BF16), (D_MODEL, BF16), (2 * D_MODEL, BF16)])


def _loss_head(x3, target, gain, branch, gate):
    def body(xt, tt, br, g, gt):
        xhat, r = _rms_parts(xt)
        err = xhat * g - tt
        d_y = err / D_MODEL
        loss = 0.5 * jnp.sum(jnp.mean(err * err, axis=-1, keepdims=True), axis=0, keepdims=True)
        d_x = _rms_bwd(d_y * g, xhat, r)
        return (d_x, d_x * gt, jnp.broadcast_to(loss, (1, 128)), jnp.sum(d_y * xhat, axis=0, keepdims=True),
                jnp.sum(d_x * br, axis=0, keepdims=True))
    return _rowmap("loss_head", body, [x3, target, branch], [gain, gate],
                   [(D_MODEL, F32), (D_MODEL, BF16)], [128, D_MODEL, D_MODEL])


def _local_step(pos, x, target, mod, norm1_g, norm2_g, norm_f_g, rel_bias, gn_g, gn_b, w_in, rest_gather):
    sh1, sc1, g1, sh2, sc2, g2 = [mod[:, i * D_MODEL:(i + 1) * D_MODEL] for i in range(6)]
    cos, sin = _rope_tables()
    din, qd, kd, cd = _decay_tables()
    buckets = _bucket_tables()
    bias = _bias_build(rel_bias, buckets)
    dils = [d for _, d in ATT_GROUPS]

    h1 = _prenorm_fwd("prenorm1_fwd", x, norm1_g, sh1, sc1)

    qk_tn = 2 * RET_DK

    def rot_epi(acc, cs, sn, scale):
        half = RET_DK // 2
        outs = []
        for h0 in range(0, qk_tn, RET_DK):
            x1, x2 = acc[:, h0:h0 + half], acc[:, h0 + half:h0 + RET_DK]
            outs += [x1 * cs - x2 * sn, x1 * sn + x2 * cs]
        return (jnp.concatenate(outs, axis=-1) * scale,)

    qk_scale = jnp.concatenate([jnp.ones((1, RET_QK_W), F32),
                                jnp.full((1, RET_QK_W), RET_DK ** -0.5, F32)], axis=-1)
    rope_ex = [(cos, (TM, RET_DK // 2), lambda i, j, kk: (i, 0)),
               (sin, (TM, RET_DK // 2), lambda i, j, kk: (i, 0)),
               (qk_scale, (1, qk_tn), lambda i, j, kk: (0, j))]
    rest_sems, rest_shards, rest_fulls, rest_token = rest_gather
    behind = [rest_token]
    rv = _matmul("proj_rv", h1, w_in, "nn", SEQ, RET_V_W, D_MODEL, [BF16], b_off=OFF_V, tk=D_MODEL,
                 after=behind)[0]
    rg = _matmul("proj_rg", h1, w_in, "nn", SEQ, RET_V_W, D_MODEL, [BF16], b_off=OFF_G, tk=D_MODEL,
                 after=behind)[0]
    gates = _matmul("proj_gates", h1, w_in, "nn", SEQ, 2 * D_MODEL, D_MODEL, [BF16], b_off=OFF_GATE,
                    tn=512, tk=D_MODEL, after=behind)[0]
    aqkv = _matmul("proj_att", h1, w_in, "nn", SEQ, 9 * ATT_W, D_MODEL, [BF16], b_off=OFF_ATT,
                   tn=512, tk=D_MODEL, after=behind)[0]

    os_, ls_ = [], []
    for gi in range(3):
        o_g, l_g = _att_fwd(gi, dils[gi], aqkv, bias)
        os_.append(o_g)
        ls_.append(l_g)

    rqk = _matmul("proj_qk", h1, w_in, "nn", SEQ, 2 * RET_QK_W, D_MODEL, [BF16], b_off=OFF_Q,
                  tn=qk_tn, tk=D_MODEL, epilogue=rot_epi, extras=rope_ex, after=behind)[0]
    ro, states = _retention_fwd(rqk, rv, din, qd, kd, cd)
    rest_sems, rest_fulls, fwd_token = _gather_rest_forward(rest_sems, rest_shards, rest_fulls,
                                                            [ro, gates] + os_)
    gated = _retpost_fwd(ro, rg, gn_g, gn_b, after=[fwd_token])
    w_ret_out, w_att_out, w_o, w_ff1, w_ff2 = _gather_rest_end(rest_sems, rest_fulls, [gated])
    ret_out = _matmul("ret_out", gated, w_ret_out, "nn", SEQ, D_MODEL, RET_V_W, [F32], tk=RET_V_W)[0]
    att, att_b, lse = _combine(os_, ls_)
    att_out = _matmul("att_out", att_b, w_att_out, "nn", SEQ, D_MODEL, ATT_W, [F32])[0]

    merged = _merge_fwd(gates, ret_out, att_out)

    def resid_epi(acc, xt, g):
        return xt + g * acc, acc

    def resid_ex(xin, g):
        return [(xin, (TM, TN), lambda i, j, kk: (i, j)), (g, (1, TN), lambda i, j, kk: (0, j))]

    x2, mix = _matmul("mix_out", merged, w_o, "nn", SEQ, D_MODEL, D_MODEL, [F32, F32],
                      epilogue=resid_epi, extras=resid_ex(x, g1))
    h2 = _prenorm_fwd("prenorm2_fwd", x2, norm2_g, sh2, sc2)

    def relu2_epi(acc):
        r = jnp.maximum(acc, 0.0)
        return r * r, r

    act, relu_u = _matmul("ff1", h2, w_ff1, "nn", SEQ, D_FF, D_MODEL, [BF16, BF16], tk=D_MODEL,
                          epilogue=relu2_epi)
    x3, y2 = _matmul("ff2", act, w_ff2, "nn", SEQ, D_MODEL, D_FF, [F32, F32], tk=2048,
                     epilogue=resid_epi, extras=resid_ex(x2, g2))

    d_x3, d_y2, loss, d_gf, d_g2 = _loss_head(x3, target, norm_f_g, y2, g2)

    def relu2_bwd_epi(acc, rt):
        return (acc * (2.0 * rt.astype(F32)),)

    gw_ff2 = _matmul_tn_pair("ff2_dw", pos, act, d_y2, D_FF, D_MODEL, SEQ, D_FF // N_CHIPS,
                             tm=512, tn=1024, tk=SEQ)
    d_u = _matmul("ff2_dx", d_y2, w_ff2, "nt", SEQ, D_FF, D_MODEL, [BF16], epilogue=relu2_bwd_epi,
                  extras=[(relu_u, (TM, TN), lambda i, j, kk: (i, j))])[0]
    gw_ff1 = _matmul_tn_pair("ff1_dw", pos, h2, d_u, D_MODEL, D_FF, SEQ, D_MODEL,
                             tm=512, tn=1024, tk=SEQ)
    ffn = ["w_ff2", "w_ff1"]
    ffn_started = _ici_start("ici_start_ffn", ffn, [gw_ff2, gw_ff1])
    d_h2 = _matmul("ff1_dx", d_u, w_ff1, "nt", SEQ, D_MODEL, D_FF, [F32], tk=2048,
                   after=[ffn_started[3]])[0]
    d_x2, d_mix, d_sh2, d_sc2, d_n2g, d_g1 = _prenorm_bwd("prenorm2_bwd", d_h2, x2, norm2_g, sc2, d_x3,
                                                          branch=mix, gate=g1)
    gw_o = _matmul_tn_pair("mix_dw", pos, merged, d_mix, D_MODEL, D_MODEL, SEQ, D_MODEL // N_CHIPS,
                           tm=128, tn=1024, tk=2048)
    d_merged = _matmul("mix_dx", d_mix, w_o, "nt", SEQ, D_MODEL, D_MODEL, [BF16])[0]
    d_ret_out, d_att_out, d_gates = _merge_bwd(d_merged, gates, ret_out, att_out)

    gw_ret_out = _matmul_tn_pair("ret_out_dw", pos, gated, d_ret_out, RET_V_W, D_MODEL, SEQ,
                                 RET_V_W // N_CHIPS, tm=256, tn=1024, tk=SEQ)
    gw_att_out = _matmul_tn_pair("att_out_dw", pos, att_b, d_att_out, ATT_W, D_MODEL, SEQ, ATT_W,
                                 tm=256, tn=1024, tk=2048)
    mixer = ["w_o", "w_ret_out", "w_att_out"]
    mixer_started = _ici_start("ici_start_mixer", mixer, [gw_o, gw_ret_out, gw_att_out])
    d_gated = _matmul("ret_out_dx", d_ret_out, w_ret_out, "nt", SEQ, RET_V_W, D_MODEL, [BF16],
                      after=[mixer_started[3]])[0]
    d_att = _matmul("att_out_dx", d_att_out, w_att_out, "nt", SEQ, ATT_W, D_MODEL, [F32],
                    after=[mixer_started[3]])[0]

    d_ro, d_rg, d_gn_g, d_gn_b = _retpost_bwd(d_gated, ro, rg, gn_g, gn_b)
    d_rqkv = _retention_bwd(rqk, rv, states, d_ro, din, qd, kd, cd, cos, sin)

    dd = _att_bwd_pre(d_att, att)
    d_aqkv, dsbs = [], []
    for gi in range(3):
        dqkv, dsb = _att_bwd(gi, dils[gi], aqkv, d_att, lse, dd, bias)
        d_aqkv.append(dqkv)
        dsbs.append(dsb)
    d_rel_bias = _bias_grad(jnp.stack(dsbs), buckets)

    d_proj = [(d_rqkv, False), (d_rg, False)] + [(t, True) for t in d_aqkv] + [(d_gates, False)]
    gw_in = _matmul_tn_pair("proj_dw", pos, h1, d_proj, D_MODEL, IN_COLS, SEQ, D_MODEL,
                            tm=512, tn=ATT_W, tk=SEQ)
    sems, (gw_in,), (land,), token = _ici_start("ici_start_w_in", ["w_in"], [gw_in])
    d_h1 = _matmul("proj_dx", d_proj, w_in, "nt", SEQ, D_MODEL, IN_COLS, [F32], tn=1024, tk=ATT_W,
                   after=[token])[0]
    pending = (sems, land)

    names = ffn + mixer
    psums, got = _ici_wait("ici_wait_rest", names, list(ffn_started[0]) + list(mixer_started[0]),
                           list(ffn_started[1]) + list(mixer_started[1]),
                           list(ffn_started[2]) + list(mixer_started[2]), [d_h1])
    g_big = {n: _final_sum("final_" + n, pos, dict(BIG)[n], psums[i], got[i], SHARD[n])
             for i, n in enumerate(names)}
    grad_x, d_sh1, d_sc1, d_n1g = _prenorm_bwd("prenorm1_bwd", d_h1, x, norm1_g, sc1, d_x2,
                                               after=list(g_big.values()))
    d_mod = jnp.concatenate([d_sh1, d_sc1, d_g1, d_sh2, d_sc2, d_g2], axis=-1)
    small = dict(norm1_g=d_n1g, norm2_g=d_n2g, norm_f_g=d_gf, gn_g=d_gn_g, gn_b=d_gn_b,
                 rel_bias=d_rel_bias)
    return loss, grad_x, d_mod, small, g_big, (gw_in,) + pending


def _me():
    return lax.axis_index("x"), lax.axis_index("y"), lax.axis_index("c")


def _peer(x, y, c, mask):
    return (x ^ ((mask >> 2) & 1), y ^ ((mask >> 1) & 1), c ^ (mask & 1))


def _gather8(src_ref, dst_ref, send_sems, recv_sems):
    x, y, c = _me()
    me = 4 * x + 2 * y + c
    copies = []
    for mask in range(1, N_DEV):
        cp = pltpu.make_async_remote_copy(
            src_ref=src_ref, dst_ref=dst_ref.at[me], send_sem=send_sems.at[mask - 1],
            recv_sem=recv_sems.at[mask - 1], device_id=_peer(x, y, c, mask), device_id_type=MESH)
        cp.start()
        copies.append(cp)
    dst_ref[me] = src_ref[...]
    for cp in copies:
        cp.wait_recv()
    for cp in copies:
        cp.wait_send()


def _ada_fwd(c_in, w_ada, b_ada):
    ncol = ADA_COLS // N_CHIPS

    def body(c_ref, w_ref, b_ref, mod_ref, sc_ref, cbuf, cg, mbuf, mg, s1, r1, s2, r2):
        x, y, c = _me()
        me = 4 * x + 2 * y + c
        cv = c_ref[...]
        cbuf[...] = jnp.broadcast_to(cv * _sigmoid(cv), cbuf.shape)
        _gather8(cbuf, cg, s1, r1)
        rows = lax.broadcasted_iota(I32, (N_DEV, D_MODEL), 0)
        sc_all = jnp.zeros((N_DEV, D_MODEL), F32)
        for d in range(N_DEV):
            sc_all = jnp.where(rows == d, cg[d], sc_all)
        sc_ref[...] = sc_all
        mbuf[...] = jnp.dot(sc_all.astype(BF16), w_ref[...].astype(BF16), preferred_element_type=F32)
        _gather8(mbuf, mg, s2, r2)
        rowsel = lax.broadcasted_iota(I32, (N_DEV, ncol), 0) == me
        for k in range(N_CHIPS):
            blk = mg[2 * k]
            row = jnp.sum(jnp.where(rowsel, blk, 0.0), axis=0, keepdims=True)
            mod_ref[:, k * ncol:(k + 1) * ncol] = row + b_ref[:, k * ncol:(k + 1) * ncol]

    vm = pl.BlockSpec(memory_space=pltpu.VMEM)
    return pl.pallas_call(
        body, name="ada_fwd",
        in_specs=[vm, vm, vm], out_specs=[vm, vm],
        out_shape=[jax.ShapeDtypeStruct((1, ADA_COLS), F32), jax.ShapeDtypeStruct((N_DEV, D_MODEL), F32)],
        scratch_shapes=[
            pltpu.VMEM((8, D_MODEL), F32), pltpu.VMEM((N_DEV, 8, D_MODEL), F32),
            pltpu.VMEM((8, ncol), F32), pltpu.VMEM((N_DEV, 8, ncol), F32),
            pltpu.SemaphoreType.DMA((N_DEV - 1,)), pltpu.SemaphoreType.DMA((N_DEV - 1,)),
            pltpu.SemaphoreType.DMA((N_DEV - 1,)), pltpu.SemaphoreType.DMA((N_DEV - 1,)),
        ],
        compiler_params=pltpu.CompilerParams(vmem_limit_bytes=VMEM_LIMIT_V7X),
    )(c_in, w_ada, b_ada)


def _small_reduce(pack, sc_all):
    ncol = ADA_COLS // N_CHIPS

    def body(p_ref, sc_ref, tot_ref, gw_ref, pg, s1, r1):
        x, y, _ = _me()
        chip = 2 * x + y
        _gather8(p_ref, pg, s1, r1)
        tot = pg[0]
        for d in range(1, N_DEV):
            tot = tot + pg[d]
        tot_ref[...] = tot
        rows = lax.broadcasted_iota(I32, (N_DEV, ncol), 0)
        dmod = jnp.zeros((N_DEV, ncol), F32)
        for k in range(N_CHIPS):
            part = jnp.zeros((N_DEV, ncol), F32)
            for d in range(N_DEV):
                part = jnp.where(rows == d, pg[d, :, k * ncol:(k + 1) * ncol][0:1, :], part)
            dmod = jnp.where(chip == k, part, dmod)
        gw_ref[...] = lax.dot_general(sc_ref[...].astype(BF16), dmod.astype(BF16), _TN,
                                      preferred_element_type=F32)

    vm = pl.BlockSpec(memory_space=pltpu.VMEM)
    return pl.pallas_call(
        body, name="small_reduce",
        in_specs=[vm, vm], out_specs=[vm, vm],
        out_shape=[jax.ShapeDtypeStruct((8, ADA_COLS), F32), jax.ShapeDtypeStruct((D_MODEL, ncol), F32)],
        scratch_shapes=[pltpu.VMEM((N_DEV, 8, ADA_COLS), F32),
                        pltpu.SemaphoreType.DMA((N_DEV - 1,)), pltpu.SemaphoreType.DMA((N_DEV - 1,))],
        compiler_params=pltpu.CompilerParams(vmem_limit_bytes=VMEM_LIMIT_V7X),
    )(pack, sc_all)


BIG = (("w_in", 1), ("w_ret_out", 0), ("w_att_out", 1), ("w_o", 0), ("w_ff1", 1), ("w_ff2", 0))
SHARD = {"w_in": (D_MODEL, IN_COLS // N_CHIPS), "w_ret_out": (RET_V_W // N_CHIPS, D_MODEL),
         "w_att_out": (ATT_W, D_MODEL // N_CHIPS), "w_o": (D_MODEL // N_CHIPS, D_MODEL),
         "w_ff1": (D_MODEL, D_FF // N_CHIPS), "w_ff2": (D_FF // N_CHIPS, D_MODEL)}
_CHIP_FLIPS = ((1, 0), (0, 1), (1, 1))


def _region(ref, axis, chip, half, shard_shape):
    r, cw = shard_shape
    hr = r // 2
    if axis == 1:
        return ref.at[pl.ds(half * hr, hr), pl.ds(chip * cw, cw)]
    return ref.at[pl.ds(chip * r + half * hr, hr), :]


def _gather_weights(shards, n_remote):
    nw = len(BIG)
    shapes = [s.shape for s in shards]
    full_shapes = [(r, N_CHIPS * cw) if ax == 1 else (N_CHIPS * r, cw)
                   for (r, cw), (_, ax) in zip(shapes, BIG)]

    def body(*refs):
        ins, outs = refs[:nw], refs[nw:2 * nw]
        own = refs[2 * nw:3 * nw]
        from_ici, from_sib = refs[3 * nw:3 * nw + n_remote], refs[3 * nw + n_remote:3 * nw + 2 * n_remote]
        ld_sem, st_sem, s_ici, r_ici, s_d2d, r_d2d, st_a, st_b = refs[3 * nw + 2 * n_remote:]
        x, y, c = _me()
        chip = 2 * x + y
        sib = (x, y, 1 - c)
        loads = [pltpu.make_async_copy(ins[i], own[i], ld_sem.at[i]) for i in range(nw)]
        for cp in loads:
            cp.start()
        pending, first = [], []
        for i, (_, ax) in enumerate(BIG):
            r, cw = shapes[i]
            hr = r // 2
            loads[i].wait()
            dst = outs[i].at[:, pl.ds(chip * cw, cw)] if ax == 1 else outs[i].at[pl.ds(chip * r, r), :]
            cp = pltpu.make_async_copy(own[i], dst, st_sem.at[i])
            cp.start()
            pending.append(cp)
            for j, (fx, fy) in enumerate(_CHIP_FLIPS if i < n_remote else ()):
                rc = pltpu.make_async_remote_copy(
                    src_ref=own[i].at[pl.ds(c * hr, hr), :], dst_ref=from_ici[i].at[j],
                    send_sem=s_ici.at[j * nw + i], recv_sem=r_ici.at[j * nw + i],
                    device_id=(x ^ fx, y ^ fy, c), device_id_type=MESH)
                rc.start()
                first.append((j, i, rc))
        passed = []
        for j, i, rc in first:
            fx, fy = _CHIP_FLIPS[j]
            src_chip = 2 * (x ^ fx) + (y ^ fy)
            ax = BIG[i][1]
            rc.wait_recv()
            fw = pltpu.make_async_remote_copy(
                src_ref=from_ici[i].at[j], dst_ref=from_sib[i].at[j], send_sem=s_d2d.at[j * nw + i],
                recv_sem=r_d2d.at[j * nw + i], device_id=sib, device_id_type=MESH)
            fw.start()
            passed.append((j, i, src_chip, fw))
            st = pltpu.make_async_copy(from_ici[i].at[j], _region(outs[i], ax, src_chip, c, shapes[i]),
                                       st_a.at[j * nw + i])
            st.start()
            pending.append(st)
        for j, i, src_chip, fw in passed:
            fw.wait_recv()
            st = pltpu.make_async_copy(from_sib[i].at[j],
                                       _region(outs[i], BIG[i][1], src_chip, 1 - c, shapes[i]),
                                       st_b.at[j * nw + i])
            st.start()
            pending.append(st)
        for _, _, rc in first:
            rc.wait_send()
        for _, _, _, fw in passed:
            fw.wait_send()
        for cp in pending:
            cp.wait()

    hbm = pl.BlockSpec(memory_space=pl.ANY)
    halves = [pltpu.VMEM((3, r // 2, cw), BF16) for r, cw in shapes[:n_remote]]
    return pl.pallas_call(
        body, name="gather_weights",
        in_specs=[hbm] * nw, out_specs=[hbm] * nw,
        out_shape=[jax.ShapeDtypeStruct(fs, BF16) for fs in full_shapes],
        scratch_shapes=[pltpu.VMEM(sh, BF16) for sh in shapes] + halves + halves
        + [pltpu.SemaphoreType.DMA((nw,)), pltpu.SemaphoreType.DMA((nw,))]
        + [pltpu.SemaphoreType.DMA((3 * nw,))] * 6,
        compiler_params=pltpu.CompilerParams(vmem_limit_bytes=VMEM_LIMIT_V7X),
    )(*shards)


REST = BIG[1:]
_SIDE_EFFECTS = pltpu.CompilerParams(has_side_effects=pltpu.SideEffectType.DATAFLOW_SIDE_EFFECTING)
_ANY_SPEC = pl.BlockSpec(memory_space=pl.ANY)


def _rest_ici_copies(shard_refs, full_refs, sems):
    x, y, c = _me()
    chip = 2 * x + y
    n = 3 * len(REST)
    copies = []
    for i, (name, ax) in enumerate(REST):
        hr = SHARD[name][0] // 2
        for j, (fx, fy) in enumerate(_CHIP_FLIPS):
            copies.append(pltpu.make_async_remote_copy(
                src_ref=shard_refs[i].at[pl.ds(c * hr, hr), :],
                dst_ref=_region(full_refs[i], ax, chip, c, SHARD[name]),
                send_sem=sems[3 * i + j], recv_sem=sems[n + 3 * i + j],
                device_id=(x ^ fx, y ^ fy, c), device_id_type=MESH))
    return copies


def _rest_d2d_copies(full_refs, sems):
    x, y, c = _me()
    n = 3 * len(REST)
    copies = []
    for i, (name, ax) in enumerate(REST):
        for j, (fx, fy) in enumerate(_CHIP_FLIPS):
            reg = _region(full_refs[i], ax, 2 * (x ^ fx) + (y ^ fy), c, SHARD[name])
            copies.append(pltpu.make_async_remote_copy(
                src_ref=reg, dst_ref=reg, send_sem=sems[3 * i + j], recv_sem=sems[n + 3 * i + j],
                device_id=(x, y, 1 - c), device_id_type=MESH))
    return copies


def _gather_rest_start(shards, fulls, after):
    nr, ns, na = len(REST), 6 * len(REST), len(after)

    def body(*refs):
        for cp in _rest_ici_copies(refs[:nr], refs[nr:2 * nr], refs[2 * nr + na:2 * nr + na + ns]):
            cp.start()
        token = refs[-1]
        token[...] = jnp.zeros_like(token)

    hbm = lambda a: pltpu.HBM(a.shape, a.dtype)
    res = pl.pallas_call(
        body, name="gather_rest_start",
        out_shape=(pltpu.SemaphoreType.DMA(()),) * ns + tuple(hbm(a) for a in shards + fulls)
        + (jax.ShapeDtypeStruct((8, 128), F32),),
        in_specs=(_HBM_SPEC,) * (2 * nr) + (_ANY_SPEC,) * na,
        out_specs=(_SEM_SPEC,) * ns + (_HBM_SPEC,) * (2 * nr) + (pl.BlockSpec(memory_space=pltpu.VMEM),),
        input_output_aliases={k: ns + k for k in range(2 * nr)}, compiler_params=_SIDE_EFFECTS,
    )(*[pltpu.with_memory_space_constraint(a, pltpu.HBM) for a in shards + fulls], *after)
    return res[:ns], res[ns:ns + nr], res[ns + nr:ns + 2 * nr], res[-1]


def _gather_rest_forward(sems, shards, fulls, after):
    nr, ns = len(REST), 6 * len(REST)

    def body(*refs):
        shard_refs, full_refs, old = refs[:nr], refs[nr:2 * nr], refs[2 * nr:2 * nr + ns]
        new = refs[2 * nr + ns + len(after):2 * nr + 2 * ns + len(after)]
        for cp in _rest_ici_copies(shard_refs, full_refs, old):
            cp.wait_send()
            cp.wait_recv()
        for cp in _rest_d2d_copies(full_refs, new):
            cp.start()
        token = refs[-1]
        token[...] = jnp.zeros_like(token)

    res = pl.pallas_call(
        body, name="gather_rest_forward",
        out_shape=(pltpu.SemaphoreType.DMA(()),) * ns + tuple(pltpu.HBM(a.shape, a.dtype) for a in fulls)
        + (jax.ShapeDtypeStruct((8, 128), F32),),
        in_specs=(_HBM_SPEC,) * (2 * nr) + (_SEM_SPEC,) * ns + (_ANY_SPEC,) * len(after),
        out_specs=(_SEM_SPEC,) * ns + (_HBM_SPEC,) * nr + (pl.BlockSpec(memory_space=pltpu.VMEM),),
        input_output_aliases={nr + k: ns + k for k in range(nr)}, compiler_params=_SIDE_EFFECTS,
    )(*shards, *fulls, *sems, *after)
    return res[:ns], res[ns:ns + nr], res[-1]


def _gather_rest_end(sems, fulls, after):
    nr, ns = len(REST), 6 * len(REST)

    def body(*refs):
        for cp in _rest_d2d_copies(refs[:nr], refs[nr:nr + ns]):
            cp.wait_send()
            cp.wait_recv()

    return pl.pallas_call(
        body, name="gather_rest_end",
        out_shape=tuple(pltpu.HBM(a.shape, a.dtype) for a in fulls),
        in_specs=(_HBM_SPEC,) * nr + (_SEM_SPEC,) * ns + (_ANY_SPEC,) * len(after),
        out_specs=(_HBM_SPEC,) * nr,
        input_output_aliases={k: k for k in range(nr)}, compiler_params=_SIDE_EFFECTS,
    )(*fulls, *sems, *after)


def _adam_update(w, g, m, v):
    mn = ADAM_B1 * m + (1.0 - ADAM_B1) * g
    vn = ADAM_B2 * v + (1.0 - ADAM_B2) * (g * g)
    m_hat = mn / (1.0 - ADAM_B1 ** ADAM_STEP)
    v_hat = vn / (1.0 - ADAM_B2 ** ADAM_STEP)
    return -ADAM_LR * (m_hat / (jnp.sqrt(v_hat) + ADAM_EPS) + ADAM_WD * w), mn, vn


def _final_sum(name, pos, axis, psum, recv, shard_shape, after=(), tr=128):
    r, cw = shard_shape
    hr = r // 2
    tr = min(tr, hr)
    nt = hr // tr
    n_after = len(after)

    def kern(pos_ref, p_ref, r_ref, *rest):
        g_ref, send_buf, land_buf, s_sem, r_sem = rest[n_after:]
        p, t = pl.program_id(0), pl.program_id(1)
        sib = _sibling()

        def copy(i):
            return pltpu.make_async_remote_copy(
                src_ref=send_buf.at[i], dst_ref=land_buf.at[i], send_sem=s_sem.at[i],
                recv_sem=r_sem.at[i], device_id=sib, device_id_type=MESH)

        @pl.when(p == 0)
        def _():
            tot = p_ref[...].astype(F32)
            for j in range(3):
                tot = tot + r_ref[j].astype(F32)
            send_buf[t] = tot
            copy(t).start()
            g_ref[...] = tot

        @pl.when(p == 1)
        def _():
            copy(t).wait_recv()
            g_ref[...] = land_buf[t]

        @pl.when(jnp.logical_and(p == 1, t == nt - 1))
        def _():
            for i in range(nt):
                copy(i).wait_send()

    def shard_rows(p, t, pos_ref):
        return (jnp.where(p == 0, pos_ref[0], 1 - pos_ref[0]) * nt + t, 0)

    def own_part(p, t, pos_ref):
        tt = jnp.where(p == 0, t, nt - 1)
        return (tt, pos_ref[1]) if axis == 1 else (pos_ref[1] * nt + tt, 0)

    grid_spec = pltpu.PrefetchScalarGridSpec(
        num_scalar_prefetch=1, grid=(2, nt),
        in_specs=[pl.BlockSpec((tr, cw), own_part),
                  pl.BlockSpec((3, tr, cw), lambda p, t, pos_ref: (0, jnp.where(p == 0, t, nt - 1), 0))]
        + [pl.BlockSpec(memory_space=pl.ANY)] * n_after,
        out_specs=pl.BlockSpec((tr, cw), shard_rows),
        scratch_shapes=[pltpu.VMEM((nt, tr, cw), F32), pltpu.VMEM((nt, tr, cw), F32),
                        pltpu.SemaphoreType.DMA((nt,)), pltpu.SemaphoreType.DMA((nt,))])
    return pl.pallas_call(
        kern, name=name, grid_spec=grid_spec, out_shape=jax.ShapeDtypeStruct((r, cw), F32),
        compiler_params=_cparams(("arbitrary", "arbitrary")),
    )(pos, psum, recv, *after)


def _adamw(name, w, g, m, v):
    r, cw = w.shape
    tr = min(r, 128)

    def kern(w_ref, g_ref, m_ref, v_ref, go_ref, d_ref, nm_ref, nv_ref):
        gv = g_ref[...]
        go_ref[...] = gv
        d_ref[...], nm_ref[...], nv_ref[...] = _adam_update(w_ref[...], gv, m_ref[...], v_ref[...])

    spec = pl.BlockSpec((tr, cw), lambda i: (i, 0))
    return pl.pallas_call(
        kern, name=name, grid=(r // tr,), in_specs=[spec] * 4, out_specs=[spec] * 4,
        out_shape=[jax.ShapeDtypeStruct((r, cw), F32)] * 4, compiler_params=_cparams(("parallel",)),
    )(w, g, m, v)


_PACK_W = ADA_COLS
_NB = REL_BUCKETS * N_ATT_HEADS
_SMALL_SLOTS = {
    "b_ada": (0, 0, ADA_COLS),
    "norm1_g": (1, 0, D_MODEL), "norm2_g": (1, D_MODEL, D_MODEL), "norm_f_g": (1, 2 * D_MODEL, D_MODEL),
    "ret_gn_g": (1, 3 * D_MODEL, RET_V_W),
    "ret_gn_b": (2, 0, RET_V_W), "rel_bias": (2, RET_V_W, _NB), "loss": (2, RET_V_W + 512, 128),
}


def _pack_small(vals):
    rows = []
    for r in range(8):
        items = sorted([(off, n) for n, (rr, off, _) in _SMALL_SLOTS.items() if rr == r and n in vals])
        parts, pos = [], 0
        for off, n in items:
            if off > pos:
                parts.append(jnp.zeros((1, off - pos), F32))
            parts.append(vals[n].reshape(1, -1).astype(F32))
            pos = off + _SMALL_SLOTS[n][2]
        if pos < _PACK_W:
            parts.append(jnp.zeros((1, _PACK_W - pos), F32))
        rows.append(jnp.concatenate(parts, axis=-1))
    return jnp.concatenate(rows, axis=0)


def _unpack_small(pack, name):
    r, off, wd = _SMALL_SLOTS[name]
    return pack[r:r + 1, off:off + wd]


def kernel(x, c, w_ada, b_ada, norm1_g, w_in, rel_bias, ret_gn_g, ret_gn_b, w_ret_out, w_att_out, w_o, norm2_g, w_ff1, w_ff2, norm_f_g, loss_target, m_w_ada, m_b_ada, m_norm1_g, m_w_in, m_rel_bias, m_ret_gn_g, m_ret_gn_b, m_w_ret_out, m_w_att_out, m_w_o, m_norm2_g, m_w_ff1, m_w_ff2, m_norm_f_g, v_w_ada, v_b_ada, v_norm1_g, v_w_in, v_rel_bias, v_ret_gn_g, v_ret_gn_b, v_w_ret_out, v_w_att_out, v_w_o, v_norm2_g, v_w_ff1, v_w_ff2, v_norm_f_g):
    given = dict(locals())
    big_names = [n for n, _ in BIG]
    shard_w = {n: given[n][0] for n in big_names}
    assert all(shard_w[n].shape == SHARD[n] for n in big_names)

    shards_bf = [shard_w[n].astype(BF16) for n in big_names]
    full = _gather_weights(shards_bf, 1)
    mod, sc_all = _ada_fwd(c, w_ada[0], b_ada)
    rest_gather = _gather_rest_start(shards_bf[1:], list(full[1:]), [mod])
    pos = _where_am_i()

    loss, grad_x, d_mod, small, g_big, pending = _local_step(
        pos, x[0], loss_target[0], mod, norm1_g, norm2_g, norm_f_g.reshape(1, -1), rel_bias, ret_gn_g,
        ret_gn_b, full[0], rest_gather)

    pack_g = _pack_small(dict(b_ada=d_mod, norm1_g=small["norm1_g"], norm2_g=small["norm2_g"],
                              norm_f_g=small["norm_f_g"], ret_gn_g=small["gn_g"], ret_gn_b=small["gn_b"],
                              rel_bias=small["rel_bias"], loss=loss))
    tot, g_w_ada = _small_reduce(pack_g, sc_all)

    small_names = ["b_ada", "norm1_g", "rel_bias", "ret_gn_g", "ret_gn_b", "norm2_g", "norm_f_g"]
    pack_w = _pack_small({n: given[n] for n in small_names})
    pack_m = _pack_small({n: given["m_" + n] for n in small_names})
    pack_v = _pack_small({n: given["v_" + n] for n in small_names})
    _, sd, sm, sv = _adamw("adamw_small", pack_w, tot, pack_m, pack_v)

    grads, deltas, new_m, new_v = {}, {}, {}, {}
    for n in small_names:
        shp = given[n].shape
        grads[n] = _unpack_small(tot, n).reshape(shp)
        deltas[n] = _unpack_small(sd, n).reshape(shp)
        new_m[n] = _unpack_small(sm, n).reshape(shp)
        new_v[n] = _unpack_small(sv, n).reshape(shp)
    g_big["w_ada"] = g_w_ada
    for n in ["w_ada"] + big_names[1:] + big_names[:1]:
        if n == "w_in":
            gw_in, sems, land = pending
            done = [tot, sd] + [deltas[k] for k in ["w_ada"] + big_names[1:]]
            (gw_in,), (got,) = _ici_wait("ici_wait_w_in", [n], sems, [gw_in], [land], done)
            g_big[n] = _final_sum("final_w_in", pos, 1, gw_in, got, SHARD[n])
        g, d, nm, nv = _adamw("adamw_" + n, given[n][0], g_big[n], given["m_" + n][0], given["v_" + n][0])
        grads[n], deltas[n], new_m[n], new_v[n] = g[None], d[None], nm[None], nv[None]

    order = ["w_ada", "b_ada", "norm1_g", "w_in", "rel_bias", "ret_gn_g", "ret_gn_b", "w_ret_out",
             "w_att_out", "w_o", "norm2_g", "w_ff1", "w_ff2", "norm_f_g"]
    loss_out = _unpack_small(tot, "loss")[0, 0]
    return (loss_out, grad_x[None], *[grads[n] for n in order], *[deltas[n] for n in order],
            *[new_m[n] for n in order], *[new_v[n] for n in order])
```

```python
import functools
import math

import jax
import jax.numpy as jnp
import numpy as np
from jax import lax
from jax.experimental import pallas as pl
from jax.experimental.pallas import tpu as pltpu

F32 = jnp.float32
BF16 = jnp.bfloat16
I32 = jnp.int32

SEQ = 2048
D_MODEL = 1024
RET_HEADS = 4
RET_DK = 256
RET_DV = 512
RET_CHUNK = 128
RET_SUB = 2
RET_QK_W = RET_HEADS * RET_DK
RET_V_W = RET_HEADS * RET_DV
ATT_GROUPS = ((128, 1), (512, 4), (2048, 16))
ATT_HPG = 4
ATT_DH = 128
ATT_W = ATT_HPG * ATT_DH
ATT_BLK = 128
N_BLK = SEQ // ATT_BLK
REL_BUCKETS = 32
REL_MAX_DIST = 2048
N_ATT_HEADS = 12
D_FF = 4 * D_MODEL
RMS_EPS = 1e-6
GN_EPS = 1e-5
ROPE_BASE = 10000.0
IN_COLS = 2 * RET_QK_W + 2 * RET_V_W + 9 * ATT_W + 2 * D_MODEL
OFF_Q, OFF_K, OFF_V, OFF_G = 0, RET_QK_W, 2 * RET_QK_W, 2 * RET_QK_W + RET_V_W
OFF_ATT = 2 * RET_QK_W + 2 * RET_V_W
OFF_GATE = OFF_ATT + 9 * ATT_W
N_CHIPS = 4
N_DEV = 8
ADA_COLS = 6 * D_MODEL

ADAM_LR = 0.001
ADAM_B1 = 0.9
ADAM_B2 = 0.999
ADAM_EPS = 1e-08
ADAM_WD = 0.01
ADAM_STEP = 10

VMEM_LIMIT_V7X = 56 * 1024 * 1024
MESH = pl.DeviceIdType.MESH


def _cparams(sem):
    return pltpu.CompilerParams(dimension_semantics=sem, vmem_limit_bytes=VMEM_LIMIT_V7X)


def _sigmoid(v):
    return 1.0 / (1.0 + jnp.exp(-v))


def _rowmap(name, body, row_ins, bcast_ins, row_outs, sum_outs=(), tm=256, after=()):
    m = row_ins[0].shape[0]
    n_in = len(row_ins) + len(bcast_ins)
    n_ro = len(row_outs)

    def kern(*refs):
        vals = [r[...] for r in refs[:n_in]]
        res = body(*vals)
        if not isinstance(res, (tuple, list)):
            res = (res,)
        outs = refs[n_in + len(after):]
        for r, v in zip(outs[:n_ro], res[:n_ro]):
            r[...] = v.astype(r.dtype)
        if sum_outs:
            @pl.when(pl.program_id(0) == 0)
            def _():
                for r in outs[n_ro:]:
                    r[...] = jnp.zeros_like(r)
            for r, v in zip(outs[n_ro:], res[n_ro:]):
                r[...] += v

    in_specs = [pl.BlockSpec((tm, a.shape[1]), lambda i: (i, 0)) for a in row_ins]
    in_specs += [pl.BlockSpec(a.shape, lambda i: (0, 0)) for a in bcast_ins]
    in_specs += [pl.BlockSpec(memory_space=pl.ANY)] * len(after)
    out_specs = [pl.BlockSpec((tm, n), lambda i: (i, 0)) for n, _ in row_outs]
    out_specs += [pl.BlockSpec((1, n), lambda i: (0, 0)) for n in sum_outs]
    out_shape = [jax.ShapeDtypeStruct((m, n), dt) for n, dt in row_outs]
    out_shape += [jax.ShapeDtypeStruct((1, n), F32) for n in sum_outs]
    return pl.pallas_call(
        kern, name=name, grid=(m // tm,), in_specs=in_specs, out_specs=out_specs,
        out_shape=out_shape, compiler_params=_cparams(("arbitrary",)),
    )(*row_ins, *bcast_ins, *after)


TM, TN = 1024, 1024


def _piece_chunks(piece, width):
    arr, stacked = piece
    return arr.shape[0] if stacked else arr.shape[1] // width


def _piece_spec(piece, rows, width, start, row_of, chunk_of):
    arr, stacked = piece
    last = _piece_chunks(piece, width) - 1

    def local(*ids):
        return jnp.clip(chunk_of(*ids) - start, 0, last)

    def row(*ids):
        rel = chunk_of(*ids) - start
        return jnp.where(jnp.logical_and(rel >= 0, rel <= last), row_of(*ids), 0)

    if stacked:
        return pl.BlockSpec((None, rows, width), lambda *ids: (local(*ids), row(*ids), 0))
    return pl.BlockSpec((rows, width), lambda *ids: (row(*ids), local(*ids)))


def _piece_starts(pieces, width):
    return [sum(_piece_chunks(p, width) for p in pieces[:q]) for q in range(len(pieces))]


def _matmul(name, a, b, kind, m, n, k, outs, *, b_off=0, tm=TM, tn=TN, tk=1024,
            epilogue=None, extras=(), after=()):
    tm, tn, tk = min(tm, m), min(tn, n), min(tk, k)
    nk = k // tk
    pieces = a if isinstance(a, list) else [(a, False)]
    starts = _piece_starts(pieces, tk)
    if kind == "nn":
        a_specs = [pl.BlockSpec((tm, tk), lambda i, j, kk: (i, kk))]
        b_spec = pl.BlockSpec((tk, tn), lambda i, j, kk: (kk, b_off // tn + j))
        dn = (((1,), (0,)), ((), ()))
    elif kind == "nt":
        a_specs = [_piece_spec(p, tm, tk, st, lambda i, j, kk: i, lambda i, j, kk: kk)
                   for p, st in zip(pieces, starts)]
        b_spec = pl.BlockSpec((tn, tk), lambda i, j, kk: (j, b_off // tk + kk))
        dn = (((1,), (1,)), ((), ()))
    else:
        a_specs = [pl.BlockSpec((tk, tm), lambda i, j, kk: (kk, i))]
        b_spec = pl.BlockSpec((tk, tn), lambda i, j, kk: (kk, j))
        dn = (((0,), (0,)), ((), ()))
    n_a, n_ex, n_out = len(pieces), len(extras), len(outs)
    if epilogue is None:
        epilogue = lambda acc: (acc,)

    def finish(acc, ex_refs, out_refs):
        res = epilogue(acc, *[r[...] for r in ex_refs])
        for r, v in zip(out_refs, res):
            r[...] = v.astype(r.dtype)

    n_in = n_a + 1 + n_ex + len(after)

    def kern(*refs):
        a_refs, b_ref = refs[:n_a], refs[n_a]
        ex_refs = refs[n_a + 1:n_a + 1 + n_ex]
        out_refs = refs[n_in:n_in + n_out]
        kk = pl.program_id(2)
        dot = lambda a_ref: lax.dot_general(a_ref[...], b_ref[...], dn, preferred_element_type=F32)
        if nk == 1:
            finish(dot(a_refs[0]), ex_refs, out_refs)
            return
        acc_ref = refs[n_in + n_out]
        if n_a == 1:
            part = dot(a_refs[0])

            @pl.when(kk == 0)
            def _():
                acc_ref[...] = part

            @pl.when(kk > 0)
            def _():
                acc_ref[...] += part
        else:
            @pl.when(kk == 0)
            def _():
                acc_ref[...] = jnp.zeros_like(acc_ref)

            for q in range(n_a):
                @pl.when(jnp.logical_and(kk >= starts[q], kk < starts[q] + _piece_chunks(pieces[q], tk)))
                def _(q=q):
                    acc_ref[...] += dot(a_refs[q])

        @pl.when(kk == nk - 1)
        def _():
            finish(acc_ref[...], ex_refs, out_refs)

    in_specs = a_specs + [b_spec] + [pl.BlockSpec(bs, im) for _, bs, im in extras]
    in_specs += [pl.BlockSpec(memory_space=pl.ANY)] * len(after)
    return pl.pallas_call(
        kern, name=name, grid=(m // tm, n // tn, nk), in_specs=in_specs,
        out_specs=[pl.BlockSpec((tm, tn), lambda i, j, kk: (i, j)) for _ in outs],
        out_shape=[jax.ShapeDtypeStruct((m, n), dt) for dt in outs],
        scratch_shapes=[] if nk == 1 else [pltpu.VMEM((tm, tn), F32)],
        compiler_params=_cparams(("parallel", "parallel", "arbitrary")),
    )(*[p[0] for p in pieces], b, *[e[0] for e in extras], *after)


def _ici_copies(psum_ref, recv_ref, s_sem, r_sem, axis, shard_shape):
    x, y, c = _me()
    hr, cw = shard_shape[0] // 2, shard_shape[1]
    pick = lambda sems, j: sems[j] if isinstance(sems, (list, tuple)) else sems.at[j]
    copies = []
    for j, (fx, fy) in enumerate(_CHIP_FLIPS):
        chip = 2 * (x ^ fx) + (y ^ fy)
        src = psum_ref.at[:, pl.ds(chip * cw, cw)] if axis == 1 else psum_ref.at[pl.ds(chip * hr, hr), :]
        copies.append(pltpu.make_async_remote_copy(
            src_ref=src, dst_ref=recv_ref.at[j], send_sem=pick(s_sem, j), recv_sem=pick(r_sem, j),
            device_id=(x ^ fx, y ^ fy, c), device_id_type=MESH))
    return copies


_HBM_SPEC = pl.BlockSpec(memory_space=pltpu.HBM)
_SEM_SPEC = pl.BlockSpec(memory_space=pltpu.SEMAPHORE)


def _split_ici_copies(names, p_refs, land_refs, sems):
    copies = []
    for i, n in enumerate(names):
        copies += _ici_copies(p_refs[i], land_refs[i], list(sems[6 * i:6 * i + 3]),
                              list(sems[6 * i + 3:6 * i + 6]), dict(BIG)[n], SHARD[n])
    return copies


def _ici_start(name, names, psums):
    nw, ns = len(names), 6 * len(names)
    lands = [lax.empty((3, SHARD[n][0] // 2, SHARD[n][1]), BF16) for n in names]

    def body(*refs):
        for cp in _split_ici_copies(names, refs[:nw], refs[nw:2 * nw], refs[2 * nw:2 * nw + ns]):
            cp.start()
        token = refs[-1]
        token[...] = jnp.zeros_like(token)

    res = pl.pallas_call(
        body, name=name,
        out_shape=(pltpu.SemaphoreType.DMA(()),) * ns
        + tuple(pltpu.HBM(a.shape, BF16) for a in list(psums) + lands)
        + (jax.ShapeDtypeStruct((8, 128), F32),),
        in_specs=(_HBM_SPEC,) * (2 * nw),
        out_specs=(_SEM_SPEC,) * ns + (_HBM_SPEC,) * (2 * nw) + (pl.BlockSpec(memory_space=pltpu.VMEM),),
        input_output_aliases={k: ns + k for k in range(2 * nw)},
        compiler_params=pltpu.CompilerParams(has_side_effects=pltpu.SideEffectType.DATAFLOW_SIDE_EFFECTING),
    )(*[pltpu.with_memory_space_constraint(a, pltpu.HBM) for a in list(psums) + lands])
    return res[:ns], res[ns:ns + nw], res[ns + nw:ns + 2 * nw], res[-1]


def _ici_wait(name, names, sems, p_thru, land_thru, after):
    nw, ns = len(names), 6 * len(names)

    def body(*refs):
        for cp in _split_ici_copies(names, refs[:nw], refs[nw:2 * nw], refs[2 * nw:2 * nw + ns]):
            cp.wait_send()
            cp.wait_recv()

    res = pl.pallas_call(
        body, name=name,
        out_shape=tuple(pltpu.HBM(a.shape, BF16) for a in list(p_thru) + list(land_thru)),
        in_specs=(_HBM_SPEC,) * (2 * nw) + (_SEM_SPEC,) * ns + (pl.BlockSpec(memory_space=pl.ANY),) * len(after),
        out_specs=(_HBM_SPEC,) * (2 * nw), input_output_aliases={k: k for k in range(2 * nw)},
        compiler_params=pltpu.CompilerParams(has_side_effects=pltpu.SideEffectType.DATAFLOW_SIDE_EFFECTING),
    )(*p_thru, *land_thru, *sems, *after)
    return res[:nw], res[nw:]


def _where_am_i():
    x, y, c = _me()
    return jnp.stack([c, 2 * x + y]).astype(I32)


def _sibling():
    x, y, c = _me()
    return (x, y, 1 - c)


N_SEND_SLOTS = 2


def _matmul_tn_pair(name, pos, a, b, m, n, k, shard_rows, *, tm, tn, tk):
    hr = shard_rows // 2
    tm, tn, tk = min(tm, hr), min(tn, n), min(tk, k)
    tph = hr // tm
    nt, nj, nk = (m // 2) // tm, n // tn, k // tk
    n_tiles = nt * nj

    def row_block(p, t, pos_ref):
        half = jnp.where(p == 0, 1 - pos_ref[0], pos_ref[0])
        return (t // tph) * (2 * tph) + half * tph + t % tph

    pieces = b if isinstance(b, list) else [(b, False)]
    starts = _piece_starts(pieces, tn)
    n_b = len(pieces)

    def kern(pos_ref, a_ref, *rest):
        b_refs = rest[:n_b]
        o_ref, acc_ref, send_buf, land_buf, s_sem, r_sem = rest[n_b:]
        p, t, j, kk = pl.program_id(0), pl.program_id(1), pl.program_id(2), pl.program_id(3)
        idx = t * nj + j
        sib = _sibling()

        def copy(i):
            return pltpu.make_async_remote_copy(
                src_ref=send_buf.at[i % N_SEND_SLOTS], dst_ref=land_buf.at[i], send_sem=s_sem.at[i],
                recv_sem=r_sem.at[i], device_id=sib, device_id_type=MESH)

        @pl.when(kk == 0)
        def _():
            acc_ref[...] = jnp.zeros_like(acc_ref)

        for q in range(n_b):
            @pl.when(jnp.logical_and(j >= starts[q], j < starts[q] + _piece_chunks(pieces[q], tn)))
            def _(q=q):
                acc_ref[...] += lax.dot_general(a_ref[...], b_refs[q][...], _TN, preferred_element_type=F32)

        @pl.when(jnp.logical_and(kk == nk - 1, p == 0))
        def _():
            @pl.when(idx >= N_SEND_SLOTS)
            def _():
                copy(idx - N_SEND_SLOTS).wait_send()

            send_buf[idx % N_SEND_SLOTS] = acc_ref[...].astype(BF16)
            copy(idx).start()

        @pl.when(jnp.logical_and(kk == nk - 1, p == 1))
        def _():
            copy(idx).wait_recv()
            o_ref[...] = (acc_ref[...] + land_buf[idx].astype(F32)).astype(BF16)

        @pl.when(jnp.logical_and(jnp.logical_and(p == 1, idx == n_tiles - 1), kk == nk - 1))
        def _():
            for i in range(max(n_tiles - N_SEND_SLOTS, 0), n_tiles):
                copy(i).wait_send()

    grid_spec = pltpu.PrefetchScalarGridSpec(
        num_scalar_prefetch=1, grid=(2, nt, nj, nk),
        in_specs=[pl.BlockSpec((tk, tm), lambda p, t, j, kk, pos_ref: (kk, row_block(p, t, pos_ref)))]
        + [_piece_spec(pc, tk, tn, st, lambda p, t, j, kk, pos_ref: kk, lambda p, t, j, kk, pos_ref: j)
           for pc, st in zip(pieces, starts)],
        out_specs=pl.BlockSpec((tm, tn), lambda p, t, j, kk, pos_ref: (p * t, p * j)),
        scratch_shapes=[pltpu.VMEM((tm, tn), F32), pltpu.VMEM((N_SEND_SLOTS, tm, tn), BF16),
                        pltpu.VMEM((n_tiles, tm, tn), BF16),
                        pltpu.SemaphoreType.DMA((n_tiles,)), pltpu.SemaphoreType.DMA((n_tiles,))])
    return pl.pallas_call(
        kern, name=name, grid_spec=grid_spec, out_shape=jax.ShapeDtypeStruct((m // 2, n), BF16),
        compiler_params=_cparams(("arbitrary",) * 4),
    )(pos, a, *[pc[0] for pc in pieces])


def _rope_tables():
    half = RET_DK // 2
    f32 = np.float32
    inv = np.power(f32(ROPE_BASE), -np.arange(half, dtype=f32) / f32(half)).astype(f32)
    ang = (np.arange(SEQ, dtype=f32)[:, None] * inv[None, :]).astype(f32)
    return jnp.asarray(np.cos(ang).astype(f32)), jnp.asarray(np.sin(ang).astype(f32))


def _decay_tables():
    c = RET_CHUNK
    f32 = np.float32
    log_g = np.log1p(-np.power(f32(2.0), f32(-5.0) - np.arange(RET_HEADS, dtype=f32))).astype(f32)
    idx = np.arange(c, dtype=f32)
    rel = idx[:, None] - idx[None, :]
    din = np.where(rel >= 0, np.exp(log_g[:, None, None] * np.maximum(rel, f32(0.0))), f32(0.0)).astype(f32)
    qd = np.exp(log_g[:, None] * (idx + f32(1.0))).astype(f32)[:, :, None]
    kd = np.exp(log_g[:, None] * (f32(c) - f32(1.0) - idx)).astype(f32)[:, :, None]
    cd = np.exp(log_g * f32(c)).astype(f32)
    return jnp.asarray(din), jnp.asarray(qd), jnp.asarray(kd), jnp.asarray(cd)


def _t5_bucket(dist):
    max_exact = REL_BUCKETS // 2
    d_f = jnp.maximum(dist, 1).astype(F32)
    large = max_exact + (jnp.log(d_f / max_exact) / math.log(REL_MAX_DIST / max_exact)
                         * (REL_BUCKETS - max_exact)).astype(I32)
    large = jnp.minimum(large, REL_BUCKETS - 1)
    return jnp.where(dist < max_exact, dist, large)


def _bucket_tables():
    qi = jnp.arange(ATT_BLK)[:, None]
    kj = jnp.arange(2 * ATT_BLK)[None, :]
    dist = jnp.clip(ATT_BLK + qi - kj, 0, ATT_BLK)
    return jnp.stack([_t5_bucket(dist * dil) for _, dil in ATT_GROUPS]).astype(I32)


def _retention_fwd(rqk, rv, rg, gn_g, gn_b, din, qd, kd, cd):
    nc = SEQ // RET_CHUNK
    c, dk, dv = RET_CHUNK, RET_DK, RET_DV

    def kern(q_ref, k_ref, v_ref, rg_ref, g_ref, b_ref, din_ref, qd_ref, kd_ref, cd_ref,
             o_ref, st_ref, gated_ref, state):
        n = pl.program_id(0)

        @pl.when(n == 0)
        def _():
            state[...] = jnp.zeros_like(state)

        for sub in range(RET_SUB):
            rows = slice(sub * c, (sub + 1) * c)
            for h in range(RET_HEADS):
                q, k = q_ref[rows, h * dk:(h + 1) * dk], k_ref[rows, h * dk:(h + 1) * dk]
                v = v_ref[rows, h * dv:(h + 1) * dv]
                s_b = state[h].astype(BF16)
                st_ref[h, sub] = s_b
                a = lax.dot_general(q, k, _NT, preferred_element_type=F32) * din_ref[h]
                o = jnp.dot(a.astype(BF16), v, preferred_element_type=F32)
                o += jnp.dot(q, s_b, preferred_element_type=F32) * qd_ref[h]
                v_cols = slice(h * dv, (h + 1) * dv)
                o_ref[rows, v_cols] = o
                nrm, _ = _gn_parts(o)
                gate = rg_ref[rows, v_cols].astype(F32)
                gated_ref[rows, v_cols] = ((gate * _sigmoid(gate))
                                           * (nrm * g_ref[:, v_cols] + b_ref[:, v_cols])).astype(BF16)
                kk = (k.astype(F32) * kd_ref[h]).astype(BF16)
                state[h] = state[h] * cd_ref[h] + lax.dot_general(kk, v, _TN, preferred_element_type=F32)

    whole = lambda a: pl.BlockSpec(a.shape, lambda n: (0,) * a.ndim)
    cs = RET_SUB * c
    rows_v = pl.BlockSpec((cs, RET_V_W), lambda n: (n, 0))
    return pl.pallas_call(
        kern, name="retention_fwd", grid=(nc // RET_SUB,),
        in_specs=[
            pl.BlockSpec((cs, RET_QK_W), lambda n: (n, 0)),
            pl.BlockSpec((cs, RET_QK_W), lambda n: (n, 1)),
            rows_v, rows_v, whole(gn_g), whole(gn_b),
            whole(din), whole(qd), whole(kd),
            pl.BlockSpec(memory_space=pltpu.SMEM),
        ],
        out_specs=[
            rows_v,
            pl.BlockSpec((RET_HEADS, RET_SUB, dk, dv), lambda n: (0, n, 0, 0)),
            rows_v,
        ],
        out_shape=[
            jax.ShapeDtypeStruct((SEQ, RET_V_W), F32),
            jax.ShapeDtypeStruct((RET_HEADS, nc, dk, dv), BF16),
            jax.ShapeDtypeStruct((SEQ, RET_V_W), BF16),
        ],
        scratch_shapes=[pltpu.VMEM((RET_HEADS, dk, dv), F32)],
        compiler_params=_cparams(("arbitrary",)),
    )(rqk, rqk, rv, rg, gn_g, gn_b, din, qd, kd, cd)


def _retention_bwd(rqk, rv, states, d_ro, din, qd, kd, cd, cos, sin):
    nc = SEQ // RET_CHUNK
    c, dk, dv = RET_CHUNK, RET_DK, RET_DV
    half = dk // 2
    last = nc // RET_SUB - 1

    def unrot(g, cs, sn):
        g1, g2 = g[:, :half], g[:, half:]
        return jnp.concatenate([g1 * cs + g2 * sn, g2 * cs - g1 * sn], axis=-1)

    def kern(q_ref, k_ref, v_ref, st_ref, do_ref, din_ref, qd_ref, kd_ref, cd_ref, cos_ref, sin_ref,
             out_ref, dstate):
        step = pl.program_id(0)

        @pl.when(step == 0)
        def _():
            dstate[...] = jnp.zeros_like(dstate)

        for sub in reversed(range(RET_SUB)):
            rows = slice(sub * c, (sub + 1) * c)
            cs, sn = cos_ref[rows, :], sin_ref[rows, :]
            for h in range(RET_HEADS):
                qk_cols, v_cols = slice(h * dk, (h + 1) * dk), slice(h * dv, (h + 1) * dv)
                q, k, v = q_ref[rows, qk_cols], k_ref[rows, qk_cols], v_ref[rows, v_cols]
                s_b = st_ref[h, sub]
                d_ob = do_ref[rows, v_cols]
                d_oq = (d_ob.astype(F32) * qd_ref[h]).astype(BF16)
                ds_b = dstate[h].astype(BF16)
                din_m = din_ref[h]
                a_b = (lax.dot_general(q, k, _NT, preferred_element_type=F32) * din_m).astype(BF16)
                kk = (k.astype(F32) * kd_ref[h]).astype(BF16)
                d_v = lax.dot_general(a_b, d_ob, _TN, preferred_element_type=F32)
                d_v += jnp.dot(kk, ds_b, preferred_element_type=F32)
                d_a = (lax.dot_general(d_ob, v, _NT, preferred_element_type=F32) * din_m).astype(BF16)
                d_q = jnp.dot(d_a, k, preferred_element_type=F32)
                d_q += lax.dot_general(d_oq, s_b, _NT, preferred_element_type=F32)
                d_k = lax.dot_general(d_a, q, _TN, preferred_element_type=F32)
                d_k += lax.dot_general(v, ds_b, _NT, preferred_element_type=F32) * kd_ref[h]
                dstate[h] = dstate[h] * cd_ref[h] + lax.dot_general(q, d_oq, _TN,
                                                                    preferred_element_type=F32)
                out_ref[rows, h * dk:(h + 1) * dk] = unrot(d_q, cs, sn).astype(BF16)
                out_ref[rows, RET_QK_W + h * dk:RET_QK_W + (h + 1) * dk] = (
                    unrot(d_k, cs, sn) * (RET_DK ** -0.5)).astype(BF16)
                out_ref[rows, 2 * RET_QK_W + h * dv:2 * RET_QK_W + (h + 1) * dv] = d_v.astype(BF16)

    whole = lambda a: pl.BlockSpec(a.shape, lambda n: (0,) * a.ndim)
    rs = RET_SUB * c
    return pl.pallas_call(
        kern, name="retention_bwd", grid=(nc // RET_SUB,),
        in_specs=[
            pl.BlockSpec((rs, RET_QK_W), lambda n: (last - n, 0)),
            pl.BlockSpec((rs, RET_QK_W), lambda n: (last - n, 1)),
            pl.BlockSpec((rs, RET_V_W), lambda n: (last - n, 0)),
            pl.BlockSpec((RET_HEADS, RET_SUB, dk, dv), lambda n: (0, last - n, 0, 0)),
            pl.BlockSpec((rs, RET_V_W), lambda n: (last - n, 0)),
            whole(din), whole(qd), whole(kd),
            pl.BlockSpec(memory_space=pltpu.SMEM),
            pl.BlockSpec((rs, half), lambda n: (last - n, 0)),
            pl.BlockSpec((rs, half), lambda n: (last - n, 0)),
        ],
        out_specs=pl.BlockSpec((rs, 2 * RET_QK_W + RET_V_W), lambda n: (last - n, 0)),
        out_shape=jax.ShapeDtypeStruct((SEQ, 2 * RET_QK_W + RET_V_W), BF16),
        scratch_shapes=[pltpu.VMEM((RET_HEADS, dk, dv), F32)],
        compiler_params=_cparams(("arbitrary",)),
    )(rqk, rqk, rv, states, d_ro, din, qd, kd, cd, cos, sin)


def _bias_build(rel_bias, buckets):
    ng = len(ATT_GROUPS)

    def kern(tab_ref, bkt_ref, o_ref):
        g, h = pl.program_id(0), pl.program_id(1)
        bkt = bkt_ref[...]
        acc = jnp.zeros(bkt.shape, F32)
        for b in range(REL_BUCKETS):
            acc = jnp.where(bkt == b, tab_ref[b, g * ATT_HPG + h], acc)
        o_ref[...] = acc

    return pl.pallas_call(
        kern, name="bias_build", grid=(ng, ATT_HPG),
        in_specs=[pl.BlockSpec(memory_space=pltpu.SMEM),
                  pl.BlockSpec((None, ATT_BLK, 2 * ATT_BLK), lambda g, h: (g, 0, 0))],
        out_specs=pl.BlockSpec((None, None, ATT_BLK, 2 * ATT_BLK), lambda g, h: (g, h, 0, 0)),
        out_shape=jax.ShapeDtypeStruct((ng, ATT_HPG, ATT_BLK, 2 * ATT_BLK), F32),
        compiler_params=_cparams(("arbitrary", "arbitrary")),
    )(rel_bias, buckets)


def _bias_grad(dsb, buckets):
    ng = len(ATT_GROUPS)

    def kern(ds_ref, bkt_ref, o_ref):
        g, h = pl.program_id(0), pl.program_id(1)
        bkt, ds = bkt_ref[...], ds_ref[...]
        for b in range(REL_BUCKETS):
            o_ref[b, g * ATT_HPG + h] = jnp.sum(jnp.where(bkt == b, ds, 0.0))

    return pl.pallas_call(
        kern, name="bias_grad", grid=(ng, ATT_HPG),
        in_specs=[pl.BlockSpec((None, None, ATT_BLK, 2 * ATT_BLK), lambda g, h: (g, h, 0, 0)),
                  pl.BlockSpec((None, ATT_BLK, 2 * ATT_BLK), lambda g, h: (g, 0, 0))],
        out_specs=pl.BlockSpec(memory_space=pltpu.SMEM),
        out_shape=jax.ShapeDtypeStruct((REL_BUCKETS, N_ATT_HEADS), F32),
        compiler_params=_cparams(("arbitrary", "arbitrary")),
    )(dsb, buckets)


_NT = (((1,), (1,)), ((), ()))
_TN = (((0,), (0,)), ((), ()))
_ATT_SCALE = ATT_DH ** -0.5


def _window_mask(has_prev):
    qi = lax.broadcasted_iota(I32, (ATT_BLK, 2 * ATT_BLK), 0)
    kj = lax.broadcasted_iota(I32, (ATT_BLK, 2 * ATT_BLK), 1)
    prev_ok = jnp.logical_and(jnp.logical_and(kj < ATT_BLK, kj >= qi), has_prev)
    return jnp.logical_or(prev_ok, jnp.logical_and(kj >= ATT_BLK, qi >= kj - ATT_BLK))


def _head_specs(col0):
    return pl.BlockSpec((SEQ, ATT_DH), lambda h: (0, col0 + h))


def _sub_rows(start, size, dil):
    return pl.ds(start, size) if dil == 1 else pl.ds(start, size, stride=dil)


def _att_blocks(dil):
    nb = SEQ // dil // ATT_BLK
    return [(r + dil * n * ATT_BLK, n > 0, n + 1 < nb) for r in range(dil) for n in range(nb)]


def _att_fwd(gi, dil, qkv, bias):
    blk, dh = ATT_BLK, ATT_DH
    pad = dil * blk
    col0 = 3 * ATT_HPG * gi

    def kern(q_ref, k_ref, v_ref, b_ref, o_ref, l_ref, qf, kpad, vpad):
        zero = jnp.zeros((pad, dh), F32)
        kpad[0:pad, :] = zero
        vpad[0:pad, :] = zero
        kpad[pad:, :] = k_ref[...].astype(F32)
        vpad[pad:, :] = v_ref[...].astype(F32)
        qf[...] = q_ref[...].astype(F32)
        bias_m = b_ref[...]
        for start, has_prev, _ in _att_blocks(dil):
            rows, window = _sub_rows(start, blk, dil), _sub_rows(start, 2 * blk, dil)
            q = qf[rows, :].astype(BF16)
            kw, vw = kpad[window, :].astype(BF16), vpad[window, :].astype(BF16)
            valid = _window_mask(has_prev)
            s = lax.dot_general(q, kw, _NT, preferred_element_type=F32) * _ATT_SCALE + bias_m
            s = jnp.where(valid, s, -1e30)
            mx = jnp.max(s, axis=-1, keepdims=True)
            e = jnp.exp(s - mx)
            den = jnp.sum(e, axis=-1, keepdims=True)
            o_ref[rows, :] = jnp.dot((e / den).astype(BF16), vw, preferred_element_type=F32)
            l_ref[rows, :] = jnp.broadcast_to(mx + jnp.log(den), (blk, dh))

    return pl.pallas_call(
        kern, name=f"att_fwd_g{gi}", grid=(ATT_HPG,),
        in_specs=[_head_specs(col0), _head_specs(col0 + ATT_HPG), _head_specs(col0 + 2 * ATT_HPG),
                  pl.BlockSpec((None, None, blk, 2 * blk), lambda h: (gi, h, 0, 0))],
        out_specs=[_head_specs(0), _head_specs(0)],
        out_shape=[jax.ShapeDtypeStruct((SEQ, ATT_W), F32), jax.ShapeDtypeStruct((SEQ, ATT_W), F32)],
        scratch_shapes=[pltpu.VMEM((SEQ, dh), F32), pltpu.VMEM((SEQ + pad, dh), F32),
                        pltpu.VMEM((SEQ + pad, dh), F32)],
        compiler_params=_cparams(("arbitrary",)),
    )(qkv, qkv, qkv, bias)


def _att_bwd(gi, dil, qkv, d_att, lse, dd, bias):
    blk, dh = ATT_BLK, ATT_DH
    pad = dil * blk
    col0 = 3 * ATT_HPG * gi

    def kern(q_ref, k_ref, v_ref, do_ref, l_ref, d_ref, b_ref, dqkv_ref, dsb_ref,
             qf, kpad, vpad, dq_s, dkpad, dvpad):
        zero = jnp.zeros((pad, dh), F32)
        kpad[0:pad, :] = zero
        vpad[0:pad, :] = zero
        kpad[pad:, :] = k_ref[...].astype(F32)
        vpad[pad:, :] = v_ref[...].astype(F32)
        qf[...] = q_ref[...].astype(F32)
        dkpad[...] = jnp.zeros_like(dkpad)
        dvpad[...] = jnp.zeros_like(dvpad)
        bias_m = b_ref[...]
        ds_sum = jnp.zeros((blk, 2 * blk), F32)

        for start, has_prev, _ in _att_blocks(dil):
            rows, window = _sub_rows(start, blk, dil), _sub_rows(start, 2 * blk, dil)
            q, d_o = qf[rows, :].astype(BF16), do_ref[rows, :].astype(BF16)
            kw, vw = kpad[window, :].astype(BF16), vpad[window, :].astype(BF16)
            lrow, drow = l_ref[rows, :][:, :1], d_ref[rows, :][:, :1]
            valid = _window_mask(has_prev)
            s = lax.dot_general(q, kw, _NT, preferred_element_type=F32) * _ATT_SCALE + bias_m
            p = jnp.where(valid, jnp.exp(jnp.where(valid, s, -1e30) - lrow), 0.0)
            dp = lax.dot_general(d_o, vw, _NT, preferred_element_type=F32)
            ds = p * (dp - drow)
            ds_b = ds.astype(BF16)
            dq_s[rows, :] = jnp.dot(ds_b, kw, preferred_element_type=F32) * _ATT_SCALE
            dkpad[window, :] += lax.dot_general(ds_b, q, _TN, preferred_element_type=F32) * _ATT_SCALE
            dvpad[window, :] += lax.dot_general(p.astype(BF16), d_o, _TN, preferred_element_type=F32)
            ds_sum = ds_sum + ds
        dsb_ref[...] = ds_sum

        dqkv_ref[0] = dq_s[...].astype(BF16)
        dqkv_ref[1] = dkpad[pad:, :].astype(BF16)
        dqkv_ref[2] = dvpad[pad:, :].astype(BF16)

    return pl.pallas_call(
        kern, name=f"att_bwd_g{gi}", grid=(ATT_HPG,),
        in_specs=[_head_specs(col0), _head_specs(col0 + ATT_HPG), _head_specs(col0 + 2 * ATT_HPG),
                  _head_specs(0), _head_specs(0), _head_specs(0),
                  pl.BlockSpec((None, None, blk, 2 * blk), lambda h: (gi, h, 0, 0))],
        out_specs=[pl.BlockSpec((3, SEQ, dh), lambda h: (0, 0, h)),
                   pl.BlockSpec((None, blk, 2 * blk), lambda h: (h, 0, 0))],
        out_shape=[jax.ShapeDtypeStruct((3, SEQ, ATT_W), BF16),
                   jax.ShapeDtypeStruct((ATT_HPG, blk, 2 * blk), F32)],
        scratch_shapes=[pltpu.VMEM((SEQ, dh), F32), pltpu.VMEM((SEQ + pad, dh), F32),
                        pltpu.VMEM((SEQ + pad, dh), F32), pltpu.VMEM((SEQ, dh), F32),
                        pltpu.VMEM((SEQ + pad, dh), F32), pltpu.VMEM((SEQ + pad, dh), F32)],
        compiler_params=_cparams(("arbitrary",)),
    )(qkv, qkv, qkv, d_att, lse, dd, bias)


def _rms_parts(x):
    r = lax.rsqrt(jnp.mean(x * x, axis=-1, keepdims=True) + RMS_EPS)
    return x * r, r


def _rms_bwd(d_xhat, xhat, r):
    return r * (d_xhat - xhat * jnp.mean(d_xhat * xhat, axis=-1, keepdims=True))


def _prenorm_fwd(name, x, gain, shift, scale):
    def body(xt, g, sh, sc):
        xhat, _ = _rms_parts(xt)
        return (xhat * g) * (1.0 + sc) + sh
    return _rowmap(name, body, [x], [gain, shift, scale], [(D_MODEL, BF16)])[0]


def _prenorm_bwd(name, d_h, x, gain, scale, resid, branch=None, gate=None, after=()):
    gated = branch is not None

    def body(d_ht, xt, res, *rest):
        g, sc = rest[-2 - gated], rest[-1 - gated]
        xhat, r = _rms_parts(xt)
        nrm = xhat * g
        d_n = d_ht * (1.0 + sc)
        dx = _rms_bwd(d_n * g, xhat, r) + res
        sums = (jnp.sum(d_ht, axis=0, keepdims=True), jnp.sum(d_ht * nrm, axis=0, keepdims=True),
                jnp.sum(d_n * xhat, axis=0, keepdims=True))
        if not gated:
            return (dx,) + sums
        return (dx, dx * rest[-1]) + sums + (jnp.sum(dx * rest[0], axis=0, keepdims=True),)

    return _rowmap(name, body, [d_h, x, resid] + ([branch] if gated else []),
                   [gain, scale] + ([gate] if gated else []),
                   [(D_MODEL, F32)] + ([(D_MODEL, BF16)] if gated else []),
                   [D_MODEL] * (3 + gated), after=after)


def _gn_parts(ro):
    mu = jnp.mean(ro, axis=-1, keepdims=True)
    cen = ro - mu
    rstd = lax.rsqrt(jnp.mean(cen * cen, axis=-1, keepdims=True) + GN_EPS)
    return cen * rstd, rstd


def _retpost_bwd(d_gated, ro, rg, gn_g, gn_b):
    def body(dgt, rot, rgt, g, b):
        d_ro, d_rg, d_g, d_b = [], [], [], []
        for h in range(RET_HEADS):
            sl = slice(h * RET_DV, (h + 1) * RET_DV)
            nrm, rstd = _gn_parts(rot[:, sl])
            gate, dg = rgt[:, sl].astype(F32), dgt[:, sl].astype(F32)
            sg = _sigmoid(gate)
            ron = nrm * g[:, sl] + b[:, sl]
            d_rg.append(dg * ron * (sg * (1.0 + gate * (1.0 - sg))))
            d_ron = dg * (gate * sg)
            d_g.append(jnp.sum(d_ron * nrm, axis=0, keepdims=True))
            d_b.append(jnp.sum(d_ron, axis=0, keepdims=True))
            d_n = d_ron * g[:, sl]
            d_ro.append(rstd * (d_n - jnp.mean(d_n, axis=-1, keepdims=True)
                                - nrm * jnp.mean(d_n * nrm, axis=-1, keepdims=True)))
        cat = lambda ts: jnp.concatenate(ts, axis=-1)
        return cat(d_ro), cat(d_rg), cat(d_g), cat(d_b)
    return _rowmap("retpost_bwd", body, [d_gated, ro, rg], [gn_g, gn_b],
                   [(RET_V_W, BF16), (RET_V_W, BF16)], [RET_V_W, RET_V_W])


def _combine(os_, ls_, after=()):
    def body(o0, o1, o2, l0, l1, l2):
        mx = jnp.maximum(jnp.maximum(l0, l1), l2)
        e0, e1, e2 = jnp.exp(l0 - mx), jnp.exp(l1 - mx), jnp.exp(l2 - mx)
        den = e0 + e1 + e2
        att = (e0 / den) * o0 + (e1 / den) * o1 + (e2 / den) * o2
        return att, att, mx + jnp.log(den)
    return _rowmap("att_combine", body, list(os_) + list(ls_), [],
                   [(ATT_W, F32), (ATT_W, BF16), (ATT_W, F32)], after=after)


def _att_bwd_pre(d_att, att):
    def body(dt, at):
        outs = []
        for h in range(ATT_HPG):
            sl = slice(h * ATT_DH, (h + 1) * ATT_DH)
            outs.append(jnp.broadcast_to(jnp.sum(dt[:, sl] * at[:, sl], axis=-1, keepdims=True),
                                         (dt.shape[0], ATT_DH)))
        return jnp.concatenate(outs, axis=-1)
    return _rowmap("att_bwd_pre", body, [d_att, att], [], [(ATT_W, F32)])[0]


def _merge_fwd(gates, ret_out, att_out):
    def body(gt, ro, ao):
        gt = gt.astype(F32)
        return _sigmoid(gt[:, :D_MODEL]) * ro + _sigmoid(gt[:, D_MODEL:]) * ao
    return _rowmap("merge_fwd", body, [gates, ret_out, att_out], [], [(D_MODEL, BF16)])[0]


def _merge_bwd(d_merged, gates, ret_out, att_out):
    def body(dm, gt, ro, ao):
        dm, gt = dm.astype(F32), gt.astype(F32)
        sa, sb = _sigmoid(gt[:, :D_MODEL]), _sigmoid(gt[:, D_MODEL:])
        d_gates = jnp.concatenate([dm * ro * (sa * (1.0 - sa)), dm * ao * (sb * (1.0 - sb))], axis=-1)
        return dm * sa, dm * sb, d_gates
    return _rowmap("merge_bwd", body, [d_merged, gates, ret_out, att_out], [],
                   [(D_MODEL, BF16), (D_MODEL, BF16), (2 * D_MODEL, BF16)])


def _loss_head(x3, target, gain, branch, gate):
    def body(xt, tt, br, g, gt):
        xhat, r = _rms_parts(xt)
        err = xhat * g - tt
        d_y = err / D_MODEL
        loss = 0.5 * jnp.sum(jnp.mean(err * err, axis=-1, keepdims=True), axis=0, keepdims=True)
        d_x = _rms_bwd(d_y * g, xhat, r)
        return (d_x, d_x * gt, jnp.broadcast_to(loss, (1, 128)), jnp.sum(d_y * xhat, axis=0, keepdims=True),
                jnp.sum(d_x * br, axis=0, keepdims=True))
    return _rowmap("loss_head", body, [x3, target, branch], [gain, gate],
                   [(D_MODEL, F32), (D_MODEL, BF16)], [128, D_MODEL, D_MODEL])


def _local_step(pos, x, target, mod, norm1_g, norm2_g, norm_f_g, rel_bias, gn_g, gn_b, w_in, rest_gather):
    sh1, sc1, g1, sh2, sc2, g2 = [mod[:, i * D_MODEL:(i + 1) * D_MODEL] for i in range(6)]
    cos, sin = _rope_tables()
    din, qd, kd, cd = _decay_tables()
    buckets = _bucket_tables()
    bias = _bias_build(rel_bias, buckets)
    dils = [d for _, d in ATT_GROUPS]

    h1 = _prenorm_fwd("prenorm1_fwd", x, norm1_g, sh1, sc1)

    qk_tn = 2 * RET_DK

    def rot_epi(acc, cs, sn, scale):
        half = RET_DK // 2
        outs = []
        for h0 in range(0, qk_tn, RET_DK):
            x1, x2 = acc[:, h0:h0 + half], acc[:, h0 + half:h0 + RET_DK]
            outs += [x1 * cs - x2 * sn, x1 * sn + x2 * cs]
        return (jnp.concatenate(outs, axis=-1) * scale,)

    qk_scale = jnp.concatenate([jnp.ones((1, RET_QK_W), F32),
                                jnp.full((1, RET_QK_W), RET_DK ** -0.5, F32)], axis=-1)
    rope_ex = [(cos, (TM, RET_DK // 2), lambda i, j, kk: (i, 0)),
               (sin, (TM, RET_DK // 2), lambda i, j, kk: (i, 0)),
               (qk_scale, (1, qk_tn), lambda i, j, kk: (0, j))]
    rest_sems, rest_shards, rest_fulls, rest_token = rest_gather
    behind = [rest_token]
    rv = _matmul("proj_rv", h1, w_in, "nn", SEQ, RET_V_W, D_MODEL, [BF16], b_off=OFF_V, tk=D_MODEL,
                 after=behind)[0]
    rg = _matmul("proj_rg", h1, w_in, "nn", SEQ, RET_V_W, D_MODEL, [BF16], b_off=OFF_G, tk=D_MODEL,
                 after=behind)[0]
    gates = _matmul("proj_gates", h1, w_in, "nn", SEQ, 2 * D_MODEL, D_MODEL, [BF16], b_off=OFF_GATE,
                    tn=512, tk=D_MODEL, after=behind)[0]
    aqkv = _matmul("proj_att", h1, w_in, "nn", SEQ, 9 * ATT_W, D_MODEL, [BF16], b_off=OFF_ATT,
                   tn=512, tk=D_MODEL, after=behind)[0]

    os_, ls_ = [], []
    for gi in range(3):
        o_g, l_g = _att_fwd(gi, dils[gi], aqkv, bias)
        os_.append(o_g)
        ls_.append(l_g)

    rqk = _matmul("proj_qk", h1, w_in, "nn", SEQ, 2 * RET_QK_W, D_MODEL, [BF16], b_off=OFF_Q,
                  tn=qk_tn, tk=D_MODEL, epilogue=rot_epi, extras=rope_ex, after=behind)[0]
    ro, states, gated = _retention_fwd(rqk, rv, rg, gn_g, gn_b, din, qd, kd, cd)
    rest_sems, rest_fulls, fwd_token = _gather_rest_forward(rest_sems, rest_shards, rest_fulls,
                                                            [gated, gates] + os_)
    att, att_b, lse = _combine(os_, ls_, after=[fwd_token])
    w_ret_out, w_att_out, w_o, w_ff1, w_ff2 = _gather_rest_end(rest_sems, rest_fulls, [att_b])
    ret_out = _matmul("ret_out", gated, w_ret_out, "nn", SEQ, D_MODEL, RET_V_W, [F32], tk=RET_V_W)[0]
    att_out = _matmul("att_out", att_b, w_att_out, "nn", SEQ, D_MODEL, ATT_W, [F32])[0]

    merged = _merge_fwd(gates, ret_out, att_out)

    def resid_epi(acc, xt, g):
        return xt + g * acc, acc

    def resid_ex(xin, g):
        return [(xin, (TM, TN), lambda i, j, kk: (i, j)), (g, (1, TN), lambda i, j, kk: (0, j))]

    x2, mix = _matmul("mix_out", merged, w_o, "nn", SEQ, D_MODEL, D_MODEL, [F32, BF16],
                      epilogue=resid_epi, extras=resid_ex(x, g1))
    h2 = _prenorm_fwd("prenorm2_fwd", x2, norm2_g, sh2, sc2)

    def relu2_epi(acc):
        r = jnp.maximum(acc, 0.0)
        return r * r, r

    act, relu_u = _matmul("ff1", h2, w_ff1, "nn", SEQ, D_FF, D_MODEL, [BF16, BF16], tk=D_MODEL,
                          epilogue=relu2_epi)
    x3, y2 = _matmul("ff2", act, w_ff2, "nn", SEQ, D_MODEL, D_FF, [F32, BF16], tk=2048,
                     epilogue=resid_epi, extras=resid_ex(x2, g2))

    d_x3, d_y2, loss, d_gf, d_g2 = _loss_head(x3, target, norm_f_g, y2, g2)

    def relu2_bwd_epi(acc, rt):
        return (acc * (2.0 * rt.astype(F32)),)

    gw_ff2 = _matmul_tn_pair("ff2_dw", pos, act, d_y2, D_FF, D_MODEL, SEQ, D_FF // N_CHIPS,
                             tm=512, tn=1024, tk=SEQ)
    d_u = _matmul("ff2_dx", d_y2, w_ff2, "nt", SEQ, D_FF, D_MODEL, [BF16], epilogue=relu2_bwd_epi,
                  extras=[(relu_u, (TM, TN), lambda i, j, kk: (i, j))])[0]
    gw_ff1 = _matmul_tn_pair("ff1_dw", pos, h2, d_u, D_MODEL, D_FF, SEQ, D_MODEL,
                             tm=512, tn=1024, tk=SEQ)
    ffn = ["w_ff2", "w_ff1"]
    ffn_started = _ici_start("ici_start_ffn", ffn, [gw_ff2, gw_ff1])
    d_h2 = _matmul("ff1_dx", d_u, w_ff1, "nt", SEQ, D_MODEL, D_FF, [F32], tk=2048,
                   after=[ffn_started[3]])[0]
    d_x2, d_mix, d_sh2, d_sc2, d_n2g, d_g1 = _prenorm_bwd("prenorm2_bwd", d_h2, x2, norm2_g, sc2, d_x3,
                                                          branch=mix, gate=g1)
    gw_o = _matmul_tn_pair("mix_dw", pos, merged, d_mix, D_MODEL, D_MODEL, SEQ, D_MODEL // N_CHIPS,
                           tm=128, tn=1024, tk=2048)
    d_merged = _matmul("mix_dx", d_mix, w_o, "nt", SEQ, D_MODEL, D_MODEL, [BF16])[0]
    d_ret_out, d_att_out, d_gates = _merge_bwd(d_merged, gates, ret_out, att_out)

    gw_ret_out = _matmul_tn_pair("ret_out_dw", pos, gated, d_ret_out, RET_V_W, D_MODEL, SEQ,
                                 RET_V_W // N_CHIPS, tm=256, tn=1024, tk=SEQ)
    gw_att_out = _matmul_tn_pair("att_out_dw", pos, att_b, d_att_out, ATT_W, D_MODEL, SEQ, ATT_W,
                                 tm=256, tn=1024, tk=2048)
    mixer = ["w_o", "w_ret_out", "w_att_out"]
    mixer_started = _ici_start("ici_start_mixer", mixer, [gw_o, gw_ret_out, gw_att_out])
    d_gated = _matmul("ret_out_dx", d_ret_out, w_ret_out, "nt", SEQ, RET_V_W, D_MODEL, [BF16],
                      after=[mixer_started[3]])[0]
    d_att = _matmul("att_out_dx", d_att_out, w_att_out, "nt", SEQ, ATT_W, D_MODEL, [F32],
                    after=[mixer_started[3]])[0]

    d_ro, d_rg, d_gn_g, d_gn_b = _retpost_bwd(d_gated, ro, rg, gn_g, gn_b)
    d_rqkv = _retention_bwd(rqk, rv, states, d_ro, din, qd, kd, cd, cos, sin)

    dd = _att_bwd_pre(d_att, att)
    d_aqkv, dsbs = [], []
    for gi in range(3):
        dqkv, dsb = _att_bwd(gi, dils[gi], aqkv, d_att, lse, dd, bias)
        d_aqkv.append(dqkv)
        dsbs.append(dsb)
    d_rel_bias = _bias_grad(jnp.stack(dsbs), buckets)

    d_proj = [(d_rqkv, False), (d_rg, False)] + [(t, True) for t in d_aqkv] + [(d_gates, False)]
    gw_in = _matmul_tn_pair("proj_dw", pos, h1, d_proj, D_MODEL, IN_COLS, SEQ, D_MODEL,
                            tm=512, tn=ATT_W, tk=SEQ)
    sems, (gw_in,), (land,), token = _ici_start("ici_start_w_in", ["w_in"], [gw_in])
    d_h1 = _matmul("proj_dx", d_proj, w_in, "nt", SEQ, D_MODEL, IN_COLS, [F32], tn=1024, tk=ATT_W,
                   after=[token])[0]
    pending = (sems, land)

    names = ffn + mixer
    psums, got = _ici_wait("ici_wait_rest", names, list(ffn_started[0]) + list(mixer_started[0]),
                           list(ffn_started[1]) + list(mixer_started[1]),
                           list(ffn_started[2]) + list(mixer_started[2]), [d_h1])
    g_big = {n: _final_sum("final_" + n, pos, dict(BIG)[n], psums[i], got[i], SHARD[n])
             for i, n in enumerate(names)}
    grad_x, d_sh1, d_sc1, d_n1g = _prenorm_bwd("prenorm1_bwd", d_h1, x, norm1_g, sc1, d_x2,
                                               after=list(g_big.values()))
    d_mod = jnp.concatenate([d_sh1, d_sc1, d_g1, d_sh2, d_sc2, d_g2], axis=-1)
    small = dict(norm1_g=d_n1g, norm2_g=d_n2g, norm_f_g=d_gf, gn_g=d_gn_g, gn_b=d_gn_b,
                 rel_bias=d_rel_bias)
    return loss, grad_x, d_mod, small, g_big, (gw_in,) + pending


def _me():
    return lax.axis_index("x"), lax.axis_index("y"), lax.axis_index("c")


def _peer(x, y, c, mask):
    return (x ^ ((mask >> 2) & 1), y ^ ((mask >> 1) & 1), c ^ (mask & 1))


def _gather8(src_ref, dst_ref, send_sems, recv_sems):
    x, y, c = _me()
    me = 4 * x + 2 * y + c
    copies = []
    for mask in range(1, N_DEV):
        cp = pltpu.make_async_remote_copy(
            src_ref=src_ref, dst_ref=dst_ref.at[me], send_sem=send_sems.at[mask - 1],
            recv_sem=recv_sems.at[mask - 1], device_id=_peer(x, y, c, mask), device_id_type=MESH)
        cp.start()
        copies.append(cp)
    dst_ref[me] = src_ref[...]
    for cp in copies:
        cp.wait_recv()
    for cp in copies:
        cp.wait_send()


def _ada_fwd(c_in, w_ada, b_ada):
    ncol = ADA_COLS // N_CHIPS

    def body(c_ref, w_ref, b_ref, mod_ref, sc_ref, cbuf, cg, mbuf, mg, s1, r1, s2, r2):
        x, y, c = _me()
        me = 4 * x + 2 * y + c
        cv = c_ref[...]
        cbuf[...] = jnp.broadcast_to(cv * _sigmoid(cv), cbuf.shape)
        _gather8(cbuf, cg, s1, r1)
        rows = lax.broadcasted_iota(I32, (N_DEV, D_MODEL), 0)
        sc_all = jnp.zeros((N_DEV, D_MODEL), F32)
        for d in range(N_DEV):
            sc_all = jnp.where(rows == d, cg[d], sc_all)
        sc_ref[...] = sc_all
        mbuf[...] = jnp.dot(sc_all.astype(BF16), w_ref[...].astype(BF16), preferred_element_type=F32)
        _gather8(mbuf, mg, s2, r2)
        rowsel = lax.broadcasted_iota(I32, (N_DEV, ncol), 0) == me
        for k in range(N_CHIPS):
            blk = mg[2 * k]
            row = jnp.sum(jnp.where(rowsel, blk, 0.0), axis=0, keepdims=True)
            mod_ref[:, k * ncol:(k + 1) * ncol] = row + b_ref[:, k * ncol:(k + 1) * ncol]

    vm = pl.BlockSpec(memory_space=pltpu.VMEM)
    return pl.pallas_call(
        body, name="ada_fwd",
        in_specs=[vm, vm, vm], out_specs=[vm, vm],
        out_shape=[jax.ShapeDtypeStruct((1, ADA_COLS), F32), jax.ShapeDtypeStruct((N_DEV, D_MODEL), F32)],
        scratch_shapes=[
            pltpu.VMEM((8, D_MODEL), F32), pltpu.VMEM((N_DEV, 8, D_MODEL), F32),
            pltpu.VMEM((8, ncol), F32), pltpu.VMEM((N_DEV, 8, ncol), F32),
            pltpu.SemaphoreType.DMA((N_DEV - 1,)), pltpu.SemaphoreType.DMA((N_DEV - 1,)),
            pltpu.SemaphoreType.DMA((N_DEV - 1,)), pltpu.SemaphoreType.DMA((N_DEV - 1,)),
        ],
        compiler_params=pltpu.CompilerParams(vmem_limit_bytes=VMEM_LIMIT_V7X),
    )(c_in, w_ada, b_ada)


def _small_reduce(pack, sc_all):
    ncol = ADA_COLS // N_CHIPS

    def body(p_ref, sc_ref, tot_ref, gw_ref, pg, s1, r1):
        x, y, _ = _me()
        chip = 2 * x + y
        _gather8(p_ref, pg, s1, r1)
        tot = pg[0]
        for d in range(1, N_DEV):
            tot = tot + pg[d]
        tot_ref[...] = tot
        rows = lax.broadcasted_iota(I32, (N_DEV, ncol), 0)
        dmod = jnp.zeros((N_DEV, ncol), F32)
        for k in range(N_CHIPS):
            part = jnp.zeros((N_DEV, ncol), F32)
            for d in range(N_DEV):
                part = jnp.where(rows == d, pg[d, :, k * ncol:(k + 1) * ncol][0:1, :], part)
            dmod = jnp.where(chip == k, part, dmod)
        gw_ref[...] = lax.dot_general(sc_ref[...].astype(BF16), dmod.astype(BF16), _TN,
                                      preferred_element_type=F32)

    vm = pl.BlockSpec(memory_space=pltpu.VMEM)
    return pl.pallas_call(
        body, name="small_reduce",
        in_specs=[vm, vm], out_specs=[vm, vm],
        out_shape=[jax.ShapeDtypeStruct((8, ADA_COLS), F32), jax.ShapeDtypeStruct((D_MODEL, ncol), F32)],
        scratch_shapes=[pltpu.VMEM((N_DEV, 8, ADA_COLS), F32),
                        pltpu.SemaphoreType.DMA((N_DEV - 1,)), pltpu.SemaphoreType.DMA((N_DEV - 1,))],
        compiler_params=pltpu.CompilerParams(vmem_limit_bytes=VMEM_LIMIT_V7X),
    )(pack, sc_all)


BIG = (("w_in", 1), ("w_ret_out", 0), ("w_att_out", 1), ("w_o", 0), ("w_ff1", 1), ("w_ff2", 0))
SHARD = {"w_in": (D_MODEL, IN_COLS // N_CHIPS), "w_ret_out": (RET_V_W // N_CHIPS, D_MODEL),
         "w_att_out": (ATT_W, D_MODEL // N_CHIPS), "w_o": (D_MODEL // N_CHIPS, D_MODEL),
         "w_ff1": (D_MODEL, D_FF // N_CHIPS), "w_ff2": (D_FF // N_CHIPS, D_MODEL)}
_CHIP_FLIPS = ((1, 0), (0, 1), (1, 1))


def _region(ref, axis, chip, half, shard_shape):
    r, cw = shard_shape
    hr = r // 2
    if axis == 1:
        return ref.at[pl.ds(half * hr, hr), pl.ds(chip * cw, cw)]
    return ref.at[pl.ds(chip * r + half * hr, hr), :]


def _gather_weights(shards, n_remote):
    nw = len(BIG)
    shapes = [s.shape for s in shards]
    full_shapes = [(r, N_CHIPS * cw) if ax == 1 else (N_CHIPS * r, cw)
                   for (r, cw), (_, ax) in zip(shapes, BIG)]

    def body(*refs):
        ins, outs = refs[:nw], refs[nw:2 * nw]
        own = refs[2 * nw:3 * nw]
        from_ici, from_sib = refs[3 * nw:3 * nw + n_remote], refs[3 * nw + n_remote:3 * nw + 2 * n_remote]
        ld_sem, st_sem, s_ici, r_ici, s_d2d, r_d2d, st_a, st_b = refs[3 * nw + 2 * n_remote:]
        x, y, c = _me()
        chip = 2 * x + y
        sib = (x, y, 1 - c)
        loads = [pltpu.make_async_copy(ins[i], own[i], ld_sem.at[i]) for i in range(nw)]
        for cp in loads:
            cp.start()
        pending, first = [], []
        for i, (_, ax) in enumerate(BIG):
            r, cw = shapes[i]
            hr = r // 2
            loads[i].wait()
            dst = outs[i].at[:, pl.ds(chip * cw, cw)] if ax == 1 else outs[i].at[pl.ds(chip * r, r), :]
            cp = pltpu.make_async_copy(own[i], dst, st_sem.at[i])
            cp.start()
            pending.append(cp)
            for j, (fx, fy) in enumerate(_CHIP_FLIPS if i < n_remote else ()):
                rc = pltpu.make_async_remote_copy(
                    src_ref=own[i].at[pl.ds(c * hr, hr), :], dst_ref=from_ici[i].at[j],
                    send_sem=s_ici.at[j * nw + i], recv_sem=r_ici.at[j * nw + i],
                    device_id=(x ^ fx, y ^ fy, c), device_id_type=MESH)
                rc.start()
                first.append((j, i, rc))
        passed = []
        for j, i, rc in first:
            fx, fy = _CHIP_FLIPS[j]
            src_chip = 2 * (x ^ fx) + (y ^ fy)
            ax = BIG[i][1]
            rc.wait_recv()
            fw = pltpu.make_async_remote_copy(
                src_ref=from_ici[i].at[j], dst_ref=from_sib[i].at[j], send_sem=s_d2d.at[j * nw + i],
                recv_sem=r_d2d.at[j * nw + i], device_id=sib, device_id_type=MESH)
            fw.start()
            passed.append((j, i, src_chip, fw))
            st = pltpu.make_async_copy(from_ici[i].at[j], _region(outs[i], ax, src_chip, c, shapes[i]),
                                       st_a.at[j * nw + i])
            st.start()
            pending.append(st)
        for j, i, src_chip, fw in passed:
            fw.wait_recv()
            st = pltpu.make_async_copy(from_sib[i].at[j],
                                       _region(outs[i], BIG[i][1], src_chip, 1 - c, shapes[i]),
                                       st_b.at[j * nw + i])
            st.start()
            pending.append(st)
        for _, _, rc in first:
            rc.wait_send()
        for _, _, _, fw in passed:
            fw.wait_send()
        for cp in pending:
            cp.wait()

    hbm = pl.BlockSpec(memory_space=pl.ANY)
    halves = [pltpu.VMEM((3, r // 2, cw), BF16) for r, cw in shapes[:n_remote]]
    return pl.pallas_call(
        body, name="gather_weights",
        in_specs=[hbm] * nw, out_specs=[hbm] * nw,
        out_shape=[jax.ShapeDtypeStruct(fs, BF16) for fs in full_shapes],
        scratch_shapes=[pltpu.VMEM(sh, BF16) for sh in shapes] + halves + halves
        + [pltpu.SemaphoreType.DMA((nw,)), pltpu.SemaphoreType.DMA((nw,))]
        + [pltpu.SemaphoreType.DMA((3 * nw,))] * 6,
        compiler_params=pltpu.CompilerParams(vmem_limit_bytes=VMEM_LIMIT_V7X),
    )(*shards)


REST = BIG[1:]
_SIDE_EFFECTS = pltpu.CompilerParams(has_side_effects=pltpu.SideEffectType.DATAFLOW_SIDE_EFFECTING)
_ANY_SPEC = pl.BlockSpec(memory_space=pl.ANY)


def _rest_ici_copies(shard_refs, full_refs, sems):
    x, y, c = _me()
    chip = 2 * x + y
    n = 3 * len(REST)
    copies = []
    for i, (name, ax) in enumerate(REST):
        hr = SHARD[name][0] // 2
        for j, (fx, fy) in enumerate(_CHIP_FLIPS):
            copies.append(pltpu.make_async_remote_copy(
                src_ref=shard_refs[i].at[pl.ds(c * hr, hr), :],
                dst_ref=_region(full_refs[i], ax, chip, c, SHARD[name]),
                send_sem=sems[3 * i + j], recv_sem=sems[n + 3 * i + j],
                device_id=(x ^ fx, y ^ fy, c), device_id_type=MESH))
    return copies


def _rest_d2d_copies(full_refs, sems):
    x, y, c = _me()
    n = 3 * len(REST)
    copies = []
    for i, (name, ax) in enumerate(REST):
        for j, (fx, fy) in enumerate(_CHIP_FLIPS):
            reg = _region(full_refs[i], ax, 2 * (x ^ fx) + (y ^ fy), c, SHARD[name])
            copies.append(pltpu.make_async_remote_copy(
                src_ref=reg, dst_ref=reg, send_sem=sems[3 * i + j], recv_sem=sems[n + 3 * i + j],
                device_id=(x, y, 1 - c), device_id_type=MESH))
    return copies


def _gather_rest_start(shards, fulls, after):
    nr, ns, na = len(REST), 6 * len(REST), len(after)

    def body(*refs):
        for cp in _rest_ici_copies(refs[:nr], refs[nr:2 * nr], refs[2 * nr + na:2 * nr + na + ns]):
            cp.start()
        token = refs[-1]
        token[...] = jnp.zeros_like(token)

    hbm = lambda a: pltpu.HBM(a.shape, a.dtype)
    res = pl.pallas_call(
        body, name="gather_rest_start",
        out_shape=(pltpu.SemaphoreType.DMA(()),) * ns + tuple(hbm(a) for a in shards + fulls)
        + (jax.ShapeDtypeStruct((8, 128), F32),),
        in_specs=(_HBM_SPEC,) * (2 * nr) + (_ANY_SPEC,) * na,
        out_specs=(_SEM_SPEC,) * ns + (_HBM_SPEC,) * (2 * nr) + (pl.BlockSpec(memory_space=pltpu.VMEM),),
        input_output_aliases={k: ns + k for k in range(2 * nr)}, compiler_params=_SIDE_EFFECTS,
    )(*[pltpu.with_memory_space_constraint(a, pltpu.HBM) for a in shards + fulls], *after)
    return res[:ns], res[ns:ns + nr], res[ns + nr:ns + 2 * nr], res[-1]


def _gather_rest_forward(sems, shards, fulls, after):
    nr, ns = len(REST), 6 * len(REST)

    def body(*refs):
        shard_refs, full_refs, old = refs[:nr], refs[nr:2 * nr], refs[2 * nr:2 * nr + ns]
        new = refs[2 * nr + ns + len(after):2 * nr + 2 * ns + len(after)]
        for cp in _rest_ici_copies(shard_refs, full_refs, old):
            cp.wait_send()
            cp.wait_recv()
        for cp in _rest_d2d_copies(full_refs, new):
            cp.start()
        token = refs[-1]
        token[...] = jnp.zeros_like(token)

    res = pl.pallas_call(
        body, name="gather_rest_forward",
        out_shape=(pltpu.SemaphoreType.DMA(()),) * ns + tuple(pltpu.HBM(a.shape, a.dtype) for a in fulls)
        + (jax.ShapeDtypeStruct((8, 128), F32),),
        in_specs=(_HBM_SPEC,) * (2 * nr) + (_SEM_SPEC,) * ns + (_ANY_SPEC,) * len(after),
        out_specs=(_SEM_SPEC,) * ns + (_HBM_SPEC,) * nr + (pl.BlockSpec(memory_space=pltpu.VMEM),),
        input_output_aliases={nr + k: ns + k for k in range(nr)}, compiler_params=_SIDE_EFFECTS,
    )(*shards, *fulls, *sems, *after)
    return res[:ns], res[ns:ns + nr], res[-1]


def _gather_rest_end(sems, fulls, after):
    nr, ns = len(REST), 6 * len(REST)

    def body(*refs):
        for cp in _rest_d2d_copies(refs[:nr], refs[nr:nr + ns]):
            cp.wait_send()
            cp.wait_recv()

    return pl.pallas_call(
        body, name="gather_rest_end",
        out_shape=tuple(pltpu.HBM(a.shape, a.dtype) for a in fulls),
        in_specs=(_HBM_SPEC,) * nr + (_SEM_SPEC,) * ns + (_ANY_SPEC,) * len(after),
        out_specs=(_HBM_SPEC,) * nr,
        input_output_aliases={k: k for k in range(nr)}, compiler_params=_SIDE_EFFECTS,
    )(*fulls, *sems, *after)


def _adam_update(w, g, m, v):
    mn = ADAM_B1 * m + (1.0 - ADAM_B1) * g
    vn = ADAM_B2 * v + (1.0 - ADAM_B2) * (g * g)
    m_hat = mn / (1.0 - ADAM_B1 ** ADAM_STEP)
    v_hat = vn / (1.0 - ADAM_B2 ** ADAM_STEP)
    return -ADAM_LR * (m_hat / (jnp.sqrt(v_hat) + ADAM_EPS) + ADAM_WD * w), mn, vn


def _final_sum(name, pos, axis, psum, recv, shard_shape, after=(), tr=128):
    r, cw = shard_shape
    hr = r // 2
    tr = min(tr, hr)
    nt = hr // tr
    n_after = len(after)

    def kern(pos_ref, p_ref, r_ref, *rest):
        g_ref, send_buf, land_buf, s_sem, r_sem = rest[n_after:]
        p, t = pl.program_id(0), pl.program_id(1)
        sib = _sibling()

        def copy(i):
            return pltpu.make_async_remote_copy(
                src_ref=send_buf.at[i], dst_ref=land_buf.at[i], send_sem=s_sem.at[i],
                recv_sem=r_sem.at[i], device_id=sib, device_id_type=MESH)

        @pl.when(p == 0)
        def _():
            tot = p_ref[...].astype(F32)
            for j in range(3):
                tot = tot + r_ref[j].astype(F32)
            send_buf[t] = tot
            copy(t).start()
            g_ref[...] = tot

        @pl.when(p == 1)
        def _():
            copy(t).wait_recv()
            g_ref[...] = land_buf[t]

        @pl.when(jnp.logical_and(p == 1, t == nt - 1))
        def _():
            for i in range(nt):
                copy(i).wait_send()

    def shard_rows(p, t, pos_ref):
        return (jnp.where(p == 0, pos_ref[0], 1 - pos_ref[0]) * nt + t, 0)

    def own_part(p, t, pos_ref):
        tt = jnp.where(p == 0, t, nt - 1)
        return (tt, pos_ref[1]) if axis == 1 else (pos_ref[1] * nt + tt, 0)

    grid_spec = pltpu.PrefetchScalarGridSpec(
        num_scalar_prefetch=1, grid=(2, nt),
        in_specs=[pl.BlockSpec((tr, cw), own_part),
                  pl.BlockSpec((3, tr, cw), lambda p, t, pos_ref: (0, jnp.where(p == 0, t, nt - 1), 0))]
        + [pl.BlockSpec(memory_space=pl.ANY)] * n_after,
        out_specs=pl.BlockSpec((tr, cw), shard_rows),
        scratch_shapes=[pltpu.VMEM((nt, tr, cw), F32), pltpu.VMEM((nt, tr, cw), F32),
                        pltpu.SemaphoreType.DMA((nt,)), pltpu.SemaphoreType.DMA((nt,))])
    return pl.pallas_call(
        kern, name=name, grid_spec=grid_spec, out_shape=jax.ShapeDtypeStruct((r, cw), F32),
        compiler_params=_cparams(("arbitrary", "arbitrary")),
    )(pos, psum, recv, *after)


def _adamw(name, w, g, m, v):
    r, cw = w.shape
    tr = min(r, 128)

    def kern(w_ref, g_ref, m_ref, v_ref, go_ref, d_ref, nm_ref, nv_ref):
        gv = g_ref[...]
        go_ref[...] = gv
        d_ref[...], nm_ref[...], nv_ref[...] = _adam_update(w_ref[...], gv, m_ref[...], v_ref[...])

    spec = pl.BlockSpec((tr, cw), lambda i: (i, 0))
    return pl.pallas_call(
        kern, name=name, grid=(r // tr,), in_specs=[spec] * 4, out_specs=[spec] * 4,
        out_shape=[jax.ShapeDtypeStruct((r, cw), F32)] * 4, compiler_params=_cparams(("parallel",)),
    )(w, g, m, v)


_PACK_W = ADA_COLS
_NB = REL_BUCKETS * N_ATT_HEADS
_SMALL_SLOTS = {
    "b_ada": (0, 0, ADA_COLS),
    "norm1_g": (1, 0, D_MODEL), "norm2_g": (1, D_MODEL, D_MODEL), "norm_f_g": (1, 2 * D_MODEL, D_MODEL),
    "ret_gn_g": (1, 3 * D_MODEL, RET_V_W),
    "ret_gn_b": (2, 0, RET_V_W), "rel_bias": (2, RET_V_W, _NB), "loss": (2, RET_V_W + 512, 128),
}


def _pack_small(vals):
    rows = []
    for r in range(8):
        items = sorted([(off, n) for n, (rr, off, _) in _SMALL_SLOTS.items() if rr == r and n in vals])
        parts, pos = [], 0
        for off, n in items:
            if off > pos:
                parts.append(jnp.zeros((1, off - pos), F32))
            parts.append(vals[n].reshape(1, -1).astype(F32))
            pos = off + _SMALL_SLOTS[n][2]
        if pos < _PACK_W:
            parts.append(jnp.zeros((1, _PACK_W - pos), F32))
        rows.append(jnp.concatenate(parts, axis=-1))
    return jnp.concatenate(rows, axis=0)


def _unpack_small(pack, name):
    r, off, wd = _SMALL_SLOTS[name]
    return pack[r:r + 1, off:off + wd]


def kernel(x, c, w_ada, b_ada, norm1_g, w_in, rel_bias, ret_gn_g, ret_gn_b, w_ret_out, w_att_out, w_o, norm2_g, w_ff1, w_ff2, norm_f_g, loss_target, m_w_ada, m_b_ada, m_norm1_g, m_w_in, m_rel_bias, m_ret_gn_g, m_ret_gn_b, m_w_ret_out, m_w_att_out, m_w_o, m_norm2_g, m_w_ff1, m_w_ff2, m_norm_f_g, v_w_ada, v_b_ada, v_norm1_g, v_w_in, v_rel_bias, v_ret_gn_g, v_ret_gn_b, v_w_ret_out, v_w_att_out, v_w_o, v_norm2_g, v_w_ff1, v_w_ff2, v_norm_f_g):
    given = dict(locals())
    big_names = [n for n, _ in BIG]
    shard_w = {n: given[n][0] for n in big_names}
    assert all(shard_w[n].shape == SHARD[n] for n in big_names)

    shards_bf = [shard_w[n].astype(BF16) for n in big_names]
    full = _gather_weights(shards_bf, 1)
    mod, sc_all = _ada_fwd(c, w_ada[0], b_ada)
    rest_gather = _gather_rest_start(shards_bf[1:], list(full[1:]), [mod])
    pos = _where_am_i()

    loss, grad_x, d_mod, small, g_big, pending = _local_step(
        pos, x[0], loss_target[0], mod, norm1_g, norm2_g, norm_f_g.reshape(1, -1), rel_bias, ret_gn_g,
        ret_gn_b, full[0], rest_gather)

    pack_g = _pack_small(dict(b_ada=d_mod, norm1_g=small["norm1_g"], norm2_g=small["norm2_g"],
                              norm_f_g=small["norm_f_g"], ret_gn_g=small["gn_g"], ret_gn_b=small["gn_b"],
                              rel_bias=small["rel_bias"], loss=loss))
    tot, g_w_ada = _small_reduce(pack_g, sc_all)

    small_names = ["b_ada", "norm1_g", "rel_bias", "ret_gn_g", "ret_gn_b", "norm2_g", "norm_f_g"]
    pack_w = _pack_small({n: given[n] for n in small_names})
    pack_m = _pack_small({n: given["m_" + n] for n in small_names})
    pack_v = _pack_small({n: given["v_" + n] for n in small_names})
    _, sd, sm, sv = _adamw("adamw_small", pack_w, tot, pack_m, pack_v)

    grads, deltas, new_m, new_v = {}, {}, {}, {}
    for n in small_names:
        shp = given[n].shape
        grads[n] = _unpack_small(tot, n).reshape(shp)
        deltas[n] = _unpack_small(sd, n).reshape(shp)
        new_m[n] = _unpack_small(sm, n).reshape(shp)
        new_v[n] = _unpack_small(sv, n).reshape(shp)
    g_big["w_ada"] = g_w_ada
    for n in ["w_ada"] + big_names[1:] + big_names[:1]:
        if n == "w_in":
            gw_in, sems, land = pending
            done = [tot, sd] + [deltas[k] for k in ["w_ada"] + big_names[1:]]
            (gw_in,), (got,) = _ici_wait("ici_wait_w_in", [n], sems, [gw_in], [land], done)
            g_big[n] = _final_sum("final_w_in", pos, 1, gw_in, got, SHARD[n])
        g, d, nm, nv = _adamw("adamw_" + n, given[n][0], g_big[n], given["m_" + n][0], given["v_" + n][0])
        grads[n], deltas[n], new_m[n], new_v[n] = g[None], d[None], nm[None], nv[None]

    order = ["w_ada", "b_ada", "norm1_g", "w_in", "rel_bias", "ret_gn_g", "ret_gn_b", "w_ret_out",
             "w_att_out", "w_o", "norm2_g", "w_ff1", "w_ff2", "norm_f_g"]
    loss_out = _unpack_small(tot, "loss")[0, 0]
    return (loss_out, grad_x[None], *[grads[n] for n in order], *[deltas[n] for n in order],
            *[new_m[n] for n in order], *[new_v[n] for n in order])
```

```python
import functools
import math

import jax
import jax.numpy as jnp
import numpy as np
from jax import lax
from jax.experimental import pallas as pl
from jax.experimental.pallas import tpu as pltpu

F32 = jnp.float32
BF16 = jnp.bfloat16
I32 = jnp.int32

SEQ = 2048
D_MODEL = 1024
RET_HEADS = 4
RET_DK = 256
RET_DV = 512
RET_CHUNK = 128
RET_SUB = 2
RET_QK_W = RET_HEADS * RET_DK
RET_V_W = RET_HEADS * RET_DV
ATT_GROUPS = ((128, 1), (512, 4), (2048, 16))
ATT_HPG = 4
ATT_DH = 128
ATT_W = ATT_HPG * ATT_DH
ATT_BLK = 128
N_BLK = SEQ // ATT_BLK
REL_BUCKETS = 32
REL_MAX_DIST = 2048
N_ATT_HEADS = 12
D_FF = 4 * D_MODEL
RMS_EPS = 1e-6
GN_EPS = 1e-5
ROPE_BASE = 10000.0
IN_COLS = 2 * RET_QK_W + 2 * RET_V_W + 9 * ATT_W + 2 * D_MODEL
OFF_Q, OFF_K, OFF_V, OFF_G = 0, RET_QK_W, 2 * RET_QK_W, 2 * RET_QK_W + RET_V_W
OFF_ATT = 2 * RET_QK_W + 2 * RET_V_W
OFF_GATE = OFF_ATT + 9 * ATT_W
N_CHIPS = 4
N_DEV = 8
ADA_COLS = 6 * D_MODEL

ADAM_LR = 0.001
ADAM_B1 = 0.9
ADAM_B2 = 0.999
ADAM_EPS = 1e-08
ADAM_WD = 0.01
ADAM_STEP = 10

VMEM_LIMIT_V7X = 56 * 1024 * 1024
MESH = pl.DeviceIdType.MESH


def _cparams(sem):
    return pltpu.CompilerParams(dimension_semantics=sem, vmem_limit_bytes=VMEM_LIMIT_V7X)


def _sigmoid(v):
    return 1.0 / (1.0 + jnp.exp(-v))


def _rowmap(name, body, row_ins, bcast_ins, row_outs, sum_outs=(), tm=256, after=()):
    m = row_ins[0].shape[0]
    n_in = len(row_ins) + len(bcast_ins)
    n_ro = len(row_outs)

    def kern(*refs):
        vals = [r[...] for r in refs[:n_in]]
        res = body(*vals)
        if not isinstance(res, (tuple, list)):
            res = (res,)
        outs = refs[n_in + len(after):]
        for r, v in zip(outs[:n_ro], res[:n_ro]):
            r[...] = v.astype(r.dtype)
        if sum_outs:
            @pl.when(pl.program_id(0) == 0)
            def _():
                for r in outs[n_ro:]:
                    r[...] = jnp.zeros_like(r)
            for r, v in zip(outs[n_ro:], res[n_ro:]):
                r[...] += v

    in_specs = [pl.BlockSpec((tm, a.shape[1]), lambda i: (i, 0)) for a in row_ins]
    in_specs += [pl.BlockSpec(a.shape, lambda i: (0, 0)) for a in bcast_ins]
    in_specs += [pl.BlockSpec(memory_space=pl.ANY)] * len(after)
    out_specs = [pl.BlockSpec((tm, n), lambda i: (i, 0)) for n, _ in row_outs]
    out_specs += [pl.BlockSpec((1, n), lambda i: (0, 0)) for n in sum_outs]
    out_shape = [jax.ShapeDtypeStruct((m, n), dt) for n, dt in row_outs]
    out_shape += [jax.ShapeDtypeStruct((1, n), F32) for n in sum_outs]
    return pl.pallas_call(
        kern, name=name, grid=(m // tm,), in_specs=in_specs, out_specs=out_specs,
        out_shape=out_shape, compiler_params=_cparams(("arbitrary",)),
    )(*row_ins, *bcast_ins, *after)


TM, TN = 1024, 1024


def _piece_chunks(piece, width):
    arr, stacked = piece
    return arr.shape[0] if stacked else arr.shape[1] // width


def _piece_spec(piece, rows, width, start, row_of, chunk_of):
    arr, stacked = piece
    last = _piece_chunks(piece, width) - 1

    def local(*ids):
        return jnp.clip(chunk_of(*ids) - start, 0, last)

    def row(*ids):
        rel = chunk_of(*ids) - start
        return jnp.where(jnp.logical_and(rel >= 0, rel <= last), row_of(*ids), 0)

    if stacked:
        return pl.BlockSpec((None, rows, width), lambda *ids: (local(*ids), row(*ids), 0))
    return pl.BlockSpec((rows, width), lambda *ids: (row(*ids), local(*ids)))


def _piece_starts(pieces, width):
    return [sum(_piece_chunks(p, width) for p in pieces[:q]) for q in range(len(pieces))]


def _matmul(name, a, b, kind, m, n, k, outs, *, b_off=0, tm=TM, tn=TN, tk=1024,
            epilogue=None, extras=(), after=()):
    tm, tn, tk = min(tm, m), min(tn, n), min(tk, k)
    nk = k // tk
    pieces = a if isinstance(a, list) else [(a, False)]
    starts = _piece_starts(pieces, tk)
    if kind == "nn":
        a_specs = [pl.BlockSpec((tm, tk), lambda i, j, kk: (i, kk))]
        b_spec = pl.BlockSpec((tk, tn), lambda i, j, kk: (kk, b_off // tn + j))
        dn = (((1,), (0,)), ((), ()))
    elif kind == "nt":
        a_specs = [_piece_spec(p, tm, tk, st, lambda i, j, kk: i, lambda i, j, kk: kk)
                   for p, st in zip(pieces, starts)]
        b_spec = pl.BlockSpec((tn, tk), lambda i, j, kk: (j, b_off // tk + kk))
        dn = (((1,), (1,)), ((), ()))
    else:
        a_specs = [pl.BlockSpec((tk, tm), lambda i, j, kk: (kk, i))]
        b_spec = pl.BlockSpec((tk, tn), lambda i, j, kk: (kk, j))
        dn = (((0,), (0,)), ((), ()))
    n_a, n_ex, n_out = len(pieces), len(extras), len(outs)
    if epilogue is None:
        epilogue = lambda acc: (acc,)

    def finish(acc, ex_refs, out_refs):
        res = epilogue(acc, *[r[...] for r in ex_refs])
        for r, v in zip(out_refs, res):
            r[...] = v.astype(r.dtype)

    n_in = n_a + 1 + n_ex + len(after)

    def kern(*refs):
        a_refs, b_ref = refs[:n_a], refs[n_a]
        ex_refs = refs[n_a + 1:n_a + 1 + n_ex]
        out_refs = refs[n_in:n_in + n_out]
        kk = pl.program_id(2)
        dot = lambda a_ref: lax.dot_general(a_ref[...], b_ref[...], dn, preferred_element_type=F32)
        if nk == 1:
            finish(dot(a_refs[0]), ex_refs, out_refs)
            return
        acc_ref = refs[n_in + n_out]
        if n_a == 1:
            part = dot(a_refs[0])

            @pl.when(kk == 0)
            def _():
                acc_ref[...] = part

            @pl.when(kk > 0)
            def _():
                acc_ref[...] += part
        else:
            @pl.when(kk == 0)
            def _():
                acc_ref[...] = jnp.zeros_like(acc_ref)

            for q in range(n_a):
                @pl.when(jnp.logical_and(kk >= starts[q], kk < starts[q] + _piece_chunks(pieces[q], tk)))
                def _(q=q):
                    acc_ref[...] += dot(a_refs[q])

        @pl.when(kk == nk - 1)
        def _():
            finish(acc_ref[...], ex_refs, out_refs)

    in_specs = a_specs + [b_spec] + [pl.BlockSpec(bs, im) for _, bs, im in extras]
    in_specs += [pl.BlockSpec(memory_space=pl.ANY)] * len(after)
    return pl.pallas_call(
        kern, name=name, grid=(m // tm, n // tn, nk), in_specs=in_specs,
        out_specs=[pl.BlockSpec((tm, tn), lambda i, j, kk: (i, j)) for _ in outs],
        out_shape=[jax.ShapeDtypeStruct((m, n), dt) for dt in outs],
        scratch_shapes=[] if nk == 1 else [pltpu.VMEM((tm, tn), F32)],
        compiler_params=_cparams(("parallel", "parallel", "arbitrary")),
    )(*[p[0] for p in pieces], b, *[e[0] for e in extras], *after)


def _ici_copies(psum_ref, recv_ref, s_sem, r_sem, axis, shard_shape):
    x, y, c = _me()
    hr, cw = shard_shape[0] // 2, shard_shape[1]
    pick = lambda sems, j: sems[j] if isinstance(sems, (list, tuple)) else sems.at[j]
    copies = []
    for j, (fx, fy) in enumerate(_CHIP_FLIPS):
        chip = 2 * (x ^ fx) + (y ^ fy)
        src = psum_ref.at[:, pl.ds(chip * cw, cw)] if axis == 1 else psum_ref.at[pl.ds(chip * hr, hr), :]
        copies.append(pltpu.make_async_remote_copy(
            src_ref=src, dst_ref=recv_ref.at[j], send_sem=pick(s_sem, j), recv_sem=pick(r_sem, j),
            device_id=(x ^ fx, y ^ fy, c), device_id_type=MESH))
    return copies


_HBM_SPEC = pl.BlockSpec(memory_space=pltpu.HBM)
_SEM_SPEC = pl.BlockSpec(memory_space=pltpu.SEMAPHORE)


def _split_ici_copies(names, p_refs, land_refs, sems):
    copies = []
    for i, n in enumerate(names):
        copies += _ici_copies(p_refs[i], land_refs[i], list(sems[6 * i:6 * i + 3]),
                              list(sems[6 * i + 3:6 * i + 6]), dict(BIG)[n], SHARD[n])
    return copies


def _ici_start(name, names, psums):
    nw, ns = len(names), 6 * len(names)
    lands = [lax.empty((3, SHARD[n][0] // 2, SHARD[n][1]), BF16) for n in names]

    def body(*refs):
        for cp in _split_ici_copies(names, refs[:nw], refs[nw:2 * nw], refs[2 * nw:2 * nw + ns]):
            cp.start()
        token = refs[-1]
        token[...] = jnp.zeros_like(token)

    res = pl.pallas_call(
        body, name=name,
        out_shape=(pltpu.SemaphoreType.DMA(()),) * ns
        + tuple(pltpu.HBM(a.shape, BF16) for a in list(psums) + lands)
        + (jax.ShapeDtypeStruct((8, 128), F32),),
        in_specs=(_HBM_SPEC,) * (2 * nw),
        out_specs=(_SEM_SPEC,) * ns + (_HBM_SPEC,) * (2 * nw) + (pl.BlockSpec(memory_space=pltpu.VMEM),),
        input_output_aliases={k: ns + k for k in range(2 * nw)},
        compiler_params=pltpu.CompilerParams(has_side_effects=pltpu.SideEffectType.DATAFLOW_SIDE_EFFECTING),
    )(*[pltpu.with_memory_space_constraint(a, pltpu.HBM) for a in list(psums) + lands])
    return res[:ns], res[ns:ns + nw], res[ns + nw:ns + 2 * nw], res[-1]


def _ici_wait(name, names, sems, p_thru, land_thru, after):
    nw, ns = len(names), 6 * len(names)

    def body(*refs):
        for cp in _split_ici_copies(names, refs[:nw], refs[nw:2 * nw], refs[2 * nw:2 * nw + ns]):
            cp.wait_send()
            cp.wait_recv()

    res = pl.pallas_call(
        body, name=name,
        out_shape=tuple(pltpu.HBM(a.shape, BF16) for a in list(p_thru) + list(land_thru)),
        in_specs=(_HBM_SPEC,) * (2 * nw) + (_SEM_SPEC,) * ns + (pl.BlockSpec(memory_space=pl.ANY),) * len(after),
        out_specs=(_HBM_SPEC,) * (2 * nw), input_output_aliases={k: k for k in range(2 * nw)},
        compiler_params=pltpu.CompilerParams(has_side_effects=pltpu.SideEffectType.DATAFLOW_SIDE_EFFECTING),
    )(*p_thru, *land_thru, *sems, *after)
    return res[:nw], res[nw:]


def _where_am_i():
    x, y, c = _me()
    return jnp.stack([c, 2 * x + y]).astype(I32)


def _sibling():
    x, y, c = _me()
    return (x, y, 1 - c)


N_SEND_SLOTS = 2


def _matmul_tn_pair(name, pos, a, b, m, n, k, shard_rows, *, tm, tn, tk):
    hr = shard_rows // 2
    tm, tn, tk = min(tm, hr), min(tn, n), min(tk, k)
    tph = hr // tm
    nt, nj, nk = (m // 2) // tm, n // tn, k // tk
    n_tiles = nt * nj

    def row_block(p, t, pos_ref):
        half = jnp.where(p == 0, 1 - pos_ref[0], pos_ref[0])
        return (t // tph) * (2 * tph) + half * tph + t % tph

    pieces = b if isinstance(b, list) else [(b, False)]
    starts = _piece_starts(pieces, tn)
    n_b = len(pieces)

    def kern(pos_ref, a_ref, *rest):
        b_refs = rest[:n_b]
        o_ref, acc_ref, send_buf, land_buf, s_sem, r_sem = rest[n_b:]
        p, t, j, kk = pl.program_id(0), pl.program_id(1), pl.program_id(2), pl.program_id(3)
        idx = t * nj + j
        sib = _sibling()

        def copy(i):
            return pltpu.make_async_remote_copy(
                src_ref=send_buf.at[i % N_SEND_SLOTS], dst_ref=land_buf.at[i], send_sem=s_sem.at[i],
                recv_sem=r_sem.at[i], device_id=sib, device_id_type=MESH)

        @pl.when(kk == 0)
        def _():
            acc_ref[...] = jnp.zeros_like(acc_ref)

        for q in range(n_b):
            @pl.when(jnp.logical_and(j >= starts[q], j < starts[q] + _piece_chunks(pieces[q], tn)))
            def _(q=q):
                acc_ref[...] += lax.dot_general(a_ref[...], b_refs[q][...], _TN, preferred_element_type=F32)

        @pl.when(jnp.logical_and(kk == nk - 1, p == 0))
        def _():
            @pl.when(idx >= N_SEND_SLOTS)
            def _():
                copy(idx - N_SEND_SLOTS).wait_send()

            send_buf[idx % N_SEND_SLOTS] = acc_ref[...].astype(BF16)
            copy(idx).start()

        @pl.when(jnp.logical_and(kk == nk - 1, p == 1))
        def _():
            copy(idx).wait_recv()
            o_ref[...] = (acc_ref[...] + land_buf[idx].astype(F32)).astype(BF16)

        @pl.when(jnp.logical_and(jnp.logical_and(p == 1, idx == n_tiles - 1), kk == nk - 1))
        def _():
            for i in range(max(n_tiles - N_SEND_SLOTS, 0), n_tiles):
                copy(i).wait_send()

    grid_spec = pltpu.PrefetchScalarGridSpec(
        num_scalar_prefetch=1, grid=(2, nt, nj, nk),
        in_specs=[pl.BlockSpec((tk, tm), lambda p, t, j, kk, pos_ref: (kk, row_block(p, t, pos_ref)))]
        + [_piece_spec(pc, tk, tn, st, lambda p, t, j, kk, pos_ref: kk, lambda p, t, j, kk, pos_ref: j)
           for pc, st in zip(pieces, starts)],
        out_specs=pl.BlockSpec((tm, tn), lambda p, t, j, kk, pos_ref: (p * t, p * j)),
        scratch_shapes=[pltpu.VMEM((tm, tn), F32), pltpu.VMEM((N_SEND_SLOTS, tm, tn), BF16),
                        pltpu.VMEM((n_tiles, tm, tn), BF16),
                        pltpu.SemaphoreType.DMA((n_tiles,)), pltpu.SemaphoreType.DMA((n_tiles,))])
    return pl.pallas_call(
        kern, name=name, grid_spec=grid_spec, out_shape=jax.ShapeDtypeStruct((m // 2, n), BF16),
        compiler_params=_cparams(("arbitrary",) * 4),
    )(pos, a, *[pc[0] for pc in pieces])


def _rope_tables():
    half = RET_DK // 2
    f32 = np.float32
    inv = np.power(f32(ROPE_BASE), -np.arange(half, dtype=f32) / f32(half)).astype(f32)
    ang = (np.arange(SEQ, dtype=f32)[:, None] * inv[None, :]).astype(f32)
    return jnp.asarray(np.cos(ang).astype(f32)), jnp.asarray(np.sin(ang).astype(f32))


def _decay_tables():
    c = RET_CHUNK
    f32 = np.float32
    log_g = np.log1p(-np.power(f32(2.0), f32(-5.0) - np.arange(RET_HEADS, dtype=f32))).astype(f32)
    idx = np.arange(c, dtype=f32)
    rel = idx[:, None] - idx[None, :]
    din = np.where(rel >= 0, np.exp(log_g[:, None, None] * np.maximum(rel, f32(0.0))), f32(0.0)).astype(f32)
    qd = np.exp(log_g[:, None] * (idx + f32(1.0))).astype(f32)[:, :, None]
    kd = np.exp(log_g[:, None] * (f32(c) - f32(1.0) - idx)).astype(f32)[:, :, None]
    cd = np.exp(log_g * f32(c)).astype(f32)
    return jnp.asarray(din), jnp.asarray(qd), jnp.asarray(kd), jnp.asarray(cd)


def _t5_bucket(dist):
    max_exact = REL_BUCKETS // 2
    d_f = jnp.maximum(dist, 1).astype(F32)
    large = max_exact + (jnp.log(d_f / max_exact) / math.log(REL_MAX_DIST / max_exact)
                         * (REL_BUCKETS - max_exact)).astype(I32)
    large = jnp.minimum(large, REL_BUCKETS - 1)
    return jnp.where(dist < max_exact, dist, large)


def _bucket_tables():
    qi = jnp.arange(ATT_BLK)[:, None]
    kj = jnp.arange(2 * ATT_BLK)[None, :]
    dist = jnp.clip(ATT_BLK + qi - kj, 0, ATT_BLK)
    return jnp.stack([_t5_bucket(dist * dil) for _, dil in ATT_GROUPS]).astype(I32)


def _retention_fwd(rqk, rv, rg, gn_g, gn_b, din, qd, kd, cd):
    nc = SEQ // RET_CHUNK
    c, dk, dv = RET_CHUNK, RET_DK, RET_DV

    def kern(q_ref, k_ref, v_ref, rg_ref, g_ref, b_ref, din_ref, qd_ref, kd_ref, cd_ref,
             o_ref, st_ref, gated_ref, state):
        n = pl.program_id(0)

        @pl.when(n == 0)
        def _():
            state[...] = jnp.zeros_like(state)

        for sub in range(RET_SUB):
            rows = slice(sub * c, (sub + 1) * c)
            for h in range(RET_HEADS):
                q, k = q_ref[rows, h * dk:(h + 1) * dk], k_ref[rows, h * dk:(h + 1) * dk]
                v = v_ref[rows, h * dv:(h + 1) * dv]
                s_b = state[h].astype(BF16)
                st_ref[h, sub] = s_b
                a = lax.dot_general(q, k, _NT, preferred_element_type=F32) * din_ref[h]
                o = jnp.dot(a.astype(BF16), v, preferred_element_type=F32)
                o += jnp.dot(q, s_b, preferred_element_type=F32) * qd_ref[h]
                v_cols = slice(h * dv, (h + 1) * dv)
                o_ref[rows, v_cols] = o
                nrm, _ = _gn_parts(o)
                gate = rg_ref[rows, v_cols].astype(F32)
                gated_ref[rows, v_cols] = ((gate * _sigmoid(gate))
                                           * (nrm * g_ref[:, v_cols] + b_ref[:, v_cols])).astype(BF16)
                kk = (k.astype(F32) * kd_ref[h]).astype(BF16)
                state[h] = state[h] * cd_ref[h] + lax.dot_general(kk, v, _TN, preferred_element_type=F32)

    whole = lambda a: pl.BlockSpec(a.shape, lambda n: (0,) * a.ndim)
    cs = RET_SUB * c
    rows_v = pl.BlockSpec((cs, RET_V_W), lambda n: (n, 0))
    return pl.pallas_call(
        kern, name="retention_fwd", grid=(nc // RET_SUB,),
        in_specs=[
            pl.BlockSpec((cs, RET_QK_W), lambda n: (n, 0)),
            pl.BlockSpec((cs, RET_QK_W), lambda n: (n, 1)),
            rows_v, rows_v, whole(gn_g), whole(gn_b),
            whole(din), whole(qd), whole(kd),
            pl.BlockSpec(memory_space=pltpu.SMEM),
        ],
        out_specs=[
            rows_v,
            pl.BlockSpec((RET_HEADS, RET_SUB, dk, dv), lambda n: (0, n, 0, 0)),
            rows_v,
        ],
        out_shape=[
            jax.ShapeDtypeStruct((SEQ, RET_V_W), F32),
            jax.ShapeDtypeStruct((RET_HEADS, nc, dk, dv), BF16),
            jax.ShapeDtypeStruct((SEQ, RET_V_W), BF16),
        ],
        scratch_shapes=[pltpu.VMEM((RET_HEADS, dk, dv), F32)],
        compiler_params=_cparams(("arbitrary",)),
    )(rqk, rqk, rv, rg, gn_g, gn_b, din, qd, kd, cd)


def _retention_bwd(rqk, rv, states, d_gated, ro, rg, gn_g, gn_b, din, qd, kd, cd, cos, sin):
    nc = SEQ // RET_CHUNK
    c, dk, dv = RET_CHUNK, RET_DK, RET_DV
    half = dk // 2
    last = nc // RET_SUB - 1

    def unrot(g, cs, sn):
        g1, g2 = g[:, :half], g[:, half:]
        return jnp.concatenate([g1 * cs + g2 * sn, g2 * cs - g1 * sn], axis=-1)

    def kern(q_ref, k_ref, v_ref, st_ref, dg_ref, ro_ref, rg_ref, g_ref, b_ref, din_ref, qd_ref, kd_ref,
             cd_ref, cos_ref, sin_ref, out_ref, drg_ref, dgn_g_ref, dgn_b_ref, dstate):
        step = pl.program_id(0)

        @pl.when(step == 0)
        def _():
            dstate[...] = jnp.zeros_like(dstate)
            dgn_g_ref[...] = jnp.zeros_like(dgn_g_ref)
            dgn_b_ref[...] = jnp.zeros_like(dgn_b_ref)

        for sub in reversed(range(RET_SUB)):
            rows = slice(sub * c, (sub + 1) * c)
            cs, sn = cos_ref[rows, :], sin_ref[rows, :]
            for h in range(RET_HEADS):
                qk_cols, v_cols = slice(h * dk, (h + 1) * dk), slice(h * dv, (h + 1) * dv)
                q, k, v = q_ref[rows, qk_cols], k_ref[rows, qk_cols], v_ref[rows, v_cols]
                s_b = st_ref[h, sub]
                nrm, rstd = _gn_parts(ro_ref[rows, v_cols])
                gate, dg = rg_ref[rows, v_cols].astype(F32), dg_ref[rows, v_cols].astype(F32)
                sg = _sigmoid(gate)
                gn_gain = g_ref[:, v_cols]
                drg_ref[rows, v_cols] = (dg * (nrm * gn_gain + b_ref[:, v_cols])
                                         * (sg * (1.0 + gate * (1.0 - sg)))).astype(BF16)
                d_ron = dg * (gate * sg)
                dgn_g_ref[:, v_cols] += jnp.sum(d_ron * nrm, axis=0, keepdims=True)
                dgn_b_ref[:, v_cols] += jnp.sum(d_ron, axis=0, keepdims=True)
                d_n = d_ron * gn_gain
                d_o = rstd * (d_n - jnp.mean(d_n, axis=-1, keepdims=True)
                              - nrm * jnp.mean(d_n * nrm, axis=-1, keepdims=True))
                d_ob = d_o.astype(BF16)
                d_oq = (d_o * qd_ref[h]).astype(BF16)
                ds_b = dstate[h].astype(BF16)
                din_m = din_ref[h]
                a_b = (lax.dot_general(q, k, _NT, preferred_element_type=F32) * din_m).astype(BF16)
                kk = (k.astype(F32) * kd_ref[h]).astype(BF16)
                d_v = lax.dot_general(a_b, d_ob, _TN, preferred_element_type=F32)
                d_v += jnp.dot(kk, ds_b, preferred_element_type=F32)
                d_a = (lax.dot_general(d_ob, v, _NT, preferred_element_type=F32) * din_m).astype(BF16)
                d_q = jnp.dot(d_a, k, preferred_element_type=F32)
                d_q += lax.dot_general(d_oq, s_b, _NT, preferred_element_type=F32)
                d_k = lax.dot_general(d_a, q, _TN, preferred_element_type=F32)
                d_k += lax.dot_general(v, ds_b, _NT, preferred_element_type=F32) * kd_ref[h]
                dstate[h] = dstate[h] * cd_ref[h] + lax.dot_general(q, d_oq, _TN,
                                                                    preferred_element_type=F32)
                out_ref[rows, h * dk:(h + 1) * dk] = unrot(d_q, cs, sn).astype(BF16)
                out_ref[rows, RET_QK_W + h * dk:RET_QK_W + (h + 1) * dk] = (
                    unrot(d_k, cs, sn) * (RET_DK ** -0.5)).astype(BF16)
                out_ref[rows, 2 * RET_QK_W + h * dv:2 * RET_QK_W + (h + 1) * dv] = d_v.astype(BF16)

    whole = lambda a: pl.BlockSpec(a.shape, lambda n: (0,) * a.ndim)
    rs = RET_SUB * c
    rows_v = pl.BlockSpec((rs, RET_V_W), lambda n: (last - n, 0))
    return pl.pallas_call(
        kern, name="retention_bwd", grid=(nc // RET_SUB,),
        in_specs=[
            pl.BlockSpec((rs, RET_QK_W), lambda n: (last - n, 0)),
            pl.BlockSpec((rs, RET_QK_W), lambda n: (last - n, 1)),
            rows_v,
            pl.BlockSpec((RET_HEADS, RET_SUB, dk, dv), lambda n: (0, last - n, 0, 0)),
            rows_v, rows_v, rows_v, whole(gn_g), whole(gn_b),
            whole(din), whole(qd), whole(kd),
            pl.BlockSpec(memory_space=pltpu.SMEM),
            pl.BlockSpec((rs, half), lambda n: (last - n, 0)),
            pl.BlockSpec((rs, half), lambda n: (last - n, 0)),
        ],
        out_specs=[pl.BlockSpec((rs, 2 * RET_QK_W + RET_V_W), lambda n: (last - n, 0)), rows_v,
                   whole(gn_g), whole(gn_b)],
        out_shape=[jax.ShapeDtypeStruct((SEQ, 2 * RET_QK_W + RET_V_W), BF16),
                   jax.ShapeDtypeStruct((SEQ, RET_V_W), BF16),
                   jax.ShapeDtypeStruct((1, RET_V_W), F32), jax.ShapeDtypeStruct((1, RET_V_W), F32)],
        scratch_shapes=[pltpu.VMEM((RET_HEADS, dk, dv), F32)],
        compiler_params=_cparams(("arbitrary",)),
    )(rqk, rqk, rv, states, d_gated, ro, rg, gn_g, gn_b, din, qd, kd, cd, cos, sin)


def _bias_build(rel_bias, buckets):
    ng = len(ATT_GROUPS)

    def kern(tab_ref, bkt_ref, o_ref):
        g, h = pl.program_id(0), pl.program_id(1)
        bkt = bkt_ref[...]
        acc = jnp.zeros(bkt.shape, F32)
        for b in range(REL_BUCKETS):
            acc = jnp.where(bkt == b, tab_ref[b, g * ATT_HPG + h], acc)
        o_ref[...] = acc

    return pl.pallas_call(
        kern, name="bias_build", grid=(ng, ATT_HPG),
        in_specs=[pl.BlockSpec(memory_space=pltpu.SMEM),
                  pl.BlockSpec((None, ATT_BLK, 2 * ATT_BLK), lambda g, h: (g, 0, 0))],
        out_specs=pl.BlockSpec((None, None, ATT_BLK, 2 * ATT_BLK), lambda g, h: (g, h, 0, 0)),
        out_shape=jax.ShapeDtypeStruct((ng, ATT_HPG, ATT_BLK, 2 * ATT_BLK), F32),
        compiler_params=_cparams(("arbitrary", "arbitrary")),
    )(rel_bias, buckets)


def _bias_grad(dsb, buckets):
    ng = len(ATT_GROUPS)

    def kern(ds_ref, bkt_ref, o_ref):
        g, h = pl.program_id(0), pl.program_id(1)
        bkt, ds = bkt_ref[...], ds_ref[...]
        for b in range(REL_BUCKETS):
            o_ref[b, g * ATT_HPG + h] = jnp.sum(jnp.where(bkt == b, ds, 0.0))

    return pl.pallas_call(
        kern, name="bias_grad", grid=(ng, ATT_HPG),
        in_specs=[pl.BlockSpec((None, None, ATT_BLK, 2 * ATT_BLK), lambda g, h: (g, h, 0, 0)),
                  pl.BlockSpec((None, ATT_BLK, 2 * ATT_BLK), lambda g, h: (g, 0, 0))],
        out_specs=pl.BlockSpec(memory_space=pltpu.SMEM),
        out_shape=jax.ShapeDtypeStruct((REL_BUCKETS, N_ATT_HEADS), F32),
        compiler_params=_cparams(("arbitrary", "arbitrary")),
    )(dsb, buckets)


_NT = (((1,), (1,)), ((), ()))
_TN = (((0,), (0,)), ((), ()))
_ATT_SCALE = ATT_DH ** -0.5


def _window_mask(has_prev):
    qi = lax.broadcasted_iota(I32, (ATT_BLK, 2 * ATT_BLK), 0)
    kj = lax.broadcasted_iota(I32, (ATT_BLK, 2 * ATT_BLK), 1)
    prev_ok = jnp.logical_and(jnp.logical_and(kj < ATT_BLK, kj >= qi), has_prev)
    return jnp.logical_or(prev_ok, jnp.logical_and(kj >= ATT_BLK, qi >= kj - ATT_BLK))


def _head_specs(col0):
    return pl.BlockSpec((SEQ, ATT_DH), lambda h: (0, col0 + h))


def _sub_rows(start, size, dil):
    return pl.ds(start, size) if dil == 1 else pl.ds(start, size, stride=dil)


def _att_blocks(dil):
    nb = SEQ // dil // ATT_BLK
    return [(r + dil * n * ATT_BLK, n > 0, n + 1 < nb) for r in range(dil) for n in range(nb)]


def _att_fwd(gi, dil, qkv, bias):
    blk, dh = ATT_BLK, ATT_DH
    pad = dil * blk
    col0 = 3 * ATT_HPG * gi

    def kern(q_ref, k_ref, v_ref, b_ref, o_ref, l_ref, qf, kpad, vpad):
        zero = jnp.zeros((pad, dh), F32)
        kpad[0:pad, :] = zero
        vpad[0:pad, :] = zero
        kpad[pad:, :] = k_ref[...].astype(F32)
        vpad[pad:, :] = v_ref[...].astype(F32)
        qf[...] = q_ref[...].astype(F32)
        bias_m = b_ref[...]
        for start, has_prev, _ in _att_blocks(dil):
            rows, window = _sub_rows(start, blk, dil), _sub_rows(start, 2 * blk, dil)
            q = qf[rows, :].astype(BF16)
            kw, vw = kpad[window, :].astype(BF16), vpad[window, :].astype(BF16)
            valid = _window_mask(has_prev)
            s = lax.dot_general(q, kw, _NT, preferred_element_type=F32) * _ATT_SCALE + bias_m
            s = jnp.where(valid, s, -1e30)
            mx = jnp.max(s, axis=-1, keepdims=True)
            e = jnp.exp(s - mx)
            den = jnp.sum(e, axis=-1, keepdims=True)
            o_ref[rows, :] = jnp.dot((e / den).astype(BF16), vw, preferred_element_type=F32)
            l_ref[rows, :] = jnp.broadcast_to(mx + jnp.log(den), (blk, dh))

    return pl.pallas_call(
        kern, name=f"att_fwd_g{gi}", grid=(ATT_HPG,),
        in_specs=[_head_specs(col0), _head_specs(col0 + ATT_HPG), _head_specs(col0 + 2 * ATT_HPG),
                  pl.BlockSpec((None, None, blk, 2 * blk), lambda h: (gi, h, 0, 0))],
        out_specs=[_head_specs(0), _head_specs(0)],
        out_shape=[jax.ShapeDtypeStruct((SEQ, ATT_W), F32), jax.ShapeDtypeStruct((SEQ, ATT_W), F32)],
        scratch_shapes=[pltpu.VMEM((SEQ, dh), F32), pltpu.VMEM((SEQ + pad, dh), F32),
                        pltpu.VMEM((SEQ + pad, dh), F32)],
        compiler_params=_cparams(("arbitrary",)),
    )(qkv, qkv, qkv, bias)


def _att_bwd(gi, dil, qkv, d_att, lse, dd, bias):
    blk, dh = ATT_BLK, ATT_DH
    pad = dil * blk
    col0 = 3 * ATT_HPG * gi

    def kern(q_ref, k_ref, v_ref, do_ref, l_ref, d_ref, b_ref, dqkv_ref, dsb_ref,
             qf, kpad, vpad, dq_s, dkpad, dvpad):
        zero = jnp.zeros((pad, dh), F32)
        kpad[0:pad, :] = zero
        vpad[0:pad, :] = zero
        kpad[pad:, :] = k_ref[...].astype(F32)
        vpad[pad:, :] = v_ref[...].astype(F32)
        qf[...] = q_ref[...].astype(F32)
        dkpad[...] = jnp.zeros_like(dkpad)
        dvpad[...] = jnp.zeros_like(dvpad)
        bias_m = b_ref[...]
        ds_sum = jnp.zeros((blk, 2 * blk), F32)

        for start, has_prev, _ in _att_blocks(dil):
            rows, window = _sub_rows(start, blk, dil), _sub_rows(start, 2 * blk, dil)
            q, d_o = qf[rows, :].astype(BF16), do_ref[rows, :].astype(BF16)
            kw, vw = kpad[window, :].astype(BF16), vpad[window, :].astype(BF16)
            lrow, drow = l_ref[rows, :][:, :1], d_ref[rows, :][:, :1]
            valid = _window_mask(has_prev)
            s = lax.dot_general(q, kw, _NT, preferred_element_type=F32) * _ATT_SCALE + bias_m
            p = jnp.where(valid, jnp.exp(jnp.where(valid, s, -1e30) - lrow), 0.0)
            dp = lax.dot_general(d_o, vw, _NT, preferred_element_type=F32)
            ds = p * (dp - drow)
            ds_b = ds.astype(BF16)
            dq_s[rows, :] = jnp.dot(ds_b, kw, preferred_element_type=F32) * _ATT_SCALE
            dkpad[window, :] += lax.dot_general(ds_b, q, _TN, preferred_element_type=F32) * _ATT_SCALE
            dvpad[window, :] += lax.dot_general(p.astype(BF16), d_o, _TN, preferred_element_type=F32)
            ds_sum = ds_sum + ds
        dsb_ref[...] = ds_sum

        dqkv_ref[0] = dq_s[...].astype(BF16)
        dqkv_ref[1] = dkpad[pad:, :].astype(BF16)
        dqkv_ref[2] = dvpad[pad:, :].astype(BF16)

    return pl.pallas_call(
        kern, name=f"att_bwd_g{gi}", grid=(ATT_HPG,),
        in_specs=[_head_specs(col0), _head_specs(col0 + ATT_HPG), _head_specs(col0 + 2 * ATT_HPG),
                  _head_specs(0), _head_specs(0), _head_specs(0),
                  pl.BlockSpec((None, None, blk, 2 * blk), lambda h: (gi, h, 0, 0))],
        out_specs=[pl.BlockSpec((3, SEQ, dh), lambda h: (0, 0, h)),
                   pl.BlockSpec((None, blk, 2 * blk), lambda h: (h, 0, 0))],
        out_shape=[jax.ShapeDtypeStruct((3, SEQ, ATT_W), BF16),
                   jax.ShapeDtypeStruct((ATT_HPG, blk, 2 * blk), F32)],
        scratch_shapes=[pltpu.VMEM((SEQ, dh), F32), pltpu.VMEM((SEQ + pad, dh), F32),
                        pltpu.VMEM((SEQ + pad, dh), F32), pltpu.VMEM((SEQ, dh), F32),
                        pltpu.VMEM((SEQ + pad, dh), F32), pltpu.VMEM((SEQ + pad, dh), F32)],
        compiler_params=_cparams(("arbitrary",)),
    )(qkv, qkv, qkv, d_att, lse, dd, bias)


def _rms_parts(x):
    r = lax.rsqrt(jnp.mean(x * x, axis=-1, keepdims=True) + RMS_EPS)
    return x * r, r


def _rms_bwd(d_xhat, xhat, r):
    return r * (d_xhat - xhat * jnp.mean(d_xhat * xhat, axis=-1, keepdims=True))


def _prenorm_fwd(name, x, gain, shift, scale):
    def body(xt, g, sh, sc):
        xhat, _ = _rms_parts(xt)
        return (xhat * g) * (1.0 + sc) + sh
    return _rowmap(name, body, [x], [gain, shift, scale], [(D_MODEL, BF16)])[0]


def _prenorm_bwd(name, d_h, x, gain, scale, resid, branch=None, gate=None, after=()):
    gated = branch is not None

    def body(d_ht, xt, res, *rest):
        g, sc = rest[-2 - gated], rest[-1 - gated]
        xhat, r = _rms_parts(xt)
        nrm = xhat * g
        d_n = d_ht * (1.0 + sc)
        dx = _rms_bwd(d_n * g, xhat, r) + res
        sums = (jnp.sum(d_ht, axis=0, keepdims=True), jnp.sum(d_ht * nrm, axis=0, keepdims=True),
                jnp.sum(d_n * xhat, axis=0, keepdims=True))
        if not gated:
            return (dx,) + sums
        return (dx, dx * rest[-1]) + sums + (jnp.sum(dx * rest[0], axis=0, keepdims=True),)

    return _rowmap(name, body, [d_h, x, resid] + ([branch] if gated else []),
                   [gain, scale] + ([gate] if gated else []),
                   [(D_MODEL, F32)] + ([(D_MODEL, BF16)] if gated else []),
                   [D_MODEL] * (3 + gated), after=after)


def _gn_parts(ro):
    mu = jnp.mean(ro, axis=-1, keepdims=True)
    cen = ro - mu
    rstd = lax.rsqrt(jnp.mean(cen * cen, axis=-1, keepdims=True) + GN_EPS)
    return cen * rstd, rstd


def _combine(os_, ls_, after=()):
    def body(o0, o1, o2, l0, l1, l2):
        mx = jnp.maximum(jnp.maximum(l0, l1), l2)
        e0, e1, e2 = jnp.exp(l0 - mx), jnp.exp(l1 - mx), jnp.exp(l2 - mx)
        den = e0 + e1 + e2
        att = (e0 / den) * o0 + (e1 / den) * o1 + (e2 / den) * o2
        return att, att, mx + jnp.log(den)
    return _rowmap("att_combine", body, list(os_) + list(ls_), [],
                   [(ATT_W, F32), (ATT_W, BF16), (ATT_W, F32)], after=after)


def _att_bwd_pre(d_att, att):
    def body(dt, at):
        outs = []
        for h in range(ATT_HPG):
            sl = slice(h * ATT_DH, (h + 1) * ATT_DH)
            outs.append(jnp.broadcast_to(jnp.sum(dt[:, sl] * at[:, sl], axis=-1, keepdims=True),
                                         (dt.shape[0], ATT_DH)))
        return jnp.concatenate(outs, axis=-1)
    return _rowmap("att_bwd_pre", body, [d_att, att], [], [(ATT_W, F32)])[0]


def _merge_fwd(gates, ret_out, att_out):
    def body(gt, ro, ao):
        gt = gt.astype(F32)
        return _sigmoid(gt[:, :D_MODEL]) * ro + _sigmoid(gt[:, D_MODEL:]) * ao
    return _rowmap("merge_fwd", body, [gates, ret_out, att_out], [], [(D_MODEL, BF16)])[0]


def _merge_bwd(d_merged, gates, ret_out, att_out):
    def body(dm, gt, ro, ao):
        dm, gt = dm.astype(F32), gt.astype(F32)
        sa, sb = _sigmoid(gt[:, :D_MODEL]), _sigmoid(gt[:, D_MODEL:])
        d_gates = jnp.concatenate([dm * ro * (sa * (1.0 - sa)), dm * ao * (sb * (1.0 - sb))], axis=-1)
        return dm * sa, dm * sb, d_gates
    return _rowmap("merge_bwd", body, [d_merged, gates, ret_out, att_out], [],
                   [(D_MODEL, BF16), (D_MODEL, BF16), (2 * D_MODEL, BF16)])


def _loss_head(x3, target, gain, branch, gate):
    def body(xt, tt, br, g, gt):
        xhat, r = _rms_parts(xt)
        err = xhat * g - tt
        d_y = err / D_MODEL
        loss = 0.5 * jnp.sum(jnp.mean(err * err, axis=-1, keepdims=True), axis=0, keepdims=True)
        d_x = _rms_bwd(d_y * g, xhat, r)
        return (d_x, d_x * gt, jnp.broadcast_to(loss, (1, 128)), jnp.sum(d_y * xhat, axis=0, keepdims=True),
                jnp.sum(d_x * br, axis=0, keepdims=True))
    return _rowmap("loss_head", body, [x3, target, branch], [gain, gate],
                   [(D_MODEL, F32), (D_MODEL, BF16)], [128, D_MODEL, D_MODEL])


def _local_step(pos, x, target, mod, norm1_g, norm2_g, norm_f_g, rel_bias, gn_g, gn_b, w_in, rest_gather):
    sh1, sc1, g1, sh2, sc2, g2 = [mod[:, i * D_MODEL:(i + 1) * D_MODEL] for i in range(6)]
    cos, sin = _rope_tables()
    din, qd, kd, cd = _decay_tables()
    buckets = _bucket_tables()
    bias = _bias_build(rel_bias, buckets)
    dils = [d for _, d in ATT_GROUPS]

    h1 = _prenorm_fwd("prenorm1_fwd", x, norm1_g, sh1, sc1)

    qk_tn = 2 * RET_DK

    def rot_epi(acc, cs, sn, scale):
        half = RET_DK // 2
        outs = []
        for h0 in range(0, qk_tn, RET_DK):
            x1, x2 = acc[:, h0:h0 + half], acc[:, h0 + half:h0 + RET_DK]
            outs += [x1 * cs - x2 * sn, x1 * sn + x2 * cs]
        return (jnp.concatenate(outs, axis=-1) * scale,)

    qk_scale = jnp.concatenate([jnp.ones((1, RET_QK_W), F32),
                                jnp.full((1, RET_QK_W), RET_DK ** -0.5, F32)], axis=-1)
    rope_ex = [(cos, (TM, RET_DK // 2), lambda i, j, kk: (i, 0)),
               (sin, (TM, RET_DK // 2), lambda i, j, kk: (i, 0)),
               (qk_scale, (1, qk_tn), lambda i, j, kk: (0, j))]
    rest_sems, rest_shards, rest_fulls, rest_token = rest_gather
    behind = [rest_token]
    rv = _matmul("proj_rv", h1, w_in, "nn", SEQ, RET_V_W, D_MODEL, [BF16], b_off=OFF_V, tk=D_MODEL,
                 after=behind)[0]
    rg = _matmul("proj_rg", h1, w_in, "nn", SEQ, RET_V_W, D_MODEL, [BF16], b_off=OFF_G, tk=D_MODEL,
                 after=behind)[0]
    gates = _matmul("proj_gates", h1, w_in, "nn", SEQ, 2 * D_MODEL, D_MODEL, [BF16], b_off=OFF_GATE,
                    tn=512, tk=D_MODEL, after=behind)[0]
    aqkv = _matmul("proj_att", h1, w_in, "nn", SEQ, 9 * ATT_W, D_MODEL, [BF16], b_off=OFF_ATT,
                   tn=512, tk=D_MODEL, after=behind)[0]

    os_, ls_ = [], []
    for gi in range(3):
        o_g, l_g = _att_fwd(gi, dils[gi], aqkv, bias)
        os_.append(o_g)
        ls_.append(l_g)

    rqk = _matmul("proj_qk", h1, w_in, "nn", SEQ, 2 * RET_QK_W, D_MODEL, [BF16], b_off=OFF_Q,
                  tn=qk_tn, tk=D_MODEL, epilogue=rot_epi, extras=rope_ex, after=behind)[0]
    ro, states, gated = _retention_fwd(rqk, rv, rg, gn_g, gn_b, din, qd, kd, cd)
    rest_sems, rest_fulls, fwd_token = _gather_rest_forward(rest_sems, rest_shards, rest_fulls,
                                                            [gated, gates] + os_)
    att, att_b, lse = _combine(os_, ls_, after=[fwd_token])
    w_ret_out, w_att_out, w_o, w_ff1, w_ff2 = _gather_rest_end(rest_sems, rest_fulls, [att_b])
    ret_out = _matmul("ret_out", gated, w_ret_out, "nn", SEQ, D_MODEL, RET_V_W, [F32], tk=RET_V_W)[0]
    att_out = _matmul("att_out", att_b, w_att_out, "nn", SEQ, D_MODEL, ATT_W, [F32])[0]

    merged = _merge_fwd(gates, ret_out, att_out)

    def resid_epi(acc, xt, g):
        return xt + g * acc, acc

    def resid_ex(xin, g):
        return [(xin, (TM, TN), lambda i, j, kk: (i, j)), (g, (1, TN), lambda i, j, kk: (0, j))]

    x2, mix = _matmul("mix_out", merged, w_o, "nn", SEQ, D_MODEL, D_MODEL, [F32, BF16],
                      epilogue=resid_epi, extras=resid_ex(x, g1))
    h2 = _prenorm_fwd("prenorm2_fwd", x2, norm2_g, sh2, sc2)

    def relu2_epi(acc):
        r = jnp.maximum(acc, 0.0)
        return r * r, r

    act, relu_u = _matmul("ff1", h2, w_ff1, "nn", SEQ, D_FF, D_MODEL, [BF16, BF16], tk=D_MODEL,
                          epilogue=relu2_epi)
    x3, y2 = _matmul("ff2", act, w_ff2, "nn", SEQ, D_MODEL, D_FF, [F32, BF16], tk=2048,
                     epilogue=resid_epi, extras=resid_ex(x2, g2))

    d_x3, d_y2, loss, d_gf, d_g2 = _loss_head(x3, target, norm_f_g, y2, g2)

    def relu2_bwd_epi(acc, rt):
        return (acc * (2.0 * rt.astype(F32)),)

    gw_ff2 = _matmul_tn_pair("ff2_dw", pos, act, d_y2, D_FF, D_MODEL, SEQ, D_FF // N_CHIPS,
                             tm=512, tn=1024, tk=SEQ)
    d_u = _matmul("ff2_dx", d_y2, w_ff2, "nt", SEQ, D_FF, D_MODEL, [BF16], epilogue=relu2_bwd_epi,
                  extras=[(relu_u, (TM, TN), lambda i, j, kk: (i, j))])[0]
    gw_ff1 = _matmul_tn_pair("ff1_dw", pos, h2, d_u, D_MODEL, D_FF, SEQ, D_MODEL,
                             tm=512, tn=1024, tk=SEQ)
    ffn = ["w_ff2", "w_ff1"]
    ffn_started = _ici_start("ici_start_ffn", ffn, [gw_ff2, gw_ff1])
    d_h2 = _matmul("ff1_dx", d_u, w_ff1, "nt", SEQ, D_MODEL, D_FF, [F32], tk=2048,
                   after=[ffn_started[3]])[0]
    d_x2, d_mix, d_sh2, d_sc2, d_n2g, d_g1 = _prenorm_bwd("prenorm2_bwd", d_h2, x2, norm2_g, sc2, d_x3,
                                                          branch=mix, gate=g1)
    gw_o = _matmul_tn_pair("mix_dw", pos, merged, d_mix, D_MODEL, D_MODEL, SEQ, D_MODEL // N_CHIPS,
                           tm=128, tn=1024, tk=2048)
    d_merged = _matmul("mix_dx", d_mix, w_o, "nt", SEQ, D_MODEL, D_MODEL, [BF16])[0]
    d_ret_out, d_att_out, d_gates = _merge_bwd(d_merged, gates, ret_out, att_out)

    gw_ret_out = _matmul_tn_pair("ret_out_dw", pos, gated, d_ret_out, RET_V_W, D_MODEL, SEQ,
                                 RET_V_W // N_CHIPS, tm=256, tn=1024, tk=SEQ)
    gw_att_out = _matmul_tn_pair("att_out_dw", pos, att_b, d_att_out, ATT_W, D_MODEL, SEQ, ATT_W,
                                 tm=256, tn=1024, tk=2048)
    mixer = ["w_o", "w_ret_out", "w_att_out"]
    mixer_started = _ici_start("ici_start_mixer", mixer, [gw_o, gw_ret_out, gw_att_out])
    d_gated = _matmul("ret_out_dx", d_ret_out, w_ret_out, "nt", SEQ, RET_V_W, D_MODEL, [BF16],
                      after=[mixer_started[3]])[0]
    d_att = _matmul("att_out_dx", d_att_out, w_att_out, "nt", SEQ, ATT_W, D_MODEL, [F32],
                    after=[mixer_started[3]])[0]

    d_rqkv, d_rg, d_gn_g, d_gn_b = _retention_bwd(rqk, rv, states, d_gated, ro, rg, gn_g, gn_b,
                                                  din, qd, kd, cd, cos, sin)

    dd = _att_bwd_pre(d_att, att)
    d_aqkv, dsbs = [], []
    for gi in range(3):
        dqkv, dsb = _att_bwd(gi, dils[gi], aqkv, d_att, lse, dd, bias)
        d_aqkv.append(dqkv)
        dsbs.append(dsb)
    d_rel_bias = _bias_grad(jnp.stack(dsbs), buckets)

    d_proj = [(d_rqkv, False), (d_rg, False)] + [(t, True) for t in d_aqkv] + [(d_gates, False)]
    gw_in = _matmul_tn_pair("proj_dw", pos, h1, d_proj, D_MODEL, IN_COLS, SEQ, D_MODEL,
                            tm=512, tn=ATT_W, tk=SEQ)
    sems, (gw_in,), (land,), token = _ici_start("ici_start_w_in", ["w_in"], [gw_in])
    d_h1 = _matmul("proj_dx", d_proj, w_in, "nt", SEQ, D_MODEL, IN_COLS, [F32], tn=1024, tk=ATT_W,
                   after=[token])[0]
    pending = (sems, land)

    names = ffn + mixer
    psums, got = _ici_wait("ici_wait_rest", names, list(ffn_started[0]) + list(mixer_started[0]),
                           list(ffn_started[1]) + list(mixer_started[1]),
                           list(ffn_started[2]) + list(mixer_started[2]), [d_h1])
    g_big = {n: _final_sum("final_" + n, pos, dict(BIG)[n], psums[i], got[i], SHARD[n])
             for i, n in enumerate(names)}
    grad_x, d_sh1, d_sc1, d_n1g = _prenorm_bwd("prenorm1_bwd", d_h1, x, norm1_g, sc1, d_x2,
                                               after=list(g_big.values()))
    d_mod = jnp.concatenate([d_sh1, d_sc1, d_g1, d_sh2, d_sc2, d_g2], axis=-1)
    small = dict(norm1_g=d_n1g, norm2_g=d_n2g, norm_f_g=d_gf, gn_g=d_gn_g, gn_b=d_gn_b,
                 rel_bias=d_rel_bias)
    return loss, grad_x, d_mod, small, g_big, (gw_in,) + pending


def _me():
    return lax.axis_index("x"), lax.axis_index("y"), lax.axis_index("c")


def _peer(x, y, c, mask):
    return (x ^ ((mask >> 2) & 1), y ^ ((mask >> 1) & 1), c ^ (mask & 1))


def _gather8(src_ref, dst_ref, send_sems, recv_sems):
    x, y, c = _me()
    me = 4 * x + 2 * y + c
    copies = []
    for mask in range(1, N_DEV):
        cp = pltpu.make_async_remote_copy(
            src_ref=src_ref, dst_ref=dst_ref.at[me], send_sem=send_sems.at[mask - 1],
            recv_sem=recv_sems.at[mask - 1], device_id=_peer(x, y, c, mask), device_id_type=MESH)
        cp.start()
        copies.append(cp)
    dst_ref[me] = src_ref[...]
    for cp in copies:
        cp.wait_recv()
    for cp in copies:
        cp.wait_send()


def _ada_fwd(c_in, w_ada, b_ada):
    ncol = ADA_COLS // N_CHIPS

    def body(c_ref, w_ref, b_ref, mod_ref, sc_ref, cbuf, cg, mbuf, mg, s1, r1, s2, r2):
        x, y, c = _me()
        me = 4 * x + 2 * y + c
        cv = c_ref[...]
        cbuf[...] = jnp.broadcast_to(cv * _sigmoid(cv), cbuf.shape)
        _gather8(cbuf, cg, s1, r1)
        rows = lax.broadcasted_iota(I32, (N_DEV, D_MODEL), 0)
        sc_all = jnp.zeros((N_DEV, D_MODEL), F32)
        for d in range(N_DEV):
            sc_all = jnp.where(rows == d, cg[d], sc_all)
        sc_ref[...] = sc_all
        mbuf[...] = jnp.dot(sc_all.astype(BF16), w_ref[...].astype(BF16), preferred_element_type=F32)
        _gather8(mbuf, mg, s2, r2)
        rowsel = lax.broadcasted_iota(I32, (N_DEV, ncol), 0) == me
        for k in range(N_CHIPS):
            blk = mg[2 * k]
            row = jnp.sum(jnp.where(rowsel, blk, 0.0), axis=0, keepdims=True)
            mod_ref[:, k * ncol:(k + 1) * ncol] = row + b_ref[:, k * ncol:(k + 1) * ncol]

    vm = pl.BlockSpec(memory_space=pltpu.VMEM)
    return pl.pallas_call(
        body, name="ada_fwd",
        in_specs=[vm, vm, vm], out_specs=[vm, vm],
        out_shape=[jax.ShapeDtypeStruct((1, ADA_COLS), F32), jax.ShapeDtypeStruct((N_DEV, D_MODEL), F32)],
        scratch_shapes=[
            pltpu.VMEM((8, D_MODEL), F32), pltpu.VMEM((N_DEV, 8, D_MODEL), F32),
            pltpu.VMEM((8, ncol), F32), pltpu.VMEM((N_DEV, 8, ncol), F32),
            pltpu.SemaphoreType.DMA((N_DEV - 1,)), pltpu.SemaphoreType.DMA((N_DEV - 1,)),
            pltpu.SemaphoreType.DMA((N_DEV - 1,)), pltpu.SemaphoreType.DMA((N_DEV - 1,)),
        ],
        compiler_params=pltpu.CompilerParams(vmem_limit_bytes=VMEM_LIMIT_V7X),
    )(c_in, w_ada, b_ada)


def _small_reduce(pack, sc_all):
    ncol = ADA_COLS // N_CHIPS

    def body(p_ref, sc_ref, tot_ref, gw_ref, pg, s1, r1):
        x, y, _ = _me()
        chip = 2 * x + y
        _gather8(p_ref, pg, s1, r1)
        tot = pg[0]
        for d in range(1, N_DEV):
            tot = tot + pg[d]
        tot_ref[...] = tot
        rows = lax.broadcasted_iota(I32, (N_DEV, ncol), 0)
        dmod = jnp.zeros((N_DEV, ncol), F32)
        for k in range(N_CHIPS):
            part = jnp.zeros((N_DEV, ncol), F32)
            for d in range(N_DEV):
                part = jnp.where(rows == d, pg[d, :, k * ncol:(k + 1) * ncol][0:1, :], part)
            dmod = jnp.where(chip == k, part, dmod)
        gw_ref[...] = lax.dot_general(sc_ref[...].astype(BF16), dmod.astype(BF16), _TN,
                                      preferred_element_type=F32)

    vm = pl.BlockSpec(memory_space=pltpu.VMEM)
    return pl.pallas_call(
        body, name="small_reduce",
        in_specs=[vm, vm], out_specs=[vm, vm],
        out_shape=[jax.ShapeDtypeStruct((8, ADA_COLS), F32), jax.ShapeDtypeStruct((D_MODEL, ncol), F32)],
        scratch_shapes=[pltpu.VMEM((N_DEV, 8, ADA_COLS), F32),
                        pltpu.SemaphoreType.DMA((N_DEV - 1,)), pltpu.SemaphoreType.DMA((N_DEV - 1,))],
        compiler_params=pltpu.CompilerParams(vmem_limit_bytes=VMEM_LIMIT_V7X),
    )(pack, sc_all)


BIG = (("w_in", 1), ("w_ret_out", 0), ("w_att_out", 1), ("w_o", 0), ("w_ff1", 1), ("w_ff2", 0))
SHARD = {"w_in": (D_MODEL, IN_COLS // N_CHIPS), "w_ret_out": (RET_V_W // N_CHIPS, D_MODEL),
         "w_att_out": (ATT_W, D_MODEL // N_CHIPS), "w_o": (D_MODEL // N_CHIPS, D_MODEL),
         "w_ff1": (D_MODEL, D_FF // N_CHIPS), "w_ff2": (D_FF // N_CHIPS, D_MODEL)}
_CHIP_FLIPS = ((1, 0), (0, 1), (1, 1))


def _region(ref, axis, chip, half, shard_shape):
    r, cw = shard_shape
    hr = r // 2
    if axis == 1:
        return ref.at[pl.ds(half * hr, hr), pl.ds(chip * cw, cw)]
    return ref.at[pl.ds(chip * r + half * hr, hr), :]


def _gather_weights(shards, n_remote):
    nw = len(BIG)
    shapes = [s.shape for s in shards]
    full_shapes = [(r, N_CHIPS * cw) if ax == 1 else (N_CHIPS * r, cw)
                   for (r, cw), (_, ax) in zip(shapes, BIG)]

    def body(*refs):
        ins, outs = refs[:nw], refs[nw:2 * nw]
        own = refs[2 * nw:3 * nw]
        from_ici, from_sib = refs[3 * nw:3 * nw + n_remote], refs[3 * nw + n_remote:3 * nw + 2 * n_remote]
        ld_sem, st_sem, s_ici, r_ici, s_d2d, r_d2d, st_a, st_b = refs[3 * nw + 2 * n_remote:]
        x, y, c = _me()
        chip = 2 * x + y
        sib = (x, y, 1 - c)
        loads = [pltpu.make_async_copy(ins[i], own[i], ld_sem.at[i]) for i in range(nw)]
        for cp in loads:
            cp.start()
        pending, first = [], []
        for i, (_, ax) in enumerate(BIG):
            r, cw = shapes[i]
            hr = r // 2
            loads[i].wait()
            dst = outs[i].at[:, pl.ds(chip * cw, cw)] if ax == 1 else outs[i].at[pl.ds(chip * r, r), :]
            cp = pltpu.make_async_copy(own[i], dst, st_sem.at[i])
            cp.start()
            pending.append(cp)
            for j, (fx, fy) in enumerate(_CHIP_FLIPS if i < n_remote else ()):
                rc = pltpu.make_async_remote_copy(
                    src_ref=own[i].at[pl.ds(c * hr, hr), :], dst_ref=from_ici[i].at[j],
                    send_sem=s_ici.at[j * nw + i], recv_sem=r_ici.at[j * nw + i],
                    device_id=(x ^ fx, y ^ fy, c), device_id_type=MESH)
                rc.start()
                first.append((j, i, rc))
        passed = []
        for j, i, rc in first:
            fx, fy = _CHIP_FLIPS[j]
            src_chip = 2 * (x ^ fx) + (y ^ fy)
            ax = BIG[i][1]
            rc.wait_recv()
            fw = pltpu.make_async_remote_copy(
                src_ref=from_ici[i].at[j], dst_ref=from_sib[i].at[j], send_sem=s_d2d.at[j * nw + i],
                recv_sem=r_d2d.at[j * nw + i], device_id=sib, device_id_type=MESH)
            fw.start()
            passed.append((j, i, src_chip, fw))
            st = pltpu.make_async_copy(from_ici[i].at[j], _region(outs[i], ax, src_chip, c, shapes[i]),
                                       st_a.at[j * nw + i])
            st.start()
            pending.append(st)
        for j, i, src_chip, fw in passed:
            fw.wait_recv()
            st = pltpu.make_async_copy(from_sib[i].at[j],
                                       _region(outs[i], BIG[i][1], src_chip, 1 - c, shapes[i]),
                                       st_b.at[j * nw + i])
            st.start()
            pending.append(st)
        for _, _, rc in first:
            rc.wait_send()
        for _, _, _, fw in passed:
            fw.wait_send()
        for cp in pending:
            cp.wait()

    hbm = pl.BlockSpec(memory_space=pl.ANY)
    halves = [pltpu.VMEM((3, r // 2, cw), BF16) for r, cw in shapes[:n_remote]]
    return pl.pallas_call(
        body, name="gather_weights",
        in_specs=[hbm] * nw, out_specs=[hbm] * nw,
        out_shape=[jax.ShapeDtypeStruct(fs, BF16) for fs in full_shapes],
        scratch_shapes=[pltpu.VMEM(sh, BF16) for sh in shapes] + halves + halves
        + [pltpu.SemaphoreType.DMA((nw,)), pltpu.SemaphoreType.DMA((nw,))]
        + [pltpu.SemaphoreType.DMA((3 * nw,))] * 6,
        compiler_params=pltpu.CompilerParams(vmem_limit_bytes=VMEM_LIMIT_V7X),
    )(*shards)


REST = BIG[1:]
_SIDE_EFFECTS = pltpu.CompilerParams(has_side_effects=pltpu.SideEffectType.DATAFLOW_SIDE_EFFECTING)
_ANY_SPEC = pl.BlockSpec(memory_space=pl.ANY)


def _rest_ici_copies(shard_refs, full_refs, sems):
    x, y, c = _me()
    chip = 2 * x + y
    n = 3 * len(REST)
    copies = []
    for i, (name, ax) in enumerate(REST):
        hr = SHARD[name][0] // 2
        for j, (fx, fy) in enumerate(_CHIP_FLIPS):
            copies.append(pltpu.make_async_remote_copy(
                src_ref=shard_refs[i].at[pl.ds(c * hr, hr), :],
                dst_ref=_region(full_refs[i], ax, chip, c, SHARD[name]),
                send_sem=sems[3 * i + j], recv_sem=sems[n + 3 * i + j],
                device_id=(x ^ fx, y ^ fy, c), device_id_type=MESH))
    return copies


def _rest_d2d_copies(full_refs, sems):
    x, y, c = _me()
    n = 3 * len(REST)
    copies = []
    for i, (name, ax) in enumerate(REST):
        for j, (fx, fy) in enumerate(_CHIP_FLIPS):
            reg = _region(full_refs[i], ax, 2 * (x ^ fx) + (y ^ fy), c, SHARD[name])
            copies.append(pltpu.make_async_remote_copy(
                src_ref=reg, dst_ref=reg, send_sem=sems[3 * i + j], recv_sem=sems[n + 3 * i + j],
                device_id=(x, y, 1 - c), device_id_type=MESH))
    return copies


def _gather_rest_start(shards, fulls, after):
    nr, ns, na = len(REST), 6 * len(REST), len(after)

    def body(*refs):
        for cp in _rest_ici_copies(refs[:nr], refs[nr:2 * nr], refs[2 * nr + na:2 * nr + na + ns]):
            cp.start()
        token = refs[-1]
        token[...] = jnp.zeros_like(token)

    hbm = lambda a: pltpu.HBM(a.shape, a.dtype)
    res = pl.pallas_call(
        body, name="gather_rest_start",
        out_shape=(pltpu.SemaphoreType.DMA(()),) * ns + tuple(hbm(a) for a in shards + fulls)
        + (jax.ShapeDtypeStruct((8, 128), F32),),
        in_specs=(_HBM_SPEC,) * (2 * nr) + (_ANY_SPEC,) * na,
        out_specs=(_SEM_SPEC,) * ns + (_HBM_SPEC,) * (2 * nr) + (pl.BlockSpec(memory_space=pltpu.VMEM),),
        input_output_aliases={k: ns + k for k in range(2 * nr)}, compiler_params=_SIDE_EFFECTS,
    )(*[pltpu.with_memory_space_constraint(a, pltpu.HBM) for a in shards + fulls], *after)
    return res[:ns], res[ns:ns + nr], res[ns + nr:ns + 2 * nr], res[-1]


def _gather_rest_forward(sems, shards, fulls, after):
    nr, ns = len(REST), 6 * len(REST)

    def body(*refs):
        shard_refs, full_refs, old = refs[:nr], refs[nr:2 * nr], refs[2 * nr:2 * nr + ns]
        new = refs[2 * nr + ns + len(after):2 * nr + 2 * ns + len(after)]
        for cp in _rest_ici_copies(shard_refs, full_refs, old):
            cp.wait_send()
            cp.wait_recv()
        for cp in _rest_d2d_copies(full_refs, new):
            cp.start()
        token = refs[-1]
        token[...] = jnp.zeros_like(token)

    res = pl.pallas_call(
        body, name="gather_rest_forward",
        out_shape=(pltpu.SemaphoreType.DMA(()),) * ns + tuple(pltpu.HBM(a.shape, a.dtype) for a in fulls)
        + (jax.ShapeDtypeStruct((8, 128), F32),),
        in_specs=(_HBM_SPEC,) * (2 * nr) + (_SEM_SPEC,) * ns + (_ANY_SPEC,) * len(after),
        out_specs=(_SEM_SPEC,) * ns + (_HBM_SPEC,) * nr + (pl.BlockSpec(memory_space=pltpu.VMEM),),
        input_output_aliases={nr + k: ns + k for k in range(nr)}, compiler_params=_SIDE_EFFECTS,
    )(*shards, *fulls, *sems, *after)
    return res[:ns], res[ns:ns + nr], res[-1]


def _gather_rest_end(sems, fulls, after):
    nr, ns = len(REST), 6 * len(REST)

    def body(*refs):
        for cp in _rest_d2d_copies(refs[:nr], refs[nr:nr + ns]):
            cp.wait_send()
            cp.wait_recv()

    return pl.pallas_call(
        body, name="gather_rest_end",
        out_shape=tuple(pltpu.HBM(a.shape, a.dtype) for a in fulls),
        in_specs=(_HBM_SPEC,) * nr + (_SEM_SPEC,) * ns + (_ANY_SPEC,) * len(after),
        out_specs=(_HBM_SPEC,) * nr,
        input_output_aliases={k: k for k in range(nr)}, compiler_params=_SIDE_EFFECTS,
    )(*fulls, *sems, *after)


def _adam_update(w, g, m, v):
    mn = ADAM_B1 * m + (1.0 - ADAM_B1) * g
    vn = ADAM_B2 * v + (1.0 - ADAM_B2) * (g * g)
    m_hat = mn / (1.0 - ADAM_B1 ** ADAM_STEP)
    v_hat = vn / (1.0 - ADAM_B2 ** ADAM_STEP)
    return -ADAM_LR * (m_hat / (jnp.sqrt(v_hat) + ADAM_EPS) + ADAM_WD * w), mn, vn


def _final_sum(name, pos, axis, psum, recv, shard_shape, after=(), tr=128):
    r, cw = shard_shape
    hr = r // 2
    tr = min(tr, hr)
    nt = hr // tr
    n_after = len(after)

    def kern(pos_ref, p_ref, r_ref, *rest):
        g_ref, send_buf, land_buf, s_sem, r_sem = rest[n_after:]
        p, t = pl.program_id(0), pl.program_id(1)
        sib = _sibling()

        def copy(i):
            return pltpu.make_async_remote_copy(
                src_ref=send_buf.at[i], dst_ref=land_buf.at[i], send_sem=s_sem.at[i],
                recv_sem=r_sem.at[i], device_id=sib, device_id_type=MESH)

        @pl.when(p == 0)
        def _():
            tot = p_ref[...].astype(F32)
            for j in range(3):
                tot = tot + r_ref[j].astype(F32)
            send_buf[t] = tot
            copy(t).start()
            g_ref[...] = tot

        @pl.when(p == 1)
        def _():
            copy(t).wait_recv()
            g_ref[...] = land_buf[t]

        @pl.when(jnp.logical_and(p == 1, t == nt - 1))
        def _():
            for i in range(nt):
                copy(i).wait_send()

    def shard_rows(p, t, pos_ref):
        return (jnp.where(p == 0, pos_ref[0], 1 - pos_ref[0]) * nt + t, 0)

    def own_part(p, t, pos_ref):
        tt = jnp.where(p == 0, t, nt - 1)
        return (tt, pos_ref[1]) if axis == 1 else (pos_ref[1] * nt + tt, 0)

    grid_spec = pltpu.PrefetchScalarGridSpec(
        num_scalar_prefetch=1, grid=(2, nt),
        in_specs=[pl.BlockSpec((tr, cw), own_part),
                  pl.BlockSpec((3, tr, cw), lambda p, t, pos_ref: (0, jnp.where(p == 0, t, nt - 1), 0))]
        + [pl.BlockSpec(memory_space=pl.ANY)] * n_after,
        out_specs=pl.BlockSpec((tr, cw), shard_rows),
        scratch_shapes=[pltpu.VMEM((nt, tr, cw), F32), pltpu.VMEM((nt, tr, cw), F32),
                        pltpu.SemaphoreType.DMA((nt,)), pltpu.SemaphoreType.DMA((nt,))])
    return pl.pallas_call(
        kern, name=name, grid_spec=grid_spec, out_shape=jax.ShapeDtypeStruct((r, cw), F32),
        compiler_params=_cparams(("arbitrary", "arbitrary")),
    )(pos, psum, recv, *after)


def _adamw(name, w, g, m, v):
    r, cw = w.shape
    tr = min(r, 128)

    def kern(w_ref, g_ref, m_ref, v_ref, go_ref, d_ref, nm_ref, nv_ref):
        gv = g_ref[...]
        go_ref[...] = gv
        d_ref[...], nm_ref[...], nv_ref[...] = _adam_update(w_ref[...], gv, m_ref[...], v_ref[...])

    spec = pl.BlockSpec((tr, cw), lambda i: (i, 0))
    return pl.pallas_call(
        kern, name=name, grid=(r // tr,), in_specs=[spec] * 4, out_specs=[spec] * 4,
        out_shape=[jax.ShapeDtypeStruct((r, cw), F32)] * 4, compiler_params=_cparams(("parallel",)),
    )(w, g, m, v)


_PACK_W = ADA_COLS
_NB = REL_BUCKETS * N_ATT_HEADS
_SMALL_SLOTS = {
    "b_ada": (0, 0, ADA_COLS),
    "norm1_g": (1, 0, D_MODEL), "norm2_g": (1, D_MODEL, D_MODEL), "norm_f_g": (1, 2 * D_MODEL, D_MODEL),
    "ret_gn_g": (1, 3 * D_MODEL, RET_V_W),
    "ret_gn_b": (2, 0, RET_V_W), "rel_bias": (2, RET_V_W, _NB), "loss": (2, RET_V_W + 512, 128),
}


def _pack_small(vals):
    rows = []
    for r in range(8):
        items = sorted([(off, n) for n, (rr, off, _) in _SMALL_SLOTS.items() if rr == r and n in vals])
        parts, pos = [], 0
        for off, n in items:
            if off > pos:
                parts.append(jnp.zeros((1, off - pos), F32))
            parts.append(vals[n].reshape(1, -1).astype(F32))
            pos = off + _SMALL_SLOTS[n][2]
        if pos < _PACK_W:
            parts.append(jnp.zeros((1, _PACK_W - pos), F32))
        rows.append(jnp.concatenate(parts, axis=-1))
    return jnp.concatenate(rows, axis=0)


def _unpack_small(pack, name):
    r, off, wd = _SMALL_SLOTS[name]
    return pack[r:r + 1, off:off + wd]


def kernel(x, c, w_ada, b_ada, norm1_g, w_in, rel_bias, ret_gn_g, ret_gn_b, w_ret_out, w_att_out, w_o, norm2_g, w_ff1, w_ff2, norm_f_g, loss_target, m_w_ada, m_b_ada, m_norm1_g, m_w_in, m_rel_bias, m_ret_gn_g, m_ret_gn_b, m_w_ret_out, m_w_att_out, m_w_o, m_norm2_g, m_w_ff1, m_w_ff2, m_norm_f_g, v_w_ada, v_b_ada, v_norm1_g, v_w_in, v_rel_bias, v_ret_gn_g, v_ret_gn_b, v_w_ret_out, v_w_att_out, v_w_o, v_norm2_g, v_w_ff1, v_w_ff2, v_norm_f_g):
    given = dict(locals())
    big_names = [n for n, _ in BIG]
    shard_w = {n: given[n][0] for n in big_names}
    assert all(shard_w[n].shape == SHARD[n] for n in big_names)

    shards_bf = [shard_w[n].astype(BF16) for n in big_names]
    full = _gather_weights(shards_bf, 1)
    mod, sc_all = _ada_fwd(c, w_ada[0], b_ada)
    rest_gather = _gather_rest_start(shards_bf[1:], list(full[1:]), [mod])
    pos = _where_am_i()

    loss, grad_x, d_mod, small, g_big, pending = _local_step(
        pos, x[0], loss_target[0], mod, norm1_g, norm2_g, norm_f_g.reshape(1, -1), rel_bias, ret_gn_g,
        ret_gn_b, full[0], rest_gather)

    pack_g = _pack_small(dict(b_ada=d_mod, norm1_g=small["norm1_g"], norm2_g=small["norm2_g"],
                              norm_f_g=small["norm_f_g"], ret_gn_g=small["gn_g"], ret_gn_b=small["gn_b"],
                              rel_bias=small["rel_bias"], loss=loss))
    tot, g_w_ada = _small_reduce(pack_g, sc_all)

    small_names = ["b_ada", "norm1_g", "rel_bias", "ret_gn_g", "ret_gn_b", "norm2_g", "norm_f_g"]
    pack_w = _pack_small({n: given[n] for n in small_names})
    pack_m = _pack_small({n: given["m_" + n] for n in small_names})
    pack_v = _pack_small({n: given["v_" + n] for n in small_names})
    _, sd, sm, sv = _adamw("adamw_small", pack_w, tot, pack_m, pack_v)

    grads, deltas, new_m, new_v = {}, {}, {}, {}
    for n in small_names:
        shp = given[n].shape
        grads[n] = _unpack_small(tot, n).reshape(shp)
        deltas[n] = _unpack_small(sd, n).reshape(shp)
        new_m[n] = _unpack_small(sm, n).reshape(shp)
        new_v[n] = _unpack_small(sv, n).reshape(shp)
    g_big["w_ada"] = g_w_ada
    for n in ["w_ada"] + big_names[1:] + big_names[:1]:
        if n == "w_in":
            gw_in, sems, land = pending
            done = [tot, sd] + [deltas[k] for k in ["w_ada"] + big_names[1:]]
            (gw_in,), (got,) = _ici_wait("ici_wait_w_in", [n], sems, [gw_in], [land], done)
            g_big[n] = _final_sum("final_w_in", pos, 1, gw_in, got, SHARD[n])
        g, d, nm, nv = _adamw("adamw_" + n, given[n][0], g_big[n], given["m_" + n][0], given["v_" + n][0])
        grads[n], deltas[n], new_m[n], new_v[n] = g[None], d[None], nm[None], nv[None]

    order = ["w_ada", "b_ada", "norm1_g", "w_in", "rel_bias", "ret_gn_g", "ret_gn_b", "w_ret_out",
             "w_att_out", "w_o", "norm2_g", "w_ff1", "w_ff2", "norm_f_g"]
    loss_out = _unpack_small(tot, "loss")[0, 0]
    return (loss_out, grad_x[None], *[grads[n] for n in order], *[deltas[n] for n in order],
            *[new_m[n] for n in order], *[new_v[n] for n in order])
```

```python
import functools
import math

import jax
import jax.numpy as jnp
import numpy as np
from jax import lax
from jax.experimental import pallas as pl
from jax.experimental.pallas import tpu as pltpu

F32 = jnp.float32
BF16 = jnp.bfloat16
I32 = jnp.int32

SEQ = 2048
D_MODEL = 1024
RET_HEADS = 4
RET_DK = 256
RET_DV = 512
RET_CHUNK = 128
RET_SUB = 2
RET_QK_W = RET_HEADS * RET_DK
RET_V_W = RET_HEADS * RET_DV
ATT_GROUPS = ((128, 1), (512, 4), (2048, 16))
ATT_HPG = 4
ATT_DH = 128
ATT_W = ATT_HPG * ATT_DH
ATT_BLK = 128
N_BLK = SEQ // ATT_BLK
REL_BUCKETS = 32
REL_MAX_DIST = 2048
N_ATT_HEADS = 12
D_FF = 4 * D_MODEL
RMS_EPS = 1e-6
GN_EPS = 1e-5
ROPE_BASE = 10000.0
IN_COLS = 2 * RET_QK_W + 2 * RET_V_W + 9 * ATT_W + 2 * D_MODEL
OFF_Q, OFF_K, OFF_V, OFF_G = 0, RET_QK_W, 2 * RET_QK_W, 2 * RET_QK_W + RET_V_W
OFF_ATT = 2 * RET_QK_W + 2 * RET_V_W
OFF_GATE = OFF_ATT + 9 * ATT_W
N_CHIPS = 4
N_DEV = 8
ADA_COLS = 6 * D_MODEL

ADAM_LR = 0.001
ADAM_B1 = 0.9
ADAM_B2 = 0.999
ADAM_EPS = 1e-08
ADAM_WD = 0.01
ADAM_STEP = 10

VMEM_LIMIT_V7X = 56 * 1024 * 1024
MESH = pl.DeviceIdType.MESH


def _cparams(sem):
    return pltpu.CompilerParams(dimension_semantics=sem, vmem_limit_bytes=VMEM_LIMIT_V7X)


def _sigmoid(v):
    return 1.0 / (1.0 + jnp.exp(-v))


def _rowmap(name, body, row_ins, bcast_ins, row_outs, sum_outs=(), tm=256, after=()):
    m = row_ins[0].shape[0]
    n_in = len(row_ins) + len(bcast_ins)
    n_ro = len(row_outs)

    def kern(*refs):
        vals = [r[...] for r in refs[:n_in]]
        res = body(*vals)
        if not isinstance(res, (tuple, list)):
            res = (res,)
        outs = refs[n_in + len(after):]
        for r, v in zip(outs[:n_ro], res[:n_ro]):
            r[...] = v.astype(r.dtype)
        if sum_outs:
            @pl.when(pl.program_id(0) == 0)
            def _():
                for r in outs[n_ro:]:
                    r[...] = jnp.zeros_like(r)
            for r, v in zip(outs[n_ro:], res[n_ro:]):
                r[...] += v

    in_specs = [pl.BlockSpec((tm, a.shape[1]), lambda i: (i, 0)) for a in row_ins]
    in_specs += [pl.BlockSpec(a.shape, lambda i: (0, 0)) for a in bcast_ins]
    in_specs += [pl.BlockSpec(memory_space=pl.ANY)] * len(after)
    out_specs = [pl.BlockSpec((tm, n), lambda i: (i, 0)) for n, _ in row_outs]
    out_specs += [pl.BlockSpec((1, n), lambda i: (0, 0)) for n in sum_outs]
    out_shape = [jax.ShapeDtypeStruct((m, n), dt) for n, dt in row_outs]
    out_shape += [jax.ShapeDtypeStruct((1, n), F32) for n in sum_outs]
    return pl.pallas_call(
        kern, name=name, grid=(m // tm,), in_specs=in_specs, out_specs=out_specs,
        out_shape=out_shape, compiler_params=_cparams(("arbitrary",)),
    )(*row_ins, *bcast_ins, *after)


TM, TN = 1024, 1024


def _piece_chunks(piece, width):
    arr, stacked = piece
    return arr.shape[0] if stacked else arr.shape[1] // width


def _piece_spec(piece, rows, width, start, row_of, chunk_of):
    arr, stacked = piece
    last = _piece_chunks(piece, width) - 1

    def local(*ids):
        return jnp.clip(chunk_of(*ids) - start, 0, last)

    def row(*ids):
        rel = chunk_of(*ids) - start
        return jnp.where(jnp.logical_and(rel >= 0, rel <= last), row_of(*ids), 0)

    if stacked:
        return pl.BlockSpec((None, rows, width), lambda *ids: (local(*ids), row(*ids), 0))
    return pl.BlockSpec((rows, width), lambda *ids: (row(*ids), local(*ids)))


def _piece_starts(pieces, width):
    return [sum(_piece_chunks(p, width) for p in pieces[:q]) for q in range(len(pieces))]


def _matmul(name, a, b, kind, m, n, k, outs, *, b_off=0, tm=TM, tn=TN, tk=1024,
            epilogue=None, extras=(), after=()):
    tm, tn, tk = min(tm, m), min(tn, n), min(tk, k)
    nk = k // tk
    pieces = a if isinstance(a, list) else [(a, False)]
    starts = _piece_starts(pieces, tk)
    if kind == "nn":
        a_specs = [pl.BlockSpec((tm, tk), lambda i, j, kk: (i, kk))]
        b_spec = pl.BlockSpec((tk, tn), lambda i, j, kk: (kk, b_off // tn + j))
        dn = (((1,), (0,)), ((), ()))
    elif kind == "nt":
        a_specs = [_piece_spec(p, tm, tk, st, lambda i, j, kk: i, lambda i, j, kk: kk)
                   for p, st in zip(pieces, starts)]
        b_spec = pl.BlockSpec((tn, tk), lambda i, j, kk: (j, b_off // tk + kk))
        dn = (((1,), (1,)), ((), ()))
    else:
        a_specs = [pl.BlockSpec((tk, tm), lambda i, j, kk: (kk, i))]
        b_spec = pl.BlockSpec((tk, tn), lambda i, j, kk: (kk, j))
        dn = (((0,), (0,)), ((), ()))
    n_a, n_ex, n_out = len(pieces), len(extras), len(outs)
    if epilogue is None:
        epilogue = lambda acc: (acc,)

    def finish(acc, ex_refs, out_refs):
        res = epilogue(acc, *[r[...] for r in ex_refs])
        for r, v in zip(out_refs, res):
            r[...] = v.astype(r.dtype)

    n_in = n_a + 1 + n_ex + len(after)

    def kern(*refs):
        a_refs, b_ref = refs[:n_a], refs[n_a]
        ex_refs = refs[n_a + 1:n_a + 1 + n_ex]
        out_refs = refs[n_in:n_in + n_out]
        kk = pl.program_id(2)
        dot = lambda a_ref: lax.dot_general(a_ref[...], b_ref[...], dn, preferred_element_type=F32)
        if nk == 1:
            finish(dot(a_refs[0]), ex_refs, out_refs)
            return
        acc_ref = refs[n_in + n_out]
        if n_a == 1:
            part = dot(a_refs[0])

            @pl.when(kk == 0)
            def _():
                acc_ref[...] = part

            @pl.when(kk > 0)
            def _():
                acc_ref[...] += part
        else:
            @pl.when(kk == 0)
            def _():
                acc_ref[...] = jnp.zeros_like(acc_ref)

            for q in range(n_a):
                @pl.when(jnp.logical_and(kk >= starts[q], kk < starts[q] + _piece_chunks(pieces[q], tk)))
                def _(q=q):
                    acc_ref[...] += dot(a_refs[q])

        @pl.when(kk == nk - 1)
        def _():
            finish(acc_ref[...], ex_refs, out_refs)

    in_specs = a_specs + [b_spec] + [pl.BlockSpec(bs, im) for _, bs, im in extras]
    in_specs += [pl.BlockSpec(memory_space=pl.ANY)] * len(after)
    return pl.pallas_call(
        kern, name=name, grid=(m // tm, n // tn, nk), in_specs=in_specs,
        out_specs=[pl.BlockSpec((tm, tn), lambda i, j, kk: (i, j)) for _ in outs],
        out_shape=[jax.ShapeDtypeStruct((m, n), dt) for dt in outs],
        scratch_shapes=[] if nk == 1 else [pltpu.VMEM((tm, tn), F32)],
        compiler_params=_cparams(("parallel", "parallel", "arbitrary")),
    )(*[p[0] for p in pieces], b, *[e[0] for e in extras], *after)


def _ici_copies(psum_ref, recv_ref, s_sem, r_sem, axis, shard_shape):
    x, y, c = _me()
    hr, cw = shard_shape[0] // 2, shard_shape[1]
    pick = lambda sems, j: sems[j] if isinstance(sems, (list, tuple)) else sems.at[j]
    copies = []
    for j, (fx, fy) in enumerate(_CHIP_FLIPS):
        chip = 2 * (x ^ fx) + (y ^ fy)
        src = psum_ref.at[:, pl.ds(chip * cw, cw)] if axis == 1 else psum_ref.at[pl.ds(chip * hr, hr), :]
        copies.append(pltpu.make_async_remote_copy(
            src_ref=src, dst_ref=recv_ref.at[j], send_sem=pick(s_sem, j), recv_sem=pick(r_sem, j),
            device_id=(x ^ fx, y ^ fy, c), device_id_type=MESH))
    return copies


_HBM_SPEC = pl.BlockSpec(memory_space=pltpu.HBM)
_SEM_SPEC = pl.BlockSpec(memory_space=pltpu.SEMAPHORE)


def _split_ici_copies(names, p_refs, land_refs, sems):
    copies = []
    for i, n in enumerate(names):
        copies += _ici_copies(p_refs[i], land_refs[i], list(sems[6 * i:6 * i + 3]),
                              list(sems[6 * i + 3:6 * i + 6]), dict(BIG)[n], SHARD[n])
    return copies


def _ici_start(name, names, psums):
    nw, ns = len(names), 6 * len(names)
    lands = [lax.empty((3, SHARD[n][0] // 2, SHARD[n][1]), BF16) for n in names]

    def body(*refs):
        for cp in _split_ici_copies(names, refs[:nw], refs[nw:2 * nw], refs[2 * nw:2 * nw + ns]):
            cp.start()
        token = refs[-1]
        token[...] = jnp.zeros_like(token)

    res = pl.pallas_call(
        body, name=name,
        out_shape=(pltpu.SemaphoreType.DMA(()),) * ns
        + tuple(pltpu.HBM(a.shape, BF16) for a in list(psums) + lands)
        + (jax.ShapeDtypeStruct((8, 128), F32),),
        in_specs=(_HBM_SPEC,) * (2 * nw),
        out_specs=(_SEM_SPEC,) * ns + (_HBM_SPEC,) * (2 * nw) + (pl.BlockSpec(memory_space=pltpu.VMEM),),
        input_output_aliases={k: ns + k for k in range(2 * nw)},
        compiler_params=pltpu.CompilerParams(has_side_effects=pltpu.SideEffectType.DATAFLOW_SIDE_EFFECTING),
    )(*[pltpu.with_memory_space_constraint(a, pltpu.HBM) for a in list(psums) + lands])
    return res[:ns], res[ns:ns + nw], res[ns + nw:ns + 2 * nw], res[-1]


def _ici_wait(name, names, sems, p_thru, land_thru, after):
    nw, ns = len(names), 6 * len(names)

    def body(*refs):
        for cp in _split_ici_copies(names, refs[:nw], refs[nw:2 * nw], refs[2 * nw:2 * nw + ns]):
            cp.wait_send()
            cp.wait_recv()

    res = pl.pallas_call(
        body, name=name,
        out_shape=tuple(pltpu.HBM(a.shape, BF16) for a in list(p_thru) + list(land_thru)),
        in_specs=(_HBM_SPEC,) * (2 * nw) + (_SEM_SPEC,) * ns + (pl.BlockSpec(memory_space=pl.ANY),) * len(after),
        out_specs=(_HBM_SPEC,) * (2 * nw), input_output_aliases={k: k for k in range(2 * nw)},
        compiler_params=pltpu.CompilerParams(has_side_effects=pltpu.SideEffectType.DATAFLOW_SIDE_EFFECTING),
    )(*p_thru, *land_thru, *sems, *after)
    return res[:nw], res[nw:]


def _where_am_i():
    x, y, c = _me()
    return jnp.stack([c, 2 * x + y]).astype(I32)


def _sibling():
    x, y, c = _me()
    return (x, y, 1 - c)


N_SEND_SLOTS = 2


def _matmul_tn_pair(name, pos, a, b, m, n, k, shard_rows, *, tm, tn, tk):
    hr = shard_rows // 2
    tm, tn, tk = min(tm, hr), min(tn, n), min(tk, k)
    tph = hr // tm
    nt, nj, nk = (m // 2) // tm, n // tn, k // tk
    n_tiles = nt * nj

    def row_block(p, t, pos_ref):
        half = jnp.where(p == 0, 1 - pos_ref[0], pos_ref[0])
        return (t // tph) * (2 * tph) + half * tph + t % tph

    pieces = b if isinstance(b, list) else [(b, False)]
    starts = _piece_starts(pieces, tn)
    n_b = len(pieces)

    def kern(pos_ref, a_ref, *rest):
        b_refs = rest[:n_b]
        o_ref, acc_ref, send_buf, land_buf, s_sem, r_sem = rest[n_b:]
        p, t, j, kk = pl.program_id(0), pl.program_id(1), pl.program_id(2), pl.program_id(3)
        idx = t * nj + j
        sib = _sibling()

        def copy(i):
            return pltpu.make_async_remote_copy(
                src_ref=send_buf.at[i % N_SEND_SLOTS], dst_ref=land_buf.at[i], send_sem=s_sem.at[i],
                recv_sem=r_sem.at[i], device_id=sib, device_id_type=MESH)

        @pl.when(kk == 0)
        def _():
            acc_ref[...] = jnp.zeros_like(acc_ref)

        for q in range(n_b):
            @pl.when(jnp.logical_and(j >= starts[q], j < starts[q] + _piece_chunks(pieces[q], tn)))
            def _(q=q):
                acc_ref[...] += lax.dot_general(a_ref[...], b_refs[q][...], _TN, preferred_element_type=F32)

        @pl.when(jnp.logical_and(kk == nk - 1, p == 0))
        def _():
            @pl.when(idx >= N_SEND_SLOTS)
            def _():
                copy(idx - N_SEND_SLOTS).wait_send()

            send_buf[idx % N_SEND_SLOTS] = acc_ref[...].astype(BF16)
            copy(idx).start()

        @pl.when(jnp.logical_and(kk == nk - 1, p == 1))
        def _():
            copy(idx).wait_recv()
            o_ref[...] = (acc_ref[...] + land_buf[idx].astype(F32)).astype(BF16)

        @pl.when(jnp.logical_and(jnp.logical_and(p == 1, idx == n_tiles - 1), kk == nk - 1))
        def _():
            for i in range(max(n_tiles - N_SEND_SLOTS, 0), n_tiles):
                copy(i).wait_send()

    grid_spec = pltpu.PrefetchScalarGridSpec(
        num_scalar_prefetch=1, grid=(2, nt, nj, nk),
        in_specs=[pl.BlockSpec((tk, tm), lambda p, t, j, kk, pos_ref: (kk, row_block(p, t, pos_ref)))]
        + [_piece_spec(pc, tk, tn, st, lambda p, t, j, kk, pos_ref: kk, lambda p, t, j, kk, pos_ref: j)
           for pc, st in zip(pieces, starts)],
        out_specs=pl.BlockSpec((tm, tn), lambda p, t, j, kk, pos_ref: (p * t, p * j)),
        scratch_shapes=[pltpu.VMEM((tm, tn), F32), pltpu.VMEM((N_SEND_SLOTS, tm, tn), BF16),
                        pltpu.VMEM((n_tiles, tm, tn), BF16),
                        pltpu.SemaphoreType.DMA((n_tiles,)), pltpu.SemaphoreType.DMA((n_tiles,))])
    return pl.pallas_call(
        kern, name=name, grid_spec=grid_spec, out_shape=jax.ShapeDtypeStruct((m // 2, n), BF16),
        compiler_params=_cparams(("arbitrary",) * 4),
    )(pos, a, *[pc[0] for pc in pieces])


def _rope_tables():
    half = RET_DK // 2
    f32 = np.float32
    inv = np.power(f32(ROPE_BASE), -np.arange(half, dtype=f32) / f32(half)).astype(f32)
    ang = (np.arange(SEQ, dtype=f32)[:, None] * inv[None, :]).astype(f32)
    return jnp.asarray(np.cos(ang).astype(f32)), jnp.asarray(np.sin(ang).astype(f32))


def _decay_tables():
    c = RET_CHUNK
    f32 = np.float32
    log_g = np.log1p(-np.power(f32(2.0), f32(-5.0) - np.arange(RET_HEADS, dtype=f32))).astype(f32)
    idx = np.arange(c, dtype=f32)
    rel = idx[:, None] - idx[None, :]
    din = np.where(rel >= 0, np.exp(log_g[:, None, None] * np.maximum(rel, f32(0.0))), f32(0.0)).astype(f32)
    qd = np.exp(log_g[:, None] * (idx + f32(1.0))).astype(f32)[:, :, None]
    kd = np.exp(log_g[:, None] * (f32(c) - f32(1.0) - idx)).astype(f32)[:, :, None]
    cd = np.exp(log_g * f32(c)).astype(f32)
    return jnp.asarray(din), jnp.asarray(qd), jnp.asarray(kd), jnp.asarray(cd)


def _t5_bucket(dist):
    max_exact = REL_BUCKETS // 2
    d_f = jnp.maximum(dist, 1).astype(F32)
    large = max_exact + (jnp.log(d_f / max_exact) / math.log(REL_MAX_DIST / max_exact)
                         * (REL_BUCKETS - max_exact)).astype(I32)
    large = jnp.minimum(large, REL_BUCKETS - 1)
    return jnp.where(dist < max_exact, dist, large)


def _bucket_tables():
    qi = jnp.arange(ATT_BLK)[:, None]
    kj = jnp.arange(2 * ATT_BLK)[None, :]
    dist = jnp.clip(ATT_BLK + qi - kj, 0, ATT_BLK)
    return jnp.stack([_t5_bucket(dist * dil) for _, dil in ATT_GROUPS]).astype(I32)


def _retention_fwd(rqk, rv, rg, gn_g, gn_b, din, qd, kd, cd):
    nc = SEQ // RET_CHUNK
    c, dk, dv = RET_CHUNK, RET_DK, RET_DV

    def kern(q_ref, k_ref, v_ref, rg_ref, g_ref, b_ref, din_ref, qd_ref, kd_ref, cd_ref,
             o_ref, st_ref, gated_ref, state):
        n = pl.program_id(0)

        @pl.when(n == 0)
        def _():
            state[...] = jnp.zeros_like(state)

        for sub in range(RET_SUB):
            rows = slice(sub * c, (sub + 1) * c)
            for h in range(RET_HEADS):
                q, k = q_ref[rows, h * dk:(h + 1) * dk], k_ref[rows, h * dk:(h + 1) * dk]
                v = v_ref[rows, h * dv:(h + 1) * dv]
                s_b = state[h].astype(BF16)
                st_ref[h, sub] = s_b
                a = lax.dot_general(q, k, _NT, preferred_element_type=F32) * din_ref[h]
                o = jnp.dot(a.astype(BF16), v, preferred_element_type=F32)
                o += jnp.dot(q, s_b, preferred_element_type=F32) * qd_ref[h]
                v_cols = slice(h * dv, (h + 1) * dv)
                o_ref[rows, v_cols] = o
                nrm, _ = _gn_parts(o)
                gate = rg_ref[rows, v_cols].astype(F32)
                gated_ref[rows, v_cols] = ((gate * _sigmoid(gate))
                                           * (nrm * g_ref[:, v_cols] + b_ref[:, v_cols])).astype(BF16)
                kk = (k.astype(F32) * kd_ref[h]).astype(BF16)
                state[h] = state[h] * cd_ref[h] + lax.dot_general(kk, v, _TN, preferred_element_type=F32)

    whole = lambda a: pl.BlockSpec(a.shape, lambda n: (0,) * a.ndim)
    cs = RET_SUB * c
    rows_v = pl.BlockSpec((cs, RET_V_W), lambda n: (n, 0))
    return pl.pallas_call(
        kern, name="retention_fwd", grid=(nc // RET_SUB,),
        in_specs=[
            pl.BlockSpec((cs, RET_QK_W), lambda n: (n, 0)),
            pl.BlockSpec((cs, RET_QK_W), lambda n: (n, 1)),
            rows_v, rows_v, whole(gn_g), whole(gn_b),
            whole(din), whole(qd), whole(kd),
            pl.BlockSpec(memory_space=pltpu.SMEM),
        ],
        out_specs=[
            rows_v,
            pl.BlockSpec((RET_HEADS, RET_SUB, dk, dv), lambda n: (0, n, 0, 0)),
            rows_v,
        ],
        out_shape=[
            jax.ShapeDtypeStruct((SEQ, RET_V_W), F32),
            jax.ShapeDtypeStruct((RET_HEADS, nc, dk, dv), BF16),
            jax.ShapeDtypeStruct((SEQ, RET_V_W), BF16),
        ],
        scratch_shapes=[pltpu.VMEM((RET_HEADS, dk, dv), F32)],
        compiler_params=_cparams(("arbitrary",)),
    )(rqk, rqk, rv, rg, gn_g, gn_b, din, qd, kd, cd)


def _retention_bwd(rqk, rv, states, d_gated, ro, rg, gn_g, gn_b, din, qd, kd, cd, cos, sin):
    nc = SEQ // RET_CHUNK
    c, dk, dv = RET_CHUNK, RET_DK, RET_DV
    half = dk // 2
    last = nc // RET_SUB - 1

    def unrot(g, cs, sn):
        g1, g2 = g[:, :half], g[:, half:]
        return jnp.concatenate([g1 * cs + g2 * sn, g2 * cs - g1 * sn], axis=-1)

    def kern(q_ref, k_ref, v_ref, st_ref, dg_ref, ro_ref, rg_ref, g_ref, b_ref, din_ref, qd_ref, kd_ref,
             cd_ref, cos_ref, sin_ref, out_ref, drg_ref, dgn_g_ref, dgn_b_ref, dstate):
        step = pl.program_id(0)

        @pl.when(step == 0)
        def _():
            dstate[...] = jnp.zeros_like(dstate)
            dgn_g_ref[...] = jnp.zeros_like(dgn_g_ref)
            dgn_b_ref[...] = jnp.zeros_like(dgn_b_ref)

        for sub in reversed(range(RET_SUB)):
            rows = slice(sub * c, (sub + 1) * c)
            cs, sn = cos_ref[rows, :], sin_ref[rows, :]
            for h in range(RET_HEADS):
                qk_cols, v_cols = slice(h * dk, (h + 1) * dk), slice(h * dv, (h + 1) * dv)
                q, k, v = q_ref[rows, qk_cols], k_ref[rows, qk_cols], v_ref[rows, v_cols]
                s_b = st_ref[h, sub]
                nrm, rstd = _gn_parts(ro_ref[rows, v_cols])
                gate, dg = rg_ref[rows, v_cols].astype(F32), dg_ref[rows, v_cols].astype(F32)
                sg = _sigmoid(gate)
                gn_gain = g_ref[:, v_cols]
                drg_ref[rows, v_cols] = (dg * (nrm * gn_gain + b_ref[:, v_cols])
                                         * (sg * (1.0 + gate * (1.0 - sg)))).astype(BF16)
                d_ron = dg * (gate * sg)
                dgn_g_ref[:, v_cols] += jnp.sum(d_ron * nrm, axis=0, keepdims=True)
                dgn_b_ref[:, v_cols] += jnp.sum(d_ron, axis=0, keepdims=True)
                d_n = d_ron * gn_gain
                d_o = rstd * (d_n - jnp.mean(d_n, axis=-1, keepdims=True)
                              - nrm * jnp.mean(d_n * nrm, axis=-1, keepdims=True))
                d_ob = d_o.astype(BF16)
                d_oq = (d_o * qd_ref[h]).astype(BF16)
                ds_b = dstate[h].astype(BF16)
                din_m = din_ref[h]
                a_b = (lax.dot_general(q, k, _NT, preferred_element_type=F32) * din_m).astype(BF16)
                kk = (k.astype(F32) * kd_ref[h]).astype(BF16)
                d_v = lax.dot_general(a_b, d_ob, _TN, preferred_element_type=F32)
                d_v += jnp.dot(kk, ds_b, preferred_element_type=F32)
                d_a = (lax.dot_general(d_ob, v, _NT, preferred_element_type=F32) * din_m).astype(BF16)
                d_q = jnp.dot(d_a, k, preferred_element_type=F32)
                d_q += lax.dot_general(d_oq, s_b, _NT, preferred_element_type=F32)
                d_k = lax.dot_general(d_a, q, _TN, preferred_element_type=F32)
                d_k += lax.dot_general(v, ds_b, _NT, preferred_element_type=F32) * kd_ref[h]
                dstate[h] = dstate[h] * cd_ref[h] + lax.dot_general(q, d_oq, _TN,
                                                                    preferred_element_type=F32)
                out_ref[rows, h * dk:(h + 1) * dk] = unrot(d_q, cs, sn).astype(BF16)
                out_ref[rows, RET_QK_W + h * dk:RET_QK_W + (h + 1) * dk] = (
                    unrot(d_k, cs, sn) * (RET_DK ** -0.5)).astype(BF16)
                out_ref[rows, 2 * RET_QK_W + h * dv:2 * RET_QK_W + (h + 1) * dv] = d_v.astype(BF16)

    whole = lambda a: pl.BlockSpec(a.shape, lambda n: (0,) * a.ndim)
    rs = RET_SUB * c
    rows_v = pl.BlockSpec((rs, RET_V_W), lambda n: (last - n, 0))
    return pl.pallas_call(
        kern, name="retention_bwd", grid=(nc // RET_SUB,),
        in_specs=[
            pl.BlockSpec((rs, RET_QK_W), lambda n: (last - n, 0)),
            pl.BlockSpec((rs, RET_QK_W), lambda n: (last - n, 1)),
            rows_v,
            pl.BlockSpec((RET_HEADS, RET_SUB, dk, dv), lambda n: (0, last - n, 0, 0)),
            rows_v, rows_v, rows_v, whole(gn_g), whole(gn_b),
            whole(din), whole(qd), whole(kd),
            pl.BlockSpec(memory_space=pltpu.SMEM),
            pl.BlockSpec((rs, half), lambda n: (last - n, 0)),
            pl.BlockSpec((rs, half), lambda n: (last - n, 0)),
        ],
        out_specs=[pl.BlockSpec((rs, 2 * RET_QK_W + RET_V_W), lambda n: (last - n, 0)), rows_v,
                   whole(gn_g), whole(gn_b)],
        out_shape=[jax.ShapeDtypeStruct((SEQ, 2 * RET_QK_W + RET_V_W), BF16),
                   jax.ShapeDtypeStruct((SEQ, RET_V_W), BF16),
                   jax.ShapeDtypeStruct((1, RET_V_W), F32), jax.ShapeDtypeStruct((1, RET_V_W), F32)],
        scratch_shapes=[pltpu.VMEM((RET_HEADS, dk, dv), F32)],
        compiler_params=_cparams(("arbitrary",)),
    )(rqk, rqk, rv, states, d_gated, ro, rg, gn_g, gn_b, din, qd, kd, cd, cos, sin)


def _bias_build(rel_bias, buckets):
    ng = len(ATT_GROUPS)

    def kern(tab_ref, bkt_ref, o_ref):
        g, h = pl.program_id(0), pl.program_id(1)
        bkt = bkt_ref[...]
        acc = jnp.zeros(bkt.shape, F32)
        for b in range(REL_BUCKETS):
            acc = jnp.where(bkt == b, tab_ref[b, g * ATT_HPG + h], acc)
        o_ref[...] = acc

    return pl.pallas_call(
        kern, name="bias_build", grid=(ng, ATT_HPG),
        in_specs=[pl.BlockSpec(memory_space=pltpu.SMEM),
                  pl.BlockSpec((None, ATT_BLK, 2 * ATT_BLK), lambda g, h: (g, 0, 0))],
        out_specs=pl.BlockSpec((None, None, ATT_BLK, 2 * ATT_BLK), lambda g, h: (g, h, 0, 0)),
        out_shape=jax.ShapeDtypeStruct((ng, ATT_HPG, ATT_BLK, 2 * ATT_BLK), F32),
        compiler_params=_cparams(("arbitrary", "arbitrary")),
    )(rel_bias, buckets)


def _bias_grad(dsb, buckets):
    ng = len(ATT_GROUPS)

    def kern(ds_ref, bkt_ref, o_ref):
        g, h = pl.program_id(0), pl.program_id(1)
        bkt, ds = bkt_ref[...], ds_ref[...]
        for b in range(REL_BUCKETS):
            o_ref[b, g * ATT_HPG + h] = jnp.sum(jnp.where(bkt == b, ds, 0.0))

    return pl.pallas_call(
        kern, name="bias_grad", grid=(ng, ATT_HPG),
        in_specs=[pl.BlockSpec((None, None, ATT_BLK, 2 * ATT_BLK), lambda g, h: (g, h, 0, 0)),
                  pl.BlockSpec((None, ATT_BLK, 2 * ATT_BLK), lambda g, h: (g, 0, 0))],
        out_specs=pl.BlockSpec(memory_space=pltpu.SMEM),
        out_shape=jax.ShapeDtypeStruct((REL_BUCKETS, N_ATT_HEADS), F32),
        compiler_params=_cparams(("arbitrary", "arbitrary")),
    )(dsb, buckets)


_NT = (((1,), (1,)), ((), ()))
_TN = (((0,), (0,)), ((), ()))
_ATT_SCALE = ATT_DH ** -0.5


def _window_mask(has_prev):
    qi = lax.broadcasted_iota(I32, (ATT_BLK, 2 * ATT_BLK), 0)
    kj = lax.broadcasted_iota(I32, (ATT_BLK, 2 * ATT_BLK), 1)
    prev_ok = jnp.logical_and(jnp.logical_and(kj < ATT_BLK, kj >= qi), has_prev)
    return jnp.logical_or(prev_ok, jnp.logical_and(kj >= ATT_BLK, qi >= kj - ATT_BLK))


def _head_specs(col0):
    return pl.BlockSpec((SEQ, ATT_DH), lambda h: (0, col0 + h))


def _sub_rows(start, size, dil):
    return pl.ds(start, size) if dil == 1 else pl.ds(start, size, stride=dil)


def _att_blocks(dil):
    nb = SEQ // dil // ATT_BLK
    return [(r + dil * n * ATT_BLK, n > 0, n + 1 < nb) for r in range(dil) for n in range(nb)]


def _att_fwd(gi, dil, qkv, bias):
    blk, dh = ATT_BLK, ATT_DH
    pad = dil * blk
    col0 = 3 * ATT_HPG * gi

    def kern(q_ref, k_ref, v_ref, b_ref, o_ref, l_ref, qf, kpad, vpad):
        zero = jnp.zeros((pad, dh), F32)
        kpad[0:pad, :] = zero
        vpad[0:pad, :] = zero
        kpad[pad:, :] = k_ref[...].astype(F32)
        vpad[pad:, :] = v_ref[...].astype(F32)
        qf[...] = q_ref[...].astype(F32)
        bias_m = b_ref[...]
        for start, has_prev, _ in _att_blocks(dil):
            rows, window = _sub_rows(start, blk, dil), _sub_rows(start, 2 * blk, dil)
            q = qf[rows, :].astype(BF16)
            kw, vw = kpad[window, :].astype(BF16), vpad[window, :].astype(BF16)
            valid = _window_mask(has_prev)
            s = lax.dot_general(q, kw, _NT, preferred_element_type=F32) * _ATT_SCALE + bias_m
            s = jnp.where(valid, s, -1e30)
            mx = jnp.max(s, axis=-1, keepdims=True)
            e = jnp.exp(s - mx)
            den = jnp.sum(e, axis=-1, keepdims=True)
            o_ref[rows, :] = jnp.dot((e / den).astype(BF16), vw, preferred_element_type=F32)
            l_ref[rows, :] = jnp.broadcast_to(mx + jnp.log(den), (blk, dh))

    return pl.pallas_call(
        kern, name=f"att_fwd_g{gi}", grid=(ATT_HPG,),
        in_specs=[_head_specs(col0), _head_specs(col0 + ATT_HPG), _head_specs(col0 + 2 * ATT_HPG),
                  pl.BlockSpec((None, None, blk, 2 * blk), lambda h: (gi, h, 0, 0))],
        out_specs=[_head_specs(0), _head_specs(0)],
        out_shape=[jax.ShapeDtypeStruct((SEQ, ATT_W), F32), jax.ShapeDtypeStruct((SEQ, ATT_W), F32)],
        scratch_shapes=[pltpu.VMEM((SEQ, dh), F32), pltpu.VMEM((SEQ + pad, dh), F32),
                        pltpu.VMEM((SEQ + pad, dh), F32)],
        compiler_params=_cparams(("arbitrary",)),
    )(qkv, qkv, qkv, bias)


def _att_bwd(gi, dil, qkv, d_att, lse, dd, bias):
    blk, dh = ATT_BLK, ATT_DH
    pad = dil * blk
    col0 = 3 * ATT_HPG * gi

    def kern(q_ref, k_ref, v_ref, do_ref, l_ref, d_ref, b_ref, dqkv_ref, dsb_ref,
             qf, kpad, vpad, dq_s, dkpad, dvpad):
        zero = jnp.zeros((pad, dh), F32)
        kpad[0:pad, :] = zero
        vpad[0:pad, :] = zero
        kpad[pad:, :] = k_ref[...].astype(F32)
        vpad[pad:, :] = v_ref[...].astype(F32)
        qf[...] = q_ref[...].astype(F32)
        dkpad[...] = jnp.zeros_like(dkpad)
        dvpad[...] = jnp.zeros_like(dvpad)
        bias_m = b_ref[...]
        ds_sum = jnp.zeros((blk, 2 * blk), F32)

        for start, has_prev, _ in _att_blocks(dil):
            rows, window = _sub_rows(start, blk, dil), _sub_rows(start, 2 * blk, dil)
            q, d_o = qf[rows, :].astype(BF16), do_ref[rows, :].astype(BF16)
            kw, vw = kpad[window, :].astype(BF16), vpad[window, :].astype(BF16)
            lrow, drow = l_ref[rows, :][:, :1], d_ref[rows, :][:, :1]
            valid = _window_mask(has_prev)
            s = lax.dot_general(q, kw, _NT, preferred_element_type=F32) * _ATT_SCALE + bias_m
            p = jnp.where(valid, jnp.exp(jnp.where(valid, s, -1e30) - lrow), 0.0)
            dp = lax.dot_general(d_o, vw, _NT, preferred_element_type=F32)
            ds = p * (dp - drow)
            ds_b = ds.astype(BF16)
            dq_s[rows, :] = jnp.dot(ds_b, kw, preferred_element_type=F32) * _ATT_SCALE
            dkpad[window, :] += lax.dot_general(ds_b, q, _TN, preferred_element_type=F32) * _ATT_SCALE
            dvpad[window, :] += lax.dot_general(p.astype(BF16), d_o, _TN, preferred_element_type=F32)
            ds_sum = ds_sum + ds
        dsb_ref[...] = ds_sum

        dqkv_ref[0] = dq_s[...].astype(BF16)
        dqkv_ref[1] = dkpad[pad:, :].astype(BF16)
        dqkv_ref[2] = dvpad[pad:, :].astype(BF16)

    return pl.pallas_call(
        kern, name=f"att_bwd_g{gi}", grid=(ATT_HPG,),
        in_specs=[_head_specs(col0), _head_specs(col0 + ATT_HPG), _head_specs(col0 + 2 * ATT_HPG),
                  _head_specs(0), _head_specs(0), _head_specs(0),
                  pl.BlockSpec((None, None, blk, 2 * blk), lambda h: (gi, h, 0, 0))],
        out_specs=[pl.BlockSpec((3, SEQ, dh), lambda h: (0, 0, h)),
                   pl.BlockSpec((None, blk, 2 * blk), lambda h: (h, 0, 0))],
        out_shape=[jax.ShapeDtypeStruct((3, SEQ, ATT_W), BF16),
                   jax.ShapeDtypeStruct((ATT_HPG, blk, 2 * blk), F32)],
        scratch_shapes=[pltpu.VMEM((SEQ, dh), F32), pltpu.VMEM((SEQ + pad, dh), F32),
                        pltpu.VMEM((SEQ + pad, dh), F32), pltpu.VMEM((SEQ, dh), F32),
                        pltpu.VMEM((SEQ + pad, dh), F32), pltpu.VMEM((SEQ + pad, dh), F32)],
        compiler_params=_cparams(("arbitrary",)),
    )(qkv, qkv, qkv, d_att, lse, dd, bias)


def _rms_parts(x):
    r = lax.rsqrt(jnp.mean(x * x, axis=-1, keepdims=True) + RMS_EPS)
    return x * r, r


def _rms_bwd(d_xhat, xhat, r):
    return r * (d_xhat - xhat * jnp.mean(d_xhat * xhat, axis=-1, keepdims=True))


def _prenorm_fwd(name, x, gain, shift, scale):
    def body(xt, g, sh, sc):
        xhat, _ = _rms_parts(xt)
        return (xhat * g) * (1.0 + sc) + sh
    return _rowmap(name, body, [x], [gain, shift, scale], [(D_MODEL, BF16)])[0]


def _prenorm_bwd(name, d_h, x, gain, scale, resid, branch=None, gate=None, after=()):
    gated = branch is not None

    def body(d_ht, xt, res, *rest):
        g, sc = rest[-2 - gated], rest[-1 - gated]
        xhat, r = _rms_parts(xt)
        nrm = xhat * g
        d_n = d_ht * (1.0 + sc)
        dx = _rms_bwd(d_n * g, xhat, r) + res
        sums = (jnp.sum(d_ht, axis=0, keepdims=True), jnp.sum(d_ht * nrm, axis=0, keepdims=True),
                jnp.sum(d_n * xhat, axis=0, keepdims=True))
        if not gated:
            return (dx,) + sums
        return (dx, dx * rest[-1]) + sums + (jnp.sum(dx * rest[0], axis=0, keepdims=True),)

    return _rowmap(name, body, [d_h, x, resid] + ([branch] if gated else []),
                   [gain, scale] + ([gate] if gated else []),
                   [(D_MODEL, F32)] + ([(D_MODEL, BF16)] if gated else []),
                   [D_MODEL] * (3 + gated), after=after)


def _gn_parts(ro):
    mu = jnp.mean(ro, axis=-1, keepdims=True)
    cen = ro - mu
    rstd = lax.rsqrt(jnp.mean(cen * cen, axis=-1, keepdims=True) + GN_EPS)
    return cen * rstd, rstd


def _combine(os_, ls_, after=()):
    def body(o0, o1, o2, l0, l1, l2):
        mx = jnp.maximum(jnp.maximum(l0, l1), l2)
        e0, e1, e2 = jnp.exp(l0 - mx), jnp.exp(l1 - mx), jnp.exp(l2 - mx)
        den = e0 + e1 + e2
        att = (e0 / den) * o0 + (e1 / den) * o1 + (e2 / den) * o2
        return att, att, mx + jnp.log(den)
    return _rowmap("att_combine", body, list(os_) + list(ls_), [],
                   [(ATT_W, F32), (ATT_W, BF16), (ATT_W, F32)], after=after)


MERGE_TM = 512


def _merge_operands(gates, ret_out, att_out=None):
    ops = [(gates, (MERGE_TM, D_MODEL), lambda i, j, kk: (i, 0)),
           (gates, (MERGE_TM, D_MODEL), lambda i, j, kk: (i, 1)),
           (ret_out, (MERGE_TM, D_MODEL), lambda i, j, kk: (i, 0))]
    if att_out is not None:
        ops.append((att_out, (MERGE_TM, D_MODEL), lambda i, j, kk: (i, 0)))
    return ops


def _merge_fwd_epi(att_out, ga, gb, ret_out):
    return att_out, _sigmoid(ga.astype(F32)) * ret_out + _sigmoid(gb.astype(F32)) * att_out


def _merge_bwd_epi(d_merged, ga, gb, ret_out, att_out):
    sa, sb = _sigmoid(ga.astype(F32)), _sigmoid(gb.astype(F32))
    return (d_merged * sa, d_merged * sb, d_merged * ret_out * (sa * (1.0 - sa)),
            d_merged * att_out * (sb * (1.0 - sb)))


def _att_out_bwd_epi(d_att, att):
    outs = []
    for h in range(ATT_HPG):
        sl = slice(h * ATT_DH, (h + 1) * ATT_DH)
        outs.append(jnp.broadcast_to(jnp.sum(d_att[:, sl] * att[:, sl], axis=-1, keepdims=True),
                                     (d_att.shape[0], ATT_DH)))
    return d_att, jnp.concatenate(outs, axis=-1)


def _loss_head(x3, target, gain, branch, gate):
    def body(xt, tt, br, g, gt):
        xhat, r = _rms_parts(xt)
        err = xhat * g - tt
        d_y = err / D_MODEL
        loss = 0.5 * jnp.sum(jnp.mean(err * err, axis=-1, keepdims=True), axis=0, keepdims=True)
        d_x = _rms_bwd(d_y * g, xhat, r)
        return (d_x, d_x * gt, jnp.broadcast_to(loss, (1, 128)), jnp.sum(d_y * xhat, axis=0, keepdims=True),
                jnp.sum(d_x * br, axis=0, keepdims=True))
    return _rowmap("loss_head", body, [x3, target, branch], [gain, gate],
                   [(D_MODEL, F32), (D_MODEL, BF16)], [128, D_MODEL, D_MODEL])


def _local_step(pos, x, target, mod, norm1_g, norm2_g, norm_f_g, rel_bias, gn_g, gn_b, w_in, rest_gather):
    sh1, sc1, g1, sh2, sc2, g2 = [mod[:, i * D_MODEL:(i + 1) * D_MODEL] for i in range(6)]
    cos, sin = _rope_tables()
    din, qd, kd, cd = _decay_tables()
    buckets = _bucket_tables()
    bias = _bias_build(rel_bias, buckets)
    dils = [d for _, d in ATT_GROUPS]

    h1 = _prenorm_fwd("prenorm1_fwd", x, norm1_g, sh1, sc1)

    qk_tn = 2 * RET_DK

    def rot_epi(acc, cs, sn, scale):
        half = RET_DK // 2
        outs = []
        for h0 in range(0, qk_tn, RET_DK):
            x1, x2 = acc[:, h0:h0 + half], acc[:, h0 + half:h0 + RET_DK]
            outs += [x1 * cs - x2 * sn, x1 * sn + x2 * cs]
        return (jnp.concatenate(outs, axis=-1) * scale,)

    qk_scale = jnp.concatenate([jnp.ones((1, RET_QK_W), F32),
                                jnp.full((1, RET_QK_W), RET_DK ** -0.5, F32)], axis=-1)
    rope_ex = [(cos, (TM, RET_DK // 2), lambda i, j, kk: (i, 0)),
               (sin, (TM, RET_DK // 2), lambda i, j, kk: (i, 0)),
               (qk_scale, (1, qk_tn), lambda i, j, kk: (0, j))]
    rest_sems, rest_shards, rest_fulls, rest_token = rest_gather
    behind = [rest_token]
    rv = _matmul("proj_rv", h1, w_in, "nn", SEQ, RET_V_W, D_MODEL, [BF16], b_off=OFF_V, tk=D_MODEL,
                 after=behind)[0]
    rg = _matmul("proj_rg", h1, w_in, "nn", SEQ, RET_V_W, D_MODEL, [BF16], b_off=OFF_G, tk=D_MODEL,
                 after=behind)[0]
    gates = _matmul("proj_gates", h1, w_in, "nn", SEQ, 2 * D_MODEL, D_MODEL, [BF16], b_off=OFF_GATE,
                    tn=512, tk=D_MODEL, after=behind)[0]
    aqkv = _matmul("proj_att", h1, w_in, "nn", SEQ, 9 * ATT_W, D_MODEL, [BF16], b_off=OFF_ATT,
                   tn=512, tk=D_MODEL, after=behind)[0]

    os_, ls_ = [], []
    for gi in range(3):
        o_g, l_g = _att_fwd(gi, dils[gi], aqkv, bias)
        os_.append(o_g)
        ls_.append(l_g)

    rqk = _matmul("proj_qk", h1, w_in, "nn", SEQ, 2 * RET_QK_W, D_MODEL, [BF16], b_off=OFF_Q,
                  tn=qk_tn, tk=D_MODEL, epilogue=rot_epi, extras=rope_ex, after=behind)[0]
    ro, states, gated = _retention_fwd(rqk, rv, rg, gn_g, gn_b, din, qd, kd, cd)
    rest_sems, rest_fulls, fwd_token = _gather_rest_forward(rest_sems, rest_shards, rest_fulls,
                                                            [gated, gates] + os_)
    att, att_b, lse = _combine(os_, ls_, after=[fwd_token])
    w_ret_out, w_att_out, w_o, w_ff1, w_ff2 = _gather_rest_end(rest_sems, rest_fulls, [att_b])
    ret_out = _matmul("ret_out", gated, w_ret_out, "nn", SEQ, D_MODEL, RET_V_W, [F32], tk=RET_V_W)[0]
    att_out, merged = _matmul("att_out", att_b, w_att_out, "nn", SEQ, D_MODEL, ATT_W, [F32, BF16],
                              tm=MERGE_TM, epilogue=_merge_fwd_epi,
                              extras=_merge_operands(gates, ret_out))

    def resid_epi(acc, xt, g):
        return xt + g * acc, acc

    def resid_ex(xin, g):
        return [(xin, (TM, TN), lambda i, j, kk: (i, j)), (g, (1, TN), lambda i, j, kk: (0, j))]

    x2, mix = _matmul("mix_out", merged, w_o, "nn", SEQ, D_MODEL, D_MODEL, [F32, BF16],
                      epilogue=resid_epi, extras=resid_ex(x, g1))
    h2 = _prenorm_fwd("prenorm2_fwd", x2, norm2_g, sh2, sc2)

    def relu2_epi(acc):
        r = jnp.maximum(acc, 0.0)
        return r * r, r

    act, relu_u = _matmul("ff1", h2, w_ff1, "nn", SEQ, D_FF, D_MODEL, [BF16, BF16], tk=D_MODEL,
                          epilogue=relu2_epi)
    x3, y2 = _matmul("ff2", act, w_ff2, "nn", SEQ, D_MODEL, D_FF, [F32, BF16], tk=2048,
                     epilogue=resid_epi, extras=resid_ex(x2, g2))

    d_x3, d_y2, loss, d_gf, d_g2 = _loss_head(x3, target, norm_f_g, y2, g2)

    def relu2_bwd_epi(acc, rt):
        return (acc * (2.0 * rt.astype(F32)),)

    gw_ff2 = _matmul_tn_pair("ff2_dw", pos, act, d_y2, D_FF, D_MODEL, SEQ, D_FF // N_CHIPS,
                             tm=512, tn=1024, tk=SEQ)
    d_u = _matmul("ff2_dx", d_y2, w_ff2, "nt", SEQ, D_FF, D_MODEL, [BF16], epilogue=relu2_bwd_epi,
                  extras=[(relu_u, (TM, TN), lambda i, j, kk: (i, j))])[0]
    gw_ff1 = _matmul_tn_pair("ff1_dw", pos, h2, d_u, D_MODEL, D_FF, SEQ, D_MODEL,
                             tm=512, tn=1024, tk=SEQ)
    ffn = ["w_ff2", "w_ff1"]
    ffn_started = _ici_start("ici_start_ffn", ffn, [gw_ff2, gw_ff1])
    d_h2 = _matmul("ff1_dx", d_u, w_ff1, "nt", SEQ, D_MODEL, D_FF, [F32], tk=2048,
                   after=[ffn_started[3]])[0]
    d_x2, d_mix, d_sh2, d_sc2, d_n2g, d_g1 = _prenorm_bwd("prenorm2_bwd", d_h2, x2, norm2_g, sc2, d_x3,
                                                          branch=mix, gate=g1)
    gw_o = _matmul_tn_pair("mix_dw", pos, merged, d_mix, D_MODEL, D_MODEL, SEQ, D_MODEL // N_CHIPS,
                           tm=128, tn=1024, tk=2048)
    d_ret_out, d_att_out, d_ga, d_gb = _matmul(
        "mix_dx", d_mix, w_o, "nt", SEQ, D_MODEL, D_MODEL, [BF16] * 4, tm=MERGE_TM,
        epilogue=_merge_bwd_epi, extras=_merge_operands(gates, ret_out, att_out))

    gw_ret_out = _matmul_tn_pair("ret_out_dw", pos, gated, d_ret_out, RET_V_W, D_MODEL, SEQ,
                                 RET_V_W // N_CHIPS, tm=256, tn=1024, tk=SEQ)
    gw_att_out = _matmul_tn_pair("att_out_dw", pos, att_b, d_att_out, ATT_W, D_MODEL, SEQ, ATT_W,
                                 tm=256, tn=1024, tk=2048)
    mixer = ["w_o", "w_ret_out", "w_att_out"]
    mixer_started = _ici_start("ici_start_mixer", mixer, [gw_o, gw_ret_out, gw_att_out])
    d_gated = _matmul("ret_out_dx", d_ret_out, w_ret_out, "nt", SEQ, RET_V_W, D_MODEL, [BF16],
                      after=[mixer_started[3]])[0]
    d_att, dd = _matmul("att_out_dx", d_att_out, w_att_out, "nt", SEQ, ATT_W, D_MODEL, [F32, F32],
                        epilogue=_att_out_bwd_epi,
                        extras=[(att, (TM, ATT_W), lambda i, j, kk: (i, 0))], after=[mixer_started[3]])

    d_rqkv, d_rg, d_gn_g, d_gn_b = _retention_bwd(rqk, rv, states, d_gated, ro, rg, gn_g, gn_b,
                                                  din, qd, kd, cd, cos, sin)

    d_aqkv, dsbs = [], []
    for gi in range(3):
        dqkv, dsb = _att_bwd(gi, dils[gi], aqkv, d_att, lse, dd, bias)
        d_aqkv.append(dqkv)
        dsbs.append(dsb)
    d_rel_bias = _bias_grad(jnp.stack(dsbs), buckets)

    d_proj = ([(d_rqkv, False), (d_rg, False)] + [(t, True) for t in d_aqkv]
              + [(d_ga, False), (d_gb, False)])
    gw_in = _matmul_tn_pair("proj_dw", pos, h1, d_proj, D_MODEL, IN_COLS, SEQ, D_MODEL,
                            tm=512, tn=ATT_W, tk=SEQ)
    sems, (gw_in,), (land,), token = _ici_start("ici_start_w_in", ["w_in"], [gw_in])
    d_h1 = _matmul("proj_dx", d_proj, w_in, "nt", SEQ, D_MODEL, IN_COLS, [F32], tn=1024, tk=ATT_W,
                   after=[token])[0]
    pending = (sems, land)

    names = ffn + mixer
    psums, got = _ici_wait("ici_wait_rest", names, list(ffn_started[0]) + list(mixer_started[0]),
                           list(ffn_started[1]) + list(mixer_started[1]),
                           list(ffn_started[2]) + list(mixer_started[2]), [d_h1])
    g_big = {n: _final_sum("final_" + n, pos, dict(BIG)[n], psums[i], got[i], SHARD[n])
             for i, n in enumerate(names)}
    grad_x, d_sh1, d_sc1, d_n1g = _prenorm_bwd("prenorm1_bwd", d_h1, x, norm1_g, sc1, d_x2,
                                               after=list(g_big.values()))
    d_mod = jnp.concatenate([d_sh1, d_sc1, d_g1, d_sh2, d_sc2, d_g2], axis=-1)
    small = dict(norm1_g=d_n1g, norm2_g=d_n2g, norm_f_g=d_gf, gn_g=d_gn_g, gn_b=d_gn_b,
                 rel_bias=d_rel_bias)
    return loss, grad_x, d_mod, small, g_big, (gw_in,) + pending


def _me():
    return lax.axis_index("x"), lax.axis_index("y"), lax.axis_index("c")


def _peer(x, y, c, mask):
    return (x ^ ((mask >> 2) & 1), y ^ ((mask >> 1) & 1), c ^ (mask & 1))


def _gather8(src_ref, dst_ref, send_sems, recv_sems):
    x, y, c = _me()
    me = 4 * x + 2 * y + c
    copies = []
    for mask in range(1, N_DEV):
        cp = pltpu.make_async_remote_copy(
            src_ref=src_ref, dst_ref=dst_ref.at[me], send_sem=send_sems.at[mask - 1],
            recv_sem=recv_sems.at[mask - 1], device_id=_peer(x, y, c, mask), device_id_type=MESH)
        cp.start()
        copies.append(cp)
    dst_ref[me] = src_ref[...]
    for cp in copies:
        cp.wait_recv()
    for cp in copies:
        cp.wait_send()


def _ada_fwd(c_in, w_ada, b_ada):
    ncol = ADA_COLS // N_CHIPS

    def body(c_ref, w_ref, b_ref, mod_ref, sc_ref, cbuf, cg, mbuf, mg, s1, r1, s2, r2):
        x, y, c = _me()
        me = 4 * x + 2 * y + c
        cv = c_ref[...]
        cbuf[...] = jnp.broadcast_to(cv * _sigmoid(cv), cbuf.shape)
        _gather8(cbuf, cg, s1, r1)
        rows = lax.broadcasted_iota(I32, (N_DEV, D_MODEL), 0)
        sc_all = jnp.zeros((N_DEV, D_MODEL), F32)
        for d in range(N_DEV):
            sc_all = jnp.where(rows == d, cg[d], sc_all)
        sc_ref[...] = sc_all
        mbuf[...] = jnp.dot(sc_all.astype(BF16), w_ref[...].astype(BF16), preferred_element_type=F32)
        _gather8(mbuf, mg, s2, r2)
        rowsel = lax.broadcasted_iota(I32, (N_DEV, ncol), 0) == me
        for k in range(N_CHIPS):
            blk = mg[2 * k]
            row = jnp.sum(jnp.where(rowsel, blk, 0.0), axis=0, keepdims=True)
            mod_ref[:, k * ncol:(k + 1) * ncol] = row + b_ref[:, k * ncol:(k + 1) * ncol]

    vm = pl.BlockSpec(memory_space=pltpu.VMEM)
    return pl.pallas_call(
        body, name="ada_fwd",
        in_specs=[vm, vm, vm], out_specs=[vm, vm],
        out_shape=[jax.ShapeDtypeStruct((1, ADA_COLS), F32), jax.ShapeDtypeStruct((N_DEV, D_MODEL), F32)],
        scratch_shapes=[
            pltpu.VMEM((8, D_MODEL), F32), pltpu.VMEM((N_DEV, 8, D_MODEL), F32),
            pltpu.VMEM((8, ncol), F32), pltpu.VMEM((N_DEV, 8, ncol), F32),
            pltpu.SemaphoreType.DMA((N_DEV - 1,)), pltpu.SemaphoreType.DMA((N_DEV - 1,)),
            pltpu.SemaphoreType.DMA((N_DEV - 1,)), pltpu.SemaphoreType.DMA((N_DEV - 1,)),
        ],
        compiler_params=pltpu.CompilerParams(vmem_limit_bytes=VMEM_LIMIT_V7X),
    )(c_in, w_ada, b_ada)


def _small_reduce(pack, sc_all):
    ncol = ADA_COLS // N_CHIPS

    def body(p_ref, sc_ref, tot_ref, gw_ref, pg, s1, r1):
        x, y, _ = _me()
        chip = 2 * x + y
        _gather8(p_ref, pg, s1, r1)
        tot = pg[0]
        for d in range(1, N_DEV):
            tot = tot + pg[d]
        tot_ref[...] = tot
        rows = lax.broadcasted_iota(I32, (N_DEV, ncol), 0)
        dmod = jnp.zeros((N_DEV, ncol), F32)
        for k in range(N_CHIPS):
            part = jnp.zeros((N_DEV, ncol), F32)
            for d in range(N_DEV):
                part = jnp.where(rows == d, pg[d, :, k * ncol:(k + 1) * ncol][0:1, :], part)
            dmod = jnp.where(chip == k, part, dmod)
        gw_ref[...] = lax.dot_general(sc_ref[...].astype(BF16), dmod.astype(BF16), _TN,
                                      preferred_element_type=F32)

    vm = pl.BlockSpec(memory_space=pltpu.VMEM)
    return pl.pallas_call(
        body, name="small_reduce",
        in_specs=[vm, vm], out_specs=[vm, vm],
        out_shape=[jax.ShapeDtypeStruct((8, ADA_COLS), F32), jax.ShapeDtypeStruct((D_MODEL, ncol), F32)],
        scratch_shapes=[pltpu.VMEM((N_DEV, 8, ADA_COLS), F32),
                        pltpu.SemaphoreType.DMA((N_DEV - 1,)), pltpu.SemaphoreType.DMA((N_DEV - 1,))],
        compiler_params=pltpu.CompilerParams(vmem_limit_bytes=VMEM_LIMIT_V7X),
    )(pack, sc_all)


BIG = (("w_in", 1), ("w_ret_out", 0), ("w_att_out", 1), ("w_o", 0), ("w_ff1", 1), ("w_ff2", 0))
SHARD = {"w_in": (D_MODEL, IN_COLS // N_CHIPS), "w_ret_out": (RET_V_W // N_CHIPS, D_MODEL),
         "w_att_out": (ATT_W, D_MODEL // N_CHIPS), "w_o": (D_MODEL // N_CHIPS, D_MODEL),
         "w_ff1": (D_MODEL, D_FF // N_CHIPS), "w_ff2": (D_FF // N_CHIPS, D_MODEL)}
_CHIP_FLIPS = ((1, 0), (0, 1), (1, 1))


def _region(ref, axis, chip, half, shard_shape):
    r, cw = shard_shape
    hr = r // 2
    if axis == 1:
        return ref.at[pl.ds(half * hr, hr), pl.ds(chip * cw, cw)]
    return ref.at[pl.ds(chip * r + half * hr, hr), :]


def _gather_weights(shards, n_remote):
    nw = len(BIG)
    shapes = [s.shape for s in shards]
    full_shapes = [(r, N_CHIPS * cw) if ax == 1 else (N_CHIPS * r, cw)
                   for (r, cw), (_, ax) in zip(shapes, BIG)]

    def body(*refs):
        ins, outs = refs[:nw], refs[nw:2 * nw]
        own = refs[2 * nw:3 * nw]
        from_ici, from_sib = refs[3 * nw:3 * nw + n_remote], refs[3 * nw + n_remote:3 * nw + 2 * n_remote]
        ld_sem, st_sem, s_ici, r_ici, s_d2d, r_d2d, st_a, st_b = refs[3 * nw + 2 * n_remote:]
        x, y, c = _me()
        chip = 2 * x + y
        sib = (x, y, 1 - c)
        loads = [pltpu.make_async_copy(ins[i], own[i], ld_sem.at[i]) for i in range(nw)]
        for cp in loads:
            cp.start()
        pending, first = [], []
        for i, (_, ax) in enumerate(BIG):
            r, cw = shapes[i]
            hr = r // 2
            loads[i].wait()
            dst = outs[i].at[:, pl.ds(chip * cw, cw)] if ax == 1 else outs[i].at[pl.ds(chip * r, r), :]
            cp = pltpu.make_async_copy(own[i], dst, st_sem.at[i])
            cp.start()
            pending.append(cp)
            for j, (fx, fy) in enumerate(_CHIP_FLIPS if i < n_remote else ()):
                rc = pltpu.make_async_remote_copy(
                    src_ref=own[i].at[pl.ds(c * hr, hr), :], dst_ref=from_ici[i].at[j],
                    send_sem=s_ici.at[j * nw + i], recv_sem=r_ici.at[j * nw + i],
                    device_id=(x ^ fx, y ^ fy, c), device_id_type=MESH)
                rc.start()
                first.append((j, i, rc))
        passed = []
        for j, i, rc in first:
            fx, fy = _CHIP_FLIPS[j]
            src_chip = 2 * (x ^ fx) + (y ^ fy)
            ax = BIG[i][1]
            rc.wait_recv()
            fw = pltpu.make_async_remote_copy(
                src_ref=from_ici[i].at[j], dst_ref=from_sib[i].at[j], send_sem=s_d2d.at[j * nw + i],
                recv_sem=r_d2d.at[j * nw + i], device_id=sib, device_id_type=MESH)
            fw.start()
            passed.append((j, i, src_chip, fw))
            st = pltpu.make_async_copy(from_ici[i].at[j], _region(outs[i], ax, src_chip, c, shapes[i]),
                                       st_a.at[j * nw + i])
            st.start()
            pending.append(st)
        for j, i, src_chip, fw in passed:
            fw.wait_recv()
            st = pltpu.make_async_copy(from_sib[i].at[j],
                                       _region(outs[i], BIG[i][1], src_chip, 1 - c, shapes[i]),
                                       st_b.at[j * nw + i])
            st.start()
            pending.append(st)
        for _, _, rc in first:
            rc.wait_send()
        for _, _, _, fw in passed:
            fw.wait_send()
        for cp in pending:
            cp.wait()

    hbm = pl.BlockSpec(memory_space=pl.ANY)
    halves = [pltpu.VMEM((3, r // 2, cw), BF16) for r, cw in shapes[:n_remote]]
    return pl.pallas_call(
        body, name="gather_weights",
        in_specs=[hbm] * nw, out_specs=[hbm] * nw,
        out_shape=[jax.ShapeDtypeStruct(fs, BF16) for fs in full_shapes],
        scratch_shapes=[pltpu.VMEM(sh, BF16) for sh in shapes] + halves + halves
        + [pltpu.SemaphoreType.DMA((nw,)), pltpu.SemaphoreType.DMA((nw,))]
        + [pltpu.SemaphoreType.DMA((3 * nw,))] * 6,
        compiler_params=pltpu.CompilerParams(vmem_limit_bytes=VMEM_LIMIT_V7X),
    )(*shards)


REST = BIG[1:]
_SIDE_EFFECTS = pltpu.CompilerParams(has_side_effects=pltpu.SideEffectType.DATAFLOW_SIDE_EFFECTING)
_ANY_SPEC = pl.BlockSpec(memory_space=pl.ANY)


def _rest_ici_copies(shard_refs, full_refs, sems):
    x, y, c = _me()
    chip = 2 * x + y
    n = 3 * len(REST)
    copies = []
    for i, (name, ax) in enumerate(REST):
        hr = SHARD[name][0] // 2
        for j, (fx, fy) in enumerate(_CHIP_FLIPS):
            copies.append(pltpu.make_async_remote_copy(
                src_ref=shard_refs[i].at[pl.ds(c * hr, hr), :],
                dst_ref=_region(full_refs[i], ax, chip, c, SHARD[name]),
                send_sem=sems[3 * i + j], recv_sem=sems[n + 3 * i + j],
                device_id=(x ^ fx, y ^ fy, c), device_id_type=MESH))
    return copies


def _rest_d2d_copies(full_refs, sems):
    x, y, c = _me()
    n = 3 * len(REST)
    copies = []
    for i, (name, ax) in enumerate(REST):
        for j, (fx, fy) in enumerate(_CHIP_FLIPS):
            reg = _region(full_refs[i], ax, 2 * (x ^ fx) + (y ^ fy), c, SHARD[name])
            copies.append(pltpu.make_async_remote_copy(
                src_ref=reg, dst_ref=reg, send_sem=sems[3 * i + j], recv_sem=sems[n + 3 * i + j],
                device_id=(x, y, 1 - c), device_id_type=MESH))
    return copies


def _gather_rest_start(shards, fulls, after):
    nr, ns, na = len(REST), 6 * len(REST), len(after)

    def body(*refs):
        for cp in _rest_ici_copies(refs[:nr], refs[nr:2 * nr], refs[2 * nr + na:2 * nr + na + ns]):
            cp.start()
        token = refs[-1]
        token[...] = jnp.zeros_like(token)

    hbm = lambda a: pltpu.HBM(a.shape, a.dtype)
    res = pl.pallas_call(
        body, name="gather_rest_start",
        out_shape=(pltpu.SemaphoreType.DMA(()),) * ns + tuple(hbm(a) for a in shards + fulls)
        + (jax.ShapeDtypeStruct((8, 128), F32),),
        in_specs=(_HBM_SPEC,) * (2 * nr) + (_ANY_SPEC,) * na,
        out_specs=(_SEM_SPEC,) * ns + (_HBM_SPEC,) * (2 * nr) + (pl.BlockSpec(memory_space=pltpu.VMEM),),
        input_output_aliases={k: ns + k for k in range(2 * nr)}, compiler_params=_SIDE_EFFECTS,
    )(*[pltpu.with_memory_space_constraint(a, pltpu.HBM) for a in shards + fulls], *after)
    return res[:ns], res[ns:ns + nr], res[ns + nr:ns + 2 * nr], res[-1]


def _gather_rest_forward(sems, shards, fulls, after):
    nr, ns = len(REST), 6 * len(REST)

    def body(*refs):
        shard_refs, full_refs, old = refs[:nr], refs[nr:2 * nr], refs[2 * nr:2 * nr + ns]
        new = refs[2 * nr + ns + len(after):2 * nr + 2 * ns + len(after)]
        for cp in _rest_ici_copies(shard_refs, full_refs, old):
            cp.wait_send()
            cp.wait_recv()
        for cp in _rest_d2d_copies(full_refs, new):
            cp.start()
        token = refs[-1]
        token[...] = jnp.zeros_like(token)

    res = pl.pallas_call(
        body, name="gather_rest_forward",
        out_shape=(pltpu.SemaphoreType.DMA(()),) * ns + tuple(pltpu.HBM(a.shape, a.dtype) for a in fulls)
        + (jax.ShapeDtypeStruct((8, 128), F32),),
        in_specs=(_HBM_SPEC,) * (2 * nr) + (_SEM_SPEC,) * ns + (_ANY_SPEC,) * len(after),
        out_specs=(_SEM_SPEC,) * ns + (_HBM_SPEC,) * nr + (pl.BlockSpec(memory_space=pltpu.VMEM),),
        input_output_aliases={nr + k: ns + k for k in range(nr)}, compiler_params=_SIDE_EFFECTS,
    )(*shards, *fulls, *sems, *after)
    return res[:ns], res[ns:ns + nr], res[-1]


def _gather_rest_end(sems, fulls, after):
    nr, ns = len(REST), 6 * len(REST)

    def body(*refs):
        for cp in _rest_d2d_copies(refs[:nr], refs[nr:nr + ns]):
            cp.wait_send()
            cp.wait_recv()

    return pl.pallas_call(
        body, name="gather_rest_end",
        out_shape=tuple(pltpu.HBM(a.shape, a.dtype) for a in fulls),
        in_specs=(_HBM_SPEC,) * nr + (_SEM_SPEC,) * ns + (_ANY_SPEC,) * len(after),
        out_specs=(_HBM_SPEC,) * nr,
        input_output_aliases={k: k for k in range(nr)}, compiler_params=_SIDE_EFFECTS,
    )(*fulls, *sems, *after)


def _adam_update(w, g, m, v):
    mn = ADAM_B1 * m + (1.0 - ADAM_B1) * g
    vn = ADAM_B2 * v + (1.0 - ADAM_B2) * (g * g)
    m_hat = mn / (1.0 - ADAM_B1 ** ADAM_STEP)
    v_hat = vn / (1.0 - ADAM_B2 ** ADAM_STEP)
    return -ADAM_LR * (m_hat / (jnp.sqrt(v_hat) + ADAM_EPS) + ADAM_WD * w), mn, vn


def _final_sum(name, pos, axis, psum, recv, shard_shape, after=(), tr=128):
    r, cw = shard_shape
    hr = r // 2
    tr = min(tr, hr)
    nt = hr // tr
    n_after = len(after)

    def kern(pos_ref, p_ref, r_ref, *rest):
        g_ref, send_buf, land_buf, s_sem, r_sem = rest[n_after:]
        p, t = pl.program_id(0), pl.program_id(1)
        sib = _sibling()

        def copy(i):
            return pltpu.make_async_remote_copy(
                src_ref=send_buf.at[i], dst_ref=land_buf.at[i], send_sem=s_sem.at[i],
                recv_sem=r_sem.at[i], device_id=sib, device_id_type=MESH)

        @pl.when(p == 0)
        def _():
            tot = p_ref[...].astype(F32)
            for j in range(3):
                tot = tot + r_ref[j].astype(F32)
            send_buf[t] = tot
            copy(t).start()
            g_ref[...] = tot

        @pl.when(p == 1)
        def _():
            copy(t).wait_recv()
            g_ref[...] = land_buf[t]

        @pl.when(jnp.logical_and(p == 1, t == nt - 1))
        def _():
            for i in range(nt):
                copy(i).wait_send()

    def shard_rows(p, t, pos_ref):
        return (jnp.where(p == 0, pos_ref[0], 1 - pos_ref[0]) * nt + t, 0)

    def own_part(p, t, pos_ref):
        tt = jnp.where(p == 0, t, nt - 1)
        return (tt, pos_ref[1]) if axis == 1 else (pos_ref[1] * nt + tt, 0)

    grid_spec = pltpu.PrefetchScalarGridSpec(
        num_scalar_prefetch=1, grid=(2, nt),
        in_specs=[pl.BlockSpec((tr, cw), own_part),
                  pl.BlockSpec((3, tr, cw), lambda p, t, pos_ref: (0, jnp.where(p == 0, t, nt - 1), 0))]
        + [pl.BlockSpec(memory_space=pl.ANY)] * n_after,
        out_specs=pl.BlockSpec((tr, cw), shard_rows),
        scratch_shapes=[pltpu.VMEM((nt, tr, cw), F32), pltpu.VMEM((nt, tr, cw), F32),
                        pltpu.SemaphoreType.DMA((nt,)), pltpu.SemaphoreType.DMA((nt,))])
    return pl.pallas_call(
        kern, name=name, grid_spec=grid_spec, out_shape=jax.ShapeDtypeStruct((r, cw), F32),
        compiler_params=_cparams(("arbitrary", "arbitrary")),
    )(pos, psum, recv, *after)


def _adamw(name, w, g, m, v):
    r, cw = w.shape
    tr = min(r, 128)

    def kern(w_ref, g_ref, m_ref, v_ref, go_ref, d_ref, nm_ref, nv_ref):
        gv = g_ref[...]
        go_ref[...] = gv
        d_ref[...], nm_ref[...], nv_ref[...] = _adam_update(w_ref[...], gv, m_ref[...], v_ref[...])

    spec = pl.BlockSpec((tr, cw), lambda i: (i, 0))
    return pl.pallas_call(
        kern, name=name, grid=(r // tr,), in_specs=[spec] * 4, out_specs=[spec] * 4,
        out_shape=[jax.ShapeDtypeStruct((r, cw), F32)] * 4, compiler_params=_cparams(("parallel",)),
    )(w, g, m, v)


_PACK_W = ADA_COLS
_NB = REL_BUCKETS * N_ATT_HEADS
_SMALL_SLOTS = {
    "b_ada": (0, 0, ADA_COLS),
    "norm1_g": (1, 0, D_MODEL), "norm2_g": (1, D_MODEL, D_MODEL), "norm_f_g": (1, 2 * D_MODEL, D_MODEL),
    "ret_gn_g": (1, 3 * D_MODEL, RET_V_W),
    "ret_gn_b": (2, 0, RET_V_W), "rel_bias": (2, RET_V_W, _NB), "loss": (2, RET_V_W + 512, 128),
}


def _pack_small(vals):
    rows = []
    for r in range(8):
        items = sorted([(off, n) for n, (rr, off, _) in _SMALL_SLOTS.items() if rr == r and n in vals])
        parts, pos = [], 0
        for off, n in items:
            if off > pos:
                parts.append(jnp.zeros((1, off - pos), F32))
            parts.append(vals[n].reshape(1, -1).astype(F32))
            pos = off + _SMALL_SLOTS[n][2]
        if pos < _PACK_W:
            parts.append(jnp.zeros((1, _PACK_W - pos), F32))
        rows.append(jnp.concatenate(parts, axis=-1))
    return jnp.concatenate(rows, axis=0)


def _unpack_small(pack, name):
    r, off, wd = _SMALL_SLOTS[name]
    return pack[r:r + 1, off:off + wd]


def kernel(x, c, w_ada, b_ada, norm1_g, w_in, rel_bias, ret_gn_g, ret_gn_b, w_ret_out, w_att_out, w_o, norm2_g, w_ff1, w_ff2, norm_f_g, loss_target, m_w_ada, m_b_ada, m_norm1_g, m_w_in, m_rel_bias, m_ret_gn_g, m_ret_gn_b, m_w_ret_out, m_w_att_out, m_w_o, m_norm2_g, m_w_ff1, m_w_ff2, m_norm_f_g, v_w_ada, v_b_ada, v_norm1_g, v_w_in, v_rel_bias, v_ret_gn_g, v_ret_gn_b, v_w_ret_out, v_w_att_out, v_w_o, v_norm2_g, v_w_ff1, v_w_ff2, v_norm_f_g):
    given = dict(locals())
    big_names = [n for n, _ in BIG]
    shard_w = {n: given[n][0] for n in big_names}
    assert all(shard_w[n].shape == SHARD[n] for n in big_names)

    shards_bf = [shard_w[n].astype(BF16) for n in big_names]
    full = _gather_weights(shards_bf, 1)
    mod, sc_all = _ada_fwd(c, w_ada[0], b_ada)
    rest_gather = _gather_rest_start(shards_bf[1:], list(full[1:]), [mod])
    pos = _where_am_i()

    loss, grad_x, d_mod, small, g_big, pending = _local_step(
        pos, x[0], loss_target[0], mod, norm1_g, norm2_g, norm_f_g.reshape(1, -1), rel_bias, ret_gn_g,
        ret_gn_b, full[0], rest_gather)

    pack_g = _pack_small(dict(b_ada=d_mod, norm1_g=small["norm1_g"], norm2_g=small["norm2_g"],
                              norm_f_g=small["norm_f_g"], ret_gn_g=small["gn_g"], ret_gn_b=small["gn_b"],
                              rel_bias=small["rel_bias"], loss=loss))
    tot, g_w_ada = _small_reduce(pack_g, sc_all)

    small_names = ["b_ada", "norm1_g", "rel_bias", "ret_gn_g", "ret_gn_b", "norm2_g", "norm_f_g"]
    pack_w = _pack_small({n: given[n] for n in small_names})
    pack_m = _pack_small({n: given["m_" + n] for n in small_names})
    pack_v = _pack_small({n: given["v_" + n] for n in small_names})
    _, sd, sm, sv = _adamw("adamw_small", pack_w, tot, pack_m, pack_v)

    grads, deltas, new_m, new_v = {}, {}, {}, {}
    for n in small_names:
        shp = given[n].shape
        grads[n] = _unpack_small(tot, n).reshape(shp)
        deltas[n] = _unpack_small(sd, n).reshape(shp)
        new_m[n] = _unpack_small(sm, n).reshape(shp)
        new_v[n] = _unpack_small(sv, n).reshape(shp)
    g_big["w_ada"] = g_w_ada
    for n in ["w_ada"] + big_names[1:] + big_names[:1]:
        if n == "w_in":
            gw_in, sems, land = pending
            done = [tot, sd] + [deltas[k] for k in ["w_ada"] + big_names[1:]]
            (gw_in,), (got,) = _ici_wait("ici_wait_w_in", [n], sems, [gw_in], [land], done)
            g_big[n] = _final_sum("final_w_in", pos, 1, gw_in, got, SHARD[n])
        g, d, nm, nv = _adamw("adamw_" + n, given[n][0], g_big[n], given["m_" + n][0], given["v_" + n][0])
        grads[n], deltas[n], new_m[n], new_v[n] = g[None], d[None], nm[None], nv[None]

    order = ["w_ada", "b_ada", "norm1_g", "w_in", "rel_bias", "ret_gn_g", "ret_gn_b", "w_ret_out",
             "w_att_out", "w_o", "norm2_g", "w_ff1", "w_ff2", "norm_f_g"]
    loss_out = _unpack_small(tot, "loss")[0, 0]
    return (loss_out, grad_x[None], *[grads[n] for n in order], *[deltas[n] for n in order],
            *[new_m[n] for n in order], *[new_v[n] for n in order])
```

```python
import functools
import math

import jax
import jax.numpy as jnp
import numpy as np
from jax import lax
from jax.experimental import pallas as pl
from jax.experimental.pallas import tpu as pltpu

F32 = jnp.float32
BF16 = jnp.bfloat16
I32 = jnp.int32

SEQ = 2048
D_MODEL = 1024
RET_HEADS = 4
RET_DK = 256
RET_DV = 512
RET_CHUNK = 128
RET_SUB = 2
RET_QK_W = RET_HEADS * RET_DK
RET_V_W = RET_HEADS * RET_DV
ATT_GROUPS = ((128, 1), (512, 4), (2048, 16))
ATT_HPG = 4
ATT_DH = 128
ATT_W = ATT_HPG * ATT_DH
ATT_BLK = 128
N_BLK = SEQ // ATT_BLK
REL_BUCKETS = 32
REL_MAX_DIST = 2048
N_ATT_HEADS = 12
D_FF = 4 * D_MODEL
RMS_EPS = 1e-6
GN_EPS = 1e-5
ROPE_BASE = 10000.0
IN_COLS = 2 * RET_QK_W + 2 * RET_V_W + 9 * ATT_W + 2 * D_MODEL
OFF_Q, OFF_K, OFF_V, OFF_G = 0, RET_QK_W, 2 * RET_QK_W, 2 * RET_QK_W + RET_V_W
OFF_ATT = 2 * RET_QK_W + 2 * RET_V_W
OFF_GATE = OFF_ATT + 9 * ATT_W
N_CHIPS = 4
N_DEV = 8
ADA_COLS = 6 * D_MODEL

ADAM_LR = 0.001
ADAM_B1 = 0.9
ADAM_B2 = 0.999
ADAM_EPS = 1e-08
ADAM_WD = 0.01
ADAM_STEP = 10

VMEM_LIMIT_V7X = 56 * 1024 * 1024
MESH = pl.DeviceIdType.MESH


def _cparams(sem):
    return pltpu.CompilerParams(dimension_semantics=sem, vmem_limit_bytes=VMEM_LIMIT_V7X)


def _sigmoid(v):
    return 1.0 / (1.0 + jnp.exp(-v))


def _rowmap(name, body, row_ins, bcast_ins, row_outs, sum_outs=(), tm=256, after=()):
    m = row_ins[0].shape[0]
    n_in = len(row_ins) + len(bcast_ins)
    n_ro = len(row_outs)

    def kern(*refs):
        vals = [r[...] for r in refs[:n_in]]
        res = body(*vals)
        if not isinstance(res, (tuple, list)):
            res = (res,)
        outs = refs[n_in + len(after):]
        for r, v in zip(outs[:n_ro], res[:n_ro]):
            r[...] = v.astype(r.dtype)
        if sum_outs:
            @pl.when(pl.program_id(0) == 0)
            def _():
                for r in outs[n_ro:]:
                    r[...] = jnp.zeros_like(r)
            for r, v in zip(outs[n_ro:], res[n_ro:]):
                r[...] += v

    in_specs = [pl.BlockSpec((tm, a.shape[1]), lambda i: (i, 0)) for a in row_ins]
    in_specs += [pl.BlockSpec(a.shape, lambda i: (0, 0)) for a in bcast_ins]
    in_specs += [pl.BlockSpec(memory_space=pl.ANY)] * len(after)
    out_specs = [pl.BlockSpec((tm, n), lambda i: (i, 0)) for n, _ in row_outs]
    out_specs += [pl.BlockSpec((1, n), lambda i: (0, 0)) for n in sum_outs]
    out_shape = [jax.ShapeDtypeStruct((m, n), dt) for n, dt in row_outs]
    out_shape += [jax.ShapeDtypeStruct((1, n), F32) for n in sum_outs]
    return pl.pallas_call(
        kern, name=name, grid=(m // tm,), in_specs=in_specs, out_specs=out_specs,
        out_shape=out_shape, compiler_params=_cparams(("arbitrary",)),
    )(*row_ins, *bcast_ins, *after)


TM, TN = 1024, 1024


def _piece_chunks(piece, width):
    arr, stacked = piece
    return arr.shape[0] if stacked else arr.shape[1] // width


def _piece_spec(piece, rows, width, start, row_of, chunk_of):
    arr, stacked = piece
    last = _piece_chunks(piece, width) - 1

    def local(*ids):
        return jnp.clip(chunk_of(*ids) - start, 0, last)

    def row(*ids):
        rel = chunk_of(*ids) - start
        return jnp.where(jnp.logical_and(rel >= 0, rel <= last), row_of(*ids), 0)

    if stacked:
        return pl.BlockSpec((None, rows, width), lambda *ids: (local(*ids), row(*ids), 0))
    return pl.BlockSpec((rows, width), lambda *ids: (row(*ids), local(*ids)))


def _piece_starts(pieces, width):
    return [sum(_piece_chunks(p, width) for p in pieces[:q]) for q in range(len(pieces))]


def _matmul(name, a, b, kind, m, n, k, outs, *, b_off=0, tm=TM, tn=TN, tk=1024,
            epilogue=None, extras=(), after=(), n_sums=0):
    tm, tn, tk = min(tm, m), min(tn, n), min(tk, k)
    nk = k // tk
    pieces = a if isinstance(a, list) else [(a, False)]
    starts = _piece_starts(pieces, tk)
    if kind == "nn":
        a_specs = [pl.BlockSpec((tm, tk), lambda i, j, kk: (i, kk))]
        b_spec = pl.BlockSpec((tk, tn), lambda i, j, kk: (kk, b_off // tn + j))
        dn = (((1,), (0,)), ((), ()))
    elif kind == "nt":
        a_specs = [_piece_spec(p, tm, tk, st, lambda i, j, kk: i, lambda i, j, kk: kk)
                   for p, st in zip(pieces, starts)]
        b_spec = pl.BlockSpec((tn, tk), lambda i, j, kk: (j, b_off // tk + kk))
        dn = (((1,), (1,)), ((), ()))
    else:
        a_specs = [pl.BlockSpec((tk, tm), lambda i, j, kk: (kk, i))]
        b_spec = pl.BlockSpec((tk, tn), lambda i, j, kk: (kk, j))
        dn = (((0,), (0,)), ((), ()))
    n_a, n_ex, n_out = len(pieces), len(extras), len(outs)
    if epilogue is None:
        epilogue = lambda acc: (acc,)

    assert n_sums == 0 or tn == n

    def finish(acc, ex_refs, out_refs, first_rows):
        res = epilogue(acc, *[r[...] for r in ex_refs])
        for r, v in zip(out_refs[:n_out], res[:n_out]):
            r[...] = v.astype(r.dtype)
        for r, v in zip(out_refs[n_out:], res[n_out:]):
            @pl.when(first_rows)
            def _(r=r, v=v):
                r[...] = v

            @pl.when(jnp.logical_not(first_rows))
            def _(r=r, v=v):
                r[...] += v

    n_in = n_a + 1 + n_ex + len(after)

    def kern(*refs):
        a_refs, b_ref = refs[:n_a], refs[n_a]
        ex_refs = refs[n_a + 1:n_a + 1 + n_ex]
        out_refs = refs[n_in:n_in + n_out + n_sums]
        first_rows, kk = pl.program_id(0) == 0, pl.program_id(2)
        dot = lambda a_ref: lax.dot_general(a_ref[...], b_ref[...], dn, preferred_element_type=F32)
        if nk == 1:
            finish(dot(a_refs[0]), ex_refs, out_refs, first_rows)
            return
        acc_ref = refs[n_in + n_out + n_sums]
        if n_a == 1:
            part = dot(a_refs[0])

            @pl.when(kk == 0)
            def _():
                acc_ref[...] = part

            @pl.when(kk > 0)
            def _():
                acc_ref[...] += part
        else:
            @pl.when(kk == 0)
            def _():
                acc_ref[...] = jnp.zeros_like(acc_ref)

            for q in range(n_a):
                @pl.when(jnp.logical_and(kk >= starts[q], kk < starts[q] + _piece_chunks(pieces[q], tk)))
                def _(q=q):
                    acc_ref[...] += dot(a_refs[q])

        @pl.when(kk == nk - 1)
        def _():
            finish(acc_ref[...], ex_refs, out_refs, first_rows)

    in_specs = a_specs + [b_spec] + [pl.BlockSpec(bs, im) for _, bs, im in extras]
    in_specs += [pl.BlockSpec(memory_space=pl.ANY)] * len(after)
    sem = ("arbitrary",) * 3 if n_sums else ("parallel", "parallel", "arbitrary")
    return pl.pallas_call(
        kern, name=name, grid=(m // tm, n // tn, nk), in_specs=in_specs,
        out_specs=[pl.BlockSpec((tm, tn), lambda i, j, kk: (i, j)) for _ in outs]
        + [pl.BlockSpec((1, tn), lambda i, j, kk: (0, 0))] * n_sums,
        out_shape=[jax.ShapeDtypeStruct((m, n), dt) for dt in outs]
        + [jax.ShapeDtypeStruct((1, n), F32)] * n_sums,
        scratch_shapes=[] if nk == 1 else [pltpu.VMEM((tm, tn), F32)],
        compiler_params=_cparams(sem),
    )(*[p[0] for p in pieces], b, *[e[0] for e in extras], *after)


def _ici_copies(psum_ref, recv_ref, s_sem, r_sem, axis, shard_shape):
    x, y, c = _me()
    hr, cw = shard_shape[0] // 2, shard_shape[1]
    pick = lambda sems, j: sems[j] if isinstance(sems, (list, tuple)) else sems.at[j]
    copies = []
    for j, (fx, fy) in enumerate(_CHIP_FLIPS):
        chip = 2 * (x ^ fx) + (y ^ fy)
        src = psum_ref.at[:, pl.ds(chip * cw, cw)] if axis == 1 else psum_ref.at[pl.ds(chip * hr, hr), :]
        copies.append(pltpu.make_async_remote_copy(
            src_ref=src, dst_ref=recv_ref.at[j], send_sem=pick(s_sem, j), recv_sem=pick(r_sem, j),
            device_id=(x ^ fx, y ^ fy, c), device_id_type=MESH))
    return copies


_HBM_SPEC = pl.BlockSpec(memory_space=pltpu.HBM)
_SEM_SPEC = pl.BlockSpec(memory_space=pltpu.SEMAPHORE)


def _split_ici_copies(names, p_refs, land_refs, sems):
    copies = []
    for i, n in enumerate(names):
        copies += _ici_copies(p_refs[i], land_refs[i], list(sems[6 * i:6 * i + 3]),
                              list(sems[6 * i + 3:6 * i + 6]), dict(BIG)[n], SHARD[n])
    return copies


def _ici_start(name, names, psums):
    nw, ns = len(names), 6 * len(names)
    lands = [lax.empty((3, SHARD[n][0] // 2, SHARD[n][1]), BF16) for n in names]

    def body(*refs):
        for cp in _split_ici_copies(names, refs[:nw], refs[nw:2 * nw], refs[2 * nw:2 * nw + ns]):
            cp.start()
        token = refs[-1]
        token[...] = jnp.zeros_like(token)

    res = pl.pallas_call(
        body, name=name,
        out_shape=(pltpu.SemaphoreType.DMA(()),) * ns
        + tuple(pltpu.HBM(a.shape, BF16) for a in list(psums) + lands)
        + (jax.ShapeDtypeStruct((8, 128), F32),),
        in_specs=(_HBM_SPEC,) * (2 * nw),
        out_specs=(_SEM_SPEC,) * ns + (_HBM_SPEC,) * (2 * nw) + (pl.BlockSpec(memory_space=pltpu.VMEM),),
        input_output_aliases={k: ns + k for k in range(2 * nw)},
        compiler_params=pltpu.CompilerParams(has_side_effects=pltpu.SideEffectType.DATAFLOW_SIDE_EFFECTING),
    )(*[pltpu.with_memory_space_constraint(a, pltpu.HBM) for a in list(psums) + lands])
    return res[:ns], res[ns:ns + nw], res[ns + nw:ns + 2 * nw], res[-1]


def _ici_wait(name, names, sems, p_thru, land_thru, after):
    nw, ns = len(names), 6 * len(names)

    def body(*refs):
        for cp in _split_ici_copies(names, refs[:nw], refs[nw:2 * nw], refs[2 * nw:2 * nw + ns]):
            cp.wait_send()
            cp.wait_recv()

    res = pl.pallas_call(
        body, name=name,
        out_shape=tuple(pltpu.HBM(a.shape, BF16) for a in list(p_thru) + list(land_thru)),
        in_specs=(_HBM_SPEC,) * (2 * nw) + (_SEM_SPEC,) * ns + (pl.BlockSpec(memory_space=pl.ANY),) * len(after),
        out_specs=(_HBM_SPEC,) * (2 * nw), input_output_aliases={k: k for k in range(2 * nw)},
        compiler_params=pltpu.CompilerParams(has_side_effects=pltpu.SideEffectType.DATAFLOW_SIDE_EFFECTING),
    )(*p_thru, *land_thru, *sems, *after)
    return res[:nw], res[nw:]


def _where_am_i():
    x, y, c = _me()
    return jnp.stack([c, 2 * x + y]).astype(I32)


def _sibling():
    x, y, c = _me()
    return (x, y, 1 - c)


N_SEND_SLOTS = 2


def _matmul_tn_pair(name, pos, a, b, m, n, k, shard_rows, *, tm, tn, tk):
    hr = shard_rows // 2
    tm, tn, tk = min(tm, hr), min(tn, n), min(tk, k)
    tph = hr // tm
    nt, nj, nk = (m // 2) // tm, n // tn, k // tk
    n_tiles = nt * nj

    def row_block(p, t, pos_ref):
        half = jnp.where(p == 0, 1 - pos_ref[0], pos_ref[0])
        return (t // tph) * (2 * tph) + half * tph + t % tph

    pieces = b if isinstance(b, list) else [(b, False)]
    starts = _piece_starts(pieces, tn)
    n_b = len(pieces)

    def kern(pos_ref, a_ref, *rest):
        b_refs = rest[:n_b]
        o_ref, acc_ref, send_buf, land_buf, s_sem, r_sem = rest[n_b:]
        p, t, j, kk = pl.program_id(0), pl.program_id(1), pl.program_id(2), pl.program_id(3)
        idx = t * nj + j
        sib = _sibling()

        def copy(i):
            return pltpu.make_async_remote_copy(
                src_ref=send_buf.at[i % N_SEND_SLOTS], dst_ref=land_buf.at[i], send_sem=s_sem.at[i],
                recv_sem=r_sem.at[i], device_id=sib, device_id_type=MESH)

        @pl.when(kk == 0)
        def _():
            acc_ref[...] = jnp.zeros_like(acc_ref)

        for q in range(n_b):
            @pl.when(jnp.logical_and(j >= starts[q], j < starts[q] + _piece_chunks(pieces[q], tn)))
            def _(q=q):
                acc_ref[...] += lax.dot_general(a_ref[...], b_refs[q][...], _TN, preferred_element_type=F32)

        @pl.when(jnp.logical_and(kk == nk - 1, p == 0))
        def _():
            @pl.when(idx >= N_SEND_SLOTS)
            def _():
                copy(idx - N_SEND_SLOTS).wait_send()

            send_buf[idx % N_SEND_SLOTS] = acc_ref[...].astype(BF16)
            copy(idx).start()

        @pl.when(jnp.logical_and(kk == nk - 1, p == 1))
        def _():
            copy(idx).wait_recv()
            o_ref[...] = (acc_ref[...] + land_buf[idx].astype(F32)).astype(BF16)

        @pl.when(jnp.logical_and(jnp.logical_and(p == 1, idx == n_tiles - 1), kk == nk - 1))
        def _():
            for i in range(max(n_tiles - N_SEND_SLOTS, 0), n_tiles):
                copy(i).wait_send()

    grid_spec = pltpu.PrefetchScalarGridSpec(
        num_scalar_prefetch=1, grid=(2, nt, nj, nk),
        in_specs=[pl.BlockSpec((tk, tm), lambda p, t, j, kk, pos_ref: (kk, row_block(p, t, pos_ref)))]
        + [_piece_spec(pc, tk, tn, st, lambda p, t, j, kk, pos_ref: kk, lambda p, t, j, kk, pos_ref: j)
           for pc, st in zip(pieces, starts)],
        out_specs=pl.BlockSpec((tm, tn), lambda p, t, j, kk, pos_ref: (p * t, p * j)),
        scratch_shapes=[pltpu.VMEM((tm, tn), F32), pltpu.VMEM((N_SEND_SLOTS, tm, tn), BF16),
                        pltpu.VMEM((n_tiles, tm, tn), BF16),
                        pltpu.SemaphoreType.DMA((n_tiles,)), pltpu.SemaphoreType.DMA((n_tiles,))])
    return pl.pallas_call(
        kern, name=name, grid_spec=grid_spec, out_shape=jax.ShapeDtypeStruct((m // 2, n), BF16),
        compiler_params=_cparams(("arbitrary",) * 4),
    )(pos, a, *[pc[0] for pc in pieces])


def _rope_tables():
    half = RET_DK // 2
    f32 = np.float32
    inv = np.power(f32(ROPE_BASE), -np.arange(half, dtype=f32) / f32(half)).astype(f32)
    ang = (np.arange(SEQ, dtype=f32)[:, None] * inv[None, :]).astype(f32)
    return jnp.asarray(np.cos(ang).astype(f32)), jnp.asarray(np.sin(ang).astype(f32))


def _decay_tables():
    c = RET_CHUNK
    f32 = np.float32
    log_g = np.log1p(-np.power(f32(2.0), f32(-5.0) - np.arange(RET_HEADS, dtype=f32))).astype(f32)
    idx = np.arange(c, dtype=f32)
    rel = idx[:, None] - idx[None, :]
    din = np.where(rel >= 0, np.exp(log_g[:, None, None] * np.maximum(rel, f32(0.0))), f32(0.0)).astype(f32)
    qd = np.exp(log_g[:, None] * (idx + f32(1.0))).astype(f32)[:, :, None]
    kd = np.exp(log_g[:, None] * (f32(c) - f32(1.0) - idx)).astype(f32)[:, :, None]
    cd = np.exp(log_g * f32(c)).astype(f32)
    return jnp.asarray(din), jnp.asarray(qd), jnp.asarray(kd), jnp.asarray(cd)


def _t5_bucket(dist):
    max_exact = REL_BUCKETS // 2
    d_f = jnp.maximum(dist, 1).astype(F32)
    large = max_exact + (jnp.log(d_f / max_exact) / math.log(REL_MAX_DIST / max_exact)
                         * (REL_BUCKETS - max_exact)).astype(I32)
    large = jnp.minimum(large, REL_BUCKETS - 1)
    return jnp.where(dist < max_exact, dist, large)


def _bucket_tables():
    qi = jnp.arange(ATT_BLK)[:, None]
    kj = jnp.arange(2 * ATT_BLK)[None, :]
    dist = jnp.clip(ATT_BLK + qi - kj, 0, ATT_BLK)
    return jnp.stack([_t5_bucket(dist * dil) for _, dil in ATT_GROUPS]).astype(I32)


def _retention_fwd(rqk, rv, rg, gn_g, gn_b, din, qd, kd, cd):
    nc = SEQ // RET_CHUNK
    c, dk, dv = RET_CHUNK, RET_DK, RET_DV

    def kern(q_ref, k_ref, v_ref, rg_ref, g_ref, b_ref, din_ref, qd_ref, kd_ref, cd_ref,
             o_ref, st_ref, gated_ref, state):
        n = pl.program_id(0)

        @pl.when(n == 0)
        def _():
            state[...] = jnp.zeros_like(state)

        for sub in range(RET_SUB):
            rows = slice(sub * c, (sub + 1) * c)
            for h in range(RET_HEADS):
                q, k = q_ref[rows, h * dk:(h + 1) * dk], k_ref[rows, h * dk:(h + 1) * dk]
                v = v_ref[rows, h * dv:(h + 1) * dv]
                s_b = state[h].astype(BF16)
                st_ref[h, sub] = s_b
                a = lax.dot_general(q, k, _NT, preferred_element_type=F32) * din_ref[h]
                o = jnp.dot(a.astype(BF16), v, preferred_element_type=F32)
                o += jnp.dot(q, s_b, preferred_element_type=F32) * qd_ref[h]
                v_cols = slice(h * dv, (h + 1) * dv)
                o_ref[rows, v_cols] = o
                nrm, _ = _gn_parts(o)
                gate = rg_ref[rows, v_cols].astype(F32)
                gated_ref[rows, v_cols] = ((gate * _sigmoid(gate))
                                           * (nrm * g_ref[:, v_cols] + b_ref[:, v_cols])).astype(BF16)
                kk = (k.astype(F32) * kd_ref[h]).astype(BF16)
                state[h] = state[h] * cd_ref[h] + lax.dot_general(kk, v, _TN, preferred_element_type=F32)

    whole = lambda a: pl.BlockSpec(a.shape, lambda n: (0,) * a.ndim)
    cs = RET_SUB * c
    rows_v = pl.BlockSpec((cs, RET_V_W), lambda n: (n, 0))
    return pl.pallas_call(
        kern, name="retention_fwd", grid=(nc // RET_SUB,),
        in_specs=[
            pl.BlockSpec((cs, RET_QK_W), lambda n: (n, 0)),
            pl.BlockSpec((cs, RET_QK_W), lambda n: (n, 1)),
            rows_v, rows_v, whole(gn_g), whole(gn_b),
            whole(din), whole(qd), whole(kd),
            pl.BlockSpec(memory_space=pltpu.SMEM),
        ],
        out_specs=[
            rows_v,
            pl.BlockSpec((RET_HEADS, RET_SUB, dk, dv), lambda n: (0, n, 0, 0)),
            rows_v,
        ],
        out_shape=[
            jax.ShapeDtypeStruct((SEQ, RET_V_W), F32),
            jax.ShapeDtypeStruct((RET_HEADS, nc, dk, dv), BF16),
            jax.ShapeDtypeStruct((SEQ, RET_V_W), BF16),
        ],
        scratch_shapes=[pltpu.VMEM((RET_HEADS, dk, dv), F32)],
        compiler_params=_cparams(("arbitrary",)),
    )(rqk, rqk, rv, rg, gn_g, gn_b, din, qd, kd, cd)


def _retention_bwd(rqk, rv, states, d_gated, ro, rg, gn_g, gn_b, din, qd, kd, cd, cos, sin):
    nc = SEQ // RET_CHUNK
    c, dk, dv = RET_CHUNK, RET_DK, RET_DV
    half = dk // 2
    last = nc // RET_SUB - 1

    def unrot(g, cs, sn):
        g1, g2 = g[:, :half], g[:, half:]
        return jnp.concatenate([g1 * cs + g2 * sn, g2 * cs - g1 * sn], axis=-1)

    def kern(q_ref, k_ref, v_ref, st_ref, dg_ref, ro_ref, rg_ref, g_ref, b_ref, din_ref, qd_ref, kd_ref,
             cd_ref, cos_ref, sin_ref, out_ref, drg_ref, dgn_g_ref, dgn_b_ref, dstate):
        step = pl.program_id(0)

        @pl.when(step == 0)
        def _():
            dstate[...] = jnp.zeros_like(dstate)
            dgn_g_ref[...] = jnp.zeros_like(dgn_g_ref)
            dgn_b_ref[...] = jnp.zeros_like(dgn_b_ref)

        for sub in reversed(range(RET_SUB)):
            rows = slice(sub * c, (sub + 1) * c)
            cs, sn = cos_ref[rows, :], sin_ref[rows, :]
            for h in range(RET_HEADS):
                qk_cols, v_cols = slice(h * dk, (h + 1) * dk), slice(h * dv, (h + 1) * dv)
                q, k, v = q_ref[rows, qk_cols], k_ref[rows, qk_cols], v_ref[rows, v_cols]
                s_b = st_ref[h, sub]
                nrm, rstd = _gn_parts(ro_ref[rows, v_cols])
                gate, dg = rg_ref[rows, v_cols].astype(F32), dg_ref[rows, v_cols].astype(F32)
                sg = _sigmoid(gate)
                gn_gain = g_ref[:, v_cols]
                drg_ref[rows, v_cols] = (dg * (nrm * gn_gain + b_ref[:, v_cols])
                                         * (sg * (1.0 + gate * (1.0 - sg)))).astype(BF16)
                d_ron = dg * (gate * sg)
                dgn_g_ref[:, v_cols] += jnp.sum(d_ron * nrm, axis=0, keepdims=True)
                dgn_b_ref[:, v_cols] += jnp.sum(d_ron, axis=0, keepdims=True)
                d_n = d_ron * gn_gain
                d_o = rstd * (d_n - jnp.mean(d_n, axis=-1, keepdims=True)
                              - nrm * jnp.mean(d_n * nrm, axis=-1, keepdims=True))
                d_ob = d_o.astype(BF16)
                d_oq = (d_o * qd_ref[h]).astype(BF16)
                ds_b = dstate[h].astype(BF16)
                din_m = din_ref[h]
                a_b = (lax.dot_general(q, k, _NT, preferred_element_type=F32) * din_m).astype(BF16)
                kk = (k.astype(F32) * kd_ref[h]).astype(BF16)
                d_v = lax.dot_general(a_b, d_ob, _TN, preferred_element_type=F32)
                d_v += jnp.dot(kk, ds_b, preferred_element_type=F32)
                d_a = (lax.dot_general(d_ob, v, _NT, preferred_element_type=F32) * din_m).astype(BF16)
                d_q = jnp.dot(d_a, k, preferred_element_type=F32)
                d_q += lax.dot_general(d_oq, s_b, _NT, preferred_element_type=F32)
                d_k = lax.dot_general(d_a, q, _TN, preferred_element_type=F32)
                d_k += lax.dot_general(v, ds_b, _NT, preferred_element_type=F32) * kd_ref[h]
                dstate[h] = dstate[h] * cd_ref[h] + lax.dot_general(q, d_oq, _TN,
                                                                    preferred_element_type=F32)
                out_ref[rows, h * dk:(h + 1) * dk] = unrot(d_q, cs, sn).astype(BF16)
                out_ref[rows, RET_QK_W + h * dk:RET_QK_W + (h + 1) * dk] = (
                    unrot(d_k, cs, sn) * (RET_DK ** -0.5)).astype(BF16)
                out_ref[rows, 2 * RET_QK_W + h * dv:2 * RET_QK_W + (h + 1) * dv] = d_v.astype(BF16)

    whole = lambda a: pl.BlockSpec(a.shape, lambda n: (0,) * a.ndim)
    rs = RET_SUB * c
    rows_v = pl.BlockSpec((rs, RET_V_W), lambda n: (last - n, 0))
    return pl.pallas_call(
        kern, name="retention_bwd", grid=(nc // RET_SUB,),
        in_specs=[
            pl.BlockSpec((rs, RET_QK_W), lambda n: (last - n, 0)),
            pl.BlockSpec((rs, RET_QK_W), lambda n: (last - n, 1)),
            rows_v,
            pl.BlockSpec((RET_HEADS, RET_SUB, dk, dv), lambda n: (0, last - n, 0, 0)),
            rows_v, rows_v, rows_v, whole(gn_g), whole(gn_b),
            whole(din), whole(qd), whole(kd),
            pl.BlockSpec(memory_space=pltpu.SMEM),
            pl.BlockSpec((rs, half), lambda n: (last - n, 0)),
            pl.BlockSpec((rs, half), lambda n: (last - n, 0)),
        ],
        out_specs=[pl.BlockSpec((rs, 2 * RET_QK_W + RET_V_W), lambda n: (last - n, 0)), rows_v,
                   whole(gn_g), whole(gn_b)],
        out_shape=[jax.ShapeDtypeStruct((SEQ, 2 * RET_QK_W + RET_V_W), BF16),
                   jax.ShapeDtypeStruct((SEQ, RET_V_W), BF16),
                   jax.ShapeDtypeStruct((1, RET_V_W), F32), jax.ShapeDtypeStruct((1, RET_V_W), F32)],
        scratch_shapes=[pltpu.VMEM((RET_HEADS, dk, dv), F32)],
        compiler_params=_cparams(("arbitrary",)),
    )(rqk, rqk, rv, states, d_gated, ro, rg, gn_g, gn_b, din, qd, kd, cd, cos, sin)


def _bias_build(rel_bias, buckets):
    ng = len(ATT_GROUPS)

    def kern(tab_ref, bkt_ref, o_ref):
        g, h = pl.program_id(0), pl.program_id(1)
        bkt = bkt_ref[...]
        acc = jnp.zeros(bkt.shape, F32)
        for b in range(REL_BUCKETS):
            acc = jnp.where(bkt == b, tab_ref[b, g * ATT_HPG + h], acc)
        o_ref[...] = acc

    return pl.pallas_call(
        kern, name="bias_build", grid=(ng, ATT_HPG),
        in_specs=[pl.BlockSpec(memory_space=pltpu.SMEM),
                  pl.BlockSpec((None, ATT_BLK, 2 * ATT_BLK), lambda g, h: (g, 0, 0))],
        out_specs=pl.BlockSpec((None, None, ATT_BLK, 2 * ATT_BLK), lambda g, h: (g, h, 0, 0)),
        out_shape=jax.ShapeDtypeStruct((ng, ATT_HPG, ATT_BLK, 2 * ATT_BLK), F32),
        compiler_params=_cparams(("arbitrary", "arbitrary")),
    )(rel_bias, buckets)


def _bias_grad(dsb, buckets):
    ng = len(ATT_GROUPS)

    def kern(ds_ref, bkt_ref, o_ref):
        g, h = pl.program_id(0), pl.program_id(1)
        bkt, ds = bkt_ref[...], ds_ref[...]
        for b in range(REL_BUCKETS):
            o_ref[b, g * ATT_HPG + h] = jnp.sum(jnp.where(bkt == b, ds, 0.0))

    return pl.pallas_call(
        kern, name="bias_grad", grid=(ng, ATT_HPG),
        in_specs=[pl.BlockSpec((None, None, ATT_BLK, 2 * ATT_BLK), lambda g, h: (g, h, 0, 0)),
                  pl.BlockSpec((None, ATT_BLK, 2 * ATT_BLK), lambda g, h: (g, 0, 0))],
        out_specs=pl.BlockSpec(memory_space=pltpu.SMEM),
        out_shape=jax.ShapeDtypeStruct((REL_BUCKETS, N_ATT_HEADS), F32),
        compiler_params=_cparams(("arbitrary", "arbitrary")),
    )(dsb, buckets)


_NT = (((1,), (1,)), ((), ()))
_TN = (((0,), (0,)), ((), ()))
_ATT_SCALE = ATT_DH ** -0.5


def _window_mask(has_prev):
    qi = lax.broadcasted_iota(I32, (ATT_BLK, 2 * ATT_BLK), 0)
    kj = lax.broadcasted_iota(I32, (ATT_BLK, 2 * ATT_BLK), 1)
    prev_ok = jnp.logical_and(jnp.logical_and(kj < ATT_BLK, kj >= qi), has_prev)
    return jnp.logical_or(prev_ok, jnp.logical_and(kj >= ATT_BLK, qi >= kj - ATT_BLK))


def _head_specs(col0):
    return pl.BlockSpec((SEQ, ATT_DH), lambda h: (0, col0 + h))


def _sub_rows(start, size, dil):
    return pl.ds(start, size) if dil == 1 else pl.ds(start, size, stride=dil)


def _att_blocks(dil):
    nb = SEQ // dil // ATT_BLK
    return [(r + dil * n * ATT_BLK, n > 0, n + 1 < nb) for r in range(dil) for n in range(nb)]


def _att_fwd(gi, dil, qkv, bias):
    blk, dh = ATT_BLK, ATT_DH
    pad = dil * blk
    col0 = 3 * ATT_HPG * gi

    def kern(q_ref, k_ref, v_ref, b_ref, o_ref, l_ref, qf, kpad, vpad):
        zero = jnp.zeros((pad, dh), F32)
        kpad[0:pad, :] = zero
        vpad[0:pad, :] = zero
        kpad[pad:, :] = k_ref[...].astype(F32)
        vpad[pad:, :] = v_ref[...].astype(F32)
        qf[...] = q_ref[...].astype(F32)
        bias_m = b_ref[...]
        for start, has_prev, _ in _att_blocks(dil):
            rows, window = _sub_rows(start, blk, dil), _sub_rows(start, 2 * blk, dil)
            q = qf[rows, :].astype(BF16)
            kw, vw = kpad[window, :].astype(BF16), vpad[window, :].astype(BF16)
            valid = _window_mask(has_prev)
            s = lax.dot_general(q, kw, _NT, preferred_element_type=F32) * _ATT_SCALE + bias_m
            s = jnp.where(valid, s, -1e30)
            mx = jnp.max(s, axis=-1, keepdims=True)
            e = jnp.exp(s - mx)
            den = jnp.sum(e, axis=-1, keepdims=True)
            o_ref[rows, :] = jnp.dot((e / den).astype(BF16), vw, preferred_element_type=F32)
            l_ref[rows, :] = jnp.broadcast_to(mx + jnp.log(den), (blk, dh))

    return pl.pallas_call(
        kern, name=f"att_fwd_g{gi}", grid=(ATT_HPG,),
        in_specs=[_head_specs(col0), _head_specs(col0 + ATT_HPG), _head_specs(col0 + 2 * ATT_HPG),
                  pl.BlockSpec((None, None, blk, 2 * blk), lambda h: (gi, h, 0, 0))],
        out_specs=[_head_specs(0), _head_specs(0)],
        out_shape=[jax.ShapeDtypeStruct((SEQ, ATT_W), F32), jax.ShapeDtypeStruct((SEQ, ATT_W), F32)],
        scratch_shapes=[pltpu.VMEM((SEQ, dh), F32), pltpu.VMEM((SEQ + pad, dh), F32),
                        pltpu.VMEM((SEQ + pad, dh), F32)],
        compiler_params=_cparams(("arbitrary",)),
    )(qkv, qkv, qkv, bias)


def _att_bwd(gi, dil, qkv, d_att, lse, dd, bias):
    blk, dh = ATT_BLK, ATT_DH
    pad = dil * blk
    col0 = 3 * ATT_HPG * gi

    def kern(q_ref, k_ref, v_ref, do_ref, l_ref, d_ref, b_ref, dqkv_ref, dsb_ref,
             qf, kpad, vpad, dq_s, dkpad, dvpad):
        zero = jnp.zeros((pad, dh), F32)
        kpad[0:pad, :] = zero
        vpad[0:pad, :] = zero
        kpad[pad:, :] = k_ref[...].astype(F32)
        vpad[pad:, :] = v_ref[...].astype(F32)
        qf[...] = q_ref[...].astype(F32)
        dkpad[...] = jnp.zeros_like(dkpad)
        dvpad[...] = jnp.zeros_like(dvpad)
        bias_m = b_ref[...]
        ds_sum = jnp.zeros((blk, 2 * blk), F32)

        for start, has_prev, _ in _att_blocks(dil):
            rows, window = _sub_rows(start, blk, dil), _sub_rows(start, 2 * blk, dil)
            q, d_o = qf[rows, :].astype(BF16), do_ref[rows, :].astype(BF16)
            kw, vw = kpad[window, :].astype(BF16), vpad[window, :].astype(BF16)
            lrow, drow = l_ref[rows, :][:, :1], d_ref[rows, :][:, :1]
            valid = _window_mask(has_prev)
            s = lax.dot_general(q, kw, _NT, preferred_element_type=F32) * _ATT_SCALE + bias_m
            p = jnp.where(valid, jnp.exp(jnp.where(valid, s, -1e30) - lrow), 0.0)
            dp = lax.dot_general(d_o, vw, _NT, preferred_element_type=F32)
            ds = p * (dp - drow)
            ds_b = ds.astype(BF16)
            dq_s[rows, :] = jnp.dot(ds_b, kw, preferred_element_type=F32) * _ATT_SCALE
            dkpad[window, :] += lax.dot_general(ds_b, q, _TN, preferred_element_type=F32) * _ATT_SCALE
            dvpad[window, :] += lax.dot_general(p.astype(BF16), d_o, _TN, preferred_element_type=F32)
            ds_sum = ds_sum + ds
        dsb_ref[...] = ds_sum

        dqkv_ref[0] = dq_s[...].astype(BF16)
        dqkv_ref[1] = dkpad[pad:, :].astype(BF16)
        dqkv_ref[2] = dvpad[pad:, :].astype(BF16)

    return pl.pallas_call(
        kern, name=f"att_bwd_g{gi}", grid=(ATT_HPG,),
        in_specs=[_head_specs(col0), _head_specs(col0 + ATT_HPG), _head_specs(col0 + 2 * ATT_HPG),
                  _head_specs(0), _head_specs(0), _head_specs(0),
                  pl.BlockSpec((None, None, blk, 2 * blk), lambda h: (gi, h, 0, 0))],
        out_specs=[pl.BlockSpec((3, SEQ, dh), lambda h: (0, 0, h)),
                   pl.BlockSpec((None, blk, 2 * blk), lambda h: (h, 0, 0))],
        out_shape=[jax.ShapeDtypeStruct((3, SEQ, ATT_W), BF16),
                   jax.ShapeDtypeStruct((ATT_HPG, blk, 2 * blk), F32)],
        scratch_shapes=[pltpu.VMEM((SEQ, dh), F32), pltpu.VMEM((SEQ + pad, dh), F32),
                        pltpu.VMEM((SEQ + pad, dh), F32), pltpu.VMEM((SEQ, dh), F32),
                        pltpu.VMEM((SEQ + pad, dh), F32), pltpu.VMEM((SEQ + pad, dh), F32)],
        compiler_params=_cparams(("arbitrary",)),
    )(qkv, qkv, qkv, d_att, lse, dd, bias)


def _rms_parts(x):
    r = lax.rsqrt(jnp.mean(x * x, axis=-1, keepdims=True) + RMS_EPS)
    return x * r, r


def _rms_bwd(d_xhat, xhat, r):
    return r * (d_xhat - xhat * jnp.mean(d_xhat * xhat, axis=-1, keepdims=True))


def _prenorm_fwd(name, x, gain, shift, scale):
    def body(xt, g, sh, sc):
        xhat, _ = _rms_parts(xt)
        return (xhat * g) * (1.0 + sc) + sh
    return _rowmap(name, body, [x], [gain, shift, scale], [(D_MODEL, BF16)])[0]


def _prenorm_bwd_epi(d_h, x, resid, gain, scale, branch=None, gate=None):
    xhat, r = _rms_parts(x)
    nrm = xhat * gain
    d_n = d_h * (1.0 + scale)
    dx = _rms_bwd(d_n * gain, xhat, r) + resid
    sums = (jnp.sum(d_h, axis=0, keepdims=True), jnp.sum(d_h * nrm, axis=0, keepdims=True),
            jnp.sum(d_n * xhat, axis=0, keepdims=True))
    if branch is None:
        return (dx,) + sums
    return (dx, dx * gate) + sums + (jnp.sum(dx * branch, axis=0, keepdims=True),)


def _row_operands(tm, rows, vecs):
    return ([(a, (tm, D_MODEL), lambda i, j, kk: (i, 0)) for a in rows]
            + [(v, (1, D_MODEL), lambda i, j, kk: (0, 0)) for v in vecs])


def _gn_parts(ro):
    mu = jnp.mean(ro, axis=-1, keepdims=True)
    cen = ro - mu
    rstd = lax.rsqrt(jnp.mean(cen * cen, axis=-1, keepdims=True) + GN_EPS)
    return cen * rstd, rstd


def _combine(os_, ls_, after=()):
    def body(o0, o1, o2, l0, l1, l2):
        mx = jnp.maximum(jnp.maximum(l0, l1), l2)
        e0, e1, e2 = jnp.exp(l0 - mx), jnp.exp(l1 - mx), jnp.exp(l2 - mx)
        den = e0 + e1 + e2
        att = (e0 / den) * o0 + (e1 / den) * o1 + (e2 / den) * o2
        return att, att, mx + jnp.log(den)
    return _rowmap("att_combine", body, list(os_) + list(ls_), [],
                   [(ATT_W, F32), (ATT_W, BF16), (ATT_W, F32)], after=after)


MERGE_TM = 512


def _merge_operands(gates, ret_out, att_out=None):
    ops = [(gates, (MERGE_TM, D_MODEL), lambda i, j, kk: (i, 0)),
           (gates, (MERGE_TM, D_MODEL), lambda i, j, kk: (i, 1)),
           (ret_out, (MERGE_TM, D_MODEL), lambda i, j, kk: (i, 0))]
    if att_out is not None:
        ops.append((att_out, (MERGE_TM, D_MODEL), lambda i, j, kk: (i, 0)))
    return ops


def _merge_fwd_epi(att_out, ga, gb, ret_out):
    return att_out, _sigmoid(ga.astype(F32)) * ret_out + _sigmoid(gb.astype(F32)) * att_out


def _merge_bwd_epi(d_merged, ga, gb, ret_out, att_out):
    sa, sb = _sigmoid(ga.astype(F32)), _sigmoid(gb.astype(F32))
    return (d_merged * sa, d_merged * sb, d_merged * ret_out * (sa * (1.0 - sa)),
            d_merged * att_out * (sb * (1.0 - sb)))


def _att_out_bwd_epi(d_att, att):
    outs = []
    for h in range(ATT_HPG):
        sl = slice(h * ATT_DH, (h + 1) * ATT_DH)
        outs.append(jnp.broadcast_to(jnp.sum(d_att[:, sl] * att[:, sl], axis=-1, keepdims=True),
                                     (d_att.shape[0], ATT_DH)))
    return d_att, jnp.concatenate(outs, axis=-1)


def _loss_head(x3, target, gain, branch, gate):
    def body(xt, tt, br, g, gt):
        xhat, r = _rms_parts(xt)
        err = xhat * g - tt
        d_y = err / D_MODEL
        loss = 0.5 * jnp.sum(jnp.mean(err * err, axis=-1, keepdims=True), axis=0, keepdims=True)
        d_x = _rms_bwd(d_y * g, xhat, r)
        return (d_x, d_x * gt, jnp.broadcast_to(loss, (1, 128)), jnp.sum(d_y * xhat, axis=0, keepdims=True),
                jnp.sum(d_x * br, axis=0, keepdims=True))
    return _rowmap("loss_head", body, [x3, target, branch], [gain, gate],
                   [(D_MODEL, F32), (D_MODEL, BF16)], [128, D_MODEL, D_MODEL])


def _local_step(pos, x, target, mod, norm1_g, norm2_g, norm_f_g, rel_bias, gn_g, gn_b, w_in, rest_gather):
    sh1, sc1, g1, sh2, sc2, g2 = [mod[:, i * D_MODEL:(i + 1) * D_MODEL] for i in range(6)]
    cos, sin = _rope_tables()
    din, qd, kd, cd = _decay_tables()
    buckets = _bucket_tables()
    bias = _bias_build(rel_bias, buckets)
    dils = [d for _, d in ATT_GROUPS]

    h1 = _prenorm_fwd("prenorm1_fwd", x, norm1_g, sh1, sc1)

    qk_tn = 2 * RET_DK

    def rot_epi(acc, cs, sn, scale):
        half = RET_DK // 2
        outs = []
        for h0 in range(0, qk_tn, RET_DK):
            x1, x2 = acc[:, h0:h0 + half], acc[:, h0 + half:h0 + RET_DK]
            outs += [x1 * cs - x2 * sn, x1 * sn + x2 * cs]
        return (jnp.concatenate(outs, axis=-1) * scale,)

    qk_scale = jnp.concatenate([jnp.ones((1, RET_QK_W), F32),
                                jnp.full((1, RET_QK_W), RET_DK ** -0.5, F32)], axis=-1)
    rope_ex = [(cos, (TM, RET_DK // 2), lambda i, j, kk: (i, 0)),
               (sin, (TM, RET_DK // 2), lambda i, j, kk: (i, 0)),
               (qk_scale, (1, qk_tn), lambda i, j, kk: (0, j))]
    rest_sems, rest_shards, rest_fulls, rest_token = rest_gather
    behind = [rest_token]
    rv = _matmul("proj_rv", h1, w_in, "nn", SEQ, RET_V_W, D_MODEL, [BF16], b_off=OFF_V, tk=D_MODEL,
                 after=behind)[0]
    rg = _matmul("proj_rg", h1, w_in, "nn", SEQ, RET_V_W, D_MODEL, [BF16], b_off=OFF_G, tk=D_MODEL,
                 after=behind)[0]
    gates = _matmul("proj_gates", h1, w_in, "nn", SEQ, 2 * D_MODEL, D_MODEL, [BF16], b_off=OFF_GATE,
                    tn=512, tk=D_MODEL, after=behind)[0]
    aqkv = _matmul("proj_att", h1, w_in, "nn", SEQ, 9 * ATT_W, D_MODEL, [BF16], b_off=OFF_ATT,
                   tn=512, tk=D_MODEL, after=behind)[0]

    os_, ls_ = [], []
    for gi in range(3):
        o_g, l_g = _att_fwd(gi, dils[gi], aqkv, bias)
        os_.append(o_g)
        ls_.append(l_g)

    rqk = _matmul("proj_qk", h1, w_in, "nn", SEQ, 2 * RET_QK_W, D_MODEL, [BF16], b_off=OFF_Q,
                  tn=qk_tn, tk=D_MODEL, epilogue=rot_epi, extras=rope_ex, after=behind)[0]
    ro, states, gated = _retention_fwd(rqk, rv, rg, gn_g, gn_b, din, qd, kd, cd)
    rest_sems, rest_fulls, fwd_token = _gather_rest_forward(rest_sems, rest_shards, rest_fulls,
                                                            [gated, gates] + os_)
    att, att_b, lse = _combine(os_, ls_, after=[fwd_token])
    w_ret_out, w_att_out, w_o, w_ff1, w_ff2 = _gather_rest_end(rest_sems, rest_fulls, [att_b])
    ret_out = _matmul("ret_out", gated, w_ret_out, "nn", SEQ, D_MODEL, RET_V_W, [F32], tk=RET_V_W)[0]
    att_out, merged = _matmul("att_out", att_b, w_att_out, "nn", SEQ, D_MODEL, ATT_W, [F32, BF16],
                              tm=MERGE_TM, epilogue=_merge_fwd_epi,
                              extras=_merge_operands(gates, ret_out))

    def resid_epi(acc, xt, g):
        return xt + g * acc, acc

    def resid_ex(xin, g):
        return [(xin, (TM, TN), lambda i, j, kk: (i, j)), (g, (1, TN), lambda i, j, kk: (0, j))]

    def mix_epi(acc, xt, g, gain, sh, sc):
        x_new = xt + g * acc
        xhat, _ = _rms_parts(x_new)
        return x_new, acc, (xhat * gain) * (1.0 + sc) + sh

    x2, mix, h2 = _matmul("mix_out", merged, w_o, "nn", SEQ, D_MODEL, D_MODEL, [F32, BF16, BF16],
                          epilogue=mix_epi, extras=_row_operands(TM, [x], [g1, norm2_g, sh2, sc2]))

    def relu2_epi(acc):
        r = jnp.maximum(acc, 0.0)
        return r * r, r

    act, relu_u = _matmul("ff1", h2, w_ff1, "nn", SEQ, D_FF, D_MODEL, [BF16, BF16], tk=D_MODEL,
                          epilogue=relu2_epi)
    x3, y2 = _matmul("ff2", act, w_ff2, "nn", SEQ, D_MODEL, D_FF, [F32, BF16], tk=2048,
                     epilogue=resid_epi, extras=resid_ex(x2, g2))

    d_x3, d_y2, loss, d_gf, d_g2 = _loss_head(x3, target, norm_f_g, y2, g2)

    def relu2_bwd_epi(acc, rt):
        return (acc * (2.0 * rt.astype(F32)),)

    gw_ff2 = _matmul_tn_pair("ff2_dw", pos, act, d_y2, D_FF, D_MODEL, SEQ, D_FF // N_CHIPS,
                             tm=512, tn=1024, tk=SEQ)
    d_u = _matmul("ff2_dx", d_y2, w_ff2, "nt", SEQ, D_FF, D_MODEL, [BF16], epilogue=relu2_bwd_epi,
                  extras=[(relu_u, (TM, TN), lambda i, j, kk: (i, j))])[0]
    gw_ff1 = _matmul_tn_pair("ff1_dw", pos, h2, d_u, D_MODEL, D_FF, SEQ, D_MODEL,
                             tm=512, tn=1024, tk=SEQ)
    ffn = ["w_ff2", "w_ff1"]
    ffn_started = _ici_start("ici_start_ffn", ffn, [gw_ff2, gw_ff1])
    d_x2, d_mix, d_sh2, d_sc2, d_n2g, d_g1 = _matmul(
        "ff1_dx", d_u, w_ff1, "nt", SEQ, D_MODEL, D_FF, [F32, BF16], tm=MERGE_TM, tk=2048, n_sums=4,
        epilogue=_prenorm_bwd_epi, extras=_row_operands(MERGE_TM, [x2, d_x3], [norm2_g, sc2])
        + _row_operands(MERGE_TM, [mix], [g1]), after=[ffn_started[3]])
    gw_o = _matmul_tn_pair("mix_dw", pos, merged, d_mix, D_MODEL, D_MODEL, SEQ, D_MODEL // N_CHIPS,
                           tm=128, tn=1024, tk=2048)
    d_ret_out, d_att_out, d_ga, d_gb = _matmul(
        "mix_dx", d_mix, w_o, "nt", SEQ, D_MODEL, D_MODEL, [BF16] * 4, tm=MERGE_TM,
        epilogue=_merge_bwd_epi, extras=_merge_operands(gates, ret_out, att_out))

    gw_ret_out = _matmul_tn_pair("ret_out_dw", pos, gated, d_ret_out, RET_V_W, D_MODEL, SEQ,
                                 RET_V_W // N_CHIPS, tm=256, tn=1024, tk=SEQ)
    gw_att_out = _matmul_tn_pair("att_out_dw", pos, att_b, d_att_out, ATT_W, D_MODEL, SEQ, ATT_W,
                                 tm=256, tn=1024, tk=2048)
    mixer = ["w_o", "w_ret_out", "w_att_out"]
    mixer_started = _ici_start("ici_start_mixer", mixer, [gw_o, gw_ret_out, gw_att_out])
    d_gated = _matmul("ret_out_dx", d_ret_out, w_ret_out, "nt", SEQ, RET_V_W, D_MODEL, [BF16],
                      after=[mixer_started[3]])[0]
    d_att, dd = _matmul("att_out_dx", d_att_out, w_att_out, "nt", SEQ, ATT_W, D_MODEL, [F32, F32],
                        epilogue=_att_out_bwd_epi,
                        extras=[(att, (TM, ATT_W), lambda i, j, kk: (i, 0))], after=[mixer_started[3]])

    d_rqkv, d_rg, d_gn_g, d_gn_b = _retention_bwd(rqk, rv, states, d_gated, ro, rg, gn_g, gn_b,
                                                  din, qd, kd, cd, cos, sin)

    d_aqkv, dsbs = [], []
    for gi in range(3):
        dqkv, dsb = _att_bwd(gi, dils[gi], aqkv, d_att, lse, dd, bias)
        d_aqkv.append(dqkv)
        dsbs.append(dsb)
    d_rel_bias = _bias_grad(jnp.stack(dsbs), buckets)

    d_proj = ([(d_rqkv, False), (d_rg, False)] + [(t, True) for t in d_aqkv]
              + [(d_ga, False), (d_gb, False)])
    gw_in = _matmul_tn_pair("proj_dw", pos, h1, d_proj, D_MODEL, IN_COLS, SEQ, D_MODEL,
                            tm=512, tn=ATT_W, tk=SEQ)
    sems, (gw_in,), (land,), token = _ici_start("ici_start_w_in", ["w_in"], [gw_in])
    grad_x, d_sh1, d_sc1, d_n1g = _matmul(
        "proj_dx", d_proj, w_in, "nt", SEQ, D_MODEL, IN_COLS, [F32], tn=1024, tk=ATT_W, n_sums=3,
        epilogue=_prenorm_bwd_epi, extras=_row_operands(TM, [x, d_x2], [norm1_g, sc1]), after=[token])
    pending = (sems, land)

    names = ffn + mixer
    psums, got = _ici_wait("ici_wait_rest", names, list(ffn_started[0]) + list(mixer_started[0]),
                           list(ffn_started[1]) + list(mixer_started[1]),
                           list(ffn_started[2]) + list(mixer_started[2]), [grad_x])
    g_big = {n: _final_sum("final_" + n, pos, dict(BIG)[n], psums[i], got[i], SHARD[n])
             for i, n in enumerate(names)}
    d_mod = jnp.concatenate([d_sh1, d_sc1, d_g1, d_sh2, d_sc2, d_g2], axis=-1)
    small = dict(norm1_g=d_n1g, norm2_g=d_n2g, norm_f_g=d_gf, gn_g=d_gn_g, gn_b=d_gn_b,
                 rel_bias=d_rel_bias)
    return loss, grad_x, d_mod, small, g_big, (gw_in,) + pending


def _me():
    return lax.axis_index("x"), lax.axis_index("y"), lax.axis_index("c")


def _peer(x, y, c, mask):
    return (x ^ ((mask >> 2) & 1), y ^ ((mask >> 1) & 1), c ^ (mask & 1))


def _gather8(src_ref, dst_ref, send_sems, recv_sems):
    x, y, c = _me()
    me = 4 * x + 2 * y + c
    copies = []
    for mask in range(1, N_DEV):
        cp = pltpu.make_async_remote_copy(
            src_ref=src_ref, dst_ref=dst_ref.at[me], send_sem=send_sems.at[mask - 1],
            recv_sem=recv_sems.at[mask - 1], device_id=_peer(x, y, c, mask), device_id_type=MESH)
        cp.start()
        copies.append(cp)
    dst_ref[me] = src_ref[...]
    for cp in copies:
        cp.wait_recv()
    for cp in copies:
        cp.wait_send()


def _ada_fwd(c_in, w_ada, b_ada):
    ncol = ADA_COLS // N_CHIPS

    def body(c_ref, w_ref, b_ref, mod_ref, sc_ref, cbuf, cg, mbuf, mg, s1, r1, s2, r2):
        x, y, c = _me()
        me = 4 * x + 2 * y + c
        cv = c_ref[...]
        cbuf[...] = jnp.broadcast_to(cv * _sigmoid(cv), cbuf.shape)
        _gather8(cbuf, cg, s1, r1)
        rows = lax.broadcasted_iota(I32, (N_DEV, D_MODEL), 0)
        sc_all = jnp.zeros((N_DEV, D_MODEL), F32)
        for d in range(N_DEV):
            sc_all = jnp.where(rows == d, cg[d], sc_all)
        sc_ref[...] = sc_all
        mbuf[...] = jnp.dot(sc_all.astype(BF16), w_ref[...].astype(BF16), preferred_element_type=F32)
        _gather8(mbuf, mg, s2, r2)
        rowsel = lax.broadcasted_iota(I32, (N_DEV, ncol), 0) == me
        for k in range(N_CHIPS):
            blk = mg[2 * k]
            row = jnp.sum(jnp.where(rowsel, blk, 0.0), axis=0, keepdims=True)
            mod_ref[:, k * ncol:(k + 1) * ncol] = row + b_ref[:, k * ncol:(k + 1) * ncol]

    vm = pl.BlockSpec(memory_space=pltpu.VMEM)
    return pl.pallas_call(
        body, name="ada_fwd",
        in_specs=[vm, vm, vm], out_specs=[vm, vm],
        out_shape=[jax.ShapeDtypeStruct((1, ADA_COLS), F32), jax.ShapeDtypeStruct((N_DEV, D_MODEL), F32)],
        scratch_shapes=[
            pltpu.VMEM((8, D_MODEL), F32), pltpu.VMEM((N_DEV, 8, D_MODEL), F32),
            pltpu.VMEM((8, ncol), F32), pltpu.VMEM((N_DEV, 8, ncol), F32),
            pltpu.SemaphoreType.DMA((N_DEV - 1,)), pltpu.SemaphoreType.DMA((N_DEV - 1,)),
            pltpu.SemaphoreType.DMA((N_DEV - 1,)), pltpu.SemaphoreType.DMA((N_DEV - 1,)),
        ],
        compiler_params=pltpu.CompilerParams(vmem_limit_bytes=VMEM_LIMIT_V7X),
    )(c_in, w_ada, b_ada)


def _small_reduce(pack, sc_all, after=()):
    ncol = ADA_COLS // N_CHIPS

    def body(p_ref, sc_ref, *rest):
        tot_ref, gw_ref, pg, s1, r1 = rest[len(after):]
        x, y, _ = _me()
        chip = 2 * x + y
        _gather8(p_ref, pg, s1, r1)
        tot = pg[0]
        for d in range(1, N_DEV):
            tot = tot + pg[d]
        tot_ref[...] = tot
        rows = lax.broadcasted_iota(I32, (N_DEV, ncol), 0)
        dmod = jnp.zeros((N_DEV, ncol), F32)
        for k in range(N_CHIPS):
            part = jnp.zeros((N_DEV, ncol), F32)
            for d in range(N_DEV):
                part = jnp.where(rows == d, pg[d, :, k * ncol:(k + 1) * ncol][0:1, :], part)
            dmod = jnp.where(chip == k, part, dmod)
        gw_ref[...] = lax.dot_general(sc_ref[...].astype(BF16), dmod.astype(BF16), _TN,
                                      preferred_element_type=F32)

    vm = pl.BlockSpec(memory_space=pltpu.VMEM)
    return pl.pallas_call(
        body, name="small_reduce",
        in_specs=[vm, vm] + [pl.BlockSpec(memory_space=pl.ANY)] * len(after), out_specs=[vm, vm],
        out_shape=[jax.ShapeDtypeStruct((8, ADA_COLS), F32), jax.ShapeDtypeStruct((D_MODEL, ncol), F32)],
        scratch_shapes=[pltpu.VMEM((N_DEV, 8, ADA_COLS), F32),
                        pltpu.SemaphoreType.DMA((N_DEV - 1,)), pltpu.SemaphoreType.DMA((N_DEV - 1,))],
        compiler_params=pltpu.CompilerParams(vmem_limit_bytes=VMEM_LIMIT_V7X),
    )(pack, sc_all, *after)


BIG = (("w_in", 1), ("w_ret_out", 0), ("w_att_out", 1), ("w_o", 0), ("w_ff1", 1), ("w_ff2", 0))
SHARD = {"w_in": (D_MODEL, IN_COLS // N_CHIPS), "w_ret_out": (RET_V_W // N_CHIPS, D_MODEL),
         "w_att_out": (ATT_W, D_MODEL // N_CHIPS), "w_o": (D_MODEL // N_CHIPS, D_MODEL),
         "w_ff1": (D_MODEL, D_FF // N_CHIPS), "w_ff2": (D_FF // N_CHIPS, D_MODEL)}
_CHIP_FLIPS = ((1, 0), (0, 1), (1, 1))


def _region(ref, axis, chip, half, shard_shape):
    r, cw = shard_shape
    hr = r // 2
    if axis == 1:
        return ref.at[pl.ds(half * hr, hr), pl.ds(chip * cw, cw)]
    return ref.at[pl.ds(chip * r + half * hr, hr), :]


def _gather_weights(shards, n_remote):
    nw = len(BIG)
    shapes = [s.shape for s in shards]
    full_shapes = [(r, N_CHIPS * cw) if ax == 1 else (N_CHIPS * r, cw)
                   for (r, cw), (_, ax) in zip(shapes, BIG)]

    def body(*refs):
        ins, outs = refs[:nw], refs[nw:2 * nw]
        own = refs[2 * nw:3 * nw]
        from_ici, from_sib = refs[3 * nw:3 * nw + n_remote], refs[3 * nw + n_remote:3 * nw + 2 * n_remote]
        ld_sem, st_sem, s_ici, r_ici, s_d2d, r_d2d, st_a, st_b = refs[3 * nw + 2 * n_remote:]
        x, y, c = _me()
        chip = 2 * x + y
        sib = (x, y, 1 - c)
        loads = [pltpu.make_async_copy(ins[i], own[i], ld_sem.at[i]) for i in range(nw)]
        for cp in loads:
            cp.start()
        pending, first = [], []
        for i, (_, ax) in enumerate(BIG):
            r, cw = shapes[i]
            hr = r // 2
            loads[i].wait()
            dst = outs[i].at[:, pl.ds(chip * cw, cw)] if ax == 1 else outs[i].at[pl.ds(chip * r, r), :]
            cp = pltpu.make_async_copy(own[i], dst, st_sem.at[i])
            cp.start()
            pending.append(cp)
            for j, (fx, fy) in enumerate(_CHIP_FLIPS if i < n_remote else ()):
                rc = pltpu.make_async_remote_copy(
                    src_ref=own[i].at[pl.ds(c * hr, hr), :], dst_ref=from_ici[i].at[j],
                    send_sem=s_ici.at[j * nw + i], recv_sem=r_ici.at[j * nw + i],
                    device_id=(x ^ fx, y ^ fy, c), device_id_type=MESH)
                rc.start()
                first.append((j, i, rc))
        passed = []
        for j, i, rc in first:
            fx, fy = _CHIP_FLIPS[j]
            src_chip = 2 * (x ^ fx) + (y ^ fy)
            ax = BIG[i][1]
            rc.wait_recv()
            fw = pltpu.make_async_remote_copy(
                src_ref=from_ici[i].at[j], dst_ref=from_sib[i].at[j], send_sem=s_d2d.at[j * nw + i],
                recv_sem=r_d2d.at[j * nw + i], device_id=sib, device_id_type=MESH)
            fw.start()
            passed.append((j, i, src_chip, fw))
            st = pltpu.make_async_copy(from_ici[i].at[j], _region(outs[i], ax, src_chip, c, shapes[i]),
                                       st_a.at[j * nw + i])
            st.start()
            pending.append(st)
        for j, i, src_chip, fw in passed:
            fw.wait_recv()
            st = pltpu.make_async_copy(from_sib[i].at[j],
                                       _region(outs[i], BIG[i][1], src_chip, 1 - c, shapes[i]),
                                       st_b.at[j * nw + i])
            st.start()
            pending.append(st)
        for _, _, rc in first:
            rc.wait_send()
        for _, _, _, fw in passed:
            fw.wait_send()
        for cp in pending:
            cp.wait()

    hbm = pl.BlockSpec(memory_space=pl.ANY)
    halves = [pltpu.VMEM((3, r // 2, cw), BF16) for r, cw in shapes[:n_remote]]
    return pl.pallas_call(
        body, name="gather_weights",
        in_specs=[hbm] * nw, out_specs=[hbm] * nw,
        out_shape=[jax.ShapeDtypeStruct(fs, BF16) for fs in full_shapes],
        scratch_shapes=[pltpu.VMEM(sh, BF16) for sh in shapes] + halves + halves
        + [pltpu.SemaphoreType.DMA((nw,)), pltpu.SemaphoreType.DMA((nw,))]
        + [pltpu.SemaphoreType.DMA((3 * nw,))] * 6,
        compiler_params=pltpu.CompilerParams(vmem_limit_bytes=VMEM_LIMIT_V7X),
    )(*shards)


REST = BIG[1:]
_SIDE_EFFECTS = pltpu.CompilerParams(has_side_effects=pltpu.SideEffectType.DATAFLOW_SIDE_EFFECTING)
_ANY_SPEC = pl.BlockSpec(memory_space=pl.ANY)


def _rest_ici_copies(shard_refs, full_refs, sems):
    x, y, c = _me()
    chip = 2 * x + y
    n = 3 * len(REST)
    copies = []
    for i, (name, ax) in enumerate(REST):
        hr = SHARD[name][0] // 2
        for j, (fx, fy) in enumerate(_CHIP_FLIPS):
            copies.append(pltpu.make_async_remote_copy(
                src_ref=shard_refs[i].at[pl.ds(c * hr, hr), :],
                dst_ref=_region(full_refs[i], ax, chip, c, SHARD[name]),
                send_sem=sems[3 * i + j], recv_sem=sems[n + 3 * i + j],
                device_id=(x ^ fx, y ^ fy, c), device_id_type=MESH))
    return copies


def _rest_d2d_copies(full_refs, sems):
    x, y, c = _me()
    n = 3 * len(REST)
    copies = []
    for i, (name, ax) in enumerate(REST):
        for j, (fx, fy) in enumerate(_CHIP_FLIPS):
            reg = _region(full_refs[i], ax, 2 * (x ^ fx) + (y ^ fy), c, SHARD[name])
            copies.append(pltpu.make_async_remote_copy(
                src_ref=reg, dst_ref=reg, send_sem=sems[3 * i + j], recv_sem=sems[n + 3 * i + j],
                device_id=(x, y, 1 - c), device_id_type=MESH))
    return copies


def _gather_rest_start(shards, fulls, after):
    nr, ns, na = len(REST), 6 * len(REST), len(after)

    def body(*refs):
        for cp in _rest_ici_copies(refs[:nr], refs[nr:2 * nr], refs[2 * nr + na:2 * nr + na + ns]):
            cp.start()
        token = refs[-1]
        token[...] = jnp.zeros_like(token)

    hbm = lambda a: pltpu.HBM(a.shape, a.dtype)
    res = pl.pallas_call(
        body, name="gather_rest_start",
        out_shape=(pltpu.SemaphoreType.DMA(()),) * ns + tuple(hbm(a) for a in shards + fulls)
        + (jax.ShapeDtypeStruct((8, 128), F32),),
        in_specs=(_HBM_SPEC,) * (2 * nr) + (_ANY_SPEC,) * na,
        out_specs=(_SEM_SPEC,) * ns + (_HBM_SPEC,) * (2 * nr) + (pl.BlockSpec(memory_space=pltpu.VMEM),),
        input_output_aliases={k: ns + k for k in range(2 * nr)}, compiler_params=_SIDE_EFFECTS,
    )(*[pltpu.with_memory_space_constraint(a, pltpu.HBM) for a in shards + fulls], *after)
    return res[:ns], res[ns:ns + nr], res[ns + nr:ns + 2 * nr], res[-1]


def _gather_rest_forward(sems, shards, fulls, after):
    nr, ns = len(REST), 6 * len(REST)

    def body(*refs):
        shard_refs, full_refs, old = refs[:nr], refs[nr:2 * nr], refs[2 * nr:2 * nr + ns]
        new = refs[2 * nr + ns + len(after):2 * nr + 2 * ns + len(after)]
        for cp in _rest_ici_copies(shard_refs, full_refs, old):
            cp.wait_send()
            cp.wait_recv()
        for cp in _rest_d2d_copies(full_refs, new):
            cp.start()
        token = refs[-1]
        token[...] = jnp.zeros_like(token)

    res = pl.pallas_call(
        body, name="gather_rest_forward",
        out_shape=(pltpu.SemaphoreType.DMA(()),) * ns + tuple(pltpu.HBM(a.shape, a.dtype) for a in fulls)
        + (jax.ShapeDtypeStruct((8, 128), F32),),
        in_specs=(_HBM_SPEC,) * (2 * nr) + (_SEM_SPEC,) * ns + (_ANY_SPEC,) * len(after),
        out_specs=(_SEM_SPEC,) * ns + (_HBM_SPEC,) * nr + (pl.BlockSpec(memory_space=pltpu.VMEM),),
        input_output_aliases={nr + k: ns + k for k in range(nr)}, compiler_params=_SIDE_EFFECTS,
    )(*shards, *fulls, *sems, *after)
    return res[:ns], res[ns:ns + nr], res[-1]


def _gather_rest_end(sems, fulls, after):
    nr, ns = len(REST), 6 * len(REST)

    def body(*refs):
        for cp in _rest_d2d_copies(refs[:nr], refs[nr:nr + ns]):
            cp.wait_send()
            cp.wait_recv()

    return pl.pallas_call(
        body, name="gather_rest_end",
        out_shape=tuple(pltpu.HBM(a.shape, a.dtype) for a in fulls),
        in_specs=(_HBM_SPEC,) * nr + (_SEM_SPEC,) * ns + (_ANY_SPEC,) * len(after),
        out_specs=(_HBM_SPEC,) * nr,
        input_output_aliases={k: k for k in range(nr)}, compiler_params=_SIDE_EFFECTS,
    )(*fulls, *sems, *after)


def _adam_update(w, g, m, v):
    mn = ADAM_B1 * m + (1.0 - ADAM_B1) * g
    vn = ADAM_B2 * v + (1.0 - ADAM_B2) * (g * g)
    m_hat = mn / (1.0 - ADAM_B1 ** ADAM_STEP)
    v_hat = vn / (1.0 - ADAM_B2 ** ADAM_STEP)
    return -ADAM_LR * (m_hat / (jnp.sqrt(v_hat) + ADAM_EPS) + ADAM_WD * w), mn, vn


def _final_sum(name, pos, axis, psum, recv, shard_shape, after=(), tr=128):
    r, cw = shard_shape
    hr = r // 2
    tr = min(tr, hr)
    nt = hr // tr
    n_after = len(after)

    def kern(pos_ref, p_ref, r_ref, *rest):
        g_ref, send_buf, land_buf, s_sem, r_sem = rest[n_after:]
        p, t = pl.program_id(0), pl.program_id(1)
        sib = _sibling()

        def copy(i):
            return pltpu.make_async_remote_copy(
                src_ref=send_buf.at[i], dst_ref=land_buf.at[i], send_sem=s_sem.at[i],
                recv_sem=r_sem.at[i], device_id=sib, device_id_type=MESH)

        @pl.when(p == 0)
        def _():
            tot = p_ref[...].astype(F32)
            for j in range(3):
                tot = tot + r_ref[j].astype(F32)
            send_buf[t] = tot
            copy(t).start()
            g_ref[...] = tot

        @pl.when(p == 1)
        def _():
            copy(t).wait_recv()
            g_ref[...] = land_buf[t]

        @pl.when(jnp.logical_and(p == 1, t == nt - 1))
        def _():
            for i in range(nt):
                copy(i).wait_send()

    def shard_rows(p, t, pos_ref):
        return (jnp.where(p == 0, pos_ref[0], 1 - pos_ref[0]) * nt + t, 0)

    def own_part(p, t, pos_ref):
        tt = jnp.where(p == 0, t, nt - 1)
        return (tt, pos_ref[1]) if axis == 1 else (pos_ref[1] * nt + tt, 0)

    grid_spec = pltpu.PrefetchScalarGridSpec(
        num_scalar_prefetch=1, grid=(2, nt),
        in_specs=[pl.BlockSpec((tr, cw), own_part),
                  pl.BlockSpec((3, tr, cw), lambda p, t, pos_ref: (0, jnp.where(p == 0, t, nt - 1), 0))]
        + [pl.BlockSpec(memory_space=pl.ANY)] * n_after,
        out_specs=pl.BlockSpec((tr, cw), shard_rows),
        scratch_shapes=[pltpu.VMEM((nt, tr, cw), F32), pltpu.VMEM((nt, tr, cw), F32),
                        pltpu.SemaphoreType.DMA((nt,)), pltpu.SemaphoreType.DMA((nt,))])
    return pl.pallas_call(
        kern, name=name, grid_spec=grid_spec, out_shape=jax.ShapeDtypeStruct((r, cw), F32),
        compiler_params=_cparams(("arbitrary", "arbitrary")),
    )(pos, psum, recv, *after)


def _adamw(name, w, g, m, v):
    r, cw = w.shape
    tr = min(r, 128)

    def kern(w_ref, g_ref, m_ref, v_ref, go_ref, d_ref, nm_ref, nv_ref):
        gv = g_ref[...]
        go_ref[...] = gv
        d_ref[...], nm_ref[...], nv_ref[...] = _adam_update(w_ref[...], gv, m_ref[...], v_ref[...])

    spec = pl.BlockSpec((tr, cw), lambda i: (i, 0))
    return pl.pallas_call(
        kern, name=name, grid=(r // tr,), in_specs=[spec] * 4, out_specs=[spec] * 4,
        out_shape=[jax.ShapeDtypeStruct((r, cw), F32)] * 4, compiler_params=_cparams(("parallel",)),
    )(w, g, m, v)


_PACK_W = ADA_COLS
_NB = REL_BUCKETS * N_ATT_HEADS
_SMALL_SLOTS = {
    "b_ada": (0, 0, ADA_COLS),
    "norm1_g": (1, 0, D_MODEL), "norm2_g": (1, D_MODEL, D_MODEL), "norm_f_g": (1, 2 * D_MODEL, D_MODEL),
    "ret_gn_g": (1, 3 * D_MODEL, RET_V_W),
    "ret_gn_b": (2, 0, RET_V_W), "rel_bias": (2, RET_V_W, _NB), "loss": (2, RET_V_W + 512, 128),
}


def _pack_small(vals):
    rows = []
    for r in range(8):
        items = sorted([(off, n) for n, (rr, off, _) in _SMALL_SLOTS.items() if rr == r and n in vals])
        parts, pos = [], 0
        for off, n in items:
            if off > pos:
                parts.append(jnp.zeros((1, off - pos), F32))
            parts.append(vals[n].reshape(1, -1).astype(F32))
            pos = off + _SMALL_SLOTS[n][2]
        if pos < _PACK_W:
            parts.append(jnp.zeros((1, _PACK_W - pos), F32))
        rows.append(jnp.concatenate(parts, axis=-1))
    return jnp.concatenate(rows, axis=0)


def _unpack_small(pack, name):
    r, off, wd = _SMALL_SLOTS[name]
    return pack[r:r + 1, off:off + wd]


def kernel(x, c, w_ada, b_ada, norm1_g, w_in, rel_bias, ret_gn_g, ret_gn_b, w_ret_out, w_att_out, w_o, norm2_g, w_ff1, w_ff2, norm_f_g, loss_target, m_w_ada, m_b_ada, m_norm1_g, m_w_in, m_rel_bias, m_ret_gn_g, m_ret_gn_b, m_w_ret_out, m_w_att_out, m_w_o, m_norm2_g, m_w_ff1, m_w_ff2, m_norm_f_g, v_w_ada, v_b_ada, v_norm1_g, v_w_in, v_rel_bias, v_ret_gn_g, v_ret_gn_b, v_w_ret_out, v_w_att_out, v_w_o, v_norm2_g, v_w_ff1, v_w_ff2, v_norm_f_g):
    given = dict(locals())
    big_names = [n for n, _ in BIG]
    shard_w = {n: given[n][0] for n in big_names}
    assert all(shard_w[n].shape == SHARD[n] for n in big_names)

    shards_bf = [shard_w[n].astype(BF16) for n in big_names]
    full = _gather_weights(shards_bf, 1)
    mod, sc_all = _ada_fwd(c, w_ada[0], b_ada)
    rest_gather = _gather_rest_start(shards_bf[1:], list(full[1:]), [mod])
    pos = _where_am_i()

    loss, grad_x, d_mod, small, g_big, pending = _local_step(
        pos, x[0], loss_target[0], mod, norm1_g, norm2_g, norm_f_g.reshape(1, -1), rel_bias, ret_gn_g,
        ret_gn_b, full[0], rest_gather)

    pack_g = _pack_small(dict(b_ada=d_mod, norm1_g=small["norm1_g"], norm2_g=small["norm2_g"],
                              norm_f_g=small["norm_f_g"], ret_gn_g=small["gn_g"], ret_gn_b=small["gn_b"],
                              rel_bias=small["rel_bias"], loss=loss))
    tot, g_w_ada = _small_reduce(pack_g, sc_all, after=list(g_big.values()))

    small_names = ["b_ada", "norm1_g", "rel_bias", "ret_gn_g", "ret_gn_b", "norm2_g", "norm_f_g"]
    pack_w = _pack_small({n: given[n] for n in small_names})
    pack_m = _pack_small({n: given["m_" + n] for n in small_names})
    pack_v = _pack_small({n: given["v_" + n] for n in small_names})
    _, sd, sm, sv = _adamw("adamw_small", pack_w, tot, pack_m, pack_v)

    grads, deltas, new_m, new_v = {}, {}, {}, {}
    for n in small_names:
        shp = given[n].shape
        grads[n] = _unpack_small(tot, n).reshape(shp)
        deltas[n] = _unpack_small(sd, n).reshape(shp)
        new_m[n] = _unpack_small(sm, n).reshape(shp)
        new_v[n] = _unpack_small(sv, n).reshape(shp)
    g_big["w_ada"] = g_w_ada
    for n in ["w_ada"] + big_names[1:] + big_names[:1]:
        if n == "w_in":
            gw_in, sems, land = pending
            done = [tot, sd] + [deltas[k] for k in ["w_ada"] + big_names[1:]]
            (gw_in,), (got,) = _ici_wait("ici_wait_w_in", [n], sems, [gw_in], [land], done)
            g_big[n] = _final_sum("final_w_in", pos, 1, gw_in, got, SHARD[n])
        g, d, nm, nv = _adamw("adamw_" + n, given[n][0], g_big[n], given["m_" + n][0], given["v_" + n][0])
        grads[n], deltas[n], new_m[n], new_v[n] = g[None], d[None], nm[None], nv[None]

    order = ["w_ada", "b_ada", "norm1_g", "w_in", "rel_bias", "ret_gn_g", "ret_gn_b", "w_ret_out",
             "w_att_out", "w_o", "norm2_g", "w_ff1", "w_ff2", "norm_f_g"]
    loss_out = _unpack_small(tot, "loss")[0, 0]
    return (loss_out, grad_x[None], *[grads[n] for n in order], *[deltas[n] for n in order],
            *[new_m[n] for n in order], *[new_v[n] for n in order])
```

```python
import functools
import math

import jax
import jax.numpy as jnp
import numpy as np
from jax import lax
from jax.experimental import pallas as pl
from jax.experimental.pallas import tpu as pltpu

F32 = jnp.float32
BF16 = jnp.bfloat16
I32 = jnp.int32

SEQ = 2048
D_MODEL = 1024
RET_HEADS = 4
RET_DK = 256
RET_DV = 512
RET_CHUNK = 128
RET_SUB = 2
RET_QK_W = RET_HEADS * RET_DK
RET_V_W = RET_HEADS * RET_DV
ATT_GROUPS = ((128, 1), (512, 4), (2048, 16))
ATT_HPG = 4
ATT_DH = 128
ATT_W = ATT_HPG * ATT_DH
ATT_BLK = 128
N_BLK = SEQ // ATT_BLK
REL_BUCKETS = 32
REL_MAX_DIST = 2048
N_ATT_HEADS = 12
D_FF = 4 * D_MODEL
RMS_EPS = 1e-6
GN_EPS = 1e-5
ROPE_BASE = 10000.0
IN_COLS = 2 * RET_QK_W + 2 * RET_V_W + 9 * ATT_W + 2 * D_MODEL
OFF_Q, OFF_K, OFF_V, OFF_G = 0, RET_QK_W, 2 * RET_QK_W, 2 * RET_QK_W + RET_V_W
OFF_ATT = 2 * RET_QK_W + 2 * RET_V_W
OFF_GATE = OFF_ATT + 9 * ATT_W
N_CHIPS = 4
N_DEV = 8
ADA_COLS = 6 * D_MODEL

ADAM_LR = 0.001
ADAM_B1 = 0.9
ADAM_B2 = 0.999
ADAM_EPS = 1e-08
ADAM_WD = 0.01
ADAM_STEP = 10

VMEM_LIMIT_V7X = 56 * 1024 * 1024
MESH = pl.DeviceIdType.MESH


def _cparams(sem):
    return pltpu.CompilerParams(dimension_semantics=sem, vmem_limit_bytes=VMEM_LIMIT_V7X)


def _sigmoid(v):
    return 1.0 / (1.0 + jnp.exp(-v))


def _rowmap(name, body, row_ins, bcast_ins, row_outs, sum_outs=(), tm=256, after=()):
    m = row_ins[0].shape[0]
    n_in = len(row_ins) + len(bcast_ins)
    n_ro = len(row_outs)

    def kern(*refs):
        vals = [r[...] for r in refs[:n_in]]
        res = body(*vals)
        if not isinstance(res, (tuple, list)):
            res = (res,)
        outs = refs[n_in + len(after):]
        for r, v in zip(outs[:n_ro], res[:n_ro]):
            r[...] = v.astype(r.dtype)
        if sum_outs:
            @pl.when(pl.program_id(0) == 0)
            def _():
                for r in outs[n_ro:]:
                    r[...] = jnp.zeros_like(r)
            for r, v in zip(outs[n_ro:], res[n_ro:]):
                r[...] += v

    in_specs = [pl.BlockSpec((tm, a.shape[1]), lambda i: (i, 0)) for a in row_ins]
    in_specs += [pl.BlockSpec(a.shape, lambda i: (0, 0)) for a in bcast_ins]
    in_specs += [pl.BlockSpec(memory_space=pl.ANY)] * len(after)
    out_specs = [pl.BlockSpec((tm, n), lambda i: (i, 0)) for n, _ in row_outs]
    out_specs += [pl.BlockSpec((1, n), lambda i: (0, 0)) for n in sum_outs]
    out_shape = [jax.ShapeDtypeStruct((m, n), dt) for n, dt in row_outs]
    out_shape += [jax.ShapeDtypeStruct((1, n), F32) for n in sum_outs]
    return pl.pallas_call(
        kern, name=name, grid=(m // tm,), in_specs=in_specs, out_specs=out_specs,
        out_shape=out_shape, compiler_params=_cparams(("arbitrary",)),
    )(*row_ins, *bcast_ins, *after)


TM, TN = 1024, 1024


def _piece_chunks(piece, width):
    arr, stacked = piece
    return arr.shape[0] if stacked else arr.shape[1] // width


def _piece_spec(piece, rows, width, start, row_of, chunk_of):
    arr, stacked = piece
    last = _piece_chunks(piece, width) - 1

    def local(*ids):
        return jnp.clip(chunk_of(*ids) - start, 0, last)

    def row(*ids):
        rel = chunk_of(*ids) - start
        return jnp.where(jnp.logical_and(rel >= 0, rel <= last), row_of(*ids), 0)

    if stacked:
        return pl.BlockSpec((None, rows, width), lambda *ids: (local(*ids), row(*ids), 0))
    return pl.BlockSpec((rows, width), lambda *ids: (row(*ids), local(*ids)))


def _piece_starts(pieces, width):
    return [sum(_piece_chunks(p, width) for p in pieces[:q]) for q in range(len(pieces))]


def _matmul(name, a, b, kind, m, n, k, outs, *, b_off=0, tm=TM, tn=TN, tk=1024,
            epilogue=None, extras=(), after=(), n_sums=0):
    tm, tn, tk = min(tm, m), min(tn, n), min(tk, k)
    nk = k // tk
    pieces = a if isinstance(a, list) else [(a, False)]
    starts = _piece_starts(pieces, tk)
    if kind == "nn":
        a_specs = [pl.BlockSpec((tm, tk), lambda i, j, kk: (i, kk))]
        b_spec = pl.BlockSpec((tk, tn), lambda i, j, kk: (kk, b_off // tn + j))
        dn = (((1,), (0,)), ((), ()))
    elif kind == "nt":
        a_specs = [_piece_spec(p, tm, tk, st, lambda i, j, kk: i, lambda i, j, kk: kk)
                   for p, st in zip(pieces, starts)]
        b_spec = pl.BlockSpec((tn, tk), lambda i, j, kk: (j, b_off // tk + kk))
        dn = (((1,), (1,)), ((), ()))
    else:
        a_specs = [pl.BlockSpec((tk, tm), lambda i, j, kk: (kk, i))]
        b_spec = pl.BlockSpec((tk, tn), lambda i, j, kk: (kk, j))
        dn = (((0,), (0,)), ((), ()))
    n_a, n_ex, n_out = len(pieces), len(extras), len(outs)
    if epilogue is None:
        epilogue = lambda acc: (acc,)

    assert n_sums == 0 or tn == n

    def finish(acc, ex_refs, out_refs, first_rows):
        res = epilogue(acc, *[r[...] for r in ex_refs])
        for r, v in zip(out_refs[:n_out], res[:n_out]):
            r[...] = v.astype(r.dtype)
        for r, v in zip(out_refs[n_out:], res[n_out:]):
            @pl.when(first_rows)
            def _(r=r, v=v):
                r[...] = v

            @pl.when(jnp.logical_not(first_rows))
            def _(r=r, v=v):
                r[...] += v

    n_in = n_a + 1 + n_ex + len(after)

    def kern(*refs):
        a_refs, b_ref = refs[:n_a], refs[n_a]
        ex_refs = refs[n_a + 1:n_a + 1 + n_ex]
        out_refs = refs[n_in:n_in + n_out + n_sums]
        first_rows, kk = pl.program_id(0) == 0, pl.program_id(2)
        dot = lambda a_ref: lax.dot_general(a_ref[...], b_ref[...], dn, preferred_element_type=F32)
        if nk == 1:
            finish(dot(a_refs[0]), ex_refs, out_refs, first_rows)
            return
        acc_ref = refs[n_in + n_out + n_sums]
        if n_a == 1:
            part = dot(a_refs[0])

            @pl.when(kk == 0)
            def _():
                acc_ref[...] = part

            @pl.when(kk > 0)
            def _():
                acc_ref[...] += part
        else:
            @pl.when(kk == 0)
            def _():
                acc_ref[...] = jnp.zeros_like(acc_ref)

            for q in range(n_a):
                @pl.when(jnp.logical_and(kk >= starts[q], kk < starts[q] + _piece_chunks(pieces[q], tk)))
                def _(q=q):
                    acc_ref[...] += dot(a_refs[q])

        @pl.when(kk == nk - 1)
        def _():
            finish(acc_ref[...], ex_refs, out_refs, first_rows)

    in_specs = a_specs + [b_spec] + [pl.BlockSpec(bs, im) for _, bs, im in extras]
    in_specs += [pl.BlockSpec(memory_space=pl.ANY)] * len(after)
    sem = ("arbitrary",) * 3 if n_sums else ("parallel", "parallel", "arbitrary")
    return pl.pallas_call(
        kern, name=name, grid=(m // tm, n // tn, nk), in_specs=in_specs,
        out_specs=[pl.BlockSpec((tm, tn), lambda i, j, kk: (i, j)) for _ in outs]
        + [pl.BlockSpec((1, tn), lambda i, j, kk: (0, 0))] * n_sums,
        out_shape=[jax.ShapeDtypeStruct((m, n), dt) for dt in outs]
        + [jax.ShapeDtypeStruct((1, n), F32)] * n_sums,
        scratch_shapes=[] if nk == 1 else [pltpu.VMEM((tm, tn), F32)],
        compiler_params=_cparams(sem),
    )(*[p[0] for p in pieces], b, *[e[0] for e in extras], *after)


def _ici_copies(psum_ref, recv_ref, s_sem, r_sem, axis, shard_shape):
    x, y, c = _me()
    hr, cw = shard_shape[0] // 2, shard_shape[1]
    pick = lambda sems, j: sems[j] if isinstance(sems, (list, tuple)) else sems.at[j]
    copies = []
    for j, (fx, fy) in enumerate(_CHIP_FLIPS):
        chip = 2 * (x ^ fx) + (y ^ fy)
        src = psum_ref.at[:, pl.ds(chip * cw, cw)] if axis == 1 else psum_ref.at[pl.ds(chip * hr, hr), :]
        copies.append(pltpu.make_async_remote_copy(
            src_ref=src, dst_ref=recv_ref.at[j], send_sem=pick(s_sem, j), recv_sem=pick(r_sem, j),
            device_id=(x ^ fx, y ^ fy, c), device_id_type=MESH))
    return copies


_HBM_SPEC = pl.BlockSpec(memory_space=pltpu.HBM)
_SEM_SPEC = pl.BlockSpec(memory_space=pltpu.SEMAPHORE)


def _split_ici_copies(names, p_refs, land_refs, sems):
    copies = []
    for i, n in enumerate(names):
        copies += _ici_copies(p_refs[i], land_refs[i], list(sems[6 * i:6 * i + 3]),
                              list(sems[6 * i + 3:6 * i + 6]), dict(BIG)[n], SHARD[n])
    return copies


def _ici_start(name, names, psums):
    nw, ns = len(names), 6 * len(names)
    lands = [lax.empty((3, SHARD[n][0] // 2, SHARD[n][1]), BF16) for n in names]

    def body(*refs):
        for cp in _split_ici_copies(names, refs[:nw], refs[nw:2 * nw], refs[2 * nw:2 * nw + ns]):
            cp.start()
        token = refs[-1]
        token[...] = jnp.zeros_like(token)

    res = pl.pallas_call(
        body, name=name,
        out_shape=(pltpu.SemaphoreType.DMA(()),) * ns
        + tuple(pltpu.HBM(a.shape, BF16) for a in list(psums) + lands)
        + (jax.ShapeDtypeStruct((8, 128), F32),),
        in_specs=(_HBM_SPEC,) * (2 * nw),
        out_specs=(_SEM_SPEC,) * ns + (_HBM_SPEC,) * (2 * nw) + (pl.BlockSpec(memory_space=pltpu.VMEM),),
        input_output_aliases={k: ns + k for k in range(2 * nw)},
        compiler_params=pltpu.CompilerParams(has_side_effects=pltpu.SideEffectType.DATAFLOW_SIDE_EFFECTING),
    )(*[pltpu.with_memory_space_constraint(a, pltpu.HBM) for a in list(psums) + lands])
    return res[:ns], res[ns:ns + nw], res[ns + nw:ns + 2 * nw], res[-1]


def _ici_wait(name, names, sems, p_thru, land_thru, after):
    nw, ns = len(names), 6 * len(names)

    def body(*refs):
        for cp in _split_ici_copies(names, refs[:nw], refs[nw:2 * nw], refs[2 * nw:2 * nw + ns]):
            cp.wait_send()
            cp.wait_recv()

    res = pl.pallas_call(
        body, name=name,
        out_shape=tuple(pltpu.HBM(a.shape, BF16) for a in list(p_thru) + list(land_thru)),
        in_specs=(_HBM_SPEC,) * (2 * nw) + (_SEM_SPEC,) * ns + (pl.BlockSpec(memory_space=pl.ANY),) * len(after),
        out_specs=(_HBM_SPEC,) * (2 * nw), input_output_aliases={k: k for k in range(2 * nw)},
        compiler_params=pltpu.CompilerParams(has_side_effects=pltpu.SideEffectType.DATAFLOW_SIDE_EFFECTING),
    )(*p_thru, *land_thru, *sems, *after)
    return res[:nw], res[nw:]


def _where_am_i():
    x, y, c = _me()
    return jnp.stack([c, 2 * x + y]).astype(I32)


def _sibling():
    x, y, c = _me()
    return (x, y, 1 - c)


N_SEND_SLOTS = 2


def _matmul_tn_pair(name, pos, a, b, m, n, k, shard_rows, *, tm, tn, tk):
    hr = shard_rows // 2
    tm, tn, tk = min(tm, hr), min(tn, n), min(tk, k)
    tph = hr // tm
    nt, nj, nk = (m // 2) // tm, n // tn, k // tk
    n_tiles = nt * nj

    def row_block(p, t, pos_ref):
        half = jnp.where(p == 0, 1 - pos_ref[0], pos_ref[0])
        return (t // tph) * (2 * tph) + half * tph + t % tph

    pieces = b if isinstance(b, list) else [(b, False)]
    starts = _piece_starts(pieces, tn)
    n_b = len(pieces)

    def kern(pos_ref, a_ref, *rest):
        b_refs = rest[:n_b]
        o_ref, acc_ref, send_buf, land_buf, s_sem, r_sem = rest[n_b:]
        p, t, j, kk = pl.program_id(0), pl.program_id(1), pl.program_id(2), pl.program_id(3)
        idx = t * nj + j
        sib = _sibling()

        def copy(i):
            return pltpu.make_async_remote_copy(
                src_ref=send_buf.at[i % N_SEND_SLOTS], dst_ref=land_buf.at[i], send_sem=s_sem.at[i],
                recv_sem=r_sem.at[i], device_id=sib, device_id_type=MESH)

        @pl.when(kk == 0)
        def _():
            acc_ref[...] = jnp.zeros_like(acc_ref)

        for q in range(n_b):
            @pl.when(jnp.logical_and(j >= starts[q], j < starts[q] + _piece_chunks(pieces[q], tn)))
            def _(q=q):
                acc_ref[...] += lax.dot_general(a_ref[...], b_refs[q][...], _TN, preferred_element_type=F32)

        @pl.when(jnp.logical_and(kk == nk - 1, p == 0))
        def _():
            @pl.when(idx >= N_SEND_SLOTS)
            def _():
                copy(idx - N_SEND_SLOTS).wait_send()

            send_buf[idx % N_SEND_SLOTS] = acc_ref[...].astype(BF16)
            copy(idx).start()

        @pl.when(jnp.logical_and(kk == nk - 1, p == 1))
        def _():
            copy(idx).wait_recv()
            o_ref[...] = (acc_ref[...] + land_buf[idx].astype(F32)).astype(BF16)

        @pl.when(jnp.logical_and(jnp.logical_and(p == 1, idx == n_tiles - 1), kk == nk - 1))
        def _():
            for i in range(max(n_tiles - N_SEND_SLOTS, 0), n_tiles):
                copy(i).wait_send()

    grid_spec = pltpu.PrefetchScalarGridSpec(
        num_scalar_prefetch=1, grid=(2, nt, nj, nk),
        in_specs=[pl.BlockSpec((tk, tm), lambda p, t, j, kk, pos_ref: (kk, row_block(p, t, pos_ref)))]
        + [_piece_spec(pc, tk, tn, st, lambda p, t, j, kk, pos_ref: kk, lambda p, t, j, kk, pos_ref: j)
           for pc, st in zip(pieces, starts)],
        out_specs=pl.BlockSpec((tm, tn), lambda p, t, j, kk, pos_ref: (p * t, p * j)),
        scratch_shapes=[pltpu.VMEM((tm, tn), F32), pltpu.VMEM((N_SEND_SLOTS, tm, tn), BF16),
                        pltpu.VMEM((n_tiles, tm, tn), BF16),
                        pltpu.SemaphoreType.DMA((n_tiles,)), pltpu.SemaphoreType.DMA((n_tiles,))])
    return pl.pallas_call(
        kern, name=name, grid_spec=grid_spec, out_shape=jax.ShapeDtypeStruct((m // 2, n), BF16),
        compiler_params=_cparams(("arbitrary",) * 4),
    )(pos, a, *[pc[0] for pc in pieces])


def _rope_tables():
    half = RET_DK // 2
    f32 = np.float32
    inv = np.power(f32(ROPE_BASE), -np.arange(half, dtype=f32) / f32(half)).astype(f32)
    ang = (np.arange(SEQ, dtype=f32)[:, None] * inv[None, :]).astype(f32)
    return jnp.asarray(np.cos(ang).astype(f32)), jnp.asarray(np.sin(ang).astype(f32))


def _decay_tables():
    c = RET_CHUNK
    f32 = np.float32
    log_g = np.log1p(-np.power(f32(2.0), f32(-5.0) - np.arange(RET_HEADS, dtype=f32))).astype(f32)
    idx = np.arange(c, dtype=f32)
    rel = idx[:, None] - idx[None, :]
    din = np.where(rel >= 0, np.exp(log_g[:, None, None] * np.maximum(rel, f32(0.0))), f32(0.0)).astype(f32)
    qd = np.exp(log_g[:, None] * (idx + f32(1.0))).astype(f32)[:, :, None]
    kd = np.exp(log_g[:, None] * (f32(c) - f32(1.0) - idx)).astype(f32)[:, :, None]
    cd = np.exp(log_g * f32(c)).astype(f32)
    return jnp.asarray(din), jnp.asarray(qd), jnp.asarray(kd), jnp.asarray(cd)


def _t5_bucket(dist):
    max_exact = REL_BUCKETS // 2
    d_f = jnp.maximum(dist, 1).astype(F32)
    large = max_exact + (jnp.log(d_f / max_exact) / math.log(REL_MAX_DIST / max_exact)
                         * (REL_BUCKETS - max_exact)).astype(I32)
    large = jnp.minimum(large, REL_BUCKETS - 1)
    return jnp.where(dist < max_exact, dist, large)


def _bucket_tables():
    qi = jnp.arange(ATT_BLK)[:, None]
    kj = jnp.arange(2 * ATT_BLK)[None, :]
    dist = jnp.clip(ATT_BLK + qi - kj, 0, ATT_BLK)
    return jnp.stack([_t5_bucket(dist * dil) for _, dil in ATT_GROUPS]).astype(I32)


def _retention_fwd(rqk, rv, rg, gn_g, gn_b, din, qd, kd, cd):
    nc = SEQ // RET_CHUNK
    c, dk, dv = RET_CHUNK, RET_DK, RET_DV

    def kern(q_ref, k_ref, v_ref, rg_ref, g_ref, b_ref, din_ref, qd_ref, kd_ref, cd_ref,
             o_ref, st_ref, gated_ref, state):
        n = pl.program_id(0)

        @pl.when(n == 0)
        def _():
            state[...] = jnp.zeros_like(state)

        for sub in range(RET_SUB):
            rows = slice(sub * c, (sub + 1) * c)
            for h in range(RET_HEADS):
                q, k = q_ref[rows, h * dk:(h + 1) * dk], k_ref[rows, h * dk:(h + 1) * dk]
                v = v_ref[rows, h * dv:(h + 1) * dv]
                s_b = state[h].astype(BF16)
                st_ref[h, sub] = s_b
                a = lax.dot_general(q, k, _NT, preferred_element_type=F32) * din_ref[h]
                o = jnp.dot(a.astype(BF16), v, preferred_element_type=F32)
                o += jnp.dot(q, s_b, preferred_element_type=F32) * qd_ref[h]
                v_cols = slice(h * dv, (h + 1) * dv)
                o_ref[rows, v_cols] = o
                nrm, _ = _gn_parts(o)
                gate = rg_ref[rows, v_cols].astype(F32)
                gated_ref[rows, v_cols] = ((gate * _sigmoid(gate))
                                           * (nrm * g_ref[:, v_cols] + b_ref[:, v_cols])).astype(BF16)
                kk = (k.astype(F32) * kd_ref[h]).astype(BF16)
                state[h] = state[h] * cd_ref[h] + lax.dot_general(kk, v, _TN, preferred_element_type=F32)

    whole = lambda a: pl.BlockSpec(a.shape, lambda n: (0,) * a.ndim)
    cs = RET_SUB * c
    rows_v = pl.BlockSpec((cs, RET_V_W), lambda n: (n, 0))
    return pl.pallas_call(
        kern, name="retention_fwd", grid=(nc // RET_SUB,),
        in_specs=[
            pl.BlockSpec((cs, RET_QK_W), lambda n: (n, 0)),
            pl.BlockSpec((cs, RET_QK_W), lambda n: (n, 1)),
            rows_v, rows_v, whole(gn_g), whole(gn_b),
            whole(din), whole(qd), whole(kd),
            pl.BlockSpec(memory_space=pltpu.SMEM),
        ],
        out_specs=[
            rows_v,
            pl.BlockSpec((RET_HEADS, RET_SUB, dk, dv), lambda n: (0, n, 0, 0)),
            rows_v,
        ],
        out_shape=[
            jax.ShapeDtypeStruct((SEQ, RET_V_W), F32),
            jax.ShapeDtypeStruct((RET_HEADS, nc, dk, dv), BF16),
            jax.ShapeDtypeStruct((SEQ, RET_V_W), BF16),
        ],
        scratch_shapes=[pltpu.VMEM((RET_HEADS, dk, dv), F32)],
        compiler_params=_cparams(("arbitrary",)),
    )(rqk, rqk, rv, rg, gn_g, gn_b, din, qd, kd, cd)


def _retention_bwd(rqk, rv, states, d_gated, ro, rg, gn_g, gn_b, din, qd, kd, cd, cos, sin):
    nc = SEQ // RET_CHUNK
    c, dk, dv = RET_CHUNK, RET_DK, RET_DV
    half = dk // 2
    last = nc // RET_SUB - 1

    def unrot(g, cs, sn):
        g1, g2 = g[:, :half], g[:, half:]
        return jnp.concatenate([g1 * cs + g2 * sn, g2 * cs - g1 * sn], axis=-1)

    def kern(q_ref, k_ref, v_ref, st_ref, dg_ref, ro_ref, rg_ref, g_ref, b_ref, din_ref, qd_ref, kd_ref,
             cd_ref, cos_ref, sin_ref, out_ref, drg_ref, dgn_g_ref, dgn_b_ref, dstate):
        step = pl.program_id(0)

        @pl.when(step == 0)
        def _():
            dstate[...] = jnp.zeros_like(dstate)
            dgn_g_ref[...] = jnp.zeros_like(dgn_g_ref)
            dgn_b_ref[...] = jnp.zeros_like(dgn_b_ref)

        for sub in reversed(range(RET_SUB)):
            rows = slice(sub * c, (sub + 1) * c)
            cs, sn = cos_ref[rows, :], sin_ref[rows, :]
            for h in range(RET_HEADS):
                qk_cols, v_cols = slice(h * dk, (h + 1) * dk), slice(h * dv, (h + 1) * dv)
                q, k, v = q_ref[rows, qk_cols], k_ref[rows, qk_cols], v_ref[rows, v_cols]
                s_b = st_ref[h, sub]
                nrm, rstd = _gn_parts(ro_ref[rows, v_cols])
                gate, dg = rg_ref[rows, v_cols].astype(F32), dg_ref[rows, v_cols].astype(F32)
                sg = _sigmoid(gate)
                gn_gain = g_ref[:, v_cols]
                drg_ref[rows, v_cols] = (dg * (nrm * gn_gain + b_ref[:, v_cols])
                                         * (sg * (1.0 + gate * (1.0 - sg)))).astype(BF16)
                d_ron = dg * (gate * sg)
                dgn_g_ref[:, v_cols] += jnp.sum(d_ron * nrm, axis=0, keepdims=True)
                dgn_b_ref[:, v_cols] += jnp.sum(d_ron, axis=0, keepdims=True)
                d_n = d_ron * gn_gain
                d_o = rstd * (d_n - jnp.mean(d_n, axis=-1, keepdims=True)
                              - nrm * jnp.mean(d_n * nrm, axis=-1, keepdims=True))
                d_ob = d_o.astype(BF16)
                d_oq = (d_o * qd_ref[h]).astype(BF16)
                ds_b = dstate[h].astype(BF16)
                din_m = din_ref[h]
                a_b = (lax.dot_general(q, k, _NT, preferred_element_type=F32) * din_m).astype(BF16)
                kk = (k.astype(F32) * kd_ref[h]).astype(BF16)
                d_v = lax.dot_general(a_b, d_ob, _TN, preferred_element_type=F32)
                d_v += jnp.dot(kk, ds_b, preferred_element_type=F32)
                d_a = (lax.dot_general(d_ob, v, _NT, preferred_element_type=F32) * din_m).astype(BF16)
                d_q = jnp.dot(d_a, k, preferred_element_type=F32)
                d_q += lax.dot_general(d_oq, s_b, _NT, preferred_element_type=F32)
                d_k = lax.dot_general(d_a, q, _TN, preferred_element_type=F32)
                d_k += lax.dot_general(v, ds_b, _NT, preferred_element_type=F32) * kd_ref[h]
                dstate[h] = dstate[h] * cd_ref[h] + lax.dot_general(q, d_oq, _TN,
                                                                    preferred_element_type=F32)
                out_ref[rows, h * dk:(h + 1) * dk] = unrot(d_q, cs, sn).astype(BF16)
                out_ref[rows, RET_QK_W + h * dk:RET_QK_W + (h + 1) * dk] = (
                    unrot(d_k, cs, sn) * (RET_DK ** -0.5)).astype(BF16)
                out_ref[rows, 2 * RET_QK_W + h * dv:2 * RET_QK_W + (h + 1) * dv] = d_v.astype(BF16)

    whole = lambda a: pl.BlockSpec(a.shape, lambda n: (0,) * a.ndim)
    rs = RET_SUB * c
    rows_v = pl.BlockSpec((rs, RET_V_W), lambda n: (last - n, 0))
    return pl.pallas_call(
        kern, name="retention_bwd", grid=(nc // RET_SUB,),
        in_specs=[
            pl.BlockSpec((rs, RET_QK_W), lambda n: (last - n, 0)),
            pl.BlockSpec((rs, RET_QK_W), lambda n: (last - n, 1)),
            rows_v,
            pl.BlockSpec((RET_HEADS, RET_SUB, dk, dv), lambda n: (0, last - n, 0, 0)),
            rows_v, rows_v, rows_v, whole(gn_g), whole(gn_b),
            whole(din), whole(qd), whole(kd),
            pl.BlockSpec(memory_space=pltpu.SMEM),
            pl.BlockSpec((rs, half), lambda n: (last - n, 0)),
            pl.BlockSpec((rs, half), lambda n: (last - n, 0)),
        ],
        out_specs=[pl.BlockSpec((rs, 2 * RET_QK_W + RET_V_W), lambda n: (last - n, 0)), rows_v,
                   whole(gn_g), whole(gn_b)],
        out_shape=[jax.ShapeDtypeStruct((SEQ, 2 * RET_QK_W + RET_V_W), BF16),
                   jax.ShapeDtypeStruct((SEQ, RET_V_W), BF16),
                   jax.ShapeDtypeStruct((1, RET_V_W), F32), jax.ShapeDtypeStruct((1, RET_V_W), F32)],
        scratch_shapes=[pltpu.VMEM((RET_HEADS, dk, dv), F32)],
        compiler_params=_cparams(("arbitrary",)),
    )(rqk, rqk, rv, states, d_gated, ro, rg, gn_g, gn_b, din, qd, kd, cd, cos, sin)


def _bias_build(rel_bias, buckets):
    ng = len(ATT_GROUPS)

    def kern(tab_ref, bkt_ref, o_ref):
        g, h = pl.program_id(0), pl.program_id(1)
        bkt = bkt_ref[...]
        acc = jnp.zeros(bkt.shape, F32)
        for b in range(REL_BUCKETS):
            acc = jnp.where(bkt == b, tab_ref[b, g * ATT_HPG + h], acc)
        o_ref[...] = acc

    return pl.pallas_call(
        kern, name="bias_build", grid=(ng, ATT_HPG),
        in_specs=[pl.BlockSpec(memory_space=pltpu.SMEM),
                  pl.BlockSpec((None, ATT_BLK, 2 * ATT_BLK), lambda g, h: (g, 0, 0))],
        out_specs=pl.BlockSpec((None, None, ATT_BLK, 2 * ATT_BLK), lambda g, h: (g, h, 0, 0)),
        out_shape=jax.ShapeDtypeStruct((ng, ATT_HPG, ATT_BLK, 2 * ATT_BLK), F32),
        compiler_params=_cparams(("arbitrary", "arbitrary")),
    )(rel_bias, buckets)


def _bias_grad(dsb, buckets):
    ng = len(ATT_GROUPS)

    def kern(ds_ref, bkt_ref, o_ref):
        g, h = pl.program_id(0), pl.program_id(1)
        bkt, ds = bkt_ref[...], ds_ref[...]
        for b in range(REL_BUCKETS):
            o_ref[b, g * ATT_HPG + h] = jnp.sum(jnp.where(bkt == b, ds, 0.0))

    return pl.pallas_call(
        kern, name="bias_grad", grid=(ng, ATT_HPG),
        in_specs=[pl.BlockSpec((None, None, ATT_BLK, 2 * ATT_BLK), lambda g, h: (g, h, 0, 0)),
                  pl.BlockSpec((None, ATT_BLK, 2 * ATT_BLK), lambda g, h: (g, 0, 0))],
        out_specs=pl.BlockSpec(memory_space=pltpu.SMEM),
        out_shape=jax.ShapeDtypeStruct((REL_BUCKETS, N_ATT_HEADS), F32),
        compiler_params=_cparams(("arbitrary", "arbitrary")),
    )(dsb, buckets)


_NT = (((1,), (1,)), ((), ()))
_TN = (((0,), (0,)), ((), ()))
_ATT_SCALE = ATT_DH ** -0.5


def _window_mask(has_prev):
    qi = lax.broadcasted_iota(I32, (ATT_BLK, 2 * ATT_BLK), 0)
    kj = lax.broadcasted_iota(I32, (ATT_BLK, 2 * ATT_BLK), 1)
    prev_ok = jnp.logical_and(jnp.logical_and(kj < ATT_BLK, kj >= qi), has_prev)
    return jnp.logical_or(prev_ok, jnp.logical_and(kj >= ATT_BLK, qi >= kj - ATT_BLK))


def _head_specs(col0):
    return pl.BlockSpec((SEQ, ATT_DH), lambda h: (0, col0 + h))


def _sub_rows(start, size, dil):
    return pl.ds(start, size) if dil == 1 else pl.ds(start, size, stride=dil)


def _att_blocks(dil):
    nb = SEQ // dil // ATT_BLK
    return [(r + dil * n * ATT_BLK, n > 0, n + 1 < nb) for r in range(dil) for n in range(nb)]


def _att_fwd(gi, dil, qkv, bias):
    blk, dh = ATT_BLK, ATT_DH
    pad = dil * blk
    col0 = 3 * ATT_HPG * gi

    def kern(q_ref, k_ref, v_ref, b_ref, o_ref, l_ref, qf, kpad, vpad):
        zero = jnp.zeros((pad, dh), F32)
        kpad[0:pad, :] = zero
        vpad[0:pad, :] = zero
        kpad[pad:, :] = k_ref[...].astype(F32)
        vpad[pad:, :] = v_ref[...].astype(F32)
        qf[...] = q_ref[...].astype(F32)
        bias_m = b_ref[...]
        for start, has_prev, _ in _att_blocks(dil):
            rows, window = _sub_rows(start, blk, dil), _sub_rows(start, 2 * blk, dil)
            q = qf[rows, :].astype(BF16)
            kw, vw = kpad[window, :].astype(BF16), vpad[window, :].astype(BF16)
            valid = _window_mask(has_prev)
            s = lax.dot_general(q, kw, _NT, preferred_element_type=F32) * _ATT_SCALE + bias_m
            s = jnp.where(valid, s, -1e30)
            mx = jnp.max(s, axis=-1, keepdims=True)
            e = jnp.exp(s - mx)
            den = jnp.sum(e, axis=-1, keepdims=True)
            o_ref[rows, :] = jnp.dot((e / den).astype(BF16), vw, preferred_element_type=F32)
            l_ref[rows, :] = jnp.broadcast_to(mx + jnp.log(den), (blk, dh))

    return pl.pallas_call(
        kern, name=f"att_fwd_g{gi}", grid=(ATT_HPG,),
        in_specs=[_head_specs(col0), _head_specs(col0 + ATT_HPG), _head_specs(col0 + 2 * ATT_HPG),
                  pl.BlockSpec((None, None, blk, 2 * blk), lambda h: (gi, h, 0, 0))],
        out_specs=[_head_specs(0), _head_specs(0)],
        out_shape=[jax.ShapeDtypeStruct((SEQ, ATT_W), F32), jax.ShapeDtypeStruct((SEQ, ATT_W), F32)],
        scratch_shapes=[pltpu.VMEM((SEQ, dh), F32), pltpu.VMEM((SEQ + pad, dh), F32),
                        pltpu.VMEM((SEQ + pad, dh), F32)],
        compiler_params=_cparams(("arbitrary",)),
    )(qkv, qkv, qkv, bias)


def _att_bwd(gi, dil, qkv, d_att, lse, dd, bias):
    blk, dh = ATT_BLK, ATT_DH
    pad = dil * blk
    col0 = 3 * ATT_HPG * gi

    def kern(q_ref, k_ref, v_ref, do_ref, l_ref, d_ref, b_ref, dqkv_ref, dsb_ref,
             qf, kpad, vpad, dq_s, dkpad, dvpad):
        zero = jnp.zeros((pad, dh), F32)
        kpad[0:pad, :] = zero
        vpad[0:pad, :] = zero
        kpad[pad:, :] = k_ref[...].astype(F32)
        vpad[pad:, :] = v_ref[...].astype(F32)
        qf[...] = q_ref[...].astype(F32)
        dkpad[...] = jnp.zeros_like(dkpad)
        dvpad[...] = jnp.zeros_like(dvpad)
        bias_m = b_ref[...]
        ds_sum = jnp.zeros((blk, 2 * blk), F32)

        for start, has_prev, _ in _att_blocks(dil):
            rows, window = _sub_rows(start, blk, dil), _sub_rows(start, 2 * blk, dil)
            q, d_o = qf[rows, :].astype(BF16), do_ref[rows, :].astype(BF16)
            kw, vw = kpad[window, :].astype(BF16), vpad[window, :].astype(BF16)
            lrow, drow = l_ref[rows, :][:, :1], d_ref[rows, :][:, :1]
            valid = _window_mask(has_prev)
            s = lax.dot_general(q, kw, _NT, preferred_element_type=F32) * _ATT_SCALE + bias_m
            p = jnp.where(valid, jnp.exp(jnp.where(valid, s, -1e30) - lrow), 0.0)
            dp = lax.dot_general(d_o, vw, _NT, preferred_element_type=F32)
            ds = p * (dp - drow)
            ds_b = ds.astype(BF16)
            dq_s[rows, :] = jnp.dot(ds_b, kw, preferred_element_type=F32) * _ATT_SCALE
            dkpad[window, :] += lax.dot_general(ds_b, q, _TN, preferred_element_type=F32) * _ATT_SCALE
            dvpad[window, :] += lax.dot_general(p.astype(BF16), d_o, _TN, preferred_element_type=F32)
            ds_sum = ds_sum + ds
        dsb_ref[...] = ds_sum

        dqkv_ref[0] = dq_s[...].astype(BF16)
        dqkv_ref[1] = dkpad[pad:, :].astype(BF16)
        dqkv_ref[2] = dvpad[pad:, :].astype(BF16)

    return pl.pallas_call(
        kern, name=f"att_bwd_g{gi}", grid=(ATT_HPG,),
        in_specs=[_head_specs(col0), _head_specs(col0 + ATT_HPG), _head_specs(col0 + 2 * ATT_HPG),
                  _head_specs(0), _head_specs(0), _head_specs(0),
                  pl.BlockSpec((None, None, blk, 2 * blk), lambda h: (gi, h, 0, 0))],
        out_specs=[pl.BlockSpec((3, SEQ, dh), lambda h: (0, 0, h)),
                   pl.BlockSpec((None, blk, 2 * blk), lambda h: (h, 0, 0))],
        out_shape=[jax.ShapeDtypeStruct((3, SEQ, ATT_W), BF16),
                   jax.ShapeDtypeStruct((ATT_HPG, blk, 2 * blk), F32)],
        scratch_shapes=[pltpu.VMEM((SEQ, dh), F32), pltpu.VMEM((SEQ + pad, dh), F32),
                        pltpu.VMEM((SEQ + pad, dh), F32), pltpu.VMEM((SEQ, dh), F32),
                        pltpu.VMEM((SEQ + pad, dh), F32), pltpu.VMEM((SEQ + pad, dh), F32)],
        compiler_params=_cparams(("arbitrary",)),
    )(qkv, qkv, qkv, d_att, lse, dd, bias)


def _rms_parts(x):
    r = lax.rsqrt(jnp.mean(x * x, axis=-1, keepdims=True) + RMS_EPS)
    return x * r, r


def _rms_bwd(d_xhat, xhat, r):
    return r * (d_xhat - xhat * jnp.mean(d_xhat * xhat, axis=-1, keepdims=True))


def _prenorm_fwd(name, x, gain, shift, scale):
    def body(xt, g, sh, sc):
        xhat, _ = _rms_parts(xt)
        return (xhat * g) * (1.0 + sc) + sh
    return _rowmap(name, body, [x], [gain, shift, scale], [(D_MODEL, BF16)])[0]


def _prenorm_bwd_epi(d_h, x, resid, gain, scale, branch=None, gate=None):
    xhat, r = _rms_parts(x)
    nrm = xhat * gain
    d_n = d_h * (1.0 + scale)
    dx = _rms_bwd(d_n * gain, xhat, r) + resid
    sums = (jnp.sum(d_h, axis=0, keepdims=True), jnp.sum(d_h * nrm, axis=0, keepdims=True),
            jnp.sum(d_n * xhat, axis=0, keepdims=True))
    if branch is None:
        return (dx,) + sums
    return (dx, dx * gate) + sums + (jnp.sum(dx * branch, axis=0, keepdims=True),)


def _row_operands(tm, rows, vecs):
    return ([(a, (tm, D_MODEL), lambda i, j, kk: (i, 0)) for a in rows]
            + [(v, (1, D_MODEL), lambda i, j, kk: (0, 0)) for v in vecs])


def _gn_parts(ro):
    mu = jnp.mean(ro, axis=-1, keepdims=True)
    cen = ro - mu
    rstd = lax.rsqrt(jnp.mean(cen * cen, axis=-1, keepdims=True) + GN_EPS)
    return cen * rstd, rstd


def _combine(os_, ls_, after=()):
    def body(o0, o1, o2, l0, l1, l2):
        mx = jnp.maximum(jnp.maximum(l0, l1), l2)
        e0, e1, e2 = jnp.exp(l0 - mx), jnp.exp(l1 - mx), jnp.exp(l2 - mx)
        den = e0 + e1 + e2
        att = (e0 / den) * o0 + (e1 / den) * o1 + (e2 / den) * o2
        return att, att, mx + jnp.log(den)
    return _rowmap("att_combine", body, list(os_) + list(ls_), [],
                   [(ATT_W, F32), (ATT_W, BF16), (ATT_W, F32)], after=after)


MERGE_TM = 512


def _merge_operands(gates, ret_out, att_out=None):
    ops = [(gates, (MERGE_TM, D_MODEL), lambda i, j, kk: (i, 0)),
           (gates, (MERGE_TM, D_MODEL), lambda i, j, kk: (i, 1)),
           (ret_out, (MERGE_TM, D_MODEL), lambda i, j, kk: (i, 0))]
    if att_out is not None:
        ops.append((att_out, (MERGE_TM, D_MODEL), lambda i, j, kk: (i, 0)))
    return ops


def _merge_fwd_epi(att_out, ga, gb, ret_out):
    return att_out, _sigmoid(ga.astype(F32)) * ret_out + _sigmoid(gb.astype(F32)) * att_out


def _merge_bwd_epi(d_merged, ga, gb, ret_out, att_out):
    sa, sb = _sigmoid(ga.astype(F32)), _sigmoid(gb.astype(F32))
    return (d_merged * sa, d_merged * sb, d_merged * ret_out * (sa * (1.0 - sa)),
            d_merged * att_out * (sb * (1.0 - sb)))


def _att_out_bwd_epi(d_att, att):
    outs = []
    for h in range(ATT_HPG):
        sl = slice(h * ATT_DH, (h + 1) * ATT_DH)
        outs.append(jnp.broadcast_to(jnp.sum(d_att[:, sl] * att[:, sl], axis=-1, keepdims=True),
                                     (d_att.shape[0], ATT_DH)))
    return d_att, jnp.concatenate(outs, axis=-1)


def _loss_head_epi(branch, x_prev, target, gate, gain):
    x3 = x_prev + gate * branch
    xhat, r = _rms_parts(x3)
    err = xhat * gain - target
    d_y = err / D_MODEL
    loss = 0.5 * jnp.sum(jnp.mean(err * err, axis=-1, keepdims=True), axis=0, keepdims=True)
    d_x = _rms_bwd(d_y * gain, xhat, r)
    return (d_x, d_x * gate, jnp.broadcast_to(loss, (1, D_MODEL)),
            jnp.sum(d_y * xhat, axis=0, keepdims=True), jnp.sum(d_x * branch, axis=0, keepdims=True))


def _local_step(pos, x, target, mod, norm1_g, norm2_g, norm_f_g, rel_bias, gn_g, gn_b, w_in, rest_gather):
    sh1, sc1, g1, sh2, sc2, g2 = [mod[:, i * D_MODEL:(i + 1) * D_MODEL] for i in range(6)]
    cos, sin = _rope_tables()
    din, qd, kd, cd = _decay_tables()
    buckets = _bucket_tables()
    bias = _bias_build(rel_bias, buckets)
    dils = [d for _, d in ATT_GROUPS]

    h1 = _prenorm_fwd("prenorm1_fwd", x, norm1_g, sh1, sc1)

    qk_tn = 2 * RET_DK

    def rot_epi(acc, cs, sn, scale):
        half = RET_DK // 2
        outs = []
        for h0 in range(0, qk_tn, RET_DK):
            x1, x2 = acc[:, h0:h0 + half], acc[:, h0 + half:h0 + RET_DK]
            outs += [x1 * cs - x2 * sn, x1 * sn + x2 * cs]
        return (jnp.concatenate(outs, axis=-1) * scale,)

    qk_scale = jnp.concatenate([jnp.ones((1, RET_QK_W), F32),
                                jnp.full((1, RET_QK_W), RET_DK ** -0.5, F32)], axis=-1)
    rope_ex = [(cos, (TM, RET_DK // 2), lambda i, j, kk: (i, 0)),
               (sin, (TM, RET_DK // 2), lambda i, j, kk: (i, 0)),
               (qk_scale, (1, qk_tn), lambda i, j, kk: (0, j))]
    rest_sems, rest_shards, rest_fulls, rest_token = rest_gather
    behind = [rest_token]
    rv = _matmul("proj_rv", h1, w_in, "nn", SEQ, RET_V_W, D_MODEL, [BF16], b_off=OFF_V, tk=D_MODEL,
                 after=behind)[0]
    rg = _matmul("proj_rg", h1, w_in, "nn", SEQ, RET_V_W, D_MODEL, [BF16], b_off=OFF_G, tk=D_MODEL,
                 after=behind)[0]
    gates = _matmul("proj_gates", h1, w_in, "nn", SEQ, 2 * D_MODEL, D_MODEL, [BF16], b_off=OFF_GATE,
                    tn=512, tk=D_MODEL, after=behind)[0]
    aqkv = _matmul("proj_att", h1, w_in, "nn", SEQ, 9 * ATT_W, D_MODEL, [BF16], b_off=OFF_ATT,
                   tn=512, tk=D_MODEL, after=behind)[0]

    os_, ls_ = [], []
    for gi in range(3):
        o_g, l_g = _att_fwd(gi, dils[gi], aqkv, bias)
        os_.append(o_g)
        ls_.append(l_g)

    rqk = _matmul("proj_qk", h1, w_in, "nn", SEQ, 2 * RET_QK_W, D_MODEL, [BF16], b_off=OFF_Q,
                  tn=qk_tn, tk=D_MODEL, epilogue=rot_epi, extras=rope_ex, after=behind)[0]
    ro, states, gated = _retention_fwd(rqk, rv, rg, gn_g, gn_b, din, qd, kd, cd)
    rest_sems, rest_fulls, fwd_token = _gather_rest_forward(rest_sems, rest_shards, rest_fulls,
                                                            [gated, gates] + os_)
    att, att_b, lse = _combine(os_, ls_, after=[fwd_token])
    w_ret_out, w_att_out, w_o, w_ff1, w_ff2 = _gather_rest_end(rest_sems, rest_fulls, [att_b])
    ret_out = _matmul("ret_out", gated, w_ret_out, "nn", SEQ, D_MODEL, RET_V_W, [F32], tk=RET_V_W)[0]
    att_out, merged = _matmul("att_out", att_b, w_att_out, "nn", SEQ, D_MODEL, ATT_W, [F32, BF16],
                              tm=MERGE_TM, epilogue=_merge_fwd_epi,
                              extras=_merge_operands(gates, ret_out))

    def mix_epi(acc, xt, g, gain, sh, sc):
        x_new = xt + g * acc
        xhat, _ = _rms_parts(x_new)
        return x_new, acc, (xhat * gain) * (1.0 + sc) + sh

    x2, mix, h2 = _matmul("mix_out", merged, w_o, "nn", SEQ, D_MODEL, D_MODEL, [F32, BF16, BF16],
                          epilogue=mix_epi, extras=_row_operands(TM, [x], [g1, norm2_g, sh2, sc2]))

    def relu2_epi(acc):
        r = jnp.maximum(acc, 0.0)
        return r * r, r

    act, relu_u = _matmul("ff1", h2, w_ff1, "nn", SEQ, D_FF, D_MODEL, [BF16, BF16], tk=D_MODEL,
                          epilogue=relu2_epi)
    d_x3, d_y2, loss, d_gf, d_g2 = _matmul(
        "ff2", act, w_ff2, "nn", SEQ, D_MODEL, D_FF, [F32, BF16], tm=MERGE_TM, tk=2048, n_sums=3,
        epilogue=_loss_head_epi, extras=_row_operands(MERGE_TM, [x2, target], [g2, norm_f_g]))

    def relu2_bwd_epi(acc, rt):
        return (acc * (2.0 * rt.astype(F32)),)

    gw_ff2 = _matmul_tn_pair("ff2_dw", pos, act, d_y2, D_FF, D_MODEL, SEQ, D_FF // N_CHIPS,
                             tm=512, tn=1024, tk=SEQ)
    d_u = _matmul("ff2_dx", d_y2, w_ff2, "nt", SEQ, D_FF, D_MODEL, [BF16], epilogue=relu2_bwd_epi,
                  extras=[(relu_u, (TM, TN), lambda i, j, kk: (i, j))])[0]
    gw_ff1 = _matmul_tn_pair("ff1_dw", pos, h2, d_u, D_MODEL, D_FF, SEQ, D_MODEL,
                             tm=512, tn=1024, tk=SEQ)
    ffn = ["w_ff2", "w_ff1"]
    ffn_started = _ici_start("ici_start_ffn", ffn, [gw_ff2, gw_ff1])
    d_x2, d_mix, d_sh2, d_sc2, d_n2g, d_g1 = _matmul(
        "ff1_dx", d_u, w_ff1, "nt", SEQ, D_MODEL, D_FF, [F32, BF16], tm=MERGE_TM, tk=2048, n_sums=4,
        epilogue=_prenorm_bwd_epi, extras=_row_operands(MERGE_TM, [x2, d_x3], [norm2_g, sc2])
        + _row_operands(MERGE_TM, [mix], [g1]), after=[ffn_started[3]])
    gw_o = _matmul_tn_pair("mix_dw", pos, merged, d_mix, D_MODEL, D_MODEL, SEQ, D_MODEL // N_CHIPS,
                           tm=128, tn=1024, tk=2048)
    d_ret_out, d_att_out, d_ga, d_gb = _matmul(
        "mix_dx", d_mix, w_o, "nt", SEQ, D_MODEL, D_MODEL, [BF16] * 4, tm=MERGE_TM,
        epilogue=_merge_bwd_epi, extras=_merge_operands(gates, ret_out, att_out))

    gw_ret_out = _matmul_tn_pair("ret_out_dw", pos, gated, d_ret_out, RET_V_W, D_MODEL, SEQ,
                                 RET_V_W // N_CHIPS, tm=256, tn=1024, tk=SEQ)
    gw_att_out = _matmul_tn_pair("att_out_dw", pos, att_b, d_att_out, ATT_W, D_MODEL, SEQ, ATT_W,
                                 tm=256, tn=1024, tk=2048)
    mixer = ["w_o", "w_ret_out", "w_att_out"]
    mixer_started = _ici_start("ici_start_mixer", mixer, [gw_o, gw_ret_out, gw_att_out])
    d_gated = _matmul("ret_out_dx", d_ret_out, w_ret_out, "nt", SEQ, RET_V_W, D_MODEL, [BF16],
                      after=[mixer_started[3]])[0]
    d_att, dd = _matmul("att_out_dx", d_att_out, w_att_out, "nt", SEQ, ATT_W, D_MODEL, [F32, F32],
                        epilogue=_att_out_bwd_epi,
                        extras=[(att, (TM, ATT_W), lambda i, j, kk: (i, 0))], after=[mixer_started[3]])

    d_rqkv, d_rg, d_gn_g, d_gn_b = _retention_bwd(rqk, rv, states, d_gated, ro, rg, gn_g, gn_b,
                                                  din, qd, kd, cd, cos, sin)

    d_aqkv, dsbs = [], []
    for gi in range(3):
        dqkv, dsb = _att_bwd(gi, dils[gi], aqkv, d_att, lse, dd, bias)
        d_aqkv.append(dqkv)
        dsbs.append(dsb)
    d_rel_bias = _bias_grad(jnp.stack(dsbs), buckets)

    d_proj = ([(d_rqkv, False), (d_rg, False)] + [(t, True) for t in d_aqkv]
              + [(d_ga, False), (d_gb, False)])
    gw_in = _matmul_tn_pair("proj_dw", pos, h1, d_proj, D_MODEL, IN_COLS, SEQ, D_MODEL,
                            tm=512, tn=ATT_W, tk=SEQ)
    sems, (gw_in,), (land,), token = _ici_start("ici_start_w_in", ["w_in"], [gw_in])
    grad_x, d_sh1, d_sc1, d_n1g = _matmul(
        "proj_dx", d_proj, w_in, "nt", SEQ, D_MODEL, IN_COLS, [F32], tn=1024, tk=ATT_W, n_sums=3,
        epilogue=_prenorm_bwd_epi, extras=_row_operands(TM, [x, d_x2], [norm1_g, sc1]), after=[token])
    pending = (sems, land)

    names = ffn + mixer
    psums, got = _ici_wait("ici_wait_rest", names, list(ffn_started[0]) + list(mixer_started[0]),
                           list(ffn_started[1]) + list(mixer_started[1]),
                           list(ffn_started[2]) + list(mixer_started[2]), [grad_x])
    g_big = {n: _final_sum("final_" + n, pos, dict(BIG)[n], psums[i], got[i], SHARD[n])
             for i, n in enumerate(names)}
    d_mod = jnp.concatenate([d_sh1, d_sc1, d_g1, d_sh2, d_sc2, d_g2], axis=-1)
    small = dict(norm1_g=d_n1g, norm2_g=d_n2g, norm_f_g=d_gf, gn_g=d_gn_g, gn_b=d_gn_b,
                 rel_bias=d_rel_bias)
    return loss, grad_x, d_mod, small, g_big, (gw_in,) + pending


def _me():
    return lax.axis_index("x"), lax.axis_index("y"), lax.axis_index("c")


def _peer(x, y, c, mask):
    return (x ^ ((mask >> 2) & 1), y ^ ((mask >> 1) & 1), c ^ (mask & 1))


def _gather8(src_ref, dst_ref, send_sems, recv_sems):
    x, y, c = _me()
    me = 4 * x + 2 * y + c
    copies = []
    for mask in range(1, N_DEV):
        cp = pltpu.make_async_remote_copy(
            src_ref=src_ref, dst_ref=dst_ref.at[me], send_sem=send_sems.at[mask - 1],
            recv_sem=recv_sems.at[mask - 1], device_id=_peer(x, y, c, mask), device_id_type=MESH)
        cp.start()
        copies.append(cp)
    dst_ref[me] = src_ref[...]
    for cp in copies:
        cp.wait_recv()
    for cp in copies:
        cp.wait_send()


def _ada_fwd(c_in, w_ada, b_ada):
    ncol = ADA_COLS // N_CHIPS

    def body(c_ref, w_ref, b_ref, mod_ref, sc_ref, cbuf, cg, mbuf, mg, s1, r1, s2, r2):
        x, y, c = _me()
        me = 4 * x + 2 * y + c
        cv = c_ref[...]
        cbuf[...] = jnp.broadcast_to(cv * _sigmoid(cv), cbuf.shape)
        _gather8(cbuf, cg, s1, r1)
        rows = lax.broadcasted_iota(I32, (N_DEV, D_MODEL), 0)
        sc_all = jnp.zeros((N_DEV, D_MODEL), F32)
        for d in range(N_DEV):
            sc_all = jnp.where(rows == d, cg[d], sc_all)
        sc_ref[...] = sc_all
        mbuf[...] = jnp.dot(sc_all.astype(BF16), w_ref[...].astype(BF16), preferred_element_type=F32)
        _gather8(mbuf, mg, s2, r2)
        rowsel = lax.broadcasted_iota(I32, (N_DEV, ncol), 0) == me
        for k in range(N_CHIPS):
            blk = mg[2 * k]
            row = jnp.sum(jnp.where(rowsel, blk, 0.0), axis=0, keepdims=True)
            mod_ref[:, k * ncol:(k + 1) * ncol] = row + b_ref[:, k * ncol:(k + 1) * ncol]

    vm = pl.BlockSpec(memory_space=pltpu.VMEM)
    return pl.pallas_call(
        body, name="ada_fwd",
        in_specs=[vm, vm, vm], out_specs=[vm, vm],
        out_shape=[jax.ShapeDtypeStruct((1, ADA_COLS), F32), jax.ShapeDtypeStruct((N_DEV, D_MODEL), F32)],
        scratch_shapes=[
            pltpu.VMEM((8, D_MODEL), F32), pltpu.VMEM((N_DEV, 8, D_MODEL), F32),
            pltpu.VMEM((8, ncol), F32), pltpu.VMEM((N_DEV, 8, ncol), F32),
            pltpu.SemaphoreType.DMA((N_DEV - 1,)), pltpu.SemaphoreType.DMA((N_DEV - 1,)),
            pltpu.SemaphoreType.DMA((N_DEV - 1,)), pltpu.SemaphoreType.DMA((N_DEV - 1,)),
        ],
        compiler_params=pltpu.CompilerParams(vmem_limit_bytes=VMEM_LIMIT_V7X),
    )(c_in, w_ada, b_ada)


def _small_reduce(pack, sc_all, after=()):
    ncol = ADA_COLS // N_CHIPS

    def body(p_ref, sc_ref, *rest):
        tot_ref, gw_ref, pg, s1, r1 = rest[len(after):]
        x, y, _ = _me()
        chip = 2 * x + y
        _gather8(p_ref, pg, s1, r1)
        tot = pg[0]
        for d in range(1, N_DEV):
            tot = tot + pg[d]
        tot_ref[...] = tot
        rows = lax.broadcasted_iota(I32, (N_DEV, ncol), 0)
        dmod = jnp.zeros((N_DEV, ncol), F32)
        for k in range(N_CHIPS):
            part = jnp.zeros((N_DEV, ncol), F32)
            for d in range(N_DEV):
                part = jnp.where(rows == d, pg[d, :, k * ncol:(k + 1) * ncol][0:1, :], part)
            dmod = jnp.where(chip == k, part, dmod)
        gw_ref[...] = lax.dot_general(sc_ref[...].astype(BF16), dmod.astype(BF16), _TN,
                                      preferred_element_type=F32)

    vm = pl.BlockSpec(memory_space=pltpu.VMEM)
    return pl.pallas_call(
        body, name="small_reduce",
        in_specs=[vm, vm] + [pl.BlockSpec(memory_space=pl.ANY)] * len(after), out_specs=[vm, vm],
        out_shape=[jax.ShapeDtypeStruct((8, ADA_COLS), F32), jax.ShapeDtypeStruct((D_MODEL, ncol), F32)],
        scratch_shapes=[pltpu.VMEM((N_DEV, 8, ADA_COLS), F32),
                        pltpu.SemaphoreType.DMA((N_DEV - 1,)), pltpu.SemaphoreType.DMA((N_DEV - 1,))],
        compiler_params=pltpu.CompilerParams(vmem_limit_bytes=VMEM_LIMIT_V7X),
    )(pack, sc_all, *after)


BIG = (("w_in", 1), ("w_ret_out", 0), ("w_att_out", 1), ("w_o", 0), ("w_ff1", 1), ("w_ff2", 0))
SHARD = {"w_in": (D_MODEL, IN_COLS // N_CHIPS), "w_ret_out": (RET_V_W // N_CHIPS, D_MODEL),
         "w_att_out": (ATT_W, D_MODEL // N_CHIPS), "w_o": (D_MODEL // N_CHIPS, D_MODEL),
         "w_ff1": (D_MODEL, D_FF // N_CHIPS), "w_ff2": (D_FF // N_CHIPS, D_MODEL)}
_CHIP_FLIPS = ((1, 0), (0, 1), (1, 1))


def _region(ref, axis, chip, half, shard_shape):
    r, cw = shard_shape
    hr = r // 2
    if axis == 1:
        return ref.at[pl.ds(half * hr, hr), pl.ds(chip * cw, cw)]
    return ref.at[pl.ds(chip * r + half * hr, hr), :]


def _gather_weights(shards, n_remote):
    nw = len(BIG)
    shapes = [s.shape for s in shards]
    full_shapes = [(r, N_CHIPS * cw) if ax == 1 else (N_CHIPS * r, cw)
                   for (r, cw), (_, ax) in zip(shapes, BIG)]

    def body(*refs):
        ins, outs = refs[:nw], refs[nw:2 * nw]
        own = refs[2 * nw:3 * nw]
        from_ici, from_sib = refs[3 * nw:3 * nw + n_remote], refs[3 * nw + n_remote:3 * nw + 2 * n_remote]
        ld_sem, st_sem, s_ici, r_ici, s_d2d, r_d2d, st_a, st_b = refs[3 * nw + 2 * n_remote:]
        x, y, c = _me()
        chip = 2 * x + y
        sib = (x, y, 1 - c)
        loads = [pltpu.make_async_copy(ins[i], own[i], ld_sem.at[i]) for i in range(nw)]
        for cp in loads:
            cp.start()
        pending, first = [], []
        for i, (_, ax) in enumerate(BIG):
            r, cw = shapes[i]
            hr = r // 2
            loads[i].wait()
            dst = outs[i].at[:, pl.ds(chip * cw, cw)] if ax == 1 else outs[i].at[pl.ds(chip * r, r), :]
            cp = pltpu.make_async_copy(own[i], dst, st_sem.at[i])
            cp.start()
            pending.append(cp)
            for j, (fx, fy) in enumerate(_CHIP_FLIPS if i < n_remote else ()):
                rc = pltpu.make_async_remote_copy(
                    src_ref=own[i].at[pl.ds(c * hr, hr), :], dst_ref=from_ici[i].at[j],
                    send_sem=s_ici.at[j * nw + i], recv_sem=r_ici.at[j * nw + i],
                    device_id=(x ^ fx, y ^ fy, c), device_id_type=MESH)
                rc.start()
                first.append((j, i, rc))
        passed = []
        for j, i, rc in first:
            fx, fy = _CHIP_FLIPS[j]
            src_chip = 2 * (x ^ fx) + (y ^ fy)
            ax = BIG[i][1]
            rc.wait_recv()
            fw = pltpu.make_async_remote_copy(
                src_ref=from_ici[i].at[j], dst_ref=from_sib[i].at[j], send_sem=s_d2d.at[j * nw + i],
                recv_sem=r_d2d.at[j * nw + i], device_id=sib, device_id_type=MESH)
            fw.start()
            passed.append((j, i, src_chip, fw))
            st = pltpu.make_async_copy(from_ici[i].at[j], _region(outs[i], ax, src_chip, c, shapes[i]),
                                       st_a.at[j * nw + i])
            st.start()
            pending.append(st)
        for j, i, src_chip, fw in passed:
            fw.wait_recv()
            st = pltpu.make_async_copy(from_sib[i].at[j],
                                       _region(outs[i], BIG[i][1], src_chip, 1 - c, shapes[i]),
                                       st_b.at[j * nw + i])
            st.start()
            pending.append(st)
        for _, _, rc in first:
            rc.wait_send()
        for _, _, _, fw in passed:
            fw.wait_send()
        for cp in pending:
            cp.wait()

    hbm = pl.BlockSpec(memory_space=pl.ANY)
    halves = [pltpu.VMEM((3, r // 2, cw), BF16) for r, cw in shapes[:n_remote]]
    return pl.pallas_call(
        body, name="gather_weights",
        in_specs=[hbm] * nw, out_specs=[hbm] * nw,
        out_shape=[jax.ShapeDtypeStruct(fs, BF16) for fs in full_shapes],
        scratch_shapes=[pltpu.VMEM(sh, BF16) for sh in shapes] + halves + halves
        + [pltpu.SemaphoreType.DMA((nw,)), pltpu.SemaphoreType.DMA((nw,))]
        + [pltpu.SemaphoreType.DMA((3 * nw,))] * 6,
        compiler_params=pltpu.CompilerParams(vmem_limit_bytes=VMEM_LIMIT_V7X),
    )(*shards)


REST = BIG[1:]
_SIDE_EFFECTS = pltpu.CompilerParams(has_side_effects=pltpu.SideEffectType.DATAFLOW_SIDE_EFFECTING)
_ANY_SPEC = pl.BlockSpec(memory_space=pl.ANY)


def _rest_ici_copies(shard_refs, full_refs, sems):
    x, y, c = _me()
    chip = 2 * x + y
    n = 3 * len(REST)
    copies = []
    for i, (name, ax) in enumerate(REST):
        hr = SHARD[name][0] // 2
        for j, (fx, fy) in enumerate(_CHIP_FLIPS):
            copies.append(pltpu.make_async_remote_copy(
                src_ref=shard_refs[i].at[pl.ds(c * hr, hr), :],
                dst_ref=_region(full_refs[i], ax, chip, c, SHARD[name]),
                send_sem=sems[3 * i + j], recv_sem=sems[n + 3 * i + j],
                device_id=(x ^ fx, y ^ fy, c), device_id_type=MESH))
    return copies


def _rest_d2d_copies(full_refs, sems):
    x, y, c = _me()
    n = 3 * len(REST)
    copies = []
    for i, (name, ax) in enumerate(REST):
        for j, (fx, fy) in enumerate(_CHIP_FLIPS):
            reg = _region(full_refs[i], ax, 2 * (x ^ fx) + (y ^ fy), c, SHARD[name])
            copies.append(pltpu.make_async_remote_copy(
                src_ref=reg, dst_ref=reg, send_sem=sems[3 * i + j], recv_sem=sems[n + 3 * i + j],
                device_id=(x, y, 1 - c), device_id_type=MESH))
    return copies


def _gather_rest_start(shards, fulls, after):
    nr, ns, na = len(REST), 6 * len(REST), len(after)

    def body(*refs):
        for cp in _rest_ici_copies(refs[:nr], refs[nr:2 * nr], refs[2 * nr + na:2 * nr + na + ns]):
            cp.start()
        token = refs[-1]
        token[...] = jnp.zeros_like(token)

    hbm = lambda a: pltpu.HBM(a.shape, a.dtype)
    res = pl.pallas_call(
        body, name="gather_rest_start",
        out_shape=(pltpu.SemaphoreType.DMA(()),) * ns + tuple(hbm(a) for a in shards + fulls)
        + (jax.ShapeDtypeStruct((8, 128), F32),),
        in_specs=(_HBM_SPEC,) * (2 * nr) + (_ANY_SPEC,) * na,
        out_specs=(_SEM_SPEC,) * ns + (_HBM_SPEC,) * (2 * nr) + (pl.BlockSpec(memory_space=pltpu.VMEM),),
        input_output_aliases={k: ns + k for k in range(2 * nr)}, compiler_params=_SIDE_EFFECTS,
    )(*[pltpu.with_memory_space_constraint(a, pltpu.HBM) for a in shards + fulls], *after)
    return res[:ns], res[ns:ns + nr], res[ns + nr:ns + 2 * nr], res[-1]


def _gather_rest_forward(sems, shards, fulls, after):
    nr, ns = len(REST), 6 * len(REST)

    def body(*refs):
        shard_refs, full_refs, old = refs[:nr], refs[nr:2 * nr], refs[2 * nr:2 * nr + ns]
        new = refs[2 * nr + ns + len(after):2 * nr + 2 * ns + len(after)]
        for cp in _rest_ici_copies(shard_refs, full_refs, old):
            cp.wait_send()
            cp.wait_recv()
        for cp in _rest_d2d_copies(full_refs, new):
            cp.start()
        token = refs[-1]
        token[...] = jnp.zeros_like(token)

    res = pl.pallas_call(
        body, name="gather_rest_forward",
        out_shape=(pltpu.SemaphoreType.DMA(()),) * ns + tuple(pltpu.HBM(a.shape, a.dtype) for a in fulls)
        + (jax.ShapeDtypeStruct((8, 128), F32),),
        in_specs=(_HBM_SPEC,) * (2 * nr) + (_SEM_SPEC,) * ns + (_ANY_SPEC,) * len(after),
        out_specs=(_SEM_SPEC,) * ns + (_HBM_SPEC,) * nr + (pl.BlockSpec(memory_space=pltpu.VMEM),),
        input_output_aliases={nr + k: ns + k for k in range(nr)}, compiler_params=_SIDE_EFFECTS,
    )(*shards, *fulls, *sems, *after)
    return res[:ns], res[ns:ns + nr], res[-1]


def _gather_rest_end(sems, fulls, after):
    nr, ns = len(REST), 6 * len(REST)

    def body(*refs):
        for cp in _rest_d2d_copies(refs[:nr], refs[nr:nr + ns]):
            cp.wait_send()
            cp.wait_recv()

    return pl.pallas_call(
        body, name="gather_rest_end",
        out_shape=tuple(pltpu.HBM(a.shape, a.dtype) for a in fulls),
        in_specs=(_HBM_SPEC,) * nr + (_SEM_SPEC,) * ns + (_ANY_SPEC,) * len(after),
        out_specs=(_HBM_SPEC,) * nr,
        input_output_aliases={k: k for k in range(nr)}, compiler_params=_SIDE_EFFECTS,
    )(*fulls, *sems, *after)


def _adam_update(w, g, m, v):
    mn = ADAM_B1 * m + (1.0 - ADAM_B1) * g
    vn = ADAM_B2 * v + (1.0 - ADAM_B2) * (g * g)
    m_hat = mn / (1.0 - ADAM_B1 ** ADAM_STEP)
    v_hat = vn / (1.0 - ADAM_B2 ** ADAM_STEP)
    return -ADAM_LR * (m_hat / (jnp.sqrt(v_hat) + ADAM_EPS) + ADAM_WD * w), mn, vn


def _final_sum(name, pos, axis, psum, recv, shard_shape, after=(), tr=128):
    r, cw = shard_shape
    hr = r // 2
    tr = min(tr, hr)
    nt = hr // tr
    n_after = len(after)

    def kern(pos_ref, p_ref, r_ref, *rest):
        g_ref, send_buf, land_buf, s_sem, r_sem = rest[n_after:]
        p, t = pl.program_id(0), pl.program_id(1)
        sib = _sibling()

        def copy(i):
            return pltpu.make_async_remote_copy(
                src_ref=send_buf.at[i], dst_ref=land_buf.at[i], send_sem=s_sem.at[i],
                recv_sem=r_sem.at[i], device_id=sib, device_id_type=MESH)

        @pl.when(p == 0)
        def _():
            tot = p_ref[...].astype(F32)
            for j in range(3):
                tot = tot + r_ref[j].astype(F32)
            send_buf[t] = tot
            copy(t).start()
            g_ref[...] = tot

        @pl.when(p == 1)
        def _():
            copy(t).wait_recv()
            g_ref[...] = land_buf[t]

        @pl.when(jnp.logical_and(p == 1, t == nt - 1))
        def _():
            for i in range(nt):
                copy(i).wait_send()

    def shard_rows(p, t, pos_ref):
        return (jnp.where(p == 0, pos_ref[0], 1 - pos_ref[0]) * nt + t, 0)

    def own_part(p, t, pos_ref):
        tt = jnp.where(p == 0, t, nt - 1)
        return (tt, pos_ref[1]) if axis == 1 else (pos_ref[1] * nt + tt, 0)

    grid_spec = pltpu.PrefetchScalarGridSpec(
        num_scalar_prefetch=1, grid=(2, nt),
        in_specs=[pl.BlockSpec((tr, cw), own_part),
                  pl.BlockSpec((3, tr, cw), lambda p, t, pos_ref: (0, jnp.where(p == 0, t, nt - 1), 0))]
        + [pl.BlockSpec(memory_space=pl.ANY)] * n_after,
        out_specs=pl.BlockSpec((tr, cw), shard_rows),
        scratch_shapes=[pltpu.VMEM((nt, tr, cw), F32), pltpu.VMEM((nt, tr, cw), F32),
                        pltpu.SemaphoreType.DMA((nt,)), pltpu.SemaphoreType.DMA((nt,))])
    return pl.pallas_call(
        kern, name=name, grid_spec=grid_spec, out_shape=jax.ShapeDtypeStruct((r, cw), F32),
        compiler_params=_cparams(("arbitrary", "arbitrary")),
    )(pos, psum, recv, *after)


def _adamw(name, w, g, m, v):
    r, cw = w.shape
    tr = min(r, 128)

    def kern(w_ref, g_ref, m_ref, v_ref, go_ref, d_ref, nm_ref, nv_ref):
        gv = g_ref[...]
        go_ref[...] = gv
        d_ref[...], nm_ref[...], nv_ref[...] = _adam_update(w_ref[...], gv, m_ref[...], v_ref[...])

    spec = pl.BlockSpec((tr, cw), lambda i: (i, 0))
    return pl.pallas_call(
        kern, name=name, grid=(r // tr,), in_specs=[spec] * 4, out_specs=[spec] * 4,
        out_shape=[jax.ShapeDtypeStruct((r, cw), F32)] * 4, compiler_params=_cparams(("parallel",)),
    )(w, g, m, v)


_PACK_W = ADA_COLS
_NB = REL_BUCKETS * N_ATT_HEADS
_SMALL_SLOTS = {
    "b_ada": (0, 0, ADA_COLS),
    "norm1_g": (1, 0, D_MODEL), "norm2_g": (1, D_MODEL, D_MODEL), "norm_f_g": (1, 2 * D_MODEL, D_MODEL),
    "ret_gn_g": (1, 3 * D_MODEL, RET_V_W),
    "ret_gn_b": (2, 0, RET_V_W), "rel_bias": (2, RET_V_W, _NB), "loss": (2, RET_V_W + 512, 128),
}


def _pack_small(vals):
    rows = []
    for r in range(8):
        items = sorted([(off, n) for n, (rr, off, _) in _SMALL_SLOTS.items() if rr == r and n in vals])
        parts, pos = [], 0
        for off, n in items:
            if off > pos:
                parts.append(jnp.zeros((1, off - pos), F32))
            parts.append(vals[n].reshape(1, -1).astype(F32))
            pos = off + _SMALL_SLOTS[n][2]
        if pos < _PACK_W:
            parts.append(jnp.zeros((1, _PACK_W - pos), F32))
        rows.append(jnp.concatenate(parts, axis=-1))
    return jnp.concatenate(rows, axis=0)


def _unpack_small(pack, name):
    r, off, wd = _SMALL_SLOTS[name]
    return pack[r:r + 1, off:off + wd]


def kernel(x, c, w_ada, b_ada, norm1_g, w_in, rel_bias, ret_gn_g, ret_gn_b, w_ret_out, w_att_out, w_o, norm2_g, w_ff1, w_ff2, norm_f_g, loss_target, m_w_ada, m_b_ada, m_norm1_g, m_w_in, m_rel_bias, m_ret_gn_g, m_ret_gn_b, m_w_ret_out, m_w_att_out, m_w_o, m_norm2_g, m_w_ff1, m_w_ff2, m_norm_f_g, v_w_ada, v_b_ada, v_norm1_g, v_w_in, v_rel_bias, v_ret_gn_g, v_ret_gn_b, v_w_ret_out, v_w_att_out, v_w_o, v_norm2_g, v_w_ff1, v_w_ff2, v_norm_f_g):
    given = dict(locals())
    big_names = [n for n, _ in BIG]
    shard_w = {n: given[n][0] for n in big_names}
    assert all(shard_w[n].shape == SHARD[n] for n in big_names)

    shards_bf = [shard_w[n].astype(BF16) for n in big_names]
    full = _gather_weights(shards_bf, 1)
    mod, sc_all = _ada_fwd(c, w_ada[0], b_ada)
    rest_gather = _gather_rest_start(shards_bf[1:], list(full[1:]), [mod])
    pos = _where_am_i()

    loss, grad_x, d_mod, small, g_big, pending = _local_step(
        pos, x[0], loss_target[0], mod, norm1_g, norm2_g, norm_f_g.reshape(1, -1), rel_bias, ret_gn_g,
        ret_gn_b, full[0], rest_gather)

    pack_g = _pack_small(dict(b_ada=d_mod, norm1_g=small["norm1_g"], norm2_g=small["norm2_g"],
                              norm_f_g=small["norm_f_g"], ret_gn_g=small["gn_g"], ret_gn_b=small["gn_b"],
                              rel_bias=small["rel_bias"], loss=loss[:, :128]))
    tot, g_w_ada = _small_reduce(pack_g, sc_all, after=list(g_big.values()))

    small_names = ["b_ada", "norm1_g", "rel_bias", "ret_gn_g", "ret_gn_b", "norm2_g", "norm_f_g"]
    pack_w = _pack_small({n: given[n] for n in small_names})
    pack_m = _pack_small({n: given["m_" + n] for n in small_names})
    pack_v = _pack_small({n: given["v_" + n] for n in small_names})
    _, sd, sm, sv = _adamw("adamw_small", pack_w, tot, pack_m, pack_v)

    grads, deltas, new_m, new_v = {}, {}, {}, {}
    for n in small_names:
        shp = given[n].shape
        grads[n] = _unpack_small(tot, n).reshape(shp)
        deltas[n] = _unpack_small(sd, n).reshape(shp)
        new_m[n] = _unpack_small(sm, n).reshape(shp)
        new_v[n] = _unpack_small(sv, n).reshape(shp)
    g_big["w_ada"] = g_w_ada
    for n in ["w_ada"] + big_names[1:] + big_names[:1]:
        if n == "w_in":
            gw_in, sems, land = pending
            done = [tot, sd] + [deltas[k] for k in ["w_ada"] + big_names[1:]]
            (gw_in,), (got,) = _ici_wait("ici_wait_w_in", [n], sems, [gw_in], [land], done)
            g_big[n] = _final_sum("final_w_in", pos, 1, gw_in, got, SHARD[n])
        g, d, nm, nv = _adamw("adamw_" + n, given[n][0], g_big[n], given["m_" + n][0], given["v_" + n][0])
        grads[n], deltas[n], new_m[n], new_v[n] = g[None], d[None], nm[None], nv[None]

    order = ["w_ada", "b_ada", "norm1_g", "w_in", "rel_bias", "ret_gn_g", "ret_gn_b", "w_ret_out",
             "w_att_out", "w_o", "norm2_g", "w_ff1", "w_ff2", "norm_f_g"]
    loss_out = _unpack_small(tot, "loss")[0, 0]
    return (loss_out, grad_x[None], *[grads[n] for n in order], *[deltas[n] for n in order],
            *[new_m[n] for n in order], *[new_v[n] for n in order])
```

```python
import functools
import math

import jax
import jax.numpy as jnp
import numpy as np
from jax import lax
from jax.experimental import pallas as pl
from jax.experimental.pallas import tpu as pltpu

F32 = jnp.float32
BF16 = jnp.bfloat16
I32 = jnp.int32

SEQ = 2048
D_MODEL = 1024
RET_HEADS = 4
RET_DK = 256
RET_DV = 512
RET_CHUNK = 128
RET_SUB = 2
RET_QK_W = RET_HEADS * RET_DK
RET_V_W = RET_HEADS * RET_DV
ATT_GROUPS = ((128, 1), (512, 4), (2048, 16))
ATT_HPG = 4
ATT_DH = 128
ATT_W = ATT_HPG * ATT_DH
ATT_BLK = 128
N_BLK = SEQ // ATT_BLK
REL_BUCKETS = 32
REL_MAX_DIST = 2048
N_ATT_HEADS = 12
D_FF = 4 * D_MODEL
RMS_EPS = 1e-6
GN_EPS = 1e-5
ROPE_BASE = 10000.0
IN_COLS = 2 * RET_QK_W + 2 * RET_V_W + 9 * ATT_W + 2 * D_MODEL
OFF_Q, OFF_K, OFF_V, OFF_G = 0, RET_QK_W, 2 * RET_QK_W, 2 * RET_QK_W + RET_V_W
OFF_ATT = 2 * RET_QK_W + 2 * RET_V_W
OFF_GATE = OFF_ATT + 9 * ATT_W
N_CHIPS = 4
N_DEV = 8
ADA_COLS = 6 * D_MODEL

ADAM_LR = 0.001
ADAM_B1 = 0.9
ADAM_B2 = 0.999
ADAM_EPS = 1e-08
ADAM_WD = 0.01
ADAM_STEP = 10

VMEM_LIMIT_V7X = 56 * 1024 * 1024
MESH = pl.DeviceIdType.MESH


def _cparams(sem):
    return pltpu.CompilerParams(dimension_semantics=sem, vmem_limit_bytes=VMEM_LIMIT_V7X)


def _sigmoid(v):
    return 1.0 / (1.0 + jnp.exp(-v))


def _rowmap(name, body, row_ins, bcast_ins, row_outs, sum_outs=(), tm=256, after=()):
    m = row_ins[0].shape[0]
    n_in = len(row_ins) + len(bcast_ins)
    n_ro = len(row_outs)

    def kern(*refs):
        vals = [r[...] for r in refs[:n_in]]
        res = body(*vals)
        if not isinstance(res, (tuple, list)):
            res = (res,)
        outs = refs[n_in + len(after):]
        for r, v in zip(outs[:n_ro], res[:n_ro]):
            r[...] = v.astype(r.dtype)
        if sum_outs:
            @pl.when(pl.program_id(0) == 0)
            def _():
                for r in outs[n_ro:]:
                    r[...] = jnp.zeros_like(r)
            for r, v in zip(outs[n_ro:], res[n_ro:]):
                r[...] += v

    in_specs = [pl.BlockSpec((tm, a.shape[1]), lambda i: (i, 0)) for a in row_ins]
    in_specs += [pl.BlockSpec(a.shape, lambda i: (0, 0)) for a in bcast_ins]
    in_specs += [pl.BlockSpec(memory_space=pl.ANY)] * len(after)
    out_specs = [pl.BlockSpec((tm, n), lambda i: (i, 0)) for n, _ in row_outs]
    out_specs += [pl.BlockSpec((1, n), lambda i: (0, 0)) for n in sum_outs]
    out_shape = [jax.ShapeDtypeStruct((m, n), dt) for n, dt in row_outs]
    out_shape += [jax.ShapeDtypeStruct((1, n), F32) for n in sum_outs]
    return pl.pallas_call(
        kern, name=name, grid=(m // tm,), in_specs=in_specs, out_specs=out_specs,
        out_shape=out_shape, compiler_params=_cparams(("arbitrary",)),
    )(*row_ins, *bcast_ins, *after)


TM, TN = 1024, 1024


def _piece_chunks(piece, width):
    arr, stacked = piece
    return arr.shape[0] if stacked else arr.shape[1] // width


def _piece_spec(piece, rows, width, start, row_of, chunk_of):
    arr, stacked = piece
    last = _piece_chunks(piece, width) - 1

    def local(*ids):
        return jnp.clip(chunk_of(*ids) - start, 0, last)

    def row(*ids):
        rel = chunk_of(*ids) - start
        return jnp.where(jnp.logical_and(rel >= 0, rel <= last), row_of(*ids), 0)

    if stacked:
        return pl.BlockSpec((None, rows, width), lambda *ids: (local(*ids), row(*ids), 0))
    return pl.BlockSpec((rows, width), lambda *ids: (row(*ids), local(*ids)))


def _piece_starts(pieces, width):
    return [sum(_piece_chunks(p, width) for p in pieces[:q]) for q in range(len(pieces))]


def _matmul(name, a, b, kind, m, n, k, outs, *, b_off=0, tm=TM, tn=TN, tk=1024,
            epilogue=None, extras=(), after=(), n_sums=0):
    tm, tn, tk = min(tm, m), min(tn, n), min(tk, k)
    nk = k // tk
    pieces = a if isinstance(a, list) else [(a, False)]
    starts = _piece_starts(pieces, tk)
    if kind == "nn":
        a_specs = [pl.BlockSpec((tm, tk), lambda i, j, kk: (i, kk))]
        b_spec = pl.BlockSpec((tk, tn), lambda i, j, kk: (kk, b_off // tn + j))
        dn = (((1,), (0,)), ((), ()))
    elif kind == "nt":
        a_specs = [_piece_spec(p, tm, tk, st, lambda i, j, kk: i, lambda i, j, kk: kk)
                   for p, st in zip(pieces, starts)]
        b_spec = pl.BlockSpec((tn, tk), lambda i, j, kk: (j, b_off // tk + kk))
        dn = (((1,), (1,)), ((), ()))
    else:
        a_specs = [pl.BlockSpec((tk, tm), lambda i, j, kk: (kk, i))]
        b_spec = pl.BlockSpec((tk, tn), lambda i, j, kk: (kk, j))
        dn = (((0,), (0,)), ((), ()))
    n_a, n_ex, n_out = len(pieces), len(extras), len(outs)
    if epilogue is None:
        epilogue = lambda acc: (acc,)

    assert n_sums == 0 or tn == n

    def finish(acc, ex_refs, out_refs, first_rows):
        res = epilogue(acc, *[r[...] for r in ex_refs])
        for r, v in zip(out_refs[:n_out], res[:n_out]):
            r[...] = v.astype(r.dtype)
        for r, v in zip(out_refs[n_out:], res[n_out:]):
            @pl.when(first_rows)
            def _(r=r, v=v):
                r[...] = v

            @pl.when(jnp.logical_not(first_rows))
            def _(r=r, v=v):
                r[...] += v

    n_in = n_a + 1 + n_ex + len(after)

    def kern(*refs):
        a_refs, b_ref = refs[:n_a], refs[n_a]
        ex_refs = refs[n_a + 1:n_a + 1 + n_ex]
        out_refs = refs[n_in:n_in + n_out + n_sums]
        first_rows, kk = pl.program_id(0) == 0, pl.program_id(2)
        dot = lambda a_ref: lax.dot_general(a_ref[...], b_ref[...], dn, preferred_element_type=F32)
        if nk == 1:
            finish(dot(a_refs[0]), ex_refs, out_refs, first_rows)
            return
        acc_ref = refs[n_in + n_out + n_sums]
        if n_a == 1:
            part = dot(a_refs[0])

            @pl.when(kk == 0)
            def _():
                acc_ref[...] = part

            @pl.when(kk > 0)
            def _():
                acc_ref[...] += part
        else:
            @pl.when(kk == 0)
            def _():
                acc_ref[...] = jnp.zeros_like(acc_ref)

            for q in range(n_a):
                @pl.when(jnp.logical_and(kk >= starts[q], kk < starts[q] + _piece_chunks(pieces[q], tk)))
                def _(q=q):
                    acc_ref[...] += dot(a_refs[q])

        @pl.when(kk == nk - 1)
        def _():
            finish(acc_ref[...], ex_refs, out_refs, first_rows)

    in_specs = a_specs + [b_spec] + [pl.BlockSpec(bs, im) for _, bs, im in extras]
    in_specs += [pl.BlockSpec(memory_space=pl.ANY)] * len(after)
    sem = ("arbitrary",) * 3 if n_sums else ("parallel", "parallel", "arbitrary")
    return pl.pallas_call(
        kern, name=name, grid=(m // tm, n // tn, nk), in_specs=in_specs,
        out_specs=[pl.BlockSpec((tm, tn), lambda i, j, kk: (i, j)) for _ in outs]
        + [pl.BlockSpec((1, tn), lambda i, j, kk: (0, 0))] * n_sums,
        out_shape=[jax.ShapeDtypeStruct((m, n), dt) for dt in outs]
        + [jax.ShapeDtypeStruct((1, n), F32)] * n_sums,
        scratch_shapes=[] if nk == 1 else [pltpu.VMEM((tm, tn), F32)],
        compiler_params=_cparams(sem),
    )(*[p[0] for p in pieces], b, *[e[0] for e in extras], *after)


def _ici_copies(psum_ref, recv_ref, s_sem, r_sem, axis, shard_shape):
    x, y, c = _me()
    hr, cw = shard_shape[0] // 2, shard_shape[1]
    pick = lambda sems, j: sems[j] if isinstance(sems, (list, tuple)) else sems.at[j]
    copies = []
    for j, (fx, fy) in enumerate(_CHIP_FLIPS):
        chip = 2 * (x ^ fx) + (y ^ fy)
        src = psum_ref.at[:, pl.ds(chip * cw, cw)] if axis == 1 else psum_ref.at[pl.ds(chip * hr, hr), :]
        copies.append(pltpu.make_async_remote_copy(
            src_ref=src, dst_ref=recv_ref.at[j], send_sem=pick(s_sem, j), recv_sem=pick(r_sem, j),
            device_id=(x ^ fx, y ^ fy, c), device_id_type=MESH))
    return copies


_HBM_SPEC = pl.BlockSpec(memory_space=pltpu.HBM)
_SEM_SPEC = pl.BlockSpec(memory_space=pltpu.SEMAPHORE)


def _split_ici_copies(names, p_refs, land_refs, sems):
    copies = []
    for i, n in enumerate(names):
        copies += _ici_copies(p_refs[i], land_refs[i], list(sems[6 * i:6 * i + 3]),
                              list(sems[6 * i + 3:6 * i + 6]), dict(BIG)[n], SHARD[n])
    return copies


def _ici_start(name, names, psums):
    nw, ns = len(names), 6 * len(names)
    lands = [lax.empty((3, SHARD[n][0] // 2, SHARD[n][1]), BF16) for n in names]

    def body(*refs):
        for cp in _split_ici_copies(names, refs[:nw], refs[nw:2 * nw], refs[2 * nw:2 * nw + ns]):
            cp.start()
        token = refs[-1]
        token[...] = jnp.zeros_like(token)

    res = pl.pallas_call(
        body, name=name,
        out_shape=(pltpu.SemaphoreType.DMA(()),) * ns
        + tuple(pltpu.HBM(a.shape, BF16) for a in list(psums) + lands)
        + (jax.ShapeDtypeStruct((8, 128), F32),),
        in_specs=(_HBM_SPEC,) * (2 * nw),
        out_specs=(_SEM_SPEC,) * ns + (_HBM_SPEC,) * (2 * nw) + (pl.BlockSpec(memory_space=pltpu.VMEM),),
        input_output_aliases={k: ns + k for k in range(2 * nw)},
        compiler_params=pltpu.CompilerParams(has_side_effects=pltpu.SideEffectType.DATAFLOW_SIDE_EFFECTING),
    )(*[pltpu.with_memory_space_constraint(a, pltpu.HBM) for a in list(psums) + lands])
    return res[:ns], res[ns:ns + nw], res[ns + nw:ns + 2 * nw], res[-1]


def _ici_wait(name, names, sems, p_thru, land_thru, after):
    nw, ns = len(names), 6 * len(names)

    def body(*refs):
        for cp in _split_ici_copies(names, refs[:nw], refs[nw:2 * nw], refs[2 * nw:2 * nw + ns]):
            cp.wait_send()
            cp.wait_recv()

    res = pl.pallas_call(
        body, name=name,
        out_shape=tuple(pltpu.HBM(a.shape, BF16) for a in list(p_thru) + list(land_thru)),
        in_specs=(_HBM_SPEC,) * (2 * nw) + (_SEM_SPEC,) * ns + (pl.BlockSpec(memory_space=pl.ANY),) * len(after),
        out_specs=(_HBM_SPEC,) * (2 * nw), input_output_aliases={k: k for k in range(2 * nw)},
        compiler_params=pltpu.CompilerParams(has_side_effects=pltpu.SideEffectType.DATAFLOW_SIDE_EFFECTING),
    )(*p_thru, *land_thru, *sems, *after)
    return res[:nw], res[nw:]


def _where_am_i():
    x, y, c = _me()
    return jnp.stack([c, 2 * x + y]).astype(I32)


def _sibling():
    x, y, c = _me()
    return (x, y, 1 - c)


N_SEND_SLOTS = 2


def _matmul_tn_pair(name, pos, a, b, m, n, k, shard_rows, *, tm, tn, tk):
    hr = shard_rows // 2
    tm, tn, tk = min(tm, hr), min(tn, n), min(tk, k)
    tph = hr // tm
    nt, nj, nk = (m // 2) // tm, n // tn, k // tk
    n_tiles = nt * nj

    def row_block(p, t, pos_ref):
        half = jnp.where(p == 0, 1 - pos_ref[0], pos_ref[0])
        return (t // tph) * (2 * tph) + half * tph + t % tph

    pieces = b if isinstance(b, list) else [(b, False)]
    starts = _piece_starts(pieces, tn)
    n_b = len(pieces)

    def kern(pos_ref, a_ref, *rest):
        b_refs = rest[:n_b]
        o_ref, acc_ref, send_buf, land_buf, s_sem, r_sem = rest[n_b:]
        p, t, j, kk = pl.program_id(0), pl.program_id(1), pl.program_id(2), pl.program_id(3)
        idx = t * nj + j
        sib = _sibling()

        def copy(i):
            return pltpu.make_async_remote_copy(
                src_ref=send_buf.at[i % N_SEND_SLOTS], dst_ref=land_buf.at[i], send_sem=s_sem.at[i],
                recv_sem=r_sem.at[i], device_id=sib, device_id_type=MESH)

        @pl.when(kk == 0)
        def _():
            acc_ref[...] = jnp.zeros_like(acc_ref)

        for q in range(n_b):
            @pl.when(jnp.logical_and(j >= starts[q], j < starts[q] + _piece_chunks(pieces[q], tn)))
            def _(q=q):
                acc_ref[...] += lax.dot_general(a_ref[...], b_refs[q][...], _TN, preferred_element_type=F32)

        @pl.when(jnp.logical_and(kk == nk - 1, p == 0))
        def _():
            @pl.when(idx >= N_SEND_SLOTS)
            def _():
                copy(idx - N_SEND_SLOTS).wait_send()

            send_buf[idx % N_SEND_SLOTS] = acc_ref[...].astype(BF16)
            copy(idx).start()

        @pl.when(jnp.logical_and(kk == nk - 1, p == 1))
        def _():
            copy(idx).wait_recv()
            o_ref[...] = (acc_ref[...] + land_buf[idx].astype(F32)).astype(BF16)

        @pl.when(jnp.logical_and(jnp.logical_and(p == 1, idx == n_tiles - 1), kk == nk - 1))
        def _():
            for i in range(max(n_tiles - N_SEND_SLOTS, 0), n_tiles):
                copy(i).wait_send()

    grid_spec = pltpu.PrefetchScalarGridSpec(
        num_scalar_prefetch=1, grid=(2, nt, nj, nk),
        in_specs=[pl.BlockSpec((tk, tm), lambda p, t, j, kk, pos_ref: (kk, row_block(p, t, pos_ref)))]
        + [_piece_spec(pc, tk, tn, st, lambda p, t, j, kk, pos_ref: kk, lambda p, t, j, kk, pos_ref: j)
           for pc, st in zip(pieces, starts)],
        out_specs=pl.BlockSpec((tm, tn), lambda p, t, j, kk, pos_ref: (p * t, p * j)),
        scratch_shapes=[pltpu.VMEM((tm, tn), F32), pltpu.VMEM((N_SEND_SLOTS, tm, tn), BF16),
                        pltpu.VMEM((n_tiles, tm, tn), BF16),
                        pltpu.SemaphoreType.DMA((n_tiles,)), pltpu.SemaphoreType.DMA((n_tiles,))])
    return pl.pallas_call(
        kern, name=name, grid_spec=grid_spec, out_shape=jax.ShapeDtypeStruct((m // 2, n), BF16),
        compiler_params=_cparams(("arbitrary",) * 4),
    )(pos, a, *[pc[0] for pc in pieces])


def _rope_tables():
    half = RET_DK // 2
    f32 = np.float32
    inv = np.power(f32(ROPE_BASE), -np.arange(half, dtype=f32) / f32(half)).astype(f32)
    ang = (np.arange(SEQ, dtype=f32)[:, None] * inv[None, :]).astype(f32)
    return jnp.asarray(np.cos(ang).astype(f32)), jnp.asarray(np.sin(ang).astype(f32))


def _decay_tables():
    c = RET_CHUNK
    f32 = np.float32
    log_g = np.log1p(-np.power(f32(2.0), f32(-5.0) - np.arange(RET_HEADS, dtype=f32))).astype(f32)
    idx = np.arange(c, dtype=f32)
    rel = idx[:, None] - idx[None, :]
    din = np.where(rel >= 0, np.exp(log_g[:, None, None] * np.maximum(rel, f32(0.0))), f32(0.0)).astype(f32)
    qd = np.exp(log_g[:, None] * (idx + f32(1.0))).astype(f32)[:, :, None]
    kd = np.exp(log_g[:, None] * (f32(c) - f32(1.0) - idx)).astype(f32)[:, :, None]
    cd = np.exp(log_g * f32(c)).astype(f32)
    return jnp.asarray(din), jnp.asarray(qd), jnp.asarray(kd), jnp.asarray(cd)


def _t5_bucket(dist):
    max_exact = REL_BUCKETS // 2
    d_f = jnp.maximum(dist, 1).astype(F32)
    large = max_exact + (jnp.log(d_f / max_exact) / math.log(REL_MAX_DIST / max_exact)
                         * (REL_BUCKETS - max_exact)).astype(I32)
    large = jnp.minimum(large, REL_BUCKETS - 1)
    return jnp.where(dist < max_exact, dist, large)


def _bucket_tables():
    qi = jnp.arange(ATT_BLK)[:, None]
    kj = jnp.arange(2 * ATT_BLK)[None, :]
    dist = jnp.clip(ATT_BLK + qi - kj, 0, ATT_BLK)
    return jnp.stack([_t5_bucket(dist * dil) for _, dil in ATT_GROUPS]).astype(I32)


def _retention_fwd(rqk, rv, rg, gn_g, gn_b, din, qd, kd, cd):
    nc = SEQ // RET_CHUNK
    c, dk, dv = RET_CHUNK, RET_DK, RET_DV

    def kern(q_ref, k_ref, v_ref, rg_ref, g_ref, b_ref, din_ref, qd_ref, kd_ref, cd_ref,
             o_ref, st_ref, gated_ref, state):
        n = pl.program_id(0)

        @pl.when(n == 0)
        def _():
            state[...] = jnp.zeros_like(state)

        for sub in range(RET_SUB):
            rows = slice(sub * c, (sub + 1) * c)
            for h in range(RET_HEADS):
                q, k = q_ref[rows, h * dk:(h + 1) * dk], k_ref[rows, h * dk:(h + 1) * dk]
                v = v_ref[rows, h * dv:(h + 1) * dv]
                s_b = state[h].astype(BF16)
                st_ref[h, sub] = s_b
                a = lax.dot_general(q, k, _NT, preferred_element_type=F32) * din_ref[h]
                o = jnp.dot(a.astype(BF16), v, preferred_element_type=F32)
                o += jnp.dot(q, s_b, preferred_element_type=F32) * qd_ref[h]
                v_cols = slice(h * dv, (h + 1) * dv)
                o_ref[rows, v_cols] = o
                nrm, _ = _gn_parts(o)
                gate = rg_ref[rows, v_cols].astype(F32)
                gated_ref[rows, v_cols] = ((gate * _sigmoid(gate))
                                           * (nrm * g_ref[:, v_cols] + b_ref[:, v_cols])).astype(BF16)
                kk = (k.astype(F32) * kd_ref[h]).astype(BF16)
                state[h] = state[h] * cd_ref[h] + lax.dot_general(kk, v, _TN, preferred_element_type=F32)

    whole = lambda a: pl.BlockSpec(a.shape, lambda n: (0,) * a.ndim)
    cs = RET_SUB * c
    rows_v = pl.BlockSpec((cs, RET_V_W), lambda n: (n, 0))
    return pl.pallas_call(
        kern, name="retention_fwd", grid=(nc // RET_SUB,),
        in_specs=[
            pl.BlockSpec((cs, RET_QK_W), lambda n: (n, 0)),
            pl.BlockSpec((cs, RET_QK_W), lambda n: (n, 1)),
            rows_v, rows_v, whole(gn_g), whole(gn_b),
            whole(din), whole(qd), whole(kd),
            pl.BlockSpec(memory_space=pltpu.SMEM),
        ],
        out_specs=[
            rows_v,
            pl.BlockSpec((RET_HEADS, RET_SUB, dk, dv), lambda n: (0, n, 0, 0)),
            rows_v,
        ],
        out_shape=[
            jax.ShapeDtypeStruct((SEQ, RET_V_W), F32),
            jax.ShapeDtypeStruct((RET_HEADS, nc, dk, dv), BF16),
            jax.ShapeDtypeStruct((SEQ, RET_V_W), BF16),
        ],
        scratch_shapes=[pltpu.VMEM((RET_HEADS, dk, dv), F32)],
        compiler_params=_cparams(("arbitrary",)),
    )(rqk, rqk, rv, rg, gn_g, gn_b, din, qd, kd, cd)


def _retention_bwd(rqk, rv, states, d_gated, ro, rg, gn_g, gn_b, din, qd, kd, cd, cos, sin):
    nc = SEQ // RET_CHUNK
    c, dk, dv = RET_CHUNK, RET_DK, RET_DV
    half = dk // 2
    last = nc // RET_SUB - 1

    def unrot(g, cs, sn):
        g1, g2 = g[:, :half], g[:, half:]
        return jnp.concatenate([g1 * cs + g2 * sn, g2 * cs - g1 * sn], axis=-1)

    def kern(q_ref, k_ref, v_ref, st_ref, dg_ref, ro_ref, rg_ref, g_ref, b_ref, din_ref, qd_ref, kd_ref,
             cd_ref, cos_ref, sin_ref, out_ref, drg_ref, dgn_g_ref, dgn_b_ref, dstate):
        step = pl.program_id(0)

        @pl.when(step == 0)
        def _():
            dstate[...] = jnp.zeros_like(dstate)
            dgn_g_ref[...] = jnp.zeros_like(dgn_g_ref)
            dgn_b_ref[...] = jnp.zeros_like(dgn_b_ref)

        for sub in reversed(range(RET_SUB)):
            rows = slice(sub * c, (sub + 1) * c)
            cs, sn = cos_ref[rows, :], sin_ref[rows, :]
            for h in range(RET_HEADS):
                qk_cols, v_cols = slice(h * dk, (h + 1) * dk), slice(h * dv, (h + 1) * dv)
                q, k, v = q_ref[rows, qk_cols], k_ref[rows, qk_cols], v_ref[rows, v_cols]
                s_b = st_ref[h, sub]
                nrm, rstd = _gn_parts(ro_ref[rows, v_cols])
                gate, dg = rg_ref[rows, v_cols].astype(F32), dg_ref[rows, v_cols].astype(F32)
                sg = _sigmoid(gate)
                gn_gain = g_ref[:, v_cols]
                drg_ref[rows, v_cols] = (dg * (nrm * gn_gain + b_ref[:, v_cols])
                                         * (sg * (1.0 + gate * (1.0 - sg)))).astype(BF16)
                d_ron = dg * (gate * sg)
                dgn_g_ref[:, v_cols] += jnp.sum(d_ron * nrm, axis=0, keepdims=True)
                dgn_b_ref[:, v_cols] += jnp.sum(d_ron, axis=0, keepdims=True)
                d_n = d_ron * gn_gain
                d_o = rstd * (d_n - jnp.mean(d_n, axis=-1, keepdims=True)
                              - nrm * jnp.mean(d_n * nrm, axis=-1, keepdims=True))
                d_ob = d_o.astype(BF16)
                d_oq = (d_o * qd_ref[h]).astype(BF16)
                ds_b = dstate[h].astype(BF16)
                din_m = din_ref[h]
                a_b = (lax.dot_general(q, k, _NT, preferred_element_type=F32) * din_m).astype(BF16)
                kk = (k.astype(F32) * kd_ref[h]).astype(BF16)
                d_v = lax.dot_general(a_b, d_ob, _TN, preferred_element_type=F32)
                d_v += jnp.dot(kk, ds_b, preferred_element_type=F32)
                d_a = (lax.dot_general(d_ob, v, _NT, preferred_element_type=F32) * din_m).astype(BF16)
                d_q = jnp.dot(d_a, k, preferred_element_type=F32)
                d_q += lax.dot_general(d_oq, s_b, _NT, preferred_element_type=F32)
                d_k = lax.dot_general(d_a, q, _TN, preferred_element_type=F32)
                d_k += lax.dot_general(v, ds_b, _NT, preferred_element_type=F32) * kd_ref[h]
                dstate[h] = dstate[h] * cd_ref[h] + lax.dot_general(q, d_oq, _TN,
                                                                    preferred_element_type=F32)
                out_ref[rows, h * dk:(h + 1) * dk] = unrot(d_q, cs, sn).astype(BF16)
                out_ref[rows, RET_QK_W + h * dk:RET_QK_W + (h + 1) * dk] = (
                    unrot(d_k, cs, sn) * (RET_DK ** -0.5)).astype(BF16)
                out_ref[rows, 2 * RET_QK_W + h * dv:2 * RET_QK_W + (h + 1) * dv] = d_v.astype(BF16)

    whole = lambda a: pl.BlockSpec(a.shape, lambda n: (0,) * a.ndim)
    rs = RET_SUB * c
    rows_v = pl.BlockSpec((rs, RET_V_W), lambda n: (last - n, 0))
    return pl.pallas_call(
        kern, name="retention_bwd", grid=(nc // RET_SUB,),
        in_specs=[
            pl.BlockSpec((rs, RET_QK_W), lambda n: (last - n, 0)),
            pl.BlockSpec((rs, RET_QK_W), lambda n: (last - n, 1)),
            rows_v,
            pl.BlockSpec((RET_HEADS, RET_SUB, dk, dv), lambda n: (0, last - n, 0, 0)),
            rows_v, rows_v, rows_v, whole(gn_g), whole(gn_b),
            whole(din), whole(qd), whole(kd),
            pl.BlockSpec(memory_space=pltpu.SMEM),
            pl.BlockSpec((rs, half), lambda n: (last - n, 0)),
            pl.BlockSpec((rs, half), lambda n: (last - n, 0)),
        ],
        out_specs=[pl.BlockSpec((rs, 2 * RET_QK_W + RET_V_W), lambda n: (last - n, 0)), rows_v,
                   whole(gn_g), whole(gn_b)],
        out_shape=[jax.ShapeDtypeStruct((SEQ, 2 * RET_QK_W + RET_V_W), BF16),
                   jax.ShapeDtypeStruct((SEQ, RET_V_W), BF16),
                   jax.ShapeDtypeStruct((1, RET_V_W), F32), jax.ShapeDtypeStruct((1, RET_V_W), F32)],
        scratch_shapes=[pltpu.VMEM((RET_HEADS, dk, dv), F32)],
        compiler_params=_cparams(("arbitrary",)),
    )(rqk, rqk, rv, states, d_gated, ro, rg, gn_g, gn_b, din, qd, kd, cd, cos, sin)


def _bias_build(rel_bias, buckets):
    ng = len(ATT_GROUPS)

    def kern(tab_ref, bkt_ref, o_ref):
        g, h = pl.program_id(0), pl.program_id(1)
        bkt = bkt_ref[...]
        acc = jnp.zeros(bkt.shape, F32)
        for b in range(REL_BUCKETS):
            acc = jnp.where(bkt == b, tab_ref[b, g * ATT_HPG + h], acc)
        o_ref[...] = acc

    return pl.pallas_call(
        kern, name="bias_build", grid=(ng, ATT_HPG),
        in_specs=[pl.BlockSpec(memory_space=pltpu.SMEM),
                  pl.BlockSpec((None, ATT_BLK, 2 * ATT_BLK), lambda g, h: (g, 0, 0))],
        out_specs=pl.BlockSpec((None, None, ATT_BLK, 2 * ATT_BLK), lambda g, h: (g, h, 0, 0)),
        out_shape=jax.ShapeDtypeStruct((ng, ATT_HPG, ATT_BLK, 2 * ATT_BLK), F32),
        compiler_params=_cparams(("arbitrary", "arbitrary")),
    )(rel_bias, buckets)


def _bias_grad(dsb, buckets):
    ng = len(ATT_GROUPS)

    def kern(ds_ref, bkt_ref, o_ref):
        g, h = pl.program_id(0), pl.program_id(1)
        bkt, ds = bkt_ref[...], ds_ref[...]
        for b in range(REL_BUCKETS):
            o_ref[b, g * ATT_HPG + h] = jnp.sum(jnp.where(bkt == b, ds, 0.0))

    return pl.pallas_call(
        kern, name="bias_grad", grid=(ng, ATT_HPG),
        in_specs=[pl.BlockSpec((None, None, ATT_BLK, 2 * ATT_BLK), lambda g, h: (g, h, 0, 0)),
                  pl.BlockSpec((None, ATT_BLK, 2 * ATT_BLK), lambda g, h: (g, 0, 0))],
        out_specs=pl.BlockSpec(memory_space=pltpu.SMEM),
        out_shape=jax.ShapeDtypeStruct((REL_BUCKETS, N_ATT_HEADS), F32),
        compiler_params=_cparams(("arbitrary", "arbitrary")),
    )(dsb, buckets)


_NT = (((1,), (1,)), ((), ()))
_TN = (((0,), (0,)), ((), ()))
_ATT_SCALE = ATT_DH ** -0.5


def _window_mask(has_prev):
    qi = lax.broadcasted_iota(I32, (ATT_BLK, 2 * ATT_BLK), 0)
    kj = lax.broadcasted_iota(I32, (ATT_BLK, 2 * ATT_BLK), 1)
    prev_ok = jnp.logical_and(jnp.logical_and(kj < ATT_BLK, kj >= qi), has_prev)
    return jnp.logical_or(prev_ok, jnp.logical_and(kj >= ATT_BLK, qi >= kj - ATT_BLK))


def _head_specs(col0):
    return pl.BlockSpec((SEQ, ATT_DH), lambda h: (0, col0 + h))


def _sub_rows(start, size, dil):
    return pl.ds(start, size) if dil == 1 else pl.ds(start, size, stride=dil)


def _att_blocks(dil):
    nb = SEQ // dil // ATT_BLK
    return [(r + dil * n * ATT_BLK, n > 0, n + 1 < nb) for r in range(dil) for n in range(nb)]


def _att_fwd(gi, dil, qkv, bias, after=()):
    blk, dh = ATT_BLK, ATT_DH
    pad = dil * blk
    col0 = 3 * ATT_HPG * gi

    def kern(q_ref, k_ref, v_ref, b_ref, *rest):
        o_ref, l_ref, qf, kpad, vpad = rest[len(after):]
        zero = jnp.zeros((pad, dh), F32)
        kpad[0:pad, :] = zero
        vpad[0:pad, :] = zero
        kpad[pad:, :] = k_ref[...].astype(F32)
        vpad[pad:, :] = v_ref[...].astype(F32)
        qf[...] = q_ref[...].astype(F32)
        bias_m = b_ref[...]
        for start, has_prev, _ in _att_blocks(dil):
            rows, window = _sub_rows(start, blk, dil), _sub_rows(start, 2 * blk, dil)
            q = qf[rows, :].astype(BF16)
            kw, vw = kpad[window, :].astype(BF16), vpad[window, :].astype(BF16)
            valid = _window_mask(has_prev)
            s = lax.dot_general(q, kw, _NT, preferred_element_type=F32) * _ATT_SCALE + bias_m
            s = jnp.where(valid, s, -1e30)
            mx = jnp.max(s, axis=-1, keepdims=True)
            e = jnp.exp(s - mx)
            den = jnp.sum(e, axis=-1, keepdims=True)
            o_ref[rows, :] = jnp.dot((e / den).astype(BF16), vw, preferred_element_type=F32)
            l_ref[rows, :] = jnp.broadcast_to(mx + jnp.log(den), (blk, dh))

    return pl.pallas_call(
        kern, name=f"att_fwd_g{gi}", grid=(ATT_HPG,),
        in_specs=[_head_specs(col0), _head_specs(col0 + ATT_HPG), _head_specs(col0 + 2 * ATT_HPG),
                  pl.BlockSpec((None, None, blk, 2 * blk), lambda h: (gi, h, 0, 0))]
        + [pl.BlockSpec(memory_space=pl.ANY)] * len(after),
        out_specs=[_head_specs(0), _head_specs(0)],
        out_shape=[jax.ShapeDtypeStruct((SEQ, ATT_W), F32), jax.ShapeDtypeStruct((SEQ, ATT_W), F32)],
        scratch_shapes=[pltpu.VMEM((SEQ, dh), F32), pltpu.VMEM((SEQ + pad, dh), F32),
                        pltpu.VMEM((SEQ + pad, dh), F32)],
        compiler_params=_cparams(("arbitrary",)),
    )(qkv, qkv, qkv, bias, *after)


def _att_bwd(gi, dil, qkv, d_att, lse, dd, bias):
    blk, dh = ATT_BLK, ATT_DH
    pad = dil * blk
    col0 = 3 * ATT_HPG * gi

    def kern(q_ref, k_ref, v_ref, do_ref, l_ref, d_ref, b_ref, dqkv_ref, dsb_ref,
             qf, kpad, vpad, dq_s, dkpad, dvpad):
        zero = jnp.zeros((pad, dh), F32)
        kpad[0:pad, :] = zero
        vpad[0:pad, :] = zero
        kpad[pad:, :] = k_ref[...].astype(F32)
        vpad[pad:, :] = v_ref[...].astype(F32)
        qf[...] = q_ref[...].astype(F32)
        dkpad[...] = jnp.zeros_like(dkpad)
        dvpad[...] = jnp.zeros_like(dvpad)
        bias_m = b_ref[...]
        ds_sum = jnp.zeros((blk, 2 * blk), F32)

        for start, has_prev, _ in _att_blocks(dil):
            rows, window = _sub_rows(start, blk, dil), _sub_rows(start, 2 * blk, dil)
            q, d_o = qf[rows, :].astype(BF16), do_ref[rows, :].astype(BF16)
            kw, vw = kpad[window, :].astype(BF16), vpad[window, :].astype(BF16)
            lrow, drow = l_ref[rows, :][:, :1], d_ref[rows, :][:, :1]
            valid = _window_mask(has_prev)
            s = lax.dot_general(q, kw, _NT, preferred_element_type=F32) * _ATT_SCALE + bias_m
            p = jnp.where(valid, jnp.exp(jnp.where(valid, s, -1e30) - lrow), 0.0)
            dp = lax.dot_general(d_o, vw, _NT, preferred_element_type=F32)
            ds = p * (dp - drow)
            ds_b = ds.astype(BF16)
            dq_s[rows, :] = jnp.dot(ds_b, kw, preferred_element_type=F32) * _ATT_SCALE
            dkpad[window, :] += lax.dot_general(ds_b, q, _TN, preferred_element_type=F32) * _ATT_SCALE
            dvpad[window, :] += lax.dot_general(p.astype(BF16), d_o, _TN, preferred_element_type=F32)
            ds_sum = ds_sum + ds
        dsb_ref[...] = ds_sum

        dqkv_ref[0] = dq_s[...].astype(BF16)
        dqkv_ref[1] = dkpad[pad:, :].astype(BF16)
        dqkv_ref[2] = dvpad[pad:, :].astype(BF16)

    return pl.pallas_call(
        kern, name=f"att_bwd_g{gi}", grid=(ATT_HPG,),
        in_specs=[_head_specs(col0), _head_specs(col0 + ATT_HPG), _head_specs(col0 + 2 * ATT_HPG),
                  _head_specs(0), _head_specs(0), _head_specs(0),
                  pl.BlockSpec((None, None, blk, 2 * blk), lambda h: (gi, h, 0, 0))],
        out_specs=[pl.BlockSpec((3, SEQ, dh), lambda h: (0, 0, h)),
                   pl.BlockSpec((None, blk, 2 * blk), lambda h: (h, 0, 0))],
        out_shape=[jax.ShapeDtypeStruct((3, SEQ, ATT_W), BF16),
                   jax.ShapeDtypeStruct((ATT_HPG, blk, 2 * blk), F32)],
        scratch_shapes=[pltpu.VMEM((SEQ, dh), F32), pltpu.VMEM((SEQ + pad, dh), F32),
                        pltpu.VMEM((SEQ + pad, dh), F32), pltpu.VMEM((SEQ, dh), F32),
                        pltpu.VMEM((SEQ + pad, dh), F32), pltpu.VMEM((SEQ + pad, dh), F32)],
        compiler_params=_cparams(("arbitrary",)),
    )(qkv, qkv, qkv, d_att, lse, dd, bias)


def _rms_parts(x):
    r = lax.rsqrt(jnp.mean(x * x, axis=-1, keepdims=True) + RMS_EPS)
    return x * r, r


def _rms_bwd(d_xhat, xhat, r):
    return r * (d_xhat - xhat * jnp.mean(d_xhat * xhat, axis=-1, keepdims=True))


def _prenorm_fwd(name, x, gain, shift, scale):
    def body(xt, g, sh, sc):
        xhat, _ = _rms_parts(xt)
        return (xhat * g) * (1.0 + sc) + sh
    return _rowmap(name, body, [x], [gain, shift, scale], [(D_MODEL, BF16)])[0]


def _prenorm_bwd_epi(d_h, x, resid, gain, scale, branch=None, gate=None):
    xhat, r = _rms_parts(x)
    nrm = xhat * gain
    d_n = d_h * (1.0 + scale)
    dx = _rms_bwd(d_n * gain, xhat, r) + resid
    sums = (jnp.sum(d_h, axis=0, keepdims=True), jnp.sum(d_h * nrm, axis=0, keepdims=True),
            jnp.sum(d_n * xhat, axis=0, keepdims=True))
    if branch is None:
        return (dx,) + sums
    return (dx, dx * gate) + sums + (jnp.sum(dx * branch, axis=0, keepdims=True),)


def _row_operands(tm, rows, vecs):
    return ([(a, (tm, D_MODEL), lambda i, j, kk: (i, 0)) for a in rows]
            + [(v, (1, D_MODEL), lambda i, j, kk: (0, 0)) for v in vecs])


def _gn_parts(ro):
    mu = jnp.mean(ro, axis=-1, keepdims=True)
    cen = ro - mu
    rstd = lax.rsqrt(jnp.mean(cen * cen, axis=-1, keepdims=True) + GN_EPS)
    return cen * rstd, rstd


MERGE_TM = 512


def _att_out(os_, ls_, w_att_out, gates, ret_out):
    tm = MERGE_TM

    def kern(o0, o1, o2, l0, l1, l2, w_ref, ga_ref, gb_ref, ro_ref, att_ref, attb_ref, lse_ref,
             ao_ref, mg_ref):
        l0v, l1v, l2v = l0[...], l1[...], l2[...]
        mx = jnp.maximum(jnp.maximum(l0v, l1v), l2v)
        e0, e1, e2 = jnp.exp(l0v - mx), jnp.exp(l1v - mx), jnp.exp(l2v - mx)
        den = e0 + e1 + e2
        att = (e0 / den) * o0[...] + (e1 / den) * o1[...] + (e2 / den) * o2[...]
        att_b = att.astype(BF16)
        att_ref[...] = att
        attb_ref[...] = att_b
        lse_ref[...] = mx + jnp.log(den)
        att_out = jnp.dot(att_b, w_ref[...], preferred_element_type=F32)
        ao_ref[...], merged = _merge_fwd_epi(att_out, ga_ref[...], gb_ref[...], ro_ref[...])
        mg_ref[...] = merged.astype(BF16)

    rows_w = pl.BlockSpec((tm, ATT_W), lambda i: (i, 0))
    rows_d = pl.BlockSpec((tm, D_MODEL), lambda i: (i, 0))
    return pl.pallas_call(
        kern, name="att_out", grid=(SEQ // tm,),
        in_specs=[rows_w] * 6 + [pl.BlockSpec((ATT_W, D_MODEL), lambda i: (0, 0)), rows_d,
                                 pl.BlockSpec((tm, D_MODEL), lambda i: (i, 1)), rows_d],
        out_specs=[rows_w, rows_w, rows_w, rows_d, rows_d],
        out_shape=[jax.ShapeDtypeStruct((SEQ, ATT_W), F32), jax.ShapeDtypeStruct((SEQ, ATT_W), BF16),
                   jax.ShapeDtypeStruct((SEQ, ATT_W), F32), jax.ShapeDtypeStruct((SEQ, D_MODEL), F32),
                   jax.ShapeDtypeStruct((SEQ, D_MODEL), BF16)],
        compiler_params=_cparams(("parallel",)),
    )(*os_, *ls_, w_att_out, gates, gates, ret_out)


def _merge_operands(gates, ret_out, att_out=None):
    ops = [(gates, (MERGE_TM, D_MODEL), lambda i, j, kk: (i, 0)),
           (gates, (MERGE_TM, D_MODEL), lambda i, j, kk: (i, 1)),
           (ret_out, (MERGE_TM, D_MODEL), lambda i, j, kk: (i, 0))]
    if att_out is not None:
        ops.append((att_out, (MERGE_TM, D_MODEL), lambda i, j, kk: (i, 0)))
    return ops


def _merge_fwd_epi(att_out, ga, gb, ret_out):
    return att_out, _sigmoid(ga.astype(F32)) * ret_out + _sigmoid(gb.astype(F32)) * att_out


def _merge_bwd_epi(d_merged, ga, gb, ret_out, att_out):
    sa, sb = _sigmoid(ga.astype(F32)), _sigmoid(gb.astype(F32))
    return (d_merged * sa, d_merged * sb, d_merged * ret_out * (sa * (1.0 - sa)),
            d_merged * att_out * (sb * (1.0 - sb)))


def _att_out_bwd_epi(d_att, att):
    outs = []
    for h in range(ATT_HPG):
        sl = slice(h * ATT_DH, (h + 1) * ATT_DH)
        outs.append(jnp.broadcast_to(jnp.sum(d_att[:, sl] * att[:, sl], axis=-1, keepdims=True),
                                     (d_att.shape[0], ATT_DH)))
    return d_att, jnp.concatenate(outs, axis=-1)


def _loss_head_epi(branch, x_prev, target, gate, gain):
    x3 = x_prev + gate * branch
    xhat, r = _rms_parts(x3)
    err = xhat * gain - target
    d_y = err / D_MODEL
    loss = 0.5 * jnp.sum(jnp.mean(err * err, axis=-1, keepdims=True), axis=0, keepdims=True)
    d_x = _rms_bwd(d_y * gain, xhat, r)
    return (d_x, d_x * gate, jnp.broadcast_to(loss, (1, D_MODEL)),
            jnp.sum(d_y * xhat, axis=0, keepdims=True), jnp.sum(d_x * branch, axis=0, keepdims=True))


def _local_step(pos, x, target, mod, norm1_g, norm2_g, norm_f_g, rel_bias, gn_g, gn_b, w_in, rest_gather):
    sh1, sc1, g1, sh2, sc2, g2 = [mod[:, i * D_MODEL:(i + 1) * D_MODEL] for i in range(6)]
    cos, sin = _rope_tables()
    din, qd, kd, cd = _decay_tables()
    buckets = _bucket_tables()
    bias = _bias_build(rel_bias, buckets)
    dils = [d for _, d in ATT_GROUPS]

    h1 = _prenorm_fwd("prenorm1_fwd", x, norm1_g, sh1, sc1)

    qk_tn = 2 * RET_DK

    def rot_epi(acc, cs, sn, scale):
        half = RET_DK // 2
        outs = []
        for h0 in range(0, qk_tn, RET_DK):
            x1, x2 = acc[:, h0:h0 + half], acc[:, h0 + half:h0 + RET_DK]
            outs += [x1 * cs - x2 * sn, x1 * sn + x2 * cs]
        return (jnp.concatenate(outs, axis=-1) * scale,)

    qk_scale = jnp.concatenate([jnp.ones((1, RET_QK_W), F32),
                                jnp.full((1, RET_QK_W), RET_DK ** -0.5, F32)], axis=-1)
    rope_ex = [(cos, (TM, RET_DK // 2), lambda i, j, kk: (i, 0)),
               (sin, (TM, RET_DK // 2), lambda i, j, kk: (i, 0)),
               (qk_scale, (1, qk_tn), lambda i, j, kk: (0, j))]
    rest_sems, rest_shards, rest_fulls, rest_token = rest_gather
    behind = [rest_token]
    rv = _matmul("proj_rv", h1, w_in, "nn", SEQ, RET_V_W, D_MODEL, [BF16], b_off=OFF_V, tk=D_MODEL,
                 after=behind)[0]
    rg = _matmul("proj_rg", h1, w_in, "nn", SEQ, RET_V_W, D_MODEL, [BF16], b_off=OFF_G, tk=D_MODEL,
                 after=behind)[0]
    gates = _matmul("proj_gates", h1, w_in, "nn", SEQ, 2 * D_MODEL, D_MODEL, [BF16], b_off=OFF_GATE,
                    tn=512, tk=D_MODEL, after=behind)[0]
    aqkv = _matmul("proj_att", h1, w_in, "nn", SEQ, 9 * ATT_W, D_MODEL, [BF16], b_off=OFF_ATT,
                   tn=512, tk=D_MODEL, after=behind)[0]

    rqk = _matmul("proj_qk", h1, w_in, "nn", SEQ, 2 * RET_QK_W, D_MODEL, [BF16], b_off=OFF_Q,
                  tn=qk_tn, tk=D_MODEL, epilogue=rot_epi, extras=rope_ex, after=behind)[0]
    ro, states, gated = _retention_fwd(rqk, rv, rg, gn_g, gn_b, din, qd, kd, cd)
    os_, ls_ = [], []
    for gi in range(3):
        if gi == 2:
            rest_sems, rest_fulls, fwd_token = _gather_rest_forward(
                rest_sems, rest_shards, rest_fulls, [gated, gates] + os_)
        o_g, l_g = _att_fwd(gi, dils[gi], aqkv, bias, after=[fwd_token] if gi == 2 else ())
        os_.append(o_g)
        ls_.append(l_g)
    w_ret_out, w_att_out, w_o, w_ff1, w_ff2 = _gather_rest_end(rest_sems, rest_fulls, [os_[2]])
    ret_out = _matmul("ret_out", gated, w_ret_out, "nn", SEQ, D_MODEL, RET_V_W, [F32], tk=RET_V_W)[0]
    att, att_b, lse, att_out, merged = _att_out(os_, ls_, w_att_out, gates, ret_out)

    def mix_epi(acc, xt, g, gain, sh, sc):
        x_new = xt + g * acc
        xhat, _ = _rms_parts(x_new)
        return x_new, acc, (xhat * gain) * (1.0 + sc) + sh

    x2, mix, h2 = _matmul("mix_out", merged, w_o, "nn", SEQ, D_MODEL, D_MODEL, [F32, BF16, BF16],
                          epilogue=mix_epi, extras=_row_operands(TM, [x], [g1, norm2_g, sh2, sc2]))

    def relu2_epi(acc):
        r = jnp.maximum(acc, 0.0)
        return r * r, r

    act, relu_u = _matmul("ff1", h2, w_ff1, "nn", SEQ, D_FF, D_MODEL, [BF16, BF16], tk=D_MODEL,
                          epilogue=relu2_epi)
    d_x3, d_y2, loss, d_gf, d_g2 = _matmul(
        "ff2", act, w_ff2, "nn", SEQ, D_MODEL, D_FF, [F32, BF16], tm=MERGE_TM, tk=2048, n_sums=3,
        epilogue=_loss_head_epi, extras=_row_operands(MERGE_TM, [x2, target], [g2, norm_f_g]))

    def relu2_bwd_epi(acc, rt):
        return (acc * (2.0 * rt.astype(F32)),)

    gw_ff2 = _matmul_tn_pair("ff2_dw", pos, act, d_y2, D_FF, D_MODEL, SEQ, D_FF // N_CHIPS,
                             tm=512, tn=1024, tk=SEQ)
    d_u = _matmul("ff2_dx", d_y2, w_ff2, "nt", SEQ, D_FF, D_MODEL, [BF16], epilogue=relu2_bwd_epi,
                  extras=[(relu_u, (TM, TN), lambda i, j, kk: (i, j))])[0]
    gw_ff1 = _matmul_tn_pair("ff1_dw", pos, h2, d_u, D_MODEL, D_FF, SEQ, D_MODEL,
                             tm=512, tn=1024, tk=SEQ)
    ffn = ["w_ff2", "w_ff1"]
    ffn_started = _ici_start("ici_start_ffn", ffn, [gw_ff2, gw_ff1])
    d_x2, d_mix, d_sh2, d_sc2, d_n2g, d_g1 = _matmul(
        "ff1_dx", d_u, w_ff1, "nt", SEQ, D_MODEL, D_FF, [F32, BF16], tm=MERGE_TM, tk=2048, n_sums=4,
        epilogue=_prenorm_bwd_epi, extras=_row_operands(MERGE_TM, [x2, d_x3], [norm2_g, sc2])
        + _row_operands(MERGE_TM, [mix], [g1]), after=[ffn_started[3]])
    gw_o = _matmul_tn_pair("mix_dw", pos, merged, d_mix, D_MODEL, D_MODEL, SEQ, D_MODEL // N_CHIPS,
                           tm=128, tn=1024, tk=2048)
    d_ret_out, d_att_out, d_ga, d_gb = _matmul(
        "mix_dx", d_mix, w_o, "nt", SEQ, D_MODEL, D_MODEL, [BF16] * 4, tm=MERGE_TM,
        epilogue=_merge_bwd_epi, extras=_merge_operands(gates, ret_out, att_out))

    gw_ret_out = _matmul_tn_pair("ret_out_dw", pos, gated, d_ret_out, RET_V_W, D_MODEL, SEQ,
                                 RET_V_W // N_CHIPS, tm=256, tn=1024, tk=SEQ)
    gw_att_out = _matmul_tn_pair("att_out_dw", pos, att_b, d_att_out, ATT_W, D_MODEL, SEQ, ATT_W,
                                 tm=256, tn=1024, tk=2048)
    mixer = ["w_o", "w_ret_out", "w_att_out"]
    mixer_started = _ici_start("ici_start_mixer", mixer, [gw_o, gw_ret_out, gw_att_out])
    d_gated = _matmul("ret_out_dx", d_ret_out, w_ret_out, "nt", SEQ, RET_V_W, D_MODEL, [BF16],
                      after=[mixer_started[3]])[0]
    d_att, dd = _matmul("att_out_dx", d_att_out, w_att_out, "nt", SEQ, ATT_W, D_MODEL, [F32, F32],
                        epilogue=_att_out_bwd_epi,
                        extras=[(att, (TM, ATT_W), lambda i, j, kk: (i, 0))], after=[mixer_started[3]])

    d_rqkv, d_rg, d_gn_g, d_gn_b = _retention_bwd(rqk, rv, states, d_gated, ro, rg, gn_g, gn_b,
                                                  din, qd, kd, cd, cos, sin)

    d_aqkv, dsbs = [], []
    for gi in range(3):
        dqkv, dsb = _att_bwd(gi, dils[gi], aqkv, d_att, lse, dd, bias)
        d_aqkv.append(dqkv)
        dsbs.append(dsb)
    d_rel_bias = _bias_grad(jnp.stack(dsbs), buckets)

    d_proj = ([(d_rqkv, False), (d_rg, False)] + [(t, True) for t in d_aqkv]
              + [(d_ga, False), (d_gb, False)])
    gw_in = _matmul_tn_pair("proj_dw", pos, h1, d_proj, D_MODEL, IN_COLS, SEQ, D_MODEL,
                            tm=512, tn=ATT_W, tk=SEQ)
    sems, (gw_in,), (land,), token = _ici_start("ici_start_w_in", ["w_in"], [gw_in])
    grad_x, d_sh1, d_sc1, d_n1g = _matmul(
        "proj_dx", d_proj, w_in, "nt", SEQ, D_MODEL, IN_COLS, [F32], tn=1024, tk=ATT_W, n_sums=3,
        epilogue=_prenorm_bwd_epi, extras=_row_operands(TM, [x, d_x2], [norm1_g, sc1]), after=[token])
    pending = (sems, land)

    names = ffn + mixer
    psums, got = _ici_wait("ici_wait_rest", names, list(ffn_started[0]) + list(mixer_started[0]),
                           list(ffn_started[1]) + list(mixer_started[1]),
                           list(ffn_started[2]) + list(mixer_started[2]), [grad_x])
    g_big = {n: _final_sum("final_" + n, pos, dict(BIG)[n], psums[i], got[i], SHARD[n])
             for i, n in enumerate(names)}
    d_mod = jnp.concatenate([d_sh1, d_sc1, d_g1, d_sh2, d_sc2, d_g2], axis=-1)
    small = dict(norm1_g=d_n1g, norm2_g=d_n2g, norm_f_g=d_gf, gn_g=d_gn_g, gn_b=d_gn_b,
                 rel_bias=d_rel_bias)
    return loss, grad_x, d_mod, small, g_big, (gw_in,) + pending


def _me():
    return lax.axis_index("x"), lax.axis_index("y"), lax.axis_index("c")


def _peer(x, y, c, mask):
    return (x ^ ((mask >> 2) & 1), y ^ ((mask >> 1) & 1), c ^ (mask & 1))


def _gather8(src_ref, dst_ref, send_sems, recv_sems):
    x, y, c = _me()
    me = 4 * x + 2 * y + c
    copies = []
    for mask in range(1, N_DEV):
        cp = pltpu.make_async_remote_copy(
            src_ref=src_ref, dst_ref=dst_ref.at[me], send_sem=send_sems.at[mask - 1],
            recv_sem=recv_sems.at[mask - 1], device_id=_peer(x, y, c, mask), device_id_type=MESH)
        cp.start()
        copies.append(cp)
    dst_ref[me] = src_ref[...]
    for cp in copies:
        cp.wait_recv()
    for cp in copies:
        cp.wait_send()


def _ada_fwd(c_in, w_ada, b_ada):
    ncol = ADA_COLS // N_CHIPS

    def body(c_ref, w_ref, b_ref, mod_ref, sc_ref, cbuf, cg, mbuf, mg, s1, r1, s2, r2):
        x, y, c = _me()
        me = 4 * x + 2 * y + c
        cv = c_ref[...]
        cbuf[...] = jnp.broadcast_to(cv * _sigmoid(cv), cbuf.shape)
        _gather8(cbuf, cg, s1, r1)
        rows = lax.broadcasted_iota(I32, (N_DEV, D_MODEL), 0)
        sc_all = jnp.zeros((N_DEV, D_MODEL), F32)
        for d in range(N_DEV):
            sc_all = jnp.where(rows == d, cg[d], sc_all)
        sc_ref[...] = sc_all
        mbuf[...] = jnp.dot(sc_all.astype(BF16), w_ref[...].astype(BF16), preferred_element_type=F32)
        _gather8(mbuf, mg, s2, r2)
        rowsel = lax.broadcasted_iota(I32, (N_DEV, ncol), 0) == me
        for k in range(N_CHIPS):
            blk = mg[2 * k]
            row = jnp.sum(jnp.where(rowsel, blk, 0.0), axis=0, keepdims=True)
            mod_ref[:, k * ncol:(k + 1) * ncol] = row + b_ref[:, k * ncol:(k + 1) * ncol]

    vm = pl.BlockSpec(memory_space=pltpu.VMEM)
    return pl.pallas_call(
        body, name="ada_fwd",
        in_specs=[vm, vm, vm], out_specs=[vm, vm],
        out_shape=[jax.ShapeDtypeStruct((1, ADA_COLS), F32), jax.ShapeDtypeStruct((N_DEV, D_MODEL), F32)],
        scratch_shapes=[
            pltpu.VMEM((8, D_MODEL), F32), pltpu.VMEM((N_DEV, 8, D_MODEL), F32),
            pltpu.VMEM((8, ncol), F32), pltpu.VMEM((N_DEV, 8, ncol), F32),
            pltpu.SemaphoreType.DMA((N_DEV - 1,)), pltpu.SemaphoreType.DMA((N_DEV - 1,)),
            pltpu.SemaphoreType.DMA((N_DEV - 1,)), pltpu.SemaphoreType.DMA((N_DEV - 1,)),
        ],
        compiler_params=pltpu.CompilerParams(vmem_limit_bytes=VMEM_LIMIT_V7X),
    )(c_in, w_ada, b_ada)


def _small_reduce(pack, sc_all, after=()):
    ncol = ADA_COLS // N_CHIPS

    def body(p_ref, sc_ref, *rest):
        tot_ref, gw_ref, pg, s1, r1 = rest[len(after):]
        x, y, _ = _me()
        chip = 2 * x + y
        _gather8(p_ref, pg, s1, r1)
        tot = pg[0]
        for d in range(1, N_DEV):
            tot = tot + pg[d]
        tot_ref[...] = tot
        rows = lax.broadcasted_iota(I32, (N_DEV, ncol), 0)
        dmod = jnp.zeros((N_DEV, ncol), F32)
        for k in range(N_CHIPS):
            part = jnp.zeros((N_DEV, ncol), F32)
            for d in range(N_DEV):
                part = jnp.where(rows == d, pg[d, :, k * ncol:(k + 1) * ncol][0:1, :], part)
            dmod = jnp.where(chip == k, part, dmod)
        gw_ref[...] = lax.dot_general(sc_ref[...].astype(BF16), dmod.astype(BF16), _TN,
                                      preferred_element_type=F32)

    vm = pl.BlockSpec(memory_space=pltpu.VMEM)
    return pl.pallas_call(
        body, name="small_reduce",
        in_specs=[vm, vm] + [pl.BlockSpec(memory_space=pl.ANY)] * len(after), out_specs=[vm, vm],
        out_shape=[jax.ShapeDtypeStruct((8, ADA_COLS), F32), jax.ShapeDtypeStruct((D_MODEL, ncol), F32)],
        scratch_shapes=[pltpu.VMEM((N_DEV, 8, ADA_COLS), F32),
                        pltpu.SemaphoreType.DMA((N_DEV - 1,)), pltpu.SemaphoreType.DMA((N_DEV - 1,))],
        compiler_params=pltpu.CompilerParams(vmem_limit_bytes=VMEM_LIMIT_V7X),
    )(pack, sc_all, *after)


BIG = (("w_in", 1), ("w_ret_out", 0), ("w_att_out", 1), ("w_o", 0), ("w_ff1", 1), ("w_ff2", 0))
SHARD = {"w_in": (D_MODEL, IN_COLS // N_CHIPS), "w_ret_out": (RET_V_W // N_CHIPS, D_MODEL),
         "w_att_out": (ATT_W, D_MODEL // N_CHIPS), "w_o": (D_MODEL // N_CHIPS, D_MODEL),
         "w_ff1": (D_MODEL, D_FF // N_CHIPS), "w_ff2": (D_FF // N_CHIPS, D_MODEL)}
_CHIP_FLIPS = ((1, 0), (0, 1), (1, 1))


def _region(ref, axis, chip, half, shard_shape):
    r, cw = shard_shape
    hr = r // 2
    if axis == 1:
        return ref.at[pl.ds(half * hr, hr), pl.ds(chip * cw, cw)]
    return ref.at[pl.ds(chip * r + half * hr, hr), :]


def _gather_weights(shards, n_remote):
    nw = len(BIG)
    shapes = [s.shape for s in shards]
    full_shapes = [(r, N_CHIPS * cw) if ax == 1 else (N_CHIPS * r, cw)
                   for (r, cw), (_, ax) in zip(shapes, BIG)]

    def body(*refs):
        ins, outs = refs[:nw], refs[nw:2 * nw]
        own = refs[2 * nw:3 * nw]
        from_ici, from_sib = refs[3 * nw:3 * nw + n_remote], refs[3 * nw + n_remote:3 * nw + 2 * n_remote]
        ld_sem, st_sem, s_ici, r_ici, s_d2d, r_d2d, st_a, st_b = refs[3 * nw + 2 * n_remote:]
        x, y, c = _me()
        chip = 2 * x + y
        sib = (x, y, 1 - c)
        loads = [pltpu.make_async_copy(ins[i], own[i], ld_sem.at[i]) for i in range(nw)]
        for cp in loads:
            cp.start()
        pending, first = [], []
        for i, (_, ax) in enumerate(BIG):
            r, cw = shapes[i]
            hr = r // 2
            loads[i].wait()
            dst = outs[i].at[:, pl.ds(chip * cw, cw)] if ax == 1 else outs[i].at[pl.ds(chip * r, r), :]
            cp = pltpu.make_async_copy(own[i], dst, st_sem.at[i])
            cp.start()
            pending.append(cp)
            for j, (fx, fy) in enumerate(_CHIP_FLIPS if i < n_remote else ()):
                rc = pltpu.make_async_remote_copy(
                    src_ref=own[i].at[pl.ds(c * hr, hr), :], dst_ref=from_ici[i].at[j],
                    send_sem=s_ici.at[j * nw + i], recv_sem=r_ici.at[j * nw + i],
                    device_id=(x ^ fx, y ^ fy, c), device_id_type=MESH)
                rc.start()
                first.append((j, i, rc))
        passed = []
        for j, i, rc in first:
            fx, fy = _CHIP_FLIPS[j]
            src_chip = 2 * (x ^ fx) + (y ^ fy)
            ax = BIG[i][1]
            rc.wait_recv()
            fw = pltpu.make_async_remote_copy(
                src_ref=from_ici[i].at[j], dst_ref=from_sib[i].at[j], send_sem=s_d2d.at[j * nw + i],
                recv_sem=r_d2d.at[j * nw + i], device_id=sib, device_id_type=MESH)
            fw.start()
            passed.append((j, i, src_chip, fw))
            st = pltpu.make_async_copy(from_ici[i].at[j], _region(outs[i], ax, src_chip, c, shapes[i]),
                                       st_a.at[j * nw + i])
            st.start()
            pending.append(st)
        for j, i, src_chip, fw in passed:
            fw.wait_recv()
            st = pltpu.make_async_copy(from_sib[i].at[j],
                                       _region(outs[i], BIG[i][1], src_chip, 1 - c, shapes[i]),
                                       st_b.at[j * nw + i])
            st.start()
            pending.append(st)
        for _, _, rc in first:
            rc.wait_send()
        for _, _, _, fw in passed:
            fw.wait_send()
        for cp in pending:
            cp.wait()

    hbm = pl.BlockSpec(memory_space=pl.ANY)
    halves = [pltpu.VMEM((3, r // 2, cw), BF16) for r, cw in shapes[:n_remote]]
    return pl.pallas_call(
        body, name="gather_weights",
        in_specs=[hbm] * nw, out_specs=[hbm] * nw,
        out_shape=[jax.ShapeDtypeStruct(fs, BF16) for fs in full_shapes],
        scratch_shapes=[pltpu.VMEM(sh, BF16) for sh in shapes] + halves + halves
        + [pltpu.SemaphoreType.DMA((nw,)), pltpu.SemaphoreType.DMA((nw,))]
        + [pltpu.SemaphoreType.DMA((3 * nw,))] * 6,
        compiler_params=pltpu.CompilerParams(vmem_limit_bytes=VMEM_LIMIT_V7X),
    )(*shards)


REST = BIG[1:]
_SIDE_EFFECTS = pltpu.CompilerParams(has_side_effects=pltpu.SideEffectType.DATAFLOW_SIDE_EFFECTING)
_ANY_SPEC = pl.BlockSpec(memory_space=pl.ANY)


def _rest_ici_copies(shard_refs, full_refs, sems):
    x, y, c = _me()
    chip = 2 * x + y
    n = 3 * len(REST)
    copies = []
    for i, (name, ax) in enumerate(REST):
        hr = SHARD[name][0] // 2
        for j, (fx, fy) in enumerate(_CHIP_FLIPS):
            copies.append(pltpu.make_async_remote_copy(
                src_ref=shard_refs[i].at[pl.ds(c * hr, hr), :],
                dst_ref=_region(full_refs[i], ax, chip, c, SHARD[name]),
                send_sem=sems[3 * i + j], recv_sem=sems[n + 3 * i + j],
                device_id=(x ^ fx, y ^ fy, c), device_id_type=MESH))
    return copies


def _rest_d2d_copies(full_refs, sems):
    x, y, c = _me()
    n = 3 * len(REST)
    copies = []
    for i, (name, ax) in enumerate(REST):
        for j, (fx, fy) in enumerate(_CHIP_FLIPS):
            reg = _region(full_refs[i], ax, 2 * (x ^ fx) + (y ^ fy), c, SHARD[name])
            copies.append(pltpu.make_async_remote_copy(
                src_ref=reg, dst_ref=reg, send_sem=sems[3 * i + j], recv_sem=sems[n + 3 * i + j],
                device_id=(x, y, 1 - c), device_id_type=MESH))
    return copies


def _gather_rest_start(shards, fulls, after):
    nr, ns, na = len(REST), 6 * len(REST), len(after)

    def body(*refs):
        for cp in _rest_ici_copies(refs[:nr], refs[nr:2 * nr], refs[2 * nr + na:2 * nr + na + ns]):
            cp.start()
        token = refs[-1]
        token[...] = jnp.zeros_like(token)

    hbm = lambda a: pltpu.HBM(a.shape, a.dtype)
    res = pl.pallas_call(
        body, name="gather_rest_start",
        out_shape=(pltpu.SemaphoreType.DMA(()),) * ns + tuple(hbm(a) for a in shards + fulls)
        + (jax.ShapeDtypeStruct((8, 128), F32),),
        in_specs=(_HBM_SPEC,) * (2 * nr) + (_ANY_SPEC,) * na,
        out_specs=(_SEM_SPEC,) * ns + (_HBM_SPEC,) * (2 * nr) + (pl.BlockSpec(memory_space=pltpu.VMEM),),
        input_output_aliases={k: ns + k for k in range(2 * nr)}, compiler_params=_SIDE_EFFECTS,
    )(*[pltpu.with_memory_space_constraint(a, pltpu.HBM) for a in shards + fulls], *after)
    return res[:ns], res[ns:ns + nr], res[ns + nr:ns + 2 * nr], res[-1]


def _gather_rest_forward(sems, shards, fulls, after):
    nr, ns = len(REST), 6 * len(REST)

    def body(*refs):
        shard_refs, full_refs, old = refs[:nr], refs[nr:2 * nr], refs[2 * nr:2 * nr + ns]
        new = refs[2 * nr + ns + len(after):2 * nr + 2 * ns + len(after)]
        for cp in _rest_ici_copies(shard_refs, full_refs, old):
            cp.wait_send()
            cp.wait_recv()
        for cp in _rest_d2d_copies(full_refs, new):
            cp.start()
        token = refs[-1]
        token[...] = jnp.zeros_like(token)

    res = pl.pallas_call(
        body, name="gather_rest_forward",
        out_shape=(pltpu.SemaphoreType.DMA(()),) * ns + tuple(pltpu.HBM(a.shape, a.dtype) for a in fulls)
        + (jax.ShapeDtypeStruct((8, 128), F32),),
        in_specs=(_HBM_SPEC,) * (2 * nr) + (_SEM_SPEC,) * ns + (_ANY_SPEC,) * len(after),
        out_specs=(_SEM_SPEC,) * ns + (_HBM_SPEC,) * nr + (pl.BlockSpec(memory_space=pltpu.VMEM),),
        input_output_aliases={nr + k: ns + k for k in range(nr)}, compiler_params=_SIDE_EFFECTS,
    )(*shards, *fulls, *sems, *after)
    return res[:ns], res[ns:ns + nr], res[-1]


def _gather_rest_end(sems, fulls, after):
    nr, ns = len(REST), 6 * len(REST)

    def body(*refs):
        for cp in _rest_d2d_copies(refs[:nr], refs[nr:nr + ns]):
            cp.wait_send()
            cp.wait_recv()

    return pl.pallas_call(
        body, name="gather_rest_end",
        out_shape=tuple(pltpu.HBM(a.shape, a.dtype) for a in fulls),
        in_specs=(_HBM_SPEC,) * nr + (_SEM_SPEC,) * ns + (_ANY_SPEC,) * len(after),
        out_specs=(_HBM_SPEC,) * nr,
        input_output_aliases={k: k for k in range(nr)}, compiler_params=_SIDE_EFFECTS,
    )(*fulls, *sems, *after)


def _adam_update(w, g, m, v):
    mn = ADAM_B1 * m + (1.0 - ADAM_B1) * g
    vn = ADAM_B2 * v + (1.0 - ADAM_B2) * (g * g)
    m_hat = mn / (1.0 - ADAM_B1 ** ADAM_STEP)
    v_hat = vn / (1.0 - ADAM_B2 ** ADAM_STEP)
    return -ADAM_LR * (m_hat / (jnp.sqrt(v_hat) + ADAM_EPS) + ADAM_WD * w), mn, vn


def _final_sum(name, pos, axis, psum, recv, shard_shape, after=(), tr=128):
    r, cw = shard_shape
    hr = r // 2
    tr = min(tr, hr)
    nt = hr // tr
    n_after = len(after)

    def kern(pos_ref, p_ref, r_ref, *rest):
        g_ref, send_buf, land_buf, s_sem, r_sem = rest[n_after:]
        p, t = pl.program_id(0), pl.program_id(1)
        sib = _sibling()

        def copy(i):
            return pltpu.make_async_remote_copy(
                src_ref=send_buf.at[i], dst_ref=land_buf.at[i], send_sem=s_sem.at[i],
                recv_sem=r_sem.at[i], device_id=sib, device_id_type=MESH)

        @pl.when(p == 0)
        def _():
            tot = p_ref[...].astype(F32)
            for j in range(3):
                tot = tot + r_ref[j].astype(F32)
            send_buf[t] = tot
            copy(t).start()
            g_ref[...] = tot

        @pl.when(p == 1)
        def _():
            copy(t).wait_recv()
            g_ref[...] = land_buf[t]

        @pl.when(jnp.logical_and(p == 1, t == nt - 1))
        def _():
            for i in range(nt):
                copy(i).wait_send()

    def shard_rows(p, t, pos_ref):
        return (jnp.where(p == 0, pos_ref[0], 1 - pos_ref[0]) * nt + t, 0)

    def own_part(p, t, pos_ref):
        tt = jnp.where(p == 0, t, nt - 1)
        return (tt, pos_ref[1]) if axis == 1 else (pos_ref[1] * nt + tt, 0)

    grid_spec = pltpu.PrefetchScalarGridSpec(
        num_scalar_prefetch=1, grid=(2, nt),
        in_specs=[pl.BlockSpec((tr, cw), own_part),
                  pl.BlockSpec((3, tr, cw), lambda p, t, pos_ref: (0, jnp.where(p == 0, t, nt - 1), 0))]
        + [pl.BlockSpec(memory_space=pl.ANY)] * n_after,
        out_specs=pl.BlockSpec((tr, cw), shard_rows),
        scratch_shapes=[pltpu.VMEM((nt, tr, cw), F32), pltpu.VMEM((nt, tr, cw), F32),
                        pltpu.SemaphoreType.DMA((nt,)), pltpu.SemaphoreType.DMA((nt,))])
    return pl.pallas_call(
        kern, name=name, grid_spec=grid_spec, out_shape=jax.ShapeDtypeStruct((r, cw), F32),
        compiler_params=_cparams(("arbitrary", "arbitrary")),
    )(pos, psum, recv, *after)


def _adamw(name, w, g, m, v):
    r, cw = w.shape
    tr = min(r, 128)

    def kern(w_ref, g_ref, m_ref, v_ref, go_ref, d_ref, nm_ref, nv_ref):
        gv = g_ref[...]
        go_ref[...] = gv
        d_ref[...], nm_ref[...], nv_ref[...] = _adam_update(w_ref[...], gv, m_ref[...], v_ref[...])

    spec = pl.BlockSpec((tr, cw), lambda i: (i, 0))
    return pl.pallas_call(
        kern, name=name, grid=(r // tr,), in_specs=[spec] * 4, out_specs=[spec] * 4,
        out_shape=[jax.ShapeDtypeStruct((r, cw), F32)] * 4, compiler_params=_cparams(("parallel",)),
    )(w, g, m, v)


_PACK_W = ADA_COLS
_NB = REL_BUCKETS * N_ATT_HEADS
_SMALL_SLOTS = {
    "b_ada": (0, 0, ADA_COLS),
    "norm1_g": (1, 0, D_MODEL), "norm2_g": (1, D_MODEL, D_MODEL), "norm_f_g": (1, 2 * D_MODEL, D_MODEL),
    "ret_gn_g": (1, 3 * D_MODEL, RET_V_W),
    "ret_gn_b": (2, 0, RET_V_W), "rel_bias": (2, RET_V_W, _NB), "loss": (2, RET_V_W + 512, 128),
}


def _pack_small(vals):
    rows = []
    for r in range(8):
        items = sorted([(off, n) for n, (rr, off, _) in _SMALL_SLOTS.items() if rr == r and n in vals])
        parts, pos = [], 0
        for off, n in items:
            if off > pos:
                parts.append(jnp.zeros((1, off - pos), F32))
            parts.append(vals[n].reshape(1, -1).astype(F32))
            pos = off + _SMALL_SLOTS[n][2]
        if pos < _PACK_W:
            parts.append(jnp.zeros((1, _PACK_W - pos), F32))
        rows.append(jnp.concatenate(parts, axis=-1))
    return jnp.concatenate(rows, axis=0)


def _unpack_small(pack, name):
    r, off, wd = _SMALL_SLOTS[name]
    return pack[r:r + 1, off:off + wd]


def kernel(x, c, w_ada, b_ada, norm1_g, w_in, rel_bias, ret_gn_g, ret_gn_b, w_ret_out, w_att_out, w_o, norm2_g, w_ff1, w_ff2, norm_f_g, loss_target, m_w_ada, m_b_ada, m_norm1_g, m_w_in, m_rel_bias, m_ret_gn_g, m_ret_gn_b, m_w_ret_out, m_w_att_out, m_w_o, m_norm2_g, m_w_ff1, m_w_ff2, m_norm_f_g, v_w_ada, v_b_ada, v_norm1_g, v_w_in, v_rel_bias, v_ret_gn_g, v_ret_gn_b, v_w_ret_out, v_w_att_out, v_w_o, v_norm2_g, v_w_ff1, v_w_ff2, v_norm_f_g):
    given = dict(locals())
    big_names = [n for n, _ in BIG]
    shard_w = {n: given[n][0] for n in big_names}
    assert all(shard_w[n].shape == SHARD[n] for n in big_names)

    shards_bf = [shard_w[n].astype(BF16) for n in big_names]
    full = _gather_weights(shards_bf, 1)
    mod, sc_all = _ada_fwd(c, w_ada[0], b_ada)
    rest_gather = _gather_rest_start(shards_bf[1:], list(full[1:]), [mod])
    pos = _where_am_i()

    loss, grad_x, d_mod, small, g_big, pending = _local_step(
        pos, x[0], loss_target[0], mod, norm1_g, norm2_g, norm_f_g.reshape(1, -1), rel_bias, ret_gn_g,
        ret_gn_b, full[0], rest_gather)

    pack_g = _pack_small(dict(b_ada=d_mod, norm1_g=small["norm1_g"], norm2_g=small["norm2_g"],
                              norm_f_g=small["norm_f_g"], ret_gn_g=small["gn_g"], ret_gn_b=small["gn_b"],
                              rel_bias=small["rel_bias"], loss=loss[:, :128]))
    tot, g_w_ada = _small_reduce(pack_g, sc_all, after=list(g_big.values()))

    small_names = ["b_ada", "norm1_g", "rel_bias", "ret_gn_g", "ret_gn_b", "norm2_g", "norm_f_g"]
    pack_w = _pack_small({n: given[n] for n in small_names})
    pack_m = _pack_small({n: given["m_" + n] for n in small_names})
    pack_v = _pack_small({n: given["v_" + n] for n in small_names})
    _, sd, sm, sv = _adamw("adamw_small", pack_w, tot, pack_m, pack_v)

    grads, deltas, new_m, new_v = {}, {}, {}, {}
    for n in small_names:
        shp = given[n].shape
        grads[n] = _unpack_small(tot, n).reshape(shp)
        deltas[n] = _unpack_small(sd, n).reshape(shp)
        new_m[n] = _unpack_small(sm, n).reshape(shp)
        new_v[n] = _unpack_small(sv, n).reshape(shp)
    g_big["w_ada"] = g_w_ada
    for n in ["w_ada"] + big_names[1:] + big_names[:1]:
        if n == "w_in":
            gw_in, sems, land = pending
            done = [tot, sd] + [deltas[k] for k in ["w_ada"] + big_names[1:]]
            (gw_in,), (got,) = _ici_wait("ici_wait_w_in", [n], sems, [gw_in], [land], done)
            g_big[n] = _final_sum("final_w_in", pos, 1, gw_in, got, SHARD[n])
        g, d, nm, nv = _adamw("adamw_" + n, given[n][0], g_big[n], given["m_" + n][0], given["v_" + n][0])
        grads[n], deltas[n], new_m[n], new_v[n] = g[None], d[None], nm[None], nv[None]

    order = ["w_ada", "b_ada", "norm1_g", "w_in", "rel_bias", "ret_gn_g", "ret_gn_b", "w_ret_out",
             "w_att_out", "w_o", "norm2_g", "w_ff1", "w_ff2", "norm_f_g"]
    loss_out = _unpack_small(tot, "loss")[0, 0]
    return (loss_out, grad_x[None], *[grads[n] for n in order], *[deltas[n] for n in order],
            *[new_m[n] for n in order], *[new_v[n] for n in order])
```

```python
import functools
import math

import jax
import jax.numpy as jnp
import numpy as np
from jax import lax
from jax.experimental import pallas as pl
from jax.experimental.pallas import tpu as pltpu

F32 = jnp.float32
BF16 = jnp.bfloat16
I32 = jnp.int32

SEQ = 2048
D_MODEL = 1024
RET_HEADS = 4
RET_DK = 256
RET_DV = 512
RET_CHUNK = 128
RET_SUB = 2
RET_QK_W = RET_HEADS * RET_DK
RET_V_W = RET_HEADS * RET_DV
ATT_GROUPS = ((128, 1), (512, 4), (2048, 16))
ATT_HPG = 4
ATT_DH = 128
ATT_W = ATT_HPG * ATT_DH
ATT_BLK = 128
N_BLK = SEQ // ATT_BLK
REL_BUCKETS = 32
REL_MAX_DIST = 2048
N_ATT_HEADS = 12
D_FF = 4 * D_MODEL
RMS_EPS = 1e-6
GN_EPS = 1e-5
ROPE_BASE = 10000.0
IN_COLS = 2 * RET_QK_W + 2 * RET_V_W + 9 * ATT_W + 2 * D_MODEL
OFF_Q, OFF_K, OFF_V, OFF_G = 0, RET_QK_W, 2 * RET_QK_W, 2 * RET_QK_W + RET_V_W
OFF_ATT = 2 * RET_QK_W + 2 * RET_V_W
OFF_GATE = OFF_ATT + 9 * ATT_W
N_CHIPS = 4
N_DEV = 8
ADA_COLS = 6 * D_MODEL

ADAM_LR = 0.001
ADAM_B1 = 0.9
ADAM_B2 = 0.999
ADAM_EPS = 1e-08
ADAM_WD = 0.01
ADAM_STEP = 10

VMEM_LIMIT_V7X = 56 * 1024 * 1024
MESH = pl.DeviceIdType.MESH


def _cparams(sem):
    return pltpu.CompilerParams(dimension_semantics=sem, vmem_limit_bytes=VMEM_LIMIT_V7X)


def _sigmoid(v):
    return 1.0 / (1.0 + jnp.exp(-v))


def _rowmap(name, body, row_ins, bcast_ins, row_outs, sum_outs=(), tm=256, after=()):
    m = row_ins[0].shape[0]
    n_in = len(row_ins) + len(bcast_ins)
    n_ro = len(row_outs)

    def kern(*refs):
        vals = [r[...] for r in refs[:n_in]]
        res = body(*vals)
        if not isinstance(res, (tuple, list)):
            res = (res,)
        outs = refs[n_in + len(after):]
        for r, v in zip(outs[:n_ro], res[:n_ro]):
            r[...] = v.astype(r.dtype)
        if sum_outs:
            @pl.when(pl.program_id(0) == 0)
            def _():
                for r in outs[n_ro:]:
                    r[...] = jnp.zeros_like(r)
            for r, v in zip(outs[n_ro:], res[n_ro:]):
                r[...] += v

    in_specs = [pl.BlockSpec((tm, a.shape[1]), lambda i: (i, 0)) for a in row_ins]
    in_specs += [pl.BlockSpec(a.shape, lambda i: (0, 0)) for a in bcast_ins]
    in_specs += [pl.BlockSpec(memory_space=pl.ANY)] * len(after)
    out_specs = [pl.BlockSpec((tm, n), lambda i: (i, 0)) for n, _ in row_outs]
    out_specs += [pl.BlockSpec((1, n), lambda i: (0, 0)) for n in sum_outs]
    out_shape = [jax.ShapeDtypeStruct((m, n), dt) for n, dt in row_outs]
    out_shape += [jax.ShapeDtypeStruct((1, n), F32) for n in sum_outs]
    return pl.pallas_call(
        kern, name=name, grid=(m // tm,), in_specs=in_specs, out_specs=out_specs,
        out_shape=out_shape, compiler_params=_cparams(("arbitrary",)),
    )(*row_ins, *bcast_ins, *after)


TM, TN = 1024, 1024


def _piece_chunks(piece, width):
    arr, stacked = piece
    return arr.shape[0] if stacked else arr.shape[1] // width


def _piece_spec(piece, rows, width, start, row_of, chunk_of):
    arr, stacked = piece
    last = _piece_chunks(piece, width) - 1

    def local(*ids):
        return jnp.clip(chunk_of(*ids) - start, 0, last)

    def row(*ids):
        rel = chunk_of(*ids) - start
        return jnp.where(jnp.logical_and(rel >= 0, rel <= last), row_of(*ids), 0)

    if stacked:
        return pl.BlockSpec((None, rows, width), lambda *ids: (local(*ids), row(*ids), 0))
    return pl.BlockSpec((rows, width), lambda *ids: (row(*ids), local(*ids)))


def _piece_starts(pieces, width):
    return [sum(_piece_chunks(p, width) for p in pieces[:q]) for q in range(len(pieces))]


def _matmul(name, a, b, kind, m, n, k, outs, *, b_off=0, tm=TM, tn=TN, tk=1024,
            epilogue=None, extras=(), after=(), n_sums=0):
    tm, tn, tk = min(tm, m), min(tn, n), min(tk, k)
    nk = k // tk
    pieces = a if isinstance(a, list) else [(a, False)]
    starts = _piece_starts(pieces, tk)
    if kind == "nn":
        a_specs = [pl.BlockSpec((tm, tk), lambda i, j, kk: (i, kk))]
        b_spec = pl.BlockSpec((tk, tn), lambda i, j, kk: (kk, b_off // tn + j))
        dn = (((1,), (0,)), ((), ()))
    elif kind == "nt":
        a_specs = [_piece_spec(p, tm, tk, st, lambda i, j, kk: i, lambda i, j, kk: kk)
                   for p, st in zip(pieces, starts)]
        b_spec = pl.BlockSpec((tn, tk), lambda i, j, kk: (j, b_off // tk + kk))
        dn = (((1,), (1,)), ((), ()))
    else:
        a_specs = [pl.BlockSpec((tk, tm), lambda i, j, kk: (kk, i))]
        b_spec = pl.BlockSpec((tk, tn), lambda i, j, kk: (kk, j))
        dn = (((0,), (0,)), ((), ()))
    n_a, n_ex, n_out = len(pieces), len(extras), len(outs)
    if epilogue is None:
        epilogue = lambda acc: (acc,)

    assert n_sums == 0 or tn == n

    def finish(acc, ex_refs, out_refs, first_rows):
        res = epilogue(acc, *[r[...] for r in ex_refs])
        for r, v in zip(out_refs[:n_out], res[:n_out]):
            r[...] = v.astype(r.dtype)
        for r, v in zip(out_refs[n_out:], res[n_out:]):
            @pl.when(first_rows)
            def _(r=r, v=v):
                r[...] = v

            @pl.when(jnp.logical_not(first_rows))
            def _(r=r, v=v):
                r[...] += v

    n_in = n_a + 1 + n_ex + len(after)

    def kern(*refs):
        a_refs, b_ref = refs[:n_a], refs[n_a]
        ex_refs = refs[n_a + 1:n_a + 1 + n_ex]
        out_refs = refs[n_in:n_in + n_out + n_sums]
        first_rows, kk = pl.program_id(0) == 0, pl.program_id(2)
        dot = lambda a_ref: lax.dot_general(a_ref[...], b_ref[...], dn, preferred_element_type=F32)
        if nk == 1:
            finish(dot(a_refs[0]), ex_refs, out_refs, first_rows)
            return
        acc_ref = refs[n_in + n_out + n_sums]
        if n_a == 1:
            part = dot(a_refs[0])

            @pl.when(kk == 0)
            def _():
                acc_ref[...] = part

            @pl.when(kk > 0)
            def _():
                acc_ref[...] += part
        else:
            @pl.when(kk == 0)
            def _():
                acc_ref[...] = jnp.zeros_like(acc_ref)

            for q in range(n_a):
                @pl.when(jnp.logical_and(kk >= starts[q], kk < starts[q] + _piece_chunks(pieces[q], tk)))
                def _(q=q):
                    acc_ref[...] += dot(a_refs[q])

        @pl.when(kk == nk - 1)
        def _():
            finish(acc_ref[...], ex_refs, out_refs, first_rows)

    in_specs = a_specs + [b_spec] + [pl.BlockSpec(bs, im) for _, bs, im in extras]
    in_specs += [pl.BlockSpec(memory_space=pl.ANY)] * len(after)
    sem = ("arbitrary",) * 3 if n_sums else ("parallel", "parallel", "arbitrary")
    return pl.pallas_call(
        kern, name=name, grid=(m // tm, n // tn, nk), in_specs=in_specs,
        out_specs=[pl.BlockSpec((tm, tn), lambda i, j, kk: (i, j)) for _ in outs]
        + [pl.BlockSpec((1, tn), lambda i, j, kk: (0, 0))] * n_sums,
        out_shape=[jax.ShapeDtypeStruct((m, n), dt) for dt in outs]
        + [jax.ShapeDtypeStruct((1, n), F32)] * n_sums,
        scratch_shapes=[] if nk == 1 else [pltpu.VMEM((tm, tn), F32)],
        compiler_params=_cparams(sem),
    )(*[p[0] for p in pieces], b, *[e[0] for e in extras], *after)


def _ici_copies(psum_ref, recv_ref, s_sem, r_sem, axis, shard_shape):
    x, y, c = _me()
    hr, cw = shard_shape[0] // 2, shard_shape[1]
    pick = lambda sems, j: sems[j] if isinstance(sems, (list, tuple)) else sems.at[j]
    copies = []
    for j, (fx, fy) in enumerate(_CHIP_FLIPS):
        chip = 2 * (x ^ fx) + (y ^ fy)
        src = psum_ref.at[:, pl.ds(chip * cw, cw)] if axis == 1 else psum_ref.at[pl.ds(chip * hr, hr), :]
        copies.append(pltpu.make_async_remote_copy(
            src_ref=src, dst_ref=recv_ref.at[j], send_sem=pick(s_sem, j), recv_sem=pick(r_sem, j),
            device_id=(x ^ fx, y ^ fy, c), device_id_type=MESH))
    return copies


_HBM_SPEC = pl.BlockSpec(memory_space=pltpu.HBM)
_SEM_SPEC = pl.BlockSpec(memory_space=pltpu.SEMAPHORE)


def _split_ici_copies(names, p_refs, land_refs, sems):
    copies = []
    for i, n in enumerate(names):
        copies += _ici_copies(p_refs[i], land_refs[i], list(sems[6 * i:6 * i + 3]),
                              list(sems[6 * i + 3:6 * i + 6]), dict(BIG)[n], SHARD[n])
    return copies


def _ici_start(name, names, psums):
    nw, ns = len(names), 6 * len(names)
    lands = [lax.empty((3, SHARD[n][0] // 2, SHARD[n][1]), BF16) for n in names]

    def body(*refs):
        for cp in _split_ici_copies(names, refs[:nw], refs[nw:2 * nw], refs[2 * nw:2 * nw + ns]):
            cp.start()
        token = refs[-1]
        token[...] = jnp.zeros_like(token)

    res = pl.pallas_call(
        body, name=name,
        out_shape=(pltpu.SemaphoreType.DMA(()),) * ns
        + tuple(pltpu.HBM(a.shape, BF16) for a in list(psums) + lands)
        + (jax.ShapeDtypeStruct((8, 128), F32),),
        in_specs=(_HBM_SPEC,) * (2 * nw),
        out_specs=(_SEM_SPEC,) * ns + (_HBM_SPEC,) * (2 * nw) + (pl.BlockSpec(memory_space=pltpu.VMEM),),
        input_output_aliases={k: ns + k for k in range(2 * nw)},
        compiler_params=pltpu.CompilerParams(has_side_effects=pltpu.SideEffectType.DATAFLOW_SIDE_EFFECTING),
    )(*[pltpu.with_memory_space_constraint(a, pltpu.HBM) for a in list(psums) + lands])
    return res[:ns], res[ns:ns + nw], res[ns + nw:ns + 2 * nw], res[-1]


def _ici_wait(name, names, sems, p_thru, land_thru, after):
    nw, ns = len(names), 6 * len(names)

    def body(*refs):
        for cp in _split_ici_copies(names, refs[:nw], refs[nw:2 * nw], refs[2 * nw:2 * nw + ns]):
            cp.wait_send()
            cp.wait_recv()

    res = pl.pallas_call(
        body, name=name,
        out_shape=tuple(pltpu.HBM(a.shape, BF16) for a in list(p_thru) + list(land_thru)),
        in_specs=(_HBM_SPEC,) * (2 * nw) + (_SEM_SPEC,) * ns + (pl.BlockSpec(memory_space=pl.ANY),) * len(after),
        out_specs=(_HBM_SPEC,) * (2 * nw), input_output_aliases={k: k for k in range(2 * nw)},
        compiler_params=pltpu.CompilerParams(has_side_effects=pltpu.SideEffectType.DATAFLOW_SIDE_EFFECTING),
    )(*p_thru, *land_thru, *sems, *after)
    return res[:nw], res[nw:]


def _where_am_i():
    x, y, c = _me()
    return jnp.stack([c, 2 * x + y]).astype(I32)


def _sibling():
    x, y, c = _me()
    return (x, y, 1 - c)


N_SEND_SLOTS = 2


def _matmul_tn_pair(name, pos, a, b, m, n, k, shard_rows, *, tm, tn, tk):
    hr = shard_rows // 2
    tm, tn, tk = min(tm, hr), min(tn, n), min(tk, k)
    tph = hr // tm
    nt, nj, nk = (m // 2) // tm, n // tn, k // tk
    n_tiles = nt * nj

    def row_block(p, t, pos_ref):
        half = jnp.where(p == 0, 1 - pos_ref[0], pos_ref[0])
        return (t // tph) * (2 * tph) + half * tph + t % tph

    pieces = b if isinstance(b, list) else [(b, False)]
    starts = _piece_starts(pieces, tn)
    n_b = len(pieces)

    def kern(pos_ref, a_ref, *rest):
        b_refs = rest[:n_b]
        o_ref, acc_ref, send_buf, land_buf, s_sem, r_sem = rest[n_b:]
        p, t, j, kk = pl.program_id(0), pl.program_id(1), pl.program_id(2), pl.program_id(3)
        idx = t * nj + j
        sib = _sibling()

        def copy(i):
            return pltpu.make_async_remote_copy(
                src_ref=send_buf.at[i % N_SEND_SLOTS], dst_ref=land_buf.at[i], send_sem=s_sem.at[i],
                recv_sem=r_sem.at[i], device_id=sib, device_id_type=MESH)

        @pl.when(kk == 0)
        def _():
            acc_ref[...] = jnp.zeros_like(acc_ref)

        for q in range(n_b):
            @pl.when(jnp.logical_and(j >= starts[q], j < starts[q] + _piece_chunks(pieces[q], tn)))
            def _(q=q):
                acc_ref[...] += lax.dot_general(a_ref[...], b_refs[q][...], _TN, preferred_element_type=F32)

        @pl.when(jnp.logical_and(kk == nk - 1, p == 0))
        def _():
            @pl.when(idx >= N_SEND_SLOTS)
            def _():
                copy(idx - N_SEND_SLOTS).wait_send()

            send_buf[idx % N_SEND_SLOTS] = acc_ref[...].astype(BF16)
            copy(idx).start()

        @pl.when(jnp.logical_and(kk == nk - 1, p == 1))
        def _():
            copy(idx).wait_recv()
            o_ref[...] = (acc_ref[...] + land_buf[idx].astype(F32)).astype(BF16)

        @pl.when(jnp.logical_and(jnp.logical_and(p == 1, idx == n_tiles - 1), kk == nk - 1))
        def _():
            for i in range(max(n_tiles - N_SEND_SLOTS, 0), n_tiles):
                copy(i).wait_send()

    grid_spec = pltpu.PrefetchScalarGridSpec(
        num_scalar_prefetch=1, grid=(2, nt, nj, nk),
        in_specs=[pl.BlockSpec((tk, tm), lambda p, t, j, kk, pos_ref: (kk, row_block(p, t, pos_ref)))]
        + [_piece_spec(pc, tk, tn, st, lambda p, t, j, kk, pos_ref: kk, lambda p, t, j, kk, pos_ref: j)
           for pc, st in zip(pieces, starts)],
        out_specs=pl.BlockSpec((tm, tn), lambda p, t, j, kk, pos_ref: (p * t, p * j)),
        scratch_shapes=[pltpu.VMEM((tm, tn), F32), pltpu.VMEM((N_SEND_SLOTS, tm, tn), BF16),
                        pltpu.VMEM((n_tiles, tm, tn), BF16),
                        pltpu.SemaphoreType.DMA((n_tiles,)), pltpu.SemaphoreType.DMA((n_tiles,))])
    return pl.pallas_call(
        kern, name=name, grid_spec=grid_spec, out_shape=jax.ShapeDtypeStruct((m // 2, n), BF16),
        compiler_params=_cparams(("arbitrary",) * 4),
    )(pos, a, *[pc[0] for pc in pieces])


def _rope_tables():
    half = RET_DK // 2
    f32 = np.float32
    inv = np.power(f32(ROPE_BASE), -np.arange(half, dtype=f32) / f32(half)).astype(f32)
    ang = (np.arange(SEQ, dtype=f32)[:, None] * inv[None, :]).astype(f32)
    return jnp.asarray(np.cos(ang).astype(f32)), jnp.asarray(np.sin(ang).astype(f32))


def _decay_tables():
    c = RET_CHUNK
    f32 = np.float32
    log_g = np.log1p(-np.power(f32(2.0), f32(-5.0) - np.arange(RET_HEADS, dtype=f32))).astype(f32)
    idx = np.arange(c, dtype=f32)
    rel = idx[:, None] - idx[None, :]
    din = np.where(rel >= 0, np.exp(log_g[:, None, None] * np.maximum(rel, f32(0.0))), f32(0.0)).astype(f32)
    qd = np.exp(log_g[:, None] * (idx + f32(1.0))).astype(f32)[:, :, None]
    kd = np.exp(log_g[:, None] * (f32(c) - f32(1.0) - idx)).astype(f32)[:, :, None]
    cd = np.exp(log_g * f32(c)).astype(f32)
    return jnp.asarray(din), jnp.asarray(qd), jnp.asarray(kd), jnp.asarray(cd)


def _t5_bucket(dist):
    max_exact = REL_BUCKETS // 2
    d_f = jnp.maximum(dist, 1).astype(F32)
    large = max_exact + (jnp.log(d_f / max_exact) / math.log(REL_MAX_DIST / max_exact)
                         * (REL_BUCKETS - max_exact)).astype(I32)
    large = jnp.minimum(large, REL_BUCKETS - 1)
    return jnp.where(dist < max_exact, dist, large)


def _bucket_tables():
    qi = jnp.arange(ATT_BLK)[:, None]
    kj = jnp.arange(2 * ATT_BLK)[None, :]
    dist = jnp.clip(ATT_BLK + qi - kj, 0, ATT_BLK)
    return jnp.stack([_t5_bucket(dist * dil) for _, dil in ATT_GROUPS]).astype(I32)


def _retention_fwd(rqk, rv, rg, gn_g, gn_b, din, qd, kd, cd):
    nc = SEQ // RET_CHUNK
    c, dk, dv = RET_CHUNK, RET_DK, RET_DV

    def kern(q_ref, k_ref, v_ref, rg_ref, g_ref, b_ref, din_ref, qd_ref, kd_ref, cd_ref,
             o_ref, st_ref, gated_ref, state):
        n = pl.program_id(0)

        @pl.when(n == 0)
        def _():
            state[...] = jnp.zeros_like(state)

        for sub in range(RET_SUB):
            rows = slice(sub * c, (sub + 1) * c)
            for h in range(RET_HEADS):
                q, k = q_ref[rows, h * dk:(h + 1) * dk], k_ref[rows, h * dk:(h + 1) * dk]
                v = v_ref[rows, h * dv:(h + 1) * dv]
                s_b = state[h].astype(BF16)
                st_ref[h, sub] = s_b
                a = lax.dot_general(q, k, _NT, preferred_element_type=F32) * din_ref[h]
                o = jnp.dot(a.astype(BF16), v, preferred_element_type=F32)
                o += jnp.dot(q, s_b, preferred_element_type=F32) * qd_ref[h]
                v_cols = slice(h * dv, (h + 1) * dv)
                o_ref[rows, v_cols] = o
                nrm, _ = _gn_parts(o)
                gate = rg_ref[rows, v_cols].astype(F32)
                gated_ref[rows, v_cols] = ((gate * _sigmoid(gate))
                                           * (nrm * g_ref[:, v_cols] + b_ref[:, v_cols])).astype(BF16)
                kk = (k.astype(F32) * kd_ref[h]).astype(BF16)
                state[h] = state[h] * cd_ref[h] + lax.dot_general(kk, v, _TN, preferred_element_type=F32)

    whole = lambda a: pl.BlockSpec(a.shape, lambda n: (0,) * a.ndim)
    cs = RET_SUB * c
    rows_v = pl.BlockSpec((cs, RET_V_W), lambda n: (n, 0))
    return pl.pallas_call(
        kern, name="retention_fwd", grid=(nc // RET_SUB,),
        in_specs=[
            pl.BlockSpec((cs, RET_QK_W), lambda n: (n, 0)),
            pl.BlockSpec((cs, RET_QK_W), lambda n: (n, 1)),
            rows_v, rows_v, whole(gn_g), whole(gn_b),
            whole(din), whole(qd), whole(kd),
            pl.BlockSpec(memory_space=pltpu.SMEM),
        ],
        out_specs=[
            rows_v,
            pl.BlockSpec((RET_HEADS, RET_SUB, dk, dv), lambda n: (0, n, 0, 0)),
            rows_v,
        ],
        out_shape=[
            jax.ShapeDtypeStruct((SEQ, RET_V_W), F32),
            jax.ShapeDtypeStruct((RET_HEADS, nc, dk, dv), BF16),
            jax.ShapeDtypeStruct((SEQ, RET_V_W), BF16),
        ],
        scratch_shapes=[pltpu.VMEM((RET_HEADS, dk, dv), F32)],
        compiler_params=_cparams(("arbitrary",)),
    )(rqk, rqk, rv, rg, gn_g, gn_b, din, qd, kd, cd)


def _retention_bwd(rqk, rv, states, d_gated, ro, rg, gn_g, gn_b, din, qd, kd, cd, cos, sin):
    nc = SEQ // RET_CHUNK
    c, dk, dv = RET_CHUNK, RET_DK, RET_DV
    half = dk // 2
    last = nc // RET_SUB - 1

    def unrot(g, cs, sn):
        g1, g2 = g[:, :half], g[:, half:]
        return jnp.concatenate([g1 * cs + g2 * sn, g2 * cs - g1 * sn], axis=-1)

    def kern(q_ref, k_ref, v_ref, st_ref, dg_ref, ro_ref, rg_ref, g_ref, b_ref, din_ref, qd_ref, kd_ref,
             cd_ref, cos_ref, sin_ref, out_ref, drg_ref, dgn_g_ref, dgn_b_ref, dstate):
        step = pl.program_id(0)

        @pl.when(step == 0)
        def _():
            dstate[...] = jnp.zeros_like(dstate)
            dgn_g_ref[...] = jnp.zeros_like(dgn_g_ref)
            dgn_b_ref[...] = jnp.zeros_like(dgn_b_ref)

        for sub in reversed(range(RET_SUB)):
            rows = slice(sub * c, (sub + 1) * c)
            cs, sn = cos_ref[rows, :], sin_ref[rows, :]
            for h in range(RET_HEADS):
                qk_cols, v_cols = slice(h * dk, (h + 1) * dk), slice(h * dv, (h + 1) * dv)
                q, k, v = q_ref[rows, qk_cols], k_ref[rows, qk_cols], v_ref[rows, v_cols]
                s_b = st_ref[h, sub]
                nrm, rstd = _gn_parts(ro_ref[rows, v_cols])
                gate, dg = rg_ref[rows, v_cols].astype(F32), dg_ref[rows, v_cols].astype(F32)
                sg = _sigmoid(gate)
                gn_gain = g_ref[:, v_cols]
                drg_ref[rows, v_cols] = (dg * (nrm * gn_gain + b_ref[:, v_cols])
                                         * (sg * (1.0 + gate * (1.0 - sg)))).astype(BF16)
                d_ron = dg * (gate * sg)
                dgn_g_ref[:, v_cols] += jnp.sum(d_ron * nrm, axis=0, keepdims=True)
                dgn_b_ref[:, v_cols] += jnp.sum(d_ron, axis=0, keepdims=True)
                d_n = d_ron * gn_gain
                d_o = rstd * (d_n - jnp.mean(d_n, axis=-1, keepdims=True)
                              - nrm * jnp.mean(d_n * nrm, axis=-1, keepdims=True))
                d_ob = d_o.astype(BF16)
                d_oq = (d_o * qd_ref[h]).astype(BF16)
                ds_b = dstate[h].astype(BF16)
                din_m = din_ref[h]
                a_b = (lax.dot_general(q, k, _NT, preferred_element_type=F32) * din_m).astype(BF16)
                kk = (k.astype(F32) * kd_ref[h]).astype(BF16)
                d_v = lax.dot_general(a_b, d_ob, _TN, preferred_element_type=F32)
                d_v += jnp.dot(kk, ds_b, preferred_element_type=F32)
                d_a = (lax.dot_general(d_ob, v, _NT, preferred_element_type=F32) * din_m).astype(BF16)
                d_q = jnp.dot(d_a, k, preferred_element_type=F32)
                d_q += lax.dot_general(d_oq, s_b, _NT, preferred_element_type=F32)
                d_k = lax.dot_general(d_a, q, _TN, preferred_element_type=F32)
                d_k += lax.dot_general(v, ds_b, _NT, preferred_element_type=F32) * kd_ref[h]
                dstate[h] = dstate[h] * cd_ref[h] + lax.dot_general(q, d_oq, _TN,
                                                                    preferred_element_type=F32)
                out_ref[rows, h * dk:(h + 1) * dk] = unrot(d_q, cs, sn).astype(BF16)
                out_ref[rows, RET_QK_W + h * dk:RET_QK_W + (h + 1) * dk] = (
                    unrot(d_k, cs, sn) * (RET_DK ** -0.5)).astype(BF16)
                out_ref[rows, 2 * RET_QK_W + h * dv:2 * RET_QK_W + (h + 1) * dv] = d_v.astype(BF16)

    whole = lambda a: pl.BlockSpec(a.shape, lambda n: (0,) * a.ndim)
    rs = RET_SUB * c
    rows_v = pl.BlockSpec((rs, RET_V_W), lambda n: (last - n, 0))
    return pl.pallas_call(
        kern, name="retention_bwd", grid=(nc // RET_SUB,),
        in_specs=[
            pl.BlockSpec((rs, RET_QK_W), lambda n: (last - n, 0)),
            pl.BlockSpec((rs, RET_QK_W), lambda n: (last - n, 1)),
            rows_v,
            pl.BlockSpec((RET_HEADS, RET_SUB, dk, dv), lambda n: (0, last - n, 0, 0)),
            rows_v, rows_v, rows_v, whole(gn_g), whole(gn_b),
            whole(din), whole(qd), whole(kd),
            pl.BlockSpec(memory_space=pltpu.SMEM),
            pl.BlockSpec((rs, half), lambda n: (last - n, 0)),
            pl.BlockSpec((rs, half), lambda n: (last - n, 0)),
        ],
        out_specs=[pl.BlockSpec((rs, 2 * RET_QK_W + RET_V_W), lambda n: (last - n, 0)), rows_v,
                   whole(gn_g), whole(gn_b)],
        out_shape=[jax.ShapeDtypeStruct((SEQ, 2 * RET_QK_W + RET_V_W), BF16),
                   jax.ShapeDtypeStruct((SEQ, RET_V_W), BF16),
                   jax.ShapeDtypeStruct((1, RET_V_W), F32), jax.ShapeDtypeStruct((1, RET_V_W), F32)],
        scratch_shapes=[pltpu.VMEM((RET_HEADS, dk, dv), F32)],
        compiler_params=_cparams(("arbitrary",)),
    )(rqk, rqk, rv, states, d_gated, ro, rg, gn_g, gn_b, din, qd, kd, cd, cos, sin)


def _bias_build(rel_bias, buckets):
    ng = len(ATT_GROUPS)

    def kern(tab_ref, bkt_ref, o_ref):
        g, h = pl.program_id(0), pl.program_id(1)
        bkt = bkt_ref[...]
        acc = jnp.zeros(bkt.shape, F32)
        for b in range(REL_BUCKETS):
            acc = jnp.where(bkt == b, tab_ref[b, g * ATT_HPG + h], acc)
        o_ref[...] = acc

    return pl.pallas_call(
        kern, name="bias_build", grid=(ng, ATT_HPG),
        in_specs=[pl.BlockSpec(memory_space=pltpu.SMEM),
                  pl.BlockSpec((None, ATT_BLK, 2 * ATT_BLK), lambda g, h: (g, 0, 0))],
        out_specs=pl.BlockSpec((None, None, ATT_BLK, 2 * ATT_BLK), lambda g, h: (g, h, 0, 0)),
        out_shape=jax.ShapeDtypeStruct((ng, ATT_HPG, ATT_BLK, 2 * ATT_BLK), F32),
        compiler_params=_cparams(("arbitrary", "arbitrary")),
    )(rel_bias, buckets)


def _bias_grad(dsb, buckets):
    ng = len(ATT_GROUPS)

    def kern(ds_ref, bkt_ref, o_ref):
        g, h = pl.program_id(0), pl.program_id(1)
        bkt, ds = bkt_ref[...], ds_ref[...]
        for b in range(REL_BUCKETS):
            o_ref[b, g * ATT_HPG + h] = jnp.sum(jnp.where(bkt == b, ds, 0.0))

    return pl.pallas_call(
        kern, name="bias_grad", grid=(ng, ATT_HPG),
        in_specs=[pl.BlockSpec((None, None, ATT_BLK, 2 * ATT_BLK), lambda g, h: (g, h, 0, 0)),
                  pl.BlockSpec((None, ATT_BLK, 2 * ATT_BLK), lambda g, h: (g, 0, 0))],
        out_specs=pl.BlockSpec(memory_space=pltpu.SMEM),
        out_shape=jax.ShapeDtypeStruct((REL_BUCKETS, N_ATT_HEADS), F32),
        compiler_params=_cparams(("arbitrary", "arbitrary")),
    )(dsb, buckets)


_NT = (((1,), (1,)), ((), ()))
_TN = (((0,), (0,)), ((), ()))
_ATT_SCALE = ATT_DH ** -0.5


def _window_mask(has_prev):
    qi = lax.broadcasted_iota(I32, (ATT_BLK, 2 * ATT_BLK), 0)
    kj = lax.broadcasted_iota(I32, (ATT_BLK, 2 * ATT_BLK), 1)
    prev_ok = jnp.logical_and(jnp.logical_and(kj < ATT_BLK, kj >= qi), has_prev)
    return jnp.logical_or(prev_ok, jnp.logical_and(kj >= ATT_BLK, qi >= kj - ATT_BLK))


def _head_specs(col0):
    return pl.BlockSpec((SEQ, ATT_DH), lambda h: (0, col0 + h))


def _sub_rows(start, size, dil):
    return pl.ds(start, size) if dil == 1 else pl.ds(start, size, stride=dil)


def _att_blocks(dil):
    nb = SEQ // dil // ATT_BLK
    return [(r + dil * n * ATT_BLK, n > 0, n + 1 < nb) for r in range(dil) for n in range(nb)]


def _att_fwd(gi, dil, qkv, bias, after=()):
    blk, dh = ATT_BLK, ATT_DH
    pad = dil * blk
    col0 = 3 * ATT_HPG * gi

    def kern(q_ref, k_ref, v_ref, b_ref, *rest):
        o_ref, l_ref, qf, kpad, vpad = rest[len(after):]
        zero = jnp.zeros((pad, dh), F32)
        kpad[0:pad, :] = zero
        vpad[0:pad, :] = zero
        kpad[pad:, :] = k_ref[...].astype(F32)
        vpad[pad:, :] = v_ref[...].astype(F32)
        qf[...] = q_ref[...].astype(F32)
        bias_m = b_ref[...]
        for start, has_prev, _ in _att_blocks(dil):
            rows, window = _sub_rows(start, blk, dil), _sub_rows(start, 2 * blk, dil)
            q = qf[rows, :].astype(BF16)
            kw, vw = kpad[window, :].astype(BF16), vpad[window, :].astype(BF16)
            valid = _window_mask(has_prev)
            s = lax.dot_general(q, kw, _NT, preferred_element_type=F32) * _ATT_SCALE + bias_m
            s = jnp.where(valid, s, -1e30)
            mx = jnp.max(s, axis=-1, keepdims=True)
            e = jnp.exp(s - mx)
            den = jnp.sum(e, axis=-1, keepdims=True)
            o_ref[rows, :] = jnp.dot((e / den).astype(BF16), vw, preferred_element_type=F32)
            l_ref[rows, :] = jnp.broadcast_to(mx + jnp.log(den), (blk, dh))

    return pl.pallas_call(
        kern, name=f"att_fwd_g{gi}", grid=(ATT_HPG,),
        in_specs=[_head_specs(col0), _head_specs(col0 + ATT_HPG), _head_specs(col0 + 2 * ATT_HPG),
                  pl.BlockSpec((None, None, blk, 2 * blk), lambda h: (gi, h, 0, 0))]
        + [pl.BlockSpec(memory_space=pl.ANY)] * len(after),
        out_specs=[_head_specs(0), _head_specs(0)],
        out_shape=[jax.ShapeDtypeStruct((SEQ, ATT_W), F32), jax.ShapeDtypeStruct((SEQ, ATT_W), F32)],
        scratch_shapes=[pltpu.VMEM((SEQ, dh), F32), pltpu.VMEM((SEQ + pad, dh), F32),
                        pltpu.VMEM((SEQ + pad, dh), F32)],
        compiler_params=_cparams(("arbitrary",)),
    )(qkv, qkv, qkv, bias, *after)


def _att_bwd(gi, dil, qkv, d_att, lse, dd, bias):
    blk, dh = ATT_BLK, ATT_DH
    pad = dil * blk
    col0 = 3 * ATT_HPG * gi

    def kern(q_ref, k_ref, v_ref, do_ref, l_ref, d_ref, b_ref, dqkv_ref, dsb_ref,
             qf, kpad, vpad, dq_s, dkpad, dvpad):
        zero = jnp.zeros((pad, dh), F32)
        kpad[0:pad, :] = zero
        vpad[0:pad, :] = zero
        kpad[pad:, :] = k_ref[...].astype(F32)
        vpad[pad:, :] = v_ref[...].astype(F32)
        qf[...] = q_ref[...].astype(F32)
        dkpad[...] = jnp.zeros_like(dkpad)
        dvpad[...] = jnp.zeros_like(dvpad)
        bias_m = b_ref[...]
        ds_sum = jnp.zeros((blk, 2 * blk), F32)

        for start, has_prev, _ in _att_blocks(dil):
            rows, window = _sub_rows(start, blk, dil), _sub_rows(start, 2 * blk, dil)
            q, d_o = qf[rows, :].astype(BF16), do_ref[rows, :].astype(BF16)
            kw, vw = kpad[window, :].astype(BF16), vpad[window, :].astype(BF16)
            lrow, drow = l_ref[rows, :][:, :1], d_ref[rows, :][:, :1]
            valid = _window_mask(has_prev)
            s = lax.dot_general(q, kw, _NT, preferred_element_type=F32) * _ATT_SCALE + bias_m
            p = jnp.where(valid, jnp.exp(jnp.where(valid, s, -1e30) - lrow), 0.0)
            dp = lax.dot_general(d_o, vw, _NT, preferred_element_type=F32)
            ds = p * (dp - drow)
            ds_b = ds.astype(BF16)
            dq_s[rows, :] = jnp.dot(ds_b, kw, preferred_element_type=F32) * _ATT_SCALE
            dkpad[window, :] += lax.dot_general(ds_b, q, _TN, preferred_element_type=F32) * _ATT_SCALE
            dvpad[window, :] += lax.dot_general(p.astype(BF16), d_o, _TN, preferred_element_type=F32)
            ds_sum = ds_sum + ds
        dsb_ref[...] = ds_sum

        dqkv_ref[0] = dq_s[...].astype(BF16)
        dqkv_ref[1] = dkpad[pad:, :].astype(BF16)
        dqkv_ref[2] = dvpad[pad:, :].astype(BF16)

    return pl.pallas_call(
        kern, name=f"att_bwd_g{gi}", grid=(ATT_HPG,),
        in_specs=[_head_specs(col0), _head_specs(col0 + ATT_HPG), _head_specs(col0 + 2 * ATT_HPG),
                  _head_specs(0), _head_specs(0), _head_specs(0),
                  pl.BlockSpec((None, None, blk, 2 * blk), lambda h: (gi, h, 0, 0))],
        out_specs=[pl.BlockSpec((3, SEQ, dh), lambda h: (0, 0, h)),
                   pl.BlockSpec((None, blk, 2 * blk), lambda h: (h, 0, 0))],
        out_shape=[jax.ShapeDtypeStruct((3, SEQ, ATT_W), BF16),
                   jax.ShapeDtypeStruct((ATT_HPG, blk, 2 * blk), F32)],
        scratch_shapes=[pltpu.VMEM((SEQ, dh), F32), pltpu.VMEM((SEQ + pad, dh), F32),
                        pltpu.VMEM((SEQ + pad, dh), F32), pltpu.VMEM((SEQ, dh), F32),
                        pltpu.VMEM((SEQ + pad, dh), F32), pltpu.VMEM((SEQ + pad, dh), F32)],
        compiler_params=_cparams(("arbitrary",)),
    )(qkv, qkv, qkv, d_att, lse, dd, bias)


def _rms_parts(x):
    r = lax.rsqrt(jnp.mean(x * x, axis=-1, keepdims=True) + RMS_EPS)
    return x * r, r


def _rms_bwd(d_xhat, xhat, r):
    return r * (d_xhat - xhat * jnp.mean(d_xhat * xhat, axis=-1, keepdims=True))


def _prenorm_fwd(name, x, gain, shift, scale):
    def body(xt, g, sh, sc):
        xhat, _ = _rms_parts(xt)
        return (xhat * g) * (1.0 + sc) + sh
    return _rowmap(name, body, [x], [gain, shift, scale], [(D_MODEL, BF16)])[0]


def _prenorm_bwd_epi(d_h, x, resid, gain, scale, branch=None, gate=None):
    xhat, r = _rms_parts(x)
    nrm = xhat * gain
    d_n = d_h * (1.0 + scale)
    dx = _rms_bwd(d_n * gain, xhat, r) + resid
    sums = (jnp.sum(d_h, axis=0, keepdims=True), jnp.sum(d_h * nrm, axis=0, keepdims=True),
            jnp.sum(d_n * xhat, axis=0, keepdims=True))
    if branch is None:
        return (dx,) + sums
    return (dx, dx * gate) + sums + (jnp.sum(dx * branch, axis=0, keepdims=True),)


def _row_operands(tm, rows, vecs):
    return ([(a, (tm, D_MODEL), lambda i, j, kk: (i, 0)) for a in rows]
            + [(v, (1, D_MODEL), lambda i, j, kk: (0, 0)) for v in vecs])


def _gn_parts(ro):
    mu = jnp.mean(ro, axis=-1, keepdims=True)
    cen = ro - mu
    rstd = lax.rsqrt(jnp.mean(cen * cen, axis=-1, keepdims=True) + GN_EPS)
    return cen * rstd, rstd


MERGE_TM = 512


def _att_out(os_, ls_, w_att_out, gates, ret_out):
    tm = MERGE_TM

    def kern(o0, o1, o2, l0, l1, l2, w_ref, ga_ref, gb_ref, ro_ref, att_ref, attb_ref, lse_ref,
             ao_ref, mg_ref):
        l0v, l1v, l2v = l0[...], l1[...], l2[...]
        mx = jnp.maximum(jnp.maximum(l0v, l1v), l2v)
        e0, e1, e2 = jnp.exp(l0v - mx), jnp.exp(l1v - mx), jnp.exp(l2v - mx)
        den = e0 + e1 + e2
        att = (e0 / den) * o0[...] + (e1 / den) * o1[...] + (e2 / den) * o2[...]
        att_b = att.astype(BF16)
        att_ref[...] = att
        attb_ref[...] = att_b
        lse_ref[...] = mx + jnp.log(den)
        att_out = jnp.dot(att_b, w_ref[...], preferred_element_type=F32)
        ao_ref[...], merged = _merge_fwd_epi(att_out, ga_ref[...], gb_ref[...], ro_ref[...])
        mg_ref[...] = merged.astype(BF16)

    rows_w = pl.BlockSpec((tm, ATT_W), lambda i: (i, 0))
    rows_d = pl.BlockSpec((tm, D_MODEL), lambda i: (i, 0))
    return pl.pallas_call(
        kern, name="att_out", grid=(SEQ // tm,),
        in_specs=[rows_w] * 6 + [pl.BlockSpec((ATT_W, D_MODEL), lambda i: (0, 0)), rows_d,
                                 pl.BlockSpec((tm, D_MODEL), lambda i: (i, 1)), rows_d],
        out_specs=[rows_w, rows_w, rows_w, rows_d, rows_d],
        out_shape=[jax.ShapeDtypeStruct((SEQ, ATT_W), F32), jax.ShapeDtypeStruct((SEQ, ATT_W), BF16),
                   jax.ShapeDtypeStruct((SEQ, ATT_W), F32), jax.ShapeDtypeStruct((SEQ, D_MODEL), F32),
                   jax.ShapeDtypeStruct((SEQ, D_MODEL), BF16)],
        compiler_params=_cparams(("parallel",)),
    )(*os_, *ls_, w_att_out, gates, gates, ret_out)


def _merge_operands(gates, ret_out, att_out=None):
    ops = [(gates, (MERGE_TM, D_MODEL), lambda i, j, kk: (i, 0)),
           (gates, (MERGE_TM, D_MODEL), lambda i, j, kk: (i, 1)),
           (ret_out, (MERGE_TM, D_MODEL), lambda i, j, kk: (i, 0))]
    if att_out is not None:
        ops.append((att_out, (MERGE_TM, D_MODEL), lambda i, j, kk: (i, 0)))
    return ops


def _merge_fwd_epi(att_out, ga, gb, ret_out):
    return att_out, _sigmoid(ga.astype(F32)) * ret_out + _sigmoid(gb.astype(F32)) * att_out


def _merge_bwd_epi(d_merged, ga, gb, ret_out, att_out):
    sa, sb = _sigmoid(ga.astype(F32)), _sigmoid(gb.astype(F32))
    return (d_merged * sa, d_merged * sb, d_merged * ret_out * (sa * (1.0 - sa)),
            d_merged * att_out * (sb * (1.0 - sb)))


def _att_out_bwd_epi(d_att, att):
    outs = []
    for h in range(ATT_HPG):
        sl = slice(h * ATT_DH, (h + 1) * ATT_DH)
        outs.append(jnp.broadcast_to(jnp.sum(d_att[:, sl] * att[:, sl], axis=-1, keepdims=True),
                                     (d_att.shape[0], ATT_DH)))
    return d_att, jnp.concatenate(outs, axis=-1)


def _loss_head_epi(branch, x_prev, target, gate, gain):
    x3 = x_prev + gate * branch
    xhat, r = _rms_parts(x3)
    err = xhat * gain - target
    d_y = err / D_MODEL
    loss = 0.5 * jnp.sum(jnp.mean(err * err, axis=-1, keepdims=True), axis=0, keepdims=True)
    d_x = _rms_bwd(d_y * gain, xhat, r)
    return (d_x, d_x * gate, jnp.broadcast_to(loss, (1, D_MODEL)),
            jnp.sum(d_y * xhat, axis=0, keepdims=True), jnp.sum(d_x * branch, axis=0, keepdims=True))


def _local_step(pos, x, target, mod, norm1_g, norm2_g, norm_f_g, rel_bias, gn_g, gn_b, w_in, rest_gather):
    sh1, sc1, g1, sh2, sc2, g2 = [mod[:, i * D_MODEL:(i + 1) * D_MODEL] for i in range(6)]
    cos, sin = _rope_tables()
    din, qd, kd, cd = _decay_tables()
    buckets = _bucket_tables()
    bias = _bias_build(rel_bias, buckets)
    dils = [d for _, d in ATT_GROUPS]

    h1 = _prenorm_fwd("prenorm1_fwd", x, norm1_g, sh1, sc1)

    qk_tn = 2 * RET_DK

    def rot_epi(acc, cs, sn, scale):
        half = RET_DK // 2
        outs = []
        for h0 in range(0, qk_tn, RET_DK):
            x1, x2 = acc[:, h0:h0 + half], acc[:, h0 + half:h0 + RET_DK]
            outs += [x1 * cs - x2 * sn, x1 * sn + x2 * cs]
        return (jnp.concatenate(outs, axis=-1) * scale,)

    qk_scale = jnp.concatenate([jnp.ones((1, RET_QK_W), F32),
                                jnp.full((1, RET_QK_W), RET_DK ** -0.5, F32)], axis=-1)
    rope_ex = [(cos, (TM, RET_DK // 2), lambda i, j, kk: (i, 0)),
               (sin, (TM, RET_DK // 2), lambda i, j, kk: (i, 0)),
               (qk_scale, (1, qk_tn), lambda i, j, kk: (0, j))]
    rest_sems, rest_shards, rest_fulls, rest_token = rest_gather
    behind = [rest_token]
    rv = _matmul("proj_rv", h1, w_in, "nn", SEQ, RET_V_W, D_MODEL, [BF16], b_off=OFF_V, tk=D_MODEL,
                 after=behind)[0]
    rg = _matmul("proj_rg", h1, w_in, "nn", SEQ, RET_V_W, D_MODEL, [BF16], b_off=OFF_G, tk=D_MODEL,
                 after=behind)[0]
    gates = _matmul("proj_gates", h1, w_in, "nn", SEQ, 2 * D_MODEL, D_MODEL, [BF16], b_off=OFF_GATE,
                    tn=512, tk=D_MODEL, after=behind)[0]
    aqkv = _matmul("proj_att", h1, w_in, "nn", SEQ, 9 * ATT_W, D_MODEL, [BF16], b_off=OFF_ATT,
                   tn=512, tk=D_MODEL, after=behind)[0]

    rqk = _matmul("proj_qk", h1, w_in, "nn", SEQ, 2 * RET_QK_W, D_MODEL, [BF16], b_off=OFF_Q,
                  tn=qk_tn, tk=D_MODEL, epilogue=rot_epi, extras=rope_ex, after=behind)[0]
    ro, states, gated = _retention_fwd(rqk, rv, rg, gn_g, gn_b, din, qd, kd, cd)
    os_, ls_ = [], []
    for gi in range(3):
        if gi == 2:
            rest_sems, rest_fulls, fwd_token = _gather_rest_forward(
                rest_sems, rest_shards, rest_fulls, [gated, gates] + os_)
        o_g, l_g = _att_fwd(gi, dils[gi], aqkv, bias, after=[fwd_token] if gi == 2 else ())
        os_.append(o_g)
        ls_.append(l_g)
    w_ret_out, w_att_out, w_o, w_ff1, w_ff2 = _gather_rest_end(rest_sems, rest_fulls, [os_[2]])
    ret_out = _matmul("ret_out", gated, w_ret_out, "nn", SEQ, D_MODEL, RET_V_W, [F32], tk=RET_V_W)[0]
    att, att_b, lse, att_out, merged = _att_out(os_, ls_, w_att_out, gates, ret_out)

    def mix_epi(acc, xt, g, gain, sh, sc):
        x_new = xt + g * acc
        xhat, _ = _rms_parts(x_new)
        return x_new, acc, (xhat * gain) * (1.0 + sc) + sh

    x2, mix, h2 = _matmul("mix_out", merged, w_o, "nn", SEQ, D_MODEL, D_MODEL, [F32, BF16, BF16],
                          epilogue=mix_epi, extras=_row_operands(TM, [x], [g1, norm2_g, sh2, sc2]))

    def relu2_epi(acc):
        r = jnp.maximum(acc, 0.0)
        return r * r, r

    act, relu_u = _matmul("ff1", h2, w_ff1, "nn", SEQ, D_FF, D_MODEL, [BF16, BF16], tk=D_MODEL,
                          epilogue=relu2_epi)
    d_x3, d_y2, loss, d_gf, d_g2 = _matmul(
        "ff2", act, w_ff2, "nn", SEQ, D_MODEL, D_FF, [F32, BF16], tm=TM, tk=1024, n_sums=3,
        epilogue=_loss_head_epi, extras=_row_operands(TM, [x2, target], [g2, norm_f_g]))

    def relu2_bwd_epi(acc, rt):
        return (acc * (2.0 * rt.astype(F32)),)

    gw_ff2 = _matmul_tn_pair("ff2_dw", pos, act, d_y2, D_FF, D_MODEL, SEQ, D_FF // N_CHIPS,
                             tm=512, tn=1024, tk=SEQ)
    d_u = _matmul("ff2_dx", d_y2, w_ff2, "nt", SEQ, D_FF, D_MODEL, [BF16], epilogue=relu2_bwd_epi,
                  extras=[(relu_u, (TM, TN), lambda i, j, kk: (i, j))])[0]
    gw_ff1 = _matmul_tn_pair("ff1_dw", pos, h2, d_u, D_MODEL, D_FF, SEQ, D_MODEL,
                             tm=512, tn=1024, tk=SEQ)
    ffn = ["w_ff2", "w_ff1"]
    ffn_started = _ici_start("ici_start_ffn", ffn, [gw_ff2, gw_ff1])
    d_x2, d_mix, d_sh2, d_sc2, d_n2g, d_g1 = _matmul(
        "ff1_dx", d_u, w_ff1, "nt", SEQ, D_MODEL, D_FF, [F32, BF16], tm=TM, tk=1024, n_sums=4,
        epilogue=_prenorm_bwd_epi, extras=_row_operands(TM, [x2, d_x3], [norm2_g, sc2])
        + _row_operands(TM, [mix], [g1]), after=[ffn_started[3]])
    gw_o = _matmul_tn_pair("mix_dw", pos, merged, d_mix, D_MODEL, D_MODEL, SEQ, D_MODEL // N_CHIPS,
                           tm=128, tn=1024, tk=2048)
    d_ret_out, d_att_out, d_ga, d_gb = _matmul(
        "mix_dx", d_mix, w_o, "nt", SEQ, D_MODEL, D_MODEL, [BF16] * 4, tm=MERGE_TM,
        epilogue=_merge_bwd_epi, extras=_merge_operands(gates, ret_out, att_out))

    gw_ret_out = _matmul_tn_pair("ret_out_dw", pos, gated, d_ret_out, RET_V_W, D_MODEL, SEQ,
                                 RET_V_W // N_CHIPS, tm=256, tn=1024, tk=SEQ)
    gw_att_out = _matmul_tn_pair("att_out_dw", pos, att_b, d_att_out, ATT_W, D_MODEL, SEQ, ATT_W,
                                 tm=256, tn=1024, tk=2048)
    mixer = ["w_o", "w_ret_out", "w_att_out"]
    mixer_started = _ici_start("ici_start_mixer", mixer, [gw_o, gw_ret_out, gw_att_out])
    d_gated = _matmul("ret_out_dx", d_ret_out, w_ret_out, "nt", SEQ, RET_V_W, D_MODEL, [BF16],
                      after=[mixer_started[3]])[0]
    d_att, dd = _matmul("att_out_dx", d_att_out, w_att_out, "nt", SEQ, ATT_W, D_MODEL, [F32, F32],
                        epilogue=_att_out_bwd_epi,
                        extras=[(att, (TM, ATT_W), lambda i, j, kk: (i, 0))], after=[mixer_started[3]])

    d_rqkv, d_rg, d_gn_g, d_gn_b = _retention_bwd(rqk, rv, states, d_gated, ro, rg, gn_g, gn_b,
                                                  din, qd, kd, cd, cos, sin)

    d_aqkv, dsbs = [], []
    for gi in range(3):
        dqkv, dsb = _att_bwd(gi, dils[gi], aqkv, d_att, lse, dd, bias)
        d_aqkv.append(dqkv)
        dsbs.append(dsb)
    d_rel_bias = _bias_grad(jnp.stack(dsbs), buckets)

    d_proj = ([(d_rqkv, False), (d_rg, False)] + [(t, True) for t in d_aqkv]
              + [(d_ga, False), (d_gb, False)])
    gw_in = _matmul_tn_pair("proj_dw", pos, h1, d_proj, D_MODEL, IN_COLS, SEQ, D_MODEL,
                            tm=512, tn=ATT_W, tk=SEQ)
    sems, (gw_in,), (land,), token = _ici_start("ici_start_w_in", ["w_in"], [gw_in])
    grad_x, d_sh1, d_sc1, d_n1g = _matmul(
        "proj_dx", d_proj, w_in, "nt", SEQ, D_MODEL, IN_COLS, [F32], tn=1024, tk=ATT_W, n_sums=3,
        epilogue=_prenorm_bwd_epi, extras=_row_operands(TM, [x, d_x2], [norm1_g, sc1]), after=[token])
    pending = (sems, land)

    names = ffn + mixer
    psums, got = _ici_wait("ici_wait_rest", names, list(ffn_started[0]) + list(mixer_started[0]),
                           list(ffn_started[1]) + list(mixer_started[1]),
                           list(ffn_started[2]) + list(mixer_started[2]), [grad_x])
    g_big = {n: _final_sum("final_" + n, pos, dict(BIG)[n], psums[i], got[i], SHARD[n])
             for i, n in enumerate(names)}
    d_mod = jnp.concatenate([d_sh1, d_sc1, d_g1, d_sh2, d_sc2, d_g2], axis=-1)
    small = dict(norm1_g=d_n1g, norm2_g=d_n2g, norm_f_g=d_gf, gn_g=d_gn_g, gn_b=d_gn_b,
                 rel_bias=d_rel_bias)
    return loss, grad_x, d_mod, small, g_big, (gw_in,) + pending


def _me():
    return lax.axis_index("x"), lax.axis_index("y"), lax.axis_index("c")


def _peer(x, y, c, mask):
    return (x ^ ((mask >> 2) & 1), y ^ ((mask >> 1) & 1), c ^ (mask & 1))


def _gather8(src_ref, dst_ref, send_sems, recv_sems):
    x, y, c = _me()
    me = 4 * x + 2 * y + c
    copies = []
    for mask in range(1, N_DEV):
        cp = pltpu.make_async_remote_copy(
            src_ref=src_ref, dst_ref=dst_ref.at[me], send_sem=send_sems.at[mask - 1],
            recv_sem=recv_sems.at[mask - 1], device_id=_peer(x, y, c, mask), device_id_type=MESH)
        cp.start()
        copies.append(cp)
    dst_ref[me] = src_ref[...]
    for cp in copies:
        cp.wait_recv()
    for cp in copies:
        cp.wait_send()


def _ada_fwd(c_in, w_ada, b_ada):
    ncol = ADA_COLS // N_CHIPS

    def body(c_ref, w_ref, b_ref, mod_ref, sc_ref, cbuf, cg, mbuf, mg, s1, r1, s2, r2):
        x, y, c = _me()
        me = 4 * x + 2 * y + c
        cv = c_ref[...]
        cbuf[...] = jnp.broadcast_to(cv * _sigmoid(cv), cbuf.shape)
        _gather8(cbuf, cg, s1, r1)
        rows = lax.broadcasted_iota(I32, (N_DEV, D_MODEL), 0)
        sc_all = jnp.zeros((N_DEV, D_MODEL), F32)
        for d in range(N_DEV):
            sc_all = jnp.where(rows == d, cg[d], sc_all)
        sc_ref[...] = sc_all
        mbuf[...] = jnp.dot(sc_all.astype(BF16), w_ref[...].astype(BF16), preferred_element_type=F32)
        _gather8(mbuf, mg, s2, r2)
        rowsel = lax.broadcasted_iota(I32, (N_DEV, ncol), 0) == me
        for k in range(N_CHIPS):
            blk = mg[2 * k]
            row = jnp.sum(jnp.where(rowsel, blk, 0.0), axis=0, keepdims=True)
            mod_ref[:, k * ncol:(k + 1) * ncol] = row + b_ref[:, k * ncol:(k + 1) * ncol]

    vm = pl.BlockSpec(memory_space=pltpu.VMEM)
    return pl.pallas_call(
        body, name="ada_fwd",
        in_specs=[vm, vm, vm], out_specs=[vm, vm],
        out_shape=[jax.ShapeDtypeStruct((1, ADA_COLS), F32), jax.ShapeDtypeStruct((N_DEV, D_MODEL), F32)],
        scratch_shapes=[
            pltpu.VMEM((8, D_MODEL), F32), pltpu.VMEM((N_DEV, 8, D_MODEL), F32),
            pltpu.VMEM((8, ncol), F32), pltpu.VMEM((N_DEV, 8, ncol), F32),
            pltpu.SemaphoreType.DMA((N_DEV - 1,)), pltpu.SemaphoreType.DMA((N_DEV - 1,)),
            pltpu.SemaphoreType.DMA((N_DEV - 1,)), pltpu.SemaphoreType.DMA((N_DEV - 1,)),
        ],
        compiler_params=pltpu.CompilerParams(vmem_limit_bytes=VMEM_LIMIT_V7X),
    )(c_in, w_ada, b_ada)


def _small_reduce(pack, sc_all, after=()):
    ncol = ADA_COLS // N_CHIPS

    def body(p_ref, sc_ref, *rest):
        tot_ref, gw_ref, pg, s1, r1 = rest[len(after):]
        x, y, _ = _me()
        chip = 2 * x + y
        _gather8(p_ref, pg, s1, r1)
        tot = pg[0]
        for d in range(1, N_DEV):
            tot = tot + pg[d]
        tot_ref[...] = tot
        rows = lax.broadcasted_iota(I32, (N_DEV, ncol), 0)
        dmod = jnp.zeros((N_DEV, ncol), F32)
        for k in range(N_CHIPS):
            part = jnp.zeros((N_DEV, ncol), F32)
            for d in range(N_DEV):
                part = jnp.where(rows == d, pg[d, :, k * ncol:(k + 1) * ncol][0:1, :], part)
            dmod = jnp.where(chip == k, part, dmod)
        gw_ref[...] = lax.dot_general(sc_ref[...].astype(BF16), dmod.astype(BF16), _TN,
                                      preferred_element_type=F32)

    vm = pl.BlockSpec(memory_space=pltpu.VMEM)
    return pl.pallas_call(
        body, name="small_reduce",
        in_specs=[vm, vm] + [pl.BlockSpec(memory_space=pl.ANY)] * len(after), out_specs=[vm, vm],
        out_shape=[jax.ShapeDtypeStruct((8, ADA_COLS), F32), jax.ShapeDtypeStruct((D_MODEL, ncol), F32)],
        scratch_shapes=[pltpu.VMEM((N_DEV, 8, ADA_COLS), F32),
                        pltpu.SemaphoreType.DMA((N_DEV - 1,)), pltpu.SemaphoreType.DMA((N_DEV - 1,))],
        compiler_params=pltpu.CompilerParams(vmem_limit_bytes=VMEM_LIMIT_V7X),
    )(pack, sc_all, *after)


BIG = (("w_in", 1), ("w_ret_out", 0), ("w_att_out", 1), ("w_o", 0), ("w_ff1", 1), ("w_ff2", 0))
SHARD = {"w_in": (D_MODEL, IN_COLS // N_CHIPS), "w_ret_out": (RET_V_W // N_CHIPS, D_MODEL),
         "w_att_out": (ATT_W, D_MODEL // N_CHIPS), "w_o": (D_MODEL // N_CHIPS, D_MODEL),
         "w_ff1": (D_MODEL, D_FF // N_CHIPS), "w_ff2": (D_FF // N_CHIPS, D_MODEL)}
_CHIP_FLIPS = ((1, 0), (0, 1), (1, 1))


def _region(ref, axis, chip, half, shard_shape):
    r, cw = shard_shape
    hr = r // 2
    if axis == 1:
        return ref.at[pl.ds(half * hr, hr), pl.ds(chip * cw, cw)]
    return ref.at[pl.ds(chip * r + half * hr, hr), :]


def _gather_weights(shards, n_remote):
    nw = len(BIG)
    shapes = [s.shape for s in shards]
    full_shapes = [(r, N_CHIPS * cw) if ax == 1 else (N_CHIPS * r, cw)
                   for (r, cw), (_, ax) in zip(shapes, BIG)]

    def body(*refs):
        ins, outs = refs[:nw], refs[nw:2 * nw]
        own = refs[2 * nw:3 * nw]
        from_ici, from_sib = refs[3 * nw:3 * nw + n_remote], refs[3 * nw + n_remote:3 * nw + 2 * n_remote]
        ld_sem, st_sem, s_ici, r_ici, s_d2d, r_d2d, st_a, st_b = refs[3 * nw + 2 * n_remote:]
        x, y, c = _me()
        chip = 2 * x + y
        sib = (x, y, 1 - c)
        loads = [pltpu.make_async_copy(ins[i], own[i], ld_sem.at[i]) for i in range(nw)]
        for cp in loads:
            cp.start()
        pending, first = [], []
        for i, (_, ax) in enumerate(BIG):
            r, cw = shapes[i]
            hr = r // 2
            loads[i].wait()
            dst = outs[i].at[:, pl.ds(chip * cw, cw)] if ax == 1 else outs[i].at[pl.ds(chip * r, r), :]
            cp = pltpu.make_async_copy(own[i], dst, st_sem.at[i])
            cp.start()
            pending.append(cp)
            for j, (fx, fy) in enumerate(_CHIP_FLIPS if i < n_remote else ()):
                rc = pltpu.make_async_remote_copy(
                    src_ref=own[i].at[pl.ds(c * hr, hr), :], dst_ref=from_ici[i].at[j],
                    send_sem=s_ici.at[j * nw + i], recv_sem=r_ici.at[j * nw + i],
                    device_id=(x ^ fx, y ^ fy, c), device_id_type=MESH)
                rc.start()
                first.append((j, i, rc))
        passed = []
        for j, i, rc in first:
            fx, fy = _CHIP_FLIPS[j]
            src_chip = 2 * (x ^ fx) + (y ^ fy)
            ax = BIG[i][1]
            rc.wait_recv()
            fw = pltpu.make_async_remote_copy(
                src_ref=from_ici[i].at[j], dst_ref=from_sib[i].at[j], send_sem=s_d2d.at[j * nw + i],
                recv_sem=r_d2d.at[j * nw + i], device_id=sib, device_id_type=MESH)
            fw.start()
            passed.append((j, i, src_chip, fw))
            st = pltpu.make_async_copy(from_ici[i].at[j], _region(outs[i], ax, src_chip, c, shapes[i]),
                                       st_a.at[j * nw + i])
            st.start()
            pending.append(st)
        for j, i, src_chip, fw in passed:
            fw.wait_recv()
            st = pltpu.make_async_copy(from_sib[i].at[j],
                                       _region(outs[i], BIG[i][1], src_chip, 1 - c, shapes[i]),
                                       st_b.at[j * nw + i])
            st.start()
            pending.append(st)
        for _, _, rc in first:
            rc.wait_send()
        for _, _, _, fw in passed:
            fw.wait_send()
        for cp in pending:
            cp.wait()

    hbm = pl.BlockSpec(memory_space=pl.ANY)
    halves = [pltpu.VMEM((3, r // 2, cw), BF16) for r, cw in shapes[:n_remote]]
    return pl.pallas_call(
        body, name="gather_weights",
        in_specs=[hbm] * nw, out_specs=[hbm] * nw,
        out_shape=[jax.ShapeDtypeStruct(fs, BF16) for fs in full_shapes],
        scratch_shapes=[pltpu.VMEM(sh, BF16) for sh in shapes] + halves + halves
        + [pltpu.SemaphoreType.DMA((nw,)), pltpu.SemaphoreType.DMA((nw,))]
        + [pltpu.SemaphoreType.DMA((3 * nw,))] * 6,
        compiler_params=pltpu.CompilerParams(vmem_limit_bytes=VMEM_LIMIT_V7X),
    )(*shards)


REST = BIG[1:]
_SIDE_EFFECTS = pltpu.CompilerParams(has_side_effects=pltpu.SideEffectType.DATAFLOW_SIDE_EFFECTING)
_ANY_SPEC = pl.BlockSpec(memory_space=pl.ANY)


def _rest_ici_copies(shard_refs, full_refs, sems):
    x, y, c = _me()
    chip = 2 * x + y
    n = 3 * len(REST)
    copies = []
    for i, (name, ax) in enumerate(REST):
        hr = SHARD[name][0] // 2
        for j, (fx, fy) in enumerate(_CHIP_FLIPS):
            copies.append(pltpu.make_async_remote_copy(
                src_ref=shard_refs[i].at[pl.ds(c * hr, hr), :],
                dst_ref=_region(full_refs[i], ax, chip, c, SHARD[name]),
                send_sem=sems[3 * i + j], recv_sem=sems[n + 3 * i + j],
                device_id=(x ^ fx, y ^ fy, c), device_id_type=MESH))
    return copies


def _rest_d2d_copies(full_refs, sems):
    x, y, c = _me()
    n = 3 * len(REST)
    copies = []
    for i, (name, ax) in enumerate(REST):
        for j, (fx, fy) in enumerate(_CHIP_FLIPS):
            reg = _region(full_refs[i], ax, 2 * (x ^ fx) + (y ^ fy), c, SHARD[name])
            copies.append(pltpu.make_async_remote_copy(
                src_ref=reg, dst_ref=reg, send_sem=sems[3 * i + j], recv_sem=sems[n + 3 * i + j],
                device_id=(x, y, 1 - c), device_id_type=MESH))
    return copies


def _gather_rest_start(shards, fulls, after):
    nr, ns, na = len(REST), 6 * len(REST), len(after)

    def body(*refs):
        for cp in _rest_ici_copies(refs[:nr], refs[nr:2 * nr], refs[2 * nr + na:2 * nr + na + ns]):
            cp.start()
        token = refs[-1]
        token[...] = jnp.zeros_like(token)

    hbm = lambda a: pltpu.HBM(a.shape, a.dtype)
    res = pl.pallas_call(
        body, name="gather_rest_start",
        out_shape=(pltpu.SemaphoreType.DMA(()),) * ns + tuple(hbm(a) for a in shards + fulls)
        + (jax.ShapeDtypeStruct((8, 128), F32),),
        in_specs=(_HBM_SPEC,) * (2 * nr) + (_ANY_SPEC,) * na,
        out_specs=(_SEM_SPEC,) * ns + (_HBM_SPEC,) * (2 * nr) + (pl.BlockSpec(memory_space=pltpu.VMEM),),
        input_output_aliases={k: ns + k for k in range(2 * nr)}, compiler_params=_SIDE_EFFECTS,
    )(*[pltpu.with_memory_space_constraint(a, pltpu.HBM) for a in shards + fulls], *after)
    return res[:ns], res[ns:ns + nr], res[ns + nr:ns + 2 * nr], res[-1]


def _gather_rest_forward(sems, shards, fulls, after):
    nr, ns = len(REST), 6 * len(REST)

    def body(*refs):
        shard_refs, full_refs, old = refs[:nr], refs[nr:2 * nr], refs[2 * nr:2 * nr + ns]
        new = refs[2 * nr + ns + len(after):2 * nr + 2 * ns + len(after)]
        for cp in _rest_ici_copies(shard_refs, full_refs, old):
            cp.wait_send()
            cp.wait_recv()
        for cp in _rest_d2d_copies(full_refs, new):
            cp.start()
        token = refs[-1]
        token[...] = jnp.zeros_like(token)

    res = pl.pallas_call(
        body, name="gather_rest_forward",
        out_shape=(pltpu.SemaphoreType.DMA(()),) * ns + tuple(pltpu.HBM(a.shape, a.dtype) for a in fulls)
        + (jax.ShapeDtypeStruct((8, 128), F32),),
        in_specs=(_HBM_SPEC,) * (2 * nr) + (_SEM_SPEC,) * ns + (_ANY_SPEC,) * len(after),
        out_specs=(_SEM_SPEC,) * ns + (_HBM_SPEC,) * nr + (pl.BlockSpec(memory_space=pltpu.VMEM),),
        input_output_aliases={nr + k: ns + k for k in range(nr)}, compiler_params=_SIDE_EFFECTS,
    )(*shards, *fulls, *sems, *after)
    return res[:ns], res[ns:ns + nr], res[-1]


def _gather_rest_end(sems, fulls, after):
    nr, ns = len(REST), 6 * len(REST)

    def body(*refs):
        for cp in _rest_d2d_copies(refs[:nr], refs[nr:nr + ns]):
            cp.wait_send()
            cp.wait_recv()

    return pl.pallas_call(
        body, name="gather_rest_end",
        out_shape=tuple(pltpu.HBM(a.shape, a.dtype) for a in fulls),
        in_specs=(_HBM_SPEC,) * nr + (_SEM_SPEC,) * ns + (_ANY_SPEC,) * len(after),
        out_specs=(_HBM_SPEC,) * nr,
        input_output_aliases={k: k for k in range(nr)}, compiler_params=_SIDE_EFFECTS,
    )(*fulls, *sems, *after)


def _adam_update(w, g, m, v):
    mn = ADAM_B1 * m + (1.0 - ADAM_B1) * g
    vn = ADAM_B2 * v + (1.0 - ADAM_B2) * (g * g)
    m_hat = mn / (1.0 - ADAM_B1 ** ADAM_STEP)
    v_hat = vn / (1.0 - ADAM_B2 ** ADAM_STEP)
    return -ADAM_LR * (m_hat / (jnp.sqrt(v_hat) + ADAM_EPS) + ADAM_WD * w), mn, vn


def _final_sum(name, pos, axis, psum, recv, shard_shape, after=(), tr=128):
    r, cw = shard_shape
    hr = r // 2
    tr = min(tr, hr)
    nt = hr // tr
    n_after = len(after)

    def kern(pos_ref, p_ref, r_ref, *rest):
        g_ref, send_buf, land_buf, s_sem, r_sem = rest[n_after:]
        p, t = pl.program_id(0), pl.program_id(1)
        sib = _sibling()

        def copy(i):
            return pltpu.make_async_remote_copy(
                src_ref=send_buf.at[i], dst_ref=land_buf.at[i], send_sem=s_sem.at[i],
                recv_sem=r_sem.at[i], device_id=sib, device_id_type=MESH)

        @pl.when(p == 0)
        def _():
            tot = p_ref[...].astype(F32)
            for j in range(3):
                tot = tot + r_ref[j].astype(F32)
            send_buf[t] = tot
            copy(t).start()
            g_ref[...] = tot

        @pl.when(p == 1)
        def _():
            copy(t).wait_recv()
            g_ref[...] = land_buf[t]

        @pl.when(jnp.logical_and(p == 1, t == nt - 1))
        def _():
            for i in range(nt):
                copy(i).wait_send()

    def shard_rows(p, t, pos_ref):
        return (jnp.where(p == 0, pos_ref[0], 1 - pos_ref[0]) * nt + t, 0)

    def own_part(p, t, pos_ref):
        tt = jnp.where(p == 0, t, nt - 1)
        return (tt, pos_ref[1]) if axis == 1 else (pos_ref[1] * nt + tt, 0)

    grid_spec = pltpu.PrefetchScalarGridSpec(
        num_scalar_prefetch=1, grid=(2, nt),
        in_specs=[pl.BlockSpec((tr, cw), own_part),
                  pl.BlockSpec((3, tr, cw), lambda p, t, pos_ref: (0, jnp.where(p == 0, t, nt - 1), 0))]
        + [pl.BlockSpec(memory_space=pl.ANY)] * n_after,
        out_specs=pl.BlockSpec((tr, cw), shard_rows),
        scratch_shapes=[pltpu.VMEM((nt, tr, cw), F32), pltpu.VMEM((nt, tr, cw), F32),
                        pltpu.SemaphoreType.DMA((nt,)), pltpu.SemaphoreType.DMA((nt,))])
    return pl.pallas_call(
        kern, name=name, grid_spec=grid_spec, out_shape=jax.ShapeDtypeStruct((r, cw), F32),
        compiler_params=_cparams(("arbitrary", "arbitrary")),
    )(pos, psum, recv, *after)


def _adamw(name, w, g, m, v):
    r, cw = w.shape
    tr = min(r, 128)

    def kern(w_ref, g_ref, m_ref, v_ref, go_ref, d_ref, nm_ref, nv_ref):
        gv = g_ref[...]
        go_ref[...] = gv
        d_ref[...], nm_ref[...], nv_ref[...] = _adam_update(w_ref[...], gv, m_ref[...], v_ref[...])

    spec = pl.BlockSpec((tr, cw), lambda i: (i, 0))
    return pl.pallas_call(
        kern, name=name, grid=(r // tr,), in_specs=[spec] * 4, out_specs=[spec] * 4,
        out_shape=[jax.ShapeDtypeStruct((r, cw), F32)] * 4, compiler_params=_cparams(("parallel",)),
    )(w, g, m, v)


_PACK_W = ADA_COLS
_NB = REL_BUCKETS * N_ATT_HEADS
_SMALL_SLOTS = {
    "b_ada": (0, 0, ADA_COLS),
    "norm1_g": (1, 0, D_MODEL), "norm2_g": (1, D_MODEL, D_MODEL), "norm_f_g": (1, 2 * D_MODEL, D_MODEL),
    "ret_gn_g": (1, 3 * D_MODEL, RET_V_W),
    "ret_gn_b": (2, 0, RET_V_W), "rel_bias": (2, RET_V_W, _NB), "loss": (2, RET_V_W + 512, 128),
}


def _pack_small(vals):
    rows = []
    for r in range(8):
        items = sorted([(off, n) for n, (rr, off, _) in _SMALL_SLOTS.items() if rr == r and n in vals])
        parts, pos = [], 0
        for off, n in items:
            if off > pos:
                parts.append(jnp.zeros((1, off - pos), F32))
            parts.append(vals[n].reshape(1, -1).astype(F32))
            pos = off + _SMALL_SLOTS[n][2]
        if pos < _PACK_W:
            parts.append(jnp.zeros((1, _PACK_W - pos), F32))
        rows.append(jnp.concatenate(parts, axis=-1))
    return jnp.concatenate(rows, axis=0)


def _adamw_small(tot, names, wmv):
    n = len(names)

    def kern(tot_ref, *refs):
        ins, outs = refs[:3 * n], refs[3 * n:]
        for i, name in enumerate(names):
            row, off, width = _SMALL_SLOTS[name]
            g = tot_ref[row:row + 1, off:off + width]
            outs[i][...] = g
            outs[n + i][...], outs[2 * n + i][...], outs[3 * n + i][...] = _adam_update(
                ins[i][...], g, ins[n + i][...], ins[2 * n + i][...])

    vm = pl.BlockSpec(memory_space=pltpu.VMEM)
    shapes = [jax.ShapeDtypeStruct((1, _SMALL_SLOTS[name][2]), F32) for name in names]
    res = pl.pallas_call(
        kern, name="adamw_small", in_specs=[vm] * (1 + 3 * n), out_specs=[vm] * (4 * n),
        out_shape=shapes * 4,
    )(tot, *wmv[0], *wmv[1], *wmv[2])
    return res[:n], res[n:2 * n], res[2 * n:3 * n], res[3 * n:]


def _unpack_small(pack, name):
    r, off, wd = _SMALL_SLOTS[name]
    return pack[r:r + 1, off:off + wd]


def kernel(x, c, w_ada, b_ada, norm1_g, w_in, rel_bias, ret_gn_g, ret_gn_b, w_ret_out, w_att_out, w_o, norm2_g, w_ff1, w_ff2, norm_f_g, loss_target, m_w_ada, m_b_ada, m_norm1_g, m_w_in, m_rel_bias, m_ret_gn_g, m_ret_gn_b, m_w_ret_out, m_w_att_out, m_w_o, m_norm2_g, m_w_ff1, m_w_ff2, m_norm_f_g, v_w_ada, v_b_ada, v_norm1_g, v_w_in, v_rel_bias, v_ret_gn_g, v_ret_gn_b, v_w_ret_out, v_w_att_out, v_w_o, v_norm2_g, v_w_ff1, v_w_ff2, v_norm_f_g):
    given = dict(locals())
    big_names = [n for n, _ in BIG]
    shard_w = {n: given[n][0] for n in big_names}
    assert all(shard_w[n].shape == SHARD[n] for n in big_names)

    shards_bf = [shard_w[n].astype(BF16) for n in big_names]
    full = _gather_weights(shards_bf, 1)
    mod, sc_all = _ada_fwd(c, w_ada[0], b_ada)
    rest_gather = _gather_rest_start(shards_bf[1:], list(full[1:]), [mod])
    pos = _where_am_i()

    loss, grad_x, d_mod, small, g_big, pending = _local_step(
        pos, x[0], loss_target[0], mod, norm1_g, norm2_g, norm_f_g.reshape(1, -1), rel_bias, ret_gn_g,
        ret_gn_b, full[0], rest_gather)

    pack_g = _pack_small(dict(b_ada=d_mod, norm1_g=small["norm1_g"], norm2_g=small["norm2_g"],
                              norm_f_g=small["norm_f_g"], ret_gn_g=small["gn_g"], ret_gn_b=small["gn_b"],
                              rel_bias=small["rel_bias"], loss=loss[:, :128]))
    tot, g_w_ada = _small_reduce(pack_g, sc_all, after=list(g_big.values()))

    small_names = ["b_ada", "norm1_g", "rel_bias", "ret_gn_g", "ret_gn_b", "norm2_g", "norm_f_g"]
    small_out = _adamw_small(tot, small_names, [[given[p + n].reshape(1, -1) for n in small_names]
                                                for p in ("", "m_", "v_")])
    grads, deltas, new_m, new_v = ({n: t.reshape(given[n].shape) for n, t in zip(small_names, group)}
                                   for group in small_out)
    sd = deltas["b_ada"]
    g_big["w_ada"] = g_w_ada
    for n in ["w_ada"] + big_names[1:] + big_names[:1]:
        if n == "w_in":
            gw_in, sems, land = pending
            done = [tot, sd] + [deltas[k] for k in ["w_ada"] + big_names[1:]]
            (gw_in,), (got,) = _ici_wait("ici_wait_w_in", [n], sems, [gw_in], [land], done)
            g_big[n] = _final_sum("final_w_in", pos, 1, gw_in, got, SHARD[n])
        g, d, nm, nv = _adamw("adamw_" + n, given[n][0], g_big[n], given["m_" + n][0], given["v_" + n][0])
        grads[n], deltas[n], new_m[n], new_v[n] = g[None], d[None], nm[None], nv[None]

    order = ["w_ada", "b_ada", "norm1_g", "w_in", "rel_bias", "ret_gn_g", "ret_gn_b", "w_ret_out",
             "w_att_out", "w_o", "norm2_g", "w_ff1", "w_ff2", "norm_f_g"]
    loss_out = _unpack_small(tot, "loss")[0, 0]
    return (loss_out, grad_x[None], *[grads[n] for n in order], *[deltas[n] for n in order],
            *[new_m[n] for n in order], *[new_v[n] for n in order])
```

```python
import math

import jax
import jax.numpy as jnp
import numpy as np
from jax import lax
from jax.experimental import pallas as pl
from jax.experimental.pallas import tpu as pltpu

F32 = jnp.float32
BF16 = jnp.bfloat16
I32 = jnp.int32

SEQ = 2048
D_MODEL = 1024
RET_HEADS = 4
RET_DK = 256
RET_DV = 512
RET_CHUNK = 128
RET_SUB = 2
RET_QK_W = RET_HEADS * RET_DK
RET_V_W = RET_HEADS * RET_DV
ATT_GROUPS = ((128, 1), (512, 4), (2048, 16))
ATT_HPG = 4
ATT_DH = 128
ATT_W = ATT_HPG * ATT_DH
ATT_BLK = 128
REL_BUCKETS = 32
REL_MAX_DIST = 2048
N_ATT_HEADS = 12
D_FF = 4 * D_MODEL
RMS_EPS = 1e-6
GN_EPS = 1e-5
ROPE_BASE = 10000.0
IN_COLS = 2 * RET_QK_W + 2 * RET_V_W + 9 * ATT_W + 2 * D_MODEL
OFF_Q, OFF_K, OFF_V, OFF_G = 0, RET_QK_W, 2 * RET_QK_W, 2 * RET_QK_W + RET_V_W
OFF_ATT = 2 * RET_QK_W + 2 * RET_V_W
OFF_GATE = OFF_ATT + 9 * ATT_W
N_CHIPS = 4
N_DEV = 8
ADA_COLS = 6 * D_MODEL

ADAM_LR = 0.001
ADAM_B1 = 0.9
ADAM_B2 = 0.999
ADAM_EPS = 1e-08
ADAM_WD = 0.01
ADAM_STEP = 10

VMEM_LIMIT_V7X = 56 * 1024 * 1024
MESH = pl.DeviceIdType.MESH


def _cparams(sem):
    return pltpu.CompilerParams(dimension_semantics=sem, vmem_limit_bytes=VMEM_LIMIT_V7X)


def _sigmoid(v):
    return 1.0 / (1.0 + jnp.exp(-v))


TM, TN = 1024, 1024


def _piece_chunks(piece, width):
    arr, stacked = piece
    return arr.shape[0] if stacked else arr.shape[1] // width


def _piece_spec(piece, rows, width, start, row_of, chunk_of):
    arr, stacked = piece
    last = _piece_chunks(piece, width) - 1

    def local(*ids):
        return jnp.clip(chunk_of(*ids) - start, 0, last)

    def row(*ids):
        rel = chunk_of(*ids) - start
        return jnp.where(jnp.logical_and(rel >= 0, rel <= last), row_of(*ids), 0)

    if stacked:
        return pl.BlockSpec((None, rows, width), lambda *ids: (local(*ids), row(*ids), 0))
    return pl.BlockSpec((rows, width), lambda *ids: (row(*ids), local(*ids)))


def _piece_starts(pieces, width):
    return [sum(_piece_chunks(p, width) for p in pieces[:q]) for q in range(len(pieces))]


def _matmul(name, a, b, kind, m, n, k, outs, *, b_off=0, tm=TM, tn=TN, tk=1024,
            epilogue=None, extras=(), after=(), n_sums=0):
    tm, tn, tk = min(tm, m), min(tn, n), min(tk, k)
    nk = k // tk
    pieces = a if isinstance(a, list) else [(a, False)]
    starts = _piece_starts(pieces, tk)
    if kind == "nn":
        a_specs = [pl.BlockSpec((tm, tk), lambda i, j, kk: (i, kk))]
        b_spec = pl.BlockSpec((tk, tn), lambda i, j, kk: (kk, b_off // tn + j))
        dn = (((1,), (0,)), ((), ()))
    elif kind == "nt":
        a_specs = [_piece_spec(p, tm, tk, st, lambda i, j, kk: i, lambda i, j, kk: kk)
                   for p, st in zip(pieces, starts)]
        b_spec = pl.BlockSpec((tn, tk), lambda i, j, kk: (j, b_off // tk + kk))
        dn = (((1,), (1,)), ((), ()))
    else:
        a_specs = [pl.BlockSpec((tk, tm), lambda i, j, kk: (kk, i))]
        b_spec = pl.BlockSpec((tk, tn), lambda i, j, kk: (kk, j))
        dn = (((0,), (0,)), ((), ()))
    n_a, n_ex, n_out = len(pieces), len(extras), len(outs)
    if epilogue is None:
        epilogue = lambda acc: (acc,)

    assert n_sums == 0 or tn == n

    def finish(acc, ex_refs, out_refs, first_rows):
        res = epilogue(acc, *[r[...] for r in ex_refs])
        for r, v in zip(out_refs[:n_out], res[:n_out]):
            r[...] = v.astype(r.dtype)
        for r, v in zip(out_refs[n_out:], res[n_out:]):
            @pl.when(first_rows)
            def _(r=r, v=v):
                r[...] = v

            @pl.when(jnp.logical_not(first_rows))
            def _(r=r, v=v):
                r[...] += v

    n_in = n_a + 1 + n_ex + len(after)

    def kern(*refs):
        a_refs, b_ref = refs[:n_a], refs[n_a]
        ex_refs = refs[n_a + 1:n_a + 1 + n_ex]
        out_refs = refs[n_in:n_in + n_out + n_sums]
        first_rows, kk = pl.program_id(0) == 0, pl.program_id(2)
        dot = lambda a_ref: lax.dot_general(a_ref[...], b_ref[...], dn, preferred_element_type=F32)
        if nk == 1:
            finish(dot(a_refs[0]), ex_refs, out_refs, first_rows)
            return
        acc_ref = refs[n_in + n_out + n_sums]
        if n_a == 1:
            part = dot(a_refs[0])

            @pl.when(kk == 0)
            def _():
                acc_ref[...] = part

            @pl.when(kk > 0)
            def _():
                acc_ref[...] += part
        else:
            @pl.when(kk == 0)
            def _():
                acc_ref[...] = jnp.zeros_like(acc_ref)

            for q in range(n_a):
                @pl.when(jnp.logical_and(kk >= starts[q], kk < starts[q] + _piece_chunks(pieces[q], tk)))
                def _(q=q):
                    acc_ref[...] += dot(a_refs[q])

        @pl.when(kk == nk - 1)
        def _():
            finish(acc_ref[...], ex_refs, out_refs, first_rows)

    in_specs = a_specs + [b_spec] + [pl.BlockSpec(bs, im) for _, bs, im in extras]
    in_specs += [pl.BlockSpec(memory_space=pl.ANY)] * len(after)
    sem = ("arbitrary",) * 3 if n_sums else ("parallel", "parallel", "arbitrary")
    return pl.pallas_call(
        kern, name=name, grid=(m // tm, n // tn, nk), in_specs=in_specs,
        out_specs=[pl.BlockSpec((tm, tn), lambda i, j, kk: (i, j)) for _ in outs]
        + [pl.BlockSpec((1, tn), lambda i, j, kk: (0, 0))] * n_sums,
        out_shape=[jax.ShapeDtypeStruct((m, n), dt) for dt in outs]
        + [jax.ShapeDtypeStruct((1, n), F32)] * n_sums,
        scratch_shapes=[] if nk == 1 else [pltpu.VMEM((tm, tn), F32)],
        compiler_params=_cparams(sem),
    )(*[p[0] for p in pieces], b, *[e[0] for e in extras], *after)


def _ici_copies(psum_ref, recv_ref, s_sem, r_sem, axis, shard_shape):
    x, y, c = _me()
    hr, cw = shard_shape[0] // 2, shard_shape[1]
    pick = lambda sems, j: sems[j] if isinstance(sems, (list, tuple)) else sems.at[j]
    copies = []
    for j, (fx, fy) in enumerate(_CHIP_FLIPS):
        chip = 2 * (x ^ fx) + (y ^ fy)
        src = psum_ref.at[:, pl.ds(chip * cw, cw)] if axis == 1 else psum_ref.at[pl.ds(chip * hr, hr), :]
        copies.append(pltpu.make_async_remote_copy(
            src_ref=src, dst_ref=recv_ref.at[j], send_sem=pick(s_sem, j), recv_sem=pick(r_sem, j),
            device_id=(x ^ fx, y ^ fy, c), device_id_type=MESH))
    return copies


_HBM_SPEC = pl.BlockSpec(memory_space=pltpu.HBM)
_SEM_SPEC = pl.BlockSpec(memory_space=pltpu.SEMAPHORE)


def _split_ici_copies(names, p_refs, land_refs, sems):
    copies = []
    for i, n in enumerate(names):
        copies += _ici_copies(p_refs[i], land_refs[i], list(sems[6 * i:6 * i + 3]),
                              list(sems[6 * i + 3:6 * i + 6]), dict(BIG)[n], SHARD[n])
    return copies


def _ici_start(name, names, psums):
    nw, ns = len(names), 6 * len(names)
    lands = [lax.empty((3, SHARD[n][0] // 2, SHARD[n][1]), BF16) for n in names]

    def body(*refs):
        for cp in _split_ici_copies(names, refs[:nw], refs[nw:2 * nw], refs[2 * nw:2 * nw + ns]):
            cp.start()
        token = refs[-1]
        token[...] = jnp.zeros_like(token)

    res = pl.pallas_call(
        body, name=name,
        out_shape=(pltpu.SemaphoreType.DMA(()),) * ns
        + tuple(pltpu.HBM(a.shape, BF16) for a in list(psums) + lands)
        + (jax.ShapeDtypeStruct((8, 128), F32),),
        in_specs=(_HBM_SPEC,) * (2 * nw),
        out_specs=(_SEM_SPEC,) * ns + (_HBM_SPEC,) * (2 * nw) + (pl.BlockSpec(memory_space=pltpu.VMEM),),
        input_output_aliases={k: ns + k for k in range(2 * nw)},
        compiler_params=pltpu.CompilerParams(has_side_effects=pltpu.SideEffectType.DATAFLOW_SIDE_EFFECTING),
    )(*[pltpu.with_memory_space_constraint(a, pltpu.HBM) for a in list(psums) + lands])
    return res[:ns], res[ns:ns + nw], res[ns + nw:ns + 2 * nw], res[-1]


def _ici_wait(name, names, sems, p_thru, land_thru, after):
    nw, ns = len(names), 6 * len(names)

    def body(*refs):
        for cp in _split_ici_copies(names, refs[:nw], refs[nw:2 * nw], refs[2 * nw:2 * nw + ns]):
            cp.wait_send()
            cp.wait_recv()

    res = pl.pallas_call(
        body, name=name,
        out_shape=tuple(pltpu.HBM(a.shape, BF16) for a in list(p_thru) + list(land_thru)),
        in_specs=(_HBM_SPEC,) * (2 * nw) + (_SEM_SPEC,) * ns + (pl.BlockSpec(memory_space=pl.ANY),) * len(after),
        out_specs=(_HBM_SPEC,) * (2 * nw), input_output_aliases={k: k for k in range(2 * nw)},
        compiler_params=pltpu.CompilerParams(has_side_effects=pltpu.SideEffectType.DATAFLOW_SIDE_EFFECTING),
    )(*p_thru, *land_thru, *sems, *after)
    return res[:nw], res[nw:]


def _where_am_i():
    x, y, c = _me()
    return jnp.stack([c, 2 * x + y]).astype(I32)


def _sibling():
    x, y, c = _me()
    return (x, y, 1 - c)


N_SEND_SLOTS = 2


def _matmul_tn_pair(name, pos, a, b, m, n, k, shard_rows, *, tm, tn, tk):
    hr = shard_rows // 2
    tm, tn, tk = min(tm, hr), min(tn, n), min(tk, k)
    tph = hr // tm
    nt, nj, nk = (m // 2) // tm, n // tn, k // tk
    n_tiles = nt * nj

    def row_block(p, t, pos_ref):
        half = jnp.where(p == 0, 1 - pos_ref[0], pos_ref[0])
        return (t // tph) * (2 * tph) + half * tph + t % tph

    pieces = b if isinstance(b, list) else [(b, False)]
    starts = _piece_starts(pieces, tn)
    n_b = len(pieces)

    def kern(pos_ref, a_ref, *rest):
        b_refs = rest[:n_b]
        o_ref, acc_ref, send_buf, land_buf, s_sem, r_sem = rest[n_b:]
        p, t, j, kk = pl.program_id(0), pl.program_id(1), pl.program_id(2), pl.program_id(3)
        idx = t * nj + j
        sib = _sibling()

        def copy(i):
            return pltpu.make_async_remote_copy(
                src_ref=send_buf.at[i % N_SEND_SLOTS], dst_ref=land_buf.at[i], send_sem=s_sem.at[i],
                recv_sem=r_sem.at[i], device_id=sib, device_id_type=MESH)

        @pl.when(kk == 0)
        def _():
            acc_ref[...] = jnp.zeros_like(acc_ref)

        for q in range(n_b):
            @pl.when(jnp.logical_and(j >= starts[q], j < starts[q] + _piece_chunks(pieces[q], tn)))
            def _(q=q):
                acc_ref[...] += lax.dot_general(a_ref[...], b_refs[q][...], _TN, preferred_element_type=F32)

        @pl.when(jnp.logical_and(kk == nk - 1, p == 0))
        def _():
            @pl.when(idx >= N_SEND_SLOTS)
            def _():
                copy(idx - N_SEND_SLOTS).wait_send()

            send_buf[idx % N_SEND_SLOTS] = acc_ref[...].astype(BF16)
            copy(idx).start()

        @pl.when(jnp.logical_and(kk == nk - 1, p == 1))
        def _():
            copy(idx).wait_recv()
            o_ref[...] = (acc_ref[...] + land_buf[idx].astype(F32)).astype(BF16)

        @pl.when(jnp.logical_and(jnp.logical_and(p == 1, idx == n_tiles - 1), kk == nk - 1))
        def _():
            for i in range(max(n_tiles - N_SEND_SLOTS, 0), n_tiles):
                copy(i).wait_send()

    grid_spec = pltpu.PrefetchScalarGridSpec(
        num_scalar_prefetch=1, grid=(2, nt, nj, nk),
        in_specs=[pl.BlockSpec((tk, tm), lambda p, t, j, kk, pos_ref: (kk, row_block(p, t, pos_ref)))]
        + [_piece_spec(pc, tk, tn, st, lambda p, t, j, kk, pos_ref: kk, lambda p, t, j, kk, pos_ref: j)
           for pc, st in zip(pieces, starts)],
        out_specs=pl.BlockSpec((tm, tn), lambda p, t, j, kk, pos_ref: (p * t, p * j)),
        scratch_shapes=[pltpu.VMEM((tm, tn), F32), pltpu.VMEM((N_SEND_SLOTS, tm, tn), BF16),
                        pltpu.VMEM((n_tiles, tm, tn), BF16),
                        pltpu.SemaphoreType.DMA((n_tiles,)), pltpu.SemaphoreType.DMA((n_tiles,))])
    return pl.pallas_call(
        kern, name=name, grid_spec=grid_spec, out_shape=jax.ShapeDtypeStruct((m // 2, n), BF16),
        compiler_params=_cparams(("arbitrary",) * 4),
    )(pos, a, *[pc[0] for pc in pieces])


def _rope_tables():
    half = RET_DK // 2
    f32 = np.float32
    inv = np.power(f32(ROPE_BASE), -np.arange(half, dtype=f32) / f32(half)).astype(f32)
    ang = (np.arange(SEQ, dtype=f32)[:, None] * inv[None, :]).astype(f32)
    return jnp.asarray(np.cos(ang).astype(f32)), jnp.asarray(np.sin(ang).astype(f32))


def _decay_tables():
    c = RET_CHUNK
    f32 = np.float32
    log_g = np.log1p(-np.power(f32(2.0), f32(-5.0) - np.arange(RET_HEADS, dtype=f32))).astype(f32)
    idx = np.arange(c, dtype=f32)
    rel = idx[:, None] - idx[None, :]
    din = np.where(rel >= 0, np.exp(log_g[:, None, None] * np.maximum(rel, f32(0.0))), f32(0.0)).astype(f32)
    qd = np.exp(log_g[:, None] * (idx + f32(1.0))).astype(f32)[:, :, None]
    kd = np.exp(log_g[:, None] * (f32(c) - f32(1.0) - idx)).astype(f32)[:, :, None]
    cd = np.exp(log_g * f32(c)).astype(f32)
    return jnp.asarray(din), jnp.asarray(qd), jnp.asarray(kd), jnp.asarray(cd)


def _t5_bucket(dist):
    max_exact = REL_BUCKETS // 2
    d_f = jnp.maximum(dist, 1).astype(F32)
    large = max_exact + (jnp.log(d_f / max_exact) / math.log(REL_MAX_DIST / max_exact)
                         * (REL_BUCKETS - max_exact)).astype(I32)
    large = jnp.minimum(large, REL_BUCKETS - 1)
    return jnp.where(dist < max_exact, dist, large)


def _bucket_tables():
    qi = jnp.arange(ATT_BLK)[:, None]
    kj = jnp.arange(2 * ATT_BLK)[None, :]
    dist = jnp.clip(ATT_BLK + qi - kj, 0, ATT_BLK)
    return jnp.stack([_t5_bucket(dist * dil) for _, dil in ATT_GROUPS]).astype(I32)


def _retention_fwd(rqk, rv, rg, gn_g, gn_b, din, qd, kd, cd):
    nc = SEQ // RET_CHUNK
    c, dk, dv = RET_CHUNK, RET_DK, RET_DV

    def kern(q_ref, k_ref, v_ref, rg_ref, g_ref, b_ref, din_ref, qd_ref, kd_ref, cd_ref,
             o_ref, st_ref, gated_ref, state):
        n = pl.program_id(0)

        @pl.when(n == 0)
        def _():
            state[...] = jnp.zeros_like(state)

        for sub in range(RET_SUB):
            rows = slice(sub * c, (sub + 1) * c)
            for h in range(RET_HEADS):
                q, k = q_ref[rows, h * dk:(h + 1) * dk], k_ref[rows, h * dk:(h + 1) * dk]
                v = v_ref[rows, h * dv:(h + 1) * dv]
                s_b = state[h].astype(BF16)
                st_ref[h, sub] = s_b
                a = lax.dot_general(q, k, _NT, preferred_element_type=F32) * din_ref[h]
                o = jnp.dot(a.astype(BF16), v, preferred_element_type=F32)
                o += jnp.dot(q, s_b, preferred_element_type=F32) * qd_ref[h]
                v_cols = slice(h * dv, (h + 1) * dv)
                o_ref[rows, v_cols] = o
                nrm, _ = _gn_parts(o)
                gate = rg_ref[rows, v_cols].astype(F32)
                gated_ref[rows, v_cols] = ((gate * _sigmoid(gate))
                                           * (nrm * g_ref[:, v_cols] + b_ref[:, v_cols])).astype(BF16)
                kk = (k.astype(F32) * kd_ref[h]).astype(BF16)
                state[h] = state[h] * cd_ref[h] + lax.dot_general(kk, v, _TN, preferred_element_type=F32)

    whole = lambda a: pl.BlockSpec(a.shape, lambda n: (0,) * a.ndim)
    cs = RET_SUB * c
    rows_v = pl.BlockSpec((cs, RET_V_W), lambda n: (n, 0))
    return pl.pallas_call(
        kern, name="retention_fwd", grid=(nc // RET_SUB,),
        in_specs=[
            pl.BlockSpec((cs, RET_QK_W), lambda n: (n, 0)),
            pl.BlockSpec((cs, RET_QK_W), lambda n: (n, 1)),
            rows_v, rows_v, whole(gn_g), whole(gn_b),
            whole(din), whole(qd), whole(kd),
            pl.BlockSpec(memory_space=pltpu.SMEM),
        ],
        out_specs=[
            rows_v,
            pl.BlockSpec((RET_HEADS, RET_SUB, dk, dv), lambda n: (0, n, 0, 0)),
            rows_v,
        ],
        out_shape=[
            jax.ShapeDtypeStruct((SEQ, RET_V_W), F32),
            jax.ShapeDtypeStruct((RET_HEADS, nc, dk, dv), BF16),
            jax.ShapeDtypeStruct((SEQ, RET_V_W), BF16),
        ],
        scratch_shapes=[pltpu.VMEM((RET_HEADS, dk, dv), F32)],
        compiler_params=_cparams(("arbitrary",)),
    )(rqk, rqk, rv, rg, gn_g, gn_b, din, qd, kd, cd)


def _retention_bwd(rqk, rv, states, d_gated, ro, rg, gn_g, gn_b, din, qd, kd, cd, cos, sin):
    nc = SEQ // RET_CHUNK
    c, dk, dv = RET_CHUNK, RET_DK, RET_DV
    half = dk // 2
    last = nc // RET_SUB - 1

    def unrot(g, cs, sn):
        g1, g2 = g[:, :half], g[:, half:]
        return jnp.concatenate([g1 * cs + g2 * sn, g2 * cs - g1 * sn], axis=-1)

    def kern(q_ref, k_ref, v_ref, st_ref, dg_ref, ro_ref, rg_ref, g_ref, b_ref, din_ref, qd_ref, kd_ref,
             cd_ref, cos_ref, sin_ref, out_ref, drg_ref, dgn_g_ref, dgn_b_ref, dstate):
        step = pl.program_id(0)

        @pl.when(step == 0)
        def _():
            dstate[...] = jnp.zeros_like(dstate)
            dgn_g_ref[...] = jnp.zeros_like(dgn_g_ref)
            dgn_b_ref[...] = jnp.zeros_like(dgn_b_ref)

        for sub in reversed(range(RET_SUB)):
            rows = slice(sub * c, (sub + 1) * c)
            cs, sn = cos_ref[rows, :], sin_ref[rows, :]
            for h in range(RET_HEADS):
                qk_cols, v_cols = slice(h * dk, (h + 1) * dk), slice(h * dv, (h + 1) * dv)
                q, k, v = q_ref[rows, qk_cols], k_ref[rows, qk_cols], v_ref[rows, v_cols]
                s_b = st_ref[h, sub]
                nrm, rstd = _gn_parts(ro_ref[rows, v_cols])
                gate, dg = rg_ref[rows, v_cols].astype(F32), dg_ref[rows, v_cols].astype(F32)
                sg = _sigmoid(gate)
                gn_gain = g_ref[:, v_cols]
                drg_ref[rows, v_cols] = (dg * (nrm * gn_gain + b_ref[:, v_cols])
                                         * (sg * (1.0 + gate * (1.0 - sg)))).astype(BF16)
                d_ron = dg * (gate * sg)
                dgn_g_ref[:, v_cols] += jnp.sum(d_ron * nrm, axis=0, keepdims=True)
                dgn_b_ref[:, v_cols] += jnp.sum(d_ron, axis=0, keepdims=True)
                d_n = d_ron * gn_gain
                d_o = rstd * (d_n - jnp.mean(d_n, axis=-1, keepdims=True)
                              - nrm * jnp.mean(d_n * nrm, axis=-1, keepdims=True))
                d_ob = d_o.astype(BF16)
                d_oq = (d_o * qd_ref[h]).astype(BF16)
                ds_b = dstate[h].astype(BF16)
                din_m = din_ref[h]
                a_b = (lax.dot_general(q, k, _NT, preferred_element_type=F32) * din_m).astype(BF16)
                kk = (k.astype(F32) * kd_ref[h]).astype(BF16)
                d_v = lax.dot_general(a_b, d_ob, _TN, preferred_element_type=F32)
                d_v += jnp.dot(kk, ds_b, preferred_element_type=F32)
                d_a = (lax.dot_general(d_ob, v, _NT, preferred_element_type=F32) * din_m).astype(BF16)
                d_q = jnp.dot(d_a, k, preferred_element_type=F32)
                d_q += lax.dot_general(d_oq, s_b, _NT, preferred_element_type=F32)
                d_k = lax.dot_general(d_a, q, _TN, preferred_element_type=F32)
                d_k += lax.dot_general(v, ds_b, _NT, preferred_element_type=F32) * kd_ref[h]
                dstate[h] = dstate[h] * cd_ref[h] + lax.dot_general(q, d_oq, _TN,
                                                                    preferred_element_type=F32)
                out_ref[rows, h * dk:(h + 1) * dk] = unrot(d_q, cs, sn).astype(BF16)
                out_ref[rows, RET_QK_W + h * dk:RET_QK_W + (h + 1) * dk] = (
                    unrot(d_k, cs, sn) * (RET_DK ** -0.5)).astype(BF16)
                out_ref[rows, 2 * RET_QK_W + h * dv:2 * RET_QK_W + (h + 1) * dv] = d_v.astype(BF16)

    whole = lambda a: pl.BlockSpec(a.shape, lambda n: (0,) * a.ndim)
    rs = RET_SUB * c
    rows_v = pl.BlockSpec((rs, RET_V_W), lambda n: (last - n, 0))
    return pl.pallas_call(
        kern, name="retention_bwd", grid=(nc // RET_SUB,),
        in_specs=[
            pl.BlockSpec((rs, RET_QK_W), lambda n: (last - n, 0)),
            pl.BlockSpec((rs, RET_QK_W), lambda n: (last - n, 1)),
            rows_v,
            pl.BlockSpec((RET_HEADS, RET_SUB, dk, dv), lambda n: (0, last - n, 0, 0)),
            rows_v, rows_v, rows_v, whole(gn_g), whole(gn_b),
            whole(din), whole(qd), whole(kd),
            pl.BlockSpec(memory_space=pltpu.SMEM),
            pl.BlockSpec((rs, half), lambda n: (last - n, 0)),
            pl.BlockSpec((rs, half), lambda n: (last - n, 0)),
        ],
        out_specs=[pl.BlockSpec((rs, 2 * RET_QK_W + RET_V_W), lambda n: (last - n, 0)), rows_v,
                   whole(gn_g), whole(gn_b)],
        out_shape=[jax.ShapeDtypeStruct((SEQ, 2 * RET_QK_W + RET_V_W), BF16),
                   jax.ShapeDtypeStruct((SEQ, RET_V_W), BF16),
                   jax.ShapeDtypeStruct((1, RET_V_W), F32), jax.ShapeDtypeStruct((1, RET_V_W), F32)],
        scratch_shapes=[pltpu.VMEM((RET_HEADS, dk, dv), F32)],
        compiler_params=_cparams(("arbitrary",)),
    )(rqk, rqk, rv, states, d_gated, ro, rg, gn_g, gn_b, din, qd, kd, cd, cos, sin)


def _bias_build(rel_bias, buckets):
    ng = len(ATT_GROUPS)

    def kern(tab_ref, bkt_ref, o_ref):
        g, h = pl.program_id(0), pl.program_id(1)
        bkt = bkt_ref[...]
        acc = jnp.zeros(bkt.shape, F32)
        for b in range(REL_BUCKETS):
            acc = jnp.where(bkt == b, tab_ref[b, g * ATT_HPG + h], acc)
        o_ref[...] = acc

    return pl.pallas_call(
        kern, name="bias_build", grid=(ng, ATT_HPG),
        in_specs=[pl.BlockSpec(memory_space=pltpu.SMEM),
                  pl.BlockSpec((None, ATT_BLK, 2 * ATT_BLK), lambda g, h: (g, 0, 0))],
        out_specs=pl.BlockSpec((None, None, ATT_BLK, 2 * ATT_BLK), lambda g, h: (g, h, 0, 0)),
        out_shape=jax.ShapeDtypeStruct((ng, ATT_HPG, ATT_BLK, 2 * ATT_BLK), F32),
        compiler_params=_cparams(("arbitrary", "arbitrary")),
    )(rel_bias, buckets)


def _bias_grad(dsb, buckets):
    ng = len(ATT_GROUPS)

    def kern(ds_ref, bkt_ref, o_ref):
        g, h = pl.program_id(0), pl.program_id(1)
        bkt, ds = bkt_ref[...], ds_ref[...]
        for b in range(REL_BUCKETS):
            o_ref[b, g * ATT_HPG + h] = jnp.sum(jnp.where(bkt == b, ds, 0.0))

    return pl.pallas_call(
        kern, name="bias_grad", grid=(ng, ATT_HPG),
        in_specs=[pl.BlockSpec((None, None, ATT_BLK, 2 * ATT_BLK), lambda g, h: (g, h, 0, 0)),
                  pl.BlockSpec((None, ATT_BLK, 2 * ATT_BLK), lambda g, h: (g, 0, 0))],
        out_specs=pl.BlockSpec(memory_space=pltpu.SMEM),
        out_shape=jax.ShapeDtypeStruct((REL_BUCKETS, N_ATT_HEADS), F32),
        compiler_params=_cparams(("arbitrary", "arbitrary")),
    )(dsb, buckets)


_NT = (((1,), (1,)), ((), ()))
_TN = (((0,), (0,)), ((), ()))
_ATT_SCALE = ATT_DH ** -0.5


def _window_mask(has_prev):
    qi = lax.broadcasted_iota(I32, (ATT_BLK, 2 * ATT_BLK), 0)
    kj = lax.broadcasted_iota(I32, (ATT_BLK, 2 * ATT_BLK), 1)
    prev_ok = jnp.logical_and(jnp.logical_and(kj < ATT_BLK, kj >= qi), has_prev)
    return jnp.logical_or(prev_ok, jnp.logical_and(kj >= ATT_BLK, qi >= kj - ATT_BLK))


def _head_specs(col0):
    return pl.BlockSpec((SEQ, ATT_DH), lambda h: (0, col0 + h))


def _sub_rows(start, size, dil):
    return pl.ds(start, size) if dil == 1 else pl.ds(start, size, stride=dil)


def _att_blocks(dil):
    nb = SEQ // dil // ATT_BLK
    return [(r + dil * n * ATT_BLK, n > 0, n + 1 < nb) for r in range(dil) for n in range(nb)]


def _att_fwd(gi, dil, qkv, bias, after=()):
    blk, dh = ATT_BLK, ATT_DH
    pad = dil * blk
    col0 = 3 * ATT_HPG * gi

    def kern(q_ref, k_ref, v_ref, b_ref, *rest):
        o_ref, l_ref, qf, kpad, vpad = rest[len(after):]
        zero = jnp.zeros((pad, dh), F32)
        kpad[0:pad, :] = zero
        vpad[0:pad, :] = zero
        kpad[pad:, :] = k_ref[...].astype(F32)
        vpad[pad:, :] = v_ref[...].astype(F32)
        qf[...] = q_ref[...].astype(F32)
        bias_m = b_ref[...]
        for start, has_prev, _ in _att_blocks(dil):
            rows, window = _sub_rows(start, blk, dil), _sub_rows(start, 2 * blk, dil)
            q = qf[rows, :].astype(BF16)
            kw, vw = kpad[window, :].astype(BF16), vpad[window, :].astype(BF16)
            valid = _window_mask(has_prev)
            s = lax.dot_general(q, kw, _NT, preferred_element_type=F32) * _ATT_SCALE + bias_m
            s = jnp.where(valid, s, -1e30)
            mx = jnp.max(s, axis=-1, keepdims=True)
            e = jnp.exp(s - mx)
            den = jnp.sum(e, axis=-1, keepdims=True)
            o_ref[rows, :] = jnp.dot((e / den).astype(BF16), vw, preferred_element_type=F32)
            l_ref[rows, :] = jnp.broadcast_to(mx + jnp.log(den), (blk, dh))

    return pl.pallas_call(
        kern, name=f"att_fwd_g{gi}", grid=(ATT_HPG,),
        in_specs=[_head_specs(col0), _head_specs(col0 + ATT_HPG), _head_specs(col0 + 2 * ATT_HPG),
                  pl.BlockSpec((None, None, blk, 2 * blk), lambda h: (gi, h, 0, 0))]
        + [pl.BlockSpec(memory_space=pl.ANY)] * len(after),
        out_specs=[_head_specs(0), _head_specs(0)],
        out_shape=[jax.ShapeDtypeStruct((SEQ, ATT_W), F32), jax.ShapeDtypeStruct((SEQ, ATT_W), F32)],
        scratch_shapes=[pltpu.VMEM((SEQ, dh), F32), pltpu.VMEM((SEQ + pad, dh), F32),
                        pltpu.VMEM((SEQ + pad, dh), F32)],
        compiler_params=_cparams(("arbitrary",)),
    )(qkv, qkv, qkv, bias, *after)


def _att_bwd(gi, dil, qkv, d_att, lse, dd, bias):
    blk, dh = ATT_BLK, ATT_DH
    pad = dil * blk
    col0 = 3 * ATT_HPG * gi

    def kern(q_ref, k_ref, v_ref, do_ref, l_ref, d_ref, b_ref, dqkv_ref, dsb_ref,
             qf, kpad, vpad, dq_s, dkpad, dvpad):
        zero = jnp.zeros((pad, dh), F32)
        kpad[0:pad, :] = zero
        vpad[0:pad, :] = zero
        kpad[pad:, :] = k_ref[...].astype(F32)
        vpad[pad:, :] = v_ref[...].astype(F32)
        qf[...] = q_ref[...].astype(F32)
        dkpad[...] = jnp.zeros_like(dkpad)
        dvpad[...] = jnp.zeros_like(dvpad)
        bias_m = b_ref[...]
        ds_sum = jnp.zeros((blk, 2 * blk), F32)

        for start, has_prev, _ in _att_blocks(dil):
            rows, window = _sub_rows(start, blk, dil), _sub_rows(start, 2 * blk, dil)
            q, d_o = qf[rows, :].astype(BF16), do_ref[rows, :].astype(BF16)
            kw, vw = kpad[window, :].astype(BF16), vpad[window, :].astype(BF16)
            lrow, drow = l_ref[rows, :][:, :1], d_ref[rows, :][:, :1]
            valid = _window_mask(has_prev)
            s = lax.dot_general(q, kw, _NT, preferred_element_type=F32) * _ATT_SCALE + bias_m
            p = jnp.where(valid, jnp.exp(jnp.where(valid, s, -1e30) - lrow), 0.0)
            dp = lax.dot_general(d_o, vw, _NT, preferred_element_type=F32)
            ds = p * (dp - drow)
            ds_b = ds.astype(BF16)
            dq_s[rows, :] = jnp.dot(ds_b, kw, preferred_element_type=F32) * _ATT_SCALE
            dkpad[window, :] += lax.dot_general(ds_b, q, _TN, preferred_element_type=F32) * _ATT_SCALE
            dvpad[window, :] += lax.dot_general(p.astype(BF16), d_o, _TN, preferred_element_type=F32)
            ds_sum = ds_sum + ds
        dsb_ref[...] = ds_sum

        dqkv_ref[0] = dq_s[...].astype(BF16)
        dqkv_ref[1] = dkpad[pad:, :].astype(BF16)
        dqkv_ref[2] = dvpad[pad:, :].astype(BF16)

    return pl.pallas_call(
        kern, name=f"att_bwd_g{gi}", grid=(ATT_HPG,),
        in_specs=[_head_specs(col0), _head_specs(col0 + ATT_HPG), _head_specs(col0 + 2 * ATT_HPG),
                  _head_specs(0), _head_specs(0), _head_specs(0),
                  pl.BlockSpec((None, None, blk, 2 * blk), lambda h: (gi, h, 0, 0))],
        out_specs=[pl.BlockSpec((3, SEQ, dh), lambda h: (0, 0, h)),
                   pl.BlockSpec((None, blk, 2 * blk), lambda h: (h, 0, 0))],
        out_shape=[jax.ShapeDtypeStruct((3, SEQ, ATT_W), BF16),
                   jax.ShapeDtypeStruct((ATT_HPG, blk, 2 * blk), F32)],
        scratch_shapes=[pltpu.VMEM((SEQ, dh), F32), pltpu.VMEM((SEQ + pad, dh), F32),
                        pltpu.VMEM((SEQ + pad, dh), F32), pltpu.VMEM((SEQ, dh), F32),
                        pltpu.VMEM((SEQ + pad, dh), F32), pltpu.VMEM((SEQ + pad, dh), F32)],
        compiler_params=_cparams(("arbitrary",)),
    )(qkv, qkv, qkv, d_att, lse, dd, bias)


def _rms_parts(x):
    r = lax.rsqrt(jnp.mean(x * x, axis=-1, keepdims=True) + RMS_EPS)
    return x * r, r


def _rms_bwd(d_xhat, xhat, r):
    return r * (d_xhat - xhat * jnp.mean(d_xhat * xhat, axis=-1, keepdims=True))


def _prenorm_fwd(name, x, gain, shift, scale, tm=256):
    def kern(x_ref, g_ref, sh_ref, sc_ref, o_ref):
        xhat, _ = _rms_parts(x_ref[...])
        o_ref[...] = ((xhat * g_ref[...]) * (1.0 + sc_ref[...]) + sh_ref[...]).astype(BF16)

    rows = pl.BlockSpec((tm, D_MODEL), lambda i: (i, 0))
    vec = pl.BlockSpec((1, D_MODEL), lambda i: (0, 0))
    return pl.pallas_call(
        kern, name=name, grid=(x.shape[0] // tm,), in_specs=[rows, vec, vec, vec], out_specs=rows,
        out_shape=jax.ShapeDtypeStruct(x.shape, BF16), compiler_params=_cparams(("parallel",)),
    )(x, gain, shift, scale)


def _prenorm_bwd_epi(d_h, x, resid, gain, scale, branch=None, gate=None):
    xhat, r = _rms_parts(x)
    nrm = xhat * gain
    d_n = d_h * (1.0 + scale)
    dx = _rms_bwd(d_n * gain, xhat, r) + resid
    sums = (jnp.sum(d_h, axis=0, keepdims=True), jnp.sum(d_h * nrm, axis=0, keepdims=True),
            jnp.sum(d_n * xhat, axis=0, keepdims=True))
    if branch is None:
        return (dx,) + sums
    return (dx, dx * gate) + sums + (jnp.sum(dx * branch, axis=0, keepdims=True),)


def _row_operands(tm, rows, vecs):
    return ([(a, (tm, D_MODEL), lambda i, j, kk: (i, 0)) for a in rows]
            + [(v, (1, D_MODEL), lambda i, j, kk: (0, 0)) for v in vecs])


def _gn_parts(ro):
    mu = jnp.mean(ro, axis=-1, keepdims=True)
    cen = ro - mu
    rstd = lax.rsqrt(jnp.mean(cen * cen, axis=-1, keepdims=True) + GN_EPS)
    return cen * rstd, rstd


MERGE_TM = 512


def _att_out(os_, ls_, w_att_out, gates, ret_out):
    tm = MERGE_TM

    def kern(o0, o1, o2, l0, l1, l2, w_ref, ga_ref, gb_ref, ro_ref, att_ref, attb_ref, lse_ref,
             ao_ref, mg_ref):
        l0v, l1v, l2v = l0[...], l1[...], l2[...]
        mx = jnp.maximum(jnp.maximum(l0v, l1v), l2v)
        e0, e1, e2 = jnp.exp(l0v - mx), jnp.exp(l1v - mx), jnp.exp(l2v - mx)
        den = e0 + e1 + e2
        att = (e0 / den) * o0[...] + (e1 / den) * o1[...] + (e2 / den) * o2[...]
        att_b = att.astype(BF16)
        att_ref[...] = att
        attb_ref[...] = att_b
        lse_ref[...] = mx + jnp.log(den)
        att_out = jnp.dot(att_b, w_ref[...], preferred_element_type=F32)
        ao_ref[...], merged = _merge_fwd_epi(att_out, ga_ref[...], gb_ref[...], ro_ref[...])
        mg_ref[...] = merged.astype(BF16)

    rows_w = pl.BlockSpec((tm, ATT_W), lambda i: (i, 0))
    rows_d = pl.BlockSpec((tm, D_MODEL), lambda i: (i, 0))
    return pl.pallas_call(
        kern, name="att_out", grid=(SEQ // tm,),
        in_specs=[rows_w] * 6 + [pl.BlockSpec((ATT_W, D_MODEL), lambda i: (0, 0)), rows_d,
                                 pl.BlockSpec((tm, D_MODEL), lambda i: (i, 1)), rows_d],
        out_specs=[rows_w, rows_w, rows_w, rows_d, rows_d],
        out_shape=[jax.ShapeDtypeStruct((SEQ, ATT_W), F32), jax.ShapeDtypeStruct((SEQ, ATT_W), BF16),
                   jax.ShapeDtypeStruct((SEQ, ATT_W), F32), jax.ShapeDtypeStruct((SEQ, D_MODEL), F32),
                   jax.ShapeDtypeStruct((SEQ, D_MODEL), BF16)],
        compiler_params=_cparams(("parallel",)),
    )(*os_, *ls_, w_att_out, gates, gates, ret_out)


def _merge_operands(gates, ret_out, att_out=None):
    ops = [(gates, (MERGE_TM, D_MODEL), lambda i, j, kk: (i, 0)),
           (gates, (MERGE_TM, D_MODEL), lambda i, j, kk: (i, 1)),
           (ret_out, (MERGE_TM, D_MODEL), lambda i, j, kk: (i, 0))]
    if att_out is not None:
        ops.append((att_out, (MERGE_TM, D_MODEL), lambda i, j, kk: (i, 0)))
    return ops


def _merge_fwd_epi(att_out, ga, gb, ret_out):
    return att_out, _sigmoid(ga.astype(F32)) * ret_out + _sigmoid(gb.astype(F32)) * att_out


def _merge_bwd_epi(d_merged, ga, gb, ret_out, att_out):
    sa, sb = _sigmoid(ga.astype(F32)), _sigmoid(gb.astype(F32))
    return (d_merged * sa, d_merged * sb, d_merged * ret_out * (sa * (1.0 - sa)),
            d_merged * att_out * (sb * (1.0 - sb)))


def _att_out_bwd_epi(d_att, att):
    outs = []
    for h in range(ATT_HPG):
        sl = slice(h * ATT_DH, (h + 1) * ATT_DH)
        outs.append(jnp.broadcast_to(jnp.sum(d_att[:, sl] * att[:, sl], axis=-1, keepdims=True),
                                     (d_att.shape[0], ATT_DH)))
    return d_att, jnp.concatenate(outs, axis=-1)


def _loss_head_epi(branch, x_prev, target, gate, gain):
    x3 = x_prev + gate * branch
    xhat, r = _rms_parts(x3)
    err = xhat * gain - target
    d_y = err / D_MODEL
    loss = 0.5 * jnp.sum(jnp.mean(err * err, axis=-1, keepdims=True), axis=0, keepdims=True)
    d_x = _rms_bwd(d_y * gain, xhat, r)
    return (d_x, d_x * gate, jnp.broadcast_to(loss, (1, D_MODEL)),
            jnp.sum(d_y * xhat, axis=0, keepdims=True), jnp.sum(d_x * branch, axis=0, keepdims=True))


def _local_step(pos, x, target, mod, norm1_g, norm2_g, norm_f_g, rel_bias, gn_g, gn_b, w_in, rest_gather):
    sh1, sc1, g1, sh2, sc2, g2 = [mod[:, i * D_MODEL:(i + 1) * D_MODEL] for i in range(6)]
    cos, sin = _rope_tables()
    din, qd, kd, cd = _decay_tables()
    buckets = _bucket_tables()
    bias = _bias_build(rel_bias, buckets)
    dils = [d for _, d in ATT_GROUPS]

    h1 = _prenorm_fwd("prenorm1_fwd", x, norm1_g, sh1, sc1)

    qk_tn = 2 * RET_DK

    def rot_epi(acc, cs, sn, scale):
        half = RET_DK // 2
        outs = []
        for h0 in range(0, qk_tn, RET_DK):
            x1, x2 = acc[:, h0:h0 + half], acc[:, h0 + half:h0 + RET_DK]
            outs += [x1 * cs - x2 * sn, x1 * sn + x2 * cs]
        return (jnp.concatenate(outs, axis=-1) * scale,)

    qk_scale = jnp.concatenate([jnp.ones((1, RET_QK_W), F32),
                                jnp.full((1, RET_QK_W), RET_DK ** -0.5, F32)], axis=-1)
    rope_ex = [(cos, (TM, RET_DK // 2), lambda i, j, kk: (i, 0)),
               (sin, (TM, RET_DK // 2), lambda i, j, kk: (i, 0)),
               (qk_scale, (1, qk_tn), lambda i, j, kk: (0, j))]
    rest_sems, rest_shards, rest_fulls, rest_token = rest_gather
    behind = [rest_token]
    rv = _matmul("proj_rv", h1, w_in, "nn", SEQ, RET_V_W, D_MODEL, [BF16], b_off=OFF_V, tk=D_MODEL,
                 after=behind)[0]
    rg = _matmul("proj_rg", h1, w_in, "nn", SEQ, RET_V_W, D_MODEL, [BF16], b_off=OFF_G, tk=D_MODEL,
                 after=behind)[0]
    gates = _matmul("proj_gates", h1, w_in, "nn", SEQ, 2 * D_MODEL, D_MODEL, [BF16], b_off=OFF_GATE,
                    tn=512, tk=D_MODEL, after=behind)[0]
    aqkv = _matmul("proj_att", h1, w_in, "nn", SEQ, 9 * ATT_W, D_MODEL, [BF16], b_off=OFF_ATT,
                   tn=512, tk=D_MODEL, after=behind)[0]

    rqk = _matmul("proj_qk", h1, w_in, "nn", SEQ, 2 * RET_QK_W, D_MODEL, [BF16], b_off=OFF_Q,
                  tn=qk_tn, tk=D_MODEL, epilogue=rot_epi, extras=rope_ex, after=behind)[0]
    ro, states, gated = _retention_fwd(rqk, rv, rg, gn_g, gn_b, din, qd, kd, cd)
    os_, ls_ = [], []
    for gi in range(3):
        if gi == 2:
            rest_sems, rest_fulls, fwd_token = _gather_rest_forward(
                rest_sems, rest_shards, rest_fulls, [gated, gates] + os_)
        o_g, l_g = _att_fwd(gi, dils[gi], aqkv, bias, after=[fwd_token] if gi == 2 else ())
        os_.append(o_g)
        ls_.append(l_g)
    w_ret_out, w_att_out, w_o, w_ff1, w_ff2 = _gather_rest_end(rest_sems, rest_fulls, [os_[2]])
    ret_out = _matmul("ret_out", gated, w_ret_out, "nn", SEQ, D_MODEL, RET_V_W, [F32], tk=RET_V_W)[0]
    att, att_b, lse, att_out, merged = _att_out(os_, ls_, w_att_out, gates, ret_out)

    def mix_epi(acc, xt, g, gain, sh, sc):
        x_new = xt + g * acc
        xhat, _ = _rms_parts(x_new)
        return x_new, acc, (xhat * gain) * (1.0 + sc) + sh

    x2, mix, h2 = _matmul("mix_out", merged, w_o, "nn", SEQ, D_MODEL, D_MODEL, [F32, BF16, BF16],
                          epilogue=mix_epi, extras=_row_operands(TM, [x], [g1, norm2_g, sh2, sc2]))

    def relu2_epi(acc):
        r = jnp.maximum(acc, 0.0)
        return r * r, r

    act, relu_u = _matmul("ff1", h2, w_ff1, "nn", SEQ, D_FF, D_MODEL, [BF16, BF16], tk=D_MODEL,
                          epilogue=relu2_epi)
    d_x3, d_y2, loss, d_gf, d_g2 = _matmul(
        "ff2", act, w_ff2, "nn", SEQ, D_MODEL, D_FF, [F32, BF16], tm=TM, tk=1024, n_sums=3,
        epilogue=_loss_head_epi, extras=_row_operands(TM, [x2, target], [g2, norm_f_g]))

    def relu2_bwd_epi(acc, rt):
        return (acc * (2.0 * rt.astype(F32)),)

    gw_ff2 = _matmul_tn_pair("ff2_dw", pos, act, d_y2, D_FF, D_MODEL, SEQ, D_FF // N_CHIPS,
                             tm=512, tn=1024, tk=SEQ)
    d_u = _matmul("ff2_dx", d_y2, w_ff2, "nt", SEQ, D_FF, D_MODEL, [BF16], epilogue=relu2_bwd_epi,
                  extras=[(relu_u, (TM, TN), lambda i, j, kk: (i, j))])[0]
    gw_ff1 = _matmul_tn_pair("ff1_dw", pos, h2, d_u, D_MODEL, D_FF, SEQ, D_MODEL,
                             tm=512, tn=1024, tk=SEQ)
    ffn = ["w_ff2", "w_ff1"]
    ffn_started = _ici_start("ici_start_ffn", ffn, [gw_ff2, gw_ff1])
    d_x2, d_mix, d_sh2, d_sc2, d_n2g, d_g1 = _matmul(
        "ff1_dx", d_u, w_ff1, "nt", SEQ, D_MODEL, D_FF, [F32, BF16], tm=TM, tk=1024, n_sums=4,
        epilogue=_prenorm_bwd_epi, extras=_row_operands(TM, [x2, d_x3], [norm2_g, sc2])
        + _row_operands(TM, [mix], [g1]), after=[ffn_started[3]])
    gw_o = _matmul_tn_pair("mix_dw", pos, merged, d_mix, D_MODEL, D_MODEL, SEQ, D_MODEL // N_CHIPS,
                           tm=128, tn=1024, tk=2048)
    d_ret_out, d_att_out, d_ga, d_gb = _matmul(
        "mix_dx", d_mix, w_o, "nt", SEQ, D_MODEL, D_MODEL, [BF16] * 4, tm=MERGE_TM,
        epilogue=_merge_bwd_epi, extras=_merge_operands(gates, ret_out, att_out))

    gw_ret_out = _matmul_tn_pair("ret_out_dw", pos, gated, d_ret_out, RET_V_W, D_MODEL, SEQ,
                                 RET_V_W // N_CHIPS, tm=256, tn=1024, tk=SEQ)
    gw_att_out = _matmul_tn_pair("att_out_dw", pos, att_b, d_att_out, ATT_W, D_MODEL, SEQ, ATT_W,
                                 tm=256, tn=1024, tk=2048)
    mixer = ["w_o", "w_ret_out", "w_att_out"]
    mixer_started = _ici_start("ici_start_mixer", mixer, [gw_o, gw_ret_out, gw_att_out])
    d_gated = _matmul("ret_out_dx", d_ret_out, w_ret_out, "nt", SEQ, RET_V_W, D_MODEL, [BF16],
                      after=[mixer_started[3]])[0]
    d_att, dd = _matmul("att_out_dx", d_att_out, w_att_out, "nt", SEQ, ATT_W, D_MODEL, [F32, F32],
                        epilogue=_att_out_bwd_epi,
                        extras=[(att, (TM, ATT_W), lambda i, j, kk: (i, 0))], after=[mixer_started[3]])

    d_rqkv, d_rg, d_gn_g, d_gn_b = _retention_bwd(rqk, rv, states, d_gated, ro, rg, gn_g, gn_b,
                                                  din, qd, kd, cd, cos, sin)

    d_aqkv, dsbs = [], []
    for gi in range(3):
        dqkv, dsb = _att_bwd(gi, dils[gi], aqkv, d_att, lse, dd, bias)
        d_aqkv.append(dqkv)
        dsbs.append(dsb)
    d_rel_bias = _bias_grad(jnp.stack(dsbs), buckets)

    d_proj = ([(d_rqkv, False), (d_rg, False)] + [(t, True) for t in d_aqkv]
              + [(d_ga, False), (d_gb, False)])
    gw_in = _matmul_tn_pair("proj_dw", pos, h1, d_proj, D_MODEL, IN_COLS, SEQ, D_MODEL,
                            tm=512, tn=ATT_W, tk=SEQ)
    sems, (gw_in,), (land,), token = _ici_start("ici_start_w_in", ["w_in"], [gw_in])
    grad_x, d_sh1, d_sc1, d_n1g = _matmul(
        "proj_dx", d_proj, w_in, "nt", SEQ, D_MODEL, IN_COLS, [F32], tn=1024, tk=ATT_W, n_sums=3,
        epilogue=_prenorm_bwd_epi, extras=_row_operands(TM, [x, d_x2], [norm1_g, sc1]), after=[token])
    pending = (sems, land)

    names = ffn + mixer
    psums, got = _ici_wait("ici_wait_rest", names, list(ffn_started[0]) + list(mixer_started[0]),
                           list(ffn_started[1]) + list(mixer_started[1]),
                           list(ffn_started[2]) + list(mixer_started[2]), [grad_x])
    g_big = {n: _final_sum("final_" + n, pos, dict(BIG)[n], psums[i], got[i], SHARD[n])
             for i, n in enumerate(names)}
    d_mod = jnp.concatenate([d_sh1, d_sc1, d_g1, d_sh2, d_sc2, d_g2], axis=-1)
    small = dict(norm1_g=d_n1g, norm2_g=d_n2g, norm_f_g=d_gf, gn_g=d_gn_g, gn_b=d_gn_b,
                 rel_bias=d_rel_bias)
    return loss, grad_x, d_mod, small, g_big, (gw_in,) + pending


def _me():
    return lax.axis_index("x"), lax.axis_index("y"), lax.axis_index("c")


def _peer(x, y, c, mask):
    return (x ^ ((mask >> 2) & 1), y ^ ((mask >> 1) & 1), c ^ (mask & 1))


def _gather8(src_ref, dst_ref, send_sems, recv_sems):
    x, y, c = _me()
    me = 4 * x + 2 * y + c
    copies = []
    for mask in range(1, N_DEV):
        cp = pltpu.make_async_remote_copy(
            src_ref=src_ref, dst_ref=dst_ref.at[me], send_sem=send_sems.at[mask - 1],
            recv_sem=recv_sems.at[mask - 1], device_id=_peer(x, y, c, mask), device_id_type=MESH)
        cp.start()
        copies.append(cp)
    dst_ref[me] = src_ref[...]
    for cp in copies:
        cp.wait_recv()
    for cp in copies:
        cp.wait_send()


def _ada_fwd(c_in, w_ada, b_ada):
    ncol = ADA_COLS // N_CHIPS

    def body(c_ref, w_ref, b_ref, mod_ref, sc_ref, cbuf, cg, mbuf, mg, s1, r1, s2, r2):
        x, y, c = _me()
        me = 4 * x + 2 * y + c
        cv = c_ref[...]
        cbuf[...] = jnp.broadcast_to(cv * _sigmoid(cv), cbuf.shape)
        _gather8(cbuf, cg, s1, r1)
        rows = lax.broadcasted_iota(I32, (N_DEV, D_MODEL), 0)
        sc_all = jnp.zeros((N_DEV, D_MODEL), F32)
        for d in range(N_DEV):
            sc_all = jnp.where(rows == d, cg[d], sc_all)
        sc_ref[...] = sc_all
        mbuf[...] = jnp.dot(sc_all.astype(BF16), w_ref[...].astype(BF16), preferred_element_type=F32)
        _gather8(mbuf, mg, s2, r2)
        rowsel = lax.broadcasted_iota(I32, (N_DEV, ncol), 0) == me
        for k in range(N_CHIPS):
            blk = mg[2 * k]
            row = jnp.sum(jnp.where(rowsel, blk, 0.0), axis=0, keepdims=True)
            mod_ref[:, k * ncol:(k + 1) * ncol] = row + b_ref[:, k * ncol:(k + 1) * ncol]

    vm = pl.BlockSpec(memory_space=pltpu.VMEM)
    return pl.pallas_call(
        body, name="ada_fwd",
        in_specs=[vm, vm, vm], out_specs=[vm, vm],
        out_shape=[jax.ShapeDtypeStruct((1, ADA_COLS), F32), jax.ShapeDtypeStruct((N_DEV, D_MODEL), F32)],
        scratch_shapes=[
            pltpu.VMEM((8, D_MODEL), F32), pltpu.VMEM((N_DEV, 8, D_MODEL), F32),
            pltpu.VMEM((8, ncol), F32), pltpu.VMEM((N_DEV, 8, ncol), F32),
            pltpu.SemaphoreType.DMA((N_DEV - 1,)), pltpu.SemaphoreType.DMA((N_DEV - 1,)),
            pltpu.SemaphoreType.DMA((N_DEV - 1,)), pltpu.SemaphoreType.DMA((N_DEV - 1,)),
        ],
        compiler_params=pltpu.CompilerParams(vmem_limit_bytes=VMEM_LIMIT_V7X),
    )(c_in, w_ada, b_ada)


def _small_reduce(pack, sc_all, after=()):
    ncol = ADA_COLS // N_CHIPS

    def body(p_ref, sc_ref, *rest):
        tot_ref, gw_ref, pg, s1, r1 = rest[len(after):]
        x, y, _ = _me()
        chip = 2 * x + y
        _gather8(p_ref, pg, s1, r1)
        tot = pg[0]
        for d in range(1, N_DEV):
            tot = tot + pg[d]
        tot_ref[...] = tot
        rows = lax.broadcasted_iota(I32, (N_DEV, ncol), 0)
        dmod = jnp.zeros((N_DEV, ncol), F32)
        for k in range(N_CHIPS):
            part = jnp.zeros((N_DEV, ncol), F32)
            for d in range(N_DEV):
                part = jnp.where(rows == d, pg[d, :, k * ncol:(k + 1) * ncol][0:1, :], part)
            dmod = jnp.where(chip == k, part, dmod)
        gw_ref[...] = lax.dot_general(sc_ref[...].astype(BF16), dmod.astype(BF16), _TN,
                                      preferred_element_type=F32)

    vm = pl.BlockSpec(memory_space=pltpu.VMEM)
    return pl.pallas_call(
        body, name="small_reduce",
        in_specs=[vm, vm] + [pl.BlockSpec(memory_space=pl.ANY)] * len(after), out_specs=[vm, vm],
        out_shape=[jax.ShapeDtypeStruct((8, ADA_COLS), F32), jax.ShapeDtypeStruct((D_MODEL, ncol), F32)],
        scratch_shapes=[pltpu.VMEM((N_DEV, 8, ADA_COLS), F32),
                        pltpu.SemaphoreType.DMA((N_DEV - 1,)), pltpu.SemaphoreType.DMA((N_DEV - 1,))],
        compiler_params=pltpu.CompilerParams(vmem_limit_bytes=VMEM_LIMIT_V7X),
    )(pack, sc_all, *after)


BIG = (("w_in", 1), ("w_ret_out", 0), ("w_att_out", 1), ("w_o", 0), ("w_ff1", 1), ("w_ff2", 0))
SHARD = {"w_in": (D_MODEL, IN_COLS // N_CHIPS), "w_ret_out": (RET_V_W // N_CHIPS, D_MODEL),
         "w_att_out": (ATT_W, D_MODEL // N_CHIPS), "w_o": (D_MODEL // N_CHIPS, D_MODEL),
         "w_ff1": (D_MODEL, D_FF // N_CHIPS), "w_ff2": (D_FF // N_CHIPS, D_MODEL)}
_CHIP_FLIPS = ((1, 0), (0, 1), (1, 1))


def _region(ref, axis, chip, half, shard_shape):
    r, cw = shard_shape
    hr = r // 2
    if axis == 1:
        return ref.at[pl.ds(half * hr, hr), pl.ds(chip * cw, cw)]
    return ref.at[pl.ds(chip * r + half * hr, hr), :]


def _gather_weights(shards, n_remote):
    nw = len(BIG)
    shapes = [s.shape for s in shards]
    full_shapes = [(r, N_CHIPS * cw) if ax == 1 else (N_CHIPS * r, cw)
                   for (r, cw), (_, ax) in zip(shapes, BIG)]

    def body(*refs):
        ins, outs = refs[:nw], refs[nw:2 * nw]
        own = refs[2 * nw:3 * nw]
        from_ici, from_sib = refs[3 * nw:3 * nw + n_remote], refs[3 * nw + n_remote:3 * nw + 2 * n_remote]
        ld_sem, st_sem, s_ici, r_ici, s_d2d, r_d2d, st_a, st_b = refs[3 * nw + 2 * n_remote:]
        x, y, c = _me()
        chip = 2 * x + y
        sib = (x, y, 1 - c)
        loads = [pltpu.make_async_copy(ins[i], own[i], ld_sem.at[i]) for i in range(nw)]
        for cp in loads:
            cp.start()
        pending, first = [], []
        for i, (_, ax) in enumerate(BIG):
            r, cw = shapes[i]
            hr = r // 2
            loads[i].wait()
            dst = outs[i].at[:, pl.ds(chip * cw, cw)] if ax == 1 else outs[i].at[pl.ds(chip * r, r), :]
            cp = pltpu.make_async_copy(own[i], dst, st_sem.at[i])
            cp.start()
            pending.append(cp)
            for j, (fx, fy) in enumerate(_CHIP_FLIPS if i < n_remote else ()):
                rc = pltpu.make_async_remote_copy(
                    src_ref=own[i].at[pl.ds(c * hr, hr), :], dst_ref=from_ici[i].at[j],
                    send_sem=s_ici.at[j * nw + i], recv_sem=r_ici.at[j * nw + i],
                    device_id=(x ^ fx, y ^ fy, c), device_id_type=MESH)
                rc.start()
                first.append((j, i, rc))
        passed = []
        for j, i, rc in first:
            fx, fy = _CHIP_FLIPS[j]
            src_chip = 2 * (x ^ fx) + (y ^ fy)
            ax = BIG[i][1]
            rc.wait_recv()
            fw = pltpu.make_async_remote_copy(
                src_ref=from_ici[i].at[j], dst_ref=from_sib[i].at[j], send_sem=s_d2d.at[j * nw + i],
                recv_sem=r_d2d.at[j * nw + i], device_id=sib, device_id_type=MESH)
            fw.start()
            passed.append((j, i, src_chip, fw))
            st = pltpu.make_async_copy(from_ici[i].at[j], _region(outs[i], ax, src_chip, c, shapes[i]),
                                       st_a.at[j * nw + i])
            st.start()
            pending.append(st)
        for j, i, src_chip, fw in passed:
            fw.wait_recv()
            st = pltpu.make_async_copy(from_sib[i].at[j],
                                       _region(outs[i], BIG[i][1], src_chip, 1 - c, shapes[i]),
                                       st_b.at[j * nw + i])
            st.start()
            pending.append(st)
        for _, _, rc in first:
            rc.wait_send()
        for _, _, _, fw in passed:
            fw.wait_send()
        for cp in pending:
            cp.wait()

    hbm = pl.BlockSpec(memory_space=pl.ANY)
    halves = [pltpu.VMEM((3, r // 2, cw), BF16) for r, cw in shapes[:n_remote]]
    return pl.pallas_call(
        body, name="gather_weights",
        in_specs=[hbm] * nw, out_specs=[hbm] * nw,
        out_shape=[jax.ShapeDtypeStruct(fs, BF16) for fs in full_shapes],
        scratch_shapes=[pltpu.VMEM(sh, BF16) for sh in shapes] + halves + halves
        + [pltpu.SemaphoreType.DMA((nw,)), pltpu.SemaphoreType.DMA((nw,))]
        + [pltpu.SemaphoreType.DMA((3 * nw,))] * 6,
        compiler_params=pltpu.CompilerParams(vmem_limit_bytes=VMEM_LIMIT_V7X),
    )(*shards)


REST = BIG[1:]
_SIDE_EFFECTS = pltpu.CompilerParams(has_side_effects=pltpu.SideEffectType.DATAFLOW_SIDE_EFFECTING)
_ANY_SPEC = pl.BlockSpec(memory_space=pl.ANY)


def _rest_ici_copies(shard_refs, full_refs, sems):
    x, y, c = _me()
    chip = 2 * x + y
    n = 3 * len(REST)
    copies = []
    for i, (name, ax) in enumerate(REST):
        hr = SHARD[name][0] // 2
        for j, (fx, fy) in enumerate(_CHIP_FLIPS):
            copies.append(pltpu.make_async_remote_copy(
                src_ref=shard_refs[i].at[pl.ds(c * hr, hr), :],
                dst_ref=_region(full_refs[i], ax, chip, c, SHARD[name]),
                send_sem=sems[3 * i + j], recv_sem=sems[n + 3 * i + j],
                device_id=(x ^ fx, y ^ fy, c), device_id_type=MESH))
    return copies


def _rest_d2d_copies(full_refs, sems):
    x, y, c = _me()
    n = 3 * len(REST)
    copies = []
    for i, (name, ax) in enumerate(REST):
        for j, (fx, fy) in enumerate(_CHIP_FLIPS):
            reg = _region(full_refs[i], ax, 2 * (x ^ fx) + (y ^ fy), c, SHARD[name])
            copies.append(pltpu.make_async_remote_copy(
                src_ref=reg, dst_ref=reg, send_sem=sems[3 * i + j], recv_sem=sems[n + 3 * i + j],
                device_id=(x, y, 1 - c), device_id_type=MESH))
    return copies


def _gather_rest_start(shards, fulls, after):
    nr, ns, na = len(REST), 6 * len(REST), len(after)

    def body(*refs):
        for cp in _rest_ici_copies(refs[:nr], refs[nr:2 * nr], refs[2 * nr + na:2 * nr + na + ns]):
            cp.start()
        token = refs[-1]
        token[...] = jnp.zeros_like(token)

    hbm = lambda a: pltpu.HBM(a.shape, a.dtype)
    res = pl.pallas_call(
        body, name="gather_rest_start",
        out_shape=(pltpu.SemaphoreType.DMA(()),) * ns + tuple(hbm(a) for a in shards + fulls)
        + (jax.ShapeDtypeStruct((8, 128), F32),),
        in_specs=(_HBM_SPEC,) * (2 * nr) + (_ANY_SPEC,) * na,
        out_specs=(_SEM_SPEC,) * ns + (_HBM_SPEC,) * (2 * nr) + (pl.BlockSpec(memory_space=pltpu.VMEM),),
        input_output_aliases={k: ns + k for k in range(2 * nr)}, compiler_params=_SIDE_EFFECTS,
    )(*[pltpu.with_memory_space_constraint(a, pltpu.HBM) for a in shards + fulls], *after)
    return res[:ns], res[ns:ns + nr], res[ns + nr:ns + 2 * nr], res[-1]


def _gather_rest_forward(sems, shards, fulls, after):
    nr, ns = len(REST), 6 * len(REST)

    def body(*refs):
        shard_refs, full_refs, old = refs[:nr], refs[nr:2 * nr], refs[2 * nr:2 * nr + ns]
        new = refs[2 * nr + ns + len(after):2 * nr + 2 * ns + len(after)]
        for cp in _rest_ici_copies(shard_refs, full_refs, old):
            cp.wait_send()
            cp.wait_recv()
        for cp in _rest_d2d_copies(full_refs, new):
            cp.start()
        token = refs[-1]
        token[...] = jnp.zeros_like(token)

    res = pl.pallas_call(
        body, name="gather_rest_forward",
        out_shape=(pltpu.SemaphoreType.DMA(()),) * ns + tuple(pltpu.HBM(a.shape, a.dtype) for a in fulls)
        + (jax.ShapeDtypeStruct((8, 128), F32),),
        in_specs=(_HBM_SPEC,) * (2 * nr) + (_SEM_SPEC,) * ns + (_ANY_SPEC,) * len(after),
        out_specs=(_SEM_SPEC,) * ns + (_HBM_SPEC,) * nr + (pl.BlockSpec(memory_space=pltpu.VMEM),),
        input_output_aliases={nr + k: ns + k for k in range(nr)}, compiler_params=_SIDE_EFFECTS,
    )(*shards, *fulls, *sems, *after)
    return res[:ns], res[ns:ns + nr], res[-1]


def _gather_rest_end(sems, fulls, after):
    nr, ns = len(REST), 6 * len(REST)

    def body(*refs):
        for cp in _rest_d2d_copies(refs[:nr], refs[nr:nr + ns]):
            cp.wait_send()
            cp.wait_recv()

    return pl.pallas_call(
        body, name="gather_rest_end",
        out_shape=tuple(pltpu.HBM(a.shape, a.dtype) for a in fulls),
        in_specs=(_HBM_SPEC,) * nr + (_SEM_SPEC,) * ns + (_ANY_SPEC,) * len(after),
        out_specs=(_HBM_SPEC,) * nr,
        input_output_aliases={k: k for k in range(nr)}, compiler_params=_SIDE_EFFECTS,
    )(*fulls, *sems, *after)


def _adam_update(w, g, m, v):
    mn = ADAM_B1 * m + (1.0 - ADAM_B1) * g
    vn = ADAM_B2 * v + (1.0 - ADAM_B2) * (g * g)
    m_hat = mn / (1.0 - ADAM_B1 ** ADAM_STEP)
    v_hat = vn / (1.0 - ADAM_B2 ** ADAM_STEP)
    return -ADAM_LR * (m_hat / (jnp.sqrt(v_hat) + ADAM_EPS) + ADAM_WD * w), mn, vn


def _final_sum(name, pos, axis, psum, recv, shard_shape, after=(), tr=128):
    r, cw = shard_shape
    hr = r // 2
    tr = min(tr, hr)
    nt = hr // tr
    n_after = len(after)

    def kern(pos_ref, p_ref, r_ref, *rest):
        g_ref, send_buf, land_buf, s_sem, r_sem = rest[n_after:]
        p, t = pl.program_id(0), pl.program_id(1)
        sib = _sibling()

        def copy(i):
            return pltpu.make_async_remote_copy(
                src_ref=send_buf.at[i], dst_ref=land_buf.at[i], send_sem=s_sem.at[i],
                recv_sem=r_sem.at[i], device_id=sib, device_id_type=MESH)

        @pl.when(p == 0)
        def _():
            tot = p_ref[...].astype(F32)
            for j in range(3):
                tot = tot + r_ref[j].astype(F32)
            send_buf[t] = tot
            copy(t).start()
            g_ref[...] = tot

        @pl.when(p == 1)
        def _():
            copy(t).wait_recv()
            g_ref[...] = land_buf[t]

        @pl.when(jnp.logical_and(p == 1, t == nt - 1))
        def _():
            for i in range(nt):
                copy(i).wait_send()

    def shard_rows(p, t, pos_ref):
        return (jnp.where(p == 0, pos_ref[0], 1 - pos_ref[0]) * nt + t, 0)

    def own_part(p, t, pos_ref):
        tt = jnp.where(p == 0, t, nt - 1)
        return (tt, pos_ref[1]) if axis == 1 else (pos_ref[1] * nt + tt, 0)

    grid_spec = pltpu.PrefetchScalarGridSpec(
        num_scalar_prefetch=1, grid=(2, nt),
        in_specs=[pl.BlockSpec((tr, cw), own_part),
                  pl.BlockSpec((3, tr, cw), lambda p, t, pos_ref: (0, jnp.where(p == 0, t, nt - 1), 0))]
        + [pl.BlockSpec(memory_space=pl.ANY)] * n_after,
        out_specs=pl.BlockSpec((tr, cw), shard_rows),
        scratch_shapes=[pltpu.VMEM((nt, tr, cw), F32), pltpu.VMEM((nt, tr, cw), F32),
                        pltpu.SemaphoreType.DMA((nt,)), pltpu.SemaphoreType.DMA((nt,))])
    return pl.pallas_call(
        kern, name=name, grid_spec=grid_spec, out_shape=jax.ShapeDtypeStruct((r, cw), F32),
        compiler_params=_cparams(("arbitrary", "arbitrary")),
    )(pos, psum, recv, *after)


def _adamw(name, w, g, m, v):
    r, cw = w.shape
    tr = min(r, 128)

    def kern(w_ref, g_ref, m_ref, v_ref, go_ref, d_ref, nm_ref, nv_ref):
        gv = g_ref[...]
        go_ref[...] = gv
        d_ref[...], nm_ref[...], nv_ref[...] = _adam_update(w_ref[...], gv, m_ref[...], v_ref[...])

    spec = pl.BlockSpec((tr, cw), lambda i: (i, 0))
    return pl.pallas_call(
        kern, name=name, grid=(r // tr,), in_specs=[spec] * 4, out_specs=[spec] * 4,
        out_shape=[jax.ShapeDtypeStruct((r, cw), F32)] * 4, compiler_params=_cparams(("parallel",)),
    )(w, g, m, v)


_PACK_W = ADA_COLS
_NB = REL_BUCKETS * N_ATT_HEADS
_SMALL_SLOTS = {
    "b_ada": (0, 0, ADA_COLS),
    "norm1_g": (1, 0, D_MODEL), "norm2_g": (1, D_MODEL, D_MODEL), "norm_f_g": (1, 2 * D_MODEL, D_MODEL),
    "ret_gn_g": (1, 3 * D_MODEL, RET_V_W),
    "ret_gn_b": (2, 0, RET_V_W), "rel_bias": (2, RET_V_W, _NB), "loss": (2, RET_V_W + 512, 128),
}


def _pack_small(vals):
    rows = []
    for r in range(8):
        items = sorted([(off, n) for n, (rr, off, _) in _SMALL_SLOTS.items() if rr == r and n in vals])
        parts, pos = [], 0
        for off, n in items:
            if off > pos:
                parts.append(jnp.zeros((1, off - pos), F32))
            parts.append(vals[n].reshape(1, -1).astype(F32))
            pos = off + _SMALL_SLOTS[n][2]
        if pos < _PACK_W:
            parts.append(jnp.zeros((1, _PACK_W - pos), F32))
        rows.append(jnp.concatenate(parts, axis=-1))
    return jnp.concatenate(rows, axis=0)


def _adamw_small(tot, names, wmv):
    n = len(names)

    def kern(tot_ref, *refs):
        ins, outs = refs[:3 * n], refs[3 * n:]
        for i, name in enumerate(names):
            row, off, width = _SMALL_SLOTS[name]
            g = tot_ref[row:row + 1, off:off + width]
            outs[i][...] = g
            outs[n + i][...], outs[2 * n + i][...], outs[3 * n + i][...] = _adam_update(
                ins[i][...], g, ins[n + i][...], ins[2 * n + i][...])

    vm = pl.BlockSpec(memory_space=pltpu.VMEM)
    shapes = [jax.ShapeDtypeStruct((1, _SMALL_SLOTS[name][2]), F32) for name in names]
    res = pl.pallas_call(
        kern, name="adamw_small", in_specs=[vm] * (1 + 3 * n), out_specs=[vm] * (4 * n),
        out_shape=shapes * 4,
    )(tot, *wmv[0], *wmv[1], *wmv[2])
    return res[:n], res[n:2 * n], res[2 * n:3 * n], res[3 * n:]


def _unpack_small(pack, name):
    r, off, wd = _SMALL_SLOTS[name]
    return pack[r:r + 1, off:off + wd]


def kernel(x, c, w_ada, b_ada, norm1_g, w_in, rel_bias, ret_gn_g, ret_gn_b, w_ret_out, w_att_out, w_o, norm2_g, w_ff1, w_ff2, norm_f_g, loss_target, m_w_ada, m_b_ada, m_norm1_g, m_w_in, m_rel_bias, m_ret_gn_g, m_ret_gn_b, m_w_ret_out, m_w_att_out, m_w_o, m_norm2_g, m_w_ff1, m_w_ff2, m_norm_f_g, v_w_ada, v_b_ada, v_norm1_g, v_w_in, v_rel_bias, v_ret_gn_g, v_ret_gn_b, v_w_ret_out, v_w_att_out, v_w_o, v_norm2_g, v_w_ff1, v_w_ff2, v_norm_f_g):
    given = dict(locals())
    big_names = [n for n, _ in BIG]
    shard_w = {n: given[n][0] for n in big_names}
    assert all(shard_w[n].shape == SHARD[n] for n in big_names)

    shards_bf = [shard_w[n].astype(BF16) for n in big_names]
    full = _gather_weights(shards_bf, 1)
    mod, sc_all = _ada_fwd(c, w_ada[0], b_ada)
    rest_gather = _gather_rest_start(shards_bf[1:], list(full[1:]), [mod])
    pos = _where_am_i()

    loss, grad_x, d_mod, small, g_big, pending = _local_step(
        pos, x[0], loss_target[0], mod, norm1_g, norm2_g, norm_f_g.reshape(1, -1), rel_bias, ret_gn_g,
        ret_gn_b, full[0], rest_gather)

    pack_g = _pack_small(dict(b_ada=d_mod, norm1_g=small["norm1_g"], norm2_g=small["norm2_g"],
                              norm_f_g=small["norm_f_g"], ret_gn_g=small["gn_g"], ret_gn_b=small["gn_b"],
                              rel_bias=small["rel_bias"], loss=loss[:, :128]))
    tot, g_w_ada = _small_reduce(pack_g, sc_all, after=list(g_big.values()))

    small_names = ["b_ada", "norm1_g", "rel_bias", "ret_gn_g", "ret_gn_b", "norm2_g", "norm_f_g"]
    small_out = _adamw_small(tot, small_names, [[given[p + n].reshape(1, -1) for n in small_names]
                                                for p in ("", "m_", "v_")])
    grads, deltas, new_m, new_v = ({n: t.reshape(given[n].shape) for n, t in zip(small_names, group)}
                                   for group in small_out)
    sd = deltas["b_ada"]
    g_big["w_ada"] = g_w_ada
    for n in ["w_ada"] + big_names[1:] + big_names[:1]:
        if n == "w_in":
            gw_in, sems, land = pending
            done = [tot, sd] + [deltas[k] for k in ["w_ada"] + big_names[1:]]
            (gw_in,), (got,) = _ici_wait("ici_wait_w_in", [n], sems, [gw_in], [land], done)
            g_big[n] = _final_sum("final_w_in", pos, 1, gw_in, got, SHARD[n])
        g, d, nm, nv = _adamw("adamw_" + n, given[n][0], g_big[n], given["m_" + n][0], given["v_" + n][0])
        grads[n], deltas[n], new_m[n], new_v[n] = g[None], d[None], nm[None], nv[None]

    order = ["w_ada", "b_ada", "norm1_g", "w_in", "rel_bias", "ret_gn_g", "ret_gn_b", "w_ret_out",
             "w_att_out", "w_o", "norm2_g", "w_ff1", "w_ff2", "norm_f_g"]
    loss_out = _unpack_small(tot, "loss")[0, 0]
    return (loss_out, grad_x[None], *[grads[n] for n in order], *[deltas[n] for n in order],
            *[new_m[n] for n in order], *[new_v[n] for n in order])
```

```python
import math

import jax
import jax.numpy as jnp
import numpy as np
from jax import lax
from jax.experimental import pallas as pl
from jax.experimental.pallas import tpu as pltpu

F32 = jnp.float32
BF16 = jnp.bfloat16
I32 = jnp.int32

SEQ = 2048
D_MODEL = 1024
RET_HEADS = 4
RET_DK = 256
RET_DV = 512
RET_CHUNK = 128
RET_SUB = 4
RET_QK_W = RET_HEADS * RET_DK
RET_V_W = RET_HEADS * RET_DV
ATT_GROUPS = ((128, 1), (512, 4), (2048, 16))
ATT_HPG = 4
ATT_DH = 128
ATT_W = ATT_HPG * ATT_DH
ATT_BLK = 128
REL_BUCKETS = 32
REL_MAX_DIST = 2048
N_ATT_HEADS = 12
D_FF = 4 * D_MODEL
RMS_EPS = 1e-6
GN_EPS = 1e-5
ROPE_BASE = 10000.0
IN_COLS = 2 * RET_QK_W + 2 * RET_V_W + 9 * ATT_W + 2 * D_MODEL
OFF_Q, OFF_K, OFF_V, OFF_G = 0, RET_QK_W, 2 * RET_QK_W, 2 * RET_QK_W + RET_V_W
OFF_ATT = 2 * RET_QK_W + 2 * RET_V_W
OFF_GATE = OFF_ATT + 9 * ATT_W
N_CHIPS = 4
N_DEV = 8
ADA_COLS = 6 * D_MODEL

ADAM_LR = 0.001
ADAM_B1 = 0.9
ADAM_B2 = 0.999
ADAM_EPS = 1e-08
ADAM_WD = 0.01
ADAM_STEP = 10

VMEM_LIMIT_V7X = 56 * 1024 * 1024
MESH = pl.DeviceIdType.MESH


def _cparams(sem):
    return pltpu.CompilerParams(dimension_semantics=sem, vmem_limit_bytes=VMEM_LIMIT_V7X)


def _sigmoid(v):
    return 1.0 / (1.0 + jnp.exp(-v))


TM, TN = 1024, 1024


def _piece_chunks(piece, width):
    arr, stacked = piece
    return arr.shape[0] if stacked else arr.shape[1] // width


def _piece_spec(piece, rows, width, start, row_of, chunk_of):
    arr, stacked = piece
    last = _piece_chunks(piece, width) - 1

    def local(*ids):
        return jnp.clip(chunk_of(*ids) - start, 0, last)

    def row(*ids):
        rel = chunk_of(*ids) - start
        return jnp.where(jnp.logical_and(rel >= 0, rel <= last), row_of(*ids), 0)

    if stacked:
        return pl.BlockSpec((None, rows, width), lambda *ids: (local(*ids), row(*ids), 0))
    return pl.BlockSpec((rows, width), lambda *ids: (row(*ids), local(*ids)))


def _piece_starts(pieces, width):
    return [sum(_piece_chunks(p, width) for p in pieces[:q]) for q in range(len(pieces))]


def _matmul(name, a, b, kind, m, n, k, outs, *, b_off=0, tm=TM, tn=TN, tk=1024,
            epilogue=None, extras=(), after=(), n_sums=0):
    tm, tn, tk = min(tm, m), min(tn, n), min(tk, k)
    nk = k // tk
    pieces = a if isinstance(a, list) else [(a, False)]
    starts = _piece_starts(pieces, tk)
    if kind == "nn":
        a_specs = [pl.BlockSpec((tm, tk), lambda i, j, kk: (i, kk))]
        b_spec = pl.BlockSpec((tk, tn), lambda i, j, kk: (kk, b_off // tn + j))
        dn = (((1,), (0,)), ((), ()))
    elif kind == "nt":
        a_specs = [_piece_spec(p, tm, tk, st, lambda i, j, kk: i, lambda i, j, kk: kk)
                   for p, st in zip(pieces, starts)]
        b_spec = pl.BlockSpec((tn, tk), lambda i, j, kk: (j, b_off // tk + kk))
        dn = (((1,), (1,)), ((), ()))
    else:
        a_specs = [pl.BlockSpec((tk, tm), lambda i, j, kk: (kk, i))]
        b_spec = pl.BlockSpec((tk, tn), lambda i, j, kk: (kk, j))
        dn = (((0,), (0,)), ((), ()))
    n_a, n_ex, n_out = len(pieces), len(extras), len(outs)
    if epilogue is None:
        epilogue = lambda acc: (acc,)

    assert n_sums == 0 or tn == n

    def finish(acc, ex_refs, out_refs, first_rows):
        res = epilogue(acc, *[r[...] for r in ex_refs])
        for r, v in zip(out_refs[:n_out], res[:n_out]):
            r[...] = v.astype(r.dtype)
        for r, v in zip(out_refs[n_out:], res[n_out:]):
            @pl.when(first_rows)
            def _(r=r, v=v):
                r[...] = v

            @pl.when(jnp.logical_not(first_rows))
            def _(r=r, v=v):
                r[...] += v

    n_in = n_a + 1 + n_ex + len(after)

    def kern(*refs):
        a_refs, b_ref = refs[:n_a], refs[n_a]
        ex_refs = refs[n_a + 1:n_a + 1 + n_ex]
        out_refs = refs[n_in:n_in + n_out + n_sums]
        first_rows, kk = pl.program_id(0) == 0, pl.program_id(2)
        dot = lambda a_ref: lax.dot_general(a_ref[...], b_ref[...], dn, preferred_element_type=F32)
        if nk == 1:
            finish(dot(a_refs[0]), ex_refs, out_refs, first_rows)
            return
        acc_ref = refs[n_in + n_out + n_sums]
        if n_a == 1:
            part = dot(a_refs[0])

            @pl.when(kk == 0)
            def _():
                acc_ref[...] = part

            @pl.when(kk > 0)
            def _():
                acc_ref[...] += part
        else:
            @pl.when(kk == 0)
            def _():
                acc_ref[...] = jnp.zeros_like(acc_ref)

            for q in range(n_a):
                @pl.when(jnp.logical_and(kk >= starts[q], kk < starts[q] + _piece_chunks(pieces[q], tk)))
                def _(q=q):
                    acc_ref[...] += dot(a_refs[q])

        @pl.when(kk == nk - 1)
        def _():
            finish(acc_ref[...], ex_refs, out_refs, first_rows)

    in_specs = a_specs + [b_spec] + [pl.BlockSpec(bs, im) for _, bs, im in extras]
    in_specs += [pl.BlockSpec(memory_space=pl.ANY)] * len(after)
    sem = ("arbitrary",) * 3 if n_sums else ("parallel", "parallel", "arbitrary")
    return pl.pallas_call(
        kern, name=name, grid=(m // tm, n // tn, nk), in_specs=in_specs,
        out_specs=[pl.BlockSpec((tm, tn), lambda i, j, kk: (i, j)) for _ in outs]
        + [pl.BlockSpec((1, tn), lambda i, j, kk: (0, 0))] * n_sums,
        out_shape=[jax.ShapeDtypeStruct((m, n), dt) for dt in outs]
        + [jax.ShapeDtypeStruct((1, n), F32)] * n_sums,
        scratch_shapes=[] if nk == 1 else [pltpu.VMEM((tm, tn), F32)],
        compiler_params=_cparams(sem),
    )(*[p[0] for p in pieces], b, *[e[0] for e in extras], *after)


def _ici_copies(psum_ref, recv_ref, s_sem, r_sem, axis, shard_shape):
    x, y, c = _me()
    hr, cw = shard_shape[0] // 2, shard_shape[1]
    pick = lambda sems, j: sems[j] if isinstance(sems, (list, tuple)) else sems.at[j]
    copies = []
    for j, (fx, fy) in enumerate(_CHIP_FLIPS):
        chip = 2 * (x ^ fx) + (y ^ fy)
        src = psum_ref.at[:, pl.ds(chip * cw, cw)] if axis == 1 else psum_ref.at[pl.ds(chip * hr, hr), :]
        copies.append(pltpu.make_async_remote_copy(
            src_ref=src, dst_ref=recv_ref.at[j], send_sem=pick(s_sem, j), recv_sem=pick(r_sem, j),
            device_id=(x ^ fx, y ^ fy, c), device_id_type=MESH))
    return copies


_HBM_SPEC = pl.BlockSpec(memory_space=pltpu.HBM)
_SEM_SPEC = pl.BlockSpec(memory_space=pltpu.SEMAPHORE)


def _split_ici_copies(names, p_refs, land_refs, sems):
    copies = []
    for i, n in enumerate(names):
        copies += _ici_copies(p_refs[i], land_refs[i], list(sems[6 * i:6 * i + 3]),
                              list(sems[6 * i + 3:6 * i + 6]), dict(BIG)[n], SHARD[n])
    return copies


def _ici_start(name, names, psums):
    nw, ns = len(names), 6 * len(names)
    lands = [lax.empty((3, SHARD[n][0] // 2, SHARD[n][1]), BF16) for n in names]

    def body(*refs):
        for cp in _split_ici_copies(names, refs[:nw], refs[nw:2 * nw], refs[2 * nw:2 * nw + ns]):
            cp.start()
        token = refs[-1]
        token[...] = jnp.zeros_like(token)

    res = pl.pallas_call(
        body, name=name,
        out_shape=(pltpu.SemaphoreType.DMA(()),) * ns
        + tuple(pltpu.HBM(a.shape, BF16) for a in list(psums) + lands)
        + (jax.ShapeDtypeStruct((8, 128), F32),),
        in_specs=(_HBM_SPEC,) * (2 * nw),
        out_specs=(_SEM_SPEC,) * ns + (_HBM_SPEC,) * (2 * nw) + (pl.BlockSpec(memory_space=pltpu.VMEM),),
        input_output_aliases={k: ns + k for k in range(2 * nw)},
        compiler_params=pltpu.CompilerParams(has_side_effects=pltpu.SideEffectType.DATAFLOW_SIDE_EFFECTING),
    )(*[pltpu.with_memory_space_constraint(a, pltpu.HBM) for a in list(psums) + lands])
    return res[:ns], res[ns:ns + nw], res[ns + nw:ns + 2 * nw], res[-1]


def _ici_wait(name, names, sems, p_thru, land_thru, after):
    nw, ns = len(names), 6 * len(names)

    def body(*refs):
        for cp in _split_ici_copies(names, refs[:nw], refs[nw:2 * nw], refs[2 * nw:2 * nw + ns]):
            cp.wait_send()
            cp.wait_recv()

    res = pl.pallas_call(
        body, name=name,
        out_shape=tuple(pltpu.HBM(a.shape, BF16) for a in list(p_thru) + list(land_thru)),
        in_specs=(_HBM_SPEC,) * (2 * nw) + (_SEM_SPEC,) * ns + (pl.BlockSpec(memory_space=pl.ANY),) * len(after),
        out_specs=(_HBM_SPEC,) * (2 * nw), input_output_aliases={k: k for k in range(2 * nw)},
        compiler_params=pltpu.CompilerParams(has_side_effects=pltpu.SideEffectType.DATAFLOW_SIDE_EFFECTING),
    )(*p_thru, *land_thru, *sems, *after)
    return res[:nw], res[nw:]


def _where_am_i():
    x, y, c = _me()
    return jnp.stack([c, 2 * x + y]).astype(I32)


def _sibling():
    x, y, c = _me()
    return (x, y, 1 - c)


N_SEND_SLOTS = 2


def _matmul_tn_pair(name, pos, a, b, m, n, k, shard_rows, *, tm, tn, tk):
    hr = shard_rows // 2
    tm, tn, tk = min(tm, hr), min(tn, n), min(tk, k)
    tph = hr // tm
    nt, nj, nk = (m // 2) // tm, n // tn, k // tk
    n_tiles = nt * nj

    def row_block(p, t, pos_ref):
        half = jnp.where(p == 0, 1 - pos_ref[0], pos_ref[0])
        return (t // tph) * (2 * tph) + half * tph + t % tph

    pieces = b if isinstance(b, list) else [(b, False)]
    starts = _piece_starts(pieces, tn)
    n_b = len(pieces)

    def kern(pos_ref, a_ref, *rest):
        b_refs = rest[:n_b]
        o_ref, acc_ref, send_buf, land_buf, s_sem, r_sem = rest[n_b:]
        p, t, j, kk = pl.program_id(0), pl.program_id(1), pl.program_id(2), pl.program_id(3)
        idx = t * nj + j
        sib = _sibling()

        def copy(i):
            return pltpu.make_async_remote_copy(
                src_ref=send_buf.at[i % N_SEND_SLOTS], dst_ref=land_buf.at[i], send_sem=s_sem.at[i],
                recv_sem=r_sem.at[i], device_id=sib, device_id_type=MESH)

        @pl.when(kk == 0)
        def _():
            acc_ref[...] = jnp.zeros_like(acc_ref)

        for q in range(n_b):
            @pl.when(jnp.logical_and(j >= starts[q], j < starts[q] + _piece_chunks(pieces[q], tn)))
            def _(q=q):
                acc_ref[...] += lax.dot_general(a_ref[...], b_refs[q][...], _TN, preferred_element_type=F32)

        @pl.when(jnp.logical_and(kk == nk - 1, p == 0))
        def _():
            @pl.when(idx >= N_SEND_SLOTS)
            def _():
                copy(idx - N_SEND_SLOTS).wait_send()

            send_buf[idx % N_SEND_SLOTS] = acc_ref[...].astype(BF16)
            copy(idx).start()

        @pl.when(jnp.logical_and(kk == nk - 1, p == 1))
        def _():
            copy(idx).wait_recv()
            o_ref[...] = (acc_ref[...] + land_buf[idx].astype(F32)).astype(BF16)

        @pl.when(jnp.logical_and(jnp.logical_and(p == 1, idx == n_tiles - 1), kk == nk - 1))
        def _():
            for i in range(max(n_tiles - N_SEND_SLOTS, 0), n_tiles):
                copy(i).wait_send()

    grid_spec = pltpu.PrefetchScalarGridSpec(
        num_scalar_prefetch=1, grid=(2, nt, nj, nk),
        in_specs=[pl.BlockSpec((tk, tm), lambda p, t, j, kk, pos_ref: (kk, row_block(p, t, pos_ref)))]
        + [_piece_spec(pc, tk, tn, st, lambda p, t, j, kk, pos_ref: kk, lambda p, t, j, kk, pos_ref: j)
           for pc, st in zip(pieces, starts)],
        out_specs=pl.BlockSpec((tm, tn), lambda p, t, j, kk, pos_ref: (p * t, p * j)),
        scratch_shapes=[pltpu.VMEM((tm, tn), F32), pltpu.VMEM((N_SEND_SLOTS, tm, tn), BF16),
                        pltpu.VMEM((n_tiles, tm, tn), BF16),
                        pltpu.SemaphoreType.DMA((n_tiles,)), pltpu.SemaphoreType.DMA((n_tiles,))])
    return pl.pallas_call(
        kern, name=name, grid_spec=grid_spec, out_shape=jax.ShapeDtypeStruct((m // 2, n), BF16),
        compiler_params=_cparams(("arbitrary",) * 4),
    )(pos, a, *[pc[0] for pc in pieces])


def _rope_tables():
    half = RET_DK // 2
    f32 = np.float32
    inv = np.power(f32(ROPE_BASE), -np.arange(half, dtype=f32) / f32(half)).astype(f32)
    ang = (np.arange(SEQ, dtype=f32)[:, None] * inv[None, :]).astype(f32)
    return jnp.asarray(np.cos(ang).astype(f32)), jnp.asarray(np.sin(ang).astype(f32))


def _decay_tables():
    c = RET_CHUNK
    f32 = np.float32
    log_g = np.log1p(-np.power(f32(2.0), f32(-5.0) - np.arange(RET_HEADS, dtype=f32))).astype(f32)
    idx = np.arange(c, dtype=f32)
    rel = idx[:, None] - idx[None, :]
    din = np.where(rel >= 0, np.exp(log_g[:, None, None] * np.maximum(rel, f32(0.0))), f32(0.0)).astype(f32)
    qd = np.exp(log_g[:, None] * (idx + f32(1.0))).astype(f32)[:, :, None]
    kd = np.exp(log_g[:, None] * (f32(c) - f32(1.0) - idx)).astype(f32)[:, :, None]
    cd = np.exp(log_g * f32(c)).astype(f32)
    return jnp.asarray(din), jnp.asarray(qd), jnp.asarray(kd), jnp.asarray(cd)


def _t5_bucket(dist):
    max_exact = REL_BUCKETS // 2
    d_f = jnp.maximum(dist, 1).astype(F32)
    large = max_exact + (jnp.log(d_f / max_exact) / math.log(REL_MAX_DIST / max_exact)
                         * (REL_BUCKETS - max_exact)).astype(I32)
    large = jnp.minimum(large, REL_BUCKETS - 1)
    return jnp.where(dist < max_exact, dist, large)


def _bucket_tables():
    qi = jnp.arange(ATT_BLK)[:, None]
    kj = jnp.arange(2 * ATT_BLK)[None, :]
    dist = jnp.clip(ATT_BLK + qi - kj, 0, ATT_BLK)
    return jnp.stack([_t5_bucket(dist * dil) for _, dil in ATT_GROUPS]).astype(I32)


def _retention_fwd(rqk, rv, rg, gn_g, gn_b, din, qd, kd, cd):
    nc = SEQ // RET_CHUNK
    c, dk, dv = RET_CHUNK, RET_DK, RET_DV

    def kern(q_ref, k_ref, v_ref, rg_ref, g_ref, b_ref, din_ref, qd_ref, kd_ref, cd_ref,
             o_ref, st_ref, gated_ref, state):
        n = pl.program_id(0)

        @pl.when(n == 0)
        def _():
            state[...] = jnp.zeros_like(state)

        for sub in range(RET_SUB):
            rows = slice(sub * c, (sub + 1) * c)
            for h in range(RET_HEADS):
                q, k = q_ref[rows, h * dk:(h + 1) * dk], k_ref[rows, h * dk:(h + 1) * dk]
                v = v_ref[rows, h * dv:(h + 1) * dv]
                s_b = state[h].astype(BF16)
                st_ref[h, sub] = s_b
                a = lax.dot_general(q, k, _NT, preferred_element_type=F32) * din_ref[h]
                o = jnp.dot(a.astype(BF16), v, preferred_element_type=F32)
                o += jnp.dot(q, s_b, preferred_element_type=F32) * qd_ref[h]
                v_cols = slice(h * dv, (h + 1) * dv)
                o_ref[rows, v_cols] = o
                nrm, _ = _gn_parts(o)
                gate = rg_ref[rows, v_cols].astype(F32)
                gated_ref[rows, v_cols] = ((gate * _sigmoid(gate))
                                           * (nrm * g_ref[:, v_cols] + b_ref[:, v_cols])).astype(BF16)
                kk = (k.astype(F32) * kd_ref[h]).astype(BF16)
                state[h] = state[h] * cd_ref[h] + lax.dot_general(kk, v, _TN, preferred_element_type=F32)

    whole = lambda a: pl.BlockSpec(a.shape, lambda n: (0,) * a.ndim)
    cs = RET_SUB * c
    rows_v = pl.BlockSpec((cs, RET_V_W), lambda n: (n, 0))
    return pl.pallas_call(
        kern, name="retention_fwd", grid=(nc // RET_SUB,),
        in_specs=[
            pl.BlockSpec((cs, RET_QK_W), lambda n: (n, 0)),
            pl.BlockSpec((cs, RET_QK_W), lambda n: (n, 1)),
            rows_v, rows_v, whole(gn_g), whole(gn_b),
            whole(din), whole(qd), whole(kd),
            pl.BlockSpec(memory_space=pltpu.SMEM),
        ],
        out_specs=[
            rows_v,
            pl.BlockSpec((RET_HEADS, RET_SUB, dk, dv), lambda n: (0, n, 0, 0)),
            rows_v,
        ],
        out_shape=[
            jax.ShapeDtypeStruct((SEQ, RET_V_W), F32),
            jax.ShapeDtypeStruct((RET_HEADS, nc, dk, dv), BF16),
            jax.ShapeDtypeStruct((SEQ, RET_V_W), BF16),
        ],
        scratch_shapes=[pltpu.VMEM((RET_HEADS, dk, dv), F32)],
        compiler_params=_cparams(("arbitrary",)),
    )(rqk, rqk, rv, rg, gn_g, gn_b, din, qd, kd, cd)


def _retention_bwd(rqk, rv, states, d_gated, ro, rg, gn_g, gn_b, din, qd, kd, cd, cos, sin):
    nc = SEQ // RET_CHUNK
    c, dk, dv = RET_CHUNK, RET_DK, RET_DV
    half = dk // 2
    last = nc // RET_SUB - 1

    def unrot(g, cs, sn):
        g1, g2 = g[:, :half], g[:, half:]
        return jnp.concatenate([g1 * cs + g2 * sn, g2 * cs - g1 * sn], axis=-1)

    def kern(q_ref, k_ref, v_ref, st_ref, dg_ref, ro_ref, rg_ref, g_ref, b_ref, din_ref, qd_ref, kd_ref,
             cd_ref, cos_ref, sin_ref, out_ref, drg_ref, dgn_g_ref, dgn_b_ref, dstate):
        step = pl.program_id(0)

        @pl.when(step == 0)
        def _():
            dstate[...] = jnp.zeros_like(dstate)
            dgn_g_ref[...] = jnp.zeros_like(dgn_g_ref)
            dgn_b_ref[...] = jnp.zeros_like(dgn_b_ref)

        for sub in reversed(range(RET_SUB)):
            rows = slice(sub * c, (sub + 1) * c)
            cs, sn = cos_ref[rows, :], sin_ref[rows, :]
            for h in range(RET_HEADS):
                qk_cols, v_cols = slice(h * dk, (h + 1) * dk), slice(h * dv, (h + 1) * dv)
                q, k, v = q_ref[rows, qk_cols], k_ref[rows, qk_cols], v_ref[rows, v_cols]
                s_b = st_ref[h, sub]
                nrm, rstd = _gn_parts(ro_ref[rows, v_cols])
                gate, dg = rg_ref[rows, v_cols].astype(F32), dg_ref[rows, v_cols].astype(F32)
                sg = _sigmoid(gate)
                gn_gain = g_ref[:, v_cols]
                drg_ref[rows, v_cols] = (dg * (nrm * gn_gain + b_ref[:, v_cols])
                                         * (sg * (1.0 + gate * (1.0 - sg)))).astype(BF16)
                d_ron = dg * (gate * sg)
                dgn_g_ref[:, v_cols] += jnp.sum(d_ron * nrm, axis=0, keepdims=True)
                dgn_b_ref[:, v_cols] += jnp.sum(d_ron, axis=0, keepdims=True)
                d_n = d_ron * gn_gain
                d_o = rstd * (d_n - jnp.mean(d_n, axis=-1, keepdims=True)
                              - nrm * jnp.mean(d_n * nrm, axis=-1, keepdims=True))
                d_ob = d_o.astype(BF16)
                d_oq = (d_o * qd_ref[h]).astype(BF16)
                ds_b = dstate[h].astype(BF16)
                din_m = din_ref[h]
                a_b = (lax.dot_general(q, k, _NT, preferred_element_type=F32) * din_m).astype(BF16)
                kk = (k.astype(F32) * kd_ref[h]).astype(BF16)
                d_v = lax.dot_general(a_b, d_ob, _TN, preferred_element_type=F32)
                d_v += jnp.dot(kk, ds_b, preferred_element_type=F32)
                d_a = (lax.dot_general(d_ob, v, _NT, preferred_element_type=F32) * din_m).astype(BF16)
                d_q = jnp.dot(d_a, k, preferred_element_type=F32)
                d_q += lax.dot_general(d_oq, s_b, _NT, preferred_element_type=F32)
                d_k = lax.dot_general(d_a, q, _TN, preferred_element_type=F32)
                d_k += lax.dot_general(v, ds_b, _NT, preferred_element_type=F32) * kd_ref[h]
                dstate[h] = dstate[h] * cd_ref[h] + lax.dot_general(q, d_oq, _TN,
                                                                    preferred_element_type=F32)
                out_ref[rows, h * dk:(h + 1) * dk] = unrot(d_q, cs, sn).astype(BF16)
                out_ref[rows, RET_QK_W + h * dk:RET_QK_W + (h + 1) * dk] = (
                    unrot(d_k, cs, sn) * (RET_DK ** -0.5)).astype(BF16)
                out_ref[rows, 2 * RET_QK_W + h * dv:2 * RET_QK_W + (h + 1) * dv] = d_v.astype(BF16)

    whole = lambda a: pl.BlockSpec(a.shape, lambda n: (0,) * a.ndim)
    rs = RET_SUB * c
    rows_v = pl.BlockSpec((rs, RET_V_W), lambda n: (last - n, 0))
    return pl.pallas_call(
        kern, name="retention_bwd", grid=(nc // RET_SUB,),
        in_specs=[
            pl.BlockSpec((rs, RET_QK_W), lambda n: (last - n, 0)),
            pl.BlockSpec((rs, RET_QK_W), lambda n: (last - n, 1)),
            rows_v,
            pl.BlockSpec((RET_HEADS, RET_SUB, dk, dv), lambda n: (0, last - n, 0, 0)),
            rows_v, rows_v, rows_v, whole(gn_g), whole(gn_b),
            whole(din), whole(qd), whole(kd),
            pl.BlockSpec(memory_space=pltpu.SMEM),
            pl.BlockSpec((rs, half), lambda n: (last - n, 0)),
            pl.BlockSpec((rs, half), lambda n: (last - n, 0)),
        ],
        out_specs=[pl.BlockSpec((rs, 2 * RET_QK_W + RET_V_W), lambda n: (last - n, 0)), rows_v,
                   whole(gn_g), whole(gn_b)],
        out_shape=[jax.ShapeDtypeStruct((SEQ, 2 * RET_QK_W + RET_V_W), BF16),
                   jax.ShapeDtypeStruct((SEQ, RET_V_W), BF16),
                   jax.ShapeDtypeStruct((1, RET_V_W), F32), jax.ShapeDtypeStruct((1, RET_V_W), F32)],
        scratch_shapes=[pltpu.VMEM((RET_HEADS, dk, dv), F32)],
        compiler_params=_cparams(("arbitrary",)),
    )(rqk, rqk, rv, states, d_gated, ro, rg, gn_g, gn_b, din, qd, kd, cd, cos, sin)


def _bias_build(rel_bias, buckets):
    ng = len(ATT_GROUPS)

    def kern(tab_ref, bkt_ref, o_ref):
        g, h = pl.program_id(0), pl.program_id(1)
        bkt = bkt_ref[...]
        acc = jnp.zeros(bkt.shape, F32)
        for b in range(REL_BUCKETS):
            acc = jnp.where(bkt == b, tab_ref[b, g * ATT_HPG + h], acc)
        o_ref[...] = acc

    return pl.pallas_call(
        kern, name="bias_build", grid=(ng, ATT_HPG),
        in_specs=[pl.BlockSpec(memory_space=pltpu.SMEM),
                  pl.BlockSpec((None, ATT_BLK, 2 * ATT_BLK), lambda g, h: (g, 0, 0))],
        out_specs=pl.BlockSpec((None, None, ATT_BLK, 2 * ATT_BLK), lambda g, h: (g, h, 0, 0)),
        out_shape=jax.ShapeDtypeStruct((ng, ATT_HPG, ATT_BLK, 2 * ATT_BLK), F32),
        compiler_params=_cparams(("arbitrary", "arbitrary")),
    )(rel_bias, buckets)


def _bias_grad(dsb, buckets):
    ng = len(ATT_GROUPS)

    def kern(ds_ref, bkt_ref, o_ref):
        g, h = pl.program_id(0), pl.program_id(1)
        bkt, ds = bkt_ref[...], ds_ref[...]
        for b in range(REL_BUCKETS):
            o_ref[b, g * ATT_HPG + h] = jnp.sum(jnp.where(bkt == b, ds, 0.0))

    return pl.pallas_call(
        kern, name="bias_grad", grid=(ng, ATT_HPG),
        in_specs=[pl.BlockSpec((None, None, ATT_BLK, 2 * ATT_BLK), lambda g, h: (g, h, 0, 0)),
                  pl.BlockSpec((None, ATT_BLK, 2 * ATT_BLK), lambda g, h: (g, 0, 0))],
        out_specs=pl.BlockSpec(memory_space=pltpu.SMEM),
        out_shape=jax.ShapeDtypeStruct((REL_BUCKETS, N_ATT_HEADS), F32),
        compiler_params=_cparams(("arbitrary", "arbitrary")),
    )(dsb, buckets)


_NT = (((1,), (1,)), ((), ()))
_TN = (((0,), (0,)), ((), ()))
_ATT_SCALE = ATT_DH ** -0.5


def _window_mask(has_prev):
    qi = lax.broadcasted_iota(I32, (ATT_BLK, 2 * ATT_BLK), 0)
    kj = lax.broadcasted_iota(I32, (ATT_BLK, 2 * ATT_BLK), 1)
    prev_ok = jnp.logical_and(jnp.logical_and(kj < ATT_BLK, kj >= qi), has_prev)
    return jnp.logical_or(prev_ok, jnp.logical_and(kj >= ATT_BLK, qi >= kj - ATT_BLK))


def _head_specs(col0):
    return pl.BlockSpec((SEQ, ATT_DH), lambda h: (0, col0 + h))


def _sub_rows(start, size, dil):
    return pl.ds(start, size) if dil == 1 else pl.ds(start, size, stride=dil)


def _att_blocks(dil):
    nb = SEQ // dil // ATT_BLK
    return [(r + dil * n * ATT_BLK, n > 0, n + 1 < nb) for r in range(dil) for n in range(nb)]


def _att_fwd(gi, dil, qkv, bias, after=()):
    blk, dh = ATT_BLK, ATT_DH
    pad = dil * blk
    col0 = 3 * ATT_HPG * gi

    def kern(q_ref, k_ref, v_ref, b_ref, *rest):
        o_ref, l_ref, qf, kpad, vpad = rest[len(after):]
        zero = jnp.zeros((pad, dh), F32)
        kpad[0:pad, :] = zero
        vpad[0:pad, :] = zero
        kpad[pad:, :] = k_ref[...].astype(F32)
        vpad[pad:, :] = v_ref[...].astype(F32)
        qf[...] = q_ref[...].astype(F32)
        bias_m = b_ref[...]
        for start, has_prev, _ in _att_blocks(dil):
            rows, window = _sub_rows(start, blk, dil), _sub_rows(start, 2 * blk, dil)
            q = qf[rows, :].astype(BF16)
            kw, vw = kpad[window, :].astype(BF16), vpad[window, :].astype(BF16)
            valid = _window_mask(has_prev)
            s = lax.dot_general(q, kw, _NT, preferred_element_type=F32) * _ATT_SCALE + bias_m
            s = jnp.where(valid, s, -1e30)
            mx = jnp.max(s, axis=-1, keepdims=True)
            e = jnp.exp(s - mx)
            den = jnp.sum(e, axis=-1, keepdims=True)
            o_ref[rows, :] = jnp.dot((e / den).astype(BF16), vw, preferred_element_type=F32)
            l_ref[rows, :] = jnp.broadcast_to(mx + jnp.log(den), (blk, dh))

    return pl.pallas_call(
        kern, name=f"att_fwd_g{gi}", grid=(ATT_HPG,),
        in_specs=[_head_specs(col0), _head_specs(col0 + ATT_HPG), _head_specs(col0 + 2 * ATT_HPG),
                  pl.BlockSpec((None, None, blk, 2 * blk), lambda h: (gi, h, 0, 0))]
        + [pl.BlockSpec(memory_space=pl.ANY)] * len(after),
        out_specs=[_head_specs(0), _head_specs(0)],
        out_shape=[jax.ShapeDtypeStruct((SEQ, ATT_W), F32), jax.ShapeDtypeStruct((SEQ, ATT_W), F32)],
        scratch_shapes=[pltpu.VMEM((SEQ, dh), F32), pltpu.VMEM((SEQ + pad, dh), F32),
                        pltpu.VMEM((SEQ + pad, dh), F32)],
        compiler_params=_cparams(("arbitrary",)),
    )(qkv, qkv, qkv, bias, *after)


def _att_bwd(gi, dil, qkv, d_att, lse, dd, bias):
    blk, dh = ATT_BLK, ATT_DH
    pad = dil * blk
    col0 = 3 * ATT_HPG * gi

    def kern(q_ref, k_ref, v_ref, do_ref, l_ref, d_ref, b_ref, dqkv_ref, dsb_ref,
             qf, kpad, vpad, dq_s, dkpad, dvpad):
        zero = jnp.zeros((pad, dh), F32)
        kpad[0:pad, :] = zero
        vpad[0:pad, :] = zero
        kpad[pad:, :] = k_ref[...].astype(F32)
        vpad[pad:, :] = v_ref[...].astype(F32)
        qf[...] = q_ref[...].astype(F32)
        dkpad[...] = jnp.zeros_like(dkpad)
        dvpad[...] = jnp.zeros_like(dvpad)
        bias_m = b_ref[...]
        ds_sum = jnp.zeros((blk, 2 * blk), F32)

        for start, has_prev, _ in _att_blocks(dil):
            rows, window = _sub_rows(start, blk, dil), _sub_rows(start, 2 * blk, dil)
            q, d_o = qf[rows, :].astype(BF16), do_ref[rows, :].astype(BF16)
            kw, vw = kpad[window, :].astype(BF16), vpad[window, :].astype(BF16)
            lrow, drow = l_ref[rows, :][:, :1], d_ref[rows, :][:, :1]
            valid = _window_mask(has_prev)
            s = lax.dot_general(q, kw, _NT, preferred_element_type=F32) * _ATT_SCALE + bias_m
            p = jnp.where(valid, jnp.exp(jnp.where(valid, s, -1e30) - lrow), 0.0)
            dp = lax.dot_general(d_o, vw, _NT, preferred_element_type=F32)
            ds = p * (dp - drow)
            ds_b = ds.astype(BF16)
            dq_s[rows, :] = jnp.dot(ds_b, kw, preferred_element_type=F32) * _ATT_SCALE
            dkpad[window, :] += lax.dot_general(ds_b, q, _TN, preferred_element_type=F32) * _ATT_SCALE
            dvpad[window, :] += lax.dot_general(p.astype(BF16), d_o, _TN, preferred_element_type=F32)
            ds_sum = ds_sum + ds
        dsb_ref[...] = ds_sum

        dqkv_ref[0] = dq_s[...].astype(BF16)
        dqkv_ref[1] = dkpad[pad:, :].astype(BF16)
        dqkv_ref[2] = dvpad[pad:, :].astype(BF16)

    return pl.pallas_call(
        kern, name=f"att_bwd_g{gi}", grid=(ATT_HPG,),
        in_specs=[_head_specs(col0), _head_specs(col0 + ATT_HPG), _head_specs(col0 + 2 * ATT_HPG),
                  _head_specs(0), _head_specs(0), _head_specs(0),
                  pl.BlockSpec((None, None, blk, 2 * blk), lambda h: (gi, h, 0, 0))],
        out_specs=[pl.BlockSpec((3, SEQ, dh), lambda h: (0, 0, h)),
                   pl.BlockSpec((None, blk, 2 * blk), lambda h: (h, 0, 0))],
        out_shape=[jax.ShapeDtypeStruct((3, SEQ, ATT_W), BF16),
                   jax.ShapeDtypeStruct((ATT_HPG, blk, 2 * blk), F32)],
        scratch_shapes=[pltpu.VMEM((SEQ, dh), F32), pltpu.VMEM((SEQ + pad, dh), F32),
                        pltpu.VMEM((SEQ + pad, dh), F32), pltpu.VMEM((SEQ, dh), F32),
                        pltpu.VMEM((SEQ + pad, dh), F32), pltpu.VMEM((SEQ + pad, dh), F32)],
        compiler_params=_cparams(("arbitrary",)),
    )(qkv, qkv, qkv, d_att, lse, dd, bias)


def _rms_parts(x):
    r = lax.rsqrt(jnp.mean(x * x, axis=-1, keepdims=True) + RMS_EPS)
    return x * r, r


def _rms_bwd(d_xhat, xhat, r):
    return r * (d_xhat - xhat * jnp.mean(d_xhat * xhat, axis=-1, keepdims=True))


def _prenorm_fwd(name, x, gain, shift, scale, tm=256):
    def kern(x_ref, g_ref, sh_ref, sc_ref, o_ref):
        xhat, _ = _rms_parts(x_ref[...])
        o_ref[...] = ((xhat * g_ref[...]) * (1.0 + sc_ref[...]) + sh_ref[...]).astype(BF16)

    rows = pl.BlockSpec((tm, D_MODEL), lambda i: (i, 0))
    vec = pl.BlockSpec((1, D_MODEL), lambda i: (0, 0))
    return pl.pallas_call(
        kern, name=name, grid=(x.shape[0] // tm,), in_specs=[rows, vec, vec, vec], out_specs=rows,
        out_shape=jax.ShapeDtypeStruct(x.shape, BF16), compiler_params=_cparams(("parallel",)),
    )(x, gain, shift, scale)


def _prenorm_bwd_epi(d_h, x, resid, gain, scale, branch=None, gate=None):
    xhat, r = _rms_parts(x)
    nrm = xhat * gain
    d_n = d_h * (1.0 + scale)
    dx = _rms_bwd(d_n * gain, xhat, r) + resid
    sums = (jnp.sum(d_h, axis=0, keepdims=True), jnp.sum(d_h * nrm, axis=0, keepdims=True),
            jnp.sum(d_n * xhat, axis=0, keepdims=True))
    if branch is None:
        return (dx,) + sums
    return (dx, dx * gate) + sums + (jnp.sum(dx * branch, axis=0, keepdims=True),)


def _row_operands(tm, rows, vecs):
    return ([(a, (tm, D_MODEL), lambda i, j, kk: (i, 0)) for a in rows]
            + [(v, (1, D_MODEL), lambda i, j, kk: (0, 0)) for v in vecs])


def _gn_parts(ro):
    mu = jnp.mean(ro, axis=-1, keepdims=True)
    cen = ro - mu
    rstd = lax.rsqrt(jnp.mean(cen * cen, axis=-1, keepdims=True) + GN_EPS)
    return cen * rstd, rstd


MERGE_TM = 512


def _att_out(os_, ls_, w_att_out, gates, ret_out):
    tm = MERGE_TM

    def kern(o0, o1, o2, l0, l1, l2, w_ref, ga_ref, gb_ref, ro_ref, att_ref, attb_ref, lse_ref,
             ao_ref, mg_ref):
        l0v, l1v, l2v = l0[...], l1[...], l2[...]
        mx = jnp.maximum(jnp.maximum(l0v, l1v), l2v)
        e0, e1, e2 = jnp.exp(l0v - mx), jnp.exp(l1v - mx), jnp.exp(l2v - mx)
        den = e0 + e1 + e2
        att = (e0 / den) * o0[...] + (e1 / den) * o1[...] + (e2 / den) * o2[...]
        att_b = att.astype(BF16)
        att_ref[...] = att
        attb_ref[...] = att_b
        lse_ref[...] = mx + jnp.log(den)
        att_out = jnp.dot(att_b, w_ref[...], preferred_element_type=F32)
        ao_ref[...], merged = _merge_fwd_epi(att_out, ga_ref[...], gb_ref[...], ro_ref[...])
        mg_ref[...] = merged.astype(BF16)

    rows_w = pl.BlockSpec((tm, ATT_W), lambda i: (i, 0))
    rows_d = pl.BlockSpec((tm, D_MODEL), lambda i: (i, 0))
    return pl.pallas_call(
        kern, name="att_out", grid=(SEQ // tm,),
        in_specs=[rows_w] * 6 + [pl.BlockSpec((ATT_W, D_MODEL), lambda i: (0, 0)), rows_d,
                                 pl.BlockSpec((tm, D_MODEL), lambda i: (i, 1)), rows_d],
        out_specs=[rows_w, rows_w, rows_w, rows_d, rows_d],
        out_shape=[jax.ShapeDtypeStruct((SEQ, ATT_W), F32), jax.ShapeDtypeStruct((SEQ, ATT_W), BF16),
                   jax.ShapeDtypeStruct((SEQ, ATT_W), F32), jax.ShapeDtypeStruct((SEQ, D_MODEL), F32),
                   jax.ShapeDtypeStruct((SEQ, D_MODEL), BF16)],
        compiler_params=_cparams(("parallel",)),
    )(*os_, *ls_, w_att_out, gates, gates, ret_out)


def _merge_operands(gates, ret_out, att_out=None):
    ops = [(gates, (MERGE_TM, D_MODEL), lambda i, j, kk: (i, 0)),
           (gates, (MERGE_TM, D_MODEL), lambda i, j, kk: (i, 1)),
           (ret_out, (MERGE_TM, D_MODEL), lambda i, j, kk: (i, 0))]
    if att_out is not None:
        ops.append((att_out, (MERGE_TM, D_MODEL), lambda i, j, kk: (i, 0)))
    return ops


def _merge_fwd_epi(att_out, ga, gb, ret_out):
    return att_out, _sigmoid(ga.astype(F32)) * ret_out + _sigmoid(gb.astype(F32)) * att_out


def _merge_bwd_epi(d_merged, ga, gb, ret_out, att_out):
    sa, sb = _sigmoid(ga.astype(F32)), _sigmoid(gb.astype(F32))
    return (d_merged * sa, d_merged * sb, d_merged * ret_out * (sa * (1.0 - sa)),
            d_merged * att_out * (sb * (1.0 - sb)))


def _att_out_bwd_epi(d_att, att):
    outs = []
    for h in range(ATT_HPG):
        sl = slice(h * ATT_DH, (h + 1) * ATT_DH)
        outs.append(jnp.broadcast_to(jnp.sum(d_att[:, sl] * att[:, sl], axis=-1, keepdims=True),
                                     (d_att.shape[0], ATT_DH)))
    return d_att, jnp.concatenate(outs, axis=-1)


def _loss_head_epi(branch, x_prev, target, gate, gain):
    x3 = x_prev + gate * branch
    xhat, r = _rms_parts(x3)
    err = xhat * gain - target
    d_y = err / D_MODEL
    loss = 0.5 * jnp.sum(jnp.mean(err * err, axis=-1, keepdims=True), axis=0, keepdims=True)
    d_x = _rms_bwd(d_y * gain, xhat, r)
    return (d_x, d_x * gate, jnp.broadcast_to(loss, (1, D_MODEL)),
            jnp.sum(d_y * xhat, axis=0, keepdims=True), jnp.sum(d_x * branch, axis=0, keepdims=True))


def _local_step(pos, x, target, mod, norm1_g, norm2_g, norm_f_g, rel_bias, gn_g, gn_b, w_in, rest_gather):
    sh1, sc1, g1, sh2, sc2, g2 = [mod[:, i * D_MODEL:(i + 1) * D_MODEL] for i in range(6)]
    cos, sin = _rope_tables()
    din, qd, kd, cd = _decay_tables()
    buckets = _bucket_tables()
    bias = _bias_build(rel_bias, buckets)
    dils = [d for _, d in ATT_GROUPS]

    h1 = _prenorm_fwd("prenorm1_fwd", x, norm1_g, sh1, sc1)

    qk_tn = 2 * RET_DK

    def rot_epi(acc, cs, sn, scale):
        half = RET_DK // 2
        outs = []
        for h0 in range(0, qk_tn, RET_DK):
            x1, x2 = acc[:, h0:h0 + half], acc[:, h0 + half:h0 + RET_DK]
            outs += [x1 * cs - x2 * sn, x1 * sn + x2 * cs]
        return (jnp.concatenate(outs, axis=-1) * scale,)

    qk_scale = jnp.concatenate([jnp.ones((1, RET_QK_W), F32),
                                jnp.full((1, RET_QK_W), RET_DK ** -0.5, F32)], axis=-1)
    rope_ex = [(cos, (TM, RET_DK // 2), lambda i, j, kk: (i, 0)),
               (sin, (TM, RET_DK // 2), lambda i, j, kk: (i, 0)),
               (qk_scale, (1, qk_tn), lambda i, j, kk: (0, j))]
    rest_sems, rest_shards, rest_fulls, rest_token = rest_gather
    behind = [rest_token]
    rv = _matmul("proj_rv", h1, w_in, "nn", SEQ, RET_V_W, D_MODEL, [BF16], b_off=OFF_V, tk=D_MODEL,
                 after=behind)[0]
    rg = _matmul("proj_rg", h1, w_in, "nn", SEQ, RET_V_W, D_MODEL, [BF16], b_off=OFF_G, tk=D_MODEL,
                 after=behind)[0]
    gates = _matmul("proj_gates", h1, w_in, "nn", SEQ, 2 * D_MODEL, D_MODEL, [BF16], b_off=OFF_GATE,
                    tn=512, tk=D_MODEL, after=behind)[0]
    aqkv = _matmul("proj_att", h1, w_in, "nn", SEQ, 9 * ATT_W, D_MODEL, [BF16], b_off=OFF_ATT,
                   tn=512, tk=D_MODEL, after=behind)[0]

    rqk = _matmul("proj_qk", h1, w_in, "nn", SEQ, 2 * RET_QK_W, D_MODEL, [BF16], b_off=OFF_Q,
                  tn=qk_tn, tk=D_MODEL, epilogue=rot_epi, extras=rope_ex, after=behind)[0]
    ro, states, gated = _retention_fwd(rqk, rv, rg, gn_g, gn_b, din, qd, kd, cd)
    os_, ls_ = [], []
    for gi in range(3):
        if gi == 2:
            rest_sems, rest_fulls, fwd_token = _gather_rest_forward(
                rest_sems, rest_shards, rest_fulls, [gated, gates] + os_)
        o_g, l_g = _att_fwd(gi, dils[gi], aqkv, bias, after=[fwd_token] if gi == 2 else ())
        os_.append(o_g)
        ls_.append(l_g)
    w_ret_out, w_att_out, w_o, w_ff1, w_ff2 = _gather_rest_end(rest_sems, rest_fulls, [os_[2]])
    ret_out = _matmul("ret_out", gated, w_ret_out, "nn", SEQ, D_MODEL, RET_V_W, [F32], tk=RET_V_W)[0]
    att, att_b, lse, att_out, merged = _att_out(os_, ls_, w_att_out, gates, ret_out)

    def mix_epi(acc, xt, g, gain, sh, sc):
        x_new = xt + g * acc
        xhat, _ = _rms_parts(x_new)
        return x_new, acc, (xhat * gain) * (1.0 + sc) + sh

    x2, mix, h2 = _matmul("mix_out", merged, w_o, "nn", SEQ, D_MODEL, D_MODEL, [F32, BF16, BF16],
                          epilogue=mix_epi, extras=_row_operands(TM, [x], [g1, norm2_g, sh2, sc2]))

    def relu2_epi(acc):
        r = jnp.maximum(acc, 0.0)
        return r * r, r

    act, relu_u = _matmul("ff1", h2, w_ff1, "nn", SEQ, D_FF, D_MODEL, [BF16, BF16], tk=D_MODEL,
                          epilogue=relu2_epi)
    d_x3, d_y2, loss, d_gf, d_g2 = _matmul(
        "ff2", act, w_ff2, "nn", SEQ, D_MODEL, D_FF, [F32, BF16], tm=TM, tk=1024, n_sums=3,
        epilogue=_loss_head_epi, extras=_row_operands(TM, [x2, target], [g2, norm_f_g]))

    def relu2_bwd_epi(acc, rt):
        return (acc * (2.0 * rt.astype(F32)),)

    gw_ff2 = _matmul_tn_pair("ff2_dw", pos, act, d_y2, D_FF, D_MODEL, SEQ, D_FF // N_CHIPS,
                             tm=512, tn=1024, tk=SEQ)
    d_u = _matmul("ff2_dx", d_y2, w_ff2, "nt", SEQ, D_FF, D_MODEL, [BF16], epilogue=relu2_bwd_epi,
                  extras=[(relu_u, (TM, TN), lambda i, j, kk: (i, j))])[0]
    gw_ff1 = _matmul_tn_pair("ff1_dw", pos, h2, d_u, D_MODEL, D_FF, SEQ, D_MODEL,
                             tm=512, tn=1024, tk=SEQ)
    ffn = ["w_ff2", "w_ff1"]
    ffn_started = _ici_start("ici_start_ffn", ffn, [gw_ff2, gw_ff1])
    d_x2, d_mix, d_sh2, d_sc2, d_n2g, d_g1 = _matmul(
        "ff1_dx", d_u, w_ff1, "nt", SEQ, D_MODEL, D_FF, [F32, BF16], tm=TM, tk=1024, n_sums=4,
        epilogue=_prenorm_bwd_epi, extras=_row_operands(TM, [x2, d_x3], [norm2_g, sc2])
        + _row_operands(TM, [mix], [g1]), after=[ffn_started[3]])
    gw_o = _matmul_tn_pair("mix_dw", pos, merged, d_mix, D_MODEL, D_MODEL, SEQ, D_MODEL // N_CHIPS,
                           tm=128, tn=1024, tk=2048)
    d_ret_out, d_att_out, d_ga, d_gb = _matmul(
        "mix_dx", d_mix, w_o, "nt", SEQ, D_MODEL, D_MODEL, [BF16] * 4, tm=MERGE_TM,
        epilogue=_merge_bwd_epi, extras=_merge_operands(gates, ret_out, att_out))

    gw_ret_out = _matmul_tn_pair("ret_out_dw", pos, gated, d_ret_out, RET_V_W, D_MODEL, SEQ,
                                 RET_V_W // N_CHIPS, tm=256, tn=1024, tk=SEQ)
    gw_att_out = _matmul_tn_pair("att_out_dw", pos, att_b, d_att_out, ATT_W, D_MODEL, SEQ, ATT_W,
                                 tm=256, tn=1024, tk=2048)
    mixer = ["w_o", "w_ret_out", "w_att_out"]
    mixer_started = _ici_start("ici_start_mixer", mixer, [gw_o, gw_ret_out, gw_att_out])
    d_gated = _matmul("ret_out_dx", d_ret_out, w_ret_out, "nt", SEQ, RET_V_W, D_MODEL, [BF16],
                      after=[mixer_started[3]])[0]
    d_att, dd = _matmul("att_out_dx", d_att_out, w_att_out, "nt", SEQ, ATT_W, D_MODEL, [F32, F32],
                        epilogue=_att_out_bwd_epi,
                        extras=[(att, (TM, ATT_W), lambda i, j, kk: (i, 0))], after=[mixer_started[3]])

    d_rqkv, d_rg, d_gn_g, d_gn_b = _retention_bwd(rqk, rv, states, d_gated, ro, rg, gn_g, gn_b,
                                                  din, qd, kd, cd, cos, sin)

    d_aqkv, dsbs = [], []
    for gi in range(3):
        dqkv, dsb = _att_bwd(gi, dils[gi], aqkv, d_att, lse, dd, bias)
        d_aqkv.append(dqkv)
        dsbs.append(dsb)
    d_rel_bias = _bias_grad(jnp.stack(dsbs), buckets)

    d_proj = ([(d_rqkv, False), (d_rg, False)] + [(t, True) for t in d_aqkv]
              + [(d_ga, False), (d_gb, False)])
    gw_in = _matmul_tn_pair("proj_dw", pos, h1, d_proj, D_MODEL, IN_COLS, SEQ, D_MODEL,
                            tm=512, tn=ATT_W, tk=SEQ)
    sems, (gw_in,), (land,), token = _ici_start("ici_start_w_in", ["w_in"], [gw_in])
    grad_x, d_sh1, d_sc1, d_n1g = _matmul(
        "proj_dx", d_proj, w_in, "nt", SEQ, D_MODEL, IN_COLS, [F32], tn=1024, tk=ATT_W, n_sums=3,
        epilogue=_prenorm_bwd_epi, extras=_row_operands(TM, [x, d_x2], [norm1_g, sc1]), after=[token])
    pending = (sems, land)

    names = ffn + mixer
    psums, got = _ici_wait("ici_wait_rest", names, list(ffn_started[0]) + list(mixer_started[0]),
                           list(ffn_started[1]) + list(mixer_started[1]),
                           list(ffn_started[2]) + list(mixer_started[2]), [grad_x])
    g_big = {n: _final_sum("final_" + n, pos, dict(BIG)[n], psums[i], got[i], SHARD[n])
             for i, n in enumerate(names)}
    d_mod = jnp.concatenate([d_sh1, d_sc1, d_g1, d_sh2, d_sc2, d_g2], axis=-1)
    small = dict(norm1_g=d_n1g, norm2_g=d_n2g, norm_f_g=d_gf, gn_g=d_gn_g, gn_b=d_gn_b,
                 rel_bias=d_rel_bias)
    return loss, grad_x, d_mod, small, g_big, (gw_in,) + pending


def _me():
    return lax.axis_index("x"), lax.axis_index("y"), lax.axis_index("c")


def _peer(x, y, c, mask):
    return (x ^ ((mask >> 2) & 1), y ^ ((mask >> 1) & 1), c ^ (mask & 1))


def _gather8(src_ref, dst_ref, send_sems, recv_sems):
    x, y, c = _me()
    me = 4 * x + 2 * y + c
    copies = []
    for mask in range(1, N_DEV):
        cp = pltpu.make_async_remote_copy(
            src_ref=src_ref, dst_ref=dst_ref.at[me], send_sem=send_sems.at[mask - 1],
            recv_sem=recv_sems.at[mask - 1], device_id=_peer(x, y, c, mask), device_id_type=MESH)
        cp.start()
        copies.append(cp)
    dst_ref[me] = src_ref[...]
    for cp in copies:
        cp.wait_recv()
    for cp in copies:
        cp.wait_send()


def _ada_fwd(c_in, w_ada, b_ada):
    ncol = ADA_COLS // N_CHIPS

    def body(c_ref, w_ref, b_ref, mod_ref, sc_ref, cbuf, cg, mbuf, mg, s1, r1, s2, r2):
        x, y, c = _me()
        me = 4 * x + 2 * y + c
        cv = c_ref[...]
        cbuf[...] = jnp.broadcast_to(cv * _sigmoid(cv), cbuf.shape)
        _gather8(cbuf, cg, s1, r1)
        rows = lax.broadcasted_iota(I32, (N_DEV, D_MODEL), 0)
        sc_all = jnp.zeros((N_DEV, D_MODEL), F32)
        for d in range(N_DEV):
            sc_all = jnp.where(rows == d, cg[d], sc_all)
        sc_ref[...] = sc_all
        mbuf[...] = jnp.dot(sc_all.astype(BF16), w_ref[...].astype(BF16), preferred_element_type=F32)
        _gather8(mbuf, mg, s2, r2)
        rowsel = lax.broadcasted_iota(I32, (N_DEV, ncol), 0) == me
        for k in range(N_CHIPS):
            blk = mg[2 * k]
            row = jnp.sum(jnp.where(rowsel, blk, 0.0), axis=0, keepdims=True)
            mod_ref[:, k * ncol:(k + 1) * ncol] = row + b_ref[:, k * ncol:(k + 1) * ncol]

    vm = pl.BlockSpec(memory_space=pltpu.VMEM)
    return pl.pallas_call(
        body, name="ada_fwd",
        in_specs=[vm, vm, vm], out_specs=[vm, vm],
        out_shape=[jax.ShapeDtypeStruct((1, ADA_COLS), F32), jax.ShapeDtypeStruct((N_DEV, D_MODEL), F32)],
        scratch_shapes=[
            pltpu.VMEM((8, D_MODEL), F32), pltpu.VMEM((N_DEV, 8, D_MODEL), F32),
            pltpu.VMEM((8, ncol), F32), pltpu.VMEM((N_DEV, 8, ncol), F32),
            pltpu.SemaphoreType.DMA((N_DEV - 1,)), pltpu.SemaphoreType.DMA((N_DEV - 1,)),
            pltpu.SemaphoreType.DMA((N_DEV - 1,)), pltpu.SemaphoreType.DMA((N_DEV - 1,)),
        ],
        compiler_params=pltpu.CompilerParams(vmem_limit_bytes=VMEM_LIMIT_V7X),
    )(c_in, w_ada, b_ada)


def _small_reduce(pack, sc_all, after=()):
    ncol = ADA_COLS // N_CHIPS

    def body(p_ref, sc_ref, *rest):
        tot_ref, gw_ref, pg, s1, r1 = rest[len(after):]
        x, y, _ = _me()
        chip = 2 * x + y
        _gather8(p_ref, pg, s1, r1)
        tot = pg[0]
        for d in range(1, N_DEV):
            tot = tot + pg[d]
        tot_ref[...] = tot
        rows = lax.broadcasted_iota(I32, (N_DEV, ncol), 0)
        dmod = jnp.zeros((N_DEV, ncol), F32)
        for k in range(N_CHIPS):
            part = jnp.zeros((N_DEV, ncol), F32)
            for d in range(N_DEV):
                part = jnp.where(rows == d, pg[d, :, k * ncol:(k + 1) * ncol][0:1, :], part)
            dmod = jnp.where(chip == k, part, dmod)
        gw_ref[...] = lax.dot_general(sc_ref[...].astype(BF16), dmod.astype(BF16), _TN,
                                      preferred_element_type=F32)

    vm = pl.BlockSpec(memory_space=pltpu.VMEM)
    return pl.pallas_call(
        body, name="small_reduce",
        in_specs=[vm, vm] + [pl.BlockSpec(memory_space=pl.ANY)] * len(after), out_specs=[vm, vm],
        out_shape=[jax.ShapeDtypeStruct((8, ADA_COLS), F32), jax.ShapeDtypeStruct((D_MODEL, ncol), F32)],
        scratch_shapes=[pltpu.VMEM((N_DEV, 8, ADA_COLS), F32),
                        pltpu.SemaphoreType.DMA((N_DEV - 1,)), pltpu.SemaphoreType.DMA((N_DEV - 1,))],
        compiler_params=pltpu.CompilerParams(vmem_limit_bytes=VMEM_LIMIT_V7X),
    )(pack, sc_all, *after)


BIG = (("w_in", 1), ("w_ret_out", 0), ("w_att_out", 1), ("w_o", 0), ("w_ff1", 1), ("w_ff2", 0))
SHARD = {"w_in": (D_MODEL, IN_COLS // N_CHIPS), "w_ret_out": (RET_V_W // N_CHIPS, D_MODEL),
         "w_att_out": (ATT_W, D_MODEL // N_CHIPS), "w_o": (D_MODEL // N_CHIPS, D_MODEL),
         "w_ff1": (D_MODEL, D_FF // N_CHIPS), "w_ff2": (D_FF // N_CHIPS, D_MODEL)}
_CHIP_FLIPS = ((1, 0), (0, 1), (1, 1))


def _region(ref, axis, chip, half, shard_shape):
    r, cw = shard_shape
    hr = r // 2
    if axis == 1:
        return ref.at[pl.ds(half * hr, hr), pl.ds(chip * cw, cw)]
    return ref.at[pl.ds(chip * r + half * hr, hr), :]


CAST_ROWS = 128


def _gather_weights(shards, n_remote):
    nw = len(BIG)
    shapes = [s.shape for s in shards]
    full_shapes = [(r, N_CHIPS * cw) if ax == 1 else (N_CHIPS * r, cw)
                   for (r, cw), (_, ax) in zip(shapes, BIG)]

    def body(*refs):
        ins, outs = refs[:nw], refs[nw:2 * nw]
        own = refs[2 * nw:3 * nw]
        from_ici, from_sib = refs[3 * nw:3 * nw + n_remote], refs[3 * nw + n_remote:3 * nw + 2 * n_remote]
        ld_sem, st_sem, s_ici, r_ici, s_d2d, r_d2d, st_a, st_b, stage = refs[3 * nw + 2 * n_remote:]
        x, y, c = _me()
        chip = 2 * x + y
        sib = (x, y, 1 - c)
        loads = [pltpu.make_async_copy(ins[i], stage if i == 0 else own[i], ld_sem.at[i])
                 for i in range(nw)]
        for cp in loads:
            cp.start()
        pending, first = [], []
        for i, (_, ax) in enumerate(BIG):
            r, cw = shapes[i]
            hr = r // 2
            loads[i].wait()
            if i == 0:
                for r0 in range(0, r, CAST_ROWS):
                    own[0][r0:r0 + CAST_ROWS, :] = stage[r0:r0 + CAST_ROWS, :].astype(BF16)
            dst = outs[i].at[:, pl.ds(chip * cw, cw)] if ax == 1 else outs[i].at[pl.ds(chip * r, r), :]
            cp = pltpu.make_async_copy(own[i], dst, st_sem.at[i])
            cp.start()
            pending.append(cp)
            for j, (fx, fy) in enumerate(_CHIP_FLIPS if i < n_remote else ()):
                rc = pltpu.make_async_remote_copy(
                    src_ref=own[i].at[pl.ds(c * hr, hr), :], dst_ref=from_ici[i].at[j],
                    send_sem=s_ici.at[j * nw + i], recv_sem=r_ici.at[j * nw + i],
                    device_id=(x ^ fx, y ^ fy, c), device_id_type=MESH)
                rc.start()
                first.append((j, i, rc))
        passed = []
        for j, i, rc in first:
            fx, fy = _CHIP_FLIPS[j]
            src_chip = 2 * (x ^ fx) + (y ^ fy)
            ax = BIG[i][1]
            rc.wait_recv()
            fw = pltpu.make_async_remote_copy(
                src_ref=from_ici[i].at[j], dst_ref=from_sib[i].at[j], send_sem=s_d2d.at[j * nw + i],
                recv_sem=r_d2d.at[j * nw + i], device_id=sib, device_id_type=MESH)
            fw.start()
            passed.append((j, i, src_chip, fw))
            st = pltpu.make_async_copy(from_ici[i].at[j], _region(outs[i], ax, src_chip, c, shapes[i]),
                                       st_a.at[j * nw + i])
            st.start()
            pending.append(st)
        for j, i, src_chip, fw in passed:
            fw.wait_recv()
            st = pltpu.make_async_copy(from_sib[i].at[j],
                                       _region(outs[i], BIG[i][1], src_chip, 1 - c, shapes[i]),
                                       st_b.at[j * nw + i])
            st.start()
            pending.append(st)
        for _, _, rc in first:
            rc.wait_send()
        for _, _, _, fw in passed:
            fw.wait_send()
        for cp in pending:
            cp.wait()

    hbm = pl.BlockSpec(memory_space=pl.ANY)
    halves = [pltpu.VMEM((3, r // 2, cw), BF16) for r, cw in shapes[:n_remote]]
    return pl.pallas_call(
        body, name="gather_weights",
        in_specs=[hbm] * nw, out_specs=[hbm] * nw,
        out_shape=[jax.ShapeDtypeStruct(fs, BF16) for fs in full_shapes],
        scratch_shapes=[pltpu.VMEM(sh, BF16) for sh in shapes] + halves + halves
        + [pltpu.SemaphoreType.DMA((nw,)), pltpu.SemaphoreType.DMA((nw,))]
        + [pltpu.SemaphoreType.DMA((3 * nw,))] * 6 + [pltpu.VMEM(shapes[0], F32)],
        compiler_params=pltpu.CompilerParams(vmem_limit_bytes=VMEM_LIMIT_V7X),
    )(*shards)


REST = BIG[1:]
_SIDE_EFFECTS = pltpu.CompilerParams(has_side_effects=pltpu.SideEffectType.DATAFLOW_SIDE_EFFECTING)
_ANY_SPEC = pl.BlockSpec(memory_space=pl.ANY)


def _rest_ici_copies(shard_refs, full_refs, sems):
    x, y, c = _me()
    chip = 2 * x + y
    n = 3 * len(REST)
    copies = []
    for i, (name, ax) in enumerate(REST):
        hr = SHARD[name][0] // 2
        for j, (fx, fy) in enumerate(_CHIP_FLIPS):
            copies.append(pltpu.make_async_remote_copy(
                src_ref=shard_refs[i].at[pl.ds(c * hr, hr), :],
                dst_ref=_region(full_refs[i], ax, chip, c, SHARD[name]),
                send_sem=sems[3 * i + j], recv_sem=sems[n + 3 * i + j],
                device_id=(x ^ fx, y ^ fy, c), device_id_type=MESH))
    return copies


def _rest_d2d_copies(full_refs, sems):
    x, y, c = _me()
    n = 3 * len(REST)
    copies = []
    for i, (name, ax) in enumerate(REST):
        for j, (fx, fy) in enumerate(_CHIP_FLIPS):
            reg = _region(full_refs[i], ax, 2 * (x ^ fx) + (y ^ fy), c, SHARD[name])
            copies.append(pltpu.make_async_remote_copy(
                src_ref=reg, dst_ref=reg, send_sem=sems[3 * i + j], recv_sem=sems[n + 3 * i + j],
                device_id=(x, y, 1 - c), device_id_type=MESH))
    return copies


def _gather_rest_start(shards, fulls, after):
    nr, ns, na = len(REST), 6 * len(REST), len(after)

    def body(*refs):
        for cp in _rest_ici_copies(refs[:nr], refs[nr:2 * nr], refs[2 * nr + na:2 * nr + na + ns]):
            cp.start()
        token = refs[-1]
        token[...] = jnp.zeros_like(token)

    hbm = lambda a: pltpu.HBM(a.shape, a.dtype)
    res = pl.pallas_call(
        body, name="gather_rest_start",
        out_shape=(pltpu.SemaphoreType.DMA(()),) * ns + tuple(hbm(a) for a in shards + fulls)
        + (jax.ShapeDtypeStruct((8, 128), F32),),
        in_specs=(_HBM_SPEC,) * (2 * nr) + (_ANY_SPEC,) * na,
        out_specs=(_SEM_SPEC,) * ns + (_HBM_SPEC,) * (2 * nr) + (pl.BlockSpec(memory_space=pltpu.VMEM),),
        input_output_aliases={k: ns + k for k in range(2 * nr)}, compiler_params=_SIDE_EFFECTS,
    )(*[pltpu.with_memory_space_constraint(a, pltpu.HBM) for a in shards + fulls], *after)
    return res[:ns], res[ns:ns + nr], res[ns + nr:ns + 2 * nr], res[-1]


def _gather_rest_forward(sems, shards, fulls, after):
    nr, ns = len(REST), 6 * len(REST)

    def body(*refs):
        shard_refs, full_refs, old = refs[:nr], refs[nr:2 * nr], refs[2 * nr:2 * nr + ns]
        new = refs[2 * nr + ns + len(after):2 * nr + 2 * ns + len(after)]
        for cp in _rest_ici_copies(shard_refs, full_refs, old):
            cp.wait_send()
            cp.wait_recv()
        for cp in _rest_d2d_copies(full_refs, new):
            cp.start()
        token = refs[-1]
        token[...] = jnp.zeros_like(token)

    res = pl.pallas_call(
        body, name="gather_rest_forward",
        out_shape=(pltpu.SemaphoreType.DMA(()),) * ns + tuple(pltpu.HBM(a.shape, a.dtype) for a in fulls)
        + (jax.ShapeDtypeStruct((8, 128), F32),),
        in_specs=(_HBM_SPEC,) * (2 * nr) + (_SEM_SPEC,) * ns + (_ANY_SPEC,) * len(after),
        out_specs=(_SEM_SPEC,) * ns + (_HBM_SPEC,) * nr + (pl.BlockSpec(memory_space=pltpu.VMEM),),
        input_output_aliases={nr + k: ns + k for k in range(nr)}, compiler_params=_SIDE_EFFECTS,
    )(*shards, *fulls, *sems, *after)
    return res[:ns], res[ns:ns + nr], res[-1]


def _gather_rest_end(sems, fulls, after):
    nr, ns = len(REST), 6 * len(REST)

    def body(*refs):
        for cp in _rest_d2d_copies(refs[:nr], refs[nr:nr + ns]):
            cp.wait_send()
            cp.wait_recv()

    return pl.pallas_call(
        body, name="gather_rest_end",
        out_shape=tuple(pltpu.HBM(a.shape, a.dtype) for a in fulls),
        in_specs=(_HBM_SPEC,) * nr + (_SEM_SPEC,) * ns + (_ANY_SPEC,) * len(after),
        out_specs=(_HBM_SPEC,) * nr,
        input_output_aliases={k: k for k in range(nr)}, compiler_params=_SIDE_EFFECTS,
    )(*fulls, *sems, *after)


def _adam_update(w, g, m, v):
    mn = ADAM_B1 * m + (1.0 - ADAM_B1) * g
    vn = ADAM_B2 * v + (1.0 - ADAM_B2) * (g * g)
    m_hat = mn / (1.0 - ADAM_B1 ** ADAM_STEP)
    v_hat = vn / (1.0 - ADAM_B2 ** ADAM_STEP)
    return -ADAM_LR * (m_hat / (jnp.sqrt(v_hat) + ADAM_EPS) + ADAM_WD * w), mn, vn


def _final_sum(name, pos, axis, psum, recv, shard_shape, after=(), tr=128):
    r, cw = shard_shape
    hr = r // 2
    tr = min(tr, hr)
    nt = hr // tr
    n_after = len(after)

    def kern(pos_ref, p_ref, r_ref, *rest):
        g_ref, send_buf, land_buf, s_sem, r_sem = rest[n_after:]
        p, t = pl.program_id(0), pl.program_id(1)
        sib = _sibling()

        def copy(i):
            return pltpu.make_async_remote_copy(
                src_ref=send_buf.at[i], dst_ref=land_buf.at[i], send_sem=s_sem.at[i],
                recv_sem=r_sem.at[i], device_id=sib, device_id_type=MESH)

        @pl.when(p == 0)
        def _():
            tot = p_ref[...].astype(F32)
            for j in range(3):
                tot = tot + r_ref[j].astype(F32)
            send_buf[t] = tot
            copy(t).start()
            g_ref[...] = tot

        @pl.when(p == 1)
        def _():
            copy(t).wait_recv()
            g_ref[...] = land_buf[t]

        @pl.when(jnp.logical_and(p == 1, t == nt - 1))
        def _():
            for i in range(nt):
                copy(i).wait_send()

    def shard_rows(p, t, pos_ref):
        return (jnp.where(p == 0, pos_ref[0], 1 - pos_ref[0]) * nt + t, 0)

    def own_part(p, t, pos_ref):
        tt = jnp.where(p == 0, t, nt - 1)
        return (tt, pos_ref[1]) if axis == 1 else (pos_ref[1] * nt + tt, 0)

    grid_spec = pltpu.PrefetchScalarGridSpec(
        num_scalar_prefetch=1, grid=(2, nt),
        in_specs=[pl.BlockSpec((tr, cw), own_part),
                  pl.BlockSpec((3, tr, cw), lambda p, t, pos_ref: (0, jnp.where(p == 0, t, nt - 1), 0))]
        + [pl.BlockSpec(memory_space=pl.ANY)] * n_after,
        out_specs=pl.BlockSpec((tr, cw), shard_rows),
        scratch_shapes=[pltpu.VMEM((nt, tr, cw), F32), pltpu.VMEM((nt, tr, cw), F32),
                        pltpu.SemaphoreType.DMA((nt,)), pltpu.SemaphoreType.DMA((nt,))])
    return pl.pallas_call(
        kern, name=name, grid_spec=grid_spec, out_shape=jax.ShapeDtypeStruct((r, cw), F32),
        compiler_params=_cparams(("arbitrary", "arbitrary")),
    )(pos, psum, recv, *after)


def _adamw(name, w, g, m, v):
    r, cw = w.shape
    tr = min(r, 128)

    def kern(w_ref, g_ref, m_ref, v_ref, go_ref, d_ref, nm_ref, nv_ref):
        gv = g_ref[...]
        go_ref[...] = gv
        d_ref[...], nm_ref[...], nv_ref[...] = _adam_update(w_ref[...], gv, m_ref[...], v_ref[...])

    spec = pl.BlockSpec((tr, cw), lambda i: (i, 0))
    return pl.pallas_call(
        kern, name=name, grid=(r // tr,), in_specs=[spec] * 4, out_specs=[spec] * 4,
        out_shape=[jax.ShapeDtypeStruct((r, cw), F32)] * 4, compiler_params=_cparams(("parallel",)),
    )(w, g, m, v)


_PACK_W = ADA_COLS
_NB = REL_BUCKETS * N_ATT_HEADS
_SMALL_SLOTS = {
    "b_ada": (0, 0, ADA_COLS),
    "norm1_g": (1, 0, D_MODEL), "norm2_g": (1, D_MODEL, D_MODEL), "norm_f_g": (1, 2 * D_MODEL, D_MODEL),
    "ret_gn_g": (1, 3 * D_MODEL, RET_V_W),
    "ret_gn_b": (2, 0, RET_V_W), "rel_bias": (2, RET_V_W, _NB), "loss": (2, RET_V_W + 512, 128),
}


def _pack_small(vals):
    rows = []
    for r in range(8):
        items = sorted([(off, n) for n, (rr, off, _) in _SMALL_SLOTS.items() if rr == r and n in vals])
        parts, pos = [], 0
        for off, n in items:
            if off > pos:
                parts.append(jnp.zeros((1, off - pos), F32))
            parts.append(vals[n].reshape(1, -1).astype(F32))
            pos = off + _SMALL_SLOTS[n][2]
        if pos < _PACK_W:
            parts.append(jnp.zeros((1, _PACK_W - pos), F32))
        rows.append(jnp.concatenate(parts, axis=-1))
    return jnp.concatenate(rows, axis=0)


def _adamw_small(tot, names, wmv):
    n = len(names)

    def kern(tot_ref, *refs):
        ins, outs = refs[:3 * n], refs[3 * n:]
        for i, name in enumerate(names):
            row, off, width = _SMALL_SLOTS[name]
            g = tot_ref[row:row + 1, off:off + width]
            outs[i][...] = g
            outs[n + i][...], outs[2 * n + i][...], outs[3 * n + i][...] = _adam_update(
                ins[i][...], g, ins[n + i][...], ins[2 * n + i][...])

    vm = pl.BlockSpec(memory_space=pltpu.VMEM)
    shapes = [jax.ShapeDtypeStruct((1, _SMALL_SLOTS[name][2]), F32) for name in names]
    res = pl.pallas_call(
        kern, name="adamw_small", in_specs=[vm] * (1 + 3 * n), out_specs=[vm] * (4 * n),
        out_shape=shapes * 4,
    )(tot, *wmv[0], *wmv[1], *wmv[2])
    return res[:n], res[n:2 * n], res[2 * n:3 * n], res[3 * n:]


def _unpack_small(pack, name):
    r, off, wd = _SMALL_SLOTS[name]
    return pack[r:r + 1, off:off + wd]


def kernel(x, c, w_ada, b_ada, norm1_g, w_in, rel_bias, ret_gn_g, ret_gn_b, w_ret_out, w_att_out, w_o, norm2_g, w_ff1, w_ff2, norm_f_g, loss_target, m_w_ada, m_b_ada, m_norm1_g, m_w_in, m_rel_bias, m_ret_gn_g, m_ret_gn_b, m_w_ret_out, m_w_att_out, m_w_o, m_norm2_g, m_w_ff1, m_w_ff2, m_norm_f_g, v_w_ada, v_b_ada, v_norm1_g, v_w_in, v_rel_bias, v_ret_gn_g, v_ret_gn_b, v_w_ret_out, v_w_att_out, v_w_o, v_norm2_g, v_w_ff1, v_w_ff2, v_norm_f_g):
    given = dict(locals())
    big_names = [n for n, _ in BIG]
    shard_w = {n: given[n][0] for n in big_names}
    assert all(shard_w[n].shape == SHARD[n] for n in big_names)

    shards_bf = [None] + [shard_w[n].astype(BF16) for n in big_names[1:]]
    full = _gather_weights([shard_w["w_in"]] + shards_bf[1:], 1)
    mod, sc_all = _ada_fwd(c, w_ada[0], b_ada)
    rest_gather = _gather_rest_start(shards_bf[1:], list(full[1:]), [mod])
    pos = _where_am_i()

    loss, grad_x, d_mod, small, g_big, pending = _local_step(
        pos, x[0], loss_target[0], mod, norm1_g, norm2_g, norm_f_g.reshape(1, -1), rel_bias, ret_gn_g,
        ret_gn_b, full[0], rest_gather)

    pack_g = _pack_small(dict(b_ada=d_mod, norm1_g=small["norm1_g"], norm2_g=small["norm2_g"],
                              norm_f_g=small["norm_f_g"], ret_gn_g=small["gn_g"], ret_gn_b=small["gn_b"],
                              rel_bias=small["rel_bias"], loss=loss[:, :128]))
    tot, g_w_ada = _small_reduce(pack_g, sc_all, after=list(g_big.values()))

    small_names = ["b_ada", "norm1_g", "rel_bias", "ret_gn_g", "ret_gn_b", "norm2_g", "norm_f_g"]
    small_out = _adamw_small(tot, small_names, [[given[p + n].reshape(1, -1) for n in small_names]
                                                for p in ("", "m_", "v_")])
    grads, deltas, new_m, new_v = ({n: t.reshape(given[n].shape) for n, t in zip(small_names, group)}
                                   for group in small_out)
    sd = deltas["b_ada"]
    g_big["w_ada"] = g_w_ada
    for n in ["w_ada"] + big_names[1:] + big_names[:1]:
        if n == "w_in":
            gw_in, sems, land = pending
            done = [tot, sd] + [deltas[k] for k in ["w_ada"] + big_names[1:]]
            (gw_in,), (got,) = _ici_wait("ici_wait_w_in", [n], sems, [gw_in], [land], done)
            g_big[n] = _final_sum("final_w_in", pos, 1, gw_in, got, SHARD[n])
        g, d, nm, nv = _adamw("adamw_" + n, given[n][0], g_big[n], given["m_" + n][0], given["v_" + n][0])
        grads[n], deltas[n], new_m[n], new_v[n] = g[None], d[None], nm[None], nv[None]

    order = ["w_ada", "b_ada", "norm1_g", "w_in", "rel_bias", "ret_gn_g", "ret_gn_b", "w_ret_out",
             "w_att_out", "w_o", "norm2_g", "w_ff1", "w_ff2", "norm_f_g"]
    loss_out = _unpack_small(tot, "loss")[0, 0]
    return (loss_out, grad_x[None], *[grads[n] for n in order], *[deltas[n] for n in order],
            *[new_m[n] for n in order], *[new_v[n] for n in order])
```

```python
import math

import jax
import jax.numpy as jnp
import numpy as np
from jax import lax
from jax.experimental import pallas as pl
from jax.experimental.pallas import tpu as pltpu

F32 = jnp.float32
BF16 = jnp.bfloat16
I32 = jnp.int32

SEQ = 2048
D_MODEL = 1024
RET_HEADS = 4
RET_DK = 256
RET_DV = 512
RET_CHUNK = 128
RET_SUB = 2
RET_QK_W = RET_HEADS * RET_DK
RET_V_W = RET_HEADS * RET_DV
ATT_GROUPS = ((128, 1), (512, 4), (2048, 16))
ATT_HPG = 4
ATT_DH = 128
ATT_W = ATT_HPG * ATT_DH
ATT_BLK = 128
REL_BUCKETS = 32
REL_MAX_DIST = 2048
N_ATT_HEADS = 12
D_FF = 4 * D_MODEL
RMS_EPS = 1e-6
GN_EPS = 1e-5
ROPE_BASE = 10000.0
IN_COLS = 2 * RET_QK_W + 2 * RET_V_W + 9 * ATT_W + 2 * D_MODEL
OFF_Q, OFF_K, OFF_V, OFF_G = 0, RET_QK_W, 2 * RET_QK_W, 2 * RET_QK_W + RET_V_W
OFF_ATT = 2 * RET_QK_W + 2 * RET_V_W
OFF_GATE = OFF_ATT + 9 * ATT_W
N_CHIPS = 4
N_DEV = 8
ADA_COLS = 6 * D_MODEL

ADAM_LR = 0.001
ADAM_B1 = 0.9
ADAM_B2 = 0.999
ADAM_EPS = 1e-08
ADAM_WD = 0.01
ADAM_STEP = 10

VMEM_LIMIT_V7X = 56 * 1024 * 1024
MESH = pl.DeviceIdType.MESH


def _cparams(sem):
    return pltpu.CompilerParams(dimension_semantics=sem, vmem_limit_bytes=VMEM_LIMIT_V7X)


def _sigmoid(v):
    return 1.0 / (1.0 + jnp.exp(-v))


TM, TN = 1024, 1024


def _piece_chunks(piece, width):
    arr, stacked = piece
    return arr.shape[0] if stacked else arr.shape[1] // width


def _piece_spec(piece, rows, width, start, row_of, chunk_of):
    arr, stacked = piece
    last = _piece_chunks(piece, width) - 1

    def local(*ids):
        return jnp.clip(chunk_of(*ids) - start, 0, last)

    def row(*ids):
        rel = chunk_of(*ids) - start
        return jnp.where(jnp.logical_and(rel >= 0, rel <= last), row_of(*ids), 0)

    if stacked:
        return pl.BlockSpec((None, rows, width), lambda *ids: (local(*ids), row(*ids), 0))
    return pl.BlockSpec((rows, width), lambda *ids: (row(*ids), local(*ids)))


def _piece_starts(pieces, width):
    return [sum(_piece_chunks(p, width) for p in pieces[:q]) for q in range(len(pieces))]


def _matmul(name, a, b, kind, m, n, k, outs, *, b_off=0, tm=TM, tn=TN, tk=1024,
            epilogue=None, extras=(), after=(), n_sums=0):
    tm, tn, tk = min(tm, m), min(tn, n), min(tk, k)
    nk = k // tk
    pieces = a if isinstance(a, list) else [(a, False)]
    starts = _piece_starts(pieces, tk)
    if kind == "nn":
        a_specs = [pl.BlockSpec((tm, tk), lambda i, j, kk: (i, kk))]
        b_spec = pl.BlockSpec((tk, tn), lambda i, j, kk: (kk, b_off // tn + j))
        dn = (((1,), (0,)), ((), ()))
    elif kind == "nt":
        a_specs = [_piece_spec(p, tm, tk, st, lambda i, j, kk: i, lambda i, j, kk: kk)
                   for p, st in zip(pieces, starts)]
        b_spec = pl.BlockSpec((tn, tk), lambda i, j, kk: (j, b_off // tk + kk))
        dn = (((1,), (1,)), ((), ()))
    else:
        a_specs = [pl.BlockSpec((tk, tm), lambda i, j, kk: (kk, i))]
        b_spec = pl.BlockSpec((tk, tn), lambda i, j, kk: (kk, j))
        dn = (((0,), (0,)), ((), ()))
    n_a, n_ex, n_out = len(pieces), len(extras), len(outs)
    if epilogue is None:
        epilogue = lambda acc: (acc,)

    assert n_sums == 0 or tn == n

    def finish(acc, ex_refs, out_refs, first_rows):
        res = epilogue(acc, *[r[...] for r in ex_refs])
        for r, v in zip(out_refs[:n_out], res[:n_out]):
            r[...] = v.astype(r.dtype)
        for r, v in zip(out_refs[n_out:], res[n_out:]):
            @pl.when(first_rows)
            def _(r=r, v=v):
                r[...] = v

            @pl.when(jnp.logical_not(first_rows))
            def _(r=r, v=v):
                r[...] += v

    n_in = n_a + 1 + n_ex + len(after)

    def kern(*refs):
        a_refs, b_ref = refs[:n_a], refs[n_a]
        ex_refs = refs[n_a + 1:n_a + 1 + n_ex]
        out_refs = refs[n_in:n_in + n_out + n_sums]
        first_rows, kk = pl.program_id(0) == 0, pl.program_id(2)
        dot = lambda a_ref: lax.dot_general(a_ref[...], b_ref[...], dn, preferred_element_type=F32)
        if nk == 1:
            finish(dot(a_refs[0]), ex_refs, out_refs, first_rows)
            return
        acc_ref = refs[n_in + n_out + n_sums]
        if n_a == 1:
            part = dot(a_refs[0])

            @pl.when(kk == 0)
            def _():
                acc_ref[...] = part

            @pl.when(kk > 0)
            def _():
                acc_ref[...] += part
        else:
            @pl.when(kk == 0)
            def _():
                acc_ref[...] = jnp.zeros_like(acc_ref)

            for q in range(n_a):
                @pl.when(jnp.logical_and(kk >= starts[q], kk < starts[q] + _piece_chunks(pieces[q], tk)))
                def _(q=q):
                    acc_ref[...] += dot(a_refs[q])

        @pl.when(kk == nk - 1)
        def _():
            finish(acc_ref[...], ex_refs, out_refs, first_rows)

    in_specs = a_specs + [b_spec] + [pl.BlockSpec(bs, im) for _, bs, im in extras]
    in_specs += [pl.BlockSpec(memory_space=pl.ANY)] * len(after)
    sem = ("arbitrary",) * 3 if n_sums else ("parallel", "parallel", "arbitrary")
    return pl.pallas_call(
        kern, name=name, grid=(m // tm, n // tn, nk), in_specs=in_specs,
        out_specs=[pl.BlockSpec((tm, tn), lambda i, j, kk: (i, j)) for _ in outs]
        + [pl.BlockSpec((1, tn), lambda i, j, kk: (0, 0))] * n_sums,
        out_shape=[jax.ShapeDtypeStruct((m, n), dt) for dt in outs]
        + [jax.ShapeDtypeStruct((1, n), F32)] * n_sums,
        scratch_shapes=[] if nk == 1 else [pltpu.VMEM((tm, tn), F32)],
        compiler_params=_cparams(sem),
    )(*[p[0] for p in pieces], b, *[e[0] for e in extras], *after)


def _ici_copies(psum_ref, recv_ref, s_sem, r_sem, axis, shard_shape):
    x, y, c = _me()
    hr, cw = shard_shape[0] // 2, shard_shape[1]
    pick = lambda sems, j: sems[j] if isinstance(sems, (list, tuple)) else sems.at[j]
    copies = []
    for j, (fx, fy) in enumerate(_CHIP_FLIPS):
        chip = 2 * (x ^ fx) + (y ^ fy)
        src = psum_ref.at[:, pl.ds(chip * cw, cw)] if axis == 1 else psum_ref.at[pl.ds(chip * hr, hr), :]
        copies.append(pltpu.make_async_remote_copy(
            src_ref=src, dst_ref=recv_ref.at[j], send_sem=pick(s_sem, j), recv_sem=pick(r_sem, j),
            device_id=(x ^ fx, y ^ fy, c), device_id_type=MESH))
    return copies


_HBM_SPEC = pl.BlockSpec(memory_space=pltpu.HBM)
_SEM_SPEC = pl.BlockSpec(memory_space=pltpu.SEMAPHORE)


def _split_ici_copies(names, p_refs, land_refs, sems):
    copies = []
    for i, n in enumerate(names):
        copies += _ici_copies(p_refs[i], land_refs[i], list(sems[6 * i:6 * i + 3]),
                              list(sems[6 * i + 3:6 * i + 6]), dict(BIG)[n], SHARD[n])
    return copies


def _ici_start(name, names, psums):
    nw, ns = len(names), 6 * len(names)
    lands = [lax.empty((3, SHARD[n][0] // 2, SHARD[n][1]), BF16) for n in names]

    def body(*refs):
        for cp in _split_ici_copies(names, refs[:nw], refs[nw:2 * nw], refs[2 * nw:2 * nw + ns]):
            cp.start()
        token = refs[-1]
        token[...] = jnp.zeros_like(token)

    res = pl.pallas_call(
        body, name=name,
        out_shape=(pltpu.SemaphoreType.DMA(()),) * ns
        + tuple(pltpu.HBM(a.shape, BF16) for a in list(psums) + lands)
        + (jax.ShapeDtypeStruct((8, 128), F32),),
        in_specs=(_HBM_SPEC,) * (2 * nw),
        out_specs=(_SEM_SPEC,) * ns + (_HBM_SPEC,) * (2 * nw) + (pl.BlockSpec(memory_space=pltpu.VMEM),),
        input_output_aliases={k: ns + k for k in range(2 * nw)},
        compiler_params=pltpu.CompilerParams(has_side_effects=pltpu.SideEffectType.DATAFLOW_SIDE_EFFECTING),
    )(*[pltpu.with_memory_space_constraint(a, pltpu.HBM) for a in list(psums) + lands])
    return res[:ns], res[ns:ns + nw], res[ns + nw:ns + 2 * nw], res[-1]


def _ici_wait(name, names, sems, p_thru, land_thru, after):
    nw, ns = len(names), 6 * len(names)

    def body(*refs):
        for cp in _split_ici_copies(names, refs[:nw], refs[nw:2 * nw], refs[2 * nw:2 * nw + ns]):
            cp.wait_send()
            cp.wait_recv()

    res = pl.pallas_call(
        body, name=name,
        out_shape=tuple(pltpu.HBM(a.shape, BF16) for a in list(p_thru) + list(land_thru)),
        in_specs=(_HBM_SPEC,) * (2 * nw) + (_SEM_SPEC,) * ns + (pl.BlockSpec(memory_space=pl.ANY),) * len(after),
        out_specs=(_HBM_SPEC,) * (2 * nw), input_output_aliases={k: k for k in range(2 * nw)},
        compiler_params=pltpu.CompilerParams(has_side_effects=pltpu.SideEffectType.DATAFLOW_SIDE_EFFECTING),
    )(*p_thru, *land_thru, *sems, *after)
    return res[:nw], res[nw:]


def _where_am_i():
    x, y, c = _me()
    return jnp.stack([c, 2 * x + y]).astype(I32)


def _sibling():
    x, y, c = _me()
    return (x, y, 1 - c)


N_SEND_SLOTS = 2


def _matmul_tn_pair(name, pos, a, b, m, n, k, shard_rows, *, tm, tn, tk):
    hr = shard_rows // 2
    tm, tn, tk = min(tm, hr), min(tn, n), min(tk, k)
    tph = hr // tm
    nt, nj, nk = (m // 2) // tm, n // tn, k // tk
    n_tiles = nt * nj

    def row_block(p, t, pos_ref):
        half = jnp.where(p == 0, 1 - pos_ref[0], pos_ref[0])
        return (t // tph) * (2 * tph) + half * tph + t % tph

    pieces = b if isinstance(b, list) else [(b, False)]
    starts = _piece_starts(pieces, tn)
    n_b = len(pieces)

    def kern(pos_ref, a_ref, *rest):
        b_refs = rest[:n_b]
        o_ref, acc_ref, send_buf, land_buf, s_sem, r_sem = rest[n_b:]
        p, t, j, kk = pl.program_id(0), pl.program_id(1), pl.program_id(2), pl.program_id(3)
        idx = t * nj + j
        sib = _sibling()

        def copy(i):
            return pltpu.make_async_remote_copy(
                src_ref=send_buf.at[i % N_SEND_SLOTS], dst_ref=land_buf.at[i], send_sem=s_sem.at[i],
                recv_sem=r_sem.at[i], device_id=sib, device_id_type=MESH)

        @pl.when(kk == 0)
        def _():
            acc_ref[...] = jnp.zeros_like(acc_ref)

        for q in range(n_b):
            @pl.when(jnp.logical_and(j >= starts[q], j < starts[q] + _piece_chunks(pieces[q], tn)))
            def _(q=q):
                acc_ref[...] += lax.dot_general(a_ref[...], b_refs[q][...], _TN, preferred_element_type=F32)

        @pl.when(jnp.logical_and(kk == nk - 1, p == 0))
        def _():
            @pl.when(idx >= N_SEND_SLOTS)
            def _():
                copy(idx - N_SEND_SLOTS).wait_send()

            send_buf[idx % N_SEND_SLOTS] = acc_ref[...].astype(BF16)
            copy(idx).start()

        @pl.when(jnp.logical_and(kk == nk - 1, p == 1))
        def _():
            copy(idx).wait_recv()
            o_ref[...] = (acc_ref[...] + land_buf[idx].astype(F32)).astype(BF16)

        @pl.when(jnp.logical_and(jnp.logical_and(p == 1, idx == n_tiles - 1), kk == nk - 1))
        def _():
            for i in range(max(n_tiles - N_SEND_SLOTS, 0), n_tiles):
                copy(i).wait_send()

    grid_spec = pltpu.PrefetchScalarGridSpec(
        num_scalar_prefetch=1, grid=(2, nt, nj, nk),
        in_specs=[pl.BlockSpec((tk, tm), lambda p, t, j, kk, pos_ref: (kk, row_block(p, t, pos_ref)))]
        + [_piece_spec(pc, tk, tn, st, lambda p, t, j, kk, pos_ref: kk, lambda p, t, j, kk, pos_ref: j)
           for pc, st in zip(pieces, starts)],
        out_specs=pl.BlockSpec((tm, tn), lambda p, t, j, kk, pos_ref: (p * t, p * j)),
        scratch_shapes=[pltpu.VMEM((tm, tn), F32), pltpu.VMEM((N_SEND_SLOTS, tm, tn), BF16),
                        pltpu.VMEM((n_tiles, tm, tn), BF16),
                        pltpu.SemaphoreType.DMA((n_tiles,)), pltpu.SemaphoreType.DMA((n_tiles,))])
    return pl.pallas_call(
        kern, name=name, grid_spec=grid_spec, out_shape=jax.ShapeDtypeStruct((m // 2, n), BF16),
        compiler_params=_cparams(("arbitrary",) * 4),
    )(pos, a, *[pc[0] for pc in pieces])


def _rope_tables():
    half = RET_DK // 2
    f32 = np.float32
    inv = np.power(f32(ROPE_BASE), -np.arange(half, dtype=f32) / f32(half)).astype(f32)
    ang = (np.arange(SEQ, dtype=f32)[:, None] * inv[None, :]).astype(f32)
    return jnp.asarray(np.cos(ang).astype(f32)), jnp.asarray(np.sin(ang).astype(f32))


def _decay_tables():
    c = RET_CHUNK
    f32 = np.float32
    log_g = np.log1p(-np.power(f32(2.0), f32(-5.0) - np.arange(RET_HEADS, dtype=f32))).astype(f32)
    idx = np.arange(c, dtype=f32)
    rel = idx[:, None] - idx[None, :]
    din = np.where(rel >= 0, np.exp(log_g[:, None, None] * np.maximum(rel, f32(0.0))), f32(0.0)).astype(f32)
    qd = np.exp(log_g[:, None] * (idx + f32(1.0))).astype(f32)[:, :, None]
    kd = np.exp(log_g[:, None] * (f32(c) - f32(1.0) - idx)).astype(f32)[:, :, None]
    cd = np.exp(log_g * f32(c)).astype(f32)
    return jnp.asarray(din), jnp.asarray(qd), jnp.asarray(kd), jnp.asarray(cd)


def _t5_bucket(dist):
    max_exact = REL_BUCKETS // 2
    d_f = jnp.maximum(dist, 1).astype(F32)
    large = max_exact + (jnp.log(d_f / max_exact) / math.log(REL_MAX_DIST / max_exact)
                         * (REL_BUCKETS - max_exact)).astype(I32)
    large = jnp.minimum(large, REL_BUCKETS - 1)
    return jnp.where(dist < max_exact, dist, large)


def _bucket_tables():
    qi = jnp.arange(ATT_BLK)[:, None]
    kj = jnp.arange(2 * ATT_BLK)[None, :]
    dist = jnp.clip(ATT_BLK + qi - kj, 0, ATT_BLK)
    return jnp.stack([_t5_bucket(dist * dil) for _, dil in ATT_GROUPS]).astype(I32)


def _retention_fwd(rqk, rv, rg, gn_g, gn_b, din, qd, kd, cd):
    nc = SEQ // RET_CHUNK
    c, dk, dv = RET_CHUNK, RET_DK, RET_DV

    def kern(q_ref, k_ref, v_ref, rg_ref, g_ref, b_ref, din_ref, qd_ref, kd_ref, cd_ref,
             o_ref, st_ref, gated_ref, state):
        n = pl.program_id(0)

        @pl.when(n == 0)
        def _():
            state[...] = jnp.zeros_like(state)

        for sub in range(RET_SUB):
            rows = slice(sub * c, (sub + 1) * c)
            for h in range(RET_HEADS):
                q, k = q_ref[rows, h * dk:(h + 1) * dk], k_ref[rows, h * dk:(h + 1) * dk]
                v = v_ref[rows, h * dv:(h + 1) * dv]
                s_b = state[h].astype(BF16)
                st_ref[h, sub] = s_b
                a = lax.dot_general(q, k, _NT, preferred_element_type=F32) * din_ref[h]
                o = jnp.dot(a.astype(BF16), v, preferred_element_type=F32)
                o += jnp.dot(q, s_b, preferred_element_type=F32) * qd_ref[h]
                v_cols = slice(h * dv, (h + 1) * dv)
                o_ref[rows, v_cols] = o
                nrm, _ = _gn_parts(o)
                gate = rg_ref[rows, v_cols].astype(F32)
                gated_ref[rows, v_cols] = ((gate * _sigmoid(gate))
                                           * (nrm * g_ref[:, v_cols] + b_ref[:, v_cols])).astype(BF16)
                kk = (k.astype(F32) * kd_ref[h]).astype(BF16)
                state[h] = state[h] * cd_ref[h] + lax.dot_general(kk, v, _TN, preferred_element_type=F32)

    whole = lambda a: pl.BlockSpec(a.shape, lambda n: (0,) * a.ndim)
    cs = RET_SUB * c
    rows_v = pl.BlockSpec((cs, RET_V_W), lambda n: (n, 0))
    return pl.pallas_call(
        kern, name="retention_fwd", grid=(nc // RET_SUB,),
        in_specs=[
            pl.BlockSpec((cs, RET_QK_W), lambda n: (n, 0)),
            pl.BlockSpec((cs, RET_QK_W), lambda n: (n, 1)),
            rows_v, rows_v, whole(gn_g), whole(gn_b),
            whole(din), whole(qd), whole(kd),
            pl.BlockSpec(memory_space=pltpu.SMEM),
        ],
        out_specs=[
            rows_v,
            pl.BlockSpec((RET_HEADS, RET_SUB, dk, dv), lambda n: (0, n, 0, 0)),
            rows_v,
        ],
        out_shape=[
            jax.ShapeDtypeStruct((SEQ, RET_V_W), F32),
            jax.ShapeDtypeStruct((RET_HEADS, nc, dk, dv), BF16),
            jax.ShapeDtypeStruct((SEQ, RET_V_W), BF16),
        ],
        scratch_shapes=[pltpu.VMEM((RET_HEADS, dk, dv), F32)],
        compiler_params=_cparams(("arbitrary",)),
    )(rqk, rqk, rv, rg, gn_g, gn_b, din, qd, kd, cd)


def _retention_bwd(rqk, rv, states, d_gated, ro, rg, gn_g, gn_b, din, qd, kd, cd, cos, sin):
    nc = SEQ // RET_CHUNK
    c, dk, dv = RET_CHUNK, RET_DK, RET_DV
    half = dk // 2
    last = nc // RET_SUB - 1

    def unrot(g, cs, sn):
        g1, g2 = g[:, :half], g[:, half:]
        return jnp.concatenate([g1 * cs + g2 * sn, g2 * cs - g1 * sn], axis=-1)

    def kern(q_ref, k_ref, v_ref, st_ref, dg_ref, ro_ref, rg_ref, g_ref, b_ref, din_ref, qd_ref, kd_ref,
             cd_ref, cos_ref, sin_ref, out_ref, drg_ref, dgn_g_ref, dgn_b_ref, dstate):
        step = pl.program_id(0)

        @pl.when(step == 0)
        def _():
            dstate[...] = jnp.zeros_like(dstate)
            dgn_g_ref[...] = jnp.zeros_like(dgn_g_ref)
            dgn_b_ref[...] = jnp.zeros_like(dgn_b_ref)

        for sub in reversed(range(RET_SUB)):
            rows = slice(sub * c, (sub + 1) * c)
            cs, sn = cos_ref[rows, :], sin_ref[rows, :]
            for h in range(RET_HEADS):
                qk_cols, v_cols = slice(h * dk, (h + 1) * dk), slice(h * dv, (h + 1) * dv)
                q, k, v = q_ref[rows, qk_cols], k_ref[rows, qk_cols], v_ref[rows, v_cols]
                s_b = st_ref[h, sub]
                nrm, rstd = _gn_parts(ro_ref[rows, v_cols])
                gate, dg = rg_ref[rows, v_cols].astype(F32), dg_ref[rows, v_cols].astype(F32)
                sg = _sigmoid(gate)
                gn_gain = g_ref[:, v_cols]
                drg_ref[rows, v_cols] = (dg * (nrm * gn_gain + b_ref[:, v_cols])
                                         * (sg * (1.0 + gate * (1.0 - sg)))).astype(BF16)
                d_ron = dg * (gate * sg)
                dgn_g_ref[:, v_cols] += jnp.sum(d_ron * nrm, axis=0, keepdims=True)
                dgn_b_ref[:, v_cols] += jnp.sum(d_ron, axis=0, keepdims=True)
                d_n = d_ron * gn_gain
                d_o = rstd * (d_n - jnp.mean(d_n, axis=-1, keepdims=True)
                              - nrm * jnp.mean(d_n * nrm, axis=-1, keepdims=True))
                d_ob = d_o.astype(BF16)
                d_oq = (d_o * qd_ref[h]).astype(BF16)
                ds_b = dstate[h].astype(BF16)
                din_m = din_ref[h]
                a_b = (lax.dot_general(q, k, _NT, preferred_element_type=F32) * din_m).astype(BF16)
                kk = (k.astype(F32) * kd_ref[h]).astype(BF16)
                d_v = lax.dot_general(a_b, d_ob, _TN, preferred_element_type=F32)
                d_v += jnp.dot(kk, ds_b, preferred_element_type=F32)
                d_a = (lax.dot_general(d_ob, v, _NT, preferred_element_type=F32) * din_m).astype(BF16)
                d_q = jnp.dot(d_a, k, preferred_element_type=F32)
                d_q += lax.dot_general(d_oq, s_b, _NT, preferred_element_type=F32)
                d_k = lax.dot_general(d_a, q, _TN, preferred_element_type=F32)
                d_k += lax.dot_general(v, ds_b, _NT, preferred_element_type=F32) * kd_ref[h]
                dstate[h] = dstate[h] * cd_ref[h] + lax.dot_general(q, d_oq, _TN,
                                                                    preferred_element_type=F32)
                out_ref[rows, h * dk:(h + 1) * dk] = unrot(d_q, cs, sn).astype(BF16)
                out_ref[rows, RET_QK_W + h * dk:RET_QK_W + (h + 1) * dk] = (
                    unrot(d_k, cs, sn) * (RET_DK ** -0.5)).astype(BF16)
                out_ref[rows, 2 * RET_QK_W + h * dv:2 * RET_QK_W + (h + 1) * dv] = d_v.astype(BF16)

    whole = lambda a: pl.BlockSpec(a.shape, lambda n: (0,) * a.ndim)
    rs = RET_SUB * c
    rows_v = pl.BlockSpec((rs, RET_V_W), lambda n: (last - n, 0))
    return pl.pallas_call(
        kern, name="retention_bwd", grid=(nc // RET_SUB,),
        in_specs=[
            pl.BlockSpec((rs, RET_QK_W), lambda n: (last - n, 0)),
            pl.BlockSpec((rs, RET_QK_W), lambda n: (last - n, 1)),
            rows_v,
            pl.BlockSpec((RET_HEADS, RET_SUB, dk, dv), lambda n: (0, last - n, 0, 0)),
            rows_v, rows_v, rows_v, whole(gn_g), whole(gn_b),
            whole(din), whole(qd), whole(kd),
            pl.BlockSpec(memory_space=pltpu.SMEM),
            pl.BlockSpec((rs, half), lambda n: (last - n, 0)),
            pl.BlockSpec((rs, half), lambda n: (last - n, 0)),
        ],
        out_specs=[pl.BlockSpec((rs, 2 * RET_QK_W + RET_V_W), lambda n: (last - n, 0)), rows_v,
                   whole(gn_g), whole(gn_b)],
        out_shape=[jax.ShapeDtypeStruct((SEQ, 2 * RET_QK_W + RET_V_W), BF16),
                   jax.ShapeDtypeStruct((SEQ, RET_V_W), BF16),
                   jax.ShapeDtypeStruct((1, RET_V_W), F32), jax.ShapeDtypeStruct((1, RET_V_W), F32)],
        scratch_shapes=[pltpu.VMEM((RET_HEADS, dk, dv), F32)],
        compiler_params=_cparams(("arbitrary",)),
    )(rqk, rqk, rv, states, d_gated, ro, rg, gn_g, gn_b, din, qd, kd, cd, cos, sin)


def _bias_build(rel_bias, buckets):
    ng = len(ATT_GROUPS)

    def kern(tab_ref, bkt_ref, o_ref):
        g, h = pl.program_id(0), pl.program_id(1)
        bkt = bkt_ref[...]
        acc = jnp.zeros(bkt.shape, F32)
        for b in range(REL_BUCKETS):
            acc = jnp.where(bkt == b, tab_ref[b, g * ATT_HPG + h], acc)
        o_ref[...] = acc

    return pl.pallas_call(
        kern, name="bias_build", grid=(ng, ATT_HPG),
        in_specs=[pl.BlockSpec(memory_space=pltpu.SMEM),
                  pl.BlockSpec((None, ATT_BLK, 2 * ATT_BLK), lambda g, h: (g, 0, 0))],
        out_specs=pl.BlockSpec((None, None, ATT_BLK, 2 * ATT_BLK), lambda g, h: (g, h, 0, 0)),
        out_shape=jax.ShapeDtypeStruct((ng, ATT_HPG, ATT_BLK, 2 * ATT_BLK), F32),
        compiler_params=_cparams(("arbitrary", "arbitrary")),
    )(rel_bias, buckets)


def _bias_grad(dsb, buckets):
    ng = len(ATT_GROUPS)

    def kern(ds_ref, bkt_ref, o_ref):
        g, h = pl.program_id(0), pl.program_id(1)
        bkt, ds = bkt_ref[...], ds_ref[...]
        for b in range(REL_BUCKETS):
            o_ref[b, g * ATT_HPG + h] = jnp.sum(jnp.where(bkt == b, ds, 0.0))

    return pl.pallas_call(
        kern, name="bias_grad", grid=(ng, ATT_HPG),
        in_specs=[pl.BlockSpec((None, None, ATT_BLK, 2 * ATT_BLK), lambda g, h: (g, h, 0, 0)),
                  pl.BlockSpec((None, ATT_BLK, 2 * ATT_BLK), lambda g, h: (g, 0, 0))],
        out_specs=pl.BlockSpec(memory_space=pltpu.SMEM),
        out_shape=jax.ShapeDtypeStruct((REL_BUCKETS, N_ATT_HEADS), F32),
        compiler_params=_cparams(("arbitrary", "arbitrary")),
    )(dsb, buckets)


_NT = (((1,), (1,)), ((), ()))
_TN = (((0,), (0,)), ((), ()))
_ATT_SCALE = ATT_DH ** -0.5


def _window_mask(has_prev):
    qi = lax.broadcasted_iota(I32, (ATT_BLK, 2 * ATT_BLK), 0)
    kj = lax.broadcasted_iota(I32, (ATT_BLK, 2 * ATT_BLK), 1)
    prev_ok = jnp.logical_and(jnp.logical_and(kj < ATT_BLK, kj >= qi), has_prev)
    return jnp.logical_or(prev_ok, jnp.logical_and(kj >= ATT_BLK, qi >= kj - ATT_BLK))


def _head_specs(col0):
    return pl.BlockSpec((SEQ, ATT_DH), lambda h: (0, col0 + h))


def _sub_rows(start, size, dil):
    return pl.ds(start, size) if dil == 1 else pl.ds(start, size, stride=dil)


def _att_blocks(dil):
    nb = SEQ // dil // ATT_BLK
    return [(r + dil * n * ATT_BLK, n > 0, n + 1 < nb) for r in range(dil) for n in range(nb)]


def _att_fwd(gi, dil, qkv, bias, after=()):
    blk, dh = ATT_BLK, ATT_DH
    pad = dil * blk
    col0 = 3 * ATT_HPG * gi

    def kern(q_ref, k_ref, v_ref, b_ref, *rest):
        o_ref, l_ref, qf, kpad, vpad = rest[len(after):]
        zero = jnp.zeros((pad, dh), F32)
        kpad[0:pad, :] = zero
        vpad[0:pad, :] = zero
        kpad[pad:, :] = k_ref[...].astype(F32)
        vpad[pad:, :] = v_ref[...].astype(F32)
        qf[...] = q_ref[...].astype(F32)
        bias_m = b_ref[...]
        for start, has_prev, _ in _att_blocks(dil):
            rows, window = _sub_rows(start, blk, dil), _sub_rows(start, 2 * blk, dil)
            q = qf[rows, :].astype(BF16)
            kw, vw = kpad[window, :].astype(BF16), vpad[window, :].astype(BF16)
            valid = _window_mask(has_prev)
            s = lax.dot_general(q, kw, _NT, preferred_element_type=F32) * _ATT_SCALE + bias_m
            s = jnp.where(valid, s, -1e30)
            mx = jnp.max(s, axis=-1, keepdims=True)
            e = jnp.exp(s - mx)
            den = jnp.sum(e, axis=-1, keepdims=True)
            o_ref[rows, :] = jnp.dot((e / den).astype(BF16), vw, preferred_element_type=F32)
            l_ref[rows, :] = jnp.broadcast_to(mx + jnp.log(den), (blk, dh))

    return pl.pallas_call(
        kern, name=f"att_fwd_g{gi}", grid=(ATT_HPG,),
        in_specs=[_head_specs(col0), _head_specs(col0 + ATT_HPG), _head_specs(col0 + 2 * ATT_HPG),
                  pl.BlockSpec((None, None, blk, 2 * blk), lambda h: (gi, h, 0, 0))]
        + [pl.BlockSpec(memory_space=pl.ANY)] * len(after),
        out_specs=[_head_specs(0), _head_specs(0)],
        out_shape=[jax.ShapeDtypeStruct((SEQ, ATT_W), F32), jax.ShapeDtypeStruct((SEQ, ATT_W), F32)],
        scratch_shapes=[pltpu.VMEM((SEQ, dh), F32), pltpu.VMEM((SEQ + pad, dh), F32),
                        pltpu.VMEM((SEQ + pad, dh), F32)],
        compiler_params=_cparams(("arbitrary",)),
    )(qkv, qkv, qkv, bias, *after)


def _att_bwd(gi, dil, qkv, d_att, lse, dd, bias):
    blk, dh = ATT_BLK, ATT_DH
    pad = dil * blk
    col0 = 3 * ATT_HPG * gi

    def kern(q_ref, k_ref, v_ref, do_ref, l_ref, d_ref, b_ref, dqkv_ref, dsb_ref,
             qf, kpad, vpad, dq_s, dkpad, dvpad):
        zero = jnp.zeros((pad, dh), F32)
        kpad[0:pad, :] = zero
        vpad[0:pad, :] = zero
        kpad[pad:, :] = k_ref[...].astype(F32)
        vpad[pad:, :] = v_ref[...].astype(F32)
        qf[...] = q_ref[...].astype(F32)
        dkpad[...] = jnp.zeros_like(dkpad)
        dvpad[...] = jnp.zeros_like(dvpad)
        bias_m = b_ref[...]
        ds_sum = jnp.zeros((blk, 2 * blk), F32)

        for start, has_prev, _ in _att_blocks(dil):
            rows, window = _sub_rows(start, blk, dil), _sub_rows(start, 2 * blk, dil)
            q, d_o = qf[rows, :].astype(BF16), do_ref[rows, :].astype(BF16)
            kw, vw = kpad[window, :].astype(BF16), vpad[window, :].astype(BF16)
            lrow, drow = l_ref[rows, :][:, :1], d_ref[rows, :][:, :1]
            valid = _window_mask(has_prev)
            s = lax.dot_general(q, kw, _NT, preferred_element_type=F32) * _ATT_SCALE + bias_m
            p = jnp.where(valid, jnp.exp(jnp.where(valid, s, -1e30) - lrow), 0.0)
            dp = lax.dot_general(d_o, vw, _NT, preferred_element_type=F32)
            ds = p * (dp - drow)
            ds_b = ds.astype(BF16)
            dq_s[rows, :] = jnp.dot(ds_b, kw, preferred_element_type=F32) * _ATT_SCALE
            dkpad[window, :] += lax.dot_general(ds_b, q, _TN, preferred_element_type=F32) * _ATT_SCALE
            dvpad[window, :] += lax.dot_general(p.astype(BF16), d_o, _TN, preferred_element_type=F32)
            ds_sum = ds_sum + ds
        dsb_ref[...] = ds_sum

        dqkv_ref[0] = dq_s[...].astype(BF16)
        dqkv_ref[1] = dkpad[pad:, :].astype(BF16)
        dqkv_ref[2] = dvpad[pad:, :].astype(BF16)

    return pl.pallas_call(
        kern, name=f"att_bwd_g{gi}", grid=(ATT_HPG,),
        in_specs=[_head_specs(col0), _head_specs(col0 + ATT_HPG), _head_specs(col0 + 2 * ATT_HPG),
                  _head_specs(0), _head_specs(0), _head_specs(0),
                  pl.BlockSpec((None, None, blk, 2 * blk), lambda h: (gi, h, 0, 0))],
        out_specs=[pl.BlockSpec((3, SEQ, dh), lambda h: (0, 0, h)),
                   pl.BlockSpec((None, blk, 2 * blk), lambda h: (h, 0, 0))],
        out_shape=[jax.ShapeDtypeStruct((3, SEQ, ATT_W), BF16),
                   jax.ShapeDtypeStruct((ATT_HPG, blk, 2 * blk), F32)],
        scratch_shapes=[pltpu.VMEM((SEQ, dh), F32), pltpu.VMEM((SEQ + pad, dh), F32),
                        pltpu.VMEM((SEQ + pad, dh), F32), pltpu.VMEM((SEQ, dh), F32),
                        pltpu.VMEM((SEQ + pad, dh), F32), pltpu.VMEM((SEQ + pad, dh), F32)],
        compiler_params=_cparams(("arbitrary",)),
    )(qkv, qkv, qkv, d_att, lse, dd, bias)


def _rms_parts(x):
    r = lax.rsqrt(jnp.mean(x * x, axis=-1, keepdims=True) + RMS_EPS)
    return x * r, r


def _rms_bwd(d_xhat, xhat, r):
    return r * (d_xhat - xhat * jnp.mean(d_xhat * xhat, axis=-1, keepdims=True))


def _prenorm_fwd(name, x, gain, shift, scale, tm=256):
    def kern(x_ref, g_ref, sh_ref, sc_ref, o_ref):
        xhat, _ = _rms_parts(x_ref[...])
        o_ref[...] = ((xhat * g_ref[...]) * (1.0 + sc_ref[...]) + sh_ref[...]).astype(BF16)

    rows = pl.BlockSpec((tm, D_MODEL), lambda i: (i, 0))
    vec = pl.BlockSpec((1, D_MODEL), lambda i: (0, 0))
    return pl.pallas_call(
        kern, name=name, grid=(x.shape[0] // tm,), in_specs=[rows, vec, vec, vec], out_specs=rows,
        out_shape=jax.ShapeDtypeStruct(x.shape, BF16), compiler_params=_cparams(("parallel",)),
    )(x, gain, shift, scale)


def _prenorm_bwd_epi(d_h, x, resid, gain, scale, branch=None, gate=None):
    xhat, r = _rms_parts(x)
    nrm = xhat * gain
    d_n = d_h * (1.0 + scale)
    dx = _rms_bwd(d_n * gain, xhat, r) + resid
    sums = (jnp.sum(d_h, axis=0, keepdims=True), jnp.sum(d_h * nrm, axis=0, keepdims=True),
            jnp.sum(d_n * xhat, axis=0, keepdims=True))
    if branch is None:
        return (dx,) + sums
    return (dx, dx * gate) + sums + (jnp.sum(dx * branch, axis=0, keepdims=True),)


def _row_operands(tm, rows, vecs):
    return ([(a, (tm, D_MODEL), lambda i, j, kk: (i, 0)) for a in rows]
            + [(v, (1, D_MODEL), lambda i, j, kk: (0, 0)) for v in vecs])


def _gn_parts(ro):
    mu = jnp.mean(ro, axis=-1, keepdims=True)
    cen = ro - mu
    rstd = lax.rsqrt(jnp.mean(cen * cen, axis=-1, keepdims=True) + GN_EPS)
    return cen * rstd, rstd


MERGE_TM = 512


def _att_out(os_, ls_, w_att_out, gates, ret_out):
    tm = MERGE_TM

    def kern(o0, o1, o2, l0, l1, l2, w_ref, ga_ref, gb_ref, ro_ref, att_ref, attb_ref, lse_ref,
             ao_ref, mg_ref):
        l0v, l1v, l2v = l0[...], l1[...], l2[...]
        mx = jnp.maximum(jnp.maximum(l0v, l1v), l2v)
        e0, e1, e2 = jnp.exp(l0v - mx), jnp.exp(l1v - mx), jnp.exp(l2v - mx)
        den = e0 + e1 + e2
        att = (e0 / den) * o0[...] + (e1 / den) * o1[...] + (e2 / den) * o2[...]
        att_b = att.astype(BF16)
        att_ref[...] = att
        attb_ref[...] = att_b
        lse_ref[...] = mx + jnp.log(den)
        att_out = jnp.dot(att_b, w_ref[...], preferred_element_type=F32)
        ao_ref[...], merged = _merge_fwd_epi(att_out, ga_ref[...], gb_ref[...], ro_ref[...])
        mg_ref[...] = merged.astype(BF16)

    rows_w = pl.BlockSpec((tm, ATT_W), lambda i: (i, 0))
    rows_d = pl.BlockSpec((tm, D_MODEL), lambda i: (i, 0))
    return pl.pallas_call(
        kern, name="att_out", grid=(SEQ // tm,),
        in_specs=[rows_w] * 6 + [pl.BlockSpec((ATT_W, D_MODEL), lambda i: (0, 0)), rows_d,
                                 pl.BlockSpec((tm, D_MODEL), lambda i: (i, 1)), rows_d],
        out_specs=[rows_w, rows_w, rows_w, rows_d, rows_d],
        out_shape=[jax.ShapeDtypeStruct((SEQ, ATT_W), F32), jax.ShapeDtypeStruct((SEQ, ATT_W), BF16),
                   jax.ShapeDtypeStruct((SEQ, ATT_W), F32), jax.ShapeDtypeStruct((SEQ, D_MODEL), F32),
                   jax.ShapeDtypeStruct((SEQ, D_MODEL), BF16)],
        compiler_params=_cparams(("parallel",)),
    )(*os_, *ls_, w_att_out, gates, gates, ret_out)


def _merge_operands(gates, ret_out, att_out=None):
    ops = [(gates, (MERGE_TM, D_MODEL), lambda i, j, kk: (i, 0)),
           (gates, (MERGE_TM, D_MODEL), lambda i, j, kk: (i, 1)),
           (ret_out, (MERGE_TM, D_MODEL), lambda i, j, kk: (i, 0))]
    if att_out is not None:
        ops.append((att_out, (MERGE_TM, D_MODEL), lambda i, j, kk: (i, 0)))
    return ops


def _merge_fwd_epi(att_out, ga, gb, ret_out):
    return att_out, _sigmoid(ga.astype(F32)) * ret_out + _sigmoid(gb.astype(F32)) * att_out


def _merge_bwd_epi(d_merged, ga, gb, ret_out, att_out):
    sa, sb = _sigmoid(ga.astype(F32)), _sigmoid(gb.astype(F32))
    return (d_merged * sa, d_merged * sb, d_merged * ret_out * (sa * (1.0 - sa)),
            d_merged * att_out * (sb * (1.0 - sb)))


def _att_out_bwd_epi(d_att, att):
    outs = []
    for h in range(ATT_HPG):
        sl = slice(h * ATT_DH, (h + 1) * ATT_DH)
        outs.append(jnp.broadcast_to(jnp.sum(d_att[:, sl] * att[:, sl], axis=-1, keepdims=True),
                                     (d_att.shape[0], ATT_DH)))
    return d_att, jnp.concatenate(outs, axis=-1)


def _loss_head_epi(branch, x_prev, target, gate, gain):
    x3 = x_prev + gate * branch
    xhat, r = _rms_parts(x3)
    err = xhat * gain - target
    d_y = err / D_MODEL
    loss = 0.5 * jnp.sum(jnp.mean(err * err, axis=-1, keepdims=True), axis=0, keepdims=True)
    d_x = _rms_bwd(d_y * gain, xhat, r)
    return (d_x, d_x * gate, jnp.broadcast_to(loss, (1, D_MODEL)),
            jnp.sum(d_y * xhat, axis=0, keepdims=True), jnp.sum(d_x * branch, axis=0, keepdims=True))


def _local_step(pos, x, target, mod, norm1_g, norm2_g, norm_f_g, rel_bias, gn_g, gn_b, w_in, rest_gather):
    sh1, sc1, g1, sh2, sc2, g2 = [mod[:, i * D_MODEL:(i + 1) * D_MODEL] for i in range(6)]
    cos, sin = _rope_tables()
    din, qd, kd, cd = _decay_tables()
    buckets = _bucket_tables()
    bias = _bias_build(rel_bias, buckets)
    dils = [d for _, d in ATT_GROUPS]

    h1 = _prenorm_fwd("prenorm1_fwd", x, norm1_g, sh1, sc1)

    qk_tn = 2 * RET_DK

    def rot_epi(acc, cs, sn, scale):
        half = RET_DK // 2
        outs = []
        for h0 in range(0, qk_tn, RET_DK):
            x1, x2 = acc[:, h0:h0 + half], acc[:, h0 + half:h0 + RET_DK]
            outs += [x1 * cs - x2 * sn, x1 * sn + x2 * cs]
        return (jnp.concatenate(outs, axis=-1) * scale,)

    qk_scale = jnp.concatenate([jnp.ones((1, RET_QK_W), F32),
                                jnp.full((1, RET_QK_W), RET_DK ** -0.5, F32)], axis=-1)
    rope_ex = [(cos, (TM, RET_DK // 2), lambda i, j, kk: (i, 0)),
               (sin, (TM, RET_DK // 2), lambda i, j, kk: (i, 0)),
               (qk_scale, (1, qk_tn), lambda i, j, kk: (0, j))]
    rest_sems, rest_shards, rest_fulls, rest_token = rest_gather
    behind = [rest_token]
    rv = _matmul("proj_rv", h1, w_in, "nn", SEQ, RET_V_W, D_MODEL, [BF16], b_off=OFF_V, tk=D_MODEL,
                 after=behind)[0]
    rg = _matmul("proj_rg", h1, w_in, "nn", SEQ, RET_V_W, D_MODEL, [BF16], b_off=OFF_G, tk=D_MODEL,
                 after=behind)[0]
    gates = _matmul("proj_gates", h1, w_in, "nn", SEQ, 2 * D_MODEL, D_MODEL, [BF16], b_off=OFF_GATE,
                    tn=512, tk=D_MODEL, after=behind)[0]
    aqkv = _matmul("proj_att", h1, w_in, "nn", SEQ, 9 * ATT_W, D_MODEL, [BF16], b_off=OFF_ATT,
                   tn=512, tk=D_MODEL, after=behind)[0]

    rqk = _matmul("proj_qk", h1, w_in, "nn", SEQ, 2 * RET_QK_W, D_MODEL, [BF16], b_off=OFF_Q,
                  tn=qk_tn, tk=D_MODEL, epilogue=rot_epi, extras=rope_ex, after=behind)[0]
    ro, states, gated = _retention_fwd(rqk, rv, rg, gn_g, gn_b, din, qd, kd, cd)
    os_, ls_ = [], []
    for gi in range(3):
        if gi == 2:
            rest_sems, rest_fulls, fwd_token = _gather_rest_forward(
                rest_sems, rest_shards, rest_fulls, [gated, gates] + os_)
        o_g, l_g = _att_fwd(gi, dils[gi], aqkv, bias, after=[fwd_token] if gi == 2 else ())
        os_.append(o_g)
        ls_.append(l_g)
    w_ret_out, w_att_out, w_o, w_ff1, w_ff2 = _gather_rest_end(rest_sems, rest_fulls, [os_[2]])
    ret_out = _matmul("ret_out", gated, w_ret_out, "nn", SEQ, D_MODEL, RET_V_W, [F32], tk=RET_V_W)[0]
    att, att_b, lse, att_out, merged = _att_out(os_, ls_, w_att_out, gates, ret_out)

    def mix_epi(acc, xt, g, gain, sh, sc):
        x_new = xt + g * acc
        xhat, _ = _rms_parts(x_new)
        return x_new, acc, (xhat * gain) * (1.0 + sc) + sh

    x2, mix, h2 = _matmul("mix_out", merged, w_o, "nn", SEQ, D_MODEL, D_MODEL, [F32, BF16, BF16],
                          epilogue=mix_epi, extras=_row_operands(TM, [x], [g1, norm2_g, sh2, sc2]))

    def relu2_epi(acc):
        r = jnp.maximum(acc, 0.0)
        return r * r, r

    act, relu_u = _matmul("ff1", h2, w_ff1, "nn", SEQ, D_FF, D_MODEL, [BF16, BF16], tk=D_MODEL,
                          epilogue=relu2_epi)
    d_x3, d_y2, loss, d_gf, d_g2 = _matmul(
        "ff2", act, w_ff2, "nn", SEQ, D_MODEL, D_FF, [F32, BF16], tm=TM, tk=1024, n_sums=3,
        epilogue=_loss_head_epi, extras=_row_operands(TM, [x2, target], [g2, norm_f_g]))

    def relu2_bwd_epi(acc, rt):
        return (acc * (2.0 * rt.astype(F32)),)

    gw_ff2 = _matmul_tn_pair("ff2_dw", pos, act, d_y2, D_FF, D_MODEL, SEQ, D_FF // N_CHIPS,
                             tm=512, tn=1024, tk=SEQ)
    d_u = _matmul("ff2_dx", d_y2, w_ff2, "nt", SEQ, D_FF, D_MODEL, [BF16], epilogue=relu2_bwd_epi,
                  extras=[(relu_u, (TM, TN), lambda i, j, kk: (i, j))])[0]
    gw_ff1 = _matmul_tn_pair("ff1_dw", pos, h2, d_u, D_MODEL, D_FF, SEQ, D_MODEL,
                             tm=512, tn=1024, tk=SEQ)
    ffn = ["w_ff2", "w_ff1"]
    ffn_started = _ici_start("ici_start_ffn", ffn, [gw_ff2, gw_ff1])
    d_x2, d_mix, d_sh2, d_sc2, d_n2g, d_g1 = _matmul(
        "ff1_dx", d_u, w_ff1, "nt", SEQ, D_MODEL, D_FF, [F32, BF16], tm=TM, tk=1024, n_sums=4,
        epilogue=_prenorm_bwd_epi, extras=_row_operands(TM, [x2, d_x3], [norm2_g, sc2])
        + _row_operands(TM, [mix], [g1]), after=[ffn_started[3]])
    gw_o = _matmul_tn_pair("mix_dw", pos, merged, d_mix, D_MODEL, D_MODEL, SEQ, D_MODEL // N_CHIPS,
                           tm=128, tn=1024, tk=2048)
    d_ret_out, d_att_out, d_ga, d_gb = _matmul(
        "mix_dx", d_mix, w_o, "nt", SEQ, D_MODEL, D_MODEL, [BF16] * 4, tm=MERGE_TM,
        epilogue=_merge_bwd_epi, extras=_merge_operands(gates, ret_out, att_out))

    gw_ret_out = _matmul_tn_pair("ret_out_dw", pos, gated, d_ret_out, RET_V_W, D_MODEL, SEQ,
                                 RET_V_W // N_CHIPS, tm=256, tn=1024, tk=SEQ)
    gw_att_out = _matmul_tn_pair("att_out_dw", pos, att_b, d_att_out, ATT_W, D_MODEL, SEQ, ATT_W,
                                 tm=256, tn=1024, tk=2048)
    mixer = ["w_o", "w_ret_out", "w_att_out"]
    mixer_started = _ici_start("ici_start_mixer", mixer, [gw_o, gw_ret_out, gw_att_out])
    d_gated = _matmul("ret_out_dx", d_ret_out, w_ret_out, "nt", SEQ, RET_V_W, D_MODEL, [BF16],
                      after=[mixer_started[3]])[0]
    d_att, dd = _matmul("att_out_dx", d_att_out, w_att_out, "nt", SEQ, ATT_W, D_MODEL, [F32, F32],
                        epilogue=_att_out_bwd_epi,
                        extras=[(att, (TM, ATT_W), lambda i, j, kk: (i, 0))], after=[mixer_started[3]])

    d_rqkv, d_rg, d_gn_g, d_gn_b = _retention_bwd(rqk, rv, states, d_gated, ro, rg, gn_g, gn_b,
                                                  din, qd, kd, cd, cos, sin)

    d_aqkv, dsbs = [], []
    for gi in range(3):
        dqkv, dsb = _att_bwd(gi, dils[gi], aqkv, d_att, lse, dd, bias)
        d_aqkv.append(dqkv)
        dsbs.append(dsb)
    d_rel_bias = _bias_grad(jnp.stack(dsbs), buckets)

    d_proj = ([(d_rqkv, False), (d_rg, False)] + [(t, True) for t in d_aqkv]
              + [(d_ga, False), (d_gb, False)])
    gw_in = _matmul_tn_pair("proj_dw", pos, h1, d_proj, D_MODEL, IN_COLS, SEQ, D_MODEL,
                            tm=512, tn=ATT_W, tk=SEQ)
    sems, (gw_in,), (land,), token = _ici_start("ici_start_w_in", ["w_in"], [gw_in])
    grad_x, d_sh1, d_sc1, d_n1g = _matmul(
        "proj_dx", d_proj, w_in, "nt", SEQ, D_MODEL, IN_COLS, [F32], tn=1024, tk=ATT_W, n_sums=3,
        epilogue=_prenorm_bwd_epi, extras=_row_operands(TM, [x, d_x2], [norm1_g, sc1]), after=[token])
    pending = (sems, land)

    names = ffn + mixer
    psums, got = _ici_wait("ici_wait_rest", names, list(ffn_started[0]) + list(mixer_started[0]),
                           list(ffn_started[1]) + list(mixer_started[1]),
                           list(ffn_started[2]) + list(mixer_started[2]), [grad_x])
    g_big = {n: _final_sum("final_" + n, pos, dict(BIG)[n], psums[i], got[i], SHARD[n])
             for i, n in enumerate(names)}
    d_mod = jnp.concatenate([d_sh1, d_sc1, d_g1, d_sh2, d_sc2, d_g2], axis=-1)
    small = dict(norm1_g=d_n1g, norm2_g=d_n2g, norm_f_g=d_gf, gn_g=d_gn_g, gn_b=d_gn_b,
                 rel_bias=d_rel_bias)
    return loss, grad_x, d_mod, small, g_big, (gw_in,) + pending


def _me():
    return lax.axis_index("x"), lax.axis_index("y"), lax.axis_index("c")


def _peer(x, y, c, mask):
    return (x ^ ((mask >> 2) & 1), y ^ ((mask >> 1) & 1), c ^ (mask & 1))


def _gather8(src_ref, dst_ref, send_sems, recv_sems):
    x, y, c = _me()
    me = 4 * x + 2 * y + c
    copies = []
    for mask in range(1, N_DEV):
        cp = pltpu.make_async_remote_copy(
            src_ref=src_ref, dst_ref=dst_ref.at[me], send_sem=send_sems.at[mask - 1],
            recv_sem=recv_sems.at[mask - 1], device_id=_peer(x, y, c, mask), device_id_type=MESH)
        cp.start()
        copies.append(cp)
    dst_ref[me] = src_ref[...]
    for cp in copies:
        cp.wait_recv()
    for cp in copies:
        cp.wait_send()


def _ada_fwd(c_in, w_ada, b_ada):
    ncol = ADA_COLS // N_CHIPS

    def body(c_ref, w_ref, b_ref, mod_ref, sc_ref, cbuf, cg, mbuf, mg, s1, r1, s2, r2):
        x, y, c = _me()
        me = 4 * x + 2 * y + c
        cv = c_ref[...]
        cbuf[...] = jnp.broadcast_to(cv * _sigmoid(cv), cbuf.shape)
        _gather8(cbuf, cg, s1, r1)
        rows = lax.broadcasted_iota(I32, (N_DEV, D_MODEL), 0)
        sc_all = jnp.zeros((N_DEV, D_MODEL), F32)
        for d in range(N_DEV):
            sc_all = jnp.where(rows == d, cg[d], sc_all)
        sc_ref[...] = sc_all
        mbuf[...] = jnp.dot(sc_all.astype(BF16), w_ref[...].astype(BF16), preferred_element_type=F32)
        _gather8(mbuf, mg, s2, r2)
        rowsel = lax.broadcasted_iota(I32, (N_DEV, ncol), 0) == me
        for k in range(N_CHIPS):
            blk = mg[2 * k]
            row = jnp.sum(jnp.where(rowsel, blk, 0.0), axis=0, keepdims=True)
            mod_ref[:, k * ncol:(k + 1) * ncol] = row + b_ref[:, k * ncol:(k + 1) * ncol]

    vm = pl.BlockSpec(memory_space=pltpu.VMEM)
    return pl.pallas_call(
        body, name="ada_fwd",
        in_specs=[vm, vm, vm], out_specs=[vm, vm],
        out_shape=[jax.ShapeDtypeStruct((1, ADA_COLS), F32), jax.ShapeDtypeStruct((N_DEV, D_MODEL), F32)],
        scratch_shapes=[
            pltpu.VMEM((8, D_MODEL), F32), pltpu.VMEM((N_DEV, 8, D_MODEL), F32),
            pltpu.VMEM((8, ncol), F32), pltpu.VMEM((N_DEV, 8, ncol), F32),
            pltpu.SemaphoreType.DMA((N_DEV - 1,)), pltpu.SemaphoreType.DMA((N_DEV - 1,)),
            pltpu.SemaphoreType.DMA((N_DEV - 1,)), pltpu.SemaphoreType.DMA((N_DEV - 1,)),
        ],
        compiler_params=pltpu.CompilerParams(vmem_limit_bytes=VMEM_LIMIT_V7X),
    )(c_in, w_ada, b_ada)


def _small_reduce(pack, sc_all, after=()):
    ncol = ADA_COLS // N_CHIPS

    def body(p_ref, sc_ref, *rest):
        tot_ref, gw_ref, pg, s1, r1 = rest[len(after):]
        x, y, _ = _me()
        chip = 2 * x + y
        _gather8(p_ref, pg, s1, r1)
        tot = pg[0]
        for d in range(1, N_DEV):
            tot = tot + pg[d]
        tot_ref[...] = tot
        rows = lax.broadcasted_iota(I32, (N_DEV, ncol), 0)
        dmod = jnp.zeros((N_DEV, ncol), F32)
        for k in range(N_CHIPS):
            part = jnp.zeros((N_DEV, ncol), F32)
            for d in range(N_DEV):
                part = jnp.where(rows == d, pg[d, :, k * ncol:(k + 1) * ncol][0:1, :], part)
            dmod = jnp.where(chip == k, part, dmod)
        gw_ref[...] = lax.dot_general(sc_ref[...].astype(BF16), dmod.astype(BF16), _TN,
                                      preferred_element_type=F32)

    vm = pl.BlockSpec(memory_space=pltpu.VMEM)
    return pl.pallas_call(
        body, name="small_reduce",
        in_specs=[vm, vm] + [pl.BlockSpec(memory_space=pl.ANY)] * len(after), out_specs=[vm, vm],
        out_shape=[jax.ShapeDtypeStruct((8, ADA_COLS), F32), jax.ShapeDtypeStruct((D_MODEL, ncol), F32)],
        scratch_shapes=[pltpu.VMEM((N_DEV, 8, ADA_COLS), F32),
                        pltpu.SemaphoreType.DMA((N_DEV - 1,)), pltpu.SemaphoreType.DMA((N_DEV - 1,))],
        compiler_params=pltpu.CompilerParams(vmem_limit_bytes=VMEM_LIMIT_V7X),
    )(pack, sc_all, *after)


BIG = (("w_in", 1), ("w_ret_out", 0), ("w_att_out", 1), ("w_o", 0), ("w_ff1", 1), ("w_ff2", 0))
SHARD = {"w_in": (D_MODEL, IN_COLS // N_CHIPS), "w_ret_out": (RET_V_W // N_CHIPS, D_MODEL),
         "w_att_out": (ATT_W, D_MODEL // N_CHIPS), "w_o": (D_MODEL // N_CHIPS, D_MODEL),
         "w_ff1": (D_MODEL, D_FF // N_CHIPS), "w_ff2": (D_FF // N_CHIPS, D_MODEL)}
_CHIP_FLIPS = ((1, 0), (0, 1), (1, 1))


def _region(ref, axis, chip, half, shard_shape):
    r, cw = shard_shape
    hr = r // 2
    if axis == 1:
        return ref.at[pl.ds(half * hr, hr), pl.ds(chip * cw, cw)]
    return ref.at[pl.ds(chip * r + half * hr, hr), :]


CAST_ROWS = 128


def _gather_weights(shards, n_remote):
    nw = len(BIG)
    shapes = [s.shape for s in shards]
    full_shapes = [(r, N_CHIPS * cw) if ax == 1 else (N_CHIPS * r, cw)
                   for (r, cw), (_, ax) in zip(shapes, BIG)]

    def body(*refs):
        ins, outs = refs[:nw], refs[nw:2 * nw]
        own = refs[2 * nw:3 * nw]
        from_ici, from_sib = refs[3 * nw:3 * nw + n_remote], refs[3 * nw + n_remote:3 * nw + 2 * n_remote]
        ld_sem, st_sem, s_ici, r_ici, s_d2d, r_d2d, st_a, st_b, stage = refs[3 * nw + 2 * n_remote:]
        x, y, c = _me()
        chip = 2 * x + y
        sib = (x, y, 1 - c)
        loads = [pltpu.make_async_copy(ins[i], stage if i == 0 else own[i], ld_sem.at[i])
                 for i in range(nw)]
        for cp in loads:
            cp.start()
        pending, first = [], []
        for i, (_, ax) in enumerate(BIG):
            r, cw = shapes[i]
            hr = r // 2
            loads[i].wait()
            if i == 0:
                for r0 in range(0, r, CAST_ROWS):
                    own[0][r0:r0 + CAST_ROWS, :] = stage[r0:r0 + CAST_ROWS, :].astype(BF16)
            dst = outs[i].at[:, pl.ds(chip * cw, cw)] if ax == 1 else outs[i].at[pl.ds(chip * r, r), :]
            cp = pltpu.make_async_copy(own[i], dst, st_sem.at[i])
            cp.start()
            pending.append(cp)
            for j, (fx, fy) in enumerate(_CHIP_FLIPS if i < n_remote else ()):
                rc = pltpu.make_async_remote_copy(
                    src_ref=own[i].at[pl.ds(c * hr, hr), :], dst_ref=from_ici[i].at[j],
                    send_sem=s_ici.at[j * nw + i], recv_sem=r_ici.at[j * nw + i],
                    device_id=(x ^ fx, y ^ fy, c), device_id_type=MESH)
                rc.start()
                first.append((j, i, rc))
        passed = []
        for j, i, rc in first:
            fx, fy = _CHIP_FLIPS[j]
            src_chip = 2 * (x ^ fx) + (y ^ fy)
            ax = BIG[i][1]
            rc.wait_recv()
            fw = pltpu.make_async_remote_copy(
                src_ref=from_ici[i].at[j], dst_ref=from_sib[i].at[j], send_sem=s_d2d.at[j * nw + i],
                recv_sem=r_d2d.at[j * nw + i], device_id=sib, device_id_type=MESH)
            fw.start()
            passed.append((j, i, src_chip, fw))
            st = pltpu.make_async_copy(from_ici[i].at[j], _region(outs[i], ax, src_chip, c, shapes[i]),
                                       st_a.at[j * nw + i])
            st.start()
            pending.append(st)
        for j, i, src_chip, fw in passed:
            fw.wait_recv()
            st = pltpu.make_async_copy(from_sib[i].at[j],
                                       _region(outs[i], BIG[i][1], src_chip, 1 - c, shapes[i]),
                                       st_b.at[j * nw + i])
            st.start()
            pending.append(st)
        for _, _, rc in first:
            rc.wait_send()
        for _, _, _, fw in passed:
            fw.wait_send()
        for cp in pending:
            cp.wait()

    hbm = pl.BlockSpec(memory_space=pl.ANY)
    halves = [pltpu.VMEM((3, r // 2, cw), BF16) for r, cw in shapes[:n_remote]]
    return pl.pallas_call(
        body, name="gather_weights",
        in_specs=[hbm] * nw, out_specs=[hbm] * nw,
        out_shape=[jax.ShapeDtypeStruct(fs, BF16) for fs in full_shapes],
        scratch_shapes=[pltpu.VMEM(sh, BF16) for sh in shapes] + halves + halves
        + [pltpu.SemaphoreType.DMA((nw,)), pltpu.SemaphoreType.DMA((nw,))]
        + [pltpu.SemaphoreType.DMA((3 * nw,))] * 6 + [pltpu.VMEM(shapes[0], F32)],
        compiler_params=pltpu.CompilerParams(vmem_limit_bytes=VMEM_LIMIT_V7X),
    )(*shards)


REST = BIG[1:]
_SIDE_EFFECTS = pltpu.CompilerParams(has_side_effects=pltpu.SideEffectType.DATAFLOW_SIDE_EFFECTING)
_ANY_SPEC = pl.BlockSpec(memory_space=pl.ANY)


def _rest_ici_copies(shard_refs, full_refs, sems):
    x, y, c = _me()
    chip = 2 * x + y
    n = 3 * len(REST)
    copies = []
    for i, (name, ax) in enumerate(REST):
        hr = SHARD[name][0] // 2
        for j, (fx, fy) in enumerate(_CHIP_FLIPS):
            copies.append(pltpu.make_async_remote_copy(
                src_ref=shard_refs[i].at[pl.ds(c * hr, hr), :],
                dst_ref=_region(full_refs[i], ax, chip, c, SHARD[name]),
                send_sem=sems[3 * i + j], recv_sem=sems[n + 3 * i + j],
                device_id=(x ^ fx, y ^ fy, c), device_id_type=MESH))
    return copies


def _rest_d2d_copies(full_refs, sems):
    x, y, c = _me()
    n = 3 * len(REST)
    copies = []
    for i, (name, ax) in enumerate(REST):
        for j, (fx, fy) in enumerate(_CHIP_FLIPS):
            reg = _region(full_refs[i], ax, 2 * (x ^ fx) + (y ^ fy), c, SHARD[name])
            copies.append(pltpu.make_async_remote_copy(
                src_ref=reg, dst_ref=reg, send_sem=sems[3 * i + j], recv_sem=sems[n + 3 * i + j],
                device_id=(x, y, 1 - c), device_id_type=MESH))
    return copies


def _gather_rest_start(shards, fulls, after):
    nr, ns, na = len(REST), 6 * len(REST), len(after)

    def body(*refs):
        for cp in _rest_ici_copies(refs[:nr], refs[nr:2 * nr], refs[2 * nr + na:2 * nr + na + ns]):
            cp.start()
        token = refs[-1]
        token[...] = jnp.zeros_like(token)

    hbm = lambda a: pltpu.HBM(a.shape, a.dtype)
    res = pl.pallas_call(
        body, name="gather_rest_start",
        out_shape=(pltpu.SemaphoreType.DMA(()),) * ns + tuple(hbm(a) for a in shards + fulls)
        + (jax.ShapeDtypeStruct((8, 128), F32),),
        in_specs=(_HBM_SPEC,) * (2 * nr) + (_ANY_SPEC,) * na,
        out_specs=(_SEM_SPEC,) * ns + (_HBM_SPEC,) * (2 * nr) + (pl.BlockSpec(memory_space=pltpu.VMEM),),
        input_output_aliases={k: ns + k for k in range(2 * nr)}, compiler_params=_SIDE_EFFECTS,
    )(*[pltpu.with_memory_space_constraint(a, pltpu.HBM) for a in shards + fulls], *after)
    return res[:ns], res[ns:ns + nr], res[ns + nr:ns + 2 * nr], res[-1]


def _gather_rest_forward(sems, shards, fulls, after):
    nr, ns = len(REST), 6 * len(REST)

    def body(*refs):
        shard_refs, full_refs, old = refs[:nr], refs[nr:2 * nr], refs[2 * nr:2 * nr + ns]
        new = refs[2 * nr + ns + len(after):2 * nr + 2 * ns + len(after)]
        for cp in _rest_ici_copies(shard_refs, full_refs, old):
            cp.wait_send()
            cp.wait_recv()
        for cp in _rest_d2d_copies(full_refs, new):
            cp.start()
        token = refs[-1]
        token[...] = jnp.zeros_like(token)

    res = pl.pallas_call(
        body, name="gather_rest_forward",
        out_shape=(pltpu.SemaphoreType.DMA(()),) * ns + tuple(pltpu.HBM(a.shape, a.dtype) for a in fulls)
        + (jax.ShapeDtypeStruct((8, 128), F32),),
        in_specs=(_HBM_SPEC,) * (2 * nr) + (_SEM_SPEC,) * ns + (_ANY_SPEC,) * len(after),
        out_specs=(_SEM_SPEC,) * ns + (_HBM_SPEC,) * nr + (pl.BlockSpec(memory_space=pltpu.VMEM),),
        input_output_aliases={nr + k: ns + k for k in range(nr)}, compiler_params=_SIDE_EFFECTS,
    )(*shards, *fulls, *sems, *after)
    return res[:ns], res[ns:ns + nr], res[-1]


def _gather_rest_end(sems, fulls, after):
    nr, ns = len(REST), 6 * len(REST)

    def body(*refs):
        for cp in _rest_d2d_copies(refs[:nr], refs[nr:nr + ns]):
            cp.wait_send()
            cp.wait_recv()

    return pl.pallas_call(
        body, name="gather_rest_end",
        out_shape=tuple(pltpu.HBM(a.shape, a.dtype) for a in fulls),
        in_specs=(_HBM_SPEC,) * nr + (_SEM_SPEC,) * ns + (_ANY_SPEC,) * len(after),
        out_specs=(_HBM_SPEC,) * nr,
        input_output_aliases={k: k for k in range(nr)}, compiler_params=_SIDE_EFFECTS,
    )(*fulls, *sems, *after)


def _adam_update(w, g, m, v):
    mn = ADAM_B1 * m + (1.0 - ADAM_B1) * g
    vn = ADAM_B2 * v + (1.0 - ADAM_B2) * (g * g)
    m_hat = mn / (1.0 - ADAM_B1 ** ADAM_STEP)
    v_hat = vn / (1.0 - ADAM_B2 ** ADAM_STEP)
    return -ADAM_LR * (m_hat / (jnp.sqrt(v_hat) + ADAM_EPS) + ADAM_WD * w), mn, vn


def _final_sum(name, pos, axis, psum, recv, shard_shape, after=(), tr=128):
    r, cw = shard_shape
    hr = r // 2
    tr = min(tr, hr)
    nt = hr // tr
    n_after = len(after)

    def kern(pos_ref, p_ref, r_ref, *rest):
        g_ref, send_buf, land_buf, s_sem, r_sem = rest[n_after:]
        p, t = pl.program_id(0), pl.program_id(1)
        sib = _sibling()

        def copy(i):
            return pltpu.make_async_remote_copy(
                src_ref=send_buf.at[i], dst_ref=land_buf.at[i], send_sem=s_sem.at[i],
                recv_sem=r_sem.at[i], device_id=sib, device_id_type=MESH)

        @pl.when(p == 0)
        def _():
            tot = p_ref[...].astype(F32)
            for j in range(3):
                tot = tot + r_ref[j].astype(F32)
            send_buf[t] = tot
            copy(t).start()
            g_ref[...] = tot

        @pl.when(p == 1)
        def _():
            copy(t).wait_recv()
            g_ref[...] = land_buf[t]

        @pl.when(jnp.logical_and(p == 1, t == nt - 1))
        def _():
            for i in range(nt):
                copy(i).wait_send()

    def shard_rows(p, t, pos_ref):
        return (jnp.where(p == 0, pos_ref[0], 1 - pos_ref[0]) * nt + t, 0)

    def own_part(p, t, pos_ref):
        tt = jnp.where(p == 0, t, nt - 1)
        return (tt, pos_ref[1]) if axis == 1 else (pos_ref[1] * nt + tt, 0)

    grid_spec = pltpu.PrefetchScalarGridSpec(
        num_scalar_prefetch=1, grid=(2, nt),
        in_specs=[pl.BlockSpec((tr, cw), own_part),
                  pl.BlockSpec((3, tr, cw), lambda p, t, pos_ref: (0, jnp.where(p == 0, t, nt - 1), 0))]
        + [pl.BlockSpec(memory_space=pl.ANY)] * n_after,
        out_specs=pl.BlockSpec((tr, cw), shard_rows),
        scratch_shapes=[pltpu.VMEM((nt, tr, cw), F32), pltpu.VMEM((nt, tr, cw), F32),
                        pltpu.SemaphoreType.DMA((nt,)), pltpu.SemaphoreType.DMA((nt,))])
    return pl.pallas_call(
        kern, name=name, grid_spec=grid_spec, out_shape=jax.ShapeDtypeStruct((r, cw), F32),
        compiler_params=_cparams(("arbitrary", "arbitrary")),
    )(pos, psum, recv, *after)


def _adamw(name, w, g, m, v):
    r, cw = w.shape
    tr = min(r, 128)

    def kern(w_ref, g_ref, m_ref, v_ref, go_ref, d_ref, nm_ref, nv_ref):
        gv = g_ref[...]
        go_ref[...] = gv
        d_ref[...], nm_ref[...], nv_ref[...] = _adam_update(w_ref[...], gv, m_ref[...], v_ref[...])

    spec = pl.BlockSpec((tr, cw), lambda i: (i, 0))
    return pl.pallas_call(
        kern, name=name, grid=(r // tr,), in_specs=[spec] * 4, out_specs=[spec] * 4,
        out_shape=[jax.ShapeDtypeStruct((r, cw), F32)] * 4, compiler_params=_cparams(("parallel",)),
    )(w, g, m, v)


_PACK_W = ADA_COLS
_NB = REL_BUCKETS * N_ATT_HEADS
_SMALL_SLOTS = {
    "b_ada": (0, 0, ADA_COLS),
    "norm1_g": (1, 0, D_MODEL), "norm2_g": (1, D_MODEL, D_MODEL), "norm_f_g": (1, 2 * D_MODEL, D_MODEL),
    "ret_gn_g": (1, 3 * D_MODEL, RET_V_W),
    "ret_gn_b": (2, 0, RET_V_W), "rel_bias": (2, RET_V_W, _NB), "loss": (2, RET_V_W + 512, 128),
}


def _pack_small(vals):
    rows = []
    for r in range(8):
        items = sorted([(off, n) for n, (rr, off, _) in _SMALL_SLOTS.items() if rr == r and n in vals])
        parts, pos = [], 0
        for off, n in items:
            if off > pos:
                parts.append(jnp.zeros((1, off - pos), F32))
            parts.append(vals[n].reshape(1, -1).astype(F32))
            pos = off + _SMALL_SLOTS[n][2]
        if pos < _PACK_W:
            parts.append(jnp.zeros((1, _PACK_W - pos), F32))
        rows.append(jnp.concatenate(parts, axis=-1))
    return jnp.concatenate(rows, axis=0)


def _adamw_small(tot, names, wmv):
    n = len(names)

    def kern(tot_ref, *refs):
        ins, outs = refs[:3 * n], refs[3 * n:]
        for i, name in enumerate(names):
            row, off, width = _SMALL_SLOTS[name]
            g = tot_ref[row:row + 1, off:off + width]
            outs[i][...] = g
            outs[n + i][...], outs[2 * n + i][...], outs[3 * n + i][...] = _adam_update(
                ins[i][...], g, ins[n + i][...], ins[2 * n + i][...])

    vm = pl.BlockSpec(memory_space=pltpu.VMEM)
    shapes = [jax.ShapeDtypeStruct((1, _SMALL_SLOTS[name][2]), F32) for name in names]
    res = pl.pallas_call(
        kern, name="adamw_small", in_specs=[vm] * (1 + 3 * n), out_specs=[vm] * (4 * n),
        out_shape=shapes * 4,
    )(tot, *wmv[0], *wmv[1], *wmv[2])
    return res[:n], res[n:2 * n], res[2 * n:3 * n], res[3 * n:]


def _unpack_small(pack, name):
    r, off, wd = _SMALL_SLOTS[name]
    return pack[r:r + 1, off:off + wd]


def kernel(x, c, w_ada, b_ada, norm1_g, w_in, rel_bias, ret_gn_g, ret_gn_b, w_ret_out, w_att_out, w_o, norm2_g, w_ff1, w_ff2, norm_f_g, loss_target, m_w_ada, m_b_ada, m_norm1_g, m_w_in, m_rel_bias, m_ret_gn_g, m_ret_gn_b, m_w_ret_out, m_w_att_out, m_w_o, m_norm2_g, m_w_ff1, m_w_ff2, m_norm_f_g, v_w_ada, v_b_ada, v_norm1_g, v_w_in, v_rel_bias, v_ret_gn_g, v_ret_gn_b, v_w_ret_out, v_w_att_out, v_w_o, v_norm2_g, v_w_ff1, v_w_ff2, v_norm_f_g):
    given = dict(locals())
    big_names = [n for n, _ in BIG]
    shard_w = {n: given[n][0] for n in big_names}
    assert all(shard_w[n].shape == SHARD[n] for n in big_names)

    shards_bf = [None] + [shard_w[n].astype(BF16) for n in big_names[1:]]
    full = _gather_weights([shard_w["w_in"]] + shards_bf[1:], 1)
    mod, sc_all = _ada_fwd(c, w_ada[0], b_ada)
    rest_gather = _gather_rest_start(shards_bf[1:], list(full[1:]), [mod])
    pos = _where_am_i()

    loss, grad_x, d_mod, small, g_big, pending = _local_step(
        pos, x[0], loss_target[0], mod, norm1_g, norm2_g, norm_f_g.reshape(1, -1), rel_bias, ret_gn_g,
        ret_gn_b, full[0], rest_gather)

    pack_g = _pack_small(dict(b_ada=d_mod, norm1_g=small["norm1_g"], norm2_g=small["norm2_g"],
                              norm_f_g=small["norm_f_g"], ret_gn_g=small["gn_g"], ret_gn_b=small["gn_b"],
                              rel_bias=small["rel_bias"], loss=loss[:, :128]))
    tot, g_w_ada = _small_reduce(pack_g, sc_all, after=list(g_big.values()))

    small_names = ["b_ada", "norm1_g", "rel_bias", "ret_gn_g", "ret_gn_b", "norm2_g", "norm_f_g"]
    small_out = _adamw_small(tot, small_names, [[given[p + n].reshape(1, -1) for n in small_names]
                                                for p in ("", "m_", "v_")])
    grads, deltas, new_m, new_v = ({n: t.reshape(given[n].shape) for n, t in zip(small_names, group)}
                                   for group in small_out)
    sd = deltas["b_ada"]
    g_big["w_ada"] = g_w_ada
    for n in ["w_ada"] + big_names[1:] + big_names[:1]:
        if n == "w_in":
            gw_in, sems, land = pending
            done = [tot, sd] + [deltas[k] for k in ["w_ada"] + big_names[1:]]
            (gw_in,), (got,) = _ici_wait("ici_wait_w_in", [n], sems, [gw_in], [land], done)
            g_big[n] = _final_sum("final_w_in", pos, 1, gw_in, got, SHARD[n])
        g, d, nm, nv = _adamw("adamw_" + n, given[n][0], g_big[n], given["m_" + n][0], given["v_" + n][0])
        grads[n], deltas[n], new_m[n], new_v[n] = g[None], d[None], nm[None], nv[None]

    order = ["w_ada", "b_ada", "norm1_g", "w_in", "rel_bias", "ret_gn_g", "ret_gn_b", "w_ret_out",
             "w_att_out", "w_o", "norm2_g", "w_ff1", "w_ff2", "norm_f_g"]
    loss_out = _unpack_small(tot, "loss")[0, 0]
    return (loss_out, grad_x[None], *[grads[n] for n in order], *[deltas[n] for n in order],
            *[new_m[n] for n in order], *[new_v[n] for n in order])
```

```python
import math

import jax
import jax.numpy as jnp
import numpy as np
from jax import lax
from jax.experimental import pallas as pl
from jax.experimental.pallas import tpu as pltpu

F32 = jnp.float32
BF16 = jnp.bfloat16
I32 = jnp.int32

SEQ = 2048
D_MODEL = 1024
RET_HEADS = 4
RET_DK = 256
RET_DV = 512
RET_CHUNK = 128
RET_SUB = 2
RET_QK_W = RET_HEADS * RET_DK
RET_V_W = RET_HEADS * RET_DV
ATT_GROUPS = ((128, 1), (512, 4), (2048, 16))
ATT_HPG = 4
ATT_DH = 128
ATT_W = ATT_HPG * ATT_DH
ATT_BLK = 128
REL_BUCKETS = 32
REL_MAX_DIST = 2048
N_ATT_HEADS = 12
D_FF = 4 * D_MODEL
RMS_EPS = 1e-6
GN_EPS = 1e-5
ROPE_BASE = 10000.0
IN_COLS = 2 * RET_QK_W + 2 * RET_V_W + 9 * ATT_W + 2 * D_MODEL
OFF_Q, OFF_K, OFF_V, OFF_G = 0, RET_QK_W, 2 * RET_QK_W, 2 * RET_QK_W + RET_V_W
OFF_ATT = 2 * RET_QK_W + 2 * RET_V_W
OFF_GATE = OFF_ATT + 9 * ATT_W
N_CHIPS = 4
N_DEV = 8
ADA_COLS = 6 * D_MODEL

ADAM_LR = 0.001
ADAM_B1 = 0.9
ADAM_B2 = 0.999
ADAM_EPS = 1e-08
ADAM_WD = 0.01
ADAM_STEP = 10

VMEM_LIMIT_V7X = 56 * 1024 * 1024
MESH = pl.DeviceIdType.MESH


def _cparams(sem):
    return pltpu.CompilerParams(dimension_semantics=sem, vmem_limit_bytes=VMEM_LIMIT_V7X)


def _sigmoid(v):
    return 1.0 / (1.0 + jnp.exp(-v))


TM, TN = 1024, 1024


def _piece_chunks(piece, width):
    arr, stacked = piece
    return arr.shape[0] if stacked else arr.shape[1] // width


def _piece_spec(piece, rows, width, start, row_of, chunk_of):
    arr, stacked = piece
    last = _piece_chunks(piece, width) - 1

    def local(*ids):
        return jnp.clip(chunk_of(*ids) - start, 0, last)

    def row(*ids):
        rel = chunk_of(*ids) - start
        return jnp.where(jnp.logical_and(rel >= 0, rel <= last), row_of(*ids), 0)

    if stacked:
        return pl.BlockSpec((None, rows, width), lambda *ids: (local(*ids), row(*ids), 0))
    return pl.BlockSpec((rows, width), lambda *ids: (row(*ids), local(*ids)))


def _piece_starts(pieces, width):
    return [sum(_piece_chunks(p, width) for p in pieces[:q]) for q in range(len(pieces))]


def _matmul(name, a, b, kind, m, n, k, outs, *, b_off=0, tm=TM, tn=TN, tk=1024,
            epilogue=None, extras=(), after=(), n_sums=0):
    tm, tn, tk = min(tm, m), min(tn, n), min(tk, k)
    nk = k // tk
    pieces = a if isinstance(a, list) else [(a, False)]
    starts = _piece_starts(pieces, tk)
    if kind == "nn":
        a_specs = [pl.BlockSpec((tm, tk), lambda i, j, kk: (i, kk))]
        b_spec = pl.BlockSpec((tk, tn), lambda i, j, kk: (kk, b_off // tn + j))
        dn = (((1,), (0,)), ((), ()))
    elif kind == "nt":
        a_specs = [_piece_spec(p, tm, tk, st, lambda i, j, kk: i, lambda i, j, kk: kk)
                   for p, st in zip(pieces, starts)]
        b_spec = pl.BlockSpec((tn, tk), lambda i, j, kk: (j, b_off // tk + kk))
        dn = (((1,), (1,)), ((), ()))
    else:
        a_specs = [pl.BlockSpec((tk, tm), lambda i, j, kk: (kk, i))]
        b_spec = pl.BlockSpec((tk, tn), lambda i, j, kk: (kk, j))
        dn = (((0,), (0,)), ((), ()))
    n_a, n_ex, n_out = len(pieces), len(extras), len(outs)
    if epilogue is None:
        epilogue = lambda acc: (acc,)

    assert n_sums == 0 or tn == n

    def finish(acc, ex_refs, out_refs, first_rows):
        res = epilogue(acc, *[r[...] for r in ex_refs])
        for r, v in zip(out_refs[:n_out], res[:n_out]):
            r[...] = v.astype(r.dtype)
        for r, v in zip(out_refs[n_out:], res[n_out:]):
            @pl.when(first_rows)
            def _(r=r, v=v):
                r[...] = v

            @pl.when(jnp.logical_not(first_rows))
            def _(r=r, v=v):
                r[...] += v

    n_in = n_a + 1 + n_ex + len(after)

    def kern(*refs):
        a_refs, b_ref = refs[:n_a], refs[n_a]
        ex_refs = refs[n_a + 1:n_a + 1 + n_ex]
        out_refs = refs[n_in:n_in + n_out + n_sums]
        first_rows, kk = pl.program_id(0) == 0, pl.program_id(2)
        dot = lambda a_ref: lax.dot_general(a_ref[...], b_ref[...], dn, preferred_element_type=F32)
        if nk == 1:
            finish(dot(a_refs[0]), ex_refs, out_refs, first_rows)
            return
        acc_ref = refs[n_in + n_out + n_sums]
        if n_a == 1:
            part = dot(a_refs[0])

            @pl.when(kk == 0)
            def _():
                acc_ref[...] = part

            @pl.when(kk > 0)
            def _():
                acc_ref[...] += part
        else:
            @pl.when(kk == 0)
            def _():
                acc_ref[...] = jnp.zeros_like(acc_ref)

            for q in range(n_a):
                @pl.when(jnp.logical_and(kk >= starts[q], kk < starts[q] + _piece_chunks(pieces[q], tk)))
                def _(q=q):
                    acc_ref[...] += dot(a_refs[q])

        @pl.when(kk == nk - 1)
        def _():
            finish(acc_ref[...], ex_refs, out_refs, first_rows)

    in_specs = a_specs + [b_spec] + [pl.BlockSpec(bs, im) for _, bs, im in extras]
    in_specs += [pl.BlockSpec(memory_space=pl.ANY)] * len(after)
    sem = ("arbitrary",) * 3 if n_sums else ("parallel", "parallel", "arbitrary")
    return pl.pallas_call(
        kern, name=name, grid=(m // tm, n // tn, nk), in_specs=in_specs,
        out_specs=[pl.BlockSpec((tm, tn), lambda i, j, kk: (i, j)) for _ in outs]
        + [pl.BlockSpec((1, tn), lambda i, j, kk: (0, 0))] * n_sums,
        out_shape=[jax.ShapeDtypeStruct((m, n), dt) for dt in outs]
        + [jax.ShapeDtypeStruct((1, n), F32)] * n_sums,
        scratch_shapes=[] if nk == 1 else [pltpu.VMEM((tm, tn), F32)],
        compiler_params=_cparams(sem),
    )(*[p[0] for p in pieces], b, *[e[0] for e in extras], *after)


def _ici_copies(psum_ref, recv_ref, s_sem, r_sem, axis, shard_shape):
    x, y, c = _me()
    hr, cw = shard_shape[0] // 2, shard_shape[1]
    pick = lambda sems, j: sems[j] if isinstance(sems, (list, tuple)) else sems.at[j]
    copies = []
    for j, (fx, fy) in enumerate(_CHIP_FLIPS):
        chip = 2 * (x ^ fx) + (y ^ fy)
        src = psum_ref.at[:, pl.ds(chip * cw, cw)] if axis == 1 else psum_ref.at[pl.ds(chip * hr, hr), :]
        copies.append(pltpu.make_async_remote_copy(
            src_ref=src, dst_ref=recv_ref.at[j], send_sem=pick(s_sem, j), recv_sem=pick(r_sem, j),
            device_id=(x ^ fx, y ^ fy, c), device_id_type=MESH))
    return copies


_HBM_SPEC = pl.BlockSpec(memory_space=pltpu.HBM)
_SEM_SPEC = pl.BlockSpec(memory_space=pltpu.SEMAPHORE)


def _split_ici_copies(names, p_refs, land_refs, sems):
    copies = []
    for i, n in enumerate(names):
        copies += _ici_copies(p_refs[i], land_refs[i], list(sems[6 * i:6 * i + 3]),
                              list(sems[6 * i + 3:6 * i + 6]), dict(BIG)[n], SHARD[n])
    return copies


def _ici_start(name, names, psums):
    nw, ns = len(names), 6 * len(names)
    lands = [lax.empty((3, SHARD[n][0] // 2, SHARD[n][1]), BF16) for n in names]

    def body(*refs):
        for cp in _split_ici_copies(names, refs[:nw], refs[nw:2 * nw], refs[2 * nw:2 * nw + ns]):
            cp.start()
        token = refs[-1]
        token[...] = jnp.zeros_like(token)

    res = pl.pallas_call(
        body, name=name,
        out_shape=(pltpu.SemaphoreType.DMA(()),) * ns
        + tuple(pltpu.HBM(a.shape, BF16) for a in list(psums) + lands)
        + (jax.ShapeDtypeStruct((8, 128), F32),),
        in_specs=(_HBM_SPEC,) * (2 * nw),
        out_specs=(_SEM_SPEC,) * ns + (_HBM_SPEC,) * (2 * nw) + (pl.BlockSpec(memory_space=pltpu.VMEM),),
        input_output_aliases={k: ns + k for k in range(2 * nw)},
        compiler_params=pltpu.CompilerParams(has_side_effects=pltpu.SideEffectType.DATAFLOW_SIDE_EFFECTING),
    )(*[pltpu.with_memory_space_constraint(a, pltpu.HBM) for a in list(psums) + lands])
    return res[:ns], res[ns:ns + nw], res[ns + nw:ns + 2 * nw], res[-1]


def _ici_wait(name, names, sems, p_thru, land_thru, after):
    nw, ns = len(names), 6 * len(names)

    def body(*refs):
        for cp in _split_ici_copies(names, refs[:nw], refs[nw:2 * nw], refs[2 * nw:2 * nw + ns]):
            cp.wait_send()
            cp.wait_recv()

    res = pl.pallas_call(
        body, name=name,
        out_shape=tuple(pltpu.HBM(a.shape, BF16) for a in list(p_thru) + list(land_thru)),
        in_specs=(_HBM_SPEC,) * (2 * nw) + (_SEM_SPEC,) * ns + (pl.BlockSpec(memory_space=pl.ANY),) * len(after),
        out_specs=(_HBM_SPEC,) * (2 * nw), input_output_aliases={k: k for k in range(2 * nw)},
        compiler_params=pltpu.CompilerParams(has_side_effects=pltpu.SideEffectType.DATAFLOW_SIDE_EFFECTING),
    )(*p_thru, *land_thru, *sems, *after)
    return res[:nw], res[nw:]


def _where_am_i():
    x, y, c = _me()
    return jnp.stack([c, 2 * x + y]).astype(I32)


def _sibling():
    x, y, c = _me()
    return (x, y, 1 - c)


N_SEND_SLOTS = 2


def _matmul_tn_pair(name, pos, a, b, m, n, k, shard_rows, *, tm, tn, tk):
    hr = shard_rows // 2
    tm, tn, tk = min(tm, hr), min(tn, n), min(tk, k)
    tph = hr // tm
    nt, nj, nk = (m // 2) // tm, n // tn, k // tk
    n_tiles = nt * nj

    def row_block(p, t, pos_ref):
        half = jnp.where(p == 0, 1 - pos_ref[0], pos_ref[0])
        return (t // tph) * (2 * tph) + half * tph + t % tph

    pieces = b if isinstance(b, list) else [(b, False)]
    starts = _piece_starts(pieces, tn)
    n_b = len(pieces)

    def kern(pos_ref, a_ref, *rest):
        b_refs = rest[:n_b]
        o_ref, acc_ref, send_buf, land_buf, s_sem, r_sem = rest[n_b:]
        p, t, j, kk = pl.program_id(0), pl.program_id(1), pl.program_id(2), pl.program_id(3)
        idx = t * nj + j
        sib = _sibling()

        def copy(i):
            return pltpu.make_async_remote_copy(
                src_ref=send_buf.at[i % N_SEND_SLOTS], dst_ref=land_buf.at[i], send_sem=s_sem.at[i],
                recv_sem=r_sem.at[i], device_id=sib, device_id_type=MESH)

        @pl.when(kk == 0)
        def _():
            acc_ref[...] = jnp.zeros_like(acc_ref)

        for q in range(n_b):
            @pl.when(jnp.logical_and(j >= starts[q], j < starts[q] + _piece_chunks(pieces[q], tn)))
            def _(q=q):
                acc_ref[...] += lax.dot_general(a_ref[...], b_refs[q][...], _TN, preferred_element_type=F32)

        @pl.when(jnp.logical_and(kk == nk - 1, p == 0))
        def _():
            @pl.when(idx >= N_SEND_SLOTS)
            def _():
                copy(idx - N_SEND_SLOTS).wait_send()

            send_buf[idx % N_SEND_SLOTS] = acc_ref[...].astype(BF16)
            copy(idx).start()

        @pl.when(jnp.logical_and(kk == nk - 1, p == 1))
        def _():
            copy(idx).wait_recv()
            o_ref[...] = (acc_ref[...] + land_buf[idx].astype(F32)).astype(BF16)

        @pl.when(jnp.logical_and(jnp.logical_and(p == 1, idx == n_tiles - 1), kk == nk - 1))
        def _():
            for i in range(max(n_tiles - N_SEND_SLOTS, 0), n_tiles):
                copy(i).wait_send()

    grid_spec = pltpu.PrefetchScalarGridSpec(
        num_scalar_prefetch=1, grid=(2, nt, nj, nk),
        in_specs=[pl.BlockSpec((tk, tm), lambda p, t, j, kk, pos_ref: (kk, row_block(p, t, pos_ref)))]
        + [_piece_spec(pc, tk, tn, st, lambda p, t, j, kk, pos_ref: kk, lambda p, t, j, kk, pos_ref: j)
           for pc, st in zip(pieces, starts)],
        out_specs=pl.BlockSpec((tm, tn), lambda p, t, j, kk, pos_ref: (p * t, p * j)),
        scratch_shapes=[pltpu.VMEM((tm, tn), F32), pltpu.VMEM((N_SEND_SLOTS, tm, tn), BF16),
                        pltpu.VMEM((n_tiles, tm, tn), BF16),
                        pltpu.SemaphoreType.DMA((n_tiles,)), pltpu.SemaphoreType.DMA((n_tiles,))])
    return pl.pallas_call(
        kern, name=name, grid_spec=grid_spec, out_shape=jax.ShapeDtypeStruct((m // 2, n), BF16),
        compiler_params=_cparams(("arbitrary",) * 4),
    )(pos, a, *[pc[0] for pc in pieces])


def _rope_tables():
    half = RET_DK // 2
    f32 = np.float32
    inv = np.power(f32(ROPE_BASE), -np.arange(half, dtype=f32) / f32(half)).astype(f32)
    ang = (np.arange(SEQ, dtype=f32)[:, None] * inv[None, :]).astype(f32)
    return jnp.asarray(np.cos(ang).astype(f32)), jnp.asarray(np.sin(ang).astype(f32))


def _decay_tables():
    c = RET_CHUNK
    f32 = np.float32
    log_g = np.log1p(-np.power(f32(2.0), f32(-5.0) - np.arange(RET_HEADS, dtype=f32))).astype(f32)
    idx = np.arange(c, dtype=f32)
    rel = idx[:, None] - idx[None, :]
    din = np.where(rel >= 0, np.exp(log_g[:, None, None] * np.maximum(rel, f32(0.0))), f32(0.0)).astype(f32)
    qd = np.exp(log_g[:, None] * (idx + f32(1.0))).astype(f32)[:, :, None]
    kd = np.exp(log_g[:, None] * (f32(c) - f32(1.0) - idx)).astype(f32)[:, :, None]
    cd = np.exp(log_g * f32(c)).astype(f32)
    return jnp.asarray(din), jnp.asarray(qd), jnp.asarray(kd), jnp.asarray(cd)


def _t5_bucket(dist):
    max_exact = REL_BUCKETS // 2
    d_f = jnp.maximum(dist, 1).astype(F32)
    large = max_exact + (jnp.log(d_f / max_exact) / math.log(REL_MAX_DIST / max_exact)
                         * (REL_BUCKETS - max_exact)).astype(I32)
    large = jnp.minimum(large, REL_BUCKETS - 1)
    return jnp.where(dist < max_exact, dist, large)


def _bucket_tables():
    qi = jnp.arange(ATT_BLK)[:, None]
    kj = jnp.arange(2 * ATT_BLK)[None, :]
    dist = jnp.clip(ATT_BLK + qi - kj, 0, ATT_BLK)
    return jnp.stack([_t5_bucket(dist * dil) for _, dil in ATT_GROUPS]).astype(I32)


def _retention_fwd(rqk, rv, rg, gn_g, gn_b, din, qd, kd, cd):
    nc = SEQ // RET_CHUNK
    c, dk, dv = RET_CHUNK, RET_DK, RET_DV

    def kern(q_ref, k_ref, v_ref, rg_ref, g_ref, b_ref, din_ref, qd_ref, kd_ref, cd_ref,
             o_ref, st_ref, gated_ref, state):
        n = pl.program_id(0)

        @pl.when(n == 0)
        def _():
            state[...] = jnp.zeros_like(state)

        for sub in range(RET_SUB):
            rows = slice(sub * c, (sub + 1) * c)
            for h in range(RET_HEADS):
                q, k = q_ref[rows, h * dk:(h + 1) * dk], k_ref[rows, h * dk:(h + 1) * dk]
                v = v_ref[rows, h * dv:(h + 1) * dv]
                s_b = state[h].astype(BF16)
                st_ref[h, sub] = s_b
                a = lax.dot_general(q, k, _NT, preferred_element_type=F32) * din_ref[h]
                o = jnp.dot(a.astype(BF16), v, preferred_element_type=F32)
                o += jnp.dot(q, s_b, preferred_element_type=F32) * qd_ref[h]
                v_cols = slice(h * dv, (h + 1) * dv)
                o_ref[rows, v_cols] = o
                nrm, _ = _gn_parts(o)
                gate = rg_ref[rows, v_cols].astype(F32)
                gated_ref[rows, v_cols] = ((gate * _sigmoid(gate))
                                           * (nrm * g_ref[:, v_cols] + b_ref[:, v_cols])).astype(BF16)
                kk = (k.astype(F32) * kd_ref[h]).astype(BF16)
                state[h] = state[h] * cd_ref[h] + lax.dot_general(kk, v, _TN, preferred_element_type=F32)

    whole = lambda a: pl.BlockSpec(a.shape, lambda n: (0,) * a.ndim)
    cs = RET_SUB * c
    rows_v = pl.BlockSpec((cs, RET_V_W), lambda n: (n, 0))
    return pl.pallas_call(
        kern, name="retention_fwd", grid=(nc // RET_SUB,),
        in_specs=[
            pl.BlockSpec((cs, RET_QK_W), lambda n: (n, 0)),
            pl.BlockSpec((cs, RET_QK_W), lambda n: (n, 1)),
            rows_v, rows_v, whole(gn_g), whole(gn_b),
            whole(din), whole(qd), whole(kd),
            pl.BlockSpec(memory_space=pltpu.SMEM),
        ],
        out_specs=[
            rows_v,
            pl.BlockSpec((RET_HEADS, RET_SUB, dk, dv), lambda n: (0, n, 0, 0)),
            rows_v,
        ],
        out_shape=[
            jax.ShapeDtypeStruct((SEQ, RET_V_W), F32),
            jax.ShapeDtypeStruct((RET_HEADS, nc, dk, dv), BF16),
            jax.ShapeDtypeStruct((SEQ, RET_V_W), BF16),
        ],
        scratch_shapes=[pltpu.VMEM((RET_HEADS, dk, dv), F32)],
        compiler_params=_cparams(("arbitrary",)),
    )(rqk, rqk, rv, rg, gn_g, gn_b, din, qd, kd, cd)


def _retention_bwd(rqk, rv, states, d_gated, ro, rg, gn_g, gn_b, din, qd, kd, cd, cos, sin):
    nc = SEQ // RET_CHUNK
    c, dk, dv = RET_CHUNK, RET_DK, RET_DV
    half = dk // 2
    last = nc // RET_SUB - 1

    def unrot(g, cs, sn):
        g1, g2 = g[:, :half], g[:, half:]
        return jnp.concatenate([g1 * cs + g2 * sn, g2 * cs - g1 * sn], axis=-1)

    def kern(q_ref, k_ref, v_ref, st_ref, dg_ref, ro_ref, rg_ref, g_ref, b_ref, din_ref, qd_ref, kd_ref,
             cd_ref, cos_ref, sin_ref, out_ref, drg_ref, dgn_g_ref, dgn_b_ref, dstate):
        step = pl.program_id(0)

        @pl.when(step == 0)
        def _():
            dstate[...] = jnp.zeros_like(dstate)
            dgn_g_ref[...] = jnp.zeros_like(dgn_g_ref)
            dgn_b_ref[...] = jnp.zeros_like(dgn_b_ref)

        for sub in reversed(range(RET_SUB)):
            rows = slice(sub * c, (sub + 1) * c)
            cs, sn = cos_ref[rows, :], sin_ref[rows, :]
            for h in range(RET_HEADS):
                qk_cols, v_cols = slice(h * dk, (h + 1) * dk), slice(h * dv, (h + 1) * dv)
                q, k, v = q_ref[rows, qk_cols], k_ref[rows, qk_cols], v_ref[rows, v_cols]
                s_b = st_ref[h, sub]
                nrm, rstd = _gn_parts(ro_ref[rows, v_cols])
                gate, dg = rg_ref[rows, v_cols].astype(F32), dg_ref[rows, v_cols].astype(F32)
                sg = _sigmoid(gate)
                gn_gain = g_ref[:, v_cols]
                drg_ref[rows, v_cols] = (dg * (nrm * gn_gain + b_ref[:, v_cols])
                                         * (sg * (1.0 + gate * (1.0 - sg)))).astype(BF16)
                d_ron = dg * (gate * sg)
                dgn_g_ref[:, v_cols] += jnp.sum(d_ron * nrm, axis=0, keepdims=True)
                dgn_b_ref[:, v_cols] += jnp.sum(d_ron, axis=0, keepdims=True)
                d_n = d_ron * gn_gain
                d_o = rstd * (d_n - jnp.mean(d_n, axis=-1, keepdims=True)
                              - nrm * jnp.mean(d_n * nrm, axis=-1, keepdims=True))
                d_ob = d_o.astype(BF16)
                d_oq = (d_o * qd_ref[h]).astype(BF16)
                ds_b = dstate[h].astype(BF16)
                din_m = din_ref[h]
                a_b = (lax.dot_general(q, k, _NT, preferred_element_type=F32) * din_m).astype(BF16)
                kk = (k.astype(F32) * kd_ref[h]).astype(BF16)
                d_v = lax.dot_general(a_b, d_ob, _TN, preferred_element_type=F32)
                d_v += jnp.dot(kk, ds_b, preferred_element_type=F32)
                d_a = (lax.dot_general(d_ob, v, _NT, preferred_element_type=F32) * din_m).astype(BF16)
                d_q = jnp.dot(d_a, k, preferred_element_type=F32)
                d_q += lax.dot_general(d_oq, s_b, _NT, preferred_element_type=F32)
                d_k = lax.dot_general(d_a, q, _TN, preferred_element_type=F32)
                d_k += lax.dot_general(v, ds_b, _NT, preferred_element_type=F32) * kd_ref[h]
                dstate[h] = dstate[h] * cd_ref[h] + lax.dot_general(q, d_oq, _TN,
                                                                    preferred_element_type=F32)
                out_ref[rows, h * dk:(h + 1) * dk] = unrot(d_q, cs, sn).astype(BF16)
                out_ref[rows, RET_QK_W + h * dk:RET_QK_W + (h + 1) * dk] = (
                    unrot(d_k, cs, sn) * (RET_DK ** -0.5)).astype(BF16)
                out_ref[rows, 2 * RET_QK_W + h * dv:2 * RET_QK_W + (h + 1) * dv] = d_v.astype(BF16)

    whole = lambda a: pl.BlockSpec(a.shape, lambda n: (0,) * a.ndim)
    rs = RET_SUB * c
    rows_v = pl.BlockSpec((rs, RET_V_W), lambda n: (last - n, 0))
    return pl.pallas_call(
        kern, name="retention_bwd", grid=(nc // RET_SUB,),
        in_specs=[
            pl.BlockSpec((rs, RET_QK_W), lambda n: (last - n, 0)),
            pl.BlockSpec((rs, RET_QK_W), lambda n: (last - n, 1)),
            rows_v,
            pl.BlockSpec((RET_HEADS, RET_SUB, dk, dv), lambda n: (0, last - n, 0, 0)),
            rows_v, rows_v, rows_v, whole(gn_g), whole(gn_b),
            whole(din), whole(qd), whole(kd),
            pl.BlockSpec(memory_space=pltpu.SMEM),
            pl.BlockSpec((rs, half), lambda n: (last - n, 0)),
            pl.BlockSpec((rs, half), lambda n: (last - n, 0)),
        ],
        out_specs=[pl.BlockSpec((rs, 2 * RET_QK_W + RET_V_W), lambda n: (last - n, 0)), rows_v,
                   whole(gn_g), whole(gn_b)],
        out_shape=[jax.ShapeDtypeStruct((SEQ, 2 * RET_QK_W + RET_V_W), BF16),
                   jax.ShapeDtypeStruct((SEQ, RET_V_W), BF16),
                   jax.ShapeDtypeStruct((1, RET_V_W), F32), jax.ShapeDtypeStruct((1, RET_V_W), F32)],
        scratch_shapes=[pltpu.VMEM((RET_HEADS, dk, dv), F32)],
        compiler_params=_cparams(("arbitrary",)),
    )(rqk, rqk, rv, states, d_gated, ro, rg, gn_g, gn_b, din, qd, kd, cd, cos, sin)


def _bias_build(rel_bias, buckets):
    ng = len(ATT_GROUPS)

    def kern(tab_ref, bkt_ref, o_ref):
        g, h = pl.program_id(0), pl.program_id(1)
        bkt = bkt_ref[...]
        acc = jnp.zeros(bkt.shape, F32)
        for b in range(REL_BUCKETS):
            acc = jnp.where(bkt == b, tab_ref[b, g * ATT_HPG + h], acc)
        o_ref[...] = acc

    return pl.pallas_call(
        kern, name="bias_build", grid=(ng, ATT_HPG),
        in_specs=[pl.BlockSpec(memory_space=pltpu.SMEM),
                  pl.BlockSpec((None, ATT_BLK, 2 * ATT_BLK), lambda g, h: (g, 0, 0))],
        out_specs=pl.BlockSpec((None, None, ATT_BLK, 2 * ATT_BLK), lambda g, h: (g, h, 0, 0)),
        out_shape=jax.ShapeDtypeStruct((ng, ATT_HPG, ATT_BLK, 2 * ATT_BLK), F32),
        compiler_params=_cparams(("arbitrary", "arbitrary")),
    )(rel_bias, buckets)


def _bias_grad(dsb, buckets):
    ng = len(ATT_GROUPS)

    def kern(ds_ref, bkt_ref, o_ref):
        g, h = pl.program_id(0), pl.program_id(1)
        bkt, ds = bkt_ref[...], ds_ref[...]
        for b in range(REL_BUCKETS):
            o_ref[b, g * ATT_HPG + h] = jnp.sum(jnp.where(bkt == b, ds, 0.0))

    return pl.pallas_call(
        kern, name="bias_grad", grid=(ng, ATT_HPG),
        in_specs=[pl.BlockSpec((None, None, ATT_BLK, 2 * ATT_BLK), lambda g, h: (g, h, 0, 0)),
                  pl.BlockSpec((None, ATT_BLK, 2 * ATT_BLK), lambda g, h: (g, 0, 0))],
        out_specs=pl.BlockSpec(memory_space=pltpu.SMEM),
        out_shape=jax.ShapeDtypeStruct((REL_BUCKETS, N_ATT_HEADS), F32),
        compiler_params=_cparams(("arbitrary", "arbitrary")),
    )(dsb, buckets)


_NT = (((1,), (1,)), ((), ()))
_TN = (((0,), (0,)), ((), ()))
_ATT_SCALE = ATT_DH ** -0.5


def _window_mask(has_prev):
    qi = lax.broadcasted_iota(I32, (ATT_BLK, 2 * ATT_BLK), 0)
    kj = lax.broadcasted_iota(I32, (ATT_BLK, 2 * ATT_BLK), 1)
    prev_ok = jnp.logical_and(jnp.logical_and(kj < ATT_BLK, kj >= qi), has_prev)
    return jnp.logical_or(prev_ok, jnp.logical_and(kj >= ATT_BLK, qi >= kj - ATT_BLK))


def _head_specs(col0):
    return pl.BlockSpec((SEQ, ATT_DH), lambda h: (0, col0 + h))


def _sub_rows(start, size, dil):
    return pl.ds(start, size) if dil == 1 else pl.ds(start, size, stride=dil)


def _att_blocks(dil):
    nb = SEQ // dil // ATT_BLK
    return [(r + dil * n * ATT_BLK, n > 0, n + 1 < nb) for r in range(dil) for n in range(nb)]


def _att_fwd(gi, dil, qkv, bias, after=()):
    blk, dh = ATT_BLK, ATT_DH
    pad = dil * blk
    col0 = 3 * ATT_HPG * gi

    def kern(q_ref, k_ref, v_ref, b_ref, *rest):
        o_ref, l_ref, qf, kpad, vpad = rest[len(after):]
        zero = jnp.zeros((pad, dh), F32)
        kpad[0:pad, :] = zero
        vpad[0:pad, :] = zero
        kpad[pad:, :] = k_ref[...].astype(F32)
        vpad[pad:, :] = v_ref[...].astype(F32)
        qf[...] = q_ref[...].astype(F32)
        bias_m = b_ref[...]
        for start, has_prev, _ in _att_blocks(dil):
            rows, window = _sub_rows(start, blk, dil), _sub_rows(start, 2 * blk, dil)
            q = qf[rows, :].astype(BF16)
            kw, vw = kpad[window, :].astype(BF16), vpad[window, :].astype(BF16)
            valid = _window_mask(has_prev)
            s = lax.dot_general(q, kw, _NT, preferred_element_type=F32) * _ATT_SCALE + bias_m
            s = jnp.where(valid, s, -1e30)
            mx = jnp.max(s, axis=-1, keepdims=True)
            e = jnp.exp(s - mx)
            den = jnp.sum(e, axis=-1, keepdims=True)
            o_ref[rows, :] = jnp.dot((e / den).astype(BF16), vw, preferred_element_type=F32)
            l_ref[rows, :] = jnp.broadcast_to(mx + jnp.log(den), (blk, dh))

    return pl.pallas_call(
        kern, name=f"att_fwd_g{gi}", grid=(ATT_HPG,),
        in_specs=[_head_specs(col0), _head_specs(col0 + ATT_HPG), _head_specs(col0 + 2 * ATT_HPG),
                  pl.BlockSpec((None, None, blk, 2 * blk), lambda h: (gi, h, 0, 0))]
        + [pl.BlockSpec(memory_space=pl.ANY)] * len(after),
        out_specs=[_head_specs(0), _head_specs(0)],
        out_shape=[jax.ShapeDtypeStruct((SEQ, ATT_W), F32), jax.ShapeDtypeStruct((SEQ, ATT_W), F32)],
        scratch_shapes=[pltpu.VMEM((SEQ, dh), F32), pltpu.VMEM((SEQ + pad, dh), F32),
                        pltpu.VMEM((SEQ + pad, dh), F32)],
        compiler_params=_cparams(("arbitrary",)),
    )(qkv, qkv, qkv, bias, *after)


def _att_bwd(gi, dil, qkv, d_att, lse, dd, bias):
    blk, dh = ATT_BLK, ATT_DH
    pad = dil * blk
    col0 = 3 * ATT_HPG * gi

    def kern(q_ref, k_ref, v_ref, do_ref, l_ref, d_ref, b_ref, dqkv_ref, dsb_ref,
             qf, kpad, vpad, dq_s, dkpad, dvpad):
        zero = jnp.zeros((pad, dh), F32)
        kpad[0:pad, :] = zero
        vpad[0:pad, :] = zero
        kpad[pad:, :] = k_ref[...].astype(F32)
        vpad[pad:, :] = v_ref[...].astype(F32)
        qf[...] = q_ref[...].astype(F32)
        dkpad[...] = jnp.zeros_like(dkpad)
        dvpad[...] = jnp.zeros_like(dvpad)
        bias_m = b_ref[...]
        ds_sum = jnp.zeros((blk, 2 * blk), F32)

        for start, has_prev, _ in _att_blocks(dil):
            rows, window = _sub_rows(start, blk, dil), _sub_rows(start, 2 * blk, dil)
            q, d_o = qf[rows, :].astype(BF16), do_ref[rows, :].astype(BF16)
            kw, vw = kpad[window, :].astype(BF16), vpad[window, :].astype(BF16)
            lrow, drow = l_ref[rows, :][:, :1], d_ref[rows, :][:, :1]
            valid = _window_mask(has_prev)
            s = lax.dot_general(q, kw, _NT, preferred_element_type=F32) * _ATT_SCALE + bias_m
            p = jnp.where(valid, jnp.exp(jnp.where(valid, s, -1e30) - lrow), 0.0)
            dp = lax.dot_general(d_o, vw, _NT, preferred_element_type=F32)
            ds = p * (dp - drow)
            ds_b = ds.astype(BF16)
            dq_s[rows, :] = jnp.dot(ds_b, kw, preferred_element_type=F32) * _ATT_SCALE
            dkpad[window, :] += lax.dot_general(ds_b, q, _TN, preferred_element_type=F32) * _ATT_SCALE
            dvpad[window, :] += lax.dot_general(p.astype(BF16), d_o, _TN, preferred_element_type=F32)
            ds_sum = ds_sum + ds
        dsb_ref[...] = ds_sum

        dqkv_ref[0] = dq_s[...].astype(BF16)
        dqkv_ref[1] = dkpad[pad:, :].astype(BF16)
        dqkv_ref[2] = dvpad[pad:, :].astype(BF16)

    return pl.pallas_call(
        kern, name=f"att_bwd_g{gi}", grid=(ATT_HPG,),
        in_specs=[_head_specs(col0), _head_specs(col0 + ATT_HPG), _head_specs(col0 + 2 * ATT_HPG),
                  _head_specs(0), _head_specs(0), _head_specs(0),
                  pl.BlockSpec((None, None, blk, 2 * blk), lambda h: (gi, h, 0, 0))],
        out_specs=[pl.BlockSpec((3, SEQ, dh), lambda h: (0, 0, h)),
                   pl.BlockSpec((None, blk, 2 * blk), lambda h: (h, 0, 0))],
        out_shape=[jax.ShapeDtypeStruct((3, SEQ, ATT_W), BF16),
                   jax.ShapeDtypeStruct((ATT_HPG, blk, 2 * blk), F32)],
        scratch_shapes=[pltpu.VMEM((SEQ, dh), F32), pltpu.VMEM((SEQ + pad, dh), F32),
                        pltpu.VMEM((SEQ + pad, dh), F32), pltpu.VMEM((SEQ, dh), F32),
                        pltpu.VMEM((SEQ + pad, dh), F32), pltpu.VMEM((SEQ + pad, dh), F32)],
        compiler_params=_cparams(("arbitrary",)),
    )(qkv, qkv, qkv, d_att, lse, dd, bias)


def _rms_parts(x):
    r = lax.rsqrt(jnp.mean(x * x, axis=-1, keepdims=True) + RMS_EPS)
    return x * r, r


def _rms_bwd(d_xhat, xhat, r):
    return r * (d_xhat - xhat * jnp.mean(d_xhat * xhat, axis=-1, keepdims=True))


def _prenorm_fwd(name, x, gain, shift, scale, tm=256):
    def kern(x_ref, g_ref, sh_ref, sc_ref, o_ref):
        xhat, _ = _rms_parts(x_ref[...])
        o_ref[...] = ((xhat * g_ref[...]) * (1.0 + sc_ref[...]) + sh_ref[...]).astype(BF16)

    rows = pl.BlockSpec((tm, D_MODEL), lambda i: (i, 0))
    vec = pl.BlockSpec((1, D_MODEL), lambda i: (0, 0))
    return pl.pallas_call(
        kern, name=name, grid=(x.shape[0] // tm,), in_specs=[rows, vec, vec, vec], out_specs=rows,
        out_shape=jax.ShapeDtypeStruct(x.shape, BF16), compiler_params=_cparams(("parallel",)),
    )(x, gain, shift, scale)


def _prenorm_bwd_epi(d_h, x, resid, gain, scale, branch=None, gate=None):
    xhat, r = _rms_parts(x)
    nrm = xhat * gain
    d_n = d_h * (1.0 + scale)
    dx = _rms_bwd(d_n * gain, xhat, r) + resid
    sums = (jnp.sum(d_h, axis=0, keepdims=True), jnp.sum(d_h * nrm, axis=0, keepdims=True),
            jnp.sum(d_n * xhat, axis=0, keepdims=True))
    if branch is None:
        return (dx,) + sums
    return (dx, dx * gate) + sums + (jnp.sum(dx * branch, axis=0, keepdims=True),)


def _row_operands(tm, rows, vecs):
    return ([(a, (tm, D_MODEL), lambda i, j, kk: (i, 0)) for a in rows]
            + [(v, (1, D_MODEL), lambda i, j, kk: (0, 0)) for v in vecs])


def _gn_parts(ro):
    mu = jnp.mean(ro, axis=-1, keepdims=True)
    cen = ro - mu
    rstd = lax.rsqrt(jnp.mean(cen * cen, axis=-1, keepdims=True) + GN_EPS)
    return cen * rstd, rstd


MERGE_TM = 512


def _att_out(os_, ls_, w_att_out, gates, ret_out):
    tm = MERGE_TM

    def kern(o0, o1, o2, l0, l1, l2, w_ref, ga_ref, gb_ref, ro_ref, att_ref, attb_ref, lse_ref,
             ao_ref, mg_ref):
        l0v, l1v, l2v = l0[...], l1[...], l2[...]
        mx = jnp.maximum(jnp.maximum(l0v, l1v), l2v)
        e0, e1, e2 = jnp.exp(l0v - mx), jnp.exp(l1v - mx), jnp.exp(l2v - mx)
        den = e0 + e1 + e2
        att = (e0 / den) * o0[...] + (e1 / den) * o1[...] + (e2 / den) * o2[...]
        att_b = att.astype(BF16)
        att_ref[...] = att
        attb_ref[...] = att_b
        lse_ref[...] = mx + jnp.log(den)
        att_out = jnp.dot(att_b, w_ref[...], preferred_element_type=F32)
        ao_ref[...], merged = _merge_fwd_epi(att_out, ga_ref[...], gb_ref[...], ro_ref[...])
        mg_ref[...] = merged.astype(BF16)

    rows_w = pl.BlockSpec((tm, ATT_W), lambda i: (i, 0))
    rows_d = pl.BlockSpec((tm, D_MODEL), lambda i: (i, 0))
    return pl.pallas_call(
        kern, name="att_out", grid=(SEQ // tm,),
        in_specs=[rows_w] * 6 + [pl.BlockSpec((ATT_W, D_MODEL), lambda i: (0, 0)), rows_d,
                                 pl.BlockSpec((tm, D_MODEL), lambda i: (i, 1)), rows_d],
        out_specs=[rows_w, rows_w, rows_w, rows_d, rows_d],
        out_shape=[jax.ShapeDtypeStruct((SEQ, ATT_W), F32), jax.ShapeDtypeStruct((SEQ, ATT_W), BF16),
                   jax.ShapeDtypeStruct((SEQ, ATT_W), F32), jax.ShapeDtypeStruct((SEQ, D_MODEL), F32),
                   jax.ShapeDtypeStruct((SEQ, D_MODEL), BF16)],
        compiler_params=_cparams(("parallel",)),
    )(*os_, *ls_, w_att_out, gates, gates, ret_out)


def _merge_operands(gates, ret_out, att_out=None):
    ops = [(gates, (MERGE_TM, D_MODEL), lambda i, j, kk: (i, 0)),
           (gates, (MERGE_TM, D_MODEL), lambda i, j, kk: (i, 1)),
           (ret_out, (MERGE_TM, D_MODEL), lambda i, j, kk: (i, 0))]
    if att_out is not None:
        ops.append((att_out, (MERGE_TM, D_MODEL), lambda i, j, kk: (i, 0)))
    return ops


def _merge_fwd_epi(att_out, ga, gb, ret_out):
    return att_out, _sigmoid(ga.astype(F32)) * ret_out + _sigmoid(gb.astype(F32)) * att_out


def _merge_bwd_epi(d_merged, ga, gb, ret_out, att_out):
    sa, sb = _sigmoid(ga.astype(F32)), _sigmoid(gb.astype(F32))
    return (d_merged * sa, d_merged * sb, d_merged * ret_out * (sa * (1.0 - sa)),
            d_merged * att_out * (sb * (1.0 - sb)))


def _att_out_bwd_epi(d_att, att):
    outs = []
    for h in range(ATT_HPG):
        sl = slice(h * ATT_DH, (h + 1) * ATT_DH)
        outs.append(jnp.broadcast_to(jnp.sum(d_att[:, sl] * att[:, sl], axis=-1, keepdims=True),
                                     (d_att.shape[0], ATT_DH)))
    return d_att, jnp.concatenate(outs, axis=-1)


def _loss_head_epi(branch, x_prev, target, gate, gain):
    x3 = x_prev + gate * branch
    xhat, r = _rms_parts(x3)
    err = xhat * gain - target
    d_y = err / D_MODEL
    loss = 0.5 * jnp.sum(jnp.mean(err * err, axis=-1, keepdims=True), axis=0, keepdims=True)
    d_x = _rms_bwd(d_y * gain, xhat, r)
    return (d_x, d_x * gate, jnp.broadcast_to(loss, (1, D_MODEL)),
            jnp.sum(d_y * xhat, axis=0, keepdims=True), jnp.sum(d_x * branch, axis=0, keepdims=True))


def _local_step(pos, x, target, mod, norm1_g, norm2_g, norm_f_g, rel_bias, gn_g, gn_b, w_in, rest_gather):
    sh1, sc1, g1, sh2, sc2, g2 = [mod[:, i * D_MODEL:(i + 1) * D_MODEL] for i in range(6)]
    cos, sin = _rope_tables()
    din, qd, kd, cd = _decay_tables()
    buckets = _bucket_tables()
    bias = _bias_build(rel_bias, buckets)
    dils = [d for _, d in ATT_GROUPS]

    h1 = _prenorm_fwd("prenorm1_fwd", x, norm1_g, sh1, sc1)

    qk_tn = 2 * RET_DK

    def rot_epi(acc, cs, sn, scale):
        half = RET_DK // 2
        outs = []
        for h0 in range(0, qk_tn, RET_DK):
            x1, x2 = acc[:, h0:h0 + half], acc[:, h0 + half:h0 + RET_DK]
            outs += [x1 * cs - x2 * sn, x1 * sn + x2 * cs]
        return (jnp.concatenate(outs, axis=-1) * scale,)

    qk_scale = jnp.concatenate([jnp.ones((1, RET_QK_W), F32),
                                jnp.full((1, RET_QK_W), RET_DK ** -0.5, F32)], axis=-1)
    rope_ex = [(cos, (TM, RET_DK // 2), lambda i, j, kk: (i, 0)),
               (sin, (TM, RET_DK // 2), lambda i, j, kk: (i, 0)),
               (qk_scale, (1, qk_tn), lambda i, j, kk: (0, j))]
    rest_sems, rest_shards, rest_fulls, rest_token = rest_gather
    behind = [rest_token]
    rv = _matmul("proj_rv", h1, w_in, "nn", SEQ, RET_V_W, D_MODEL, [BF16], b_off=OFF_V, tk=D_MODEL,
                 after=behind)[0]
    rg = _matmul("proj_rg", h1, w_in, "nn", SEQ, RET_V_W, D_MODEL, [BF16], b_off=OFF_G, tk=D_MODEL,
                 after=behind)[0]
    gates = _matmul("proj_gates", h1, w_in, "nn", SEQ, 2 * D_MODEL, D_MODEL, [BF16], b_off=OFF_GATE,
                    tn=512, tk=D_MODEL, after=behind)[0]
    aqkv = _matmul("proj_att", h1, w_in, "nn", SEQ, 9 * ATT_W, D_MODEL, [BF16], b_off=OFF_ATT,
                   tn=512, tk=D_MODEL, after=behind)[0]

    rqk = _matmul("proj_qk", h1, w_in, "nn", SEQ, 2 * RET_QK_W, D_MODEL, [BF16], b_off=OFF_Q,
                  tn=qk_tn, tk=D_MODEL, epilogue=rot_epi, extras=rope_ex, after=behind)[0]
    ro, states, gated = _retention_fwd(rqk, rv, rg, gn_g, gn_b, din, qd, kd, cd)
    os_, ls_ = [], []
    for gi in range(3):
        if gi == 2:
            rest_sems, rest_fulls, fwd_token = _gather_rest_forward(
                rest_sems, rest_shards, rest_fulls, [gated, gates] + os_)
        o_g, l_g = _att_fwd(gi, dils[gi], aqkv, bias, after=[fwd_token] if gi == 2 else ())
        os_.append(o_g)
        ls_.append(l_g)
    w_ret_out, w_att_out, w_o, w_ff1, w_ff2 = _gather_rest_end(rest_sems, rest_fulls, [os_[2]])
    ret_out = _matmul("ret_out", gated, w_ret_out, "nn", SEQ, D_MODEL, RET_V_W, [F32], tk=RET_V_W)[0]
    att, att_b, lse, att_out, merged = _att_out(os_, ls_, w_att_out, gates, ret_out)

    def mix_epi(acc, xt, g, gain, sh, sc):
        x_new = xt + g * acc
        xhat, _ = _rms_parts(x_new)
        return x_new, acc, (xhat * gain) * (1.0 + sc) + sh

    x2, mix, h2 = _matmul("mix_out", merged, w_o, "nn", SEQ, D_MODEL, D_MODEL, [F32, BF16, BF16],
                          epilogue=mix_epi, extras=_row_operands(TM, [x], [g1, norm2_g, sh2, sc2]))

    def relu2_epi(acc):
        r = jnp.maximum(acc, 0.0)
        return r * r, r

    act, relu_u = _matmul("ff1", h2, w_ff1, "nn", SEQ, D_FF, D_MODEL, [BF16, BF16], tk=D_MODEL,
                          epilogue=relu2_epi)
    d_x3, d_y2, loss, d_gf, d_g2 = _matmul(
        "ff2", act, w_ff2, "nn", SEQ, D_MODEL, D_FF, [F32, BF16], tm=TM, tk=1024, n_sums=3,
        epilogue=_loss_head_epi, extras=_row_operands(TM, [x2, target], [g2, norm_f_g]))

    def relu2_bwd_epi(acc, rt):
        return (acc * (2.0 * rt.astype(F32)),)

    gw_ff2 = _matmul_tn_pair("ff2_dw", pos, act, d_y2, D_FF, D_MODEL, SEQ, D_FF // N_CHIPS,
                             tm=512, tn=1024, tk=SEQ)
    d_u = _matmul("ff2_dx", d_y2, w_ff2, "nt", SEQ, D_FF, D_MODEL, [BF16], epilogue=relu2_bwd_epi,
                  extras=[(relu_u, (TM, TN), lambda i, j, kk: (i, j))])[0]
    gw_ff1 = _matmul_tn_pair("ff1_dw", pos, h2, d_u, D_MODEL, D_FF, SEQ, D_MODEL,
                             tm=512, tn=1024, tk=SEQ)
    ffn = ["w_ff2", "w_ff1"]
    ffn_started = _ici_start("ici_start_ffn", ffn, [gw_ff2, gw_ff1])
    d_x2, d_mix, d_sh2, d_sc2, d_n2g, d_g1 = _matmul(
        "ff1_dx", d_u, w_ff1, "nt", SEQ, D_MODEL, D_FF, [F32, BF16], tm=TM, tk=1024, n_sums=4,
        epilogue=_prenorm_bwd_epi, extras=_row_operands(TM, [x2, d_x3], [norm2_g, sc2])
        + _row_operands(TM, [mix], [g1]), after=[ffn_started[3]])
    gw_o = _matmul_tn_pair("mix_dw", pos, merged, d_mix, D_MODEL, D_MODEL, SEQ, D_MODEL // N_CHIPS,
                           tm=128, tn=1024, tk=2048)
    d_ret_out, d_att_out, d_ga, d_gb = _matmul(
        "mix_dx", d_mix, w_o, "nt", SEQ, D_MODEL, D_MODEL, [BF16] * 4, tm=MERGE_TM,
        epilogue=_merge_bwd_epi, extras=_merge_operands(gates, ret_out, att_out))

    gw_ret_out = _matmul_tn_pair("ret_out_dw", pos, gated, d_ret_out, RET_V_W, D_MODEL, SEQ,
                                 RET_V_W // N_CHIPS, tm=256, tn=1024, tk=SEQ)
    gw_att_out = _matmul_tn_pair("att_out_dw", pos, att_b, d_att_out, ATT_W, D_MODEL, SEQ, ATT_W,
                                 tm=256, tn=1024, tk=2048)
    mixer = ["w_o", "w_ret_out", "w_att_out"]
    mixer_started = _ici_start("ici_start_mixer", mixer, [gw_o, gw_ret_out, gw_att_out])
    d_gated = _matmul("ret_out_dx", d_ret_out, w_ret_out, "nt", SEQ, RET_V_W, D_MODEL, [BF16],
                      after=[mixer_started[3]])[0]
    d_att, dd = _matmul("att_out_dx", d_att_out, w_att_out, "nt", SEQ, ATT_W, D_MODEL, [F32, F32],
                        epilogue=_att_out_bwd_epi,
                        extras=[(att, (TM, ATT_W), lambda i, j, kk: (i, 0))], after=[mixer_started[3]])

    d_rqkv, d_rg, d_gn_g, d_gn_b = _retention_bwd(rqk, rv, states, d_gated, ro, rg, gn_g, gn_b,
                                                  din, qd, kd, cd, cos, sin)

    d_aqkv, dsbs = [], []
    for gi in range(3):
        dqkv, dsb = _att_bwd(gi, dils[gi], aqkv, d_att, lse, dd, bias)
        d_aqkv.append(dqkv)
        dsbs.append(dsb)
    d_rel_bias = _bias_grad(jnp.stack(dsbs), buckets)

    d_proj = ([(d_rqkv, False), (d_rg, False)] + [(t, True) for t in d_aqkv]
              + [(d_ga, False), (d_gb, False)])
    gw_in = _matmul_tn_pair("proj_dw", pos, h1, d_proj, D_MODEL, IN_COLS, SEQ, D_MODEL,
                            tm=512, tn=ATT_W, tk=SEQ)
    sems, (gw_in,), (land,), token = _ici_start("ici_start_w_in", ["w_in"], [gw_in])
    grad_x, d_sh1, d_sc1, d_n1g = _matmul(
        "proj_dx", d_proj, w_in, "nt", SEQ, D_MODEL, IN_COLS, [F32], tn=1024, tk=ATT_W, n_sums=3,
        epilogue=_prenorm_bwd_epi, extras=_row_operands(TM, [x, d_x2], [norm1_g, sc1]), after=[token])
    pending = (sems, land)

    names = ffn + mixer
    psums, got = _ici_wait("ici_wait_rest", names, list(ffn_started[0]) + list(mixer_started[0]),
                           list(ffn_started[1]) + list(mixer_started[1]),
                           list(ffn_started[2]) + list(mixer_started[2]), [grad_x])
    g_big = {n: _final_sum("final_" + n, pos, dict(BIG)[n], psums[i], got[i], SHARD[n])
             for i, n in enumerate(names)}
    d_mod = jnp.concatenate([d_sh1, d_sc1, d_g1, d_sh2, d_sc2, d_g2], axis=-1)
    small = dict(norm1_g=d_n1g, norm2_g=d_n2g, norm_f_g=d_gf, gn_g=d_gn_g, gn_b=d_gn_b,
                 rel_bias=d_rel_bias)
    return loss, grad_x, d_mod, small, g_big, (gw_in,) + pending


def _me():
    return lax.axis_index("x"), lax.axis_index("y"), lax.axis_index("c")


def _peer(x, y, c, mask):
    return (x ^ ((mask >> 2) & 1), y ^ ((mask >> 1) & 1), c ^ (mask & 1))


def _gather8(src_ref, dst_ref, send_sems, recv_sems):
    x, y, c = _me()
    me = 4 * x + 2 * y + c
    copies = []
    for mask in range(1, N_DEV):
        cp = pltpu.make_async_remote_copy(
            src_ref=src_ref, dst_ref=dst_ref.at[me], send_sem=send_sems.at[mask - 1],
            recv_sem=recv_sems.at[mask - 1], device_id=_peer(x, y, c, mask), device_id_type=MESH)
        cp.start()
        copies.append(cp)
    dst_ref[me] = src_ref[...]
    for cp in copies:
        cp.wait_recv()
    for cp in copies:
        cp.wait_send()


def _ada_fwd(c_in, w_ada, b_ada):
    ncol = ADA_COLS // N_CHIPS

    def body(c_ref, w_ref, b_ref, mod_ref, sc_ref, cbuf, cg, mbuf, mg, s1, r1, s2, r2):
        x, y, c = _me()
        me = 4 * x + 2 * y + c
        cv = c_ref[...]
        cbuf[...] = jnp.broadcast_to(cv * _sigmoid(cv), cbuf.shape)
        _gather8(cbuf, cg, s1, r1)
        rows = lax.broadcasted_iota(I32, (N_DEV, D_MODEL), 0)
        sc_all = jnp.zeros((N_DEV, D_MODEL), F32)
        for d in range(N_DEV):
            sc_all = jnp.where(rows == d, cg[d], sc_all)
        sc_ref[...] = sc_all
        mbuf[...] = jnp.dot(sc_all.astype(BF16), w_ref[...].astype(BF16), preferred_element_type=F32)
        _gather8(mbuf, mg, s2, r2)
        rowsel = lax.broadcasted_iota(I32, (N_DEV, ncol), 0) == me
        for k in range(N_CHIPS):
            blk = mg[2 * k]
            row = jnp.sum(jnp.where(rowsel, blk, 0.0), axis=0, keepdims=True)
            mod_ref[:, k * ncol:(k + 1) * ncol] = row + b_ref[:, k * ncol:(k + 1) * ncol]

    vm = pl.BlockSpec(memory_space=pltpu.VMEM)
    return pl.pallas_call(
        body, name="ada_fwd",
        in_specs=[vm, vm, vm], out_specs=[vm, vm],
        out_shape=[jax.ShapeDtypeStruct((1, ADA_COLS), F32), jax.ShapeDtypeStruct((N_DEV, D_MODEL), F32)],
        scratch_shapes=[
            pltpu.VMEM((8, D_MODEL), F32), pltpu.VMEM((N_DEV, 8, D_MODEL), F32),
            pltpu.VMEM((8, ncol), F32), pltpu.VMEM((N_DEV, 8, ncol), F32),
            pltpu.SemaphoreType.DMA((N_DEV - 1,)), pltpu.SemaphoreType.DMA((N_DEV - 1,)),
            pltpu.SemaphoreType.DMA((N_DEV - 1,)), pltpu.SemaphoreType.DMA((N_DEV - 1,)),
        ],
        compiler_params=pltpu.CompilerParams(vmem_limit_bytes=VMEM_LIMIT_V7X),
    )(c_in, w_ada, b_ada)


def _small_reduce(pack, sc_all, after=()):
    ncol = ADA_COLS // N_CHIPS

    def body(p_ref, sc_ref, *rest):
        tot_ref, gw_ref, pg, s1, r1 = rest[len(after):]
        x, y, _ = _me()
        chip = 2 * x + y
        _gather8(p_ref, pg, s1, r1)
        tot = pg[0]
        for d in range(1, N_DEV):
            tot = tot + pg[d]
        tot_ref[...] = tot
        rows = lax.broadcasted_iota(I32, (N_DEV, ncol), 0)
        dmod = jnp.zeros((N_DEV, ncol), F32)
        for k in range(N_CHIPS):
            part = jnp.zeros((N_DEV, ncol), F32)
            for d in range(N_DEV):
                part = jnp.where(rows == d, pg[d, :, k * ncol:(k + 1) * ncol][0:1, :], part)
            dmod = jnp.where(chip == k, part, dmod)
        gw_ref[...] = lax.dot_general(sc_ref[...].astype(BF16), dmod.astype(BF16), _TN,
                                      preferred_element_type=F32)

    vm = pl.BlockSpec(memory_space=pltpu.VMEM)
    return pl.pallas_call(
        body, name="small_reduce",
        in_specs=[vm, vm] + [pl.BlockSpec(memory_space=pl.ANY)] * len(after), out_specs=[vm, vm],
        out_shape=[jax.ShapeDtypeStruct((8, ADA_COLS), F32), jax.ShapeDtypeStruct((D_MODEL, ncol), F32)],
        scratch_shapes=[pltpu.VMEM((N_DEV, 8, ADA_COLS), F32),
                        pltpu.SemaphoreType.DMA((N_DEV - 1,)), pltpu.SemaphoreType.DMA((N_DEV - 1,))],
        compiler_params=pltpu.CompilerParams(vmem_limit_bytes=VMEM_LIMIT_V7X),
    )(pack, sc_all, *after)


BIG = (("w_in", 1), ("w_ret_out", 0), ("w_att_out", 1), ("w_o", 0), ("w_ff1", 1), ("w_ff2", 0))
SHARD = {"w_in": (D_MODEL, IN_COLS // N_CHIPS), "w_ret_out": (RET_V_W // N_CHIPS, D_MODEL),
         "w_att_out": (ATT_W, D_MODEL // N_CHIPS), "w_o": (D_MODEL // N_CHIPS, D_MODEL),
         "w_ff1": (D_MODEL, D_FF // N_CHIPS), "w_ff2": (D_FF // N_CHIPS, D_MODEL)}
_CHIP_FLIPS = ((1, 0), (0, 1), (1, 1))


def _region(ref, axis, chip, half, shard_shape):
    r, cw = shard_shape
    hr = r // 2
    if axis == 1:
        return ref.at[pl.ds(half * hr, hr), pl.ds(chip * cw, cw)]
    return ref.at[pl.ds(chip * r + half * hr, hr), :]


CAST_ROWS = 128


def _gather_weights(shards, n_remote):
    nw = len(BIG)
    shapes = [s.shape for s in shards]
    full_shapes = [(r, N_CHIPS * cw) if ax == 1 else (N_CHIPS * r, cw)
                   for (r, cw), (_, ax) in zip(shapes, BIG)]

    def body(*refs):
        ins, outs = refs[:nw], refs[nw:2 * nw]
        own = refs[2 * nw:3 * nw]
        from_ici, from_sib = refs[3 * nw:3 * nw + n_remote], refs[3 * nw + n_remote:3 * nw + 2 * n_remote]
        ld_sem, st_sem, s_ici, r_ici, s_d2d, r_d2d, st_a, st_b, stage = refs[3 * nw + 2 * n_remote:]
        x, y, c = _me()
        chip = 2 * x + y
        sib = (x, y, 1 - c)
        loads = [pltpu.make_async_copy(ins[i], stage if i == 0 else own[i], ld_sem.at[i])
                 for i in range(nw)]
        for cp in loads:
            cp.start()
        pending, first = [], []
        for i, (_, ax) in enumerate(BIG):
            r, cw = shapes[i]
            hr = r // 2
            loads[i].wait()
            if i == 0:
                for r0 in range(0, r, CAST_ROWS):
                    own[0][r0:r0 + CAST_ROWS, :] = stage[r0:r0 + CAST_ROWS, :].astype(BF16)
            dst = outs[i].at[:, pl.ds(chip * cw, cw)] if ax == 1 else outs[i].at[pl.ds(chip * r, r), :]
            cp = pltpu.make_async_copy(own[i], dst, st_sem.at[i])
            cp.start()
            pending.append(cp)
            for j, (fx, fy) in enumerate(_CHIP_FLIPS if i < n_remote else ()):
                rc = pltpu.make_async_remote_copy(
                    src_ref=own[i].at[pl.ds(c * hr, hr), :], dst_ref=from_ici[i].at[j],
                    send_sem=s_ici.at[j * nw + i], recv_sem=r_ici.at[j * nw + i],
                    device_id=(x ^ fx, y ^ fy, c), device_id_type=MESH)
                rc.start()
                first.append((j, i, rc))
        passed = []
        for j, i, rc in first:
            fx, fy = _CHIP_FLIPS[j]
            src_chip = 2 * (x ^ fx) + (y ^ fy)
            ax = BIG[i][1]
            rc.wait_recv()
            fw = pltpu.make_async_remote_copy(
                src_ref=from_ici[i].at[j], dst_ref=from_sib[i].at[j], send_sem=s_d2d.at[j * nw + i],
                recv_sem=r_d2d.at[j * nw + i], device_id=sib, device_id_type=MESH)
            fw.start()
            passed.append((j, i, src_chip, fw))
            st = pltpu.make_async_copy(from_ici[i].at[j], _region(outs[i], ax, src_chip, c, shapes[i]),
                                       st_a.at[j * nw + i])
            st.start()
            pending.append(st)
        for j, i, src_chip, fw in passed:
            fw.wait_recv()
            st = pltpu.make_async_copy(from_sib[i].at[j],
                                       _region(outs[i], BIG[i][1], src_chip, 1 - c, shapes[i]),
                                       st_b.at[j * nw + i])
            st.start()
            pending.append(st)
        for _, _, rc in first:
            rc.wait_send()
        for _, _, _, fw in passed:
            fw.wait_send()
        for cp in pending:
            cp.wait()

    hbm = pl.BlockSpec(memory_space=pl.ANY)
    halves = [pltpu.VMEM((3, r // 2, cw), BF16) for r, cw in shapes[:n_remote]]
    return pl.pallas_call(
        body, name="gather_weights",
        in_specs=[hbm] * nw, out_specs=[hbm] * nw,
        out_shape=[jax.ShapeDtypeStruct(fs, BF16) for fs in full_shapes],
        scratch_shapes=[pltpu.VMEM(sh, BF16) for sh in shapes] + halves + halves
        + [pltpu.SemaphoreType.DMA((nw,)), pltpu.SemaphoreType.DMA((nw,))]
        + [pltpu.SemaphoreType.DMA((3 * nw,))] * 6 + [pltpu.VMEM(shapes[0], F32)],
        compiler_params=pltpu.CompilerParams(vmem_limit_bytes=VMEM_LIMIT_V7X),
    )(*shards)


REST = BIG[1:]
_SIDE_EFFECTS = pltpu.CompilerParams(has_side_effects=pltpu.SideEffectType.DATAFLOW_SIDE_EFFECTING)
_ANY_SPEC = pl.BlockSpec(memory_space=pl.ANY)


def _rest_ici_copies(shard_refs, full_refs, sems):
    x, y, c = _me()
    chip = 2 * x + y
    n = 3 * len(REST)
    copies = []
    for i, (name, ax) in enumerate(REST):
        hr = SHARD[name][0] // 2
        for j, (fx, fy) in enumerate(_CHIP_FLIPS):
            copies.append(pltpu.make_async_remote_copy(
                src_ref=shard_refs[i].at[pl.ds(c * hr, hr), :],
                dst_ref=_region(full_refs[i], ax, chip, c, SHARD[name]),
                send_sem=sems[3 * i + j], recv_sem=sems[n + 3 * i + j],
                device_id=(x ^ fx, y ^ fy, c), device_id_type=MESH))
    return copies


def _rest_d2d_copies(full_refs, sems):
    x, y, c = _me()
    n = 3 * len(REST)
    copies = []
    for i, (name, ax) in enumerate(REST):
        for j, (fx, fy) in enumerate(_CHIP_FLIPS):
            reg = _region(full_refs[i], ax, 2 * (x ^ fx) + (y ^ fy), c, SHARD[name])
            copies.append(pltpu.make_async_remote_copy(
                src_ref=reg, dst_ref=reg, send_sem=sems[3 * i + j], recv_sem=sems[n + 3 * i + j],
                device_id=(x, y, 1 - c), device_id_type=MESH))
    return copies


def _gather_rest_start(shards, fulls, after):
    nr, ns, na = len(REST), 6 * len(REST), len(after)

    def body(*refs):
        for cp in _rest_ici_copies(refs[:nr], refs[nr:2 * nr], refs[2 * nr + na:2 * nr + na + ns]):
            cp.start()
        token = refs[-1]
        token[...] = jnp.zeros_like(token)

    hbm = lambda a: pltpu.HBM(a.shape, a.dtype)
    res = pl.pallas_call(
        body, name="gather_rest_start",
        out_shape=(pltpu.SemaphoreType.DMA(()),) * ns + tuple(hbm(a) for a in shards + fulls)
        + (jax.ShapeDtypeStruct((8, 128), F32),),
        in_specs=(_HBM_SPEC,) * (2 * nr) + (_ANY_SPEC,) * na,
        out_specs=(_SEM_SPEC,) * ns + (_HBM_SPEC,) * (2 * nr) + (pl.BlockSpec(memory_space=pltpu.VMEM),),
        input_output_aliases={k: ns + k for k in range(2 * nr)}, compiler_params=_SIDE_EFFECTS,
    )(*[pltpu.with_memory_space_constraint(a, pltpu.HBM) for a in shards + fulls], *after)
    return res[:ns], res[ns:ns + nr], res[ns + nr:ns + 2 * nr], res[-1]


def _gather_rest_forward(sems, shards, fulls, after):
    nr, ns = len(REST), 6 * len(REST)

    def body(*refs):
        shard_refs, full_refs, old = refs[:nr], refs[nr:2 * nr], refs[2 * nr:2 * nr + ns]
        new = refs[2 * nr + ns + len(after):2 * nr + 2 * ns + len(after)]
        for cp in _rest_ici_copies(shard_refs, full_refs, old):
            cp.wait_send()
            cp.wait_recv()
        for cp in _rest_d2d_copies(full_refs, new):
            cp.start()
        token = refs[-1]
        token[...] = jnp.zeros_like(token)

    res = pl.pallas_call(
        body, name="gather_rest_forward",
        out_shape=(pltpu.SemaphoreType.DMA(()),) * ns + tuple(pltpu.HBM(a.shape, a.dtype) for a in fulls)
        + (jax.ShapeDtypeStruct((8, 128), F32),),
        in_specs=(_HBM_SPEC,) * (2 * nr) + (_SEM_SPEC,) * ns + (_ANY_SPEC,) * len(after),
        out_specs=(_SEM_SPEC,) * ns + (_HBM_SPEC,) * nr + (pl.BlockSpec(memory_space=pltpu.VMEM),),
        input_output_aliases={nr + k: ns + k for k in range(nr)}, compiler_params=_SIDE_EFFECTS,
    )(*shards, *fulls, *sems, *after)
    return res[:ns], res[ns:ns + nr], res[-1]


def _gather_rest_end(sems, fulls, after):
    nr, ns = len(REST), 6 * len(REST)

    def body(*refs):
        for cp in _rest_d2d_copies(refs[:nr], refs[nr:nr + ns]):
            cp.wait_send()
            cp.wait_recv()

    return pl.pallas_call(
        body, name="gather_rest_end",
        out_shape=tuple(pltpu.HBM(a.shape, a.dtype) for a in fulls),
        in_specs=(_HBM_SPEC,) * nr + (_SEM_SPEC,) * ns + (_ANY_SPEC,) * len(after),
        out_specs=(_HBM_SPEC,) * nr,
        input_output_aliases={k: k for k in range(nr)}, compiler_params=_SIDE_EFFECTS,
    )(*fulls, *sems, *after)


def _adam_update(w, g, m, v):
    mn = ADAM_B1 * m + (1.0 - ADAM_B1) * g
    vn = ADAM_B2 * v + (1.0 - ADAM_B2) * (g * g)
    m_hat = mn / (1.0 - ADAM_B1 ** ADAM_STEP)
    v_hat = vn / (1.0 - ADAM_B2 ** ADAM_STEP)
    return -ADAM_LR * (m_hat / (jnp.sqrt(v_hat) + ADAM_EPS) + ADAM_WD * w), mn, vn


def _final_sum(name, pos, axis, psum, recv, shard_shape, after=(), tr=256):
    r, cw = shard_shape
    hr = r // 2
    tr = min(tr, hr)
    nt = hr // tr
    n_after = len(after)

    def kern(pos_ref, p_ref, r_ref, *rest):
        g_ref, send_buf, land_buf, s_sem, r_sem = rest[n_after:]
        p, t = pl.program_id(0), pl.program_id(1)
        sib = _sibling()

        def copy(i):
            return pltpu.make_async_remote_copy(
                src_ref=send_buf.at[i], dst_ref=land_buf.at[i], send_sem=s_sem.at[i],
                recv_sem=r_sem.at[i], device_id=sib, device_id_type=MESH)

        @pl.when(p == 0)
        def _():
            tot = p_ref[...].astype(F32)
            for j in range(3):
                tot = tot + r_ref[j].astype(F32)
            send_buf[t] = tot
            copy(t).start()
            g_ref[...] = tot

        @pl.when(p == 1)
        def _():
            copy(t).wait_recv()
            g_ref[...] = land_buf[t]

        @pl.when(jnp.logical_and(p == 1, t == nt - 1))
        def _():
            for i in range(nt):
                copy(i).wait_send()

    def shard_rows(p, t, pos_ref):
        return (jnp.where(p == 0, pos_ref[0], 1 - pos_ref[0]) * nt + t, 0)

    def own_part(p, t, pos_ref):
        tt = jnp.where(p == 0, t, nt - 1)
        return (tt, pos_ref[1]) if axis == 1 else (pos_ref[1] * nt + tt, 0)

    grid_spec = pltpu.PrefetchScalarGridSpec(
        num_scalar_prefetch=1, grid=(2, nt),
        in_specs=[pl.BlockSpec((tr, cw), own_part),
                  pl.BlockSpec((3, tr, cw), lambda p, t, pos_ref: (0, jnp.where(p == 0, t, nt - 1), 0))]
        + [pl.BlockSpec(memory_space=pl.ANY)] * n_after,
        out_specs=pl.BlockSpec((tr, cw), shard_rows),
        scratch_shapes=[pltpu.VMEM((nt, tr, cw), F32), pltpu.VMEM((nt, tr, cw), F32),
                        pltpu.SemaphoreType.DMA((nt,)), pltpu.SemaphoreType.DMA((nt,))])
    return pl.pallas_call(
        kern, name=name, grid_spec=grid_spec, out_shape=jax.ShapeDtypeStruct((r, cw), F32),
        compiler_params=_cparams(("arbitrary", "arbitrary")),
    )(pos, psum, recv, *after)


def _adamw(name, w, g, m, v):
    r, cw = w.shape
    tr = min(r, 128)

    def kern(w_ref, g_ref, m_ref, v_ref, go_ref, d_ref, nm_ref, nv_ref):
        gv = g_ref[...]
        go_ref[...] = gv
        d_ref[...], nm_ref[...], nv_ref[...] = _adam_update(w_ref[...], gv, m_ref[...], v_ref[...])

    spec = pl.BlockSpec((tr, cw), lambda i: (i, 0))
    return pl.pallas_call(
        kern, name=name, grid=(r // tr,), in_specs=[spec] * 4, out_specs=[spec] * 4,
        out_shape=[jax.ShapeDtypeStruct((r, cw), F32)] * 4, compiler_params=_cparams(("parallel",)),
    )(w, g, m, v)


_PACK_W = ADA_COLS
_NB = REL_BUCKETS * N_ATT_HEADS
_SMALL_SLOTS = {
    "b_ada": (0, 0, ADA_COLS),
    "norm1_g": (1, 0, D_MODEL), "norm2_g": (1, D_MODEL, D_MODEL), "norm_f_g": (1, 2 * D_MODEL, D_MODEL),
    "ret_gn_g": (1, 3 * D_MODEL, RET_V_W),
    "ret_gn_b": (2, 0, RET_V_W), "rel_bias": (2, RET_V_W, _NB), "loss": (2, RET_V_W + 512, 128),
}


def _pack_small(vals):
    rows = []
    for r in range(8):
        items = sorted([(off, n) for n, (rr, off, _) in _SMALL_SLOTS.items() if rr == r and n in vals])
        parts, pos = [], 0
        for off, n in items:
            if off > pos:
                parts.append(jnp.zeros((1, off - pos), F32))
            parts.append(vals[n].reshape(1, -1).astype(F32))
            pos = off + _SMALL_SLOTS[n][2]
        if pos < _PACK_W:
            parts.append(jnp.zeros((1, _PACK_W - pos), F32))
        rows.append(jnp.concatenate(parts, axis=-1))
    return jnp.concatenate(rows, axis=0)


def _adamw_small(tot, names, wmv):
    n = len(names)

    def kern(tot_ref, *refs):
        ins, outs = refs[:3 * n], refs[3 * n:]
        for i, name in enumerate(names):
            row, off, width = _SMALL_SLOTS[name]
            g = tot_ref[row:row + 1, off:off + width]
            outs[i][...] = g
            outs[n + i][...], outs[2 * n + i][...], outs[3 * n + i][...] = _adam_update(
                ins[i][...], g, ins[n + i][...], ins[2 * n + i][...])

    vm = pl.BlockSpec(memory_space=pltpu.VMEM)
    shapes = [jax.ShapeDtypeStruct((1, _SMALL_SLOTS[name][2]), F32) for name in names]
    res = pl.pallas_call(
        kern, name="adamw_small", in_specs=[vm] * (1 + 3 * n), out_specs=[vm] * (4 * n),
        out_shape=shapes * 4,
    )(tot, *wmv[0], *wmv[1], *wmv[2])
    return res[:n], res[n:2 * n], res[2 * n:3 * n], res[3 * n:]


def _unpack_small(pack, name):
    r, off, wd = _SMALL_SLOTS[name]
    return pack[r:r + 1, off:off + wd]


def kernel(x, c, w_ada, b_ada, norm1_g, w_in, rel_bias, ret_gn_g, ret_gn_b, w_ret_out, w_att_out, w_o, norm2_g, w_ff1, w_ff2, norm_f_g, loss_target, m_w_ada, m_b_ada, m_norm1_g, m_w_in, m_rel_bias, m_ret_gn_g, m_ret_gn_b, m_w_ret_out, m_w_att_out, m_w_o, m_norm2_g, m_w_ff1, m_w_ff2, m_norm_f_g, v_w_ada, v_b_ada, v_norm1_g, v_w_in, v_rel_bias, v_ret_gn_g, v_ret_gn_b, v_w_ret_out, v_w_att_out, v_w_o, v_norm2_g, v_w_ff1, v_w_ff2, v_norm_f_g):
    given = dict(locals())
    big_names = [n for n, _ in BIG]
    shard_w = {n: given[n][0] for n in big_names}
    assert all(shard_w[n].shape == SHARD[n] for n in big_names)

    shards_bf = [None] + [shard_w[n].astype(BF16) for n in big_names[1:]]
    full = _gather_weights([shard_w["w_in"]] + shards_bf[1:], 1)
    mod, sc_all = _ada_fwd(c, w_ada[0], b_ada)
    rest_gather = _gather_rest_start(shards_bf[1:], list(full[1:]), [mod])
    pos = _where_am_i()

    loss, grad_x, d_mod, small, g_big, pending = _local_step(
        pos, x[0], loss_target[0], mod, norm1_g, norm2_g, norm_f_g.reshape(1, -1), rel_bias, ret_gn_g,
        ret_gn_b, full[0], rest_gather)

    pack_g = _pack_small(dict(b_ada=d_mod, norm1_g=small["norm1_g"], norm2_g=small["norm2_g"],
                              norm_f_g=small["norm_f_g"], ret_gn_g=small["gn_g"], ret_gn_b=small["gn_b"],
                              rel_bias=small["rel_bias"], loss=loss[:, :128]))
    tot, g_w_ada = _small_reduce(pack_g, sc_all, after=list(g_big.values()))

    small_names = ["b_ada", "norm1_g", "rel_bias", "ret_gn_g", "ret_gn_b", "norm2_g", "norm_f_g"]
    small_out = _adamw_small(tot, small_names, [[given[p + n].reshape(1, -1) for n in small_names]
                                                for p in ("", "m_", "v_")])
    grads, deltas, new_m, new_v = ({n: t.reshape(given[n].shape) for n, t in zip(small_names, group)}
                                   for group in small_out)
    sd = deltas["b_ada"]
    g_big["w_ada"] = g_w_ada
    for n in ["w_ada"] + big_names[1:] + big_names[:1]:
        if n == "w_in":
            gw_in, sems, land = pending
            done = [tot, sd] + [deltas[k] for k in ["w_ada"] + big_names[1:]]
            (gw_in,), (got,) = _ici_wait("ici_wait_w_in", [n], sems, [gw_in], [land], done)
            g_big[n] = _final_sum("final_w_in", pos, 1, gw_in, got, SHARD[n])
        g, d, nm, nv = _adamw("adamw_" + n, given[n][0], g_big[n], given["m_" + n][0], given["v_" + n][0])
        grads[n], deltas[n], new_m[n], new_v[n] = g[None], d[None], nm[None], nv[None]

    order = ["w_ada", "b_ada", "norm1_g", "w_in", "rel_bias", "ret_gn_g", "ret_gn_b", "w_ret_out",
             "w_att_out", "w_o", "norm2_g", "w_ff1", "w_ff2", "norm_f_g"]
    loss_out = _unpack_small(tot, "loss")[0, 0]
    return (loss_out, grad_x[None], *[grads[n] for n in order], *[deltas[n] for n in order],
            *[new_m[n] for n in order], *[new_v[n] for n in order])
```

```python
import math

import jax
import jax.numpy as jnp
import numpy as np
from jax import lax
from jax.experimental import pallas as pl
from jax.experimental.pallas import tpu as pltpu

F32 = jnp.float32
BF16 = jnp.bfloat16
I32 = jnp.int32

SEQ = 2048
D_MODEL = 1024
RET_HEADS = 4
RET_DK = 256
RET_DV = 512
RET_CHUNK = 128
RET_SUB = 2
RET_QK_W = RET_HEADS * RET_DK
RET_V_W = RET_HEADS * RET_DV
ATT_GROUPS = ((128, 1), (512, 4), (2048, 16))
ATT_HPG = 4
ATT_DH = 128
ATT_W = ATT_HPG * ATT_DH
ATT_BLK = 128
REL_BUCKETS = 32
REL_MAX_DIST = 2048
N_ATT_HEADS = 12
D_FF = 4 * D_MODEL
RMS_EPS = 1e-6
GN_EPS = 1e-5
ROPE_BASE = 10000.0
IN_COLS = 2 * RET_QK_W + 2 * RET_V_W + 9 * ATT_W + 2 * D_MODEL
OFF_Q, OFF_K, OFF_V, OFF_G = 0, RET_QK_W, 2 * RET_QK_W, 2 * RET_QK_W + RET_V_W
OFF_ATT = 2 * RET_QK_W + 2 * RET_V_W
OFF_GATE = OFF_ATT + 9 * ATT_W
N_CHIPS = 4
N_DEV = 8
ADA_COLS = 6 * D_MODEL

ADAM_LR = 0.001
ADAM_B1 = 0.9
ADAM_B2 = 0.999
ADAM_EPS = 1e-08
ADAM_WD = 0.01
ADAM_STEP = 10

VMEM_LIMIT_V7X = 56 * 1024 * 1024
MESH = pl.DeviceIdType.MESH


def _cparams(sem, collective_id=None):
    return pltpu.CompilerParams(dimension_semantics=sem, vmem_limit_bytes=VMEM_LIMIT_V7X,
                                collective_id=collective_id)


PAIR_COLLECTIVE_ID = 0


def _pair_barrier(sib):
    barrier = pltpu.get_barrier_semaphore()
    pl.semaphore_signal(barrier, inc=1, device_id=sib, device_id_type=MESH)
    pl.semaphore_wait(barrier, 1)


def _sigmoid(v):
    return 1.0 / (1.0 + jnp.exp(-v))


TM, TN = 1024, 1024


def _piece_chunks(piece, width):
    arr, stacked = piece
    return arr.shape[0] if stacked else arr.shape[1] // width


def _piece_spec(piece, rows, width, start, row_of, chunk_of):
    arr, stacked = piece
    last = _piece_chunks(piece, width) - 1

    def local(*ids):
        return jnp.clip(chunk_of(*ids) - start, 0, last)

    def row(*ids):
        rel = chunk_of(*ids) - start
        return jnp.where(jnp.logical_and(rel >= 0, rel <= last), row_of(*ids), 0)

    if stacked:
        return pl.BlockSpec((None, rows, width), lambda *ids: (local(*ids), row(*ids), 0))
    return pl.BlockSpec((rows, width), lambda *ids: (row(*ids), local(*ids)))


def _piece_starts(pieces, width):
    return [sum(_piece_chunks(p, width) for p in pieces[:q]) for q in range(len(pieces))]


def _matmul(name, a, b, kind, m, n, k, outs, *, b_off=0, tm=TM, tn=TN, tk=1024,
            epilogue=None, extras=(), after=(), n_sums=0):
    tm, tn, tk = min(tm, m), min(tn, n), min(tk, k)
    nk = k // tk
    pieces = a if isinstance(a, list) else [(a, False)]
    starts = _piece_starts(pieces, tk)
    if kind == "nn":
        a_specs = [pl.BlockSpec((tm, tk), lambda i, j, kk: (i, kk))]
        b_spec = pl.BlockSpec((tk, tn), lambda i, j, kk: (kk, b_off // tn + j))
        dn = (((1,), (0,)), ((), ()))
    elif kind == "nt":
        a_specs = [_piece_spec(p, tm, tk, st, lambda i, j, kk: i, lambda i, j, kk: kk)
                   for p, st in zip(pieces, starts)]
        b_spec = pl.BlockSpec((tn, tk), lambda i, j, kk: (j, b_off // tk + kk))
        dn = (((1,), (1,)), ((), ()))
    else:
        a_specs = [pl.BlockSpec((tk, tm), lambda i, j, kk: (kk, i))]
        b_spec = pl.BlockSpec((tk, tn), lambda i, j, kk: (kk, j))
        dn = (((0,), (0,)), ((), ()))
    n_a, n_ex, n_out = len(pieces), len(extras), len(outs)
    if epilogue is None:
        epilogue = lambda acc: (acc,)

    assert n_sums == 0 or tn == n

    def finish(acc, ex_refs, out_refs, first_rows):
        res = epilogue(acc, *[r[...] for r in ex_refs])
        for r, v in zip(out_refs[:n_out], res[:n_out]):
            r[...] = v.astype(r.dtype)
        for r, v in zip(out_refs[n_out:], res[n_out:]):
            @pl.when(first_rows)
            def _(r=r, v=v):
                r[...] = v

            @pl.when(jnp.logical_not(first_rows))
            def _(r=r, v=v):
                r[...] += v

    n_in = n_a + 1 + n_ex + len(after)

    def kern(*refs):
        a_refs, b_ref = refs[:n_a], refs[n_a]
        ex_refs = refs[n_a + 1:n_a + 1 + n_ex]
        out_refs = refs[n_in:n_in + n_out + n_sums]
        first_rows, kk = pl.program_id(0) == 0, pl.program_id(2)
        dot = lambda a_ref: lax.dot_general(a_ref[...], b_ref[...], dn, preferred_element_type=F32)
        if nk == 1:
            finish(dot(a_refs[0]), ex_refs, out_refs, first_rows)
            return
        acc_ref = refs[n_in + n_out + n_sums]
        if n_a == 1:
            part = dot(a_refs[0])

            @pl.when(kk == 0)
            def _():
                acc_ref[...] = part

            @pl.when(kk > 0)
            def _():
                acc_ref[...] += part
        else:
            @pl.when(kk == 0)
            def _():
                acc_ref[...] = jnp.zeros_like(acc_ref)

            for q in range(n_a):
                @pl.when(jnp.logical_and(kk >= starts[q], kk < starts[q] + _piece_chunks(pieces[q], tk)))
                def _(q=q):
                    acc_ref[...] += dot(a_refs[q])

        @pl.when(kk == nk - 1)
        def _():
            finish(acc_ref[...], ex_refs, out_refs, first_rows)

    in_specs = a_specs + [b_spec] + [pl.BlockSpec(bs, im) for _, bs, im in extras]
    in_specs += [pl.BlockSpec(memory_space=pl.ANY)] * len(after)
    sem = ("arbitrary",) * 3 if n_sums else ("parallel", "parallel", "arbitrary")
    return pl.pallas_call(
        kern, name=name, grid=(m // tm, n // tn, nk), in_specs=in_specs,
        out_specs=[pl.BlockSpec((tm, tn), lambda i, j, kk: (i, j)) for _ in outs]
        + [pl.BlockSpec((1, tn), lambda i, j, kk: (0, 0))] * n_sums,
        out_shape=[jax.ShapeDtypeStruct((m, n), dt) for dt in outs]
        + [jax.ShapeDtypeStruct((1, n), F32)] * n_sums,
        scratch_shapes=[] if nk == 1 else [pltpu.VMEM((tm, tn), F32)],
        compiler_params=_cparams(sem),
    )(*[p[0] for p in pieces], b, *[e[0] for e in extras], *after)


def _ici_copies(psum_ref, recv_ref, s_sem, r_sem, axis, shard_shape):
    x, y, c = _me()
    hr, cw = shard_shape[0] // 2, shard_shape[1]
    pick = lambda sems, j: sems[j] if isinstance(sems, (list, tuple)) else sems.at[j]
    copies = []
    for j, (fx, fy) in enumerate(_CHIP_FLIPS):
        chip = 2 * (x ^ fx) + (y ^ fy)
        src = psum_ref.at[:, pl.ds(chip * cw, cw)] if axis == 1 else psum_ref.at[pl.ds(chip * hr, hr), :]
        copies.append(pltpu.make_async_remote_copy(
            src_ref=src, dst_ref=recv_ref.at[j], send_sem=pick(s_sem, j), recv_sem=pick(r_sem, j),
            device_id=(x ^ fx, y ^ fy, c), device_id_type=MESH))
    return copies


_HBM_SPEC = pl.BlockSpec(memory_space=pltpu.HBM)
_SEM_SPEC = pl.BlockSpec(memory_space=pltpu.SEMAPHORE)


def _split_ici_copies(names, p_refs, land_refs, sems):
    copies = []
    for i, n in enumerate(names):
        copies += _ici_copies(p_refs[i], land_refs[i], list(sems[6 * i:6 * i + 3]),
                              list(sems[6 * i + 3:6 * i + 6]), dict(BIG)[n], SHARD[n])
    return copies


def _ici_start(name, names, psums):
    nw, ns = len(names), 6 * len(names)
    lands = [lax.empty((3, SHARD[n][0] // 2, SHARD[n][1]), BF16) for n in names]

    def body(*refs):
        for cp in _split_ici_copies(names, refs[:nw], refs[nw:2 * nw], refs[2 * nw:2 * nw + ns]):
            cp.start()
        token = refs[-1]
        token[...] = jnp.zeros_like(token)

    res = pl.pallas_call(
        body, name=name,
        out_shape=(pltpu.SemaphoreType.DMA(()),) * ns
        + tuple(pltpu.HBM(a.shape, BF16) for a in list(psums) + lands)
        + (jax.ShapeDtypeStruct((8, 128), F32),),
        in_specs=(_HBM_SPEC,) * (2 * nw),
        out_specs=(_SEM_SPEC,) * ns + (_HBM_SPEC,) * (2 * nw) + (pl.BlockSpec(memory_space=pltpu.VMEM),),
        input_output_aliases={k: ns + k for k in range(2 * nw)},
        compiler_params=pltpu.CompilerParams(has_side_effects=pltpu.SideEffectType.DATAFLOW_SIDE_EFFECTING),
    )(*[pltpu.with_memory_space_constraint(a, pltpu.HBM) for a in list(psums) + lands])
    return res[:ns], res[ns:ns + nw], res[ns + nw:ns + 2 * nw], res[-1]


def _ici_wait(name, names, sems, p_thru, land_thru, after):
    nw, ns = len(names), 6 * len(names)

    def body(*refs):
        for cp in _split_ici_copies(names, refs[:nw], refs[nw:2 * nw], refs[2 * nw:2 * nw + ns]):
            cp.wait_send()
            cp.wait_recv()

    res = pl.pallas_call(
        body, name=name,
        out_shape=tuple(pltpu.HBM(a.shape, BF16) for a in list(p_thru) + list(land_thru)),
        in_specs=(_HBM_SPEC,) * (2 * nw) + (_SEM_SPEC,) * ns + (pl.BlockSpec(memory_space=pl.ANY),) * len(after),
        out_specs=(_HBM_SPEC,) * (2 * nw), input_output_aliases={k: k for k in range(2 * nw)},
        compiler_params=pltpu.CompilerParams(has_side_effects=pltpu.SideEffectType.DATAFLOW_SIDE_EFFECTING),
    )(*p_thru, *land_thru, *sems, *after)
    return res[:nw], res[nw:]


def _where_am_i():
    x, y, c = _me()
    return jnp.stack([c, 2 * x + y]).astype(I32)


def _sibling():
    x, y, c = _me()
    return (x, y, 1 - c)


N_SEND_SLOTS = 2


def _matmul_tn_pair(name, pos, a, b, m, n, k, shard_rows, *, tm, tn, tk):
    hr = shard_rows // 2
    tm, tn, tk = min(tm, hr), min(tn, n), min(tk, k)
    tph = hr // tm
    nt, nj, nk = (m // 2) // tm, n // tn, k // tk
    n_tiles = nt * nj

    def row_block(p, t, pos_ref):
        half = jnp.where(p == 0, 1 - pos_ref[0], pos_ref[0])
        return (t // tph) * (2 * tph) + half * tph + t % tph

    pieces = b if isinstance(b, list) else [(b, False)]
    starts = _piece_starts(pieces, tn)
    n_b = len(pieces)

    def kern(pos_ref, a_ref, *rest):
        b_refs = rest[:n_b]
        o_ref, acc_ref, send_buf, land_buf, s_sem, r_sem = rest[n_b:]
        p, t, j, kk = pl.program_id(0), pl.program_id(1), pl.program_id(2), pl.program_id(3)
        idx = t * nj + j
        sib = _sibling()

        @pl.when(jnp.logical_and(jnp.logical_and(p == 0, idx == 0), kk == 0))
        def _():
            _pair_barrier(sib)

        def copy(i):
            return pltpu.make_async_remote_copy(
                src_ref=send_buf.at[i % N_SEND_SLOTS], dst_ref=land_buf.at[i], send_sem=s_sem.at[i],
                recv_sem=r_sem.at[i], device_id=sib, device_id_type=MESH)

        @pl.when(kk == 0)
        def _():
            acc_ref[...] = jnp.zeros_like(acc_ref)

        for q in range(n_b):
            @pl.when(jnp.logical_and(j >= starts[q], j < starts[q] + _piece_chunks(pieces[q], tn)))
            def _(q=q):
                acc_ref[...] += lax.dot_general(a_ref[...], b_refs[q][...], _TN, preferred_element_type=F32)

        @pl.when(jnp.logical_and(kk == nk - 1, p == 0))
        def _():
            @pl.when(idx >= N_SEND_SLOTS)
            def _():
                copy(idx - N_SEND_SLOTS).wait_send()

            send_buf[idx % N_SEND_SLOTS] = acc_ref[...].astype(BF16)
            copy(idx).start()

        @pl.when(jnp.logical_and(kk == nk - 1, p == 1))
        def _():
            copy(idx).wait_recv()
            o_ref[...] = (acc_ref[...] + land_buf[idx].astype(F32)).astype(BF16)

        @pl.when(jnp.logical_and(jnp.logical_and(p == 1, idx == n_tiles - 1), kk == nk - 1))
        def _():
            for i in range(max(n_tiles - N_SEND_SLOTS, 0), n_tiles):
                copy(i).wait_send()

    grid_spec = pltpu.PrefetchScalarGridSpec(
        num_scalar_prefetch=1, grid=(2, nt, nj, nk),
        in_specs=[pl.BlockSpec((tk, tm), lambda p, t, j, kk, pos_ref: (kk, row_block(p, t, pos_ref)))]
        + [_piece_spec(pc, tk, tn, st, lambda p, t, j, kk, pos_ref: kk, lambda p, t, j, kk, pos_ref: j)
           for pc, st in zip(pieces, starts)],
        out_specs=pl.BlockSpec((tm, tn), lambda p, t, j, kk, pos_ref: (p * t, p * j)),
        scratch_shapes=[pltpu.VMEM((tm, tn), F32), pltpu.VMEM((N_SEND_SLOTS, tm, tn), BF16),
                        pltpu.VMEM((n_tiles, tm, tn), BF16),
                        pltpu.SemaphoreType.DMA((n_tiles,)), pltpu.SemaphoreType.DMA((n_tiles,))])
    return pl.pallas_call(
        kern, name=name, grid_spec=grid_spec, out_shape=jax.ShapeDtypeStruct((m // 2, n), BF16),
        compiler_params=_cparams(("arbitrary",) * 4, PAIR_COLLECTIVE_ID),
    )(pos, a, *[pc[0] for pc in pieces])


def _rope_tables():
    half = RET_DK // 2
    f32 = np.float32
    inv = np.power(f32(ROPE_BASE), -np.arange(half, dtype=f32) / f32(half)).astype(f32)
    ang = (np.arange(SEQ, dtype=f32)[:, None] * inv[None, :]).astype(f32)
    return jnp.asarray(np.cos(ang).astype(f32)), jnp.asarray(np.sin(ang).astype(f32))


def _decay_tables():
    c = RET_CHUNK
    f32 = np.float32
    log_g = np.log1p(-np.power(f32(2.0), f32(-5.0) - np.arange(RET_HEADS, dtype=f32))).astype(f32)
    idx = np.arange(c, dtype=f32)
    rel = idx[:, None] - idx[None, :]
    din = np.where(rel >= 0, np.exp(log_g[:, None, None] * np.maximum(rel, f32(0.0))), f32(0.0)).astype(f32)
    qd = np.exp(log_g[:, None] * (idx + f32(1.0))).astype(f32)[:, :, None]
    kd = np.exp(log_g[:, None] * (f32(c) - f32(1.0) - idx)).astype(f32)[:, :, None]
    cd = np.exp(log_g * f32(c)).astype(f32)
    return jnp.asarray(din), jnp.asarray(qd), jnp.asarray(kd), jnp.asarray(cd)


def _t5_bucket(dist):
    max_exact = REL_BUCKETS // 2
    d_f = jnp.maximum(dist, 1).astype(F32)
    large = max_exact + (jnp.log(d_f / max_exact) / math.log(REL_MAX_DIST / max_exact)
                         * (REL_BUCKETS - max_exact)).astype(I32)
    large = jnp.minimum(large, REL_BUCKETS - 1)
    return jnp.where(dist < max_exact, dist, large)


def _bucket_tables():
    qi = jnp.arange(ATT_BLK)[:, None]
    kj = jnp.arange(2 * ATT_BLK)[None, :]
    dist = jnp.clip(ATT_BLK + qi - kj, 0, ATT_BLK)
    return jnp.stack([_t5_bucket(dist * dil) for _, dil in ATT_GROUPS]).astype(I32)


def _retention_fwd(rqk, rv, rg, gn_g, gn_b, din, qd, kd, cd):
    nc = SEQ // RET_CHUNK
    c, dk, dv = RET_CHUNK, RET_DK, RET_DV

    def kern(q_ref, k_ref, v_ref, rg_ref, g_ref, b_ref, din_ref, qd_ref, kd_ref, cd_ref,
             o_ref, st_ref, gated_ref, state):
        n = pl.program_id(0)

        @pl.when(n == 0)
        def _():
            state[...] = jnp.zeros_like(state)

        for sub in range(RET_SUB):
            rows = slice(sub * c, (sub + 1) * c)
            for h in range(RET_HEADS):
                q, k = q_ref[rows, h * dk:(h + 1) * dk], k_ref[rows, h * dk:(h + 1) * dk]
                v = v_ref[rows, h * dv:(h + 1) * dv]
                s_b = state[h].astype(BF16)
                st_ref[h, sub] = s_b
                a = lax.dot_general(q, k, _NT, preferred_element_type=F32) * din_ref[h]
                o = jnp.dot(a.astype(BF16), v, preferred_element_type=F32)
                o += jnp.dot(q, s_b, preferred_element_type=F32) * qd_ref[h]
                v_cols = slice(h * dv, (h + 1) * dv)
                o_ref[rows, v_cols] = o
                nrm, _ = _gn_parts(o)
                gate = rg_ref[rows, v_cols].astype(F32)
                gated_ref[rows, v_cols] = ((gate * _sigmoid(gate))
                                           * (nrm * g_ref[:, v_cols] + b_ref[:, v_cols])).astype(BF16)
                kk = (k.astype(F32) * kd_ref[h]).astype(BF16)
                state[h] = state[h] * cd_ref[h] + lax.dot_general(kk, v, _TN, preferred_element_type=F32)

    whole = lambda a: pl.BlockSpec(a.shape, lambda n: (0,) * a.ndim)
    cs = RET_SUB * c
    rows_v = pl.BlockSpec((cs, RET_V_W), lambda n: (n, 0))
    return pl.pallas_call(
        kern, name="retention_fwd", grid=(nc // RET_SUB,),
        in_specs=[
            pl.BlockSpec((cs, RET_QK_W), lambda n: (n, 0)),
            pl.BlockSpec((cs, RET_QK_W), lambda n: (n, 1)),
            rows_v, rows_v, whole(gn_g), whole(gn_b),
            whole(din), whole(qd), whole(kd),
            pl.BlockSpec(memory_space=pltpu.SMEM),
        ],
        out_specs=[
            rows_v,
            pl.BlockSpec((RET_HEADS, RET_SUB, dk, dv), lambda n: (0, n, 0, 0)),
            rows_v,
        ],
        out_shape=[
            jax.ShapeDtypeStruct((SEQ, RET_V_W), F32),
            jax.ShapeDtypeStruct((RET_HEADS, nc, dk, dv), BF16),
            jax.ShapeDtypeStruct((SEQ, RET_V_W), BF16),
        ],
        scratch_shapes=[pltpu.VMEM((RET_HEADS, dk, dv), F32)],
        compiler_params=_cparams(("arbitrary",)),
    )(rqk, rqk, rv, rg, gn_g, gn_b, din, qd, kd, cd)


def _retention_bwd(rqk, rv, states, d_gated, ro, rg, gn_g, gn_b, din, qd, kd, cd, cos, sin):
    nc = SEQ // RET_CHUNK
    c, dk, dv = RET_CHUNK, RET_DK, RET_DV
    half = dk // 2
    last = nc // RET_SUB - 1

    def unrot(g, cs, sn):
        g1, g2 = g[:, :half], g[:, half:]
        return jnp.concatenate([g1 * cs + g2 * sn, g2 * cs - g1 * sn], axis=-1)

    def kern(q_ref, k_ref, v_ref, st_ref, dg_ref, ro_ref, rg_ref, g_ref, b_ref, din_ref, qd_ref, kd_ref,
             cd_ref, cos_ref, sin_ref, out_ref, drg_ref, dgn_g_ref, dgn_b_ref, dstate):
        step = pl.program_id(0)

        @pl.when(step == 0)
        def _():
            dstate[...] = jnp.zeros_like(dstate)
            dgn_g_ref[...] = jnp.zeros_like(dgn_g_ref)
            dgn_b_ref[...] = jnp.zeros_like(dgn_b_ref)

        for sub in reversed(range(RET_SUB)):
            rows = slice(sub * c, (sub + 1) * c)
            cs, sn = cos_ref[rows, :], sin_ref[rows, :]
            for h in range(RET_HEADS):
                qk_cols, v_cols = slice(h * dk, (h + 1) * dk), slice(h * dv, (h + 1) * dv)
                q, k, v = q_ref[rows, qk_cols], k_ref[rows, qk_cols], v_ref[rows, v_cols]
                s_b = st_ref[h, sub]
                nrm, rstd = _gn_parts(ro_ref[rows, v_cols])
                gate, dg = rg_ref[rows, v_cols].astype(F32), dg_ref[rows, v_cols].astype(F32)
                sg = _sigmoid(gate)
                gn_gain = g_ref[:, v_cols]
                drg_ref[rows, v_cols] = (dg * (nrm * gn_gain + b_ref[:, v_cols])
                                         * (sg * (1.0 + gate * (1.0 - sg)))).astype(BF16)
                d_ron = dg * (gate * sg)
                dgn_g_ref[:, v_cols] += jnp.sum(d_ron * nrm, axis=0, keepdims=True)
                dgn_b_ref[:, v_cols] += jnp.sum(d_ron, axis=0, keepdims=True)
                d_n = d_ron * gn_gain
                d_o = rstd * (d_n - jnp.mean(d_n, axis=-1, keepdims=True)
                              - nrm * jnp.mean(d_n * nrm, axis=-1, keepdims=True))
                d_ob = d_o.astype(BF16)
                d_oq = (d_o * qd_ref[h]).astype(BF16)
                ds_b = dstate[h].astype(BF16)
                din_m = din_ref[h]
                a_b = (lax.dot_general(q, k, _NT, preferred_element_type=F32) * din_m).astype(BF16)
                kk = (k.astype(F32) * kd_ref[h]).astype(BF16)
                d_v = lax.dot_general(a_b, d_ob, _TN, preferred_element_type=F32)
                d_v += jnp.dot(kk, ds_b, preferred_element_type=F32)
                d_a = (lax.dot_general(d_ob, v, _NT, preferred_element_type=F32) * din_m).astype(BF16)
                d_q = jnp.dot(d_a, k, preferred_element_type=F32)
                d_q += lax.dot_general(d_oq, s_b, _NT, preferred_element_type=F32)
                d_k = lax.dot_general(d_a, q, _TN, preferred_element_type=F32)
                d_k += lax.dot_general(v, ds_b, _NT, preferred_element_type=F32) * kd_ref[h]
                dstate[h] = dstate[h] * cd_ref[h] + lax.dot_general(q, d_oq, _TN,
                                                                    preferred_element_type=F32)
                out_ref[rows, h * dk:(h + 1) * dk] = unrot(d_q, cs, sn).astype(BF16)
                out_ref[rows, RET_QK_W + h * dk:RET_QK_W + (h + 1) * dk] = (
                    unrot(d_k, cs, sn) * (RET_DK ** -0.5)).astype(BF16)
                out_ref[rows, 2 * RET_QK_W + h * dv:2 * RET_QK_W + (h + 1) * dv] = d_v.astype(BF16)

    whole = lambda a: pl.BlockSpec(a.shape, lambda n: (0,) * a.ndim)
    rs = RET_SUB * c
    rows_v = pl.BlockSpec((rs, RET_V_W), lambda n: (last - n, 0))
    return pl.pallas_call(
        kern, name="retention_bwd", grid=(nc // RET_SUB,),
        in_specs=[
            pl.BlockSpec((rs, RET_QK_W), lambda n: (last - n, 0)),
            pl.BlockSpec((rs, RET_QK_W), lambda n: (last - n, 1)),
            rows_v,
            pl.BlockSpec((RET_HEADS, RET_SUB, dk, dv), lambda n: (0, last - n, 0, 0)),
            rows_v, rows_v, rows_v, whole(gn_g), whole(gn_b),
            whole(din), whole(qd), whole(kd),
            pl.BlockSpec(memory_space=pltpu.SMEM),
            pl.BlockSpec((rs, half), lambda n: (last - n, 0)),
            pl.BlockSpec((rs, half), lambda n: (last - n, 0)),
        ],
        out_specs=[pl.BlockSpec((rs, 2 * RET_QK_W + RET_V_W), lambda n: (last - n, 0)), rows_v,
                   whole(gn_g), whole(gn_b)],
        out_shape=[jax.ShapeDtypeStruct((SEQ, 2 * RET_QK_W + RET_V_W), BF16),
                   jax.ShapeDtypeStruct((SEQ, RET_V_W), BF16),
                   jax.ShapeDtypeStruct((1, RET_V_W), F32), jax.ShapeDtypeStruct((1, RET_V_W), F32)],
        scratch_shapes=[pltpu.VMEM((RET_HEADS, dk, dv), F32)],
        compiler_params=_cparams(("arbitrary",)),
    )(rqk, rqk, rv, states, d_gated, ro, rg, gn_g, gn_b, din, qd, kd, cd, cos, sin)


def _bias_build(rel_bias, buckets):
    ng = len(ATT_GROUPS)

    def kern(tab_ref, bkt_ref, o_ref):
        g, h = pl.program_id(0), pl.program_id(1)
        bkt = bkt_ref[...]
        acc = jnp.zeros(bkt.shape, F32)
        for b in range(REL_BUCKETS):
            acc = jnp.where(bkt == b, tab_ref[b, g * ATT_HPG + h], acc)
        o_ref[...] = acc

    return pl.pallas_call(
        kern, name="bias_build", grid=(ng, ATT_HPG),
        in_specs=[pl.BlockSpec(memory_space=pltpu.SMEM),
                  pl.BlockSpec((None, ATT_BLK, 2 * ATT_BLK), lambda g, h: (g, 0, 0))],
        out_specs=pl.BlockSpec((None, None, ATT_BLK, 2 * ATT_BLK), lambda g, h: (g, h, 0, 0)),
        out_shape=jax.ShapeDtypeStruct((ng, ATT_HPG, ATT_BLK, 2 * ATT_BLK), F32),
        compiler_params=_cparams(("arbitrary", "arbitrary")),
    )(rel_bias, buckets)


def _bias_grad(dsb, buckets):
    ng = len(ATT_GROUPS)

    def kern(ds_ref, bkt_ref, o_ref):
        g, h = pl.program_id(0), pl.program_id(1)
        bkt, ds = bkt_ref[...], ds_ref[...]
        for b in range(REL_BUCKETS):
            o_ref[b, g * ATT_HPG + h] = jnp.sum(jnp.where(bkt == b, ds, 0.0))

    return pl.pallas_call(
        kern, name="bias_grad", grid=(ng, ATT_HPG),
        in_specs=[pl.BlockSpec((None, None, ATT_BLK, 2 * ATT_BLK), lambda g, h: (g, h, 0, 0)),
                  pl.BlockSpec((None, ATT_BLK, 2 * ATT_BLK), lambda g, h: (g, 0, 0))],
        out_specs=pl.BlockSpec(memory_space=pltpu.SMEM),
        out_shape=jax.ShapeDtypeStruct((REL_BUCKETS, N_ATT_HEADS), F32),
        compiler_params=_cparams(("arbitrary", "arbitrary")),
    )(dsb, buckets)


_NT = (((1,), (1,)), ((), ()))
_TN = (((0,), (0,)), ((), ()))
_ATT_SCALE = ATT_DH ** -0.5


def _window_mask(has_prev):
    qi = lax.broadcasted_iota(I32, (ATT_BLK, 2 * ATT_BLK), 0)
    kj = lax.broadcasted_iota(I32, (ATT_BLK, 2 * ATT_BLK), 1)
    prev_ok = jnp.logical_and(jnp.logical_and(kj < ATT_BLK, kj >= qi), has_prev)
    return jnp.logical_or(prev_ok, jnp.logical_and(kj >= ATT_BLK, qi >= kj - ATT_BLK))


def _head_specs(col0):
    return pl.BlockSpec((SEQ, ATT_DH), lambda h: (0, col0 + h))


def _sub_rows(start, size, dil):
    return pl.ds(start, size) if dil == 1 else pl.ds(start, size, stride=dil)


def _att_blocks(dil):
    nb = SEQ // dil // ATT_BLK
    return [(r + dil * n * ATT_BLK, n > 0, n + 1 < nb) for r in range(dil) for n in range(nb)]


def _att_fwd(gi, dil, qkv, bias, after=()):
    blk, dh = ATT_BLK, ATT_DH
    pad = dil * blk
    col0 = 3 * ATT_HPG * gi

    def kern(q_ref, k_ref, v_ref, b_ref, *rest):
        o_ref, l_ref, qf, kpad, vpad = rest[len(after):]
        zero = jnp.zeros((pad, dh), F32)
        kpad[0:pad, :] = zero
        vpad[0:pad, :] = zero
        kpad[pad:, :] = k_ref[...].astype(F32)
        vpad[pad:, :] = v_ref[...].astype(F32)
        qf[...] = q_ref[...].astype(F32)
        bias_m = b_ref[...]
        for start, has_prev, _ in _att_blocks(dil):
            rows, window = _sub_rows(start, blk, dil), _sub_rows(start, 2 * blk, dil)
            q = qf[rows, :].astype(BF16)
            kw, vw = kpad[window, :].astype(BF16), vpad[window, :].astype(BF16)
            valid = _window_mask(has_prev)
            s = lax.dot_general(q, kw, _NT, preferred_element_type=F32) * _ATT_SCALE + bias_m
            s = jnp.where(valid, s, -1e30)
            mx = jnp.max(s, axis=-1, keepdims=True)
            e = jnp.exp(s - mx)
            den = jnp.sum(e, axis=-1, keepdims=True)
            o_ref[rows, :] = jnp.dot((e / den).astype(BF16), vw, preferred_element_type=F32)
            l_ref[rows, :] = jnp.broadcast_to(mx + jnp.log(den), (blk, dh))

    return pl.pallas_call(
        kern, name=f"att_fwd_g{gi}", grid=(ATT_HPG,),
        in_specs=[_head_specs(col0), _head_specs(col0 + ATT_HPG), _head_specs(col0 + 2 * ATT_HPG),
                  pl.BlockSpec((None, None, blk, 2 * blk), lambda h: (gi, h, 0, 0))]
        + [pl.BlockSpec(memory_space=pl.ANY)] * len(after),
        out_specs=[_head_specs(0), _head_specs(0)],
        out_shape=[jax.ShapeDtypeStruct((SEQ, ATT_W), F32), jax.ShapeDtypeStruct((SEQ, ATT_W), F32)],
        scratch_shapes=[pltpu.VMEM((SEQ, dh), F32), pltpu.VMEM((SEQ + pad, dh), F32),
                        pltpu.VMEM((SEQ + pad, dh), F32)],
        compiler_params=_cparams(("arbitrary",)),
    )(qkv, qkv, qkv, bias, *after)


def _att_bwd(gi, dil, qkv, d_att, lse, dd, bias):
    blk, dh = ATT_BLK, ATT_DH
    pad = dil * blk
    col0 = 3 * ATT_HPG * gi

    def kern(q_ref, k_ref, v_ref, do_ref, l_ref, d_ref, b_ref, dqkv_ref, dsb_ref,
             qf, kpad, vpad, dq_s, dkpad, dvpad):
        zero = jnp.zeros((pad, dh), F32)
        kpad[0:pad, :] = zero
        vpad[0:pad, :] = zero
        kpad[pad:, :] = k_ref[...].astype(F32)
        vpad[pad:, :] = v_ref[...].astype(F32)
        qf[...] = q_ref[...].astype(F32)
        dkpad[...] = jnp.zeros_like(dkpad)
        dvpad[...] = jnp.zeros_like(dvpad)
        bias_m = b_ref[...]
        ds_sum = jnp.zeros((blk, 2 * blk), F32)

        for start, has_prev, _ in _att_blocks(dil):
            rows, window = _sub_rows(start, blk, dil), _sub_rows(start, 2 * blk, dil)
            q, d_o = qf[rows, :].astype(BF16), do_ref[rows, :].astype(BF16)
            kw, vw = kpad[window, :].astype(BF16), vpad[window, :].astype(BF16)
            lrow, drow = l_ref[rows, :][:, :1], d_ref[rows, :][:, :1]
            valid = _window_mask(has_prev)
            s = lax.dot_general(q, kw, _NT, preferred_element_type=F32) * _ATT_SCALE + bias_m
            p = jnp.where(valid, jnp.exp(jnp.where(valid, s, -1e30) - lrow), 0.0)
            dp = lax.dot_general(d_o, vw, _NT, preferred_element_type=F32)
            ds = p * (dp - drow)
            ds_b = ds.astype(BF16)
            dq_s[rows, :] = jnp.dot(ds_b, kw, preferred_element_type=F32) * _ATT_SCALE
            dkpad[window, :] += lax.dot_general(ds_b, q, _TN, preferred_element_type=F32) * _ATT_SCALE
            dvpad[window, :] += lax.dot_general(p.astype(BF16), d_o, _TN, preferred_element_type=F32)
            ds_sum = ds_sum + ds
        dsb_ref[...] = ds_sum

        dqkv_ref[0] = dq_s[...].astype(BF16)
        dqkv_ref[1] = dkpad[pad:, :].astype(BF16)
        dqkv_ref[2] = dvpad[pad:, :].astype(BF16)

    return pl.pallas_call(
        kern, name=f"att_bwd_g{gi}", grid=(ATT_HPG,),
        in_specs=[_head_specs(col0), _head_specs(col0 + ATT_HPG), _head_specs(col0 + 2 * ATT_HPG),
                  _head_specs(0), _head_specs(0), _head_specs(0),
                  pl.BlockSpec((None, None, blk, 2 * blk), lambda h: (gi, h, 0, 0))],
        out_specs=[pl.BlockSpec((3, SEQ, dh), lambda h: (0, 0, h)),
                   pl.BlockSpec((None, blk, 2 * blk), lambda h: (h, 0, 0))],
        out_shape=[jax.ShapeDtypeStruct((3, SEQ, ATT_W), BF16),
                   jax.ShapeDtypeStruct((ATT_HPG, blk, 2 * blk), F32)],
        scratch_shapes=[pltpu.VMEM((SEQ, dh), F32), pltpu.VMEM((SEQ + pad, dh), F32),
                        pltpu.VMEM((SEQ + pad, dh), F32), pltpu.VMEM((SEQ, dh), F32),
                        pltpu.VMEM((SEQ + pad, dh), F32), pltpu.VMEM((SEQ + pad, dh), F32)],
        compiler_params=_cparams(("arbitrary",)),
    )(qkv, qkv, qkv, d_att, lse, dd, bias)


def _rms_parts(x):
    r = lax.rsqrt(jnp.mean(x * x, axis=-1, keepdims=True) + RMS_EPS)
    return x * r, r


def _rms_bwd(d_xhat, xhat, r):
    return r * (d_xhat - xhat * jnp.mean(d_xhat * xhat, axis=-1, keepdims=True))


def _prenorm_fwd(name, x, gain, shift, scale, tm=256):
    def kern(x_ref, g_ref, sh_ref, sc_ref, o_ref):
        xhat, _ = _rms_parts(x_ref[...])
        o_ref[...] = ((xhat * g_ref[...]) * (1.0 + sc_ref[...]) + sh_ref[...]).astype(BF16)

    rows = pl.BlockSpec((tm, D_MODEL), lambda i: (i, 0))
    vec = pl.BlockSpec((1, D_MODEL), lambda i: (0, 0))
    return pl.pallas_call(
        kern, name=name, grid=(x.shape[0] // tm,), in_specs=[rows, vec, vec, vec], out_specs=rows,
        out_shape=jax.ShapeDtypeStruct(x.shape, BF16), compiler_params=_cparams(("parallel",)),
    )(x, gain, shift, scale)


def _prenorm_bwd_epi(d_h, x, resid, gain, scale, branch=None, gate=None):
    xhat, r = _rms_parts(x)
    nrm = xhat * gain
    d_n = d_h * (1.0 + scale)
    dx = _rms_bwd(d_n * gain, xhat, r) + resid
    sums = (jnp.sum(d_h, axis=0, keepdims=True), jnp.sum(d_h * nrm, axis=0, keepdims=True),
            jnp.sum(d_n * xhat, axis=0, keepdims=True))
    if branch is None:
        return (dx,) + sums
    return (dx, dx * gate) + sums + (jnp.sum(dx * branch, axis=0, keepdims=True),)


def _row_operands(tm, rows, vecs):
    return ([(a, (tm, D_MODEL), lambda i, j, kk: (i, 0)) for a in rows]
            + [(v, (1, D_MODEL), lambda i, j, kk: (0, 0)) for v in vecs])


def _gn_parts(ro):
    mu = jnp.mean(ro, axis=-1, keepdims=True)
    cen = ro - mu
    rstd = lax.rsqrt(jnp.mean(cen * cen, axis=-1, keepdims=True) + GN_EPS)
    return cen * rstd, rstd


MERGE_TM = 512


def _att_out(os_, ls_, w_att_out, gates, ret_out):
    tm = MERGE_TM

    def kern(o0, o1, o2, l0, l1, l2, w_ref, ga_ref, gb_ref, ro_ref, att_ref, attb_ref, lse_ref,
             ao_ref, mg_ref):
        l0v, l1v, l2v = l0[...], l1[...], l2[...]
        mx = jnp.maximum(jnp.maximum(l0v, l1v), l2v)
        e0, e1, e2 = jnp.exp(l0v - mx), jnp.exp(l1v - mx), jnp.exp(l2v - mx)
        den = e0 + e1 + e2
        att = (e0 / den) * o0[...] + (e1 / den) * o1[...] + (e2 / den) * o2[...]
        att_b = att.astype(BF16)
        att_ref[...] = att
        attb_ref[...] = att_b
        lse_ref[...] = mx + jnp.log(den)
        att_out = jnp.dot(att_b, w_ref[...], preferred_element_type=F32)
        ao_ref[...], merged = _merge_fwd_epi(att_out, ga_ref[...], gb_ref[...], ro_ref[...])
        mg_ref[...] = merged.astype(BF16)

    rows_w = pl.BlockSpec((tm, ATT_W), lambda i: (i, 0))
    rows_d = pl.BlockSpec((tm, D_MODEL), lambda i: (i, 0))
    return pl.pallas_call(
        kern, name="att_out", grid=(SEQ // tm,),
        in_specs=[rows_w] * 6 + [pl.BlockSpec((ATT_W, D_MODEL), lambda i: (0, 0)), rows_d,
                                 pl.BlockSpec((tm, D_MODEL), lambda i: (i, 1)), rows_d],
        out_specs=[rows_w, rows_w, rows_w, rows_d, rows_d],
        out_shape=[jax.ShapeDtypeStruct((SEQ, ATT_W), F32), jax.ShapeDtypeStruct((SEQ, ATT_W), BF16),
                   jax.ShapeDtypeStruct((SEQ, ATT_W), F32), jax.ShapeDtypeStruct((SEQ, D_MODEL), F32),
                   jax.ShapeDtypeStruct((SEQ, D_MODEL), BF16)],
        compiler_params=_cparams(("parallel",)),
    )(*os_, *ls_, w_att_out, gates, gates, ret_out)


def _merge_operands(gates, ret_out, att_out=None):
    ops = [(gates, (MERGE_TM, D_MODEL), lambda i, j, kk: (i, 0)),
           (gates, (MERGE_TM, D_MODEL), lambda i, j, kk: (i, 1)),
           (ret_out, (MERGE_TM, D_MODEL), lambda i, j, kk: (i, 0))]
    if att_out is not None:
        ops.append((att_out, (MERGE_TM, D_MODEL), lambda i, j, kk: (i, 0)))
    return ops


def _merge_fwd_epi(att_out, ga, gb, ret_out):
    return att_out, _sigmoid(ga.astype(F32)) * ret_out + _sigmoid(gb.astype(F32)) * att_out


def _merge_bwd_epi(d_merged, ga, gb, ret_out, att_out):
    sa, sb = _sigmoid(ga.astype(F32)), _sigmoid(gb.astype(F32))
    return (d_merged * sa, d_merged * sb, d_merged * ret_out * (sa * (1.0 - sa)),
            d_merged * att_out * (sb * (1.0 - sb)))


def _att_out_bwd_epi(d_att, att):
    outs = []
    for h in range(ATT_HPG):
        sl = slice(h * ATT_DH, (h + 1) * ATT_DH)
        outs.append(jnp.broadcast_to(jnp.sum(d_att[:, sl] * att[:, sl], axis=-1, keepdims=True),
                                     (d_att.shape[0], ATT_DH)))
    return d_att, jnp.concatenate(outs, axis=-1)


def _loss_head_epi(branch, x_prev, target, gate, gain):
    x3 = x_prev + gate * branch
    xhat, r = _rms_parts(x3)
    err = xhat * gain - target
    d_y = err / D_MODEL
    loss = 0.5 * jnp.sum(jnp.mean(err * err, axis=-1, keepdims=True), axis=0, keepdims=True)
    d_x = _rms_bwd(d_y * gain, xhat, r)
    return (d_x, d_x * gate, jnp.broadcast_to(loss, (1, D_MODEL)),
            jnp.sum(d_y * xhat, axis=0, keepdims=True), jnp.sum(d_x * branch, axis=0, keepdims=True))


def _local_step(pos, x, target, mod, norm1_g, norm2_g, norm_f_g, rel_bias, gn_g, gn_b, w_in, rest_gather):
    sh1, sc1, g1, sh2, sc2, g2 = [mod[:, i * D_MODEL:(i + 1) * D_MODEL] for i in range(6)]
    cos, sin = _rope_tables()
    din, qd, kd, cd = _decay_tables()
    buckets = _bucket_tables()
    bias = _bias_build(rel_bias, buckets)
    dils = [d for _, d in ATT_GROUPS]

    h1 = _prenorm_fwd("prenorm1_fwd", x, norm1_g, sh1, sc1)

    qk_tn = 2 * RET_DK

    def rot_epi(acc, cs, sn, scale):
        half = RET_DK // 2
        outs = []
        for h0 in range(0, qk_tn, RET_DK):
            x1, x2 = acc[:, h0:h0 + half], acc[:, h0 + half:h0 + RET_DK]
            outs += [x1 * cs - x2 * sn, x1 * sn + x2 * cs]
        return (jnp.concatenate(outs, axis=-1) * scale,)

    qk_scale = jnp.concatenate([jnp.ones((1, RET_QK_W), F32),
                                jnp.full((1, RET_QK_W), RET_DK ** -0.5, F32)], axis=-1)
    rope_ex = [(cos, (TM, RET_DK // 2), lambda i, j, kk: (i, 0)),
               (sin, (TM, RET_DK // 2), lambda i, j, kk: (i, 0)),
               (qk_scale, (1, qk_tn), lambda i, j, kk: (0, j))]
    rest_sems, rest_shards, rest_fulls, rest_token = rest_gather
    behind = [rest_token]
    rv = _matmul("proj_rv", h1, w_in, "nn", SEQ, RET_V_W, D_MODEL, [BF16], b_off=OFF_V, tk=D_MODEL,
                 after=behind)[0]
    rg = _matmul("proj_rg", h1, w_in, "nn", SEQ, RET_V_W, D_MODEL, [BF16], b_off=OFF_G, tk=D_MODEL,
                 after=behind)[0]
    gates = _matmul("proj_gates", h1, w_in, "nn", SEQ, 2 * D_MODEL, D_MODEL, [BF16], b_off=OFF_GATE,
                    tn=512, tk=D_MODEL, after=behind)[0]
    aqkv = _matmul("proj_att", h1, w_in, "nn", SEQ, 9 * ATT_W, D_MODEL, [BF16], b_off=OFF_ATT,
                   tn=512, tk=D_MODEL, after=behind)[0]

    rqk = _matmul("proj_qk", h1, w_in, "nn", SEQ, 2 * RET_QK_W, D_MODEL, [BF16], b_off=OFF_Q,
                  tn=qk_tn, tk=D_MODEL, epilogue=rot_epi, extras=rope_ex, after=behind)[0]
    ro, states, gated = _retention_fwd(rqk, rv, rg, gn_g, gn_b, din, qd, kd, cd)
    os_, ls_ = [], []
    for gi in range(3):
        if gi == 2:
            rest_sems, rest_fulls, fwd_token = _gather_rest_forward(
                rest_sems, rest_shards, rest_fulls, [gated, gates] + os_)
        o_g, l_g = _att_fwd(gi, dils[gi], aqkv, bias, after=[fwd_token] if gi == 2 else ())
        os_.append(o_g)
        ls_.append(l_g)
    w_ret_out, w_att_out, w_o, w_ff1, w_ff2 = _gather_rest_end(rest_sems, rest_fulls, [os_[2]])
    ret_out = _matmul("ret_out", gated, w_ret_out, "nn", SEQ, D_MODEL, RET_V_W, [F32], tk=RET_V_W)[0]
    att, att_b, lse, att_out, merged = _att_out(os_, ls_, w_att_out, gates, ret_out)

    def mix_epi(acc, xt, g, gain, sh, sc):
        x_new = xt + g * acc
        xhat, _ = _rms_parts(x_new)
        return x_new, acc, (xhat * gain) * (1.0 + sc) + sh

    x2, mix, h2 = _matmul("mix_out", merged, w_o, "nn", SEQ, D_MODEL, D_MODEL, [F32, BF16, BF16],
                          epilogue=mix_epi, extras=_row_operands(TM, [x], [g1, norm2_g, sh2, sc2]))

    def relu2_epi(acc):
        r = jnp.maximum(acc, 0.0)
        return r * r, r

    act, relu_u = _matmul("ff1", h2, w_ff1, "nn", SEQ, D_FF, D_MODEL, [BF16, BF16], tk=D_MODEL,
                          epilogue=relu2_epi)
    d_x3, d_y2, loss, d_gf, d_g2 = _matmul(
        "ff2", act, w_ff2, "nn", SEQ, D_MODEL, D_FF, [F32, BF16], tm=TM, tk=1024, n_sums=3,
        epilogue=_loss_head_epi, extras=_row_operands(TM, [x2, target], [g2, norm_f_g]))

    def relu2_bwd_epi(acc, rt):
        return (acc * (2.0 * rt.astype(F32)),)

    gw_ff2 = _matmul_tn_pair("ff2_dw", pos, act, d_y2, D_FF, D_MODEL, SEQ, D_FF // N_CHIPS,
                             tm=512, tn=1024, tk=SEQ)
    d_u = _matmul("ff2_dx", d_y2, w_ff2, "nt", SEQ, D_FF, D_MODEL, [BF16], epilogue=relu2_bwd_epi,
                  extras=[(relu_u, (TM, TN), lambda i, j, kk: (i, j))])[0]
    gw_ff1 = _matmul_tn_pair("ff1_dw", pos, h2, d_u, D_MODEL, D_FF, SEQ, D_MODEL,
                             tm=512, tn=1024, tk=SEQ)
    ffn = ["w_ff2", "w_ff1"]
    ffn_started = _ici_start("ici_start_ffn", ffn, [gw_ff2, gw_ff1])
    d_x2, d_mix, d_sh2, d_sc2, d_n2g, d_g1 = _matmul(
        "ff1_dx", d_u, w_ff1, "nt", SEQ, D_MODEL, D_FF, [F32, BF16], tm=TM, tk=1024, n_sums=4,
        epilogue=_prenorm_bwd_epi, extras=_row_operands(TM, [x2, d_x3], [norm2_g, sc2])
        + _row_operands(TM, [mix], [g1]), after=[ffn_started[3]])
    gw_o = _matmul_tn_pair("mix_dw", pos, merged, d_mix, D_MODEL, D_MODEL, SEQ, D_MODEL // N_CHIPS,
                           tm=128, tn=1024, tk=2048)
    d_ret_out, d_att_out, d_ga, d_gb = _matmul(
        "mix_dx", d_mix, w_o, "nt", SEQ, D_MODEL, D_MODEL, [BF16] * 4, tm=MERGE_TM,
        epilogue=_merge_bwd_epi, extras=_merge_operands(gates, ret_out, att_out))

    gw_ret_out = _matmul_tn_pair("ret_out_dw", pos, gated, d_ret_out, RET_V_W, D_MODEL, SEQ,
                                 RET_V_W // N_CHIPS, tm=256, tn=1024, tk=SEQ)
    gw_att_out = _matmul_tn_pair("att_out_dw", pos, att_b, d_att_out, ATT_W, D_MODEL, SEQ, ATT_W,
                                 tm=256, tn=1024, tk=2048)
    mixer = ["w_o", "w_ret_out", "w_att_out"]
    mixer_started = _ici_start("ici_start_mixer", mixer, [gw_o, gw_ret_out, gw_att_out])
    d_gated = _matmul("ret_out_dx", d_ret_out, w_ret_out, "nt", SEQ, RET_V_W, D_MODEL, [BF16],
                      after=[mixer_started[3]])[0]
    d_att, dd = _matmul("att_out_dx", d_att_out, w_att_out, "nt", SEQ, ATT_W, D_MODEL, [F32, F32],
                        epilogue=_att_out_bwd_epi,
                        extras=[(att, (TM, ATT_W), lambda i, j, kk: (i, 0))], after=[mixer_started[3]])

    d_rqkv, d_rg, d_gn_g, d_gn_b = _retention_bwd(rqk, rv, states, d_gated, ro, rg, gn_g, gn_b,
                                                  din, qd, kd, cd, cos, sin)

    d_aqkv, dsbs = [], []
    for gi in range(3):
        dqkv, dsb = _att_bwd(gi, dils[gi], aqkv, d_att, lse, dd, bias)
        d_aqkv.append(dqkv)
        dsbs.append(dsb)
    d_rel_bias = _bias_grad(jnp.stack(dsbs), buckets)

    d_proj = ([(d_rqkv, False), (d_rg, False)] + [(t, True) for t in d_aqkv]
              + [(d_ga, False), (d_gb, False)])
    gw_in = _matmul_tn_pair("proj_dw", pos, h1, d_proj, D_MODEL, IN_COLS, SEQ, D_MODEL,
                            tm=512, tn=ATT_W, tk=SEQ)
    sems, (gw_in,), (land,), token = _ici_start("ici_start_w_in", ["w_in"], [gw_in])
    grad_x, d_sh1, d_sc1, d_n1g = _matmul(
        "proj_dx", d_proj, w_in, "nt", SEQ, D_MODEL, IN_COLS, [F32], tn=1024, tk=ATT_W, n_sums=3,
        epilogue=_prenorm_bwd_epi, extras=_row_operands(TM, [x, d_x2], [norm1_g, sc1]), after=[token])
    pending = (sems, land)

    names = ffn + mixer
    psums, got = _ici_wait("ici_wait_rest", names, list(ffn_started[0]) + list(mixer_started[0]),
                           list(ffn_started[1]) + list(mixer_started[1]),
                           list(ffn_started[2]) + list(mixer_started[2]), [grad_x])
    g_big = {n: _final_sum("final_" + n, pos, dict(BIG)[n], psums[i], got[i], SHARD[n])
             for i, n in enumerate(names)}
    d_mod = jnp.concatenate([d_sh1, d_sc1, d_g1, d_sh2, d_sc2, d_g2], axis=-1)
    small = dict(norm1_g=d_n1g, norm2_g=d_n2g, norm_f_g=d_gf, gn_g=d_gn_g, gn_b=d_gn_b,
                 rel_bias=d_rel_bias)
    return loss, grad_x, d_mod, small, g_big, (gw_in,) + pending


def _me():
    return lax.axis_index("x"), lax.axis_index("y"), lax.axis_index("c")


def _peer(x, y, c, mask):
    return (x ^ ((mask >> 2) & 1), y ^ ((mask >> 1) & 1), c ^ (mask & 1))


def _gather8(src_ref, dst_ref, send_sems, recv_sems):
    x, y, c = _me()
    me = 4 * x + 2 * y + c
    copies = []
    for mask in range(1, N_DEV):
        cp = pltpu.make_async_remote_copy(
            src_ref=src_ref, dst_ref=dst_ref.at[me], send_sem=send_sems.at[mask - 1],
            recv_sem=recv_sems.at[mask - 1], device_id=_peer(x, y, c, mask), device_id_type=MESH)
        cp.start()
        copies.append(cp)
    dst_ref[me] = src_ref[...]
    for cp in copies:
        cp.wait_recv()
    for cp in copies:
        cp.wait_send()


def _ada_fwd(c_in, w_ada, b_ada):
    ncol = ADA_COLS // N_CHIPS

    def body(c_ref, w_ref, b_ref, mod_ref, sc_ref, cbuf, cg, mbuf, mg, s1, r1, s2, r2):
        x, y, c = _me()
        me = 4 * x + 2 * y + c
        cv = c_ref[...]
        cbuf[...] = jnp.broadcast_to(cv * _sigmoid(cv), cbuf.shape)
        _gather8(cbuf, cg, s1, r1)
        rows = lax.broadcasted_iota(I32, (N_DEV, D_MODEL), 0)
        sc_all = jnp.zeros((N_DEV, D_MODEL), F32)
        for d in range(N_DEV):
            sc_all = jnp.where(rows == d, cg[d], sc_all)
        sc_ref[...] = sc_all
        mbuf[...] = jnp.dot(sc_all.astype(BF16), w_ref[...].astype(BF16), preferred_element_type=F32)
        _gather8(mbuf, mg, s2, r2)
        rowsel = lax.broadcasted_iota(I32, (N_DEV, ncol), 0) == me
        for k in range(N_CHIPS):
            blk = mg[2 * k]
            row = jnp.sum(jnp.where(rowsel, blk, 0.0), axis=0, keepdims=True)
            mod_ref[:, k * ncol:(k + 1) * ncol] = row + b_ref[:, k * ncol:(k + 1) * ncol]

    vm = pl.BlockSpec(memory_space=pltpu.VMEM)
    return pl.pallas_call(
        body, name="ada_fwd",
        in_specs=[vm, vm, vm], out_specs=[vm, vm],
        out_shape=[jax.ShapeDtypeStruct((1, ADA_COLS), F32), jax.ShapeDtypeStruct((N_DEV, D_MODEL), F32)],
        scratch_shapes=[
            pltpu.VMEM((8, D_MODEL), F32), pltpu.VMEM((N_DEV, 8, D_MODEL), F32),
            pltpu.VMEM((8, ncol), F32), pltpu.VMEM((N_DEV, 8, ncol), F32),
            pltpu.SemaphoreType.DMA((N_DEV - 1,)), pltpu.SemaphoreType.DMA((N_DEV - 1,)),
            pltpu.SemaphoreType.DMA((N_DEV - 1,)), pltpu.SemaphoreType.DMA((N_DEV - 1,)),
        ],
        compiler_params=pltpu.CompilerParams(vmem_limit_bytes=VMEM_LIMIT_V7X),
    )(c_in, w_ada, b_ada)


def _small_reduce(pack, sc_all, after=()):
    ncol = ADA_COLS // N_CHIPS

    def body(p_ref, sc_ref, *rest):
        tot_ref, gw_ref, pg, s1, r1 = rest[len(after):]
        x, y, _ = _me()
        chip = 2 * x + y
        _gather8(p_ref, pg, s1, r1)
        tot = pg[0]
        for d in range(1, N_DEV):
            tot = tot + pg[d]
        tot_ref[...] = tot
        rows = lax.broadcasted_iota(I32, (N_DEV, ncol), 0)
        dmod = jnp.zeros((N_DEV, ncol), F32)
        for k in range(N_CHIPS):
            part = jnp.zeros((N_DEV, ncol), F32)
            for d in range(N_DEV):
                part = jnp.where(rows == d, pg[d, :, k * ncol:(k + 1) * ncol][0:1, :], part)
            dmod = jnp.where(chip == k, part, dmod)
        gw_ref[...] = lax.dot_general(sc_ref[...].astype(BF16), dmod.astype(BF16), _TN,
                                      preferred_element_type=F32)

    vm = pl.BlockSpec(memory_space=pltpu.VMEM)
    return pl.pallas_call(
        body, name="small_reduce",
        in_specs=[vm, vm] + [pl.BlockSpec(memory_space=pl.ANY)] * len(after), out_specs=[vm, vm],
        out_shape=[jax.ShapeDtypeStruct((8, ADA_COLS), F32), jax.ShapeDtypeStruct((D_MODEL, ncol), F32)],
        scratch_shapes=[pltpu.VMEM((N_DEV, 8, ADA_COLS), F32),
                        pltpu.SemaphoreType.DMA((N_DEV - 1,)), pltpu.SemaphoreType.DMA((N_DEV - 1,))],
        compiler_params=pltpu.CompilerParams(vmem_limit_bytes=VMEM_LIMIT_V7X),
    )(pack, sc_all, *after)


BIG = (("w_in", 1), ("w_ret_out", 0), ("w_att_out", 1), ("w_o", 0), ("w_ff1", 1), ("w_ff2", 0))
SHARD = {"w_in": (D_MODEL, IN_COLS // N_CHIPS), "w_ret_out": (RET_V_W // N_CHIPS, D_MODEL),
         "w_att_out": (ATT_W, D_MODEL // N_CHIPS), "w_o": (D_MODEL // N_CHIPS, D_MODEL),
         "w_ff1": (D_MODEL, D_FF // N_CHIPS), "w_ff2": (D_FF // N_CHIPS, D_MODEL)}
_CHIP_FLIPS = ((1, 0), (0, 1), (1, 1))


def _region(ref, axis, chip, half, shard_shape):
    r, cw = shard_shape
    hr = r // 2
    if axis == 1:
        return ref.at[pl.ds(half * hr, hr), pl.ds(chip * cw, cw)]
    return ref.at[pl.ds(chip * r + half * hr, hr), :]


CAST_ROWS = 128


def _gather_weights(shards, n_remote):
    nw = len(BIG)
    shapes = [s.shape for s in shards]
    full_shapes = [(r, N_CHIPS * cw) if ax == 1 else (N_CHIPS * r, cw)
                   for (r, cw), (_, ax) in zip(shapes, BIG)]

    def body(*refs):
        ins, outs = refs[:nw], refs[nw:2 * nw]
        own = refs[2 * nw:3 * nw]
        from_ici, from_sib = refs[3 * nw:3 * nw + n_remote], refs[3 * nw + n_remote:3 * nw + 2 * n_remote]
        ld_sem, st_sem, s_ici, r_ici, s_d2d, r_d2d, st_a, st_b, stage = refs[3 * nw + 2 * n_remote:]
        x, y, c = _me()
        chip = 2 * x + y
        sib = (x, y, 1 - c)
        loads = [pltpu.make_async_copy(ins[i], stage if i == 0 else own[i], ld_sem.at[i])
                 for i in range(nw)]
        for cp in loads:
            cp.start()
        pending, first = [], []
        for i, (_, ax) in enumerate(BIG):
            r, cw = shapes[i]
            hr = r // 2
            loads[i].wait()
            if i == 0:
                for r0 in range(0, r, CAST_ROWS):
                    own[0][r0:r0 + CAST_ROWS, :] = stage[r0:r0 + CAST_ROWS, :].astype(BF16)
            dst = outs[i].at[:, pl.ds(chip * cw, cw)] if ax == 1 else outs[i].at[pl.ds(chip * r, r), :]
            cp = pltpu.make_async_copy(own[i], dst, st_sem.at[i])
            cp.start()
            pending.append(cp)
            for j, (fx, fy) in enumerate(_CHIP_FLIPS if i < n_remote else ()):
                rc = pltpu.make_async_remote_copy(
                    src_ref=own[i].at[pl.ds(c * hr, hr), :], dst_ref=from_ici[i].at[j],
                    send_sem=s_ici.at[j * nw + i], recv_sem=r_ici.at[j * nw + i],
                    device_id=(x ^ fx, y ^ fy, c), device_id_type=MESH)
                rc.start()
                first.append((j, i, rc))
        passed = []
        for j, i, rc in first:
            fx, fy = _CHIP_FLIPS[j]
            src_chip = 2 * (x ^ fx) + (y ^ fy)
            ax = BIG[i][1]
            rc.wait_recv()
            fw = pltpu.make_async_remote_copy(
                src_ref=from_ici[i].at[j], dst_ref=from_sib[i].at[j], send_sem=s_d2d.at[j * nw + i],
                recv_sem=r_d2d.at[j * nw + i], device_id=sib, device_id_type=MESH)
            fw.start()
            passed.append((j, i, src_chip, fw))
            st = pltpu.make_async_copy(from_ici[i].at[j], _region(outs[i], ax, src_chip, c, shapes[i]),
                                       st_a.at[j * nw + i])
            st.start()
            pending.append(st)
        for j, i, src_chip, fw in passed:
            fw.wait_recv()
            st = pltpu.make_async_copy(from_sib[i].at[j],
                                       _region(outs[i], BIG[i][1], src_chip, 1 - c, shapes[i]),
                                       st_b.at[j * nw + i])
            st.start()
            pending.append(st)
        for _, _, rc in first:
            rc.wait_send()
        for _, _, _, fw in passed:
            fw.wait_send()
        for cp in pending:
            cp.wait()

    hbm = pl.BlockSpec(memory_space=pl.ANY)
    halves = [pltpu.VMEM((3, r // 2, cw), BF16) for r, cw in shapes[:n_remote]]
    return pl.pallas_call(
        body, name="gather_weights",
        in_specs=[hbm] * nw, out_specs=[hbm] * nw,
        out_shape=[jax.ShapeDtypeStruct(fs, BF16) for fs in full_shapes],
        scratch_shapes=[pltpu.VMEM(sh, BF16) for sh in shapes] + halves + halves
        + [pltpu.SemaphoreType.DMA((nw,)), pltpu.SemaphoreType.DMA((nw,))]
        + [pltpu.SemaphoreType.DMA((3 * nw,))] * 6 + [pltpu.VMEM(shapes[0], F32)],
        compiler_params=pltpu.CompilerParams(vmem_limit_bytes=VMEM_LIMIT_V7X),
    )(*shards)


REST = BIG[1:]
_SIDE_EFFECTS = pltpu.CompilerParams(has_side_effects=pltpu.SideEffectType.DATAFLOW_SIDE_EFFECTING)
_ANY_SPEC = pl.BlockSpec(memory_space=pl.ANY)


def _rest_ici_copies(shard_refs, full_refs, sems):
    x, y, c = _me()
    chip = 2 * x + y
    n = 3 * len(REST)
    copies = []
    for i, (name, ax) in enumerate(REST):
        hr = SHARD[name][0] // 2
        for j, (fx, fy) in enumerate(_CHIP_FLIPS):
            copies.append(pltpu.make_async_remote_copy(
                src_ref=shard_refs[i].at[pl.ds(c * hr, hr), :],
                dst_ref=_region(full_refs[i], ax, chip, c, SHARD[name]),
                send_sem=sems[3 * i + j], recv_sem=sems[n + 3 * i + j],
                device_id=(x ^ fx, y ^ fy, c), device_id_type=MESH))
    return copies


def _rest_d2d_copies(full_refs, sems):
    x, y, c = _me()
    n = 3 * len(REST)
    copies = []
    for i, (name, ax) in enumerate(REST):
        for j, (fx, fy) in enumerate(_CHIP_FLIPS):
            reg = _region(full_refs[i], ax, 2 * (x ^ fx) + (y ^ fy), c, SHARD[name])
            copies.append(pltpu.make_async_remote_copy(
                src_ref=reg, dst_ref=reg, send_sem=sems[3 * i + j], recv_sem=sems[n + 3 * i + j],
                device_id=(x, y, 1 - c), device_id_type=MESH))
    return copies


def _gather_rest_start(shards, fulls, after):
    nr, ns, na = len(REST), 6 * len(REST), len(after)

    def body(*refs):
        for cp in _rest_ici_copies(refs[:nr], refs[nr:2 * nr], refs[2 * nr + na:2 * nr + na + ns]):
            cp.start()
        token = refs[-1]
        token[...] = jnp.zeros_like(token)

    hbm = lambda a: pltpu.HBM(a.shape, a.dtype)
    res = pl.pallas_call(
        body, name="gather_rest_start",
        out_shape=(pltpu.SemaphoreType.DMA(()),) * ns + tuple(hbm(a) for a in shards + fulls)
        + (jax.ShapeDtypeStruct((8, 128), F32),),
        in_specs=(_HBM_SPEC,) * (2 * nr) + (_ANY_SPEC,) * na,
        out_specs=(_SEM_SPEC,) * ns + (_HBM_SPEC,) * (2 * nr) + (pl.BlockSpec(memory_space=pltpu.VMEM),),
        input_output_aliases={k: ns + k for k in range(2 * nr)}, compiler_params=_SIDE_EFFECTS,
    )(*[pltpu.with_memory_space_constraint(a, pltpu.HBM) for a in shards + fulls], *after)
    return res[:ns], res[ns:ns + nr], res[ns + nr:ns + 2 * nr], res[-1]


def _gather_rest_forward(sems, shards, fulls, after):
    nr, ns = len(REST), 6 * len(REST)

    def body(*refs):
        shard_refs, full_refs, old = refs[:nr], refs[nr:2 * nr], refs[2 * nr:2 * nr + ns]
        new = refs[2 * nr + ns + len(after):2 * nr + 2 * ns + len(after)]
        for cp in _rest_ici_copies(shard_refs, full_refs, old):
            cp.wait_send()
            cp.wait_recv()
        for cp in _rest_d2d_copies(full_refs, new):
            cp.start()
        token = refs[-1]
        token[...] = jnp.zeros_like(token)

    res = pl.pallas_call(
        body, name="gather_rest_forward",
        out_shape=(pltpu.SemaphoreType.DMA(()),) * ns + tuple(pltpu.HBM(a.shape, a.dtype) for a in fulls)
        + (jax.ShapeDtypeStruct((8, 128), F32),),
        in_specs=(_HBM_SPEC,) * (2 * nr) + (_SEM_SPEC,) * ns + (_ANY_SPEC,) * len(after),
        out_specs=(_SEM_SPEC,) * ns + (_HBM_SPEC,) * nr + (pl.BlockSpec(memory_space=pltpu.VMEM),),
        input_output_aliases={nr + k: ns + k for k in range(nr)}, compiler_params=_SIDE_EFFECTS,
    )(*shards, *fulls, *sems, *after)
    return res[:ns], res[ns:ns + nr], res[-1]


def _gather_rest_end(sems, fulls, after):
    nr, ns = len(REST), 6 * len(REST)

    def body(*refs):
        for cp in _rest_d2d_copies(refs[:nr], refs[nr:nr + ns]):
            cp.wait_send()
            cp.wait_recv()

    return pl.pallas_call(
        body, name="gather_rest_end",
        out_shape=tuple(pltpu.HBM(a.shape, a.dtype) for a in fulls),
        in_specs=(_HBM_SPEC,) * nr + (_SEM_SPEC,) * ns + (_ANY_SPEC,) * len(after),
        out_specs=(_HBM_SPEC,) * nr,
        input_output_aliases={k: k for k in range(nr)}, compiler_params=_SIDE_EFFECTS,
    )(*fulls, *sems, *after)


def _adam_update(w, g, m, v):
    mn = ADAM_B1 * m + (1.0 - ADAM_B1) * g
    vn = ADAM_B2 * v + (1.0 - ADAM_B2) * (g * g)
    m_hat = mn / (1.0 - ADAM_B1 ** ADAM_STEP)
    v_hat = vn / (1.0 - ADAM_B2 ** ADAM_STEP)
    return -ADAM_LR * (m_hat / (jnp.sqrt(v_hat) + ADAM_EPS) + ADAM_WD * w), mn, vn


def _final_sum(name, pos, axis, psum, recv, shard_shape, after=(), tr=128):
    r, cw = shard_shape
    hr = r // 2
    tr = min(tr, hr)
    nt = hr // tr
    n_after = len(after)

    def kern(pos_ref, p_ref, r_ref, *rest):
        g_ref, send_buf, land_buf, s_sem, r_sem = rest[n_after:]
        p, t = pl.program_id(0), pl.program_id(1)
        sib = _sibling()

        @pl.when(jnp.logical_and(p == 0, t == 0))
        def _():
            _pair_barrier(sib)

        def copy(i):
            return pltpu.make_async_remote_copy(
                src_ref=send_buf.at[i], dst_ref=land_buf.at[i], send_sem=s_sem.at[i],
                recv_sem=r_sem.at[i], device_id=sib, device_id_type=MESH)

        @pl.when(p == 0)
        def _():
            tot = p_ref[...].astype(F32)
            for j in range(3):
                tot = tot + r_ref[j].astype(F32)
            send_buf[t] = tot
            copy(t).start()
            g_ref[...] = tot

        @pl.when(p == 1)
        def _():
            copy(t).wait_recv()
            g_ref[...] = land_buf[t]

        @pl.when(jnp.logical_and(p == 1, t == nt - 1))
        def _():
            for i in range(nt):
                copy(i).wait_send()

    def shard_rows(p, t, pos_ref):
        return (jnp.where(p == 0, pos_ref[0], 1 - pos_ref[0]) * nt + t, 0)

    def own_part(p, t, pos_ref):
        tt = jnp.where(p == 0, t, nt - 1)
        return (tt, pos_ref[1]) if axis == 1 else (pos_ref[1] * nt + tt, 0)

    grid_spec = pltpu.PrefetchScalarGridSpec(
        num_scalar_prefetch=1, grid=(2, nt),
        in_specs=[pl.BlockSpec((tr, cw), own_part),
                  pl.BlockSpec((3, tr, cw), lambda p, t, pos_ref: (0, jnp.where(p == 0, t, nt - 1), 0))]
        + [pl.BlockSpec(memory_space=pl.ANY)] * n_after,
        out_specs=pl.BlockSpec((tr, cw), shard_rows),
        scratch_shapes=[pltpu.VMEM((nt, tr, cw), F32), pltpu.VMEM((nt, tr, cw), F32),
                        pltpu.SemaphoreType.DMA((nt,)), pltpu.SemaphoreType.DMA((nt,))])
    return pl.pallas_call(
        kern, name=name, grid_spec=grid_spec, out_shape=jax.ShapeDtypeStruct((r, cw), F32),
        compiler_params=_cparams(("arbitrary", "arbitrary"), PAIR_COLLECTIVE_ID),
    )(pos, psum, recv, *after)


def _adamw(name, w, g, m, v):
    r, cw = w.shape
    tr = min(r, 128)

    def kern(w_ref, g_ref, m_ref, v_ref, go_ref, d_ref, nm_ref, nv_ref):
        gv = g_ref[...]
        go_ref[...] = gv
        d_ref[...], nm_ref[...], nv_ref[...] = _adam_update(w_ref[...], gv, m_ref[...], v_ref[...])

    spec = pl.BlockSpec((tr, cw), lambda i: (i, 0))
    return pl.pallas_call(
        kern, name=name, grid=(r // tr,), in_specs=[spec] * 4, out_specs=[spec] * 4,
        out_shape=[jax.ShapeDtypeStruct((r, cw), F32)] * 4, compiler_params=_cparams(("parallel",)),
    )(w, g, m, v)


_PACK_W = ADA_COLS
_NB = REL_BUCKETS * N_ATT_HEADS
_SMALL_SLOTS = {
    "b_ada": (0, 0, ADA_COLS),
    "norm1_g": (1, 0, D_MODEL), "norm2_g": (1, D_MODEL, D_MODEL), "norm_f_g": (1, 2 * D_MODEL, D_MODEL),
    "ret_gn_g": (1, 3 * D_MODEL, RET_V_W),
    "ret_gn_b": (2, 0, RET_V_W), "rel_bias": (2, RET_V_W, _NB), "loss": (2, RET_V_W + 512, 128),
}


def _pack_small(vals):
    rows = []
    for r in range(8):
        items = sorted([(off, n) for n, (rr, off, _) in _SMALL_SLOTS.items() if rr == r and n in vals])
        parts, pos = [], 0
        for off, n in items:
            if off > pos:
                parts.append(jnp.zeros((1, off - pos), F32))
            parts.append(vals[n].reshape(1, -1).astype(F32))
            pos = off + _SMALL_SLOTS[n][2]
        if pos < _PACK_W:
            parts.append(jnp.zeros((1, _PACK_W - pos), F32))
        rows.append(jnp.concatenate(parts, axis=-1))
    return jnp.concatenate(rows, axis=0)


def _adamw_small(tot, names, wmv):
    n = len(names)

    def kern(tot_ref, *refs):
        ins, outs = refs[:3 * n], refs[3 * n:]
        for i, name in enumerate(names):
            row, off, width = _SMALL_SLOTS[name]
            g = tot_ref[row:row + 1, off:off + width]
            outs[i][...] = g
            outs[n + i][...], outs[2 * n + i][...], outs[3 * n + i][...] = _adam_update(
                ins[i][...], g, ins[n + i][...], ins[2 * n + i][...])

    vm = pl.BlockSpec(memory_space=pltpu.VMEM)
    shapes = [jax.ShapeDtypeStruct((1, _SMALL_SLOTS[name][2]), F32) for name in names]
    res = pl.pallas_call(
        kern, name="adamw_small", in_specs=[vm] * (1 + 3 * n), out_specs=[vm] * (4 * n),
        out_shape=shapes * 4,
    )(tot, *wmv[0], *wmv[1], *wmv[2])
    return res[:n], res[n:2 * n], res[2 * n:3 * n], res[3 * n:]


def _unpack_small(pack, name):
    r, off, wd = _SMALL_SLOTS[name]
    return pack[r:r + 1, off:off + wd]


def kernel(x, c, w_ada, b_ada, norm1_g, w_in, rel_bias, ret_gn_g, ret_gn_b, w_ret_out, w_att_out, w_o, norm2_g, w_ff1, w_ff2, norm_f_g, loss_target, m_w_ada, m_b_ada, m_norm1_g, m_w_in, m_rel_bias, m_ret_gn_g, m_ret_gn_b, m_w_ret_out, m_w_att_out, m_w_o, m_norm2_g, m_w_ff1, m_w_ff2, m_norm_f_g, v_w_ada, v_b_ada, v_norm1_g, v_w_in, v_rel_bias, v_ret_gn_g, v_ret_gn_b, v_w_ret_out, v_w_att_out, v_w_o, v_norm2_g, v_w_ff1, v_w_ff2, v_norm_f_g):
    given = dict(locals())
    big_names = [n for n, _ in BIG]
    shard_w = {n: given[n][0] for n in big_names}
    assert all(shard_w[n].shape == SHARD[n] for n in big_names)

    shards_bf = [None] + [shard_w[n].astype(BF16) for n in big_names[1:]]
    full = _gather_weights([shard_w["w_in"]] + shards_bf[1:], 1)
    mod, sc_all = _ada_fwd(c, w_ada[0], b_ada)
    rest_gather = _gather_rest_start(shards_bf[1:], list(full[1:]), [mod])
    pos = _where_am_i()

    loss, grad_x, d_mod, small, g_big, pending = _local_step(
        pos, x[0], loss_target[0], mod, norm1_g, norm2_g, norm_f_g.reshape(1, -1), rel_bias, ret_gn_g,
        ret_gn_b, full[0], rest_gather)

    pack_g = _pack_small(dict(b_ada=d_mod, norm1_g=small["norm1_g"], norm2_g=small["norm2_g"],
                              norm_f_g=small["norm_f_g"], ret_gn_g=small["gn_g"], ret_gn_b=small["gn_b"],
                              rel_bias=small["rel_bias"], loss=loss[:, :128]))
    tot, g_w_ada = _small_reduce(pack_g, sc_all, after=list(g_big.values()))

    small_names = ["b_ada", "norm1_g", "rel_bias", "ret_gn_g", "ret_gn_b", "norm2_g", "norm_f_g"]
    small_out = _adamw_small(tot, small_names, [[given[p + n].reshape(1, -1) for n in small_names]
                                                for p in ("", "m_", "v_")])
    grads, deltas, new_m, new_v = ({n: t.reshape(given[n].shape) for n, t in zip(small_names, group)}
                                   for group in small_out)
    sd = deltas["b_ada"]
    g_big["w_ada"] = g_w_ada
    for n in ["w_ada"] + big_names[1:] + big_names[:1]:
        if n == "w_in":
            gw_in, sems, land = pending
            done = [tot, sd] + [deltas[k] for k in ["w_ada"] + big_names[1:]]
            (gw_in,), (got,) = _ici_wait("ici_wait_w_in", [n], sems, [gw_in], [land], done)
            g_big[n] = _final_sum("final_w_in", pos, 1, gw_in, got, SHARD[n])
        g, d, nm, nv = _adamw("adamw_" + n, given[n][0], g_big[n], given["m_" + n][0], given["v_" + n][0])
        grads[n], deltas[n], new_m[n], new_v[n] = g[None], d[None], nm[None], nv[None]

    order = ["w_ada", "b_ada", "norm1_g", "w_in", "rel_bias", "ret_gn_g", "ret_gn_b", "w_ret_out",
             "w_att_out", "w_o", "norm2_g", "w_ff1", "w_ff2", "norm_f_g"]
    loss_out = _unpack_small(tot, "loss")[0, 0]
    return (loss_out, grad_x[None], *[grads[n] for n in order], *[deltas[n] for n in order],
            *[new_m[n] for n in order], *[new_v[n] for n in order])
```

```python
import math

import jax
import jax.numpy as jnp
import numpy as np
from jax import lax
from jax.experimental import pallas as pl
from jax.experimental.pallas import tpu as pltpu

F32 = jnp.float32
BF16 = jnp.bfloat16
I32 = jnp.int32

SEQ = 2048
D_MODEL = 1024
RET_HEADS = 4
RET_DK = 256
RET_DV = 512
RET_CHUNK = 128
RET_SUB = 2
RET_QK_W = RET_HEADS * RET_DK
RET_V_W = RET_HEADS * RET_DV
ATT_GROUPS = ((128, 1), (512, 4), (2048, 16))
ATT_HPG = 4
ATT_DH = 128
ATT_W = ATT_HPG * ATT_DH
ATT_BLK = 128
REL_BUCKETS = 32
REL_MAX_DIST = 2048
N_ATT_HEADS = 12
D_FF = 4 * D_MODEL
RMS_EPS = 1e-6
GN_EPS = 1e-5
ROPE_BASE = 10000.0
IN_COLS = 2 * RET_QK_W + 2 * RET_V_W + 9 * ATT_W + 2 * D_MODEL
OFF_Q, OFF_K, OFF_V, OFF_G = 0, RET_QK_W, 2 * RET_QK_W, 2 * RET_QK_W + RET_V_W
OFF_ATT = 2 * RET_QK_W + 2 * RET_V_W
OFF_GATE = OFF_ATT + 9 * ATT_W
N_CHIPS = 4
N_DEV = 8
ADA_COLS = 6 * D_MODEL

ADAM_LR = 0.001
ADAM_B1 = 0.9
ADAM_B2 = 0.999
ADAM_EPS = 1e-08
ADAM_WD = 0.01
ADAM_STEP = 10

VMEM_LIMIT_V7X = 56 * 1024 * 1024
MESH = pl.DeviceIdType.MESH


def _cparams(sem, collective_id=None):
    return pltpu.CompilerParams(dimension_semantics=sem, vmem_limit_bytes=VMEM_LIMIT_V7X,
                                collective_id=collective_id)


PAIR_COLLECTIVE_ID = 0


def _pair_barrier(sib):
    barrier = pltpu.get_barrier_semaphore()
    pl.semaphore_signal(barrier, inc=1, device_id=sib, device_id_type=MESH)
    pl.semaphore_wait(barrier, 1)


def _sigmoid(v):
    return 1.0 / (1.0 + jnp.exp(-v))


TM, TN = 1024, 1024


def _piece_chunks(piece, width):
    arr, stacked = piece
    return arr.shape[0] if stacked else arr.shape[1] // width


def _piece_spec(piece, rows, width, start, row_of, chunk_of):
    arr, stacked = piece
    last = _piece_chunks(piece, width) - 1

    def local(*ids):
        return jnp.clip(chunk_of(*ids) - start, 0, last)

    def row(*ids):
        rel = chunk_of(*ids) - start
        return jnp.where(jnp.logical_and(rel >= 0, rel <= last), row_of(*ids), 0)

    if stacked:
        return pl.BlockSpec((None, rows, width), lambda *ids: (local(*ids), row(*ids), 0))
    return pl.BlockSpec((rows, width), lambda *ids: (row(*ids), local(*ids)))


def _piece_starts(pieces, width):
    return [sum(_piece_chunks(p, width) for p in pieces[:q]) for q in range(len(pieces))]


def _matmul(name, a, b, kind, m, n, k, outs, *, b_off=0, tm=TM, tn=TN, tk=1024,
            epilogue=None, extras=(), after=(), n_sums=0, b_buffers=None):
    tm, tn, tk = min(tm, m), min(tn, n), min(tk, k)
    nk = k // tk
    pieces = a if isinstance(a, list) else [(a, False)]
    starts = _piece_starts(pieces, tk)
    if kind == "nn":
        a_specs = [pl.BlockSpec((tm, tk), lambda i, j, kk: (i, kk))]
        b_spec = pl.BlockSpec((tk, tn), lambda i, j, kk: (kk, b_off // tn + j))
        dn = (((1,), (0,)), ((), ()))
    elif kind == "nt":
        a_specs = [_piece_spec(p, tm, tk, st, lambda i, j, kk: i, lambda i, j, kk: kk)
                   for p, st in zip(pieces, starts)]
        b_spec = pl.BlockSpec((tn, tk), lambda i, j, kk: (j, b_off // tk + kk))
        dn = (((1,), (1,)), ((), ()))
    else:
        a_specs = [pl.BlockSpec((tk, tm), lambda i, j, kk: (kk, i))]
        b_spec = pl.BlockSpec((tk, tn), lambda i, j, kk: (kk, j))
        dn = (((0,), (0,)), ((), ()))
    nb, nj = b_buffers, n // tn
    n_steps = (m // tm) * nj * nk
    if nb is not None:
        assert kind in ("nn", "nt") and len(pieces) == 1
        b_block = b_spec.block_shape
        b_spec = pl.BlockSpec(memory_space=pl.ANY)

    def b_copy(b_hbm, buf, sem, t):
        kk_t, j_t = t % nk, (t // nk) % nj
        if kind == "nn":
            src = b_hbm.at[pl.ds(pl.multiple_of(kk_t * tk, tk), tk),
                           pl.ds(pl.multiple_of(b_off + j_t * tn, tn), tn)]
        else:
            src = b_hbm.at[pl.ds(pl.multiple_of(j_t * tn, tn), tn),
                           pl.ds(pl.multiple_of(b_off + kk_t * tk, tk), tk)]
        return pltpu.make_async_copy(src, buf.at[t % nb], sem.at[t % nb])

    n_a, n_ex, n_out = len(pieces), len(extras), len(outs)
    if epilogue is None:
        epilogue = lambda acc: (acc,)

    assert n_sums == 0 or tn == n

    def finish(acc, ex_refs, out_refs, first_rows):
        res = epilogue(acc, *[r[...] for r in ex_refs])
        for r, v in zip(out_refs[:n_out], res[:n_out]):
            r[...] = v.astype(r.dtype)
        for r, v in zip(out_refs[n_out:], res[n_out:]):
            @pl.when(first_rows)
            def _(r=r, v=v):
                r[...] = v

            @pl.when(jnp.logical_not(first_rows))
            def _(r=r, v=v):
                r[...] += v

    n_in = n_a + 1 + n_ex + len(after)

    def kern(*refs):
        a_refs, b_ref = refs[:n_a], refs[n_a]
        ex_refs = refs[n_a + 1:n_a + 1 + n_ex]
        out_refs = refs[n_in:n_in + n_out + n_sums]
        first_rows, kk = pl.program_id(0) == 0, pl.program_id(2)
        if nb is not None:
            b_buf, b_sem = refs[-2:]
            t = (pl.program_id(0) * nj + pl.program_id(1)) * nk + kk

            @pl.when(t == 0)
            def _():
                for s in range(min(nb - 1, n_steps)):
                    b_copy(b_ref, b_buf, b_sem, s).start()

            @pl.when(t + nb - 1 < n_steps)
            def _():
                b_copy(b_ref, b_buf, b_sem, t + nb - 1).start()

            b_copy(b_ref, b_buf, b_sem, t).wait()
            b_tile = b_buf.at[t % nb]
        else:
            b_tile = b_ref
        dot = lambda a_ref: lax.dot_general(a_ref[...], b_tile[...], dn, preferred_element_type=F32)
        if nk == 1:
            finish(dot(a_refs[0]), ex_refs, out_refs, first_rows)
            return
        acc_ref = refs[n_in + n_out + n_sums]
        if n_a == 1:
            part = dot(a_refs[0])

            @pl.when(kk == 0)
            def _():
                acc_ref[...] = part

            @pl.when(kk > 0)
            def _():
                acc_ref[...] += part
        else:
            @pl.when(kk == 0)
            def _():
                acc_ref[...] = jnp.zeros_like(acc_ref)

            for q in range(n_a):
                @pl.when(jnp.logical_and(kk >= starts[q], kk < starts[q] + _piece_chunks(pieces[q], tk)))
                def _(q=q):
                    acc_ref[...] += dot(a_refs[q])

        @pl.when(kk == nk - 1)
        def _():
            finish(acc_ref[...], ex_refs, out_refs, first_rows)

    in_specs = a_specs + [b_spec] + [pl.BlockSpec(bs, im) for _, bs, im in extras]
    in_specs += [pl.BlockSpec(memory_space=pl.ANY)] * len(after)
    sem = ("arbitrary",) * 3 if n_sums or nb is not None else ("parallel", "parallel", "arbitrary")
    ring = [] if nb is None else [pltpu.VMEM((nb,) + tuple(b_block), b.dtype), pltpu.SemaphoreType.DMA((nb,))]
    return pl.pallas_call(
        kern, name=name, grid=(m // tm, n // tn, nk), in_specs=in_specs,
        out_specs=[pl.BlockSpec((tm, tn), lambda i, j, kk: (i, j)) for _ in outs]
        + [pl.BlockSpec((1, tn), lambda i, j, kk: (0, 0))] * n_sums,
        out_shape=[jax.ShapeDtypeStruct((m, n), dt) for dt in outs]
        + [jax.ShapeDtypeStruct((1, n), F32)] * n_sums,
        scratch_shapes=([] if nk == 1 else [pltpu.VMEM((tm, tn), F32)]) + ring,
        compiler_params=_cparams(sem),
    )(*[p[0] for p in pieces], b, *[e[0] for e in extras], *after)


def _ici_copies(psum_ref, recv_ref, s_sem, r_sem, axis, shard_shape):
    x, y, c = _me()
    hr, cw = shard_shape[0] // 2, shard_shape[1]
    pick = lambda sems, j: sems[j] if isinstance(sems, (list, tuple)) else sems.at[j]
    copies = []
    for j, (fx, fy) in enumerate(_CHIP_FLIPS):
        chip = 2 * (x ^ fx) + (y ^ fy)
        src = psum_ref.at[:, pl.ds(chip * cw, cw)] if axis == 1 else psum_ref.at[pl.ds(chip * hr, hr), :]
        copies.append(pltpu.make_async_remote_copy(
            src_ref=src, dst_ref=recv_ref.at[j], send_sem=pick(s_sem, j), recv_sem=pick(r_sem, j),
            device_id=(x ^ fx, y ^ fy, c), device_id_type=MESH))
    return copies


_HBM_SPEC = pl.BlockSpec(memory_space=pltpu.HBM)
_SEM_SPEC = pl.BlockSpec(memory_space=pltpu.SEMAPHORE)


def _split_ici_copies(names, p_refs, land_refs, sems):
    copies = []
    for i, n in enumerate(names):
        copies += _ici_copies(p_refs[i], land_refs[i], list(sems[6 * i:6 * i + 3]),
                              list(sems[6 * i + 3:6 * i + 6]), dict(BIG)[n], SHARD[n])
    return copies


def _ici_start(name, names, psums):
    nw, ns = len(names), 6 * len(names)
    lands = [lax.empty((3, SHARD[n][0] // 2, SHARD[n][1]), BF16) for n in names]

    def body(*refs):
        for cp in _split_ici_copies(names, refs[:nw], refs[nw:2 * nw], refs[2 * nw:2 * nw + ns]):
            cp.start()
        token = refs[-1]
        token[...] = jnp.zeros_like(token)

    res = pl.pallas_call(
        body, name=name,
        out_shape=(pltpu.SemaphoreType.DMA(()),) * ns
        + tuple(pltpu.HBM(a.shape, BF16) for a in list(psums) + lands)
        + (jax.ShapeDtypeStruct((8, 128), F32),),
        in_specs=(_HBM_SPEC,) * (2 * nw),
        out_specs=(_SEM_SPEC,) * ns + (_HBM_SPEC,) * (2 * nw) + (pl.BlockSpec(memory_space=pltpu.VMEM),),
        input_output_aliases={k: ns + k for k in range(2 * nw)},
        compiler_params=pltpu.CompilerParams(has_side_effects=pltpu.SideEffectType.DATAFLOW_SIDE_EFFECTING),
    )(*[pltpu.with_memory_space_constraint(a, pltpu.HBM) for a in list(psums) + lands])
    return res[:ns], res[ns:ns + nw], res[ns + nw:ns + 2 * nw], res[-1]


def _ici_wait(name, names, sems, p_thru, land_thru, after):
    nw, ns = len(names), 6 * len(names)

    def body(*refs):
        for cp in _split_ici_copies(names, refs[:nw], refs[nw:2 * nw], refs[2 * nw:2 * nw + ns]):
            cp.wait_send()
            cp.wait_recv()

    res = pl.pallas_call(
        body, name=name,
        out_shape=tuple(pltpu.HBM(a.shape, BF16) for a in list(p_thru) + list(land_thru)),
        in_specs=(_HBM_SPEC,) * (2 * nw) + (_SEM_SPEC,) * ns + (pl.BlockSpec(memory_space=pl.ANY),) * len(after),
        out_specs=(_HBM_SPEC,) * (2 * nw), input_output_aliases={k: k for k in range(2 * nw)},
        compiler_params=pltpu.CompilerParams(has_side_effects=pltpu.SideEffectType.DATAFLOW_SIDE_EFFECTING),
    )(*p_thru, *land_thru, *sems, *after)
    return res[:nw], res[nw:]


def _where_am_i():
    x, y, c = _me()
    return jnp.stack([c, 2 * x + y]).astype(I32)


def _sibling():
    x, y, c = _me()
    return (x, y, 1 - c)


N_SEND_SLOTS = 2


def _matmul_tn_pair(name, pos, a, b, m, n, k, shard_rows, *, tm, tn, tk):
    hr = shard_rows // 2
    tm, tn, tk = min(tm, hr), min(tn, n), min(tk, k)
    tph = hr // tm
    nt, nj, nk = (m // 2) // tm, n // tn, k // tk
    n_tiles = nt * nj

    def row_block(p, t, pos_ref):
        half = jnp.where(p == 0, 1 - pos_ref[0], pos_ref[0])
        return (t // tph) * (2 * tph) + half * tph + t % tph

    pieces = b if isinstance(b, list) else [(b, False)]
    starts = _piece_starts(pieces, tn)
    n_b = len(pieces)

    def kern(pos_ref, a_ref, *rest):
        b_refs = rest[:n_b]
        o_ref, acc_ref, send_buf, land_buf, s_sem, r_sem = rest[n_b:]
        p, t, j, kk = pl.program_id(0), pl.program_id(1), pl.program_id(2), pl.program_id(3)
        idx = t * nj + j
        sib = _sibling()

        @pl.when(jnp.logical_and(jnp.logical_and(p == 0, idx == 0), kk == 0))
        def _():
            _pair_barrier(sib)

        def copy(i):
            return pltpu.make_async_remote_copy(
                src_ref=send_buf.at[i % N_SEND_SLOTS], dst_ref=land_buf.at[i], send_sem=s_sem.at[i],
                recv_sem=r_sem.at[i], device_id=sib, device_id_type=MESH)

        @pl.when(kk == 0)
        def _():
            acc_ref[...] = jnp.zeros_like(acc_ref)

        for q in range(n_b):
            @pl.when(jnp.logical_and(j >= starts[q], j < starts[q] + _piece_chunks(pieces[q], tn)))
            def _(q=q):
                acc_ref[...] += lax.dot_general(a_ref[...], b_refs[q][...], _TN, preferred_element_type=F32)

        @pl.when(jnp.logical_and(kk == nk - 1, p == 0))
        def _():
            @pl.when(idx >= N_SEND_SLOTS)
            def _():
                copy(idx - N_SEND_SLOTS).wait_send()

            send_buf[idx % N_SEND_SLOTS] = acc_ref[...].astype(BF16)
            copy(idx).start()

        @pl.when(jnp.logical_and(kk == nk - 1, p == 1))
        def _():
            copy(idx).wait_recv()
            o_ref[...] = (acc_ref[...] + land_buf[idx].astype(F32)).astype(BF16)

        @pl.when(jnp.logical_and(jnp.logical_and(p == 1, idx == n_tiles - 1), kk == nk - 1))
        def _():
            for i in range(max(n_tiles - N_SEND_SLOTS, 0), n_tiles):
                copy(i).wait_send()

    grid_spec = pltpu.PrefetchScalarGridSpec(
        num_scalar_prefetch=1, grid=(2, nt, nj, nk),
        in_specs=[pl.BlockSpec((tk, tm), lambda p, t, j, kk, pos_ref: (kk, row_block(p, t, pos_ref)))]
        + [_piece_spec(pc, tk, tn, st, lambda p, t, j, kk, pos_ref: kk, lambda p, t, j, kk, pos_ref: j)
           for pc, st in zip(pieces, starts)],
        out_specs=pl.BlockSpec((tm, tn), lambda p, t, j, kk, pos_ref: (p * t, p * j)),
        scratch_shapes=[pltpu.VMEM((tm, tn), F32), pltpu.VMEM((N_SEND_SLOTS, tm, tn), BF16),
                        pltpu.VMEM((n_tiles, tm, tn), BF16),
                        pltpu.SemaphoreType.DMA((n_tiles,)), pltpu.SemaphoreType.DMA((n_tiles,))])
    return pl.pallas_call(
        kern, name=name, grid_spec=grid_spec, out_shape=jax.ShapeDtypeStruct((m // 2, n), BF16),
        compiler_params=_cparams(("arbitrary",) * 4, PAIR_COLLECTIVE_ID),
    )(pos, a, *[pc[0] for pc in pieces])


def _rope_tables():
    half = RET_DK // 2
    f32 = np.float32
    inv = np.power(f32(ROPE_BASE), -np.arange(half, dtype=f32) / f32(half)).astype(f32)
    ang = (np.arange(SEQ, dtype=f32)[:, None] * inv[None, :]).astype(f32)
    return jnp.asarray(np.cos(ang).astype(f32)), jnp.asarray(np.sin(ang).astype(f32))


def _decay_tables():
    c = RET_CHUNK
    f32 = np.float32
    log_g = np.log1p(-np.power(f32(2.0), f32(-5.0) - np.arange(RET_HEADS, dtype=f32))).astype(f32)
    idx = np.arange(c, dtype=f32)
    rel = idx[:, None] - idx[None, :]
    din = np.where(rel >= 0, np.exp(log_g[:, None, None] * np.maximum(rel, f32(0.0))), f32(0.0)).astype(f32)
    qd = np.exp(log_g[:, None] * (idx + f32(1.0))).astype(f32)[:, :, None]
    kd = np.exp(log_g[:, None] * (f32(c) - f32(1.0) - idx)).astype(f32)[:, :, None]
    cd = np.exp(log_g * f32(c)).astype(f32)
    return jnp.asarray(din), jnp.asarray(qd), jnp.asarray(kd), jnp.asarray(cd)


def _t5_bucket(dist):
    max_exact = REL_BUCKETS // 2
    d_f = jnp.maximum(dist, 1).astype(F32)
    large = max_exact + (jnp.log(d_f / max_exact) / math.log(REL_MAX_DIST / max_exact)
                         * (REL_BUCKETS - max_exact)).astype(I32)
    large = jnp.minimum(large, REL_BUCKETS - 1)
    return jnp.where(dist < max_exact, dist, large)


def _bucket_tables():
    qi = jnp.arange(ATT_BLK)[:, None]
    kj = jnp.arange(2 * ATT_BLK)[None, :]
    dist = jnp.clip(ATT_BLK + qi - kj, 0, ATT_BLK)
    return jnp.stack([_t5_bucket(dist * dil) for _, dil in ATT_GROUPS]).astype(I32)


def _retention_fwd(rqk, rv, rg, gn_g, gn_b, din, qd, kd, cd):
    nc = SEQ // RET_CHUNK
    c, dk, dv = RET_CHUNK, RET_DK, RET_DV

    def kern(q_ref, k_ref, v_ref, rg_ref, g_ref, b_ref, din_ref, qd_ref, kd_ref, cd_ref,
             o_ref, st_ref, gated_ref, state):
        n = pl.program_id(0)

        @pl.when(n == 0)
        def _():
            state[...] = jnp.zeros_like(state)

        for sub in range(RET_SUB):
            rows = slice(sub * c, (sub + 1) * c)
            for h in range(RET_HEADS):
                q, k = q_ref[rows, h * dk:(h + 1) * dk], k_ref[rows, h * dk:(h + 1) * dk]
                v = v_ref[rows, h * dv:(h + 1) * dv]
                s_b = state[h].astype(BF16)
                st_ref[h, sub] = s_b
                a = lax.dot_general(q, k, _NT, preferred_element_type=F32) * din_ref[h]
                o = jnp.dot(a.astype(BF16), v, preferred_element_type=F32)
                o += jnp.dot(q, s_b, preferred_element_type=F32) * qd_ref[h]
                v_cols = slice(h * dv, (h + 1) * dv)
                o_ref[rows, v_cols] = o
                nrm, _ = _gn_parts(o)
                gate = rg_ref[rows, v_cols].astype(F32)
                gated_ref[rows, v_cols] = ((gate * _sigmoid(gate))
                                           * (nrm * g_ref[:, v_cols] + b_ref[:, v_cols])).astype(BF16)
                kk = (k.astype(F32) * kd_ref[h]).astype(BF16)
                state[h] = state[h] * cd_ref[h] + lax.dot_general(kk, v, _TN, preferred_element_type=F32)

    whole = lambda a: pl.BlockSpec(a.shape, lambda n: (0,) * a.ndim)
    cs = RET_SUB * c
    rows_v = pl.BlockSpec((cs, RET_V_W), lambda n: (n, 0))
    return pl.pallas_call(
        kern, name="retention_fwd", grid=(nc // RET_SUB,),
        in_specs=[
            pl.BlockSpec((cs, RET_QK_W), lambda n: (n, 0)),
            pl.BlockSpec((cs, RET_QK_W), lambda n: (n, 1)),
            rows_v, rows_v, whole(gn_g), whole(gn_b),
            whole(din), whole(qd), whole(kd),
            pl.BlockSpec(memory_space=pltpu.SMEM),
        ],
        out_specs=[
            rows_v,
            pl.BlockSpec((RET_HEADS, RET_SUB, dk, dv), lambda n: (0, n, 0, 0)),
            rows_v,
        ],
        out_shape=[
            jax.ShapeDtypeStruct((SEQ, RET_V_W), F32),
            jax.ShapeDtypeStruct((RET_HEADS, nc, dk, dv), BF16),
            jax.ShapeDtypeStruct((SEQ, RET_V_W), BF16),
        ],
        scratch_shapes=[pltpu.VMEM((RET_HEADS, dk, dv), F32)],
        compiler_params=_cparams(("arbitrary",)),
    )(rqk, rqk, rv, rg, gn_g, gn_b, din, qd, kd, cd)


def _retention_bwd(rqk, rv, states, d_gated, ro, rg, gn_g, gn_b, din, qd, kd, cd, cos, sin):
    nc = SEQ // RET_CHUNK
    c, dk, dv = RET_CHUNK, RET_DK, RET_DV
    half = dk // 2
    last = nc // RET_SUB - 1

    def unrot(g, cs, sn):
        g1, g2 = g[:, :half], g[:, half:]
        return jnp.concatenate([g1 * cs + g2 * sn, g2 * cs - g1 * sn], axis=-1)

    def kern(q_ref, k_ref, v_ref, st_ref, dg_ref, ro_ref, rg_ref, g_ref, b_ref, din_ref, qd_ref, kd_ref,
             cd_ref, cos_ref, sin_ref, out_ref, drg_ref, dgn_g_ref, dgn_b_ref, dstate):
        step = pl.program_id(0)

        @pl.when(step == 0)
        def _():
            dstate[...] = jnp.zeros_like(dstate)
            dgn_g_ref[...] = jnp.zeros_like(dgn_g_ref)
            dgn_b_ref[...] = jnp.zeros_like(dgn_b_ref)

        for sub in reversed(range(RET_SUB)):
            rows = slice(sub * c, (sub + 1) * c)
            cs, sn = cos_ref[rows, :], sin_ref[rows, :]
            for h in range(RET_HEADS):
                qk_cols, v_cols = slice(h * dk, (h + 1) * dk), slice(h * dv, (h + 1) * dv)
                q, k, v = q_ref[rows, qk_cols], k_ref[rows, qk_cols], v_ref[rows, v_cols]
                s_b = st_ref[h, sub]
                nrm, rstd = _gn_parts(ro_ref[rows, v_cols])
                gate, dg = rg_ref[rows, v_cols].astype(F32), dg_ref[rows, v_cols].astype(F32)
                sg = _sigmoid(gate)
                gn_gain = g_ref[:, v_cols]
                drg_ref[rows, v_cols] = (dg * (nrm * gn_gain + b_ref[:, v_cols])
                                         * (sg * (1.0 + gate * (1.0 - sg)))).astype(BF16)
                d_ron = dg * (gate * sg)
                dgn_g_ref[:, v_cols] += jnp.sum(d_ron * nrm, axis=0, keepdims=True)
                dgn_b_ref[:, v_cols] += jnp.sum(d_ron, axis=0, keepdims=True)
                d_n = d_ron * gn_gain
                d_o = rstd * (d_n - jnp.mean(d_n, axis=-1, keepdims=True)
                              - nrm * jnp.mean(d_n * nrm, axis=-1, keepdims=True))
                d_ob = d_o.astype(BF16)
                d_oq = (d_o * qd_ref[h]).astype(BF16)
                ds_b = dstate[h].astype(BF16)
                din_m = din_ref[h]
                a_b = (lax.dot_general(q, k, _NT, preferred_element_type=F32) * din_m).astype(BF16)
                kk = (k.astype(F32) * kd_ref[h]).astype(BF16)
                d_v = lax.dot_general(a_b, d_ob, _TN, preferred_element_type=F32)
                d_v += jnp.dot(kk, ds_b, preferred_element_type=F32)
                d_a = (lax.dot_general(d_ob, v, _NT, preferred_element_type=F32) * din_m).astype(BF16)
                d_q = jnp.dot(d_a, k, preferred_element_type=F32)
                d_q += lax.dot_general(d_oq, s_b, _NT, preferred_element_type=F32)
                d_k = lax.dot_general(d_a, q, _TN, preferred_element_type=F32)
                d_k += lax.dot_general(v, ds_b, _NT, preferred_element_type=F32) * kd_ref[h]
                dstate[h] = dstate[h] * cd_ref[h] + lax.dot_general(q, d_oq, _TN,
                                                                    preferred_element_type=F32)
                out_ref[rows, h * dk:(h + 1) * dk] = unrot(d_q, cs, sn).astype(BF16)
                out_ref[rows, RET_QK_W + h * dk:RET_QK_W + (h + 1) * dk] = (
                    unrot(d_k, cs, sn) * (RET_DK ** -0.5)).astype(BF16)
                out_ref[rows, 2 * RET_QK_W + h * dv:2 * RET_QK_W + (h + 1) * dv] = d_v.astype(BF16)

    whole = lambda a: pl.BlockSpec(a.shape, lambda n: (0,) * a.ndim)
    rs = RET_SUB * c
    rows_v = pl.BlockSpec((rs, RET_V_W), lambda n: (last - n, 0))
    return pl.pallas_call(
        kern, name="retention_bwd", grid=(nc // RET_SUB,),
        in_specs=[
            pl.BlockSpec((rs, RET_QK_W), lambda n: (last - n, 0)),
            pl.BlockSpec((rs, RET_QK_W), lambda n: (last - n, 1)),
            rows_v,
            pl.BlockSpec((RET_HEADS, RET_SUB, dk, dv), lambda n: (0, last - n, 0, 0)),
            rows_v, rows_v, rows_v, whole(gn_g), whole(gn_b),
            whole(din), whole(qd), whole(kd),
            pl.BlockSpec(memory_space=pltpu.SMEM),
            pl.BlockSpec((rs, half), lambda n: (last - n, 0)),
            pl.BlockSpec((rs, half), lambda n: (last - n, 0)),
        ],
        out_specs=[pl.BlockSpec((rs, 2 * RET_QK_W + RET_V_W), lambda n: (last - n, 0)), rows_v,
                   whole(gn_g), whole(gn_b)],
        out_shape=[jax.ShapeDtypeStruct((SEQ, 2 * RET_QK_W + RET_V_W), BF16),
                   jax.ShapeDtypeStruct((SEQ, RET_V_W), BF16),
                   jax.ShapeDtypeStruct((1, RET_V_W), F32), jax.ShapeDtypeStruct((1, RET_V_W), F32)],
        scratch_shapes=[pltpu.VMEM((RET_HEADS, dk, dv), F32)],
        compiler_params=_cparams(("arbitrary",)),
    )(rqk, rqk, rv, states, d_gated, ro, rg, gn_g, gn_b, din, qd, kd, cd, cos, sin)


def _bias_build(rel_bias, buckets):
    ng = len(ATT_GROUPS)

    def kern(tab_ref, bkt_ref, o_ref):
        g, h = pl.program_id(0), pl.program_id(1)
        bkt = bkt_ref[...]
        acc = jnp.zeros(bkt.shape, F32)
        for b in range(REL_BUCKETS):
            acc = jnp.where(bkt == b, tab_ref[b, g * ATT_HPG + h], acc)
        o_ref[...] = acc

    return pl.pallas_call(
        kern, name="bias_build", grid=(ng, ATT_HPG),
        in_specs=[pl.BlockSpec(memory_space=pltpu.SMEM),
                  pl.BlockSpec((None, ATT_BLK, 2 * ATT_BLK), lambda g, h: (g, 0, 0))],
        out_specs=pl.BlockSpec((None, None, ATT_BLK, 2 * ATT_BLK), lambda g, h: (g, h, 0, 0)),
        out_shape=jax.ShapeDtypeStruct((ng, ATT_HPG, ATT_BLK, 2 * ATT_BLK), F32),
        compiler_params=_cparams(("arbitrary", "arbitrary")),
    )(rel_bias, buckets)


def _bias_grad(dsb, buckets):
    ng = len(ATT_GROUPS)

    def kern(ds_ref, bkt_ref, o_ref):
        g, h = pl.program_id(0), pl.program_id(1)
        bkt, ds = bkt_ref[...], ds_ref[...]
        for b in range(REL_BUCKETS):
            o_ref[b, g * ATT_HPG + h] = jnp.sum(jnp.where(bkt == b, ds, 0.0))

    return pl.pallas_call(
        kern, name="bias_grad", grid=(ng, ATT_HPG),
        in_specs=[pl.BlockSpec((None, None, ATT_BLK, 2 * ATT_BLK), lambda g, h: (g, h, 0, 0)),
                  pl.BlockSpec((None, ATT_BLK, 2 * ATT_BLK), lambda g, h: (g, 0, 0))],
        out_specs=pl.BlockSpec(memory_space=pltpu.SMEM),
        out_shape=jax.ShapeDtypeStruct((REL_BUCKETS, N_ATT_HEADS), F32),
        compiler_params=_cparams(("arbitrary", "arbitrary")),
    )(dsb, buckets)


_NT = (((1,), (1,)), ((), ()))
_TN = (((0,), (0,)), ((), ()))
_ATT_SCALE = ATT_DH ** -0.5


def _window_mask(has_prev):
    qi = lax.broadcasted_iota(I32, (ATT_BLK, 2 * ATT_BLK), 0)
    kj = lax.broadcasted_iota(I32, (ATT_BLK, 2 * ATT_BLK), 1)
    prev_ok = jnp.logical_and(jnp.logical_and(kj < ATT_BLK, kj >= qi), has_prev)
    return jnp.logical_or(prev_ok, jnp.logical_and(kj >= ATT_BLK, qi >= kj - ATT_BLK))


def _head_specs(col0):
    return pl.BlockSpec((SEQ, ATT_DH), lambda h: (0, col0 + h))


def _sub_rows(start, size, dil):
    return pl.ds(start, size) if dil == 1 else pl.ds(start, size, stride=dil)


def _att_blocks(dil):
    nb = SEQ // dil // ATT_BLK
    return [(r + dil * n * ATT_BLK, n > 0, n + 1 < nb) for r in range(dil) for n in range(nb)]


def _att_fwd(gi, dil, qkv, bias, after=()):
    blk, dh = ATT_BLK, ATT_DH
    pad = dil * blk
    col0 = 3 * ATT_HPG * gi

    def kern(q_ref, k_ref, v_ref, b_ref, *rest):
        o_ref, l_ref, qf, kpad, vpad = rest[len(after):]
        zero = jnp.zeros((pad, dh), F32)
        kpad[0:pad, :] = zero
        vpad[0:pad, :] = zero
        kpad[pad:, :] = k_ref[...].astype(F32)
        vpad[pad:, :] = v_ref[...].astype(F32)
        qf[...] = q_ref[...].astype(F32)
        bias_m = b_ref[...]
        for start, has_prev, _ in _att_blocks(dil):
            rows, window = _sub_rows(start, blk, dil), _sub_rows(start, 2 * blk, dil)
            q = qf[rows, :].astype(BF16)
            kw, vw = kpad[window, :].astype(BF16), vpad[window, :].astype(BF16)
            valid = _window_mask(has_prev)
            s = lax.dot_general(q, kw, _NT, preferred_element_type=F32) * _ATT_SCALE + bias_m
            s = jnp.where(valid, s, -1e30)
            mx = jnp.max(s, axis=-1, keepdims=True)
            e = jnp.exp(s - mx)
            den = jnp.sum(e, axis=-1, keepdims=True)
            o_ref[rows, :] = jnp.dot((e / den).astype(BF16), vw, preferred_element_type=F32)
            l_ref[rows, :] = jnp.broadcast_to(mx + jnp.log(den), (blk, dh))

    return pl.pallas_call(
        kern, name=f"att_fwd_g{gi}", grid=(ATT_HPG,),
        in_specs=[_head_specs(col0), _head_specs(col0 + ATT_HPG), _head_specs(col0 + 2 * ATT_HPG),
                  pl.BlockSpec((None, None, blk, 2 * blk), lambda h: (gi, h, 0, 0))]
        + [pl.BlockSpec(memory_space=pl.ANY)] * len(after),
        out_specs=[_head_specs(0), _head_specs(0)],
        out_shape=[jax.ShapeDtypeStruct((SEQ, ATT_W), F32), jax.ShapeDtypeStruct((SEQ, ATT_W), F32)],
        scratch_shapes=[pltpu.VMEM((SEQ, dh), F32), pltpu.VMEM((SEQ + pad, dh), F32),
                        pltpu.VMEM((SEQ + pad, dh), F32)],
        compiler_params=_cparams(("arbitrary",)),
    )(qkv, qkv, qkv, bias, *after)


def _att_bwd(gi, dil, qkv, d_att, lse, dd, bias):
    blk, dh = ATT_BLK, ATT_DH
    pad = dil * blk
    col0 = 3 * ATT_HPG * gi

    def kern(q_ref, k_ref, v_ref, do_ref, l_ref, d_ref, b_ref, dqkv_ref, dsb_ref,
             qf, kpad, vpad, dq_s, dkpad, dvpad):
        zero = jnp.zeros((pad, dh), F32)
        kpad[0:pad, :] = zero
        vpad[0:pad, :] = zero
        kpad[pad:, :] = k_ref[...].astype(F32)
        vpad[pad:, :] = v_ref[...].astype(F32)
        qf[...] = q_ref[...].astype(F32)
        dkpad[...] = jnp.zeros_like(dkpad)
        dvpad[...] = jnp.zeros_like(dvpad)
        bias_m = b_ref[...]
        ds_sum = jnp.zeros((blk, 2 * blk), F32)

        for start, has_prev, _ in _att_blocks(dil):
            rows, window = _sub_rows(start, blk, dil), _sub_rows(start, 2 * blk, dil)
            q, d_o = qf[rows, :].astype(BF16), do_ref[rows, :].astype(BF16)
            kw, vw = kpad[window, :].astype(BF16), vpad[window, :].astype(BF16)
            lrow, drow = l_ref[rows, :][:, :1], d_ref[rows, :][:, :1]
            valid = _window_mask(has_prev)
            s = lax.dot_general(q, kw, _NT, preferred_element_type=F32) * _ATT_SCALE + bias_m
            p = jnp.where(valid, jnp.exp(jnp.where(valid, s, -1e30) - lrow), 0.0)
            dp = lax.dot_general(d_o, vw, _NT, preferred_element_type=F32)
            ds = p * (dp - drow)
            ds_b = ds.astype(BF16)
            dq_s[rows, :] = jnp.dot(ds_b, kw, preferred_element_type=F32) * _ATT_SCALE
            dkpad[window, :] += lax.dot_general(ds_b, q, _TN, preferred_element_type=F32) * _ATT_SCALE
            dvpad[window, :] += lax.dot_general(p.astype(BF16), d_o, _TN, preferred_element_type=F32)
            ds_sum = ds_sum + ds
        dsb_ref[...] = ds_sum

        dqkv_ref[0] = dq_s[...].astype(BF16)
        dqkv_ref[1] = dkpad[pad:, :].astype(BF16)
        dqkv_ref[2] = dvpad[pad:, :].astype(BF16)

    return pl.pallas_call(
        kern, name=f"att_bwd_g{gi}", grid=(ATT_HPG,),
        in_specs=[_head_specs(col0), _head_specs(col0 + ATT_HPG), _head_specs(col0 + 2 * ATT_HPG),
                  _head_specs(0), _head_specs(0), _head_specs(0),
                  pl.BlockSpec((None, None, blk, 2 * blk), lambda h: (gi, h, 0, 0))],
        out_specs=[pl.BlockSpec((3, SEQ, dh), lambda h: (0, 0, h)),
                   pl.BlockSpec((None, blk, 2 * blk), lambda h: (h, 0, 0))],
        out_shape=[jax.ShapeDtypeStruct((3, SEQ, ATT_W), BF16),
                   jax.ShapeDtypeStruct((ATT_HPG, blk, 2 * blk), F32)],
        scratch_shapes=[pltpu.VMEM((SEQ, dh), F32), pltpu.VMEM((SEQ + pad, dh), F32),
                        pltpu.VMEM((SEQ + pad, dh), F32), pltpu.VMEM((SEQ, dh), F32),
                        pltpu.VMEM((SEQ + pad, dh), F32), pltpu.VMEM((SEQ + pad, dh), F32)],
        compiler_params=_cparams(("arbitrary",)),
    )(qkv, qkv, qkv, d_att, lse, dd, bias)


def _rms_parts(x):
    r = lax.rsqrt(jnp.mean(x * x, axis=-1, keepdims=True) + RMS_EPS)
    return x * r, r


def _rms_bwd(d_xhat, xhat, r):
    return r * (d_xhat - xhat * jnp.mean(d_xhat * xhat, axis=-1, keepdims=True))


def _prenorm_fwd(name, x, gain, shift, scale, tm=256):
    def kern(x_ref, g_ref, sh_ref, sc_ref, o_ref):
        xhat, _ = _rms_parts(x_ref[...])
        o_ref[...] = ((xhat * g_ref[...]) * (1.0 + sc_ref[...]) + sh_ref[...]).astype(BF16)

    rows = pl.BlockSpec((tm, D_MODEL), lambda i: (i, 0))
    vec = pl.BlockSpec((1, D_MODEL), lambda i: (0, 0))
    return pl.pallas_call(
        kern, name=name, grid=(x.shape[0] // tm,), in_specs=[rows, vec, vec, vec], out_specs=rows,
        out_shape=jax.ShapeDtypeStruct(x.shape, BF16), compiler_params=_cparams(("parallel",)),
    )(x, gain, shift, scale)


def _prenorm_bwd_epi(d_h, x, resid, gain, scale, branch=None, gate=None):
    xhat, r = _rms_parts(x)
    nrm = xhat * gain
    d_n = d_h * (1.0 + scale)
    dx = _rms_bwd(d_n * gain, xhat, r) + resid
    sums = (jnp.sum(d_h, axis=0, keepdims=True), jnp.sum(d_h * nrm, axis=0, keepdims=True),
            jnp.sum(d_n * xhat, axis=0, keepdims=True))
    if branch is None:
        return (dx,) + sums
    return (dx, dx * gate) + sums + (jnp.sum(dx * branch, axis=0, keepdims=True),)


def _row_operands(tm, rows, vecs):
    return ([(a, (tm, D_MODEL), lambda i, j, kk: (i, 0)) for a in rows]
            + [(v, (1, D_MODEL), lambda i, j, kk: (0, 0)) for v in vecs])


def _gn_parts(ro):
    mu = jnp.mean(ro, axis=-1, keepdims=True)
    cen = ro - mu
    rstd = lax.rsqrt(jnp.mean(cen * cen, axis=-1, keepdims=True) + GN_EPS)
    return cen * rstd, rstd


MERGE_TM = 512


def _att_out(os_, ls_, w_att_out, gates, ret_out):
    tm = MERGE_TM

    def kern(o0, o1, o2, l0, l1, l2, w_ref, ga_ref, gb_ref, ro_ref, att_ref, attb_ref, lse_ref,
             ao_ref, mg_ref):
        l0v, l1v, l2v = l0[...], l1[...], l2[...]
        mx = jnp.maximum(jnp.maximum(l0v, l1v), l2v)
        e0, e1, e2 = jnp.exp(l0v - mx), jnp.exp(l1v - mx), jnp.exp(l2v - mx)
        den = e0 + e1 + e2
        att = (e0 / den) * o0[...] + (e1 / den) * o1[...] + (e2 / den) * o2[...]
        att_b = att.astype(BF16)
        att_ref[...] = att
        attb_ref[...] = att_b
        lse_ref[...] = mx + jnp.log(den)
        att_out = jnp.dot(att_b, w_ref[...], preferred_element_type=F32)
        ao_ref[...], merged = _merge_fwd_epi(att_out, ga_ref[...], gb_ref[...], ro_ref[...])
        mg_ref[...] = merged.astype(BF16)

    rows_w = pl.BlockSpec((tm, ATT_W), lambda i: (i, 0))
    rows_d = pl.BlockSpec((tm, D_MODEL), lambda i: (i, 0))
    return pl.pallas_call(
        kern, name="att_out", grid=(SEQ // tm,),
        in_specs=[rows_w] * 6 + [pl.BlockSpec((ATT_W, D_MODEL), lambda i: (0, 0)), rows_d,
                                 pl.BlockSpec((tm, D_MODEL), lambda i: (i, 1)), rows_d],
        out_specs=[rows_w, rows_w, rows_w, rows_d, rows_d],
        out_shape=[jax.ShapeDtypeStruct((SEQ, ATT_W), F32), jax.ShapeDtypeStruct((SEQ, ATT_W), BF16),
                   jax.ShapeDtypeStruct((SEQ, ATT_W), F32), jax.ShapeDtypeStruct((SEQ, D_MODEL), F32),
                   jax.ShapeDtypeStruct((SEQ, D_MODEL), BF16)],
        compiler_params=_cparams(("parallel",)),
    )(*os_, *ls_, w_att_out, gates, gates, ret_out)


def _merge_operands(gates, ret_out, att_out=None):
    ops = [(gates, (MERGE_TM, D_MODEL), lambda i, j, kk: (i, 0)),
           (gates, (MERGE_TM, D_MODEL), lambda i, j, kk: (i, 1)),
           (ret_out, (MERGE_TM, D_MODEL), lambda i, j, kk: (i, 0))]
    if att_out is not None:
        ops.append((att_out, (MERGE_TM, D_MODEL), lambda i, j, kk: (i, 0)))
    return ops


def _merge_fwd_epi(att_out, ga, gb, ret_out):
    return att_out, _sigmoid(ga.astype(F32)) * ret_out + _sigmoid(gb.astype(F32)) * att_out


def _merge_bwd_epi(d_merged, ga, gb, ret_out, att_out):
    sa, sb = _sigmoid(ga.astype(F32)), _sigmoid(gb.astype(F32))
    return (d_merged * sa, d_merged * sb, d_merged * ret_out * (sa * (1.0 - sa)),
            d_merged * att_out * (sb * (1.0 - sb)))


def _att_out_bwd_epi(d_att, att):
    outs = []
    for h in range(ATT_HPG):
        sl = slice(h * ATT_DH, (h + 1) * ATT_DH)
        outs.append(jnp.broadcast_to(jnp.sum(d_att[:, sl] * att[:, sl], axis=-1, keepdims=True),
                                     (d_att.shape[0], ATT_DH)))
    return d_att, jnp.concatenate(outs, axis=-1)


def _loss_head_epi(branch, x_prev, target, gate, gain):
    x3 = x_prev + gate * branch
    xhat, r = _rms_parts(x3)
    err = xhat * gain - target
    d_y = err / D_MODEL
    loss = 0.5 * jnp.sum(jnp.mean(err * err, axis=-1, keepdims=True), axis=0, keepdims=True)
    d_x = _rms_bwd(d_y * gain, xhat, r)
    return (d_x, d_x * gate, jnp.broadcast_to(loss, (1, D_MODEL)),
            jnp.sum(d_y * xhat, axis=0, keepdims=True), jnp.sum(d_x * branch, axis=0, keepdims=True))


def _local_step(pos, x, target, mod, norm1_g, norm2_g, norm_f_g, rel_bias, gn_g, gn_b, w_in, rest_gather):
    sh1, sc1, g1, sh2, sc2, g2 = [mod[:, i * D_MODEL:(i + 1) * D_MODEL] for i in range(6)]
    cos, sin = _rope_tables()
    din, qd, kd, cd = _decay_tables()
    buckets = _bucket_tables()
    bias = _bias_build(rel_bias, buckets)
    dils = [d for _, d in ATT_GROUPS]

    h1 = _prenorm_fwd("prenorm1_fwd", x, norm1_g, sh1, sc1)

    qk_tn = 2 * RET_DK

    def rot_epi(acc, cs, sn, scale):
        half = RET_DK // 2
        outs = []
        for h0 in range(0, qk_tn, RET_DK):
            x1, x2 = acc[:, h0:h0 + half], acc[:, h0 + half:h0 + RET_DK]
            outs += [x1 * cs - x2 * sn, x1 * sn + x2 * cs]
        return (jnp.concatenate(outs, axis=-1) * scale,)

    qk_scale = jnp.concatenate([jnp.ones((1, RET_QK_W), F32),
                                jnp.full((1, RET_QK_W), RET_DK ** -0.5, F32)], axis=-1)
    rope_ex = [(cos, (TM, RET_DK // 2), lambda i, j, kk: (i, 0)),
               (sin, (TM, RET_DK // 2), lambda i, j, kk: (i, 0)),
               (qk_scale, (1, qk_tn), lambda i, j, kk: (0, j))]
    rest_sems, rest_shards, rest_fulls, rest_token = rest_gather
    behind = [rest_token]
    rv = _matmul("proj_rv", h1, w_in, "nn", SEQ, RET_V_W, D_MODEL, [BF16], b_off=OFF_V, tk=D_MODEL,
                 after=behind)[0]
    rg = _matmul("proj_rg", h1, w_in, "nn", SEQ, RET_V_W, D_MODEL, [BF16], b_off=OFF_G, tk=D_MODEL,
                 after=behind)[0]
    gates = _matmul("proj_gates", h1, w_in, "nn", SEQ, 2 * D_MODEL, D_MODEL, [BF16], b_off=OFF_GATE,
                    tn=512, tk=D_MODEL, after=behind)[0]
    aqkv = _matmul("proj_att", h1, w_in, "nn", SEQ, 9 * ATT_W, D_MODEL, [BF16], b_off=OFF_ATT,
                   tn=512, tk=D_MODEL, after=behind)[0]

    rqk = _matmul("proj_qk", h1, w_in, "nn", SEQ, 2 * RET_QK_W, D_MODEL, [BF16], b_off=OFF_Q,
                  tn=qk_tn, tk=D_MODEL, epilogue=rot_epi, extras=rope_ex, after=behind)[0]
    ro, states, gated = _retention_fwd(rqk, rv, rg, gn_g, gn_b, din, qd, kd, cd)
    os_, ls_ = [], []
    for gi in range(3):
        if gi == 2:
            rest_sems, rest_fulls, fwd_token = _gather_rest_forward(
                rest_sems, rest_shards, rest_fulls, [gated, gates] + os_)
        o_g, l_g = _att_fwd(gi, dils[gi], aqkv, bias, after=[fwd_token] if gi == 2 else ())
        os_.append(o_g)
        ls_.append(l_g)
    w_ret_out, w_att_out, w_o, w_ff1, w_ff2 = _gather_rest_end(rest_sems, rest_fulls, [os_[2]])
    ret_out = _matmul("ret_out", gated, w_ret_out, "nn", SEQ, D_MODEL, RET_V_W, [F32], tk=RET_V_W)[0]
    att, att_b, lse, att_out, merged = _att_out(os_, ls_, w_att_out, gates, ret_out)

    def mix_epi(acc, xt, g, gain, sh, sc):
        x_new = xt + g * acc
        xhat, _ = _rms_parts(x_new)
        return x_new, acc, (xhat * gain) * (1.0 + sc) + sh

    x2, mix, h2 = _matmul("mix_out", merged, w_o, "nn", SEQ, D_MODEL, D_MODEL, [F32, BF16, BF16],
                          epilogue=mix_epi, extras=_row_operands(TM, [x], [g1, norm2_g, sh2, sc2]))

    def relu2_epi(acc):
        r = jnp.maximum(acc, 0.0)
        return r * r, r

    act, relu_u = _matmul("ff1", h2, w_ff1, "nn", SEQ, D_FF, D_MODEL, [BF16, BF16], tk=D_MODEL,
                          epilogue=relu2_epi, b_buffers=3)
    d_x3, d_y2, loss, d_gf, d_g2 = _matmul(
        "ff2", act, w_ff2, "nn", SEQ, D_MODEL, D_FF, [F32, BF16], tm=TM, tk=1024, n_sums=3,
        epilogue=_loss_head_epi, extras=_row_operands(TM, [x2, target], [g2, norm_f_g]), b_buffers=3)

    def relu2_bwd_epi(acc, rt):
        return (acc * (2.0 * rt.astype(F32)),)

    gw_ff2 = _matmul_tn_pair("ff2_dw", pos, act, d_y2, D_FF, D_MODEL, SEQ, D_FF // N_CHIPS,
                             tm=512, tn=1024, tk=SEQ)
    d_u = _matmul("ff2_dx", d_y2, w_ff2, "nt", SEQ, D_FF, D_MODEL, [BF16], epilogue=relu2_bwd_epi,
                  extras=[(relu_u, (TM, TN), lambda i, j, kk: (i, j))], b_buffers=3)[0]
    gw_ff1 = _matmul_tn_pair("ff1_dw", pos, h2, d_u, D_MODEL, D_FF, SEQ, D_MODEL,
                             tm=512, tn=1024, tk=SEQ)
    ffn = ["w_ff2", "w_ff1"]
    ffn_started = _ici_start("ici_start_ffn", ffn, [gw_ff2, gw_ff1])
    d_x2, d_mix, d_sh2, d_sc2, d_n2g, d_g1 = _matmul(
        "ff1_dx", d_u, w_ff1, "nt", SEQ, D_MODEL, D_FF, [F32, BF16], tm=TM, tk=1024, n_sums=4,
        epilogue=_prenorm_bwd_epi, extras=_row_operands(TM, [x2, d_x3], [norm2_g, sc2])
        + _row_operands(TM, [mix], [g1]), after=[ffn_started[3]])
    gw_o = _matmul_tn_pair("mix_dw", pos, merged, d_mix, D_MODEL, D_MODEL, SEQ, D_MODEL // N_CHIPS,
                           tm=128, tn=1024, tk=2048)
    d_ret_out, d_att_out, d_ga, d_gb = _matmul(
        "mix_dx", d_mix, w_o, "nt", SEQ, D_MODEL, D_MODEL, [BF16] * 4, tm=MERGE_TM,
        epilogue=_merge_bwd_epi, extras=_merge_operands(gates, ret_out, att_out))

    gw_ret_out = _matmul_tn_pair("ret_out_dw", pos, gated, d_ret_out, RET_V_W, D_MODEL, SEQ,
                                 RET_V_W // N_CHIPS, tm=256, tn=1024, tk=SEQ)
    gw_att_out = _matmul_tn_pair("att_out_dw", pos, att_b, d_att_out, ATT_W, D_MODEL, SEQ, ATT_W,
                                 tm=256, tn=1024, tk=2048)
    mixer = ["w_o", "w_ret_out", "w_att_out"]
    mixer_started = _ici_start("ici_start_mixer", mixer, [gw_o, gw_ret_out, gw_att_out])
    d_gated = _matmul("ret_out_dx", d_ret_out, w_ret_out, "nt", SEQ, RET_V_W, D_MODEL, [BF16],
                      after=[mixer_started[3]])[0]
    d_att, dd = _matmul("att_out_dx", d_att_out, w_att_out, "nt", SEQ, ATT_W, D_MODEL, [F32, F32],
                        epilogue=_att_out_bwd_epi,
                        extras=[(att, (TM, ATT_W), lambda i, j, kk: (i, 0))], after=[mixer_started[3]])

    d_rqkv, d_rg, d_gn_g, d_gn_b = _retention_bwd(rqk, rv, states, d_gated, ro, rg, gn_g, gn_b,
                                                  din, qd, kd, cd, cos, sin)

    d_aqkv, dsbs = [], []
    for gi in range(3):
        dqkv, dsb = _att_bwd(gi, dils[gi], aqkv, d_att, lse, dd, bias)
        d_aqkv.append(dqkv)
        dsbs.append(dsb)
    d_rel_bias = _bias_grad(jnp.stack(dsbs), buckets)

    d_proj = ([(d_rqkv, False), (d_rg, False)] + [(t, True) for t in d_aqkv]
              + [(d_ga, False), (d_gb, False)])
    gw_in = _matmul_tn_pair("proj_dw", pos, h1, d_proj, D_MODEL, IN_COLS, SEQ, D_MODEL,
                            tm=512, tn=ATT_W, tk=SEQ)
    sems, (gw_in,), (land,), token = _ici_start("ici_start_w_in", ["w_in"], [gw_in])
    grad_x, d_sh1, d_sc1, d_n1g = _matmul(
        "proj_dx", d_proj, w_in, "nt", SEQ, D_MODEL, IN_COLS, [F32], tn=1024, tk=ATT_W, n_sums=3,
        epilogue=_prenorm_bwd_epi, extras=_row_operands(TM, [x, d_x2], [norm1_g, sc1]), after=[token])
    pending = (sems, land)

    names = ffn + mixer
    psums, got = _ici_wait("ici_wait_rest", names, list(ffn_started[0]) + list(mixer_started[0]),
                           list(ffn_started[1]) + list(mixer_started[1]),
                           list(ffn_started[2]) + list(mixer_started[2]), [grad_x])
    g_big = {n: _final_sum("final_" + n, pos, dict(BIG)[n], psums[i], got[i], SHARD[n])
             for i, n in enumerate(names)}
    d_mod = jnp.concatenate([d_sh1, d_sc1, d_g1, d_sh2, d_sc2, d_g2], axis=-1)
    small = dict(norm1_g=d_n1g, norm2_g=d_n2g, norm_f_g=d_gf, gn_g=d_gn_g, gn_b=d_gn_b,
                 rel_bias=d_rel_bias)
    return loss, grad_x, d_mod, small, g_big, (gw_in,) + pending


def _me():
    return lax.axis_index("x"), lax.axis_index("y"), lax.axis_index("c")


def _peer(x, y, c, mask):
    return (x ^ ((mask >> 2) & 1), y ^ ((mask >> 1) & 1), c ^ (mask & 1))


def _gather8(src_ref, dst_ref, send_sems, recv_sems):
    x, y, c = _me()
    me = 4 * x + 2 * y + c
    copies = []
    for mask in range(1, N_DEV):
        cp = pltpu.make_async_remote_copy(
            src_ref=src_ref, dst_ref=dst_ref.at[me], send_sem=send_sems.at[mask - 1],
            recv_sem=recv_sems.at[mask - 1], device_id=_peer(x, y, c, mask), device_id_type=MESH)
        cp.start()
        copies.append(cp)
    dst_ref[me] = src_ref[...]
    for cp in copies:
        cp.wait_recv()
    for cp in copies:
        cp.wait_send()


def _ada_fwd(c_in, w_ada, b_ada):
    ncol = ADA_COLS // N_CHIPS

    def body(c_ref, w_ref, b_ref, mod_ref, sc_ref, cbuf, cg, mbuf, mg, s1, r1, s2, r2):
        x, y, c = _me()
        me = 4 * x + 2 * y + c
        cv = c_ref[...]
        cbuf[...] = jnp.broadcast_to(cv * _sigmoid(cv), cbuf.shape)
        _gather8(cbuf, cg, s1, r1)
        rows = lax.broadcasted_iota(I32, (N_DEV, D_MODEL), 0)
        sc_all = jnp.zeros((N_DEV, D_MODEL), F32)
        for d in range(N_DEV):
            sc_all = jnp.where(rows == d, cg[d], sc_all)
        sc_ref[...] = sc_all
        mbuf[...] = jnp.dot(sc_all.astype(BF16), w_ref[...].astype(BF16), preferred_element_type=F32)
        _gather8(mbuf, mg, s2, r2)
        rowsel = lax.broadcasted_iota(I32, (N_DEV, ncol), 0) == me
        for k in range(N_CHIPS):
            blk = mg[2 * k]
            row = jnp.sum(jnp.where(rowsel, blk, 0.0), axis=0, keepdims=True)
            mod_ref[:, k * ncol:(k + 1) * ncol] = row + b_ref[:, k * ncol:(k + 1) * ncol]

    vm = pl.BlockSpec(memory_space=pltpu.VMEM)
    return pl.pallas_call(
        body, name="ada_fwd",
        in_specs=[vm, vm, vm], out_specs=[vm, vm],
        out_shape=[jax.ShapeDtypeStruct((1, ADA_COLS), F32), jax.ShapeDtypeStruct((N_DEV, D_MODEL), F32)],
        scratch_shapes=[
            pltpu.VMEM((8, D_MODEL), F32), pltpu.VMEM((N_DEV, 8, D_MODEL), F32),
            pltpu.VMEM((8, ncol), F32), pltpu.VMEM((N_DEV, 8, ncol), F32),
            pltpu.SemaphoreType.DMA((N_DEV - 1,)), pltpu.SemaphoreType.DMA((N_DEV - 1,)),
            pltpu.SemaphoreType.DMA((N_DEV - 1,)), pltpu.SemaphoreType.DMA((N_DEV - 1,)),
        ],
        compiler_params=pltpu.CompilerParams(vmem_limit_bytes=VMEM_LIMIT_V7X),
    )(c_in, w_ada, b_ada)


def _small_reduce(pack, sc_all, after=()):
    ncol = ADA_COLS // N_CHIPS

    def body(p_ref, sc_ref, *rest):
        tot_ref, gw_ref, pg, s1, r1 = rest[len(after):]
        x, y, _ = _me()
        chip = 2 * x + y
        _gather8(p_ref, pg, s1, r1)
        tot = pg[0]
        for d in range(1, N_DEV):
            tot = tot + pg[d]
        tot_ref[...] = tot
        rows = lax.broadcasted_iota(I32, (N_DEV, ncol), 0)
        dmod = jnp.zeros((N_DEV, ncol), F32)
        for k in range(N_CHIPS):
            part = jnp.zeros((N_DEV, ncol), F32)
            for d in range(N_DEV):
                part = jnp.where(rows == d, pg[d, :, k * ncol:(k + 1) * ncol][0:1, :], part)
            dmod = jnp.where(chip == k, part, dmod)
        gw_ref[...] = lax.dot_general(sc_ref[...].astype(BF16), dmod.astype(BF16), _TN,
                                      preferred_element_type=F32)

    vm = pl.BlockSpec(memory_space=pltpu.VMEM)
    return pl.pallas_call(
        body, name="small_reduce",
        in_specs=[vm, vm] + [pl.BlockSpec(memory_space=pl.ANY)] * len(after), out_specs=[vm, vm],
        out_shape=[jax.ShapeDtypeStruct((8, ADA_COLS), F32), jax.ShapeDtypeStruct((D_MODEL, ncol), F32)],
        scratch_shapes=[pltpu.VMEM((N_DEV, 8, ADA_COLS), F32),
                        pltpu.SemaphoreType.DMA((N_DEV - 1,)), pltpu.SemaphoreType.DMA((N_DEV - 1,))],
        compiler_params=pltpu.CompilerParams(vmem_limit_bytes=VMEM_LIMIT_V7X),
    )(pack, sc_all, *after)


BIG = (("w_in", 1), ("w_ret_out", 0), ("w_att_out", 1), ("w_o", 0), ("w_ff1", 1), ("w_ff2", 0))
SHARD = {"w_in": (D_MODEL, IN_COLS // N_CHIPS), "w_ret_out": (RET_V_W // N_CHIPS, D_MODEL),
         "w_att_out": (ATT_W, D_MODEL // N_CHIPS), "w_o": (D_MODEL // N_CHIPS, D_MODEL),
         "w_ff1": (D_MODEL, D_FF // N_CHIPS), "w_ff2": (D_FF // N_CHIPS, D_MODEL)}
_CHIP_FLIPS = ((1, 0), (0, 1), (1, 1))


def _region(ref, axis, chip, half, shard_shape):
    r, cw = shard_shape
    hr = r // 2
    if axis == 1:
        return ref.at[pl.ds(half * hr, hr), pl.ds(chip * cw, cw)]
    return ref.at[pl.ds(chip * r + half * hr, hr), :]


CAST_ROWS = 128


def _gather_weights(shards, n_remote):
    nw = len(BIG)
    shapes = [s.shape for s in shards]
    full_shapes = [(r, N_CHIPS * cw) if ax == 1 else (N_CHIPS * r, cw)
                   for (r, cw), (_, ax) in zip(shapes, BIG)]

    def body(*refs):
        ins, outs = refs[:nw], refs[nw:2 * nw]
        own = refs[2 * nw:3 * nw]
        from_ici, from_sib = refs[3 * nw:3 * nw + n_remote], refs[3 * nw + n_remote:3 * nw + 2 * n_remote]
        ld_sem, st_sem, s_ici, r_ici, s_d2d, r_d2d, st_a, st_b, stage = refs[3 * nw + 2 * n_remote:]
        x, y, c = _me()
        chip = 2 * x + y
        sib = (x, y, 1 - c)
        loads = [pltpu.make_async_copy(ins[i], stage if i == 0 else own[i], ld_sem.at[i])
                 for i in range(nw)]
        for cp in loads:
            cp.start()
        pending, first = [], []
        for i, (_, ax) in enumerate(BIG):
            r, cw = shapes[i]
            hr = r // 2
            loads[i].wait()
            if i == 0:
                for r0 in range(0, r, CAST_ROWS):
                    own[0][r0:r0 + CAST_ROWS, :] = stage[r0:r0 + CAST_ROWS, :].astype(BF16)
            dst = outs[i].at[:, pl.ds(chip * cw, cw)] if ax == 1 else outs[i].at[pl.ds(chip * r, r), :]
            cp = pltpu.make_async_copy(own[i], dst, st_sem.at[i])
            cp.start()
            pending.append(cp)
            for j, (fx, fy) in enumerate(_CHIP_FLIPS if i < n_remote else ()):
                rc = pltpu.make_async_remote_copy(
                    src_ref=own[i].at[pl.ds(c * hr, hr), :], dst_ref=from_ici[i].at[j],
                    send_sem=s_ici.at[j * nw + i], recv_sem=r_ici.at[j * nw + i],
                    device_id=(x ^ fx, y ^ fy, c), device_id_type=MESH)
                rc.start()
                first.append((j, i, rc))
        passed = []
        for j, i, rc in first:
            fx, fy = _CHIP_FLIPS[j]
            src_chip = 2 * (x ^ fx) + (y ^ fy)
            ax = BIG[i][1]
            rc.wait_recv()
            fw = pltpu.make_async_remote_copy(
                src_ref=from_ici[i].at[j], dst_ref=from_sib[i].at[j], send_sem=s_d2d.at[j * nw + i],
                recv_sem=r_d2d.at[j * nw + i], device_id=sib, device_id_type=MESH)
            fw.start()
            passed.append((j, i, src_chip, fw))
            st = pltpu.make_async_copy(from_ici[i].at[j], _region(outs[i], ax, src_chip, c, shapes[i]),
                                       st_a.at[j * nw + i])
            st.start()
            pending.append(st)
        for j, i, src_chip, fw in passed:
            fw.wait_recv()
            st = pltpu.make_async_copy(from_sib[i].at[j],
                                       _region(outs[i], BIG[i][1], src_chip, 1 - c, shapes[i]),
                                       st_b.at[j * nw + i])
            st.start()
            pending.append(st)
        for _, _, rc in first:
            rc.wait_send()
        for _, _, _, fw in passed:
            fw.wait_send()
        for cp in pending:
            cp.wait()

    hbm = pl.BlockSpec(memory_space=pl.ANY)
    halves = [pltpu.VMEM((3, r // 2, cw), BF16) for r, cw in shapes[:n_remote]]
    return pl.pallas_call(
        body, name="gather_weights",
        in_specs=[hbm] * nw, out_specs=[hbm] * nw,
        out_shape=[jax.ShapeDtypeStruct(fs, BF16) for fs in full_shapes],
        scratch_shapes=[pltpu.VMEM(sh, BF16) for sh in shapes] + halves + halves
        + [pltpu.SemaphoreType.DMA((nw,)), pltpu.SemaphoreType.DMA((nw,))]
        + [pltpu.SemaphoreType.DMA((3 * nw,))] * 6 + [pltpu.VMEM(shapes[0], F32)],
        compiler_params=pltpu.CompilerParams(vmem_limit_bytes=VMEM_LIMIT_V7X),
    )(*shards)


REST = BIG[1:]
_SIDE_EFFECTS = pltpu.CompilerParams(has_side_effects=pltpu.SideEffectType.DATAFLOW_SIDE_EFFECTING)
_ANY_SPEC = pl.BlockSpec(memory_space=pl.ANY)


def _rest_ici_copies(shard_refs, full_refs, sems):
    x, y, c = _me()
    chip = 2 * x + y
    n = 3 * len(REST)
    copies = []
    for i, (name, ax) in enumerate(REST):
        hr = SHARD[name][0] // 2
        for j, (fx, fy) in enumerate(_CHIP_FLIPS):
            copies.append(pltpu.make_async_remote_copy(
                src_ref=shard_refs[i].at[pl.ds(c * hr, hr), :],
                dst_ref=_region(full_refs[i], ax, chip, c, SHARD[name]),
                send_sem=sems[3 * i + j], recv_sem=sems[n + 3 * i + j],
                device_id=(x ^ fx, y ^ fy, c), device_id_type=MESH))
    return copies


def _rest_d2d_copies(full_refs, sems):
    x, y, c = _me()
    n = 3 * len(REST)
    copies = []
    for i, (name, ax) in enumerate(REST):
        for j, (fx, fy) in enumerate(_CHIP_FLIPS):
            reg = _region(full_refs[i], ax, 2 * (x ^ fx) + (y ^ fy), c, SHARD[name])
            copies.append(pltpu.make_async_remote_copy(
                src_ref=reg, dst_ref=reg, send_sem=sems[3 * i + j], recv_sem=sems[n + 3 * i + j],
                device_id=(x, y, 1 - c), device_id_type=MESH))
    return copies


def _gather_rest_start(shards, fulls, after):
    nr, ns, na = len(REST), 6 * len(REST), len(after)

    def body(*refs):
        for cp in _rest_ici_copies(refs[:nr], refs[nr:2 * nr], refs[2 * nr + na:2 * nr + na + ns]):
            cp.start()
        token = refs[-1]
        token[...] = jnp.zeros_like(token)

    hbm = lambda a: pltpu.HBM(a.shape, a.dtype)
    res = pl.pallas_call(
        body, name="gather_rest_start",
        out_shape=(pltpu.SemaphoreType.DMA(()),) * ns + tuple(hbm(a) for a in shards + fulls)
        + (jax.ShapeDtypeStruct((8, 128), F32),),
        in_specs=(_HBM_SPEC,) * (2 * nr) + (_ANY_SPEC,) * na,
        out_specs=(_SEM_SPEC,) * ns + (_HBM_SPEC,) * (2 * nr) + (pl.BlockSpec(memory_space=pltpu.VMEM),),
        input_output_aliases={k: ns + k for k in range(2 * nr)}, compiler_params=_SIDE_EFFECTS,
    )(*[pltpu.with_memory_space_constraint(a, pltpu.HBM) for a in shards + fulls], *after)
    return res[:ns], res[ns:ns + nr], res[ns + nr:ns + 2 * nr], res[-1]


def _gather_rest_forward(sems, shards, fulls, after):
    nr, ns = len(REST), 6 * len(REST)

    def body(*refs):
        shard_refs, full_refs, old = refs[:nr], refs[nr:2 * nr], refs[2 * nr:2 * nr + ns]
        new = refs[2 * nr + ns + len(after):2 * nr + 2 * ns + len(after)]
        for cp in _rest_ici_copies(shard_refs, full_refs, old):
            cp.wait_send()
            cp.wait_recv()
        for cp in _rest_d2d_copies(full_refs, new):
            cp.start()
        token = refs[-1]
        token[...] = jnp.zeros_like(token)

    res = pl.pallas_call(
        body, name="gather_rest_forward",
        out_shape=(pltpu.SemaphoreType.DMA(()),) * ns + tuple(pltpu.HBM(a.shape, a.dtype) for a in fulls)
        + (jax.ShapeDtypeStruct((8, 128), F32),),
        in_specs=(_HBM_SPEC,) * (2 * nr) + (_SEM_SPEC,) * ns + (_ANY_SPEC,) * len(after),
        out_specs=(_SEM_SPEC,) * ns + (_HBM_SPEC,) * nr + (pl.BlockSpec(memory_space=pltpu.VMEM),),
        input_output_aliases={nr + k: ns + k for k in range(nr)}, compiler_params=_SIDE_EFFECTS,
    )(*shards, *fulls, *sems, *after)
    return res[:ns], res[ns:ns + nr], res[-1]


def _gather_rest_end(sems, fulls, after):
    nr, ns = len(REST), 6 * len(REST)

    def body(*refs):
        for cp in _rest_d2d_copies(refs[:nr], refs[nr:nr + ns]):
            cp.wait_send()
            cp.wait_recv()

    return pl.pallas_call(
        body, name="gather_rest_end",
        out_shape=tuple(pltpu.HBM(a.shape, a.dtype) for a in fulls),
        in_specs=(_HBM_SPEC,) * nr + (_SEM_SPEC,) * ns + (_ANY_SPEC,) * len(after),
        out_specs=(_HBM_SPEC,) * nr,
        input_output_aliases={k: k for k in range(nr)}, compiler_params=_SIDE_EFFECTS,
    )(*fulls, *sems, *after)


def _adam_update(w, g, m, v):
    mn = ADAM_B1 * m + (1.0 - ADAM_B1) * g
    vn = ADAM_B2 * v + (1.0 - ADAM_B2) * (g * g)
    m_hat = mn / (1.0 - ADAM_B1 ** ADAM_STEP)
    v_hat = vn / (1.0 - ADAM_B2 ** ADAM_STEP)
    return -ADAM_LR * (m_hat / (jnp.sqrt(v_hat) + ADAM_EPS) + ADAM_WD * w), mn, vn


def _final_sum(name, pos, axis, psum, recv, shard_shape, after=(), tr=128):
    r, cw = shard_shape
    hr = r // 2
    tr = min(tr, hr)
    nt = hr // tr
    n_after = len(after)

    def kern(pos_ref, p_ref, r_ref, *rest):
        g_ref, send_buf, land_buf, s_sem, r_sem = rest[n_after:]
        p, t = pl.program_id(0), pl.program_id(1)
        sib = _sibling()

        @pl.when(jnp.logical_and(p == 0, t == 0))
        def _():
            _pair_barrier(sib)

        def copy(i):
            return pltpu.make_async_remote_copy(
                src_ref=send_buf.at[i], dst_ref=land_buf.at[i], send_sem=s_sem.at[i],
                recv_sem=r_sem.at[i], device_id=sib, device_id_type=MESH)

        @pl.when(p == 0)
        def _():
            tot = p_ref[...].astype(F32)
            for j in range(3):
                tot = tot + r_ref[j].astype(F32)
            send_buf[t] = tot
            copy(t).start()
            g_ref[...] = tot

        @pl.when(p == 1)
        def _():
            copy(t).wait_recv()
            g_ref[...] = land_buf[t]

        @pl.when(jnp.logical_and(p == 1, t == nt - 1))
        def _():
            for i in range(nt):
                copy(i).wait_send()

    def shard_rows(p, t, pos_ref):
        return (jnp.where(p == 0, pos_ref[0], 1 - pos_ref[0]) * nt + t, 0)

    def own_part(p, t, pos_ref):
        tt = jnp.where(p == 0, t, nt - 1)
        return (tt, pos_ref[1]) if axis == 1 else (pos_ref[1] * nt + tt, 0)

    grid_spec = pltpu.PrefetchScalarGridSpec(
        num_scalar_prefetch=1, grid=(2, nt),
        in_specs=[pl.BlockSpec((tr, cw), own_part),
                  pl.BlockSpec((3, tr, cw), lambda p, t, pos_ref: (0, jnp.where(p == 0, t, nt - 1), 0))]
        + [pl.BlockSpec(memory_space=pl.ANY)] * n_after,
        out_specs=pl.BlockSpec((tr, cw), shard_rows),
        scratch_shapes=[pltpu.VMEM((nt, tr, cw), F32), pltpu.VMEM((nt, tr, cw), F32),
                        pltpu.SemaphoreType.DMA((nt,)), pltpu.SemaphoreType.DMA((nt,))])
    return pl.pallas_call(
        kern, name=name, grid_spec=grid_spec, out_shape=jax.ShapeDtypeStruct((r, cw), F32),
        compiler_params=_cparams(("arbitrary", "arbitrary"), PAIR_COLLECTIVE_ID),
    )(pos, psum, recv, *after)


def _adamw(name, w, g, m, v):
    r, cw = w.shape
    tr = min(r, 128)

    def kern(w_ref, g_ref, m_ref, v_ref, go_ref, d_ref, nm_ref, nv_ref):
        gv = g_ref[...]
        go_ref[...] = gv
        d_ref[...], nm_ref[...], nv_ref[...] = _adam_update(w_ref[...], gv, m_ref[...], v_ref[...])

    spec = pl.BlockSpec((tr, cw), lambda i: (i, 0))
    return pl.pallas_call(
        kern, name=name, grid=(r // tr,), in_specs=[spec] * 4, out_specs=[spec] * 4,
        out_shape=[jax.ShapeDtypeStruct((r, cw), F32)] * 4, compiler_params=_cparams(("parallel",)),
    )(w, g, m, v)


_PACK_W = ADA_COLS
_NB = REL_BUCKETS * N_ATT_HEADS
_SMALL_SLOTS = {
    "b_ada": (0, 0, ADA_COLS),
    "norm1_g": (1, 0, D_MODEL), "norm2_g": (1, D_MODEL, D_MODEL), "norm_f_g": (1, 2 * D_MODEL, D_MODEL),
    "ret_gn_g": (1, 3 * D_MODEL, RET_V_W),
    "ret_gn_b": (2, 0, RET_V_W), "rel_bias": (2, RET_V_W, _NB), "loss": (2, RET_V_W + 512, 128),
}


def _pack_small(vals):
    rows = []
    for r in range(8):
        items = sorted([(off, n) for n, (rr, off, _) in _SMALL_SLOTS.items() if rr == r and n in vals])
        parts, pos = [], 0
        for off, n in items:
            if off > pos:
                parts.append(jnp.zeros((1, off - pos), F32))
            parts.append(vals[n].reshape(1, -1).astype(F32))
            pos = off + _SMALL_SLOTS[n][2]
        if pos < _PACK_W:
            parts.append(jnp.zeros((1, _PACK_W - pos), F32))
        rows.append(jnp.concatenate(parts, axis=-1))
    return jnp.concatenate(rows, axis=0)


def _adamw_small(tot, names, wmv):
    n = len(names)

    def kern(tot_ref, *refs):
        ins, outs = refs[:3 * n], refs[3 * n:]
        for i, name in enumerate(names):
            row, off, width = _SMALL_SLOTS[name]
            g = tot_ref[row:row + 1, off:off + width]
            outs[i][...] = g
            outs[n + i][...], outs[2 * n + i][...], outs[3 * n + i][...] = _adam_update(
                ins[i][...], g, ins[n + i][...], ins[2 * n + i][...])

    vm = pl.BlockSpec(memory_space=pltpu.VMEM)
    shapes = [jax.ShapeDtypeStruct((1, _SMALL_SLOTS[name][2]), F32) for name in names]
    res = pl.pallas_call(
        kern, name="adamw_small", in_specs=[vm] * (1 + 3 * n), out_specs=[vm] * (4 * n),
        out_shape=shapes * 4,
    )(tot, *wmv[0], *wmv[1], *wmv[2])
    return res[:n], res[n:2 * n], res[2 * n:3 * n], res[3 * n:]


def _unpack_small(pack, name):
    r, off, wd = _SMALL_SLOTS[name]
    return pack[r:r + 1, off:off + wd]


def kernel(x, c, w_ada, b_ada, norm1_g, w_in, rel_bias, ret_gn_g, ret_gn_b, w_ret_out, w_att_out, w_o, norm2_g, w_ff1, w_ff2, norm_f_g, loss_target, m_w_ada, m_b_ada, m_norm1_g, m_w_in, m_rel_bias, m_ret_gn_g, m_ret_gn_b, m_w_ret_out, m_w_att_out, m_w_o, m_norm2_g, m_w_ff1, m_w_ff2, m_norm_f_g, v_w_ada, v_b_ada, v_norm1_g, v_w_in, v_rel_bias, v_ret_gn_g, v_ret_gn_b, v_w_ret_out, v_w_att_out, v_w_o, v_norm2_g, v_w_ff1, v_w_ff2, v_norm_f_g):
    given = dict(locals())
    big_names = [n for n, _ in BIG]
    shard_w = {n: given[n][0] for n in big_names}
    assert all(shard_w[n].shape == SHARD[n] for n in big_names)

    shards_bf = [None] + [shard_w[n].astype(BF16) for n in big_names[1:]]
    full = _gather_weights([shard_w["w_in"]] + shards_bf[1:], 1)
    mod, sc_all = _ada_fwd(c, w_ada[0], b_ada)
    rest_gather = _gather_rest_start(shards_bf[1:], list(full[1:]), [mod])
    pos = _where_am_i()

    loss, grad_x, d_mod, small, g_big, pending = _local_step(
        pos, x[0], loss_target[0], mod, norm1_g, norm2_g, norm_f_g.reshape(1, -1), rel_bias, ret_gn_g,
        ret_gn_b, full[0], rest_gather)

    pack_g = _pack_small(dict(b_ada=d_mod, norm1_g=small["norm1_g"], norm2_g=small["norm2_g"],
                              norm_f_g=small["norm_f_g"], ret_gn_g=small["gn_g"], ret_gn_b=small["gn_b"],
                              rel_bias=small["rel_bias"], loss=loss[:, :128]))
    tot, g_w_ada = _small_reduce(pack_g, sc_all, after=list(g_big.values()))

    small_names = ["b_ada", "norm1_g", "rel_bias", "ret_gn_g", "ret_gn_b", "norm2_g", "norm_f_g"]
    small_out = _adamw_small(tot, small_names, [[given[p + n].reshape(1, -1) for n in small_names]
                                                for p in ("", "m_", "v_")])
    grads, deltas, new_m, new_v = ({n: t.reshape(given[n].shape) for n, t in zip(small_names, group)}
                                   for group in small_out)
    sd = deltas["b_ada"]
    g_big["w_ada"] = g_w_ada
    for n in ["w_ada"] + big_names[1:] + big_names[:1]:
        if n == "w_in":
            gw_in, sems, land = pending
            done = [tot, sd] + [deltas[k] for k in ["w_ada"] + big_names[1:]]
            (gw_in,), (got,) = _ici_wait("ici_wait_w_in", [n], sems, [gw_in], [land], done)
            g_big[n] = _final_sum("final_w_in", pos, 1, gw_in, got, SHARD[n])
        g, d, nm, nv = _adamw("adamw_" + n, given[n][0], g_big[n], given["m_" + n][0], given["v_" + n][0])
        grads[n], deltas[n], new_m[n], new_v[n] = g[None], d[None], nm[None], nv[None]

    order = ["w_ada", "b_ada", "norm1_g", "w_in", "rel_bias", "ret_gn_g", "ret_gn_b", "w_ret_out",
             "w_att_out", "w_o", "norm2_g", "w_ff1", "w_ff2", "norm_f_g"]
    loss_out = _unpack_small(tot, "loss")[0, 0]
    return (loss_out, grad_x[None], *[grads[n] for n in order], *[deltas[n] for n in order],
            *[new_m[n] for n in order], *[new_v[n] for n in order])
```

```python
import math

import jax
import jax.numpy as jnp
import numpy as np
from jax import lax
from jax.experimental import pallas as pl
from jax.experimental.pallas import tpu as pltpu

F32 = jnp.float32
BF16 = jnp.bfloat16
I32 = jnp.int32

SEQ = 2048
D_MODEL = 1024
RET_HEADS = 4
RET_DK = 256
RET_DV = 512
RET_CHUNK = 128
RET_SUB = 2
RET_QK_W = RET_HEADS * RET_DK
RET_V_W = RET_HEADS * RET_DV
ATT_GROUPS = ((128, 1), (512, 4), (2048, 16))
ATT_HPG = 4
ATT_DH = 128
ATT_W = ATT_HPG * ATT_DH
ATT_BLK = 128
REL_BUCKETS = 32
REL_MAX_DIST = 2048
N_ATT_HEADS = 12
D_FF = 4 * D_MODEL
RMS_EPS = 1e-6
GN_EPS = 1e-5
ROPE_BASE = 10000.0
IN_COLS = 2 * RET_QK_W + 2 * RET_V_W + 9 * ATT_W + 2 * D_MODEL
OFF_Q, OFF_K, OFF_V, OFF_G = 0, RET_QK_W, 2 * RET_QK_W, 2 * RET_QK_W + RET_V_W
OFF_ATT = 2 * RET_QK_W + 2 * RET_V_W
OFF_GATE = OFF_ATT + 9 * ATT_W
N_CHIPS = 4
N_DEV = 8
ADA_COLS = 6 * D_MODEL

ADAM_LR = 0.001
ADAM_B1 = 0.9
ADAM_B2 = 0.999
ADAM_EPS = 1e-08
ADAM_WD = 0.01
ADAM_STEP = 10

VMEM_LIMIT_V7X = 56 * 1024 * 1024
MESH = pl.DeviceIdType.MESH


def _cparams(sem, collective_id=None):
    return pltpu.CompilerParams(dimension_semantics=sem, vmem_limit_bytes=VMEM_LIMIT_V7X,
                                collective_id=collective_id)


PAIR_COLLECTIVE_ID = 0


def _pair_barrier(sib):
    barrier = pltpu.get_barrier_semaphore()
    pl.semaphore_signal(barrier, inc=1, device_id=sib, device_id_type=MESH)
    pl.semaphore_wait(barrier, 1)


def _sigmoid(v):
    return 1.0 / (1.0 + jnp.exp(-v))


TM, TN = 1024, 1024


def _piece_chunks(piece, width):
    arr, stacked = piece
    return arr.shape[0] if stacked else arr.shape[1] // width


def _piece_spec(piece, rows, width, start, row_of, chunk_of):
    arr, stacked = piece
    last = _piece_chunks(piece, width) - 1

    def local(*ids):
        return jnp.clip(chunk_of(*ids) - start, 0, last)

    def row(*ids):
        rel = chunk_of(*ids) - start
        return jnp.where(jnp.logical_and(rel >= 0, rel <= last), row_of(*ids), 0)

    if stacked:
        return pl.BlockSpec((None, rows, width), lambda *ids: (local(*ids), row(*ids), 0))
    return pl.BlockSpec((rows, width), lambda *ids: (row(*ids), local(*ids)))


def _piece_starts(pieces, width):
    return [sum(_piece_chunks(p, width) for p in pieces[:q]) for q in range(len(pieces))]


def _matmul(name, a, b, kind, m, n, k, outs, *, b_off=0, tm=TM, tn=TN, tk=1024,
            epilogue=None, extras=(), after=(), n_sums=0, b_buffers=None):
    tm, tn, tk = min(tm, m), min(tn, n), min(tk, k)
    nk = k // tk
    pieces = a if isinstance(a, list) else [(a, False)]
    starts = _piece_starts(pieces, tk)
    if kind == "nn":
        a_specs = [pl.BlockSpec((tm, tk), lambda i, j, kk: (i, kk))]
        b_spec = pl.BlockSpec((tk, tn), lambda i, j, kk: (kk, b_off // tn + j))
        dn = (((1,), (0,)), ((), ()))
    elif kind == "nt":
        a_specs = [_piece_spec(p, tm, tk, st, lambda i, j, kk: i, lambda i, j, kk: kk)
                   for p, st in zip(pieces, starts)]
        b_spec = pl.BlockSpec((tn, tk), lambda i, j, kk: (j, b_off // tk + kk))
        dn = (((1,), (1,)), ((), ()))
    else:
        a_specs = [pl.BlockSpec((tk, tm), lambda i, j, kk: (kk, i))]
        b_spec = pl.BlockSpec((tk, tn), lambda i, j, kk: (kk, j))
        dn = (((0,), (0,)), ((), ()))
    nb, nj = b_buffers, n // tn
    n_steps = (m // tm) * nj * nk
    if nb is not None:
        assert kind in ("nn", "nt") and len(pieces) == 1
        b_block = b_spec.block_shape
        b_spec = pl.BlockSpec(memory_space=pl.ANY)

    def b_copy(b_hbm, buf, sem, t):
        kk_t, j_t = t % nk, (t // nk) % nj
        if kind == "nn":
            src = b_hbm.at[pl.ds(pl.multiple_of(kk_t * tk, tk), tk),
                           pl.ds(pl.multiple_of(b_off + j_t * tn, tn), tn)]
        else:
            src = b_hbm.at[pl.ds(pl.multiple_of(j_t * tn, tn), tn),
                           pl.ds(pl.multiple_of(b_off + kk_t * tk, tk), tk)]
        return pltpu.make_async_copy(src, buf.at[t % nb], sem.at[t % nb])

    n_a, n_ex, n_out = len(pieces), len(extras), len(outs)
    if epilogue is None:
        epilogue = lambda acc: (acc,)

    assert n_sums == 0 or tn == n

    def finish(acc, ex_refs, out_refs, first_rows):
        res = epilogue(acc, *[r[...] for r in ex_refs])
        for r, v in zip(out_refs[:n_out], res[:n_out]):
            r[...] = v.astype(r.dtype)
        for r, v in zip(out_refs[n_out:], res[n_out:]):
            @pl.when(first_rows)
            def _(r=r, v=v):
                r[...] = v

            @pl.when(jnp.logical_not(first_rows))
            def _(r=r, v=v):
                r[...] += v

    n_in = n_a + 1 + n_ex + len(after)

    def kern(*refs):
        a_refs, b_ref = refs[:n_a], refs[n_a]
        ex_refs = refs[n_a + 1:n_a + 1 + n_ex]
        out_refs = refs[n_in:n_in + n_out + n_sums]
        first_rows, kk = pl.program_id(0) == 0, pl.program_id(2)
        if nb is not None:
            b_buf, b_sem = refs[-2:]
            t = (pl.program_id(0) * nj + pl.program_id(1)) * nk + kk

            @pl.when(t == 0)
            def _():
                for s in range(min(nb - 1, n_steps)):
                    b_copy(b_ref, b_buf, b_sem, s).start()

            @pl.when(t + nb - 1 < n_steps)
            def _():
                b_copy(b_ref, b_buf, b_sem, t + nb - 1).start()

            b_copy(b_ref, b_buf, b_sem, t).wait()
            b_tile = b_buf.at[t % nb]
        else:
            b_tile = b_ref
        dot = lambda a_ref: lax.dot_general(a_ref[...], b_tile[...], dn, preferred_element_type=F32)
        if nk == 1:
            finish(dot(a_refs[0]), ex_refs, out_refs, first_rows)
            return
        acc_ref = refs[n_in + n_out + n_sums]
        if n_a == 1:
            part = dot(a_refs[0])

            @pl.when(kk == 0)
            def _():
                acc_ref[...] = part

            @pl.when(kk > 0)
            def _():
                acc_ref[...] += part
        else:
            @pl.when(kk == 0)
            def _():
                acc_ref[...] = jnp.zeros_like(acc_ref)

            for q in range(n_a):
                @pl.when(jnp.logical_and(kk >= starts[q], kk < starts[q] + _piece_chunks(pieces[q], tk)))
                def _(q=q):
                    acc_ref[...] += dot(a_refs[q])

        @pl.when(kk == nk - 1)
        def _():
            finish(acc_ref[...], ex_refs, out_refs, first_rows)

    in_specs = a_specs + [b_spec] + [pl.BlockSpec(bs, im) for _, bs, im in extras]
    in_specs += [pl.BlockSpec(memory_space=pl.ANY)] * len(after)
    sem = ("arbitrary",) * 3 if n_sums or nb is not None else ("parallel", "parallel", "arbitrary")
    ring = [] if nb is None else [pltpu.VMEM((nb,) + tuple(b_block), b.dtype), pltpu.SemaphoreType.DMA((nb,))]
    return pl.pallas_call(
        kern, name=name, grid=(m // tm, n // tn, nk), in_specs=in_specs,
        out_specs=[pl.BlockSpec((tm, tn), lambda i, j, kk: (i, j)) for _ in outs]
        + [pl.BlockSpec((1, tn), lambda i, j, kk: (0, 0))] * n_sums,
        out_shape=[jax.ShapeDtypeStruct((m, n), dt) for dt in outs]
        + [jax.ShapeDtypeStruct((1, n), F32)] * n_sums,
        scratch_shapes=([] if nk == 1 else [pltpu.VMEM((tm, tn), F32)]) + ring,
        compiler_params=_cparams(sem),
    )(*[p[0] for p in pieces], b, *[e[0] for e in extras], *after)


def _ici_copies(psum_ref, recv_ref, s_sem, r_sem, axis, shard_shape):
    x, y, c = _me()
    hr, cw = shard_shape[0] // 2, shard_shape[1]
    pick = lambda sems, j: sems[j] if isinstance(sems, (list, tuple)) else sems.at[j]
    copies = []
    for j, (fx, fy) in enumerate(_CHIP_FLIPS):
        chip = 2 * (x ^ fx) + (y ^ fy)
        src = psum_ref.at[:, pl.ds(chip * cw, cw)] if axis == 1 else psum_ref.at[pl.ds(chip * hr, hr), :]
        copies.append(pltpu.make_async_remote_copy(
            src_ref=src, dst_ref=recv_ref.at[j], send_sem=pick(s_sem, j), recv_sem=pick(r_sem, j),
            device_id=(x ^ fx, y ^ fy, c), device_id_type=MESH))
    return copies


_HBM_SPEC = pl.BlockSpec(memory_space=pltpu.HBM)
_SEM_SPEC = pl.BlockSpec(memory_space=pltpu.SEMAPHORE)


def _split_ici_copies(names, p_refs, land_refs, sems):
    copies = []
    for i, n in enumerate(names):
        copies += _ici_copies(p_refs[i], land_refs[i], list(sems[6 * i:6 * i + 3]),
                              list(sems[6 * i + 3:6 * i + 6]), dict(BIG)[n], SHARD[n])
    return copies


def _ici_start(name, names, psums):
    nw, ns = len(names), 6 * len(names)
    lands = [lax.empty((3, SHARD[n][0] // 2, SHARD[n][1]), BF16) for n in names]

    def body(*refs):
        for cp in _split_ici_copies(names, refs[:nw], refs[nw:2 * nw], refs[2 * nw:2 * nw + ns]):
            cp.start()
        token = refs[-1]
        token[...] = jnp.zeros_like(token)

    res = pl.pallas_call(
        body, name=name,
        out_shape=(pltpu.SemaphoreType.DMA(()),) * ns
        + tuple(pltpu.HBM(a.shape, BF16) for a in list(psums) + lands)
        + (jax.ShapeDtypeStruct((8, 128), F32),),
        in_specs=(_HBM_SPEC,) * (2 * nw),
        out_specs=(_SEM_SPEC,) * ns + (_HBM_SPEC,) * (2 * nw) + (pl.BlockSpec(memory_space=pltpu.VMEM),),
        input_output_aliases={k: ns + k for k in range(2 * nw)},
        compiler_params=pltpu.CompilerParams(has_side_effects=pltpu.SideEffectType.DATAFLOW_SIDE_EFFECTING),
    )(*[pltpu.with_memory_space_constraint(a, pltpu.HBM) for a in list(psums) + lands])
    return res[:ns], res[ns:ns + nw], res[ns + nw:ns + 2 * nw], res[-1]


def _ici_wait(name, names, sems, p_thru, land_thru, after):
    nw, ns = len(names), 6 * len(names)

    def body(*refs):
        for cp in _split_ici_copies(names, refs[:nw], refs[nw:2 * nw], refs[2 * nw:2 * nw + ns]):
            cp.wait_send()
            cp.wait_recv()

    res = pl.pallas_call(
        body, name=name,
        out_shape=tuple(pltpu.HBM(a.shape, BF16) for a in list(p_thru) + list(land_thru)),
        in_specs=(_HBM_SPEC,) * (2 * nw) + (_SEM_SPEC,) * ns + (pl.BlockSpec(memory_space=pl.ANY),) * len(after),
        out_specs=(_HBM_SPEC,) * (2 * nw), input_output_aliases={k: k for k in range(2 * nw)},
        compiler_params=pltpu.CompilerParams(has_side_effects=pltpu.SideEffectType.DATAFLOW_SIDE_EFFECTING),
    )(*p_thru, *land_thru, *sems, *after)
    return res[:nw], res[nw:]


def _where_am_i():
    x, y, c = _me()
    return jnp.stack([c, 2 * x + y]).astype(I32)


def _sibling():
    x, y, c = _me()
    return (x, y, 1 - c)


N_SEND_SLOTS = 2


def _matmul_tn_pair(name, pos, a, b, m, n, k, shard_rows, *, tm, tn, tk):
    hr = shard_rows // 2
    tm, tn, tk = min(tm, hr), min(tn, n), min(tk, k)
    tph = hr // tm
    nt, nj, nk = (m // 2) // tm, n // tn, k // tk
    n_tiles = nt * nj

    def row_block(p, t, pos_ref):
        half = jnp.where(p == 0, 1 - pos_ref[0], pos_ref[0])
        return (t // tph) * (2 * tph) + half * tph + t % tph

    pieces = b if isinstance(b, list) else [(b, False)]
    starts = _piece_starts(pieces, tn)
    n_b = len(pieces)

    def kern(pos_ref, a_ref, *rest):
        b_refs = rest[:n_b]
        o_ref, acc_ref, send_buf, land_buf, s_sem, r_sem = rest[n_b:]
        p, t, j, kk = pl.program_id(0), pl.program_id(1), pl.program_id(2), pl.program_id(3)
        idx = t * nj + j
        sib = _sibling()

        @pl.when(jnp.logical_and(jnp.logical_and(p == 0, idx == 0), kk == 0))
        def _():
            _pair_barrier(sib)

        def copy(i):
            return pltpu.make_async_remote_copy(
                src_ref=send_buf.at[i % N_SEND_SLOTS], dst_ref=land_buf.at[i], send_sem=s_sem.at[i],
                recv_sem=r_sem.at[i], device_id=sib, device_id_type=MESH)

        @pl.when(kk == 0)
        def _():
            acc_ref[...] = jnp.zeros_like(acc_ref)

        for q in range(n_b):
            @pl.when(jnp.logical_and(j >= starts[q], j < starts[q] + _piece_chunks(pieces[q], tn)))
            def _(q=q):
                acc_ref[...] += lax.dot_general(a_ref[...], b_refs[q][...], _TN, preferred_element_type=F32)

        @pl.when(jnp.logical_and(kk == nk - 1, p == 0))
        def _():
            @pl.when(idx >= N_SEND_SLOTS)
            def _():
                copy(idx - N_SEND_SLOTS).wait_send()

            send_buf[idx % N_SEND_SLOTS] = acc_ref[...].astype(BF16)
            copy(idx).start()

        @pl.when(jnp.logical_and(kk == nk - 1, p == 1))
        def _():
            copy(idx).wait_recv()
            o_ref[...] = (acc_ref[...] + land_buf[idx].astype(F32)).astype(BF16)

        @pl.when(jnp.logical_and(jnp.logical_and(p == 1, idx == n_tiles - 1), kk == nk - 1))
        def _():
            for i in range(max(n_tiles - N_SEND_SLOTS, 0), n_tiles):
                copy(i).wait_send()

    grid_spec = pltpu.PrefetchScalarGridSpec(
        num_scalar_prefetch=1, grid=(2, nt, nj, nk),
        in_specs=[pl.BlockSpec((tk, tm), lambda p, t, j, kk, pos_ref: (kk, row_block(p, t, pos_ref)))]
        + [_piece_spec(pc, tk, tn, st, lambda p, t, j, kk, pos_ref: kk, lambda p, t, j, kk, pos_ref: j)
           for pc, st in zip(pieces, starts)],
        out_specs=pl.BlockSpec((tm, tn), lambda p, t, j, kk, pos_ref: (p * t, p * j)),
        scratch_shapes=[pltpu.VMEM((tm, tn), F32), pltpu.VMEM((N_SEND_SLOTS, tm, tn), BF16),
                        pltpu.VMEM((n_tiles, tm, tn), BF16),
                        pltpu.SemaphoreType.DMA((n_tiles,)), pltpu.SemaphoreType.DMA((n_tiles,))])
    return pl.pallas_call(
        kern, name=name, grid_spec=grid_spec, out_shape=jax.ShapeDtypeStruct((m // 2, n), BF16),
        compiler_params=_cparams(("arbitrary",) * 4, PAIR_COLLECTIVE_ID),
    )(pos, a, *[pc[0] for pc in pieces])


def _rope_tables():
    half = RET_DK // 2
    f32 = np.float32
    inv = np.power(f32(ROPE_BASE), -np.arange(half, dtype=f32) / f32(half)).astype(f32)
    ang = (np.arange(SEQ, dtype=f32)[:, None] * inv[None, :]).astype(f32)
    return jnp.asarray(np.cos(ang).astype(f32)), jnp.asarray(np.sin(ang).astype(f32))


def _decay_tables():
    c = RET_CHUNK
    f32 = np.float32
    log_g = np.log1p(-np.power(f32(2.0), f32(-5.0) - np.arange(RET_HEADS, dtype=f32))).astype(f32)
    idx = np.arange(c, dtype=f32)
    rel = idx[:, None] - idx[None, :]
    din = np.where(rel >= 0, np.exp(log_g[:, None, None] * np.maximum(rel, f32(0.0))), f32(0.0)).astype(f32)
    qd = np.exp(log_g[:, None] * (idx + f32(1.0))).astype(f32)[:, :, None]
    kd = np.exp(log_g[:, None] * (f32(c) - f32(1.0) - idx)).astype(f32)[:, :, None]
    cd = np.exp(log_g * f32(c)).astype(f32)
    return jnp.asarray(din), jnp.asarray(qd), jnp.asarray(kd), jnp.asarray(cd)


def _t5_bucket(dist):
    max_exact = REL_BUCKETS // 2
    d_f = jnp.maximum(dist, 1).astype(F32)
    large = max_exact + (jnp.log(d_f / max_exact) / math.log(REL_MAX_DIST / max_exact)
                         * (REL_BUCKETS - max_exact)).astype(I32)
    large = jnp.minimum(large, REL_BUCKETS - 1)
    return jnp.where(dist < max_exact, dist, large)


def _bucket_tables():
    qi = jnp.arange(ATT_BLK)[:, None]
    kj = jnp.arange(2 * ATT_BLK)[None, :]
    dist = jnp.clip(ATT_BLK + qi - kj, 0, ATT_BLK)
    return jnp.stack([_t5_bucket(dist * dil) for _, dil in ATT_GROUPS]).astype(I32)


def _retention_fwd(rqk, rv, rg, gn_g, gn_b, din, qd, kd, cd):
    nc = SEQ // RET_CHUNK
    c, dk, dv = RET_CHUNK, RET_DK, RET_DV

    def kern(q_ref, k_ref, v_ref, rg_ref, g_ref, b_ref, din_ref, qd_ref, kd_ref, cd_ref,
             o_ref, st_ref, gated_ref, state):
        n = pl.program_id(0)

        @pl.when(n == 0)
        def _():
            state[...] = jnp.zeros_like(state)

        for sub in range(RET_SUB):
            rows = slice(sub * c, (sub + 1) * c)
            for h in range(RET_HEADS):
                q, k = q_ref[rows, h * dk:(h + 1) * dk], k_ref[rows, h * dk:(h + 1) * dk]
                v = v_ref[rows, h * dv:(h + 1) * dv]
                s_b = state[h].astype(BF16)
                st_ref[h, sub] = s_b
                a = lax.dot_general(q, k, _NT, preferred_element_type=F32) * din_ref[h]
                o = jnp.dot(a.astype(BF16), v, preferred_element_type=F32)
                o += jnp.dot(q, s_b, preferred_element_type=F32) * qd_ref[h]
                v_cols = slice(h * dv, (h + 1) * dv)
                o_ref[rows, v_cols] = o
                nrm, _ = _gn_parts(o)
                gate = rg_ref[rows, v_cols].astype(F32)
                gated_ref[rows, v_cols] = ((gate * _sigmoid(gate))
                                           * (nrm * g_ref[:, v_cols] + b_ref[:, v_cols])).astype(BF16)
                kk = (k.astype(F32) * kd_ref[h]).astype(BF16)
                state[h] = state[h] * cd_ref[h] + lax.dot_general(kk, v, _TN, preferred_element_type=F32)

    whole = lambda a: pl.BlockSpec(a.shape, lambda n: (0,) * a.ndim)
    cs = RET_SUB * c
    rows_v = pl.BlockSpec((cs, RET_V_W), lambda n: (n, 0))
    return pl.pallas_call(
        kern, name="retention_fwd", grid=(nc // RET_SUB,),
        in_specs=[
            pl.BlockSpec((cs, RET_QK_W), lambda n: (n, 0)),
            pl.BlockSpec((cs, RET_QK_W), lambda n: (n, 1)),
            rows_v, rows_v, whole(gn_g), whole(gn_b),
            whole(din), whole(qd), whole(kd),
            pl.BlockSpec(memory_space=pltpu.SMEM),
        ],
        out_specs=[
            rows_v,
            pl.BlockSpec((RET_HEADS, RET_SUB, dk, dv), lambda n: (0, n, 0, 0)),
            rows_v,
        ],
        out_shape=[
            jax.ShapeDtypeStruct((SEQ, RET_V_W), F32),
            jax.ShapeDtypeStruct((RET_HEADS, nc, dk, dv), BF16),
            jax.ShapeDtypeStruct((SEQ, RET_V_W), BF16),
        ],
        scratch_shapes=[pltpu.VMEM((RET_HEADS, dk, dv), F32)],
        compiler_params=_cparams(("arbitrary",)),
    )(rqk, rqk, rv, rg, gn_g, gn_b, din, qd, kd, cd)


def _retention_bwd(rqk, rv, states, d_gated, ro, rg, gn_g, gn_b, din, qd, kd, cd, cos, sin):
    nc = SEQ // RET_CHUNK
    c, dk, dv = RET_CHUNK, RET_DK, RET_DV
    half = dk // 2
    last = nc // RET_SUB - 1

    def unrot(g, cs, sn):
        g1, g2 = g[:, :half], g[:, half:]
        return jnp.concatenate([g1 * cs + g2 * sn, g2 * cs - g1 * sn], axis=-1)

    def kern(q_ref, k_ref, v_ref, st_ref, dg_ref, ro_ref, rg_ref, g_ref, b_ref, din_ref, qd_ref, kd_ref,
             cd_ref, cos_ref, sin_ref, out_ref, drg_ref, dgn_g_ref, dgn_b_ref, dstate):
        step = pl.program_id(0)

        @pl.when(step == 0)
        def _():
            dstate[...] = jnp.zeros_like(dstate)
            dgn_g_ref[...] = jnp.zeros_like(dgn_g_ref)
            dgn_b_ref[...] = jnp.zeros_like(dgn_b_ref)

        for sub in reversed(range(RET_SUB)):
            rows = slice(sub * c, (sub + 1) * c)
            cs, sn = cos_ref[rows, :], sin_ref[rows, :]
            for h in range(RET_HEADS):
                qk_cols, v_cols = slice(h * dk, (h + 1) * dk), slice(h * dv, (h + 1) * dv)
                q, k, v = q_ref[rows, qk_cols], k_ref[rows, qk_cols], v_ref[rows, v_cols]
                s_b = st_ref[h, sub]
                nrm, rstd = _gn_parts(ro_ref[rows, v_cols])
                gate, dg = rg_ref[rows, v_cols].astype(F32), dg_ref[rows, v_cols].astype(F32)
                sg = _sigmoid(gate)
                gn_gain = g_ref[:, v_cols]
                drg_ref[rows, v_cols] = (dg * (nrm * gn_gain + b_ref[:, v_cols])
                                         * (sg * (1.0 + gate * (1.0 - sg)))).astype(BF16)
                d_ron = dg * (gate * sg)
                dgn_g_ref[:, v_cols] += jnp.sum(d_ron * nrm, axis=0, keepdims=True)
                dgn_b_ref[:, v_cols] += jnp.sum(d_ron, axis=0, keepdims=True)
                d_n = d_ron * gn_gain
                d_o = rstd * (d_n - jnp.mean(d_n, axis=-1, keepdims=True)
                              - nrm * jnp.mean(d_n * nrm, axis=-1, keepdims=True))
                d_ob = d_o.astype(BF16)
                d_oq = (d_o * qd_ref[h]).astype(BF16)
                ds_b = dstate[h].astype(BF16)
                din_m = din_ref[h]
                a_b = (lax.dot_general(q, k, _NT, preferred_element_type=F32) * din_m).astype(BF16)
                kk = (k.astype(F32) * kd_ref[h]).astype(BF16)
                d_v = lax.dot_general(a_b, d_ob, _TN, preferred_element_type=F32)
                d_v += jnp.dot(kk, ds_b, preferred_element_type=F32)
                d_a = (lax.dot_general(d_ob, v, _NT, preferred_element_type=F32) * din_m).astype(BF16)
                d_q = jnp.dot(d_a, k, preferred_element_type=F32)
                d_q += lax.dot_general(d_oq, s_b, _NT, preferred_element_type=F32)
                d_k = lax.dot_general(d_a, q, _TN, preferred_element_type=F32)
                d_k += lax.dot_general(v, ds_b, _NT, preferred_element_type=F32) * kd_ref[h]
                dstate[h] = dstate[h] * cd_ref[h] + lax.dot_general(q, d_oq, _TN,
                                                                    preferred_element_type=F32)
                out_ref[rows, h * dk:(h + 1) * dk] = unrot(d_q, cs, sn).astype(BF16)
                out_ref[rows, RET_QK_W + h * dk:RET_QK_W + (h + 1) * dk] = (
                    unrot(d_k, cs, sn) * (RET_DK ** -0.5)).astype(BF16)
                out_ref[rows, 2 * RET_QK_W + h * dv:2 * RET_QK_W + (h + 1) * dv] = d_v.astype(BF16)

    whole = lambda a: pl.BlockSpec(a.shape, lambda n: (0,) * a.ndim)
    rs = RET_SUB * c
    rows_v = pl.BlockSpec((rs, RET_V_W), lambda n: (last - n, 0))
    return pl.pallas_call(
        kern, name="retention_bwd", grid=(nc // RET_SUB,),
        in_specs=[
            pl.BlockSpec((rs, RET_QK_W), lambda n: (last - n, 0)),
            pl.BlockSpec((rs, RET_QK_W), lambda n: (last - n, 1)),
            rows_v,
            pl.BlockSpec((RET_HEADS, RET_SUB, dk, dv), lambda n: (0, last - n, 0, 0)),
            rows_v, rows_v, rows_v, whole(gn_g), whole(gn_b),
            whole(din), whole(qd), whole(kd),
            pl.BlockSpec(memory_space=pltpu.SMEM),
            pl.BlockSpec((rs, half), lambda n: (last - n, 0)),
            pl.BlockSpec((rs, half), lambda n: (last - n, 0)),
        ],
        out_specs=[pl.BlockSpec((rs, 2 * RET_QK_W + RET_V_W), lambda n: (last - n, 0)), rows_v,
                   whole(gn_g), whole(gn_b)],
        out_shape=[jax.ShapeDtypeStruct((SEQ, 2 * RET_QK_W + RET_V_W), BF16),
                   jax.ShapeDtypeStruct((SEQ, RET_V_W), BF16),
                   jax.ShapeDtypeStruct((1, RET_V_W), F32), jax.ShapeDtypeStruct((1, RET_V_W), F32)],
        scratch_shapes=[pltpu.VMEM((RET_HEADS, dk, dv), F32)],
        compiler_params=_cparams(("arbitrary",)),
    )(rqk, rqk, rv, states, d_gated, ro, rg, gn_g, gn_b, din, qd, kd, cd, cos, sin)


def _bias_build(rel_bias, buckets):
    ng = len(ATT_GROUPS)

    def kern(tab_ref, bkt_ref, o_ref):
        g, h = pl.program_id(0), pl.program_id(1)
        bkt = bkt_ref[...]
        acc = jnp.zeros(bkt.shape, F32)
        for b in range(REL_BUCKETS):
            acc = jnp.where(bkt == b, tab_ref[b, g * ATT_HPG + h], acc)
        o_ref[...] = acc

    return pl.pallas_call(
        kern, name="bias_build", grid=(ng, ATT_HPG),
        in_specs=[pl.BlockSpec(memory_space=pltpu.SMEM),
                  pl.BlockSpec((None, ATT_BLK, 2 * ATT_BLK), lambda g, h: (g, 0, 0))],
        out_specs=pl.BlockSpec((None, None, ATT_BLK, 2 * ATT_BLK), lambda g, h: (g, h, 0, 0)),
        out_shape=jax.ShapeDtypeStruct((ng, ATT_HPG, ATT_BLK, 2 * ATT_BLK), F32),
        compiler_params=_cparams(("arbitrary", "arbitrary")),
    )(rel_bias, buckets)


def _bias_grad(dsb, buckets):
    ng = len(ATT_GROUPS)

    def kern(ds_ref, bkt_ref, o_ref):
        g, h = pl.program_id(0), pl.program_id(1)
        bkt, ds = bkt_ref[...], ds_ref[...]
        for b in range(REL_BUCKETS):
            o_ref[b, g * ATT_HPG + h] = jnp.sum(jnp.where(bkt == b, ds, 0.0))

    return pl.pallas_call(
        kern, name="bias_grad", grid=(ng, ATT_HPG),
        in_specs=[pl.BlockSpec((None, None, ATT_BLK, 2 * ATT_BLK), lambda g, h: (g, h, 0, 0)),
                  pl.BlockSpec((None, ATT_BLK, 2 * ATT_BLK), lambda g, h: (g, 0, 0))],
        out_specs=pl.BlockSpec(memory_space=pltpu.SMEM),
        out_shape=jax.ShapeDtypeStruct((REL_BUCKETS, N_ATT_HEADS), F32),
        compiler_params=_cparams(("arbitrary", "arbitrary")),
    )(dsb, buckets)


_NT = (((1,), (1,)), ((), ()))
_TN = (((0,), (0,)), ((), ()))
_ATT_SCALE = ATT_DH ** -0.5


def _window_mask(has_prev):
    qi = lax.broadcasted_iota(I32, (ATT_BLK, 2 * ATT_BLK), 0)
    kj = lax.broadcasted_iota(I32, (ATT_BLK, 2 * ATT_BLK), 1)
    prev_ok = jnp.logical_and(jnp.logical_and(kj < ATT_BLK, kj >= qi), has_prev)
    return jnp.logical_or(prev_ok, jnp.logical_and(kj >= ATT_BLK, qi >= kj - ATT_BLK))


def _head_specs(col0):
    return pl.BlockSpec((SEQ, ATT_DH), lambda h: (0, col0 + h))


def _sub_rows(start, size, dil):
    return pl.ds(start, size) if dil == 1 else pl.ds(start, size, stride=dil)


def _att_blocks(dil):
    nb = SEQ // dil // ATT_BLK
    return [(r + dil * n * ATT_BLK, n > 0, n + 1 < nb) for r in range(dil) for n in range(nb)]


def _att_fwd(gi, dil, qkv, bias, after=()):
    blk, dh = ATT_BLK, ATT_DH
    pad = dil * blk
    col0 = 3 * ATT_HPG * gi

    def kern(q_ref, k_ref, v_ref, b_ref, *rest):
        o_ref, l_ref, qf, kpad, vpad = rest[len(after):]
        zero = jnp.zeros((pad, dh), F32)
        kpad[0:pad, :] = zero
        vpad[0:pad, :] = zero
        kpad[pad:, :] = k_ref[...].astype(F32)
        vpad[pad:, :] = v_ref[...].astype(F32)
        qf[...] = q_ref[...].astype(F32)
        bias_m = b_ref[...]
        for start, has_prev, _ in _att_blocks(dil):
            rows, window = _sub_rows(start, blk, dil), _sub_rows(start, 2 * blk, dil)
            q = qf[rows, :].astype(BF16)
            kw, vw = kpad[window, :].astype(BF16), vpad[window, :].astype(BF16)
            valid = _window_mask(has_prev)
            s = lax.dot_general(q, kw, _NT, preferred_element_type=F32) * _ATT_SCALE + bias_m
            s = jnp.where(valid, s, -1e30)
            mx = jnp.max(s, axis=-1, keepdims=True)
            e = jnp.exp(s - mx)
            den = jnp.sum(e, axis=-1, keepdims=True)
            o_ref[rows, :] = jnp.dot((e / den).astype(BF16), vw, preferred_element_type=F32)
            l_ref[rows, :] = jnp.broadcast_to(mx + jnp.log(den), (blk, dh))

    return pl.pallas_call(
        kern, name=f"att_fwd_g{gi}", grid=(ATT_HPG,),
        in_specs=[_head_specs(col0), _head_specs(col0 + ATT_HPG), _head_specs(col0 + 2 * ATT_HPG),
                  pl.BlockSpec((None, None, blk, 2 * blk), lambda h: (gi, h, 0, 0))]
        + [pl.BlockSpec(memory_space=pl.ANY)] * len(after),
        out_specs=[_head_specs(0), _head_specs(0)],
        out_shape=[jax.ShapeDtypeStruct((SEQ, ATT_W), F32), jax.ShapeDtypeStruct((SEQ, ATT_W), F32)],
        scratch_shapes=[pltpu.VMEM((SEQ, dh), F32), pltpu.VMEM((SEQ + pad, dh), F32),
                        pltpu.VMEM((SEQ + pad, dh), F32)],
        compiler_params=_cparams(("arbitrary",)),
    )(qkv, qkv, qkv, bias, *after)


def _att_bwd(gi, dil, qkv, d_att, lse, dd, bias):
    blk, dh = ATT_BLK, ATT_DH
    pad = dil * blk
    col0 = 3 * ATT_HPG * gi

    def kern(q_ref, k_ref, v_ref, do_ref, l_ref, d_ref, b_ref, dqkv_ref, dsb_ref,
             qf, kpad, vpad, dq_s, dkpad, dvpad):
        zero = jnp.zeros((pad, dh), F32)
        kpad[0:pad, :] = zero
        vpad[0:pad, :] = zero
        kpad[pad:, :] = k_ref[...].astype(F32)
        vpad[pad:, :] = v_ref[...].astype(F32)
        qf[...] = q_ref[...].astype(F32)
        dkpad[...] = jnp.zeros_like(dkpad)
        dvpad[...] = jnp.zeros_like(dvpad)
        bias_m = b_ref[...]
        ds_sum = jnp.zeros((blk, 2 * blk), F32)

        for start, has_prev, _ in _att_blocks(dil):
            rows, window = _sub_rows(start, blk, dil), _sub_rows(start, 2 * blk, dil)
            q, d_o = qf[rows, :].astype(BF16), do_ref[rows, :].astype(BF16)
            kw, vw = kpad[window, :].astype(BF16), vpad[window, :].astype(BF16)
            lrow, drow = l_ref[rows, :][:, :1], d_ref[rows, :][:, :1]
            valid = _window_mask(has_prev)
            s = lax.dot_general(q, kw, _NT, preferred_element_type=F32) * _ATT_SCALE + bias_m
            p = jnp.where(valid, jnp.exp(jnp.where(valid, s, -1e30) - lrow), 0.0)
            dp = lax.dot_general(d_o, vw, _NT, preferred_element_type=F32)
            ds = p * (dp - drow)
            ds_b = ds.astype(BF16)
            dq_s[rows, :] = jnp.dot(ds_b, kw, preferred_element_type=F32) * _ATT_SCALE
            dkpad[window, :] += lax.dot_general(ds_b, q, _TN, preferred_element_type=F32) * _ATT_SCALE
            dvpad[window, :] += lax.dot_general(p.astype(BF16), d_o, _TN, preferred_element_type=F32)
            ds_sum = ds_sum + ds
        dsb_ref[...] = ds_sum

        dqkv_ref[0] = dq_s[...].astype(BF16)
        dqkv_ref[1] = dkpad[pad:, :].astype(BF16)
        dqkv_ref[2] = dvpad[pad:, :].astype(BF16)

    return pl.pallas_call(
        kern, name=f"att_bwd_g{gi}", grid=(ATT_HPG,),
        in_specs=[_head_specs(col0), _head_specs(col0 + ATT_HPG), _head_specs(col0 + 2 * ATT_HPG),
                  _head_specs(0), _head_specs(0), _head_specs(0),
                  pl.BlockSpec((None, None, blk, 2 * blk), lambda h: (gi, h, 0, 0))],
        out_specs=[pl.BlockSpec((3, SEQ, dh), lambda h: (0, 0, h)),
                   pl.BlockSpec((None, blk, 2 * blk), lambda h: (h, 0, 0))],
        out_shape=[jax.ShapeDtypeStruct((3, SEQ, ATT_W), BF16),
                   jax.ShapeDtypeStruct((ATT_HPG, blk, 2 * blk), F32)],
        scratch_shapes=[pltpu.VMEM((SEQ, dh), F32), pltpu.VMEM((SEQ + pad, dh), F32),
                        pltpu.VMEM((SEQ + pad, dh), F32), pltpu.VMEM((SEQ, dh), F32),
                        pltpu.VMEM((SEQ + pad, dh), F32), pltpu.VMEM((SEQ + pad, dh), F32)],
        compiler_params=_cparams(("arbitrary",)),
    )(qkv, qkv, qkv, d_att, lse, dd, bias)


def _rms_parts(x):
    r = lax.rsqrt(jnp.mean(x * x, axis=-1, keepdims=True) + RMS_EPS)
    return x * r, r


def _rms_bwd(d_xhat, xhat, r):
    return r * (d_xhat - xhat * jnp.mean(d_xhat * xhat, axis=-1, keepdims=True))


def _prenorm_fwd(name, x, gain, shift, scale, tm=256):
    def kern(x_ref, g_ref, sh_ref, sc_ref, o_ref):
        xhat, _ = _rms_parts(x_ref[...])
        o_ref[...] = ((xhat * g_ref[...]) * (1.0 + sc_ref[...]) + sh_ref[...]).astype(BF16)

    rows = pl.BlockSpec((tm, D_MODEL), lambda i: (i, 0))
    vec = pl.BlockSpec((1, D_MODEL), lambda i: (0, 0))
    return pl.pallas_call(
        kern, name=name, grid=(x.shape[0] // tm,), in_specs=[rows, vec, vec, vec], out_specs=rows,
        out_shape=jax.ShapeDtypeStruct(x.shape, BF16), compiler_params=_cparams(("parallel",)),
    )(x, gain, shift, scale)


def _prenorm_bwd_epi(d_h, x, resid, gain, scale, branch=None, gate=None):
    xhat, r = _rms_parts(x)
    nrm = xhat * gain
    d_n = d_h * (1.0 + scale)
    dx = _rms_bwd(d_n * gain, xhat, r) + resid
    sums = (jnp.sum(d_h, axis=0, keepdims=True), jnp.sum(d_h * nrm, axis=0, keepdims=True),
            jnp.sum(d_n * xhat, axis=0, keepdims=True))
    if branch is None:
        return (dx,) + sums
    return (dx, dx * gate) + sums + (jnp.sum(dx * branch, axis=0, keepdims=True),)


def _row_operands(tm, rows, vecs):
    return ([(a, (tm, D_MODEL), lambda i, j, kk: (i, 0)) for a in rows]
            + [(v, (1, D_MODEL), lambda i, j, kk: (0, 0)) for v in vecs])


def _gn_parts(ro):
    mu = jnp.mean(ro, axis=-1, keepdims=True)
    cen = ro - mu
    rstd = lax.rsqrt(jnp.mean(cen * cen, axis=-1, keepdims=True) + GN_EPS)
    return cen * rstd, rstd


MERGE_TM = 512


def _att_out(os_, ls_, w_att_out, gates, ret_out):
    tm = MERGE_TM

    def kern(o0, o1, o2, l0, l1, l2, w_ref, ga_ref, gb_ref, ro_ref, att_ref, attb_ref, lse_ref,
             ao_ref, mg_ref):
        l0v, l1v, l2v = l0[...], l1[...], l2[...]
        mx = jnp.maximum(jnp.maximum(l0v, l1v), l2v)
        e0, e1, e2 = jnp.exp(l0v - mx), jnp.exp(l1v - mx), jnp.exp(l2v - mx)
        den = e0 + e1 + e2
        att = (e0 / den) * o0[...] + (e1 / den) * o1[...] + (e2 / den) * o2[...]
        att_b = att.astype(BF16)
        att_ref[...] = att
        attb_ref[...] = att_b
        lse_ref[...] = mx + jnp.log(den)
        att_out = jnp.dot(att_b, w_ref[...], preferred_element_type=F32)
        ao_ref[...], merged = _merge_fwd_epi(att_out, ga_ref[...], gb_ref[...], ro_ref[...])
        mg_ref[...] = merged.astype(BF16)

    rows_w = pl.BlockSpec((tm, ATT_W), lambda i: (i, 0))
    rows_d = pl.BlockSpec((tm, D_MODEL), lambda i: (i, 0))
    return pl.pallas_call(
        kern, name="att_out", grid=(SEQ // tm,),
        in_specs=[rows_w] * 6 + [pl.BlockSpec((ATT_W, D_MODEL), lambda i: (0, 0)), rows_d,
                                 pl.BlockSpec((tm, D_MODEL), lambda i: (i, 1)), rows_d],
        out_specs=[rows_w, rows_w, rows_w, rows_d, rows_d],
        out_shape=[jax.ShapeDtypeStruct((SEQ, ATT_W), F32), jax.ShapeDtypeStruct((SEQ, ATT_W), BF16),
                   jax.ShapeDtypeStruct((SEQ, ATT_W), F32), jax.ShapeDtypeStruct((SEQ, D_MODEL), F32),
                   jax.ShapeDtypeStruct((SEQ, D_MODEL), BF16)],
        compiler_params=_cparams(("parallel",)),
    )(*os_, *ls_, w_att_out, gates, gates, ret_out)


def _merge_operands(gates, ret_out, att_out=None):
    ops = [(gates, (MERGE_TM, D_MODEL), lambda i, j, kk: (i, 0)),
           (gates, (MERGE_TM, D_MODEL), lambda i, j, kk: (i, 1)),
           (ret_out, (MERGE_TM, D_MODEL), lambda i, j, kk: (i, 0))]
    if att_out is not None:
        ops.append((att_out, (MERGE_TM, D_MODEL), lambda i, j, kk: (i, 0)))
    return ops


def _merge_fwd_epi(att_out, ga, gb, ret_out):
    return att_out, _sigmoid(ga.astype(F32)) * ret_out + _sigmoid(gb.astype(F32)) * att_out


def _merge_bwd_epi(d_merged, ga, gb, ret_out, att_out):
    sa, sb = _sigmoid(ga.astype(F32)), _sigmoid(gb.astype(F32))
    return (d_merged * sa, d_merged * sb, d_merged * ret_out * (sa * (1.0 - sa)),
            d_merged * att_out * (sb * (1.0 - sb)))


def _att_out_bwd_epi(d_att, att):
    outs = []
    for h in range(ATT_HPG):
        sl = slice(h * ATT_DH, (h + 1) * ATT_DH)
        outs.append(jnp.broadcast_to(jnp.sum(d_att[:, sl] * att[:, sl], axis=-1, keepdims=True),
                                     (d_att.shape[0], ATT_DH)))
    return d_att, jnp.concatenate(outs, axis=-1)


def _loss_head_epi(branch, x_prev, target, gate, gain):
    x3 = x_prev + gate * branch
    xhat, r = _rms_parts(x3)
    err = xhat * gain - target
    d_y = err / D_MODEL
    loss = 0.5 * jnp.sum(jnp.mean(err * err, axis=-1, keepdims=True), axis=0, keepdims=True)
    d_x = _rms_bwd(d_y * gain, xhat, r)
    return (d_x, d_x * gate, jnp.broadcast_to(loss, (1, D_MODEL)),
            jnp.sum(d_y * xhat, axis=0, keepdims=True), jnp.sum(d_x * branch, axis=0, keepdims=True))


def _local_step(pos, x, target, mod, norm1_g, norm2_g, norm_f_g, rel_bias, gn_g, gn_b, w_in, rest_gather):
    sh1, sc1, g1, sh2, sc2, g2 = [mod[:, i * D_MODEL:(i + 1) * D_MODEL] for i in range(6)]
    cos, sin = _rope_tables()
    din, qd, kd, cd = _decay_tables()
    buckets = _bucket_tables()
    bias = _bias_build(rel_bias, buckets)
    dils = [d for _, d in ATT_GROUPS]

    h1 = _prenorm_fwd("prenorm1_fwd", x, norm1_g, sh1, sc1)

    qk_tn = 2 * RET_DK

    def rot_epi(acc, cs, sn, scale):
        half = RET_DK // 2
        outs = []
        for h0 in range(0, qk_tn, RET_DK):
            x1, x2 = acc[:, h0:h0 + half], acc[:, h0 + half:h0 + RET_DK]
            outs += [x1 * cs - x2 * sn, x1 * sn + x2 * cs]
        return (jnp.concatenate(outs, axis=-1) * scale,)

    qk_scale = jnp.concatenate([jnp.ones((1, RET_QK_W), F32),
                                jnp.full((1, RET_QK_W), RET_DK ** -0.5, F32)], axis=-1)
    rope_ex = [(cos, (TM, RET_DK // 2), lambda i, j, kk: (i, 0)),
               (sin, (TM, RET_DK // 2), lambda i, j, kk: (i, 0)),
               (qk_scale, (1, qk_tn), lambda i, j, kk: (0, j))]
    rest_sems, rest_shards, rest_fulls, rest_token = rest_gather
    behind = [rest_token]
    rv = _matmul("proj_rv", h1, w_in, "nn", SEQ, RET_V_W, D_MODEL, [BF16], b_off=OFF_V, tk=D_MODEL,
                 after=behind)[0]
    rg = _matmul("proj_rg", h1, w_in, "nn", SEQ, RET_V_W, D_MODEL, [BF16], b_off=OFF_G, tk=D_MODEL,
                 after=behind)[0]
    gates = _matmul("proj_gates", h1, w_in, "nn", SEQ, 2 * D_MODEL, D_MODEL, [BF16], b_off=OFF_GATE,
                    tn=512, tk=D_MODEL, after=behind, b_buffers=3)[0]
    aqkv = _matmul("proj_att", h1, w_in, "nn", SEQ, 9 * ATT_W, D_MODEL, [BF16], b_off=OFF_ATT,
                   tn=512, tk=D_MODEL, after=behind, b_buffers=3)[0]

    rqk = _matmul("proj_qk", h1, w_in, "nn", SEQ, 2 * RET_QK_W, D_MODEL, [BF16], b_off=OFF_Q,
                  tn=qk_tn, tk=D_MODEL, epilogue=rot_epi, extras=rope_ex, after=behind)[0]
    ro, states, gated = _retention_fwd(rqk, rv, rg, gn_g, gn_b, din, qd, kd, cd)
    os_, ls_ = [], []
    for gi in range(3):
        if gi == 2:
            rest_sems, rest_fulls, fwd_token = _gather_rest_forward(
                rest_sems, rest_shards, rest_fulls, [gated, gates] + os_)
        o_g, l_g = _att_fwd(gi, dils[gi], aqkv, bias, after=[fwd_token] if gi == 2 else ())
        os_.append(o_g)
        ls_.append(l_g)
    w_ret_out, w_att_out, w_o, w_ff1, w_ff2 = _gather_rest_end(rest_sems, rest_fulls, [os_[2]])
    ret_out = _matmul("ret_out", gated, w_ret_out, "nn", SEQ, D_MODEL, RET_V_W, [F32], tk=RET_V_W)[0]
    att, att_b, lse, att_out, merged = _att_out(os_, ls_, w_att_out, gates, ret_out)

    def mix_epi(acc, xt, g, gain, sh, sc):
        x_new = xt + g * acc
        xhat, _ = _rms_parts(x_new)
        return x_new, acc, (xhat * gain) * (1.0 + sc) + sh

    x2, mix, h2 = _matmul("mix_out", merged, w_o, "nn", SEQ, D_MODEL, D_MODEL, [F32, BF16, BF16],
                          epilogue=mix_epi, extras=_row_operands(TM, [x], [g1, norm2_g, sh2, sc2]))

    def relu2_epi(acc):
        r = jnp.maximum(acc, 0.0)
        return r * r, r

    act, relu_u = _matmul("ff1", h2, w_ff1, "nn", SEQ, D_FF, D_MODEL, [BF16, BF16], tk=D_MODEL,
                          epilogue=relu2_epi, b_buffers=3)
    d_x3, d_y2, loss, d_gf, d_g2 = _matmul(
        "ff2", act, w_ff2, "nn", SEQ, D_MODEL, D_FF, [F32, BF16], tm=TM, tk=1024, n_sums=3,
        epilogue=_loss_head_epi, extras=_row_operands(TM, [x2, target], [g2, norm_f_g]), b_buffers=3)

    def relu2_bwd_epi(acc, rt):
        return (acc * (2.0 * rt.astype(F32)),)

    gw_ff2 = _matmul_tn_pair("ff2_dw", pos, act, d_y2, D_FF, D_MODEL, SEQ, D_FF // N_CHIPS,
                             tm=512, tn=1024, tk=SEQ)
    d_u = _matmul("ff2_dx", d_y2, w_ff2, "nt", SEQ, D_FF, D_MODEL, [BF16], epilogue=relu2_bwd_epi,
                  extras=[(relu_u, (TM, TN), lambda i, j, kk: (i, j))], b_buffers=3)[0]
    gw_ff1 = _matmul_tn_pair("ff1_dw", pos, h2, d_u, D_MODEL, D_FF, SEQ, D_MODEL,
                             tm=512, tn=1024, tk=SEQ)
    ffn = ["w_ff2", "w_ff1"]
    ffn_started = _ici_start("ici_start_ffn", ffn, [gw_ff2, gw_ff1])
    d_x2, d_mix, d_sh2, d_sc2, d_n2g, d_g1 = _matmul(
        "ff1_dx", d_u, w_ff1, "nt", SEQ, D_MODEL, D_FF, [F32, BF16], tm=TM, tk=1024, n_sums=4,
        epilogue=_prenorm_bwd_epi, extras=_row_operands(TM, [x2, d_x3], [norm2_g, sc2])
        + _row_operands(TM, [mix], [g1]), after=[ffn_started[3]], b_buffers=3)
    gw_o = _matmul_tn_pair("mix_dw", pos, merged, d_mix, D_MODEL, D_MODEL, SEQ, D_MODEL // N_CHIPS,
                           tm=128, tn=1024, tk=2048)
    d_ret_out, d_att_out, d_ga, d_gb = _matmul(
        "mix_dx", d_mix, w_o, "nt", SEQ, D_MODEL, D_MODEL, [BF16] * 4, tm=MERGE_TM,
        epilogue=_merge_bwd_epi, extras=_merge_operands(gates, ret_out, att_out))

    gw_ret_out = _matmul_tn_pair("ret_out_dw", pos, gated, d_ret_out, RET_V_W, D_MODEL, SEQ,
                                 RET_V_W // N_CHIPS, tm=256, tn=1024, tk=SEQ)
    gw_att_out = _matmul_tn_pair("att_out_dw", pos, att_b, d_att_out, ATT_W, D_MODEL, SEQ, ATT_W,
                                 tm=256, tn=1024, tk=2048)
    mixer = ["w_o", "w_ret_out", "w_att_out"]
    mixer_started = _ici_start("ici_start_mixer", mixer, [gw_o, gw_ret_out, gw_att_out])
    d_gated = _matmul("ret_out_dx", d_ret_out, w_ret_out, "nt", SEQ, RET_V_W, D_MODEL, [BF16],
                      after=[mixer_started[3]], b_buffers=3)[0]
    d_att, dd = _matmul("att_out_dx", d_att_out, w_att_out, "nt", SEQ, ATT_W, D_MODEL, [F32, F32],
                        epilogue=_att_out_bwd_epi,
                        extras=[(att, (TM, ATT_W), lambda i, j, kk: (i, 0))], after=[mixer_started[3]])

    d_rqkv, d_rg, d_gn_g, d_gn_b = _retention_bwd(rqk, rv, states, d_gated, ro, rg, gn_g, gn_b,
                                                  din, qd, kd, cd, cos, sin)

    d_aqkv, dsbs = [], []
    for gi in range(3):
        dqkv, dsb = _att_bwd(gi, dils[gi], aqkv, d_att, lse, dd, bias)
        d_aqkv.append(dqkv)
        dsbs.append(dsb)
    d_rel_bias = _bias_grad(jnp.stack(dsbs), buckets)

    d_proj = ([(d_rqkv, False), (d_rg, False)] + [(t, True) for t in d_aqkv]
              + [(d_ga, False), (d_gb, False)])
    gw_in = _matmul_tn_pair("proj_dw", pos, h1, d_proj, D_MODEL, IN_COLS, SEQ, D_MODEL,
                            tm=512, tn=ATT_W, tk=SEQ)
    sems, (gw_in,), (land,), token = _ici_start("ici_start_w_in", ["w_in"], [gw_in])
    grad_x, d_sh1, d_sc1, d_n1g = _matmul(
        "proj_dx", d_proj, w_in, "nt", SEQ, D_MODEL, IN_COLS, [F32], tn=1024, tk=ATT_W, n_sums=3,
        epilogue=_prenorm_bwd_epi, extras=_row_operands(TM, [x, d_x2], [norm1_g, sc1]), after=[token])
    pending = (sems, land)

    names = ffn + mixer
    psums, got = _ici_wait("ici_wait_rest", names, list(ffn_started[0]) + list(mixer_started[0]),
                           list(ffn_started[1]) + list(mixer_started[1]),
                           list(ffn_started[2]) + list(mixer_started[2]), [grad_x])
    g_big = {n: _final_sum("final_" + n, pos, dict(BIG)[n], psums[i], got[i], SHARD[n])
             for i, n in enumerate(names)}
    d_mod = jnp.concatenate([d_sh1, d_sc1, d_g1, d_sh2, d_sc2, d_g2], axis=-1)
    small = dict(norm1_g=d_n1g, norm2_g=d_n2g, norm_f_g=d_gf, gn_g=d_gn_g, gn_b=d_gn_b,
                 rel_bias=d_rel_bias)
    return loss, grad_x, d_mod, small, g_big, (gw_in,) + pending


def _me():
    return lax.axis_index("x"), lax.axis_index("y"), lax.axis_index("c")


def _peer(x, y, c, mask):
    return (x ^ ((mask >> 2) & 1), y ^ ((mask >> 1) & 1), c ^ (mask & 1))


def _gather8(src_ref, dst_ref, send_sems, recv_sems):
    x, y, c = _me()
    me = 4 * x + 2 * y + c
    copies = []
    for mask in range(1, N_DEV):
        cp = pltpu.make_async_remote_copy(
            src_ref=src_ref, dst_ref=dst_ref.at[me], send_sem=send_sems.at[mask - 1],
            recv_sem=recv_sems.at[mask - 1], device_id=_peer(x, y, c, mask), device_id_type=MESH)
        cp.start()
        copies.append(cp)
    dst_ref[me] = src_ref[...]
    for cp in copies:
        cp.wait_recv()
    for cp in copies:
        cp.wait_send()


def _ada_fwd(c_in, w_ada, b_ada):
    ncol = ADA_COLS // N_CHIPS

    def body(c_ref, w_ref, b_ref, mod_ref, sc_ref, cbuf, cg, mbuf, mg, s1, r1, s2, r2):
        x, y, c = _me()
        me = 4 * x + 2 * y + c
        cv = c_ref[...]
        cbuf[...] = jnp.broadcast_to(cv * _sigmoid(cv), cbuf.shape)
        _gather8(cbuf, cg, s1, r1)
        rows = lax.broadcasted_iota(I32, (N_DEV, D_MODEL), 0)
        sc_all = jnp.zeros((N_DEV, D_MODEL), F32)
        for d in range(N_DEV):
            sc_all = jnp.where(rows == d, cg[d], sc_all)
        sc_ref[...] = sc_all
        mbuf[...] = jnp.dot(sc_all.astype(BF16), w_ref[...].astype(BF16), preferred_element_type=F32)
        _gather8(mbuf, mg, s2, r2)
        rowsel = lax.broadcasted_iota(I32, (N_DEV, ncol), 0) == me
        for k in range(N_CHIPS):
            blk = mg[2 * k]
            row = jnp.sum(jnp.where(rowsel, blk, 0.0), axis=0, keepdims=True)
            mod_ref[:, k * ncol:(k + 1) * ncol] = row + b_ref[:, k * ncol:(k + 1) * ncol]

    vm = pl.BlockSpec(memory_space=pltpu.VMEM)
    return pl.pallas_call(
        body, name="ada_fwd",
        in_specs=[vm, vm, vm], out_specs=[vm, vm],
        out_shape=[jax.ShapeDtypeStruct((1, ADA_COLS), F32), jax.ShapeDtypeStruct((N_DEV, D_MODEL), F32)],
        scratch_shapes=[
            pltpu.VMEM((8, D_MODEL), F32), pltpu.VMEM((N_DEV, 8, D_MODEL), F32),
            pltpu.VMEM((8, ncol), F32), pltpu.VMEM((N_DEV, 8, ncol), F32),
            pltpu.SemaphoreType.DMA((N_DEV - 1,)), pltpu.SemaphoreType.DMA((N_DEV - 1,)),
            pltpu.SemaphoreType.DMA((N_DEV - 1,)), pltpu.SemaphoreType.DMA((N_DEV - 1,)),
        ],
        compiler_params=pltpu.CompilerParams(vmem_limit_bytes=VMEM_LIMIT_V7X),
    )(c_in, w_ada, b_ada)


def _small_reduce(pack, sc_all, after=()):
    ncol = ADA_COLS // N_CHIPS

    def body(p_ref, sc_ref, *rest):
        tot_ref, gw_ref, pg, s1, r1 = rest[len(after):]
        x, y, _ = _me()
        chip = 2 * x + y
        _gather8(p_ref, pg, s1, r1)
        tot = pg[0]
        for d in range(1, N_DEV):
            tot = tot + pg[d]
        tot_ref[...] = tot
        rows = lax.broadcasted_iota(I32, (N_DEV, ncol), 0)
        dmod = jnp.zeros((N_DEV, ncol), F32)
        for k in range(N_CHIPS):
            part = jnp.zeros((N_DEV, ncol), F32)
            for d in range(N_DEV):
                part = jnp.where(rows == d, pg[d, :, k * ncol:(k + 1) * ncol][0:1, :], part)
            dmod = jnp.where(chip == k, part, dmod)
        gw_ref[...] = lax.dot_general(sc_ref[...].astype(BF16), dmod.astype(BF16), _TN,
                                      preferred_element_type=F32)

    vm = pl.BlockSpec(memory_space=pltpu.VMEM)
    return pl.pallas_call(
        body, name="small_reduce",
        in_specs=[vm, vm] + [pl.BlockSpec(memory_space=pl.ANY)] * len(after), out_specs=[vm, vm],
        out_shape=[jax.ShapeDtypeStruct((8, ADA_COLS), F32), jax.ShapeDtypeStruct((D_MODEL, ncol), F32)],
        scratch_shapes=[pltpu.VMEM((N_DEV, 8, ADA_COLS), F32),
                        pltpu.SemaphoreType.DMA((N_DEV - 1,)), pltpu.SemaphoreType.DMA((N_DEV - 1,))],
        compiler_params=pltpu.CompilerParams(vmem_limit_bytes=VMEM_LIMIT_V7X),
    )(pack, sc_all, *after)


BIG = (("w_in", 1), ("w_ret_out", 0), ("w_att_out", 1), ("w_o", 0), ("w_ff1", 1), ("w_ff2", 0))
SHARD = {"w_in": (D_MODEL, IN_COLS // N_CHIPS), "w_ret_out": (RET_V_W // N_CHIPS, D_MODEL),
         "w_att_out": (ATT_W, D_MODEL // N_CHIPS), "w_o": (D_MODEL // N_CHIPS, D_MODEL),
         "w_ff1": (D_MODEL, D_FF // N_CHIPS), "w_ff2": (D_FF // N_CHIPS, D_MODEL)}
_CHIP_FLIPS = ((1, 0), (0, 1), (1, 1))


def _region(ref, axis, chip, half, shard_shape):
    r, cw = shard_shape
    hr = r // 2
    if axis == 1:
        return ref.at[pl.ds(half * hr, hr), pl.ds(chip * cw, cw)]
    return ref.at[pl.ds(chip * r + half * hr, hr), :]


CAST_ROWS = 128


def _gather_weights(shards, n_remote):
    nw = len(BIG)
    shapes = [s.shape for s in shards]
    full_shapes = [(r, N_CHIPS * cw) if ax == 1 else (N_CHIPS * r, cw)
                   for (r, cw), (_, ax) in zip(shapes, BIG)]

    def body(*refs):
        ins, outs = refs[:nw], refs[nw:2 * nw]
        own = refs[2 * nw:3 * nw]
        from_ici, from_sib = refs[3 * nw:3 * nw + n_remote], refs[3 * nw + n_remote:3 * nw + 2 * n_remote]
        ld_sem, st_sem, s_ici, r_ici, s_d2d, r_d2d, st_a, st_b, stage = refs[3 * nw + 2 * n_remote:]
        x, y, c = _me()
        chip = 2 * x + y
        sib = (x, y, 1 - c)
        loads = [pltpu.make_async_copy(ins[i], stage if i == 0 else own[i], ld_sem.at[i])
                 for i in range(nw)]
        for cp in loads:
            cp.start()
        pending, first = [], []
        for i, (_, ax) in enumerate(BIG):
            r, cw = shapes[i]
            hr = r // 2
            loads[i].wait()
            if i == 0:
                for r0 in range(0, r, CAST_ROWS):
                    own[0][r0:r0 + CAST_ROWS, :] = stage[r0:r0 + CAST_ROWS, :].astype(BF16)
            dst = outs[i].at[:, pl.ds(chip * cw, cw)] if ax == 1 else outs[i].at[pl.ds(chip * r, r), :]
            cp = pltpu.make_async_copy(own[i], dst, st_sem.at[i])
            cp.start()
            pending.append(cp)
            for j, (fx, fy) in enumerate(_CHIP_FLIPS if i < n_remote else ()):
                rc = pltpu.make_async_remote_copy(
                    src_ref=own[i].at[pl.ds(c * hr, hr), :], dst_ref=from_ici[i].at[j],
                    send_sem=s_ici.at[j * nw + i], recv_sem=r_ici.at[j * nw + i],
                    device_id=(x ^ fx, y ^ fy, c), device_id_type=MESH)
                rc.start()
                first.append((j, i, rc))
        passed = []
        for j, i, rc in first:
            fx, fy = _CHIP_FLIPS[j]
            src_chip = 2 * (x ^ fx) + (y ^ fy)
            ax = BIG[i][1]
            rc.wait_recv()
            fw = pltpu.make_async_remote_copy(
                src_ref=from_ici[i].at[j], dst_ref=from_sib[i].at[j], send_sem=s_d2d.at[j * nw + i],
                recv_sem=r_d2d.at[j * nw + i], device_id=sib, device_id_type=MESH)
            fw.start()
            passed.append((j, i, src_chip, fw))
            st = pltpu.make_async_copy(from_ici[i].at[j], _region(outs[i], ax, src_chip, c, shapes[i]),
                                       st_a.at[j * nw + i])
            st.start()
            pending.append(st)
        for j, i, src_chip, fw in passed:
            fw.wait_recv()
            st = pltpu.make_async_copy(from_sib[i].at[j],
                                       _region(outs[i], BIG[i][1], src_chip, 1 - c, shapes[i]),
                                       st_b.at[j * nw + i])
            st.start()
            pending.append(st)
        for _, _, rc in first:
            rc.wait_send()
        for _, _, _, fw in passed:
            fw.wait_send()
        for cp in pending:
            cp.wait()

    hbm = pl.BlockSpec(memory_space=pl.ANY)
    halves = [pltpu.VMEM((3, r // 2, cw), BF16) for r, cw in shapes[:n_remote]]
    return pl.pallas_call(
        body, name="gather_weights",
        in_specs=[hbm] * nw, out_specs=[hbm] * nw,
        out_shape=[jax.ShapeDtypeStruct(fs, BF16) for fs in full_shapes],
        scratch_shapes=[pltpu.VMEM(sh, BF16) for sh in shapes] + halves + halves
        + [pltpu.SemaphoreType.DMA((nw,)), pltpu.SemaphoreType.DMA((nw,))]
        + [pltpu.SemaphoreType.DMA((3 * nw,))] * 6 + [pltpu.VMEM(shapes[0], F32)],
        compiler_params=pltpu.CompilerParams(vmem_limit_bytes=VMEM_LIMIT_V7X),
    )(*shards)


REST = BIG[1:]
_SIDE_EFFECTS = pltpu.CompilerParams(has_side_effects=pltpu.SideEffectType.DATAFLOW_SIDE_EFFECTING)
_ANY_SPEC = pl.BlockSpec(memory_space=pl.ANY)


def _rest_ici_copies(shard_refs, full_refs, sems):
    x, y, c = _me()
    chip = 2 * x + y
    n = 3 * len(REST)
    copies = []
    for i, (name, ax) in enumerate(REST):
        hr = SHARD[name][0] // 2
        for j, (fx, fy) in enumerate(_CHIP_FLIPS):
            copies.append(pltpu.make_async_remote_copy(
                src_ref=shard_refs[i].at[pl.ds(c * hr, hr), :],
                dst_ref=_region(full_refs[i], ax, chip, c, SHARD[name]),
                send_sem=sems[3 * i + j], recv_sem=sems[n + 3 * i + j],
                device_id=(x ^ fx, y ^ fy, c), device_id_type=MESH))
    return copies


def _rest_d2d_copies(full_refs, sems):
    x, y, c = _me()
    n = 3 * len(REST)
    copies = []
    for i, (name, ax) in enumerate(REST):
        for j, (fx, fy) in enumerate(_CHIP_FLIPS):
            reg = _region(full_refs[i], ax, 2 * (x ^ fx) + (y ^ fy), c, SHARD[name])
            copies.append(pltpu.make_async_remote_copy(
                src_ref=reg, dst_ref=reg, send_sem=sems[3 * i + j], recv_sem=sems[n + 3 * i + j],
                device_id=(x, y, 1 - c), device_id_type=MESH))
    return copies


def _gather_rest_start(shards, fulls, after):
    nr, ns, na = len(REST), 6 * len(REST), len(after)

    def body(*refs):
        for cp in _rest_ici_copies(refs[:nr], refs[nr:2 * nr], refs[2 * nr + na:2 * nr + na + ns]):
            cp.start()
        token = refs[-1]
        token[...] = jnp.zeros_like(token)

    hbm = lambda a: pltpu.HBM(a.shape, a.dtype)
    res = pl.pallas_call(
        body, name="gather_rest_start",
        out_shape=(pltpu.SemaphoreType.DMA(()),) * ns + tuple(hbm(a) for a in shards + fulls)
        + (jax.ShapeDtypeStruct((8, 128), F32),),
        in_specs=(_HBM_SPEC,) * (2 * nr) + (_ANY_SPEC,) * na,
        out_specs=(_SEM_SPEC,) * ns + (_HBM_SPEC,) * (2 * nr) + (pl.BlockSpec(memory_space=pltpu.VMEM),),
        input_output_aliases={k: ns + k for k in range(2 * nr)}, compiler_params=_SIDE_EFFECTS,
    )(*[pltpu.with_memory_space_constraint(a, pltpu.HBM) for a in shards + fulls], *after)
    return res[:ns], res[ns:ns + nr], res[ns + nr:ns + 2 * nr], res[-1]


def _gather_rest_forward(sems, shards, fulls, after):
    nr, ns = len(REST), 6 * len(REST)

    def body(*refs):
        shard_refs, full_refs, old = refs[:nr], refs[nr:2 * nr], refs[2 * nr:2 * nr + ns]
        new = refs[2 * nr + ns + len(after):2 * nr + 2 * ns + len(after)]
        for cp in _rest_ici_copies(shard_refs, full_refs, old):
            cp.wait_send()
            cp.wait_recv()
        for cp in _rest_d2d_copies(full_refs, new):
            cp.start()
        token = refs[-1]
        token[...] = jnp.zeros_like(token)

    res = pl.pallas_call(
        body, name="gather_rest_forward",
        out_shape=(pltpu.SemaphoreType.DMA(()),) * ns + tuple(pltpu.HBM(a.shape, a.dtype) for a in fulls)
        + (jax.ShapeDtypeStruct((8, 128), F32),),
        in_specs=(_HBM_SPEC,) * (2 * nr) + (_SEM_SPEC,) * ns + (_ANY_SPEC,) * len(after),
        out_specs=(_SEM_SPEC,) * ns + (_HBM_SPEC,) * nr + (pl.BlockSpec(memory_space=pltpu.VMEM),),
        input_output_aliases={nr + k: ns + k for k in range(nr)}, compiler_params=_SIDE_EFFECTS,
    )(*shards, *fulls, *sems, *after)
    return res[:ns], res[ns:ns + nr], res[-1]


def _gather_rest_end(sems, fulls, after):
    nr, ns = len(REST), 6 * len(REST)

    def body(*refs):
        for cp in _rest_d2d_copies(refs[:nr], refs[nr:nr + ns]):
            cp.wait_send()
            cp.wait_recv()

    return pl.pallas_call(
        body, name="gather_rest_end",
        out_shape=tuple(pltpu.HBM(a.shape, a.dtype) for a in fulls),
        in_specs=(_HBM_SPEC,) * nr + (_SEM_SPEC,) * ns + (_ANY_SPEC,) * len(after),
        out_specs=(_HBM_SPEC,) * nr,
        input_output_aliases={k: k for k in range(nr)}, compiler_params=_SIDE_EFFECTS,
    )(*fulls, *sems, *after)


def _adam_update(w, g, m, v):
    mn = ADAM_B1 * m + (1.0 - ADAM_B1) * g
    vn = ADAM_B2 * v + (1.0 - ADAM_B2) * (g * g)
    m_hat = mn / (1.0 - ADAM_B1 ** ADAM_STEP)
    v_hat = vn / (1.0 - ADAM_B2 ** ADAM_STEP)
    return -ADAM_LR * (m_hat / (jnp.sqrt(v_hat) + ADAM_EPS) + ADAM_WD * w), mn, vn


def _final_sum(name, pos, axis, psum, recv, shard_shape, after=(), tr=128):
    r, cw = shard_shape
    hr = r // 2
    tr = min(tr, hr)
    nt = hr // tr
    n_after = len(after)

    def kern(pos_ref, p_ref, r_ref, *rest):
        g_ref, send_buf, land_buf, s_sem, r_sem = rest[n_after:]
        p, t = pl.program_id(0), pl.program_id(1)
        sib = _sibling()

        @pl.when(jnp.logical_and(p == 0, t == 0))
        def _():
            _pair_barrier(sib)

        def copy(i):
            return pltpu.make_async_remote_copy(
                src_ref=send_buf.at[i], dst_ref=land_buf.at[i], send_sem=s_sem.at[i],
                recv_sem=r_sem.at[i], device_id=sib, device_id_type=MESH)

        @pl.when(p == 0)
        def _():
            tot = p_ref[...].astype(F32)
            for j in range(3):
                tot = tot + r_ref[j].astype(F32)
            send_buf[t] = tot
            copy(t).start()
            g_ref[...] = tot

        @pl.when(p == 1)
        def _():
            copy(t).wait_recv()
            g_ref[...] = land_buf[t]

        @pl.when(jnp.logical_and(p == 1, t == nt - 1))
        def _():
            for i in range(nt):
                copy(i).wait_send()

    def shard_rows(p, t, pos_ref):
        return (jnp.where(p == 0, pos_ref[0], 1 - pos_ref[0]) * nt + t, 0)

    def own_part(p, t, pos_ref):
        tt = jnp.where(p == 0, t, nt - 1)
        return (tt, pos_ref[1]) if axis == 1 else (pos_ref[1] * nt + tt, 0)

    grid_spec = pltpu.PrefetchScalarGridSpec(
        num_scalar_prefetch=1, grid=(2, nt),
        in_specs=[pl.BlockSpec((tr, cw), own_part),
                  pl.BlockSpec((3, tr, cw), lambda p, t, pos_ref: (0, jnp.where(p == 0, t, nt - 1), 0))]
        + [pl.BlockSpec(memory_space=pl.ANY)] * n_after,
        out_specs=pl.BlockSpec((tr, cw), shard_rows),
        scratch_shapes=[pltpu.VMEM((nt, tr, cw), F32), pltpu.VMEM((nt, tr, cw), F32),
                        pltpu.SemaphoreType.DMA((nt,)), pltpu.SemaphoreType.DMA((nt,))])
    return pl.pallas_call(
        kern, name=name, grid_spec=grid_spec, out_shape=jax.ShapeDtypeStruct((r, cw), F32),
        compiler_params=_cparams(("arbitrary", "arbitrary"), PAIR_COLLECTIVE_ID),
    )(pos, psum, recv, *after)


def _adamw(name, w, g, m, v):
    r, cw = w.shape
    tr = min(r, 128)

    def kern(w_ref, g_ref, m_ref, v_ref, go_ref, d_ref, nm_ref, nv_ref):
        gv = g_ref[...]
        go_ref[...] = gv
        d_ref[...], nm_ref[...], nv_ref[...] = _adam_update(w_ref[...], gv, m_ref[...], v_ref[...])

    spec = pl.BlockSpec((tr, cw), lambda i: (i, 0))
    return pl.pallas_call(
        kern, name=name, grid=(r // tr,), in_specs=[spec] * 4, out_specs=[spec] * 4,
        out_shape=[jax.ShapeDtypeStruct((r, cw), F32)] * 4, compiler_params=_cparams(("parallel",)),
    )(w, g, m, v)


_PACK_W = ADA_COLS
_NB = REL_BUCKETS * N_ATT_HEADS
_SMALL_SLOTS = {
    "b_ada": (0, 0, ADA_COLS),
    "norm1_g": (1, 0, D_MODEL), "norm2_g": (1, D_MODEL, D_MODEL), "norm_f_g": (1, 2 * D_MODEL, D_MODEL),
    "ret_gn_g": (1, 3 * D_MODEL, RET_V_W),
    "ret_gn_b": (2, 0, RET_V_W), "rel_bias": (2, RET_V_W, _NB), "loss": (2, RET_V_W + 512, 128),
}


def _pack_small(vals):
    rows = []
    for r in range(8):
        items = sorted([(off, n) for n, (rr, off, _) in _SMALL_SLOTS.items() if rr == r and n in vals])
        parts, pos = [], 0
        for off, n in items:
            if off > pos:
                parts.append(jnp.zeros((1, off - pos), F32))
            parts.append(vals[n].reshape(1, -1).astype(F32))
            pos = off + _SMALL_SLOTS[n][2]
        if pos < _PACK_W:
            parts.append(jnp.zeros((1, _PACK_W - pos), F32))
        rows.append(jnp.concatenate(parts, axis=-1))
    return jnp.concatenate(rows, axis=0)


def _adamw_small(tot, names, wmv):
    n = len(names)

    def kern(tot_ref, *refs):
        ins, outs = refs[:3 * n], refs[3 * n:]
        for i, name in enumerate(names):
            row, off, width = _SMALL_SLOTS[name]
            g = tot_ref[row:row + 1, off:off + width]
            outs[i][...] = g
            outs[n + i][...], outs[2 * n + i][...], outs[3 * n + i][...] = _adam_update(
                ins[i][...], g, ins[n + i][...], ins[2 * n + i][...])

    vm = pl.BlockSpec(memory_space=pltpu.VMEM)
    shapes = [jax.ShapeDtypeStruct((1, _SMALL_SLOTS[name][2]), F32) for name in names]
    res = pl.pallas_call(
        kern, name="adamw_small", in_specs=[vm] * (1 + 3 * n), out_specs=[vm] * (4 * n),
        out_shape=shapes * 4,
    )(tot, *wmv[0], *wmv[1], *wmv[2])
    return res[:n], res[n:2 * n], res[2 * n:3 * n], res[3 * n:]


def _unpack_small(pack, name):
    r, off, wd = _SMALL_SLOTS[name]
    return pack[r:r + 1, off:off + wd]


def kernel(x, c, w_ada, b_ada, norm1_g, w_in, rel_bias, ret_gn_g, ret_gn_b, w_ret_out, w_att_out, w_o, norm2_g, w_ff1, w_ff2, norm_f_g, loss_target, m_w_ada, m_b_ada, m_norm1_g, m_w_in, m_rel_bias, m_ret_gn_g, m_ret_gn_b, m_w_ret_out, m_w_att_out, m_w_o, m_norm2_g, m_w_ff1, m_w_ff2, m_norm_f_g, v_w_ada, v_b_ada, v_norm1_g, v_w_in, v_rel_bias, v_ret_gn_g, v_ret_gn_b, v_w_ret_out, v_w_att_out, v_w_o, v_norm2_g, v_w_ff1, v_w_ff2, v_norm_f_g):
    given = dict(locals())
    big_names = [n for n, _ in BIG]
    shard_w = {n: given[n][0] for n in big_names}
    assert all(shard_w[n].shape == SHARD[n] for n in big_names)

    shards_bf = [None] + [shard_w[n].astype(BF16) for n in big_names[1:]]
    full = _gather_weights([shard_w["w_in"]] + shards_bf[1:], 1)
    mod, sc_all = _ada_fwd(c, w_ada[0], b_ada)
    rest_gather = _gather_rest_start(shards_bf[1:], list(full[1:]), [mod])
    pos = _where_am_i()

    loss, grad_x, d_mod, small, g_big, pending = _local_step(
        pos, x[0], loss_target[0], mod, norm1_g, norm2_g, norm_f_g.reshape(1, -1), rel_bias, ret_gn_g,
        ret_gn_b, full[0], rest_gather)

    pack_g = _pack_small(dict(b_ada=d_mod, norm1_g=small["norm1_g"], norm2_g=small["norm2_g"],
                              norm_f_g=small["norm_f_g"], ret_gn_g=small["gn_g"], ret_gn_b=small["gn_b"],
                              rel_bias=small["rel_bias"], loss=loss[:, :128]))
    tot, g_w_ada = _small_reduce(pack_g, sc_all, after=list(g_big.values()))

    small_names = ["b_ada", "norm1_g", "rel_bias", "ret_gn_g", "ret_gn_b", "norm2_g", "norm_f_g"]
    small_out = _adamw_small(tot, small_names, [[given[p + n].reshape(1, -1) for n in small_names]
                                                for p in ("", "m_", "v_")])
    grads, deltas, new_m, new_v = ({n: t.reshape(given[n].shape) for n, t in zip(small_names, group)}
                                   for group in small_out)
    sd = deltas["b_ada"]
    g_big["w_ada"] = g_w_ada
    for n in ["w_ada"] + big_names[1:] + big_names[:1]:
        if n == "w_in":
            gw_in, sems, land = pending
            done = [tot, sd] + [deltas[k] for k in ["w_ada"] + big_names[1:]]
            (gw_in,), (got,) = _ici_wait("ici_wait_w_in", [n], sems, [gw_in], [land], done)
            g_big[n] = _final_sum("final_w_in", pos, 1, gw_in, got, SHARD[n])
        g, d, nm, nv = _adamw("adamw_" + n, given[n][0], g_big[n], given["m_" + n][0], given["v_" + n][0])
        grads[n], deltas[n], new_m[n], new_v[n] = g[None], d[None], nm[None], nv[None]

    order = ["w_ada", "b_ada", "norm1_g", "w_in", "rel_bias", "ret_gn_g", "ret_gn_b", "w_ret_out",
             "w_att_out", "w_o", "norm2_g", "w_ff1", "w_ff2", "norm_f_g"]
    loss_out = _unpack_small(tot, "loss")[0, 0]
    return (loss_out, grad_x[None], *[grads[n] for n in order], *[deltas[n] for n in order],
            *[new_m[n] for n in order], *[new_v[n] for n in order])
```
